```python
import jax, jax.numpy as jnp
from jax import lax
import numpy as np

D_MODEL = 1024
BATCH = 8
SEQ = 4096
DEPTH = 2

N_MEM = 256
XA_HEADS = 4
XA_HD = D_MODEL // XA_HEADS
W_A = D_MODEL
H_A = 8
HD_A = W_A // H_A
CONV_A = 4
C_RG = 8.0
W_B = D_MODEL // 2
POOL_WINDOWS = (2, 4, 8, 16)
G_B = len(POOL_WINDOWS)
HD_B = W_B // G_B
IN_AB = 2 * W_A + W_B
OUT_AB = W_A + W_B
CONV_C = 31
D_FF = 3 * D_MODEL
CONV_F = 3
EPS = 1e-6
N_EVEN = (DEPTH + 1) // 2
N_ODD = DEPTH // 2

kernel_name = "hybrid_rglru_pool_conformer_xattn_convffn"


def rms_norm(x, g):
    xf = x.astype(jnp.float32)
    y = xf * lax.rsqrt(jnp.mean(xf * xf, axis=-1, keepdims=True) + EPS)
    return (y * g.astype(jnp.float32)).astype(x.dtype)


def layer_norm(x, g, b):
    xf = x.astype(jnp.float32)
    mu = jnp.mean(xf, axis=-1, keepdims=True)
    var = jnp.mean(jnp.square(xf - mu), axis=-1, keepdims=True)
    y = (xf - mu) * lax.rsqrt(var + EPS)
    return (y * g.astype(jnp.float32) + b.astype(jnp.float32)).astype(x.dtype)


def causal_dwconv(x, w, b):
    K, C = w.shape
    y = lax.conv_general_dilated(
        x, w[:, None, :], window_strides=(1,), padding=[(K - 1, 0)],
        dimension_numbers=("NWC", "WIO", "NWC"), feature_group_count=C)
    return y + b


def rg_lru(x, w_gx, b_gx, w_ga, b_ga, lam):
    Bn, S, W = x.shape
    xh = x.reshape(Bn, S, H_A, HD_A)
    gate_x = jax.nn.sigmoid(jnp.einsum('bshi,hij->bshj', xh, w_gx).reshape(Bn, S, W) + b_gx)
    gate_a = jax.nn.sigmoid(jnp.einsum('bshi,hij->bshj', xh, w_ga).reshape(Bn, S, W) + b_ga)
    log_a = -C_RG * gate_a.astype(jnp.float32) * jax.nn.softplus(-lam.astype(jnp.float32))
    a = jnp.exp(log_a)
    mult = jnp.sqrt(-jnp.expm1(2.0 * log_a))
    bx = mult * (gate_x * x).astype(jnp.float32)

    def combine(lhs, rhs):
        a_l, b_l = lhs
        a_r, b_r = rhs
        return a_l * a_r, a_r * b_l + b_r

    _, h = lax.associative_scan(combine, (a, bx), axis=1)
    return h.astype(x.dtype)


def multi_scale_pool(u, w_g, b_g, scale):
    Bn, S, W = u.shape
    uf = u.astype(jnp.float32)
    csum = jnp.cumsum(uf, axis=1)
    t = jnp.arange(1, S + 1, dtype=jnp.float32)[:, None]
    outs = []
    for g, w in enumerate(POOL_WINDOWS):
        sl = slice(g * HD_B, (g + 1) * HD_B)
        cg = csum[..., sl]
        lagged = jnp.pad(cg[:, :S - w], ((0, 0), (w, 0), (0, 0)))
        mean = (cg - lagged) / jnp.minimum(t, float(w))
        outs.append(mean - uf[..., sl])
    p = jnp.stack(outs, axis=2).astype(u.dtype)
    y = jnp.einsum('bsgi,gij->bsgj', p, w_g).reshape(Bn, S, W) + b_g
    return y * scale


def mixer_ab(x, norm, w_in, conv_w, conv_b, w_gx, b_gx, w_ga, b_ga, lam, w_pool, b_pool, pool_scale, w_out):
    z = rms_norm(x, norm) @ w_in
    z_gate = z[..., :W_A]
    z_rec = z[..., W_A:2 * W_A]
    z_pool = z[..., 2 * W_A:]
    xr = causal_dwconv(z_rec, conv_w, conv_b)
    y_a = jax.nn.gelu(z_gate) * rg_lru(xr, w_gx, b_gx, w_ga, b_ga, lam)
    y_b = multi_scale_pool(z_pool, w_pool, b_pool, pool_scale)
    return jnp.concatenate([y_a, y_b], axis=-1) @ w_out


def conformer_conv(x, norm, w1, b1, dw_w, dw_b, ln_g, ln_b, w2, b2):
    h = rms_norm(x, norm) @ w1 + b1
    h = jax.nn.glu(h, axis=-1)
    h = causal_dwconv(h, dw_w, dw_b)
    h = jax.nn.silu(layer_norm(h, ln_g, ln_b))
    return h @ w2 + b2


def cross_attn(x, mem, norm, mem_norm, wq, wk, wv, wo):
    Bn, S, _ = x.shape
    M = mem.shape[1]
    q = (rms_norm(x, norm) @ wq).reshape(Bn, S, XA_HEADS, XA_HD)
    m = rms_norm(mem, mem_norm)
    k = (m @ wk).reshape(Bn, M, XA_HEADS, XA_HD)
    v = (m @ wv).reshape(Bn, M, XA_HEADS, XA_HD)
    s = jnp.einsum('bqhd,bkhd->bhqk', q, k).astype(jnp.float32) * (XA_HD ** -0.5)
    p = jax.nn.softmax(s, axis=-1).astype(x.dtype)
    o = jnp.einsum('bhqk,bkhd->bqhd', p, v).reshape(Bn, S, D_MODEL)
    return o @ wo


def conv_ffn(x, norm, w_up, dw_w, dw_b, w_down):
    h = rms_norm(x, norm) @ w_up
    g = causal_dwconv(h[..., :D_FF], dw_w, dw_b)
    u = h[..., D_FF:]
    return (jax.nn.gelu(g) * u) @ w_down


def _fwd_setup_inputs(seed: int = 0) -> dict:
    key = jax.random.key(seed)
    keys = iter(jax.random.split(key, 48))

    def nrm(shape, scale):
        return jax.random.normal(next(keys), shape, jnp.float32) * scale

    def gain(shape):
        return 1.0 + 0.02 * jax.random.normal(next(keys), shape, jnp.float32)

    L, NE, NO, D = DEPTH, N_EVEN, N_ODD, D_MODEL
    u = jax.random.uniform(next(keys), (NE, W_A), jnp.float32, 0.9, 0.999) ** (1.0 / C_RG)
    lam = jnp.log(u) - jnp.log1p(-u)
    return {
        "x": jax.random.normal(next(keys), (BATCH, SEQ, D), jnp.float32),
        "mem": jax.random.normal(next(keys), (BATCH, N_MEM, D), jnp.float32),
        "ab_norm": gain((NE, D)),
        "ab_w_in": nrm((NE, D, IN_AB), D ** -0.5),
        "a_conv_w": nrm((NE, CONV_A, W_A), CONV_A ** -0.5),
        "a_conv_b": nrm((NE, W_A), 0.01),
        "a_gate_x_w": nrm((NE, H_A, HD_A, HD_A), HD_A ** -0.5),
        "a_gate_x_b": nrm((NE, W_A), 0.01),
        "a_gate_a_w": nrm((NE, H_A, HD_A, HD_A), HD_A ** -0.5),
        "a_gate_a_b": nrm((NE, W_A), 0.01),
        "a_lambda": lam,
        "b_group_w": nrm((NE, G_B, HD_B, HD_B), HD_B ** -0.5),
        "b_group_b": nrm((NE, W_B), 0.01),
        "b_scale": 1.0 + 0.1 * jax.random.normal(next(keys), (NE, W_B), jnp.float32),
        "ab_w_out": nrm((NE, OUT_AB, D), OUT_AB ** -0.5),
        "c_norm": gain((NO, D)),
        "c_w_pw1": nrm((NO, D, 2 * D), D ** -0.5),
        "c_b_pw1": nrm((NO, 2 * D), 0.01),
        "c_dw_w": nrm((NO, CONV_C, D), CONV_C ** -0.5),
        "c_dw_b": nrm((NO, D), 0.01),
        "c_ln_g": gain((NO, D)),
        "c_ln_b": nrm((NO, D), 0.01),
        "c_w_pw2": nrm((NO, D, D), D ** -0.5),
        "c_b_pw2": nrm((NO, D), 0.01),
        "xa_norm": gain((L, D)),
        "xa_mem_norm": gain((L, D)),
        "xa_wq": nrm((L, D, D), D ** -0.5),
        "xa_wk": nrm((L, D, D), D ** -0.5),
        "xa_wv": nrm((L, D, D), D ** -0.5),
        "xa_wo": nrm((L, D, D), D ** -0.5),
        "f_norm": gain((L, D)),
        "f_w_up": nrm((L, D, 2 * D_FF), D ** -0.5),
        "f_dw_w": nrm((L, CONV_F, D_FF), CONV_F ** -0.5),
        "f_dw_b": nrm((L, D_FF), 0.01),
        "f_w_down": nrm((L, D_FF, D), D_FF ** -0.5),
        "final_norm": gain((D,)),
    }


def _fwd_reference(x, mem, ab_norm, ab_w_in, a_conv_w, a_conv_b, a_gate_x_w, a_gate_x_b,
              a_gate_a_w, a_gate_a_b, a_lambda, b_group_w, b_group_b, b_scale, ab_w_out,
              c_norm, c_w_pw1, c_b_pw1, c_dw_w, c_dw_b, c_ln_g, c_ln_b, c_w_pw2, c_b_pw2,
              xa_norm, xa_mem_norm, xa_wq, xa_wk, xa_wv, xa_wo,
              f_norm, f_w_up, f_dw_w, f_dw_b, f_w_down, final_norm):
    for layer in range(DEPTH):
        if layer % 2 == 0:
            i = layer // 2
            x = x + mixer_ab(x, ab_norm[i], ab_w_in[i], a_conv_w[i], a_conv_b[i],
                             a_gate_x_w[i], a_gate_x_b[i], a_gate_a_w[i], a_gate_a_b[i],
                             a_lambda[i], b_group_w[i], b_group_b[i], b_scale[i], ab_w_out[i])
        else:
            j = layer // 2
            x = x + conformer_conv(x, c_norm[j], c_w_pw1[j], c_b_pw1[j], c_dw_w[j], c_dw_b[j],
                                   c_ln_g[j], c_ln_b[j], c_w_pw2[j], c_b_pw2[j])
        x = x + cross_attn(x, mem, xa_norm[layer], xa_mem_norm[layer], xa_wq[layer],
                           xa_wk[layer], xa_wv[layer], xa_wo[layer])
        x = x + conv_ffn(x, f_norm[layer], f_w_up[layer], f_dw_w[layer], f_dw_b[layer],
                         f_w_down[layer])
    return rms_norm(x, final_norm)


import jax as _jax
import jax.numpy as _jnp

TWIN_FORMAT = 'train_step'
FWD_PARAMS = ['x', 'mem', 'ab_norm', 'ab_w_in', 'a_conv_w', 'a_conv_b', 'a_gate_x_w', 'a_gate_x_b', 'a_gate_a_w', 'a_gate_a_b', 'a_lambda', 'b_group_w', 'b_group_b', 'b_scale', 'ab_w_out', 'c_norm', 'c_w_pw1', 'c_b_pw1', 'c_dw_w', 'c_dw_b', 'c_ln_g', 'c_ln_b', 'c_w_pw2', 'c_b_pw2', 'xa_norm', 'xa_mem_norm', 'xa_wq', 'xa_wk', 'xa_wv', 'xa_wo', 'f_norm', 'f_w_up', 'f_dw_w', 'f_dw_b', 'f_w_down', 'final_norm']
TWIN_WEIGHTS = ['ab_norm', 'ab_w_in', 'a_conv_w', 'a_conv_b', 'a_gate_x_w', 'a_gate_x_b', 'a_gate_a_w', 'a_gate_a_b', 'a_lambda', 'b_group_w', 'b_group_b', 'b_scale', 'ab_w_out', 'c_norm', 'c_w_pw1', 'c_b_pw1', 'c_dw_w', 'c_dw_b', 'c_ln_g', 'c_ln_b', 'c_w_pw2', 'c_b_pw2', 'xa_norm', 'xa_mem_norm', 'xa_wq', 'xa_wk', 'xa_wv', 'xa_wo', 'f_norm', 'f_w_up', 'f_dw_w', 'f_dw_b', 'f_w_down', 'final_norm']
TWIN_DIFF_INPUT = 'x'
TWIN_INPUTS = ['x', 'mem', 'ab_norm', 'ab_w_in', 'a_conv_w', 'a_conv_b', 'a_gate_x_w', 'a_gate_x_b', 'a_gate_a_w', 'a_gate_a_b', 'a_lambda', 'b_group_w', 'b_group_b', 'b_scale', 'ab_w_out', 'c_norm', 'c_w_pw1', 'c_b_pw1', 'c_dw_w', 'c_dw_b', 'c_ln_g', 'c_ln_b', 'c_w_pw2', 'c_b_pw2', 'xa_norm', 'xa_mem_norm', 'xa_wq', 'xa_wk', 'xa_wv', 'xa_wo', 'f_norm', 'f_w_up', 'f_dw_w', 'f_dw_b', 'f_w_down', 'final_norm', 'loss_target', 'm_ab_norm', 'm_ab_w_in', 'm_a_conv_w', 'm_a_conv_b', 'm_a_gate_x_w', 'm_a_gate_x_b', 'm_a_gate_a_w', 'm_a_gate_a_b', 'm_a_lambda', 'm_b_group_w', 'm_b_group_b', 'm_b_scale', 'm_ab_w_out', 'm_c_norm', 'm_c_w_pw1', 'm_c_b_pw1', 'm_c_dw_w', 'm_c_dw_b', 'm_c_ln_g', 'm_c_ln_b', 'm_c_w_pw2', 'm_c_b_pw2', 'm_xa_norm', 'm_xa_mem_norm', 'm_xa_wq', 'm_xa_wk', 'm_xa_wv', 'm_xa_wo', 'm_f_norm', 'm_f_w_up', 'm_f_dw_w', 'm_f_dw_b', 'm_f_w_down', 'm_final_norm', 'v_ab_norm', 'v_ab_w_in', 'v_a_conv_w', 'v_a_conv_b', 'v_a_gate_x_w', 'v_a_gate_x_b', 'v_a_gate_a_w', 'v_a_gate_a_b', 'v_a_lambda', 'v_b_group_w', 'v_b_group_b', 'v_b_scale', 'v_ab_w_out', 'v_c_norm', 'v_c_w_pw1', 'v_c_b_pw1', 'v_c_dw_w', 'v_c_dw_b', 'v_c_ln_g', 'v_c_ln_b', 'v_c_w_pw2', 'v_c_b_pw2', 'v_xa_norm', 'v_xa_mem_norm', 'v_xa_wq', 'v_xa_wk', 'v_xa_wv', 'v_xa_wo', 'v_f_norm', 'v_f_w_up', 'v_f_dw_w', 'v_f_dw_b', 'v_f_w_down', 'v_final_norm']
TWIN_OUTPUTS = ['loss', 'grad_x', 'grad_ab_norm', 'grad_ab_w_in', 'grad_a_conv_w', 'grad_a_conv_b', 'grad_a_gate_x_w', 'grad_a_gate_x_b', 'grad_a_gate_a_w', 'grad_a_gate_a_b', 'grad_a_lambda', 'grad_b_group_w', 'grad_b_group_b', 'grad_b_scale', 'grad_ab_w_out', 'grad_c_norm', 'grad_c_w_pw1', 'grad_c_b_pw1', 'grad_c_dw_w', 'grad_c_dw_b', 'grad_c_ln_g', 'grad_c_ln_b', 'grad_c_w_pw2', 'grad_c_b_pw2', 'grad_xa_norm', 'grad_xa_mem_norm', 'grad_xa_wq', 'grad_xa_wk', 'grad_xa_wv', 'grad_xa_wo', 'grad_f_norm', 'grad_f_w_up', 'grad_f_dw_w', 'grad_f_dw_b', 'grad_f_w_down', 'grad_final_norm', 'delta_ab_norm', 'delta_ab_w_in', 'delta_a_conv_w', 'delta_a_conv_b', 'delta_a_gate_x_w', 'delta_a_gate_x_b', 'delta_a_gate_a_w', 'delta_a_gate_a_b', 'delta_a_lambda', 'delta_b_group_w', 'delta_b_group_b', 'delta_b_scale', 'delta_ab_w_out', 'delta_c_norm', 'delta_c_w_pw1', 'delta_c_b_pw1', 'delta_c_dw_w', 'delta_c_dw_b', 'delta_c_ln_g', 'delta_c_ln_b', 'delta_c_w_pw2', 'delta_c_b_pw2', 'delta_xa_norm', 'delta_xa_mem_norm', 'delta_xa_wq', 'delta_xa_wk', 'delta_xa_wv', 'delta_xa_wo', 'delta_f_norm', 'delta_f_w_up', 'delta_f_dw_w', 'delta_f_dw_b', 'delta_f_w_down', 'delta_final_norm', 'new_m_ab_norm', 'new_m_ab_w_in', 'new_m_a_conv_w', 'new_m_a_conv_b', 'new_m_a_gate_x_w', 'new_m_a_gate_x_b', 'new_m_a_gate_a_w', 'new_m_a_gate_a_b', 'new_m_a_lambda', 'new_m_b_group_w', 'new_m_b_group_b', 'new_m_b_scale', 'new_m_ab_w_out', 'new_m_c_norm', 'new_m_c_w_pw1', 'new_m_c_b_pw1', 'new_m_c_dw_w', 'new_m_c_dw_b', 'new_m_c_ln_g', 'new_m_c_ln_b', 'new_m_c_w_pw2', 'new_m_c_b_pw2', 'new_m_xa_norm', 'new_m_xa_mem_norm', 'new_m_xa_wq', 'new_m_xa_wk', 'new_m_xa_wv', 'new_m_xa_wo', 'new_m_f_norm', 'new_m_f_w_up', 'new_m_f_dw_w', 'new_m_f_dw_b', 'new_m_f_w_down', 'new_m_final_norm', 'new_v_ab_norm', 'new_v_ab_w_in', 'new_v_a_conv_w', 'new_v_a_conv_b', 'new_v_a_gate_x_w', 'new_v_a_gate_x_b', 'new_v_a_gate_a_w', 'new_v_a_gate_a_b', 'new_v_a_lambda', 'new_v_b_group_w', 'new_v_b_group_b', 'new_v_b_scale', 'new_v_ab_w_out', 'new_v_c_norm', 'new_v_c_w_pw1', 'new_v_c_b_pw1', 'new_v_c_dw_w', 'new_v_c_dw_b', 'new_v_c_ln_g', 'new_v_c_ln_b', 'new_v_c_w_pw2', 'new_v_c_b_pw2', 'new_v_xa_norm', 'new_v_xa_mem_norm', 'new_v_xa_wq', 'new_v_xa_wk', 'new_v_xa_wv', 'new_v_xa_wo', 'new_v_f_norm', 'new_v_f_w_up', 'new_v_f_dw_w', 'new_v_f_dw_b', 'new_v_f_w_down', 'new_v_final_norm']
TWIN_LEAF_KINDS = {'loss': 'loss', 'grad_x': 'grad_x', 'grad_ab_norm': 'grad_w', 'grad_ab_w_in': 'grad_w', 'grad_a_conv_w': 'grad_w', 'grad_a_conv_b': 'grad_w', 'grad_a_gate_x_w': 'grad_w', 'grad_a_gate_x_b': 'grad_w', 'grad_a_gate_a_w': 'grad_w', 'grad_a_gate_a_b': 'grad_w', 'grad_a_lambda': 'grad_w', 'grad_b_group_w': 'grad_w', 'grad_b_group_b': 'grad_w', 'grad_b_scale': 'grad_w', 'grad_ab_w_out': 'grad_w', 'grad_c_norm': 'grad_w', 'grad_c_w_pw1': 'grad_w', 'grad_c_b_pw1': 'grad_w', 'grad_c_dw_w': 'grad_w', 'grad_c_dw_b': 'grad_w', 'grad_c_ln_g': 'grad_w', 'grad_c_ln_b': 'grad_w', 'grad_c_w_pw2': 'grad_w', 'grad_c_b_pw2': 'grad_w', 'grad_xa_norm': 'grad_w', 'grad_xa_mem_norm': 'grad_w', 'grad_xa_wq': 'grad_w', 'grad_xa_wk': 'grad_w', 'grad_xa_wv': 'grad_w', 'grad_xa_wo': 'grad_w', 'grad_f_norm': 'grad_w', 'grad_f_w_up': 'grad_w', 'grad_f_dw_w': 'grad_w', 'grad_f_dw_b': 'grad_w', 'grad_f_w_down': 'grad_w', 'grad_final_norm': 'grad_w', 'delta_ab_norm': 'delta_w', 'delta_ab_w_in': 'delta_w', 'delta_a_conv_w': 'delta_w', 'delta_a_conv_b': 'delta_w', 'delta_a_gate_x_w': 'delta_w', 'delta_a_gate_x_b': 'delta_w', 'delta_a_gate_a_w': 'delta_w', 'delta_a_gate_a_b': 'delta_w', 'delta_a_lambda': 'delta_w', 'delta_b_group_w': 'delta_w', 'delta_b_group_b': 'delta_w', 'delta_b_scale': 'delta_w', 'delta_ab_w_out': 'delta_w', 'delta_c_norm': 'delta_w', 'delta_c_w_pw1': 'delta_w', 'delta_c_b_pw1': 'delta_w', 'delta_c_dw_w': 'delta_w', 'delta_c_dw_b': 'delta_w', 'delta_c_ln_g': 'delta_w', 'delta_c_ln_b': 'delta_w', 'delta_c_w_pw2': 'delta_w', 'delta_c_b_pw2': 'delta_w', 'delta_xa_norm': 'delta_w', 'delta_xa_mem_norm': 'delta_w', 'delta_xa_wq': 'delta_w', 'delta_xa_wk': 'delta_w', 'delta_xa_wv': 'delta_w', 'delta_xa_wo': 'delta_w', 'delta_f_norm': 'delta_w', 'delta_f_w_up': 'delta_w', 'delta_f_dw_w': 'delta_w', 'delta_f_dw_b': 'delta_w', 'delta_f_w_down': 'delta_w', 'delta_final_norm': 'delta_w', 'new_m_ab_norm': 'new_m', 'new_m_ab_w_in': 'new_m', 'new_m_a_conv_w': 'new_m', 'new_m_a_conv_b': 'new_m', 'new_m_a_gate_x_w': 'new_m', 'new_m_a_gate_x_b': 'new_m', 'new_m_a_gate_a_w': 'new_m', 'new_m_a_gate_a_b': 'new_m', 'new_m_a_lambda': 'new_m', 'new_m_b_group_w': 'new_m', 'new_m_b_group_b': 'new_m', 'new_m_b_scale': 'new_m', 'new_m_ab_w_out': 'new_m', 'new_m_c_norm': 'new_m', 'new_m_c_w_pw1': 'new_m', 'new_m_c_b_pw1': 'new_m', 'new_m_c_dw_w': 'new_m', 'new_m_c_dw_b': 'new_m', 'new_m_c_ln_g': 'new_m', 'new_m_c_ln_b': 'new_m', 'new_m_c_w_pw2': 'new_m', 'new_m_c_b_pw2': 'new_m', 'new_m_xa_norm': 'new_m', 'new_m_xa_mem_norm': 'new_m', 'new_m_xa_wq': 'new_m', 'new_m_xa_wk': 'new_m', 'new_m_xa_wv': 'new_m', 'new_m_xa_wo': 'new_m', 'new_m_f_norm': 'new_m', 'new_m_f_w_up': 'new_m', 'new_m_f_dw_w': 'new_m', 'new_m_f_dw_b': 'new_m', 'new_m_f_w_down': 'new_m', 'new_m_final_norm': 'new_m', 'new_v_ab_norm': 'new_v', 'new_v_ab_w_in': 'new_v', 'new_v_a_conv_w': 'new_v', 'new_v_a_conv_b': 'new_v', 'new_v_a_gate_x_w': 'new_v', 'new_v_a_gate_x_b': 'new_v', 'new_v_a_gate_a_w': 'new_v', 'new_v_a_gate_a_b': 'new_v', 'new_v_a_lambda': 'new_v', 'new_v_b_group_w': 'new_v', 'new_v_b_group_b': 'new_v', 'new_v_b_scale': 'new_v', 'new_v_ab_w_out': 'new_v', 'new_v_c_norm': 'new_v', 'new_v_c_w_pw1': 'new_v', 'new_v_c_b_pw1': 'new_v', 'new_v_c_dw_w': 'new_v', 'new_v_c_dw_b': 'new_v', 'new_v_c_ln_g': 'new_v', 'new_v_c_ln_b': 'new_v', 'new_v_c_w_pw2': 'new_v', 'new_v_c_b_pw2': 'new_v', 'new_v_xa_norm': 'new_v', 'new_v_xa_mem_norm': 'new_v', 'new_v_xa_wq': 'new_v', 'new_v_xa_wk': 'new_v', 'new_v_xa_wv': 'new_v', 'new_v_xa_wo': 'new_v', 'new_v_f_norm': 'new_v', 'new_v_f_w_up': 'new_v', 'new_v_f_dw_w': 'new_v', 'new_v_f_dw_b': 'new_v', 'new_v_f_w_down': 'new_v', 'new_v_final_norm': 'new_v'}


def _forward(args):
    return _fwd_reference(*[args[k] for k in FWD_PARAMS])


def _output_shape():
    out = _jax.eval_shape(lambda: _forward(_fwd_setup_inputs(0)))
    return out.shape, out.dtype

N_MICROBATCH = 1
ADAM_LR = 0.001
ADAM_B1 = 0.9
ADAM_B2 = 0.999
ADAM_EPS = 1e-08
ADAM_WD = 0.01
ADAM_STEP = 10
PER_EXAMPLE_BATCH_AXIS = {'x': 0, 'mem': 0, 'loss_target': 0}
SHARED_INPUTS = []
_WEIGHT_DTYPES = {'ab_norm': _jnp.float32, 'ab_w_in': _jnp.float32, 'a_conv_w': _jnp.float32, 'a_conv_b': _jnp.float32, 'a_gate_x_w': _jnp.float32, 'a_gate_x_b': _jnp.float32, 'a_gate_a_w': _jnp.float32, 'a_gate_a_b': _jnp.float32, 'a_lambda': _jnp.float32, 'b_group_w': _jnp.float32, 'b_group_b': _jnp.float32, 'b_scale': _jnp.float32, 'ab_w_out': _jnp.float32, 'c_norm': _jnp.float32, 'c_w_pw1': _jnp.float32, 'c_b_pw1': _jnp.float32, 'c_dw_w': _jnp.float32, 'c_dw_b': _jnp.float32, 'c_ln_g': _jnp.float32, 'c_ln_b': _jnp.float32, 'c_w_pw2': _jnp.float32, 'c_b_pw2': _jnp.float32, 'xa_norm': _jnp.float32, 'xa_mem_norm': _jnp.float32, 'xa_wq': _jnp.float32, 'xa_wk': _jnp.float32, 'xa_wv': _jnp.float32, 'xa_wo': _jnp.float32, 'f_norm': _jnp.float32, 'f_w_up': _jnp.float32, 'f_dw_w': _jnp.float32, 'f_dw_b': _jnp.float32, 'f_w_down': _jnp.float32, 'final_norm': _jnp.float32}
MOMENT_SCALE = {'ab_norm': 1.439177e-01, 'ab_w_in': 8.685106e-02, 'a_conv_w': 6.929367e-02, 'a_conv_b': 7.949723e-01, 'a_gate_x_w': 4.244990e-02, 'a_gate_x_b': 2.319629e-02, 'a_gate_a_w': 2.409736e-02, 'a_gate_a_b': 1.926614e-02, 'a_lambda': 3.293795e-02, 'b_group_w': 1.357401e-01, 'b_group_b': 1.709237e-01, 'b_scale': 1.453496e-01, 'ab_w_out': 1.200693e-01, 'c_norm': 9.027172e-02, 'c_w_pw1': 6.198424e-02, 'c_b_pw1': 7.430739e-02, 'c_dw_w': 8.265905e-02, 'c_dw_b': 1.824747e-01, 'c_ln_g': 9.689747e-02, 'c_ln_b': 9.027412e-02, 'c_w_pw2': 7.922235e-02, 'c_b_pw2': 1.418535e-01, 'xa_norm': 1.763051e-02, 'xa_mem_norm': 2.584401e-02, 'xa_wq': 1.700647e-02, 'xa_wk': 1.709223e-02, 'xa_wv': 1.736670e-02, 'xa_wo': 1.735398e-02, 'f_norm': 1.260292e-01, 'f_w_up': 5.037623e-02, 'f_dw_w': 5.093899e-02, 'f_dw_b': 4.900272e-02, 'f_w_down': 8.575170e-02, 'final_norm': 3.193531e+01}


def _to_microbatches(a, axis):
    t = _jnp.moveaxis(a, axis, 0)
    t = t.reshape((N_MICROBATCH, t.shape[0] // N_MICROBATCH) + t.shape[1:])
    return _jnp.moveaxis(t, 1, axis + 1)


def setup_inputs(seed: int = 0) -> dict:
    inp = _fwd_setup_inputs(seed)
    key = _jax.random.fold_in(_jax.random.key(seed), 7919)
    shape, _ = _output_shape()
    out = dict(inp)
    out["loss_target"] = _jax.random.normal(_jax.random.fold_in(key, 0), shape, _jnp.float32)
    for i, name in enumerate(TWIN_WEIGHTS):
        w = inp[name].astype(_jnp.float32)
        if MOMENT_SCALE is None:
            s = _jnp.sqrt(_jnp.mean(_jnp.square(w)) + 1e-30)
        else:
            s = MOMENT_SCALE[name]
        km, kv = _jax.random.split(_jax.random.fold_in(key, i + 1))
        out[name] = w
        out["m_" + name] = s * _jax.random.normal(km, w.shape, _jnp.float32)
        out["v_" + name] = (s * s) * _jax.random.uniform(kv, w.shape, _jnp.float32, 0.5, 1.5)
    if N_MICROBATCH > 1:
        for name, axis in PER_EXAMPLE_BATCH_AXIS.items():
            out[name] = _to_microbatches(out[name], axis)
    return {'x': out['x'], 'mem': out['mem'], 'ab_norm': out['ab_norm'], 'ab_w_in': out['ab_w_in'], 'a_conv_w': out['a_conv_w'], 'a_conv_b': out['a_conv_b'], 'a_gate_x_w': out['a_gate_x_w'], 'a_gate_x_b': out['a_gate_x_b'], 'a_gate_a_w': out['a_gate_a_w'], 'a_gate_a_b': out['a_gate_a_b'], 'a_lambda': out['a_lambda'], 'b_group_w': out['b_group_w'], 'b_group_b': out['b_group_b'], 'b_scale': out['b_scale'], 'ab_w_out': out['ab_w_out'], 'c_norm': out['c_norm'], 'c_w_pw1': out['c_w_pw1'], 'c_b_pw1': out['c_b_pw1'], 'c_dw_w': out['c_dw_w'], 'c_dw_b': out['c_dw_b'], 'c_ln_g': out['c_ln_g'], 'c_ln_b': out['c_ln_b'], 'c_w_pw2': out['c_w_pw2'], 'c_b_pw2': out['c_b_pw2'], 'xa_norm': out['xa_norm'], 'xa_mem_norm': out['xa_mem_norm'], 'xa_wq': out['xa_wq'], 'xa_wk': out['xa_wk'], 'xa_wv': out['xa_wv'], 'xa_wo': out['xa_wo'], 'f_norm': out['f_norm'], 'f_w_up': out['f_w_up'], 'f_dw_w': out['f_dw_w'], 'f_dw_b': out['f_dw_b'], 'f_w_down': out['f_w_down'], 'final_norm': out['final_norm'], 'loss_target': out['loss_target'], 'm_ab_norm': out['m_ab_norm'], 'm_ab_w_in': out['m_ab_w_in'], 'm_a_conv_w': out['m_a_conv_w'], 'm_a_conv_b': out['m_a_conv_b'], 'm_a_gate_x_w': out['m_a_gate_x_w'], 'm_a_gate_x_b': out['m_a_gate_x_b'], 'm_a_gate_a_w': out['m_a_gate_a_w'], 'm_a_gate_a_b': out['m_a_gate_a_b'], 'm_a_lambda': out['m_a_lambda'], 'm_b_group_w': out['m_b_group_w'], 'm_b_group_b': out['m_b_group_b'], 'm_b_scale': out['m_b_scale'], 'm_ab_w_out': out['m_ab_w_out'], 'm_c_norm': out['m_c_norm'], 'm_c_w_pw1': out['m_c_w_pw1'], 'm_c_b_pw1': out['m_c_b_pw1'], 'm_c_dw_w': out['m_c_dw_w'], 'm_c_dw_b': out['m_c_dw_b'], 'm_c_ln_g': out['m_c_ln_g'], 'm_c_ln_b': out['m_c_ln_b'], 'm_c_w_pw2': out['m_c_w_pw2'], 'm_c_b_pw2': out['m_c_b_pw2'], 'm_xa_norm': out['m_xa_norm'], 'm_xa_mem_norm': out['m_xa_mem_norm'], 'm_xa_wq': out['m_xa_wq'], 'm_xa_wk': out['m_xa_wk'], 'm_xa_wv': out['m_xa_wv'], 'm_xa_wo': out['m_xa_wo'], 'm_f_norm': out['m_f_norm'], 'm_f_w_up': out['m_f_w_up'], 'm_f_dw_w': out['m_f_dw_w'], 'm_f_dw_b': out['m_f_dw_b'], 'm_f_w_down': out['m_f_w_down'], 'm_final_norm': out['m_final_norm'], 'v_ab_norm': out['v_ab_norm'], 'v_ab_w_in': out['v_ab_w_in'], 'v_a_conv_w': out['v_a_conv_w'], 'v_a_conv_b': out['v_a_conv_b'], 'v_a_gate_x_w': out['v_a_gate_x_w'], 'v_a_gate_x_b': out['v_a_gate_x_b'], 'v_a_gate_a_w': out['v_a_gate_a_w'], 'v_a_gate_a_b': out['v_a_gate_a_b'], 'v_a_lambda': out['v_a_lambda'], 'v_b_group_w': out['v_b_group_w'], 'v_b_group_b': out['v_b_group_b'], 'v_b_scale': out['v_b_scale'], 'v_ab_w_out': out['v_ab_w_out'], 'v_c_norm': out['v_c_norm'], 'v_c_w_pw1': out['v_c_w_pw1'], 'v_c_b_pw1': out['v_c_b_pw1'], 'v_c_dw_w': out['v_c_dw_w'], 'v_c_dw_b': out['v_c_dw_b'], 'v_c_ln_g': out['v_c_ln_g'], 'v_c_ln_b': out['v_c_ln_b'], 'v_c_w_pw2': out['v_c_w_pw2'], 'v_c_b_pw2': out['v_c_b_pw2'], 'v_xa_norm': out['v_xa_norm'], 'v_xa_mem_norm': out['v_xa_mem_norm'], 'v_xa_wq': out['v_xa_wq'], 'v_xa_wk': out['v_xa_wk'], 'v_xa_wv': out['v_xa_wv'], 'v_xa_wo': out['v_xa_wo'], 'v_f_norm': out['v_f_norm'], 'v_f_w_up': out['v_f_w_up'], 'v_f_dw_w': out['v_f_dw_w'], 'v_f_dw_b': out['v_f_dw_b'], 'v_f_w_down': out['v_f_w_down'], 'v_final_norm': out['v_final_norm']}


def _loss(weights, diff, rest, loss_target):
    with _jax.named_scope("forward"):
        args = {**rest, TWIN_DIFF_INPUT: diff, **{k: w.astype(_WEIGHT_DTYPES[k]) for k, w in weights.items()}}
        y = _forward(args)
    with _jax.named_scope("loss_head"):
        err = _jnp.square(y.astype(_jnp.float32) - loss_target)
        return 0.5 * _jnp.sum(_jnp.mean(err, axis=-1)) if err.ndim else 0.5 * err


def _adamw(w, g, m, v):
    m = ADAM_B1 * m + (1.0 - ADAM_B1) * g
    v = ADAM_B2 * v + (1.0 - ADAM_B2) * _jnp.square(g)
    m_hat = m / (1.0 - ADAM_B1 ** ADAM_STEP)
    v_hat = v / (1.0 - ADAM_B2 ** ADAM_STEP)
    delta = -ADAM_LR * (m_hat / (_jnp.sqrt(v_hat) + ADAM_EPS) + ADAM_WD * w)
    return delta, m, v


def reference(x, mem, ab_norm, ab_w_in, a_conv_w, a_conv_b, a_gate_x_w, a_gate_x_b, a_gate_a_w, a_gate_a_b, a_lambda, b_group_w, b_group_b, b_scale, ab_w_out, c_norm, c_w_pw1, c_b_pw1, c_dw_w, c_dw_b, c_ln_g, c_ln_b, c_w_pw2, c_b_pw2, xa_norm, xa_mem_norm, xa_wq, xa_wk, xa_wv, xa_wo, f_norm, f_w_up, f_dw_w, f_dw_b, f_w_down, final_norm, loss_target, m_ab_norm, m_ab_w_in, m_a_conv_w, m_a_conv_b, m_a_gate_x_w, m_a_gate_x_b, m_a_gate_a_w, m_a_gate_a_b, m_a_lambda, m_b_group_w, m_b_group_b, m_b_scale, m_ab_w_out, m_c_norm, m_c_w_pw1, m_c_b_pw1, m_c_dw_w, m_c_dw_b, m_c_ln_g, m_c_ln_b, m_c_w_pw2, m_c_b_pw2, m_xa_norm, m_xa_mem_norm, m_xa_wq, m_xa_wk, m_xa_wv, m_xa_wo, m_f_norm, m_f_w_up, m_f_dw_w, m_f_dw_b, m_f_w_down, m_final_norm, v_ab_norm, v_ab_w_in, v_a_conv_w, v_a_conv_b, v_a_gate_x_w, v_a_gate_x_b, v_a_gate_a_w, v_a_gate_a_b, v_a_lambda, v_b_group_w, v_b_group_b, v_b_scale, v_ab_w_out, v_c_norm, v_c_w_pw1, v_c_b_pw1, v_c_dw_w, v_c_dw_b, v_c_ln_g, v_c_ln_b, v_c_w_pw2, v_c_b_pw2, v_xa_norm, v_xa_mem_norm, v_xa_wq, v_xa_wk, v_xa_wv, v_xa_wo, v_f_norm, v_f_w_up, v_f_dw_w, v_f_dw_b, v_f_w_down, v_final_norm):
    given = dict(x=x, mem=mem, ab_norm=ab_norm, ab_w_in=ab_w_in, a_conv_w=a_conv_w, a_conv_b=a_conv_b, a_gate_x_w=a_gate_x_w, a_gate_x_b=a_gate_x_b, a_gate_a_w=a_gate_a_w, a_gate_a_b=a_gate_a_b, a_lambda=a_lambda, b_group_w=b_group_w, b_group_b=b_group_b, b_scale=b_scale, ab_w_out=ab_w_out, c_norm=c_norm, c_w_pw1=c_w_pw1, c_b_pw1=c_b_pw1, c_dw_w=c_dw_w, c_dw_b=c_dw_b, c_ln_g=c_ln_g, c_ln_b=c_ln_b, c_w_pw2=c_w_pw2, c_b_pw2=c_b_pw2, xa_norm=xa_norm, xa_mem_norm=xa_mem_norm, xa_wq=xa_wq, xa_wk=xa_wk, xa_wv=xa_wv, xa_wo=xa_wo, f_norm=f_norm, f_w_up=f_w_up, f_dw_w=f_dw_w, f_dw_b=f_dw_b, f_w_down=f_w_down, final_norm=final_norm, loss_target=loss_target, m_ab_norm=m_ab_norm, m_ab_w_in=m_ab_w_in, m_a_conv_w=m_a_conv_w, m_a_conv_b=m_a_conv_b, m_a_gate_x_w=m_a_gate_x_w, m_a_gate_x_b=m_a_gate_x_b, m_a_gate_a_w=m_a_gate_a_w, m_a_gate_a_b=m_a_gate_a_b, m_a_lambda=m_a_lambda, m_b_group_w=m_b_group_w, m_b_group_b=m_b_group_b, m_b_scale=m_b_scale, m_ab_w_out=m_ab_w_out, m_c_norm=m_c_norm, m_c_w_pw1=m_c_w_pw1, m_c_b_pw1=m_c_b_pw1, m_c_dw_w=m_c_dw_w, m_c_dw_b=m_c_dw_b, m_c_ln_g=m_c_ln_g, m_c_ln_b=m_c_ln_b, m_c_w_pw2=m_c_w_pw2, m_c_b_pw2=m_c_b_pw2, m_xa_norm=m_xa_norm, m_xa_mem_norm=m_xa_mem_norm, m_xa_wq=m_xa_wq, m_xa_wk=m_xa_wk, m_xa_wv=m_xa_wv, m_xa_wo=m_xa_wo, m_f_norm=m_f_norm, m_f_w_up=m_f_w_up, m_f_dw_w=m_f_dw_w, m_f_dw_b=m_f_dw_b, m_f_w_down=m_f_w_down, m_final_norm=m_final_norm, v_ab_norm=v_ab_norm, v_ab_w_in=v_ab_w_in, v_a_conv_w=v_a_conv_w, v_a_conv_b=v_a_conv_b, v_a_gate_x_w=v_a_gate_x_w, v_a_gate_x_b=v_a_gate_x_b, v_a_gate_a_w=v_a_gate_a_w, v_a_gate_a_b=v_a_gate_a_b, v_a_lambda=v_a_lambda, v_b_group_w=v_b_group_w, v_b_group_b=v_b_group_b, v_b_scale=v_b_scale, v_ab_w_out=v_ab_w_out, v_c_norm=v_c_norm, v_c_w_pw1=v_c_w_pw1, v_c_b_pw1=v_c_b_pw1, v_c_dw_w=v_c_dw_w, v_c_dw_b=v_c_dw_b, v_c_ln_g=v_c_ln_g, v_c_ln_b=v_c_ln_b, v_c_w_pw2=v_c_w_pw2, v_c_b_pw2=v_c_b_pw2, v_xa_norm=v_xa_norm, v_xa_mem_norm=v_xa_mem_norm, v_xa_wq=v_xa_wq, v_xa_wk=v_xa_wk, v_xa_wv=v_xa_wv, v_xa_wo=v_xa_wo, v_f_norm=v_f_norm, v_f_w_up=v_f_w_up, v_f_dw_w=v_f_dw_w, v_f_dw_b=v_f_dw_b, v_f_w_down=v_f_w_down, v_final_norm=v_final_norm)
    weights = {n: given[n] for n in TWIN_WEIGHTS}
    shared = {n: given[n] for n in SHARED_INPUTS}
    per_example = {n: given[n] for n in ['x', 'mem']}
    grad_fn = _jax.value_and_grad(_loss, argnums=(0, 1))

    def one_microbatch(ex, loss_target):
        ex = dict(ex)
        diff = ex.pop(TWIN_DIFF_INPUT)
        return grad_fn(weights, diff, {**shared, **ex}, loss_target)

    if N_MICROBATCH == 1:
        loss, (grad_w, grad_x) = one_microbatch(per_example, given["loss_target"])
    else:
        def body(carry, xs):
            loss_sum, grad_sum = carry
            l_k, (gw_k, gx_k) = one_microbatch(xs[0], xs[1])
            with _jax.named_scope("update"):
                return (loss_sum + l_k, _jax.tree.map(_jnp.add, grad_sum, gw_k)), gx_k

        init = (_jnp.zeros((), _jnp.float32), _jax.tree.map(_jnp.zeros_like, weights))
        (loss, grad_w), grad_x = _jax.lax.scan(body, init, (per_example, given["loss_target"]))
    with _jax.named_scope("update"):
        delta_w, new_m, new_v = {}, {}, {}
        for n in TWIN_WEIGHTS:
            delta_w[n], new_m[n], new_v[n] = _adamw(weights[n], grad_w[n], given["m_" + n], given["v_" + n])
    return (loss, grad_x, *[grad_w[n] for n in TWIN_WEIGHTS], *[delta_w[n] for n in TWIN_WEIGHTS],
            *[new_m[n] for n in TWIN_WEIGHTS], *[new_v[n] for n in TWIN_WEIGHTS])
```

```python
import functools

import jax
import jax.numpy as jnp
from jax import lax
from jax.experimental import pallas as pl
from jax.experimental.pallas import tpu as pltpu

F32, BF16 = jnp.float32, jnp.bfloat16
SDS = jax.ShapeDtypeStruct
MESH = pl.DeviceIdType.MESH

N_DEV = 8
D = 1024
N_MEM = 256
XA_HEADS, XA_HD = 4, 256
HD_A = 128
CONV_A, CONV_C, CONV_F = 4, 31, 3
C_RG = 8.0
POOL_WINDOWS = (2, 4, 8, 16)
D_FF = 3 * D
EPS = 1e-6
ADAM_LR, ADAM_B1, ADAM_B2, ADAM_EPS, ADAM_WD, ADAM_STEP = 0.001, 0.9, 0.999, 1e-08, 0.01, 10

LANE = 128
SUB = 8
VMEM_LIMIT = 56 * 1024 * 1024
R_SEQ = 256
TM_ROW = 512


def _cp(n_axes):
    return pltpu.CompilerParams(dimension_semantics=("arbitrary",) * n_axes, vmem_limit_bytes=VMEM_LIMIT)


def _tile(n, pref):
    if n <= pref:
        return n
    best = None
    for t in range(LANE, pref + 1, LANE):
        if n % t == 0:
            best = t
    assert best is not None, (n, pref)
    return best


def _perm2(n):
    return (n % 2) * 4 + n // 2


_NN = (((1,), (0,)), ((), ()))
_NT = (((1,), (1,)), ((), ()))
_TN = (((0,), (0,)), ((), ()))


def _mm_call(name, grid, a, b, a_spec, b_spec, o_spec, out_shape, dims, acc_shape, extras=()):
    nk = grid[2]
    n_ex = len(extras)

    def body(a_ref, b_ref, *rest):
        ex_refs, o_ref, acc = rest[:n_ex], rest[n_ex], rest[n_ex + 1]
        k = pl.program_id(2)

        @pl.when(k == 0)
        def _():
            acc[...] = jnp.zeros_like(acc)

        acc[...] += lax.dot_general(a_ref[...], b_ref[...], dims, preferred_element_type=F32)

        @pl.when(k == nk - 1)
        def _():
            r = acc[...]
            for e in ex_refs:
                r = r + e[...]
            o_ref[...] = r.astype(o_ref.dtype)

    return pl.pallas_call(
        body, out_shape=out_shape, grid=grid,
        in_specs=[a_spec, b_spec] + [s for _, s in extras], out_specs=o_spec,
        scratch_shapes=[pltpu.VMEM(acc_shape, F32)], name=name, compiler_params=_cp(3),
    )(a, b, *[e for e, _ in extras])


def _mm_nn(a, b, *, out_dtype, name, bias=None, add=None):
    M, K = a.shape
    tm, tk = _tile(M, 1024), _tile(K, 512)
    if b.ndim == 3:
        nb, _, bw = b.shape
        N, tn, nn = nb * bw, bw, nb
        b_spec = pl.BlockSpec((None, tk, bw), lambda m, n, k: (_perm2(n), k, 0))
    else:
        N = b.shape[1]
        tn = _tile(N, 1024)
        nn = N // tn
        b_spec = pl.BlockSpec((tk, tn), lambda m, n, k: (k, n))
    extras = []
    if bias is not None:
        extras.append((bias, pl.BlockSpec((1, tn), lambda m, n, k: (0, n))))
    if add is not None:
        extras.append((add, pl.BlockSpec((tm, tn), lambda m, n, k: (m, n))))
    return _mm_call(name, (M // tm, nn, K // tk), a, b, pl.BlockSpec((tm, tk), lambda m, n, k: (m, k)), b_spec,
                    pl.BlockSpec((tm, tn), lambda m, n, k: (m, n)), SDS((M, N), out_dtype), _NN, (tm, tn), extras)


def _mm_nt(a, b, *, out_dtype, name, add=None):
    M, N = a.shape
    tm = _tile(M, 1024)
    if b.ndim == 3:
        nb, Ko, bw = b.shape
        tn, tk, nk = _tile(Ko, 1024), bw, nb
        b_spec = pl.BlockSpec((None, tn, bw), lambda m, n, k: (_perm2(k), n, 0))
    else:
        Ko = b.shape[0]
        tn, tk = _tile(Ko, 1024), _tile(N, 512)
        nk = N // tk
        b_spec = pl.BlockSpec((tn, tk), lambda m, n, k: (n, k))
    extras = []
    if add is not None:
        extras.append((add, pl.BlockSpec((tm, tn), lambda m, n, k: (m, n))))
    return _mm_call(name, (M // tm, Ko // tn, nk), a, b, pl.BlockSpec((tm, tk), lambda m, n, k: (m, k)), b_spec,
                    pl.BlockSpec((tm, tn), lambda m, n, k: (m, n)), SDS((M, Ko), out_dtype), _NT, (tm, tn), extras)


def _mm_tn(a, b, *, out_dtype, name, blocks=None):
    S, Ka = a.shape
    Nb = b.shape[1]
    tm, tk = _tile(Ka, 1024), _tile(S, 512)
    if blocks is not None:
        bw = blocks
        tn, nn = bw, Nb // bw
        o_spec = pl.BlockSpec((None, tm, bw), lambda m, n, k: (_perm2(n), m, 0))
        out_shape = SDS((nn, Ka, bw), out_dtype)
    else:
        tn = _tile(Nb, 1024)
        nn = Nb // tn
        o_spec = pl.BlockSpec((tm, tn), lambda m, n, k: (m, n))
        out_shape = SDS((Ka, Nb), out_dtype)
    return _mm_call(name, (Ka // tm, nn, S // tk), a, b, pl.BlockSpec((tk, tm), lambda m, n, k: (k, m)),
                    pl.BlockSpec((tk, tn), lambda m, n, k: (k, n)), o_spec, out_shape, _TN, (tm, tn))


def _row(tm, c):
    return pl.BlockSpec((tm, c), lambda i: (i, 0))


def _full(shape):
    nd = len(shape)
    return pl.BlockSpec(shape, lambda i: (0,) * nd)


def _rms_fwd(x, g, name):
    S = x.shape[0]
    tm = min(S, TM_ROW)

    def body(x_ref, g_ref, o_ref):
        xf = x_ref[...]
        r = lax.rsqrt(jnp.mean(xf * xf, axis=-1, keepdims=True) + EPS)
        o_ref[...] = ((xf * r) * g_ref[...]).astype(BF16)

    return pl.pallas_call(body, out_shape=SDS((S, D), BF16), grid=(S // tm,), in_specs=[_row(tm, D), _full((1, D))],
                          out_specs=_row(tm, D), name=name, compiler_params=_cp(1))(x, g)


def _rms_bwd(x, g, dn, dres, name):
    S = x.shape[0]
    tm = min(S, TM_ROW)
    want_dx = dres is not None

    def body(x_ref, g_ref, dn_ref, *rest):
        i = pl.program_id(0)
        dg_ref = rest[-1]

        @pl.when(i == 0)
        def _():
            dg_ref[...] = jnp.zeros_like(dg_ref)

        xf = x_ref[...]
        r = lax.rsqrt(jnp.mean(xf * xf, axis=-1, keepdims=True) + EPS)
        y = xf * r
        dn_v = dn_ref[...]
        dg_ref[...] += jnp.sum(dn_v * y, axis=0, keepdims=True)
        if want_dx:
            dres_ref, dx_ref, dxb_ref = rest[0], rest[1], rest[2]
            dy = dn_v * g_ref[...]
            dx = r * (dy - y * jnp.mean(dy * y, axis=-1, keepdims=True)) + dres_ref[...]
            dx_ref[...] = dx
            dxb_ref[...] = dx.astype(BF16)

    ins = [x, g, dn] + ([dres] if want_dx else [])
    in_specs = [_row(tm, D), _full((1, D)), _row(tm, D)] + ([_row(tm, D)] if want_dx else [])
    outs = ([SDS((S, D), F32), SDS((S, D), BF16)] if want_dx else []) + [SDS((1, D), F32)]
    out_specs = ([_row(tm, D), _row(tm, D)] if want_dx else []) + [_full((1, D))]
    return pl.pallas_call(body, out_shape=outs, grid=(S // tm,), in_specs=in_specs, out_specs=out_specs, name=name,
                          compiler_params=_cp(1))(*ins)


def _loss_head(x, g, tgt):
    S = x.shape[0]
    tm = min(S, TM_ROW)

    def body(x_ref, g_ref, t_ref, loss_ref, dx_ref, dxb_ref, dg_ref):
        i = pl.program_id(0)

        @pl.when(i == 0)
        def _():
            loss_ref[...] = jnp.zeros_like(loss_ref)
            dg_ref[...] = jnp.zeros_like(dg_ref)

        xf = x_ref[...]
        r = lax.rsqrt(jnp.mean(xf * xf, axis=-1, keepdims=True) + EPS)
        y = xf * r
        gv = g_ref[...]
        err = y * gv - t_ref[...]
        per_row = jnp.mean(err * err, axis=-1, keepdims=True)
        loss_ref[...] += 0.5 * jnp.sum(per_row, axis=0, keepdims=True)
        dn_v = err * (1.0 / D)
        dg_ref[...] += jnp.sum(dn_v * y, axis=0, keepdims=True)
        dy = dn_v * gv
        dx = r * (dy - y * jnp.mean(dy * y, axis=-1, keepdims=True))
        dx_ref[...] = dx
        dxb_ref[...] = dx.astype(BF16)

    return pl.pallas_call(
        body, out_shape=[SDS((1, 1), F32), SDS((S, D), F32), SDS((S, D), BF16), SDS((1, D), F32)], grid=(S // tm,),
        in_specs=[_row(tm, D), _full((1, D)), _row(tm, D)],
        out_specs=[_full((1, 1)), _row(tm, D), _row(tm, D), _full((1, D))], name="loss_head", compiler_params=_cp(1),
    )(x, g, tgt)


def _softmax_rows(s):
    m = jnp.max(s, axis=-1, keepdims=True)
    e = jnp.exp(s - m)
    return e / jnp.sum(e, axis=-1, keepdims=True)


def _attn_fwd(q, k, v, name):
    S = q.shape[0]
    tm = min(S, TM_ROW)
    scale = XA_HD ** -0.5

    def body(q_ref, k_ref, v_ref, o_ref):
        for h in range(XA_HEADS):
            sl = slice(h * XA_HD, (h + 1) * XA_HD)
            s = lax.dot_general(q_ref[:, sl], k_ref[:, sl], _NT, preferred_element_type=F32) * scale
            p = _softmax_rows(s)
            o_ref[:, sl] = lax.dot_general(p.astype(BF16), v_ref[:, sl], _NN, preferred_element_type=F32).astype(BF16)

    return pl.pallas_call(body, out_shape=SDS((S, D), BF16), grid=(S // tm,),
                          in_specs=[_row(tm, D), _full((N_MEM, D)), _full((N_MEM, D))], out_specs=_row(tm, D),
                          name=name, compiler_params=_cp(1))(q, k, v)


def _attn_bwd(q, k, v, do, name):
    S = q.shape[0]
    tm = min(S, TM_ROW)
    scale = XA_HD ** -0.5

    def body(q_ref, k_ref, v_ref, do_ref, dq_ref, dk_ref, dv_ref):
        i = pl.program_id(0)

        @pl.when(i == 0)
        def _():
            dk_ref[...] = jnp.zeros_like(dk_ref)
            dv_ref[...] = jnp.zeros_like(dv_ref)

        for h in range(XA_HEADS):
            sl = slice(h * XA_HD, (h + 1) * XA_HD)
            qh, kh, vh, doh = q_ref[:, sl], k_ref[:, sl], v_ref[:, sl], do_ref[:, sl]
            s = lax.dot_general(qh, kh, _NT, preferred_element_type=F32) * scale
            p = _softmax_rows(s)
            pb = p.astype(BF16)
            dv_ref[:, sl] += lax.dot_general(pb, doh, _TN, preferred_element_type=F32)
            dp = lax.dot_general(doh, vh, _NT, preferred_element_type=F32)
            ds = (p * (dp - jnp.sum(dp * p, axis=-1, keepdims=True)) * scale).astype(BF16)
            dq_ref[:, sl] = lax.dot_general(ds, kh, _NN, preferred_element_type=F32).astype(BF16)
            dk_ref[:, sl] += lax.dot_general(ds, qh, _TN, preferred_element_type=F32)

    return pl.pallas_call(
        body, out_shape=[SDS((S, D), BF16), SDS((N_MEM, D), F32), SDS((N_MEM, D), F32)], grid=(S // tm,),
        in_specs=[_row(tm, D), _full((N_MEM, D)), _full((N_MEM, D)), _row(tm, D)],
        out_specs=[_row(tm, D), _full((N_MEM, D)), _full((N_MEM, D))], name=name, compiler_params=_cp(1),
    )(q, k, v, do)


def _sigmoid(x):
    return 1.0 / (1.0 + jnp.exp(-x))


def _ln_silu_fwd(cv, g, b):
    S = cv.shape[0]
    tm = min(S, TM_ROW)

    def body(x_ref, g_ref, b_ref, o_ref):
        xf = x_ref[...]
        mu = jnp.mean(xf, axis=-1, keepdims=True)
        xc = xf - mu
        rstd = lax.rsqrt(jnp.mean(xc * xc, axis=-1, keepdims=True) + EPS)
        ln = (xc * rstd) * g_ref[...] + b_ref[...]
        o_ref[...] = (ln * _sigmoid(ln)).astype(BF16)

    return pl.pallas_call(body, out_shape=SDS((S, D), BF16), grid=(S // tm,),
                          in_specs=[_row(tm, D), _full((1, D)), _full((1, D))], out_specs=_row(tm, D),
                          name="ln_silu_fwd", compiler_params=_cp(1))(cv, g, b)


def _ln_silu_bwd(ds, cv, g, b, dx):
    S = cv.shape[0]
    tm = min(S, TM_ROW)

    def body(ds_ref, x_ref, g_ref, b_ref, dx_ref, dcv_ref, dg_ref, db_ref, db2_ref):
        i = pl.program_id(0)

        @pl.when(i == 0)
        def _():
            dg_ref[...] = jnp.zeros_like(dg_ref)
            db_ref[...] = jnp.zeros_like(db_ref)
            db2_ref[...] = jnp.zeros_like(db2_ref)

        xf = x_ref[...]
        mu = jnp.mean(xf, axis=-1, keepdims=True)
        xc = xf - mu
        rstd = lax.rsqrt(jnp.mean(xc * xc, axis=-1, keepdims=True) + EPS)
        xhat = xc * rstd
        gv = g_ref[...]
        ln = xhat * gv + b_ref[...]
        sg = _sigmoid(ln)
        dln = ds_ref[...] * (sg + ln * sg * (1.0 - sg))
        dg_ref[...] += jnp.sum(dln * xhat, axis=0, keepdims=True)
        db_ref[...] += jnp.sum(dln, axis=0, keepdims=True)
        db2_ref[...] += jnp.sum(dx_ref[...], axis=0, keepdims=True)
        dxh = dln * gv
        dcv_ref[...] = rstd * (dxh - jnp.mean(dxh, axis=-1, keepdims=True)
                               - xhat * jnp.mean(dxh * xhat, axis=-1, keepdims=True))

    return pl.pallas_call(
        body, out_shape=[SDS((S, D), F32), SDS((1, D), F32), SDS((1, D), F32), SDS((1, D), F32)], grid=(S // tm,),
        in_specs=[_row(tm, D), _row(tm, D), _full((1, D)), _full((1, D)), _row(tm, D)],
        out_specs=[_row(tm, D), _full((1, D)), _full((1, D)), _full((1, D))], name="ln_silu_bwd",
        compiler_params=_cp(1),
    )(ds, cv, g, b, dx)


_GELU_C, _GELU_K = 0.7978845608028654, 0.044715


def _gelu(x, with_grad=False):
    x2 = x * x
    t = jnp.tanh(_GELU_C * (x + _GELU_K * x * x2))
    gel = 0.5 * x * (1.0 + t)
    if not with_grad:
        return gel
    return gel, 0.5 * (1.0 + t) + 0.5 * x * (1.0 - t * t) * (_GELU_C * (1.0 + 3.0 * _GELU_K * x2))


def _expm1(x):
    poly = x * (1.0 + x * (0.5 + x * (1.0 / 6.0 + x * (1.0 / 24.0 + x * (1.0 / 120.0)))))
    return jnp.where(jnp.abs(x) < 0.05, poly, jnp.exp(x) - 1.0)


def _softplus(x):
    return jnp.maximum(x, 0.0) + jnp.log1p(jnp.exp(-jnp.abs(x)))


def _scan_fwd(a_s, b_s, out_ref, carry_ref, n_groups):
    row = lax.broadcasted_iota(jnp.int32, (SUB, LANE), 0)

    def step(g, carry):
        i = pl.multiple_of(g * SUB, SUB)
        a8, b8 = a_s[pl.ds(i, SUB), :], b_s[pl.ds(i, SUB), :]
        for s in (1, 2, 4):
            a_sh = jnp.where(row >= s, pltpu.roll(a8, s, 0), 1.0)
            b_sh = jnp.where(row >= s, pltpu.roll(b8, s, 0), 0.0)
            b8 = a8 * b_sh + b8
            a8 = a8 * a_sh
        h8 = a8 * carry + b8
        out_ref[pl.ds(i, SUB), :] = h8
        return jnp.broadcast_to(h8[SUB - 1:SUB, :], (SUB, LANE))

    carry_ref[...] = lax.fori_loop(0, n_groups, step, carry_ref[...])


def _scan_bwd(a_s, b_s, out_ref, carry_ref, n_groups):
    row = lax.broadcasted_iota(jnp.int32, (SUB, LANE), 0)

    def step(gi, carry):
        i = pl.multiple_of((n_groups - 1 - gi) * SUB, SUB)
        a8, b8 = a_s[pl.ds(i, SUB), :], b_s[pl.ds(i, SUB), :]
        for s in (1, 2, 4):
            a_sh = jnp.where(row < SUB - s, pltpu.roll(a8, SUB - s, 0), 1.0)
            b_sh = jnp.where(row < SUB - s, pltpu.roll(b8, SUB - s, 0), 0.0)
            b8 = a8 * b_sh + b8
            a8 = a8 * a_sh
        h8 = a8 * carry + b8
        out_ref[pl.ds(i, SUB), :] = h8
        return jnp.broadcast_to(h8[0:1, :], (SUB, LANE))

    carry_ref[...] = lax.fori_loop(0, n_groups, step, carry_ref[...])


def _rglru_pre(xr, wgx_ref, bgx_ref, wga_ref, bga_ref, lam_ref):
    xrb = xr.astype(BF16)
    wgx, wga = wgx_ref[0].astype(BF16), wga_ref[0].astype(BF16)
    gx = _sigmoid(lax.dot_general(xrb, wgx, _NN, preferred_element_type=F32) + bgx_ref[...])
    ga = _sigmoid(lax.dot_general(xrb, wga, _NN, preferred_element_type=F32) + bga_ref[...])
    sp = _softplus(-lam_ref[...])
    log_a = -C_RG * ga * sp
    a = jnp.exp(log_a)
    mult = jnp.sqrt(-_expm1(2.0 * log_a))
    return gx, ga, sp, a, mult, xrb, wgx, wga


def _a_specs():
    vec = pl.BlockSpec((1, HD_A), lambda c, j: (0, c))
    mat = pl.BlockSpec((1, HD_A, HD_A), lambda c, j: (c, 0, 0))
    return [pl.BlockSpec((CONV_A, HD_A), lambda c, j: (0, c)), vec, mat, vec, mat, vec, vec]


def _a_fwd(zp, conv_w, conv_b, wgx, bgx, wga, bga, lam):
    S = zp.shape[0]
    R, nt = R_SEQ, D // HD_A
    H = SUB

    def body(z_ref, cw_ref, cb_ref, wgx_ref, bgx_ref, wga_ref, bga_ref, lam_ref, ya_ref, h_ref, ext, a_s, b_s, hc):
        j = pl.program_id(1)

        @pl.when(j == 0)
        def _():
            ext[0:H, :] = jnp.zeros((H, HD_A), F32)
            hc[...] = jnp.zeros_like(hc)

        ext[H:H + R, :] = z_ref[:, HD_A:2 * HD_A]
        xr = cb_ref[...]
        for k in range(CONV_A):
            xr = xr + cw_ref[k:k + 1, :] * ext[pl.ds(H - (CONV_A - 1 - k), R), :]
        gx, _, _, a, mult, _, _, _ = _rglru_pre(xr, wgx_ref, bgx_ref, wga_ref, bga_ref, lam_ref)
        a_s[...] = a
        b_s[...] = mult * (gx * xr)
        _scan_fwd(a_s, b_s, h_ref, hc, R // SUB)
        ya_ref[...] = (_gelu(z_ref[:, 0:HD_A]) * h_ref[...]).astype(BF16)
        ext[0:H, :] = ext[R:R + H, :]

    return pl.pallas_call(
        body, out_shape=[SDS((S, D + D // 2), BF16), SDS((S, D), F32)], grid=(nt, S // R),
        in_specs=[pl.BlockSpec((R, 2 * HD_A), lambda c, j: (j, c))] + _a_specs(),
        out_specs=[pl.BlockSpec((R, HD_A), lambda c, j: (j, c)), pl.BlockSpec((R, HD_A), lambda c, j: (j, c))],
        scratch_shapes=[pltpu.VMEM((H + R, HD_A), F32), pltpu.VMEM((R, HD_A), F32), pltpu.VMEM((R, HD_A), F32),
                        pltpu.VMEM((SUB, HD_A), F32)],
        name="rglru_fwd", compiler_params=_cp(2),
    )(zp, conv_w, conv_b, wgx, bgx, wga, bga, lam)


def _a_bwd(dyab, zp, h, conv_w, conv_b, wgx, bgx, wga, bga, lam):
    S = zp.shape[0]
    R, nt, nch = R_SEQ, D // HD_A, S // R_SEQ
    H = SUB

    def rows(c, j):
        return (nch - 1 - j, c)

    def halo(c, j):
        return (jnp.maximum((nch - 1 - j) * (R // H) - 1, 0), c)

    def body(dy_ref, z_ref, zh_ref, h_ref, hh_ref, cw_ref, cb_ref, wgx_ref, bgx_ref, wga_ref, bga_ref, lam_ref,
             dz_ref, dcw_ref, dcb_ref, dwgx_ref, dbgx_ref, dwga_ref, dbga_ref, dlam_ref,
             ext_z, ext_h, ext_mu, ext_d, a_s, b_s, muc):
        j = pl.program_id(1)
        first_chunk = (nch - 1 - j) == 0

        @pl.when(j == 0)
        def _():
            ext_mu[R:R + H, :] = jnp.zeros((H, HD_A), F32)
            ext_d[R:R + H, :] = jnp.zeros((H, HD_A), F32)
            muc[...] = jnp.zeros_like(muc)
            for r in (dcw_ref, dcb_ref, dwgx_ref, dbgx_ref, dwga_ref, dbga_ref, dlam_ref):
                r[...] = jnp.zeros_like(r)

        zg = z_ref[:, 0:HD_A]
        ext_z[0:H, :] = jnp.where(first_chunk, 0.0, zh_ref[:, HD_A:2 * HD_A])
        ext_z[H:H + R, :] = z_ref[:, HD_A:2 * HD_A]
        ext_h[0:H, :] = jnp.where(first_chunk, 0.0, hh_ref[...])
        ext_h[H:H + R, :] = h_ref[...]
        xr = cb_ref[...]
        for k in range(CONV_A):
            xr = xr + cw_ref[k:k + 1, :] * ext_z[pl.ds(H - (CONV_A - 1 - k), R), :]
        gx, ga, sp, a, mult, xrb, wgxb, wgab = _rglru_pre(xr, wgx_ref, bgx_ref, wga_ref, bga_ref, lam_ref)
        gel, dgel = _gelu(zg, with_grad=True)
        dy = dy_ref[...]
        dh = dy * gel
        dz_ref[:, 0:HD_A] = (dy * h_ref[...] * dgel).astype(BF16)
        a_s[...] = a
        b_s[...] = a * dh
        _scan_bwd(a_s, b_s, ext_mu, muc, R // SUB)
        lam_t = dh + ext_mu[pl.ds(1, R), :]
        ext_mu[R:R + H, :] = ext_mu[0:H, :]
        da = lam_t * ext_h[pl.ds(H - 1, R), :]
        gxr = gx * xr
        dlog_a = da * a - (lam_t * gxr) * (a * a) / mult
        dgx = lam_t * mult * xr
        dxr = lam_t * mult * gx
        lam_v = lam_ref[...]
        dlam_ref[...] += jnp.sum(dlog_a * ga, axis=0, keepdims=True) * (C_RG * _sigmoid(-lam_v))
        dpa = (dlog_a * (-C_RG * sp)) * ga * (1.0 - ga)
        dpx = dgx * gx * (1.0 - gx)
        dbga_ref[...] += jnp.sum(dpa, axis=0, keepdims=True)
        dbgx_ref[...] += jnp.sum(dpx, axis=0, keepdims=True)
        dpab, dpxb = dpa.astype(BF16), dpx.astype(BF16)
        dwga_ref[0] += lax.dot_general(xrb, dpab, _TN, preferred_element_type=F32)
        dwgx_ref[0] += lax.dot_general(xrb, dpxb, _TN, preferred_element_type=F32)
        dxr = (dxr + lax.dot_general(dpab, wgab, _NT, preferred_element_type=F32)
               + lax.dot_general(dpxb, wgxb, _NT, preferred_element_type=F32))
        dcb_ref[...] += jnp.sum(dxr, axis=0, keepdims=True)
        ext_d[0:R, :] = dxr
        dzr = jnp.zeros((R, HD_A), F32)
        for k in range(CONV_A):
            sh = CONV_A - 1 - k
            dcw_ref[k:k + 1, :] += jnp.sum(dxr * ext_z[pl.ds(H - sh, R), :], axis=0, keepdims=True)
            dzr = dzr + cw_ref[k:k + 1, :] * ext_d[pl.ds(sh, R), :]
        dz_ref[:, HD_A:2 * HD_A] = dzr.astype(BF16)
        ext_d[R:R + H, :] = ext_d[0:H, :]

    vec_o = pl.BlockSpec((1, HD_A), lambda c, j: (0, c))
    mat_o = pl.BlockSpec((1, HD_A, HD_A), lambda c, j: (c, 0, 0))
    return pl.pallas_call(
        body,
        out_shape=[SDS((S, 2 * D + D // 2), BF16), SDS((CONV_A, D), F32), SDS((1, D), F32), SDS((nt, HD_A, HD_A), F32),
                   SDS((1, D), F32), SDS((nt, HD_A, HD_A), F32), SDS((1, D), F32), SDS((1, D), F32)],
        grid=(nt, nch),
        in_specs=[pl.BlockSpec((R, HD_A), rows), pl.BlockSpec((R, 2 * HD_A), rows), pl.BlockSpec((H, 2 * HD_A), halo),
                  pl.BlockSpec((R, HD_A), rows), pl.BlockSpec((H, HD_A), halo)] + _a_specs(),
        out_specs=[pl.BlockSpec((R, 2 * HD_A), rows), pl.BlockSpec((CONV_A, HD_A), lambda c, j: (0, c)), vec_o, mat_o,
                   vec_o, mat_o, vec_o, vec_o],
        scratch_shapes=[pltpu.VMEM((H + R, HD_A), F32), pltpu.VMEM((H + R, HD_A), F32), pltpu.VMEM((R + H, HD_A), F32),
                        pltpu.VMEM((R + H, HD_A), F32), pltpu.VMEM((R, HD_A), F32), pltpu.VMEM((R, HD_A), F32),
                        pltpu.VMEM((SUB, HD_A), F32)],
        name="rglru_bwd", compiler_params=_cp(2),
    )(dyab, zp, zp, h, h, conv_w, conv_b, wgx, bgx, wga, bga, lam)


_POOL_H = 16
_POOL_T0 = 2 * D // HD_A
_POOL_Y0 = D // HD_A


def _pool_mean_minus(u, ext, g, t1):
    R = u.shape[0]
    acc, wins = u, []
    for k in range(1, _POOL_H):
        acc = acc + ext[pl.ds(_POOL_H - k, R), :]
        if k + 1 in POOL_WINDOWS:
            wins.append(acc)
    win = jnp.where(g == 0, wins[0], jnp.where(g == 1, wins[1], jnp.where(g == 2, wins[2], wins[3])))
    return win / jnp.minimum(t1, _pool_width(g)) - u


def _pool_width(g):
    return jnp.where(g == 0, 2.0, jnp.where(g == 1, 4.0, jnp.where(g == 2, 8.0, 16.0)))


def _b_fwd(zp, yab, wg, bg, sc):
    S = zp.shape[0]
    R, H = R_SEQ, _POOL_H

    def body(z_ref, wg_ref, bg_ref, sc_ref, yab_in, yb_ref, ext):
        del yab_in
        g, j = pl.program_id(0), pl.program_id(1)

        @pl.when(j == 0)
        def _():
            ext[0:H, :] = jnp.zeros((H, HD_A), F32)

        u = z_ref[...]
        ext[H:H + R, :] = u
        t1 = (j * R + 1 + lax.broadcasted_iota(jnp.int32, (R, HD_A), 0)).astype(F32)
        p = _pool_mean_minus(u, ext, g, t1)
        lin = lax.dot_general(p.astype(BF16), wg_ref[0].astype(BF16), _NN, preferred_element_type=F32) + bg_ref[...]
        yb_ref[...] = (lin * sc_ref[...]).astype(BF16)
        ext[0:H, :] = ext[R:R + H, :]

    vec = pl.BlockSpec((1, HD_A), lambda g, j: (0, g))
    return pl.pallas_call(
        body, out_shape=SDS(yab.shape, yab.dtype), grid=(len(POOL_WINDOWS), S // R),
        in_specs=[pl.BlockSpec((R, HD_A), lambda g, j: (j, _POOL_T0 + g)),
                  pl.BlockSpec((1, HD_A, HD_A), lambda g, j: (g, 0, 0)), vec, vec, pl.BlockSpec(memory_space=pl.ANY)],
        out_specs=pl.BlockSpec((R, HD_A), lambda g, j: (j, _POOL_Y0 + g)),
        scratch_shapes=[pltpu.VMEM((H + R, HD_A), F32)], input_output_aliases={4: 0},
        name="pool_fwd", compiler_params=_cp(2),
    )(zp, wg, bg, sc, yab)


def _b_bwd(dyab, zp, dzp, wg, bg, sc):
    S = zp.shape[0]
    R, H, nch, ng = R_SEQ, _POOL_H, S // R_SEQ, len(POOL_WINDOWS)

    def body(dy_ref, z_ref, zh_ref, wg_ref, bg_ref, sc_ref, dz_in, dz_ref, dwg_ref, dbg_ref, dsc_ref, ext_u, ext_q):
        del dz_in
        g, j = pl.program_id(0), pl.program_id(1)
        jj = nch - 1 - j

        @pl.when(j == 0)
        def _():
            ext_q[R:R + H, :] = jnp.zeros((H, HD_A), F32)
            for r in (dwg_ref, dbg_ref, dsc_ref):
                r[...] = jnp.zeros_like(r)

        u = z_ref[...]
        ext_u[0:H, :] = jnp.where(jj == 0, 0.0, zh_ref[...])
        ext_u[H:H + R, :] = u
        t1 = (jj * R + 1 + lax.broadcasted_iota(jnp.int32, (R, HD_A), 0)).astype(F32)
        pb = _pool_mean_minus(u, ext_u, g, t1).astype(BF16)
        wgb = wg_ref[0].astype(BF16)
        lin = lax.dot_general(pb, wgb, _NN, preferred_element_type=F32) + bg_ref[...]
        dy = dy_ref[...]
        dsc_ref[...] += jnp.sum(dy * lin, axis=0, keepdims=True)
        dlin = dy * sc_ref[...]
        dbg_ref[...] += jnp.sum(dlin, axis=0, keepdims=True)
        dlb = dlin.astype(BF16)
        dwg_ref[0] += lax.dot_general(pb, dlb, _TN, preferred_element_type=F32)
        dp = lax.dot_general(dlb, wgb, _NT, preferred_element_type=F32)
        q = dp / jnp.minimum(t1, _pool_width(g))
        ext_q[0:R, :] = q
        acc, wins = q, []
        for k in range(1, H):
            acc = acc + ext_q[pl.ds(k, R), :]
            if k + 1 in POOL_WINDOWS:
                wins.append(acc)
        win = jnp.where(g == 0, wins[0], jnp.where(g == 1, wins[1], jnp.where(g == 2, wins[2], wins[3])))
        dz_ref[...] = (win - dp).astype(BF16)
        ext_q[R:R + H, :] = ext_q[0:H, :]

    vec = pl.BlockSpec((1, HD_A), lambda g, j: (0, g))
    mat = pl.BlockSpec((1, HD_A, HD_A), lambda g, j: (g, 0, 0))
    return pl.pallas_call(
        body, out_shape=[SDS(dzp.shape, dzp.dtype), SDS((ng, HD_A, HD_A), F32), SDS((1, D // 2), F32),
                         SDS((1, D // 2), F32)],
        grid=(ng, nch),
        in_specs=[pl.BlockSpec((R, HD_A), lambda g, j: (nch - 1 - j, _POOL_Y0 + g)),
                  pl.BlockSpec((R, HD_A), lambda g, j: (nch - 1 - j, _POOL_T0 + g)),
                  pl.BlockSpec((H, HD_A), lambda g, j: (jnp.maximum((nch - 1 - j) * (R // H) - 1, 0), _POOL_T0 + g)),
                  mat, vec, vec, pl.BlockSpec(memory_space=pl.ANY)],
        out_specs=[pl.BlockSpec((R, HD_A), lambda g, j: (nch - 1 - j, _POOL_T0 + g)), mat, vec, vec],
        scratch_shapes=[pltpu.VMEM((H + R, HD_A), F32), pltpu.VMEM((R + H, HD_A), F32)],
        input_output_aliases={6: 0}, name="pool_bwd", compiler_params=_cp(2),
    )(dyab, zp, zp, wg, bg, sc, dzp)


_CW_F = 768


def _f_fwd(hp, w, b, name):
    S = hp.shape[0]
    R, H, cw = R_SEQ, SUB, _CW_F

    def body(h_ref, w_ref, b_ref, o_ref, ext):
        j = pl.program_id(1)

        @pl.when(j == 0)
        def _():
            ext[0:H, :] = jnp.zeros((H, cw), F32)

        ext[H:H + R, :] = h_ref[:, 0:cw]
        gp = b_ref[...]
        for k in range(CONV_F):
            gp = gp + w_ref[k:k + 1, :] * ext[pl.ds(H - (CONV_F - 1 - k), R), :]
        o_ref[...] = (_gelu(gp) * h_ref[:, cw:2 * cw]).astype(BF16)
        ext[0:H, :] = ext[R:R + H, :]

    return pl.pallas_call(
        body, out_shape=SDS((S, D_FF), BF16), grid=(D_FF // cw, S // R),
        in_specs=[pl.BlockSpec((R, 2 * cw), lambda c, j: (j, c)), pl.BlockSpec((CONV_F, cw), lambda c, j: (0, c)),
                  pl.BlockSpec((1, cw), lambda c, j: (0, c))],
        out_specs=pl.BlockSpec((R, cw), lambda c, j: (j, c)), scratch_shapes=[pltpu.VMEM((H + R, cw), F32)],
        name=name, compiler_params=_cp(2),
    )(hp, w, b)


def _f_bwd(dact, hp, w, b, name):
    S = hp.shape[0]
    R, H, cw, nch = R_SEQ, SUB, _CW_F, S // R_SEQ

    def body(da_ref, h_ref, hh_ref, w_ref, b_ref, dh_ref, dw_ref, db_ref, ext_g, ext_d):
        j = pl.program_id(1)
        jj = nch - 1 - j

        @pl.when(j == 0)
        def _():
            ext_d[R:R + H, :] = jnp.zeros((H, cw), F32)
            dw_ref[...] = jnp.zeros_like(dw_ref)
            db_ref[...] = jnp.zeros_like(db_ref)

        ext_g[0:H, :] = jnp.where(jj == 0, 0.0, hh_ref[:, 0:cw])
        ext_g[H:H + R, :] = h_ref[:, 0:cw]
        gp = b_ref[...]
        for k in range(CONV_F):
            gp = gp + w_ref[k:k + 1, :] * ext_g[pl.ds(H - (CONV_F - 1 - k), R), :]
        gel, dgel = _gelu(gp, with_grad=True)
        da = da_ref[...]
        dh_ref[:, cw:2 * cw] = (da * gel).astype(BF16)
        dgp = da * h_ref[:, cw:2 * cw] * dgel
        db_ref[...] += jnp.sum(dgp, axis=0, keepdims=True)
        ext_d[0:R, :] = dgp
        dhg = jnp.zeros((R, cw), F32)
        for k in range(CONV_F):
            sh = CONV_F - 1 - k
            dw_ref[k:k + 1, :] += jnp.sum(dgp * ext_g[pl.ds(H - sh, R), :], axis=0, keepdims=True)
            dhg = dhg + w_ref[k:k + 1, :] * ext_d[pl.ds(sh, R), :]
        dh_ref[:, 0:cw] = dhg.astype(BF16)
        ext_d[R:R + H, :] = ext_d[0:H, :]

    rows = lambda c, j: (nch - 1 - j, c)
    return pl.pallas_call(
        body, out_shape=[SDS((S, 2 * D_FF), BF16), SDS((CONV_F, D_FF), F32), SDS((1, D_FF), F32)],
        grid=(D_FF // cw, nch),
        in_specs=[pl.BlockSpec((R, cw), rows), pl.BlockSpec((R, 2 * cw), rows),
                  pl.BlockSpec((H, 2 * cw), lambda c, j: (jnp.maximum((nch - 1 - j) * (R // H) - 1, 0), c)),
                  pl.BlockSpec((CONV_F, cw), lambda c, j: (0, c)), pl.BlockSpec((1, cw), lambda c, j: (0, c))],
        out_specs=[pl.BlockSpec((R, 2 * cw), rows), pl.BlockSpec((CONV_F, cw), lambda c, j: (0, c)),
                   pl.BlockSpec((1, cw), lambda c, j: (0, c))],
        scratch_shapes=[pltpu.VMEM((H + R, cw), F32), pltpu.VMEM((R + H, cw), F32)], name=name,
        compiler_params=_cp(2),
    )(dact, hp, hp, w, b)


_CW_C = 256
_H_C = 32


def _c_fwd(h1p, w, b):
    S = h1p.shape[0]
    R, H, cw = R_SEQ, _H_C, _CW_C

    def body(h_ref, w_ref, b_ref, o_ref, ext):
        j = pl.program_id(1)

        @pl.when(j == 0)
        def _():
            ext[0:H, :] = jnp.zeros((H, cw), F32)

        ext[H:H + R, :] = h_ref[:, 0:cw] * _sigmoid(h_ref[:, cw:2 * cw])
        cv = b_ref[...]
        for k in range(CONV_C):
            cv = cv + w_ref[k:k + 1, :] * ext[pl.ds(H - (CONV_C - 1 - k), R), :]
        o_ref[...] = cv
        ext[0:H, :] = ext[R:R + H, :]

    return pl.pallas_call(
        body, out_shape=SDS((S, D), F32), grid=(D // cw, S // R),
        in_specs=[pl.BlockSpec((R, 2 * cw), lambda c, j: (j, c)), pl.BlockSpec((CONV_C, cw), lambda c, j: (0, c)),
                  pl.BlockSpec((1, cw), lambda c, j: (0, c))],
        out_specs=pl.BlockSpec((R, cw), lambda c, j: (j, c)), scratch_shapes=[pltpu.VMEM((H + R, cw), F32)],
        name="conf_conv_fwd", compiler_params=_cp(2),
    )(h1p, w, b)


def _c_bwd(dcv, h1p, w):
    S = h1p.shape[0]
    R, H, cw, nch = R_SEQ, _H_C, _CW_C, S // R_SEQ

    def body(dc_ref, h_ref, hh_ref, w_ref, dh_ref, dw_ref, db_ref, db1_ref, ext_u, ext_d):
        j = pl.program_id(1)
        jj = nch - 1 - j

        @pl.when(j == 0)
        def _():
            ext_d[R:R + H, :] = jnp.zeros((H, cw), F32)
            for r in (dw_ref, db_ref, db1_ref):
                r[...] = jnp.zeros_like(r)

        ext_u[0:H, :] = jnp.where(jj == 0, 0.0, hh_ref[:, 0:cw] * _sigmoid(hh_ref[:, cw:2 * cw]))
        val = h_ref[:, 0:cw]
        sg = _sigmoid(h_ref[:, cw:2 * cw])
        ext_u[H:H + R, :] = val * sg
        dc = dc_ref[...]
        db_ref[...] += jnp.sum(dc, axis=0, keepdims=True)
        ext_d[0:R, :] = dc
        du = jnp.zeros((R, cw), F32)
        for k in range(CONV_C):
            sh = CONV_C - 1 - k
            dw_ref[k:k + 1, :] += jnp.sum(dc * ext_u[pl.ds(H - sh, R), :], axis=0, keepdims=True)
            du = du + w_ref[k:k + 1, :] * ext_d[pl.ds(sh, R), :]
        dval = du * sg
        dgate = du * val * sg * (1.0 - sg)
        db1_ref[:, 0:cw] += jnp.sum(dval, axis=0, keepdims=True)
        db1_ref[:, cw:2 * cw] += jnp.sum(dgate, axis=0, keepdims=True)
        dh_ref[:, 0:cw] = dval.astype(BF16)
        dh_ref[:, cw:2 * cw] = dgate.astype(BF16)
        ext_d[R:R + H, :] = ext_d[0:H, :]

    rows = lambda c, j: (nch - 1 - j, c)
    return pl.pallas_call(
        body, out_shape=[SDS((S, 2 * D), BF16), SDS((CONV_C, D), F32), SDS((1, D), F32), SDS((1, 2 * D), F32)],
        grid=(D // cw, nch),
        in_specs=[pl.BlockSpec((R, cw), rows), pl.BlockSpec((R, 2 * cw), rows),
                  pl.BlockSpec((H, 2 * cw), lambda c, j: (jnp.maximum((nch - 1 - j) * (R // H) - 1, 0), c)),
                  pl.BlockSpec((CONV_C, cw), lambda c, j: (0, c))],
        out_specs=[pl.BlockSpec((R, 2 * cw), rows), pl.BlockSpec((CONV_C, cw), lambda c, j: (0, c)),
                   pl.BlockSpec((1, cw), lambda c, j: (0, c)), pl.BlockSpec((1, 2 * cw), lambda c, j: (0, c))],
        scratch_shapes=[pltpu.VMEM((H + R, cw), F32), pltpu.VMEM((R + H, cw), F32)], name="conf_conv_bwd",
        compiler_params=_cp(2),
    )(dcv, h1p, h1p, w)


def _local_step(x, mem, tgt, W):
    G = {}

    def xattn_fwd(xin, l):
        n = _rms_fwd(xin, W["xa_norm"][l:l + 1], f"xa_norm_fwd{l}")
        q = _mm_nn(n, W["xa_wq"][l], out_dtype=BF16, name=f"xa_q{l}")
        mn = _rms_fwd(mem, W["xa_mem_norm"][l:l + 1], f"xa_memnorm_fwd{l}")
        k = _mm_nn(mn, W["xa_wk"][l], out_dtype=BF16, name=f"xa_k{l}")
        v = _mm_nn(mn, W["xa_wv"][l], out_dtype=BF16, name=f"xa_v{l}")
        o = _attn_fwd(q, k, v, f"xa_attn_fwd{l}")
        xout = _mm_nn(o, W["xa_wo"][l], out_dtype=F32, name=f"xa_o{l}", add=xin)
        return xout, (xin, n, q, mn, k, v, o)

    def xattn_bwd(dx, dxb, saved, l):
        xin, n, q, mn, k, v, o = saved
        do = _mm_nt(dxb, W["xa_wo"][l], out_dtype=BF16, name=f"xa_do{l}")
        G[f"xa_wo{l}"] = _mm_tn(o, dxb, out_dtype=BF16, name=f"xa_dwo{l}")
        dq, dk, dv = _attn_bwd(q, k, v, do, f"xa_attn_bwd{l}")
        dkb, dvb = dk.astype(BF16), dv.astype(BF16)
        G[f"xa_wq{l}"] = _mm_tn(n, dq, out_dtype=BF16, name=f"xa_dwq{l}")
        G[f"xa_wk{l}"] = _mm_tn(mn, dkb, out_dtype=BF16, name=f"xa_dwk{l}")
        G[f"xa_wv{l}"] = _mm_tn(mn, dvb, out_dtype=BF16, name=f"xa_dwv{l}")
        dmn = _mm_nt(dkb, W["xa_wk"][l], out_dtype=F32, name=f"xa_dmn_k{l}")
        dmn = _mm_nt(dvb, W["xa_wv"][l], out_dtype=F32, name=f"xa_dmn_v{l}", add=dmn)
        (G[f"xa_mem_norm{l}"],) = _rms_bwd(mem, W["xa_mem_norm"][l:l + 1], dmn, None, f"xa_memnorm_bwd{l}")
        dn = _mm_nt(dq, W["xa_wq"][l], out_dtype=F32, name=f"xa_dn{l}")
        dx, dxb, G[f"xa_norm{l}"] = _rms_bwd(xin, W["xa_norm"][l:l + 1], dn, dx, f"xa_norm_bwd{l}")
        return dx, dxb

    def ffn_fwd(xin, l):
        n = _rms_fwd(xin, W["f_norm"][l:l + 1], f"f_norm_fwd{l}")
        hp = _mm_nn(n, W["f_w_up"][l], out_dtype=F32, name=f"f_up{l}")
        act = _f_fwd(hp, W["f_dw_w"][l], W["f_dw_b"][l:l + 1], f"f_conv_fwd{l}")
        xout = _mm_nn(act, W["f_w_down"][l], out_dtype=F32, name=f"f_down{l}", add=xin)
        return xout, (xin, n, hp, act)

    def ffn_bwd(dx, dxb, saved, l):
        xin, n, hp, act = saved
        dact = _mm_nt(dxb, W["f_w_down"][l], out_dtype=F32, name=f"f_dact{l}")
        G[f"f_w_down{l}"] = _mm_tn(act, dxb, out_dtype=BF16, name=f"f_dwdown{l}")
        dhp, G[f"f_dw_w{l}"], G[f"f_dw_b{l}"] = _f_bwd(dact, hp, W["f_dw_w"][l], W["f_dw_b"][l:l + 1], f"f_conv_bwd{l}")
        G[f"f_w_up{l}"] = _mm_tn(n, dhp, out_dtype=BF16, name=f"f_dwup{l}", blocks=_CW_F)
        dn = _mm_nt(dhp, W["f_w_up"][l], out_dtype=F32, name=f"f_dn{l}")
        dx, dxb, G[f"f_norm{l}"] = _rms_bwd(xin, W["f_norm"][l:l + 1], dn, dx, f"f_norm_bwd{l}")
        return dx, dxb

    a_par = (W["a_conv_w"], W["a_conv_b"], W["a_gate_x_w"], W["a_gate_x_b"], W["a_gate_a_w"], W["a_gate_a_b"],
             W["a_lambda"])
    b_par = (W["b_group_w"], W["b_group_b"], W["b_scale"])
    n0 = _rms_fwd(x, W["ab_norm"], "ab_norm_fwd")
    zp = _mm_nn(n0, W["ab_w_in"], out_dtype=F32, name="ab_in")
    yab, h_a = _a_fwd(zp, *a_par)
    yab = _b_fwd(zp, yab, *b_par)
    x1 = _mm_nn(yab, W["ab_w_out"], out_dtype=F32, name="ab_out", add=x)
    x2, s_xa0 = xattn_fwd(x1, 0)
    x3, s_f0 = ffn_fwd(x2, 0)
    n3 = _rms_fwd(x3, W["c_norm"], "c_norm_fwd")
    h1p = _mm_nn(n3, W["c_w_pw1"], out_dtype=F32, name="c_pw1", bias=W["c_b_pw1"])
    cv = _c_fwd(h1p, W["c_dw_w"], W["c_dw_b"])
    sc = _ln_silu_fwd(cv, W["c_ln_g"], W["c_ln_b"])
    x4 = _mm_nn(sc, W["c_w_pw2"], out_dtype=F32, name="c_pw2", bias=W["c_b_pw2"], add=x3)
    x5, s_xa1 = xattn_fwd(x4, 1)
    x6, s_f1 = ffn_fwd(x5, 1)
    loss, dx, dxb, G["final_norm"] = _loss_head(x6, W["final_norm"], tgt)

    dx, dxb = ffn_bwd(dx, dxb, s_f1, 1)
    dx, dxb = xattn_bwd(dx, dxb, s_xa1, 1)
    dsc = _mm_nt(dxb, W["c_w_pw2"], out_dtype=F32, name="c_dsc")
    G["c_w_pw2"] = _mm_tn(sc, dxb, out_dtype=BF16, name="c_dwpw2")
    dcv, G["c_ln_g"], G["c_ln_b"], G["c_b_pw2"] = _ln_silu_bwd(dsc, cv, W["c_ln_g"], W["c_ln_b"], dx)
    dh1p, G["c_dw_w"], G["c_dw_b"], G["c_b_pw1"] = _c_bwd(dcv, h1p, W["c_dw_w"])
    G["c_w_pw1"] = _mm_tn(n3, dh1p, out_dtype=BF16, name="c_dwpw1", blocks=_CW_C)
    dn3 = _mm_nt(dh1p, W["c_w_pw1"], out_dtype=F32, name="c_dn")
    dx, dxb, G["c_norm"] = _rms_bwd(x3, W["c_norm"], dn3, dx, "c_norm_bwd")
    dx, dxb = ffn_bwd(dx, dxb, s_f0, 0)
    dx, dxb = xattn_bwd(dx, dxb, s_xa0, 0)
    dyab = _mm_nt(dxb, W["ab_w_out"], out_dtype=F32, name="ab_dyab")
    G["ab_w_out"] = _mm_tn(yab, dxb, out_dtype=BF16, name="ab_dwout")
    (dzp, G["a_conv_w"], G["a_conv_b"], G["a_gate_x_w"], G["a_gate_x_b"], G["a_gate_a_w"], G["a_gate_a_b"],
     G["a_lambda"]) = _a_bwd(dyab, zp, h_a, *a_par)
    dzp, G["b_group_w"], G["b_group_b"], G["b_scale"] = _b_bwd(dyab, zp, dzp, *b_par)
    G["ab_w_in"] = _mm_tn(n0, dzp, out_dtype=BF16, name="ab_dwin")
    dn0 = _mm_nt(dzp, W["ab_w_in"], out_dtype=F32, name="ab_dn")
    dx, _, G["ab_norm"] = _rms_bwd(x, W["ab_norm"], dn0, dx, "ab_norm_bwd")
    return loss, dx, G


def _my_place():
    x, y, c = lax.axis_index("x"), lax.axis_index("y"), lax.axis_index("c")
    return x, y, c


def _all_gather(shards, name):
    n = len(shards)

    def body(*refs):
        ins, outs = refs[:n], refs[n:2 * n]
        send_sems, recv_sems, local_sems = refs[2 * n:]
        x, y, c = _my_place()
        me, sibling = (x, y, c), (x, y, 1 - c)
        chips = [(1 - x, y), (x, 1 - y), (1 - x, 1 - y)]

        def slab(a, place):
            px, py, pc = place
            return outs[a].at[4 * px + 2 * py + pc]

        def copy(a, k, block, to, src=None):
            return pltpu.make_async_remote_copy(
                src_ref=slab(a, block) if src is None else src, dst_ref=slab(a, block),
                send_sem=send_sems.at[a, k], recv_sem=recv_sems.at[a, k], device_id=to, device_id_type=MESH)

        mine = [pltpu.make_async_copy(ins[a], slab(a, me), local_sems.at[a]) for a in range(n)]
        for cp in mine:
            cp.start()
        first = []
        for j, chip in enumerate(chips):
            first += [copy(a, 1 + j, me, (*chip, c), src=ins[a]) for a in range(n)]
        first += [copy(a, 0, me, sibling, src=ins[a]) for a in range(n)]
        for cp in first:
            cp.start()
        passed = []
        for j, chip in enumerate(chips):
            for a in range(n):
                copy(a, 1 + j, (*chip, c), me).wait_recv()
                cp = copy(a, 4 + j, (*chip, c), sibling)
                cp.start()
                passed.append(cp)
        for a in range(n):
            copy(a, 0, sibling, me).wait_recv()
        for j, chip in enumerate(chips):
            for a in range(n):
                copy(a, 4 + j, (*chip, 1 - c), me).wait_recv()
        for cp in first + passed:
            cp.wait_send()
        for cp in mine:
            cp.wait()

    any_spec = pl.BlockSpec(memory_space=pl.ANY)
    return pl.pallas_call(
        body, out_shape=[SDS((N_DEV,) + s.shape, s.dtype) for s in shards], in_specs=[any_spec] * n,
        out_specs=[any_spec] * n,
        scratch_shapes=[pltpu.SemaphoreType.DMA((n, 7)), pltpu.SemaphoreType.DMA((n, 7)), pltpu.SemaphoreType.DMA((n,))],
        name=name,
    )(*shards)


def _exchange(groups, name):
    flat = [(gi, li, a) for gi, grp in enumerate(groups) for li, a in enumerate(grp)]
    n, ng = len(flat), len(groups)

    def body(*refs):
        ins, outs = refs[:n], refs[n:n + ng]
        send_sems, recv_sems, local_sems = refs[n + ng:]
        x, y, c = _my_place()
        me = 4 * x + 2 * y + c
        peers = []
        for k in range(1, N_DEV):
            px = 1 - x if (k >> 2) & 1 else x
            py = 1 - y if (k >> 1) & 1 else y
            pc = 1 - c if k & 1 else c
            peers.append(((px, py, pc), 4 * px + 2 * py + pc))

        mine = [pltpu.make_async_copy(ins[i].at[me], outs[gi].at[me, li], local_sems.at[i])
                for i, (gi, li, _) in enumerate(flat)]
        for cp in mine:
            cp.start()
        sent = []
        for k, (peer, pidx) in enumerate(peers):
            for i, (gi, li, _) in enumerate(flat):
                cp = pltpu.make_async_remote_copy(
                    src_ref=ins[i].at[pidx], dst_ref=outs[gi].at[me, li], send_sem=send_sems.at[i, k],
                    recv_sem=recv_sems.at[i, k], device_id=peer, device_id_type=MESH)
                cp.start()
                sent.append(cp)
        for k, (peer, pidx) in enumerate(peers):
            for i, (gi, li, _) in enumerate(flat):
                pltpu.make_async_remote_copy(
                    src_ref=ins[i].at[pidx], dst_ref=outs[gi].at[pidx, li], send_sem=send_sems.at[i, k],
                    recv_sem=recv_sems.at[i, k], device_id=peer, device_id_type=MESH).wait_recv()
        for cp in sent:
            cp.wait_send()
        for cp in mine:
            cp.wait()

    any_spec = pl.BlockSpec(memory_space=pl.ANY)
    out_shape = [SDS((N_DEV, len(grp)) + grp[0].shape[1:], grp[0].dtype) for grp in groups]
    return pl.pallas_call(
        body, out_shape=out_shape, in_specs=[any_spec] * n, out_specs=[any_spec] * ng,
        scratch_shapes=[pltpu.SemaphoreType.DMA((n, 7)), pltpu.SemaphoreType.DMA((n, 7)), pltpu.SemaphoreType.DMA((n,))],
        name=name,
    )(*[a for _, _, a in flat])


def _adamw_math(w, g, m, v):
    m = ADAM_B1 * m + (1.0 - ADAM_B1) * g
    v = ADAM_B2 * v + (1.0 - ADAM_B2) * (g * g)
    m_hat = m / (1.0 - ADAM_B1 ** ADAM_STEP)
    v_hat = v / (1.0 - ADAM_B2 ** ADAM_STEP)
    delta = -ADAM_LR * (m_hat / (jnp.sqrt(v_hat) + ADAM_EPS) + ADAM_WD * w)
    return delta, m, v


def _row_tile(r, c, itemsize_rows):
    cap = max(SUB, (itemsize_rows // (4 * c)) // SUB * SUB)
    if r <= cap:
        return r
    best = None
    for t in range(SUB, cap + 1, SUB):
        if r % t == 0:
            best = t
    return best if best is not None else r


def _sum_adamw(landing, w, m, v, name):
    _, r, c = landing.shape
    tr = _row_tile(r, c, 1 << 20)

    def body(l_ref, w_ref, m_ref, v_ref, g_ref, d_ref, mo_ref, vo_ref):
        g = l_ref[0].astype(F32)
        for s in range(1, N_DEV):
            g = g + l_ref[s].astype(F32)
        g_ref[...] = g
        d_ref[...], mo_ref[...], vo_ref[...] = _adamw_math(w_ref[...], g, m_ref[...], v_ref[...])

    blk = pl.BlockSpec((tr, c), lambda i: (i, 0))
    return pl.pallas_call(
        body, out_shape=[SDS((r, c), F32)] * 4, grid=(r // tr,),
        in_specs=[pl.BlockSpec((N_DEV, tr, c), lambda i: (0, i, 0)), blk, blk, blk], out_specs=[blk] * 4, name=name,
        compiler_params=_cp(1),
    )(landing, w, m, v)


def _sum8(landing, name):
    _, r, c = landing.shape

    def body(l_ref, g_ref):
        g = l_ref[0]
        for s in range(1, N_DEV):
            g = g + l_ref[s]
        g_ref[...] = g

    return pl.pallas_call(body, out_shape=SDS((r, c), F32), name=name, compiler_params=_cp(0))(landing)


def _adamw(g, w, m, v, name):
    r, c = g.shape
    tr = _row_tile(r, c, 1 << 20)

    def body(g_ref, w_ref, m_ref, v_ref, d_ref, mo_ref, vo_ref):
        d_ref[...], mo_ref[...], vo_ref[...] = _adamw_math(w_ref[...], g_ref[...], m_ref[...], v_ref[...])

    blk = pl.BlockSpec((tr, c), lambda i: (i, 0))
    return pl.pallas_call(body, out_shape=[SDS((r, c), F32)] * 3, grid=(r // tr,), in_specs=[blk] * 4,
                          out_specs=[blk] * 3, name=name, compiler_params=_cp(1))(g, w, m, v)


_BIG = {
    "ab_w_in": (1, D, 320), "ab_w_out": (1, 192, D), "c_w_pw1": (1, D, 256), "c_w_pw2": (1, 128, D),
    "xa_wq": (2, 128, D), "xa_wk": (2, 128, D), "xa_wv": (2, 128, D), "xa_wo": (2, 128, D),
    "f_w_up": (2, D, 768), "f_w_down": (2, 384, D),
}
_SMALL_SHARDED = {
    "a_conv_w": (1, 4, 128), "c_norm": (1, 128), "c_b_pw1": (1, 256), "c_dw_w": (1, 31, 128), "c_dw_b": (1, 128),
    "c_ln_g": (1, 128), "c_ln_b": (1, 128), "c_b_pw2": (1, 128), "f_dw_w": (2, 3, 384),
}
_REPL = {
    "ab_norm": (1, D), "a_conv_b": (1, D), "a_gate_x_w": (1, 8, 128, 128), "a_gate_x_b": (1, D),
    "a_gate_a_w": (1, 8, 128, 128), "a_gate_a_b": (1, D), "a_lambda": (1, D), "b_group_w": (1, 4, 128, 128),
    "b_group_b": (1, 512), "b_scale": (1, 512), "xa_norm": (2, D), "xa_mem_norm": (2, D), "f_norm": (2, D),
    "f_dw_b": (2, D_FF), "final_norm": (D,),
}


def _size(shape):
    n = 1
    for s in shape:
        n *= s
    return n


_N_SS = sum(_size(s) for s in _SMALL_SHARDED.values())
_N_REPL = sum(_size(s) for s in _REPL.values())
_REPL_ROWS = -(-_N_REPL // (N_DEV * SUB * LANE)) * SUB
_SS_ROWS = _N_SS // LANE
_SMALL_ROWS = -(-(_REPL_ROWS + _SS_ROWS) // SUB) * SUB


def _pack(parts, rows):
    flat = jnp.concatenate([p.reshape(-1).astype(F32) for p in parts])
    return jnp.pad(flat, (0, rows * LANE - flat.shape[0])).reshape(rows, LANE)


def _unpack(buf, table):
    flat, out, off = buf.reshape(-1), {}, 0
    for name, shape in table.items():
        n = _size(shape)
        out[name] = flat[off:off + n].reshape(shape)
        off += n
    return out


def _w_in_to_tiles(w):
    K = w.shape[0]
    gr = jnp.stack([w[:, :D].reshape(K, 8, HD_A), w[:, D:2 * D].reshape(K, 8, HD_A)], axis=2)
    return jnp.concatenate([gr.reshape(K, 2 * D), w[:, 2 * D:]], axis=1)


def _w_in_from_tiles(w):
    K = w.shape[0]
    gr = w[:, :2 * D].reshape(K, 8, 2, HD_A)
    return jnp.concatenate([gr[:, :, 0].reshape(K, D), gr[:, :, 1].reshape(K, D), w[:, 2 * D:]], axis=1)


def _pair_blocks(v, bw):
    lead, n = v.shape[:-1], v.shape[-1]
    return jnp.swapaxes(v.reshape(lead + (2, n // (2 * bw), bw)), -3, -2).reshape(lead + (n,))


def _unpair_blocks(v, bw):
    lead, n = v.shape[:-1], v.shape[-1]
    return jnp.swapaxes(v.reshape(lead + (n // (2 * bw), 2, bw)), -3, -2).reshape(lead + (n,))


def _gather_weights(P):
    big_names, big_shards = [], []
    for name, (layers, r, c) in _BIG.items():
        for l in range(layers):
            big_names.append((name, l))
            big_shards.append(P[name][l].astype(BF16))
    small = _pack([P[n] for n in _SMALL_SHARDED], _SS_ROWS + 4)
    gathered = _all_gather(big_shards + [small], "gather_weights")
    big = {}
    for (name, l), g in zip(big_names, gathered[:-1]):
        big.setdefault(name, []).append(g)
    W = {n: P[n] for n in _REPL}
    W["final_norm"] = P["final_norm"].reshape(1, D)
    W["a_gate_x_w"], W["a_gate_a_w"], W["b_group_w"] = P["a_gate_x_w"][0], P["a_gate_a_w"][0], P["b_group_w"][0]
    w_in = jnp.swapaxes(big["ab_w_in"][0], 0, 1).reshape(D, N_DEV * 320)
    W["ab_w_in"] = _w_in_to_tiles(w_in)
    W["c_w_pw1"] = big["c_w_pw1"][0]
    W["f_w_up"] = big["f_w_up"]
    W["ab_w_out"] = big["ab_w_out"][0].reshape(N_DEV * 192, D)
    W["c_w_pw2"] = big["c_w_pw2"][0].reshape(D, D)
    for n in ("xa_wq", "xa_wk", "xa_wv", "xa_wo"):
        W[n] = [g.reshape(D, D) for g in big[n]]
    W["f_w_down"] = [g.reshape(D_FF, D) for g in big["f_w_down"]]
    sm = gathered[-1].reshape(N_DEV, -1)
    off = 0
    for name, shape in _SMALL_SHARDED.items():
        n = _size(shape)
        blocks = sm[:, off:off + n].reshape((N_DEV,) + shape)
        off += n
        full = jnp.moveaxis(blocks, 0, -2)
        W[name] = full.reshape(shape[:-1] + (N_DEV * shape[-1],))
    W["a_conv_w"], W["c_dw_w"] = W["a_conv_w"][0], W["c_dw_w"][0]
    W["c_b_pw1"] = _pair_blocks(W["c_b_pw1"], _CW_C)
    return W


def _to_dest_major(g, shape):
    full = g.reshape(shape[:-1] + (N_DEV, shape[-1]))
    return jnp.moveaxis(full, -2, 0).reshape(N_DEV, -1)


def kernel(x, mem, ab_norm, ab_w_in, a_conv_w, a_conv_b, a_gate_x_w, a_gate_x_b, a_gate_a_w, a_gate_a_b, a_lambda, b_group_w, b_group_b, b_scale, ab_w_out, c_norm, c_w_pw1, c_b_pw1, c_dw_w, c_dw_b, c_ln_g, c_ln_b, c_w_pw2, c_b_pw2, xa_norm, xa_mem_norm, xa_wq, xa_wk, xa_wv, xa_wo, f_norm, f_w_up, f_dw_w, f_dw_b, f_w_down, final_norm, loss_target, m_ab_norm, m_ab_w_in, m_a_conv_w, m_a_conv_b, m_a_gate_x_w, m_a_gate_x_b, m_a_gate_a_w, m_a_gate_a_b, m_a_lambda, m_b_group_w, m_b_group_b, m_b_scale, m_ab_w_out, m_c_norm, m_c_w_pw1, m_c_b_pw1, m_c_dw_w, m_c_dw_b, m_c_ln_g, m_c_ln_b, m_c_w_pw2, m_c_b_pw2, m_xa_norm, m_xa_mem_norm, m_xa_wq, m_xa_wk, m_xa_wv, m_xa_wo, m_f_norm, m_f_w_up, m_f_dw_w, m_f_dw_b, m_f_w_down, m_final_norm, v_ab_norm, v_ab_w_in, v_a_conv_w, v_a_conv_b, v_a_gate_x_w, v_a_gate_x_b, v_a_gate_a_w, v_a_gate_a_b, v_a_lambda, v_b_group_w, v_b_group_b, v_b_scale, v_ab_w_out, v_c_norm, v_c_w_pw1, v_c_b_pw1, v_c_dw_w, v_c_dw_b, v_c_ln_g, v_c_ln_b, v_c_w_pw2, v_c_b_pw2, v_xa_norm, v_xa_mem_norm, v_xa_wq, v_xa_wk, v_xa_wv, v_xa_wo, v_f_norm, v_f_w_up, v_f_dw_w, v_f_dw_b, v_f_w_down, v_final_norm):
    args = dict(locals())
    P = {n: args[n] for n in _NAMES}
    M = {n: args["m_" + n] for n in _NAMES}
    V = {n: args["v_" + n] for n in _NAMES}

    W = _gather_weights(P)
    loss, grad_x, G = _local_step(x[0], mem[0], loss_target[0], W)
    loss = lax.psum(loss[0, 0], ("x", "y", "c"))

    groups, group_names = [], []
    for name, (layers, r, c) in _BIG.items():
        if name == "ab_w_in":
            nat = _w_in_from_tiles(G["ab_w_in"])
            grp = [jnp.swapaxes(nat.reshape(D, N_DEV, 320), 0, 1)]
        elif name in ("c_w_pw1",):
            grp = [G[name]]
        elif name == "f_w_up":
            grp = [G[f"f_w_up{l}"] for l in range(layers)]
        elif layers == 1:
            grp = [G[name].reshape(N_DEV, r, c)]
        else:
            grp = [G[f"{name}{l}"].reshape(N_DEV, r, c) for l in range(layers)]
        groups.append(grp)
        group_names.append(name)
    Gs = dict(G)
    Gs["c_b_pw1"] = _unpair_blocks(G["c_b_pw1"], _CW_C)
    Gs["f_dw_w"] = jnp.stack([G["f_dw_w0"], G["f_dw_w1"]])
    Gs["a_conv_w"], Gs["c_dw_w"] = G["a_conv_w"][None], G["c_dw_w"][None]
    for n in ("xa_norm", "xa_mem_norm", "f_norm", "f_dw_b"):
        Gs[n] = jnp.concatenate([G[f"{n}0"], G[f"{n}1"]], axis=0)
    for n in ("a_gate_x_w", "a_gate_a_w", "b_group_w"):
        Gs[n] = G[n][None]
    repl_flat = jnp.concatenate([Gs[n].reshape(-1) for n in _REPL])
    repl_rows = jnp.pad(repl_flat, (0, N_DEV * _REPL_ROWS * LANE - _N_REPL)).reshape(N_DEV, _REPL_ROWS, LANE)
    ss_rows = jnp.concatenate([_to_dest_major(Gs[n], s) for n, s in _SMALL_SHARDED.items()], axis=1)
    ss_rows = ss_rows.reshape(N_DEV, _SS_ROWS, LANE)
    small_pack = jnp.concatenate(
        [repl_rows, ss_rows, jnp.zeros((N_DEV, _SMALL_ROWS - _REPL_ROWS - _SS_ROWS, LANE), F32)], axis=1)
    landed = _exchange(groups + [[small_pack]], "exchange_grads")

    out_g, out_d, out_m, out_v = {}, {}, {}, {}
    for name, land in zip(group_names, landed[:-1]):
        layers, r, c = _BIG[name]
        shape = P[name].shape
        res = _sum_adamw(land.reshape(N_DEV, layers * r, c), P[name].reshape(layers * r, c),
                         M[name].reshape(layers * r, c), V[name].reshape(layers * r, c), f"adamw_{name}")
        out_g[name], out_d[name], out_m[name], out_v[name] = [t.reshape(shape) for t in res]

    small_sum = _sum8(landed[-1].reshape(N_DEV, _SMALL_ROWS, LANE), "sum_small")
    (repl_all,) = _all_gather([small_sum[:_REPL_ROWS]], "gather_small_grads")
    g_repl = _unpack(repl_all, _REPL)
    g_ss = _unpack(small_sum[_REPL_ROWS:_REPL_ROWS + _SS_ROWS], _SMALL_SHARDED)
    table = dict(_REPL)
    table.update(_SMALL_SHARDED)
    rows = -(-(_N_REPL + _N_SS) // (SUB * LANE)) * SUB
    g_small = dict(g_repl)
    g_small.update(g_ss)
    packs = [_pack([src[n] for n in table], rows) for src in (g_small, P, M, V)]
    res = _adamw(*packs, "adamw_small")
    for out, buf in zip((out_d, out_m, out_v), res):
        out.update(_unpack(buf, table))
    out_g.update(g_small)

    return (loss, grad_x[None], *[out_g[n] for n in _NAMES], *[out_d[n] for n in _NAMES],
            *[out_m[n] for n in _NAMES], *[out_v[n] for n in _NAMES])


_NAMES = ("ab_norm", "ab_w_in", "a_conv_w", "a_conv_b", "a_gate_x_w", "a_gate_x_b", "a_gate_a_w", "a_gate_a_b",
          "a_lambda", "b_group_w", "b_group_b", "b_scale", "ab_w_out", "c_norm", "c_w_pw1", "c_b_pw1", "c_dw_w",
          "c_dw_b", "c_ln_g", "c_ln_b", "c_w_pw2", "c_b_pw2", "xa_norm", "xa_mem_norm", "xa_wq", "xa_wk", "xa_wv",
          "xa_wo", "f_norm", "f_w_up", "f_dw_w", "f_dw_b", "f_w_down", "final_norm")
```

```python
import functools

import jax
import jax.numpy as jnp
from jax import lax
from jax.experimental import pallas as pl
from jax.experimental.pallas import tpu as pltpu

F32, BF16 = jnp.float32, jnp.bfloat16
SDS = jax.ShapeDtypeStruct
MESH = pl.DeviceIdType.MESH

N_DEV = 8
D = 1024
N_MEM = 256
XA_HEADS, XA_HD = 4, 256
HD_A = 128
CONV_A, CONV_C, CONV_F = 4, 31, 3
C_RG = 8.0
POOL_WINDOWS = (2, 4, 8, 16)
D_FF = 3 * D
EPS = 1e-6
ADAM_LR, ADAM_B1, ADAM_B2, ADAM_EPS, ADAM_WD, ADAM_STEP = 0.001, 0.9, 0.999, 1e-08, 0.01, 10

LANE = 128
SUB = 8
VMEM_LIMIT = 56 * 1024 * 1024
R_SEQ = 256
TM_ROW = 512


def _cp(n_axes):
    return pltpu.CompilerParams(dimension_semantics=("arbitrary",) * n_axes, vmem_limit_bytes=VMEM_LIMIT)


def _tile(n, pref):
    if n <= pref:
        return n
    best = None
    for t in range(LANE, pref + 1, LANE):
        if n % t == 0:
            best = t
    assert best is not None, (n, pref)
    return best


def _perm2(n):
    return (n % 2) * 4 + n // 2


_NN = (((1,), (0,)), ((), ()))
_NT = (((1,), (1,)), ((), ()))
_TN = (((0,), (0,)), ((), ()))


def _mm_call(name, grid, a, b, a_spec, b_spec, o_spec, out_shape, dims, acc_shape, extras=()):
    nk = grid[2]
    n_ex = len(extras)

    def finish(r, ex_refs, o_ref):
        for e in ex_refs:
            r = r + e[...]
        o_ref[...] = r.astype(o_ref.dtype)

    def body_one(a_ref, b_ref, *rest):
        finish(lax.dot_general(a_ref[...], b_ref[...], dims, preferred_element_type=F32), rest[:n_ex], rest[n_ex])

    def body_acc(a_ref, b_ref, *rest):
        ex_refs, o_ref, acc = rest[:n_ex], rest[n_ex], rest[n_ex + 1]
        k = pl.program_id(2)

        @pl.when(k == 0)
        def _():
            acc[...] = jnp.zeros_like(acc)

        acc[...] += lax.dot_general(a_ref[...], b_ref[...], dims, preferred_element_type=F32)

        @pl.when(k == nk - 1)
        def _():
            finish(acc[...], ex_refs, o_ref)

    return pl.pallas_call(
        body_one if nk == 1 else body_acc, out_shape=out_shape, grid=grid,
        in_specs=[a_spec, b_spec] + [s for _, s in extras], out_specs=o_spec,
        scratch_shapes=[] if nk == 1 else [pltpu.VMEM(acc_shape, F32)], name=name, compiler_params=_cp(3),
    )(a, b, *[e for e, _ in extras])


_K_WHOLE = 3072


def _mm_nn(a, b, *, out_dtype, name, bias=None, add=None, old=False):
    M, K = a.shape
    if old:
        tm, tk = _tile(M, 1024), _tile(K, 512)
    else:
        tk = K if K <= _K_WHOLE else _tile(K, 1024)
        tm = _tile(M, 1024 if K <= 1024 else 512)
    if b.ndim == 3:
        nb, _, bw = b.shape
        N, tn, nn = nb * bw, bw, nb
        b_spec = pl.BlockSpec((None, tk, bw), lambda m, n, k: (_perm2(n), k, 0))
    else:
        N = b.shape[1]
        tn = _tile(N, 1024)
        nn = N // tn
        b_spec = pl.BlockSpec((tk, tn), lambda m, n, k: (k, n))
    extras = []
    if bias is not None:
        extras.append((bias, pl.BlockSpec((1, tn), lambda m, n, k: (0, n))))
    if add is not None:
        extras.append((add, pl.BlockSpec((tm, tn), lambda m, n, k: (m, n))))
    return _mm_call(name, (M // tm, nn, K // tk), a, b, pl.BlockSpec((tm, tk), lambda m, n, k: (m, k)), b_spec,
                    pl.BlockSpec((tm, tn), lambda m, n, k: (m, n)), SDS((M, N), out_dtype), _NN, (tm, tn), extras)


def _mm_nt(a, b, *, out_dtype, name, add=None, old=False):
    M, N = a.shape
    if b.ndim == 3:
        nb, Ko, bw = b.shape
        tm = _tile(M, 1024)
        tn, tk, nk = _tile(Ko, 1024), bw, nb
        b_spec = pl.BlockSpec((None, tn, bw), lambda m, n, k: (_perm2(k), n, 0))
    else:
        Ko = b.shape[0]
        if old:
            tm, tk = _tile(M, 1024), _tile(N, 512)
        else:
            tk = N if N <= _K_WHOLE else _tile(N, 1024)
            tm = _tile(M, 1024 if N <= 1024 else 512)
        tn = _tile(Ko, 1024)
        nk = N // tk
        b_spec = pl.BlockSpec((tn, tk), lambda m, n, k: (n, k))
    extras = []
    if add is not None:
        extras.append((add, pl.BlockSpec((tm, tn), lambda m, n, k: (m, n))))
    return _mm_call(name, (M // tm, Ko // tn, nk), a, b, pl.BlockSpec((tm, tk), lambda m, n, k: (m, k)), b_spec,
                    pl.BlockSpec((tm, tn), lambda m, n, k: (m, n)), SDS((M, Ko), out_dtype), _NT, (tm, tn), extras)


def _mm_tn(a, b, *, out_dtype, name, blocks=None, old=False):
    S, Ka = a.shape
    Nb = b.shape[1]
    tm, tk = _tile(Ka, 1024), _tile(S, 512 if old else 2048)
    if blocks is not None:
        bw = blocks
        tn, nn = bw, Nb // bw
        o_spec = pl.BlockSpec((None, tm, bw), lambda m, n, k: (_perm2(n), m, 0))
        out_shape = SDS((nn, Ka, bw), out_dtype)
    else:
        tn = _tile(Nb, 1024)
        nn = Nb // tn
        o_spec = pl.BlockSpec((tm, tn), lambda m, n, k: (m, n))
        out_shape = SDS((Ka, Nb), out_dtype)
    return _mm_call(name, (Ka // tm, nn, S // tk), a, b, pl.BlockSpec((tk, tm), lambda m, n, k: (k, m)),
                    pl.BlockSpec((tk, tn), lambda m, n, k: (k, n)), o_spec, out_shape, _TN, (tm, tn))


def _row(tm, c):
    return pl.BlockSpec((tm, c), lambda i: (i, 0))


def _full(shape):
    nd = len(shape)
    return pl.BlockSpec(shape, lambda i: (0,) * nd)


def _rms_fwd(x, g, name):
    S = x.shape[0]
    tm = min(S, TM_ROW)

    def body(x_ref, g_ref, o_ref):
        xf = x_ref[...]
        r = lax.rsqrt(jnp.mean(xf * xf, axis=-1, keepdims=True) + EPS)
        o_ref[...] = ((xf * r) * g_ref[...]).astype(BF16)

    return pl.pallas_call(body, out_shape=SDS((S, D), BF16), grid=(S // tm,), in_specs=[_row(tm, D), _full((1, D))],
                          out_specs=_row(tm, D), name=name, compiler_params=_cp(1))(x, g)


def _rms_bwd(x, g, dn, dres, name):
    S = x.shape[0]
    tm = min(S, TM_ROW)
    want_dx = dres is not None

    def body(x_ref, g_ref, dn_ref, *rest):
        i = pl.program_id(0)
        dg_ref = rest[-1]

        @pl.when(i == 0)
        def _():
            dg_ref[...] = jnp.zeros_like(dg_ref)

        xf = x_ref[...]
        r = lax.rsqrt(jnp.mean(xf * xf, axis=-1, keepdims=True) + EPS)
        y = xf * r
        dn_v = dn_ref[...]
        dg_ref[...] += jnp.sum(dn_v * y, axis=0, keepdims=True)
        if want_dx:
            dres_ref, dx_ref, dxb_ref = rest[0], rest[1], rest[2]
            dy = dn_v * g_ref[...]
            dx = r * (dy - y * jnp.mean(dy * y, axis=-1, keepdims=True)) + dres_ref[...]
            dx_ref[...] = dx
            dxb_ref[...] = dx.astype(BF16)

    ins = [x, g, dn] + ([dres] if want_dx else [])
    in_specs = [_row(tm, D), _full((1, D)), _row(tm, D)] + ([_row(tm, D)] if want_dx else [])
    outs = ([SDS((S, D), F32), SDS((S, D), BF16)] if want_dx else []) + [SDS((1, D), F32)]
    out_specs = ([_row(tm, D), _row(tm, D)] if want_dx else []) + [_full((1, D))]
    return pl.pallas_call(body, out_shape=outs, grid=(S // tm,), in_specs=in_specs, out_specs=out_specs, name=name,
                          compiler_params=_cp(1))(*ins)


def _loss_head(x, g, tgt):
    S = x.shape[0]
    tm = min(S, TM_ROW)

    def body(x_ref, g_ref, t_ref, loss_ref, dx_ref, dxb_ref, dg_ref):
        i = pl.program_id(0)

        @pl.when(i == 0)
        def _():
            loss_ref[...] = jnp.zeros_like(loss_ref)
            dg_ref[...] = jnp.zeros_like(dg_ref)

        xf = x_ref[...]
        r = lax.rsqrt(jnp.mean(xf * xf, axis=-1, keepdims=True) + EPS)
        y = xf * r
        gv = g_ref[...]
        err = y * gv - t_ref[...]
        per_row = jnp.mean(err * err, axis=-1, keepdims=True)
        loss_ref[...] += 0.5 * jnp.sum(per_row, axis=0, keepdims=True)
        dn_v = err * (1.0 / D)
        dg_ref[...] += jnp.sum(dn_v * y, axis=0, keepdims=True)
        dy = dn_v * gv
        dx = r * (dy - y * jnp.mean(dy * y, axis=-1, keepdims=True))
        dx_ref[...] = dx
        dxb_ref[...] = dx.astype(BF16)

    return pl.pallas_call(
        body, out_shape=[SDS((1, 1), F32), SDS((S, D), F32), SDS((S, D), BF16), SDS((1, D), F32)], grid=(S // tm,),
        in_specs=[_row(tm, D), _full((1, D)), _row(tm, D)],
        out_specs=[_full((1, 1)), _row(tm, D), _row(tm, D), _full((1, D))], name="loss_head", compiler_params=_cp(1),
    )(x, g, tgt)


def _softmax_rows(s):
    m = jnp.max(s, axis=-1, keepdims=True)
    e = jnp.exp(s - m)
    return e / jnp.sum(e, axis=-1, keepdims=True)


def _attn_fwd(q, k, v, name):
    S = q.shape[0]
    tm = min(S, TM_ROW)
    scale = XA_HD ** -0.5

    def body(q_ref, k_ref, v_ref, o_ref):
        for h in range(XA_HEADS):
            sl = slice(h * XA_HD, (h + 1) * XA_HD)
            s = lax.dot_general(q_ref[:, sl], k_ref[:, sl], _NT, preferred_element_type=F32) * scale
            p = _softmax_rows(s)
            o_ref[:, sl] = lax.dot_general(p.astype(BF16), v_ref[:, sl], _NN, preferred_element_type=F32).astype(BF16)

    return pl.pallas_call(body, out_shape=SDS((S, D), BF16), grid=(S // tm,),
                          in_specs=[_row(tm, D), _full((N_MEM, D)), _full((N_MEM, D))], out_specs=_row(tm, D),
                          name=name, compiler_params=_cp(1))(q, k, v)


def _attn_bwd(q, k, v, do, name):
    S = q.shape[0]
    tm = min(S, TM_ROW)
    scale = XA_HD ** -0.5

    def body(q_ref, k_ref, v_ref, do_ref, dq_ref, dk_ref, dv_ref):
        i = pl.program_id(0)

        @pl.when(i == 0)
        def _():
            dk_ref[...] = jnp.zeros_like(dk_ref)
            dv_ref[...] = jnp.zeros_like(dv_ref)

        for h in range(XA_HEADS):
            sl = slice(h * XA_HD, (h + 1) * XA_HD)
            qh, kh, vh, doh = q_ref[:, sl], k_ref[:, sl], v_ref[:, sl], do_ref[:, sl]
            s = lax.dot_general(qh, kh, _NT, preferred_element_type=F32) * scale
            p = _softmax_rows(s)
            pb = p.astype(BF16)
            dv_ref[:, sl] += lax.dot_general(pb, doh, _TN, preferred_element_type=F32)
            dp = lax.dot_general(doh, vh, _NT, preferred_element_type=F32)
            ds = (p * (dp - jnp.sum(dp * p, axis=-1, keepdims=True)) * scale).astype(BF16)
            dq_ref[:, sl] = lax.dot_general(ds, kh, _NN, preferred_element_type=F32).astype(BF16)
            dk_ref[:, sl] += lax.dot_general(ds, qh, _TN, preferred_element_type=F32)

    return pl.pallas_call(
        body, out_shape=[SDS((S, D), BF16), SDS((N_MEM, D), F32), SDS((N_MEM, D), F32)], grid=(S // tm,),
        in_specs=[_row(tm, D), _full((N_MEM, D)), _full((N_MEM, D)), _row(tm, D)],
        out_specs=[_row(tm, D), _full((N_MEM, D)), _full((N_MEM, D))], name=name, compiler_params=_cp(1),
    )(q, k, v, do)


def _sigmoid(x):
    return 1.0 / (1.0 + jnp.exp(-x))


def _ln_silu_fwd(cv, g, b):
    S = cv.shape[0]
    tm = min(S, TM_ROW)

    def body(x_ref, g_ref, b_ref, o_ref):
        xf = x_ref[...]
        mu = jnp.mean(xf, axis=-1, keepdims=True)
        xc = xf - mu
        rstd = lax.rsqrt(jnp.mean(xc * xc, axis=-1, keepdims=True) + EPS)
        ln = (xc * rstd) * g_ref[...] + b_ref[...]
        o_ref[...] = (ln * _sigmoid(ln)).astype(BF16)

    return pl.pallas_call(body, out_shape=SDS((S, D), BF16), grid=(S // tm,),
                          in_specs=[_row(tm, D), _full((1, D)), _full((1, D))], out_specs=_row(tm, D),
                          name="ln_silu_fwd", compiler_params=_cp(1))(cv, g, b)


def _ln_silu_bwd(ds, cv, g, b, dx):
    S = cv.shape[0]
    tm = min(S, TM_ROW)

    def body(ds_ref, x_ref, g_ref, b_ref, dx_ref, dcv_ref, dg_ref, db_ref, db2_ref):
        i = pl.program_id(0)

        @pl.when(i == 0)
        def _():
            dg_ref[...] = jnp.zeros_like(dg_ref)
            db_ref[...] = jnp.zeros_like(db_ref)
            db2_ref[...] = jnp.zeros_like(db2_ref)

        xf = x_ref[...]
        mu = jnp.mean(xf, axis=-1, keepdims=True)
        xc = xf - mu
        rstd = lax.rsqrt(jnp.mean(xc * xc, axis=-1, keepdims=True) + EPS)
        xhat = xc * rstd
        gv = g_ref[...]
        ln = xhat * gv + b_ref[...]
        sg = _sigmoid(ln)
        dln = ds_ref[...].astype(F32) * (sg + ln * sg * (1.0 - sg))
        dg_ref[...] += jnp.sum(dln * xhat, axis=0, keepdims=True)
        db_ref[...] += jnp.sum(dln, axis=0, keepdims=True)
        db2_ref[...] += jnp.sum(dx_ref[...], axis=0, keepdims=True)
        dxh = dln * gv
        dcv_ref[...] = rstd * (dxh - jnp.mean(dxh, axis=-1, keepdims=True)
                               - xhat * jnp.mean(dxh * xhat, axis=-1, keepdims=True))

    return pl.pallas_call(
        body, out_shape=[SDS((S, D), F32), SDS((1, D), F32), SDS((1, D), F32), SDS((1, D), F32)], grid=(S // tm,),
        in_specs=[_row(tm, D), _row(tm, D), _full((1, D)), _full((1, D)), _row(tm, D)],
        out_specs=[_row(tm, D), _full((1, D)), _full((1, D)), _full((1, D))], name="ln_silu_bwd",
        compiler_params=_cp(1),
    )(ds, cv, g, b, dx)


_GELU_C, _GELU_K = 0.7978845608028654, 0.044715


def _gelu(x, with_grad=False):
    x2 = x * x
    t = jnp.tanh(_GELU_C * (x + _GELU_K * x * x2))
    gel = 0.5 * x * (1.0 + t)
    if not with_grad:
        return gel
    return gel, 0.5 * (1.0 + t) + 0.5 * x * (1.0 - t * t) * (_GELU_C * (1.0 + 3.0 * _GELU_K * x2))


def _expm1(x):
    poly = x * (1.0 + x * (0.5 + x * (1.0 / 6.0 + x * (1.0 / 24.0 + x * (1.0 / 120.0)))))
    return jnp.where(jnp.abs(x) < 0.05, poly, jnp.exp(x) - 1.0)


def _softplus(x):
    return jnp.maximum(x, 0.0) + jnp.log1p(jnp.exp(-jnp.abs(x)))


_SCAN_UNROLL = 4
_RB = 32
_HB = 16


def _sub_blocks(n_rows, n_lanes, fn):
    def step(idx, c):
        r0 = pl.multiple_of(idx * _RB, _RB)
        for lt in range(n_lanes // LANE):
            fn(r0, lt)
        return c

    lax.fori_loop(0, n_rows // _RB, step, 0)


def _lanes(lt):
    return pl.ds(lt * LANE, LANE)


def _psum8(x):
    parts = [x[i * SUB:(i + 1) * SUB] for i in range(x.shape[0] // SUB)]
    return functools.reduce(lambda p, q: p + q, parts)


def _scan_fwd(a_s, b_s, out_ref, carry_ref, n_groups):
    row = lax.broadcasted_iota(jnp.int32, (SUB, LANE), 0)

    def step(g, carry):
        i = pl.multiple_of(g * SUB, SUB)
        a8, b8 = a_s[pl.ds(i, SUB), :], b_s[pl.ds(i, SUB), :]
        for s in (1, 2, 4):
            a_sh = jnp.where(row >= s, pltpu.roll(a8, s, 0), 1.0)
            b_sh = jnp.where(row >= s, pltpu.roll(b8, s, 0), 0.0)
            b8 = a8 * b_sh + b8
            a8 = a8 * a_sh
        h8 = a8 * carry + b8
        out_ref[pl.ds(i, SUB), :] = h8
        return jnp.broadcast_to(h8[SUB - 1:SUB, :], (SUB, LANE))

    carry_ref[...] = lax.fori_loop(0, n_groups, step, carry_ref[...], unroll=_SCAN_UNROLL)


def _scan_bwd(a_s, b_s, out_ref, carry_ref, n_groups):
    row = lax.broadcasted_iota(jnp.int32, (SUB, LANE), 0)

    def step(gi, carry):
        i = pl.multiple_of((n_groups - 1 - gi) * SUB, SUB)
        a8, b8 = a_s[pl.ds(i, SUB), :], b_s[pl.ds(i, SUB), :]
        for s in (1, 2, 4):
            a_sh = jnp.where(row < SUB - s, pltpu.roll(a8, SUB - s, 0), 1.0)
            b_sh = jnp.where(row < SUB - s, pltpu.roll(b8, SUB - s, 0), 0.0)
            b8 = a8 * b_sh + b8
            a8 = a8 * a_sh
        h8 = a8 * carry + b8
        out_ref[pl.ds(i, SUB), :] = h8
        return jnp.broadcast_to(h8[0:1, :], (SUB, LANE))

    carry_ref[...] = lax.fori_loop(0, n_groups, step, carry_ref[...], unroll=_SCAN_UNROLL)


def _rglru_pre(xr, wgx_ref, bgx_ref, wga_ref, bga_ref, lam_ref):
    xrb = xr.astype(BF16)
    wgx, wga = wgx_ref[0].astype(BF16), wga_ref[0].astype(BF16)
    gx = _sigmoid(lax.dot_general(xrb, wgx, _NN, preferred_element_type=F32) + bgx_ref[...])
    ga = _sigmoid(lax.dot_general(xrb, wga, _NN, preferred_element_type=F32) + bga_ref[...])
    sp = _softplus(-lam_ref[...])
    log_a = -C_RG * ga * sp
    a = jnp.exp(log_a)
    mult = jnp.sqrt(-_expm1(2.0 * log_a))
    return gx, ga, sp, a, mult, xrb, wgx, wga


def _a_specs():
    vec = pl.BlockSpec((1, HD_A), lambda c, j: (0, c))
    mat = pl.BlockSpec((1, HD_A, HD_A), lambda c, j: (c, 0, 0))
    return [pl.BlockSpec((CONV_A, HD_A), lambda c, j: (0, c)), vec, mat, vec, mat, vec, vec]


def _a_fwd(zp, conv_w, conv_b, wgx, bgx, wga, bga, lam):
    S = zp.shape[0]
    R, nt = R_SEQ, D // HD_A
    H = SUB

    def body(z_ref, cw_ref, cb_ref, wgx_ref, bgx_ref, wga_ref, bga_ref, lam_ref, ya_ref, h_ref, ext, a_s, b_s, hc):
        j = pl.program_id(1)

        @pl.when(j == 0)
        def _():
            ext[0:H, :] = jnp.zeros((H, HD_A), F32)
            hc[...] = jnp.zeros_like(hc)

        ext[H:H + R, :] = z_ref[:, HD_A:2 * HD_A].astype(F32)
        xr = cb_ref[...]
        for k in range(CONV_A):
            xr = xr + cw_ref[k:k + 1, :] * ext[pl.ds(H - (CONV_A - 1 - k), R), :]
        gx, _, _, a, mult, _, _, _ = _rglru_pre(xr, wgx_ref, bgx_ref, wga_ref, bga_ref, lam_ref)
        a_s[...] = a
        b_s[...] = mult * (gx * xr)
        _scan_fwd(a_s, b_s, h_ref, hc, R // SUB)
        ya_ref[...] = (_gelu(z_ref[:, 0:HD_A].astype(F32)) * h_ref[...]).astype(BF16)
        ext[0:H, :] = ext[R:R + H, :]

    return pl.pallas_call(
        body, out_shape=[SDS((S, D + D // 2), BF16), SDS((S, D), F32)], grid=(nt, S // R),
        in_specs=[pl.BlockSpec((R, 2 * HD_A), lambda c, j: (j, c))] + _a_specs(),
        out_specs=[pl.BlockSpec((R, HD_A), lambda c, j: (j, c)), pl.BlockSpec((R, HD_A), lambda c, j: (j, c))],
        scratch_shapes=[pltpu.VMEM((H + R, HD_A), F32), pltpu.VMEM((R, HD_A), F32), pltpu.VMEM((R, HD_A), F32),
                        pltpu.VMEM((SUB, HD_A), F32)],
        name="rglru_fwd", compiler_params=_cp(2),
    )(zp, conv_w, conv_b, wgx, bgx, wga, bga, lam)


def _a_bwd(dyab, zp, h, conv_w, conv_b, wgx, bgx, wga, bga, lam):
    S = zp.shape[0]
    R, nt, nch = R_SEQ, D // HD_A, S // R_SEQ
    H = SUB

    def rows(c, j):
        return (nch - 1 - j, c)

    def halo(c, j):
        return (jnp.maximum((nch - 1 - j) * (R // H) - 1, 0), c)

    def halo_z(c, j):
        return (jnp.maximum((nch - 1 - j) * (R // _HB) - 1, 0), c)

    def body(dy_ref, z_ref, zh_ref, h_ref, hh_ref, cw_ref, cb_ref, wgx_ref, bgx_ref, wga_ref, bga_ref, lam_ref,
             dz_ref, dcw_ref, dcb_ref, dwgx_ref, dbgx_ref, dwga_ref, dbga_ref, dlam_ref,
             ext_z, ext_h, ext_mu, ext_d, a_s, b_s, muc):
        j = pl.program_id(1)
        first_chunk = (nch - 1 - j) == 0

        @pl.when(j == 0)
        def _():
            ext_mu[R:R + H, :] = jnp.zeros((H, HD_A), F32)
            ext_d[R:R + H, :] = jnp.zeros((H, HD_A), F32)
            muc[...] = jnp.zeros_like(muc)
            for r in (dcw_ref, dcb_ref, dwgx_ref, dbgx_ref, dwga_ref, dbga_ref, dlam_ref):
                r[...] = jnp.zeros_like(r)

        zg = z_ref[:, 0:HD_A].astype(F32)
        ext_z[0:H, :] = jnp.where(first_chunk, 0.0, zh_ref[_HB - H:_HB, HD_A:2 * HD_A].astype(F32))
        ext_z[H:H + R, :] = z_ref[:, HD_A:2 * HD_A].astype(F32)
        ext_h[0:H, :] = jnp.where(first_chunk, 0.0, hh_ref[...])
        ext_h[H:H + R, :] = h_ref[...]
        xr = cb_ref[...]
        for k in range(CONV_A):
            xr = xr + cw_ref[k:k + 1, :] * ext_z[pl.ds(H - (CONV_A - 1 - k), R), :]
        gx, ga, sp, a, mult, xrb, wgxb, wgab = _rglru_pre(xr, wgx_ref, bgx_ref, wga_ref, bga_ref, lam_ref)
        gel, dgel = _gelu(zg, with_grad=True)
        dy = dy_ref[...].astype(F32)
        dh = dy * gel
        dz_ref[:, 0:HD_A] = (dy * h_ref[...] * dgel).astype(BF16)
        a_s[...] = a
        b_s[...] = a * dh
        _scan_bwd(a_s, b_s, ext_mu, muc, R // SUB)
        lam_t = dh + ext_mu[pl.ds(1, R), :]
        ext_mu[R:R + H, :] = ext_mu[0:H, :]
        da = lam_t * ext_h[pl.ds(H - 1, R), :]
        gxr = gx * xr
        dlog_a = da * a - (lam_t * gxr) * (a * a) / mult
        dgx = lam_t * mult * xr
        dxr = lam_t * mult * gx
        lam_v = lam_ref[...]
        dlam_ref[...] += jnp.sum(dlog_a * ga, axis=0, keepdims=True) * (C_RG * _sigmoid(-lam_v))
        dpa = (dlog_a * (-C_RG * sp)) * ga * (1.0 - ga)
        dpx = dgx * gx * (1.0 - gx)
        dbga_ref[...] += jnp.sum(dpa, axis=0, keepdims=True)
        dbgx_ref[...] += jnp.sum(dpx, axis=0, keepdims=True)
        dpab, dpxb = dpa.astype(BF16), dpx.astype(BF16)
        dwga_ref[0] += lax.dot_general(xrb, dpab, _TN, preferred_element_type=F32)
        dwgx_ref[0] += lax.dot_general(xrb, dpxb, _TN, preferred_element_type=F32)
        dxr = (dxr + lax.dot_general(dpab, wgab, _NT, preferred_element_type=F32)
               + lax.dot_general(dpxb, wgxb, _NT, preferred_element_type=F32))
        dcb_ref[...] += jnp.sum(dxr, axis=0, keepdims=True)
        ext_d[0:R, :] = dxr
        dzr = jnp.zeros((R, HD_A), F32)
        for k in range(CONV_A):
            sh = CONV_A - 1 - k
            dcw_ref[k:k + 1, :] += jnp.sum(dxr * ext_z[pl.ds(H - sh, R), :], axis=0, keepdims=True)
            dzr = dzr + cw_ref[k:k + 1, :] * ext_d[pl.ds(sh, R), :]
        dz_ref[:, HD_A:2 * HD_A] = dzr.astype(BF16)
        ext_d[R:R + H, :] = ext_d[0:H, :]

    vec_o = pl.BlockSpec((1, HD_A), lambda c, j: (0, c))
    mat_o = pl.BlockSpec((1, HD_A, HD_A), lambda c, j: (c, 0, 0))
    return pl.pallas_call(
        body,
        out_shape=[SDS((S, 2 * D + D // 2), BF16), SDS((CONV_A, D), F32), SDS((1, D), F32), SDS((nt, HD_A, HD_A), F32),
                   SDS((1, D), F32), SDS((nt, HD_A, HD_A), F32), SDS((1, D), F32), SDS((1, D), F32)],
        grid=(nt, nch),
        in_specs=[pl.BlockSpec((R, HD_A), rows), pl.BlockSpec((R, 2 * HD_A), rows),
                  pl.BlockSpec((_HB, 2 * HD_A), halo_z), pl.BlockSpec((R, HD_A), rows),
                  pl.BlockSpec((H, HD_A), halo)] + _a_specs(),
        out_specs=[pl.BlockSpec((R, 2 * HD_A), rows), pl.BlockSpec((CONV_A, HD_A), lambda c, j: (0, c)), vec_o, mat_o,
                   vec_o, mat_o, vec_o, vec_o],
        scratch_shapes=[pltpu.VMEM((H + R, HD_A), F32), pltpu.VMEM((H + R, HD_A), F32), pltpu.VMEM((R + H, HD_A), F32),
                        pltpu.VMEM((R + H, HD_A), F32), pltpu.VMEM((R, HD_A), F32), pltpu.VMEM((R, HD_A), F32),
                        pltpu.VMEM((SUB, HD_A), F32)],
        name="rglru_bwd", compiler_params=_cp(2),
    )(dyab, zp, zp, h, h, conv_w, conv_b, wgx, bgx, wga, bga, lam)


_POOL_H = 16
_POOL_T0 = 2 * D // HD_A
_POOL_Y0 = D // HD_A


def _pool_mean_minus(u, ext, g, t1):
    R = u.shape[0]
    acc, wins = u, []
    for k in range(1, _POOL_H):
        acc = acc + ext[pl.ds(_POOL_H - k, R), :]
        if k + 1 in POOL_WINDOWS:
            wins.append(acc)
    win = jnp.where(g == 0, wins[0], jnp.where(g == 1, wins[1], jnp.where(g == 2, wins[2], wins[3])))
    return win / jnp.minimum(t1, _pool_width(g)) - u


def _pool_width(g):
    return jnp.where(g == 0, 2.0, jnp.where(g == 1, 4.0, jnp.where(g == 2, 8.0, 16.0)))


def _b_fwd(zp, yab, wg, bg, sc):
    S = zp.shape[0]
    R, H = R_SEQ, _POOL_H

    def body(z_ref, wg_ref, bg_ref, sc_ref, yab_in, yb_ref, ext):
        del yab_in
        g, j = pl.program_id(0), pl.program_id(1)

        @pl.when(j == 0)
        def _():
            ext[0:H, :] = jnp.zeros((H, HD_A), F32)

        u = z_ref[...].astype(F32)
        ext[H:H + R, :] = u
        t1 = (j * R + 1 + lax.broadcasted_iota(jnp.int32, (R, HD_A), 0)).astype(F32)
        p = _pool_mean_minus(u, ext, g, t1)
        lin = lax.dot_general(p.astype(BF16), wg_ref[0].astype(BF16), _NN, preferred_element_type=F32) + bg_ref[...]
        yb_ref[...] = (lin * sc_ref[...]).astype(BF16)
        ext[0:H, :] = ext[R:R + H, :]

    vec = pl.BlockSpec((1, HD_A), lambda g, j: (0, g))
    return pl.pallas_call(
        body, out_shape=SDS(yab.shape, yab.dtype), grid=(len(POOL_WINDOWS), S // R),
        in_specs=[pl.BlockSpec((R, HD_A), lambda g, j: (j, _POOL_T0 + g)),
                  pl.BlockSpec((1, HD_A, HD_A), lambda g, j: (g, 0, 0)), vec, vec, pl.BlockSpec(memory_space=pl.ANY)],
        out_specs=pl.BlockSpec((R, HD_A), lambda g, j: (j, _POOL_Y0 + g)),
        scratch_shapes=[pltpu.VMEM((H + R, HD_A), F32)], input_output_aliases={4: 0},
        name="pool_fwd", compiler_params=_cp(2),
    )(zp, wg, bg, sc, yab)


def _b_bwd(dyab, zp, dzp, wg, bg, sc):
    S = zp.shape[0]
    R, H, nch, ng = R_SEQ, _POOL_H, S // R_SEQ, len(POOL_WINDOWS)

    def body(dy_ref, z_ref, zh_ref, wg_ref, bg_ref, sc_ref, dz_in, dz_ref, dwg_ref, dbg_ref, dsc_ref, ext_u, ext_q):
        del dz_in
        g, j = pl.program_id(0), pl.program_id(1)
        jj = nch - 1 - j

        @pl.when(j == 0)
        def _():
            ext_q[R:R + H, :] = jnp.zeros((H, HD_A), F32)
            for r in (dwg_ref, dbg_ref, dsc_ref):
                r[...] = jnp.zeros_like(r)

        u = z_ref[...].astype(F32)
        ext_u[0:H, :] = jnp.where(jj == 0, 0.0, zh_ref[...].astype(F32))
        ext_u[H:H + R, :] = u
        t1 = (jj * R + 1 + lax.broadcasted_iota(jnp.int32, (R, HD_A), 0)).astype(F32)
        pb = _pool_mean_minus(u, ext_u, g, t1).astype(BF16)
        wgb = wg_ref[0].astype(BF16)
        lin = lax.dot_general(pb, wgb, _NN, preferred_element_type=F32) + bg_ref[...]
        dy = dy_ref[...].astype(F32)
        dsc_ref[...] += jnp.sum(dy * lin, axis=0, keepdims=True)
        dlin = dy * sc_ref[...]
        dbg_ref[...] += jnp.sum(dlin, axis=0, keepdims=True)
        dlb = dlin.astype(BF16)
        dwg_ref[0] += lax.dot_general(pb, dlb, _TN, preferred_element_type=F32)
        dp = lax.dot_general(dlb, wgb, _NT, preferred_element_type=F32)
        q = dp / jnp.minimum(t1, _pool_width(g))
        ext_q[0:R, :] = q
        acc, wins = q, []
        for k in range(1, H):
            acc = acc + ext_q[pl.ds(k, R), :]
            if k + 1 in POOL_WINDOWS:
                wins.append(acc)
        win = jnp.where(g == 0, wins[0], jnp.where(g == 1, wins[1], jnp.where(g == 2, wins[2], wins[3])))
        dz_ref[...] = (win - dp).astype(BF16)
        ext_q[R:R + H, :] = ext_q[0:H, :]

    vec = pl.BlockSpec((1, HD_A), lambda g, j: (0, g))
    mat = pl.BlockSpec((1, HD_A, HD_A), lambda g, j: (g, 0, 0))
    return pl.pallas_call(
        body, out_shape=[SDS(dzp.shape, dzp.dtype), SDS((ng, HD_A, HD_A), F32), SDS((1, D // 2), F32),
                         SDS((1, D // 2), F32)],
        grid=(ng, nch),
        in_specs=[pl.BlockSpec((R, HD_A), lambda g, j: (nch - 1 - j, _POOL_Y0 + g)),
                  pl.BlockSpec((R, HD_A), lambda g, j: (nch - 1 - j, _POOL_T0 + g)),
                  pl.BlockSpec((H, HD_A), lambda g, j: (jnp.maximum((nch - 1 - j) * (R // H) - 1, 0), _POOL_T0 + g)),
                  mat, vec, vec, pl.BlockSpec(memory_space=pl.ANY)],
        out_specs=[pl.BlockSpec((R, HD_A), lambda g, j: (nch - 1 - j, _POOL_T0 + g)), mat, vec, vec],
        scratch_shapes=[pltpu.VMEM((H + R, HD_A), F32), pltpu.VMEM((R + H, HD_A), F32)],
        input_output_aliases={6: 0}, name="pool_bwd", compiler_params=_cp(2),
    )(dyab, zp, zp, wg, bg, sc, dzp)


_CW_F = 768


def _f_fwd(hp, w, b, name):
    S = hp.shape[0]
    R, H, cw = R_SEQ, SUB, _CW_F
    nlt = cw // LANE

    def body(h_ref, w_ref, b_ref, o_ref, ext):
        j = pl.program_id(1)

        @pl.when(j == 0)
        def _():
            ext[:, 0:H, :] = jnp.zeros((nlt, H, LANE), F32)

        def stage(r0, lt):
            ext[lt, pl.ds(pl.multiple_of(r0 + H, SUB), _RB), :] = h_ref[pl.ds(r0, _RB), _lanes(lt)].astype(F32)

        def main(r0, lt):
            ls = _lanes(lt)
            gp = b_ref[:, ls]
            for k in range(CONV_F):
                gp = gp + w_ref[k:k + 1, ls] * ext[lt, pl.ds(r0 + (H - (CONV_F - 1 - k)), _RB), :]
            up = h_ref[pl.ds(r0, _RB), _lanes(lt + nlt)].astype(F32)
            o_ref[pl.ds(r0, _RB), ls] = (_gelu(gp) * up).astype(BF16)

        _sub_blocks(R, cw, stage)
        _sub_blocks(R, cw, main)
        ext[:, 0:H, :] = ext[:, R:R + H, :]

    return pl.pallas_call(
        body, out_shape=SDS((S, D_FF), BF16), grid=(D_FF // cw, S // R),
        in_specs=[pl.BlockSpec((R, 2 * cw), lambda c, j: (j, c)), pl.BlockSpec((CONV_F, cw), lambda c, j: (0, c)),
                  pl.BlockSpec((1, cw), lambda c, j: (0, c))],
        out_specs=pl.BlockSpec((R, cw), lambda c, j: (j, c)),
        scratch_shapes=[pltpu.VMEM((nlt, H + R, LANE), F32)], name=name, compiler_params=_cp(2),
    )(hp, w, b)


def _f_bwd(dact, hp, w, b, name):
    S = hp.shape[0]
    R, H, cw, nch = R_SEQ, SUB, _CW_F, S // R_SEQ
    nlt = cw // LANE

    def body(da_ref, h_ref, hh_ref, w_ref, b_ref, dh_ref, dw_ref, db_ref, ext_g, ext_d, acc):
        j = pl.program_id(1)
        jj = nch - 1 - j

        @pl.when(j == 0)
        def _():
            ext_d[:, R:R + H, :] = jnp.zeros((nlt, H, LANE), F32)
            acc[...] = jnp.zeros_like(acc)

        for lt in range(nlt):
            ext_g[lt, 0:H, :] = jnp.where(jj == 0, 0.0, hh_ref[_HB - H:_HB, lt * LANE:(lt + 1) * LANE].astype(F32))

        def stage(r0, lt):
            ext_g[lt, pl.ds(pl.multiple_of(r0 + H, SUB), _RB), :] = h_ref[pl.ds(r0, _RB), _lanes(lt)].astype(F32)

        def first(r0, lt):
            ls, lu, rs = _lanes(lt), _lanes(lt + nlt), pl.ds(r0, _RB)
            taps = [ext_g[lt, pl.ds(r0 + (H - (CONV_F - 1 - k)), _RB), :] for k in range(CONV_F)]
            gp = b_ref[:, ls]
            for k in range(CONV_F):
                gp = gp + w_ref[k:k + 1, ls] * taps[k]
            gel, dgel = _gelu(gp, with_grad=True)
            da = da_ref[rs, ls].astype(F32)
            dh_ref[rs, lu] = (da * gel).astype(BF16)
            dgp = da * h_ref[rs, lu].astype(F32) * dgel
            ext_d[lt, rs, :] = dgp
            acc[CONV_F * SUB:(CONV_F + 1) * SUB, ls] += _psum8(dgp)
            for k in range(CONV_F):
                acc[k * SUB:(k + 1) * SUB, ls] += _psum8(dgp * taps[k])

        def second(r0, lt):
            ls = _lanes(lt)
            dhg = w_ref[CONV_F - 1:CONV_F, ls] * ext_d[lt, pl.ds(r0, _RB), :]
            for k in range(CONV_F - 1):
                dhg = dhg + w_ref[k:k + 1, ls] * ext_d[lt, pl.ds(r0 + (CONV_F - 1 - k), _RB), :]
            dh_ref[pl.ds(r0, _RB), ls] = dhg.astype(BF16)

        _sub_blocks(R, cw, stage)
        _sub_blocks(R, cw, first)
        _sub_blocks(R, cw, second)
        ext_d[:, R:R + H, :] = ext_d[:, 0:H, :]

        @pl.when(j == nch - 1)
        def _():
            for k in range(CONV_F):
                dw_ref[k:k + 1, :] = jnp.sum(acc[k * SUB:(k + 1) * SUB, :], axis=0, keepdims=True)
            db_ref[...] = jnp.sum(acc[CONV_F * SUB:(CONV_F + 1) * SUB, :], axis=0, keepdims=True)

    rows = lambda c, j: (nch - 1 - j, c)
    return pl.pallas_call(
        body, out_shape=[SDS((S, 2 * D_FF), BF16), SDS((CONV_F, D_FF), F32), SDS((1, D_FF), F32)],
        grid=(D_FF // cw, nch),
        in_specs=[pl.BlockSpec((R, cw), rows), pl.BlockSpec((R, 2 * cw), rows),
                  pl.BlockSpec((_HB, 2 * cw), lambda c, j: (jnp.maximum((nch - 1 - j) * (R // _HB) - 1, 0), c)),
                  pl.BlockSpec((CONV_F, cw), lambda c, j: (0, c)), pl.BlockSpec((1, cw), lambda c, j: (0, c))],
        out_specs=[pl.BlockSpec((R, 2 * cw), rows), pl.BlockSpec((CONV_F, cw), lambda c, j: (0, c)),
                   pl.BlockSpec((1, cw), lambda c, j: (0, c))],
        scratch_shapes=[pltpu.VMEM((nlt, H + R, LANE), F32), pltpu.VMEM((nlt, R + H, LANE), F32),
                        pltpu.VMEM(((CONV_F + 1) * SUB, cw), F32)], name=name,
        compiler_params=_cp(2),
    )(dact, hp, hp, w, b)


_CW_C = 256
_H_C = 32


def _c_fwd(h1p, w, b):
    S = h1p.shape[0]
    R, H, cw = R_SEQ, _H_C, _CW_C
    nlt = cw // LANE

    def body(h_ref, w_ref, b_ref, o_ref, ext):
        j = pl.program_id(1)

        @pl.when(j == 0)
        def _():
            ext[:, 0:H, :] = jnp.zeros((nlt, H, LANE), F32)

        def stage(r0, lt):
            rs = pl.ds(r0, _RB)
            gate = h_ref[rs, _lanes(lt + nlt)].astype(F32)
            ext[lt, pl.ds(pl.multiple_of(r0 + H, SUB), _RB), :] = h_ref[rs, _lanes(lt)].astype(F32) * _sigmoid(gate)

        def main(r0, lt):
            ls = _lanes(lt)
            cv = b_ref[:, ls]
            for k in range(CONV_C):
                cv = cv + w_ref[k:k + 1, ls] * ext[lt, pl.ds(r0 + (H - (CONV_C - 1 - k)), _RB), :]
            o_ref[pl.ds(r0, _RB), ls] = cv

        _sub_blocks(R, cw, stage)
        _sub_blocks(R, cw, main)
        ext[:, 0:H, :] = ext[:, R:R + H, :]

    return pl.pallas_call(
        body, out_shape=SDS((S, D), F32), grid=(D // cw, S // R),
        in_specs=[pl.BlockSpec((R, 2 * cw), lambda c, j: (j, c)), pl.BlockSpec((CONV_C, cw), lambda c, j: (0, c)),
                  pl.BlockSpec((1, cw), lambda c, j: (0, c))],
        out_specs=pl.BlockSpec((R, cw), lambda c, j: (j, c)),
        scratch_shapes=[pltpu.VMEM((nlt, H + R, LANE), F32)], name="conf_conv_fwd", compiler_params=_cp(2),
    )(h1p, w, b)


def _c_bwd(dcv, h1p, w):
    S = h1p.shape[0]
    R, H, cw, nch = R_SEQ, _H_C, _CW_C, S // R_SEQ
    nlt = cw // LANE
    a_b, a_val, a_gate = CONV_C * SUB, (CONV_C + 1) * SUB, (CONV_C + 2) * SUB

    def body(dc_ref, h_ref, hh_ref, w_ref, dh_ref, dw_ref, db_ref, db1_ref, ext_u, ext_d, acc):
        j = pl.program_id(1)
        jj = nch - 1 - j

        @pl.when(j == 0)
        def _():
            ext_d[:, R:R + H, :] = jnp.zeros((nlt, H, LANE), F32)
            acc[...] = jnp.zeros_like(acc)

        for lt in range(nlt):
            ext_u[lt, 0:H, :] = jnp.where(
                jj == 0, 0.0, hh_ref[:, lt * LANE:(lt + 1) * LANE].astype(F32)
                * _sigmoid(hh_ref[:, cw + lt * LANE:cw + (lt + 1) * LANE].astype(F32)))

        def stage(r0, lt):
            rs, ls = pl.ds(r0, _RB), _lanes(lt)
            gate = h_ref[rs, _lanes(lt + nlt)].astype(F32)
            ext_u[lt, pl.ds(pl.multiple_of(r0 + H, SUB), _RB), :] = h_ref[rs, ls].astype(F32) * _sigmoid(gate)
            ext_d[lt, rs, :] = dc_ref[rs, ls]

        def first(r0, lt):
            ls = _lanes(lt)
            dc = dc_ref[pl.ds(r0, _RB), ls]
            acc[a_b:a_b + SUB, ls] += _psum8(dc)
            for k in range(CONV_C):
                tap = ext_u[lt, pl.ds(r0 + (H - (CONV_C - 1 - k)), _RB), :]
                acc[k * SUB:(k + 1) * SUB, ls] += _psum8(dc * tap)

        def second(r0, lt):
            rs, ls, lg = pl.ds(r0, _RB), _lanes(lt), _lanes(lt + nlt)
            du = w_ref[CONV_C - 1:CONV_C, ls] * ext_d[lt, rs, :]
            for k in range(CONV_C - 1):
                du = du + w_ref[k:k + 1, ls] * ext_d[lt, pl.ds(r0 + (CONV_C - 1 - k), _RB), :]
            val = h_ref[rs, ls].astype(F32)
            sg = _sigmoid(h_ref[rs, lg].astype(F32))
            dval = du * sg
            dgate = du * val * sg * (1.0 - sg)
            acc[a_val:a_val + SUB, ls] += _psum8(dval)
            acc[a_gate:a_gate + SUB, ls] += _psum8(dgate)
            dh_ref[rs, ls] = dval.astype(BF16)
            dh_ref[rs, lg] = dgate.astype(BF16)

        _sub_blocks(R, cw, stage)
        _sub_blocks(R, cw, first)
        _sub_blocks(R, cw, second)
        ext_d[:, R:R + H, :] = ext_d[:, 0:H, :]

        @pl.when(j == nch - 1)
        def _():
            for k in range(CONV_C):
                dw_ref[k:k + 1, :] = jnp.sum(acc[k * SUB:(k + 1) * SUB, :], axis=0, keepdims=True)
            db_ref[...] = jnp.sum(acc[a_b:a_b + SUB, :], axis=0, keepdims=True)
            db1_ref[:, 0:cw] = jnp.sum(acc[a_val:a_val + SUB, :], axis=0, keepdims=True)
            db1_ref[:, cw:2 * cw] = jnp.sum(acc[a_gate:a_gate + SUB, :], axis=0, keepdims=True)

    rows = lambda c, j: (nch - 1 - j, c)
    return pl.pallas_call(
        body, out_shape=[SDS((S, 2 * D), BF16), SDS((CONV_C, D), F32), SDS((1, D), F32), SDS((1, 2 * D), F32)],
        grid=(D // cw, nch),
        in_specs=[pl.BlockSpec((R, cw), rows), pl.BlockSpec((R, 2 * cw), rows),
                  pl.BlockSpec((H, 2 * cw), lambda c, j: (jnp.maximum((nch - 1 - j) * (R // H) - 1, 0), c)),
                  pl.BlockSpec((CONV_C, cw), lambda c, j: (0, c))],
        out_specs=[pl.BlockSpec((R, 2 * cw), rows), pl.BlockSpec((CONV_C, cw), lambda c, j: (0, c)),
                   pl.BlockSpec((1, cw), lambda c, j: (0, c)), pl.BlockSpec((1, 2 * cw), lambda c, j: (0, c))],
        scratch_shapes=[pltpu.VMEM((nlt, H + R, LANE), F32), pltpu.VMEM((nlt, R + H, LANE), F32),
                        pltpu.VMEM(((CONV_C + 3) * SUB, cw), F32)], name="conf_conv_bwd",
        compiler_params=_cp(2),
    )(dcv, h1p, h1p, w)


def _local_step(x, mem, tgt, W):
    G = {}

    def xattn_fwd(xin, l):
        old = l == 1
        n = _rms_fwd(xin, W["xa_norm"][l:l + 1], f"xa_norm_fwd{l}")
        q = _mm_nn(n, W["xa_wq"][l], out_dtype=BF16, name=f"xa_q{l}", old=old)
        mn = _rms_fwd(mem, W["xa_mem_norm"][l:l + 1], f"xa_memnorm_fwd{l}")
        k = _mm_nn(mn, W["xa_wk"][l], out_dtype=BF16, name=f"xa_k{l}", old=old)
        v = _mm_nn(mn, W["xa_wv"][l], out_dtype=BF16, name=f"xa_v{l}", old=old)
        o = _attn_fwd(q, k, v, f"xa_attn_fwd{l}")
        xout = _mm_nn(o, W["xa_wo"][l], out_dtype=F32, name=f"xa_o{l}", add=xin, old=old)
        return xout, (xin, n, q, mn, k, v, o)

    def xattn_bwd(dx, dxb, saved, l):
        old = l == 1
        xin, n, q, mn, k, v, o = saved
        do = _mm_nt(dxb, W["xa_wo"][l], out_dtype=BF16, name=f"xa_do{l}", old=old)
        G[f"xa_wo{l}"] = _mm_tn(o, dxb, out_dtype=BF16, name=f"xa_dwo{l}", old=old)
        dq, dk, dv = _attn_bwd(q, k, v, do, f"xa_attn_bwd{l}")
        dkb, dvb = dk.astype(BF16), dv.astype(BF16)
        G[f"xa_wq{l}"] = _mm_tn(n, dq, out_dtype=BF16, name=f"xa_dwq{l}", old=old)
        G[f"xa_wk{l}"] = _mm_tn(mn, dkb, out_dtype=BF16, name=f"xa_dwk{l}", old=old)
        G[f"xa_wv{l}"] = _mm_tn(mn, dvb, out_dtype=BF16, name=f"xa_dwv{l}", old=old)
        dmn = _mm_nt(dkb, W["xa_wk"][l], out_dtype=F32, name=f"xa_dmn_k{l}", old=old)
        dmn = _mm_nt(dvb, W["xa_wv"][l], out_dtype=F32, name=f"xa_dmn_v{l}", add=dmn, old=old)
        (G[f"xa_mem_norm{l}"],) = _rms_bwd(mem, W["xa_mem_norm"][l:l + 1], dmn, None, f"xa_memnorm_bwd{l}")
        dn = _mm_nt(dq, W["xa_wq"][l], out_dtype=F32, name=f"xa_dn{l}", old=old)
        dx, dxb, G[f"xa_norm{l}"] = _rms_bwd(xin, W["xa_norm"][l:l + 1], dn, dx, f"xa_norm_bwd{l}")
        return dx, dxb

    def ffn_fwd(xin, l):
        old = l == 1
        n = _rms_fwd(xin, W["f_norm"][l:l + 1], f"f_norm_fwd{l}")
        hp = _mm_nn(n, W["f_w_up"][l], out_dtype=BF16, name=f"f_up{l}", old=old)
        act = _f_fwd(hp, W["f_dw_w"][l], W["f_dw_b"][l:l + 1], f"f_conv_fwd{l}")
        xout = _mm_nn(act, W["f_w_down"][l], out_dtype=F32, name=f"f_down{l}", add=xin, old=old)
        return xout, (xin, n, hp, act)

    def ffn_bwd(dx, dxb, saved, l):
        old = l == 1
        xin, n, hp, act = saved
        dact = _mm_nt(dxb, W["f_w_down"][l], out_dtype=BF16, name=f"f_dact{l}", old=old)
        G[f"f_w_down{l}"] = _mm_tn(act, dxb, out_dtype=BF16, name=f"f_dwdown{l}", old=old)
        dhp, G[f"f_dw_w{l}"], G[f"f_dw_b{l}"] = _f_bwd(dact, hp, W["f_dw_w"][l], W["f_dw_b"][l:l + 1], f"f_conv_bwd{l}")
        G[f"f_w_up{l}"] = _mm_tn(n, dhp, out_dtype=BF16, name=f"f_dwup{l}", blocks=_CW_F, old=old)
        dn = _mm_nt(dhp, W["f_w_up"][l], out_dtype=F32, name=f"f_dn{l}")
        dx, dxb, G[f"f_norm{l}"] = _rms_bwd(xin, W["f_norm"][l:l + 1], dn, dx, f"f_norm_bwd{l}")
        return dx, dxb

    a_par = (W["a_conv_w"], W["a_conv_b"], W["a_gate_x_w"], W["a_gate_x_b"], W["a_gate_a_w"], W["a_gate_a_b"],
             W["a_lambda"])
    b_par = (W["b_group_w"], W["b_group_b"], W["b_scale"])
    n0 = _rms_fwd(x, W["ab_norm"], "ab_norm_fwd")
    zp = _mm_nn(n0, W["ab_w_in"], out_dtype=BF16, name="ab_in")
    yab, h_a = _a_fwd(zp, *a_par)
    yab = _b_fwd(zp, yab, *b_par)
    x1 = _mm_nn(yab, W["ab_w_out"], out_dtype=F32, name="ab_out", add=x)
    x2, s_xa0 = xattn_fwd(x1, 0)
    x3, s_f0 = ffn_fwd(x2, 0)
    n3 = _rms_fwd(x3, W["c_norm"], "c_norm_fwd")
    h1p = _mm_nn(n3, W["c_w_pw1"], out_dtype=BF16, name="c_pw1", bias=W["c_b_pw1"])
    cv = _c_fwd(h1p, W["c_dw_w"], W["c_dw_b"])
    sc = _ln_silu_fwd(cv, W["c_ln_g"], W["c_ln_b"])
    x4 = _mm_nn(sc, W["c_w_pw2"], out_dtype=F32, name="c_pw2", bias=W["c_b_pw2"], add=x3)
    x5, s_xa1 = xattn_fwd(x4, 1)
    x6, s_f1 = ffn_fwd(x5, 1)
    loss, dx, dxb, G["final_norm"] = _loss_head(x6, W["final_norm"], tgt)

    dx, dxb = ffn_bwd(dx, dxb, s_f1, 1)
    dx, dxb = xattn_bwd(dx, dxb, s_xa1, 1)
    dsc = _mm_nt(dxb, W["c_w_pw2"], out_dtype=BF16, name="c_dsc")
    G["c_w_pw2"] = _mm_tn(sc, dxb, out_dtype=BF16, name="c_dwpw2")
    dcv, G["c_ln_g"], G["c_ln_b"], G["c_b_pw2"] = _ln_silu_bwd(dsc, cv, W["c_ln_g"], W["c_ln_b"], dx)
    dh1p, G["c_dw_w"], G["c_dw_b"], G["c_b_pw1"] = _c_bwd(dcv, h1p, W["c_dw_w"])
    G["c_w_pw1"] = _mm_tn(n3, dh1p, out_dtype=BF16, name="c_dwpw1", blocks=_CW_C)
    dn3 = _mm_nt(dh1p, W["c_w_pw1"], out_dtype=F32, name="c_dn")
    dx, dxb, G["c_norm"] = _rms_bwd(x3, W["c_norm"], dn3, dx, "c_norm_bwd")
    dx, dxb = ffn_bwd(dx, dxb, s_f0, 0)
    dx, dxb = xattn_bwd(dx, dxb, s_xa0, 0)
    dyab = _mm_nt(dxb, W["ab_w_out"], out_dtype=BF16, name="ab_dyab")
    G["ab_w_out"] = _mm_tn(yab, dxb, out_dtype=BF16, name="ab_dwout")
    (dzp, G["a_conv_w"], G["a_conv_b"], G["a_gate_x_w"], G["a_gate_x_b"], G["a_gate_a_w"], G["a_gate_a_b"],
     G["a_lambda"]) = _a_bwd(dyab, zp, h_a, *a_par)
    dzp, G["b_group_w"], G["b_group_b"], G["b_scale"] = _b_bwd(dyab, zp, dzp, *b_par)
    G["ab_w_in"] = _mm_tn(n0, dzp, out_dtype=BF16, name="ab_dwin")
    dn0 = _mm_nt(dzp, W["ab_w_in"], out_dtype=F32, name="ab_dn")
    dx, _, G["ab_norm"] = _rms_bwd(x, W["ab_norm"], dn0, dx, "ab_norm_bwd")
    return loss, dx, G


def _my_place():
    x, y, c = lax.axis_index("x"), lax.axis_index("y"), lax.axis_index("c")
    return x, y, c


def _all_gather(shards, name):
    n = len(shards)

    def body(*refs):
        ins, outs = refs[:n], refs[n:2 * n]
        send_sems, recv_sems, local_sems = refs[2 * n:]
        x, y, c = _my_place()
        me, sibling = (x, y, c), (x, y, 1 - c)
        chips = [(1 - x, y), (x, 1 - y), (1 - x, 1 - y)]

        def slab(a, place):
            px, py, pc = place
            return outs[a].at[4 * px + 2 * py + pc]

        def copy(a, k, block, to, src=None):
            return pltpu.make_async_remote_copy(
                src_ref=slab(a, block) if src is None else src, dst_ref=slab(a, block),
                send_sem=send_sems.at[a, k], recv_sem=recv_sems.at[a, k], device_id=to, device_id_type=MESH)

        mine = [pltpu.make_async_copy(ins[a], slab(a, me), local_sems.at[a]) for a in range(n)]
        for cp in mine:
            cp.start()
        first = []
        for j, chip in enumerate(chips):
            first += [copy(a, 1 + j, me, (*chip, c), src=ins[a]) for a in range(n)]
        first += [copy(a, 0, me, sibling, src=ins[a]) for a in range(n)]
        for cp in first:
            cp.start()
        passed = []
        for j, chip in enumerate(chips):
            for a in range(n):
                copy(a, 1 + j, (*chip, c), me).wait_recv()
                cp = copy(a, 4 + j, (*chip, c), sibling)
                cp.start()
                passed.append(cp)
        for a in range(n):
            copy(a, 0, sibling, me).wait_recv()
        for j, chip in enumerate(chips):
            for a in range(n):
                copy(a, 4 + j, (*chip, 1 - c), me).wait_recv()
        for cp in first + passed:
            cp.wait_send()
        for cp in mine:
            cp.wait()

    any_spec = pl.BlockSpec(memory_space=pl.ANY)
    return pl.pallas_call(
        body, out_shape=[SDS((N_DEV,) + s.shape, s.dtype) for s in shards], in_specs=[any_spec] * n,
        out_specs=[any_spec] * n,
        scratch_shapes=[pltpu.SemaphoreType.DMA((n, 7)), pltpu.SemaphoreType.DMA((n, 7)), pltpu.SemaphoreType.DMA((n,))],
        name=name,
    )(*shards)


def _exchange(groups, name):
    flat = [(gi, li, a) for gi, grp in enumerate(groups) for li, a in enumerate(grp)]
    n, ng = len(flat), len(groups)

    def body(*refs):
        ins, outs = refs[:n], refs[n:n + ng]
        send_sems, recv_sems, local_sems = refs[n + ng:]
        x, y, c = _my_place()
        me = 4 * x + 2 * y + c
        peers = []
        for k in range(1, N_DEV):
            px = 1 - x if (k >> 2) & 1 else x
            py = 1 - y if (k >> 1) & 1 else y
            pc = 1 - c if k & 1 else c
            peers.append(((px, py, pc), 4 * px + 2 * py + pc))

        mine = [pltpu.make_async_copy(ins[i].at[me], outs[gi].at[me, li], local_sems.at[i])
                for i, (gi, li, _) in enumerate(flat)]
        for cp in mine:
            cp.start()
        sent = []
        for k, (peer, pidx) in enumerate(peers):
            for i, (gi, li, _) in enumerate(flat):
                cp = pltpu.make_async_remote_copy(
                    src_ref=ins[i].at[pidx], dst_ref=outs[gi].at[me, li], send_sem=send_sems.at[i, k],
                    recv_sem=recv_sems.at[i, k], device_id=peer, device_id_type=MESH)
                cp.start()
                sent.append(cp)
        for k, (peer, pidx) in enumerate(peers):
            for i, (gi, li, _) in enumerate(flat):
                pltpu.make_async_remote_copy(
                    src_ref=ins[i].at[pidx], dst_ref=outs[gi].at[pidx, li], send_sem=send_sems.at[i, k],
                    recv_sem=recv_sems.at[i, k], device_id=peer, device_id_type=MESH).wait_recv()
        for cp in sent:
            cp.wait_send()
        for cp in mine:
            cp.wait()

    any_spec = pl.BlockSpec(memory_space=pl.ANY)
    out_shape = [SDS((N_DEV, len(grp)) + grp[0].shape[1:], grp[0].dtype) for grp in groups]
    return pl.pallas_call(
        body, out_shape=out_shape, in_specs=[any_spec] * n, out_specs=[any_spec] * ng,
        scratch_shapes=[pltpu.SemaphoreType.DMA((n, 7)), pltpu.SemaphoreType.DMA((n, 7)), pltpu.SemaphoreType.DMA((n,))],
        name=name,
    )(*[a for _, _, a in flat])


def _adamw_math(w, g, m, v):
    m = ADAM_B1 * m + (1.0 - ADAM_B1) * g
    v = ADAM_B2 * v + (1.0 - ADAM_B2) * (g * g)
    m_hat = m / (1.0 - ADAM_B1 ** ADAM_STEP)
    v_hat = v / (1.0 - ADAM_B2 ** ADAM_STEP)
    delta = -ADAM_LR * (m_hat / (jnp.sqrt(v_hat) + ADAM_EPS) + ADAM_WD * w)
    return delta, m, v


def _row_tile(r, c, itemsize_rows):
    cap = max(SUB, (itemsize_rows // (4 * c)) // SUB * SUB)
    if r <= cap:
        return r
    best = None
    for t in range(SUB, cap + 1, SUB):
        if r % t == 0:
            best = t
    return best if best is not None else r


def _sum_adamw(landing, w, m, v, name):
    _, r, c = landing.shape
    tr = _row_tile(r, c, 1 << 20)

    def body(l_ref, w_ref, m_ref, v_ref, g_ref, d_ref, mo_ref, vo_ref):
        g = l_ref[0].astype(F32)
        for s in range(1, N_DEV):
            g = g + l_ref[s].astype(F32)
        g_ref[...] = g
        d_ref[...], mo_ref[...], vo_ref[...] = _adamw_math(w_ref[...], g, m_ref[...], v_ref[...])

    blk = pl.BlockSpec((tr, c), lambda i: (i, 0))
    return pl.pallas_call(
        body, out_shape=[SDS((r, c), F32)] * 4, grid=(r // tr,),
        in_specs=[pl.BlockSpec((N_DEV, tr, c), lambda i: (0, i, 0)), blk, blk, blk], out_specs=[blk] * 4, name=name,
        compiler_params=_cp(1),
    )(landing, w, m, v)


def _sum8(landing, name):
    _, r, c = landing.shape

    def body(l_ref, g_ref):
        g = l_ref[0]
        for s in range(1, N_DEV):
            g = g + l_ref[s]
        g_ref[...] = g

    return pl.pallas_call(body, out_shape=SDS((r, c), F32), name=name, compiler_params=_cp(0))(landing)


def _adamw(g, w, m, v, name):
    r, c = g.shape
    tr = _row_tile(r, c, 1 << 20)

    def body(g_ref, w_ref, m_ref, v_ref, d_ref, mo_ref, vo_ref):
        d_ref[...], mo_ref[...], vo_ref[...] = _adamw_math(w_ref[...], g_ref[...], m_ref[...], v_ref[...])

    blk = pl.BlockSpec((tr, c), lambda i: (i, 0))
    return pl.pallas_call(body, out_shape=[SDS((r, c), F32)] * 3, grid=(r // tr,), in_specs=[blk] * 4,
                          out_specs=[blk] * 3, name=name, compiler_params=_cp(1))(g, w, m, v)


_BIG = {
    "ab_w_in": (1, D, 320), "ab_w_out": (1, 192, D), "c_w_pw1": (1, D, 256), "c_w_pw2": (1, 128, D),
    "xa_wq": (2, 128, D), "xa_wk": (2, 128, D), "xa_wv": (2, 128, D), "xa_wo": (2, 128, D),
    "f_w_up": (2, D, 768), "f_w_down": (2, 384, D),
}
_SMALL_SHARDED = {
    "a_conv_w": (1, 4, 128), "c_norm": (1, 128), "c_b_pw1": (1, 256), "c_dw_w": (1, 31, 128), "c_dw_b": (1, 128),
    "c_ln_g": (1, 128), "c_ln_b": (1, 128), "c_b_pw2": (1, 128), "f_dw_w": (2, 3, 384),
}
_REPL = {
    "ab_norm": (1, D), "a_conv_b": (1, D), "a_gate_x_w": (1, 8, 128, 128), "a_gate_x_b": (1, D),
    "a_gate_a_w": (1, 8, 128, 128), "a_gate_a_b": (1, D), "a_lambda": (1, D), "b_group_w": (1, 4, 128, 128),
    "b_group_b": (1, 512), "b_scale": (1, 512), "xa_norm": (2, D), "xa_mem_norm": (2, D), "f_norm": (2, D),
    "f_dw_b": (2, D_FF), "final_norm": (D,),
}


def _size(shape):
    n = 1
    for s in shape:
        n *= s
    return n


_N_SS = sum(_size(s) for s in _SMALL_SHARDED.values())
_N_REPL = sum(_size(s) for s in _REPL.values())
_REPL_ROWS = -(-_N_REPL // (N_DEV * SUB * LANE)) * SUB
_SS_ROWS = _N_SS // LANE
_SMALL_ROWS = -(-(_REPL_ROWS + _SS_ROWS) // SUB) * SUB


def _pack(parts, rows):
    flat = jnp.concatenate([p.reshape(-1).astype(F32) for p in parts])
    return jnp.pad(flat, (0, rows * LANE - flat.shape[0])).reshape(rows, LANE)


def _unpack(buf, table):
    flat, out, off = buf.reshape(-1), {}, 0
    for name, shape in table.items():
        n = _size(shape)
        out[name] = flat[off:off + n].reshape(shape)
        off += n
    return out


def _w_in_to_tiles(w):
    K = w.shape[0]
    gr = jnp.stack([w[:, :D].reshape(K, 8, HD_A), w[:, D:2 * D].reshape(K, 8, HD_A)], axis=2)
    return jnp.concatenate([gr.reshape(K, 2 * D), w[:, 2 * D:]], axis=1)


def _w_in_from_tiles(w):
    K = w.shape[0]
    gr = w[:, :2 * D].reshape(K, 8, 2, HD_A)
    return jnp.concatenate([gr[:, :, 0].reshape(K, D), gr[:, :, 1].reshape(K, D), w[:, 2 * D:]], axis=1)


def _pair_blocks(v, bw):
    lead, n = v.shape[:-1], v.shape[-1]
    return jnp.swapaxes(v.reshape(lead + (2, n // (2 * bw), bw)), -3, -2).reshape(lead + (n,))


def _unpair_blocks(v, bw):
    lead, n = v.shape[:-1], v.shape[-1]
    return jnp.swapaxes(v.reshape(lead + (n // (2 * bw), 2, bw)), -3, -2).reshape(lead + (n,))


def _gather_weights(P):
    big_names, big_shards = [], []
    for name, (layers, r, c) in _BIG.items():
        for l in range(layers):
            big_names.append((name, l))
            big_shards.append(P[name][l].astype(BF16))
    small = _pack([P[n] for n in _SMALL_SHARDED], _SS_ROWS + 4)
    gathered = _all_gather(big_shards + [small], "gather_weights")
    big = {}
    for (name, l), g in zip(big_names, gathered[:-1]):
        big.setdefault(name, []).append(g)
    W = {n: P[n] for n in _REPL}
    W["final_norm"] = P["final_norm"].reshape(1, D)
    W["a_gate_x_w"], W["a_gate_a_w"], W["b_group_w"] = P["a_gate_x_w"][0], P["a_gate_a_w"][0], P["b_group_w"][0]
    w_in = jnp.swapaxes(big["ab_w_in"][0], 0, 1).reshape(D, N_DEV * 320)
    W["ab_w_in"] = _w_in_to_tiles(w_in)
    W["c_w_pw1"] = big["c_w_pw1"][0]
    W["f_w_up"] = big["f_w_up"]
    W["ab_w_out"] = big["ab_w_out"][0].reshape(N_DEV * 192, D)
    W["c_w_pw2"] = big["c_w_pw2"][0].reshape(D, D)
    for n in ("xa_wq", "xa_wk", "xa_wv", "xa_wo"):
        W[n] = [g.reshape(D, D) for g in big[n]]
    W["f_w_down"] = [g.reshape(D_FF, D) for g in big["f_w_down"]]
    sm = gathered[-1].reshape(N_DEV, -1)
    off = 0
    for name, shape in _SMALL_SHARDED.items():
        n = _size(shape)
        blocks = sm[:, off:off + n].reshape((N_DEV,) + shape)
        off += n
        full = jnp.moveaxis(blocks, 0, -2)
        W[name] = full.reshape(shape[:-1] + (N_DEV * shape[-1],))
    W["a_conv_w"], W["c_dw_w"] = W["a_conv_w"][0], W["c_dw_w"][0]
    W["c_b_pw1"] = _pair_blocks(W["c_b_pw1"], _CW_C)
    return W


def _to_dest_major(g, shape):
    full = g.reshape(shape[:-1] + (N_DEV, shape[-1]))
    return jnp.moveaxis(full, -2, 0).reshape(N_DEV, -1)


def kernel(x, mem, ab_norm, ab_w_in, a_conv_w, a_conv_b, a_gate_x_w, a_gate_x_b, a_gate_a_w, a_gate_a_b, a_lambda, b_group_w, b_group_b, b_scale, ab_w_out, c_norm, c_w_pw1, c_b_pw1, c_dw_w, c_dw_b, c_ln_g, c_ln_b, c_w_pw2, c_b_pw2, xa_norm, xa_mem_norm, xa_wq, xa_wk, xa_wv, xa_wo, f_norm, f_w_up, f_dw_w, f_dw_b, f_w_down, final_norm, loss_target, m_ab_norm, m_ab_w_in, m_a_conv_w, m_a_conv_b, m_a_gate_x_w, m_a_gate_x_b, m_a_gate_a_w, m_a_gate_a_b, m_a_lambda, m_b_group_w, m_b_group_b, m_b_scale, m_ab_w_out, m_c_norm, m_c_w_pw1, m_c_b_pw1, m_c_dw_w, m_c_dw_b, m_c_ln_g, m_c_ln_b, m_c_w_pw2, m_c_b_pw2, m_xa_norm, m_xa_mem_norm, m_xa_wq, m_xa_wk, m_xa_wv, m_xa_wo, m_f_norm, m_f_w_up, m_f_dw_w, m_f_dw_b, m_f_w_down, m_final_norm, v_ab_norm, v_ab_w_in, v_a_conv_w, v_a_conv_b, v_a_gate_x_w, v_a_gate_x_b, v_a_gate_a_w, v_a_gate_a_b, v_a_lambda, v_b_group_w, v_b_group_b, v_b_scale, v_ab_w_out, v_c_norm, v_c_w_pw1, v_c_b_pw1, v_c_dw_w, v_c_dw_b, v_c_ln_g, v_c_ln_b, v_c_w_pw2, v_c_b_pw2, v_xa_norm, v_xa_mem_norm, v_xa_wq, v_xa_wk, v_xa_wv, v_xa_wo, v_f_norm, v_f_w_up, v_f_dw_w, v_f_dw_b, v_f_w_down, v_final_norm):
    args = dict(locals())
    P = {n: args[n] for n in _NAMES}
    M = {n: args["m_" + n] for n in _NAMES}
    V = {n: args["v_" + n] for n in _NAMES}

    W = _gather_weights(P)
    loss, grad_x, G = _local_step(x[0], mem[0], loss_target[0], W)
    loss = lax.psum(loss[0, 0], ("x", "y", "c"))

    groups, group_names = [], []
    for name, (layers, r, c) in _BIG.items():
        if name == "ab_w_in":
            nat = _w_in_from_tiles(G["ab_w_in"])
            grp = [jnp.swapaxes(nat.reshape(D, N_DEV, 320), 0, 1)]
        elif name in ("c_w_pw1",):
            grp = [G[name]]
        elif name == "f_w_up":
            grp = [G[f"f_w_up{l}"] for l in range(layers)]
        elif layers == 1:
            grp = [G[name].reshape(N_DEV, r, c)]
        else:
            grp = [G[f"{name}{l}"].reshape(N_DEV, r, c) for l in range(layers)]
        groups.append(grp)
        group_names.append(name)
    Gs = dict(G)
    Gs["c_b_pw1"] = _unpair_blocks(G["c_b_pw1"], _CW_C)
    Gs["f_dw_w"] = jnp.stack([G["f_dw_w0"], G["f_dw_w1"]])
    Gs["a_conv_w"], Gs["c_dw_w"] = G["a_conv_w"][None], G["c_dw_w"][None]
    for n in ("xa_norm", "xa_mem_norm", "f_norm", "f_dw_b"):
        Gs[n] = jnp.concatenate([G[f"{n}0"], G[f"{n}1"]], axis=0)
    for n in ("a_gate_x_w", "a_gate_a_w", "b_group_w"):
        Gs[n] = G[n][None]
    repl_flat = jnp.concatenate([Gs[n].reshape(-1) for n in _REPL])
    repl_rows = jnp.pad(repl_flat, (0, N_DEV * _REPL_ROWS * LANE - _N_REPL)).reshape(N_DEV, _REPL_ROWS, LANE)
    ss_rows = jnp.concatenate([_to_dest_major(Gs[n], s) for n, s in _SMALL_SHARDED.items()], axis=1)
    ss_rows = ss_rows.reshape(N_DEV, _SS_ROWS, LANE)
    small_pack = jnp.concatenate(
        [repl_rows, ss_rows, jnp.zeros((N_DEV, _SMALL_ROWS - _REPL_ROWS - _SS_ROWS, LANE), F32)], axis=1)
    landed = _exchange(groups + [[small_pack]], "exchange_grads")

    out_g, out_d, out_m, out_v = {}, {}, {}, {}
    for name, land in zip(group_names, landed[:-1]):
        layers, r, c = _BIG[name]
        shape = P[name].shape
        res = _sum_adamw(land.reshape(N_DEV, layers * r, c), P[name].reshape(layers * r, c),
                         M[name].reshape(layers * r, c), V[name].reshape(layers * r, c), f"adamw_{name}")
        out_g[name], out_d[name], out_m[name], out_v[name] = [t.reshape(shape) for t in res]

    small_sum = _sum8(landed[-1].reshape(N_DEV, _SMALL_ROWS, LANE), "sum_small")
    (repl_all,) = _all_gather([small_sum[:_REPL_ROWS]], "gather_small_grads")
    g_repl = _unpack(repl_all, _REPL)
    g_ss = _unpack(small_sum[_REPL_ROWS:_REPL_ROWS + _SS_ROWS], _SMALL_SHARDED)
    table = dict(_REPL)
    table.update(_SMALL_SHARDED)
    rows = -(-(_N_REPL + _N_SS) // (256 * LANE)) * 256
    g_small = dict(g_repl)
    g_small.update(g_ss)
    packs = [_pack([src[n] for n in table], rows) for src in (g_small, P, M, V)]
    res = _adamw(*packs, "adamw_small")
    for out, buf in zip((out_d, out_m, out_v), res):
        out.update(_unpack(buf, table))
    out_g.update(g_small)

    return (loss, grad_x[None], *[out_g[n] for n in _NAMES], *[out_d[n] for n in _NAMES],
            *[out_m[n] for n in _NAMES], *[out_v[n] for n in _NAMES])


_NAMES = ("ab_norm", "ab_w_in", "a_conv_w", "a_conv_b", "a_gate_x_w", "a_gate_x_b", "a_gate_a_w", "a_gate_a_b",
          "a_lambda", "b_group_w", "b_group_b", "b_scale", "ab_w_out", "c_norm", "c_w_pw1", "c_b_pw1", "c_dw_w",
          "c_dw_b", "c_ln_g", "c_ln_b", "c_w_pw2", "c_b_pw2", "xa_norm", "xa_mem_norm", "xa_wq", "xa_wk", "xa_wv",
          "xa_wo", "f_norm", "f_w_up", "f_dw_w", "f_dw_b", "f_w_down", "final_norm")
```

```python
import functools

import jax
import jax.numpy as jnp
from jax import lax
from jax.experimental import pallas as pl
from jax.experimental.pallas import tpu as pltpu

F32, BF16 = jnp.float32, jnp.bfloat16
SDS = jax.ShapeDtypeStruct
MESH = pl.DeviceIdType.MESH

N_DEV = 8
D = 1024
N_MEM = 256
XA_HEADS, XA_HD = 4, 256
HD_A = 128
CONV_A, CONV_C, CONV_F = 4, 31, 3
C_RG = 8.0
POOL_WINDOWS = (2, 4, 8, 16)
D_FF = 3 * D
EPS = 1e-6
ADAM_LR, ADAM_B1, ADAM_B2, ADAM_EPS, ADAM_WD, ADAM_STEP = 0.001, 0.9, 0.999, 1e-08, 0.01, 10

LANE = 128
SUB = 8
VMEM_LIMIT = 56 * 1024 * 1024
R_SEQ = 256
TM_ROW = 512


def _cp(n_axes):
    return pltpu.CompilerParams(dimension_semantics=("arbitrary",) * n_axes, vmem_limit_bytes=VMEM_LIMIT)


def _tile(n, pref):
    if n <= pref:
        return n
    best = None
    for t in range(LANE, pref + 1, LANE):
        if n % t == 0:
            best = t
    assert best is not None, (n, pref)
    return best


def _perm2(n):
    return (n % 2) * 4 + n // 2


_NN = (((1,), (0,)), ((), ()))
_NT = (((1,), (1,)), ((), ()))
_TN = (((0,), (0,)), ((), ()))


def _mm_call(name, grid, a, b, a_spec, b_spec, o_spec, out_shape, dims, acc_shape, extras=()):
    nk = grid[2]
    n_ex = len(extras)

    def finish(r, ex_refs, o_ref):
        for e in ex_refs:
            r = r + e[...]
        o_ref[...] = r.astype(o_ref.dtype)

    def body_one(a_ref, b_ref, *rest):
        finish(lax.dot_general(a_ref[...], b_ref[...], dims, preferred_element_type=F32), rest[:n_ex], rest[n_ex])

    def body_acc(a_ref, b_ref, *rest):
        ex_refs, o_ref, acc = rest[:n_ex], rest[n_ex], rest[n_ex + 1]
        k = pl.program_id(2)

        @pl.when(k == 0)
        def _():
            acc[...] = jnp.zeros_like(acc)

        acc[...] += lax.dot_general(a_ref[...], b_ref[...], dims, preferred_element_type=F32)

        @pl.when(k == nk - 1)
        def _():
            finish(acc[...], ex_refs, o_ref)

    return pl.pallas_call(
        body_one if nk == 1 else body_acc, out_shape=out_shape, grid=grid,
        in_specs=[a_spec, b_spec] + [s for _, s in extras], out_specs=o_spec,
        scratch_shapes=[] if nk == 1 else [pltpu.VMEM(acc_shape, F32)], name=name, compiler_params=_cp(3),
    )(a, b, *[e for e, _ in extras])


_K_WHOLE = 3072


def _mm_nn(a, b, *, out_dtype, name, bias=None, add=None, old=False):
    M, K = a.shape
    if old:
        tm, tk = _tile(M, 1024), _tile(K, 512)
    else:
        tk = K if K <= _K_WHOLE else _tile(K, 1024)
        tm = _tile(M, 1024 if K <= 1024 else 512)
    if b.ndim == 3:
        nb, _, bw = b.shape
        N, tn, nn = nb * bw, bw, nb
        b_spec = pl.BlockSpec((None, tk, bw), lambda m, n, k: (_perm2(n), k, 0))
    else:
        N = b.shape[1]
        tn = _tile(N, 1024)
        nn = N // tn
        b_spec = pl.BlockSpec((tk, tn), lambda m, n, k: (k, n))
    extras = []
    if bias is not None:
        extras.append((bias, pl.BlockSpec((1, tn), lambda m, n, k: (0, n))))
    if add is not None:
        extras.append((add, pl.BlockSpec((tm, tn), lambda m, n, k: (m, n))))
    return _mm_call(name, (M // tm, nn, K // tk), a, b, pl.BlockSpec((tm, tk), lambda m, n, k: (m, k)), b_spec,
                    pl.BlockSpec((tm, tn), lambda m, n, k: (m, n)), SDS((M, N), out_dtype), _NN, (tm, tn), extras)


def _mm_nt(a, b, *, out_dtype, name, add=None, old=False):
    M, N = a.shape
    if b.ndim == 3:
        nb, Ko, bw = b.shape
        tm = _tile(M, 1024)
        tn, tk, nk = _tile(Ko, 1024), bw, nb
        b_spec = pl.BlockSpec((None, tn, bw), lambda m, n, k: (_perm2(k), n, 0))
    else:
        Ko = b.shape[0]
        if old:
            tm, tk = _tile(M, 1024), _tile(N, 512)
        else:
            tk = N if N <= _K_WHOLE else _tile(N, 1024)
            tm = _tile(M, 1024 if N <= 1024 else 512)
        tn = _tile(Ko, 1024)
        nk = N // tk
        b_spec = pl.BlockSpec((tn, tk), lambda m, n, k: (n, k))
    extras = []
    if add is not None:
        extras.append((add, pl.BlockSpec((tm, tn), lambda m, n, k: (m, n))))
    return _mm_call(name, (M // tm, Ko // tn, nk), a, b, pl.BlockSpec((tm, tk), lambda m, n, k: (m, k)), b_spec,
                    pl.BlockSpec((tm, tn), lambda m, n, k: (m, n)), SDS((M, Ko), out_dtype), _NT, (tm, tn), extras)


def _mm_tn(a, b, *, out_dtype, name, blocks=None, old=False):
    S, Ka = a.shape
    Nb = b.shape[1]
    tm, tk = _tile(Ka, 1024), _tile(S, 512 if old else 2048)
    if blocks is not None:
        bw = blocks
        tn, nn = bw, Nb // bw
        o_spec = pl.BlockSpec((None, tm, bw), lambda m, n, k: (_perm2(n), m, 0))
        out_shape = SDS((nn, Ka, bw), out_dtype)
    else:
        tn = _tile(Nb, 1024)
        nn = Nb // tn
        o_spec = pl.BlockSpec((tm, tn), lambda m, n, k: (m, n))
        out_shape = SDS((Ka, Nb), out_dtype)
    return _mm_call(name, (Ka // tm, nn, S // tk), a, b, pl.BlockSpec((tk, tm), lambda m, n, k: (k, m)),
                    pl.BlockSpec((tk, tn), lambda m, n, k: (k, n)), o_spec, out_shape, _TN, (tm, tn))


def _row(tm, c):
    return pl.BlockSpec((tm, c), lambda i: (i, 0))


def _full(shape):
    nd = len(shape)
    return pl.BlockSpec(shape, lambda i: (0,) * nd)


def _rms_fwd(x, g, name):
    S = x.shape[0]
    tm = min(S, TM_ROW)

    def body(x_ref, g_ref, o_ref):
        xf = x_ref[...]
        r = lax.rsqrt(jnp.mean(xf * xf, axis=-1, keepdims=True) + EPS)
        o_ref[...] = ((xf * r) * g_ref[...]).astype(BF16)

    return pl.pallas_call(body, out_shape=SDS((S, D), BF16), grid=(S // tm,), in_specs=[_row(tm, D), _full((1, D))],
                          out_specs=_row(tm, D), name=name, compiler_params=_cp(1))(x, g)


def _rms_bwd(x, g, dn, dres, name):
    S = x.shape[0]
    tm = min(S, TM_ROW)
    want_dx = dres is not None

    def body(x_ref, g_ref, dn_ref, *rest):
        i = pl.program_id(0)
        dg_ref = rest[-1]

        @pl.when(i == 0)
        def _():
            dg_ref[...] = jnp.zeros_like(dg_ref)

        xf = x_ref[...]
        r = lax.rsqrt(jnp.mean(xf * xf, axis=-1, keepdims=True) + EPS)
        y = xf * r
        dn_v = dn_ref[...]
        dg_ref[...] += jnp.sum(dn_v * y, axis=0, keepdims=True)
        if want_dx:
            dres_ref, dx_ref, dxb_ref = rest[0], rest[1], rest[2]
            dy = dn_v * g_ref[...]
            dx = r * (dy - y * jnp.mean(dy * y, axis=-1, keepdims=True)) + dres_ref[...]
            dx_ref[...] = dx
            dxb_ref[...] = dx.astype(BF16)

    ins = [x, g, dn] + ([dres] if want_dx else [])
    in_specs = [_row(tm, D), _full((1, D)), _row(tm, D)] + ([_row(tm, D)] if want_dx else [])
    outs = ([SDS((S, D), F32), SDS((S, D), BF16)] if want_dx else []) + [SDS((1, D), F32)]
    out_specs = ([_row(tm, D), _row(tm, D)] if want_dx else []) + [_full((1, D))]
    return pl.pallas_call(body, out_shape=outs, grid=(S // tm,), in_specs=in_specs, out_specs=out_specs, name=name,
                          compiler_params=_cp(1))(*ins)


def _loss_head(x, g, tgt):
    S = x.shape[0]
    tm = min(S, TM_ROW)

    def body(x_ref, g_ref, t_ref, loss_ref, dx_ref, dxb_ref, dg_ref):
        i = pl.program_id(0)

        @pl.when(i == 0)
        def _():
            loss_ref[...] = jnp.zeros_like(loss_ref)
            dg_ref[...] = jnp.zeros_like(dg_ref)

        xf = x_ref[...]
        r = lax.rsqrt(jnp.mean(xf * xf, axis=-1, keepdims=True) + EPS)
        y = xf * r
        gv = g_ref[...]
        err = y * gv - t_ref[...]
        per_row = jnp.mean(err * err, axis=-1, keepdims=True)
        loss_ref[...] += 0.5 * jnp.sum(per_row, axis=0, keepdims=True)
        dn_v = err * (1.0 / D)
        dg_ref[...] += jnp.sum(dn_v * y, axis=0, keepdims=True)
        dy = dn_v * gv
        dx = r * (dy - y * jnp.mean(dy * y, axis=-1, keepdims=True))
        dx_ref[...] = dx
        dxb_ref[...] = dx.astype(BF16)

    return pl.pallas_call(
        body, out_shape=[SDS((1, 1), F32), SDS((S, D), F32), SDS((S, D), BF16), SDS((1, D), F32)], grid=(S // tm,),
        in_specs=[_row(tm, D), _full((1, D)), _row(tm, D)],
        out_specs=[_full((1, 1)), _row(tm, D), _row(tm, D), _full((1, D))], name="loss_head", compiler_params=_cp(1),
    )(x, g, tgt)


def _softmax_rows(s):
    m = jnp.max(s, axis=-1, keepdims=True)
    e = jnp.exp(s - m)
    return e / jnp.sum(e, axis=-1, keepdims=True)


def _attn_fwd(q, k, v, name):
    S = q.shape[0]
    tm = min(S, TM_ROW)
    scale = XA_HD ** -0.5

    def body(q_ref, k_ref, v_ref, o_ref):
        for h in range(XA_HEADS):
            sl = slice(h * XA_HD, (h + 1) * XA_HD)
            s = lax.dot_general(q_ref[:, sl], k_ref[:, sl], _NT, preferred_element_type=F32) * scale
            p = _softmax_rows(s)
            o_ref[:, sl] = lax.dot_general(p.astype(BF16), v_ref[:, sl], _NN, preferred_element_type=F32).astype(BF16)

    return pl.pallas_call(body, out_shape=SDS((S, D), BF16), grid=(S // tm,),
                          in_specs=[_row(tm, D), _full((N_MEM, D)), _full((N_MEM, D))], out_specs=_row(tm, D),
                          name=name, compiler_params=_cp(1))(q, k, v)


def _attn_bwd(q, k, v, do, name):
    S = q.shape[0]
    tm = min(S, TM_ROW)
    scale = XA_HD ** -0.5

    def body(q_ref, k_ref, v_ref, do_ref, dq_ref, dk_ref, dv_ref):
        i = pl.program_id(0)

        @pl.when(i == 0)
        def _():
            dk_ref[...] = jnp.zeros_like(dk_ref)
            dv_ref[...] = jnp.zeros_like(dv_ref)

        for h in range(XA_HEADS):
            sl = slice(h * XA_HD, (h + 1) * XA_HD)
            qh, kh, vh, doh = q_ref[:, sl], k_ref[:, sl], v_ref[:, sl], do_ref[:, sl]
            s = lax.dot_general(qh, kh, _NT, preferred_element_type=F32) * scale
            p = _softmax_rows(s)
            pb = p.astype(BF16)
            dv_ref[:, sl] += lax.dot_general(pb, doh, _TN, preferred_element_type=F32)
            dp = lax.dot_general(doh, vh, _NT, preferred_element_type=F32)
            ds = (p * (dp - jnp.sum(dp * p, axis=-1, keepdims=True)) * scale).astype(BF16)
            dq_ref[:, sl] = lax.dot_general(ds, kh, _NN, preferred_element_type=F32).astype(BF16)
            dk_ref[:, sl] += lax.dot_general(ds, qh, _TN, preferred_element_type=F32)

    return pl.pallas_call(
        body, out_shape=[SDS((S, D), BF16), SDS((N_MEM, D), F32), SDS((N_MEM, D), F32)], grid=(S // tm,),
        in_specs=[_row(tm, D), _full((N_MEM, D)), _full((N_MEM, D)), _row(tm, D)],
        out_specs=[_row(tm, D), _full((N_MEM, D)), _full((N_MEM, D))], name=name, compiler_params=_cp(1),
    )(q, k, v, do)


def _sigmoid(x):
    return 1.0 / (1.0 + jnp.exp(-x))


def _ln_silu_fwd(cv, g, b):
    S = cv.shape[0]
    tm = min(S, TM_ROW)

    def body(x_ref, g_ref, b_ref, o_ref):
        xf = x_ref[...]
        mu = jnp.mean(xf, axis=-1, keepdims=True)
        xc = xf - mu
        rstd = lax.rsqrt(jnp.mean(xc * xc, axis=-1, keepdims=True) + EPS)
        ln = (xc * rstd) * g_ref[...] + b_ref[...]
        o_ref[...] = (ln * _sigmoid(ln)).astype(BF16)

    return pl.pallas_call(body, out_shape=SDS((S, D), BF16), grid=(S // tm,),
                          in_specs=[_row(tm, D), _full((1, D)), _full((1, D))], out_specs=_row(tm, D),
                          name="ln_silu_fwd", compiler_params=_cp(1))(cv, g, b)


def _ln_silu_bwd(ds, cv, g, b, dx):
    S = cv.shape[0]
    tm = min(S, TM_ROW)

    def body(ds_ref, x_ref, g_ref, b_ref, dx_ref, dcv_ref, dg_ref, db_ref, db2_ref):
        i = pl.program_id(0)

        @pl.when(i == 0)
        def _():
            dg_ref[...] = jnp.zeros_like(dg_ref)
            db_ref[...] = jnp.zeros_like(db_ref)
            db2_ref[...] = jnp.zeros_like(db2_ref)

        xf = x_ref[...]
        mu = jnp.mean(xf, axis=-1, keepdims=True)
        xc = xf - mu
        rstd = lax.rsqrt(jnp.mean(xc * xc, axis=-1, keepdims=True) + EPS)
        xhat = xc * rstd
        gv = g_ref[...]
        ln = xhat * gv + b_ref[...]
        sg = _sigmoid(ln)
        dln = ds_ref[...].astype(F32) * (sg + ln * sg * (1.0 - sg))
        dg_ref[...] += jnp.sum(dln * xhat, axis=0, keepdims=True)
        db_ref[...] += jnp.sum(dln, axis=0, keepdims=True)
        db2_ref[...] += jnp.sum(dx_ref[...], axis=0, keepdims=True)
        dxh = dln * gv
        dcv_ref[...] = rstd * (dxh - jnp.mean(dxh, axis=-1, keepdims=True)
                               - xhat * jnp.mean(dxh * xhat, axis=-1, keepdims=True))

    return pl.pallas_call(
        body, out_shape=[SDS((S, D), F32), SDS((1, D), F32), SDS((1, D), F32), SDS((1, D), F32)], grid=(S // tm,),
        in_specs=[_row(tm, D), _row(tm, D), _full((1, D)), _full((1, D)), _row(tm, D)],
        out_specs=[_row(tm, D), _full((1, D)), _full((1, D)), _full((1, D))], name="ln_silu_bwd",
        compiler_params=_cp(1),
    )(ds, cv, g, b, dx)


_GELU_C, _GELU_K = 0.7978845608028654, 0.044715


def _gelu(x, with_grad=False):
    x2 = x * x
    t = jnp.tanh(_GELU_C * (x + _GELU_K * x * x2))
    gel = 0.5 * x * (1.0 + t)
    if not with_grad:
        return gel
    return gel, 0.5 * (1.0 + t) + 0.5 * x * (1.0 - t * t) * (_GELU_C * (1.0 + 3.0 * _GELU_K * x2))


def _expm1(x):
    poly = x * (1.0 + x * (0.5 + x * (1.0 / 6.0 + x * (1.0 / 24.0 + x * (1.0 / 120.0)))))
    return jnp.where(jnp.abs(x) < 0.05, poly, jnp.exp(x) - 1.0)


def _softplus(x):
    return jnp.maximum(x, 0.0) + jnp.log1p(jnp.exp(-jnp.abs(x)))


_SCAN_UNROLL = 4
_RB = 32
_HB = 16


def _sub_blocks(n_rows, n_lanes, fn):
    def step(idx, c):
        r0 = pl.multiple_of(idx * _RB, _RB)
        for lt in range(n_lanes // LANE):
            fn(r0, lt)
        return c

    lax.fori_loop(0, n_rows // _RB, step, 0)


def _lanes(lt):
    return pl.ds(lt * LANE, LANE)


def _psum8(x):
    parts = [x[i * SUB:(i + 1) * SUB] for i in range(x.shape[0] // SUB)]
    return functools.reduce(lambda p, q: p + q, parts)


def _scan_fwd(a_s, b_s, out_ref, carry_ref, n_groups):
    row = lax.broadcasted_iota(jnp.int32, (SUB, LANE), 0)

    def step(g, carry):
        i = pl.multiple_of(g * SUB, SUB)
        a8, b8 = a_s[pl.ds(i, SUB), :], b_s[pl.ds(i, SUB), :]
        for s in (1, 2, 4):
            a_sh = jnp.where(row >= s, pltpu.roll(a8, s, 0), 1.0)
            b_sh = jnp.where(row >= s, pltpu.roll(b8, s, 0), 0.0)
            b8 = a8 * b_sh + b8
            a8 = a8 * a_sh
        h8 = a8 * carry + b8
        out_ref[pl.ds(i, SUB), :] = h8
        return jnp.broadcast_to(h8[SUB - 1:SUB, :], (SUB, LANE))

    carry_ref[...] = lax.fori_loop(0, n_groups, step, carry_ref[...], unroll=_SCAN_UNROLL)


def _scan_bwd(a_s, b_s, out_ref, carry_ref, n_groups):
    row = lax.broadcasted_iota(jnp.int32, (SUB, LANE), 0)

    def step(gi, carry):
        i = pl.multiple_of((n_groups - 1 - gi) * SUB, SUB)
        a8, b8 = a_s[pl.ds(i, SUB), :], b_s[pl.ds(i, SUB), :]
        for s in (1, 2, 4):
            a_sh = jnp.where(row < SUB - s, pltpu.roll(a8, SUB - s, 0), 1.0)
            b_sh = jnp.where(row < SUB - s, pltpu.roll(b8, SUB - s, 0), 0.0)
            b8 = a8 * b_sh + b8
            a8 = a8 * a_sh
        h8 = a8 * carry + b8
        out_ref[pl.ds(i, SUB), :] = h8
        return jnp.broadcast_to(h8[0:1, :], (SUB, LANE))

    carry_ref[...] = lax.fori_loop(0, n_groups, step, carry_ref[...], unroll=_SCAN_UNROLL)


def _rglru_pre(xr, wgx_ref, bgx_ref, wga_ref, bga_ref, lam_ref):
    xrb = xr.astype(BF16)
    wgx, wga = wgx_ref[0].astype(BF16), wga_ref[0].astype(BF16)
    gx = _sigmoid(lax.dot_general(xrb, wgx, _NN, preferred_element_type=F32) + bgx_ref[...])
    ga = _sigmoid(lax.dot_general(xrb, wga, _NN, preferred_element_type=F32) + bga_ref[...])
    sp = _softplus(-lam_ref[...])
    log_a = -C_RG * ga * sp
    a = jnp.exp(log_a)
    mult = jnp.sqrt(-_expm1(2.0 * log_a))
    return gx, ga, sp, a, mult, xrb, wgx, wga


def _a_specs():
    vec = pl.BlockSpec((1, HD_A), lambda c, j: (0, c))
    mat = pl.BlockSpec((1, HD_A, HD_A), lambda c, j: (c, 0, 0))
    return [pl.BlockSpec((CONV_A, HD_A), lambda c, j: (0, c)), vec, mat, vec, mat, vec, vec]


def _a_fwd(zp, conv_w, conv_b, wgx, bgx, wga, bga, lam):
    S = zp.shape[0]
    R, nt = R_SEQ, D // HD_A
    H = SUB

    def body(z_ref, cw_ref, cb_ref, wgx_ref, bgx_ref, wga_ref, bga_ref, lam_ref, ya_ref, h_ref, ext, a_s, b_s, hc):
        j = pl.program_id(1)

        @pl.when(j == 0)
        def _():
            ext[0:H, :] = jnp.zeros((H, HD_A), F32)
            hc[...] = jnp.zeros_like(hc)

        ext[H:H + R, :] = z_ref[:, HD_A:2 * HD_A].astype(F32)
        xr = cb_ref[...]
        for k in range(CONV_A):
            xr = xr + cw_ref[k:k + 1, :] * ext[pl.ds(H - (CONV_A - 1 - k), R), :]
        gx, _, _, a, mult, _, _, _ = _rglru_pre(xr, wgx_ref, bgx_ref, wga_ref, bga_ref, lam_ref)
        a_s[...] = a
        b_s[...] = mult * (gx * xr)
        _scan_fwd(a_s, b_s, h_ref, hc, R // SUB)
        ya_ref[...] = (_gelu(z_ref[:, 0:HD_A].astype(F32)) * h_ref[...]).astype(BF16)
        ext[0:H, :] = ext[R:R + H, :]

    return pl.pallas_call(
        body, out_shape=[SDS((S, D + D // 2), BF16), SDS((S, D), F32)], grid=(nt, S // R),
        in_specs=[pl.BlockSpec((R, 2 * HD_A), lambda c, j: (j, c))] + _a_specs(),
        out_specs=[pl.BlockSpec((R, HD_A), lambda c, j: (j, c)), pl.BlockSpec((R, HD_A), lambda c, j: (j, c))],
        scratch_shapes=[pltpu.VMEM((H + R, HD_A), F32), pltpu.VMEM((R, HD_A), F32), pltpu.VMEM((R, HD_A), F32),
                        pltpu.VMEM((SUB, HD_A), F32)],
        name="rglru_fwd", compiler_params=_cp(2),
    )(zp, conv_w, conv_b, wgx, bgx, wga, bga, lam)


def _a_bwd(dyab, zp, h, conv_w, conv_b, wgx, bgx, wga, bga, lam):
    S = zp.shape[0]
    R, nt, nch = R_SEQ, D // HD_A, S // R_SEQ
    H = SUB

    def rows(c, j):
        return (nch - 1 - j, c)

    def halo(c, j):
        return (jnp.maximum((nch - 1 - j) * (R // H) - 1, 0), c)

    def halo_z(c, j):
        return (jnp.maximum((nch - 1 - j) * (R // _HB) - 1, 0), c)

    def body(dy_ref, z_ref, zh_ref, h_ref, hh_ref, cw_ref, cb_ref, wgx_ref, bgx_ref, wga_ref, bga_ref, lam_ref,
             dz_ref, dcw_ref, dcb_ref, dwgx_ref, dbgx_ref, dwga_ref, dbga_ref, dlam_ref,
             ext_z, ext_h, ext_mu, ext_d, a_s, b_s, muc):
        j = pl.program_id(1)
        first_chunk = (nch - 1 - j) == 0

        @pl.when(j == 0)
        def _():
            ext_mu[R:R + H, :] = jnp.zeros((H, HD_A), F32)
            ext_d[R:R + H, :] = jnp.zeros((H, HD_A), F32)
            muc[...] = jnp.zeros_like(muc)
            for r in (dcw_ref, dcb_ref, dwgx_ref, dbgx_ref, dwga_ref, dbga_ref, dlam_ref):
                r[...] = jnp.zeros_like(r)

        zg = z_ref[:, 0:HD_A].astype(F32)
        ext_z[0:H, :] = jnp.where(first_chunk, 0.0, zh_ref[_HB - H:_HB, HD_A:2 * HD_A].astype(F32))
        ext_z[H:H + R, :] = z_ref[:, HD_A:2 * HD_A].astype(F32)
        ext_h[0:H, :] = jnp.where(first_chunk, 0.0, hh_ref[...])
        ext_h[H:H + R, :] = h_ref[...]
        xr = cb_ref[...]
        for k in range(CONV_A):
            xr = xr + cw_ref[k:k + 1, :] * ext_z[pl.ds(H - (CONV_A - 1 - k), R), :]
        gx, ga, sp, a, mult, xrb, wgxb, wgab = _rglru_pre(xr, wgx_ref, bgx_ref, wga_ref, bga_ref, lam_ref)
        gel, dgel = _gelu(zg, with_grad=True)
        dy = dy_ref[...].astype(F32)
        dh = dy * gel
        dz_ref[:, 0:HD_A] = (dy * h_ref[...] * dgel).astype(BF16)
        a_s[...] = a
        b_s[...] = a * dh
        _scan_bwd(a_s, b_s, ext_mu, muc, R // SUB)
        lam_t = dh + ext_mu[pl.ds(1, R), :]
        ext_mu[R:R + H, :] = ext_mu[0:H, :]
        da = lam_t * ext_h[pl.ds(H - 1, R), :]
        gxr = gx * xr
        dlog_a = da * a - (lam_t * gxr) * (a * a) / mult
        dgx = lam_t * mult * xr
        dxr = lam_t * mult * gx
        lam_v = lam_ref[...]
        dlam_ref[...] += jnp.sum(dlog_a * ga, axis=0, keepdims=True) * (C_RG * _sigmoid(-lam_v))
        dpa = (dlog_a * (-C_RG * sp)) * ga * (1.0 - ga)
        dpx = dgx * gx * (1.0 - gx)
        dbga_ref[...] += jnp.sum(dpa, axis=0, keepdims=True)
        dbgx_ref[...] += jnp.sum(dpx, axis=0, keepdims=True)
        dpab, dpxb = dpa.astype(BF16), dpx.astype(BF16)
        dwga_ref[0] += lax.dot_general(xrb, dpab, _TN, preferred_element_type=F32)
        dwgx_ref[0] += lax.dot_general(xrb, dpxb, _TN, preferred_element_type=F32)
        dxr = (dxr + lax.dot_general(dpab, wgab, _NT, preferred_element_type=F32)
               + lax.dot_general(dpxb, wgxb, _NT, preferred_element_type=F32))
        dcb_ref[...] += jnp.sum(dxr, axis=0, keepdims=True)
        ext_d[0:R, :] = dxr
        dzr = jnp.zeros((R, HD_A), F32)
        for k in range(CONV_A):
            sh = CONV_A - 1 - k
            dcw_ref[k:k + 1, :] += jnp.sum(dxr * ext_z[pl.ds(H - sh, R), :], axis=0, keepdims=True)
            dzr = dzr + cw_ref[k:k + 1, :] * ext_d[pl.ds(sh, R), :]
        dz_ref[:, HD_A:2 * HD_A] = dzr.astype(BF16)
        ext_d[R:R + H, :] = ext_d[0:H, :]

    vec_o = pl.BlockSpec((1, HD_A), lambda c, j: (0, c))
    mat_o = pl.BlockSpec((1, HD_A, HD_A), lambda c, j: (c, 0, 0))
    return pl.pallas_call(
        body,
        out_shape=[SDS((S, 2 * D + D // 2), BF16), SDS((CONV_A, D), F32), SDS((1, D), F32), SDS((nt, HD_A, HD_A), F32),
                   SDS((1, D), F32), SDS((nt, HD_A, HD_A), F32), SDS((1, D), F32), SDS((1, D), F32)],
        grid=(nt, nch),
        in_specs=[pl.BlockSpec((R, HD_A), rows), pl.BlockSpec((R, 2 * HD_A), rows),
                  pl.BlockSpec((_HB, 2 * HD_A), halo_z), pl.BlockSpec((R, HD_A), rows),
                  pl.BlockSpec((H, HD_A), halo)] + _a_specs(),
        out_specs=[pl.BlockSpec((R, 2 * HD_A), rows), pl.BlockSpec((CONV_A, HD_A), lambda c, j: (0, c)), vec_o, mat_o,
                   vec_o, mat_o, vec_o, vec_o],
        scratch_shapes=[pltpu.VMEM((H + R, HD_A), F32), pltpu.VMEM((H + R, HD_A), F32), pltpu.VMEM((R + H, HD_A), F32),
                        pltpu.VMEM((R + H, HD_A), F32), pltpu.VMEM((R, HD_A), F32), pltpu.VMEM((R, HD_A), F32),
                        pltpu.VMEM((SUB, HD_A), F32)],
        name="rglru_bwd", compiler_params=_cp(2),
    )(dyab, zp, zp, h, h, conv_w, conv_b, wgx, bgx, wga, bga, lam)


_POOL_H = 16
_POOL_T0 = 2 * D // HD_A
_POOL_Y0 = D // HD_A


def _pool_mean_minus(u, ext, g, t1):
    R = u.shape[0]
    acc, wins = u, []
    for k in range(1, _POOL_H):
        acc = acc + ext[pl.ds(_POOL_H - k, R), :]
        if k + 1 in POOL_WINDOWS:
            wins.append(acc)
    win = jnp.where(g == 0, wins[0], jnp.where(g == 1, wins[1], jnp.where(g == 2, wins[2], wins[3])))
    return win / jnp.minimum(t1, _pool_width(g)) - u


def _pool_width(g):
    return jnp.where(g == 0, 2.0, jnp.where(g == 1, 4.0, jnp.where(g == 2, 8.0, 16.0)))


def _b_fwd(zp, yab, wg, bg, sc):
    S = zp.shape[0]
    R, H = R_SEQ, _POOL_H

    def body(z_ref, wg_ref, bg_ref, sc_ref, yab_in, yb_ref, ext):
        del yab_in
        g, j = pl.program_id(0), pl.program_id(1)

        @pl.when(j == 0)
        def _():
            ext[0:H, :] = jnp.zeros((H, HD_A), F32)

        u = z_ref[...].astype(F32)
        ext[H:H + R, :] = u
        t1 = (j * R + 1 + lax.broadcasted_iota(jnp.int32, (R, HD_A), 0)).astype(F32)
        p = _pool_mean_minus(u, ext, g, t1)
        lin = lax.dot_general(p.astype(BF16), wg_ref[0].astype(BF16), _NN, preferred_element_type=F32) + bg_ref[...]
        yb_ref[...] = (lin * sc_ref[...]).astype(BF16)
        ext[0:H, :] = ext[R:R + H, :]

    vec = pl.BlockSpec((1, HD_A), lambda g, j: (0, g))
    return pl.pallas_call(
        body, out_shape=SDS(yab.shape, yab.dtype), grid=(len(POOL_WINDOWS), S // R),
        in_specs=[pl.BlockSpec((R, HD_A), lambda g, j: (j, _POOL_T0 + g)),
                  pl.BlockSpec((1, HD_A, HD_A), lambda g, j: (g, 0, 0)), vec, vec, pl.BlockSpec(memory_space=pl.ANY)],
        out_specs=pl.BlockSpec((R, HD_A), lambda g, j: (j, _POOL_Y0 + g)),
        scratch_shapes=[pltpu.VMEM((H + R, HD_A), F32)], input_output_aliases={4: 0},
        name="pool_fwd", compiler_params=_cp(2),
    )(zp, wg, bg, sc, yab)


def _b_bwd(dyab, zp, dzp, wg, bg, sc):
    S = zp.shape[0]
    R, H, nch, ng = R_SEQ, _POOL_H, S // R_SEQ, len(POOL_WINDOWS)

    def body(dy_ref, z_ref, zh_ref, wg_ref, bg_ref, sc_ref, dz_in, dz_ref, dwg_ref, dbg_ref, dsc_ref, ext_u, ext_q):
        del dz_in
        g, j = pl.program_id(0), pl.program_id(1)
        jj = nch - 1 - j

        @pl.when(j == 0)
        def _():
            ext_q[R:R + H, :] = jnp.zeros((H, HD_A), F32)
            for r in (dwg_ref, dbg_ref, dsc_ref):
                r[...] = jnp.zeros_like(r)

        u = z_ref[...].astype(F32)
        ext_u[0:H, :] = jnp.where(jj == 0, 0.0, zh_ref[...].astype(F32))
        ext_u[H:H + R, :] = u
        t1 = (jj * R + 1 + lax.broadcasted_iota(jnp.int32, (R, HD_A), 0)).astype(F32)
        pb = _pool_mean_minus(u, ext_u, g, t1).astype(BF16)
        wgb = wg_ref[0].astype(BF16)
        lin = lax.dot_general(pb, wgb, _NN, preferred_element_type=F32) + bg_ref[...]
        dy = dy_ref[...].astype(F32)
        dsc_ref[...] += jnp.sum(dy * lin, axis=0, keepdims=True)
        dlin = dy * sc_ref[...]
        dbg_ref[...] += jnp.sum(dlin, axis=0, keepdims=True)
        dlb = dlin.astype(BF16)
        dwg_ref[0] += lax.dot_general(pb, dlb, _TN, preferred_element_type=F32)
        dp = lax.dot_general(dlb, wgb, _NT, preferred_element_type=F32)
        q = dp / jnp.minimum(t1, _pool_width(g))
        ext_q[0:R, :] = q
        acc, wins = q, []
        for k in range(1, H):
            acc = acc + ext_q[pl.ds(k, R), :]
            if k + 1 in POOL_WINDOWS:
                wins.append(acc)
        win = jnp.where(g == 0, wins[0], jnp.where(g == 1, wins[1], jnp.where(g == 2, wins[2], wins[3])))
        dz_ref[...] = (win - dp).astype(BF16)
        ext_q[R:R + H, :] = ext_q[0:H, :]

    vec = pl.BlockSpec((1, HD_A), lambda g, j: (0, g))
    mat = pl.BlockSpec((1, HD_A, HD_A), lambda g, j: (g, 0, 0))
    return pl.pallas_call(
        body, out_shape=[SDS(dzp.shape, dzp.dtype), SDS((ng, HD_A, HD_A), F32), SDS((1, D // 2), F32),
                         SDS((1, D // 2), F32)],
        grid=(ng, nch),
        in_specs=[pl.BlockSpec((R, HD_A), lambda g, j: (nch - 1 - j, _POOL_Y0 + g)),
                  pl.BlockSpec((R, HD_A), lambda g, j: (nch - 1 - j, _POOL_T0 + g)),
                  pl.BlockSpec((H, HD_A), lambda g, j: (jnp.maximum((nch - 1 - j) * (R // H) - 1, 0), _POOL_T0 + g)),
                  mat, vec, vec, pl.BlockSpec(memory_space=pl.ANY)],
        out_specs=[pl.BlockSpec((R, HD_A), lambda g, j: (nch - 1 - j, _POOL_T0 + g)), mat, vec, vec],
        scratch_shapes=[pltpu.VMEM((H + R, HD_A), F32), pltpu.VMEM((R + H, HD_A), F32)],
        input_output_aliases={6: 0}, name="pool_bwd", compiler_params=_cp(2),
    )(dyab, zp, zp, wg, bg, sc, dzp)


_CW_F = 768


def _f_fwd(hp, w, b, name):
    S = hp.shape[0]
    R, H, cw = R_SEQ, SUB, _CW_F
    nlt = cw // LANE

    def body(h_ref, w_ref, b_ref, o_ref, ext):
        j = pl.program_id(1)

        @pl.when(j == 0)
        def _():
            ext[:, 0:H, :] = jnp.zeros((nlt, H, LANE), F32)

        def stage(r0, lt):
            ext[lt, pl.ds(pl.multiple_of(r0 + H, SUB), _RB), :] = h_ref[pl.ds(r0, _RB), _lanes(lt)].astype(F32)

        def main(r0, lt):
            ls = _lanes(lt)
            gp = b_ref[:, ls]
            for k in range(CONV_F):
                gp = gp + w_ref[k:k + 1, ls] * ext[lt, pl.ds(r0 + (H - (CONV_F - 1 - k)), _RB), :]
            up = h_ref[pl.ds(r0, _RB), _lanes(lt + nlt)].astype(F32)
            o_ref[pl.ds(r0, _RB), ls] = (_gelu(gp) * up).astype(BF16)

        _sub_blocks(R, cw, stage)
        _sub_blocks(R, cw, main)
        ext[:, 0:H, :] = ext[:, R:R + H, :]

    return pl.pallas_call(
        body, out_shape=SDS((S, D_FF), BF16), grid=(D_FF // cw, S // R),
        in_specs=[pl.BlockSpec((R, 2 * cw), lambda c, j: (j, c)), pl.BlockSpec((CONV_F, cw), lambda c, j: (0, c)),
                  pl.BlockSpec((1, cw), lambda c, j: (0, c))],
        out_specs=pl.BlockSpec((R, cw), lambda c, j: (j, c)),
        scratch_shapes=[pltpu.VMEM((nlt, H + R, LANE), F32)], name=name, compiler_params=_cp(2),
    )(hp, w, b)


def _f_bwd(dact, hp, w, b, name):
    S = hp.shape[0]
    R, H, cw, nch = R_SEQ, SUB, _CW_F, S // R_SEQ
    nlt = cw // LANE

    def body(da_ref, h_ref, hh_ref, w_ref, b_ref, dh_ref, dw_ref, db_ref, ext_g, ext_d, acc):
        j = pl.program_id(1)
        jj = nch - 1 - j

        @pl.when(j == 0)
        def _():
            ext_d[:, R:R + H, :] = jnp.zeros((nlt, H, LANE), F32)
            acc[...] = jnp.zeros_like(acc)

        for lt in range(nlt):
            ext_g[lt, 0:H, :] = jnp.where(jj == 0, 0.0, hh_ref[_HB - H:_HB, lt * LANE:(lt + 1) * LANE].astype(F32))

        def stage(r0, lt):
            ext_g[lt, pl.ds(pl.multiple_of(r0 + H, SUB), _RB), :] = h_ref[pl.ds(r0, _RB), _lanes(lt)].astype(F32)

        def first(r0, lt):
            ls, lu, rs = _lanes(lt), _lanes(lt + nlt), pl.ds(r0, _RB)
            taps = [ext_g[lt, pl.ds(r0 + (H - (CONV_F - 1 - k)), _RB), :] for k in range(CONV_F)]
            gp = b_ref[:, ls]
            for k in range(CONV_F):
                gp = gp + w_ref[k:k + 1, ls] * taps[k]
            gel, dgel = _gelu(gp, with_grad=True)
            da = da_ref[rs, ls].astype(F32)
            dh_ref[rs, lu] = (da * gel).astype(BF16)
            dgp = da * h_ref[rs, lu].astype(F32) * dgel
            ext_d[lt, rs, :] = dgp
            acc[CONV_F * SUB:(CONV_F + 1) * SUB, ls] += _psum8(dgp)
            for k in range(CONV_F):
                acc[k * SUB:(k + 1) * SUB, ls] += _psum8(dgp * taps[k])

        def second(r0, lt):
            ls = _lanes(lt)
            dhg = w_ref[CONV_F - 1:CONV_F, ls] * ext_d[lt, pl.ds(r0, _RB), :]
            for k in range(CONV_F - 1):
                dhg = dhg + w_ref[k:k + 1, ls] * ext_d[lt, pl.ds(r0 + (CONV_F - 1 - k), _RB), :]
            dh_ref[pl.ds(r0, _RB), ls] = dhg.astype(BF16)

        _sub_blocks(R, cw, stage)
        _sub_blocks(R, cw, first)
        _sub_blocks(R, cw, second)
        ext_d[:, R:R + H, :] = ext_d[:, 0:H, :]

        @pl.when(j == nch - 1)
        def _():
            for k in range(CONV_F):
                dw_ref[k:k + 1, :] = jnp.sum(acc[k * SUB:(k + 1) * SUB, :], axis=0, keepdims=True)
            db_ref[...] = jnp.sum(acc[CONV_F * SUB:(CONV_F + 1) * SUB, :], axis=0, keepdims=True)

    rows = lambda c, j: (nch - 1 - j, c)
    return pl.pallas_call(
        body, out_shape=[SDS((S, 2 * D_FF), BF16), SDS((CONV_F, D_FF), F32), SDS((1, D_FF), F32)],
        grid=(D_FF // cw, nch),
        in_specs=[pl.BlockSpec((R, cw), rows), pl.BlockSpec((R, 2 * cw), rows),
                  pl.BlockSpec((_HB, 2 * cw), lambda c, j: (jnp.maximum((nch - 1 - j) * (R // _HB) - 1, 0), c)),
                  pl.BlockSpec((CONV_F, cw), lambda c, j: (0, c)), pl.BlockSpec((1, cw), lambda c, j: (0, c))],
        out_specs=[pl.BlockSpec((R, 2 * cw), rows), pl.BlockSpec((CONV_F, cw), lambda c, j: (0, c)),
                   pl.BlockSpec((1, cw), lambda c, j: (0, c))],
        scratch_shapes=[pltpu.VMEM((nlt, H + R, LANE), F32), pltpu.VMEM((nlt, R + H, LANE), F32),
                        pltpu.VMEM(((CONV_F + 1) * SUB, cw), F32)], name=name,
        compiler_params=_cp(2),
    )(dact, hp, hp, w, b)


_CW_C = 256
_H_C = 32


def _c_fwd(h1p, w, b):
    S = h1p.shape[0]
    R, H, cw = R_SEQ, _H_C, _CW_C
    nlt = cw // LANE

    def body(h_ref, w_ref, b_ref, o_ref, ext):
        j = pl.program_id(1)

        @pl.when(j == 0)
        def _():
            ext[:, 0:H, :] = jnp.zeros((nlt, H, LANE), F32)

        def stage(r0, lt):
            rs = pl.ds(r0, _RB)
            gate = h_ref[rs, _lanes(lt + nlt)].astype(F32)
            ext[lt, pl.ds(pl.multiple_of(r0 + H, SUB), _RB), :] = h_ref[rs, _lanes(lt)].astype(F32) * _sigmoid(gate)

        def main(r0, lt):
            ls = _lanes(lt)
            cv = b_ref[:, ls]
            for k in range(CONV_C):
                cv = cv + w_ref[k:k + 1, ls] * ext[lt, pl.ds(r0 + (H - (CONV_C - 1 - k)), _RB), :]
            o_ref[pl.ds(r0, _RB), ls] = cv

        _sub_blocks(R, cw, stage)
        _sub_blocks(R, cw, main)
        ext[:, 0:H, :] = ext[:, R:R + H, :]

    return pl.pallas_call(
        body, out_shape=SDS((S, D), F32), grid=(D // cw, S // R),
        in_specs=[pl.BlockSpec((R, 2 * cw), lambda c, j: (j, c)), pl.BlockSpec((CONV_C, cw), lambda c, j: (0, c)),
                  pl.BlockSpec((1, cw), lambda c, j: (0, c))],
        out_specs=pl.BlockSpec((R, cw), lambda c, j: (j, c)),
        scratch_shapes=[pltpu.VMEM((nlt, H + R, LANE), F32)], name="conf_conv_fwd", compiler_params=_cp(2),
    )(h1p, w, b)


def _c_bwd(dcv, h1p, w):
    S = h1p.shape[0]
    R, H, cw, nch = R_SEQ, _H_C, _CW_C, S // R_SEQ
    nlt = cw // LANE
    a_b, a_val, a_gate = CONV_C * SUB, (CONV_C + 1) * SUB, (CONV_C + 2) * SUB

    def body(dc_ref, h_ref, hh_ref, w_ref, dh_ref, dw_ref, db_ref, db1_ref, ext_u, ext_d, acc):
        j = pl.program_id(1)
        jj = nch - 1 - j

        @pl.when(j == 0)
        def _():
            ext_d[:, R:R + H, :] = jnp.zeros((nlt, H, LANE), F32)
            acc[...] = jnp.zeros_like(acc)

        for lt in range(nlt):
            ext_u[lt, 0:H, :] = jnp.where(
                jj == 0, 0.0, hh_ref[:, lt * LANE:(lt + 1) * LANE].astype(F32)
                * _sigmoid(hh_ref[:, cw + lt * LANE:cw + (lt + 1) * LANE].astype(F32)))

        def stage(r0, lt):
            rs, ls = pl.ds(r0, _RB), _lanes(lt)
            gate = h_ref[rs, _lanes(lt + nlt)].astype(F32)
            ext_u[lt, pl.ds(pl.multiple_of(r0 + H, SUB), _RB), :] = h_ref[rs, ls].astype(F32) * _sigmoid(gate)
            ext_d[lt, rs, :] = dc_ref[rs, ls]

        def first(r0, lt):
            ls = _lanes(lt)
            dc = dc_ref[pl.ds(r0, _RB), ls]
            acc[a_b:a_b + SUB, ls] += _psum8(dc)
            for k in range(CONV_C):
                tap = ext_u[lt, pl.ds(r0 + (H - (CONV_C - 1 - k)), _RB), :]
                acc[k * SUB:(k + 1) * SUB, ls] += _psum8(dc * tap)

        def second(r0, lt):
            rs, ls, lg = pl.ds(r0, _RB), _lanes(lt), _lanes(lt + nlt)
            du = w_ref[CONV_C - 1:CONV_C, ls] * ext_d[lt, rs, :]
            for k in range(CONV_C - 1):
                du = du + w_ref[k:k + 1, ls] * ext_d[lt, pl.ds(r0 + (CONV_C - 1 - k), _RB), :]
            val = h_ref[rs, ls].astype(F32)
            sg = _sigmoid(h_ref[rs, lg].astype(F32))
            dval = du * sg
            dgate = du * val * sg * (1.0 - sg)
            acc[a_val:a_val + SUB, ls] += _psum8(dval)
            acc[a_gate:a_gate + SUB, ls] += _psum8(dgate)
            dh_ref[rs, ls] = dval.astype(BF16)
            dh_ref[rs, lg] = dgate.astype(BF16)

        _sub_blocks(R, cw, stage)
        _sub_blocks(R, cw, first)
        _sub_blocks(R, cw, second)
        ext_d[:, R:R + H, :] = ext_d[:, 0:H, :]

        @pl.when(j == nch - 1)
        def _():
            for k in range(CONV_C):
                dw_ref[k:k + 1, :] = jnp.sum(acc[k * SUB:(k + 1) * SUB, :], axis=0, keepdims=True)
            db_ref[...] = jnp.sum(acc[a_b:a_b + SUB, :], axis=0, keepdims=True)
            db1_ref[:, 0:cw] = jnp.sum(acc[a_val:a_val + SUB, :], axis=0, keepdims=True)
            db1_ref[:, cw:2 * cw] = jnp.sum(acc[a_gate:a_gate + SUB, :], axis=0, keepdims=True)

    rows = lambda c, j: (nch - 1 - j, c)
    return pl.pallas_call(
        body, out_shape=[SDS((S, 2 * D), BF16), SDS((CONV_C, D), F32), SDS((1, D), F32), SDS((1, 2 * D), F32)],
        grid=(D // cw, nch),
        in_specs=[pl.BlockSpec((R, cw), rows), pl.BlockSpec((R, 2 * cw), rows),
                  pl.BlockSpec((H, 2 * cw), lambda c, j: (jnp.maximum((nch - 1 - j) * (R // H) - 1, 0), c)),
                  pl.BlockSpec((CONV_C, cw), lambda c, j: (0, c))],
        out_specs=[pl.BlockSpec((R, 2 * cw), rows), pl.BlockSpec((CONV_C, cw), lambda c, j: (0, c)),
                   pl.BlockSpec((1, cw), lambda c, j: (0, c)), pl.BlockSpec((1, 2 * cw), lambda c, j: (0, c))],
        scratch_shapes=[pltpu.VMEM((nlt, H + R, LANE), F32), pltpu.VMEM((nlt, R + H, LANE), F32),
                        pltpu.VMEM(((CONV_C + 3) * SUB, cw), F32)], name="conf_conv_bwd",
        compiler_params=_cp(2),
    )(dcv, h1p, h1p, w)


def _local_step(x, mem, tgt, W, fetch=None, send=None):
    G = {}
    W = dict(W)

    def arrive(group, after):
        if fetch is not None:
            for key, val in fetch(group, after).items():
                W[key] = {**W.get(key, {}), **val} if isinstance(val, dict) else val

    def gain(g, tok):
        return g if tok is None else g + tok

    def sent(group):
        return None if send is None else send(group, G)

    def xattn_fwd(xin, l):
        n = _rms_fwd(xin, W["xa_norm"][l:l + 1], f"xa_norm_fwd{l}")
        arrive(("xa", l), n)
        q = _mm_nn(n, W["xa_wq"][l], out_dtype=BF16, name=f"xa_q{l}")
        mn = _rms_fwd(mem, W["xa_mem_norm"][l:l + 1], f"xa_memnorm_fwd{l}")
        k = _mm_nn(mn, W["xa_wk"][l], out_dtype=BF16, name=f"xa_k{l}")
        v = _mm_nn(mn, W["xa_wv"][l], out_dtype=BF16, name=f"xa_v{l}")
        o = _attn_fwd(q, k, v, f"xa_attn_fwd{l}")
        xout = _mm_nn(o, W["xa_wo"][l], out_dtype=F32, name=f"xa_o{l}", add=xin)
        return xout, (xin, n, q, mn, k, v, o)

    def xattn_bwd(dx, dxb, saved, l):
        xin, n, q, mn, k, v, o = saved
        do = _mm_nt(dxb, W["xa_wo"][l], out_dtype=BF16, name=f"xa_do{l}")
        G[f"xa_wo{l}"] = _mm_tn(o, dxb, out_dtype=BF16, name=f"xa_dwo{l}")
        dq, dk, dv = _attn_bwd(q, k, v, do, f"xa_attn_bwd{l}")
        dkb, dvb = dk.astype(BF16), dv.astype(BF16)
        G[f"xa_wq{l}"] = _mm_tn(n, dq, out_dtype=BF16, name=f"xa_dwq{l}")
        G[f"xa_wk{l}"] = _mm_tn(mn, dkb, out_dtype=BF16, name=f"xa_dwk{l}")
        G[f"xa_wv{l}"] = _mm_tn(mn, dvb, out_dtype=BF16, name=f"xa_dwv{l}")
        tok = sent(("xa", l))
        dmn = _mm_nt(dkb, W["xa_wk"][l], out_dtype=F32, name=f"xa_dmn_k{l}")
        dmn = _mm_nt(dvb, W["xa_wv"][l], out_dtype=F32, name=f"xa_dmn_v{l}", add=dmn)
        (G[f"xa_mem_norm{l}"],) = _rms_bwd(mem, W["xa_mem_norm"][l:l + 1], dmn, None, f"xa_memnorm_bwd{l}")
        dn = _mm_nt(dq, W["xa_wq"][l], out_dtype=F32, name=f"xa_dn{l}")
        dx, dxb, G[f"xa_norm{l}"] = _rms_bwd(xin, gain(W["xa_norm"][l:l + 1], tok), dn, dx, f"xa_norm_bwd{l}")
        return dx, dxb

    def ffn_fwd(xin, l):
        n = _rms_fwd(xin, W["f_norm"][l:l + 1], f"f_norm_fwd{l}")
        arrive(("f", l), n)
        hp = _mm_nn(n, W["f_w_up"][l], out_dtype=BF16, name=f"f_up{l}")
        act = _f_fwd(hp, W["f_dw_w"][l], W["f_dw_b"][l:l + 1], f"f_conv_fwd{l}")
        xout = _mm_nn(act, W["f_w_down"][l], out_dtype=F32, name=f"f_down{l}", add=xin)
        return xout, (xin, n, hp, act)

    def ffn_bwd(dx, dxb, saved, l):
        xin, n, hp, act = saved
        dact = _mm_nt(dxb, W["f_w_down"][l], out_dtype=BF16, name=f"f_dact{l}")
        G[f"f_w_down{l}"] = _mm_tn(act, dxb, out_dtype=BF16, name=f"f_dwdown{l}")
        dhp, G[f"f_dw_w{l}"], G[f"f_dw_b{l}"] = _f_bwd(dact, hp, W["f_dw_w"][l], W["f_dw_b"][l:l + 1], f"f_conv_bwd{l}")
        G[f"f_w_up{l}"] = _mm_tn(n, dhp, out_dtype=BF16, name=f"f_dwup{l}", blocks=_CW_F)
        tok = sent(("f", l))
        dn = _mm_nt(dhp, W["f_w_up"][l], out_dtype=F32, name=f"f_dn{l}")
        dx, dxb, G[f"f_norm{l}"] = _rms_bwd(xin, gain(W["f_norm"][l:l + 1], tok), dn, dx, f"f_norm_bwd{l}")
        return dx, dxb

    n0 = _rms_fwd(x, W["ab_norm"], "ab_norm_fwd")
    arrive(("ab", 0), n0)
    a_par = (W["a_conv_w"], W["a_conv_b"], W["a_gate_x_w"], W["a_gate_x_b"], W["a_gate_a_w"], W["a_gate_a_b"],
             W["a_lambda"])
    b_par = (W["b_group_w"], W["b_group_b"], W["b_scale"])
    zp = _mm_nn(n0, W["ab_w_in"], out_dtype=BF16, name="ab_in")
    yab, h_a = _a_fwd(zp, *a_par)
    yab = _b_fwd(zp, yab, *b_par)
    x1 = _mm_nn(yab, W["ab_w_out"], out_dtype=F32, name="ab_out", add=x)
    x2, s_xa0 = xattn_fwd(x1, 0)
    x3, s_f0 = ffn_fwd(x2, 0)
    n3 = _rms_fwd(x3, W["c_norm"], "c_norm_fwd")
    arrive(("c", 0), n3)
    h1p = _mm_nn(n3, W["c_w_pw1"], out_dtype=BF16, name="c_pw1", bias=W["c_b_pw1"])
    cv = _c_fwd(h1p, W["c_dw_w"], W["c_dw_b"])
    sc = _ln_silu_fwd(cv, W["c_ln_g"], W["c_ln_b"])
    x4 = _mm_nn(sc, W["c_w_pw2"], out_dtype=F32, name="c_pw2", bias=W["c_b_pw2"], add=x3)
    x5, s_xa1 = xattn_fwd(x4, 1)
    x6, s_f1 = ffn_fwd(x5, 1)
    loss, dx, dxb, G["final_norm"] = _loss_head(x6, W["final_norm"], tgt)

    dx, dxb = ffn_bwd(dx, dxb, s_f1, 1)
    dx, dxb = xattn_bwd(dx, dxb, s_xa1, 1)
    dsc = _mm_nt(dxb, W["c_w_pw2"], out_dtype=BF16, name="c_dsc")
    G["c_w_pw2"] = _mm_tn(sc, dxb, out_dtype=BF16, name="c_dwpw2")
    dcv, G["c_ln_g"], G["c_ln_b"], G["c_b_pw2"] = _ln_silu_bwd(dsc, cv, W["c_ln_g"], W["c_ln_b"], dx)
    dh1p, G["c_dw_w"], G["c_dw_b"], G["c_b_pw1"] = _c_bwd(dcv, h1p, W["c_dw_w"])
    G["c_w_pw1"] = _mm_tn(n3, dh1p, out_dtype=BF16, name="c_dwpw1", blocks=_CW_C)
    tok = sent(("c", 0))
    dn3 = _mm_nt(dh1p, W["c_w_pw1"], out_dtype=F32, name="c_dn")
    dx, dxb, G["c_norm"] = _rms_bwd(x3, gain(W["c_norm"], tok), dn3, dx, "c_norm_bwd")
    dx, dxb = ffn_bwd(dx, dxb, s_f0, 0)
    dx, dxb = xattn_bwd(dx, dxb, s_xa0, 0)
    dyab = _mm_nt(dxb, W["ab_w_out"], out_dtype=BF16, name="ab_dyab")
    G["ab_w_out"] = _mm_tn(yab, dxb, out_dtype=BF16, name="ab_dwout")
    tok = sent(("ab", 1))
    a_par = (a_par[0], gain(a_par[1], tok)) + a_par[2:]
    (dzp, G["a_conv_w"], G["a_conv_b"], G["a_gate_x_w"], G["a_gate_x_b"], G["a_gate_a_w"], G["a_gate_a_b"],
     G["a_lambda"]) = _a_bwd(dyab, zp, h_a, *a_par)
    dzp, G["b_group_w"], G["b_group_b"], G["b_scale"] = _b_bwd(dyab, zp, dzp, *b_par)
    G["ab_w_in"] = _mm_tn(n0, dzp, out_dtype=BF16, name="ab_dwin")
    dn0 = _mm_nt(dzp, W["ab_w_in"], out_dtype=F32, name="ab_dn")
    dx, _, G["ab_norm"] = _rms_bwd(x, W["ab_norm"], dn0, dx, "ab_norm_bwd")
    return loss, dx, G


def _my_place():
    x, y, c = lax.axis_index("x"), lax.axis_index("y"), lax.axis_index("c")
    return x, y, c


def _all_gather(shards, name):
    n = len(shards)

    def body(*refs):
        ins, outs = refs[:n], refs[n:2 * n]
        send_sems, recv_sems, local_sems = refs[2 * n:]
        x, y, c = _my_place()
        me, sibling = (x, y, c), (x, y, 1 - c)
        chips = [(1 - x, y), (x, 1 - y), (1 - x, 1 - y)]

        def slab(a, place):
            px, py, pc = place
            return outs[a].at[4 * px + 2 * py + pc]

        def copy(a, k, block, to, src=None):
            return pltpu.make_async_remote_copy(
                src_ref=slab(a, block) if src is None else src, dst_ref=slab(a, block),
                send_sem=send_sems.at[a, k], recv_sem=recv_sems.at[a, k], device_id=to, device_id_type=MESH)

        mine = [pltpu.make_async_copy(ins[a], slab(a, me), local_sems.at[a]) for a in range(n)]
        for cp in mine:
            cp.start()
        first = []
        for j, chip in enumerate(chips):
            first += [copy(a, 1 + j, me, (*chip, c), src=ins[a]) for a in range(n)]
        first += [copy(a, 0, me, sibling, src=ins[a]) for a in range(n)]
        for cp in first:
            cp.start()
        passed = []
        for j, chip in enumerate(chips):
            for a in range(n):
                copy(a, 1 + j, (*chip, c), me).wait_recv()
                cp = copy(a, 4 + j, (*chip, c), sibling)
                cp.start()
                passed.append(cp)
        for a in range(n):
            copy(a, 0, sibling, me).wait_recv()
        for j, chip in enumerate(chips):
            for a in range(n):
                copy(a, 4 + j, (*chip, 1 - c), me).wait_recv()
        for cp in first + passed:
            cp.wait_send()
        for cp in mine:
            cp.wait()

    any_spec = pl.BlockSpec(memory_space=pl.ANY)
    return pl.pallas_call(
        body, out_shape=[SDS((N_DEV,) + s.shape, s.dtype) for s in shards], in_specs=[any_spec] * n,
        out_specs=[any_spec] * n,
        scratch_shapes=[pltpu.SemaphoreType.DMA((n, 7)), pltpu.SemaphoreType.DMA((n, 7)), pltpu.SemaphoreType.DMA((n,))],
        name=name,
    )(*shards)


_HBM = pl.BlockSpec(memory_space=pltpu.HBM)
_SEM = pl.BlockSpec(memory_space=pltpu.SEMAPHORE)
_EFFECT = pltpu.SideEffectType.DATAFLOW_SIDE_EFFECTING


def _peer_places():
    x, y, c = _my_place()
    peers = []
    for k in range(1, N_DEV):
        px = 1 - x if (k >> 2) & 1 else x
        py = 1 - y if (k >> 1) & 1 else y
        pc = 1 - c if k & 1 else c
        peers.append(((px, py, pc), 4 * px + 2 * py + pc))
    return (x, y, c), 4 * x + 2 * y + c, peers


def _send_start(srcs, per_dest, name):
    n = len(srcs)
    lands = [lax.empty((N_DEV,) + (s.shape[1:] if per_dest else s.shape), s.dtype) for s in srcs]

    def body(*refs):
        src, land = refs[:n], refs[n:2 * n]
        outs = refs[2 * n:]
        send, recv, token = outs[:n], outs[n:2 * n], outs[4 * n]
        _, me, peers = _peer_places()
        for peer, pidx in peers:
            for a in range(n):
                pltpu.make_async_remote_copy(
                    src_ref=src[a].at[pidx] if per_dest else src[a], dst_ref=land[a].at[me], send_sem=send[a],
                    recv_sem=recv[a], device_id=peer, device_id_type=MESH).start()
        token[...] = jnp.zeros_like(token)

    hbm = lambda a: pltpu.HBM(a.shape, a.dtype)
    sem = pltpu.SemaphoreType.DMA(())
    res = pl.pallas_call(
        body, name=name,
        out_shape=tuple([sem] * (2 * n) + [hbm(s) for s in srcs] + [hbm(l) for l in lands]
                        + [SDS((SUB, LANE), F32)]),
        in_specs=[_HBM] * (2 * n),
        out_specs=tuple([_SEM] * (2 * n) + [_HBM] * (2 * n) + [pl.BlockSpec(memory_space=pltpu.VMEM)]),
        input_output_aliases={i: 2 * n + i for i in range(2 * n)},
        compiler_params=pltpu.CompilerParams(has_side_effects=_EFFECT),
    )(*[pltpu.with_memory_space_constraint(s, pltpu.HBM) for s in srcs],
      *[pltpu.with_memory_space_constraint(l, pltpu.HBM) for l in lands])
    return res[:n], res[n:2 * n], res[2 * n:3 * n], res[3 * n:4 * n], res[4 * n]


def _send_wait(send, recv, srcs, lands, after, per_dest, name):
    n = len(srcs)

    def body(*refs):
        src, land = refs[:n], refs[n:2 * n]
        send_s, recv_s = refs[2 * n:3 * n], refs[3 * n:4 * n]
        place, _, _ = _peer_places()
        for a in range(n):
            seven = land[a].at[pl.ds(0, N_DEV - 1)]
            copy = pltpu.make_async_remote_copy(
                src_ref=src[a].at[pl.ds(0, N_DEV - 1)] if per_dest else seven, dst_ref=seven, send_sem=send_s[a],
                recv_sem=recv_s[a], device_id=place, device_id_type=MESH)
            copy.wait_send()
            copy.wait_recv()

    hbm = lambda a: pltpu.HBM(a.shape, a.dtype)
    res = pl.pallas_call(
        body, name=name, out_shape=tuple([hbm(s) for s in srcs] + [hbm(l) for l in lands]),
        in_specs=[_HBM] * (2 * n) + [_SEM] * (2 * n) + [pl.BlockSpec(memory_space=pl.ANY)],
        out_specs=tuple([_HBM] * (2 * n)), input_output_aliases={i: i for i in range(2 * n)},
        compiler_params=pltpu.CompilerParams(has_side_effects=_EFFECT),
    )(*srcs, *lands, *send, *recv, after)
    return res[:n], res[n:]


def _adamw_math(w, g, m, v):
    m = ADAM_B1 * m + (1.0 - ADAM_B1) * g
    v = ADAM_B2 * v + (1.0 - ADAM_B2) * (g * g)
    m_hat = m / (1.0 - ADAM_B1 ** ADAM_STEP)
    v_hat = v / (1.0 - ADAM_B2 ** ADAM_STEP)
    delta = -ADAM_LR * (m_hat / (jnp.sqrt(v_hat) + ADAM_EPS) + ADAM_WD * w)
    return delta, m, v


def _row_tile(r, c, itemsize_rows):
    cap = max(SUB, (itemsize_rows // (4 * c)) // SUB * SUB)
    if r <= cap:
        return r
    best = None
    for t in range(SUB, cap + 1, SUB):
        if r % t == 0:
            best = t
    return best if best is not None else r


def _sum_adamw(landing, w, m, v, name, layer=0, prev=None):
    _, r, c = landing.shape
    tr = _row_tile(r, c, 1 << 20)
    off = layer * (r // tr)

    def body(l_ref, w_ref, m_ref, v_ref, *rest):
        g_ref, d_ref, mo_ref, vo_ref = rest[-4:]
        g = l_ref[0].astype(F32)
        for s in range(1, N_DEV):
            g = g + l_ref[s].astype(F32)
        g_ref[...] = g
        d_ref[...], mo_ref[...], vo_ref[...] = _adamw_math(w_ref[...], g, m_ref[...], v_ref[...])

    blk = pl.BlockSpec((tr, c), lambda i: (i + off, 0))
    n_prev = 0 if prev is None else 4
    return pl.pallas_call(
        body, out_shape=[SDS(w.shape, F32)] * 4, grid=(r // tr,),
        in_specs=[pl.BlockSpec((N_DEV, tr, c), lambda i: (0, i, 0)), blk, blk, blk]
        + [pl.BlockSpec(memory_space=pl.ANY)] * n_prev,
        out_specs=[blk] * 4, input_output_aliases={4 + i: i for i in range(n_prev)}, name=name,
        compiler_params=_cp(1),
    )(landing, w, m, v, *([] if prev is None else prev))


def _sum8(landing, name):
    _, r, c = landing.shape

    def body(l_ref, g_ref):
        g = l_ref[0]
        for s in range(1, N_DEV):
            g = g + l_ref[s]
        g_ref[...] = g

    return pl.pallas_call(body, out_shape=SDS((r, c), F32), name=name, compiler_params=_cp(0))(landing)


def _adamw(g, w, m, v, name):
    r, c = g.shape
    tr = _row_tile(r, c, 1 << 20)

    def body(g_ref, w_ref, m_ref, v_ref, d_ref, mo_ref, vo_ref):
        d_ref[...], mo_ref[...], vo_ref[...] = _adamw_math(w_ref[...], g_ref[...], m_ref[...], v_ref[...])

    blk = pl.BlockSpec((tr, c), lambda i: (i, 0))
    return pl.pallas_call(body, out_shape=[SDS((r, c), F32)] * 3, grid=(r // tr,), in_specs=[blk] * 4,
                          out_specs=[blk] * 3, name=name, compiler_params=_cp(1))(g, w, m, v)


_BIG = {
    "ab_w_in": (1, D, 320), "ab_w_out": (1, 192, D), "c_w_pw1": (1, D, 256), "c_w_pw2": (1, 128, D),
    "xa_wq": (2, 128, D), "xa_wk": (2, 128, D), "xa_wv": (2, 128, D), "xa_wo": (2, 128, D),
    "f_w_up": (2, D, 768), "f_w_down": (2, 384, D),
}
_SMALL_SHARDED = {
    "a_conv_w": (1, 4, 128), "c_norm": (1, 128), "c_b_pw1": (1, 256), "c_dw_w": (1, 31, 128), "c_dw_b": (1, 128),
    "c_ln_g": (1, 128), "c_ln_b": (1, 128), "c_b_pw2": (1, 128), "f_dw_w": (2, 3, 384),
}
_REPL = {
    "ab_norm": (1, D), "a_conv_b": (1, D), "a_gate_x_w": (1, 8, 128, 128), "a_gate_x_b": (1, D),
    "a_gate_a_w": (1, 8, 128, 128), "a_gate_a_b": (1, D), "a_lambda": (1, D), "b_group_w": (1, 4, 128, 128),
    "b_group_b": (1, 512), "b_scale": (1, 512), "xa_norm": (2, D), "xa_mem_norm": (2, D), "f_norm": (2, D),
    "f_dw_b": (2, D_FF), "final_norm": (D,),
}


def _size(shape):
    n = 1
    for s in shape:
        n *= s
    return n


_N_SS = sum(_size(s) for s in _SMALL_SHARDED.values())
_N_REPL = sum(_size(s) for s in _REPL.values())
_REPL_ROWS = -(-_N_REPL // (N_DEV * SUB * LANE)) * SUB
_SS_ROWS = _N_SS // LANE
_SMALL_ROWS = -(-(_REPL_ROWS + _SS_ROWS) // SUB) * SUB


def _pack(parts, rows):
    flat = jnp.concatenate([p.reshape(-1).astype(F32) for p in parts])
    return jnp.pad(flat, (0, rows * LANE - flat.shape[0])).reshape(rows, LANE)


def _unpack(buf, table):
    flat, out, off = buf.reshape(-1), {}, 0
    for name, shape in table.items():
        n = _size(shape)
        out[name] = flat[off:off + n].reshape(shape)
        off += n
    return out


def _w_in_to_tiles(w):
    K = w.shape[0]
    gr = jnp.stack([w[:, :D].reshape(K, 8, HD_A), w[:, D:2 * D].reshape(K, 8, HD_A)], axis=2)
    return jnp.concatenate([gr.reshape(K, 2 * D), w[:, 2 * D:]], axis=1)


def _w_in_from_tiles(w):
    K = w.shape[0]
    gr = w[:, :2 * D].reshape(K, 8, 2, HD_A)
    return jnp.concatenate([gr[:, :, 0].reshape(K, D), gr[:, :, 1].reshape(K, D), w[:, 2 * D:]], axis=1)


def _pair_blocks(v, bw):
    lead, n = v.shape[:-1], v.shape[-1]
    return jnp.swapaxes(v.reshape(lead + (2, n // (2 * bw), bw)), -3, -2).reshape(lead + (n,))


def _unpair_blocks(v, bw):
    lead, n = v.shape[:-1], v.shape[-1]
    return jnp.swapaxes(v.reshape(lead + (n // (2 * bw), 2, bw)), -3, -2).reshape(lead + (n,))


_GROUPS = {
    ("ab", 0): (("ab_w_in", 0), ("ab_w_out", 0)),
    ("xa", 0): (("xa_wq", 0), ("xa_wk", 0), ("xa_wv", 0), ("xa_wo", 0)),
    ("f", 0): (("f_w_up", 0), ("f_w_down", 0)),
    ("c", 0): (("c_w_pw1", 0), ("c_w_pw2", 0)),
    ("xa", 1): (("xa_wq", 1), ("xa_wk", 1), ("xa_wv", 1), ("xa_wo", 1)),
    ("f", 1): (("f_w_up", 1), ("f_w_down", 1)),
}
_SEND_GROUPS = dict(_GROUPS)
_SEND_GROUPS[("ab", 1)] = (("ab_w_out", 0),)
_SEND_GROUPS[("ab", 0)] = (("ab_w_in", 0),)


def _weight_layout(name, g):
    if name == "ab_w_in":
        return _w_in_to_tiles(jnp.swapaxes(g, 0, 1).reshape(D, N_DEV * 320))
    if name in ("c_w_pw1", "f_w_up"):
        return g
    return g.reshape(N_DEV * g.shape[1], D)


def _grad_blocks(name, l, G):
    _, r, c = _BIG[name]
    if name == "ab_w_in":
        return jnp.swapaxes(_w_in_from_tiles(G[name]).reshape(D, N_DEV, 320), 0, 1)
    if name == "c_w_pw1":
        return G[name]
    if name == "f_w_up":
        return G[f"{name}{l}"]
    return (G[name] if _BIG[name][0] == 1 else G[f"{name}{l}"]).reshape(N_DEV, r, c)


def _small_layouts(sm):
    W = {}
    sm = sm.reshape(N_DEV, -1)
    off = 0
    for name, shape in _SMALL_SHARDED.items():
        n = _size(shape)
        blocks = sm[:, off:off + n].reshape((N_DEV,) + shape)
        off += n
        W[name] = jnp.moveaxis(blocks, 0, -2).reshape(shape[:-1] + (N_DEV * shape[-1],))
    W["a_conv_w"], W["c_dw_w"] = W["a_conv_w"][0], W["c_dw_w"][0]
    W["c_b_pw1"] = _pair_blocks(W["c_b_pw1"], _CW_C)
    return W


def _with_own(land, src, me, per_dest):
    own = lax.dynamic_slice_in_dim(src, me, 1, 0) if per_dest else src[None]
    return lax.dynamic_update_slice_in_dim(land, own, me, 0)


def _to_dest_major(g, shape):
    full = g.reshape(shape[:-1] + (N_DEV, shape[-1]))
    return jnp.moveaxis(full, -2, 0).reshape(N_DEV, -1)


def kernel(x, mem, ab_norm, ab_w_in, a_conv_w, a_conv_b, a_gate_x_w, a_gate_x_b, a_gate_a_w, a_gate_a_b, a_lambda, b_group_w, b_group_b, b_scale, ab_w_out, c_norm, c_w_pw1, c_b_pw1, c_dw_w, c_dw_b, c_ln_g, c_ln_b, c_w_pw2, c_b_pw2, xa_norm, xa_mem_norm, xa_wq, xa_wk, xa_wv, xa_wo, f_norm, f_w_up, f_dw_w, f_dw_b, f_w_down, final_norm, loss_target, m_ab_norm, m_ab_w_in, m_a_conv_w, m_a_conv_b, m_a_gate_x_w, m_a_gate_x_b, m_a_gate_a_w, m_a_gate_a_b, m_a_lambda, m_b_group_w, m_b_group_b, m_b_scale, m_ab_w_out, m_c_norm, m_c_w_pw1, m_c_b_pw1, m_c_dw_w, m_c_dw_b, m_c_ln_g, m_c_ln_b, m_c_w_pw2, m_c_b_pw2, m_xa_norm, m_xa_mem_norm, m_xa_wq, m_xa_wk, m_xa_wv, m_xa_wo, m_f_norm, m_f_w_up, m_f_dw_w, m_f_dw_b, m_f_w_down, m_final_norm, v_ab_norm, v_ab_w_in, v_a_conv_w, v_a_conv_b, v_a_gate_x_w, v_a_gate_x_b, v_a_gate_a_w, v_a_gate_a_b, v_a_lambda, v_b_group_w, v_b_group_b, v_b_scale, v_ab_w_out, v_c_norm, v_c_w_pw1, v_c_b_pw1, v_c_dw_w, v_c_dw_b, v_c_ln_g, v_c_ln_b, v_c_w_pw2, v_c_b_pw2, v_xa_norm, v_xa_mem_norm, v_xa_wq, v_xa_wk, v_xa_wv, v_xa_wo, v_f_norm, v_f_w_up, v_f_dw_w, v_f_dw_b, v_f_w_down, v_final_norm):
    args = dict(locals())
    P = {n: args[n] for n in _NAMES}
    M = {n: args["m_" + n] for n in _NAMES}
    V = {n: args["v_" + n] for n in _NAMES}

    me = 4 * lax.axis_index("x") + 2 * lax.axis_index("y") + lax.axis_index("c")

    slots, shards = {}, []
    for grp, members in _GROUPS.items():
        slots[grp] = list(range(len(shards), len(shards) + len(members)))
        shards += [P[name][l].astype(BF16) for name, l in members]
    slots[("ab", 0)].append(len(shards))
    shards.append(_pack([P[n] for n in _SMALL_SHARDED], _SS_ROWS + 4))
    g_send, g_recv, g_src, g_land, token = _send_start(shards, False, "gather_start")
    zero = token[:1, :1]

    def fetch(grp, after):
        idx = slots[grp]
        srcs, lands = _send_wait([g_send[i] for i in idx], [g_recv[i] for i in idx], [g_src[i] for i in idx],
                                 [g_land[i] for i in idx], after, False, f"gather_wait_{grp[0]}{grp[1]}")
        full = [_with_own(land, src, me, False) for land, src in zip(lands, srcs)]
        out = {}
        for (name, l), g in zip(_GROUPS[grp], full):
            w = _weight_layout(name, g)
            if _BIG[name][0] == 1:
                out[name] = w
            else:
                out[name] = {l: w}
        if grp == ("ab", 0):
            out.update(_small_layouts(full[-1]))
        return out

    pending = []

    def send(grp, G):
        members = _SEND_GROUPS[grp]
        res = _send_start([_grad_blocks(name, l, G) for name, l in members], True, f"send_{grp[0]}{grp[1]}")
        pending.append((members, res))
        return res[4][:1, :1]

    W = {n: P[n] for n in _REPL}
    W["ab_norm"] = P["ab_norm"] + zero
    W["final_norm"] = P["final_norm"].reshape(1, D)
    W["a_gate_x_w"], W["a_gate_a_w"], W["b_group_w"] = P["a_gate_x_w"][0], P["a_gate_a_w"][0], P["b_group_w"][0]
    loss, grad_x, G = _local_step(x[0], mem[0], loss_target[0], W, fetch, send)
    loss = lax.psum(loss[0, 0], ("x", "y", "c"))

    Gs = dict(G)
    Gs["c_b_pw1"] = _unpair_blocks(G["c_b_pw1"], _CW_C)
    Gs["f_dw_w"] = jnp.stack([G["f_dw_w0"], G["f_dw_w1"]])
    Gs["a_conv_w"], Gs["c_dw_w"] = G["a_conv_w"][None], G["c_dw_w"][None]
    for n in ("xa_norm", "xa_mem_norm", "f_norm", "f_dw_b"):
        Gs[n] = jnp.concatenate([G[f"{n}0"], G[f"{n}1"]], axis=0)
    for n in ("a_gate_x_w", "a_gate_a_w", "b_group_w"):
        Gs[n] = G[n][None]
    repl_flat = jnp.concatenate([Gs[n].reshape(-1) for n in _REPL])
    repl_rows = jnp.pad(repl_flat, (0, N_DEV * _REPL_ROWS * LANE - _N_REPL)).reshape(N_DEV, _REPL_ROWS, LANE)
    ss_rows = jnp.concatenate([_to_dest_major(Gs[n], s) for n, s in _SMALL_SHARDED.items()], axis=1)
    ss_rows = ss_rows.reshape(N_DEV, _SS_ROWS, LANE)
    small_pack = jnp.concatenate(
        [repl_rows, ss_rows, jnp.zeros((N_DEV, _SMALL_ROWS - _REPL_ROWS - _SS_ROWS, LANE), F32)], axis=1)
    last = _send_start([_grad_blocks("ab_w_in", 0, G), small_pack], True, "send_ab0")
    pending.append(((("ab_w_in", 0), ("small", 0)), last))

    members = [m for mem_, _ in pending for m in mem_]
    cat = [[a for _, res in pending for a in res[i]] for i in range(4)]
    srcs, lands = _send_wait(cat[0], cat[1], cat[2], cat[3], grad_x, True, "send_wait")
    landed = {m: _with_own(land, src, me, True) for m, land, src in zip(members, lands, srcs)}

    out_g, out_d, out_m, out_v = {}, {}, {}, {}
    for name, (layers, r, c) in _BIG.items():
        shape = P[name].shape
        w2, m2, v2 = [t[name].reshape(layers * r, c) for t in (P, M, V)]
        res = None
        for l in range(layers):
            res = _sum_adamw(landed[(name, l)], w2, m2, v2, f"adamw_{name}{l}", layer=l, prev=res)
        out_g[name], out_d[name], out_m[name], out_v[name] = [t.reshape(shape) for t in res]

    small_sum = _sum8(landed[("small", 0)], "sum_small")
    (repl_all,) = _all_gather([small_sum[:_REPL_ROWS]], "gather_small_grads")
    g_repl = _unpack(repl_all, _REPL)
    g_ss = _unpack(small_sum[_REPL_ROWS:_REPL_ROWS + _SS_ROWS], _SMALL_SHARDED)
    table = dict(_REPL)
    table.update(_SMALL_SHARDED)
    rows = -(-(_N_REPL + _N_SS) // (256 * LANE)) * 256
    g_small = dict(g_repl)
    g_small.update(g_ss)
    packs = [_pack([src[n] for n in table], rows) for src in (g_small, P, M, V)]
    res = _adamw(*packs, "adamw_small")
    for out, buf in zip((out_d, out_m, out_v), res):
        out.update(_unpack(buf, table))
    out_g.update(g_small)

    return (loss, grad_x[None], *[out_g[n] for n in _NAMES], *[out_d[n] for n in _NAMES],
            *[out_m[n] for n in _NAMES], *[out_v[n] for n in _NAMES])


_NAMES = ("ab_norm", "ab_w_in", "a_conv_w", "a_conv_b", "a_gate_x_w", "a_gate_x_b", "a_gate_a_w", "a_gate_a_b",
          "a_lambda", "b_group_w", "b_group_b", "b_scale", "ab_w_out", "c_norm", "c_w_pw1", "c_b_pw1", "c_dw_w",
          "c_dw_b", "c_ln_g", "c_ln_b", "c_w_pw2", "c_b_pw2", "xa_norm", "xa_mem_norm", "xa_wq", "xa_wk", "xa_wv",
          "xa_wo", "f_norm", "f_w_up", "f_dw_w", "f_dw_b", "f_w_down", "final_norm")
```

```python
import functools

import jax
import jax.numpy as jnp
from jax import lax
from jax.experimental import pallas as pl
from jax.experimental.pallas import tpu as pltpu

F32, BF16 = jnp.float32, jnp.bfloat16
SDS = jax.ShapeDtypeStruct
MESH = pl.DeviceIdType.MESH

N_DEV = 8
D = 1024
N_MEM = 256
XA_HEADS, XA_HD = 4, 256
HD_A = 128
CONV_A, CONV_C, CONV_F = 4, 31, 3
C_RG = 8.0
POOL_WINDOWS = (2, 4, 8, 16)
D_FF = 3 * D
EPS = 1e-6
ADAM_LR, ADAM_B1, ADAM_B2, ADAM_EPS, ADAM_WD, ADAM_STEP = 0.001, 0.9, 0.999, 1e-08, 0.01, 10

LANE = 128
SUB = 8
VMEM_LIMIT = 56 * 1024 * 1024
R_SEQ = 256
TM_ROW = 512


def _cp(n_axes):
    return pltpu.CompilerParams(dimension_semantics=("arbitrary",) * n_axes, vmem_limit_bytes=VMEM_LIMIT)


def _tile(n, pref):
    if n <= pref:
        return n
    best = None
    for t in range(LANE, pref + 1, LANE):
        if n % t == 0:
            best = t
    assert best is not None, (n, pref)
    return best


def _perm2(n):
    return (n % 2) * 4 + n // 2


_NN = (((1,), (0,)), ((), ()))
_NT = (((1,), (1,)), ((), ()))
_TN = (((0,), (0,)), ((), ()))


def _mm_call(name, grid, a, b, a_spec, b_spec, o_spec, out_shape, dims, acc_shape, extras=()):
    nk = grid[2]
    n_ex = len(extras)

    def finish(r, ex_refs, o_ref):
        for e in ex_refs:
            r = r + e[...]
        o_ref[...] = r.astype(o_ref.dtype)

    def body_one(a_ref, b_ref, *rest):
        finish(lax.dot_general(a_ref[...], b_ref[...], dims, preferred_element_type=F32), rest[:n_ex], rest[n_ex])

    def body_acc(a_ref, b_ref, *rest):
        ex_refs, o_ref, acc = rest[:n_ex], rest[n_ex], rest[n_ex + 1]
        k = pl.program_id(2)

        @pl.when(k == 0)
        def _():
            acc[...] = jnp.zeros_like(acc)

        acc[...] += lax.dot_general(a_ref[...], b_ref[...], dims, preferred_element_type=F32)

        @pl.when(k == nk - 1)
        def _():
            finish(acc[...], ex_refs, o_ref)

    return pl.pallas_call(
        body_one if nk == 1 else body_acc, out_shape=out_shape, grid=grid,
        in_specs=[a_spec, b_spec] + [s for _, s in extras], out_specs=o_spec,
        scratch_shapes=[] if nk == 1 else [pltpu.VMEM(acc_shape, F32)], name=name, compiler_params=_cp(3),
    )(a, b, *[e for e, _ in extras])


_K_WHOLE = 3072


def _mm_nn(a, b, *, out_dtype, name, bias=None, add=None, old=False):
    M, K = a.shape
    if old:
        tm, tk = _tile(M, 1024), _tile(K, 512)
    else:
        tk = K if K <= _K_WHOLE else _tile(K, 1024)
        tm = _tile(M, 1024 if K <= 1024 else 512)
    if b.ndim == 3:
        nb, _, bw = b.shape
        N, tn, nn = nb * bw, bw, nb
        b_spec = pl.BlockSpec((None, tk, bw), lambda m, n, k: (_perm2(n), k, 0))
    else:
        N = b.shape[1]
        tn = _tile(N, 1024)
        nn = N // tn
        b_spec = pl.BlockSpec((tk, tn), lambda m, n, k: (k, n))
    extras = []
    if bias is not None:
        extras.append((bias, pl.BlockSpec((1, tn), lambda m, n, k: (0, n))))
    if add is not None:
        extras.append((add, pl.BlockSpec((tm, tn), lambda m, n, k: (m, n))))
    return _mm_call(name, (M // tm, nn, K // tk), a, b, pl.BlockSpec((tm, tk), lambda m, n, k: (m, k)), b_spec,
                    pl.BlockSpec((tm, tn), lambda m, n, k: (m, n)), SDS((M, N), out_dtype), _NN, (tm, tn), extras)


def _mm_nt(a, b, *, out_dtype, name, add=None, old=False):
    M, N = a.shape
    if b.ndim == 3:
        nb, Ko, bw = b.shape
        tm = _tile(M, 1024)
        tn, tk, nk = _tile(Ko, 1024), bw, nb
        b_spec = pl.BlockSpec((None, tn, bw), lambda m, n, k: (_perm2(k), n, 0))
    else:
        Ko = b.shape[0]
        if old:
            tm, tk = _tile(M, 1024), _tile(N, 512)
        else:
            tk = N if N <= _K_WHOLE else _tile(N, 1024)
            tm = _tile(M, 1024 if N <= 1024 else 512)
        tn = _tile(Ko, 1024)
        nk = N // tk
        b_spec = pl.BlockSpec((tn, tk), lambda m, n, k: (n, k))
    extras = []
    if add is not None:
        extras.append((add, pl.BlockSpec((tm, tn), lambda m, n, k: (m, n))))
    return _mm_call(name, (M // tm, Ko // tn, nk), a, b, pl.BlockSpec((tm, tk), lambda m, n, k: (m, k)), b_spec,
                    pl.BlockSpec((tm, tn), lambda m, n, k: (m, n)), SDS((M, Ko), out_dtype), _NT, (tm, tn), extras)


def _mm_tn(a, b, *, out_dtype, name, blocks=None, old=False):
    S, Ka = a.shape
    Nb = b.shape[1]
    tm, tk = _tile(Ka, 1024), _tile(S, 512 if old else 2048)
    if blocks is not None:
        bw = blocks
        tn, nn = bw, Nb // bw
        o_spec = pl.BlockSpec((None, tm, bw), lambda m, n, k: (_perm2(n), m, 0))
        out_shape = SDS((nn, Ka, bw), out_dtype)
    else:
        tn = _tile(Nb, 1024)
        nn = Nb // tn
        o_spec = pl.BlockSpec((tm, tn), lambda m, n, k: (m, n))
        out_shape = SDS((Ka, Nb), out_dtype)
    return _mm_call(name, (Ka // tm, nn, S // tk), a, b, pl.BlockSpec((tk, tm), lambda m, n, k: (k, m)),
                    pl.BlockSpec((tk, tn), lambda m, n, k: (k, n)), o_spec, out_shape, _TN, (tm, tn))


def _row(tm, c):
    return pl.BlockSpec((tm, c), lambda i: (i, 0))


def _full(shape):
    nd = len(shape)
    return pl.BlockSpec(shape, lambda i: (0,) * nd)


def _rms_fwd(x, g, name):
    S = x.shape[0]
    tm = min(S, TM_ROW)

    def body(x_ref, g_ref, o_ref):
        xf = x_ref[...]
        r = lax.rsqrt(jnp.mean(xf * xf, axis=-1, keepdims=True) + EPS)
        o_ref[...] = ((xf * r) * g_ref[...]).astype(BF16)

    return pl.pallas_call(body, out_shape=SDS((S, D), BF16), grid=(S // tm,), in_specs=[_row(tm, D), _full((1, D))],
                          out_specs=_row(tm, D), name=name, compiler_params=_cp(1))(x, g)


def _rms_bwd(x, g, dn, dres, name):
    S = x.shape[0]
    tm = min(S, TM_ROW)
    want_dx = dres is not None

    def body(x_ref, g_ref, dn_ref, *rest):
        i = pl.program_id(0)
        dg_ref = rest[-1]

        @pl.when(i == 0)
        def _():
            dg_ref[...] = jnp.zeros_like(dg_ref)

        xf = x_ref[...]
        r = lax.rsqrt(jnp.mean(xf * xf, axis=-1, keepdims=True) + EPS)
        y = xf * r
        dn_v = dn_ref[...]
        dg_ref[...] += jnp.sum(dn_v * y, axis=0, keepdims=True)
        if want_dx:
            dres_ref, dx_ref, dxb_ref = rest[0], rest[1], rest[2]
            dy = dn_v * g_ref[...]
            dx = r * (dy - y * jnp.mean(dy * y, axis=-1, keepdims=True)) + dres_ref[...]
            dx_ref[...] = dx
            dxb_ref[...] = dx.astype(BF16)

    ins = [x, g, dn] + ([dres] if want_dx else [])
    in_specs = [_row(tm, D), _full((1, D)), _row(tm, D)] + ([_row(tm, D)] if want_dx else [])
    outs = ([SDS((S, D), F32), SDS((S, D), BF16)] if want_dx else []) + [SDS((1, D), F32)]
    out_specs = ([_row(tm, D), _row(tm, D)] if want_dx else []) + [_full((1, D))]
    return pl.pallas_call(body, out_shape=outs, grid=(S // tm,), in_specs=in_specs, out_specs=out_specs, name=name,
                          compiler_params=_cp(1))(*ins)


def _loss_head(x, g, tgt):
    S = x.shape[0]
    tm = min(S, TM_ROW)

    def body(x_ref, g_ref, t_ref, loss_ref, dx_ref, dxb_ref, dg_ref):
        i = pl.program_id(0)

        @pl.when(i == 0)
        def _():
            loss_ref[...] = jnp.zeros_like(loss_ref)
            dg_ref[...] = jnp.zeros_like(dg_ref)

        xf = x_ref[...]
        r = lax.rsqrt(jnp.mean(xf * xf, axis=-1, keepdims=True) + EPS)
        y = xf * r
        gv = g_ref[...]
        err = y * gv - t_ref[...]
        per_row = jnp.mean(err * err, axis=-1, keepdims=True)
        loss_ref[...] += 0.5 * jnp.sum(per_row, axis=0, keepdims=True)
        dn_v = err * (1.0 / D)
        dg_ref[...] += jnp.sum(dn_v * y, axis=0, keepdims=True)
        dy = dn_v * gv
        dx = r * (dy - y * jnp.mean(dy * y, axis=-1, keepdims=True))
        dx_ref[...] = dx
        dxb_ref[...] = dx.astype(BF16)

    return pl.pallas_call(
        body, out_shape=[SDS((1, 1), F32), SDS((S, D), F32), SDS((S, D), BF16), SDS((1, D), F32)], grid=(S // tm,),
        in_specs=[_row(tm, D), _full((1, D)), _row(tm, D)],
        out_specs=[_full((1, 1)), _row(tm, D), _row(tm, D), _full((1, D))], name="loss_head", compiler_params=_cp(1),
    )(x, g, tgt)


def _softmax_rows(s):
    m = jnp.max(s, axis=-1, keepdims=True)
    e = jnp.exp(s - m)
    return e / jnp.sum(e, axis=-1, keepdims=True)


def _attn_fwd(q, k, v, name):
    S = q.shape[0]
    tm = min(S, TM_ROW)
    scale = XA_HD ** -0.5

    def body(q_ref, k_ref, v_ref, o_ref):
        for h in range(XA_HEADS):
            sl = slice(h * XA_HD, (h + 1) * XA_HD)
            s = lax.dot_general(q_ref[:, sl], k_ref[:, sl], _NT, preferred_element_type=F32) * scale
            p = _softmax_rows(s)
            o_ref[:, sl] = lax.dot_general(p.astype(BF16), v_ref[:, sl], _NN, preferred_element_type=F32).astype(BF16)

    return pl.pallas_call(body, out_shape=SDS((S, D), BF16), grid=(S // tm,),
                          in_specs=[_row(tm, D), _full((N_MEM, D)), _full((N_MEM, D))], out_specs=_row(tm, D),
                          name=name, compiler_params=_cp(1))(q, k, v)


def _attn_bwd(q, k, v, do, name):
    S = q.shape[0]
    tm = min(S, TM_ROW)
    scale = XA_HD ** -0.5

    def body(q_ref, k_ref, v_ref, do_ref, dq_ref, dk_ref, dv_ref):
        i = pl.program_id(0)

        @pl.when(i == 0)
        def _():
            dk_ref[...] = jnp.zeros_like(dk_ref)
            dv_ref[...] = jnp.zeros_like(dv_ref)

        for h in range(XA_HEADS):
            sl = slice(h * XA_HD, (h + 1) * XA_HD)
            qh, kh, vh, doh = q_ref[:, sl], k_ref[:, sl], v_ref[:, sl], do_ref[:, sl]
            s = lax.dot_general(qh, kh, _NT, preferred_element_type=F32) * scale
            p = _softmax_rows(s)
            pb = p.astype(BF16)
            dv_ref[:, sl] += lax.dot_general(pb, doh, _TN, preferred_element_type=F32)
            dp = lax.dot_general(doh, vh, _NT, preferred_element_type=F32)
            ds = (p * (dp - jnp.sum(dp * p, axis=-1, keepdims=True)) * scale).astype(BF16)
            dq_ref[:, sl] = lax.dot_general(ds, kh, _NN, preferred_element_type=F32).astype(BF16)
            dk_ref[:, sl] += lax.dot_general(ds, qh, _TN, preferred_element_type=F32)

    return pl.pallas_call(
        body, out_shape=[SDS((S, D), BF16), SDS((N_MEM, D), F32), SDS((N_MEM, D), F32)], grid=(S // tm,),
        in_specs=[_row(tm, D), _full((N_MEM, D)), _full((N_MEM, D)), _row(tm, D)],
        out_specs=[_row(tm, D), _full((N_MEM, D)), _full((N_MEM, D))], name=name, compiler_params=_cp(1),
    )(q, k, v, do)


def _sigmoid(x):
    return 1.0 / (1.0 + jnp.exp(-x))


def _ln_silu_fwd(cv, g, b):
    S = cv.shape[0]
    tm = min(S, TM_ROW)

    def body(x_ref, g_ref, b_ref, o_ref):
        xf = x_ref[...]
        mu = jnp.mean(xf, axis=-1, keepdims=True)
        xc = xf - mu
        rstd = lax.rsqrt(jnp.mean(xc * xc, axis=-1, keepdims=True) + EPS)
        ln = (xc * rstd) * g_ref[...] + b_ref[...]
        o_ref[...] = (ln * _sigmoid(ln)).astype(BF16)

    return pl.pallas_call(body, out_shape=SDS((S, D), BF16), grid=(S // tm,),
                          in_specs=[_row(tm, D), _full((1, D)), _full((1, D))], out_specs=_row(tm, D),
                          name="ln_silu_fwd", compiler_params=_cp(1))(cv, g, b)


def _ln_silu_bwd(ds, cv, g, b, dx):
    S = cv.shape[0]
    tm = min(S, TM_ROW)

    def body(ds_ref, x_ref, g_ref, b_ref, dx_ref, dcv_ref, dg_ref, db_ref, db2_ref):
        i = pl.program_id(0)

        @pl.when(i == 0)
        def _():
            dg_ref[...] = jnp.zeros_like(dg_ref)
            db_ref[...] = jnp.zeros_like(db_ref)
            db2_ref[...] = jnp.zeros_like(db2_ref)

        xf = x_ref[...]
        mu = jnp.mean(xf, axis=-1, keepdims=True)
        xc = xf - mu
        rstd = lax.rsqrt(jnp.mean(xc * xc, axis=-1, keepdims=True) + EPS)
        xhat = xc * rstd
        gv = g_ref[...]
        ln = xhat * gv + b_ref[...]
        sg = _sigmoid(ln)
        dln = ds_ref[...].astype(F32) * (sg + ln * sg * (1.0 - sg))
        dg_ref[...] += jnp.sum(dln * xhat, axis=0, keepdims=True)
        db_ref[...] += jnp.sum(dln, axis=0, keepdims=True)
        db2_ref[...] += jnp.sum(dx_ref[...], axis=0, keepdims=True)
        dxh = dln * gv
        dcv_ref[...] = rstd * (dxh - jnp.mean(dxh, axis=-1, keepdims=True)
                               - xhat * jnp.mean(dxh * xhat, axis=-1, keepdims=True))

    return pl.pallas_call(
        body, out_shape=[SDS((S, D), F32), SDS((1, D), F32), SDS((1, D), F32), SDS((1, D), F32)], grid=(S // tm,),
        in_specs=[_row(tm, D), _row(tm, D), _full((1, D)), _full((1, D)), _row(tm, D)],
        out_specs=[_row(tm, D), _full((1, D)), _full((1, D)), _full((1, D))], name="ln_silu_bwd",
        compiler_params=_cp(1),
    )(ds, cv, g, b, dx)


_GELU_C, _GELU_K = 0.7978845608028654, 0.044715


def _gelu(x, with_grad=False):
    x2 = x * x
    t = jnp.tanh(_GELU_C * (x + _GELU_K * x * x2))
    gel = 0.5 * x * (1.0 + t)
    if not with_grad:
        return gel
    return gel, 0.5 * (1.0 + t) + 0.5 * x * (1.0 - t * t) * (_GELU_C * (1.0 + 3.0 * _GELU_K * x2))


def _expm1(x):
    poly = x * (1.0 + x * (0.5 + x * (1.0 / 6.0 + x * (1.0 / 24.0 + x * (1.0 / 120.0)))))
    return jnp.where(jnp.abs(x) < 0.05, poly, jnp.exp(x) - 1.0)


def _softplus(x):
    return jnp.maximum(x, 0.0) + jnp.log1p(jnp.exp(-jnp.abs(x)))


_SCAN_UNROLL = 4
_RB = 32
_HB = 16


def _sub_blocks(n_rows, n_lanes, fn):
    def step(idx, c):
        r0 = pl.multiple_of(idx * _RB, _RB)
        for lt in range(n_lanes // LANE):
            fn(r0, lt)
        return c

    lax.fori_loop(0, n_rows // _RB, step, 0)


def _lanes(lt):
    return pl.ds(lt * LANE, LANE)


def _psum8(x):
    parts = [x[i * SUB:(i + 1) * SUB] for i in range(x.shape[0] // SUB)]
    return functools.reduce(lambda p, q: p + q, parts)


def _scan_fwd(a_s, b_s, out_ref, carry_ref, n_groups):
    row = lax.broadcasted_iota(jnp.int32, (SUB, LANE), 0)
    U = _SCAN_UNROLL

    def step(gi, carry):
        base = gi * (SUB * U)
        parts = []
        for u in range(U):
            i = pl.multiple_of(base + u * SUB, SUB)
            a8, b8 = a_s[pl.ds(i, SUB), :], b_s[pl.ds(i, SUB), :]
            for s in (1, 2, 4):
                a_sh = jnp.where(row >= s, pltpu.roll(a8, s, 0), 1.0)
                b_sh = jnp.where(row >= s, pltpu.roll(b8, s, 0), 0.0)
                b8 = a8 * b_sh + b8
                a8 = a8 * a_sh
            parts.append((i, a8, b8))
        for i, a8, b8 in parts:
            h8 = a8 * carry + b8
            out_ref[pl.ds(i, SUB), :] = h8
            carry = jnp.broadcast_to(h8[SUB - 1:SUB, :], (SUB, LANE))
        return carry

    carry_ref[...] = lax.fori_loop(0, n_groups // U, step, carry_ref[...])


def _scan_bwd(a_s, b_s, out_ref, carry_ref, n_groups):
    row = lax.broadcasted_iota(jnp.int32, (SUB, LANE), 0)
    U = _SCAN_UNROLL

    def step(gi, carry):
        base = (n_groups // U - 1 - gi) * (SUB * U)
        parts = []
        for u in reversed(range(U)):
            i = pl.multiple_of(base + u * SUB, SUB)
            a8, b8 = a_s[pl.ds(i, SUB), :], b_s[pl.ds(i, SUB), :]
            for s in (1, 2, 4):
                a_sh = jnp.where(row < SUB - s, pltpu.roll(a8, SUB - s, 0), 1.0)
                b_sh = jnp.where(row < SUB - s, pltpu.roll(b8, SUB - s, 0), 0.0)
                b8 = a8 * b_sh + b8
                a8 = a8 * a_sh
            parts.append((i, a8, b8))
        for i, a8, b8 in parts:
            h8 = a8 * carry + b8
            out_ref[pl.ds(i, SUB), :] = h8
            carry = jnp.broadcast_to(h8[0:1, :], (SUB, LANE))
        return carry

    carry_ref[...] = lax.fori_loop(0, n_groups // U, step, carry_ref[...])


def _rglru_pre(xr, wgx_ref, bgx_ref, wga_ref, bga_ref, lam_ref):
    xrb = xr.astype(BF16)
    wgx, wga = wgx_ref[0].astype(BF16), wga_ref[0].astype(BF16)
    gx = _sigmoid(lax.dot_general(xrb, wgx, _NN, preferred_element_type=F32) + bgx_ref[...])
    ga = _sigmoid(lax.dot_general(xrb, wga, _NN, preferred_element_type=F32) + bga_ref[...])
    sp = _softplus(-lam_ref[...])
    log_a = -C_RG * ga * sp
    a = jnp.exp(log_a)
    mult = jnp.sqrt(-_expm1(2.0 * log_a))
    return gx, ga, sp, a, mult, xrb, wgx, wga


def _a_specs():
    vec = pl.BlockSpec((1, HD_A), lambda c, j: (0, c))
    mat = pl.BlockSpec((1, HD_A, HD_A), lambda c, j: (c, 0, 0))
    return [pl.BlockSpec((CONV_A, HD_A), lambda c, j: (0, c)), vec, mat, vec, mat, vec, vec]


def _a_fwd(zp, conv_w, conv_b, wgx, bgx, wga, bga, lam):
    S = zp.shape[0]
    R, nt = R_SEQ, D // HD_A
    H = SUB

    def body(z_ref, cw_ref, cb_ref, wgx_ref, bgx_ref, wga_ref, bga_ref, lam_ref, ya_ref, h_ref, ext, a_s, b_s, hc):
        j = pl.program_id(1)

        @pl.when(j == 0)
        def _():
            ext[0:H, :] = jnp.zeros((H, HD_A), F32)
            hc[...] = jnp.zeros_like(hc)

        ext[H:H + R, :] = z_ref[:, HD_A:2 * HD_A].astype(F32)
        xr = cb_ref[...]
        for k in range(CONV_A):
            xr = xr + cw_ref[k:k + 1, :] * ext[pl.ds(H - (CONV_A - 1 - k), R), :]
        gx, _, _, a, mult, _, _, _ = _rglru_pre(xr, wgx_ref, bgx_ref, wga_ref, bga_ref, lam_ref)
        a_s[...] = a
        b_s[...] = mult * (gx * xr)
        _scan_fwd(a_s, b_s, h_ref, hc, R // SUB)
        ya_ref[...] = (_gelu(z_ref[:, 0:HD_A].astype(F32)) * h_ref[...]).astype(BF16)
        ext[0:H, :] = ext[R:R + H, :]

    return pl.pallas_call(
        body, out_shape=[SDS((S, D + D // 2), BF16), SDS((S, D), F32)], grid=(nt, S // R),
        in_specs=[pl.BlockSpec((R, 2 * HD_A), lambda c, j: (j, c))] + _a_specs(),
        out_specs=[pl.BlockSpec((R, HD_A), lambda c, j: (j, c)), pl.BlockSpec((R, HD_A), lambda c, j: (j, c))],
        scratch_shapes=[pltpu.VMEM((H + R, HD_A), F32), pltpu.VMEM((R, HD_A), F32), pltpu.VMEM((R, HD_A), F32),
                        pltpu.VMEM((SUB, HD_A), F32)],
        name="rglru_fwd", compiler_params=_cp(2),
    )(zp, conv_w, conv_b, wgx, bgx, wga, bga, lam)


def _a_bwd(dyab, zp, h, conv_w, conv_b, wgx, bgx, wga, bga, lam):
    S = zp.shape[0]
    R, nt, nch = R_SEQ, D // HD_A, S // R_SEQ
    H = SUB

    def rows(c, j):
        return (nch - 1 - j, c)

    def halo(c, j):
        return (jnp.maximum((nch - 1 - j) * (R // H) - 1, 0), c)

    def halo_z(c, j):
        return (jnp.maximum((nch - 1 - j) * (R // _HB) - 1, 0), c)

    def body(dy_ref, z_ref, zh_ref, h_ref, hh_ref, cw_ref, cb_ref, wgx_ref, bgx_ref, wga_ref, bga_ref, lam_ref,
             dz_ref, dcw_ref, dcb_ref, dwgx_ref, dbgx_ref, dwga_ref, dbga_ref, dlam_ref,
             ext_z, ext_h, ext_mu, ext_d, a_s, b_s, muc):
        j = pl.program_id(1)
        first_chunk = (nch - 1 - j) == 0

        @pl.when(j == 0)
        def _():
            ext_mu[R:R + H, :] = jnp.zeros((H, HD_A), F32)
            ext_d[R:R + H, :] = jnp.zeros((H, HD_A), F32)
            muc[...] = jnp.zeros_like(muc)
            for r in (dcw_ref, dcb_ref, dwgx_ref, dbgx_ref, dwga_ref, dbga_ref, dlam_ref):
                r[...] = jnp.zeros_like(r)

        zg = z_ref[:, 0:HD_A].astype(F32)
        ext_z[0:H, :] = jnp.where(first_chunk, 0.0, zh_ref[_HB - H:_HB, HD_A:2 * HD_A].astype(F32))
        ext_z[H:H + R, :] = z_ref[:, HD_A:2 * HD_A].astype(F32)
        ext_h[0:H, :] = jnp.where(first_chunk, 0.0, hh_ref[...])
        ext_h[H:H + R, :] = h_ref[...]
        xr = cb_ref[...]
        for k in range(CONV_A):
            xr = xr + cw_ref[k:k + 1, :] * ext_z[pl.ds(H - (CONV_A - 1 - k), R), :]
        gx, ga, sp, a, mult, xrb, wgxb, wgab = _rglru_pre(xr, wgx_ref, bgx_ref, wga_ref, bga_ref, lam_ref)
        gel, dgel = _gelu(zg, with_grad=True)
        dy = dy_ref[...].astype(F32)
        dh = dy * gel
        dz_ref[:, 0:HD_A] = (dy * h_ref[...] * dgel).astype(BF16)
        a_s[...] = a
        b_s[...] = a * dh
        _scan_bwd(a_s, b_s, ext_mu, muc, R // SUB)
        lam_t = dh + ext_mu[pl.ds(1, R), :]
        ext_mu[R:R + H, :] = ext_mu[0:H, :]
        da = lam_t * ext_h[pl.ds(H - 1, R), :]
        gxr = gx * xr
        dlog_a = da * a - (lam_t * gxr) * (a * a) / mult
        dgx = lam_t * mult * xr
        dxr = lam_t * mult * gx
        lam_v = lam_ref[...]
        dlam_ref[...] += jnp.sum(dlog_a * ga, axis=0, keepdims=True) * (C_RG * _sigmoid(-lam_v))
        dpa = (dlog_a * (-C_RG * sp)) * ga * (1.0 - ga)
        dpx = dgx * gx * (1.0 - gx)
        dbga_ref[...] += jnp.sum(dpa, axis=0, keepdims=True)
        dbgx_ref[...] += jnp.sum(dpx, axis=0, keepdims=True)
        dpab, dpxb = dpa.astype(BF16), dpx.astype(BF16)
        dwga_ref[0] += lax.dot_general(xrb, dpab, _TN, preferred_element_type=F32)
        dwgx_ref[0] += lax.dot_general(xrb, dpxb, _TN, preferred_element_type=F32)
        dxr = (dxr + lax.dot_general(dpab, wgab, _NT, preferred_element_type=F32)
               + lax.dot_general(dpxb, wgxb, _NT, preferred_element_type=F32))
        dcb_ref[...] += jnp.sum(dxr, axis=0, keepdims=True)
        ext_d[0:R, :] = dxr
        dzr = jnp.zeros((R, HD_A), F32)
        for k in range(CONV_A):
            sh = CONV_A - 1 - k
            dcw_ref[k:k + 1, :] += jnp.sum(dxr * ext_z[pl.ds(H - sh, R), :], axis=0, keepdims=True)
            dzr = dzr + cw_ref[k:k + 1, :] * ext_d[pl.ds(sh, R), :]
        dz_ref[:, HD_A:2 * HD_A] = dzr.astype(BF16)
        ext_d[R:R + H, :] = ext_d[0:H, :]

    vec_o = pl.BlockSpec((1, HD_A), lambda c, j: (0, c))
    mat_o = pl.BlockSpec((1, HD_A, HD_A), lambda c, j: (c, 0, 0))
    return pl.pallas_call(
        body,
        out_shape=[SDS((S, 2 * D + D // 2), BF16), SDS((CONV_A, D), F32), SDS((1, D), F32), SDS((nt, HD_A, HD_A), F32),
                   SDS((1, D), F32), SDS((nt, HD_A, HD_A), F32), SDS((1, D), F32), SDS((1, D), F32)],
        grid=(nt, nch),
        in_specs=[pl.BlockSpec((R, HD_A), rows), pl.BlockSpec((R, 2 * HD_A), rows),
                  pl.BlockSpec((_HB, 2 * HD_A), halo_z), pl.BlockSpec((R, HD_A), rows),
                  pl.BlockSpec((H, HD_A), halo)] + _a_specs(),
        out_specs=[pl.BlockSpec((R, 2 * HD_A), rows), pl.BlockSpec((CONV_A, HD_A), lambda c, j: (0, c)), vec_o, mat_o,
                   vec_o, mat_o, vec_o, vec_o],
        scratch_shapes=[pltpu.VMEM((H + R, HD_A), F32), pltpu.VMEM((H + R, HD_A), F32), pltpu.VMEM((R + H, HD_A), F32),
                        pltpu.VMEM((R + H, HD_A), F32), pltpu.VMEM((R, HD_A), F32), pltpu.VMEM((R, HD_A), F32),
                        pltpu.VMEM((SUB, HD_A), F32)],
        name="rglru_bwd", compiler_params=_cp(2),
    )(dyab, zp, zp, h, h, conv_w, conv_b, wgx, bgx, wga, bga, lam)


_POOL_H = 16
_POOL_T0 = 2 * D // HD_A
_POOL_Y0 = D // HD_A


def _pool_mean_minus(u, ext, g, t1):
    R = u.shape[0]
    acc, wins = u, []
    for k in range(1, _POOL_H):
        acc = acc + ext[pl.ds(_POOL_H - k, R), :]
        if k + 1 in POOL_WINDOWS:
            wins.append(acc)
    win = jnp.where(g == 0, wins[0], jnp.where(g == 1, wins[1], jnp.where(g == 2, wins[2], wins[3])))
    return win / jnp.minimum(t1, _pool_width(g)) - u


def _pool_width(g):
    return jnp.where(g == 0, 2.0, jnp.where(g == 1, 4.0, jnp.where(g == 2, 8.0, 16.0)))


def _b_fwd(zp, yab, wg, bg, sc):
    S = zp.shape[0]
    R, H = R_SEQ, _POOL_H

    def body(z_ref, wg_ref, bg_ref, sc_ref, yab_in, yb_ref, ext):
        del yab_in
        g, j = pl.program_id(0), pl.program_id(1)

        @pl.when(j == 0)
        def _():
            ext[0:H, :] = jnp.zeros((H, HD_A), F32)

        u = z_ref[...].astype(F32)
        ext[H:H + R, :] = u
        t1 = (j * R + 1 + lax.broadcasted_iota(jnp.int32, (R, HD_A), 0)).astype(F32)
        p = _pool_mean_minus(u, ext, g, t1)
        lin = lax.dot_general(p.astype(BF16), wg_ref[0].astype(BF16), _NN, preferred_element_type=F32) + bg_ref[...]
        yb_ref[...] = (lin * sc_ref[...]).astype(BF16)
        ext[0:H, :] = ext[R:R + H, :]

    vec = pl.BlockSpec((1, HD_A), lambda g, j: (0, g))
    return pl.pallas_call(
        body, out_shape=SDS(yab.shape, yab.dtype), grid=(len(POOL_WINDOWS), S // R),
        in_specs=[pl.BlockSpec((R, HD_A), lambda g, j: (j, _POOL_T0 + g)),
                  pl.BlockSpec((1, HD_A, HD_A), lambda g, j: (g, 0, 0)), vec, vec, pl.BlockSpec(memory_space=pl.ANY)],
        out_specs=pl.BlockSpec((R, HD_A), lambda g, j: (j, _POOL_Y0 + g)),
        scratch_shapes=[pltpu.VMEM((H + R, HD_A), F32)], input_output_aliases={4: 0},
        name="pool_fwd", compiler_params=_cp(2),
    )(zp, wg, bg, sc, yab)


def _b_bwd(dyab, zp, dzp, wg, bg, sc):
    S = zp.shape[0]
    R, H, nch, ng = R_SEQ, _POOL_H, S // R_SEQ, len(POOL_WINDOWS)

    def body(dy_ref, z_ref, zh_ref, wg_ref, bg_ref, sc_ref, dz_in, dz_ref, dwg_ref, dbg_ref, dsc_ref, ext_u, ext_q):
        del dz_in
        g, j = pl.program_id(0), pl.program_id(1)
        jj = nch - 1 - j

        @pl.when(j == 0)
        def _():
            ext_q[R:R + H, :] = jnp.zeros((H, HD_A), F32)
            for r in (dwg_ref, dbg_ref, dsc_ref):
                r[...] = jnp.zeros_like(r)

        u = z_ref[...].astype(F32)
        ext_u[0:H, :] = jnp.where(jj == 0, 0.0, zh_ref[...].astype(F32))
        ext_u[H:H + R, :] = u
        t1 = (jj * R + 1 + lax.broadcasted_iota(jnp.int32, (R, HD_A), 0)).astype(F32)
        pb = _pool_mean_minus(u, ext_u, g, t1).astype(BF16)
        wgb = wg_ref[0].astype(BF16)
        lin = lax.dot_general(pb, wgb, _NN, preferred_element_type=F32) + bg_ref[...]
        dy = dy_ref[...].astype(F32)
        dsc_ref[...] += jnp.sum(dy * lin, axis=0, keepdims=True)
        dlin = dy * sc_ref[...]
        dbg_ref[...] += jnp.sum(dlin, axis=0, keepdims=True)
        dlb = dlin.astype(BF16)
        dwg_ref[0] += lax.dot_general(pb, dlb, _TN, preferred_element_type=F32)
        dp = lax.dot_general(dlb, wgb, _NT, preferred_element_type=F32)
        q = dp / jnp.minimum(t1, _pool_width(g))
        ext_q[0:R, :] = q
        acc, wins = q, []
        for k in range(1, H):
            acc = acc + ext_q[pl.ds(k, R), :]
            if k + 1 in POOL_WINDOWS:
                wins.append(acc)
        win = jnp.where(g == 0, wins[0], jnp.where(g == 1, wins[1], jnp.where(g == 2, wins[2], wins[3])))
        dz_ref[...] = (win - dp).astype(BF16)
        ext_q[R:R + H, :] = ext_q[0:H, :]

    vec = pl.BlockSpec((1, HD_A), lambda g, j: (0, g))
    mat = pl.BlockSpec((1, HD_A, HD_A), lambda g, j: (g, 0, 0))
    return pl.pallas_call(
        body, out_shape=[SDS(dzp.shape, dzp.dtype), SDS((ng, HD_A, HD_A), F32), SDS((1, D // 2), F32),
                         SDS((1, D // 2), F32)],
        grid=(ng, nch),
        in_specs=[pl.BlockSpec((R, HD_A), lambda g, j: (nch - 1 - j, _POOL_Y0 + g)),
                  pl.BlockSpec((R, HD_A), lambda g, j: (nch - 1 - j, _POOL_T0 + g)),
                  pl.BlockSpec((H, HD_A), lambda g, j: (jnp.maximum((nch - 1 - j) * (R // H) - 1, 0), _POOL_T0 + g)),
                  mat, vec, vec, pl.BlockSpec(memory_space=pl.ANY)],
        out_specs=[pl.BlockSpec((R, HD_A), lambda g, j: (nch - 1 - j, _POOL_T0 + g)), mat, vec, vec],
        scratch_shapes=[pltpu.VMEM((H + R, HD_A), F32), pltpu.VMEM((R + H, HD_A), F32)],
        input_output_aliases={6: 0}, name="pool_bwd", compiler_params=_cp(2),
    )(dyab, zp, zp, wg, bg, sc, dzp)


_CW_F = 768


def _f_fwd(hp, w, b, name):
    S = hp.shape[0]
    R, H, cw = R_SEQ, SUB, _CW_F
    nlt = cw // LANE

    def body(h_ref, w_ref, b_ref, o_ref, ext):
        j = pl.program_id(1)

        @pl.when(j == 0)
        def _():
            ext[:, 0:H, :] = jnp.zeros((nlt, H, LANE), F32)

        def stage(r0, lt):
            ext[lt, pl.ds(pl.multiple_of(r0 + H, SUB), _RB), :] = h_ref[pl.ds(r0, _RB), _lanes(lt)].astype(F32)

        def main(r0, lt):
            ls = _lanes(lt)
            gp = b_ref[:, ls]
            for k in range(CONV_F):
                gp = gp + w_ref[k:k + 1, ls] * ext[lt, pl.ds(r0 + (H - (CONV_F - 1 - k)), _RB), :]
            up = h_ref[pl.ds(r0, _RB), _lanes(lt + nlt)].astype(F32)
            o_ref[pl.ds(r0, _RB), ls] = (_gelu(gp) * up).astype(BF16)

        _sub_blocks(R, cw, stage)
        _sub_blocks(R, cw, main)
        ext[:, 0:H, :] = ext[:, R:R + H, :]

    return pl.pallas_call(
        body, out_shape=SDS((S, D_FF), BF16), grid=(D_FF // cw, S // R),
        in_specs=[pl.BlockSpec((R, 2 * cw), lambda c, j: (j, c)), pl.BlockSpec((CONV_F, cw), lambda c, j: (0, c)),
                  pl.BlockSpec((1, cw), lambda c, j: (0, c))],
        out_specs=pl.BlockSpec((R, cw), lambda c, j: (j, c)),
        scratch_shapes=[pltpu.VMEM((nlt, H + R, LANE), F32)], name=name, compiler_params=_cp(2),
    )(hp, w, b)


def _f_bwd(dact, hp, w, b, name):
    S = hp.shape[0]
    R, H, cw, nch = R_SEQ, SUB, _CW_F, S // R_SEQ
    nlt = cw // LANE

    def body(da_ref, h_ref, hh_ref, w_ref, b_ref, dh_ref, dw_ref, db_ref, ext_g, ext_d, acc):
        j = pl.program_id(1)
        jj = nch - 1 - j

        @pl.when(j == 0)
        def _():
            ext_d[:, R:R + H, :] = jnp.zeros((nlt, H, LANE), F32)
            acc[...] = jnp.zeros_like(acc)

        for lt in range(nlt):
            ext_g[lt, 0:H, :] = jnp.where(jj == 0, 0.0, hh_ref[_HB - H:_HB, lt * LANE:(lt + 1) * LANE].astype(F32))

        def stage(r0, lt):
            ext_g[lt, pl.ds(pl.multiple_of(r0 + H, SUB), _RB), :] = h_ref[pl.ds(r0, _RB), _lanes(lt)].astype(F32)

        def first(r0, lt):
            ls, lu, rs = _lanes(lt), _lanes(lt + nlt), pl.ds(r0, _RB)
            taps = [ext_g[lt, pl.ds(r0 + (H - (CONV_F - 1 - k)), _RB), :] for k in range(CONV_F)]
            gp = b_ref[:, ls]
            for k in range(CONV_F):
                gp = gp + w_ref[k:k + 1, ls] * taps[k]
            gel, dgel = _gelu(gp, with_grad=True)
            da = da_ref[rs, ls].astype(F32)
            dh_ref[rs, lu] = (da * gel).astype(BF16)
            dgp = da * h_ref[rs, lu].astype(F32) * dgel
            ext_d[lt, rs, :] = dgp
            acc[CONV_F * SUB:(CONV_F + 1) * SUB, ls] += _psum8(dgp)
            for k in range(CONV_F):
                acc[k * SUB:(k + 1) * SUB, ls] += _psum8(dgp * taps[k])

        def second(r0, lt):
            ls = _lanes(lt)
            dhg = w_ref[CONV_F - 1:CONV_F, ls] * ext_d[lt, pl.ds(r0, _RB), :]
            for k in range(CONV_F - 1):
                dhg = dhg + w_ref[k:k + 1, ls] * ext_d[lt, pl.ds(r0 + (CONV_F - 1 - k), _RB), :]
            dh_ref[pl.ds(r0, _RB), ls] = dhg.astype(BF16)

        _sub_blocks(R, cw, stage)
        _sub_blocks(R, cw, first)
        _sub_blocks(R, cw, second)
        ext_d[:, R:R + H, :] = ext_d[:, 0:H, :]

        @pl.when(j == nch - 1)
        def _():
            for k in range(CONV_F):
                dw_ref[k:k + 1, :] = jnp.sum(acc[k * SUB:(k + 1) * SUB, :], axis=0, keepdims=True)
            db_ref[...] = jnp.sum(acc[CONV_F * SUB:(CONV_F + 1) * SUB, :], axis=0, keepdims=True)

    rows = lambda c, j: (nch - 1 - j, c)
    return pl.pallas_call(
        body, out_shape=[SDS((S, 2 * D_FF), BF16), SDS((CONV_F, D_FF), F32), SDS((1, D_FF), F32)],
        grid=(D_FF // cw, nch),
        in_specs=[pl.BlockSpec((R, cw), rows), pl.BlockSpec((R, 2 * cw), rows),
                  pl.BlockSpec((_HB, 2 * cw), lambda c, j: (jnp.maximum((nch - 1 - j) * (R // _HB) - 1, 0), c)),
                  pl.BlockSpec((CONV_F, cw), lambda c, j: (0, c)), pl.BlockSpec((1, cw), lambda c, j: (0, c))],
        out_specs=[pl.BlockSpec((R, 2 * cw), rows), pl.BlockSpec((CONV_F, cw), lambda c, j: (0, c)),
                   pl.BlockSpec((1, cw), lambda c, j: (0, c))],
        scratch_shapes=[pltpu.VMEM((nlt, H + R, LANE), F32), pltpu.VMEM((nlt, R + H, LANE), F32),
                        pltpu.VMEM(((CONV_F + 1) * SUB, cw), F32)], name=name,
        compiler_params=_cp(2),
    )(dact, hp, hp, w, b)


_CW_C = 256
_H_C = 32


def _c_fwd(h1p, w, b):
    S = h1p.shape[0]
    R, H, cw = R_SEQ, _H_C, _CW_C
    nlt = cw // LANE

    def body(h_ref, w_ref, b_ref, o_ref, ext):
        j = pl.program_id(1)

        @pl.when(j == 0)
        def _():
            ext[:, 0:H, :] = jnp.zeros((nlt, H, LANE), F32)

        def stage(r0, lt):
            rs = pl.ds(r0, _RB)
            gate = h_ref[rs, _lanes(lt + nlt)].astype(F32)
            ext[lt, pl.ds(pl.multiple_of(r0 + H, SUB), _RB), :] = h_ref[rs, _lanes(lt)].astype(F32) * _sigmoid(gate)

        def main(r0, lt):
            ls = _lanes(lt)
            cv = b_ref[:, ls]
            for k in range(CONV_C):
                cv = cv + w_ref[k:k + 1, ls] * ext[lt, pl.ds(r0 + (H - (CONV_C - 1 - k)), _RB), :]
            o_ref[pl.ds(r0, _RB), ls] = cv

        _sub_blocks(R, cw, stage)
        _sub_blocks(R, cw, main)
        ext[:, 0:H, :] = ext[:, R:R + H, :]

    return pl.pallas_call(
        body, out_shape=SDS((S, D), F32), grid=(D // cw, S // R),
        in_specs=[pl.BlockSpec((R, 2 * cw), lambda c, j: (j, c)), pl.BlockSpec((CONV_C, cw), lambda c, j: (0, c)),
                  pl.BlockSpec((1, cw), lambda c, j: (0, c))],
        out_specs=pl.BlockSpec((R, cw), lambda c, j: (j, c)),
        scratch_shapes=[pltpu.VMEM((nlt, H + R, LANE), F32)], name="conf_conv_fwd", compiler_params=_cp(2),
    )(h1p, w, b)


def _c_bwd(dcv, h1p, w):
    S = h1p.shape[0]
    R, H, cw, nch = R_SEQ, _H_C, _CW_C, S // R_SEQ
    nlt = cw // LANE
    a_b, a_val, a_gate = CONV_C * SUB, (CONV_C + 1) * SUB, (CONV_C + 2) * SUB

    def body(dc_ref, h_ref, hh_ref, w_ref, dh_ref, dw_ref, db_ref, db1_ref, ext_u, ext_d, acc):
        j = pl.program_id(1)
        jj = nch - 1 - j

        @pl.when(j == 0)
        def _():
            ext_d[:, R:R + H, :] = jnp.zeros((nlt, H, LANE), F32)
            acc[...] = jnp.zeros_like(acc)

        for lt in range(nlt):
            ext_u[lt, 0:H, :] = jnp.where(
                jj == 0, 0.0, hh_ref[:, lt * LANE:(lt + 1) * LANE].astype(F32)
                * _sigmoid(hh_ref[:, cw + lt * LANE:cw + (lt + 1) * LANE].astype(F32)))

        def stage(r0, lt):
            rs, ls = pl.ds(r0, _RB), _lanes(lt)
            gate = h_ref[rs, _lanes(lt + nlt)].astype(F32)
            ext_u[lt, pl.ds(pl.multiple_of(r0 + H, SUB), _RB), :] = h_ref[rs, ls].astype(F32) * _sigmoid(gate)
            ext_d[lt, rs, :] = dc_ref[rs, ls]

        def first(r0, lt):
            ls = _lanes(lt)
            dc = dc_ref[pl.ds(r0, _RB), ls]
            acc[a_b:a_b + SUB, ls] += _psum8(dc)
            for k in range(CONV_C):
                tap = ext_u[lt, pl.ds(r0 + (H - (CONV_C - 1 - k)), _RB), :]
                acc[k * SUB:(k + 1) * SUB, ls] += _psum8(dc * tap)

        def second(r0, lt):
            rs, ls, lg = pl.ds(r0, _RB), _lanes(lt), _lanes(lt + nlt)
            du = w_ref[CONV_C - 1:CONV_C, ls] * ext_d[lt, rs, :]
            for k in range(CONV_C - 1):
                du = du + w_ref[k:k + 1, ls] * ext_d[lt, pl.ds(r0 + (CONV_C - 1 - k), _RB), :]
            val = h_ref[rs, ls].astype(F32)
            sg = _sigmoid(h_ref[rs, lg].astype(F32))
            dval = du * sg
            dgate = du * val * sg * (1.0 - sg)
            acc[a_val:a_val + SUB, ls] += _psum8(dval)
            acc[a_gate:a_gate + SUB, ls] += _psum8(dgate)
            dh_ref[rs, ls] = dval.astype(BF16)
            dh_ref[rs, lg] = dgate.astype(BF16)

        _sub_blocks(R, cw, stage)
        _sub_blocks(R, cw, first)
        _sub_blocks(R, cw, second)
        ext_d[:, R:R + H, :] = ext_d[:, 0:H, :]

        @pl.when(j == nch - 1)
        def _():
            for k in range(CONV_C):
                dw_ref[k:k + 1, :] = jnp.sum(acc[k * SUB:(k + 1) * SUB, :], axis=0, keepdims=True)
            db_ref[...] = jnp.sum(acc[a_b:a_b + SUB, :], axis=0, keepdims=True)
            db1_ref[:, 0:cw] = jnp.sum(acc[a_val:a_val + SUB, :], axis=0, keepdims=True)
            db1_ref[:, cw:2 * cw] = jnp.sum(acc[a_gate:a_gate + SUB, :], axis=0, keepdims=True)

    rows = lambda c, j: (nch - 1 - j, c)
    return pl.pallas_call(
        body, out_shape=[SDS((S, 2 * D), BF16), SDS((CONV_C, D), F32), SDS((1, D), F32), SDS((1, 2 * D), F32)],
        grid=(D // cw, nch),
        in_specs=[pl.BlockSpec((R, cw), rows), pl.BlockSpec((R, 2 * cw), rows),
                  pl.BlockSpec((H, 2 * cw), lambda c, j: (jnp.maximum((nch - 1 - j) * (R // H) - 1, 0), c)),
                  pl.BlockSpec((CONV_C, cw), lambda c, j: (0, c))],
        out_specs=[pl.BlockSpec((R, 2 * cw), rows), pl.BlockSpec((CONV_C, cw), lambda c, j: (0, c)),
                   pl.BlockSpec((1, cw), lambda c, j: (0, c)), pl.BlockSpec((1, 2 * cw), lambda c, j: (0, c))],
        scratch_shapes=[pltpu.VMEM((nlt, H + R, LANE), F32), pltpu.VMEM((nlt, R + H, LANE), F32),
                        pltpu.VMEM(((CONV_C + 3) * SUB, cw), F32)], name="conf_conv_bwd",
        compiler_params=_cp(2),
    )(dcv, h1p, h1p, w)


def _local_step(x, mem, tgt, W, fetch=None, send=None):
    G = {}
    W = dict(W)

    def arrive(group, after):
        if fetch is not None:
            for key, val in fetch(group, after).items():
                W[key] = {**W.get(key, {}), **val} if isinstance(val, dict) else val

    def gain(g, tok):
        return g if tok is None else g + tok

    def sent(group):
        return None if send is None else send(group, G)

    def xattn_fwd(xin, l):
        n = _rms_fwd(xin, W["xa_norm"][l:l + 1], f"xa_norm_fwd{l}")
        arrive(("xa", l), n)
        q = _mm_nn(n, W["xa_wq"][l], out_dtype=BF16, name=f"xa_q{l}")
        mn = _rms_fwd(mem, W["xa_mem_norm"][l:l + 1], f"xa_memnorm_fwd{l}")
        k = _mm_nn(mn, W["xa_wk"][l], out_dtype=BF16, name=f"xa_k{l}")
        v = _mm_nn(mn, W["xa_wv"][l], out_dtype=BF16, name=f"xa_v{l}")
        o = _attn_fwd(q, k, v, f"xa_attn_fwd{l}")
        xout = _mm_nn(o, W["xa_wo"][l], out_dtype=F32, name=f"xa_o{l}", add=xin)
        return xout, (xin, n, q, mn, k, v, o)

    def xattn_bwd(dx, dxb, saved, l):
        xin, n, q, mn, k, v, o = saved
        do = _mm_nt(dxb, W["xa_wo"][l], out_dtype=BF16, name=f"xa_do{l}")
        G[f"xa_wo{l}"] = _mm_tn(o, dxb, out_dtype=BF16, name=f"xa_dwo{l}")
        dq, dk, dv = _attn_bwd(q, k, v, do, f"xa_attn_bwd{l}")
        dkb, dvb = dk.astype(BF16), dv.astype(BF16)
        G[f"xa_wq{l}"] = _mm_tn(n, dq, out_dtype=BF16, name=f"xa_dwq{l}")
        G[f"xa_wk{l}"] = _mm_tn(mn, dkb, out_dtype=BF16, name=f"xa_dwk{l}")
        G[f"xa_wv{l}"] = _mm_tn(mn, dvb, out_dtype=BF16, name=f"xa_dwv{l}")
        tok = sent(("xa", l))
        dmn = _mm_nt(dkb, W["xa_wk"][l], out_dtype=F32, name=f"xa_dmn_k{l}")
        dmn = _mm_nt(dvb, W["xa_wv"][l], out_dtype=F32, name=f"xa_dmn_v{l}", add=dmn)
        (G[f"xa_mem_norm{l}"],) = _rms_bwd(mem, W["xa_mem_norm"][l:l + 1], dmn, None, f"xa_memnorm_bwd{l}")
        dn = _mm_nt(dq, W["xa_wq"][l], out_dtype=F32, name=f"xa_dn{l}")
        dx, dxb, G[f"xa_norm{l}"] = _rms_bwd(xin, gain(W["xa_norm"][l:l + 1], tok), dn, dx, f"xa_norm_bwd{l}")
        return dx, dxb

    def ffn_fwd(xin, l):
        n = _rms_fwd(xin, W["f_norm"][l:l + 1], f"f_norm_fwd{l}")
        arrive(("f", l), n)
        hp = _mm_nn(n, W["f_w_up"][l], out_dtype=BF16, name=f"f_up{l}")
        act = _f_fwd(hp, W["f_dw_w"][l], W["f_dw_b"][l:l + 1], f"f_conv_fwd{l}")
        xout = _mm_nn(act, W["f_w_down"][l], out_dtype=F32, name=f"f_down{l}", add=xin)
        return xout, (xin, n, hp, act)

    def ffn_bwd(dx, dxb, saved, l):
        xin, n, hp, act = saved
        dact = _mm_nt(dxb, W["f_w_down"][l], out_dtype=BF16, name=f"f_dact{l}")
        G[f"f_w_down{l}"] = _mm_tn(act, dxb, out_dtype=BF16, name=f"f_dwdown{l}")
        dhp, G[f"f_dw_w{l}"], G[f"f_dw_b{l}"] = _f_bwd(dact, hp, W["f_dw_w"][l], W["f_dw_b"][l:l + 1], f"f_conv_bwd{l}")
        G[f"f_w_up{l}"] = _mm_tn(n, dhp, out_dtype=BF16, name=f"f_dwup{l}", blocks=_CW_F)
        tok = sent(("f", l))
        dn = _mm_nt(dhp, W["f_w_up"][l], out_dtype=F32, name=f"f_dn{l}")
        dx, dxb, G[f"f_norm{l}"] = _rms_bwd(xin, gain(W["f_norm"][l:l + 1], tok), dn, dx, f"f_norm_bwd{l}")
        return dx, dxb

    n0 = _rms_fwd(x, W["ab_norm"], "ab_norm_fwd")
    arrive(("ab", 0), n0)
    a_par = (W["a_conv_w"], W["a_conv_b"], W["a_gate_x_w"], W["a_gate_x_b"], W["a_gate_a_w"], W["a_gate_a_b"],
             W["a_lambda"])
    b_par = (W["b_group_w"], W["b_group_b"], W["b_scale"])
    zp = _mm_nn(n0, W["ab_w_in"], out_dtype=BF16, name="ab_in")
    yab, h_a = _a_fwd(zp, *a_par)
    yab = _b_fwd(zp, yab, *b_par)
    x1 = _mm_nn(yab, W["ab_w_out"], out_dtype=F32, name="ab_out", add=x)
    x2, s_xa0 = xattn_fwd(x1, 0)
    x3, s_f0 = ffn_fwd(x2, 0)
    n3 = _rms_fwd(x3, W["c_norm"], "c_norm_fwd")
    arrive(("c", 0), n3)
    h1p = _mm_nn(n3, W["c_w_pw1"], out_dtype=BF16, name="c_pw1", bias=W["c_b_pw1"])
    cv = _c_fwd(h1p, W["c_dw_w"], W["c_dw_b"])
    sc = _ln_silu_fwd(cv, W["c_ln_g"], W["c_ln_b"])
    x4 = _mm_nn(sc, W["c_w_pw2"], out_dtype=F32, name="c_pw2", bias=W["c_b_pw2"], add=x3)
    x5, s_xa1 = xattn_fwd(x4, 1)
    x6, s_f1 = ffn_fwd(x5, 1)
    loss, dx, dxb, G["final_norm"] = _loss_head(x6, W["final_norm"], tgt)

    dx, dxb = ffn_bwd(dx, dxb, s_f1, 1)
    dx, dxb = xattn_bwd(dx, dxb, s_xa1, 1)
    dsc = _mm_nt(dxb, W["c_w_pw2"], out_dtype=BF16, name="c_dsc")
    G["c_w_pw2"] = _mm_tn(sc, dxb, out_dtype=BF16, name="c_dwpw2")
    dcv, G["c_ln_g"], G["c_ln_b"], G["c_b_pw2"] = _ln_silu_bwd(dsc, cv, W["c_ln_g"], W["c_ln_b"], dx)
    dh1p, G["c_dw_w"], G["c_dw_b"], G["c_b_pw1"] = _c_bwd(dcv, h1p, W["c_dw_w"])
    G["c_w_pw1"] = _mm_tn(n3, dh1p, out_dtype=BF16, name="c_dwpw1", blocks=_CW_C)
    tok = sent(("c", 0))
    dn3 = _mm_nt(dh1p, W["c_w_pw1"], out_dtype=F32, name="c_dn")
    dx, dxb, G["c_norm"] = _rms_bwd(x3, gain(W["c_norm"], tok), dn3, dx, "c_norm_bwd")
    dx, dxb = ffn_bwd(dx, dxb, s_f0, 0)
    dx, dxb = xattn_bwd(dx, dxb, s_xa0, 0)
    dyab = _mm_nt(dxb, W["ab_w_out"], out_dtype=BF16, name="ab_dyab")
    G["ab_w_out"] = _mm_tn(yab, dxb, out_dtype=BF16, name="ab_dwout")
    tok = sent(("ab", 1))
    a_par = (a_par[0], gain(a_par[1], tok)) + a_par[2:]
    (dzp, G["a_conv_w"], G["a_conv_b"], G["a_gate_x_w"], G["a_gate_x_b"], G["a_gate_a_w"], G["a_gate_a_b"],
     G["a_lambda"]) = _a_bwd(dyab, zp, h_a, *a_par)
    dzp, G["b_group_w"], G["b_group_b"], G["b_scale"] = _b_bwd(dyab, zp, dzp, *b_par)
    G["ab_w_in"] = _mm_tn(n0, dzp, out_dtype=BF16, name="ab_dwin")
    dn0 = _mm_nt(dzp, W["ab_w_in"], out_dtype=F32, name="ab_dn")
    dx, _, G["ab_norm"] = _rms_bwd(x, W["ab_norm"], dn0, dx, "ab_norm_bwd")
    return loss, dx, G


def _my_place():
    x, y, c = lax.axis_index("x"), lax.axis_index("y"), lax.axis_index("c")
    return x, y, c


def _all_gather(shards, name):
    n = len(shards)

    def body(*refs):
        ins, outs = refs[:n], refs[n:2 * n]
        send_sems, recv_sems, local_sems = refs[2 * n:]
        x, y, c = _my_place()
        me, sibling = (x, y, c), (x, y, 1 - c)
        chips = [(1 - x, y), (x, 1 - y), (1 - x, 1 - y)]

        def slab(a, place):
            px, py, pc = place
            return outs[a].at[4 * px + 2 * py + pc]

        def copy(a, k, block, to, src=None):
            return pltpu.make_async_remote_copy(
                src_ref=slab(a, block) if src is None else src, dst_ref=slab(a, block),
                send_sem=send_sems.at[a, k], recv_sem=recv_sems.at[a, k], device_id=to, device_id_type=MESH)

        mine = [pltpu.make_async_copy(ins[a], slab(a, me), local_sems.at[a]) for a in range(n)]
        for cp in mine:
            cp.start()
        first = []
        for j, chip in enumerate(chips):
            first += [copy(a, 1 + j, me, (*chip, c), src=ins[a]) for a in range(n)]
        first += [copy(a, 0, me, sibling, src=ins[a]) for a in range(n)]
        for cp in first:
            cp.start()
        passed = []
        for j, chip in enumerate(chips):
            for a in range(n):
                copy(a, 1 + j, (*chip, c), me).wait_recv()
                cp = copy(a, 4 + j, (*chip, c), sibling)
                cp.start()
                passed.append(cp)
        for a in range(n):
            copy(a, 0, sibling, me).wait_recv()
        for j, chip in enumerate(chips):
            for a in range(n):
                copy(a, 4 + j, (*chip, 1 - c), me).wait_recv()
        for cp in first + passed:
            cp.wait_send()
        for cp in mine:
            cp.wait()

    any_spec = pl.BlockSpec(memory_space=pl.ANY)
    return pl.pallas_call(
        body, out_shape=[SDS((N_DEV,) + s.shape, s.dtype) for s in shards], in_specs=[any_spec] * n,
        out_specs=[any_spec] * n,
        scratch_shapes=[pltpu.SemaphoreType.DMA((n, 7)), pltpu.SemaphoreType.DMA((n, 7)), pltpu.SemaphoreType.DMA((n,))],
        name=name,
    )(*shards)


_HBM = pl.BlockSpec(memory_space=pltpu.HBM)
_SEM = pl.BlockSpec(memory_space=pltpu.SEMAPHORE)
_EFFECT = pltpu.SideEffectType.DATAFLOW_SIDE_EFFECTING


def _peer_places():
    x, y, c = _my_place()
    peers = []
    for k in range(1, N_DEV):
        px = 1 - x if (k >> 2) & 1 else x
        py = 1 - y if (k >> 1) & 1 else y
        pc = 1 - c if k & 1 else c
        peers.append(((px, py, pc), 4 * px + 2 * py + pc))
    return (x, y, c), 4 * x + 2 * y + c, peers


def _send_start(srcs, per_dest, name):
    n = len(srcs)
    lands = [lax.empty((N_DEV,) + (s.shape[1:] if per_dest else s.shape), s.dtype) for s in srcs]

    def body(*refs):
        src, land = refs[:n], refs[n:2 * n]
        outs = refs[2 * n:]
        send, recv, token = outs[:n], outs[n:2 * n], outs[4 * n]
        _, me, peers = _peer_places()
        for a in range(n):
            for peer, pidx in peers:
                pltpu.make_async_remote_copy(
                    src_ref=src[a].at[pidx] if per_dest else src[a], dst_ref=land[a].at[me], send_sem=send[a],
                    recv_sem=recv[a], device_id=peer, device_id_type=MESH).start()
        token[...] = jnp.zeros_like(token)

    hbm = lambda a: pltpu.HBM(a.shape, a.dtype)
    sem = pltpu.SemaphoreType.DMA(())
    res = pl.pallas_call(
        body, name=name,
        out_shape=tuple([sem] * (2 * n) + [hbm(s) for s in srcs] + [hbm(l) for l in lands]
                        + [SDS((SUB, LANE), F32)]),
        in_specs=[_HBM] * (2 * n),
        out_specs=tuple([_SEM] * (2 * n) + [_HBM] * (2 * n) + [pl.BlockSpec(memory_space=pltpu.VMEM)]),
        input_output_aliases={i: 2 * n + i for i in range(2 * n)},
        compiler_params=pltpu.CompilerParams(has_side_effects=_EFFECT),
    )(*[pltpu.with_memory_space_constraint(s, pltpu.HBM) for s in srcs],
      *[pltpu.with_memory_space_constraint(l, pltpu.HBM) for l in lands])
    return res[:n], res[n:2 * n], res[2 * n:3 * n], res[3 * n:4 * n], res[4 * n]


def _send_wait(send, recv, srcs, lands, after, per_dest, name):
    n = len(srcs)

    def body(*refs):
        src, land = refs[:n], refs[n:2 * n]
        send_s, recv_s = refs[2 * n:3 * n], refs[3 * n:4 * n]
        place, _, _ = _peer_places()
        for a in range(n):
            seven = land[a].at[pl.ds(0, N_DEV - 1)]
            copy = pltpu.make_async_remote_copy(
                src_ref=src[a].at[pl.ds(0, N_DEV - 1)] if per_dest else seven, dst_ref=seven, send_sem=send_s[a],
                recv_sem=recv_s[a], device_id=place, device_id_type=MESH)
            copy.wait_send()
            copy.wait_recv()

    hbm = lambda a: pltpu.HBM(a.shape, a.dtype)
    res = pl.pallas_call(
        body, name=name, out_shape=tuple([hbm(s) for s in srcs] + [hbm(l) for l in lands]),
        in_specs=[_HBM] * (2 * n) + [_SEM] * (2 * n) + [pl.BlockSpec(memory_space=pl.ANY)],
        out_specs=tuple([_HBM] * (2 * n)), input_output_aliases={i: i for i in range(2 * n)},
        compiler_params=pltpu.CompilerParams(has_side_effects=_EFFECT),
    )(*srcs, *lands, *send, *recv, after)
    return res[:n], res[n:]


def _adamw_math(w, g, m, v):
    m = ADAM_B1 * m + (1.0 - ADAM_B1) * g
    v = ADAM_B2 * v + (1.0 - ADAM_B2) * (g * g)
    m_hat = m / (1.0 - ADAM_B1 ** ADAM_STEP)
    v_hat = v / (1.0 - ADAM_B2 ** ADAM_STEP)
    delta = -ADAM_LR * (m_hat / (jnp.sqrt(v_hat) + ADAM_EPS) + ADAM_WD * w)
    return delta, m, v


def _row_tile(r, c, itemsize_rows):
    cap = max(SUB, (itemsize_rows // (4 * c)) // SUB * SUB)
    if r <= cap:
        return r
    best = None
    for t in range(SUB, cap + 1, SUB):
        if r % t == 0:
            best = t
    return best if best is not None else r


def _sum_adamw(landing, w, m, v, name, layer=0, prev=None):
    _, r, c = landing.shape
    tr = _row_tile(r, c, 1 << 20)
    off = layer * (r // tr)

    def body(l_ref, w_ref, m_ref, v_ref, *rest):
        g_ref, d_ref, mo_ref, vo_ref = rest[-4:]
        g = l_ref[0].astype(F32)
        for s in range(1, N_DEV):
            g = g + l_ref[s].astype(F32)
        g_ref[...] = g
        d_ref[...], mo_ref[...], vo_ref[...] = _adamw_math(w_ref[...], g, m_ref[...], v_ref[...])

    blk = pl.BlockSpec((tr, c), lambda i: (i + off, 0))
    n_prev = 0 if prev is None else 4
    return pl.pallas_call(
        body, out_shape=[SDS(w.shape, F32)] * 4, grid=(r // tr,),
        in_specs=[pl.BlockSpec((N_DEV, tr, c), lambda i: (0, i, 0)), blk, blk, blk]
        + [pl.BlockSpec(memory_space=pl.ANY)] * n_prev,
        out_specs=[blk] * 4, input_output_aliases={4 + i: i for i in range(n_prev)}, name=name,
        compiler_params=_cp(1),
    )(landing, w, m, v, *([] if prev is None else prev))


def _sum8(landing, name):
    _, r, c = landing.shape

    def body(l_ref, g_ref):
        g = l_ref[0]
        for s in range(1, N_DEV):
            g = g + l_ref[s]
        g_ref[...] = g

    return pl.pallas_call(body, out_shape=SDS((r, c), F32), name=name, compiler_params=_cp(0))(landing)


def _adamw(g, w, m, v, name):
    r, c = g.shape
    tr = _row_tile(r, c, 1 << 20)

    def body(g_ref, w_ref, m_ref, v_ref, d_ref, mo_ref, vo_ref):
        d_ref[...], mo_ref[...], vo_ref[...] = _adamw_math(w_ref[...], g_ref[...], m_ref[...], v_ref[...])

    blk = pl.BlockSpec((tr, c), lambda i: (i, 0))
    return pl.pallas_call(body, out_shape=[SDS((r, c), F32)] * 3, grid=(r // tr,), in_specs=[blk] * 4,
                          out_specs=[blk] * 3, name=name, compiler_params=_cp(1))(g, w, m, v)


_BIG = {
    "ab_w_in": (1, D, 320), "ab_w_out": (1, 192, D), "c_w_pw1": (1, D, 256), "c_w_pw2": (1, 128, D),
    "xa_wq": (2, 128, D), "xa_wk": (2, 128, D), "xa_wv": (2, 128, D), "xa_wo": (2, 128, D),
    "f_w_up": (2, D, 768), "f_w_down": (2, 384, D),
}
_SMALL_SHARDED = {
    "a_conv_w": (1, 4, 128), "c_norm": (1, 128), "c_b_pw1": (1, 256), "c_dw_w": (1, 31, 128), "c_dw_b": (1, 128),
    "c_ln_g": (1, 128), "c_ln_b": (1, 128), "c_b_pw2": (1, 128), "f_dw_w": (2, 3, 384),
}
_REPL = {
    "ab_norm": (1, D), "a_conv_b": (1, D), "a_gate_x_w": (1, 8, 128, 128), "a_gate_x_b": (1, D),
    "a_gate_a_w": (1, 8, 128, 128), "a_gate_a_b": (1, D), "a_lambda": (1, D), "b_group_w": (1, 4, 128, 128),
    "b_group_b": (1, 512), "b_scale": (1, 512), "xa_norm": (2, D), "xa_mem_norm": (2, D), "f_norm": (2, D),
    "f_dw_b": (2, D_FF), "final_norm": (D,),
}


def _size(shape):
    n = 1
    for s in shape:
        n *= s
    return n


_N_SS = sum(_size(s) for s in _SMALL_SHARDED.values())
_N_REPL = sum(_size(s) for s in _REPL.values())
_REPL_ROWS = -(-_N_REPL // (N_DEV * SUB * LANE)) * SUB
_SS_ROWS = _N_SS // LANE
_SMALL_ROWS = -(-(_REPL_ROWS + _SS_ROWS) // SUB) * SUB


def _pack(parts, rows):
    flat = jnp.concatenate([p.reshape(-1).astype(F32) for p in parts])
    return jnp.pad(flat, (0, rows * LANE - flat.shape[0])).reshape(rows, LANE)


def _unpack(buf, table):
    flat, out, off = buf.reshape(-1), {}, 0
    for name, shape in table.items():
        n = _size(shape)
        out[name] = flat[off:off + n].reshape(shape)
        off += n
    return out


def _w_in_to_tiles(w):
    K = w.shape[0]
    gr = jnp.stack([w[:, :D].reshape(K, 8, HD_A), w[:, D:2 * D].reshape(K, 8, HD_A)], axis=2)
    return jnp.concatenate([gr.reshape(K, 2 * D), w[:, 2 * D:]], axis=1)


def _w_in_from_tiles(w):
    K = w.shape[0]
    gr = w[:, :2 * D].reshape(K, 8, 2, HD_A)
    return jnp.concatenate([gr[:, :, 0].reshape(K, D), gr[:, :, 1].reshape(K, D), w[:, 2 * D:]], axis=1)


def _pair_blocks(v, bw):
    lead, n = v.shape[:-1], v.shape[-1]
    return jnp.swapaxes(v.reshape(lead + (2, n // (2 * bw), bw)), -3, -2).reshape(lead + (n,))


def _unpair_blocks(v, bw):
    lead, n = v.shape[:-1], v.shape[-1]
    return jnp.swapaxes(v.reshape(lead + (n // (2 * bw), 2, bw)), -3, -2).reshape(lead + (n,))


_GROUPS = {
    ("ab", 0): (("ab_w_in", 0), ("ab_w_out", 0)),
    ("xa", 0): (("xa_wq", 0), ("xa_wk", 0), ("xa_wv", 0), ("xa_wo", 0)),
    ("f", 0): (("f_w_up", 0), ("f_w_down", 0)),
    ("c", 0): (("c_w_pw1", 0), ("c_w_pw2", 0)),
    ("xa", 1): (("xa_wq", 1), ("xa_wk", 1), ("xa_wv", 1), ("xa_wo", 1)),
    ("f", 1): (("f_w_up", 1), ("f_w_down", 1)),
}
_SEND_GROUPS = dict(_GROUPS)
_SEND_GROUPS[("ab", 1)] = (("ab_w_out", 0),)
_SEND_GROUPS[("ab", 0)] = (("ab_w_in", 0),)


def _weight_layout(name, g):
    if name == "ab_w_in":
        return _w_in_to_tiles(jnp.swapaxes(g, 0, 1).reshape(D, N_DEV * 320))
    if name in ("c_w_pw1", "f_w_up"):
        return g
    return g.reshape(N_DEV * g.shape[1], D)


def _grad_blocks(name, l, G):
    _, r, c = _BIG[name]
    if name == "ab_w_in":
        return jnp.swapaxes(_w_in_from_tiles(G[name]).reshape(D, N_DEV, 320), 0, 1)
    if name == "c_w_pw1":
        return G[name]
    if name == "f_w_up":
        return G[f"{name}{l}"]
    return (G[name] if _BIG[name][0] == 1 else G[f"{name}{l}"]).reshape(N_DEV, r, c)


def _small_layouts(sm):
    W = {}
    sm = sm.reshape(N_DEV, -1)
    off = 0
    for name, shape in _SMALL_SHARDED.items():
        n = _size(shape)
        blocks = sm[:, off:off + n].reshape((N_DEV,) + shape)
        off += n
        W[name] = jnp.moveaxis(blocks, 0, -2).reshape(shape[:-1] + (N_DEV * shape[-1],))
    W["a_conv_w"], W["c_dw_w"] = W["a_conv_w"][0], W["c_dw_w"][0]
    W["c_b_pw1"] = _pair_blocks(W["c_b_pw1"], _CW_C)
    return W


def _with_own(land, src, me, per_dest):
    own = lax.dynamic_slice_in_dim(src, me, 1, 0) if per_dest else src[None]
    return lax.dynamic_update_slice_in_dim(land, own, me, 0)


def _to_dest_major(g, shape):
    full = g.reshape(shape[:-1] + (N_DEV, shape[-1]))
    return jnp.moveaxis(full, -2, 0).reshape(N_DEV, -1)


def kernel(x, mem, ab_norm, ab_w_in, a_conv_w, a_conv_b, a_gate_x_w, a_gate_x_b, a_gate_a_w, a_gate_a_b, a_lambda, b_group_w, b_group_b, b_scale, ab_w_out, c_norm, c_w_pw1, c_b_pw1, c_dw_w, c_dw_b, c_ln_g, c_ln_b, c_w_pw2, c_b_pw2, xa_norm, xa_mem_norm, xa_wq, xa_wk, xa_wv, xa_wo, f_norm, f_w_up, f_dw_w, f_dw_b, f_w_down, final_norm, loss_target, m_ab_norm, m_ab_w_in, m_a_conv_w, m_a_conv_b, m_a_gate_x_w, m_a_gate_x_b, m_a_gate_a_w, m_a_gate_a_b, m_a_lambda, m_b_group_w, m_b_group_b, m_b_scale, m_ab_w_out, m_c_norm, m_c_w_pw1, m_c_b_pw1, m_c_dw_w, m_c_dw_b, m_c_ln_g, m_c_ln_b, m_c_w_pw2, m_c_b_pw2, m_xa_norm, m_xa_mem_norm, m_xa_wq, m_xa_wk, m_xa_wv, m_xa_wo, m_f_norm, m_f_w_up, m_f_dw_w, m_f_dw_b, m_f_w_down, m_final_norm, v_ab_norm, v_ab_w_in, v_a_conv_w, v_a_conv_b, v_a_gate_x_w, v_a_gate_x_b, v_a_gate_a_w, v_a_gate_a_b, v_a_lambda, v_b_group_w, v_b_group_b, v_b_scale, v_ab_w_out, v_c_norm, v_c_w_pw1, v_c_b_pw1, v_c_dw_w, v_c_dw_b, v_c_ln_g, v_c_ln_b, v_c_w_pw2, v_c_b_pw2, v_xa_norm, v_xa_mem_norm, v_xa_wq, v_xa_wk, v_xa_wv, v_xa_wo, v_f_norm, v_f_w_up, v_f_dw_w, v_f_dw_b, v_f_w_down, v_final_norm):
    args = dict(locals())
    P = {n: args[n] for n in _NAMES}
    M = {n: args["m_" + n] for n in _NAMES}
    V = {n: args["v_" + n] for n in _NAMES}

    me = 4 * lax.axis_index("x") + 2 * lax.axis_index("y") + lax.axis_index("c")

    slots, shards = {}, []
    for grp, members in _GROUPS.items():
        slots[grp] = list(range(len(shards), len(shards) + len(members)))
        shards += [P[name][l].astype(BF16) for name, l in members]
    slots[("ab", 0)].append(len(shards))
    shards.append(_pack([P[n] for n in _SMALL_SHARDED], _SS_ROWS + 4))
    g_send, g_recv, g_src, g_land, token = _send_start(shards, False, "gather_start")
    zero = token[:1, :1]

    def fetch(grp, after):
        idx = slots[grp]
        srcs, lands = _send_wait([g_send[i] for i in idx], [g_recv[i] for i in idx], [g_src[i] for i in idx],
                                 [g_land[i] for i in idx], after, False, f"gather_wait_{grp[0]}{grp[1]}")
        full = [_with_own(land, src, me, False) for land, src in zip(lands, srcs)]
        out = {}
        for (name, l), g in zip(_GROUPS[grp], full):
            w = _weight_layout(name, g)
            if _BIG[name][0] == 1:
                out[name] = w
            else:
                out[name] = {l: w}
        if grp == ("ab", 0):
            out.update(_small_layouts(full[-1]))
        return out

    pending = []

    def send(grp, G):
        members = _SEND_GROUPS[grp]
        res = _send_start([_grad_blocks(name, l, G) for name, l in members], True, f"send_{grp[0]}{grp[1]}")
        pending.append((members, res))
        return res[4][:1, :1]

    W = {n: P[n] for n in _REPL}
    W["ab_norm"] = P["ab_norm"] + zero
    W["final_norm"] = P["final_norm"].reshape(1, D)
    W["a_gate_x_w"], W["a_gate_a_w"], W["b_group_w"] = P["a_gate_x_w"][0], P["a_gate_a_w"][0], P["b_group_w"][0]
    loss, grad_x, G = _local_step(x[0], mem[0], loss_target[0], W, fetch, send)
    loss = lax.psum(loss[0, 0], ("x", "y", "c"))

    Gs = dict(G)
    Gs["c_b_pw1"] = _unpair_blocks(G["c_b_pw1"], _CW_C)
    Gs["f_dw_w"] = jnp.stack([G["f_dw_w0"], G["f_dw_w1"]])
    Gs["a_conv_w"], Gs["c_dw_w"] = G["a_conv_w"][None], G["c_dw_w"][None]
    for n in ("xa_norm", "xa_mem_norm", "f_norm", "f_dw_b"):
        Gs[n] = jnp.concatenate([G[f"{n}0"], G[f"{n}1"]], axis=0)
    for n in ("a_gate_x_w", "a_gate_a_w", "b_group_w"):
        Gs[n] = G[n][None]
    repl_flat = jnp.concatenate([Gs[n].reshape(-1) for n in _REPL])
    repl_rows = jnp.pad(repl_flat, (0, N_DEV * _REPL_ROWS * LANE - _N_REPL)).reshape(N_DEV, _REPL_ROWS, LANE)
    ss_rows = jnp.concatenate([_to_dest_major(Gs[n], s) for n, s in _SMALL_SHARDED.items()], axis=1)
    ss_rows = ss_rows.reshape(N_DEV, _SS_ROWS, LANE)
    small_pack = jnp.concatenate(
        [repl_rows, ss_rows, jnp.zeros((N_DEV, _SMALL_ROWS - _REPL_ROWS - _SS_ROWS, LANE), F32)], axis=1)
    last = _send_start([_grad_blocks("ab_w_in", 0, G), small_pack], True, "send_ab0")
    pending.append(((("ab_w_in", 0), ("small", 0)), last))

    members = [m for mem_, _ in pending for m in mem_]
    cat = [[a for _, res in pending for a in res[i]] for i in range(4)]
    srcs, lands = _send_wait(cat[0], cat[1], cat[2], cat[3], grad_x, True, "send_wait")
    landed = {m: _with_own(land, src, me, True) for m, land, src in zip(members, lands, srcs)}

    out_g, out_d, out_m, out_v = {}, {}, {}, {}
    for name, (layers, r, c) in _BIG.items():
        shape = P[name].shape
        w2, m2, v2 = [t[name].reshape(layers * r, c) for t in (P, M, V)]
        res = None
        for l in range(layers):
            res = _sum_adamw(landed[(name, l)], w2, m2, v2, f"adamw_{name}{l}", layer=l, prev=res)
        out_g[name], out_d[name], out_m[name], out_v[name] = [t.reshape(shape) for t in res]

    small_sum = _sum8(landed[("small", 0)], "sum_small")
    (repl_all,) = _all_gather([small_sum[:_REPL_ROWS]], "gather_small_grads")
    g_repl = _unpack(repl_all, _REPL)
    g_ss = _unpack(small_sum[_REPL_ROWS:_REPL_ROWS + _SS_ROWS], _SMALL_SHARDED)
    table = dict(_REPL)
    table.update(_SMALL_SHARDED)
    rows = -(-(_N_REPL + _N_SS) // (256 * LANE)) * 256
    g_small = dict(g_repl)
    g_small.update(g_ss)
    packs = [_pack([src[n] for n in table], rows) for src in (g_small, P, M, V)]
    res = _adamw(*packs, "adamw_small")
    for out, buf in zip((out_d, out_m, out_v), res):
        out.update(_unpack(buf, table))
    out_g.update(g_small)

    return (loss, grad_x[None], *[out_g[n] for n in _NAMES], *[out_d[n] for n in _NAMES],
            *[out_m[n] for n in _NAMES], *[out_v[n] for n in _NAMES])


_NAMES = ("ab_norm", "ab_w_in", "a_conv_w", "a_conv_b", "a_gate_x_w", "a_gate_x_b", "a_gate_a_w", "a_gate_a_b",
          "a_lambda", "b_group_w", "b_group_b", "b_scale", "ab_w_out", "c_norm", "c_w_pw1", "c_b_pw1", "c_dw_w",
          "c_dw_b", "c_ln_g", "c_ln_b", "c_w_pw2", "c_b_pw2", "xa_norm", "xa_mem_norm", "xa_wq", "xa_wk", "xa_wv",
          "xa_wo", "f_norm", "f_w_up", "f_dw_w", "f_dw_b", "f_w_down", "final_norm")
```

```python
import functools

import jax
import jax.numpy as jnp
from jax import lax
from jax.experimental import pallas as pl
from jax.experimental.pallas import tpu as pltpu

F32, BF16 = jnp.float32, jnp.bfloat16
SDS = jax.ShapeDtypeStruct
MESH = pl.DeviceIdType.MESH

N_DEV = 8
D = 1024
N_MEM = 256
XA_HEADS, XA_HD = 4, 256
HD_A = 128
CONV_A, CONV_C, CONV_F = 4, 31, 3
C_RG = 8.0
POOL_WINDOWS = (2, 4, 8, 16)
D_FF = 3 * D
EPS = 1e-6
ADAM_LR, ADAM_B1, ADAM_B2, ADAM_EPS, ADAM_WD, ADAM_STEP = 0.001, 0.9, 0.999, 1e-08, 0.01, 10

LANE = 128
SUB = 8
VMEM_LIMIT = 56 * 1024 * 1024
R_SEQ = 256
TM_ROW = 512


def _cp(n_axes):
    return pltpu.CompilerParams(dimension_semantics=("arbitrary",) * n_axes, vmem_limit_bytes=VMEM_LIMIT)


def _tile(n, pref):
    if n <= pref:
        return n
    best = None
    for t in range(LANE, pref + 1, LANE):
        if n % t == 0:
            best = t
    assert best is not None, (n, pref)
    return best


def _perm2(n):
    return (n % 2) * 4 + n // 2


_NN = (((1,), (0,)), ((), ()))
_NT = (((1,), (1,)), ((), ()))
_TN = (((0,), (0,)), ((), ()))


def _mm_call(name, grid, a, b, a_spec, b_spec, o_spec, out_shape, dims, acc_shape, extras=()):
    nk = grid[2]
    n_ex = len(extras)

    def finish(r, ex_refs, o_ref):
        for e in ex_refs:
            r = r + e[...]
        o_ref[...] = r.astype(o_ref.dtype)

    def body_one(a_ref, b_ref, *rest):
        finish(lax.dot_general(a_ref[...], b_ref[...], dims, preferred_element_type=F32), rest[:n_ex], rest[n_ex])

    def body_acc(a_ref, b_ref, *rest):
        ex_refs, o_ref, acc = rest[:n_ex], rest[n_ex], rest[n_ex + 1]
        k = pl.program_id(2)

        @pl.when(k == 0)
        def _():
            acc[...] = jnp.zeros_like(acc)

        acc[...] += lax.dot_general(a_ref[...], b_ref[...], dims, preferred_element_type=F32)

        @pl.when(k == nk - 1)
        def _():
            finish(acc[...], ex_refs, o_ref)

    return pl.pallas_call(
        body_one if nk == 1 else body_acc, out_shape=out_shape, grid=grid,
        in_specs=[a_spec, b_spec] + [s for _, s in extras], out_specs=o_spec,
        scratch_shapes=[] if nk == 1 else [pltpu.VMEM(acc_shape, F32)], name=name, compiler_params=_cp(3),
    )(a, b, *[e for e, _ in extras])


_K_WHOLE = 3072


def _mm_nn(a, b, *, out_dtype, name, bias=None, add=None, old=False):
    M, K = a.shape
    if old:
        tm, tk = _tile(M, 1024), _tile(K, 512)
    else:
        tk = K if K <= _K_WHOLE else _tile(K, 1024)
        tm = _tile(M, 1024 if K <= 1024 else 512)
    if b.ndim == 3:
        nb, _, bw = b.shape
        N, tn, nn = nb * bw, bw, nb
        b_spec = pl.BlockSpec((None, tk, bw), lambda m, n, k: (_perm2(n), k, 0))
    else:
        N = b.shape[1]
        tn = _tile(N, 1024)
        nn = N // tn
        b_spec = pl.BlockSpec((tk, tn), lambda m, n, k: (k, n))
    extras = []
    if bias is not None:
        extras.append((bias, pl.BlockSpec((1, tn), lambda m, n, k: (0, n))))
    if add is not None:
        extras.append((add, pl.BlockSpec((tm, tn), lambda m, n, k: (m, n))))
    return _mm_call(name, (M // tm, nn, K // tk), a, b, pl.BlockSpec((tm, tk), lambda m, n, k: (m, k)), b_spec,
                    pl.BlockSpec((tm, tn), lambda m, n, k: (m, n)), SDS((M, N), out_dtype), _NN, (tm, tn), extras)


def _mm_nt(a, b, *, out_dtype, name, add=None, old=False):
    M, N = a.shape
    if b.ndim == 3:
        nb, Ko, bw = b.shape
        tm = _tile(M, 1024)
        tn, tk, nk = _tile(Ko, 1024), bw, nb
        b_spec = pl.BlockSpec((None, tn, bw), lambda m, n, k: (_perm2(k), n, 0))
    else:
        Ko = b.shape[0]
        if old:
            tm, tk = _tile(M, 1024), _tile(N, 512)
        else:
            tk = N if N <= _K_WHOLE else _tile(N, 1024)
            tm = _tile(M, 1024 if N <= 1024 else 512)
        tn = _tile(Ko, 1024)
        nk = N // tk
        b_spec = pl.BlockSpec((tn, tk), lambda m, n, k: (n, k))
    extras = []
    if add is not None:
        extras.append((add, pl.BlockSpec((tm, tn), lambda m, n, k: (m, n))))
    return _mm_call(name, (M // tm, Ko // tn, nk), a, b, pl.BlockSpec((tm, tk), lambda m, n, k: (m, k)), b_spec,
                    pl.BlockSpec((tm, tn), lambda m, n, k: (m, n)), SDS((M, Ko), out_dtype), _NT, (tm, tn), extras)


def _mm_tn(a, b, *, out_dtype, name, blocks=None, old=False):
    S, Ka = a.shape
    Nb = b.shape[1]
    tm, tk = _tile(Ka, 1024), _tile(S, 512 if old else 2048)
    if blocks is not None:
        bw = blocks
        tn, nn = bw, Nb // bw
        o_spec = pl.BlockSpec((None, tm, bw), lambda m, n, k: (_perm2(n), m, 0))
        out_shape = SDS((nn, Ka, bw), out_dtype)
    else:
        tn = _tile(Nb, 1024)
        nn = Nb // tn
        o_spec = pl.BlockSpec((tm, tn), lambda m, n, k: (m, n))
        out_shape = SDS((Ka, Nb), out_dtype)
    return _mm_call(name, (Ka // tm, nn, S // tk), a, b, pl.BlockSpec((tk, tm), lambda m, n, k: (k, m)),
                    pl.BlockSpec((tk, tn), lambda m, n, k: (k, n)), o_spec, out_shape, _TN, (tm, tn))


def _row(tm, c):
    return pl.BlockSpec((tm, c), lambda i: (i, 0))


def _full(shape):
    nd = len(shape)
    return pl.BlockSpec(shape, lambda i: (0,) * nd)


def _rms_fwd(x, g, name):
    S = x.shape[0]
    tm = min(S, TM_ROW)

    def body(x_ref, g_ref, o_ref):
        xf = x_ref[...]
        r = lax.rsqrt(jnp.mean(xf * xf, axis=-1, keepdims=True) + EPS)
        o_ref[...] = ((xf * r) * g_ref[...]).astype(BF16)

    return pl.pallas_call(body, out_shape=SDS((S, D), BF16), grid=(S // tm,), in_specs=[_row(tm, D), _full((1, D))],
                          out_specs=_row(tm, D), name=name, compiler_params=_cp(1))(x, g)


def _rms_bwd(x, g, dn, dres, name):
    S = x.shape[0]
    tm = min(S, TM_ROW)
    want_dx = dres is not None

    def body(x_ref, g_ref, dn_ref, *rest):
        i = pl.program_id(0)
        dg_ref = rest[-1]

        @pl.when(i == 0)
        def _():
            dg_ref[...] = jnp.zeros_like(dg_ref)

        xf = x_ref[...]
        r = lax.rsqrt(jnp.mean(xf * xf, axis=-1, keepdims=True) + EPS)
        y = xf * r
        dn_v = dn_ref[...]
        dg_ref[...] += jnp.sum(dn_v * y, axis=0, keepdims=True)
        if want_dx:
            dres_ref, dx_ref, dxb_ref = rest[0], rest[1], rest[2]
            dy = dn_v * g_ref[...]
            dx = r * (dy - y * jnp.mean(dy * y, axis=-1, keepdims=True)) + dres_ref[...]
            dx_ref[...] = dx
            dxb_ref[...] = dx.astype(BF16)

    ins = [x, g, dn] + ([dres] if want_dx else [])
    in_specs = [_row(tm, D), _full((1, D)), _row(tm, D)] + ([_row(tm, D)] if want_dx else [])
    outs = ([SDS((S, D), F32), SDS((S, D), BF16)] if want_dx else []) + [SDS((1, D), F32)]
    out_specs = ([_row(tm, D), _row(tm, D)] if want_dx else []) + [_full((1, D))]
    return pl.pallas_call(body, out_shape=outs, grid=(S // tm,), in_specs=in_specs, out_specs=out_specs, name=name,
                          compiler_params=_cp(1))(*ins)


def _loss_head(x, g, tgt):
    S = x.shape[0]
    tm = min(S, TM_ROW)

    def body(x_ref, g_ref, t_ref, loss_ref, dx_ref, dxb_ref, dg_ref):
        i = pl.program_id(0)

        @pl.when(i == 0)
        def _():
            loss_ref[...] = jnp.zeros_like(loss_ref)
            dg_ref[...] = jnp.zeros_like(dg_ref)

        xf = x_ref[...]
        r = lax.rsqrt(jnp.mean(xf * xf, axis=-1, keepdims=True) + EPS)
        y = xf * r
        gv = g_ref[...]
        err = y * gv - t_ref[...]
        per_row = jnp.mean(err * err, axis=-1, keepdims=True)
        loss_ref[...] += 0.5 * jnp.sum(per_row, axis=0, keepdims=True)
        dn_v = err * (1.0 / D)
        dg_ref[...] += jnp.sum(dn_v * y, axis=0, keepdims=True)
        dy = dn_v * gv
        dx = r * (dy - y * jnp.mean(dy * y, axis=-1, keepdims=True))
        dx_ref[...] = dx
        dxb_ref[...] = dx.astype(BF16)

    return pl.pallas_call(
        body, out_shape=[SDS((1, 1), F32), SDS((S, D), F32), SDS((S, D), BF16), SDS((1, D), F32)], grid=(S // tm,),
        in_specs=[_row(tm, D), _full((1, D)), _row(tm, D)],
        out_specs=[_full((1, 1)), _row(tm, D), _row(tm, D), _full((1, D))], name="loss_head", compiler_params=_cp(1),
    )(x, g, tgt)


def _softmax_rows(s):
    m = jnp.max(s, axis=-1, keepdims=True)
    e = jnp.exp(s - m)
    return e / jnp.sum(e, axis=-1, keepdims=True)


def _attn_fwd(q, k, v, name):
    S = q.shape[0]
    tm = min(S, TM_ROW)
    scale = XA_HD ** -0.5

    def body(q_ref, k_ref, v_ref, o_ref):
        for h in range(XA_HEADS):
            sl = slice(h * XA_HD, (h + 1) * XA_HD)
            s = lax.dot_general(q_ref[:, sl], k_ref[:, sl], _NT, preferred_element_type=F32) * scale
            p = _softmax_rows(s)
            o_ref[:, sl] = lax.dot_general(p.astype(BF16), v_ref[:, sl], _NN, preferred_element_type=F32).astype(BF16)

    return pl.pallas_call(body, out_shape=SDS((S, D), BF16), grid=(S // tm,),
                          in_specs=[_row(tm, D), _full((N_MEM, D)), _full((N_MEM, D))], out_specs=_row(tm, D),
                          name=name, compiler_params=_cp(1))(q, k, v)


def _attn_bwd(q, k, v, do, name):
    S = q.shape[0]
    tm = min(S, TM_ROW)
    scale = XA_HD ** -0.5

    def body(q_ref, k_ref, v_ref, do_ref, dq_ref, dk_ref, dv_ref):
        i = pl.program_id(0)

        @pl.when(i == 0)
        def _():
            dk_ref[...] = jnp.zeros_like(dk_ref)
            dv_ref[...] = jnp.zeros_like(dv_ref)

        for h in range(XA_HEADS):
            sl = slice(h * XA_HD, (h + 1) * XA_HD)
            qh, kh, vh, doh = q_ref[:, sl], k_ref[:, sl], v_ref[:, sl], do_ref[:, sl]
            s = lax.dot_general(qh, kh, _NT, preferred_element_type=F32) * scale
            p = _softmax_rows(s)
            pb = p.astype(BF16)
            dv_ref[:, sl] += lax.dot_general(pb, doh, _TN, preferred_element_type=F32)
            dp = lax.dot_general(doh, vh, _NT, preferred_element_type=F32)
            ds = (p * (dp - jnp.sum(dp * p, axis=-1, keepdims=True)) * scale).astype(BF16)
            dq_ref[:, sl] = lax.dot_general(ds, kh, _NN, preferred_element_type=F32).astype(BF16)
            dk_ref[:, sl] += lax.dot_general(ds, qh, _TN, preferred_element_type=F32)

    return pl.pallas_call(
        body, out_shape=[SDS((S, D), BF16), SDS((N_MEM, D), F32), SDS((N_MEM, D), F32)], grid=(S // tm,),
        in_specs=[_row(tm, D), _full((N_MEM, D)), _full((N_MEM, D)), _row(tm, D)],
        out_specs=[_row(tm, D), _full((N_MEM, D)), _full((N_MEM, D))], name=name, compiler_params=_cp(1),
    )(q, k, v, do)


def _sigmoid(x):
    return 1.0 / (1.0 + jnp.exp(-x))


def _ln_silu_fwd(cv, g, b):
    S = cv.shape[0]
    tm = min(S, TM_ROW)

    def body(x_ref, g_ref, b_ref, o_ref):
        xf = x_ref[...]
        mu = jnp.mean(xf, axis=-1, keepdims=True)
        xc = xf - mu
        rstd = lax.rsqrt(jnp.mean(xc * xc, axis=-1, keepdims=True) + EPS)
        ln = (xc * rstd) * g_ref[...] + b_ref[...]
        o_ref[...] = (ln * _sigmoid(ln)).astype(BF16)

    return pl.pallas_call(body, out_shape=SDS((S, D), BF16), grid=(S // tm,),
                          in_specs=[_row(tm, D), _full((1, D)), _full((1, D))], out_specs=_row(tm, D),
                          name="ln_silu_fwd", compiler_params=_cp(1))(cv, g, b)


def _ln_silu_bwd(ds, cv, g, b, dx):
    S = cv.shape[0]
    tm = min(S, TM_ROW)

    def body(ds_ref, x_ref, g_ref, b_ref, dx_ref, dcv_ref, dg_ref, db_ref, db2_ref):
        i = pl.program_id(0)

        @pl.when(i == 0)
        def _():
            dg_ref[...] = jnp.zeros_like(dg_ref)
            db_ref[...] = jnp.zeros_like(db_ref)
            db2_ref[...] = jnp.zeros_like(db2_ref)

        xf = x_ref[...]
        mu = jnp.mean(xf, axis=-1, keepdims=True)
        xc = xf - mu
        rstd = lax.rsqrt(jnp.mean(xc * xc, axis=-1, keepdims=True) + EPS)
        xhat = xc * rstd
        gv = g_ref[...]
        ln = xhat * gv + b_ref[...]
        sg = _sigmoid(ln)
        dln = ds_ref[...].astype(F32) * (sg + ln * sg * (1.0 - sg))
        dg_ref[...] += jnp.sum(dln * xhat, axis=0, keepdims=True)
        db_ref[...] += jnp.sum(dln, axis=0, keepdims=True)
        db2_ref[...] += jnp.sum(dx_ref[...], axis=0, keepdims=True)
        dxh = dln * gv
        dcv_ref[...] = rstd * (dxh - jnp.mean(dxh, axis=-1, keepdims=True)
                               - xhat * jnp.mean(dxh * xhat, axis=-1, keepdims=True))

    return pl.pallas_call(
        body, out_shape=[SDS((S, D), F32), SDS((1, D), F32), SDS((1, D), F32), SDS((1, D), F32)], grid=(S // tm,),
        in_specs=[_row(tm, D), _row(tm, D), _full((1, D)), _full((1, D)), _row(tm, D)],
        out_specs=[_row(tm, D), _full((1, D)), _full((1, D)), _full((1, D))], name="ln_silu_bwd",
        compiler_params=_cp(1),
    )(ds, cv, g, b, dx)


_GELU_C, _GELU_K = 0.7978845608028654, 0.044715


def _gelu(x, with_grad=False):
    x2 = x * x
    t = jnp.tanh(_GELU_C * (x + _GELU_K * x * x2))
    gel = 0.5 * x * (1.0 + t)
    if not with_grad:
        return gel
    return gel, 0.5 * (1.0 + t) + 0.5 * x * (1.0 - t * t) * (_GELU_C * (1.0 + 3.0 * _GELU_K * x2))


def _expm1(x):
    poly = x * (1.0 + x * (0.5 + x * (1.0 / 6.0 + x * (1.0 / 24.0 + x * (1.0 / 120.0)))))
    return jnp.where(jnp.abs(x) < 0.05, poly, jnp.exp(x) - 1.0)


def _softplus(x):
    return jnp.maximum(x, 0.0) + jnp.log1p(jnp.exp(-jnp.abs(x)))


_SCAN_UNROLL = 4
_RB = 32
_HB = 16


def _sub_blocks(n_rows, n_lanes, fn):
    def step(idx, c):
        r0 = pl.multiple_of(idx * _RB, _RB)
        for lt in range(n_lanes // LANE):
            fn(r0, lt)
        return c

    lax.fori_loop(0, n_rows // _RB, step, 0)


def _lanes(lt):
    return pl.ds(lt * LANE, LANE)


def _psum8(x):
    parts = [x[i * SUB:(i + 1) * SUB] for i in range(x.shape[0] // SUB)]
    return functools.reduce(lambda p, q: p + q, parts)


def _scan_fwd(a_s, b_s, out_ref, carry_ref, n_groups):
    row = lax.broadcasted_iota(jnp.int32, (SUB, LANE), 0)
    U = _SCAN_UNROLL

    def step(gi, carry):
        base = gi * (SUB * U)
        parts = []
        for u in range(U):
            i = pl.multiple_of(base + u * SUB, SUB)
            a8, b8 = a_s[pl.ds(i, SUB), :], b_s[pl.ds(i, SUB), :]
            for s in (1, 2, 4):
                a_sh = jnp.where(row >= s, pltpu.roll(a8, s, 0), 1.0)
                b_sh = jnp.where(row >= s, pltpu.roll(b8, s, 0), 0.0)
                b8 = a8 * b_sh + b8
                a8 = a8 * a_sh
            parts.append((i, a8, b8))
        for i, a8, b8 in parts:
            h8 = a8 * carry + b8
            out_ref[pl.ds(i, SUB), :] = h8
            carry = jnp.broadcast_to(h8[SUB - 1:SUB, :], (SUB, LANE))
        return carry

    carry_ref[...] = lax.fori_loop(0, n_groups // U, step, carry_ref[...])


def _scan_bwd(a_s, b_s, out_ref, carry_ref, n_groups):
    row = lax.broadcasted_iota(jnp.int32, (SUB, LANE), 0)
    U = _SCAN_UNROLL

    def step(gi, carry):
        base = (n_groups // U - 1 - gi) * (SUB * U)
        parts = []
        for u in reversed(range(U)):
            i = pl.multiple_of(base + u * SUB, SUB)
            a8, b8 = a_s[pl.ds(i, SUB), :], b_s[pl.ds(i, SUB), :]
            for s in (1, 2, 4):
                a_sh = jnp.where(row < SUB - s, pltpu.roll(a8, SUB - s, 0), 1.0)
                b_sh = jnp.where(row < SUB - s, pltpu.roll(b8, SUB - s, 0), 0.0)
                b8 = a8 * b_sh + b8
                a8 = a8 * a_sh
            parts.append((i, a8, b8))
        for i, a8, b8 in parts:
            h8 = a8 * carry + b8
            out_ref[pl.ds(i, SUB), :] = h8
            carry = jnp.broadcast_to(h8[0:1, :], (SUB, LANE))
        return carry

    carry_ref[...] = lax.fori_loop(0, n_groups // U, step, carry_ref[...])


def _rglru_pre(xr, wgx_ref, bgx_ref, wga_ref, bga_ref, lam_ref):
    xrb = xr.astype(BF16)
    wgx, wga = wgx_ref[0].astype(BF16), wga_ref[0].astype(BF16)
    gx = _sigmoid(lax.dot_general(xrb, wgx, _NN, preferred_element_type=F32) + bgx_ref[...])
    ga = _sigmoid(lax.dot_general(xrb, wga, _NN, preferred_element_type=F32) + bga_ref[...])
    sp = _softplus(-lam_ref[...])
    log_a = -C_RG * ga * sp
    a = jnp.exp(log_a)
    mult = jnp.sqrt(-_expm1(2.0 * log_a))
    return gx, ga, sp, a, mult, xrb, wgx, wga


def _a_specs():
    vec = pl.BlockSpec((1, HD_A), lambda c, j: (0, c))
    mat = pl.BlockSpec((1, HD_A, HD_A), lambda c, j: (c, 0, 0))
    return [pl.BlockSpec((CONV_A, HD_A), lambda c, j: (0, c)), vec, mat, vec, mat, vec, vec]


def _a_fwd(zp, conv_w, conv_b, wgx, bgx, wga, bga, lam):
    S = zp.shape[0]
    R, nt = R_SEQ, D // HD_A
    H = SUB

    def body(z_ref, cw_ref, cb_ref, wgx_ref, bgx_ref, wga_ref, bga_ref, lam_ref, ya_ref, h_ref, ext, a_s, b_s, hc):
        j = pl.program_id(1)

        @pl.when(j == 0)
        def _():
            ext[0:H, :] = jnp.zeros((H, HD_A), F32)
            hc[...] = jnp.zeros_like(hc)

        ext[H:H + R, :] = z_ref[:, HD_A:2 * HD_A].astype(F32)
        xr = cb_ref[...]
        for k in range(CONV_A):
            xr = xr + cw_ref[k:k + 1, :] * ext[pl.ds(H - (CONV_A - 1 - k), R), :]
        gx, _, _, a, mult, _, _, _ = _rglru_pre(xr, wgx_ref, bgx_ref, wga_ref, bga_ref, lam_ref)
        a_s[...] = a
        b_s[...] = mult * (gx * xr)
        _scan_fwd(a_s, b_s, h_ref, hc, R // SUB)
        ya_ref[...] = (_gelu(z_ref[:, 0:HD_A].astype(F32)) * h_ref[...]).astype(BF16)
        ext[0:H, :] = ext[R:R + H, :]

    return pl.pallas_call(
        body, out_shape=[SDS((S, D + D // 2), BF16), SDS((S, D), F32)], grid=(nt, S // R),
        in_specs=[pl.BlockSpec((R, 2 * HD_A), lambda c, j: (j, c))] + _a_specs(),
        out_specs=[pl.BlockSpec((R, HD_A), lambda c, j: (j, c)), pl.BlockSpec((R, HD_A), lambda c, j: (j, c))],
        scratch_shapes=[pltpu.VMEM((H + R, HD_A), F32), pltpu.VMEM((R, HD_A), F32), pltpu.VMEM((R, HD_A), F32),
                        pltpu.VMEM((SUB, HD_A), F32)],
        name="rglru_fwd", compiler_params=_cp(2),
    )(zp, conv_w, conv_b, wgx, bgx, wga, bga, lam)


def _a_bwd(dyab, zp, h, conv_w, conv_b, wgx, bgx, wga, bga, lam):
    S = zp.shape[0]
    R, nt, nch = R_SEQ, D // HD_A, S // R_SEQ
    H = SUB

    def rows(c, j):
        return (nch - 1 - j, c)

    def halo(c, j):
        return (jnp.maximum((nch - 1 - j) * (R // H) - 1, 0), c)

    def halo_z(c, j):
        return (jnp.maximum((nch - 1 - j) * (R // _HB) - 1, 0), c)

    def body(dy_ref, z_ref, zh_ref, h_ref, hh_ref, cw_ref, cb_ref, wgx_ref, bgx_ref, wga_ref, bga_ref, lam_ref,
             dz_ref, dcw_ref, dcb_ref, dwgx_ref, dbgx_ref, dwga_ref, dbga_ref, dlam_ref,
             ext_z, ext_h, ext_mu, ext_d, a_s, b_s, muc):
        j = pl.program_id(1)
        first_chunk = (nch - 1 - j) == 0

        @pl.when(j == 0)
        def _():
            ext_mu[R:R + H, :] = jnp.zeros((H, HD_A), F32)
            ext_d[R:R + H, :] = jnp.zeros((H, HD_A), F32)
            muc[...] = jnp.zeros_like(muc)
            for r in (dcw_ref, dcb_ref, dwgx_ref, dbgx_ref, dwga_ref, dbga_ref, dlam_ref):
                r[...] = jnp.zeros_like(r)

        zg = z_ref[:, 0:HD_A].astype(F32)
        ext_z[0:H, :] = jnp.where(first_chunk, 0.0, zh_ref[_HB - H:_HB, HD_A:2 * HD_A].astype(F32))
        ext_z[H:H + R, :] = z_ref[:, HD_A:2 * HD_A].astype(F32)
        ext_h[0:H, :] = jnp.where(first_chunk, 0.0, hh_ref[...])
        ext_h[H:H + R, :] = h_ref[...]
        xr = cb_ref[...]
        for k in range(CONV_A):
            xr = xr + cw_ref[k:k + 1, :] * ext_z[pl.ds(H - (CONV_A - 1 - k), R), :]
        gx, ga, sp, a, mult, xrb, wgxb, wgab = _rglru_pre(xr, wgx_ref, bgx_ref, wga_ref, bga_ref, lam_ref)
        gel, dgel = _gelu(zg, with_grad=True)
        dy = dy_ref[...].astype(F32)
        dh = dy * gel
        dz_ref[:, 0:HD_A] = (dy * h_ref[...] * dgel).astype(BF16)
        a_s[...] = a
        b_s[...] = a * dh
        _scan_bwd(a_s, b_s, ext_mu, muc, R // SUB)
        lam_t = dh + ext_mu[pl.ds(1, R), :]
        ext_mu[R:R + H, :] = ext_mu[0:H, :]
        da = lam_t * ext_h[pl.ds(H - 1, R), :]
        gxr = gx * xr
        dlog_a = da * a - (lam_t * gxr) * (a * a) / mult
        dgx = lam_t * mult * xr
        dxr = lam_t * mult * gx
        lam_v = lam_ref[...]
        dlam_ref[...] += jnp.sum(dlog_a * ga, axis=0, keepdims=True) * (C_RG * _sigmoid(-lam_v))
        dpa = (dlog_a * (-C_RG * sp)) * ga * (1.0 - ga)
        dpx = dgx * gx * (1.0 - gx)
        dbga_ref[...] += jnp.sum(dpa, axis=0, keepdims=True)
        dbgx_ref[...] += jnp.sum(dpx, axis=0, keepdims=True)
        dpab, dpxb = dpa.astype(BF16), dpx.astype(BF16)
        dwga_ref[0] += lax.dot_general(xrb, dpab, _TN, preferred_element_type=F32)
        dwgx_ref[0] += lax.dot_general(xrb, dpxb, _TN, preferred_element_type=F32)
        dxr = (dxr + lax.dot_general(dpab, wgab, _NT, preferred_element_type=F32)
               + lax.dot_general(dpxb, wgxb, _NT, preferred_element_type=F32))
        dcb_ref[...] += jnp.sum(dxr, axis=0, keepdims=True)
        ext_d[0:R, :] = dxr
        dzr = jnp.zeros((R, HD_A), F32)
        for k in range(CONV_A):
            sh = CONV_A - 1 - k
            dcw_ref[k:k + 1, :] += jnp.sum(dxr * ext_z[pl.ds(H - sh, R), :], axis=0, keepdims=True)
            dzr = dzr + cw_ref[k:k + 1, :] * ext_d[pl.ds(sh, R), :]
        dz_ref[:, HD_A:2 * HD_A] = dzr.astype(BF16)
        ext_d[R:R + H, :] = ext_d[0:H, :]

    vec_o = pl.BlockSpec((1, HD_A), lambda c, j: (0, c))
    mat_o = pl.BlockSpec((1, HD_A, HD_A), lambda c, j: (c, 0, 0))
    return pl.pallas_call(
        body,
        out_shape=[SDS((S, 2 * D + D // 2), BF16), SDS((CONV_A, D), F32), SDS((1, D), F32), SDS((nt, HD_A, HD_A), F32),
                   SDS((1, D), F32), SDS((nt, HD_A, HD_A), F32), SDS((1, D), F32), SDS((1, D), F32)],
        grid=(nt, nch),
        in_specs=[pl.BlockSpec((R, HD_A), rows), pl.BlockSpec((R, 2 * HD_A), rows),
                  pl.BlockSpec((_HB, 2 * HD_A), halo_z), pl.BlockSpec((R, HD_A), rows),
                  pl.BlockSpec((H, HD_A), halo)] + _a_specs(),
        out_specs=[pl.BlockSpec((R, 2 * HD_A), rows), pl.BlockSpec((CONV_A, HD_A), lambda c, j: (0, c)), vec_o, mat_o,
                   vec_o, mat_o, vec_o, vec_o],
        scratch_shapes=[pltpu.VMEM((H + R, HD_A), F32), pltpu.VMEM((H + R, HD_A), F32), pltpu.VMEM((R + H, HD_A), F32),
                        pltpu.VMEM((R + H, HD_A), F32), pltpu.VMEM((R, HD_A), F32), pltpu.VMEM((R, HD_A), F32),
                        pltpu.VMEM((SUB, HD_A), F32)],
        name="rglru_bwd", compiler_params=_cp(2),
    )(dyab, zp, zp, h, h, conv_w, conv_b, wgx, bgx, wga, bga, lam)


_POOL_H = 16
_POOL_T0 = 2 * D // HD_A
_POOL_Y0 = D // HD_A


def _pool_mean_minus(u, ext, g, t1):
    R = u.shape[0]
    acc, wins = u, []
    for k in range(1, _POOL_H):
        acc = acc + ext[pl.ds(_POOL_H - k, R), :]
        if k + 1 in POOL_WINDOWS:
            wins.append(acc)
    win = jnp.where(g == 0, wins[0], jnp.where(g == 1, wins[1], jnp.where(g == 2, wins[2], wins[3])))
    return win / jnp.minimum(t1, _pool_width(g)) - u


def _pool_width(g):
    return jnp.where(g == 0, 2.0, jnp.where(g == 1, 4.0, jnp.where(g == 2, 8.0, 16.0)))


def _b_fwd(zp, yab, wg, bg, sc):
    S = zp.shape[0]
    R, H = R_SEQ, _POOL_H

    def body(z_ref, wg_ref, bg_ref, sc_ref, yab_in, yb_ref, ext):
        del yab_in
        g, j = pl.program_id(0), pl.program_id(1)

        @pl.when(j == 0)
        def _():
            ext[0:H, :] = jnp.zeros((H, HD_A), F32)

        u = z_ref[...].astype(F32)
        ext[H:H + R, :] = u
        t1 = (j * R + 1 + lax.broadcasted_iota(jnp.int32, (R, HD_A), 0)).astype(F32)
        p = _pool_mean_minus(u, ext, g, t1)
        lin = lax.dot_general(p.astype(BF16), wg_ref[0].astype(BF16), _NN, preferred_element_type=F32) + bg_ref[...]
        yb_ref[...] = (lin * sc_ref[...]).astype(BF16)
        ext[0:H, :] = ext[R:R + H, :]

    vec = pl.BlockSpec((1, HD_A), lambda g, j: (0, g))
    return pl.pallas_call(
        body, out_shape=SDS(yab.shape, yab.dtype), grid=(len(POOL_WINDOWS), S // R),
        in_specs=[pl.BlockSpec((R, HD_A), lambda g, j: (j, _POOL_T0 + g)),
                  pl.BlockSpec((1, HD_A, HD_A), lambda g, j: (g, 0, 0)), vec, vec, pl.BlockSpec(memory_space=pl.ANY)],
        out_specs=pl.BlockSpec((R, HD_A), lambda g, j: (j, _POOL_Y0 + g)),
        scratch_shapes=[pltpu.VMEM((H + R, HD_A), F32)], input_output_aliases={4: 0},
        name="pool_fwd", compiler_params=_cp(2),
    )(zp, wg, bg, sc, yab)


def _b_bwd(dyab, zp, dzp, wg, bg, sc):
    S = zp.shape[0]
    R, H, nch, ng = R_SEQ, _POOL_H, S // R_SEQ, len(POOL_WINDOWS)

    def body(dy_ref, z_ref, zh_ref, wg_ref, bg_ref, sc_ref, dz_in, dz_ref, dwg_ref, dbg_ref, dsc_ref, ext_u, ext_q):
        del dz_in
        g, j = pl.program_id(0), pl.program_id(1)
        jj = nch - 1 - j

        @pl.when(j == 0)
        def _():
            ext_q[R:R + H, :] = jnp.zeros((H, HD_A), F32)
            for r in (dwg_ref, dbg_ref, dsc_ref):
                r[...] = jnp.zeros_like(r)

        u = z_ref[...].astype(F32)
        ext_u[0:H, :] = jnp.where(jj == 0, 0.0, zh_ref[...].astype(F32))
        ext_u[H:H + R, :] = u
        t1 = (jj * R + 1 + lax.broadcasted_iota(jnp.int32, (R, HD_A), 0)).astype(F32)
        pb = _pool_mean_minus(u, ext_u, g, t1).astype(BF16)
        wgb = wg_ref[0].astype(BF16)
        lin = lax.dot_general(pb, wgb, _NN, preferred_element_type=F32) + bg_ref[...]
        dy = dy_ref[...].astype(F32)
        dsc_ref[...] += jnp.sum(dy * lin, axis=0, keepdims=True)
        dlin = dy * sc_ref[...]
        dbg_ref[...] += jnp.sum(dlin, axis=0, keepdims=True)
        dlb = dlin.astype(BF16)
        dwg_ref[0] += lax.dot_general(pb, dlb, _TN, preferred_element_type=F32)
        dp = lax.dot_general(dlb, wgb, _NT, preferred_element_type=F32)
        q = dp / jnp.minimum(t1, _pool_width(g))
        ext_q[0:R, :] = q
        acc, wins = q, []
        for k in range(1, H):
            acc = acc + ext_q[pl.ds(k, R), :]
            if k + 1 in POOL_WINDOWS:
                wins.append(acc)
        win = jnp.where(g == 0, wins[0], jnp.where(g == 1, wins[1], jnp.where(g == 2, wins[2], wins[3])))
        dz_ref[...] = (win - dp).astype(BF16)
        ext_q[R:R + H, :] = ext_q[0:H, :]

    vec = pl.BlockSpec((1, HD_A), lambda g, j: (0, g))
    mat = pl.BlockSpec((1, HD_A, HD_A), lambda g, j: (g, 0, 0))
    return pl.pallas_call(
        body, out_shape=[SDS(dzp.shape, dzp.dtype), SDS((ng, HD_A, HD_A), F32), SDS((1, D // 2), F32),
                         SDS((1, D // 2), F32)],
        grid=(ng, nch),
        in_specs=[pl.BlockSpec((R, HD_A), lambda g, j: (nch - 1 - j, _POOL_Y0 + g)),
                  pl.BlockSpec((R, HD_A), lambda g, j: (nch - 1 - j, _POOL_T0 + g)),
                  pl.BlockSpec((H, HD_A), lambda g, j: (jnp.maximum((nch - 1 - j) * (R // H) - 1, 0), _POOL_T0 + g)),
                  mat, vec, vec, pl.BlockSpec(memory_space=pl.ANY)],
        out_specs=[pl.BlockSpec((R, HD_A), lambda g, j: (nch - 1 - j, _POOL_T0 + g)), mat, vec, vec],
        scratch_shapes=[pltpu.VMEM((H + R, HD_A), F32), pltpu.VMEM((R + H, HD_A), F32)],
        input_output_aliases={6: 0}, name="pool_bwd", compiler_params=_cp(2),
    )(dyab, zp, zp, wg, bg, sc, dzp)


_CW_F = 768


def _f_fwd(hp, w, b, name):
    S = hp.shape[0]
    R, H, cw = R_SEQ, SUB, _CW_F
    nlt = cw // LANE

    def body(h_ref, w_ref, b_ref, o_ref, ext):
        j = pl.program_id(1)

        @pl.when(j == 0)
        def _():
            ext[:, 0:H, :] = jnp.zeros((nlt, H, LANE), F32)

        def stage(r0, lt):
            ext[lt, pl.ds(pl.multiple_of(r0 + H, SUB), _RB), :] = h_ref[pl.ds(r0, _RB), _lanes(lt)].astype(F32)

        def main(r0, lt):
            ls = _lanes(lt)
            gp = b_ref[:, ls]
            for k in range(CONV_F):
                gp = gp + w_ref[k:k + 1, ls] * ext[lt, pl.ds(r0 + (H - (CONV_F - 1 - k)), _RB), :]
            up = h_ref[pl.ds(r0, _RB), _lanes(lt + nlt)].astype(F32)
            o_ref[pl.ds(r0, _RB), ls] = (_gelu(gp) * up).astype(BF16)

        _sub_blocks(R, cw, stage)
        _sub_blocks(R, cw, main)
        ext[:, 0:H, :] = ext[:, R:R + H, :]

    return pl.pallas_call(
        body, out_shape=SDS((S, D_FF), BF16), grid=(D_FF // cw, S // R),
        in_specs=[pl.BlockSpec((R, 2 * cw), lambda c, j: (j, c)), pl.BlockSpec((CONV_F, cw), lambda c, j: (0, c)),
                  pl.BlockSpec((1, cw), lambda c, j: (0, c))],
        out_specs=pl.BlockSpec((R, cw), lambda c, j: (j, c)),
        scratch_shapes=[pltpu.VMEM((nlt, H + R, LANE), F32)], name=name, compiler_params=_cp(2),
    )(hp, w, b)


def _f_bwd(dact, hp, w, b, name):
    S = hp.shape[0]
    R, H, cw, nch = R_SEQ, SUB, _CW_F, S // R_SEQ
    nlt = cw // LANE

    def body(da_ref, h_ref, hh_ref, w_ref, b_ref, dh_ref, dw_ref, db_ref, ext_g, ext_d, acc):
        j = pl.program_id(1)
        jj = nch - 1 - j

        @pl.when(j == 0)
        def _():
            ext_d[:, R:R + H, :] = jnp.zeros((nlt, H, LANE), F32)
            acc[...] = jnp.zeros_like(acc)

        for lt in range(nlt):
            ext_g[lt, 0:H, :] = jnp.where(jj == 0, 0.0, hh_ref[_HB - H:_HB, lt * LANE:(lt + 1) * LANE].astype(F32))

        def stage(r0, lt):
            ext_g[lt, pl.ds(pl.multiple_of(r0 + H, SUB), _RB), :] = h_ref[pl.ds(r0, _RB), _lanes(lt)].astype(F32)

        def first(r0, lt):
            ls, lu, rs = _lanes(lt), _lanes(lt + nlt), pl.ds(r0, _RB)
            taps = [ext_g[lt, pl.ds(r0 + (H - (CONV_F - 1 - k)), _RB), :] for k in range(CONV_F)]
            gp = b_ref[:, ls]
            for k in range(CONV_F):
                gp = gp + w_ref[k:k + 1, ls] * taps[k]
            gel, dgel = _gelu(gp, with_grad=True)
            da = da_ref[rs, ls].astype(F32)
            dh_ref[rs, lu] = (da * gel).astype(BF16)
            dgp = da * h_ref[rs, lu].astype(F32) * dgel
            ext_d[lt, rs, :] = dgp
            acc[CONV_F * SUB:(CONV_F + 1) * SUB, ls] += _psum8(dgp)
            for k in range(CONV_F):
                acc[k * SUB:(k + 1) * SUB, ls] += _psum8(dgp * taps[k])

        def second(r0, lt):
            ls = _lanes(lt)
            dhg = w_ref[CONV_F - 1:CONV_F, ls] * ext_d[lt, pl.ds(r0, _RB), :]
            for k in range(CONV_F - 1):
                dhg = dhg + w_ref[k:k + 1, ls] * ext_d[lt, pl.ds(r0 + (CONV_F - 1 - k), _RB), :]
            dh_ref[pl.ds(r0, _RB), ls] = dhg.astype(BF16)

        _sub_blocks(R, cw, stage)
        _sub_blocks(R, cw, first)
        _sub_blocks(R, cw, second)
        ext_d[:, R:R + H, :] = ext_d[:, 0:H, :]

        @pl.when(j == nch - 1)
        def _():
            for k in range(CONV_F):
                dw_ref[k:k + 1, :] = jnp.sum(acc[k * SUB:(k + 1) * SUB, :], axis=0, keepdims=True)
            db_ref[...] = jnp.sum(acc[CONV_F * SUB:(CONV_F + 1) * SUB, :], axis=0, keepdims=True)

    rows = lambda c, j: (nch - 1 - j, c)
    return pl.pallas_call(
        body, out_shape=[SDS((S, 2 * D_FF), BF16), SDS((CONV_F, D_FF), F32), SDS((1, D_FF), F32)],
        grid=(D_FF // cw, nch),
        in_specs=[pl.BlockSpec((R, cw), rows), pl.BlockSpec((R, 2 * cw), rows),
                  pl.BlockSpec((_HB, 2 * cw), lambda c, j: (jnp.maximum((nch - 1 - j) * (R // _HB) - 1, 0), c)),
                  pl.BlockSpec((CONV_F, cw), lambda c, j: (0, c)), pl.BlockSpec((1, cw), lambda c, j: (0, c))],
        out_specs=[pl.BlockSpec((R, 2 * cw), rows), pl.BlockSpec((CONV_F, cw), lambda c, j: (0, c)),
                   pl.BlockSpec((1, cw), lambda c, j: (0, c))],
        scratch_shapes=[pltpu.VMEM((nlt, H + R, LANE), F32), pltpu.VMEM((nlt, R + H, LANE), F32),
                        pltpu.VMEM(((CONV_F + 1) * SUB, cw), F32)], name=name,
        compiler_params=_cp(2),
    )(dact, hp, hp, w, b)


_CW_C = 256
_H_C = 32


def _c_fwd(h1p, w, b):
    S = h1p.shape[0]
    R, H, cw = R_SEQ, _H_C, _CW_C
    nlt = cw // LANE

    def body(h_ref, w_ref, b_ref, o_ref, ext):
        j = pl.program_id(1)

        @pl.when(j == 0)
        def _():
            ext[:, 0:H, :] = jnp.zeros((nlt, H, LANE), F32)

        def stage(r0, lt):
            rs = pl.ds(r0, _RB)
            gate = h_ref[rs, _lanes(lt + nlt)].astype(F32)
            ext[lt, pl.ds(pl.multiple_of(r0 + H, SUB), _RB), :] = h_ref[rs, _lanes(lt)].astype(F32) * _sigmoid(gate)

        def main(r0, lt):
            ls = _lanes(lt)
            cv = b_ref[:, ls]
            for k in range(CONV_C):
                cv = cv + w_ref[k:k + 1, ls] * ext[lt, pl.ds(r0 + (H - (CONV_C - 1 - k)), _RB), :]
            o_ref[pl.ds(r0, _RB), ls] = cv

        _sub_blocks(R, cw, stage)
        _sub_blocks(R, cw, main)
        ext[:, 0:H, :] = ext[:, R:R + H, :]

    return pl.pallas_call(
        body, out_shape=SDS((S, D), F32), grid=(D // cw, S // R),
        in_specs=[pl.BlockSpec((R, 2 * cw), lambda c, j: (j, c)), pl.BlockSpec((CONV_C, cw), lambda c, j: (0, c)),
                  pl.BlockSpec((1, cw), lambda c, j: (0, c))],
        out_specs=pl.BlockSpec((R, cw), lambda c, j: (j, c)),
        scratch_shapes=[pltpu.VMEM((nlt, H + R, LANE), F32)], name="conf_conv_fwd", compiler_params=_cp(2),
    )(h1p, w, b)


def _c_bwd(dcv, h1p, w):
    S = h1p.shape[0]
    R, H, cw, nch = R_SEQ, _H_C, _CW_C, S // R_SEQ
    nlt = cw // LANE
    a_b, a_val, a_gate = CONV_C * SUB, (CONV_C + 1) * SUB, (CONV_C + 2) * SUB

    def body(dc_ref, h_ref, hh_ref, w_ref, dh_ref, dw_ref, db_ref, db1_ref, ext_u, ext_d, acc):
        j = pl.program_id(1)
        jj = nch - 1 - j

        @pl.when(j == 0)
        def _():
            ext_d[:, R:R + H, :] = jnp.zeros((nlt, H, LANE), F32)
            acc[...] = jnp.zeros_like(acc)

        for lt in range(nlt):
            ext_u[lt, 0:H, :] = jnp.where(
                jj == 0, 0.0, hh_ref[:, lt * LANE:(lt + 1) * LANE].astype(F32)
                * _sigmoid(hh_ref[:, cw + lt * LANE:cw + (lt + 1) * LANE].astype(F32)))

        def stage(r0, lt):
            rs, ls = pl.ds(r0, _RB), _lanes(lt)
            gate = h_ref[rs, _lanes(lt + nlt)].astype(F32)
            ext_u[lt, pl.ds(pl.multiple_of(r0 + H, SUB), _RB), :] = h_ref[rs, ls].astype(F32) * _sigmoid(gate)
            ext_d[lt, rs, :] = dc_ref[rs, ls]

        def first(r0, lt):
            ls = _lanes(lt)
            dc = dc_ref[pl.ds(r0, _RB), ls]
            acc[a_b:a_b + SUB, ls] += _psum8(dc)
            for k in range(CONV_C):
                tap = ext_u[lt, pl.ds(r0 + (H - (CONV_C - 1 - k)), _RB), :]
                acc[k * SUB:(k + 1) * SUB, ls] += _psum8(dc * tap)

        def second(r0, lt):
            rs, ls, lg = pl.ds(r0, _RB), _lanes(lt), _lanes(lt + nlt)
            du = w_ref[CONV_C - 1:CONV_C, ls] * ext_d[lt, rs, :]
            for k in range(CONV_C - 1):
                du = du + w_ref[k:k + 1, ls] * ext_d[lt, pl.ds(r0 + (CONV_C - 1 - k), _RB), :]
            val = h_ref[rs, ls].astype(F32)
            sg = _sigmoid(h_ref[rs, lg].astype(F32))
            dval = du * sg
            dgate = du * val * sg * (1.0 - sg)
            acc[a_val:a_val + SUB, ls] += _psum8(dval)
            acc[a_gate:a_gate + SUB, ls] += _psum8(dgate)
            dh_ref[rs, ls] = dval.astype(BF16)
            dh_ref[rs, lg] = dgate.astype(BF16)

        _sub_blocks(R, cw, stage)
        _sub_blocks(R, cw, first)
        _sub_blocks(R, cw, second)
        ext_d[:, R:R + H, :] = ext_d[:, 0:H, :]

        @pl.when(j == nch - 1)
        def _():
            for k in range(CONV_C):
                dw_ref[k:k + 1, :] = jnp.sum(acc[k * SUB:(k + 1) * SUB, :], axis=0, keepdims=True)
            db_ref[...] = jnp.sum(acc[a_b:a_b + SUB, :], axis=0, keepdims=True)
            db1_ref[:, 0:cw] = jnp.sum(acc[a_val:a_val + SUB, :], axis=0, keepdims=True)
            db1_ref[:, cw:2 * cw] = jnp.sum(acc[a_gate:a_gate + SUB, :], axis=0, keepdims=True)

    rows = lambda c, j: (nch - 1 - j, c)
    return pl.pallas_call(
        body, out_shape=[SDS((S, 2 * D), BF16), SDS((CONV_C, D), F32), SDS((1, D), F32), SDS((1, 2 * D), F32)],
        grid=(D // cw, nch),
        in_specs=[pl.BlockSpec((R, cw), rows), pl.BlockSpec((R, 2 * cw), rows),
                  pl.BlockSpec((H, 2 * cw), lambda c, j: (jnp.maximum((nch - 1 - j) * (R // H) - 1, 0), c)),
                  pl.BlockSpec((CONV_C, cw), lambda c, j: (0, c))],
        out_specs=[pl.BlockSpec((R, 2 * cw), rows), pl.BlockSpec((CONV_C, cw), lambda c, j: (0, c)),
                   pl.BlockSpec((1, cw), lambda c, j: (0, c)), pl.BlockSpec((1, 2 * cw), lambda c, j: (0, c))],
        scratch_shapes=[pltpu.VMEM((nlt, H + R, LANE), F32), pltpu.VMEM((nlt, R + H, LANE), F32),
                        pltpu.VMEM(((CONV_C + 3) * SUB, cw), F32)], name="conf_conv_bwd",
        compiler_params=_cp(2),
    )(dcv, h1p, h1p, w)


def _local_step(x, mem, tgt, W, fetch=None, send=None):
    G = {}
    W = dict(W)

    def arrive(group, after):
        if fetch is None:
            return None
        got, tok = fetch(group, after)
        for key, val in got.items():
            W[key] = {**W.get(key, {}), **val} if isinstance(val, dict) else val
        return tok

    def gain(g, tok):
        return g if tok is None else g + tok

    def sent(group):
        return None if send is None else send(group, G)

    def xattn_fwd(xin, l):
        n = _rms_fwd(xin, W["xa_norm"][l:l + 1], f"xa_norm_fwd{l}")
        tok = arrive(("xa", l), n)
        mn = _rms_fwd(mem, gain(W["xa_mem_norm"][l:l + 1], tok), f"xa_memnorm_fwd{l}")
        q = _mm_nn(n, W["xa_wq"][l], out_dtype=BF16, name=f"xa_q{l}")
        k = _mm_nn(mn, W["xa_wk"][l], out_dtype=BF16, name=f"xa_k{l}")
        v = _mm_nn(mn, W["xa_wv"][l], out_dtype=BF16, name=f"xa_v{l}")
        o = _attn_fwd(q, k, v, f"xa_attn_fwd{l}")
        xout = _mm_nn(o, W["xa_wo"][l], out_dtype=F32, name=f"xa_o{l}", add=xin)
        return xout, (xin, n, q, mn, k, v, o)

    def xattn_bwd(dx, dxb, saved, l):
        xin, n, q, mn, k, v, o = saved
        do = _mm_nt(dxb, W["xa_wo"][l], out_dtype=BF16, name=f"xa_do{l}")
        G[f"xa_wo{l}"] = _mm_tn(o, dxb, out_dtype=BF16, name=f"xa_dwo{l}")
        dq, dk, dv = _attn_bwd(q, k, v, do, f"xa_attn_bwd{l}")
        dkb, dvb = dk.astype(BF16), dv.astype(BF16)
        G[f"xa_wq{l}"] = _mm_tn(n, dq, out_dtype=BF16, name=f"xa_dwq{l}")
        G[f"xa_wk{l}"] = _mm_tn(mn, dkb, out_dtype=BF16, name=f"xa_dwk{l}")
        G[f"xa_wv{l}"] = _mm_tn(mn, dvb, out_dtype=BF16, name=f"xa_dwv{l}")
        tok = sent(("xa", l))
        dmn = _mm_nt(dkb, W["xa_wk"][l], out_dtype=F32, name=f"xa_dmn_k{l}")
        dmn = _mm_nt(dvb, W["xa_wv"][l], out_dtype=F32, name=f"xa_dmn_v{l}", add=dmn)
        (G[f"xa_mem_norm{l}"],) = _rms_bwd(mem, W["xa_mem_norm"][l:l + 1], dmn, None, f"xa_memnorm_bwd{l}")
        dn = _mm_nt(dq, W["xa_wq"][l], out_dtype=F32, name=f"xa_dn{l}")
        dx, dxb, G[f"xa_norm{l}"] = _rms_bwd(xin, gain(W["xa_norm"][l:l + 1], tok), dn, dx, f"xa_norm_bwd{l}")
        return dx, dxb

    def ffn_fwd(xin, l):
        n = _rms_fwd(xin, W["f_norm"][l:l + 1], f"f_norm_fwd{l}")
        tok = arrive(("f", l), n)
        hp = _mm_nn(n, W["f_w_up"][l], out_dtype=BF16, name=f"f_up{l}")
        act = _f_fwd(hp, W["f_dw_w"][l], gain(W["f_dw_b"][l:l + 1], tok), f"f_conv_fwd{l}")
        xout = _mm_nn(act, W["f_w_down"][l], out_dtype=F32, name=f"f_down{l}", add=xin)
        return xout, (xin, n, hp, act)

    def ffn_bwd(dx, dxb, saved, l):
        xin, n, hp, act = saved
        dact = _mm_nt(dxb, W["f_w_down"][l], out_dtype=BF16, name=f"f_dact{l}")
        G[f"f_w_down{l}"] = _mm_tn(act, dxb, out_dtype=BF16, name=f"f_dwdown{l}")
        dhp, G[f"f_dw_w{l}"], G[f"f_dw_b{l}"] = _f_bwd(dact, hp, W["f_dw_w"][l], W["f_dw_b"][l:l + 1], f"f_conv_bwd{l}")
        G[f"f_w_up{l}"] = _mm_tn(n, dhp, out_dtype=BF16, name=f"f_dwup{l}", blocks=_CW_F)
        tok = sent(("f", l))
        dn = _mm_nt(dhp, W["f_w_up"][l], out_dtype=F32, name=f"f_dn{l}")
        dx, dxb, G[f"f_norm{l}"] = _rms_bwd(xin, gain(W["f_norm"][l:l + 1], tok), dn, dx, f"f_norm_bwd{l}")
        return dx, dxb

    n0 = _rms_fwd(x, W["ab_norm"], "ab_norm_fwd")
    tok = arrive(("ab", 0), n0)
    a_par = (W["a_conv_w"], gain(W["a_conv_b"], tok), W["a_gate_x_w"], W["a_gate_x_b"], W["a_gate_a_w"],
             W["a_gate_a_b"], W["a_lambda"])
    b_par = (W["b_group_w"], W["b_group_b"], W["b_scale"])
    zp = _mm_nn(n0, W["ab_w_in"], out_dtype=BF16, name="ab_in")
    yab, h_a = _a_fwd(zp, *a_par)
    yab = _b_fwd(zp, yab, *b_par)
    arrive(("ab", 1), yab)
    x1 = _mm_nn(yab, W["ab_w_out"], out_dtype=F32, name="ab_out", add=x)
    x2, s_xa0 = xattn_fwd(x1, 0)
    x3, s_f0 = ffn_fwd(x2, 0)
    n3 = _rms_fwd(x3, W["c_norm"], "c_norm_fwd")
    tok = arrive(("c", 0), n3)
    h1p = _mm_nn(n3, W["c_w_pw1"], out_dtype=BF16, name="c_pw1", bias=gain(W["c_b_pw1"], tok))
    cv = _c_fwd(h1p, W["c_dw_w"], W["c_dw_b"])
    sc = _ln_silu_fwd(cv, W["c_ln_g"], W["c_ln_b"])
    x4 = _mm_nn(sc, W["c_w_pw2"], out_dtype=F32, name="c_pw2", bias=W["c_b_pw2"], add=x3)
    x5, s_xa1 = xattn_fwd(x4, 1)
    x6, s_f1 = ffn_fwd(x5, 1)
    loss, dx, dxb, G["final_norm"] = _loss_head(x6, W["final_norm"], tgt)

    dx, dxb = ffn_bwd(dx, dxb, s_f1, 1)
    dx, dxb = xattn_bwd(dx, dxb, s_xa1, 1)
    dsc = _mm_nt(dxb, W["c_w_pw2"], out_dtype=BF16, name="c_dsc")
    G["c_w_pw2"] = _mm_tn(sc, dxb, out_dtype=BF16, name="c_dwpw2")
    dcv, G["c_ln_g"], G["c_ln_b"], G["c_b_pw2"] = _ln_silu_bwd(dsc, cv, W["c_ln_g"], W["c_ln_b"], dx)
    dh1p, G["c_dw_w"], G["c_dw_b"], G["c_b_pw1"] = _c_bwd(dcv, h1p, W["c_dw_w"])
    G["c_w_pw1"] = _mm_tn(n3, dh1p, out_dtype=BF16, name="c_dwpw1", blocks=_CW_C)
    tok = sent(("c", 0))
    dn3 = _mm_nt(dh1p, W["c_w_pw1"], out_dtype=F32, name="c_dn")
    dx, dxb, G["c_norm"] = _rms_bwd(x3, gain(W["c_norm"], tok), dn3, dx, "c_norm_bwd")
    dx, dxb = ffn_bwd(dx, dxb, s_f0, 0)
    dx, dxb = xattn_bwd(dx, dxb, s_xa0, 0)
    dyab = _mm_nt(dxb, W["ab_w_out"], out_dtype=BF16, name="ab_dyab")
    G["ab_w_out"] = _mm_tn(yab, dxb, out_dtype=BF16, name="ab_dwout")
    tok = sent(("ab", 1))
    a_par = (a_par[0], gain(a_par[1], tok)) + a_par[2:]
    (dzp, G["a_conv_w"], G["a_conv_b"], G["a_gate_x_w"], G["a_gate_x_b"], G["a_gate_a_w"], G["a_gate_a_b"],
     G["a_lambda"]) = _a_bwd(dyab, zp, h_a, *a_par)
    dzp, G["b_group_w"], G["b_group_b"], G["b_scale"] = _b_bwd(dyab, zp, dzp, *b_par)
    G["ab_w_in"] = _mm_tn(n0, dzp, out_dtype=BF16, name="ab_dwin")
    dn0 = _mm_nt(dzp, W["ab_w_in"], out_dtype=F32, name="ab_dn")
    dx, _, G["ab_norm"] = _rms_bwd(x, W["ab_norm"], dn0, dx, "ab_norm_bwd")
    return loss, dx, G


def _my_place():
    x, y, c = lax.axis_index("x"), lax.axis_index("y"), lax.axis_index("c")
    return x, y, c


def _all_gather(shards, name):
    n = len(shards)

    def body(*refs):
        ins, outs = refs[:n], refs[n:2 * n]
        send_sems, recv_sems, local_sems = refs[2 * n:]
        x, y, c = _my_place()
        me, sibling = (x, y, c), (x, y, 1 - c)
        chips = [(1 - x, y), (x, 1 - y), (1 - x, 1 - y)]

        def slab(a, place):
            px, py, pc = place
            return outs[a].at[4 * px + 2 * py + pc]

        def copy(a, k, block, to, src=None):
            return pltpu.make_async_remote_copy(
                src_ref=slab(a, block) if src is None else src, dst_ref=slab(a, block),
                send_sem=send_sems.at[a, k], recv_sem=recv_sems.at[a, k], device_id=to, device_id_type=MESH)

        mine = [pltpu.make_async_copy(ins[a], slab(a, me), local_sems.at[a]) for a in range(n)]
        for cp in mine:
            cp.start()
        first = []
        for j, chip in enumerate(chips):
            first += [copy(a, 1 + j, me, (*chip, c), src=ins[a]) for a in range(n)]
        first += [copy(a, 0, me, sibling, src=ins[a]) for a in range(n)]
        for cp in first:
            cp.start()
        passed = []
        for j, chip in enumerate(chips):
            for a in range(n):
                copy(a, 1 + j, (*chip, c), me).wait_recv()
                cp = copy(a, 4 + j, (*chip, c), sibling)
                cp.start()
                passed.append(cp)
        for a in range(n):
            copy(a, 0, sibling, me).wait_recv()
        for j, chip in enumerate(chips):
            for a in range(n):
                copy(a, 4 + j, (*chip, 1 - c), me).wait_recv()
        for cp in first + passed:
            cp.wait_send()
        for cp in mine:
            cp.wait()

    any_spec = pl.BlockSpec(memory_space=pl.ANY)
    return pl.pallas_call(
        body, out_shape=[SDS((N_DEV,) + s.shape, s.dtype) for s in shards], in_specs=[any_spec] * n,
        out_specs=[any_spec] * n,
        scratch_shapes=[pltpu.SemaphoreType.DMA((n, 7)), pltpu.SemaphoreType.DMA((n, 7)), pltpu.SemaphoreType.DMA((n,))],
        name=name,
    )(*shards)


_HBM = pl.BlockSpec(memory_space=pltpu.HBM)
_SEM = pl.BlockSpec(memory_space=pltpu.SEMAPHORE)
_EFFECT = pltpu.SideEffectType.DATAFLOW_SIDE_EFFECTING


def _peer_places():
    x, y, c = _my_place()
    peers = []
    for k in range(1, N_DEV):
        px = 1 - x if (k >> 2) & 1 else x
        py = 1 - y if (k >> 1) & 1 else y
        pc = 1 - c if k & 1 else c
        peers.append(((px, py, pc), 4 * px + 2 * py + pc))
    return (x, y, c), 4 * x + 2 * y + c, peers


def _send_start(srcs, per_dest, name):
    n = len(srcs)
    lands = [lax.empty((N_DEV,) + (s.shape[1:] if per_dest else s.shape), s.dtype) for s in srcs]

    def body(*refs):
        src, land = refs[:n], refs[n:2 * n]
        outs = refs[2 * n:]
        send, recv, token = outs[:n], outs[n:2 * n], outs[4 * n]
        _, me, peers = _peer_places()
        for a in range(n):
            for peer, pidx in peers:
                pltpu.make_async_remote_copy(
                    src_ref=src[a].at[pidx] if per_dest else src[a], dst_ref=land[a].at[me], send_sem=send[a],
                    recv_sem=recv[a], device_id=peer, device_id_type=MESH).start()
        token[...] = jnp.zeros_like(token)

    hbm = lambda a: pltpu.HBM(a.shape, a.dtype)
    sem = pltpu.SemaphoreType.DMA(())
    res = pl.pallas_call(
        body, name=name,
        out_shape=tuple([sem] * (2 * n) + [hbm(s) for s in srcs] + [hbm(l) for l in lands]
                        + [SDS((SUB, LANE), F32)]),
        in_specs=[_HBM] * (2 * n),
        out_specs=tuple([_SEM] * (2 * n) + [_HBM] * (2 * n) + [pl.BlockSpec(memory_space=pltpu.VMEM)]),
        input_output_aliases={i: 2 * n + i for i in range(2 * n)},
        compiler_params=pltpu.CompilerParams(has_side_effects=_EFFECT),
    )(*[pltpu.with_memory_space_constraint(s, pltpu.HBM) for s in srcs],
      *[pltpu.with_memory_space_constraint(l, pltpu.HBM) for l in lands])
    return res[:n], res[n:2 * n], res[2 * n:3 * n], res[3 * n:4 * n], res[4 * n]


def _send_wait(send, recv, srcs, lands, after, per_dest, name):
    n = len(srcs)

    def body(*refs):
        src, land = refs[:n], refs[n:2 * n]
        send_s, recv_s = refs[2 * n:3 * n], refs[3 * n:4 * n]
        token = refs[-1]
        place, _, _ = _peer_places()
        for a in range(n):
            seven = land[a].at[pl.ds(0, N_DEV - 1)]
            copy = pltpu.make_async_remote_copy(
                src_ref=src[a].at[pl.ds(0, N_DEV - 1)] if per_dest else seven, dst_ref=seven, send_sem=send_s[a],
                recv_sem=recv_s[a], device_id=place, device_id_type=MESH)
            copy.wait_send()
            copy.wait_recv()
        token[...] = jnp.zeros_like(token)

    hbm = lambda a: pltpu.HBM(a.shape, a.dtype)
    res = pl.pallas_call(
        body, name=name,
        out_shape=tuple([hbm(s) for s in srcs] + [hbm(l) for l in lands] + [SDS((SUB, LANE), F32)]),
        in_specs=[_HBM] * (2 * n) + [_SEM] * (2 * n) + [pl.BlockSpec(memory_space=pl.ANY)],
        out_specs=tuple([_HBM] * (2 * n) + [pl.BlockSpec(memory_space=pltpu.VMEM)]),
        input_output_aliases={i: i for i in range(2 * n)},
        compiler_params=pltpu.CompilerParams(has_side_effects=_EFFECT),
    )(*srcs, *lands, *send, *recv, after)
    return res[:n], res[n:2 * n], res[2 * n]


def _adamw_math(w, g, m, v):
    m = ADAM_B1 * m + (1.0 - ADAM_B1) * g
    v = ADAM_B2 * v + (1.0 - ADAM_B2) * (g * g)
    m_hat = m / (1.0 - ADAM_B1 ** ADAM_STEP)
    v_hat = v / (1.0 - ADAM_B2 ** ADAM_STEP)
    delta = -ADAM_LR * (m_hat / (jnp.sqrt(v_hat) + ADAM_EPS) + ADAM_WD * w)
    return delta, m, v


def _row_tile(r, c, itemsize_rows):
    cap = max(SUB, (itemsize_rows // (4 * c)) // SUB * SUB)
    if r <= cap:
        return r
    best = None
    for t in range(SUB, cap + 1, SUB):
        if r % t == 0:
            best = t
    return best if best is not None else r


def _sum_adamw(landing, w, m, v, name, layer=0, prev=None):
    _, r, c = landing.shape
    tr = _row_tile(r, c, 1 << 20)
    off = layer * (r // tr)

    def body(l_ref, w_ref, m_ref, v_ref, *rest):
        g_ref, d_ref, mo_ref, vo_ref = rest[-4:]
        g = l_ref[0].astype(F32)
        for s in range(1, N_DEV):
            g = g + l_ref[s].astype(F32)
        g_ref[...] = g
        d_ref[...], mo_ref[...], vo_ref[...] = _adamw_math(w_ref[...], g, m_ref[...], v_ref[...])

    blk = pl.BlockSpec((tr, c), lambda i: (i + off, 0))
    n_prev = 0 if prev is None else 4
    return pl.pallas_call(
        body, out_shape=[SDS(w.shape, F32)] * 4, grid=(r // tr,),
        in_specs=[pl.BlockSpec((N_DEV, tr, c), lambda i: (0, i, 0)), blk, blk, blk]
        + [pl.BlockSpec(memory_space=pl.ANY)] * n_prev,
        out_specs=[blk] * 4, input_output_aliases={4 + i: i for i in range(n_prev)}, name=name,
        compiler_params=_cp(1),
    )(landing, w, m, v, *([] if prev is None else prev))


def _sum8(landing, name):
    _, r, c = landing.shape

    def body(l_ref, g_ref):
        g = l_ref[0]
        for s in range(1, N_DEV):
            g = g + l_ref[s]
        g_ref[...] = g

    return pl.pallas_call(body, out_shape=SDS((r, c), F32), name=name, compiler_params=_cp(0))(landing)


def _adamw(g, w, m, v, name):
    r, c = g.shape
    tr = _row_tile(r, c, 1 << 20)

    def body(g_ref, w_ref, m_ref, v_ref, d_ref, mo_ref, vo_ref):
        d_ref[...], mo_ref[...], vo_ref[...] = _adamw_math(w_ref[...], g_ref[...], m_ref[...], v_ref[...])

    blk = pl.BlockSpec((tr, c), lambda i: (i, 0))
    return pl.pallas_call(body, out_shape=[SDS((r, c), F32)] * 3, grid=(r // tr,), in_specs=[blk] * 4,
                          out_specs=[blk] * 3, name=name, compiler_params=_cp(1))(g, w, m, v)


_BIG = {
    "ab_w_in": (1, D, 320), "ab_w_out": (1, 192, D), "c_w_pw1": (1, D, 256), "c_w_pw2": (1, 128, D),
    "xa_wq": (2, 128, D), "xa_wk": (2, 128, D), "xa_wv": (2, 128, D), "xa_wo": (2, 128, D),
    "f_w_up": (2, D, 768), "f_w_down": (2, 384, D),
}
_SMALL_SHARDED = {
    "a_conv_w": (1, 4, 128), "c_norm": (1, 128), "c_b_pw1": (1, 256), "c_dw_w": (1, 31, 128), "c_dw_b": (1, 128),
    "c_ln_g": (1, 128), "c_ln_b": (1, 128), "c_b_pw2": (1, 128), "f_dw_w": (2, 3, 384),
}
_REPL = {
    "ab_norm": (1, D), "a_conv_b": (1, D), "a_gate_x_w": (1, 8, 128, 128), "a_gate_x_b": (1, D),
    "a_gate_a_w": (1, 8, 128, 128), "a_gate_a_b": (1, D), "a_lambda": (1, D), "b_group_w": (1, 4, 128, 128),
    "b_group_b": (1, 512), "b_scale": (1, 512), "xa_norm": (2, D), "xa_mem_norm": (2, D), "f_norm": (2, D),
    "f_dw_b": (2, D_FF), "final_norm": (D,),
}


def _size(shape):
    n = 1
    for s in shape:
        n *= s
    return n


_N_SS = sum(_size(s) for s in _SMALL_SHARDED.values())
_N_REPL = sum(_size(s) for s in _REPL.values())
_REPL_ROWS = -(-_N_REPL // (N_DEV * SUB * LANE)) * SUB
_SS_ROWS = _N_SS // LANE
_SMALL_ROWS = -(-(_REPL_ROWS + _SS_ROWS) // SUB) * SUB


def _pack(parts, rows):
    flat = jnp.concatenate([p.reshape(-1).astype(F32) for p in parts])
    return jnp.pad(flat, (0, rows * LANE - flat.shape[0])).reshape(rows, LANE)


def _unpack(buf, table):
    flat, out, off = buf.reshape(-1), {}, 0
    for name, shape in table.items():
        n = _size(shape)
        out[name] = flat[off:off + n].reshape(shape)
        off += n
    return out


def _w_in_to_tiles(w):
    K = w.shape[0]
    gr = jnp.stack([w[:, :D].reshape(K, 8, HD_A), w[:, D:2 * D].reshape(K, 8, HD_A)], axis=2)
    return jnp.concatenate([gr.reshape(K, 2 * D), w[:, 2 * D:]], axis=1)


def _w_in_from_tiles(w):
    K = w.shape[0]
    gr = w[:, :2 * D].reshape(K, 8, 2, HD_A)
    return jnp.concatenate([gr[:, :, 0].reshape(K, D), gr[:, :, 1].reshape(K, D), w[:, 2 * D:]], axis=1)


def _pair_blocks(v, bw):
    lead, n = v.shape[:-1], v.shape[-1]
    return jnp.swapaxes(v.reshape(lead + (2, n // (2 * bw), bw)), -3, -2).reshape(lead + (n,))


def _unpair_blocks(v, bw):
    lead, n = v.shape[:-1], v.shape[-1]
    return jnp.swapaxes(v.reshape(lead + (n // (2 * bw), 2, bw)), -3, -2).reshape(lead + (n,))


_GROUPS = {
    ("ab", 0): (("ab_w_in", 0),),
    ("ab", 1): (("ab_w_out", 0),),
    ("xa", 0): (("xa_wq", 0), ("xa_wk", 0), ("xa_wv", 0), ("xa_wo", 0)),
    ("f", 0): (("f_w_up", 0), ("f_w_down", 0)),
    ("c", 0): (("c_w_pw1", 0), ("c_w_pw2", 0)),
    ("xa", 1): (("xa_wq", 1), ("xa_wk", 1), ("xa_wv", 1), ("xa_wo", 1)),
    ("f", 1): (("f_w_up", 1), ("f_w_down", 1)),
}
_SEND_GROUPS = _GROUPS


def _weight_layout(name, g):
    if name == "ab_w_in":
        return _w_in_to_tiles(jnp.swapaxes(g, 0, 1).reshape(D, N_DEV * 320))
    if name in ("c_w_pw1", "f_w_up"):
        return g
    return g.reshape(N_DEV * g.shape[1], D)


def _grad_blocks(name, l, G):
    _, r, c = _BIG[name]
    if name == "ab_w_in":
        return jnp.swapaxes(_w_in_from_tiles(G[name]).reshape(D, N_DEV, 320), 0, 1)
    if name == "c_w_pw1":
        return G[name]
    if name == "f_w_up":
        return G[f"{name}{l}"]
    return (G[name] if _BIG[name][0] == 1 else G[f"{name}{l}"]).reshape(N_DEV, r, c)


def _small_layouts(sm):
    W = {}
    sm = sm.reshape(N_DEV, -1)
    off = 0
    for name, shape in _SMALL_SHARDED.items():
        n = _size(shape)
        blocks = sm[:, off:off + n].reshape((N_DEV,) + shape)
        off += n
        W[name] = jnp.moveaxis(blocks, 0, -2).reshape(shape[:-1] + (N_DEV * shape[-1],))
    W["a_conv_w"], W["c_dw_w"] = W["a_conv_w"][0], W["c_dw_w"][0]
    W["c_b_pw1"] = _pair_blocks(W["c_b_pw1"], _CW_C)
    return W


def _with_own(land, src, me, per_dest):
    own = lax.dynamic_slice_in_dim(src, me, 1, 0) if per_dest else src[None]
    return lax.dynamic_update_slice_in_dim(land, own, me, 0)


def _to_dest_major(g, shape):
    full = g.reshape(shape[:-1] + (N_DEV, shape[-1]))
    return jnp.moveaxis(full, -2, 0).reshape(N_DEV, -1)


def kernel(x, mem, ab_norm, ab_w_in, a_conv_w, a_conv_b, a_gate_x_w, a_gate_x_b, a_gate_a_w, a_gate_a_b, a_lambda, b_group_w, b_group_b, b_scale, ab_w_out, c_norm, c_w_pw1, c_b_pw1, c_dw_w, c_dw_b, c_ln_g, c_ln_b, c_w_pw2, c_b_pw2, xa_norm, xa_mem_norm, xa_wq, xa_wk, xa_wv, xa_wo, f_norm, f_w_up, f_dw_w, f_dw_b, f_w_down, final_norm, loss_target, m_ab_norm, m_ab_w_in, m_a_conv_w, m_a_conv_b, m_a_gate_x_w, m_a_gate_x_b, m_a_gate_a_w, m_a_gate_a_b, m_a_lambda, m_b_group_w, m_b_group_b, m_b_scale, m_ab_w_out, m_c_norm, m_c_w_pw1, m_c_b_pw1, m_c_dw_w, m_c_dw_b, m_c_ln_g, m_c_ln_b, m_c_w_pw2, m_c_b_pw2, m_xa_norm, m_xa_mem_norm, m_xa_wq, m_xa_wk, m_xa_wv, m_xa_wo, m_f_norm, m_f_w_up, m_f_dw_w, m_f_dw_b, m_f_w_down, m_final_norm, v_ab_norm, v_ab_w_in, v_a_conv_w, v_a_conv_b, v_a_gate_x_w, v_a_gate_x_b, v_a_gate_a_w, v_a_gate_a_b, v_a_lambda, v_b_group_w, v_b_group_b, v_b_scale, v_ab_w_out, v_c_norm, v_c_w_pw1, v_c_b_pw1, v_c_dw_w, v_c_dw_b, v_c_ln_g, v_c_ln_b, v_c_w_pw2, v_c_b_pw2, v_xa_norm, v_xa_mem_norm, v_xa_wq, v_xa_wk, v_xa_wv, v_xa_wo, v_f_norm, v_f_w_up, v_f_dw_w, v_f_dw_b, v_f_w_down, v_final_norm):
    args = dict(locals())
    P = {n: args[n] for n in _NAMES}
    M = {n: args["m_" + n] for n in _NAMES}
    V = {n: args["v_" + n] for n in _NAMES}

    me = 4 * lax.axis_index("x") + 2 * lax.axis_index("y") + lax.axis_index("c")

    in_flight = {}

    def launch(groups, tok):
        shards, n_of = [], {}
        for grp in groups:
            for name, l in _GROUPS[grp]:
                w = P[name][l] if tok is None else P[name][l] + tok
                shards.append(w.astype(BF16))
            if grp == ("ab", 0):
                shards.append(_pack([P[n] for n in _SMALL_SHARDED], _SS_ROWS + 4))
            n_of[grp] = len(shards)
        res = _send_start(shards, False, "gather_start_" + "_".join(g[0] + str(g[1]) for g in groups))
        lo = 0
        for grp in groups:
            in_flight[grp] = [r[lo:n_of[grp]] for r in res[:4]]
            lo = n_of[grp]
        return res[4][:1, :1]

    follow = {("ab", 0): [("ab", 1), ("xa", 0), ("f", 0)], ("xa", 0): [("c", 0)], ("f", 0): [("xa", 1)],
              ("c", 0): [("f", 1)]}

    def fetch(grp, after):
        send_s, recv_s, srcs, lands = in_flight.pop(grp)
        srcs, lands, tok = _send_wait(send_s, recv_s, srcs, lands, after, False, f"gather_wait_{grp[0]}{grp[1]}")
        tok = launch(follow[grp], tok[:1, :1]) if grp in follow else None
        full = [_with_own(land, src, me, False) for land, src in zip(lands, srcs)]
        out = {}
        for (name, l), g in zip(_GROUPS[grp], full):
            w = _weight_layout(name, g)
            if _BIG[name][0] == 1:
                out[name] = w
            else:
                out[name] = {l: w}
        if grp == ("ab", 0):
            out.update(_small_layouts(full[-1]))
        return out, tok

    zero = launch([("ab", 0)], None)

    pending = []

    def send(grp, G):
        members = _SEND_GROUPS[grp]
        res = _send_start([_grad_blocks(name, l, G) for name, l in members], True, f"send_{grp[0]}{grp[1]}")
        pending.append((members, res))
        return res[4][:1, :1]

    W = {n: P[n] for n in _REPL}
    W["ab_norm"] = P["ab_norm"] + zero
    W["final_norm"] = P["final_norm"].reshape(1, D)
    W["a_gate_x_w"], W["a_gate_a_w"], W["b_group_w"] = P["a_gate_x_w"][0], P["a_gate_a_w"][0], P["b_group_w"][0]
    loss, grad_x, G = _local_step(x[0], mem[0], loss_target[0], W, fetch, send)
    loss = lax.psum(loss[0, 0], ("x", "y", "c"))

    Gs = dict(G)
    Gs["c_b_pw1"] = _unpair_blocks(G["c_b_pw1"], _CW_C)
    Gs["f_dw_w"] = jnp.stack([G["f_dw_w0"], G["f_dw_w1"]])
    Gs["a_conv_w"], Gs["c_dw_w"] = G["a_conv_w"][None], G["c_dw_w"][None]
    for n in ("xa_norm", "xa_mem_norm", "f_norm", "f_dw_b"):
        Gs[n] = jnp.concatenate([G[f"{n}0"], G[f"{n}1"]], axis=0)
    for n in ("a_gate_x_w", "a_gate_a_w", "b_group_w"):
        Gs[n] = G[n][None]
    repl_flat = jnp.concatenate([Gs[n].reshape(-1) for n in _REPL])
    repl_rows = jnp.pad(repl_flat, (0, N_DEV * _REPL_ROWS * LANE - _N_REPL)).reshape(N_DEV, _REPL_ROWS, LANE)
    ss_rows = jnp.concatenate([_to_dest_major(Gs[n], s) for n, s in _SMALL_SHARDED.items()], axis=1)
    ss_rows = ss_rows.reshape(N_DEV, _SS_ROWS, LANE)
    small_pack = jnp.concatenate(
        [repl_rows, ss_rows, jnp.zeros((N_DEV, _SMALL_ROWS - _REPL_ROWS - _SS_ROWS, LANE), F32)], axis=1)
    last = _send_start([_grad_blocks("ab_w_in", 0, G), small_pack], True, "send_ab0")
    pending.append(((("ab_w_in", 0), ("small", 0)), last))

    members = [m for mem_, _ in pending for m in mem_]
    cat = [[a for _, res in pending for a in res[i]] for i in range(4)]
    srcs, lands, _ = _send_wait(cat[0], cat[1], cat[2], cat[3], grad_x, True, "send_wait")
    landed = {m: _with_own(land, src, me, True) for m, land, src in zip(members, lands, srcs)}

    out_g, out_d, out_m, out_v = {}, {}, {}, {}
    for name, (layers, r, c) in _BIG.items():
        shape = P[name].shape
        w2, m2, v2 = [t[name].reshape(layers * r, c) for t in (P, M, V)]
        res = None
        for l in range(layers):
            res = _sum_adamw(landed[(name, l)], w2, m2, v2, f"adamw_{name}{l}", layer=l, prev=res)
        out_g[name], out_d[name], out_m[name], out_v[name] = [t.reshape(shape) for t in res]

    small_sum = _sum8(landed[("small", 0)], "sum_small")
    (repl_all,) = _all_gather([small_sum[:_REPL_ROWS]], "gather_small_grads")
    g_repl = _unpack(repl_all, _REPL)
    g_ss = _unpack(small_sum[_REPL_ROWS:_REPL_ROWS + _SS_ROWS], _SMALL_SHARDED)
    table = dict(_REPL)
    table.update(_SMALL_SHARDED)
    rows = -(-(_N_REPL + _N_SS) // (256 * LANE)) * 256
    g_small = dict(g_repl)
    g_small.update(g_ss)
    packs = [_pack([src[n] for n in table], rows) for src in (g_small, P, M, V)]
    res = _adamw(*packs, "adamw_small")
    for out, buf in zip((out_d, out_m, out_v), res):
        out.update(_unpack(buf, table))
    out_g.update(g_small)

    return (loss, grad_x[None], *[out_g[n] for n in _NAMES], *[out_d[n] for n in _NAMES],
            *[out_m[n] for n in _NAMES], *[out_v[n] for n in _NAMES])


_NAMES = ("ab_norm", "ab_w_in", "a_conv_w", "a_conv_b", "a_gate_x_w", "a_gate_x_b", "a_gate_a_w", "a_gate_a_b",
          "a_lambda", "b_group_w", "b_group_b", "b_scale", "ab_w_out", "c_norm", "c_w_pw1", "c_b_pw1", "c_dw_w",
          "c_dw_b", "c_ln_g", "c_ln_b", "c_w_pw2", "c_b_pw2", "xa_norm", "xa_mem_norm", "xa_wq", "xa_wk", "xa_wv",
          "xa_wo", "f_norm", "f_w_up", "f_dw_w", "f_dw_b", "f_w_down", "final_norm")
```

```python
import functools

import jax
import jax.numpy as jnp
from jax import lax
from jax.experimental import pallas as pl
from jax.experimental.pallas import tpu as pltpu

F32, BF16 = jnp.float32, jnp.bfloat16
SDS = jax.ShapeDtypeStruct
MESH = pl.DeviceIdType.MESH

N_DEV = 8
D = 1024
N_MEM = 256
XA_HEADS, XA_HD = 4, 256
HD_A = 128
CONV_A, CONV_C, CONV_F = 4, 31, 3
C_RG = 8.0
POOL_WINDOWS = (2, 4, 8, 16)
D_FF = 3 * D
EPS = 1e-6
ADAM_LR, ADAM_B1, ADAM_B2, ADAM_EPS, ADAM_WD, ADAM_STEP = 0.001, 0.9, 0.999, 1e-08, 0.01, 10

LANE = 128
SUB = 8
VMEM_LIMIT = 56 * 1024 * 1024
R_SEQ = 256
TM_ROW = 512


def _cp(n_axes):
    return pltpu.CompilerParams(dimension_semantics=("arbitrary",) * n_axes, vmem_limit_bytes=VMEM_LIMIT)


def _tile(n, pref):
    if n <= pref:
        return n
    best = None
    for t in range(LANE, pref + 1, LANE):
        if n % t == 0:
            best = t
    assert best is not None, (n, pref)
    return best


def _perm2(n):
    return (n % 2) * 4 + n // 2


_NN = (((1,), (0,)), ((), ()))
_NT = (((1,), (1,)), ((), ()))
_TN = (((0,), (0,)), ((), ()))


def _mm_call(name, grid, a, b, a_spec, b_spec, o_spec, out_shape, dims, acc_shape, extras=()):
    nk = grid[2]
    n_ex = len(extras)

    def finish(r, ex_refs, o_ref):
        for e in ex_refs:
            r = r + e[...]
        o_ref[...] = r.astype(o_ref.dtype)

    def body_one(a_ref, b_ref, *rest):
        finish(lax.dot_general(a_ref[...], b_ref[...], dims, preferred_element_type=F32), rest[:n_ex], rest[n_ex])

    def body_acc(a_ref, b_ref, *rest):
        ex_refs, o_ref, acc = rest[:n_ex], rest[n_ex], rest[n_ex + 1]
        k = pl.program_id(2)

        @pl.when(k == 0)
        def _():
            acc[...] = jnp.zeros_like(acc)

        acc[...] += lax.dot_general(a_ref[...], b_ref[...], dims, preferred_element_type=F32)

        @pl.when(k == nk - 1)
        def _():
            finish(acc[...], ex_refs, o_ref)

    return pl.pallas_call(
        body_one if nk == 1 else body_acc, out_shape=out_shape, grid=grid,
        in_specs=[a_spec, b_spec] + [s for _, s in extras], out_specs=o_spec,
        scratch_shapes=[] if nk == 1 else [pltpu.VMEM(acc_shape, F32)], name=name, compiler_params=_cp(3),
    )(a, b, *[e for e, _ in extras])


_K_WHOLE = 3072


def _mm_nn(a, b, *, out_dtype, name, bias=None, add=None, old=False):
    M, K = a.shape
    if old:
        tm, tk = _tile(M, 1024), _tile(K, 512)
    else:
        tk = K if K <= _K_WHOLE else _tile(K, 1024)
        tm = _tile(M, 1024 if K <= 1024 else 512)
    if b.ndim == 3:
        nb, _, bw = b.shape
        N, tn, nn = nb * bw, bw, nb
        b_spec = pl.BlockSpec((None, tk, bw), lambda m, n, k: (_perm2(n), k, 0))
    else:
        N = b.shape[1]
        tn = _tile(N, 1024)
        nn = N // tn
        b_spec = pl.BlockSpec((tk, tn), lambda m, n, k: (k, n))
    extras = []
    if bias is not None:
        extras.append((bias, pl.BlockSpec((1, tn), lambda m, n, k: (0, n))))
    if add is not None:
        extras.append((add, pl.BlockSpec((tm, tn), lambda m, n, k: (m, n))))
    return _mm_call(name, (M // tm, nn, K // tk), a, b, pl.BlockSpec((tm, tk), lambda m, n, k: (m, k)), b_spec,
                    pl.BlockSpec((tm, tn), lambda m, n, k: (m, n)), SDS((M, N), out_dtype), _NN, (tm, tn), extras)


def _mm_nt(a, b, *, out_dtype, name, add=None, old=False):
    M, N = a.shape
    if b.ndim == 3:
        nb, Ko, bw = b.shape
        tm = _tile(M, 1024)
        tn, tk, nk = _tile(Ko, 1024), bw, nb
        b_spec = pl.BlockSpec((None, tn, bw), lambda m, n, k: (_perm2(k), n, 0))
    else:
        Ko = b.shape[0]
        if old:
            tm, tk = _tile(M, 1024), _tile(N, 512)
        else:
            tk = N if N <= _K_WHOLE else _tile(N, 1024)
            tm = _tile(M, 1024 if N <= 1024 else 512)
        tn = _tile(Ko, 1024)
        nk = N // tk
        b_spec = pl.BlockSpec((tn, tk), lambda m, n, k: (n, k))
    extras = []
    if add is not None:
        extras.append((add, pl.BlockSpec((tm, tn), lambda m, n, k: (m, n))))
    return _mm_call(name, (M // tm, Ko // tn, nk), a, b, pl.BlockSpec((tm, tk), lambda m, n, k: (m, k)), b_spec,
                    pl.BlockSpec((tm, tn), lambda m, n, k: (m, n)), SDS((M, Ko), out_dtype), _NT, (tm, tn), extras)


def _mm_nt_cols(parts, b, *, out_dtype, name):
    n = len(parts)
    M, Ko = parts[0].shape[0], b.shape[0]
    tm, tn = _tile(M, 512), _tile(Ko, 1024)
    offs, off = [], 0
    for p in parts:
        assert off % p.shape[1] == 0
        offs.append(off // p.shape[1])
        off += p.shape[1]

    def body(*refs):
        r = lax.dot_general(refs[0][...], refs[n][...], _NT, preferred_element_type=F32)
        for i in range(1, n):
            r = r + lax.dot_general(refs[i][...], refs[n + i][...], _NT, preferred_element_type=F32)
        refs[2 * n][...] = r.astype(refs[2 * n].dtype)

    a_specs = [pl.BlockSpec((tm, p.shape[1]), lambda m, k: (m, 0)) for p in parts]
    b_specs = [pl.BlockSpec((tn, p.shape[1]), functools.partial(lambda m, k, o: (k, o), o=o))
               for p, o in zip(parts, offs)]
    return pl.pallas_call(
        body, out_shape=SDS((M, Ko), out_dtype), grid=(M // tm, Ko // tn), in_specs=a_specs + b_specs,
        out_specs=pl.BlockSpec((tm, tn), lambda m, k: (m, k)), name=name, compiler_params=_cp(2),
    )(*parts, *([b] * n))


def _mm_tn(a, b, *, out_dtype, name, blocks=None, old=False):
    S, Ka = a.shape
    Nb = b.shape[1]
    tm, tk = _tile(Ka, 1024), _tile(S, 512 if old else 2048)
    if blocks is not None:
        bw = blocks
        tn, nn = bw, Nb // bw
        o_spec = pl.BlockSpec((None, tm, bw), lambda m, n, k: (_perm2(n), m, 0))
        out_shape = SDS((nn, Ka, bw), out_dtype)
    else:
        tn = _tile(Nb, 1024)
        nn = Nb // tn
        o_spec = pl.BlockSpec((tm, tn), lambda m, n, k: (m, n))
        out_shape = SDS((Ka, Nb), out_dtype)
    return _mm_call(name, (Ka // tm, nn, S // tk), a, b, pl.BlockSpec((tk, tm), lambda m, n, k: (k, m)),
                    pl.BlockSpec((tk, tn), lambda m, n, k: (k, n)), o_spec, out_shape, _TN, (tm, tn))


def _row(tm, c):
    return pl.BlockSpec((tm, c), lambda i: (i, 0))


def _full(shape):
    nd = len(shape)
    return pl.BlockSpec(shape, lambda i: (0,) * nd)


def _rms_fwd(x, g, name):
    S = x.shape[0]
    tm = min(S, TM_ROW)

    def body(x_ref, g_ref, o_ref):
        xf = x_ref[...]
        r = lax.rsqrt(jnp.mean(xf * xf, axis=-1, keepdims=True) + EPS)
        o_ref[...] = ((xf * r) * g_ref[...]).astype(BF16)

    return pl.pallas_call(body, out_shape=SDS((S, D), BF16), grid=(S // tm,), in_specs=[_row(tm, D), _full((1, D))],
                          out_specs=_row(tm, D), name=name, compiler_params=_cp(1))(x, g)


def _rms_bwd(x, g, dn, dres, name):
    S = x.shape[0]
    tm = min(S, TM_ROW)
    want_dx = dres is not None

    def body(x_ref, g_ref, dn_ref, *rest):
        i = pl.program_id(0)
        dg_ref = rest[-1]

        @pl.when(i == 0)
        def _():
            dg_ref[...] = jnp.zeros_like(dg_ref)

        xf = x_ref[...]
        r = lax.rsqrt(jnp.mean(xf * xf, axis=-1, keepdims=True) + EPS)
        y = xf * r
        dn_v = dn_ref[...]
        dg_ref[...] += jnp.sum(dn_v * y, axis=0, keepdims=True)
        if want_dx:
            dres_ref, dx_ref, dxb_ref = rest[0], rest[1], rest[2]
            dy = dn_v * g_ref[...]
            dx = r * (dy - y * jnp.mean(dy * y, axis=-1, keepdims=True)) + dres_ref[...]
            dx_ref[...] = dx
            dxb_ref[...] = dx.astype(BF16)

    ins = [x, g, dn] + ([dres] if want_dx else [])
    in_specs = [_row(tm, D), _full((1, D)), _row(tm, D)] + ([_row(tm, D)] if want_dx else [])
    outs = ([SDS((S, D), F32), SDS((S, D), BF16)] if want_dx else []) + [SDS((1, D), F32)]
    out_specs = ([_row(tm, D), _row(tm, D)] if want_dx else []) + [_full((1, D))]
    return pl.pallas_call(body, out_shape=outs, grid=(S // tm,), in_specs=in_specs, out_specs=out_specs, name=name,
                          compiler_params=_cp(1))(*ins)


def _loss_head(x, g, tgt):
    S = x.shape[0]
    tm = min(S, TM_ROW)

    def body(x_ref, g_ref, t_ref, loss_ref, dx_ref, dxb_ref, dg_ref):
        i = pl.program_id(0)

        @pl.when(i == 0)
        def _():
            loss_ref[...] = jnp.zeros_like(loss_ref)
            dg_ref[...] = jnp.zeros_like(dg_ref)

        xf = x_ref[...]
        r = lax.rsqrt(jnp.mean(xf * xf, axis=-1, keepdims=True) + EPS)
        y = xf * r
        gv = g_ref[...]
        err = y * gv - t_ref[...]
        per_row = jnp.mean(err * err, axis=-1, keepdims=True)
        loss_ref[...] += 0.5 * jnp.sum(per_row, axis=0, keepdims=True)
        dn_v = err * (1.0 / D)
        dg_ref[...] += jnp.sum(dn_v * y, axis=0, keepdims=True)
        dy = dn_v * gv
        dx = r * (dy - y * jnp.mean(dy * y, axis=-1, keepdims=True))
        dx_ref[...] = dx
        dxb_ref[...] = dx.astype(BF16)

    return pl.pallas_call(
        body, out_shape=[SDS((1, 1), F32), SDS((S, D), F32), SDS((S, D), BF16), SDS((1, D), F32)], grid=(S // tm,),
        in_specs=[_row(tm, D), _full((1, D)), _row(tm, D)],
        out_specs=[_full((1, 1)), _row(tm, D), _row(tm, D), _full((1, D))], name="loss_head", compiler_params=_cp(1),
    )(x, g, tgt)


def _softmax_rows(s):
    m = jnp.max(s, axis=-1, keepdims=True)
    e = jnp.exp(s - m)
    return e / jnp.sum(e, axis=-1, keepdims=True)


def _attn_fwd(q, k, v, name):
    S = q.shape[0]
    tm = min(S, TM_ROW)
    scale = XA_HD ** -0.5

    def body(q_ref, k_ref, v_ref, o_ref):
        for h in range(XA_HEADS):
            sl = slice(h * XA_HD, (h + 1) * XA_HD)
            s = lax.dot_general(q_ref[:, sl], k_ref[:, sl], _NT, preferred_element_type=F32) * scale
            p = _softmax_rows(s)
            o_ref[:, sl] = lax.dot_general(p.astype(BF16), v_ref[:, sl], _NN, preferred_element_type=F32).astype(BF16)

    return pl.pallas_call(body, out_shape=SDS((S, D), BF16), grid=(S // tm,),
                          in_specs=[_row(tm, D), _full((N_MEM, D)), _full((N_MEM, D))], out_specs=_row(tm, D),
                          name=name, compiler_params=_cp(1))(q, k, v)


def _attn_bwd(q, k, v, do, name):
    S = q.shape[0]
    tm = min(S, TM_ROW)
    scale = XA_HD ** -0.5

    def body(q_ref, k_ref, v_ref, do_ref, dq_ref, dk_ref, dv_ref):
        i = pl.program_id(0)

        @pl.when(i == 0)
        def _():
            dk_ref[...] = jnp.zeros_like(dk_ref)
            dv_ref[...] = jnp.zeros_like(dv_ref)

        for h in range(XA_HEADS):
            sl = slice(h * XA_HD, (h + 1) * XA_HD)
            qh, kh, vh, doh = q_ref[:, sl], k_ref[:, sl], v_ref[:, sl], do_ref[:, sl]
            s = lax.dot_general(qh, kh, _NT, preferred_element_type=F32) * scale
            p = _softmax_rows(s)
            pb = p.astype(BF16)
            dv_ref[:, sl] += lax.dot_general(pb, doh, _TN, preferred_element_type=F32)
            dp = lax.dot_general(doh, vh, _NT, preferred_element_type=F32)
            ds = (p * (dp - jnp.sum(dp * p, axis=-1, keepdims=True)) * scale).astype(BF16)
            dq_ref[:, sl] = lax.dot_general(ds, kh, _NN, preferred_element_type=F32).astype(BF16)
            dk_ref[:, sl] += lax.dot_general(ds, qh, _TN, preferred_element_type=F32)

    return pl.pallas_call(
        body, out_shape=[SDS((S, D), BF16), SDS((N_MEM, D), F32), SDS((N_MEM, D), F32)], grid=(S // tm,),
        in_specs=[_row(tm, D), _full((N_MEM, D)), _full((N_MEM, D)), _row(tm, D)],
        out_specs=[_row(tm, D), _full((N_MEM, D)), _full((N_MEM, D))], name=name, compiler_params=_cp(1),
    )(q, k, v, do)


def _sigmoid(x):
    return 1.0 / (1.0 + jnp.exp(-x))


def _ln_silu_fwd(cv, g, b):
    S = cv.shape[0]
    tm = min(S, TM_ROW)

    def body(x_ref, g_ref, b_ref, o_ref):
        xf = x_ref[...]
        mu = jnp.mean(xf, axis=-1, keepdims=True)
        xc = xf - mu
        rstd = lax.rsqrt(jnp.mean(xc * xc, axis=-1, keepdims=True) + EPS)
        ln = (xc * rstd) * g_ref[...] + b_ref[...]
        o_ref[...] = (ln * _sigmoid(ln)).astype(BF16)

    return pl.pallas_call(body, out_shape=SDS((S, D), BF16), grid=(S // tm,),
                          in_specs=[_row(tm, D), _full((1, D)), _full((1, D))], out_specs=_row(tm, D),
                          name="ln_silu_fwd", compiler_params=_cp(1))(cv, g, b)


def _ln_silu_bwd(ds, cv, g, b, dx):
    S = cv.shape[0]
    tm = min(S, TM_ROW)

    def body(ds_ref, x_ref, g_ref, b_ref, dx_ref, dcv_ref, dg_ref, db_ref, db2_ref):
        i = pl.program_id(0)

        @pl.when(i == 0)
        def _():
            dg_ref[...] = jnp.zeros_like(dg_ref)
            db_ref[...] = jnp.zeros_like(db_ref)
            db2_ref[...] = jnp.zeros_like(db2_ref)

        xf = x_ref[...]
        mu = jnp.mean(xf, axis=-1, keepdims=True)
        xc = xf - mu
        rstd = lax.rsqrt(jnp.mean(xc * xc, axis=-1, keepdims=True) + EPS)
        xhat = xc * rstd
        gv = g_ref[...]
        ln = xhat * gv + b_ref[...]
        sg = _sigmoid(ln)
        dln = ds_ref[...].astype(F32) * (sg + ln * sg * (1.0 - sg))
        dg_ref[...] += jnp.sum(dln * xhat, axis=0, keepdims=True)
        db_ref[...] += jnp.sum(dln, axis=0, keepdims=True)
        db2_ref[...] += jnp.sum(dx_ref[...], axis=0, keepdims=True)
        dxh = dln * gv
        dcv_ref[...] = rstd * (dxh - jnp.mean(dxh, axis=-1, keepdims=True)
                               - xhat * jnp.mean(dxh * xhat, axis=-1, keepdims=True))

    return pl.pallas_call(
        body, out_shape=[SDS((S, D), F32), SDS((1, D), F32), SDS((1, D), F32), SDS((1, D), F32)], grid=(S // tm,),
        in_specs=[_row(tm, D), _row(tm, D), _full((1, D)), _full((1, D)), _row(tm, D)],
        out_specs=[_row(tm, D), _full((1, D)), _full((1, D)), _full((1, D))], name="ln_silu_bwd",
        compiler_params=_cp(1),
    )(ds, cv, g, b, dx)


_GELU_C, _GELU_K = 0.7978845608028654, 0.044715


def _gelu(x, with_grad=False):
    x2 = x * x
    t = jnp.tanh(_GELU_C * (x + _GELU_K * x * x2))
    gel = 0.5 * x * (1.0 + t)
    if not with_grad:
        return gel
    return gel, 0.5 * (1.0 + t) + 0.5 * x * (1.0 - t * t) * (_GELU_C * (1.0 + 3.0 * _GELU_K * x2))


def _expm1(x):
    poly = x * (1.0 + x * (0.5 + x * (1.0 / 6.0 + x * (1.0 / 24.0 + x * (1.0 / 120.0)))))
    return jnp.where(jnp.abs(x) < 0.05, poly, jnp.exp(x) - 1.0)


def _softplus(x):
    return jnp.maximum(x, 0.0) + jnp.log1p(jnp.exp(-jnp.abs(x)))


_SCAN_UNROLL = 4
_RB = 32
_HB = 16


def _sub_blocks(n_rows, n_lanes, fn):
    def step(idx, c):
        r0 = pl.multiple_of(idx * _RB, _RB)
        for lt in range(n_lanes // LANE):
            fn(r0, lt)
        return c

    lax.fori_loop(0, n_rows // _RB, step, 0)


def _lanes(lt):
    return pl.ds(lt * LANE, LANE)


def _psum8(x):
    parts = [x[i * SUB:(i + 1) * SUB] for i in range(x.shape[0] // SUB)]
    return functools.reduce(lambda p, q: p + q, parts)


def _scan_fwd(a_s, b_s, out_ref, carry_ref, n_groups):
    row = lax.broadcasted_iota(jnp.int32, (SUB, LANE), 0)
    U = _SCAN_UNROLL

    def step(gi, carry):
        base = gi * (SUB * U)
        parts = []
        for u in range(U):
            i = pl.multiple_of(base + u * SUB, SUB)
            a8, b8 = a_s[pl.ds(i, SUB), :], b_s[pl.ds(i, SUB), :]
            for s in (1, 2, 4):
                a_sh = jnp.where(row >= s, pltpu.roll(a8, s, 0), 1.0)
                b_sh = jnp.where(row >= s, pltpu.roll(b8, s, 0), 0.0)
                b8 = a8 * b_sh + b8
                a8 = a8 * a_sh
            parts.append((i, a8, b8))
        for i, a8, b8 in parts:
            h8 = a8 * carry + b8
            out_ref[pl.ds(i, SUB), :] = h8
            carry = jnp.broadcast_to(h8[SUB - 1:SUB, :], (SUB, LANE))
        return carry

    carry_ref[...] = lax.fori_loop(0, n_groups // U, step, carry_ref[...])


def _scan_bwd(a_s, b_s, out_ref, carry_ref, n_groups):
    row = lax.broadcasted_iota(jnp.int32, (SUB, LANE), 0)
    U = _SCAN_UNROLL

    def step(gi, carry):
        base = (n_groups // U - 1 - gi) * (SUB * U)
        parts = []
        for u in reversed(range(U)):
            i = pl.multiple_of(base + u * SUB, SUB)
            a8, b8 = a_s[pl.ds(i, SUB), :], b_s[pl.ds(i, SUB), :]
            for s in (1, 2, 4):
                a_sh = jnp.where(row < SUB - s, pltpu.roll(a8, SUB - s, 0), 1.0)
                b_sh = jnp.where(row < SUB - s, pltpu.roll(b8, SUB - s, 0), 0.0)
                b8 = a8 * b_sh + b8
                a8 = a8 * a_sh
            parts.append((i, a8, b8))
        for i, a8, b8 in parts:
            h8 = a8 * carry + b8
            out_ref[pl.ds(i, SUB), :] = h8
            carry = jnp.broadcast_to(h8[0:1, :], (SUB, LANE))
        return carry

    carry_ref[...] = lax.fori_loop(0, n_groups // U, step, carry_ref[...])


def _rglru_pre(xr, wgx_ref, bgx_ref, wga_ref, bga_ref, lam_ref):
    xrb = xr.astype(BF16)
    wgx, wga = wgx_ref[0].astype(BF16), wga_ref[0].astype(BF16)
    gx = _sigmoid(lax.dot_general(xrb, wgx, _NN, preferred_element_type=F32) + bgx_ref[...])
    ga = _sigmoid(lax.dot_general(xrb, wga, _NN, preferred_element_type=F32) + bga_ref[...])
    sp = _softplus(-lam_ref[...])
    log_a = -C_RG * ga * sp
    a = jnp.exp(log_a)
    mult = jnp.sqrt(-_expm1(2.0 * log_a))
    return gx, ga, sp, a, mult, xrb, wgx, wga


def _a_specs():
    vec = pl.BlockSpec((1, HD_A), lambda c, j: (0, c))
    mat = pl.BlockSpec((1, HD_A, HD_A), lambda c, j: (c, 0, 0))
    return [pl.BlockSpec((CONV_A, HD_A), lambda c, j: (0, c)), vec, mat, vec, mat, vec, vec]


def _a_fwd(zp, conv_w, conv_b, wgx, bgx, wga, bga, lam):
    S = zp.shape[0]
    R, nt = R_SEQ, D // HD_A
    H = SUB

    def body(zg_ref, zr_ref, cw_ref, cb_ref, wgx_ref, bgx_ref, wga_ref, bga_ref, lam_ref, ya_ref, h_ref,
             ext, a_s, b_s, hc):
        j = pl.program_id(1)

        @pl.when(j == 0)
        def _():
            ext[0:H, :] = jnp.zeros((H, HD_A), F32)
            hc[...] = jnp.zeros_like(hc)

        ext[H:H + R, :] = zr_ref[...].astype(F32)
        xr = cb_ref[...]
        for k in range(CONV_A):
            xr = xr + cw_ref[k:k + 1, :] * ext[pl.ds(H - (CONV_A - 1 - k), R), :]
        gx, _, _, a, mult, _, _, _ = _rglru_pre(xr, wgx_ref, bgx_ref, wga_ref, bga_ref, lam_ref)
        a_s[...] = a
        b_s[...] = mult * (gx * xr)
        _scan_fwd(a_s, b_s, h_ref, hc, R // SUB)
        ya_ref[...] = (_gelu(zg_ref[...].astype(F32)) * h_ref[...]).astype(BF16)
        ext[0:H, :] = ext[R:R + H, :]

    return pl.pallas_call(
        body, out_shape=[SDS((S, D + D // 2), BF16), SDS((S, D), F32)], grid=(nt, S // R),
        in_specs=[pl.BlockSpec((R, HD_A), lambda c, j: (j, c)), pl.BlockSpec((R, HD_A), lambda c, j: (j, nt + c))]
        + _a_specs(),
        out_specs=[pl.BlockSpec((R, HD_A), lambda c, j: (j, c)), pl.BlockSpec((R, HD_A), lambda c, j: (j, c))],
        scratch_shapes=[pltpu.VMEM((H + R, HD_A), F32), pltpu.VMEM((R, HD_A), F32), pltpu.VMEM((R, HD_A), F32),
                        pltpu.VMEM((SUB, HD_A), F32)],
        name="rglru_fwd", compiler_params=_cp(2),
    )(zp, zp, conv_w, conv_b, wgx, bgx, wga, bga, lam)


def _a_bwd(dyab, zp, h, conv_w, conv_b, wgx, bgx, wga, bga, lam):
    S = zp.shape[0]
    R, nt, nch = R_SEQ, D // HD_A, S // R_SEQ
    H = SUB

    def rows(c, j):
        return (nch - 1 - j, c)

    def rows_rec(c, j):
        return (nch - 1 - j, nt + c)

    def halo(c, j):
        return (jnp.maximum((nch - 1 - j) * (R // H) - 1, 0), c)

    def halo_z(c, j):
        return (jnp.maximum((nch - 1 - j) * (R // _HB) - 1, 0), nt + c)

    def body(dy_ref, zg_ref, zr_ref, zh_ref, h_ref, hh_ref, cw_ref, cb_ref, wgx_ref, bgx_ref, wga_ref, bga_ref,
             lam_ref, dzg_ref, dzr_ref, dcw_ref, dcb_ref, dwgx_ref, dbgx_ref, dwga_ref, dbga_ref, dlam_ref,
             ext_z, ext_h, ext_mu, ext_d, a_s, b_s, muc):
        j = pl.program_id(1)
        first_chunk = (nch - 1 - j) == 0

        @pl.when(j == 0)
        def _():
            ext_mu[R:R + H, :] = jnp.zeros((H, HD_A), F32)
            ext_d[R:R + H, :] = jnp.zeros((H, HD_A), F32)
            muc[...] = jnp.zeros_like(muc)
            for r in (dcw_ref, dcb_ref, dwgx_ref, dbgx_ref, dwga_ref, dbga_ref, dlam_ref):
                r[...] = jnp.zeros_like(r)

        zg = zg_ref[...].astype(F32)
        ext_z[0:H, :] = jnp.where(first_chunk, 0.0, zh_ref[_HB - H:_HB, :].astype(F32))
        ext_z[H:H + R, :] = zr_ref[...].astype(F32)
        ext_h[0:H, :] = jnp.where(first_chunk, 0.0, hh_ref[...])
        ext_h[H:H + R, :] = h_ref[...]
        xr = cb_ref[...]
        for k in range(CONV_A):
            xr = xr + cw_ref[k:k + 1, :] * ext_z[pl.ds(H - (CONV_A - 1 - k), R), :]
        gx, ga, sp, a, mult, xrb, wgxb, wgab = _rglru_pre(xr, wgx_ref, bgx_ref, wga_ref, bga_ref, lam_ref)
        gel, dgel = _gelu(zg, with_grad=True)
        dy = dy_ref[...].astype(F32)
        dh = dy * gel
        dzg_ref[...] = (dy * h_ref[...] * dgel).astype(BF16)
        a_s[...] = a
        b_s[...] = a * dh
        _scan_bwd(a_s, b_s, ext_mu, muc, R // SUB)
        lam_t = dh + ext_mu[pl.ds(1, R), :]
        ext_mu[R:R + H, :] = ext_mu[0:H, :]
        da = lam_t * ext_h[pl.ds(H - 1, R), :]
        gxr = gx * xr
        dlog_a = da * a - (lam_t * gxr) * (a * a) / mult
        dgx = lam_t * mult * xr
        dxr = lam_t * mult * gx
        lam_v = lam_ref[...]
        dlam_ref[...] += jnp.sum(dlog_a * ga, axis=0, keepdims=True) * (C_RG * _sigmoid(-lam_v))
        dpa = (dlog_a * (-C_RG * sp)) * ga * (1.0 - ga)
        dpx = dgx * gx * (1.0 - gx)
        dbga_ref[...] += jnp.sum(dpa, axis=0, keepdims=True)
        dbgx_ref[...] += jnp.sum(dpx, axis=0, keepdims=True)
        dpab, dpxb = dpa.astype(BF16), dpx.astype(BF16)
        dwga_ref[0] += lax.dot_general(xrb, dpab, _TN, preferred_element_type=F32)
        dwgx_ref[0] += lax.dot_general(xrb, dpxb, _TN, preferred_element_type=F32)
        dxr = (dxr + lax.dot_general(dpab, wgab, _NT, preferred_element_type=F32)
               + lax.dot_general(dpxb, wgxb, _NT, preferred_element_type=F32))
        dcb_ref[...] += jnp.sum(dxr, axis=0, keepdims=True)
        ext_d[0:R, :] = dxr
        dzr = jnp.zeros((R, HD_A), F32)
        for k in range(CONV_A):
            sh = CONV_A - 1 - k
            dcw_ref[k:k + 1, :] += jnp.sum(dxr * ext_z[pl.ds(H - sh, R), :], axis=0, keepdims=True)
            dzr = dzr + cw_ref[k:k + 1, :] * ext_d[pl.ds(sh, R), :]
        dzr_ref[...] = dzr.astype(BF16)
        ext_d[R:R + H, :] = ext_d[0:H, :]

    vec_o = pl.BlockSpec((1, HD_A), lambda c, j: (0, c))
    mat_o = pl.BlockSpec((1, HD_A, HD_A), lambda c, j: (c, 0, 0))
    return pl.pallas_call(
        body,
        out_shape=[SDS((S, D), BF16), SDS((S, D), BF16), SDS((CONV_A, D), F32), SDS((1, D), F32),
                   SDS((nt, HD_A, HD_A), F32), SDS((1, D), F32), SDS((nt, HD_A, HD_A), F32), SDS((1, D), F32),
                   SDS((1, D), F32)],
        grid=(nt, nch),
        in_specs=[pl.BlockSpec((R, HD_A), rows), pl.BlockSpec((R, HD_A), rows), pl.BlockSpec((R, HD_A), rows_rec),
                  pl.BlockSpec((_HB, HD_A), halo_z), pl.BlockSpec((R, HD_A), rows),
                  pl.BlockSpec((H, HD_A), halo)] + _a_specs(),
        out_specs=[pl.BlockSpec((R, HD_A), rows), pl.BlockSpec((R, HD_A), rows),
                   pl.BlockSpec((CONV_A, HD_A), lambda c, j: (0, c)), vec_o, mat_o, vec_o, mat_o, vec_o, vec_o],
        scratch_shapes=[pltpu.VMEM((H + R, HD_A), F32), pltpu.VMEM((H + R, HD_A), F32), pltpu.VMEM((R + H, HD_A), F32),
                        pltpu.VMEM((R + H, HD_A), F32), pltpu.VMEM((R, HD_A), F32), pltpu.VMEM((R, HD_A), F32),
                        pltpu.VMEM((SUB, HD_A), F32)],
        name="rglru_bwd", compiler_params=_cp(2),
    )(dyab, zp, zp, zp, h, h, conv_w, conv_b, wgx, bgx, wga, bga, lam)


_POOL_H = 16
_POOL_T0 = 2 * D // HD_A
_POOL_Y0 = D // HD_A


def _pool_mean_minus(u, ext, g, t1):
    R = u.shape[0]
    acc, wins = u, []
    for k in range(1, _POOL_H):
        acc = acc + ext[pl.ds(_POOL_H - k, R), :]
        if k + 1 in POOL_WINDOWS:
            wins.append(acc)
    win = jnp.where(g == 0, wins[0], jnp.where(g == 1, wins[1], jnp.where(g == 2, wins[2], wins[3])))
    return win / jnp.minimum(t1, _pool_width(g)) - u


def _pool_width(g):
    return jnp.where(g == 0, 2.0, jnp.where(g == 1, 4.0, jnp.where(g == 2, 8.0, 16.0)))


def _b_fwd(zp, yab, wg, bg, sc):
    S = zp.shape[0]
    R, H = R_SEQ, _POOL_H

    def body(z_ref, wg_ref, bg_ref, sc_ref, yab_in, yb_ref, ext):
        del yab_in
        g, j = pl.program_id(0), pl.program_id(1)

        @pl.when(j == 0)
        def _():
            ext[0:H, :] = jnp.zeros((H, HD_A), F32)

        u = z_ref[...].astype(F32)
        ext[H:H + R, :] = u
        t1 = (j * R + 1 + lax.broadcasted_iota(jnp.int32, (R, HD_A), 0)).astype(F32)
        p = _pool_mean_minus(u, ext, g, t1)
        lin = lax.dot_general(p.astype(BF16), wg_ref[0].astype(BF16), _NN, preferred_element_type=F32) + bg_ref[...]
        yb_ref[...] = (lin * sc_ref[...]).astype(BF16)
        ext[0:H, :] = ext[R:R + H, :]

    vec = pl.BlockSpec((1, HD_A), lambda g, j: (0, g))
    return pl.pallas_call(
        body, out_shape=SDS(yab.shape, yab.dtype), grid=(len(POOL_WINDOWS), S // R),
        in_specs=[pl.BlockSpec((R, HD_A), lambda g, j: (j, _POOL_T0 + g)),
                  pl.BlockSpec((1, HD_A, HD_A), lambda g, j: (g, 0, 0)), vec, vec, pl.BlockSpec(memory_space=pl.ANY)],
        out_specs=pl.BlockSpec((R, HD_A), lambda g, j: (j, _POOL_Y0 + g)),
        scratch_shapes=[pltpu.VMEM((H + R, HD_A), F32)], input_output_aliases={4: 0},
        name="pool_fwd", compiler_params=_cp(2),
    )(zp, wg, bg, sc, yab)


def _b_bwd(dyab, zp, wg, bg, sc):
    S = zp.shape[0]
    R, H, nch, ng = R_SEQ, _POOL_H, S // R_SEQ, len(POOL_WINDOWS)

    def body(dy_ref, z_ref, zh_ref, wg_ref, bg_ref, sc_ref, dz_ref, dwg_ref, dbg_ref, dsc_ref, ext_u, ext_q):
        g, j = pl.program_id(0), pl.program_id(1)
        jj = nch - 1 - j

        @pl.when(j == 0)
        def _():
            ext_q[R:R + H, :] = jnp.zeros((H, HD_A), F32)
            for r in (dwg_ref, dbg_ref, dsc_ref):
                r[...] = jnp.zeros_like(r)

        u = z_ref[...].astype(F32)
        ext_u[0:H, :] = jnp.where(jj == 0, 0.0, zh_ref[...].astype(F32))
        ext_u[H:H + R, :] = u
        t1 = (jj * R + 1 + lax.broadcasted_iota(jnp.int32, (R, HD_A), 0)).astype(F32)
        pb = _pool_mean_minus(u, ext_u, g, t1).astype(BF16)
        wgb = wg_ref[0].astype(BF16)
        lin = lax.dot_general(pb, wgb, _NN, preferred_element_type=F32) + bg_ref[...]
        dy = dy_ref[...].astype(F32)
        dsc_ref[...] += jnp.sum(dy * lin, axis=0, keepdims=True)
        dlin = dy * sc_ref[...]
        dbg_ref[...] += jnp.sum(dlin, axis=0, keepdims=True)
        dlb = dlin.astype(BF16)
        dwg_ref[0] += lax.dot_general(pb, dlb, _TN, preferred_element_type=F32)
        dp = lax.dot_general(dlb, wgb, _NT, preferred_element_type=F32)
        q = dp / jnp.minimum(t1, _pool_width(g))
        ext_q[0:R, :] = q
        acc, wins = q, []
        for k in range(1, H):
            acc = acc + ext_q[pl.ds(k, R), :]
            if k + 1 in POOL_WINDOWS:
                wins.append(acc)
        win = jnp.where(g == 0, wins[0], jnp.where(g == 1, wins[1], jnp.where(g == 2, wins[2], wins[3])))
        dz_ref[...] = (win - dp).astype(BF16)
        ext_q[R:R + H, :] = ext_q[0:H, :]

    vec = pl.BlockSpec((1, HD_A), lambda g, j: (0, g))
    mat = pl.BlockSpec((1, HD_A, HD_A), lambda g, j: (g, 0, 0))
    return pl.pallas_call(
        body, out_shape=[SDS((S, D // 2), BF16), SDS((ng, HD_A, HD_A), F32), SDS((1, D // 2), F32),
                         SDS((1, D // 2), F32)],
        grid=(ng, nch),
        in_specs=[pl.BlockSpec((R, HD_A), lambda g, j: (nch - 1 - j, _POOL_Y0 + g)),
                  pl.BlockSpec((R, HD_A), lambda g, j: (nch - 1 - j, _POOL_T0 + g)),
                  pl.BlockSpec((H, HD_A), lambda g, j: (jnp.maximum((nch - 1 - j) * (R // H) - 1, 0), _POOL_T0 + g)),
                  mat, vec, vec],
        out_specs=[pl.BlockSpec((R, HD_A), lambda g, j: (nch - 1 - j, g)), mat, vec, vec],
        scratch_shapes=[pltpu.VMEM((H + R, HD_A), F32), pltpu.VMEM((R + H, HD_A), F32)],
        name="pool_bwd", compiler_params=_cp(2),
    )(dyab, zp, zp, wg, bg, sc)


_CW_F = 768


def _f_fwd(hp, w, b, name):
    S = hp.shape[0]
    R, H, cw = R_SEQ, SUB, _CW_F
    nlt = cw // LANE

    def body(h_ref, w_ref, b_ref, o_ref, ext):
        j = pl.program_id(1)

        @pl.when(j == 0)
        def _():
            ext[:, 0:H, :] = jnp.zeros((nlt, H, LANE), F32)

        def stage(r0, lt):
            ext[lt, pl.ds(pl.multiple_of(r0 + H, SUB), _RB), :] = h_ref[pl.ds(r0, _RB), _lanes(lt)].astype(F32)

        def main(r0, lt):
            ls = _lanes(lt)
            gp = b_ref[:, ls]
            for k in range(CONV_F):
                gp = gp + w_ref[k:k + 1, ls] * ext[lt, pl.ds(r0 + (H - (CONV_F - 1 - k)), _RB), :]
            up = h_ref[pl.ds(r0, _RB), _lanes(lt + nlt)].astype(F32)
            o_ref[pl.ds(r0, _RB), ls] = (_gelu(gp) * up).astype(BF16)

        _sub_blocks(R, cw, stage)
        _sub_blocks(R, cw, main)
        ext[:, 0:H, :] = ext[:, R:R + H, :]

    return pl.pallas_call(
        body, out_shape=SDS((S, D_FF), BF16), grid=(D_FF // cw, S // R),
        in_specs=[pl.BlockSpec((R, 2 * cw), lambda c, j: (j, c)), pl.BlockSpec((CONV_F, cw), lambda c, j: (0, c)),
                  pl.BlockSpec((1, cw), lambda c, j: (0, c))],
        out_specs=pl.BlockSpec((R, cw), lambda c, j: (j, c)),
        scratch_shapes=[pltpu.VMEM((nlt, H + R, LANE), F32)], name=name, compiler_params=_cp(2),
    )(hp, w, b)


def _f_bwd(dact, hp, w, b, name):
    S = hp.shape[0]
    R, H, cw, nch = R_SEQ, SUB, _CW_F, S // R_SEQ
    nlt = cw // LANE

    def body(da_ref, h_ref, hh_ref, w_ref, b_ref, dh_ref, dw_ref, db_ref, ext_g, ext_d, acc):
        j = pl.program_id(1)
        jj = nch - 1 - j

        @pl.when(j == 0)
        def _():
            ext_d[:, R:R + H, :] = jnp.zeros((nlt, H, LANE), F32)
            acc[...] = jnp.zeros_like(acc)

        for lt in range(nlt):
            ext_g[lt, 0:H, :] = jnp.where(jj == 0, 0.0, hh_ref[_HB - H:_HB, lt * LANE:(lt + 1) * LANE].astype(F32))

        def stage(r0, lt):
            ext_g[lt, pl.ds(pl.multiple_of(r0 + H, SUB), _RB), :] = h_ref[pl.ds(r0, _RB), _lanes(lt)].astype(F32)

        def first(r0, lt):
            ls, lu, rs = _lanes(lt), _lanes(lt + nlt), pl.ds(r0, _RB)
            taps = [ext_g[lt, pl.ds(r0 + (H - (CONV_F - 1 - k)), _RB), :] for k in range(CONV_F)]
            gp = b_ref[:, ls]
            for k in range(CONV_F):
                gp = gp + w_ref[k:k + 1, ls] * taps[k]
            gel, dgel = _gelu(gp, with_grad=True)
            da = da_ref[rs, ls].astype(F32)
            dh_ref[rs, lu] = (da * gel).astype(BF16)
            dgp = da * h_ref[rs, lu].astype(F32) * dgel
            ext_d[lt, rs, :] = dgp
            acc[CONV_F * SUB:(CONV_F + 1) * SUB, ls] += _psum8(dgp)
            for k in range(CONV_F):
                acc[k * SUB:(k + 1) * SUB, ls] += _psum8(dgp * taps[k])

        def second(r0, lt):
            ls = _lanes(lt)
            dhg = w_ref[CONV_F - 1:CONV_F, ls] * ext_d[lt, pl.ds(r0, _RB), :]
            for k in range(CONV_F - 1):
                dhg = dhg + w_ref[k:k + 1, ls] * ext_d[lt, pl.ds(r0 + (CONV_F - 1 - k), _RB), :]
            dh_ref[pl.ds(r0, _RB), ls] = dhg.astype(BF16)

        _sub_blocks(R, cw, stage)
        _sub_blocks(R, cw, first)
        _sub_blocks(R, cw, second)
        ext_d[:, R:R + H, :] = ext_d[:, 0:H, :]

        @pl.when(j == nch - 1)
        def _():
            for k in range(CONV_F):
                dw_ref[k:k + 1, :] = jnp.sum(acc[k * SUB:(k + 1) * SUB, :], axis=0, keepdims=True)
            db_ref[...] = jnp.sum(acc[CONV_F * SUB:(CONV_F + 1) * SUB, :], axis=0, keepdims=True)

    rows = lambda c, j: (nch - 1 - j, c)
    return pl.pallas_call(
        body, out_shape=[SDS((S, 2 * D_FF), BF16), SDS((CONV_F, D_FF), F32), SDS((1, D_FF), F32)],
        grid=(D_FF // cw, nch),
        in_specs=[pl.BlockSpec((R, cw), rows), pl.BlockSpec((R, 2 * cw), rows),
                  pl.BlockSpec((_HB, 2 * cw), lambda c, j: (jnp.maximum((nch - 1 - j) * (R // _HB) - 1, 0), c)),
                  pl.BlockSpec((CONV_F, cw), lambda c, j: (0, c)), pl.BlockSpec((1, cw), lambda c, j: (0, c))],
        out_specs=[pl.BlockSpec((R, 2 * cw), rows), pl.BlockSpec((CONV_F, cw), lambda c, j: (0, c)),
                   pl.BlockSpec((1, cw), lambda c, j: (0, c))],
        scratch_shapes=[pltpu.VMEM((nlt, H + R, LANE), F32), pltpu.VMEM((nlt, R + H, LANE), F32),
                        pltpu.VMEM(((CONV_F + 1) * SUB, cw), F32)], name=name,
        compiler_params=_cp(2),
    )(dact, hp, hp, w, b)


_CW_C = 256
_H_C = 32


def _c_fwd(h1p, w, b):
    S = h1p.shape[0]
    R, H, cw = R_SEQ, _H_C, _CW_C
    nlt = cw // LANE

    def body(h_ref, w_ref, b_ref, o_ref, ext):
        j = pl.program_id(1)

        @pl.when(j == 0)
        def _():
            ext[:, 0:H, :] = jnp.zeros((nlt, H, LANE), F32)

        def stage(r0, lt):
            rs = pl.ds(r0, _RB)
            gate = h_ref[rs, _lanes(lt + nlt)].astype(F32)
            ext[lt, pl.ds(pl.multiple_of(r0 + H, SUB), _RB), :] = h_ref[rs, _lanes(lt)].astype(F32) * _sigmoid(gate)

        def main(r0, lt):
            ls = _lanes(lt)
            cv = b_ref[:, ls]
            for k in range(CONV_C):
                cv = cv + w_ref[k:k + 1, ls] * ext[lt, pl.ds(r0 + (H - (CONV_C - 1 - k)), _RB), :]
            o_ref[pl.ds(r0, _RB), ls] = cv

        _sub_blocks(R, cw, stage)
        _sub_blocks(R, cw, main)
        ext[:, 0:H, :] = ext[:, R:R + H, :]

    return pl.pallas_call(
        body, out_shape=SDS((S, D), F32), grid=(D // cw, S // R),
        in_specs=[pl.BlockSpec((R, 2 * cw), lambda c, j: (j, c)), pl.BlockSpec((CONV_C, cw), lambda c, j: (0, c)),
                  pl.BlockSpec((1, cw), lambda c, j: (0, c))],
        out_specs=pl.BlockSpec((R, cw), lambda c, j: (j, c)),
        scratch_shapes=[pltpu.VMEM((nlt, H + R, LANE), F32)], name="conf_conv_fwd", compiler_params=_cp(2),
    )(h1p, w, b)


def _c_bwd(dcv, h1p, w):
    S = h1p.shape[0]
    R, H, cw, nch = R_SEQ, _H_C, _CW_C, S // R_SEQ
    nlt = cw // LANE
    a_b, a_val, a_gate = CONV_C * SUB, (CONV_C + 1) * SUB, (CONV_C + 2) * SUB

    def body(dc_ref, h_ref, hh_ref, w_ref, dh_ref, dw_ref, db_ref, db1_ref, ext_u, ext_d, acc):
        j = pl.program_id(1)
        jj = nch - 1 - j

        @pl.when(j == 0)
        def _():
            ext_d[:, R:R + H, :] = jnp.zeros((nlt, H, LANE), F32)
            acc[...] = jnp.zeros_like(acc)

        for lt in range(nlt):
            ext_u[lt, 0:H, :] = jnp.where(
                jj == 0, 0.0, hh_ref[:, lt * LANE:(lt + 1) * LANE].astype(F32)
                * _sigmoid(hh_ref[:, cw + lt * LANE:cw + (lt + 1) * LANE].astype(F32)))

        def stage(r0, lt):
            rs, ls = pl.ds(r0, _RB), _lanes(lt)
            gate = h_ref[rs, _lanes(lt + nlt)].astype(F32)
            ext_u[lt, pl.ds(pl.multiple_of(r0 + H, SUB), _RB), :] = h_ref[rs, ls].astype(F32) * _sigmoid(gate)
            ext_d[lt, rs, :] = dc_ref[rs, ls]

        def first(r0, lt):
            ls = _lanes(lt)
            dc = dc_ref[pl.ds(r0, _RB), ls]
            acc[a_b:a_b + SUB, ls] += _psum8(dc)
            for k in range(CONV_C):
                tap = ext_u[lt, pl.ds(r0 + (H - (CONV_C - 1 - k)), _RB), :]
                acc[k * SUB:(k + 1) * SUB, ls] += _psum8(dc * tap)

        def second(r0, lt):
            rs, ls, lg = pl.ds(r0, _RB), _lanes(lt), _lanes(lt + nlt)
            du = w_ref[CONV_C - 1:CONV_C, ls] * ext_d[lt, rs, :]
            for k in range(CONV_C - 1):
                du = du + w_ref[k:k + 1, ls] * ext_d[lt, pl.ds(r0 + (CONV_C - 1 - k), _RB), :]
            val = h_ref[rs, ls].astype(F32)
            sg = _sigmoid(h_ref[rs, lg].astype(F32))
            dval = du * sg
            dgate = du * val * sg * (1.0 - sg)
            acc[a_val:a_val + SUB, ls] += _psum8(dval)
            acc[a_gate:a_gate + SUB, ls] += _psum8(dgate)
            dh_ref[rs, ls] = dval.astype(BF16)
            dh_ref[rs, lg] = dgate.astype(BF16)

        _sub_blocks(R, cw, stage)
        _sub_blocks(R, cw, first)
        _sub_blocks(R, cw, second)
        ext_d[:, R:R + H, :] = ext_d[:, 0:H, :]

        @pl.when(j == nch - 1)
        def _():
            for k in range(CONV_C):
                dw_ref[k:k + 1, :] = jnp.sum(acc[k * SUB:(k + 1) * SUB, :], axis=0, keepdims=True)
            db_ref[...] = jnp.sum(acc[a_b:a_b + SUB, :], axis=0, keepdims=True)
            db1_ref[:, 0:cw] = jnp.sum(acc[a_val:a_val + SUB, :], axis=0, keepdims=True)
            db1_ref[:, cw:2 * cw] = jnp.sum(acc[a_gate:a_gate + SUB, :], axis=0, keepdims=True)

    rows = lambda c, j: (nch - 1 - j, c)
    return pl.pallas_call(
        body, out_shape=[SDS((S, 2 * D), BF16), SDS((CONV_C, D), F32), SDS((1, D), F32), SDS((1, 2 * D), F32)],
        grid=(D // cw, nch),
        in_specs=[pl.BlockSpec((R, cw), rows), pl.BlockSpec((R, 2 * cw), rows),
                  pl.BlockSpec((H, 2 * cw), lambda c, j: (jnp.maximum((nch - 1 - j) * (R // H) - 1, 0), c)),
                  pl.BlockSpec((CONV_C, cw), lambda c, j: (0, c))],
        out_specs=[pl.BlockSpec((R, 2 * cw), rows), pl.BlockSpec((CONV_C, cw), lambda c, j: (0, c)),
                   pl.BlockSpec((1, cw), lambda c, j: (0, c)), pl.BlockSpec((1, 2 * cw), lambda c, j: (0, c))],
        scratch_shapes=[pltpu.VMEM((nlt, H + R, LANE), F32), pltpu.VMEM((nlt, R + H, LANE), F32),
                        pltpu.VMEM(((CONV_C + 3) * SUB, cw), F32)], name="conf_conv_bwd",
        compiler_params=_cp(2),
    )(dcv, h1p, h1p, w)


def _local_step(x, mem, tgt, W, fetch=None, send=None):
    G = {}
    W = dict(W)

    def arrive(group, after):
        if fetch is None:
            return None
        got, tok = fetch(group, after)
        for key, val in got.items():
            W[key] = {**W.get(key, {}), **val} if isinstance(val, dict) else val
        return tok

    def gain(g, tok):
        return g if tok is None else g + tok

    def sent(group):
        return None if send is None else send(group, G)

    def xattn_fwd(xin, l):
        n = _rms_fwd(xin, W["xa_norm"][l:l + 1], f"xa_norm_fwd{l}")
        tok = arrive(("xa", l), n)
        mn = _rms_fwd(mem, gain(W["xa_mem_norm"][l:l + 1], tok), f"xa_memnorm_fwd{l}")
        q = _mm_nn(n, W["xa_wq"][l], out_dtype=BF16, name=f"xa_q{l}")
        k = _mm_nn(mn, W["xa_wk"][l], out_dtype=BF16, name=f"xa_k{l}")
        v = _mm_nn(mn, W["xa_wv"][l], out_dtype=BF16, name=f"xa_v{l}")
        o = _attn_fwd(q, k, v, f"xa_attn_fwd{l}")
        xout = _mm_nn(o, W["xa_wo"][l], out_dtype=F32, name=f"xa_o{l}", add=xin)
        return xout, (xin, n, q, mn, k, v, o)

    def xattn_bwd(dx, dxb, saved, l):
        xin, n, q, mn, k, v, o = saved
        do = _mm_nt(dxb, W["xa_wo"][l], out_dtype=BF16, name=f"xa_do{l}")
        G[f"xa_wo{l}"] = _mm_tn(o, dxb, out_dtype=BF16, name=f"xa_dwo{l}")
        dq, dk, dv = _attn_bwd(q, k, v, do, f"xa_attn_bwd{l}")
        dkb, dvb = dk.astype(BF16), dv.astype(BF16)
        G[f"xa_wq{l}"] = _mm_tn(n, dq, out_dtype=BF16, name=f"xa_dwq{l}")
        G[f"xa_wk{l}"] = _mm_tn(mn, dkb, out_dtype=BF16, name=f"xa_dwk{l}")
        G[f"xa_wv{l}"] = _mm_tn(mn, dvb, out_dtype=BF16, name=f"xa_dwv{l}")
        tok = sent(("xa", l))
        dmn = _mm_nt(dkb, W["xa_wk"][l], out_dtype=F32, name=f"xa_dmn_k{l}")
        dmn = _mm_nt(dvb, W["xa_wv"][l], out_dtype=F32, name=f"xa_dmn_v{l}", add=dmn)
        (G[f"xa_mem_norm{l}"],) = _rms_bwd(mem, W["xa_mem_norm"][l:l + 1], dmn, None, f"xa_memnorm_bwd{l}")
        dn = _mm_nt(dq, W["xa_wq"][l], out_dtype=F32, name=f"xa_dn{l}")
        dx, dxb, G[f"xa_norm{l}"] = _rms_bwd(xin, gain(W["xa_norm"][l:l + 1], tok), dn, dx, f"xa_norm_bwd{l}")
        return dx, dxb

    def ffn_fwd(xin, l):
        n = _rms_fwd(xin, W["f_norm"][l:l + 1], f"f_norm_fwd{l}")
        tok = arrive(("f", l), n)
        hp = _mm_nn(n, W["f_w_up"][l], out_dtype=BF16, name=f"f_up{l}")
        act = _f_fwd(hp, W["f_dw_w"][l], gain(W["f_dw_b"][l:l + 1], tok), f"f_conv_fwd{l}")
        xout = _mm_nn(act, W["f_w_down"][l], out_dtype=F32, name=f"f_down{l}", add=xin)
        return xout, (xin, n, hp, act)

    def ffn_bwd(dx, dxb, saved, l):
        xin, n, hp, act = saved
        dact = _mm_nt(dxb, W["f_w_down"][l], out_dtype=BF16, name=f"f_dact{l}")
        G[f"f_w_down{l}"] = _mm_tn(act, dxb, out_dtype=BF16, name=f"f_dwdown{l}")
        dhp, G[f"f_dw_w{l}"], G[f"f_dw_b{l}"] = _f_bwd(dact, hp, W["f_dw_w"][l], W["f_dw_b"][l:l + 1], f"f_conv_bwd{l}")
        G[f"f_w_up{l}"] = _mm_tn(n, dhp, out_dtype=BF16, name=f"f_dwup{l}", blocks=_CW_F)
        tok = sent(("f", l))
        dn = _mm_nt(dhp, W["f_w_up"][l], out_dtype=F32, name=f"f_dn{l}")
        dx, dxb, G[f"f_norm{l}"] = _rms_bwd(xin, gain(W["f_norm"][l:l + 1], tok), dn, dx, f"f_norm_bwd{l}")
        return dx, dxb

    n0 = _rms_fwd(x, W["ab_norm"], "ab_norm_fwd")
    tok = arrive(("ab", 0), n0)
    a_par = (W["a_conv_w"], gain(W["a_conv_b"], tok), W["a_gate_x_w"], W["a_gate_x_b"], W["a_gate_a_w"],
             W["a_gate_a_b"], W["a_lambda"])
    b_par = (W["b_group_w"], W["b_group_b"], W["b_scale"])
    zp = _mm_nn(n0, W["ab_w_in"], out_dtype=BF16, name="ab_in")
    yab, h_a = _a_fwd(zp, *a_par)
    yab = _b_fwd(zp, yab, *b_par)
    arrive(("ab", 1), yab)
    x1 = _mm_nn(yab, W["ab_w_out"], out_dtype=F32, name="ab_out", add=x)
    x2, s_xa0 = xattn_fwd(x1, 0)
    x3, s_f0 = ffn_fwd(x2, 0)
    n3 = _rms_fwd(x3, W["c_norm"], "c_norm_fwd")
    tok = arrive(("c", 0), n3)
    h1p = _mm_nn(n3, W["c_w_pw1"], out_dtype=BF16, name="c_pw1", bias=gain(W["c_b_pw1"], tok))
    cv = _c_fwd(h1p, W["c_dw_w"], W["c_dw_b"])
    sc = _ln_silu_fwd(cv, W["c_ln_g"], W["c_ln_b"])
    x4 = _mm_nn(sc, W["c_w_pw2"], out_dtype=F32, name="c_pw2", bias=W["c_b_pw2"], add=x3)
    x5, s_xa1 = xattn_fwd(x4, 1)
    x6, s_f1 = ffn_fwd(x5, 1)
    loss, dx, dxb, G["final_norm"] = _loss_head(x6, W["final_norm"], tgt)

    dx, dxb = ffn_bwd(dx, dxb, s_f1, 1)
    dx, dxb = xattn_bwd(dx, dxb, s_xa1, 1)
    dsc = _mm_nt(dxb, W["c_w_pw2"], out_dtype=BF16, name="c_dsc")
    G["c_w_pw2"] = _mm_tn(sc, dxb, out_dtype=BF16, name="c_dwpw2")
    dcv, G["c_ln_g"], G["c_ln_b"], G["c_b_pw2"] = _ln_silu_bwd(dsc, cv, W["c_ln_g"], W["c_ln_b"], dx)
    dh1p, G["c_dw_w"], G["c_dw_b"], G["c_b_pw1"] = _c_bwd(dcv, h1p, W["c_dw_w"])
    G["c_w_pw1"] = _mm_tn(n3, dh1p, out_dtype=BF16, name="c_dwpw1", blocks=_CW_C)
    tok = sent(("c", 0))
    dn3 = _mm_nt(dh1p, W["c_w_pw1"], out_dtype=F32, name="c_dn")
    dx, dxb, G["c_norm"] = _rms_bwd(x3, gain(W["c_norm"], tok), dn3, dx, "c_norm_bwd")
    dx, dxb = ffn_bwd(dx, dxb, s_f0, 0)
    dx, dxb = xattn_bwd(dx, dxb, s_xa0, 0)
    dyab = _mm_nt(dxb, W["ab_w_out"], out_dtype=BF16, name="ab_dyab")
    G["ab_w_out"] = _mm_tn(yab, dxb, out_dtype=BF16, name="ab_dwout")
    tok = sent(("ab", 1))
    a_par = (a_par[0], gain(a_par[1], tok)) + a_par[2:]
    (dzg, dzr, G["a_conv_w"], G["a_conv_b"], G["a_gate_x_w"], G["a_gate_x_b"], G["a_gate_a_w"], G["a_gate_a_b"],
     G["a_lambda"]) = _a_bwd(dyab, zp, h_a, *a_par)
    dzq, G["b_group_w"], G["b_group_b"], G["b_scale"] = _b_bwd(dyab, zp, *b_par)
    G["ab_w_in"] = jnp.concatenate(
        [_mm_tn(n0, dz, out_dtype=BF16, name=f"ab_dwin_{part}")
         for part, dz in (("gate", dzg), ("rec", dzr), ("pool", dzq))], axis=1)
    tok = sent(("ab", 0))
    dn0 = _mm_nt_cols([dzg, dzr, dzq], W["ab_w_in"], out_dtype=F32, name="ab_dn")
    dx, _, G["ab_norm"] = _rms_bwd(x, gain(W["ab_norm"], tok), dn0, dx, "ab_norm_bwd")
    return loss, dx, G


def _my_place():
    x, y, c = lax.axis_index("x"), lax.axis_index("y"), lax.axis_index("c")
    return x, y, c


def _all_gather(shards, name):
    n = len(shards)

    def body(*refs):
        ins, outs = refs[:n], refs[n:2 * n]
        send_sems, recv_sems, local_sems = refs[2 * n:]
        x, y, c = _my_place()
        me, sibling = (x, y, c), (x, y, 1 - c)
        chips = [(1 - x, y), (x, 1 - y), (1 - x, 1 - y)]

        def slab(a, place):
            px, py, pc = place
            return outs[a].at[4 * px + 2 * py + pc]

        def copy(a, k, block, to, src=None):
            return pltpu.make_async_remote_copy(
                src_ref=slab(a, block) if src is None else src, dst_ref=slab(a, block),
                send_sem=send_sems.at[a, k], recv_sem=recv_sems.at[a, k], device_id=to, device_id_type=MESH)

        mine = [pltpu.make_async_copy(ins[a], slab(a, me), local_sems.at[a]) for a in range(n)]
        for cp in mine:
            cp.start()
        first = []
        for j, chip in enumerate(chips):
            first += [copy(a, 1 + j, me, (*chip, c), src=ins[a]) for a in range(n)]
        first += [copy(a, 0, me, sibling, src=ins[a]) for a in range(n)]
        for cp in first:
            cp.start()
        passed = []
        for j, chip in enumerate(chips):
            for a in range(n):
                copy(a, 1 + j, (*chip, c), me).wait_recv()
                cp = copy(a, 4 + j, (*chip, c), sibling)
                cp.start()
                passed.append(cp)
        for a in range(n):
            copy(a, 0, sibling, me).wait_recv()
        for j, chip in enumerate(chips):
            for a in range(n):
                copy(a, 4 + j, (*chip, 1 - c), me).wait_recv()
        for cp in first + passed:
            cp.wait_send()
        for cp in mine:
            cp.wait()

    any_spec = pl.BlockSpec(memory_space=pl.ANY)
    return pl.pallas_call(
        body, out_shape=[SDS((N_DEV,) + s.shape, s.dtype) for s in shards], in_specs=[any_spec] * n,
        out_specs=[any_spec] * n,
        scratch_shapes=[pltpu.SemaphoreType.DMA((n, 7)), pltpu.SemaphoreType.DMA((n, 7)), pltpu.SemaphoreType.DMA((n,))],
        name=name,
    )(*shards)


_HBM = pl.BlockSpec(memory_space=pltpu.HBM)
_SEM = pl.BlockSpec(memory_space=pltpu.SEMAPHORE)
_EFFECT = pltpu.SideEffectType.DATAFLOW_SIDE_EFFECTING


def _peer_places():
    x, y, c = _my_place()
    peers = []
    for k in range(1, N_DEV):
        px = 1 - x if (k >> 2) & 1 else x
        py = 1 - y if (k >> 1) & 1 else y
        pc = 1 - c if k & 1 else c
        peers.append(((px, py, pc), 4 * px + 2 * py + pc))
    return (x, y, c), 4 * x + 2 * y + c, peers


def _send_start(srcs, per_dest, name):
    n = len(srcs)
    lands = [lax.empty((N_DEV,) + (s.shape[1:] if per_dest else s.shape), s.dtype) for s in srcs]

    def body(*refs):
        src, land = refs[:n], refs[n:2 * n]
        outs = refs[2 * n:]
        send, recv, token = outs[:n], outs[n:2 * n], outs[4 * n]
        _, me, peers = _peer_places()
        for a in range(n):
            for peer, pidx in peers:
                pltpu.make_async_remote_copy(
                    src_ref=src[a].at[pidx] if per_dest else src[a], dst_ref=land[a].at[me], send_sem=send[a],
                    recv_sem=recv[a], device_id=peer, device_id_type=MESH).start()
        token[...] = jnp.zeros_like(token)

    hbm = lambda a: pltpu.HBM(a.shape, a.dtype)
    sem = pltpu.SemaphoreType.DMA(())
    res = pl.pallas_call(
        body, name=name,
        out_shape=tuple([sem] * (2 * n) + [hbm(s) for s in srcs] + [hbm(l) for l in lands]
                        + [SDS((SUB, LANE), F32)]),
        in_specs=[_HBM] * (2 * n),
        out_specs=tuple([_SEM] * (2 * n) + [_HBM] * (2 * n) + [pl.BlockSpec(memory_space=pltpu.VMEM)]),
        input_output_aliases={i: 2 * n + i for i in range(2 * n)},
        compiler_params=pltpu.CompilerParams(has_side_effects=_EFFECT),
    )(*[pltpu.with_memory_space_constraint(s, pltpu.HBM) for s in srcs],
      *[pltpu.with_memory_space_constraint(l, pltpu.HBM) for l in lands])
    return res[:n], res[n:2 * n], res[2 * n:3 * n], res[3 * n:4 * n], res[4 * n]


def _send_wait(send, recv, srcs, lands, after, per_dest, name):
    n = len(srcs)

    def body(*refs):
        src, land = refs[:n], refs[n:2 * n]
        send_s, recv_s = refs[2 * n:3 * n], refs[3 * n:4 * n]
        token = refs[-1]
        place, _, _ = _peer_places()
        for a in range(n):
            seven = land[a].at[pl.ds(0, N_DEV - 1)]
            copy = pltpu.make_async_remote_copy(
                src_ref=src[a].at[pl.ds(0, N_DEV - 1)] if per_dest else seven, dst_ref=seven, send_sem=send_s[a],
                recv_sem=recv_s[a], device_id=place, device_id_type=MESH)
            copy.wait_send()
            copy.wait_recv()
        token[...] = jnp.zeros_like(token)

    hbm = lambda a: pltpu.HBM(a.shape, a.dtype)
    res = pl.pallas_call(
        body, name=name,
        out_shape=tuple([hbm(s) for s in srcs] + [hbm(l) for l in lands] + [SDS((SUB, LANE), F32)]),
        in_specs=[_HBM] * (2 * n) + [_SEM] * (2 * n) + [pl.BlockSpec(memory_space=pl.ANY)],
        out_specs=tuple([_HBM] * (2 * n) + [pl.BlockSpec(memory_space=pltpu.VMEM)]),
        input_output_aliases={i: i for i in range(2 * n)},
        compiler_params=pltpu.CompilerParams(has_side_effects=_EFFECT),
    )(*srcs, *lands, *send, *recv, after)
    return res[:n], res[n:2 * n], res[2 * n]


def _adamw_math(w, g, m, v):
    m = ADAM_B1 * m + (1.0 - ADAM_B1) * g
    v = ADAM_B2 * v + (1.0 - ADAM_B2) * (g * g)
    m_hat = m / (1.0 - ADAM_B1 ** ADAM_STEP)
    v_hat = v / (1.0 - ADAM_B2 ** ADAM_STEP)
    delta = -ADAM_LR * (m_hat / (jnp.sqrt(v_hat) + ADAM_EPS) + ADAM_WD * w)
    return delta, m, v


def _row_tile(r, c, itemsize_rows):
    cap = max(SUB, (itemsize_rows // (4 * c)) // SUB * SUB)
    if r <= cap:
        return r
    best = None
    for t in range(SUB, cap + 1, SUB):
        if r % t == 0:
            best = t
    return best if best is not None else r


def _sum_adamw(landing, w, m, v, name, layer=0, prev=None):
    _, r, c = landing.shape
    tr = _row_tile(r, c, 1 << 20)
    off = layer * (r // tr)

    def body(l_ref, w_ref, m_ref, v_ref, *rest):
        g_ref, d_ref, mo_ref, vo_ref = rest[-4:]
        g = l_ref[0].astype(F32)
        for s in range(1, N_DEV):
            g = g + l_ref[s].astype(F32)
        g_ref[...] = g
        d_ref[...], mo_ref[...], vo_ref[...] = _adamw_math(w_ref[...], g, m_ref[...], v_ref[...])

    blk = pl.BlockSpec((tr, c), lambda i: (i + off, 0))
    n_prev = 0 if prev is None else 4
    return pl.pallas_call(
        body, out_shape=[SDS(w.shape, F32)] * 4, grid=(r // tr,),
        in_specs=[pl.BlockSpec((N_DEV, tr, c), lambda i: (0, i, 0)), blk, blk, blk]
        + [pl.BlockSpec(memory_space=pl.ANY)] * n_prev,
        out_specs=[blk] * 4, input_output_aliases={4 + i: i for i in range(n_prev)}, name=name,
        compiler_params=_cp(1),
    )(landing, w, m, v, *([] if prev is None else prev))


def _sum8(landing, name):
    _, r, c = landing.shape

    def body(l_ref, g_ref):
        g = l_ref[0]
        for s in range(1, N_DEV):
            g = g + l_ref[s]
        g_ref[...] = g

    return pl.pallas_call(body, out_shape=SDS((r, c), F32), name=name, compiler_params=_cp(0))(landing)


def _adamw(g, w, m, v, name):
    r, c = g.shape
    tr = _row_tile(r, c, 1 << 20)

    def body(g_ref, w_ref, m_ref, v_ref, d_ref, mo_ref, vo_ref):
        d_ref[...], mo_ref[...], vo_ref[...] = _adamw_math(w_ref[...], g_ref[...], m_ref[...], v_ref[...])

    blk = pl.BlockSpec((tr, c), lambda i: (i, 0))
    return pl.pallas_call(body, out_shape=[SDS((r, c), F32)] * 3, grid=(r // tr,), in_specs=[blk] * 4,
                          out_specs=[blk] * 3, name=name, compiler_params=_cp(1))(g, w, m, v)


_BIG = {
    "ab_w_in": (1, D, 320), "ab_w_out": (1, 192, D), "c_w_pw1": (1, D, 256), "c_w_pw2": (1, 128, D),
    "xa_wq": (2, 128, D), "xa_wk": (2, 128, D), "xa_wv": (2, 128, D), "xa_wo": (2, 128, D),
    "f_w_up": (2, D, 768), "f_w_down": (2, 384, D),
}
_SMALL_SHARDED = {
    "a_conv_w": (1, 4, 128), "c_norm": (1, 128), "c_b_pw1": (1, 256), "c_dw_w": (1, 31, 128), "c_dw_b": (1, 128),
    "c_ln_g": (1, 128), "c_ln_b": (1, 128), "c_b_pw2": (1, 128), "f_dw_w": (2, 3, 384),
}
_REPL = {
    "ab_norm": (1, D), "a_conv_b": (1, D), "a_gate_x_w": (1, 8, 128, 128), "a_gate_x_b": (1, D),
    "a_gate_a_w": (1, 8, 128, 128), "a_gate_a_b": (1, D), "a_lambda": (1, D), "b_group_w": (1, 4, 128, 128),
    "b_group_b": (1, 512), "b_scale": (1, 512), "xa_norm": (2, D), "xa_mem_norm": (2, D), "f_norm": (2, D),
    "f_dw_b": (2, D_FF), "final_norm": (D,),
}


def _size(shape):
    n = 1
    for s in shape:
        n *= s
    return n


_N_SS = sum(_size(s) for s in _SMALL_SHARDED.values())
_N_REPL = sum(_size(s) for s in _REPL.values())
_REPL_ROWS = -(-_N_REPL // (N_DEV * SUB * LANE)) * SUB
_SS_ROWS = _N_SS // LANE
_SMALL_ROWS = -(-(_REPL_ROWS + _SS_ROWS) // SUB) * SUB


def _pack(parts, rows):
    flat = jnp.concatenate([p.reshape(-1).astype(F32) for p in parts])
    return jnp.pad(flat, (0, rows * LANE - flat.shape[0])).reshape(rows, LANE)


def _unpack(buf, table):
    flat, out, off = buf.reshape(-1), {}, 0
    for name, shape in table.items():
        n = _size(shape)
        out[name] = flat[off:off + n].reshape(shape)
        off += n
    return out


def _pair_blocks(v, bw):
    lead, n = v.shape[:-1], v.shape[-1]
    return jnp.swapaxes(v.reshape(lead + (2, n // (2 * bw), bw)), -3, -2).reshape(lead + (n,))


def _unpair_blocks(v, bw):
    lead, n = v.shape[:-1], v.shape[-1]
    return jnp.swapaxes(v.reshape(lead + (n // (2 * bw), 2, bw)), -3, -2).reshape(lead + (n,))


_GROUPS = {
    ("ab", 0): (("ab_w_in", 0),),
    ("ab", 1): (("ab_w_out", 0),),
    ("xa", 0): (("xa_wq", 0), ("xa_wk", 0), ("xa_wv", 0), ("xa_wo", 0)),
    ("f", 0): (("f_w_up", 0), ("f_w_down", 0)),
    ("c", 0): (("c_w_pw1", 0), ("c_w_pw2", 0)),
    ("xa", 1): (("xa_wq", 1), ("xa_wk", 1), ("xa_wv", 1), ("xa_wo", 1)),
    ("f", 1): (("f_w_up", 1), ("f_w_down", 1)),
}
_SEND_GROUPS = _GROUPS


def _weight_layout(name, g):
    if name == "ab_w_in":
        return jnp.swapaxes(g, 0, 1).reshape(D, N_DEV * 320)
    if name in ("c_w_pw1", "f_w_up"):
        return g
    return g.reshape(N_DEV * g.shape[1], D)


def _grad_blocks(name, l, G):
    _, r, c = _BIG[name]
    if name == "ab_w_in":
        return jnp.swapaxes(G[name].reshape(D, N_DEV, 320), 0, 1)
    if name == "c_w_pw1":
        return G[name]
    if name == "f_w_up":
        return G[f"{name}{l}"]
    return (G[name] if _BIG[name][0] == 1 else G[f"{name}{l}"]).reshape(N_DEV, r, c)


def _small_layouts(sm):
    W = {}
    sm = sm.reshape(N_DEV, -1)
    off = 0
    for name, shape in _SMALL_SHARDED.items():
        n = _size(shape)
        blocks = sm[:, off:off + n].reshape((N_DEV,) + shape)
        off += n
        W[name] = jnp.moveaxis(blocks, 0, -2).reshape(shape[:-1] + (N_DEV * shape[-1],))
    W["a_conv_w"], W["c_dw_w"] = W["a_conv_w"][0], W["c_dw_w"][0]
    W["c_b_pw1"] = _pair_blocks(W["c_b_pw1"], _CW_C)
    return W


def _with_own(land, src, me, per_dest):
    own = lax.dynamic_slice_in_dim(src, me, 1, 0) if per_dest else src[None]
    return lax.dynamic_update_slice_in_dim(land, own, me, 0)


def _to_dest_major(g, shape):
    full = g.reshape(shape[:-1] + (N_DEV, shape[-1]))
    return jnp.moveaxis(full, -2, 0).reshape(N_DEV, -1)


def kernel(x, mem, ab_norm, ab_w_in, a_conv_w, a_conv_b, a_gate_x_w, a_gate_x_b, a_gate_a_w, a_gate_a_b, a_lambda, b_group_w, b_group_b, b_scale, ab_w_out, c_norm, c_w_pw1, c_b_pw1, c_dw_w, c_dw_b, c_ln_g, c_ln_b, c_w_pw2, c_b_pw2, xa_norm, xa_mem_norm, xa_wq, xa_wk, xa_wv, xa_wo, f_norm, f_w_up, f_dw_w, f_dw_b, f_w_down, final_norm, loss_target, m_ab_norm, m_ab_w_in, m_a_conv_w, m_a_conv_b, m_a_gate_x_w, m_a_gate_x_b, m_a_gate_a_w, m_a_gate_a_b, m_a_lambda, m_b_group_w, m_b_group_b, m_b_scale, m_ab_w_out, m_c_norm, m_c_w_pw1, m_c_b_pw1, m_c_dw_w, m_c_dw_b, m_c_ln_g, m_c_ln_b, m_c_w_pw2, m_c_b_pw2, m_xa_norm, m_xa_mem_norm, m_xa_wq, m_xa_wk, m_xa_wv, m_xa_wo, m_f_norm, m_f_w_up, m_f_dw_w, m_f_dw_b, m_f_w_down, m_final_norm, v_ab_norm, v_ab_w_in, v_a_conv_w, v_a_conv_b, v_a_gate_x_w, v_a_gate_x_b, v_a_gate_a_w, v_a_gate_a_b, v_a_lambda, v_b_group_w, v_b_group_b, v_b_scale, v_ab_w_out, v_c_norm, v_c_w_pw1, v_c_b_pw1, v_c_dw_w, v_c_dw_b, v_c_ln_g, v_c_ln_b, v_c_w_pw2, v_c_b_pw2, v_xa_norm, v_xa_mem_norm, v_xa_wq, v_xa_wk, v_xa_wv, v_xa_wo, v_f_norm, v_f_w_up, v_f_dw_w, v_f_dw_b, v_f_w_down, v_final_norm):
    args = dict(locals())
    P = {n: args[n] for n in _NAMES}
    M = {n: args["m_" + n] for n in _NAMES}
    V = {n: args["v_" + n] for n in _NAMES}

    me = 4 * lax.axis_index("x") + 2 * lax.axis_index("y") + lax.axis_index("c")

    in_flight = {}

    def launch(groups, tok):
        shards, n_of = [], {}
        for grp in groups:
            for name, l in _GROUPS[grp]:
                w = P[name][l] if tok is None else P[name][l] + tok
                shards.append(w.astype(BF16))
            if grp == ("ab", 0):
                shards.append(_pack([P[n] for n in _SMALL_SHARDED], _SS_ROWS + 4))
            n_of[grp] = len(shards)
        res = _send_start(shards, False, "gather_start_" + "_".join(g[0] + str(g[1]) for g in groups))
        lo = 0
        for grp in groups:
            in_flight[grp] = [r[lo:n_of[grp]] for r in res[:4]]
            lo = n_of[grp]
        return res[4][:1, :1]

    follow = {("ab", 0): [("ab", 1), ("xa", 0), ("f", 0)], ("xa", 0): [("c", 0)], ("f", 0): [("xa", 1)],
              ("c", 0): [("f", 1)]}

    def fetch(grp, after):
        send_s, recv_s, srcs, lands = in_flight.pop(grp)
        srcs, lands, tok = _send_wait(send_s, recv_s, srcs, lands, after, False, f"gather_wait_{grp[0]}{grp[1]}")
        tok = launch(follow[grp], tok[:1, :1]) if grp in follow else None
        full = [_with_own(land, src, me, False) for land, src in zip(lands, srcs)]
        out = {}
        for (name, l), g in zip(_GROUPS[grp], full):
            w = _weight_layout(name, g)
            if _BIG[name][0] == 1:
                out[name] = w
            else:
                out[name] = {l: w}
        if grp == ("ab", 0):
            out.update(_small_layouts(full[-1]))
        return out, tok

    zero = launch([("ab", 0)], None)

    pending = []

    def send(grp, G):
        members = _SEND_GROUPS[grp]
        res = _send_start([_grad_blocks(name, l, G) for name, l in members], True, f"send_{grp[0]}{grp[1]}")
        pending.append((members, res))
        return res[4][:1, :1]

    W = {n: P[n] for n in _REPL}
    W["ab_norm"] = P["ab_norm"] + zero
    W["final_norm"] = P["final_norm"].reshape(1, D)
    W["a_gate_x_w"], W["a_gate_a_w"], W["b_group_w"] = P["a_gate_x_w"][0], P["a_gate_a_w"][0], P["b_group_w"][0]
    loss, grad_x, G = _local_step(x[0], mem[0], loss_target[0], W, fetch, send)
    loss = lax.psum(loss[0, 0], ("x", "y", "c"))

    Gs = dict(G)
    Gs["c_b_pw1"] = _unpair_blocks(G["c_b_pw1"], _CW_C)
    Gs["f_dw_w"] = jnp.stack([G["f_dw_w0"], G["f_dw_w1"]])
    Gs["a_conv_w"], Gs["c_dw_w"] = G["a_conv_w"][None], G["c_dw_w"][None]
    for n in ("xa_norm", "xa_mem_norm", "f_norm", "f_dw_b"):
        Gs[n] = jnp.concatenate([G[f"{n}0"], G[f"{n}1"]], axis=0)
    for n in ("a_gate_x_w", "a_gate_a_w", "b_group_w"):
        Gs[n] = G[n][None]
    repl_flat = jnp.concatenate([Gs[n].reshape(-1) for n in _REPL])
    repl_rows = jnp.pad(repl_flat, (0, N_DEV * _REPL_ROWS * LANE - _N_REPL)).reshape(N_DEV, _REPL_ROWS, LANE)
    ss_rows = jnp.concatenate([_to_dest_major(Gs[n], s) for n, s in _SMALL_SHARDED.items()], axis=1)
    ss_rows = ss_rows.reshape(N_DEV, _SS_ROWS, LANE)
    small_pack = jnp.concatenate(
        [repl_rows, ss_rows, jnp.zeros((N_DEV, _SMALL_ROWS - _REPL_ROWS - _SS_ROWS, LANE), F32)], axis=1)
    last = _send_start([small_pack], True, "send_small")
    pending.append(((("small", 0),), last))

    members = [m for mem_, _ in pending for m in mem_]
    cat = [[a for _, res in pending for a in res[i]] for i in range(4)]
    srcs, lands, _ = _send_wait(cat[0], cat[1], cat[2], cat[3], grad_x, True, "send_wait")
    landed = {m: _with_own(land, src, me, True) for m, land, src in zip(members, lands, srcs)}

    out_g, out_d, out_m, out_v = {}, {}, {}, {}
    for name, (layers, r, c) in _BIG.items():
        shape = P[name].shape
        w2, m2, v2 = [t[name].reshape(layers * r, c) for t in (P, M, V)]
        res = None
        for l in range(layers):
            res = _sum_adamw(landed[(name, l)], w2, m2, v2, f"adamw_{name}{l}", layer=l, prev=res)
        out_g[name], out_d[name], out_m[name], out_v[name] = [t.reshape(shape) for t in res]

    small_sum = _sum8(landed[("small", 0)], "sum_small")
    (repl_all,) = _all_gather([small_sum[:_REPL_ROWS]], "gather_small_grads")
    g_repl = _unpack(repl_all, _REPL)
    g_ss = _unpack(small_sum[_REPL_ROWS:_REPL_ROWS + _SS_ROWS], _SMALL_SHARDED)
    table = dict(_REPL)
    table.update(_SMALL_SHARDED)
    rows = -(-(_N_REPL + _N_SS) // (256 * LANE)) * 256
    g_small = dict(g_repl)
    g_small.update(g_ss)
    packs = [_pack([src[n] for n in table], rows) for src in (g_small, P, M, V)]
    res = _adamw(*packs, "adamw_small")
    for out, buf in zip((out_d, out_m, out_v), res):
        out.update(_unpack(buf, table))
    out_g.update(g_small)

    return (loss, grad_x[None], *[out_g[n] for n in _NAMES], *[out_d[n] for n in _NAMES],
            *[out_m[n] for n in _NAMES], *[out_v[n] for n in _NAMES])


_NAMES = ("ab_norm", "ab_w_in", "a_conv_w", "a_conv_b", "a_gate_x_w", "a_gate_x_b", "a_gate_a_w", "a_gate_a_b",
          "a_lambda", "b_group_w", "b_group_b", "b_scale", "ab_w_out", "c_norm", "c_w_pw1", "c_b_pw1", "c_dw_w",
          "c_dw_b", "c_ln_g", "c_ln_b", "c_w_pw2", "c_b_pw2", "xa_norm", "xa_mem_norm", "xa_wq", "xa_wk", "xa_wv",
          "xa_wo", "f_norm", "f_w_up", "f_dw_w", "f_dw_b", "f_w_down", "final_norm")
```

```python
import functools

import jax
import jax.numpy as jnp
from jax import lax
from jax.experimental import pallas as pl
from jax.experimental.pallas import tpu as pltpu

F32, BF16 = jnp.float32, jnp.bfloat16
SDS = jax.ShapeDtypeStruct
MESH = pl.DeviceIdType.MESH

N_DEV = 8
D = 1024
N_MEM = 256
XA_HEADS, XA_HD = 4, 256
HD_A = 128
CONV_A, CONV_C, CONV_F = 4, 31, 3
C_RG = 8.0
POOL_WINDOWS = (2, 4, 8, 16)
D_FF = 3 * D
EPS = 1e-6
ADAM_LR, ADAM_B1, ADAM_B2, ADAM_EPS, ADAM_WD, ADAM_STEP = 0.001, 0.9, 0.999, 1e-08, 0.01, 10

LANE = 128
SUB = 8
VMEM_LIMIT = 56 * 1024 * 1024
R_SEQ = 256
TM_ROW = 512


def _cp(n_axes):
    return pltpu.CompilerParams(dimension_semantics=("arbitrary",) * n_axes, vmem_limit_bytes=VMEM_LIMIT)


def _tile(n, pref):
    if n <= pref:
        return n
    best = None
    for t in range(LANE, pref + 1, LANE):
        if n % t == 0:
            best = t
    assert best is not None, (n, pref)
    return best


def _perm2(n):
    return (n % 2) * 4 + n // 2


_NN = (((1,), (0,)), ((), ()))
_NT = (((1,), (1,)), ((), ()))
_TN = (((0,), (0,)), ((), ()))


def _mm_call(name, grid, ab, ab_specs, dims, acc_shape, extras, outs, finish):
    nk = grid[2]
    n_ab, n_ex, n_out = len(ab), len(extras), len(outs)

    def product(refs):
        r = lax.dot_general(refs[0][...], refs[1][...], dims, preferred_element_type=F32)
        for i in range(1, n_ab):
            r = r + lax.dot_general(refs[2 * i][...], refs[2 * i + 1][...], dims, preferred_element_type=F32)
        return r

    def body_one(*refs):
        rest = refs[2 * n_ab:]
        finish(product(refs), rest[:n_ex], rest[n_ex:n_ex + n_out], pl.program_id(0) == 0)

    def body_acc(*refs):
        rest = refs[2 * n_ab:]
        acc = rest[n_ex + n_out]
        k = pl.program_id(2)
        first_rows = pl.program_id(0) == 0

        @pl.when(k == 0)
        def _():
            acc[...] = jnp.zeros_like(acc)

        acc[...] += product(refs)

        @pl.when(k == nk - 1)
        def _():
            finish(acc[...], rest[:n_ex], rest[n_ex:n_ex + n_out], first_rows)

    res = pl.pallas_call(
        body_one if nk == 1 else body_acc, out_shape=[o for o, _ in outs], grid=grid,
        in_specs=list(ab_specs) + [s for _, s in extras], out_specs=[s for _, s in outs],
        scratch_shapes=[] if nk == 1 else [pltpu.VMEM(acc_shape, F32)], name=name, compiler_params=_cp(3),
    )(*[t for pair in ab for t in pair], *[e for e, _ in extras])
    return res[0] if n_out == 1 else res


def _finish_sum(r, ex_refs, o_refs, first_rows):
    del first_rows
    for e in ex_refs:
        r = r + e[...]
    o_refs[0][...] = r.astype(o_refs[0].dtype)


def _finish_sum_norm(r, ex_refs, o_refs, first_rows):
    del first_rows
    for e in ex_refs[:-1]:
        r = r + e[...]
    o_refs[0][...] = r
    o_refs[1][...] = ((r * lax.rsqrt(jnp.mean(r * r, axis=-1, keepdims=True) + EPS)) * ex_refs[-1][...]).astype(BF16)


def _finish_rms_bwd(r, ex_refs, o_refs, first_rows):
    x_ref, g_ref, dres_ref = ex_refs
    dx_ref, dxb_ref, dg_ref = o_refs

    @pl.when(first_rows)
    def _():
        dg_ref[...] = jnp.zeros_like(dg_ref)

    xf = x_ref[...]
    rs = lax.rsqrt(jnp.mean(xf * xf, axis=-1, keepdims=True) + EPS)
    y = xf * rs
    dg_ref[...] += jnp.sum(r * y, axis=0, keepdims=True)
    dy = r * g_ref[...]
    dx = rs * (dy - y * jnp.mean(dy * y, axis=-1, keepdims=True)) + dres_ref[...]
    dx_ref[...] = dx
    dxb_ref[...] = dx.astype(BF16)


def _rms_bwd_io(M, tm, x, g, dres):
    rows = pl.BlockSpec((tm, D), lambda m, n, k: (m, 0))
    vec = pl.BlockSpec((1, D), lambda m, n, k: (0, 0))
    return ([(x, rows), (g, vec), (dres, rows)],
            [(SDS((M, D), F32), rows), (SDS((M, D), BF16), rows), (SDS((1, D), F32), vec)])


_K_WHOLE = 3072


def _mm_nn(a, b, *, out_dtype, name, bias=None, add=None, norm=None):
    M, K = a.shape
    tk = K if K <= _K_WHOLE else _tile(K, 1024)
    tm = _tile(M, 1024 if K <= 1024 and norm is None else 512)
    if b.ndim == 3:
        nb, _, bw = b.shape
        N, tn, nn = nb * bw, bw, nb
        b_spec = pl.BlockSpec((None, tk, bw), lambda m, n, k: (_perm2(n), k, 0))
    else:
        N = b.shape[1]
        tn = _tile(N, 1024)
        nn = N // tn
        b_spec = pl.BlockSpec((tk, tn), lambda m, n, k: (k, n))
    tile = pl.BlockSpec((tm, tn), lambda m, n, k: (m, n))
    vec = pl.BlockSpec((1, tn), lambda m, n, k: (0, n))
    extras = ([] if bias is None else [(bias, vec)]) + ([] if add is None else [(add, tile)])
    outs, finish = [(SDS((M, N), out_dtype), tile)], _finish_sum
    if norm is not None:
        assert tn == N == D and out_dtype == F32
        extras.append((norm, vec))
        outs, finish = outs + [(SDS((M, N), BF16), tile)], _finish_sum_norm
    return _mm_call(name, (M // tm, nn, K // tk), [(a, b)], [pl.BlockSpec((tm, tk), lambda m, n, k: (m, k)), b_spec],
                    _NN, (tm, tn), extras, outs, finish)


def _mm_nt(a, b, *, out_dtype, name, add=None, rms=None):
    M, N = a.shape
    if b.ndim == 3:
        nb, Ko, bw = b.shape
        tm = _tile(M, 1024 if rms is None else 512)
        tn, tk, nk = _tile(Ko, 1024), bw, nb
        b_spec = pl.BlockSpec((None, tn, bw), lambda m, n, k: (_perm2(k), n, 0))
    else:
        Ko = b.shape[0]
        tk = N if N <= _K_WHOLE else _tile(N, 1024)
        tm = _tile(M, 1024 if N <= 1024 and rms is None else 512)
        tn = _tile(Ko, 1024)
        nk = N // tk
        b_spec = pl.BlockSpec((tn, tk), lambda m, n, k: (n, k))
    tile = pl.BlockSpec((tm, tn), lambda m, n, k: (m, n))
    extras = [] if add is None else [(add, tile)]
    outs, finish = [(SDS((M, Ko), out_dtype), tile)], _finish_sum
    if rms is not None:
        assert tn == Ko == D and add is None
        (extras, outs), finish = _rms_bwd_io(M, tm, *rms), _finish_rms_bwd
    return _mm_call(name, (M // tm, Ko // tn, nk), [(a, b)], [pl.BlockSpec((tm, tk), lambda m, n, k: (m, k)), b_spec],
                    _NT, (tm, tn), extras, outs, finish)


def _mm_nt_cols(parts, b, *, name, rms):
    M = parts[0].shape[0]
    tm = _tile(M, 512)
    specs, off = [], 0
    for p in parts:
        w = p.shape[1]
        assert off % w == 0
        specs.append(pl.BlockSpec((tm, w), lambda m, n, k: (m, 0)))
        specs.append(pl.BlockSpec((D, w), functools.partial(lambda m, n, k, o: (0, o), o=off // w)))
        off += w
    extras, outs = _rms_bwd_io(M, tm, *rms)
    return _mm_call(name, (M // tm, 1, 1), [(p, b) for p in parts], specs, _NT, (tm, D), extras, outs, _finish_rms_bwd)


def _mm_tn(a, b, *, out_dtype, name, blocks=None):
    S, Ka = a.shape
    Nb = b.shape[1]
    tm, tk = _tile(Ka, 1024), _tile(S, 2048)
    if blocks is not None:
        bw = blocks
        tn, nn = bw, Nb // bw
        out = (SDS((nn, Ka, bw), out_dtype), pl.BlockSpec((None, tm, bw), lambda m, n, k: (_perm2(n), m, 0)))
    else:
        tn = _tile(Nb, 1024)
        nn = Nb // tn
        out = (SDS((Ka, Nb), out_dtype), pl.BlockSpec((tm, tn), lambda m, n, k: (m, n)))
    return _mm_call(name, (Ka // tm, nn, S // tk), [(a, b)],
                    [pl.BlockSpec((tk, tm), lambda m, n, k: (k, m)), pl.BlockSpec((tk, tn), lambda m, n, k: (k, n))],
                    _TN, (tm, tn), [], [out], _finish_sum)


def _row(tm, c):
    return pl.BlockSpec((tm, c), lambda i: (i, 0))


def _full(shape):
    nd = len(shape)
    return pl.BlockSpec(shape, lambda i: (0,) * nd)


def _rms_fwd(x, g, name):
    S = x.shape[0]
    tm = min(S, TM_ROW)

    def body(x_ref, g_ref, o_ref):
        xf = x_ref[...]
        r = lax.rsqrt(jnp.mean(xf * xf, axis=-1, keepdims=True) + EPS)
        o_ref[...] = ((xf * r) * g_ref[...]).astype(BF16)

    return pl.pallas_call(body, out_shape=SDS((S, D), BF16), grid=(S // tm,), in_specs=[_row(tm, D), _full((1, D))],
                          out_specs=_row(tm, D), name=name, compiler_params=_cp(1))(x, g)


def _rms_bwd(x, g, dn, dres, name):
    S = x.shape[0]
    tm = min(S, TM_ROW)
    want_dx = dres is not None

    def body(x_ref, g_ref, dn_ref, *rest):
        i = pl.program_id(0)
        dg_ref = rest[-1]

        @pl.when(i == 0)
        def _():
            dg_ref[...] = jnp.zeros_like(dg_ref)

        xf = x_ref[...]
        r = lax.rsqrt(jnp.mean(xf * xf, axis=-1, keepdims=True) + EPS)
        y = xf * r
        dn_v = dn_ref[...]
        dg_ref[...] += jnp.sum(dn_v * y, axis=0, keepdims=True)
        if want_dx:
            dres_ref, dx_ref, dxb_ref = rest[0], rest[1], rest[2]
            dy = dn_v * g_ref[...]
            dx = r * (dy - y * jnp.mean(dy * y, axis=-1, keepdims=True)) + dres_ref[...]
            dx_ref[...] = dx
            dxb_ref[...] = dx.astype(BF16)

    ins = [x, g, dn] + ([dres] if want_dx else [])
    in_specs = [_row(tm, D), _full((1, D)), _row(tm, D)] + ([_row(tm, D)] if want_dx else [])
    outs = ([SDS((S, D), F32), SDS((S, D), BF16)] if want_dx else []) + [SDS((1, D), F32)]
    out_specs = ([_row(tm, D), _row(tm, D)] if want_dx else []) + [_full((1, D))]
    return pl.pallas_call(body, out_shape=outs, grid=(S // tm,), in_specs=in_specs, out_specs=out_specs, name=name,
                          compiler_params=_cp(1))(*ins)


def _loss_head(x, g, tgt):
    S = x.shape[0]
    tm = min(S, TM_ROW)

    def body(x_ref, g_ref, t_ref, loss_ref, dx_ref, dxb_ref, dg_ref):
        i = pl.program_id(0)

        @pl.when(i == 0)
        def _():
            loss_ref[...] = jnp.zeros_like(loss_ref)
            dg_ref[...] = jnp.zeros_like(dg_ref)

        xf = x_ref[...]
        r = lax.rsqrt(jnp.mean(xf * xf, axis=-1, keepdims=True) + EPS)
        y = xf * r
        gv = g_ref[...]
        err = y * gv - t_ref[...]
        per_row = jnp.mean(err * err, axis=-1, keepdims=True)
        loss_ref[...] += 0.5 * jnp.sum(per_row, axis=0, keepdims=True)
        dn_v = err * (1.0 / D)
        dg_ref[...] += jnp.sum(dn_v * y, axis=0, keepdims=True)
        dy = dn_v * gv
        dx = r * (dy - y * jnp.mean(dy * y, axis=-1, keepdims=True))
        dx_ref[...] = dx
        dxb_ref[...] = dx.astype(BF16)

    return pl.pallas_call(
        body, out_shape=[SDS((1, 1), F32), SDS((S, D), F32), SDS((S, D), BF16), SDS((1, D), F32)], grid=(S // tm,),
        in_specs=[_row(tm, D), _full((1, D)), _row(tm, D)],
        out_specs=[_full((1, 1)), _row(tm, D), _row(tm, D), _full((1, D))], name="loss_head", compiler_params=_cp(1),
    )(x, g, tgt)


def _softmax_rows(s):
    m = jnp.max(s, axis=-1, keepdims=True)
    e = jnp.exp(s - m)
    return e / jnp.sum(e, axis=-1, keepdims=True)


def _attn_fwd(q, k, v, name):
    S = q.shape[0]
    tm = min(S, TM_ROW)
    scale = XA_HD ** -0.5

    def body(q_ref, k_ref, v_ref, o_ref):
        for h in range(XA_HEADS):
            sl = slice(h * XA_HD, (h + 1) * XA_HD)
            s = lax.dot_general(q_ref[:, sl], k_ref[:, sl], _NT, preferred_element_type=F32) * scale
            p = _softmax_rows(s)
            o_ref[:, sl] = lax.dot_general(p.astype(BF16), v_ref[:, sl], _NN, preferred_element_type=F32).astype(BF16)

    return pl.pallas_call(body, out_shape=SDS((S, D), BF16), grid=(S // tm,),
                          in_specs=[_row(tm, D), _full((N_MEM, D)), _full((N_MEM, D))], out_specs=_row(tm, D),
                          name=name, compiler_params=_cp(1))(q, k, v)


def _attn_bwd(q, k, v, do, name):
    S = q.shape[0]
    tm = min(S, TM_ROW)
    scale = XA_HD ** -0.5

    def body(q_ref, k_ref, v_ref, do_ref, dq_ref, dk_ref, dv_ref):
        i = pl.program_id(0)

        @pl.when(i == 0)
        def _():
            dk_ref[...] = jnp.zeros_like(dk_ref)
            dv_ref[...] = jnp.zeros_like(dv_ref)

        for h in range(XA_HEADS):
            sl = slice(h * XA_HD, (h + 1) * XA_HD)
            qh, kh, vh, doh = q_ref[:, sl], k_ref[:, sl], v_ref[:, sl], do_ref[:, sl]
            s = lax.dot_general(qh, kh, _NT, preferred_element_type=F32) * scale
            p = _softmax_rows(s)
            pb = p.astype(BF16)
            dv_ref[:, sl] += lax.dot_general(pb, doh, _TN, preferred_element_type=F32)
            dp = lax.dot_general(doh, vh, _NT, preferred_element_type=F32)
            ds = (p * (dp - jnp.sum(dp * p, axis=-1, keepdims=True)) * scale).astype(BF16)
            dq_ref[:, sl] = lax.dot_general(ds, kh, _NN, preferred_element_type=F32).astype(BF16)
            dk_ref[:, sl] += lax.dot_general(ds, qh, _TN, preferred_element_type=F32)

    return pl.pallas_call(
        body, out_shape=[SDS((S, D), BF16), SDS((N_MEM, D), F32), SDS((N_MEM, D), F32)], grid=(S // tm,),
        in_specs=[_row(tm, D), _full((N_MEM, D)), _full((N_MEM, D)), _row(tm, D)],
        out_specs=[_row(tm, D), _full((N_MEM, D)), _full((N_MEM, D))], name=name, compiler_params=_cp(1),
    )(q, k, v, do)


def _sigmoid(x):
    return 1.0 / (1.0 + jnp.exp(-x))


def _ln_silu_fwd(cv, g, b):
    S = cv.shape[0]
    tm = min(S, TM_ROW)

    def body(x_ref, g_ref, b_ref, o_ref):
        xf = x_ref[...]
        mu = jnp.mean(xf, axis=-1, keepdims=True)
        xc = xf - mu
        rstd = lax.rsqrt(jnp.mean(xc * xc, axis=-1, keepdims=True) + EPS)
        ln = (xc * rstd) * g_ref[...] + b_ref[...]
        o_ref[...] = (ln * _sigmoid(ln)).astype(BF16)

    return pl.pallas_call(body, out_shape=SDS((S, D), BF16), grid=(S // tm,),
                          in_specs=[_row(tm, D), _full((1, D)), _full((1, D))], out_specs=_row(tm, D),
                          name="ln_silu_fwd", compiler_params=_cp(1))(cv, g, b)


def _ln_silu_bwd(ds, cv, g, b, dx):
    S = cv.shape[0]
    tm = min(S, TM_ROW)

    def body(ds_ref, x_ref, g_ref, b_ref, dx_ref, dcv_ref, dg_ref, db_ref, db2_ref):
        i = pl.program_id(0)

        @pl.when(i == 0)
        def _():
            dg_ref[...] = jnp.zeros_like(dg_ref)
            db_ref[...] = jnp.zeros_like(db_ref)
            db2_ref[...] = jnp.zeros_like(db2_ref)

        xf = x_ref[...]
        mu = jnp.mean(xf, axis=-1, keepdims=True)
        xc = xf - mu
        rstd = lax.rsqrt(jnp.mean(xc * xc, axis=-1, keepdims=True) + EPS)
        xhat = xc * rstd
        gv = g_ref[...]
        ln = xhat * gv + b_ref[...]
        sg = _sigmoid(ln)
        dln = ds_ref[...].astype(F32) * (sg + ln * sg * (1.0 - sg))
        dg_ref[...] += jnp.sum(dln * xhat, axis=0, keepdims=True)
        db_ref[...] += jnp.sum(dln, axis=0, keepdims=True)
        db2_ref[...] += jnp.sum(dx_ref[...], axis=0, keepdims=True)
        dxh = dln * gv
        dcv_ref[...] = rstd * (dxh - jnp.mean(dxh, axis=-1, keepdims=True)
                               - xhat * jnp.mean(dxh * xhat, axis=-1, keepdims=True))

    return pl.pallas_call(
        body, out_shape=[SDS((S, D), F32), SDS((1, D), F32), SDS((1, D), F32), SDS((1, D), F32)], grid=(S // tm,),
        in_specs=[_row(tm, D), _row(tm, D), _full((1, D)), _full((1, D)), _row(tm, D)],
        out_specs=[_row(tm, D), _full((1, D)), _full((1, D)), _full((1, D))], name="ln_silu_bwd",
        compiler_params=_cp(1),
    )(ds, cv, g, b, dx)


_GELU_C, _GELU_K = 0.7978845608028654, 0.044715


def _gelu(x, with_grad=False):
    x2 = x * x
    t = jnp.tanh(_GELU_C * (x + _GELU_K * x * x2))
    gel = 0.5 * x * (1.0 + t)
    if not with_grad:
        return gel
    return gel, 0.5 * (1.0 + t) + 0.5 * x * (1.0 - t * t) * (_GELU_C * (1.0 + 3.0 * _GELU_K * x2))


def _expm1(x):
    poly = x * (1.0 + x * (0.5 + x * (1.0 / 6.0 + x * (1.0 / 24.0 + x * (1.0 / 120.0)))))
    return jnp.where(jnp.abs(x) < 0.05, poly, jnp.exp(x) - 1.0)


def _softplus(x):
    return jnp.maximum(x, 0.0) + jnp.log1p(jnp.exp(-jnp.abs(x)))


_SCAN_UNROLL = 4
_RB = 32
_HB = 16


def _sub_blocks(n_rows, n_lanes, fn):
    def step(idx, c):
        r0 = pl.multiple_of(idx * _RB, _RB)
        for lt in range(n_lanes // LANE):
            fn(r0, lt)
        return c

    lax.fori_loop(0, n_rows // _RB, step, 0)


def _lanes(lt):
    return pl.ds(lt * LANE, LANE)


def _psum8(x):
    parts = [x[i * SUB:(i + 1) * SUB] for i in range(x.shape[0] // SUB)]
    return functools.reduce(lambda p, q: p + q, parts)


def _scan_fwd(a_s, b_s, out_ref, carry_ref, n_groups):
    row = lax.broadcasted_iota(jnp.int32, (SUB, LANE), 0)
    U = _SCAN_UNROLL

    def step(gi, carry):
        base = gi * (SUB * U)
        parts = []
        for u in range(U):
            i = pl.multiple_of(base + u * SUB, SUB)
            a8, b8 = a_s[pl.ds(i, SUB), :], b_s[pl.ds(i, SUB), :]
            for s in (1, 2, 4):
                a_sh = jnp.where(row >= s, pltpu.roll(a8, s, 0), 1.0)
                b_sh = jnp.where(row >= s, pltpu.roll(b8, s, 0), 0.0)
                b8 = a8 * b_sh + b8
                a8 = a8 * a_sh
            parts.append((i, a8, b8))
        for i, a8, b8 in parts:
            h8 = a8 * carry + b8
            out_ref[pl.ds(i, SUB), :] = h8
            carry = jnp.broadcast_to(h8[SUB - 1:SUB, :], (SUB, LANE))
        return carry

    carry_ref[...] = lax.fori_loop(0, n_groups // U, step, carry_ref[...])


def _scan_bwd(a_s, b_s, out_ref, carry_ref, n_groups):
    row = lax.broadcasted_iota(jnp.int32, (SUB, LANE), 0)
    U = _SCAN_UNROLL

    def step(gi, carry):
        base = (n_groups // U - 1 - gi) * (SUB * U)
        parts = []
        for u in reversed(range(U)):
            i = pl.multiple_of(base + u * SUB, SUB)
            a8, b8 = a_s[pl.ds(i, SUB), :], b_s[pl.ds(i, SUB), :]
            for s in (1, 2, 4):
                a_sh = jnp.where(row < SUB - s, pltpu.roll(a8, SUB - s, 0), 1.0)
                b_sh = jnp.where(row < SUB - s, pltpu.roll(b8, SUB - s, 0), 0.0)
                b8 = a8 * b_sh + b8
                a8 = a8 * a_sh
            parts.append((i, a8, b8))
        for i, a8, b8 in parts:
            h8 = a8 * carry + b8
            out_ref[pl.ds(i, SUB), :] = h8
            carry = jnp.broadcast_to(h8[0:1, :], (SUB, LANE))
        return carry

    carry_ref[...] = lax.fori_loop(0, n_groups // U, step, carry_ref[...])


def _rglru_pre(xr, wgx_ref, bgx_ref, wga_ref, bga_ref, lam_ref):
    xrb = xr.astype(BF16)
    wgx, wga = wgx_ref[0].astype(BF16), wga_ref[0].astype(BF16)
    gx = _sigmoid(lax.dot_general(xrb, wgx, _NN, preferred_element_type=F32) + bgx_ref[...])
    ga = _sigmoid(lax.dot_general(xrb, wga, _NN, preferred_element_type=F32) + bga_ref[...])
    sp = _softplus(-lam_ref[...])
    log_a = -C_RG * ga * sp
    a = jnp.exp(log_a)
    mult = jnp.sqrt(-_expm1(2.0 * log_a))
    return gx, ga, sp, a, mult, xrb, wgx, wga


def _a_specs():
    vec = pl.BlockSpec((1, HD_A), lambda c, j: (0, c))
    mat = pl.BlockSpec((1, HD_A, HD_A), lambda c, j: (c, 0, 0))
    return [pl.BlockSpec((CONV_A, HD_A), lambda c, j: (0, c)), vec, mat, vec, mat, vec, vec]


def _a_fwd(zp, conv_w, conv_b, wgx, bgx, wga, bga, lam):
    S = zp.shape[0]
    R, nt = R_SEQ, D // HD_A
    H = SUB

    def body(zg_ref, zr_ref, cw_ref, cb_ref, wgx_ref, bgx_ref, wga_ref, bga_ref, lam_ref, ya_ref, h_ref,
             ext, a_s, b_s, hc):
        j = pl.program_id(1)

        @pl.when(j == 0)
        def _():
            ext[0:H, :] = jnp.zeros((H, HD_A), F32)
            hc[...] = jnp.zeros_like(hc)

        ext[H:H + R, :] = zr_ref[...].astype(F32)
        xr = cb_ref[...]
        for k in range(CONV_A):
            xr = xr + cw_ref[k:k + 1, :] * ext[pl.ds(H - (CONV_A - 1 - k), R), :]
        gx, _, _, a, mult, _, _, _ = _rglru_pre(xr, wgx_ref, bgx_ref, wga_ref, bga_ref, lam_ref)
        a_s[...] = a
        b_s[...] = mult * (gx * xr)
        _scan_fwd(a_s, b_s, h_ref, hc, R // SUB)
        ya_ref[...] = (_gelu(zg_ref[...].astype(F32)) * h_ref[...]).astype(BF16)
        ext[0:H, :] = ext[R:R + H, :]

    return pl.pallas_call(
        body, out_shape=[SDS((S, D + D // 2), BF16), SDS((S, D), F32)], grid=(nt, S // R),
        in_specs=[pl.BlockSpec((R, HD_A), lambda c, j: (j, c)), pl.BlockSpec((R, HD_A), lambda c, j: (j, nt + c))]
        + _a_specs(),
        out_specs=[pl.BlockSpec((R, HD_A), lambda c, j: (j, c)), pl.BlockSpec((R, HD_A), lambda c, j: (j, c))],
        scratch_shapes=[pltpu.VMEM((H + R, HD_A), F32), pltpu.VMEM((R, HD_A), F32), pltpu.VMEM((R, HD_A), F32),
                        pltpu.VMEM((SUB, HD_A), F32)],
        name="rglru_fwd", compiler_params=_cp(2),
    )(zp, zp, conv_w, conv_b, wgx, bgx, wga, bga, lam)


def _a_bwd(dyab, zp, h, conv_w, conv_b, wgx, bgx, wga, bga, lam):
    S = zp.shape[0]
    R, nt, nch = R_SEQ, D // HD_A, S // R_SEQ
    H = SUB

    def rows(c, j):
        return (nch - 1 - j, c)

    def rows_rec(c, j):
        return (nch - 1 - j, nt + c)

    def halo(c, j):
        return (jnp.maximum((nch - 1 - j) * (R // H) - 1, 0), c)

    def halo_z(c, j):
        return (jnp.maximum((nch - 1 - j) * (R // _HB) - 1, 0), nt + c)

    def body(dy_ref, zg_ref, zr_ref, zh_ref, h_ref, hh_ref, cw_ref, cb_ref, wgx_ref, bgx_ref, wga_ref, bga_ref,
             lam_ref, dzg_ref, dzr_ref, dcw_ref, dcb_ref, dwgx_ref, dbgx_ref, dwga_ref, dbga_ref, dlam_ref,
             ext_z, ext_h, ext_mu, ext_d, a_s, b_s, muc):
        j = pl.program_id(1)
        first_chunk = (nch - 1 - j) == 0

        @pl.when(j == 0)
        def _():
            ext_mu[R:R + H, :] = jnp.zeros((H, HD_A), F32)
            ext_d[R:R + H, :] = jnp.zeros((H, HD_A), F32)
            muc[...] = jnp.zeros_like(muc)
            for r in (dcw_ref, dcb_ref, dwgx_ref, dbgx_ref, dwga_ref, dbga_ref, dlam_ref):
                r[...] = jnp.zeros_like(r)

        zg = zg_ref[...].astype(F32)
        ext_z[0:H, :] = jnp.where(first_chunk, 0.0, zh_ref[_HB - H:_HB, :].astype(F32))
        ext_z[H:H + R, :] = zr_ref[...].astype(F32)
        ext_h[0:H, :] = jnp.where(first_chunk, 0.0, hh_ref[...])
        ext_h[H:H + R, :] = h_ref[...]
        xr = cb_ref[...]
        for k in range(CONV_A):
            xr = xr + cw_ref[k:k + 1, :] * ext_z[pl.ds(H - (CONV_A - 1 - k), R), :]
        gx, ga, sp, a, mult, xrb, wgxb, wgab = _rglru_pre(xr, wgx_ref, bgx_ref, wga_ref, bga_ref, lam_ref)
        gel, dgel = _gelu(zg, with_grad=True)
        dy = dy_ref[...].astype(F32)
        dh = dy * gel
        dzg_ref[...] = (dy * h_ref[...] * dgel).astype(BF16)
        a_s[...] = a
        b_s[...] = a * dh
        _scan_bwd(a_s, b_s, ext_mu, muc, R // SUB)
        lam_t = dh + ext_mu[pl.ds(1, R), :]
        ext_mu[R:R + H, :] = ext_mu[0:H, :]
        da = lam_t * ext_h[pl.ds(H - 1, R), :]
        gxr = gx * xr
        dlog_a = da * a - (lam_t * gxr) * (a * a) / mult
        dgx = lam_t * mult * xr
        dxr = lam_t * mult * gx
        lam_v = lam_ref[...]
        dlam_ref[...] += jnp.sum(dlog_a * ga, axis=0, keepdims=True) * (C_RG * _sigmoid(-lam_v))
        dpa = (dlog_a * (-C_RG * sp)) * ga * (1.0 - ga)
        dpx = dgx * gx * (1.0 - gx)
        dbga_ref[...] += jnp.sum(dpa, axis=0, keepdims=True)
        dbgx_ref[...] += jnp.sum(dpx, axis=0, keepdims=True)
        dpab, dpxb = dpa.astype(BF16), dpx.astype(BF16)
        dwga_ref[0] += lax.dot_general(xrb, dpab, _TN, preferred_element_type=F32)
        dwgx_ref[0] += lax.dot_general(xrb, dpxb, _TN, preferred_element_type=F32)
        dxr = (dxr + lax.dot_general(dpab, wgab, _NT, preferred_element_type=F32)
               + lax.dot_general(dpxb, wgxb, _NT, preferred_element_type=F32))
        dcb_ref[...] += jnp.sum(dxr, axis=0, keepdims=True)
        ext_d[0:R, :] = dxr
        dzr = jnp.zeros((R, HD_A), F32)
        for k in range(CONV_A):
            sh = CONV_A - 1 - k
            dcw_ref[k:k + 1, :] += jnp.sum(dxr * ext_z[pl.ds(H - sh, R), :], axis=0, keepdims=True)
            dzr = dzr + cw_ref[k:k + 1, :] * ext_d[pl.ds(sh, R), :]
        dzr_ref[...] = dzr.astype(BF16)
        ext_d[R:R + H, :] = ext_d[0:H, :]

    vec_o = pl.BlockSpec((1, HD_A), lambda c, j: (0, c))
    mat_o = pl.BlockSpec((1, HD_A, HD_A), lambda c, j: (c, 0, 0))
    return pl.pallas_call(
        body,
        out_shape=[SDS((S, D), BF16), SDS((S, D), BF16), SDS((CONV_A, D), F32), SDS((1, D), F32),
                   SDS((nt, HD_A, HD_A), F32), SDS((1, D), F32), SDS((nt, HD_A, HD_A), F32), SDS((1, D), F32),
                   SDS((1, D), F32)],
        grid=(nt, nch),
        in_specs=[pl.BlockSpec((R, HD_A), rows), pl.BlockSpec((R, HD_A), rows), pl.BlockSpec((R, HD_A), rows_rec),
                  pl.BlockSpec((_HB, HD_A), halo_z), pl.BlockSpec((R, HD_A), rows),
                  pl.BlockSpec((H, HD_A), halo)] + _a_specs(),
        out_specs=[pl.BlockSpec((R, HD_A), rows), pl.BlockSpec((R, HD_A), rows),
                   pl.BlockSpec((CONV_A, HD_A), lambda c, j: (0, c)), vec_o, mat_o, vec_o, mat_o, vec_o, vec_o],
        scratch_shapes=[pltpu.VMEM((H + R, HD_A), F32), pltpu.VMEM((H + R, HD_A), F32), pltpu.VMEM((R + H, HD_A), F32),
                        pltpu.VMEM((R + H, HD_A), F32), pltpu.VMEM((R, HD_A), F32), pltpu.VMEM((R, HD_A), F32),
                        pltpu.VMEM((SUB, HD_A), F32)],
        name="rglru_bwd", compiler_params=_cp(2),
    )(dyab, zp, zp, zp, h, h, conv_w, conv_b, wgx, bgx, wga, bga, lam)


_POOL_H = 16
_POOL_T0 = 2 * D // HD_A
_POOL_Y0 = D // HD_A


def _window_sum(lv, n, lo, rows, g, ahead):
    base = 0 if ahead else SUB
    cur, win = lv[0], None
    for i, s in enumerate((1, 2, 4, 8)):
        val = cur[pl.ds(base, n), :] + cur[pl.ds(base + (s if ahead else -s), n), :]
        sel = val[lo:lo + rows]
        win = sel if win is None else jnp.where(g >= i, sel, win)
        if i < 3:
            lv[i + 1][pl.ds(base, n), :] = val
            cur = lv[i + 1]
    return win


def _pool_width(g):
    return jnp.where(g == 0, 2.0, jnp.where(g == 1, 4.0, jnp.where(g == 2, 8.0, 16.0)))


def _b_fwd(zp, yab, wg, bg, sc):
    S = zp.shape[0]
    R, H = R_SEQ, _POOL_H

    def body(z_ref, wg_ref, bg_ref, sc_ref, yab_in, yb_ref, *lv):
        del yab_in
        g, j = pl.program_id(0), pl.program_id(1)

        @pl.when(j == 0)
        def _():
            for r in lv:
                r[0:SUB, :] = jnp.zeros((SUB, HD_A), F32)
            lv[0][SUB:SUB + H, :] = jnp.zeros((H, HD_A), F32)

        u = z_ref[...].astype(F32)
        lv[0][SUB + H:SUB + H + R, :] = u
        t1 = (j * R + 1 + lax.broadcasted_iota(jnp.int32, (R, HD_A), 0)).astype(F32)
        p = _window_sum(lv, H + R, H, R, g, False) / jnp.minimum(t1, _pool_width(g)) - u
        lin = lax.dot_general(p.astype(BF16), wg_ref[0].astype(BF16), _NN, preferred_element_type=F32) + bg_ref[...]
        yb_ref[...] = (lin * sc_ref[...]).astype(BF16)
        lv[0][SUB:SUB + H, :] = lv[0][SUB + R:SUB + R + H, :]

    vec = pl.BlockSpec((1, HD_A), lambda g, j: (0, g))
    return pl.pallas_call(
        body, out_shape=SDS(yab.shape, yab.dtype), grid=(len(POOL_WINDOWS), S // R),
        in_specs=[pl.BlockSpec((R, HD_A), lambda g, j: (j, _POOL_T0 + g)),
                  pl.BlockSpec((1, HD_A, HD_A), lambda g, j: (g, 0, 0)), vec, vec, pl.BlockSpec(memory_space=pl.ANY)],
        out_specs=pl.BlockSpec((R, HD_A), lambda g, j: (j, _POOL_Y0 + g)),
        scratch_shapes=[pltpu.VMEM((SUB + H + R, HD_A), F32)] * 4, input_output_aliases={4: 0},
        name="pool_fwd", compiler_params=_cp(2),
    )(zp, wg, bg, sc, yab)


def _b_bwd(dyab, zp, wg, bg, sc):
    S = zp.shape[0]
    R, H, nch, ng = R_SEQ, _POOL_H, S // R_SEQ, len(POOL_WINDOWS)

    def body(dy_ref, z_ref, zh_ref, wg_ref, bg_ref, sc_ref, dz_ref, dwg_ref, dbg_ref, dsc_ref, *scratch):
        lu, lq = scratch[:4], scratch[4:]
        g, j = pl.program_id(0), pl.program_id(1)
        jj = nch - 1 - j

        @pl.when(j == 0)
        def _():
            for r in lu:
                r[0:SUB, :] = jnp.zeros((SUB, HD_A), F32)
            for r in lq:
                r[R + H:R + H + SUB, :] = jnp.zeros((SUB, HD_A), F32)
            lq[0][R:R + H, :] = jnp.zeros((H, HD_A), F32)
            for r in (dwg_ref, dbg_ref, dsc_ref):
                r[...] = jnp.zeros_like(r)

        u = z_ref[...].astype(F32)
        lu[0][SUB:SUB + H, :] = jnp.where(jj == 0, 0.0, zh_ref[...].astype(F32))
        lu[0][SUB + H:SUB + H + R, :] = u
        t1 = (jj * R + 1 + lax.broadcasted_iota(jnp.int32, (R, HD_A), 0)).astype(F32)
        cnt = jnp.minimum(t1, _pool_width(g))
        pb = (_window_sum(lu, H + R, H, R, g, False) / cnt - u).astype(BF16)
        wgb = wg_ref[0].astype(BF16)
        lin = lax.dot_general(pb, wgb, _NN, preferred_element_type=F32) + bg_ref[...]
        dy = dy_ref[...].astype(F32)
        dsc_ref[...] += jnp.sum(dy * lin, axis=0, keepdims=True)
        dlin = dy * sc_ref[...]
        dbg_ref[...] += jnp.sum(dlin, axis=0, keepdims=True)
        dlb = dlin.astype(BF16)
        dwg_ref[0] += lax.dot_general(pb, dlb, _TN, preferred_element_type=F32)
        dp = lax.dot_general(dlb, wgb, _NT, preferred_element_type=F32)
        lq[0][0:R, :] = dp / cnt
        dz_ref[...] = (_window_sum(lq, R + H, 0, R, g, True) - dp).astype(BF16)
        lq[0][R:R + H, :] = lq[0][0:H, :]

    vec = pl.BlockSpec((1, HD_A), lambda g, j: (0, g))
    mat = pl.BlockSpec((1, HD_A, HD_A), lambda g, j: (g, 0, 0))
    return pl.pallas_call(
        body, out_shape=[SDS((S, D // 2), BF16), SDS((ng, HD_A, HD_A), F32), SDS((1, D // 2), F32),
                         SDS((1, D // 2), F32)],
        grid=(ng, nch),
        in_specs=[pl.BlockSpec((R, HD_A), lambda g, j: (nch - 1 - j, _POOL_Y0 + g)),
                  pl.BlockSpec((R, HD_A), lambda g, j: (nch - 1 - j, _POOL_T0 + g)),
                  pl.BlockSpec((H, HD_A), lambda g, j: (jnp.maximum((nch - 1 - j) * (R // H) - 1, 0), _POOL_T0 + g)),
                  mat, vec, vec],
        out_specs=[pl.BlockSpec((R, HD_A), lambda g, j: (nch - 1 - j, g)), mat, vec, vec],
        scratch_shapes=[pltpu.VMEM((SUB + H + R, HD_A), F32)] * 8,
        name="pool_bwd", compiler_params=_cp(2),
    )(dyab, zp, zp, wg, bg, sc)


_CW_F = 768


def _f_fwd(hp, w, b, name):
    S = hp.shape[0]
    R, H, cw = R_SEQ, SUB, _CW_F
    nlt = cw // LANE

    def body(h_ref, w_ref, b_ref, o_ref, ext):
        j = pl.program_id(1)

        @pl.when(j == 0)
        def _():
            ext[:, 0:H, :] = jnp.zeros((nlt, H, LANE), F32)

        def stage(r0, lt):
            ext[lt, pl.ds(pl.multiple_of(r0 + H, SUB), _RB), :] = h_ref[pl.ds(r0, _RB), _lanes(lt)].astype(F32)

        def main(r0, lt):
            ls = _lanes(lt)
            gp = b_ref[:, ls]
            for k in range(CONV_F):
                gp = gp + w_ref[k:k + 1, ls] * ext[lt, pl.ds(r0 + (H - (CONV_F - 1 - k)), _RB), :]
            up = h_ref[pl.ds(r0, _RB), _lanes(lt + nlt)].astype(F32)
            o_ref[pl.ds(r0, _RB), ls] = (_gelu(gp) * up).astype(BF16)

        _sub_blocks(R, cw, stage)
        _sub_blocks(R, cw, main)
        ext[:, 0:H, :] = ext[:, R:R + H, :]

    return pl.pallas_call(
        body, out_shape=SDS((S, D_FF), BF16), grid=(D_FF // cw, S // R),
        in_specs=[pl.BlockSpec((R, 2 * cw), lambda c, j: (j, c)), pl.BlockSpec((CONV_F, cw), lambda c, j: (0, c)),
                  pl.BlockSpec((1, cw), lambda c, j: (0, c))],
        out_specs=pl.BlockSpec((R, cw), lambda c, j: (j, c)),
        scratch_shapes=[pltpu.VMEM((nlt, H + R, LANE), F32)], name=name, compiler_params=_cp(2),
    )(hp, w, b)


def _f_bwd(dact, hp, w, b, name):
    S = hp.shape[0]
    R, H, cw, nch = R_SEQ, SUB, _CW_F, S // R_SEQ
    nlt = cw // LANE

    def body(da_ref, h_ref, hh_ref, w_ref, b_ref, dh_ref, dw_ref, db_ref, ext_g, ext_d, acc):
        j = pl.program_id(1)
        jj = nch - 1 - j

        @pl.when(j == 0)
        def _():
            ext_d[:, R:R + H, :] = jnp.zeros((nlt, H, LANE), F32)
            acc[...] = jnp.zeros_like(acc)

        for lt in range(nlt):
            ext_g[lt, 0:H, :] = jnp.where(jj == 0, 0.0, hh_ref[_HB - H:_HB, lt * LANE:(lt + 1) * LANE].astype(F32))

        def stage(r0, lt):
            ext_g[lt, pl.ds(pl.multiple_of(r0 + H, SUB), _RB), :] = h_ref[pl.ds(r0, _RB), _lanes(lt)].astype(F32)

        def first(r0, lt):
            ls, lu, rs = _lanes(lt), _lanes(lt + nlt), pl.ds(r0, _RB)
            taps = [ext_g[lt, pl.ds(r0 + (H - (CONV_F - 1 - k)), _RB), :] for k in range(CONV_F)]
            gp = b_ref[:, ls]
            for k in range(CONV_F):
                gp = gp + w_ref[k:k + 1, ls] * taps[k]
            gel, dgel = _gelu(gp, with_grad=True)
            da = da_ref[rs, ls].astype(F32)
            dh_ref[rs, lu] = (da * gel).astype(BF16)
            dgp = da * h_ref[rs, lu].astype(F32) * dgel
            ext_d[lt, rs, :] = dgp
            acc[CONV_F * SUB:(CONV_F + 1) * SUB, ls] += _psum8(dgp)
            for k in range(CONV_F):
                acc[k * SUB:(k + 1) * SUB, ls] += _psum8(dgp * taps[k])

        def second(r0, lt):
            ls = _lanes(lt)
            dhg = w_ref[CONV_F - 1:CONV_F, ls] * ext_d[lt, pl.ds(r0, _RB), :]
            for k in range(CONV_F - 1):
                dhg = dhg + w_ref[k:k + 1, ls] * ext_d[lt, pl.ds(r0 + (CONV_F - 1 - k), _RB), :]
            dh_ref[pl.ds(r0, _RB), ls] = dhg.astype(BF16)

        _sub_blocks(R, cw, stage)
        _sub_blocks(R, cw, first)
        _sub_blocks(R, cw, second)
        ext_d[:, R:R + H, :] = ext_d[:, 0:H, :]

        @pl.when(j == nch - 1)
        def _():
            for k in range(CONV_F):
                dw_ref[k:k + 1, :] = jnp.sum(acc[k * SUB:(k + 1) * SUB, :], axis=0, keepdims=True)
            db_ref[...] = jnp.sum(acc[CONV_F * SUB:(CONV_F + 1) * SUB, :], axis=0, keepdims=True)

    rows = lambda c, j: (nch - 1 - j, c)
    return pl.pallas_call(
        body, out_shape=[SDS((S, 2 * D_FF), BF16), SDS((CONV_F, D_FF), F32), SDS((1, D_FF), F32)],
        grid=(D_FF // cw, nch),
        in_specs=[pl.BlockSpec((R, cw), rows), pl.BlockSpec((R, 2 * cw), rows),
                  pl.BlockSpec((_HB, 2 * cw), lambda c, j: (jnp.maximum((nch - 1 - j) * (R // _HB) - 1, 0), c)),
                  pl.BlockSpec((CONV_F, cw), lambda c, j: (0, c)), pl.BlockSpec((1, cw), lambda c, j: (0, c))],
        out_specs=[pl.BlockSpec((R, 2 * cw), rows), pl.BlockSpec((CONV_F, cw), lambda c, j: (0, c)),
                   pl.BlockSpec((1, cw), lambda c, j: (0, c))],
        scratch_shapes=[pltpu.VMEM((nlt, H + R, LANE), F32), pltpu.VMEM((nlt, R + H, LANE), F32),
                        pltpu.VMEM(((CONV_F + 1) * SUB, cw), F32)], name=name,
        compiler_params=_cp(2),
    )(dact, hp, hp, w, b)


_CW_C = 256
_H_C = 32


def _c_fwd(h1p, w, b):
    S = h1p.shape[0]
    R, H, cw = R_SEQ, _H_C, _CW_C
    nlt = cw // LANE

    def body(h_ref, w_ref, b_ref, o_ref, ext):
        j = pl.program_id(1)

        @pl.when(j == 0)
        def _():
            ext[:, 0:H, :] = jnp.zeros((nlt, H, LANE), F32)

        def stage(r0, lt):
            rs = pl.ds(r0, _RB)
            gate = h_ref[rs, _lanes(lt + nlt)].astype(F32)
            ext[lt, pl.ds(pl.multiple_of(r0 + H, SUB), _RB), :] = h_ref[rs, _lanes(lt)].astype(F32) * _sigmoid(gate)

        def main(r0, lt):
            ls = _lanes(lt)
            cv = b_ref[:, ls]
            for k in range(CONV_C):
                cv = cv + w_ref[k:k + 1, ls] * ext[lt, pl.ds(r0 + (H - (CONV_C - 1 - k)), _RB), :]
            o_ref[pl.ds(r0, _RB), ls] = cv

        _sub_blocks(R, cw, stage)
        _sub_blocks(R, cw, main)
        ext[:, 0:H, :] = ext[:, R:R + H, :]

    return pl.pallas_call(
        body, out_shape=SDS((S, D), F32), grid=(D // cw, S // R),
        in_specs=[pl.BlockSpec((R, 2 * cw), lambda c, j: (j, c)), pl.BlockSpec((CONV_C, cw), lambda c, j: (0, c)),
                  pl.BlockSpec((1, cw), lambda c, j: (0, c))],
        out_specs=pl.BlockSpec((R, cw), lambda c, j: (j, c)),
        scratch_shapes=[pltpu.VMEM((nlt, H + R, LANE), F32)], name="conf_conv_fwd", compiler_params=_cp(2),
    )(h1p, w, b)


def _c_bwd(dcv, h1p, w):
    S = h1p.shape[0]
    R, H, cw, nch = R_SEQ, _H_C, _CW_C, S // R_SEQ
    nlt = cw // LANE
    a_b, a_val, a_gate = CONV_C * SUB, (CONV_C + 1) * SUB, (CONV_C + 2) * SUB

    def body(dc_ref, h_ref, hh_ref, w_ref, dh_ref, dw_ref, db_ref, db1_ref, ext_u, ext_d, acc):
        j = pl.program_id(1)
        jj = nch - 1 - j

        @pl.when(j == 0)
        def _():
            ext_d[:, R:R + H, :] = jnp.zeros((nlt, H, LANE), F32)
            acc[...] = jnp.zeros_like(acc)

        for lt in range(nlt):
            ext_u[lt, 0:H, :] = jnp.where(
                jj == 0, 0.0, hh_ref[:, lt * LANE:(lt + 1) * LANE].astype(F32)
                * _sigmoid(hh_ref[:, cw + lt * LANE:cw + (lt + 1) * LANE].astype(F32)))

        def stage(r0, lt):
            rs, ls = pl.ds(r0, _RB), _lanes(lt)
            gate = h_ref[rs, _lanes(lt + nlt)].astype(F32)
            ext_u[lt, pl.ds(pl.multiple_of(r0 + H, SUB), _RB), :] = h_ref[rs, ls].astype(F32) * _sigmoid(gate)
            ext_d[lt, rs, :] = dc_ref[rs, ls]

        def first(r0, lt):
            ls = _lanes(lt)
            dc = dc_ref[pl.ds(r0, _RB), ls]
            acc[a_b:a_b + SUB, ls] += _psum8(dc)
            for k in range(CONV_C):
                tap = ext_u[lt, pl.ds(r0 + (H - (CONV_C - 1 - k)), _RB), :]
                acc[k * SUB:(k + 1) * SUB, ls] += _psum8(dc * tap)

        def second(r0, lt):
            rs, ls, lg = pl.ds(r0, _RB), _lanes(lt), _lanes(lt + nlt)
            du = w_ref[CONV_C - 1:CONV_C, ls] * ext_d[lt, rs, :]
            for k in range(CONV_C - 1):
                du = du + w_ref[k:k + 1, ls] * ext_d[lt, pl.ds(r0 + (CONV_C - 1 - k), _RB), :]
            val = h_ref[rs, ls].astype(F32)
            sg = _sigmoid(h_ref[rs, lg].astype(F32))
            dval = du * sg
            dgate = du * val * sg * (1.0 - sg)
            acc[a_val:a_val + SUB, ls] += _psum8(dval)
            acc[a_gate:a_gate + SUB, ls] += _psum8(dgate)
            dh_ref[rs, ls] = dval.astype(BF16)
            dh_ref[rs, lg] = dgate.astype(BF16)

        _sub_blocks(R, cw, stage)
        _sub_blocks(R, cw, first)
        _sub_blocks(R, cw, second)
        ext_d[:, R:R + H, :] = ext_d[:, 0:H, :]

        @pl.when(j == nch - 1)
        def _():
            for k in range(CONV_C):
                dw_ref[k:k + 1, :] = jnp.sum(acc[k * SUB:(k + 1) * SUB, :], axis=0, keepdims=True)
            db_ref[...] = jnp.sum(acc[a_b:a_b + SUB, :], axis=0, keepdims=True)
            db1_ref[:, 0:cw] = jnp.sum(acc[a_val:a_val + SUB, :], axis=0, keepdims=True)
            db1_ref[:, cw:2 * cw] = jnp.sum(acc[a_gate:a_gate + SUB, :], axis=0, keepdims=True)

    rows = lambda c, j: (nch - 1 - j, c)
    return pl.pallas_call(
        body, out_shape=[SDS((S, 2 * D), BF16), SDS((CONV_C, D), F32), SDS((1, D), F32), SDS((1, 2 * D), F32)],
        grid=(D // cw, nch),
        in_specs=[pl.BlockSpec((R, cw), rows), pl.BlockSpec((R, 2 * cw), rows),
                  pl.BlockSpec((H, 2 * cw), lambda c, j: (jnp.maximum((nch - 1 - j) * (R // H) - 1, 0), c)),
                  pl.BlockSpec((CONV_C, cw), lambda c, j: (0, c))],
        out_specs=[pl.BlockSpec((R, 2 * cw), rows), pl.BlockSpec((CONV_C, cw), lambda c, j: (0, c)),
                   pl.BlockSpec((1, cw), lambda c, j: (0, c)), pl.BlockSpec((1, 2 * cw), lambda c, j: (0, c))],
        scratch_shapes=[pltpu.VMEM((nlt, H + R, LANE), F32), pltpu.VMEM((nlt, R + H, LANE), F32),
                        pltpu.VMEM(((CONV_C + 3) * SUB, cw), F32)], name="conf_conv_bwd",
        compiler_params=_cp(2),
    )(dcv, h1p, h1p, w)


def _local_step(x, mem, tgt, W, fetch=None, send=None):
    G = {}
    W = dict(W)

    def arrive(group, after):
        if fetch is None:
            return None
        got, tok = fetch(group, after)
        for key, val in got.items():
            W[key] = {**W.get(key, {}), **val} if isinstance(val, dict) else val
        return tok

    def gain(g, tok):
        return g if tok is None else g + tok

    def sent(group):
        return None if send is None else send(group, G)

    def xattn_fwd(xin, n, l):
        tok = arrive(("xa", l), n)
        mn = _rms_fwd(mem, gain(W["xa_mem_norm"][l:l + 1], tok), f"xa_memnorm_fwd{l}")
        q = _mm_nn(n, W["xa_wq"][l], out_dtype=BF16, name=f"xa_q{l}")
        k = _mm_nn(mn, W["xa_wk"][l], out_dtype=BF16, name=f"xa_k{l}")
        v = _mm_nn(mn, W["xa_wv"][l], out_dtype=BF16, name=f"xa_v{l}")
        o = _attn_fwd(q, k, v, f"xa_attn_fwd{l}")
        xout, nout = _mm_nn(o, W["xa_wo"][l], out_dtype=F32, name=f"xa_o{l}", add=xin, norm=W["f_norm"][l:l + 1])
        return xout, nout, (xin, n, q, mn, k, v, o)

    def xattn_bwd(dx, dxb, saved, l):
        xin, n, q, mn, k, v, o = saved
        do = _mm_nt(dxb, W["xa_wo"][l], out_dtype=BF16, name=f"xa_do{l}")
        G[f"xa_wo{l}"] = _mm_tn(o, dxb, out_dtype=BF16, name=f"xa_dwo{l}")
        dq, dk, dv = _attn_bwd(q, k, v, do, f"xa_attn_bwd{l}")
        dkb, dvb = dk.astype(BF16), dv.astype(BF16)
        G[f"xa_wq{l}"] = _mm_tn(n, dq, out_dtype=BF16, name=f"xa_dwq{l}")
        G[f"xa_wk{l}"] = _mm_tn(mn, dkb, out_dtype=BF16, name=f"xa_dwk{l}")
        G[f"xa_wv{l}"] = _mm_tn(mn, dvb, out_dtype=BF16, name=f"xa_dwv{l}")
        tok = sent(("xa", l))
        dmn = _mm_nt(dkb, W["xa_wk"][l], out_dtype=F32, name=f"xa_dmn_k{l}")
        dmn = _mm_nt(dvb, W["xa_wv"][l], out_dtype=F32, name=f"xa_dmn_v{l}", add=dmn)
        (G[f"xa_mem_norm{l}"],) = _rms_bwd(mem, W["xa_mem_norm"][l:l + 1], dmn, None, f"xa_memnorm_bwd{l}")
        dx, dxb, G[f"xa_norm{l}"] = _mm_nt(dq, W["xa_wq"][l], out_dtype=F32, name=f"xa_dn{l}",
                                           rms=(xin, gain(W["xa_norm"][l:l + 1], tok), dx))
        return dx, dxb

    def ffn_fwd(xin, n, l, next_gain):
        tok = arrive(("f", l), n)
        hp = _mm_nn(n, W["f_w_up"][l], out_dtype=BF16, name=f"f_up{l}")
        act = _f_fwd(hp, W["f_dw_w"][l], gain(W["f_dw_b"][l:l + 1], tok), f"f_conv_fwd{l}")
        res = _mm_nn(act, W["f_w_down"][l], out_dtype=F32, name=f"f_down{l}", add=xin, norm=next_gain)
        xout, nout = res if next_gain is not None else (res, None)
        return xout, nout, (xin, n, hp, act)

    def ffn_bwd(dx, dxb, saved, l):
        xin, n, hp, act = saved
        dact = _mm_nt(dxb, W["f_w_down"][l], out_dtype=BF16, name=f"f_dact{l}")
        G[f"f_w_down{l}"] = _mm_tn(act, dxb, out_dtype=BF16, name=f"f_dwdown{l}")
        dhp, G[f"f_dw_w{l}"], G[f"f_dw_b{l}"] = _f_bwd(dact, hp, W["f_dw_w"][l], W["f_dw_b"][l:l + 1], f"f_conv_bwd{l}")
        G[f"f_w_up{l}"] = _mm_tn(n, dhp, out_dtype=BF16, name=f"f_dwup{l}", blocks=_CW_F)
        tok = sent(("f", l))
        dx, dxb, G[f"f_norm{l}"] = _mm_nt(dhp, W["f_w_up"][l], out_dtype=F32, name=f"f_dn{l}",
                                          rms=(xin, gain(W["f_norm"][l:l + 1], tok), dx))
        return dx, dxb

    n0 = _rms_fwd(x, W["ab_norm"], "ab_norm_fwd")
    tok = arrive(("ab", 0), n0)
    a_par = (W["a_conv_w"], gain(W["a_conv_b"], tok), W["a_gate_x_w"], W["a_gate_x_b"], W["a_gate_a_w"],
             W["a_gate_a_b"], W["a_lambda"])
    b_par = (W["b_group_w"], W["b_group_b"], W["b_scale"])
    zp = _mm_nn(n0, W["ab_w_in"], out_dtype=BF16, name="ab_in")
    yab, h_a = _a_fwd(zp, *a_par)
    yab = _b_fwd(zp, yab, *b_par)
    arrive(("ab", 1), yab)
    x1, n1 = _mm_nn(yab, W["ab_w_out"], out_dtype=F32, name="ab_out", add=x, norm=W["xa_norm"][0:1])
    x2, n2, s_xa0 = xattn_fwd(x1, n1, 0)
    x3, n3, s_f0 = ffn_fwd(x2, n2, 0, W["c_norm"])
    tok = arrive(("c", 0), n3)
    h1p = _mm_nn(n3, W["c_w_pw1"], out_dtype=BF16, name="c_pw1", bias=gain(W["c_b_pw1"], tok))
    cv = _c_fwd(h1p, W["c_dw_w"], W["c_dw_b"])
    sc = _ln_silu_fwd(cv, W["c_ln_g"], W["c_ln_b"])
    x4, n4 = _mm_nn(sc, W["c_w_pw2"], out_dtype=F32, name="c_pw2", bias=W["c_b_pw2"], add=x3, norm=W["xa_norm"][1:2])
    x5, n5, s_xa1 = xattn_fwd(x4, n4, 1)
    x6, _, s_f1 = ffn_fwd(x5, n5, 1, None)
    loss, dx, dxb, G["final_norm"] = _loss_head(x6, W["final_norm"], tgt)

    dx, dxb = ffn_bwd(dx, dxb, s_f1, 1)
    dx, dxb = xattn_bwd(dx, dxb, s_xa1, 1)
    dsc = _mm_nt(dxb, W["c_w_pw2"], out_dtype=BF16, name="c_dsc")
    G["c_w_pw2"] = _mm_tn(sc, dxb, out_dtype=BF16, name="c_dwpw2")
    dcv, G["c_ln_g"], G["c_ln_b"], G["c_b_pw2"] = _ln_silu_bwd(dsc, cv, W["c_ln_g"], W["c_ln_b"], dx)
    dh1p, G["c_dw_w"], G["c_dw_b"], G["c_b_pw1"] = _c_bwd(dcv, h1p, W["c_dw_w"])
    G["c_w_pw1"] = _mm_tn(n3, dh1p, out_dtype=BF16, name="c_dwpw1", blocks=_CW_C)
    tok = sent(("c", 0))
    dx, dxb, G["c_norm"] = _mm_nt(dh1p, W["c_w_pw1"], out_dtype=F32, name="c_dn",
                                  rms=(x3, gain(W["c_norm"], tok), dx))
    dx, dxb = ffn_bwd(dx, dxb, s_f0, 0)
    dx, dxb = xattn_bwd(dx, dxb, s_xa0, 0)
    dyab = _mm_nt(dxb, W["ab_w_out"], out_dtype=BF16, name="ab_dyab")
    G["ab_w_out"] = _mm_tn(yab, dxb, out_dtype=BF16, name="ab_dwout")
    tok = sent(("ab", 1))
    a_par = (a_par[0], gain(a_par[1], tok)) + a_par[2:]
    (dzg, dzr, G["a_conv_w"], G["a_conv_b"], G["a_gate_x_w"], G["a_gate_x_b"], G["a_gate_a_w"], G["a_gate_a_b"],
     G["a_lambda"]) = _a_bwd(dyab, zp, h_a, *a_par)
    dzq, G["b_group_w"], G["b_group_b"], G["b_scale"] = _b_bwd(dyab, zp, *b_par)
    G["ab_w_in"] = jnp.concatenate(
        [_mm_tn(n0, dz, out_dtype=BF16, name=f"ab_dwin_{part}")
         for part, dz in (("gate", dzg), ("rec", dzr), ("pool", dzq))], axis=1)
    tok = sent(("ab", 0))
    dx, _, G["ab_norm"] = _mm_nt_cols([dzg, dzr, dzq], W["ab_w_in"], name="ab_dn",
                                      rms=(x, gain(W["ab_norm"], tok), dx))
    return loss, dx, G


def _my_place():
    x, y, c = lax.axis_index("x"), lax.axis_index("y"), lax.axis_index("c")
    return x, y, c


def _all_gather(shards, name):
    n = len(shards)

    def body(*refs):
        ins, outs = refs[:n], refs[n:2 * n]
        send_sems, recv_sems, local_sems = refs[2 * n:]
        x, y, c = _my_place()
        me, sibling = (x, y, c), (x, y, 1 - c)
        chips = [(1 - x, y), (x, 1 - y), (1 - x, 1 - y)]

        def slab(a, place):
            px, py, pc = place
            return outs[a].at[4 * px + 2 * py + pc]

        def copy(a, k, block, to, src=None):
            return pltpu.make_async_remote_copy(
                src_ref=slab(a, block) if src is None else src, dst_ref=slab(a, block),
                send_sem=send_sems.at[a, k], recv_sem=recv_sems.at[a, k], device_id=to, device_id_type=MESH)

        mine = [pltpu.make_async_copy(ins[a], slab(a, me), local_sems.at[a]) for a in range(n)]
        for cp in mine:
            cp.start()
        first = []
        for j, chip in enumerate(chips):
            first += [copy(a, 1 + j, me, (*chip, c), src=ins[a]) for a in range(n)]
        first += [copy(a, 0, me, sibling, src=ins[a]) for a in range(n)]
        for cp in first:
            cp.start()
        passed = []
        for j, chip in enumerate(chips):
            for a in range(n):
                copy(a, 1 + j, (*chip, c), me).wait_recv()
                cp = copy(a, 4 + j, (*chip, c), sibling)
                cp.start()
                passed.append(cp)
        for a in range(n):
            copy(a, 0, sibling, me).wait_recv()
        for j, chip in enumerate(chips):
            for a in range(n):
                copy(a, 4 + j, (*chip, 1 - c), me).wait_recv()
        for cp in first + passed:
            cp.wait_send()
        for cp in mine:
            cp.wait()

    any_spec = pl.BlockSpec(memory_space=pl.ANY)
    return pl.pallas_call(
        body, out_shape=[SDS((N_DEV,) + s.shape, s.dtype) for s in shards], in_specs=[any_spec] * n,
        out_specs=[any_spec] * n,
        scratch_shapes=[pltpu.SemaphoreType.DMA((n, 7)), pltpu.SemaphoreType.DMA((n, 7)), pltpu.SemaphoreType.DMA((n,))],
        name=name,
    )(*shards)


_HBM = pl.BlockSpec(memory_space=pltpu.HBM)
_SEM = pl.BlockSpec(memory_space=pltpu.SEMAPHORE)
_EFFECT = pltpu.SideEffectType.DATAFLOW_SIDE_EFFECTING


def _peer_places():
    x, y, c = _my_place()
    peers = []
    for k in range(1, N_DEV):
        px = 1 - x if (k >> 2) & 1 else x
        py = 1 - y if (k >> 1) & 1 else y
        pc = 1 - c if k & 1 else c
        peers.append(((px, py, pc), 4 * px + 2 * py + pc))
    return (x, y, c), 4 * x + 2 * y + c, peers


def _send_start(srcs, per_dest, name):
    n = len(srcs)
    lands = [lax.empty((N_DEV,) + (s.shape[1:] if per_dest else s.shape), s.dtype) for s in srcs]

    def body(*refs):
        src, land = refs[:n], refs[n:2 * n]
        outs = refs[2 * n:]
        send, recv, token = outs[:n], outs[n:2 * n], outs[4 * n]
        _, me, peers = _peer_places()
        for a in range(n):
            for peer, pidx in peers:
                pltpu.make_async_remote_copy(
                    src_ref=src[a].at[pidx] if per_dest else src[a], dst_ref=land[a].at[me], send_sem=send[a],
                    recv_sem=recv[a], device_id=peer, device_id_type=MESH).start()
        token[...] = jnp.zeros_like(token)

    hbm = lambda a: pltpu.HBM(a.shape, a.dtype)
    sem = pltpu.SemaphoreType.DMA(())
    res = pl.pallas_call(
        body, name=name,
        out_shape=tuple([sem] * (2 * n) + [hbm(s) for s in srcs] + [hbm(l) for l in lands]
                        + [SDS((SUB, LANE), F32)]),
        in_specs=[_HBM] * (2 * n),
        out_specs=tuple([_SEM] * (2 * n) + [_HBM] * (2 * n) + [pl.BlockSpec(memory_space=pltpu.VMEM)]),
        input_output_aliases={i: 2 * n + i for i in range(2 * n)},
        compiler_params=pltpu.CompilerParams(has_side_effects=_EFFECT),
    )(*[pltpu.with_memory_space_constraint(s, pltpu.HBM) for s in srcs],
      *[pltpu.with_memory_space_constraint(l, pltpu.HBM) for l in lands])
    return res[:n], res[n:2 * n], res[2 * n:3 * n], res[3 * n:4 * n], res[4 * n]


def _send_wait(send, recv, srcs, lands, after, per_dest, name):
    n = len(srcs)

    def body(*refs):
        src, land = refs[:n], refs[n:2 * n]
        send_s, recv_s = refs[2 * n:3 * n], refs[3 * n:4 * n]
        token = refs[-1]
        place, _, _ = _peer_places()
        for a in range(n):
            seven = land[a].at[pl.ds(0, N_DEV - 1)]
            copy = pltpu.make_async_remote_copy(
                src_ref=src[a].at[pl.ds(0, N_DEV - 1)] if per_dest else seven, dst_ref=seven, send_sem=send_s[a],
                recv_sem=recv_s[a], device_id=place, device_id_type=MESH)
            copy.wait_send()
            copy.wait_recv()
        token[...] = jnp.zeros_like(token)

    hbm = lambda a: pltpu.HBM(a.shape, a.dtype)
    res = pl.pallas_call(
        body, name=name,
        out_shape=tuple([hbm(s) for s in srcs] + [hbm(l) for l in lands] + [SDS((SUB, LANE), F32)]),
        in_specs=[_HBM] * (2 * n) + [_SEM] * (2 * n) + [pl.BlockSpec(memory_space=pl.ANY)],
        out_specs=tuple([_HBM] * (2 * n) + [pl.BlockSpec(memory_space=pltpu.VMEM)]),
        input_output_aliases={i: i for i in range(2 * n)},
        compiler_params=pltpu.CompilerParams(has_side_effects=_EFFECT),
    )(*srcs, *lands, *send, *recv, after)
    return res[:n], res[n:2 * n], res[2 * n]


def _adamw_math(w, g, m, v):
    m = ADAM_B1 * m + (1.0 - ADAM_B1) * g
    v = ADAM_B2 * v + (1.0 - ADAM_B2) * (g * g)
    m_hat = m / (1.0 - ADAM_B1 ** ADAM_STEP)
    v_hat = v / (1.0 - ADAM_B2 ** ADAM_STEP)
    delta = -ADAM_LR * (m_hat / (jnp.sqrt(v_hat) + ADAM_EPS) + ADAM_WD * w)
    return delta, m, v


def _row_tile(r, c, itemsize_rows):
    cap = max(SUB, (itemsize_rows // (4 * c)) // SUB * SUB)
    if r <= cap:
        return r
    best = None
    for t in range(SUB, cap + 1, SUB):
        if r % t == 0:
            best = t
    return best if best is not None else r


def _sum_adamw(landing, w, m, v, name, layer=0, prev=None):
    _, r, c = landing.shape
    tr = _row_tile(r, c, 1 << 20)
    off = layer * (r // tr)

    def body(l_ref, w_ref, m_ref, v_ref, *rest):
        g_ref, d_ref, mo_ref, vo_ref = rest[-4:]
        g = l_ref[0].astype(F32)
        for s in range(1, N_DEV):
            g = g + l_ref[s].astype(F32)
        g_ref[...] = g
        d_ref[...], mo_ref[...], vo_ref[...] = _adamw_math(w_ref[...], g, m_ref[...], v_ref[...])

    blk = pl.BlockSpec((tr, c), lambda i: (i + off, 0))
    n_prev = 0 if prev is None else 4
    return pl.pallas_call(
        body, out_shape=[SDS(w.shape, F32)] * 4, grid=(r // tr,),
        in_specs=[pl.BlockSpec((N_DEV, tr, c), lambda i: (0, i, 0)), blk, blk, blk]
        + [pl.BlockSpec(memory_space=pl.ANY)] * n_prev,
        out_specs=[blk] * 4, input_output_aliases={4 + i: i for i in range(n_prev)}, name=name,
        compiler_params=_cp(1),
    )(landing, w, m, v, *([] if prev is None else prev))


def _sum8(landing, name):
    _, r, c = landing.shape

    def body(l_ref, g_ref):
        g = l_ref[0]
        for s in range(1, N_DEV):
            g = g + l_ref[s]
        g_ref[...] = g

    return pl.pallas_call(body, out_shape=SDS((r, c), F32), name=name, compiler_params=_cp(0))(landing)


def _adamw(g, w, m, v, name):
    r, c = g.shape
    tr = _row_tile(r, c, 1 << 20)

    def body(g_ref, w_ref, m_ref, v_ref, d_ref, mo_ref, vo_ref):
        d_ref[...], mo_ref[...], vo_ref[...] = _adamw_math(w_ref[...], g_ref[...], m_ref[...], v_ref[...])

    blk = pl.BlockSpec((tr, c), lambda i: (i, 0))
    return pl.pallas_call(body, out_shape=[SDS((r, c), F32)] * 3, grid=(r // tr,), in_specs=[blk] * 4,
                          out_specs=[blk] * 3, name=name, compiler_params=_cp(1))(g, w, m, v)


_BIG = {
    "ab_w_in": (1, D, 320), "ab_w_out": (1, 192, D), "c_w_pw1": (1, D, 256), "c_w_pw2": (1, 128, D),
    "xa_wq": (2, 128, D), "xa_wk": (2, 128, D), "xa_wv": (2, 128, D), "xa_wo": (2, 128, D),
    "f_w_up": (2, D, 768), "f_w_down": (2, 384, D),
}
_SMALL_SHARDED = {
    "a_conv_w": (1, 4, 128), "c_norm": (1, 128), "c_b_pw1": (1, 256), "c_dw_w": (1, 31, 128), "c_dw_b": (1, 128),
    "c_ln_g": (1, 128), "c_ln_b": (1, 128), "c_b_pw2": (1, 128), "f_dw_w": (2, 3, 384),
}
_REPL = {
    "ab_norm": (1, D), "a_conv_b": (1, D), "a_gate_x_w": (1, 8, 128, 128), "a_gate_x_b": (1, D),
    "a_gate_a_w": (1, 8, 128, 128), "a_gate_a_b": (1, D), "a_lambda": (1, D), "b_group_w": (1, 4, 128, 128),
    "b_group_b": (1, 512), "b_scale": (1, 512), "xa_norm": (2, D), "xa_mem_norm": (2, D), "f_norm": (2, D),
    "f_dw_b": (2, D_FF), "final_norm": (D,),
}


def _size(shape):
    n = 1
    for s in shape:
        n *= s
    return n


_N_SS = sum(_size(s) for s in _SMALL_SHARDED.values())
_N_REPL = sum(_size(s) for s in _REPL.values())
_REPL_ROWS = -(-_N_REPL // (N_DEV * SUB * LANE)) * SUB
_SS_ROWS = _N_SS // LANE
_SMALL_ROWS = -(-(_REPL_ROWS + _SS_ROWS) // SUB) * SUB


def _pack(parts, rows):
    flat = jnp.concatenate([p.reshape(-1).astype(F32) for p in parts])
    return jnp.pad(flat, (0, rows * LANE - flat.shape[0])).reshape(rows, LANE)


def _unpack(buf, table):
    flat, out, off = buf.reshape(-1), {}, 0
    for name, shape in table.items():
        n = _size(shape)
        out[name] = flat[off:off + n].reshape(shape)
        off += n
    return out


def _pair_blocks(v, bw):
    lead, n = v.shape[:-1], v.shape[-1]
    return jnp.swapaxes(v.reshape(lead + (2, n // (2 * bw), bw)), -3, -2).reshape(lead + (n,))


def _unpair_blocks(v, bw):
    lead, n = v.shape[:-1], v.shape[-1]
    return jnp.swapaxes(v.reshape(lead + (n // (2 * bw), 2, bw)), -3, -2).reshape(lead + (n,))


_GROUPS = {
    ("ab", 0): (("ab_w_in", 0),),
    ("ab", 1): (("ab_w_out", 0),),
    ("xa", 0): (("xa_wq", 0), ("xa_wk", 0), ("xa_wv", 0), ("xa_wo", 0)),
    ("f", 0): (("f_w_up", 0), ("f_w_down", 0)),
    ("c", 0): (("c_w_pw1", 0), ("c_w_pw2", 0)),
    ("xa", 1): (("xa_wq", 1), ("xa_wk", 1), ("xa_wv", 1), ("xa_wo", 1)),
    ("f", 1): (("f_w_up", 1), ("f_w_down", 1)),
}
_SEND_GROUPS = _GROUPS


def _weight_layout(name, g):
    if name == "ab_w_in":
        return jnp.swapaxes(g, 0, 1).reshape(D, N_DEV * 320)
    if name in ("c_w_pw1", "f_w_up"):
        return g
    return g.reshape(N_DEV * g.shape[1], D)


def _grad_blocks(name, l, G):
    _, r, c = _BIG[name]
    if name == "ab_w_in":
        return jnp.swapaxes(G[name].reshape(D, N_DEV, 320), 0, 1)
    if name == "c_w_pw1":
        return G[name]
    if name == "f_w_up":
        return G[f"{name}{l}"]
    return (G[name] if _BIG[name][0] == 1 else G[f"{name}{l}"]).reshape(N_DEV, r, c)


def _small_layouts(sm):
    W = {}
    sm = sm.reshape(N_DEV, -1)
    off = 0
    for name, shape in _SMALL_SHARDED.items():
        n = _size(shape)
        blocks = sm[:, off:off + n].reshape((N_DEV,) + shape)
        off += n
        W[name] = jnp.moveaxis(blocks, 0, -2).reshape(shape[:-1] + (N_DEV * shape[-1],))
    W["a_conv_w"], W["c_dw_w"] = W["a_conv_w"][0], W["c_dw_w"][0]
    W["c_b_pw1"] = _pair_blocks(W["c_b_pw1"], _CW_C)
    return W


def _with_own(land, src, me, per_dest):
    own = lax.dynamic_slice_in_dim(src, me, 1, 0) if per_dest else src[None]
    return lax.dynamic_update_slice_in_dim(land, own, me, 0)


def _to_dest_major(g, shape):
    full = g.reshape(shape[:-1] + (N_DEV, shape[-1]))
    return jnp.moveaxis(full, -2, 0).reshape(N_DEV, -1)


def kernel(x, mem, ab_norm, ab_w_in, a_conv_w, a_conv_b, a_gate_x_w, a_gate_x_b, a_gate_a_w, a_gate_a_b, a_lambda, b_group_w, b_group_b, b_scale, ab_w_out, c_norm, c_w_pw1, c_b_pw1, c_dw_w, c_dw_b, c_ln_g, c_ln_b, c_w_pw2, c_b_pw2, xa_norm, xa_mem_norm, xa_wq, xa_wk, xa_wv, xa_wo, f_norm, f_w_up, f_dw_w, f_dw_b, f_w_down, final_norm, loss_target, m_ab_norm, m_ab_w_in, m_a_conv_w, m_a_conv_b, m_a_gate_x_w, m_a_gate_x_b, m_a_gate_a_w, m_a_gate_a_b, m_a_lambda, m_b_group_w, m_b_group_b, m_b_scale, m_ab_w_out, m_c_norm, m_c_w_pw1, m_c_b_pw1, m_c_dw_w, m_c_dw_b, m_c_ln_g, m_c_ln_b, m_c_w_pw2, m_c_b_pw2, m_xa_norm, m_xa_mem_norm, m_xa_wq, m_xa_wk, m_xa_wv, m_xa_wo, m_f_norm, m_f_w_up, m_f_dw_w, m_f_dw_b, m_f_w_down, m_final_norm, v_ab_norm, v_ab_w_in, v_a_conv_w, v_a_conv_b, v_a_gate_x_w, v_a_gate_x_b, v_a_gate_a_w, v_a_gate_a_b, v_a_lambda, v_b_group_w, v_b_group_b, v_b_scale, v_ab_w_out, v_c_norm, v_c_w_pw1, v_c_b_pw1, v_c_dw_w, v_c_dw_b, v_c_ln_g, v_c_ln_b, v_c_w_pw2, v_c_b_pw2, v_xa_norm, v_xa_mem_norm, v_xa_wq, v_xa_wk, v_xa_wv, v_xa_wo, v_f_norm, v_f_w_up, v_f_dw_w, v_f_dw_b, v_f_w_down, v_final_norm):
    args = dict(locals())
    P = {n: args[n] for n in _NAMES}
    M = {n: args["m_" + n] for n in _NAMES}
    V = {n: args["v_" + n] for n in _NAMES}

    me = 4 * lax.axis_index("x") + 2 * lax.axis_index("y") + lax.axis_index("c")

    in_flight = {}

    def launch(groups, tok):
        shards, n_of = [], {}
        for grp in groups:
            for name, l in _GROUPS[grp]:
                w = P[name][l] if tok is None else P[name][l] + tok
                shards.append(w.astype(BF16))
            if grp == ("ab", 0):
                shards.append(_pack([P[n] for n in _SMALL_SHARDED], _SS_ROWS + 4))
            n_of[grp] = len(shards)
        res = _send_start(shards, False, "gather_start_" + "_".join(g[0] + str(g[1]) for g in groups))
        lo = 0
        for grp in groups:
            in_flight[grp] = [r[lo:n_of[grp]] for r in res[:4]]
            lo = n_of[grp]
        return res[4][:1, :1]

    follow = {("ab", 0): [("ab", 1), ("xa", 0), ("f", 0)], ("xa", 0): [("c", 0)], ("f", 0): [("xa", 1)],
              ("c", 0): [("f", 1)]}

    def fetch(grp, after):
        send_s, recv_s, srcs, lands = in_flight.pop(grp)
        srcs, lands, tok = _send_wait(send_s, recv_s, srcs, lands, after, False, f"gather_wait_{grp[0]}{grp[1]}")
        tok = launch(follow[grp], tok[:1, :1]) if grp in follow else None
        full = [_with_own(land, src, me, False) for land, src in zip(lands, srcs)]
        out = {}
        for (name, l), g in zip(_GROUPS[grp], full):
            w = _weight_layout(name, g)
            if _BIG[name][0] == 1:
                out[name] = w
            else:
                out[name] = {l: w}
        if grp == ("ab", 0):
            out.update(_small_layouts(full[-1]))
        return out, tok

    zero = launch([("ab", 0)], None)

    pending = []

    def send(grp, G):
        members = _SEND_GROUPS[grp]
        res = _send_start([_grad_blocks(name, l, G) for name, l in members], True, f"send_{grp[0]}{grp[1]}")
        pending.append((members, res))
        return res[4][:1, :1]

    W = {n: P[n] for n in _REPL}
    W["ab_norm"] = P["ab_norm"] + zero
    W["final_norm"] = P["final_norm"].reshape(1, D)
    W["a_gate_x_w"], W["a_gate_a_w"], W["b_group_w"] = P["a_gate_x_w"][0], P["a_gate_a_w"][0], P["b_group_w"][0]
    loss, grad_x, G = _local_step(x[0], mem[0], loss_target[0], W, fetch, send)
    loss = lax.psum(loss[0, 0], ("x", "y", "c"))

    Gs = dict(G)
    Gs["c_b_pw1"] = _unpair_blocks(G["c_b_pw1"], _CW_C)
    Gs["f_dw_w"] = jnp.stack([G["f_dw_w0"], G["f_dw_w1"]])
    Gs["a_conv_w"], Gs["c_dw_w"] = G["a_conv_w"][None], G["c_dw_w"][None]
    for n in ("xa_norm", "xa_mem_norm", "f_norm", "f_dw_b"):
        Gs[n] = jnp.concatenate([G[f"{n}0"], G[f"{n}1"]], axis=0)
    for n in ("a_gate_x_w", "a_gate_a_w", "b_group_w"):
        Gs[n] = G[n][None]
    repl_flat = jnp.concatenate([Gs[n].reshape(-1) for n in _REPL])
    repl_rows = jnp.pad(repl_flat, (0, N_DEV * _REPL_ROWS * LANE - _N_REPL)).reshape(N_DEV, _REPL_ROWS, LANE)
    ss_rows = jnp.concatenate([_to_dest_major(Gs[n], s) for n, s in _SMALL_SHARDED.items()], axis=1)
    ss_rows = ss_rows.reshape(N_DEV, _SS_ROWS, LANE)
    small_pack = jnp.concatenate(
        [repl_rows, ss_rows, jnp.zeros((N_DEV, _SMALL_ROWS - _REPL_ROWS - _SS_ROWS, LANE), F32)], axis=1)
    last = _send_start([small_pack], True, "send_small")
    pending.append(((("small", 0),), last))

    members = [m for mem_, _ in pending for m in mem_]
    cat = [[a for _, res in pending for a in res[i]] for i in range(4)]
    srcs, lands, _ = _send_wait(cat[0], cat[1], cat[2], cat[3], grad_x, True, "send_wait")
    landed = {m: _with_own(land, src, me, True) for m, land, src in zip(members, lands, srcs)}

    out_g, out_d, out_m, out_v = {}, {}, {}, {}
    for name, (layers, r, c) in _BIG.items():
        shape = P[name].shape
        w2, m2, v2 = [t[name].reshape(layers * r, c) for t in (P, M, V)]
        res = None
        for l in range(layers):
            res = _sum_adamw(landed[(name, l)], w2, m2, v2, f"adamw_{name}{l}", layer=l, prev=res)
        out_g[name], out_d[name], out_m[name], out_v[name] = [t.reshape(shape) for t in res]

    small_sum = _sum8(landed[("small", 0)], "sum_small")
    (repl_all,) = _all_gather([small_sum[:_REPL_ROWS]], "gather_small_grads")
    g_repl = _unpack(repl_all, _REPL)
    g_ss = _unpack(small_sum[_REPL_ROWS:_REPL_ROWS + _SS_ROWS], _SMALL_SHARDED)
    table = dict(_REPL)
    table.update(_SMALL_SHARDED)
    rows = -(-(_N_REPL + _N_SS) // (256 * LANE)) * 256
    g_small = dict(g_repl)
    g_small.update(g_ss)
    packs = [_pack([src[n] for n in table], rows) for src in (g_small, P, M, V)]
    res = _adamw(*packs, "adamw_small")
    for out, buf in zip((out_d, out_m, out_v), res):
        out.update(_unpack(buf, table))
    out_g.update(g_small)

    return (loss, grad_x[None], *[out_g[n] for n in _NAMES], *[out_d[n] for n in _NAMES],
            *[out_m[n] for n in _NAMES], *[out_v[n] for n in _NAMES])


_NAMES = ("ab_norm", "ab_w_in", "a_conv_w", "a_conv_b", "a_gate_x_w", "a_gate_x_b", "a_gate_a_w", "a_gate_a_b",
          "a_lambda", "b_group_w", "b_group_b", "b_scale", "ab_w_out", "c_norm", "c_w_pw1", "c_b_pw1", "c_dw_w",
          "c_dw_b", "c_ln_g", "c_ln_b", "c_w_pw2", "c_b_pw2", "xa_norm", "xa_mem_norm", "xa_wq", "xa_wk", "xa_wv",
          "xa_wo", "f_norm", "f_w_up", "f_dw_w", "f_dw_b", "f_w_down", "final_norm")
```

```python
import functools

import jax
import jax.numpy as jnp
from jax import lax
from jax.experimental import pallas as pl
from jax.experimental.pallas import tpu as pltpu

F32, BF16 = jnp.float32, jnp.bfloat16
SDS = jax.ShapeDtypeStruct
MESH = pl.DeviceIdType.MESH

N_DEV = 8
D = 1024
N_MEM = 256
XA_HEADS, XA_HD = 4, 256
HD_A = 128
CONV_A, CONV_C, CONV_F = 4, 31, 3
C_RG = 8.0
POOL_WINDOWS = (2, 4, 8, 16)
D_FF = 3 * D
EPS = 1e-6
ADAM_LR, ADAM_B1, ADAM_B2, ADAM_EPS, ADAM_WD, ADAM_STEP = 0.001, 0.9, 0.999, 1e-08, 0.01, 10

LANE = 128
SUB = 8
VMEM_LIMIT = 56 * 1024 * 1024
R_SEQ = 256
TM_ROW = 512


def _cp(n_axes):
    return pltpu.CompilerParams(dimension_semantics=("arbitrary",) * n_axes, vmem_limit_bytes=VMEM_LIMIT)


def _tile(n, pref):
    if n <= pref:
        return n
    best = None
    for t in range(LANE, pref + 1, LANE):
        if n % t == 0:
            best = t
    assert best is not None, (n, pref)
    return best


def _perm2(n):
    return (n % 2) * 4 + n // 2


_NN = (((1,), (0,)), ((), ()))
_NT = (((1,), (1,)), ((), ()))
_TN = (((0,), (0,)), ((), ()))


def _mm_call(name, grid, ab, ab_specs, dims, acc_shape, extras, outs, finish, from_ref=False):
    nk = grid[2]
    n_ab, n_ex, n_out = len(ab), len(extras), len(outs)
    use_acc = nk > 1 or from_ref

    def product(refs):
        r = lax.dot_general(refs[0][...], refs[1][...], dims, preferred_element_type=F32)
        for i in range(1, n_ab):
            r = r + lax.dot_general(refs[2 * i][...], refs[2 * i + 1][...], dims, preferred_element_type=F32)
        return r

    def body(*refs):
        rest = refs[2 * n_ab:]
        ex_refs, o_refs = rest[:n_ex], rest[n_ex:n_ex + n_out]
        first_rows = pl.program_id(0) == 0
        if not use_acc:
            finish(product(refs), ex_refs, o_refs, first_rows)
            return
        acc = rest[n_ex + n_out]
        if nk == 1:
            acc[...] = product(refs)
            finish(acc, ex_refs, o_refs, first_rows)
            return
        k = pl.program_id(2)

        @pl.when(k == 0)
        def _():
            acc[...] = jnp.zeros_like(acc)

        acc[...] += product(refs)

        @pl.when(k == nk - 1)
        def _():
            finish(acc if from_ref else acc[...], ex_refs, o_refs, first_rows)

    res = pl.pallas_call(
        body, out_shape=[o for o, _ in outs], grid=grid,
        in_specs=list(ab_specs) + [s for _, s in extras], out_specs=[s for _, s in outs],
        scratch_shapes=[pltpu.VMEM(acc_shape, F32)] if use_acc else [], name=name, compiler_params=_cp(3),
    )(*[t for pair in ab for t in pair], *[e for e, _ in extras])
    return res[0] if n_out == 1 else res


def _finish_sum(r, ex_refs, o_refs, first_rows):
    del first_rows
    for e in ex_refs:
        r = r + e[...]
    o_refs[0][...] = r.astype(o_refs[0].dtype)


def _finish_sum_norm(r, ex_refs, o_refs, first_rows):
    del first_rows
    for e in ex_refs[:-1]:
        r = r + e[...]
    o_refs[0][...] = r
    o_refs[1][...] = ((r * lax.rsqrt(jnp.mean(r * r, axis=-1, keepdims=True) + EPS)) * ex_refs[-1][...]).astype(BF16)


_EPI_ROWS = 16


def _finish_rms_bwd(r_ref, ex_refs, o_refs, first_rows):
    x_ref, g_ref, dres_ref = ex_refs
    dx_ref, dxb_ref, dg_ref = o_refs

    @pl.when(first_rows)
    def _():
        dg_ref[...] = jnp.zeros_like(dg_ref)

    gv = g_ref[...]
    inv_d = 1.0 / r_ref.shape[1]

    def step(i, dg_acc):
        groups = [pl.ds(pl.multiple_of(i * (2 * _EPI_ROWS) + u * _EPI_ROWS, _EPI_ROWS), _EPI_ROWS) for u in range(2)]
        sums = []
        for rows in groups:
            r, xf = r_ref[rows, :], x_ref[rows, :]
            sums.append((jnp.sum(xf * xf, axis=-1, keepdims=True), jnp.sum((r * gv) * xf, axis=-1, keepdims=True)))
        for rows, (sxx, sax) in zip(groups, sums):
            r, xf = r_ref[rows, :], x_ref[rows, :]
            rs = lax.rsqrt(sxx * inv_d + EPS)
            dg_acc = dg_acc + _psum8(r * (xf * rs))
            dx = rs * (r * gv) - xf * (rs * rs * (sax * rs * inv_d)) + dres_ref[rows, :]
            dx_ref[rows, :] = dx
            dxb_ref[rows, :] = dx.astype(BF16)
        return dg_acc

    dg_acc = lax.fori_loop(0, r_ref.shape[0] // (2 * _EPI_ROWS), step, jnp.zeros((SUB, r_ref.shape[1]), F32))
    dg_ref[...] += jnp.sum(dg_acc, axis=0, keepdims=True)


def _rms_bwd_io(M, tm, x, g, dres):
    rows = pl.BlockSpec((tm, D), lambda m, n, k: (m, 0))
    vec = pl.BlockSpec((1, D), lambda m, n, k: (0, 0))
    return ([(x, rows), (g, vec), (dres, rows)],
            [(SDS((M, D), F32), rows), (SDS((M, D), BF16), rows), (SDS((1, D), F32), vec)])


_K_WHOLE = 3072


def _mm_nn(a, b, *, out_dtype, name, bias=None, add=None, norm=None):
    M, K = a.shape
    tk = K if K <= _K_WHOLE else _tile(K, 1024)
    tm = _tile(M, 1024 if K <= 1024 and norm is None else 512)
    if b.ndim == 3:
        nb, _, bw = b.shape
        N, tn, nn = nb * bw, bw, nb
        b_spec = pl.BlockSpec((None, tk, bw), lambda m, n, k: (_perm2(n), k, 0))
    else:
        N = b.shape[1]
        tn = _tile(N, 1024)
        nn = N // tn
        b_spec = pl.BlockSpec((tk, tn), lambda m, n, k: (k, n))
    tile = pl.BlockSpec((tm, tn), lambda m, n, k: (m, n))
    vec = pl.BlockSpec((1, tn), lambda m, n, k: (0, n))
    extras = ([] if bias is None else [(bias, vec)]) + ([] if add is None else [(add, tile)])
    outs, finish = [(SDS((M, N), out_dtype), tile)], _finish_sum
    if norm is not None:
        assert tn == N == D and out_dtype == F32
        extras.append((norm, vec))
        outs, finish = outs + [(SDS((M, N), BF16), tile)], _finish_sum_norm
    return _mm_call(name, (M // tm, nn, K // tk), [(a, b)], [pl.BlockSpec((tm, tk), lambda m, n, k: (m, k)), b_spec],
                    _NN, (tm, tn), extras, outs, finish)


def _mm_nt(a, b, *, out_dtype, name, add=None, rms=None):
    M, N = a.shape
    if b.ndim == 3:
        nb, Ko, bw = b.shape
        tm = _tile(M, 1024 if rms is None else 512)
        tn, tk, nk = _tile(Ko, 1024), bw, nb
        b_spec = pl.BlockSpec((None, tn, bw), lambda m, n, k: (_perm2(k), n, 0))
    else:
        Ko = b.shape[0]
        tk = N if N <= _K_WHOLE else _tile(N, 1024)
        tm = _tile(M, 1024 if N <= 1024 and rms is None else 512)
        tn = _tile(Ko, 1024)
        nk = N // tk
        b_spec = pl.BlockSpec((tn, tk), lambda m, n, k: (n, k))
    tile = pl.BlockSpec((tm, tn), lambda m, n, k: (m, n))
    extras = [] if add is None else [(add, tile)]
    outs, finish = [(SDS((M, Ko), out_dtype), tile)], _finish_sum
    if rms is not None:
        assert tn == Ko == D and add is None
        (extras, outs), finish = _rms_bwd_io(M, tm, *rms), _finish_rms_bwd
    return _mm_call(name, (M // tm, Ko // tn, nk), [(a, b)], [pl.BlockSpec((tm, tk), lambda m, n, k: (m, k)), b_spec],
                    _NT, (tm, tn), extras, outs, finish, from_ref=rms is not None)


def _mm_nt_cols(parts, b, *, name, rms):
    M = parts[0].shape[0]
    tm = _tile(M, 512)
    specs, off = [], 0
    for p in parts:
        w = p.shape[1]
        assert off % w == 0
        specs.append(pl.BlockSpec((tm, w), lambda m, n, k: (m, 0)))
        specs.append(pl.BlockSpec((D, w), functools.partial(lambda m, n, k, o: (0, o), o=off // w)))
        off += w
    extras, outs = _rms_bwd_io(M, tm, *rms)
    return _mm_call(name, (M // tm, 1, 1), [(p, b) for p in parts], specs, _NT, (tm, D), extras, outs, _finish_rms_bwd,
                    from_ref=True)


def _mm_tn(a, b, *, out_dtype, name, blocks=None):
    S, Ka = a.shape
    Nb = b.shape[1]
    tm, tk = _tile(Ka, 1024), _tile(S, 2048)
    if blocks is not None:
        bw = blocks
        tn, nn = bw, Nb // bw
        out = (SDS((nn, Ka, bw), out_dtype), pl.BlockSpec((None, tm, bw), lambda m, n, k: (_perm2(n), m, 0)))
    else:
        tn = _tile(Nb, 1024)
        nn = Nb // tn
        out = (SDS((Ka, Nb), out_dtype), pl.BlockSpec((tm, tn), lambda m, n, k: (m, n)))
    return _mm_call(name, (Ka // tm, nn, S // tk), [(a, b)],
                    [pl.BlockSpec((tk, tm), lambda m, n, k: (k, m)), pl.BlockSpec((tk, tn), lambda m, n, k: (k, n))],
                    _TN, (tm, tn), [], [out], _finish_sum)


def _row(tm, c):
    return pl.BlockSpec((tm, c), lambda i: (i, 0))


def _full(shape):
    nd = len(shape)
    return pl.BlockSpec(shape, lambda i: (0,) * nd)


def _rms_fwd(x, g, name):
    S = x.shape[0]
    tm = min(S, TM_ROW)

    def body(x_ref, g_ref, o_ref):
        xf = x_ref[...]
        r = lax.rsqrt(jnp.mean(xf * xf, axis=-1, keepdims=True) + EPS)
        o_ref[...] = ((xf * r) * g_ref[...]).astype(BF16)

    return pl.pallas_call(body, out_shape=SDS((S, D), BF16), grid=(S // tm,), in_specs=[_row(tm, D), _full((1, D))],
                          out_specs=_row(tm, D), name=name, compiler_params=_cp(1))(x, g)


def _rms_bwd(x, g, dn, dres, name):
    S = x.shape[0]
    tm = min(S, TM_ROW)
    want_dx = dres is not None

    def body(x_ref, g_ref, dn_ref, *rest):
        i = pl.program_id(0)
        dg_ref = rest[-1]

        @pl.when(i == 0)
        def _():
            dg_ref[...] = jnp.zeros_like(dg_ref)

        xf = x_ref[...]
        r = lax.rsqrt(jnp.mean(xf * xf, axis=-1, keepdims=True) + EPS)
        y = xf * r
        dn_v = dn_ref[...]
        dg_ref[...] += jnp.sum(dn_v * y, axis=0, keepdims=True)
        if want_dx:
            dres_ref, dx_ref, dxb_ref = rest[0], rest[1], rest[2]
            dy = dn_v * g_ref[...]
            dx = r * (dy - y * jnp.mean(dy * y, axis=-1, keepdims=True)) + dres_ref[...]
            dx_ref[...] = dx
            dxb_ref[...] = dx.astype(BF16)

    ins = [x, g, dn] + ([dres] if want_dx else [])
    in_specs = [_row(tm, D), _full((1, D)), _row(tm, D)] + ([_row(tm, D)] if want_dx else [])
    outs = ([SDS((S, D), F32), SDS((S, D), BF16)] if want_dx else []) + [SDS((1, D), F32)]
    out_specs = ([_row(tm, D), _row(tm, D)] if want_dx else []) + [_full((1, D))]
    return pl.pallas_call(body, out_shape=outs, grid=(S // tm,), in_specs=in_specs, out_specs=out_specs, name=name,
                          compiler_params=_cp(1))(*ins)


def _loss_head(x, g, tgt):
    S = x.shape[0]
    tm = min(S, TM_ROW)

    def body(x_ref, g_ref, t_ref, loss_ref, dx_ref, dxb_ref, dg_ref):
        i = pl.program_id(0)

        @pl.when(i == 0)
        def _():
            loss_ref[...] = jnp.zeros_like(loss_ref)
            dg_ref[...] = jnp.zeros_like(dg_ref)

        xf = x_ref[...]
        r = lax.rsqrt(jnp.mean(xf * xf, axis=-1, keepdims=True) + EPS)
        y = xf * r
        gv = g_ref[...]
        err = y * gv - t_ref[...]
        per_row = jnp.mean(err * err, axis=-1, keepdims=True)
        loss_ref[...] += 0.5 * jnp.sum(per_row, axis=0, keepdims=True)
        dn_v = err * (1.0 / D)
        dg_ref[...] += jnp.sum(dn_v * y, axis=0, keepdims=True)
        dy = dn_v * gv
        dx = r * (dy - y * jnp.mean(dy * y, axis=-1, keepdims=True))
        dx_ref[...] = dx
        dxb_ref[...] = dx.astype(BF16)

    return pl.pallas_call(
        body, out_shape=[SDS((1, 1), F32), SDS((S, D), F32), SDS((S, D), BF16), SDS((1, D), F32)], grid=(S // tm,),
        in_specs=[_row(tm, D), _full((1, D)), _row(tm, D)],
        out_specs=[_full((1, 1)), _row(tm, D), _row(tm, D), _full((1, D))], name="loss_head", compiler_params=_cp(1),
    )(x, g, tgt)


def _softmax_rows(s):
    m = jnp.max(s, axis=-1, keepdims=True)
    e = jnp.exp(s - m)
    return e / jnp.sum(e, axis=-1, keepdims=True)


def _attn_fwd(q, k, v, name):
    S = q.shape[0]
    tm = min(S, TM_ROW)
    scale = XA_HD ** -0.5

    def body(q_ref, k_ref, v_ref, o_ref):
        for h in range(XA_HEADS):
            sl = slice(h * XA_HD, (h + 1) * XA_HD)
            s = lax.dot_general(q_ref[:, sl], k_ref[:, sl], _NT, preferred_element_type=F32) * scale
            p = _softmax_rows(s)
            o_ref[:, sl] = lax.dot_general(p.astype(BF16), v_ref[:, sl], _NN, preferred_element_type=F32).astype(BF16)

    return pl.pallas_call(body, out_shape=SDS((S, D), BF16), grid=(S // tm,),
                          in_specs=[_row(tm, D), _full((N_MEM, D)), _full((N_MEM, D))], out_specs=_row(tm, D),
                          name=name, compiler_params=_cp(1))(q, k, v)


def _attn_bwd(q, k, v, do, name):
    S = q.shape[0]
    tm = min(S, TM_ROW)
    scale = XA_HD ** -0.5

    def body(q_ref, k_ref, v_ref, do_ref, dq_ref, dk_ref, dv_ref):
        i = pl.program_id(0)

        @pl.when(i == 0)
        def _():
            dk_ref[...] = jnp.zeros_like(dk_ref)
            dv_ref[...] = jnp.zeros_like(dv_ref)

        for h in range(XA_HEADS):
            sl = slice(h * XA_HD, (h + 1) * XA_HD)
            qh, kh, vh, doh = q_ref[:, sl], k_ref[:, sl], v_ref[:, sl], do_ref[:, sl]
            s = lax.dot_general(qh, kh, _NT, preferred_element_type=F32) * scale
            p = _softmax_rows(s)
            pb = p.astype(BF16)
            dv_ref[:, sl] += lax.dot_general(pb, doh, _TN, preferred_element_type=F32)
            dp = lax.dot_general(doh, vh, _NT, preferred_element_type=F32)
            ds = (p * (dp - jnp.sum(dp * p, axis=-1, keepdims=True)) * scale).astype(BF16)
            dq_ref[:, sl] = lax.dot_general(ds, kh, _NN, preferred_element_type=F32).astype(BF16)
            dk_ref[:, sl] += lax.dot_general(ds, qh, _TN, preferred_element_type=F32)

    return pl.pallas_call(
        body, out_shape=[SDS((S, D), BF16), SDS((N_MEM, D), F32), SDS((N_MEM, D), F32)], grid=(S // tm,),
        in_specs=[_row(tm, D), _full((N_MEM, D)), _full((N_MEM, D)), _row(tm, D)],
        out_specs=[_row(tm, D), _full((N_MEM, D)), _full((N_MEM, D))], name=name, compiler_params=_cp(1),
    )(q, k, v, do)


def _sigmoid(x):
    return 1.0 / (1.0 + jnp.exp(-x))


def _ln_silu_fwd(cv, g, b):
    S = cv.shape[0]
    tm = min(S, TM_ROW)

    def body(x_ref, g_ref, b_ref, o_ref):
        xf = x_ref[...]
        mu = jnp.mean(xf, axis=-1, keepdims=True)
        xc = xf - mu
        rstd = lax.rsqrt(jnp.mean(xc * xc, axis=-1, keepdims=True) + EPS)
        ln = (xc * rstd) * g_ref[...] + b_ref[...]
        o_ref[...] = (ln * _sigmoid(ln)).astype(BF16)

    return pl.pallas_call(body, out_shape=SDS((S, D), BF16), grid=(S // tm,),
                          in_specs=[_row(tm, D), _full((1, D)), _full((1, D))], out_specs=_row(tm, D),
                          name="ln_silu_fwd", compiler_params=_cp(1))(cv, g, b)


def _ln_silu_bwd(ds, cv, g, b, dx):
    S = cv.shape[0]
    tm = min(S, TM_ROW)

    def body(ds_ref, x_ref, g_ref, b_ref, dx_ref, dcv_ref, dg_ref, db_ref, db2_ref):
        i = pl.program_id(0)

        @pl.when(i == 0)
        def _():
            dg_ref[...] = jnp.zeros_like(dg_ref)
            db_ref[...] = jnp.zeros_like(db_ref)
            db2_ref[...] = jnp.zeros_like(db2_ref)

        xf = x_ref[...]
        mu = jnp.mean(xf, axis=-1, keepdims=True)
        xc = xf - mu
        rstd = lax.rsqrt(jnp.mean(xc * xc, axis=-1, keepdims=True) + EPS)
        xhat = xc * rstd
        gv = g_ref[...]
        ln = xhat * gv + b_ref[...]
        sg = _sigmoid(ln)
        dln = ds_ref[...].astype(F32) * (sg + ln * sg * (1.0 - sg))
        dg_ref[...] += jnp.sum(dln * xhat, axis=0, keepdims=True)
        db_ref[...] += jnp.sum(dln, axis=0, keepdims=True)
        db2_ref[...] += jnp.sum(dx_ref[...], axis=0, keepdims=True)
        dxh = dln * gv
        dcv_ref[...] = rstd * (dxh - jnp.mean(dxh, axis=-1, keepdims=True)
                               - xhat * jnp.mean(dxh * xhat, axis=-1, keepdims=True))

    return pl.pallas_call(
        body, out_shape=[SDS((S, D), F32), SDS((1, D), F32), SDS((1, D), F32), SDS((1, D), F32)], grid=(S // tm,),
        in_specs=[_row(tm, D), _row(tm, D), _full((1, D)), _full((1, D)), _row(tm, D)],
        out_specs=[_row(tm, D), _full((1, D)), _full((1, D)), _full((1, D))], name="ln_silu_bwd",
        compiler_params=_cp(1),
    )(ds, cv, g, b, dx)


_GELU_C, _GELU_K = 0.7978845608028654, 0.044715


def _gelu(x, with_grad=False):
    x2 = x * x
    t = jnp.tanh(_GELU_C * (x + _GELU_K * x * x2))
    gel = 0.5 * x * (1.0 + t)
    if not with_grad:
        return gel
    return gel, 0.5 * (1.0 + t) + 0.5 * x * (1.0 - t * t) * (_GELU_C * (1.0 + 3.0 * _GELU_K * x2))


def _expm1(x):
    poly = x * (1.0 + x * (0.5 + x * (1.0 / 6.0 + x * (1.0 / 24.0 + x * (1.0 / 120.0)))))
    return jnp.where(jnp.abs(x) < 0.05, poly, jnp.exp(x) - 1.0)


def _softplus(x):
    return jnp.maximum(x, 0.0) + jnp.log1p(jnp.exp(-jnp.abs(x)))


_SCAN_UNROLL = 4
_RB = 32
_HB = 16


def _sub_blocks(n_rows, n_lanes, fn):
    def step(idx, c):
        r0 = pl.multiple_of(idx * _RB, _RB)
        for lt in range(n_lanes // LANE):
            fn(r0, lt)
        return c

    lax.fori_loop(0, n_rows // _RB, step, 0)


def _lanes(lt):
    return pl.ds(lt * LANE, LANE)


def _psum8(x):
    parts = [x[i * SUB:(i + 1) * SUB] for i in range(x.shape[0] // SUB)]
    return functools.reduce(lambda p, q: p + q, parts)


def _scan_fwd(a_s, b_s, out_ref, carry_ref, n_groups):
    row = lax.broadcasted_iota(jnp.int32, (SUB, LANE), 0)
    U = _SCAN_UNROLL

    def step(gi, carry):
        base = gi * (SUB * U)
        parts = []
        for u in range(U):
            i = pl.multiple_of(base + u * SUB, SUB)
            a8, b8 = a_s[pl.ds(i, SUB), :], b_s[pl.ds(i, SUB), :]
            for s in (1, 2, 4):
                a_sh = jnp.where(row >= s, pltpu.roll(a8, s, 0), 1.0)
                b_sh = jnp.where(row >= s, pltpu.roll(b8, s, 0), 0.0)
                b8 = a8 * b_sh + b8
                a8 = a8 * a_sh
            parts.append((i, a8, b8))
        for i, a8, b8 in parts:
            h8 = a8 * carry + b8
            out_ref[pl.ds(i, SUB), :] = h8
            carry = jnp.broadcast_to(h8[SUB - 1:SUB, :], (SUB, LANE))
        return carry

    carry_ref[...] = lax.fori_loop(0, n_groups // U, step, carry_ref[...])


def _scan_bwd(a_s, b_s, out_ref, carry_ref, n_groups):
    row = lax.broadcasted_iota(jnp.int32, (SUB, LANE), 0)
    U = _SCAN_UNROLL

    def step(gi, carry):
        base = (n_groups // U - 1 - gi) * (SUB * U)
        parts = []
        for u in reversed(range(U)):
            i = pl.multiple_of(base + u * SUB, SUB)
            a8, b8 = a_s[pl.ds(i, SUB), :], b_s[pl.ds(i, SUB), :]
            for s in (1, 2, 4):
                a_sh = jnp.where(row < SUB - s, pltpu.roll(a8, SUB - s, 0), 1.0)
                b_sh = jnp.where(row < SUB - s, pltpu.roll(b8, SUB - s, 0), 0.0)
                b8 = a8 * b_sh + b8
                a8 = a8 * a_sh
            parts.append((i, a8, b8))
        for i, a8, b8 in parts:
            h8 = a8 * carry + b8
            out_ref[pl.ds(i, SUB), :] = h8
            carry = jnp.broadcast_to(h8[0:1, :], (SUB, LANE))
        return carry

    carry_ref[...] = lax.fori_loop(0, n_groups // U, step, carry_ref[...])


def _rglru_pre(xr, wgx_ref, bgx_ref, wga_ref, bga_ref, lam_ref):
    xrb = xr.astype(BF16)
    wgx, wga = wgx_ref[0].astype(BF16), wga_ref[0].astype(BF16)
    gx = _sigmoid(lax.dot_general(xrb, wgx, _NN, preferred_element_type=F32) + bgx_ref[...])
    ga = _sigmoid(lax.dot_general(xrb, wga, _NN, preferred_element_type=F32) + bga_ref[...])
    sp = _softplus(-lam_ref[...])
    log_a = -C_RG * ga * sp
    a = jnp.exp(log_a)
    mult = jnp.sqrt(-_expm1(2.0 * log_a))
    return gx, ga, sp, a, mult, xrb, wgx, wga


def _a_specs():
    vec = pl.BlockSpec((1, HD_A), lambda c, j: (0, c))
    mat = pl.BlockSpec((1, HD_A, HD_A), lambda c, j: (c, 0, 0))
    return [pl.BlockSpec((CONV_A, HD_A), lambda c, j: (0, c)), vec, mat, vec, mat, vec, vec]


def _a_fwd(zp, conv_w, conv_b, wgx, bgx, wga, bga, lam):
    S = zp.shape[0]
    R, nt = R_SEQ, D // HD_A
    H = SUB

    def body(zg_ref, zr_ref, cw_ref, cb_ref, wgx_ref, bgx_ref, wga_ref, bga_ref, lam_ref, ya_ref, h_ref,
             ext, a_s, b_s, hc):
        j = pl.program_id(1)

        @pl.when(j == 0)
        def _():
            ext[0:H, :] = jnp.zeros((H, HD_A), F32)
            hc[...] = jnp.zeros_like(hc)

        ext[H:H + R, :] = zr_ref[...].astype(F32)
        xr = cb_ref[...]
        for k in range(CONV_A):
            xr = xr + cw_ref[k:k + 1, :] * ext[pl.ds(H - (CONV_A - 1 - k), R), :]
        gx, _, _, a, mult, _, _, _ = _rglru_pre(xr, wgx_ref, bgx_ref, wga_ref, bga_ref, lam_ref)
        a_s[...] = a
        b_s[...] = mult * (gx * xr)
        _scan_fwd(a_s, b_s, h_ref, hc, R // SUB)
        ya_ref[...] = (_gelu(zg_ref[...].astype(F32)) * h_ref[...]).astype(BF16)
        ext[0:H, :] = ext[R:R + H, :]

    return pl.pallas_call(
        body, out_shape=[SDS((S, D + D // 2), BF16), SDS((S, D), F32)], grid=(nt, S // R),
        in_specs=[pl.BlockSpec((R, HD_A), lambda c, j: (j, c)), pl.BlockSpec((R, HD_A), lambda c, j: (j, nt + c))]
        + _a_specs(),
        out_specs=[pl.BlockSpec((R, HD_A), lambda c, j: (j, c)), pl.BlockSpec((R, HD_A), lambda c, j: (j, c))],
        scratch_shapes=[pltpu.VMEM((H + R, HD_A), F32), pltpu.VMEM((R, HD_A), F32), pltpu.VMEM((R, HD_A), F32),
                        pltpu.VMEM((SUB, HD_A), F32)],
        name="rglru_fwd", compiler_params=_cp(2),
    )(zp, zp, conv_w, conv_b, wgx, bgx, wga, bga, lam)


def _a_bwd(dyab, zp, h, conv_w, conv_b, wgx, bgx, wga, bga, lam):
    S = zp.shape[0]
    R, nt, nch = R_SEQ, D // HD_A, S // R_SEQ
    H = SUB

    def rows(c, j):
        return (nch - 1 - j, c)

    def rows_rec(c, j):
        return (nch - 1 - j, nt + c)

    def halo(c, j):
        return (jnp.maximum((nch - 1 - j) * (R // H) - 1, 0), c)

    def halo_z(c, j):
        return (jnp.maximum((nch - 1 - j) * (R // _HB) - 1, 0), nt + c)

    def body(dy_ref, zg_ref, zr_ref, zh_ref, h_ref, hh_ref, cw_ref, cb_ref, wgx_ref, bgx_ref, wga_ref, bga_ref,
             lam_ref, dzg_ref, dzr_ref, dcw_ref, dcb_ref, dwgx_ref, dbgx_ref, dwga_ref, dbga_ref, dlam_ref,
             ext_z, ext_h, ext_mu, ext_d, a_s, b_s, muc):
        j = pl.program_id(1)
        first_chunk = (nch - 1 - j) == 0

        @pl.when(j == 0)
        def _():
            ext_mu[R:R + H, :] = jnp.zeros((H, HD_A), F32)
            ext_d[R:R + H, :] = jnp.zeros((H, HD_A), F32)
            muc[...] = jnp.zeros_like(muc)
            for r in (dcw_ref, dcb_ref, dwgx_ref, dbgx_ref, dwga_ref, dbga_ref, dlam_ref):
                r[...] = jnp.zeros_like(r)

        zg = zg_ref[...].astype(F32)
        ext_z[0:H, :] = jnp.where(first_chunk, 0.0, zh_ref[_HB - H:_HB, :].astype(F32))
        ext_z[H:H + R, :] = zr_ref[...].astype(F32)
        ext_h[0:H, :] = jnp.where(first_chunk, 0.0, hh_ref[...])
        ext_h[H:H + R, :] = h_ref[...]
        xr = cb_ref[...]
        for k in range(CONV_A):
            xr = xr + cw_ref[k:k + 1, :] * ext_z[pl.ds(H - (CONV_A - 1 - k), R), :]
        gx, ga, sp, a, mult, xrb, wgxb, wgab = _rglru_pre(xr, wgx_ref, bgx_ref, wga_ref, bga_ref, lam_ref)
        gel, dgel = _gelu(zg, with_grad=True)
        dy = dy_ref[...].astype(F32)
        dh = dy * gel
        dzg_ref[...] = (dy * h_ref[...] * dgel).astype(BF16)
        a_s[...] = a
        b_s[...] = a * dh
        _scan_bwd(a_s, b_s, ext_mu, muc, R // SUB)
        lam_t = dh + ext_mu[pl.ds(1, R), :]
        ext_mu[R:R + H, :] = ext_mu[0:H, :]
        da = lam_t * ext_h[pl.ds(H - 1, R), :]
        gxr = gx * xr
        dlog_a = da * a - (lam_t * gxr) * (a * a) / mult
        dgx = lam_t * mult * xr
        dxr = lam_t * mult * gx
        lam_v = lam_ref[...]
        dlam_ref[...] += jnp.sum(dlog_a * ga, axis=0, keepdims=True) * (C_RG * _sigmoid(-lam_v))
        dpa = (dlog_a * (-C_RG * sp)) * ga * (1.0 - ga)
        dpx = dgx * gx * (1.0 - gx)
        dbga_ref[...] += jnp.sum(dpa, axis=0, keepdims=True)
        dbgx_ref[...] += jnp.sum(dpx, axis=0, keepdims=True)
        dpab, dpxb = dpa.astype(BF16), dpx.astype(BF16)
        dwga_ref[0] += lax.dot_general(xrb, dpab, _TN, preferred_element_type=F32)
        dwgx_ref[0] += lax.dot_general(xrb, dpxb, _TN, preferred_element_type=F32)
        dxr = (dxr + lax.dot_general(dpab, wgab, _NT, preferred_element_type=F32)
               + lax.dot_general(dpxb, wgxb, _NT, preferred_element_type=F32))
        dcb_ref[...] += jnp.sum(dxr, axis=0, keepdims=True)
        ext_d[0:R, :] = dxr
        dzr = jnp.zeros((R, HD_A), F32)
        for k in range(CONV_A):
            sh = CONV_A - 1 - k
            dcw_ref[k:k + 1, :] += jnp.sum(dxr * ext_z[pl.ds(H - sh, R), :], axis=0, keepdims=True)
            dzr = dzr + cw_ref[k:k + 1, :] * ext_d[pl.ds(sh, R), :]
        dzr_ref[...] = dzr.astype(BF16)
        ext_d[R:R + H, :] = ext_d[0:H, :]

    vec_o = pl.BlockSpec((1, HD_A), lambda c, j: (0, c))
    mat_o = pl.BlockSpec((1, HD_A, HD_A), lambda c, j: (c, 0, 0))
    return pl.pallas_call(
        body,
        out_shape=[SDS((S, D), BF16), SDS((S, D), BF16), SDS((CONV_A, D), F32), SDS((1, D), F32),
                   SDS((nt, HD_A, HD_A), F32), SDS((1, D), F32), SDS((nt, HD_A, HD_A), F32), SDS((1, D), F32),
                   SDS((1, D), F32)],
        grid=(nt, nch),
        in_specs=[pl.BlockSpec((R, HD_A), rows), pl.BlockSpec((R, HD_A), rows), pl.BlockSpec((R, HD_A), rows_rec),
                  pl.BlockSpec((_HB, HD_A), halo_z), pl.BlockSpec((R, HD_A), rows),
                  pl.BlockSpec((H, HD_A), halo)] + _a_specs(),
        out_specs=[pl.BlockSpec((R, HD_A), rows), pl.BlockSpec((R, HD_A), rows),
                   pl.BlockSpec((CONV_A, HD_A), lambda c, j: (0, c)), vec_o, mat_o, vec_o, mat_o, vec_o, vec_o],
        scratch_shapes=[pltpu.VMEM((H + R, HD_A), F32), pltpu.VMEM((H + R, HD_A), F32), pltpu.VMEM((R + H, HD_A), F32),
                        pltpu.VMEM((R + H, HD_A), F32), pltpu.VMEM((R, HD_A), F32), pltpu.VMEM((R, HD_A), F32),
                        pltpu.VMEM((SUB, HD_A), F32)],
        name="rglru_bwd", compiler_params=_cp(2),
    )(dyab, zp, zp, zp, h, h, conv_w, conv_b, wgx, bgx, wga, bga, lam)


_POOL_H = 16
_POOL_T0 = 2 * D // HD_A
_POOL_Y0 = D // HD_A


def _window_sum(lv, n, lo, rows, g, ahead):
    base = 0 if ahead else SUB
    cur, win = lv[0], None
    for i, s in enumerate((1, 2, 4, 8)):
        val = cur[pl.ds(base, n), :] + cur[pl.ds(base + (s if ahead else -s), n), :]
        sel = val[lo:lo + rows]
        win = sel if win is None else jnp.where(g >= i, sel, win)
        if i < 3:
            lv[i + 1][pl.ds(base, n), :] = val
            cur = lv[i + 1]
    return win


def _pool_width(g):
    return jnp.where(g == 0, 2.0, jnp.where(g == 1, 4.0, jnp.where(g == 2, 8.0, 16.0)))


def _b_fwd(zp, yab, wg, bg, sc):
    S = zp.shape[0]
    R, H = R_SEQ, _POOL_H

    def body(z_ref, wg_ref, bg_ref, sc_ref, yab_in, yb_ref, *lv):
        del yab_in
        g, j = pl.program_id(0), pl.program_id(1)

        @pl.when(j == 0)
        def _():
            for r in lv:
                r[0:SUB, :] = jnp.zeros((SUB, HD_A), F32)
            lv[0][SUB:SUB + H, :] = jnp.zeros((H, HD_A), F32)

        u = z_ref[...].astype(F32)
        lv[0][SUB + H:SUB + H + R, :] = u
        t1 = (j * R + 1 + lax.broadcasted_iota(jnp.int32, (R, HD_A), 0)).astype(F32)
        p = _window_sum(lv, H + R, H, R, g, False) / jnp.minimum(t1, _pool_width(g)) - u
        lin = lax.dot_general(p.astype(BF16), wg_ref[0].astype(BF16), _NN, preferred_element_type=F32) + bg_ref[...]
        yb_ref[...] = (lin * sc_ref[...]).astype(BF16)
        lv[0][SUB:SUB + H, :] = lv[0][SUB + R:SUB + R + H, :]

    vec = pl.BlockSpec((1, HD_A), lambda g, j: (0, g))
    return pl.pallas_call(
        body, out_shape=SDS(yab.shape, yab.dtype), grid=(len(POOL_WINDOWS), S // R),
        in_specs=[pl.BlockSpec((R, HD_A), lambda g, j: (j, _POOL_T0 + g)),
                  pl.BlockSpec((1, HD_A, HD_A), lambda g, j: (g, 0, 0)), vec, vec, pl.BlockSpec(memory_space=pl.ANY)],
        out_specs=pl.BlockSpec((R, HD_A), lambda g, j: (j, _POOL_Y0 + g)),
        scratch_shapes=[pltpu.VMEM((SUB + H + R, HD_A), F32)] * 4, input_output_aliases={4: 0},
        name="pool_fwd", compiler_params=_cp(2),
    )(zp, wg, bg, sc, yab)


def _b_bwd(dyab, zp, wg, bg, sc):
    S = zp.shape[0]
    R, H, nch, ng = R_SEQ, _POOL_H, S // R_SEQ, len(POOL_WINDOWS)

    def body(dy_ref, z_ref, zh_ref, wg_ref, bg_ref, sc_ref, dz_ref, dwg_ref, dbg_ref, dsc_ref, *scratch):
        lu, lq = scratch[:4], scratch[4:]
        g, j = pl.program_id(0), pl.program_id(1)
        jj = nch - 1 - j

        @pl.when(j == 0)
        def _():
            for r in lu:
                r[0:SUB, :] = jnp.zeros((SUB, HD_A), F32)
            for r in lq:
                r[R + H:R + H + SUB, :] = jnp.zeros((SUB, HD_A), F32)
            lq[0][R:R + H, :] = jnp.zeros((H, HD_A), F32)
            for r in (dwg_ref, dbg_ref, dsc_ref):
                r[...] = jnp.zeros_like(r)

        u = z_ref[...].astype(F32)
        lu[0][SUB:SUB + H, :] = jnp.where(jj == 0, 0.0, zh_ref[...].astype(F32))
        lu[0][SUB + H:SUB + H + R, :] = u
        t1 = (jj * R + 1 + lax.broadcasted_iota(jnp.int32, (R, HD_A), 0)).astype(F32)
        cnt = jnp.minimum(t1, _pool_width(g))
        pb = (_window_sum(lu, H + R, H, R, g, False) / cnt - u).astype(BF16)
        wgb = wg_ref[0].astype(BF16)
        lin = lax.dot_general(pb, wgb, _NN, preferred_element_type=F32) + bg_ref[...]
        dy = dy_ref[...].astype(F32)
        dsc_ref[...] += jnp.sum(dy * lin, axis=0, keepdims=True)
        dlin = dy * sc_ref[...]
        dbg_ref[...] += jnp.sum(dlin, axis=0, keepdims=True)
        dlb = dlin.astype(BF16)
        dwg_ref[0] += lax.dot_general(pb, dlb, _TN, preferred_element_type=F32)
        dp = lax.dot_general(dlb, wgb, _NT, preferred_element_type=F32)
        lq[0][0:R, :] = dp / cnt
        dz_ref[...] = (_window_sum(lq, R + H, 0, R, g, True) - dp).astype(BF16)
        lq[0][R:R + H, :] = lq[0][0:H, :]

    vec = pl.BlockSpec((1, HD_A), lambda g, j: (0, g))
    mat = pl.BlockSpec((1, HD_A, HD_A), lambda g, j: (g, 0, 0))
    return pl.pallas_call(
        body, out_shape=[SDS((S, D // 2), BF16), SDS((ng, HD_A, HD_A), F32), SDS((1, D // 2), F32),
                         SDS((1, D // 2), F32)],
        grid=(ng, nch),
        in_specs=[pl.BlockSpec((R, HD_A), lambda g, j: (nch - 1 - j, _POOL_Y0 + g)),
                  pl.BlockSpec((R, HD_A), lambda g, j: (nch - 1 - j, _POOL_T0 + g)),
                  pl.BlockSpec((H, HD_A), lambda g, j: (jnp.maximum((nch - 1 - j) * (R // H) - 1, 0), _POOL_T0 + g)),
                  mat, vec, vec],
        out_specs=[pl.BlockSpec((R, HD_A), lambda g, j: (nch - 1 - j, g)), mat, vec, vec],
        scratch_shapes=[pltpu.VMEM((SUB + H + R, HD_A), F32)] * 8,
        name="pool_bwd", compiler_params=_cp(2),
    )(dyab, zp, zp, wg, bg, sc)


_CW_F = 768


def _f_fwd(hp, w, b, name):
    S = hp.shape[0]
    R, H, cw = R_SEQ, SUB, _CW_F
    nlt = cw // LANE

    def body(h_ref, w_ref, b_ref, o_ref, ext):
        j = pl.program_id(1)

        @pl.when(j == 0)
        def _():
            ext[:, 0:H, :] = jnp.zeros((nlt, H, LANE), F32)

        def stage(r0, lt):
            ext[lt, pl.ds(pl.multiple_of(r0 + H, SUB), _RB), :] = h_ref[pl.ds(r0, _RB), _lanes(lt)].astype(F32)

        def main(r0, lt):
            ls = _lanes(lt)
            gp = b_ref[:, ls]
            for k in range(CONV_F):
                gp = gp + w_ref[k:k + 1, ls] * ext[lt, pl.ds(r0 + (H - (CONV_F - 1 - k)), _RB), :]
            up = h_ref[pl.ds(r0, _RB), _lanes(lt + nlt)].astype(F32)
            o_ref[pl.ds(r0, _RB), ls] = (_gelu(gp) * up).astype(BF16)

        _sub_blocks(R, cw, stage)
        _sub_blocks(R, cw, main)
        ext[:, 0:H, :] = ext[:, R:R + H, :]

    return pl.pallas_call(
        body, out_shape=SDS((S, D_FF), BF16), grid=(D_FF // cw, S // R),
        in_specs=[pl.BlockSpec((R, 2 * cw), lambda c, j: (j, c)), pl.BlockSpec((CONV_F, cw), lambda c, j: (0, c)),
                  pl.BlockSpec((1, cw), lambda c, j: (0, c))],
        out_specs=pl.BlockSpec((R, cw), lambda c, j: (j, c)),
        scratch_shapes=[pltpu.VMEM((nlt, H + R, LANE), F32)], name=name, compiler_params=_cp(2),
    )(hp, w, b)


def _f_bwd(dact, hp, w, b, name):
    S = hp.shape[0]
    R, H, cw, nch = R_SEQ, SUB, _CW_F, S // R_SEQ
    nlt = cw // LANE

    def body(da_ref, h_ref, hh_ref, w_ref, b_ref, dh_ref, dw_ref, db_ref, ext_g, ext_d, acc):
        j = pl.program_id(1)
        jj = nch - 1 - j

        @pl.when(j == 0)
        def _():
            ext_d[:, R:R + H, :] = jnp.zeros((nlt, H, LANE), F32)
            acc[...] = jnp.zeros_like(acc)

        for lt in range(nlt):
            ext_g[lt, 0:H, :] = jnp.where(jj == 0, 0.0, hh_ref[_HB - H:_HB, lt * LANE:(lt + 1) * LANE].astype(F32))

        def stage(r0, lt):
            ext_g[lt, pl.ds(pl.multiple_of(r0 + H, SUB), _RB), :] = h_ref[pl.ds(r0, _RB), _lanes(lt)].astype(F32)

        def first(r0, lt):
            ls, lu, rs = _lanes(lt), _lanes(lt + nlt), pl.ds(r0, _RB)
            taps = [ext_g[lt, pl.ds(r0 + (H - (CONV_F - 1 - k)), _RB), :] for k in range(CONV_F)]
            gp = b_ref[:, ls]
            for k in range(CONV_F):
                gp = gp + w_ref[k:k + 1, ls] * taps[k]
            gel, dgel = _gelu(gp, with_grad=True)
            da = da_ref[rs, ls].astype(F32)
            dh_ref[rs, lu] = (da * gel).astype(BF16)
            dgp = da * h_ref[rs, lu].astype(F32) * dgel
            ext_d[lt, rs, :] = dgp
            acc[CONV_F * SUB:(CONV_F + 1) * SUB, ls] += _psum8(dgp)
            for k in range(CONV_F):
                acc[k * SUB:(k + 1) * SUB, ls] += _psum8(dgp * taps[k])

        def second(r0, lt):
            ls = _lanes(lt)
            dhg = w_ref[CONV_F - 1:CONV_F, ls] * ext_d[lt, pl.ds(r0, _RB), :]
            for k in range(CONV_F - 1):
                dhg = dhg + w_ref[k:k + 1, ls] * ext_d[lt, pl.ds(r0 + (CONV_F - 1 - k), _RB), :]
            dh_ref[pl.ds(r0, _RB), ls] = dhg.astype(BF16)

        _sub_blocks(R, cw, stage)
        _sub_blocks(R, cw, first)
        _sub_blocks(R, cw, second)
        ext_d[:, R:R + H, :] = ext_d[:, 0:H, :]

        @pl.when(j == nch - 1)
        def _():
            for k in range(CONV_F):
                dw_ref[k:k + 1, :] = jnp.sum(acc[k * SUB:(k + 1) * SUB, :], axis=0, keepdims=True)
            db_ref[...] = jnp.sum(acc[CONV_F * SUB:(CONV_F + 1) * SUB, :], axis=0, keepdims=True)

    rows = lambda c, j: (nch - 1 - j, c)
    return pl.pallas_call(
        body, out_shape=[SDS((S, 2 * D_FF), BF16), SDS((CONV_F, D_FF), F32), SDS((1, D_FF), F32)],
        grid=(D_FF // cw, nch),
        in_specs=[pl.BlockSpec((R, cw), rows), pl.BlockSpec((R, 2 * cw), rows),
                  pl.BlockSpec((_HB, 2 * cw), lambda c, j: (jnp.maximum((nch - 1 - j) * (R // _HB) - 1, 0), c)),
                  pl.BlockSpec((CONV_F, cw), lambda c, j: (0, c)), pl.BlockSpec((1, cw), lambda c, j: (0, c))],
        out_specs=[pl.BlockSpec((R, 2 * cw), rows), pl.BlockSpec((CONV_F, cw), lambda c, j: (0, c)),
                   pl.BlockSpec((1, cw), lambda c, j: (0, c))],
        scratch_shapes=[pltpu.VMEM((nlt, H + R, LANE), F32), pltpu.VMEM((nlt, R + H, LANE), F32),
                        pltpu.VMEM(((CONV_F + 1) * SUB, cw), F32)], name=name,
        compiler_params=_cp(2),
    )(dact, hp, hp, w, b)


_CW_C = 256
_H_C = 32


def _c_fwd(h1p, w, b):
    S = h1p.shape[0]
    R, H, cw = R_SEQ, _H_C, _CW_C
    nlt = cw // LANE

    def body(h_ref, w_ref, b_ref, o_ref, ext):
        j = pl.program_id(1)

        @pl.when(j == 0)
        def _():
            ext[:, 0:H, :] = jnp.zeros((nlt, H, LANE), F32)

        def stage(r0, lt):
            rs = pl.ds(r0, _RB)
            gate = h_ref[rs, _lanes(lt + nlt)].astype(F32)
            ext[lt, pl.ds(pl.multiple_of(r0 + H, SUB), _RB), :] = h_ref[rs, _lanes(lt)].astype(F32) * _sigmoid(gate)

        def main(r0, lt):
            ls = _lanes(lt)
            cv = b_ref[:, ls]
            for k in range(CONV_C):
                cv = cv + w_ref[k:k + 1, ls] * ext[lt, pl.ds(r0 + (H - (CONV_C - 1 - k)), _RB), :]
            o_ref[pl.ds(r0, _RB), ls] = cv

        _sub_blocks(R, cw, stage)
        _sub_blocks(R, cw, main)
        ext[:, 0:H, :] = ext[:, R:R + H, :]

    return pl.pallas_call(
        body, out_shape=SDS((S, D), F32), grid=(D // cw, S // R),
        in_specs=[pl.BlockSpec((R, 2 * cw), lambda c, j: (j, c)), pl.BlockSpec((CONV_C, cw), lambda c, j: (0, c)),
                  pl.BlockSpec((1, cw), lambda c, j: (0, c))],
        out_specs=pl.BlockSpec((R, cw), lambda c, j: (j, c)),
        scratch_shapes=[pltpu.VMEM((nlt, H + R, LANE), F32)], name="conf_conv_fwd", compiler_params=_cp(2),
    )(h1p, w, b)


def _c_bwd(dcv, h1p, w):
    S = h1p.shape[0]
    R, H, cw, nch = R_SEQ, _H_C, _CW_C, S // R_SEQ
    nlt = cw // LANE
    a_b, a_val, a_gate = CONV_C * SUB, (CONV_C + 1) * SUB, (CONV_C + 2) * SUB

    def body(dc_ref, h_ref, hh_ref, w_ref, dh_ref, dw_ref, db_ref, db1_ref, ext_u, ext_d, acc):
        j = pl.program_id(1)
        jj = nch - 1 - j

        @pl.when(j == 0)
        def _():
            ext_d[:, R:R + H, :] = jnp.zeros((nlt, H, LANE), F32)
            acc[...] = jnp.zeros_like(acc)

        for lt in range(nlt):
            ext_u[lt, 0:H, :] = jnp.where(
                jj == 0, 0.0, hh_ref[:, lt * LANE:(lt + 1) * LANE].astype(F32)
                * _sigmoid(hh_ref[:, cw + lt * LANE:cw + (lt + 1) * LANE].astype(F32)))

        def stage(r0, lt):
            rs, ls = pl.ds(r0, _RB), _lanes(lt)
            gate = h_ref[rs, _lanes(lt + nlt)].astype(F32)
            ext_u[lt, pl.ds(pl.multiple_of(r0 + H, SUB), _RB), :] = h_ref[rs, ls].astype(F32) * _sigmoid(gate)
            ext_d[lt, rs, :] = dc_ref[rs, ls]

        def first(r0, lt):
            ls = _lanes(lt)
            dc = dc_ref[pl.ds(r0, _RB), ls]
            acc[a_b:a_b + SUB, ls] += _psum8(dc)
            for k in range(CONV_C):
                tap = ext_u[lt, pl.ds(r0 + (H - (CONV_C - 1 - k)), _RB), :]
                acc[k * SUB:(k + 1) * SUB, ls] += _psum8(dc * tap)

        def second(r0, lt):
            rs, ls, lg = pl.ds(r0, _RB), _lanes(lt), _lanes(lt + nlt)
            du = w_ref[CONV_C - 1:CONV_C, ls] * ext_d[lt, rs, :]
            for k in range(CONV_C - 1):
                du = du + w_ref[k:k + 1, ls] * ext_d[lt, pl.ds(r0 + (CONV_C - 1 - k), _RB), :]
            val = h_ref[rs, ls].astype(F32)
            sg = _sigmoid(h_ref[rs, lg].astype(F32))
            dval = du * sg
            dgate = du * val * sg * (1.0 - sg)
            acc[a_val:a_val + SUB, ls] += _psum8(dval)
            acc[a_gate:a_gate + SUB, ls] += _psum8(dgate)
            dh_ref[rs, ls] = dval.astype(BF16)
            dh_ref[rs, lg] = dgate.astype(BF16)

        _sub_blocks(R, cw, stage)
        _sub_blocks(R, cw, first)
        _sub_blocks(R, cw, second)
        ext_d[:, R:R + H, :] = ext_d[:, 0:H, :]

        @pl.when(j == nch - 1)
        def _():
            for k in range(CONV_C):
                dw_ref[k:k + 1, :] = jnp.sum(acc[k * SUB:(k + 1) * SUB, :], axis=0, keepdims=True)
            db_ref[...] = jnp.sum(acc[a_b:a_b + SUB, :], axis=0, keepdims=True)
            db1_ref[:, 0:cw] = jnp.sum(acc[a_val:a_val + SUB, :], axis=0, keepdims=True)
            db1_ref[:, cw:2 * cw] = jnp.sum(acc[a_gate:a_gate + SUB, :], axis=0, keepdims=True)

    rows = lambda c, j: (nch - 1 - j, c)
    return pl.pallas_call(
        body, out_shape=[SDS((S, 2 * D), BF16), SDS((CONV_C, D), F32), SDS((1, D), F32), SDS((1, 2 * D), F32)],
        grid=(D // cw, nch),
        in_specs=[pl.BlockSpec((R, cw), rows), pl.BlockSpec((R, 2 * cw), rows),
                  pl.BlockSpec((H, 2 * cw), lambda c, j: (jnp.maximum((nch - 1 - j) * (R // H) - 1, 0), c)),
                  pl.BlockSpec((CONV_C, cw), lambda c, j: (0, c))],
        out_specs=[pl.BlockSpec((R, 2 * cw), rows), pl.BlockSpec((CONV_C, cw), lambda c, j: (0, c)),
                   pl.BlockSpec((1, cw), lambda c, j: (0, c)), pl.BlockSpec((1, 2 * cw), lambda c, j: (0, c))],
        scratch_shapes=[pltpu.VMEM((nlt, H + R, LANE), F32), pltpu.VMEM((nlt, R + H, LANE), F32),
                        pltpu.VMEM(((CONV_C + 3) * SUB, cw), F32)], name="conf_conv_bwd",
        compiler_params=_cp(2),
    )(dcv, h1p, h1p, w)


def _local_step(x, mem, tgt, W, fetch=None, send=None):
    G = {}
    W = dict(W)

    def arrive(group, after):
        if fetch is None:
            return None
        got, tok = fetch(group, after)
        for key, val in got.items():
            W[key] = {**W.get(key, {}), **val} if isinstance(val, dict) else val
        return tok

    def gain(g, tok):
        return g if tok is None else g + tok

    def sent(group):
        return None if send is None else send(group, G)

    def xattn_fwd(xin, n, l):
        tok = arrive(("xa", l), n)
        mn = _rms_fwd(mem, gain(W["xa_mem_norm"][l:l + 1], tok), f"xa_memnorm_fwd{l}")
        q = _mm_nn(n, W["xa_wq"][l], out_dtype=BF16, name=f"xa_q{l}")
        k = _mm_nn(mn, W["xa_wk"][l], out_dtype=BF16, name=f"xa_k{l}")
        v = _mm_nn(mn, W["xa_wv"][l], out_dtype=BF16, name=f"xa_v{l}")
        o = _attn_fwd(q, k, v, f"xa_attn_fwd{l}")
        xout, nout = _mm_nn(o, W["xa_wo"][l], out_dtype=F32, name=f"xa_o{l}", add=xin, norm=W["f_norm"][l:l + 1])
        return xout, nout, (xin, n, q, mn, k, v, o)

    def xattn_bwd(dx, dxb, saved, l):
        xin, n, q, mn, k, v, o = saved
        do = _mm_nt(dxb, W["xa_wo"][l], out_dtype=BF16, name=f"xa_do{l}")
        G[f"xa_wo{l}"] = _mm_tn(o, dxb, out_dtype=BF16, name=f"xa_dwo{l}")
        dq, dk, dv = _attn_bwd(q, k, v, do, f"xa_attn_bwd{l}")
        dkb, dvb = dk.astype(BF16), dv.astype(BF16)
        G[f"xa_wq{l}"] = _mm_tn(n, dq, out_dtype=BF16, name=f"xa_dwq{l}")
        G[f"xa_wk{l}"] = _mm_tn(mn, dkb, out_dtype=BF16, name=f"xa_dwk{l}")
        G[f"xa_wv{l}"] = _mm_tn(mn, dvb, out_dtype=BF16, name=f"xa_dwv{l}")
        tok = sent(("xa", l))
        dmn = _mm_nt(dkb, W["xa_wk"][l], out_dtype=F32, name=f"xa_dmn_k{l}")
        dmn = _mm_nt(dvb, W["xa_wv"][l], out_dtype=F32, name=f"xa_dmn_v{l}", add=dmn)
        (G[f"xa_mem_norm{l}"],) = _rms_bwd(mem, W["xa_mem_norm"][l:l + 1], dmn, None, f"xa_memnorm_bwd{l}")
        dx, dxb, G[f"xa_norm{l}"] = _mm_nt(dq, W["xa_wq"][l], out_dtype=F32, name=f"xa_dn{l}",
                                           rms=(xin, gain(W["xa_norm"][l:l + 1], tok), dx))
        return dx, dxb

    def ffn_fwd(xin, n, l, next_gain):
        tok = arrive(("f", l), n)
        hp = _mm_nn(n, W["f_w_up"][l], out_dtype=BF16, name=f"f_up{l}")
        act = _f_fwd(hp, W["f_dw_w"][l], gain(W["f_dw_b"][l:l + 1], tok), f"f_conv_fwd{l}")
        res = _mm_nn(act, W["f_w_down"][l], out_dtype=F32, name=f"f_down{l}", add=xin, norm=next_gain)
        xout, nout = res if next_gain is not None else (res, None)
        return xout, nout, (xin, n, hp, act)

    def ffn_bwd(dx, dxb, saved, l):
        xin, n, hp, act = saved
        dact = _mm_nt(dxb, W["f_w_down"][l], out_dtype=BF16, name=f"f_dact{l}")
        G[f"f_w_down{l}"] = _mm_tn(act, dxb, out_dtype=BF16, name=f"f_dwdown{l}")
        dhp, G[f"f_dw_w{l}"], G[f"f_dw_b{l}"] = _f_bwd(dact, hp, W["f_dw_w"][l], W["f_dw_b"][l:l + 1], f"f_conv_bwd{l}")
        G[f"f_w_up{l}"] = _mm_tn(n, dhp, out_dtype=BF16, name=f"f_dwup{l}", blocks=_CW_F)
        tok = sent(("f", l))
        dx, dxb, G[f"f_norm{l}"] = _mm_nt(dhp, W["f_w_up"][l], out_dtype=F32, name=f"f_dn{l}",
                                          rms=(xin, gain(W["f_norm"][l:l + 1], tok), dx))
        return dx, dxb

    n0 = _rms_fwd(x, W["ab_norm"], "ab_norm_fwd")
    tok = arrive(("ab", 0), n0)
    a_par = (W["a_conv_w"], gain(W["a_conv_b"], tok), W["a_gate_x_w"], W["a_gate_x_b"], W["a_gate_a_w"],
             W["a_gate_a_b"], W["a_lambda"])
    b_par = (W["b_group_w"], W["b_group_b"], W["b_scale"])
    zp = _mm_nn(n0, W["ab_w_in"], out_dtype=BF16, name="ab_in")
    yab, h_a = _a_fwd(zp, *a_par)
    yab = _b_fwd(zp, yab, *b_par)
    arrive(("ab", 1), yab)
    x1, n1 = _mm_nn(yab, W["ab_w_out"], out_dtype=F32, name="ab_out", add=x, norm=W["xa_norm"][0:1])
    x2, n2, s_xa0 = xattn_fwd(x1, n1, 0)
    x3, n3, s_f0 = ffn_fwd(x2, n2, 0, W["c_norm"])
    tok = arrive(("c", 0), n3)
    h1p = _mm_nn(n3, W["c_w_pw1"], out_dtype=BF16, name="c_pw1", bias=gain(W["c_b_pw1"], tok))
    cv = _c_fwd(h1p, W["c_dw_w"], W["c_dw_b"])
    sc = _ln_silu_fwd(cv, W["c_ln_g"], W["c_ln_b"])
    x4, n4 = _mm_nn(sc, W["c_w_pw2"], out_dtype=F32, name="c_pw2", bias=W["c_b_pw2"], add=x3, norm=W["xa_norm"][1:2])
    x5, n5, s_xa1 = xattn_fwd(x4, n4, 1)
    x6, _, s_f1 = ffn_fwd(x5, n5, 1, None)
    loss, dx, dxb, G["final_norm"] = _loss_head(x6, W["final_norm"], tgt)

    dx, dxb = ffn_bwd(dx, dxb, s_f1, 1)
    dx, dxb = xattn_bwd(dx, dxb, s_xa1, 1)
    dsc = _mm_nt(dxb, W["c_w_pw2"], out_dtype=BF16, name="c_dsc")
    G["c_w_pw2"] = _mm_tn(sc, dxb, out_dtype=BF16, name="c_dwpw2")
    dcv, G["c_ln_g"], G["c_ln_b"], G["c_b_pw2"] = _ln_silu_bwd(dsc, cv, W["c_ln_g"], W["c_ln_b"], dx)
    dh1p, G["c_dw_w"], G["c_dw_b"], G["c_b_pw1"] = _c_bwd(dcv, h1p, W["c_dw_w"])
    G["c_w_pw1"] = _mm_tn(n3, dh1p, out_dtype=BF16, name="c_dwpw1", blocks=_CW_C)
    tok = sent(("c", 0))
    dx, dxb, G["c_norm"] = _mm_nt(dh1p, W["c_w_pw1"], out_dtype=F32, name="c_dn",
                                  rms=(x3, gain(W["c_norm"], tok), dx))
    dx, dxb = ffn_bwd(dx, dxb, s_f0, 0)
    dx, dxb = xattn_bwd(dx, dxb, s_xa0, 0)
    dyab = _mm_nt(dxb, W["ab_w_out"], out_dtype=BF16, name="ab_dyab")
    G["ab_w_out"] = _mm_tn(yab, dxb, out_dtype=BF16, name="ab_dwout")
    tok = sent(("ab", 1))
    a_par = (a_par[0], gain(a_par[1], tok)) + a_par[2:]
    (dzg, dzr, G["a_conv_w"], G["a_conv_b"], G["a_gate_x_w"], G["a_gate_x_b"], G["a_gate_a_w"], G["a_gate_a_b"],
     G["a_lambda"]) = _a_bwd(dyab, zp, h_a, *a_par)
    dzq, G["b_group_w"], G["b_group_b"], G["b_scale"] = _b_bwd(dyab, zp, *b_par)
    G["ab_w_in"] = jnp.concatenate(
        [_mm_tn(n0, dz, out_dtype=BF16, name=f"ab_dwin_{part}")
         for part, dz in (("gate", dzg), ("rec", dzr), ("pool", dzq))], axis=1)
    tok = sent(("ab", 0))
    dx, _, G["ab_norm"] = _mm_nt_cols([dzg, dzr, dzq], W["ab_w_in"], name="ab_dn",
                                      rms=(x, gain(W["ab_norm"], tok), dx))
    return loss, dx, G


def _my_place():
    x, y, c = lax.axis_index("x"), lax.axis_index("y"), lax.axis_index("c")
    return x, y, c


def _all_gather(shards, name):
    n = len(shards)

    def body(*refs):
        ins, outs = refs[:n], refs[n:2 * n]
        send_sems, recv_sems, local_sems = refs[2 * n:]
        x, y, c = _my_place()
        me, sibling = (x, y, c), (x, y, 1 - c)
        chips = [(1 - x, y), (x, 1 - y), (1 - x, 1 - y)]

        def slab(a, place):
            px, py, pc = place
            return outs[a].at[4 * px + 2 * py + pc]

        def copy(a, k, block, to, src=None):
            return pltpu.make_async_remote_copy(
                src_ref=slab(a, block) if src is None else src, dst_ref=slab(a, block),
                send_sem=send_sems.at[a, k], recv_sem=recv_sems.at[a, k], device_id=to, device_id_type=MESH)

        mine = [pltpu.make_async_copy(ins[a], slab(a, me), local_sems.at[a]) for a in range(n)]
        for cp in mine:
            cp.start()
        first = []
        for j, chip in enumerate(chips):
            first += [copy(a, 1 + j, me, (*chip, c), src=ins[a]) for a in range(n)]
        first += [copy(a, 0, me, sibling, src=ins[a]) for a in range(n)]
        for cp in first:
            cp.start()
        passed = []
        for j, chip in enumerate(chips):
            for a in range(n):
                copy(a, 1 + j, (*chip, c), me).wait_recv()
                cp = copy(a, 4 + j, (*chip, c), sibling)
                cp.start()
                passed.append(cp)
        for a in range(n):
            copy(a, 0, sibling, me).wait_recv()
        for j, chip in enumerate(chips):
            for a in range(n):
                copy(a, 4 + j, (*chip, 1 - c), me).wait_recv()
        for cp in first + passed:
            cp.wait_send()
        for cp in mine:
            cp.wait()

    any_spec = pl.BlockSpec(memory_space=pl.ANY)
    return pl.pallas_call(
        body, out_shape=[SDS((N_DEV,) + s.shape, s.dtype) for s in shards], in_specs=[any_spec] * n,
        out_specs=[any_spec] * n,
        scratch_shapes=[pltpu.SemaphoreType.DMA((n, 7)), pltpu.SemaphoreType.DMA((n, 7)), pltpu.SemaphoreType.DMA((n,))],
        name=name,
    )(*shards)


_HBM = pl.BlockSpec(memory_space=pltpu.HBM)
_SEM = pl.BlockSpec(memory_space=pltpu.SEMAPHORE)
_EFFECT = pltpu.SideEffectType.DATAFLOW_SIDE_EFFECTING


def _peer_places():
    x, y, c = _my_place()
    peers = []
    for k in range(1, N_DEV):
        px = 1 - x if (k >> 2) & 1 else x
        py = 1 - y if (k >> 1) & 1 else y
        pc = 1 - c if k & 1 else c
        peers.append(((px, py, pc), 4 * px + 2 * py + pc))
    return (x, y, c), 4 * x + 2 * y + c, peers


def _send_start(srcs, per_dest, name):
    n = len(srcs)
    lands = [lax.empty((N_DEV,) + (s.shape[1:] if per_dest else s.shape), s.dtype) for s in srcs]

    def body(*refs):
        src, land = refs[:n], refs[n:2 * n]
        outs = refs[2 * n:]
        send, recv, token = outs[:n], outs[n:2 * n], outs[4 * n]
        _, me, peers = _peer_places()
        for a in range(n):
            for peer, pidx in peers:
                pltpu.make_async_remote_copy(
                    src_ref=src[a].at[pidx] if per_dest else src[a], dst_ref=land[a].at[me], send_sem=send[a],
                    recv_sem=recv[a], device_id=peer, device_id_type=MESH).start()
        token[...] = jnp.zeros_like(token)

    hbm = lambda a: pltpu.HBM(a.shape, a.dtype)
    sem = pltpu.SemaphoreType.DMA(())
    res = pl.pallas_call(
        body, name=name,
        out_shape=tuple([sem] * (2 * n) + [hbm(s) for s in srcs] + [hbm(l) for l in lands]
                        + [SDS((SUB, LANE), F32)]),
        in_specs=[_HBM] * (2 * n),
        out_specs=tuple([_SEM] * (2 * n) + [_HBM] * (2 * n) + [pl.BlockSpec(memory_space=pltpu.VMEM)]),
        input_output_aliases={i: 2 * n + i for i in range(2 * n)},
        compiler_params=pltpu.CompilerParams(has_side_effects=_EFFECT),
    )(*[pltpu.with_memory_space_constraint(s, pltpu.HBM) for s in srcs],
      *[pltpu.with_memory_space_constraint(l, pltpu.HBM) for l in lands])
    return res[:n], res[n:2 * n], res[2 * n:3 * n], res[3 * n:4 * n], res[4 * n]


def _send_wait(send, recv, srcs, lands, after, per_dest, name):
    n = len(srcs)

    def body(*refs):
        src, land = refs[:n], refs[n:2 * n]
        send_s, recv_s = refs[2 * n:3 * n], refs[3 * n:4 * n]
        token = refs[-1]
        place, _, _ = _peer_places()
        for a in range(n):
            seven = land[a].at[pl.ds(0, N_DEV - 1)]
            copy = pltpu.make_async_remote_copy(
                src_ref=src[a].at[pl.ds(0, N_DEV - 1)] if per_dest else seven, dst_ref=seven, send_sem=send_s[a],
                recv_sem=recv_s[a], device_id=place, device_id_type=MESH)
            copy.wait_send()
            copy.wait_recv()
        token[...] = jnp.zeros_like(token)

    hbm = lambda a: pltpu.HBM(a.shape, a.dtype)
    res = pl.pallas_call(
        body, name=name,
        out_shape=tuple([hbm(s) for s in srcs] + [hbm(l) for l in lands] + [SDS((SUB, LANE), F32)]),
        in_specs=[_HBM] * (2 * n) + [_SEM] * (2 * n) + [pl.BlockSpec(memory_space=pl.ANY)],
        out_specs=tuple([_HBM] * (2 * n) + [pl.BlockSpec(memory_space=pltpu.VMEM)]),
        input_output_aliases={i: i for i in range(2 * n)},
        compiler_params=pltpu.CompilerParams(has_side_effects=_EFFECT),
    )(*srcs, *lands, *send, *recv, after)
    return res[:n], res[n:2 * n], res[2 * n]


def _adamw_math(w, g, m, v):
    m = ADAM_B1 * m + (1.0 - ADAM_B1) * g
    v = ADAM_B2 * v + (1.0 - ADAM_B2) * (g * g)
    m_hat = m / (1.0 - ADAM_B1 ** ADAM_STEP)
    v_hat = v / (1.0 - ADAM_B2 ** ADAM_STEP)
    delta = -ADAM_LR * (m_hat / (jnp.sqrt(v_hat) + ADAM_EPS) + ADAM_WD * w)
    return delta, m, v


def _row_tile(r, c, itemsize_rows):
    cap = max(SUB, (itemsize_rows // (4 * c)) // SUB * SUB)
    if r <= cap:
        return r
    best = None
    for t in range(SUB, cap + 1, SUB):
        if r % t == 0:
            best = t
    return best if best is not None else r


def _sum_adamw(landing, w, m, v, name, layer=0, prev=None):
    _, r, c = landing.shape
    tr = _row_tile(r, c, 1 << 20)
    off = layer * (r // tr)

    def body(l_ref, w_ref, m_ref, v_ref, *rest):
        g_ref, d_ref, mo_ref, vo_ref = rest[-4:]
        g = l_ref[0].astype(F32)
        for s in range(1, N_DEV):
            g = g + l_ref[s].astype(F32)
        g_ref[...] = g
        d_ref[...], mo_ref[...], vo_ref[...] = _adamw_math(w_ref[...], g, m_ref[...], v_ref[...])

    blk = pl.BlockSpec((tr, c), lambda i: (i + off, 0))
    n_prev = 0 if prev is None else 4
    return pl.pallas_call(
        body, out_shape=[SDS(w.shape, F32)] * 4, grid=(r // tr,),
        in_specs=[pl.BlockSpec((N_DEV, tr, c), lambda i: (0, i, 0)), blk, blk, blk]
        + [pl.BlockSpec(memory_space=pl.ANY)] * n_prev,
        out_specs=[blk] * 4, input_output_aliases={4 + i: i for i in range(n_prev)}, name=name,
        compiler_params=_cp(1),
    )(landing, w, m, v, *([] if prev is None else prev))


def _sum8(landing, name):
    _, r, c = landing.shape

    def body(l_ref, g_ref):
        g = l_ref[0]
        for s in range(1, N_DEV):
            g = g + l_ref[s]
        g_ref[...] = g

    return pl.pallas_call(body, out_shape=SDS((r, c), F32), name=name, compiler_params=_cp(0))(landing)


def _adamw(g, w, m, v, name):
    r, c = g.shape
    tr = _row_tile(r, c, 1 << 20)

    def body(g_ref, w_ref, m_ref, v_ref, d_ref, mo_ref, vo_ref):
        d_ref[...], mo_ref[...], vo_ref[...] = _adamw_math(w_ref[...], g_ref[...], m_ref[...], v_ref[...])

    blk = pl.BlockSpec((tr, c), lambda i: (i, 0))
    return pl.pallas_call(body, out_shape=[SDS((r, c), F32)] * 3, grid=(r // tr,), in_specs=[blk] * 4,
                          out_specs=[blk] * 3, name=name, compiler_params=_cp(1))(g, w, m, v)


_BIG = {
    "ab_w_in": (1, D, 320), "ab_w_out": (1, 192, D), "c_w_pw1": (1, D, 256), "c_w_pw2": (1, 128, D),
    "xa_wq": (2, 128, D), "xa_wk": (2, 128, D), "xa_wv": (2, 128, D), "xa_wo": (2, 128, D),
    "f_w_up": (2, D, 768), "f_w_down": (2, 384, D),
}
_SMALL_SHARDED = {
    "a_conv_w": (1, 4, 128), "c_norm": (1, 128), "c_b_pw1": (1, 256), "c_dw_w": (1, 31, 128), "c_dw_b": (1, 128),
    "c_ln_g": (1, 128), "c_ln_b": (1, 128), "c_b_pw2": (1, 128), "f_dw_w": (2, 3, 384),
}
_REPL = {
    "ab_norm": (1, D), "a_conv_b": (1, D), "a_gate_x_w": (1, 8, 128, 128), "a_gate_x_b": (1, D),
    "a_gate_a_w": (1, 8, 128, 128), "a_gate_a_b": (1, D), "a_lambda": (1, D), "b_group_w": (1, 4, 128, 128),
    "b_group_b": (1, 512), "b_scale": (1, 512), "xa_norm": (2, D), "xa_mem_norm": (2, D), "f_norm": (2, D),
    "f_dw_b": (2, D_FF), "final_norm": (D,),
}


def _size(shape):
    n = 1
    for s in shape:
        n *= s
    return n


_N_SS = sum(_size(s) for s in _SMALL_SHARDED.values())
_N_REPL = sum(_size(s) for s in _REPL.values())
_REPL_ROWS = -(-_N_REPL // (N_DEV * SUB * LANE)) * SUB
_SS_ROWS = _N_SS // LANE
_SMALL_ROWS = -(-(_REPL_ROWS + _SS_ROWS) // SUB) * SUB


def _pack(parts, rows):
    flat = jnp.concatenate([p.reshape(-1).astype(F32) for p in parts])
    return jnp.pad(flat, (0, rows * LANE - flat.shape[0])).reshape(rows, LANE)


def _unpack(buf, table):
    flat, out, off = buf.reshape(-1), {}, 0
    for name, shape in table.items():
        n = _size(shape)
        out[name] = flat[off:off + n].reshape(shape)
        off += n
    return out


def _pair_blocks(v, bw):
    lead, n = v.shape[:-1], v.shape[-1]
    return jnp.swapaxes(v.reshape(lead + (2, n // (2 * bw), bw)), -3, -2).reshape(lead + (n,))


def _unpair_blocks(v, bw):
    lead, n = v.shape[:-1], v.shape[-1]
    return jnp.swapaxes(v.reshape(lead + (n // (2 * bw), 2, bw)), -3, -2).reshape(lead + (n,))


_GROUPS = {
    ("ab", 0): (("ab_w_in", 0),),
    ("ab", 1): (("ab_w_out", 0),),
    ("xa", 0): (("xa_wq", 0), ("xa_wk", 0), ("xa_wv", 0), ("xa_wo", 0)),
    ("f", 0): (("f_w_up", 0), ("f_w_down", 0)),
    ("c", 0): (("c_w_pw1", 0), ("c_w_pw2", 0)),
    ("xa", 1): (("xa_wq", 1), ("xa_wk", 1), ("xa_wv", 1), ("xa_wo", 1)),
    ("f", 1): (("f_w_up", 1), ("f_w_down", 1)),
}
_SEND_GROUPS = _GROUPS


def _weight_layout(name, g):
    if name == "ab_w_in":
        return jnp.swapaxes(g, 0, 1).reshape(D, N_DEV * 320)
    if name in ("c_w_pw1", "f_w_up"):
        return g
    return g.reshape(N_DEV * g.shape[1], D)


def _grad_blocks(name, l, G):
    _, r, c = _BIG[name]
    if name == "ab_w_in":
        return jnp.swapaxes(G[name].reshape(D, N_DEV, 320), 0, 1)
    if name == "c_w_pw1":
        return G[name]
    if name == "f_w_up":
        return G[f"{name}{l}"]
    return (G[name] if _BIG[name][0] == 1 else G[f"{name}{l}"]).reshape(N_DEV, r, c)


def _small_layouts(sm):
    W = {}
    sm = sm.reshape(N_DEV, -1)
    off = 0
    for name, shape in _SMALL_SHARDED.items():
        n = _size(shape)
        blocks = sm[:, off:off + n].reshape((N_DEV,) + shape)
        off += n
        W[name] = jnp.moveaxis(blocks, 0, -2).reshape(shape[:-1] + (N_DEV * shape[-1],))
    W["a_conv_w"], W["c_dw_w"] = W["a_conv_w"][0], W["c_dw_w"][0]
    W["c_b_pw1"] = _pair_blocks(W["c_b_pw1"], _CW_C)
    return W


def _with_own(land, src, me, per_dest):
    own = lax.dynamic_slice_in_dim(src, me, 1, 0) if per_dest else src[None]
    return lax.dynamic_update_slice_in_dim(land, own, me, 0)


def _to_dest_major(g, shape):
    full = g.reshape(shape[:-1] + (N_DEV, shape[-1]))
    return jnp.moveaxis(full, -2, 0).reshape(N_DEV, -1)


def kernel(x, mem, ab_norm, ab_w_in, a_conv_w, a_conv_b, a_gate_x_w, a_gate_x_b, a_gate_a_w, a_gate_a_b, a_lambda, b_group_w, b_group_b, b_scale, ab_w_out, c_norm, c_w_pw1, c_b_pw1, c_dw_w, c_dw_b, c_ln_g, c_ln_b, c_w_pw2, c_b_pw2, xa_norm, xa_mem_norm, xa_wq, xa_wk, xa_wv, xa_wo, f_norm, f_w_up, f_dw_w, f_dw_b, f_w_down, final_norm, loss_target, m_ab_norm, m_ab_w_in, m_a_conv_w, m_a_conv_b, m_a_gate_x_w, m_a_gate_x_b, m_a_gate_a_w, m_a_gate_a_b, m_a_lambda, m_b_group_w, m_b_group_b, m_b_scale, m_ab_w_out, m_c_norm, m_c_w_pw1, m_c_b_pw1, m_c_dw_w, m_c_dw_b, m_c_ln_g, m_c_ln_b, m_c_w_pw2, m_c_b_pw2, m_xa_norm, m_xa_mem_norm, m_xa_wq, m_xa_wk, m_xa_wv, m_xa_wo, m_f_norm, m_f_w_up, m_f_dw_w, m_f_dw_b, m_f_w_down, m_final_norm, v_ab_norm, v_ab_w_in, v_a_conv_w, v_a_conv_b, v_a_gate_x_w, v_a_gate_x_b, v_a_gate_a_w, v_a_gate_a_b, v_a_lambda, v_b_group_w, v_b_group_b, v_b_scale, v_ab_w_out, v_c_norm, v_c_w_pw1, v_c_b_pw1, v_c_dw_w, v_c_dw_b, v_c_ln_g, v_c_ln_b, v_c_w_pw2, v_c_b_pw2, v_xa_norm, v_xa_mem_norm, v_xa_wq, v_xa_wk, v_xa_wv, v_xa_wo, v_f_norm, v_f_w_up, v_f_dw_w, v_f_dw_b, v_f_w_down, v_final_norm):
    args = dict(locals())
    P = {n: args[n] for n in _NAMES}
    M = {n: args["m_" + n] for n in _NAMES}
    V = {n: args["v_" + n] for n in _NAMES}

    me = 4 * lax.axis_index("x") + 2 * lax.axis_index("y") + lax.axis_index("c")

    in_flight = {}

    def launch(groups, tok):
        shards, n_of = [], {}
        for grp in groups:
            for name, l in _GROUPS[grp]:
                w = P[name][l] if tok is None else P[name][l] + tok
                shards.append(w.astype(BF16))
            if grp == ("ab", 0):
                shards.append(_pack([P[n] for n in _SMALL_SHARDED], _SS_ROWS + 4))
            n_of[grp] = len(shards)
        res = _send_start(shards, False, "gather_start_" + "_".join(g[0] + str(g[1]) for g in groups))
        lo = 0
        for grp in groups:
            in_flight[grp] = [r[lo:n_of[grp]] for r in res[:4]]
            lo = n_of[grp]
        return res[4][:1, :1]

    follow = {("ab", 0): [("ab", 1), ("xa", 0), ("f", 0)], ("xa", 0): [("c", 0)], ("f", 0): [("xa", 1)],
              ("c", 0): [("f", 1)]}

    def fetch(grp, after):
        send_s, recv_s, srcs, lands = in_flight.pop(grp)
        srcs, lands, tok = _send_wait(send_s, recv_s, srcs, lands, after, False, f"gather_wait_{grp[0]}{grp[1]}")
        tok = launch(follow[grp], tok[:1, :1]) if grp in follow else None
        full = [_with_own(land, src, me, False) for land, src in zip(lands, srcs)]
        out = {}
        for (name, l), g in zip(_GROUPS[grp], full):
            w = _weight_layout(name, g)
            if _BIG[name][0] == 1:
                out[name] = w
            else:
                out[name] = {l: w}
        if grp == ("ab", 0):
            out.update(_small_layouts(full[-1]))
        return out, tok

    zero = launch([("ab", 0)], None)

    pending = []

    def send(grp, G):
        members = _SEND_GROUPS[grp]
        res = _send_start([_grad_blocks(name, l, G) for name, l in members], True, f"send_{grp[0]}{grp[1]}")
        pending.append((members, res))
        return res[4][:1, :1]

    W = {n: P[n] for n in _REPL}
    W["ab_norm"] = P["ab_norm"] + zero
    W["final_norm"] = P["final_norm"].reshape(1, D)
    W["a_gate_x_w"], W["a_gate_a_w"], W["b_group_w"] = P["a_gate_x_w"][0], P["a_gate_a_w"][0], P["b_group_w"][0]
    loss, grad_x, G = _local_step(x[0], mem[0], loss_target[0], W, fetch, send)
    loss = lax.psum(loss[0, 0], ("x", "y", "c"))

    Gs = dict(G)
    Gs["c_b_pw1"] = _unpair_blocks(G["c_b_pw1"], _CW_C)
    Gs["f_dw_w"] = jnp.stack([G["f_dw_w0"], G["f_dw_w1"]])
    Gs["a_conv_w"], Gs["c_dw_w"] = G["a_conv_w"][None], G["c_dw_w"][None]
    for n in ("xa_norm", "xa_mem_norm", "f_norm", "f_dw_b"):
        Gs[n] = jnp.concatenate([G[f"{n}0"], G[f"{n}1"]], axis=0)
    for n in ("a_gate_x_w", "a_gate_a_w", "b_group_w"):
        Gs[n] = G[n][None]
    repl_flat = jnp.concatenate([Gs[n].reshape(-1) for n in _REPL])
    repl_rows = jnp.pad(repl_flat, (0, N_DEV * _REPL_ROWS * LANE - _N_REPL)).reshape(N_DEV, _REPL_ROWS, LANE)
    ss_rows = jnp.concatenate([_to_dest_major(Gs[n], s) for n, s in _SMALL_SHARDED.items()], axis=1)
    ss_rows = ss_rows.reshape(N_DEV, _SS_ROWS, LANE)
    small_pack = jnp.concatenate(
        [repl_rows, ss_rows, jnp.zeros((N_DEV, _SMALL_ROWS - _REPL_ROWS - _SS_ROWS, LANE), F32)], axis=1)
    last = _send_start([small_pack], True, "send_small")
    pending.append(((("small", 0),), last))

    members = [m for mem_, _ in pending for m in mem_]
    cat = [[a for _, res in pending for a in res[i]] for i in range(4)]
    srcs, lands, _ = _send_wait(cat[0], cat[1], cat[2], cat[3], grad_x, True, "send_wait")
    landed = {m: _with_own(land, src, me, True) for m, land, src in zip(members, lands, srcs)}

    out_g, out_d, out_m, out_v = {}, {}, {}, {}
    for name, (layers, r, c) in _BIG.items():
        shape = P[name].shape
        w2, m2, v2 = [t[name].reshape(layers * r, c) for t in (P, M, V)]
        res = None
        for l in range(layers):
            res = _sum_adamw(landed[(name, l)], w2, m2, v2, f"adamw_{name}{l}", layer=l, prev=res)
        out_g[name], out_d[name], out_m[name], out_v[name] = [t.reshape(shape) for t in res]

    small_sum = _sum8(landed[("small", 0)], "sum_small")
    (repl_all,) = _all_gather([small_sum[:_REPL_ROWS]], "gather_small_grads")
    g_repl = _unpack(repl_all, _REPL)
    g_ss = _unpack(small_sum[_REPL_ROWS:_REPL_ROWS + _SS_ROWS], _SMALL_SHARDED)
    table = dict(_REPL)
    table.update(_SMALL_SHARDED)
    rows = -(-(_N_REPL + _N_SS) // (256 * LANE)) * 256
    g_small = dict(g_repl)
    g_small.update(g_ss)
    packs = [_pack([src[n] for n in table], rows) for src in (g_small, P, M, V)]
    res = _adamw(*packs, "adamw_small")
    for out, buf in zip((out_d, out_m, out_v), res):
        out.update(_unpack(buf, table))
    out_g.update(g_small)

    return (loss, grad_x[None], *[out_g[n] for n in _NAMES], *[out_d[n] for n in _NAMES],
            *[out_m[n] for n in _NAMES], *[out_v[n] for n in _NAMES])


_NAMES = ("ab_norm", "ab_w_in", "a_conv_w", "a_conv_b", "a_gate_x_w", "a_gate_x_b", "a_gate_a_w", "a_gate_a_b",
          "a_lambda", "b_group_w", "b_group_b", "b_scale", "ab_w_out", "c_norm", "c_w_pw1", "c_b_pw1", "c_dw_w",
          "c_dw_b", "c_ln_g", "c_ln_b", "c_w_pw2", "c_b_pw2", "xa_norm", "xa_mem_norm", "xa_wq", "xa_wk", "xa_wv",
          "xa_wo", "f_norm", "f_w_up", "f_dw_w", "f_dw_b", "f_w_down", "final_norm")
```

```python
import functools

import jax
import jax.numpy as jnp
from jax import lax
from jax.experimental import pallas as pl
from jax.experimental.pallas import tpu as pltpu

F32, BF16 = jnp.float32, jnp.bfloat16
SDS = jax.ShapeDtypeStruct
MESH = pl.DeviceIdType.MESH

N_DEV = 8
D = 1024
N_MEM = 256
XA_HEADS, XA_HD = 4, 256
HD_A = 128
CONV_A, CONV_C, CONV_F = 4, 31, 3
C_RG = 8.0
POOL_WINDOWS = (2, 4, 8, 16)
D_FF = 3 * D
EPS = 1e-6
ADAM_LR, ADAM_B1, ADAM_B2, ADAM_EPS, ADAM_WD, ADAM_STEP = 0.001, 0.9, 0.999, 1e-08, 0.01, 10

LANE = 128
SUB = 8
VMEM_LIMIT = 56 * 1024 * 1024
R_SEQ = 256
TM_ROW = 512


def _cp(n_axes):
    return pltpu.CompilerParams(dimension_semantics=("arbitrary",) * n_axes, vmem_limit_bytes=VMEM_LIMIT)


def _tile(n, pref):
    if n <= pref:
        return n
    best = None
    for t in range(LANE, pref + 1, LANE):
        if n % t == 0:
            best = t
    assert best is not None, (n, pref)
    return best


def _perm2(n):
    return (n % 2) * 4 + n // 2


_NN = (((1,), (0,)), ((), ()))
_NT = (((1,), (1,)), ((), ()))
_TN = (((0,), (0,)), ((), ()))


def _mm_call(name, grid, ab, ab_specs, dims, acc_shape, extras, outs, finish, from_ref=False):
    nk = grid[2]
    n_ab, n_ex, n_out = len(ab), len(extras), len(outs)
    use_acc = nk > 1 or from_ref

    def product(refs):
        r = lax.dot_general(refs[0][...], refs[1][...], dims, preferred_element_type=F32)
        for i in range(1, n_ab):
            r = r + lax.dot_general(refs[2 * i][...], refs[2 * i + 1][...], dims, preferred_element_type=F32)
        return r

    def body(*refs):
        rest = refs[2 * n_ab:]
        ex_refs, o_refs = rest[:n_ex], rest[n_ex:n_ex + n_out]
        first_rows = pl.program_id(0) == 0
        if not use_acc:
            finish(product(refs), ex_refs, o_refs, first_rows)
            return
        acc = rest[n_ex + n_out]
        if nk == 1:
            acc[...] = product(refs)
            finish(acc, ex_refs, o_refs, first_rows)
            return
        k = pl.program_id(2)

        @pl.when(k == 0)
        def _():
            acc[...] = jnp.zeros_like(acc)

        acc[...] += product(refs)

        @pl.when(k == nk - 1)
        def _():
            finish(acc if from_ref else acc[...], ex_refs, o_refs, first_rows)

    res = pl.pallas_call(
        body, out_shape=[o for o, _ in outs], grid=grid,
        in_specs=list(ab_specs) + [s for _, s in extras], out_specs=[s for _, s in outs],
        scratch_shapes=[pltpu.VMEM(acc_shape, F32)] if use_acc else [], name=name, compiler_params=_cp(3),
    )(*[t for pair in ab for t in pair], *[e for e, _ in extras])
    return res[0] if n_out == 1 else res


def _finish_sum(r, ex_refs, o_refs, first_rows):
    del first_rows
    for e in ex_refs:
        r = r + e[...]
    o_refs[0][...] = r.astype(o_refs[0].dtype)


def _finish_sum_norm(r, ex_refs, o_refs, first_rows):
    del first_rows
    for e in ex_refs[:-1]:
        r = r + e[...]
    o_refs[0][...] = r
    o_refs[1][...] = ((r * lax.rsqrt(jnp.mean(r * r, axis=-1, keepdims=True) + EPS)) * ex_refs[-1][...]).astype(BF16)


_EPI_ROWS = 16


def _finish_rms_bwd(r_ref, ex_refs, o_refs, first_rows):
    x_ref, g_ref, dres_ref = ex_refs
    dx_ref, dxb_ref, dg_ref = o_refs

    @pl.when(first_rows)
    def _():
        dg_ref[...] = jnp.zeros_like(dg_ref)

    gv = g_ref[...]
    inv_d = 1.0 / r_ref.shape[1]

    def step(i, dg_acc):
        groups = [pl.ds(pl.multiple_of(i * (2 * _EPI_ROWS) + u * _EPI_ROWS, _EPI_ROWS), _EPI_ROWS) for u in range(2)]
        sums = []
        for rows in groups:
            r, xf = r_ref[rows, :], x_ref[rows, :]
            sums.append((jnp.sum(xf * xf, axis=-1, keepdims=True), jnp.sum((r * gv) * xf, axis=-1, keepdims=True)))
        for rows, (sxx, sax) in zip(groups, sums):
            r, xf = r_ref[rows, :], x_ref[rows, :]
            rs = lax.rsqrt(sxx * inv_d + EPS)
            dg_acc = dg_acc + _psum8(r * (xf * rs))
            dx = rs * (r * gv) - xf * (rs * rs * (sax * rs * inv_d)) + dres_ref[rows, :]
            dx_ref[rows, :] = dx
            dxb_ref[rows, :] = dx.astype(BF16)
        return dg_acc

    dg_acc = lax.fori_loop(0, r_ref.shape[0] // (2 * _EPI_ROWS), step, jnp.zeros((SUB, r_ref.shape[1]), F32))
    dg_ref[...] += jnp.sum(dg_acc, axis=0, keepdims=True)


def _rms_bwd_io(M, tm, x, g, dres):
    rows = pl.BlockSpec((tm, D), lambda m, n, k: (m, 0))
    vec = pl.BlockSpec((1, D), lambda m, n, k: (0, 0))
    return ([(x, rows), (g, vec), (dres, rows)],
            [(SDS((M, D), F32), rows), (SDS((M, D), BF16), rows), (SDS((1, D), F32), vec)])


_K_WHOLE = 3072


def _mm_nn(a, b, *, out_dtype, name, bias=None, add=None, norm=None):
    M, K = a.shape
    tk = K if K <= _K_WHOLE else _tile(K, 1024)
    tm = _tile(M, 1024 if K <= 1024 and norm is None else 512)
    if b.ndim == 3:
        nb, _, bw = b.shape
        N, tn, nn = nb * bw, bw, nb
        b_spec = pl.BlockSpec((None, tk, bw), lambda m, n, k: (_perm2(n), k, 0))
    else:
        N = b.shape[1]
        tn = _tile(N, 1024)
        nn = N // tn
        b_spec = pl.BlockSpec((tk, tn), lambda m, n, k: (k, n))
    tile = pl.BlockSpec((tm, tn), lambda m, n, k: (m, n))
    vec = pl.BlockSpec((1, tn), lambda m, n, k: (0, n))
    extras = ([] if bias is None else [(bias, vec)]) + ([] if add is None else [(add, tile)])
    outs, finish = [(SDS((M, N), out_dtype), tile)], _finish_sum
    if norm is not None:
        assert tn == N == D and out_dtype == F32
        extras.append((norm, vec))
        outs, finish = outs + [(SDS((M, N), BF16), tile)], _finish_sum_norm
    return _mm_call(name, (M // tm, nn, K // tk), [(a, b)], [pl.BlockSpec((tm, tk), lambda m, n, k: (m, k)), b_spec],
                    _NN, (tm, tn), extras, outs, finish)


def _mm_nt(a, b, *, out_dtype, name, add=None, rms=None):
    M, N = a.shape
    if b.ndim == 3:
        nb, Ko, bw = b.shape
        tm = _tile(M, 1024)
        tn, tk, nk = _tile(Ko, 1024), bw, nb
        b_spec = pl.BlockSpec((None, tn, bw), lambda m, n, k: (_perm2(k), n, 0))
    else:
        Ko = b.shape[0]
        tk = N if N <= _K_WHOLE else _tile(N, 1024)
        tm = _tile(M, 1024 if N <= 1024 and rms is None else 512)
        tn = _tile(Ko, 1024)
        nk = N // tk
        b_spec = pl.BlockSpec((tn, tk), lambda m, n, k: (n, k))
    tile = pl.BlockSpec((tm, tn), lambda m, n, k: (m, n))
    extras = [] if add is None else [(add, tile)]
    outs, finish = [(SDS((M, Ko), out_dtype), tile)], _finish_sum
    if rms is not None:
        assert tn == Ko == D and add is None
        (extras, outs), finish = _rms_bwd_io(M, tm, *rms), _finish_rms_bwd
    return _mm_call(name, (M // tm, Ko // tn, nk), [(a, b)], [pl.BlockSpec((tm, tk), lambda m, n, k: (m, k)), b_spec],
                    _NT, (tm, tn), extras, outs, finish, from_ref=rms is not None)


def _mm_nt_cols(parts, b, *, name, rms):
    M = parts[0].shape[0]
    tm = _tile(M, 512)
    specs, off = [], 0
    for p in parts:
        w = p.shape[1]
        assert off % w == 0
        specs.append(pl.BlockSpec((tm, w), lambda m, n, k: (m, 0)))
        specs.append(pl.BlockSpec((D, w), functools.partial(lambda m, n, k, o: (0, o), o=off // w)))
        off += w
    extras, outs = _rms_bwd_io(M, tm, *rms)
    return _mm_call(name, (M // tm, 1, 1), [(p, b) for p in parts], specs, _NT, (tm, D), extras, outs, _finish_rms_bwd,
                    from_ref=True)


def _mm_tn(a, b, *, out_dtype, name, blocks=None):
    S, Ka = a.shape
    Nb = b.shape[1]
    tm, tk = _tile(Ka, 1024), _tile(S, 2048)
    if blocks is not None:
        bw = blocks
        tn, nn = bw, Nb // bw
        out = (SDS((nn, Ka, bw), out_dtype), pl.BlockSpec((None, tm, bw), lambda m, n, k: (_perm2(n), m, 0)))
    else:
        tn = _tile(Nb, 1024)
        nn = Nb // tn
        out = (SDS((Ka, Nb), out_dtype), pl.BlockSpec((tm, tn), lambda m, n, k: (m, n)))
    return _mm_call(name, (Ka // tm, nn, S // tk), [(a, b)],
                    [pl.BlockSpec((tk, tm), lambda m, n, k: (k, m)), pl.BlockSpec((tk, tn), lambda m, n, k: (k, n))],
                    _TN, (tm, tn), [], [out], _finish_sum)


def _row(tm, c):
    return pl.BlockSpec((tm, c), lambda i: (i, 0))


def _full(shape):
    nd = len(shape)
    return pl.BlockSpec(shape, lambda i: (0,) * nd)


def _rms_fwd(x, g, name):
    S = x.shape[0]
    tm = min(S, TM_ROW)

    def body(x_ref, g_ref, o_ref):
        xf = x_ref[...]
        r = lax.rsqrt(jnp.mean(xf * xf, axis=-1, keepdims=True) + EPS)
        o_ref[...] = ((xf * r) * g_ref[...]).astype(BF16)

    return pl.pallas_call(body, out_shape=SDS((S, D), BF16), grid=(S // tm,), in_specs=[_row(tm, D), _full((1, D))],
                          out_specs=_row(tm, D), name=name, compiler_params=_cp(1))(x, g)


def _rms_bwd(x, g, dn, dres, name):
    S = x.shape[0]
    tm = min(S, TM_ROW)
    want_dx = dres is not None

    def body(x_ref, g_ref, dn_ref, *rest):
        i = pl.program_id(0)
        dg_ref = rest[-1]

        @pl.when(i == 0)
        def _():
            dg_ref[...] = jnp.zeros_like(dg_ref)

        xf = x_ref[...]
        r = lax.rsqrt(jnp.mean(xf * xf, axis=-1, keepdims=True) + EPS)
        y = xf * r
        dn_v = dn_ref[...]
        dg_ref[...] += jnp.sum(dn_v * y, axis=0, keepdims=True)
        if want_dx:
            dres_ref, dx_ref, dxb_ref = rest[0], rest[1], rest[2]
            dy = dn_v * g_ref[...]
            dx = r * (dy - y * jnp.mean(dy * y, axis=-1, keepdims=True)) + dres_ref[...]
            dx_ref[...] = dx
            dxb_ref[...] = dx.astype(BF16)

    ins = [x, g, dn] + ([dres] if want_dx else [])
    in_specs = [_row(tm, D), _full((1, D)), _row(tm, D)] + ([_row(tm, D)] if want_dx else [])
    outs = ([SDS((S, D), F32), SDS((S, D), BF16)] if want_dx else []) + [SDS((1, D), F32)]
    out_specs = ([_row(tm, D), _row(tm, D)] if want_dx else []) + [_full((1, D))]
    return pl.pallas_call(body, out_shape=outs, grid=(S // tm,), in_specs=in_specs, out_specs=out_specs, name=name,
                          compiler_params=_cp(1))(*ins)


def _loss_head(x, g, tgt):
    S = x.shape[0]
    tm = min(S, TM_ROW)

    def body(x_ref, g_ref, t_ref, loss_ref, dx_ref, dxb_ref, dg_ref):
        i = pl.program_id(0)

        @pl.when(i == 0)
        def _():
            loss_ref[...] = jnp.zeros_like(loss_ref)
            dg_ref[...] = jnp.zeros_like(dg_ref)

        xf = x_ref[...]
        r = lax.rsqrt(jnp.mean(xf * xf, axis=-1, keepdims=True) + EPS)
        y = xf * r
        gv = g_ref[...]
        err = y * gv - t_ref[...]
        per_row = jnp.mean(err * err, axis=-1, keepdims=True)
        loss_ref[...] += 0.5 * jnp.sum(per_row, axis=0, keepdims=True)
        dn_v = err * (1.0 / D)
        dg_ref[...] += jnp.sum(dn_v * y, axis=0, keepdims=True)
        dy = dn_v * gv
        dx = r * (dy - y * jnp.mean(dy * y, axis=-1, keepdims=True))
        dx_ref[...] = dx
        dxb_ref[...] = dx.astype(BF16)

    return pl.pallas_call(
        body, out_shape=[SDS((1, 1), F32), SDS((S, D), F32), SDS((S, D), BF16), SDS((1, D), F32)], grid=(S // tm,),
        in_specs=[_row(tm, D), _full((1, D)), _row(tm, D)],
        out_specs=[_full((1, 1)), _row(tm, D), _row(tm, D), _full((1, D))], name="loss_head", compiler_params=_cp(1),
    )(x, g, tgt)


def _softmax_rows(s):
    m = jnp.max(s, axis=-1, keepdims=True)
    e = jnp.exp(s - m)
    return e / jnp.sum(e, axis=-1, keepdims=True)


def _attn_fwd(q, k, v, name):
    S = q.shape[0]
    tm = min(S, TM_ROW)
    scale = XA_HD ** -0.5

    def body(q_ref, k_ref, v_ref, o_ref):
        for h in range(XA_HEADS):
            sl = slice(h * XA_HD, (h + 1) * XA_HD)
            s = lax.dot_general(q_ref[:, sl], k_ref[:, sl], _NT, preferred_element_type=F32) * scale
            p = _softmax_rows(s)
            o_ref[:, sl] = lax.dot_general(p.astype(BF16), v_ref[:, sl], _NN, preferred_element_type=F32).astype(BF16)

    return pl.pallas_call(body, out_shape=SDS((S, D), BF16), grid=(S // tm,),
                          in_specs=[_row(tm, D), _full((N_MEM, D)), _full((N_MEM, D))], out_specs=_row(tm, D),
                          name=name, compiler_params=_cp(1))(q, k, v)


def _attn_bwd(q, k, v, do, name):
    S = q.shape[0]
    tm = min(S, TM_ROW)
    scale = XA_HD ** -0.5

    def body(q_ref, k_ref, v_ref, do_ref, dq_ref, dk_ref, dv_ref):
        i = pl.program_id(0)

        @pl.when(i == 0)
        def _():
            dk_ref[...] = jnp.zeros_like(dk_ref)
            dv_ref[...] = jnp.zeros_like(dv_ref)

        for h in range(XA_HEADS):
            sl = slice(h * XA_HD, (h + 1) * XA_HD)
            qh, kh, vh, doh = q_ref[:, sl], k_ref[:, sl], v_ref[:, sl], do_ref[:, sl]
            s = lax.dot_general(qh, kh, _NT, preferred_element_type=F32) * scale
            p = _softmax_rows(s)
            pb = p.astype(BF16)
            dv_ref[:, sl] += lax.dot_general(pb, doh, _TN, preferred_element_type=F32)
            dp = lax.dot_general(doh, vh, _NT, preferred_element_type=F32)
            ds = (p * (dp - jnp.sum(dp * p, axis=-1, keepdims=True)) * scale).astype(BF16)
            dq_ref[:, sl] = lax.dot_general(ds, kh, _NN, preferred_element_type=F32).astype(BF16)
            dk_ref[:, sl] += lax.dot_general(ds, qh, _TN, preferred_element_type=F32)

    return pl.pallas_call(
        body, out_shape=[SDS((S, D), BF16), SDS((N_MEM, D), F32), SDS((N_MEM, D), F32)], grid=(S // tm,),
        in_specs=[_row(tm, D), _full((N_MEM, D)), _full((N_MEM, D)), _row(tm, D)],
        out_specs=[_row(tm, D), _full((N_MEM, D)), _full((N_MEM, D))], name=name, compiler_params=_cp(1),
    )(q, k, v, do)


def _sigmoid(x):
    return 1.0 / (1.0 + jnp.exp(-x))


def _ln_silu_fwd(cv, g, b):
    S = cv.shape[0]
    tm = min(S, TM_ROW)

    def body(x_ref, g_ref, b_ref, o_ref):
        xf = x_ref[...]
        mu = jnp.mean(xf, axis=-1, keepdims=True)
        xc = xf - mu
        rstd = lax.rsqrt(jnp.mean(xc * xc, axis=-1, keepdims=True) + EPS)
        ln = (xc * rstd) * g_ref[...] + b_ref[...]
        o_ref[...] = (ln * _sigmoid(ln)).astype(BF16)

    return pl.pallas_call(body, out_shape=SDS((S, D), BF16), grid=(S // tm,),
                          in_specs=[_row(tm, D), _full((1, D)), _full((1, D))], out_specs=_row(tm, D),
                          name="ln_silu_fwd", compiler_params=_cp(1))(cv, g, b)


def _ln_silu_bwd(ds, cv, g, b, dx):
    S = cv.shape[0]
    tm = min(S, TM_ROW)

    def body(ds_ref, x_ref, g_ref, b_ref, dx_ref, dcv_ref, dg_ref, db_ref, db2_ref):
        i = pl.program_id(0)

        @pl.when(i == 0)
        def _():
            dg_ref[...] = jnp.zeros_like(dg_ref)
            db_ref[...] = jnp.zeros_like(db_ref)
            db2_ref[...] = jnp.zeros_like(db2_ref)

        xf = x_ref[...]
        mu = jnp.mean(xf, axis=-1, keepdims=True)
        xc = xf - mu
        rstd = lax.rsqrt(jnp.mean(xc * xc, axis=-1, keepdims=True) + EPS)
        xhat = xc * rstd
        gv = g_ref[...]
        ln = xhat * gv + b_ref[...]
        sg = _sigmoid(ln)
        dln = ds_ref[...].astype(F32) * (sg + ln * sg * (1.0 - sg))
        dg_ref[...] += jnp.sum(dln * xhat, axis=0, keepdims=True)
        db_ref[...] += jnp.sum(dln, axis=0, keepdims=True)
        db2_ref[...] += jnp.sum(dx_ref[...], axis=0, keepdims=True)
        dxh = dln * gv
        dcv_ref[...] = rstd * (dxh - jnp.mean(dxh, axis=-1, keepdims=True)
                               - xhat * jnp.mean(dxh * xhat, axis=-1, keepdims=True))

    return pl.pallas_call(
        body, out_shape=[SDS((S, D), F32), SDS((1, D), F32), SDS((1, D), F32), SDS((1, D), F32)], grid=(S // tm,),
        in_specs=[_row(tm, D), _row(tm, D), _full((1, D)), _full((1, D)), _row(tm, D)],
        out_specs=[_row(tm, D), _full((1, D)), _full((1, D)), _full((1, D))], name="ln_silu_bwd",
        compiler_params=_cp(1),
    )(ds, cv, g, b, dx)


_GELU_C, _GELU_K = 0.7978845608028654, 0.044715


def _gelu(x, with_grad=False):
    x2 = x * x
    t = jnp.tanh(_GELU_C * (x + _GELU_K * x * x2))
    gel = 0.5 * x * (1.0 + t)
    if not with_grad:
        return gel
    return gel, 0.5 * (1.0 + t) + 0.5 * x * (1.0 - t * t) * (_GELU_C * (1.0 + 3.0 * _GELU_K * x2))


def _expm1(x):
    poly = x * (1.0 + x * (0.5 + x * (1.0 / 6.0 + x * (1.0 / 24.0 + x * (1.0 / 120.0)))))
    return jnp.where(jnp.abs(x) < 0.05, poly, jnp.exp(x) - 1.0)


def _softplus(x):
    return jnp.maximum(x, 0.0) + jnp.log1p(jnp.exp(-jnp.abs(x)))


_SCAN_UNROLL = 4
_RB = 32
_HB = 16


def _sub_blocks(n_rows, n_lanes, fn):
    def step(idx, c):
        r0 = pl.multiple_of(idx * _RB, _RB)
        for lt in range(n_lanes // LANE):
            fn(r0, lt)
        return c

    lax.fori_loop(0, n_rows // _RB, step, 0)


def _lanes(lt):
    return pl.ds(lt * LANE, LANE)


def _psum8(x):
    parts = [x[i * SUB:(i + 1) * SUB] for i in range(x.shape[0] // SUB)]
    return functools.reduce(lambda p, q: p + q, parts)


def _scan_fwd(a_s, b_s, out_ref, carry_ref, n_groups):
    row = lax.broadcasted_iota(jnp.int32, (SUB, LANE), 0)
    U = _SCAN_UNROLL

    def step(gi, carry):
        base = gi * (SUB * U)
        parts = []
        for u in range(U):
            i = pl.multiple_of(base + u * SUB, SUB)
            a8, b8 = a_s[pl.ds(i, SUB), :], b_s[pl.ds(i, SUB), :]
            for s in (1, 2, 4):
                a_sh = jnp.where(row >= s, pltpu.roll(a8, s, 0), 1.0)
                b_sh = jnp.where(row >= s, pltpu.roll(b8, s, 0), 0.0)
                b8 = a8 * b_sh + b8
                a8 = a8 * a_sh
            parts.append((i, a8, b8))
        for i, a8, b8 in parts:
            h8 = a8 * carry + b8
            out_ref[pl.ds(i, SUB), :] = h8
            carry = jnp.broadcast_to(h8[SUB - 1:SUB, :], (SUB, LANE))
        return carry

    carry_ref[...] = lax.fori_loop(0, n_groups // U, step, carry_ref[...])


def _scan_bwd(a_s, b_s, out_ref, carry_ref, n_groups):
    row = lax.broadcasted_iota(jnp.int32, (SUB, LANE), 0)
    U = _SCAN_UNROLL

    def step(gi, carry):
        base = (n_groups // U - 1 - gi) * (SUB * U)
        parts = []
        for u in reversed(range(U)):
            i = pl.multiple_of(base + u * SUB, SUB)
            a8, b8 = a_s[pl.ds(i, SUB), :], b_s[pl.ds(i, SUB), :]
            for s in (1, 2, 4):
                a_sh = jnp.where(row < SUB - s, pltpu.roll(a8, SUB - s, 0), 1.0)
                b_sh = jnp.where(row < SUB - s, pltpu.roll(b8, SUB - s, 0), 0.0)
                b8 = a8 * b_sh + b8
                a8 = a8 * a_sh
            parts.append((i, a8, b8))
        for i, a8, b8 in parts:
            h8 = a8 * carry + b8
            out_ref[pl.ds(i, SUB), :] = h8
            carry = jnp.broadcast_to(h8[0:1, :], (SUB, LANE))
        return carry

    carry_ref[...] = lax.fori_loop(0, n_groups // U, step, carry_ref[...])


def _rglru_pre(xr, wgx_ref, bgx_ref, wga_ref, bga_ref, lam_ref):
    xrb = xr.astype(BF16)
    wgx, wga = wgx_ref[0].astype(BF16), wga_ref[0].astype(BF16)
    gx = _sigmoid(lax.dot_general(xrb, wgx, _NN, preferred_element_type=F32) + bgx_ref[...])
    ga = _sigmoid(lax.dot_general(xrb, wga, _NN, preferred_element_type=F32) + bga_ref[...])
    sp = _softplus(-lam_ref[...])
    log_a = -C_RG * ga * sp
    a = jnp.exp(log_a)
    mult = jnp.sqrt(-_expm1(2.0 * log_a))
    return gx, ga, sp, a, mult, xrb, wgx, wga


def _a_specs():
    vec = pl.BlockSpec((1, HD_A), lambda c, j: (0, c))
    mat = pl.BlockSpec((1, HD_A, HD_A), lambda c, j: (c, 0, 0))
    return [pl.BlockSpec((CONV_A, HD_A), lambda c, j: (0, c)), vec, mat, vec, mat, vec, vec]


def _a_fwd(zp, conv_w, conv_b, wgx, bgx, wga, bga, lam):
    S = zp.shape[0]
    R, nt = R_SEQ, D // HD_A
    H = SUB

    def body(zg_ref, zr_ref, cw_ref, cb_ref, wgx_ref, bgx_ref, wga_ref, bga_ref, lam_ref, ya_ref, h_ref,
             ext, a_s, b_s, hc):
        j = pl.program_id(1)

        @pl.when(j == 0)
        def _():
            ext[0:H, :] = jnp.zeros((H, HD_A), F32)
            hc[...] = jnp.zeros_like(hc)

        ext[H:H + R, :] = zr_ref[...].astype(F32)
        xr = cb_ref[...]
        for k in range(CONV_A):
            xr = xr + cw_ref[k:k + 1, :] * ext[pl.ds(H - (CONV_A - 1 - k), R), :]
        gx, _, _, a, mult, _, _, _ = _rglru_pre(xr, wgx_ref, bgx_ref, wga_ref, bga_ref, lam_ref)
        a_s[...] = a
        b_s[...] = mult * (gx * xr)
        _scan_fwd(a_s, b_s, h_ref, hc, R // SUB)
        ya_ref[...] = (_gelu(zg_ref[...].astype(F32)) * h_ref[...]).astype(BF16)
        ext[0:H, :] = ext[R:R + H, :]

    return pl.pallas_call(
        body, out_shape=[SDS((S, D + D // 2), BF16), SDS((S, D), F32)], grid=(nt, S // R),
        in_specs=[pl.BlockSpec((R, HD_A), lambda c, j: (j, c)), pl.BlockSpec((R, HD_A), lambda c, j: (j, nt + c))]
        + _a_specs(),
        out_specs=[pl.BlockSpec((R, HD_A), lambda c, j: (j, c)), pl.BlockSpec((R, HD_A), lambda c, j: (j, c))],
        scratch_shapes=[pltpu.VMEM((H + R, HD_A), F32), pltpu.VMEM((R, HD_A), F32), pltpu.VMEM((R, HD_A), F32),
                        pltpu.VMEM((SUB, HD_A), F32)],
        name="rglru_fwd", compiler_params=_cp(2),
    )(zp, zp, conv_w, conv_b, wgx, bgx, wga, bga, lam)


def _a_bwd(dyab, zp, h, conv_w, conv_b, wgx, bgx, wga, bga, lam):
    S = zp.shape[0]
    R, nt, nch = R_SEQ, D // HD_A, S // R_SEQ
    H = SUB

    def rows(c, j):
        return (nch - 1 - j, c)

    def rows_rec(c, j):
        return (nch - 1 - j, nt + c)

    def halo(c, j):
        return (jnp.maximum((nch - 1 - j) * (R // H) - 1, 0), c)

    def halo_z(c, j):
        return (jnp.maximum((nch - 1 - j) * (R // _HB) - 1, 0), nt + c)

    def body(dy_ref, zg_ref, zr_ref, zh_ref, h_ref, hh_ref, cw_ref, cb_ref, wgx_ref, bgx_ref, wga_ref, bga_ref,
             lam_ref, dzg_ref, dzr_ref, dcw_ref, dcb_ref, dwgx_ref, dbgx_ref, dwga_ref, dbga_ref, dlam_ref,
             ext_z, ext_h, ext_mu, ext_d, a_s, b_s, muc):
        j = pl.program_id(1)
        first_chunk = (nch - 1 - j) == 0

        @pl.when(j == 0)
        def _():
            ext_mu[R:R + H, :] = jnp.zeros((H, HD_A), F32)
            ext_d[R:R + H, :] = jnp.zeros((H, HD_A), F32)
            muc[...] = jnp.zeros_like(muc)
            for r in (dcw_ref, dcb_ref, dwgx_ref, dbgx_ref, dwga_ref, dbga_ref, dlam_ref):
                r[...] = jnp.zeros_like(r)

        zg = zg_ref[...].astype(F32)
        ext_z[0:H, :] = jnp.where(first_chunk, 0.0, zh_ref[_HB - H:_HB, :].astype(F32))
        ext_z[H:H + R, :] = zr_ref[...].astype(F32)
        ext_h[0:H, :] = jnp.where(first_chunk, 0.0, hh_ref[...])
        ext_h[H:H + R, :] = h_ref[...]
        xr = cb_ref[...]
        for k in range(CONV_A):
            xr = xr + cw_ref[k:k + 1, :] * ext_z[pl.ds(H - (CONV_A - 1 - k), R), :]
        gx, ga, sp, a, mult, xrb, wgxb, wgab = _rglru_pre(xr, wgx_ref, bgx_ref, wga_ref, bga_ref, lam_ref)
        gel, dgel = _gelu(zg, with_grad=True)
        dy = dy_ref[...].astype(F32)
        dh = dy * gel
        dzg_ref[...] = (dy * h_ref[...] * dgel).astype(BF16)
        a_s[...] = a
        b_s[...] = a * dh
        _scan_bwd(a_s, b_s, ext_mu, muc, R // SUB)
        lam_t = dh + ext_mu[pl.ds(1, R), :]
        ext_mu[R:R + H, :] = ext_mu[0:H, :]
        da = lam_t * ext_h[pl.ds(H - 1, R), :]
        gxr = gx * xr
        dlog_a = da * a - (lam_t * gxr) * (a * a) / mult
        dgx = lam_t * mult * xr
        dxr = lam_t * mult * gx
        lam_v = lam_ref[...]
        dlam_ref[...] += jnp.sum(dlog_a * ga, axis=0, keepdims=True) * (C_RG * _sigmoid(-lam_v))
        dpa = (dlog_a * (-C_RG * sp)) * ga * (1.0 - ga)
        dpx = dgx * gx * (1.0 - gx)
        dbga_ref[...] += jnp.sum(dpa, axis=0, keepdims=True)
        dbgx_ref[...] += jnp.sum(dpx, axis=0, keepdims=True)
        dpab, dpxb = dpa.astype(BF16), dpx.astype(BF16)
        dwga_ref[0] += lax.dot_general(xrb, dpab, _TN, preferred_element_type=F32)
        dwgx_ref[0] += lax.dot_general(xrb, dpxb, _TN, preferred_element_type=F32)
        dxr = (dxr + lax.dot_general(dpab, wgab, _NT, preferred_element_type=F32)
               + lax.dot_general(dpxb, wgxb, _NT, preferred_element_type=F32))
        dcb_ref[...] += jnp.sum(dxr, axis=0, keepdims=True)
        ext_d[0:R, :] = dxr
        dzr = jnp.zeros((R, HD_A), F32)
        for k in range(CONV_A):
            sh = CONV_A - 1 - k
            dcw_ref[k:k + 1, :] += jnp.sum(dxr * ext_z[pl.ds(H - sh, R), :], axis=0, keepdims=True)
            dzr = dzr + cw_ref[k:k + 1, :] * ext_d[pl.ds(sh, R), :]
        dzr_ref[...] = dzr.astype(BF16)
        ext_d[R:R + H, :] = ext_d[0:H, :]

    vec_o = pl.BlockSpec((1, HD_A), lambda c, j: (0, c))
    mat_o = pl.BlockSpec((1, HD_A, HD_A), lambda c, j: (c, 0, 0))
    return pl.pallas_call(
        body,
        out_shape=[SDS((S, D), BF16), SDS((S, D), BF16), SDS((CONV_A, D), F32), SDS((1, D), F32),
                   SDS((nt, HD_A, HD_A), F32), SDS((1, D), F32), SDS((nt, HD_A, HD_A), F32), SDS((1, D), F32),
                   SDS((1, D), F32)],
        grid=(nt, nch),
        in_specs=[pl.BlockSpec((R, HD_A), rows), pl.BlockSpec((R, HD_A), rows), pl.BlockSpec((R, HD_A), rows_rec),
                  pl.BlockSpec((_HB, HD_A), halo_z), pl.BlockSpec((R, HD_A), rows),
                  pl.BlockSpec((H, HD_A), halo)] + _a_specs(),
        out_specs=[pl.BlockSpec((R, HD_A), rows), pl.BlockSpec((R, HD_A), rows),
                   pl.BlockSpec((CONV_A, HD_A), lambda c, j: (0, c)), vec_o, mat_o, vec_o, mat_o, vec_o, vec_o],
        scratch_shapes=[pltpu.VMEM((H + R, HD_A), F32), pltpu.VMEM((H + R, HD_A), F32), pltpu.VMEM((R + H, HD_A), F32),
                        pltpu.VMEM((R + H, HD_A), F32), pltpu.VMEM((R, HD_A), F32), pltpu.VMEM((R, HD_A), F32),
                        pltpu.VMEM((SUB, HD_A), F32)],
        name="rglru_bwd", compiler_params=_cp(2),
    )(dyab, zp, zp, zp, h, h, conv_w, conv_b, wgx, bgx, wga, bga, lam)


_POOL_H = 16
_POOL_T0 = 2 * D // HD_A
_POOL_Y0 = D // HD_A


def _window_sum(lv, n, lo, rows, g, ahead):
    base = 0 if ahead else SUB
    cur, win = lv[0], None
    for i, s in enumerate((1, 2, 4, 8)):
        val = cur[pl.ds(base, n), :] + cur[pl.ds(base + (s if ahead else -s), n), :]
        sel = val[lo:lo + rows]
        win = sel if win is None else jnp.where(g >= i, sel, win)
        if i < 3:
            lv[i + 1][pl.ds(base, n), :] = val
            cur = lv[i + 1]
    return win


def _pool_width(g):
    return jnp.where(g == 0, 2.0, jnp.where(g == 1, 4.0, jnp.where(g == 2, 8.0, 16.0)))


def _b_fwd(zp, yab, wg, bg, sc):
    S = zp.shape[0]
    R, H = R_SEQ, _POOL_H

    def body(z_ref, wg_ref, bg_ref, sc_ref, yab_in, yb_ref, *lv):
        del yab_in
        g, j = pl.program_id(0), pl.program_id(1)

        @pl.when(j == 0)
        def _():
            for r in lv:
                r[0:SUB, :] = jnp.zeros((SUB, HD_A), F32)
            lv[0][SUB:SUB + H, :] = jnp.zeros((H, HD_A), F32)

        u = z_ref[...].astype(F32)
        lv[0][SUB + H:SUB + H + R, :] = u
        t1 = (j * R + 1 + lax.broadcasted_iota(jnp.int32, (R, HD_A), 0)).astype(F32)
        p = _window_sum(lv, H + R, H, R, g, False) / jnp.minimum(t1, _pool_width(g)) - u
        lin = lax.dot_general(p.astype(BF16), wg_ref[0].astype(BF16), _NN, preferred_element_type=F32) + bg_ref[...]
        yb_ref[...] = (lin * sc_ref[...]).astype(BF16)
        lv[0][SUB:SUB + H, :] = lv[0][SUB + R:SUB + R + H, :]

    vec = pl.BlockSpec((1, HD_A), lambda g, j: (0, g))
    return pl.pallas_call(
        body, out_shape=SDS(yab.shape, yab.dtype), grid=(len(POOL_WINDOWS), S // R),
        in_specs=[pl.BlockSpec((R, HD_A), lambda g, j: (j, _POOL_T0 + g)),
                  pl.BlockSpec((1, HD_A, HD_A), lambda g, j: (g, 0, 0)), vec, vec, pl.BlockSpec(memory_space=pl.ANY)],
        out_specs=pl.BlockSpec((R, HD_A), lambda g, j: (j, _POOL_Y0 + g)),
        scratch_shapes=[pltpu.VMEM((SUB + H + R, HD_A), F32)] * 4, input_output_aliases={4: 0},
        name="pool_fwd", compiler_params=_cp(2),
    )(zp, wg, bg, sc, yab)


def _b_bwd(dyab, zp, wg, bg, sc):
    S = zp.shape[0]
    R, H, nch, ng = R_SEQ, _POOL_H, S // R_SEQ, len(POOL_WINDOWS)

    def body(dy_ref, z_ref, zh_ref, wg_ref, bg_ref, sc_ref, dz_ref, dwg_ref, dbg_ref, dsc_ref, *scratch):
        lu, lq = scratch[:4], scratch[4:]
        g, j = pl.program_id(0), pl.program_id(1)
        jj = nch - 1 - j

        @pl.when(j == 0)
        def _():
            for r in lu:
                r[0:SUB, :] = jnp.zeros((SUB, HD_A), F32)
            for r in lq:
                r[R + H:R + H + SUB, :] = jnp.zeros((SUB, HD_A), F32)
            lq[0][R:R + H, :] = jnp.zeros((H, HD_A), F32)
            for r in (dwg_ref, dbg_ref, dsc_ref):
                r[...] = jnp.zeros_like(r)

        u = z_ref[...].astype(F32)
        lu[0][SUB:SUB + H, :] = jnp.where(jj == 0, 0.0, zh_ref[...].astype(F32))
        lu[0][SUB + H:SUB + H + R, :] = u
        t1 = (jj * R + 1 + lax.broadcasted_iota(jnp.int32, (R, HD_A), 0)).astype(F32)
        cnt = jnp.minimum(t1, _pool_width(g))
        pb = (_window_sum(lu, H + R, H, R, g, False) / cnt - u).astype(BF16)
        wgb = wg_ref[0].astype(BF16)
        lin = lax.dot_general(pb, wgb, _NN, preferred_element_type=F32) + bg_ref[...]
        dy = dy_ref[...].astype(F32)
        dsc_ref[...] += jnp.sum(dy * lin, axis=0, keepdims=True)
        dlin = dy * sc_ref[...]
        dbg_ref[...] += jnp.sum(dlin, axis=0, keepdims=True)
        dlb = dlin.astype(BF16)
        dwg_ref[0] += lax.dot_general(pb, dlb, _TN, preferred_element_type=F32)
        dp = lax.dot_general(dlb, wgb, _NT, preferred_element_type=F32)
        lq[0][0:R, :] = dp / cnt
        dz_ref[...] = (_window_sum(lq, R + H, 0, R, g, True) - dp).astype(BF16)
        lq[0][R:R + H, :] = lq[0][0:H, :]

    vec = pl.BlockSpec((1, HD_A), lambda g, j: (0, g))
    mat = pl.BlockSpec((1, HD_A, HD_A), lambda g, j: (g, 0, 0))
    return pl.pallas_call(
        body, out_shape=[SDS((S, D // 2), BF16), SDS((ng, HD_A, HD_A), F32), SDS((1, D // 2), F32),
                         SDS((1, D // 2), F32)],
        grid=(ng, nch),
        in_specs=[pl.BlockSpec((R, HD_A), lambda g, j: (nch - 1 - j, _POOL_Y0 + g)),
                  pl.BlockSpec((R, HD_A), lambda g, j: (nch - 1 - j, _POOL_T0 + g)),
                  pl.BlockSpec((H, HD_A), lambda g, j: (jnp.maximum((nch - 1 - j) * (R // H) - 1, 0), _POOL_T0 + g)),
                  mat, vec, vec],
        out_specs=[pl.BlockSpec((R, HD_A), lambda g, j: (nch - 1 - j, g)), mat, vec, vec],
        scratch_shapes=[pltpu.VMEM((SUB + H + R, HD_A), F32)] * 8,
        name="pool_bwd", compiler_params=_cp(2),
    )(dyab, zp, zp, wg, bg, sc)


_CW_F = 768


def _f_fwd(hp, w, b, name):
    S = hp.shape[0]
    R, H, cw = R_SEQ, SUB, _CW_F
    nlt = cw // LANE

    def body(h_ref, w_ref, b_ref, o_ref, ext):
        j = pl.program_id(1)

        @pl.when(j == 0)
        def _():
            ext[:, 0:H, :] = jnp.zeros((nlt, H, LANE), F32)

        def stage(r0, lt):
            ext[lt, pl.ds(pl.multiple_of(r0 + H, SUB), _RB), :] = h_ref[pl.ds(r0, _RB), _lanes(lt)].astype(F32)

        def main(r0, lt):
            ls = _lanes(lt)
            gp = b_ref[:, ls]
            for k in range(CONV_F):
                gp = gp + w_ref[k:k + 1, ls] * ext[lt, pl.ds(r0 + (H - (CONV_F - 1 - k)), _RB), :]
            up = h_ref[pl.ds(r0, _RB), _lanes(lt + nlt)].astype(F32)
            o_ref[pl.ds(r0, _RB), ls] = (_gelu(gp) * up).astype(BF16)

        _sub_blocks(R, cw, stage)
        _sub_blocks(R, cw, main)
        ext[:, 0:H, :] = ext[:, R:R + H, :]

    return pl.pallas_call(
        body, out_shape=SDS((S, D_FF), BF16), grid=(D_FF // cw, S // R),
        in_specs=[pl.BlockSpec((R, 2 * cw), lambda c, j: (j, c)), pl.BlockSpec((CONV_F, cw), lambda c, j: (0, c)),
                  pl.BlockSpec((1, cw), lambda c, j: (0, c))],
        out_specs=pl.BlockSpec((R, cw), lambda c, j: (j, c)),
        scratch_shapes=[pltpu.VMEM((nlt, H + R, LANE), F32)], name=name, compiler_params=_cp(2),
    )(hp, w, b)


def _f_bwd(dact, hp, w, b, name):
    S = hp.shape[0]
    R, H, cw, nch = R_SEQ, SUB, _CW_F, S // R_SEQ
    nlt = cw // LANE

    def body(da_ref, h_ref, hh_ref, w_ref, b_ref, dh_ref, dw_ref, db_ref, ext_g, ext_d, acc):
        j = pl.program_id(1)
        jj = nch - 1 - j

        @pl.when(j == 0)
        def _():
            ext_d[:, R:R + H, :] = jnp.zeros((nlt, H, LANE), F32)
            acc[...] = jnp.zeros_like(acc)

        for lt in range(nlt):
            ext_g[lt, 0:H, :] = jnp.where(jj == 0, 0.0, hh_ref[_HB - H:_HB, lt * LANE:(lt + 1) * LANE].astype(F32))

        def stage(r0, lt):
            ext_g[lt, pl.ds(pl.multiple_of(r0 + H, SUB), _RB), :] = h_ref[pl.ds(r0, _RB), _lanes(lt)].astype(F32)

        def first(r0, lt):
            ls, lu, rs = _lanes(lt), _lanes(lt + nlt), pl.ds(r0, _RB)
            taps = [ext_g[lt, pl.ds(r0 + (H - (CONV_F - 1 - k)), _RB), :] for k in range(CONV_F)]
            gp = b_ref[:, ls]
            for k in range(CONV_F):
                gp = gp + w_ref[k:k + 1, ls] * taps[k]
            gel, dgel = _gelu(gp, with_grad=True)
            da = da_ref[rs, ls].astype(F32)
            dh_ref[rs, lu] = (da * gel).astype(BF16)
            dgp = da * h_ref[rs, lu].astype(F32) * dgel
            ext_d[lt, rs, :] = dgp
            acc[CONV_F * SUB:(CONV_F + 1) * SUB, ls] += _psum8(dgp)
            for k in range(CONV_F):
                acc[k * SUB:(k + 1) * SUB, ls] += _psum8(dgp * taps[k])

        def second(r0, lt):
            ls = _lanes(lt)
            dhg = w_ref[CONV_F - 1:CONV_F, ls] * ext_d[lt, pl.ds(r0, _RB), :]
            for k in range(CONV_F - 1):
                dhg = dhg + w_ref[k:k + 1, ls] * ext_d[lt, pl.ds(r0 + (CONV_F - 1 - k), _RB), :]
            dh_ref[pl.ds(r0, _RB), ls] = dhg.astype(BF16)

        _sub_blocks(R, cw, stage)
        _sub_blocks(R, cw, first)
        _sub_blocks(R, cw, second)
        ext_d[:, R:R + H, :] = ext_d[:, 0:H, :]

        @pl.when(j == nch - 1)
        def _():
            for k in range(CONV_F):
                dw_ref[k:k + 1, :] = jnp.sum(acc[k * SUB:(k + 1) * SUB, :], axis=0, keepdims=True)
            db_ref[...] = jnp.sum(acc[CONV_F * SUB:(CONV_F + 1) * SUB, :], axis=0, keepdims=True)

    rows = lambda c, j: (nch - 1 - j, c)
    return pl.pallas_call(
        body, out_shape=[SDS((S, 2 * D_FF), BF16), SDS((CONV_F, D_FF), F32), SDS((1, D_FF), F32)],
        grid=(D_FF // cw, nch),
        in_specs=[pl.BlockSpec((R, cw), rows), pl.BlockSpec((R, 2 * cw), rows),
                  pl.BlockSpec((_HB, 2 * cw), lambda c, j: (jnp.maximum((nch - 1 - j) * (R // _HB) - 1, 0), c)),
                  pl.BlockSpec((CONV_F, cw), lambda c, j: (0, c)), pl.BlockSpec((1, cw), lambda c, j: (0, c))],
        out_specs=[pl.BlockSpec((R, 2 * cw), rows), pl.BlockSpec((CONV_F, cw), lambda c, j: (0, c)),
                   pl.BlockSpec((1, cw), lambda c, j: (0, c))],
        scratch_shapes=[pltpu.VMEM((nlt, H + R, LANE), F32), pltpu.VMEM((nlt, R + H, LANE), F32),
                        pltpu.VMEM(((CONV_F + 1) * SUB, cw), F32)], name=name,
        compiler_params=_cp(2),
    )(dact, hp, hp, w, b)


_CW_C = 256
_H_C = 32


def _c_fwd(h1p, w, b):
    S = h1p.shape[0]
    R, H, cw = R_SEQ, _H_C, _CW_C
    nlt = cw // LANE

    def body(h_ref, w_ref, b_ref, o_ref, ext):
        j = pl.program_id(1)

        @pl.when(j == 0)
        def _():
            ext[:, 0:H, :] = jnp.zeros((nlt, H, LANE), F32)

        def stage(r0, lt):
            rs = pl.ds(r0, _RB)
            gate = h_ref[rs, _lanes(lt + nlt)].astype(F32)
            ext[lt, pl.ds(pl.multiple_of(r0 + H, SUB), _RB), :] = h_ref[rs, _lanes(lt)].astype(F32) * _sigmoid(gate)

        def main(r0, lt):
            ls = _lanes(lt)
            cv = b_ref[:, ls]
            for k in range(CONV_C):
                cv = cv + w_ref[k:k + 1, ls] * ext[lt, pl.ds(r0 + (H - (CONV_C - 1 - k)), _RB), :]
            o_ref[pl.ds(r0, _RB), ls] = cv

        _sub_blocks(R, cw, stage)
        _sub_blocks(R, cw, main)
        ext[:, 0:H, :] = ext[:, R:R + H, :]

    return pl.pallas_call(
        body, out_shape=SDS((S, D), F32), grid=(D // cw, S // R),
        in_specs=[pl.BlockSpec((R, 2 * cw), lambda c, j: (j, c)), pl.BlockSpec((CONV_C, cw), lambda c, j: (0, c)),
                  pl.BlockSpec((1, cw), lambda c, j: (0, c))],
        out_specs=pl.BlockSpec((R, cw), lambda c, j: (j, c)),
        scratch_shapes=[pltpu.VMEM((nlt, H + R, LANE), F32)], name="conf_conv_fwd", compiler_params=_cp(2),
    )(h1p, w, b)


def _c_bwd(dcv, h1p, w):
    S = h1p.shape[0]
    R, H, cw, nch = R_SEQ, _H_C, _CW_C, S // R_SEQ
    nlt = cw // LANE
    a_b, a_val, a_gate = CONV_C * SUB, (CONV_C + 1) * SUB, (CONV_C + 2) * SUB

    def body(dc_ref, h_ref, hh_ref, w_ref, dh_ref, dw_ref, db_ref, db1_ref, ext_u, ext_d, acc):
        j = pl.program_id(1)
        jj = nch - 1 - j

        @pl.when(j == 0)
        def _():
            ext_d[:, R:R + H, :] = jnp.zeros((nlt, H, LANE), F32)
            acc[...] = jnp.zeros_like(acc)

        for lt in range(nlt):
            ext_u[lt, 0:H, :] = jnp.where(
                jj == 0, 0.0, hh_ref[:, lt * LANE:(lt + 1) * LANE].astype(F32)
                * _sigmoid(hh_ref[:, cw + lt * LANE:cw + (lt + 1) * LANE].astype(F32)))

        def stage(r0, lt):
            rs, ls = pl.ds(r0, _RB), _lanes(lt)
            gate = h_ref[rs, _lanes(lt + nlt)].astype(F32)
            ext_u[lt, pl.ds(pl.multiple_of(r0 + H, SUB), _RB), :] = h_ref[rs, ls].astype(F32) * _sigmoid(gate)
            ext_d[lt, rs, :] = dc_ref[rs, ls]

        def first(r0, lt):
            ls = _lanes(lt)
            dc = dc_ref[pl.ds(r0, _RB), ls]
            acc[a_b:a_b + SUB, ls] += _psum8(dc)
            for k in range(CONV_C):
                tap = ext_u[lt, pl.ds(r0 + (H - (CONV_C - 1 - k)), _RB), :]
                acc[k * SUB:(k + 1) * SUB, ls] += _psum8(dc * tap)

        def second(r0, lt):
            rs, ls, lg = pl.ds(r0, _RB), _lanes(lt), _lanes(lt + nlt)
            du = w_ref[CONV_C - 1:CONV_C, ls] * ext_d[lt, rs, :]
            for k in range(CONV_C - 1):
                du = du + w_ref[k:k + 1, ls] * ext_d[lt, pl.ds(r0 + (CONV_C - 1 - k), _RB), :]
            val = h_ref[rs, ls].astype(F32)
            sg = _sigmoid(h_ref[rs, lg].astype(F32))
            dval = du * sg
            dgate = du * val * sg * (1.0 - sg)
            acc[a_val:a_val + SUB, ls] += _psum8(dval)
            acc[a_gate:a_gate + SUB, ls] += _psum8(dgate)
            dh_ref[rs, ls] = dval.astype(BF16)
            dh_ref[rs, lg] = dgate.astype(BF16)

        _sub_blocks(R, cw, stage)
        _sub_blocks(R, cw, first)
        _sub_blocks(R, cw, second)
        ext_d[:, R:R + H, :] = ext_d[:, 0:H, :]

        @pl.when(j == nch - 1)
        def _():
            for k in range(CONV_C):
                dw_ref[k:k + 1, :] = jnp.sum(acc[k * SUB:(k + 1) * SUB, :], axis=0, keepdims=True)
            db_ref[...] = jnp.sum(acc[a_b:a_b + SUB, :], axis=0, keepdims=True)
            db1_ref[:, 0:cw] = jnp.sum(acc[a_val:a_val + SUB, :], axis=0, keepdims=True)
            db1_ref[:, cw:2 * cw] = jnp.sum(acc[a_gate:a_gate + SUB, :], axis=0, keepdims=True)

    rows = lambda c, j: (nch - 1 - j, c)
    return pl.pallas_call(
        body, out_shape=[SDS((S, 2 * D), BF16), SDS((CONV_C, D), F32), SDS((1, D), F32), SDS((1, 2 * D), F32)],
        grid=(D // cw, nch),
        in_specs=[pl.BlockSpec((R, cw), rows), pl.BlockSpec((R, 2 * cw), rows),
                  pl.BlockSpec((H, 2 * cw), lambda c, j: (jnp.maximum((nch - 1 - j) * (R // H) - 1, 0), c)),
                  pl.BlockSpec((CONV_C, cw), lambda c, j: (0, c))],
        out_specs=[pl.BlockSpec((R, 2 * cw), rows), pl.BlockSpec((CONV_C, cw), lambda c, j: (0, c)),
                   pl.BlockSpec((1, cw), lambda c, j: (0, c)), pl.BlockSpec((1, 2 * cw), lambda c, j: (0, c))],
        scratch_shapes=[pltpu.VMEM((nlt, H + R, LANE), F32), pltpu.VMEM((nlt, R + H, LANE), F32),
                        pltpu.VMEM(((CONV_C + 3) * SUB, cw), F32)], name="conf_conv_bwd",
        compiler_params=_cp(2),
    )(dcv, h1p, h1p, w)


def _local_step(x, mem, tgt, W, fetch=None, send=None):
    G = {}
    W = dict(W)

    def arrive(group, after):
        if fetch is None:
            return None
        got, tok = fetch(group, after)
        for key, val in got.items():
            W[key] = {**W.get(key, {}), **val} if isinstance(val, dict) else val
        return tok

    def gain(g, tok):
        return g if tok is None else g + tok

    def sent(group):
        return None if send is None else send(group, G)

    def xattn_fwd(xin, n, l):
        tok = arrive(("xa", l), n)
        mn = _rms_fwd(mem, gain(W["xa_mem_norm"][l:l + 1], tok), f"xa_memnorm_fwd{l}")
        q = _mm_nn(n, W["xa_wq"][l], out_dtype=BF16, name=f"xa_q{l}")
        k = _mm_nn(mn, W["xa_wk"][l], out_dtype=BF16, name=f"xa_k{l}")
        v = _mm_nn(mn, W["xa_wv"][l], out_dtype=BF16, name=f"xa_v{l}")
        o = _attn_fwd(q, k, v, f"xa_attn_fwd{l}")
        xout, nout = _mm_nn(o, W["xa_wo"][l], out_dtype=F32, name=f"xa_o{l}", add=xin, norm=W["f_norm"][l:l + 1])
        return xout, nout, (xin, n, q, mn, k, v, o)

    def xattn_bwd(dx, dxb, saved, l):
        xin, n, q, mn, k, v, o = saved
        do = _mm_nt(dxb, W["xa_wo"][l], out_dtype=BF16, name=f"xa_do{l}")
        G[f"xa_wo{l}"] = _mm_tn(o, dxb, out_dtype=BF16, name=f"xa_dwo{l}")
        dq, dk, dv = _attn_bwd(q, k, v, do, f"xa_attn_bwd{l}")
        dkb, dvb = dk.astype(BF16), dv.astype(BF16)
        G[f"xa_wq{l}"] = _mm_tn(n, dq, out_dtype=BF16, name=f"xa_dwq{l}")
        G[f"xa_wk{l}"] = _mm_tn(mn, dkb, out_dtype=BF16, name=f"xa_dwk{l}")
        G[f"xa_wv{l}"] = _mm_tn(mn, dvb, out_dtype=BF16, name=f"xa_dwv{l}")
        tok = sent(("xa", l))
        dmn = _mm_nt(dkb, W["xa_wk"][l], out_dtype=F32, name=f"xa_dmn_k{l}")
        dmn = _mm_nt(dvb, W["xa_wv"][l], out_dtype=F32, name=f"xa_dmn_v{l}", add=dmn)
        (G[f"xa_mem_norm{l}"],) = _rms_bwd(mem, W["xa_mem_norm"][l:l + 1], dmn, None, f"xa_memnorm_bwd{l}")
        dx, dxb, G[f"xa_norm{l}"] = _mm_nt(dq, W["xa_wq"][l], out_dtype=F32, name=f"xa_dn{l}",
                                           rms=(xin, gain(W["xa_norm"][l:l + 1], tok), dx))
        return dx, dxb

    def ffn_fwd(xin, n, l, next_gain):
        tok = arrive(("f", l), n)
        hp = _mm_nn(n, W["f_w_up"][l], out_dtype=BF16, name=f"f_up{l}")
        act = _f_fwd(hp, W["f_dw_w"][l], gain(W["f_dw_b"][l:l + 1], tok), f"f_conv_fwd{l}")
        res = _mm_nn(act, W["f_w_down"][l], out_dtype=F32, name=f"f_down{l}", add=xin, norm=next_gain)
        xout, nout = res if next_gain is not None else (res, None)
        return xout, nout, (xin, n, hp, act)

    def ffn_bwd(dx, dxb, saved, l):
        xin, n, hp, act = saved
        dact = _mm_nt(dxb, W["f_w_down"][l], out_dtype=BF16, name=f"f_dact{l}")
        G[f"f_w_down{l}"] = _mm_tn(act, dxb, out_dtype=BF16, name=f"f_dwdown{l}")
        dhp, G[f"f_dw_w{l}"], G[f"f_dw_b{l}"] = _f_bwd(dact, hp, W["f_dw_w"][l], W["f_dw_b"][l:l + 1], f"f_conv_bwd{l}")
        G[f"f_w_up{l}"] = _mm_tn(n, dhp, out_dtype=BF16, name=f"f_dwup{l}", blocks=_CW_F)
        tok = sent(("f", l))
        dx, dxb, G[f"f_norm{l}"] = _mm_nt(dhp, W["f_w_up"][l], out_dtype=F32, name=f"f_dn{l}",
                                          rms=(xin, gain(W["f_norm"][l:l + 1], tok), dx))
        return dx, dxb

    n0 = _rms_fwd(x, W["ab_norm"], "ab_norm_fwd")
    tok = arrive(("ab", 0), n0)
    a_par = (W["a_conv_w"], gain(W["a_conv_b"], tok), W["a_gate_x_w"], W["a_gate_x_b"], W["a_gate_a_w"],
             W["a_gate_a_b"], W["a_lambda"])
    b_par = (W["b_group_w"], W["b_group_b"], W["b_scale"])
    zp = _mm_nn(n0, W["ab_w_in"], out_dtype=BF16, name="ab_in")
    yab, h_a = _a_fwd(zp, *a_par)
    yab = _b_fwd(zp, yab, *b_par)
    arrive(("ab", 1), yab)
    x1, n1 = _mm_nn(yab, W["ab_w_out"], out_dtype=F32, name="ab_out", add=x, norm=W["xa_norm"][0:1])
    x2, n2, s_xa0 = xattn_fwd(x1, n1, 0)
    x3, n3, s_f0 = ffn_fwd(x2, n2, 0, W["c_norm"])
    tok = arrive(("c", 0), n3)
    h1p = _mm_nn(n3, W["c_w_pw1"], out_dtype=BF16, name="c_pw1", bias=gain(W["c_b_pw1"], tok))
    cv = _c_fwd(h1p, W["c_dw_w"], W["c_dw_b"])
    sc = _ln_silu_fwd(cv, W["c_ln_g"], W["c_ln_b"])
    x4, n4 = _mm_nn(sc, W["c_w_pw2"], out_dtype=F32, name="c_pw2", bias=W["c_b_pw2"], add=x3, norm=W["xa_norm"][1:2])
    x5, n5, s_xa1 = xattn_fwd(x4, n4, 1)
    x6, _, s_f1 = ffn_fwd(x5, n5, 1, None)
    loss, dx, dxb, G["final_norm"] = _loss_head(x6, W["final_norm"], tgt)

    dx, dxb = ffn_bwd(dx, dxb, s_f1, 1)
    dx, dxb = xattn_bwd(dx, dxb, s_xa1, 1)
    dsc = _mm_nt(dxb, W["c_w_pw2"], out_dtype=BF16, name="c_dsc")
    G["c_w_pw2"] = _mm_tn(sc, dxb, out_dtype=BF16, name="c_dwpw2")
    dcv, G["c_ln_g"], G["c_ln_b"], G["c_b_pw2"] = _ln_silu_bwd(dsc, cv, W["c_ln_g"], W["c_ln_b"], dx)
    dh1p, G["c_dw_w"], G["c_dw_b"], G["c_b_pw1"] = _c_bwd(dcv, h1p, W["c_dw_w"])
    G["c_w_pw1"] = _mm_tn(n3, dh1p, out_dtype=BF16, name="c_dwpw1", blocks=_CW_C)
    tok = sent(("c", 0))
    dx, dxb, G["c_norm"] = _mm_nt(dh1p, W["c_w_pw1"], out_dtype=F32, name="c_dn",
                                  rms=(x3, gain(W["c_norm"], tok), dx))
    dx, dxb = ffn_bwd(dx, dxb, s_f0, 0)
    dx, dxb = xattn_bwd(dx, dxb, s_xa0, 0)
    dyab = _mm_nt(dxb, W["ab_w_out"], out_dtype=BF16, name="ab_dyab")
    G["ab_w_out"] = _mm_tn(yab, dxb, out_dtype=BF16, name="ab_dwout")
    tok = sent(("ab", 1))
    a_par = (a_par[0], gain(a_par[1], tok)) + a_par[2:]
    (dzg, dzr, G["a_conv_w"], G["a_conv_b"], G["a_gate_x_w"], G["a_gate_x_b"], G["a_gate_a_w"], G["a_gate_a_b"],
     G["a_lambda"]) = _a_bwd(dyab, zp, h_a, *a_par)
    dzq, G["b_group_w"], G["b_group_b"], G["b_scale"] = _b_bwd(dyab, zp, *b_par)
    G["ab_w_in"] = jnp.concatenate(
        [_mm_tn(n0, dz, out_dtype=BF16, name=f"ab_dwin_{part}")
         for part, dz in (("gate", dzg), ("rec", dzr), ("pool", dzq))], axis=1)
    tok = sent(("ab", 0))
    dx, _, G["ab_norm"] = _mm_nt_cols([dzg, dzr, dzq], W["ab_w_in"], name="ab_dn",
                                      rms=(x, gain(W["ab_norm"], tok), dx))
    return loss, dx, G


def _my_place():
    x, y, c = lax.axis_index("x"), lax.axis_index("y"), lax.axis_index("c")
    return x, y, c


def _all_gather(shards, name):
    n = len(shards)

    def body(*refs):
        ins, outs = refs[:n], refs[n:2 * n]
        send_sems, recv_sems, local_sems = refs[2 * n:]
        x, y, c = _my_place()
        me, sibling = (x, y, c), (x, y, 1 - c)
        chips = [(1 - x, y), (x, 1 - y), (1 - x, 1 - y)]

        def slab(a, place):
            px, py, pc = place
            return outs[a].at[4 * px + 2 * py + pc]

        def copy(a, k, block, to, src=None):
            return pltpu.make_async_remote_copy(
                src_ref=slab(a, block) if src is None else src, dst_ref=slab(a, block),
                send_sem=send_sems.at[a, k], recv_sem=recv_sems.at[a, k], device_id=to, device_id_type=MESH)

        mine = [pltpu.make_async_copy(ins[a], slab(a, me), local_sems.at[a]) for a in range(n)]
        for cp in mine:
            cp.start()
        first = []
        for j, chip in enumerate(chips):
            first += [copy(a, 1 + j, me, (*chip, c), src=ins[a]) for a in range(n)]
        first += [copy(a, 0, me, sibling, src=ins[a]) for a in range(n)]
        for cp in first:
            cp.start()
        passed = []
        for j, chip in enumerate(chips):
            for a in range(n):
                copy(a, 1 + j, (*chip, c), me).wait_recv()
                cp = copy(a, 4 + j, (*chip, c), sibling)
                cp.start()
                passed.append(cp)
        for a in range(n):
            copy(a, 0, sibling, me).wait_recv()
        for j, chip in enumerate(chips):
            for a in range(n):
                copy(a, 4 + j, (*chip, 1 - c), me).wait_recv()
        for cp in first + passed:
            cp.wait_send()
        for cp in mine:
            cp.wait()

    any_spec = pl.BlockSpec(memory_space=pl.ANY)
    return pl.pallas_call(
        body, out_shape=[SDS((N_DEV,) + s.shape, s.dtype) for s in shards], in_specs=[any_spec] * n,
        out_specs=[any_spec] * n,
        scratch_shapes=[pltpu.SemaphoreType.DMA((n, 7)), pltpu.SemaphoreType.DMA((n, 7)), pltpu.SemaphoreType.DMA((n,))],
        name=name,
    )(*shards)


_HBM = pl.BlockSpec(memory_space=pltpu.HBM)
_SEM = pl.BlockSpec(memory_space=pltpu.SEMAPHORE)
_EFFECT = pltpu.SideEffectType.DATAFLOW_SIDE_EFFECTING


def _peer_places():
    x, y, c = _my_place()
    peers = []
    for k in range(1, N_DEV):
        px = 1 - x if (k >> 2) & 1 else x
        py = 1 - y if (k >> 1) & 1 else y
        pc = 1 - c if k & 1 else c
        peers.append(((px, py, pc), 4 * px + 2 * py + pc))
    return (x, y, c), 4 * x + 2 * y + c, peers


def _send_start(srcs, per_dest, name):
    n = len(srcs)
    lands = [lax.empty((N_DEV,) + (s.shape[1:] if per_dest else s.shape), s.dtype) for s in srcs]

    def body(*refs):
        src, land = refs[:n], refs[n:2 * n]
        outs = refs[2 * n:]
        send, recv, token = outs[:n], outs[n:2 * n], outs[4 * n]
        _, me, peers = _peer_places()
        for a in range(n):
            for peer, pidx in peers:
                pltpu.make_async_remote_copy(
                    src_ref=src[a].at[pidx] if per_dest else src[a], dst_ref=land[a].at[me], send_sem=send[a],
                    recv_sem=recv[a], device_id=peer, device_id_type=MESH).start()
        token[...] = jnp.zeros_like(token)

    hbm = lambda a: pltpu.HBM(a.shape, a.dtype)
    sem = pltpu.SemaphoreType.DMA(())
    res = pl.pallas_call(
        body, name=name,
        out_shape=tuple([sem] * (2 * n) + [hbm(s) for s in srcs] + [hbm(l) for l in lands]
                        + [SDS((SUB, LANE), F32)]),
        in_specs=[_HBM] * (2 * n),
        out_specs=tuple([_SEM] * (2 * n) + [_HBM] * (2 * n) + [pl.BlockSpec(memory_space=pltpu.VMEM)]),
        input_output_aliases={i: 2 * n + i for i in range(2 * n)},
        compiler_params=pltpu.CompilerParams(has_side_effects=_EFFECT),
    )(*[pltpu.with_memory_space_constraint(s, pltpu.HBM) for s in srcs],
      *[pltpu.with_memory_space_constraint(l, pltpu.HBM) for l in lands])
    return res[:n], res[n:2 * n], res[2 * n:3 * n], res[3 * n:4 * n], res[4 * n]


def _send_wait(send, recv, srcs, lands, after, per_dest, name):
    n = len(srcs)

    def body(*refs):
        src, land = refs[:n], refs[n:2 * n]
        send_s, recv_s = refs[2 * n:3 * n], refs[3 * n:4 * n]
        token = refs[-1]
        place, _, _ = _peer_places()
        for a in range(n):
            seven = land[a].at[pl.ds(0, N_DEV - 1)]
            copy = pltpu.make_async_remote_copy(
                src_ref=src[a].at[pl.ds(0, N_DEV - 1)] if per_dest else seven, dst_ref=seven, send_sem=send_s[a],
                recv_sem=recv_s[a], device_id=place, device_id_type=MESH)
            copy.wait_send()
            copy.wait_recv()
        token[...] = jnp.zeros_like(token)

    hbm = lambda a: pltpu.HBM(a.shape, a.dtype)
    res = pl.pallas_call(
        body, name=name,
        out_shape=tuple([hbm(s) for s in srcs] + [hbm(l) for l in lands] + [SDS((SUB, LANE), F32)]),
        in_specs=[_HBM] * (2 * n) + [_SEM] * (2 * n) + [pl.BlockSpec(memory_space=pl.ANY)],
        out_specs=tuple([_HBM] * (2 * n) + [pl.BlockSpec(memory_space=pltpu.VMEM)]),
        input_output_aliases={i: i for i in range(2 * n)},
        compiler_params=pltpu.CompilerParams(has_side_effects=_EFFECT),
    )(*srcs, *lands, *send, *recv, after)
    return res[:n], res[n:2 * n], res[2 * n]


def _adamw_math(w, g, m, v):
    m = ADAM_B1 * m + (1.0 - ADAM_B1) * g
    v = ADAM_B2 * v + (1.0 - ADAM_B2) * (g * g)
    m_hat = m / (1.0 - ADAM_B1 ** ADAM_STEP)
    v_hat = v / (1.0 - ADAM_B2 ** ADAM_STEP)
    delta = -ADAM_LR * (m_hat / (jnp.sqrt(v_hat) + ADAM_EPS) + ADAM_WD * w)
    return delta, m, v


def _row_tile(r, c, itemsize_rows):
    cap = max(SUB, (itemsize_rows // (4 * c)) // SUB * SUB)
    if r <= cap:
        return r
    best = None
    for t in range(SUB, cap + 1, SUB):
        if r % t == 0:
            best = t
    return best if best is not None else r


def _sum_adamw(landing, w, m, v, name, layer=0, prev=None):
    _, r, c = landing.shape
    tr = _row_tile(r, c, 1 << 20)
    off = layer * (r // tr)

    def body(l_ref, w_ref, m_ref, v_ref, *rest):
        g_ref, d_ref, mo_ref, vo_ref = rest[-4:]
        g = l_ref[0].astype(F32)
        for s in range(1, N_DEV):
            g = g + l_ref[s].astype(F32)
        g_ref[...] = g
        d_ref[...], mo_ref[...], vo_ref[...] = _adamw_math(w_ref[...], g, m_ref[...], v_ref[...])

    blk = pl.BlockSpec((tr, c), lambda i: (i + off, 0))
    n_prev = 0 if prev is None else 4
    return pl.pallas_call(
        body, out_shape=[SDS(w.shape, F32)] * 4, grid=(r // tr,),
        in_specs=[pl.BlockSpec((N_DEV, tr, c), lambda i: (0, i, 0)), blk, blk, blk]
        + [pl.BlockSpec(memory_space=pl.ANY)] * n_prev,
        out_specs=[blk] * 4, input_output_aliases={4 + i: i for i in range(n_prev)}, name=name,
        compiler_params=_cp(1),
    )(landing, w, m, v, *([] if prev is None else prev))


def _sum8(landing, name):
    _, r, c = landing.shape

    def body(l_ref, g_ref):
        g = l_ref[0]
        for s in range(1, N_DEV):
            g = g + l_ref[s]
        g_ref[...] = g

    return pl.pallas_call(body, out_shape=SDS((r, c), F32), name=name, compiler_params=_cp(0))(landing)


def _adamw_small(repl_pack, own_pack, P, M, V):
    table, off = [], 0
    for name, shape in _REPL.items():
        table.append((name, shape if len(shape) > 1 else (1,) + shape, 0, off // LANE))
        off += _size(shape)
    off = _REPL_ROWS * LANE
    for name, shape in _SMALL_SHARDED.items():
        table.append((name, shape, 1, off // LANE))
        off += _size(shape)
    n = len(table)

    def body(*refs):
        packs, ins, outs = refs[:2], refs[2:2 + 3 * n], refs[2 + 3 * n:]
        for p, (_, shape, which, r0) in enumerate(table):
            w_ref, m_ref, v_ref = ins[3 * p:3 * p + 3]
            g_ref, d_ref, mo_ref, vo_ref = outs[4 * p:4 * p + 4]
            pack, rows, q = packs[which], shape[-2], shape[-1] // LANE
            lead = [()]
            for dim in shape[:-2]:
                lead = [t + (i,) for t in lead for i in range(dim)]
            for li, idx in enumerate(lead):
                if q == 1:
                    dst = g_ref.at[idx] if idx else g_ref
                    dst[...] = pack[r0 + li * rows:r0 + (li + 1) * rows, :]
                    continue
                for i in range(rows):
                    for k in range(q):
                        row = r0 + (li * rows + i) * q + k
                        g_ref[idx + (slice(i, i + 1), slice(k * LANE, (k + 1) * LANE))] = pack[row:row + 1, :]
            d_ref[...], mo_ref[...], vo_ref[...] = _adamw_math(w_ref[...], g_ref[...], m_ref[...], v_ref[...])

    ins, out_shape = [], []
    for name, shape, _, _ in table:
        ins += [t[name].reshape(shape) for t in (P, M, V)]
        out_shape += [SDS(shape, F32)] * 4
    res = pl.pallas_call(body, out_shape=out_shape, name="adamw_small", compiler_params=_cp(0))(
        repl_pack, own_pack, *ins)
    dicts = ({}, {}, {}, {})
    for p, (name, shape, _, _) in enumerate(table):
        for d, arr in zip(dicts, res[4 * p:4 * p + 4]):
            d[name] = arr.reshape(P[name].shape)
    return dicts


_BIG = {
    "ab_w_in": (1, D, 320), "ab_w_out": (1, 192, D), "c_w_pw1": (1, D, 256), "c_w_pw2": (1, 128, D),
    "xa_wq": (2, 128, D), "xa_wk": (2, 128, D), "xa_wv": (2, 128, D), "xa_wo": (2, 128, D),
    "f_w_up": (2, D, 768), "f_w_down": (2, 384, D),
}
_SMALL_SHARDED = {
    "a_conv_w": (1, 4, 128), "c_norm": (1, 128), "c_b_pw1": (1, 256), "c_dw_w": (1, 31, 128), "c_dw_b": (1, 128),
    "c_ln_g": (1, 128), "c_ln_b": (1, 128), "c_b_pw2": (1, 128), "f_dw_w": (2, 3, 384),
}
_REPL = {
    "ab_norm": (1, D), "a_conv_b": (1, D), "a_gate_x_w": (1, 8, 128, 128), "a_gate_x_b": (1, D),
    "a_gate_a_w": (1, 8, 128, 128), "a_gate_a_b": (1, D), "a_lambda": (1, D), "b_group_w": (1, 4, 128, 128),
    "b_group_b": (1, 512), "b_scale": (1, 512), "xa_norm": (2, D), "xa_mem_norm": (2, D), "f_norm": (2, D),
    "f_dw_b": (2, D_FF), "final_norm": (D,),
}


def _size(shape):
    n = 1
    for s in shape:
        n *= s
    return n


_N_SS = sum(_size(s) for s in _SMALL_SHARDED.values())
_N_REPL = sum(_size(s) for s in _REPL.values())
_REPL_ROWS = -(-_N_REPL // (N_DEV * SUB * LANE)) * SUB
_SS_ROWS = _N_SS // LANE
_SMALL_ROWS = -(-(_REPL_ROWS + _SS_ROWS) // SUB) * SUB


def _pack(parts, rows):
    flat = jnp.concatenate([p.reshape(-1).astype(F32) for p in parts])
    return jnp.pad(flat, (0, rows * LANE - flat.shape[0])).reshape(rows, LANE)


def _pair_blocks(v, bw):
    lead, n = v.shape[:-1], v.shape[-1]
    return jnp.swapaxes(v.reshape(lead + (2, n // (2 * bw), bw)), -3, -2).reshape(lead + (n,))


def _unpair_blocks(v, bw):
    lead, n = v.shape[:-1], v.shape[-1]
    return jnp.swapaxes(v.reshape(lead + (n // (2 * bw), 2, bw)), -3, -2).reshape(lead + (n,))


_GROUPS = {
    ("ab", 0): (("ab_w_in", 0),),
    ("ab", 1): (("ab_w_out", 0),),
    ("xa", 0): (("xa_wq", 0), ("xa_wk", 0), ("xa_wv", 0), ("xa_wo", 0)),
    ("f", 0): (("f_w_up", 0), ("f_w_down", 0)),
    ("c", 0): (("c_w_pw1", 0), ("c_w_pw2", 0)),
    ("xa", 1): (("xa_wq", 1), ("xa_wk", 1), ("xa_wv", 1), ("xa_wo", 1)),
    ("f", 1): (("f_w_up", 1), ("f_w_down", 1)),
}
_SEND_GROUPS = _GROUPS


def _weight_layout(name, g):
    if name == "ab_w_in":
        return jnp.swapaxes(g, 0, 1).reshape(D, N_DEV * 320)
    if name in ("c_w_pw1", "f_w_up"):
        return g
    return g.reshape(N_DEV * g.shape[1], D)


def _grad_blocks(name, l, G):
    _, r, c = _BIG[name]
    if name == "ab_w_in":
        return jnp.swapaxes(G[name].reshape(D, N_DEV, 320), 0, 1)
    if name == "c_w_pw1":
        return G[name]
    if name == "f_w_up":
        return G[f"{name}{l}"]
    return (G[name] if _BIG[name][0] == 1 else G[f"{name}{l}"]).reshape(N_DEV, r, c)


def _small_layouts(sm):
    W = {}
    sm = sm.reshape(N_DEV, -1)
    off = 0
    for name, shape in _SMALL_SHARDED.items():
        n = _size(shape)
        blocks = sm[:, off:off + n].reshape((N_DEV,) + shape)
        off += n
        W[name] = jnp.moveaxis(blocks, 0, -2).reshape(shape[:-1] + (N_DEV * shape[-1],))
    W["a_conv_w"], W["c_dw_w"] = W["a_conv_w"][0], W["c_dw_w"][0]
    W["c_b_pw1"] = _pair_blocks(W["c_b_pw1"], _CW_C)
    return W


def _with_own(land, src, me, per_dest):
    own = lax.dynamic_slice_in_dim(src, me, 1, 0) if per_dest else src[None]
    return lax.dynamic_update_slice_in_dim(land, own, me, 0)


def _to_dest_major(g, shape):
    full = g.reshape(shape[:-1] + (N_DEV, shape[-1]))
    return jnp.moveaxis(full, -2, 0).reshape(N_DEV, -1)


def kernel(x, mem, ab_norm, ab_w_in, a_conv_w, a_conv_b, a_gate_x_w, a_gate_x_b, a_gate_a_w, a_gate_a_b, a_lambda, b_group_w, b_group_b, b_scale, ab_w_out, c_norm, c_w_pw1, c_b_pw1, c_dw_w, c_dw_b, c_ln_g, c_ln_b, c_w_pw2, c_b_pw2, xa_norm, xa_mem_norm, xa_wq, xa_wk, xa_wv, xa_wo, f_norm, f_w_up, f_dw_w, f_dw_b, f_w_down, final_norm, loss_target, m_ab_norm, m_ab_w_in, m_a_conv_w, m_a_conv_b, m_a_gate_x_w, m_a_gate_x_b, m_a_gate_a_w, m_a_gate_a_b, m_a_lambda, m_b_group_w, m_b_group_b, m_b_scale, m_ab_w_out, m_c_norm, m_c_w_pw1, m_c_b_pw1, m_c_dw_w, m_c_dw_b, m_c_ln_g, m_c_ln_b, m_c_w_pw2, m_c_b_pw2, m_xa_norm, m_xa_mem_norm, m_xa_wq, m_xa_wk, m_xa_wv, m_xa_wo, m_f_norm, m_f_w_up, m_f_dw_w, m_f_dw_b, m_f_w_down, m_final_norm, v_ab_norm, v_ab_w_in, v_a_conv_w, v_a_conv_b, v_a_gate_x_w, v_a_gate_x_b, v_a_gate_a_w, v_a_gate_a_b, v_a_lambda, v_b_group_w, v_b_group_b, v_b_scale, v_ab_w_out, v_c_norm, v_c_w_pw1, v_c_b_pw1, v_c_dw_w, v_c_dw_b, v_c_ln_g, v_c_ln_b, v_c_w_pw2, v_c_b_pw2, v_xa_norm, v_xa_mem_norm, v_xa_wq, v_xa_wk, v_xa_wv, v_xa_wo, v_f_norm, v_f_w_up, v_f_dw_w, v_f_dw_b, v_f_w_down, v_final_norm):
    args = dict(locals())
    P = {n: args[n] for n in _NAMES}
    M = {n: args["m_" + n] for n in _NAMES}
    V = {n: args["v_" + n] for n in _NAMES}

    me = 4 * lax.axis_index("x") + 2 * lax.axis_index("y") + lax.axis_index("c")

    in_flight = {}

    def launch(groups, tok):
        shards, n_of = [], {}
        for grp in groups:
            for name, l in _GROUPS[grp]:
                w = P[name][l] if tok is None else P[name][l] + tok
                shards.append(w.astype(BF16))
            if grp == ("ab", 0):
                shards.append(_pack([P[n] for n in _SMALL_SHARDED], _SS_ROWS + 4))
            n_of[grp] = len(shards)
        res = _send_start(shards, False, "gather_start_" + "_".join(g[0] + str(g[1]) for g in groups))
        lo = 0
        for grp in groups:
            in_flight[grp] = [r[lo:n_of[grp]] for r in res[:4]]
            lo = n_of[grp]
        return res[4][:1, :1]

    follow = {("ab", 0): [("ab", 1), ("xa", 0), ("f", 0)], ("xa", 0): [("c", 0)], ("f", 0): [("xa", 1)],
              ("c", 0): [("f", 1)]}

    def fetch(grp, after):
        send_s, recv_s, srcs, lands = in_flight.pop(grp)
        srcs, lands, tok = _send_wait(send_s, recv_s, srcs, lands, after, False, f"gather_wait_{grp[0]}{grp[1]}")
        tok = launch(follow[grp], tok[:1, :1]) if grp in follow else None
        full = [_with_own(land, src, me, False) for land, src in zip(lands, srcs)]
        out = {}
        for (name, l), g in zip(_GROUPS[grp], full):
            w = _weight_layout(name, g)
            if _BIG[name][0] == 1:
                out[name] = w
            else:
                out[name] = {l: w}
        if grp == ("ab", 0):
            out.update(_small_layouts(full[-1]))
        return out, tok

    zero = launch([("ab", 0)], None)

    pending = []

    def send(grp, G):
        members = _SEND_GROUPS[grp]
        res = _send_start([_grad_blocks(name, l, G) for name, l in members], True, f"send_{grp[0]}{grp[1]}")
        pending.append((members, res))
        return res[4][:1, :1]

    W = {n: P[n] for n in _REPL}
    W["ab_norm"] = P["ab_norm"] + zero
    W["final_norm"] = P["final_norm"].reshape(1, D)
    W["a_gate_x_w"], W["a_gate_a_w"], W["b_group_w"] = P["a_gate_x_w"][0], P["a_gate_a_w"][0], P["b_group_w"][0]
    loss, grad_x, G = _local_step(x[0], mem[0], loss_target[0], W, fetch, send)
    loss = lax.psum(loss[0, 0], ("x", "y", "c"))

    Gs = dict(G)
    Gs["c_b_pw1"] = _unpair_blocks(G["c_b_pw1"], _CW_C)
    Gs["f_dw_w"] = jnp.stack([G["f_dw_w0"], G["f_dw_w1"]])
    Gs["a_conv_w"], Gs["c_dw_w"] = G["a_conv_w"][None], G["c_dw_w"][None]
    for n in ("xa_norm", "xa_mem_norm", "f_norm", "f_dw_b"):
        Gs[n] = jnp.concatenate([G[f"{n}0"], G[f"{n}1"]], axis=0)
    for n in ("a_gate_x_w", "a_gate_a_w", "b_group_w"):
        Gs[n] = G[n][None]
    repl_flat = jnp.concatenate([Gs[n].reshape(-1) for n in _REPL])
    repl_rows = jnp.pad(repl_flat, (0, N_DEV * _REPL_ROWS * LANE - _N_REPL)).reshape(N_DEV, _REPL_ROWS, LANE)
    ss_rows = jnp.concatenate([_to_dest_major(Gs[n], s) for n, s in _SMALL_SHARDED.items()], axis=1)
    ss_rows = ss_rows.reshape(N_DEV, _SS_ROWS, LANE)
    small_pack = jnp.concatenate(
        [repl_rows, ss_rows, jnp.zeros((N_DEV, _SMALL_ROWS - _REPL_ROWS - _SS_ROWS, LANE), F32)], axis=1)
    last = _send_start([small_pack], True, "send_small")
    pending.append(((("small", 0),), last))

    members = [m for mem_, _ in pending for m in mem_]
    cat = [[a for _, res in pending for a in res[i]] for i in range(4)]
    srcs, lands, _ = _send_wait(cat[0], cat[1], cat[2], cat[3], grad_x, True, "send_wait")
    landed = {m: _with_own(land, src, me, True) for m, land, src in zip(members, lands, srcs)}

    out_g, out_d, out_m, out_v = {}, {}, {}, {}
    for name, (layers, r, c) in _BIG.items():
        shape = P[name].shape
        w2, m2, v2 = [t[name].reshape(layers * r, c) for t in (P, M, V)]
        res = None
        for l in range(layers):
            res = _sum_adamw(landed[(name, l)], w2, m2, v2, f"adamw_{name}{l}", layer=l, prev=res)
        out_g[name], out_d[name], out_m[name], out_v[name] = [t.reshape(shape) for t in res]

    small_sum = _sum8(landed[("small", 0)], "sum_small")
    (repl_all,) = _all_gather([small_sum[:_REPL_ROWS]], "gather_small_grads")
    for out, got in zip((out_g, out_d, out_m, out_v),
                        _adamw_small(repl_all.reshape(N_DEV * _REPL_ROWS, LANE), small_sum, P, M, V)):
        out.update(got)

    return (loss, grad_x[None], *[out_g[n] for n in _NAMES], *[out_d[n] for n in _NAMES],
            *[out_m[n] for n in _NAMES], *[out_v[n] for n in _NAMES])


_NAMES = ("ab_norm", "ab_w_in", "a_conv_w", "a_conv_b", "a_gate_x_w", "a_gate_x_b", "a_gate_a_w", "a_gate_a_b",
          "a_lambda", "b_group_w", "b_group_b", "b_scale", "ab_w_out", "c_norm", "c_w_pw1", "c_b_pw1", "c_dw_w",
          "c_dw_b", "c_ln_g", "c_ln_b", "c_w_pw2", "c_b_pw2", "xa_norm", "xa_mem_norm", "xa_wq", "xa_wk", "xa_wv",
          "xa_wo", "f_norm", "f_w_up", "f_dw_w", "f_dw_b", "f_w_down", "final_norm")
```

```python
import functools

import jax
import jax.numpy as jnp
from jax import lax
from jax.experimental import pallas as pl
from jax.experimental.pallas import tpu as pltpu

F32, BF16 = jnp.float32, jnp.bfloat16
SDS = jax.ShapeDtypeStruct
MESH = pl.DeviceIdType.MESH

N_DEV = 8
D = 1024
N_MEM = 256
XA_HEADS, XA_HD = 4, 256
HD_A = 128
CONV_A, CONV_C, CONV_F = 4, 31, 3
C_RG = 8.0
POOL_WINDOWS = (2, 4, 8, 16)
D_FF = 3 * D
EPS = 1e-6
ADAM_LR, ADAM_B1, ADAM_B2, ADAM_EPS, ADAM_WD, ADAM_STEP = 0.001, 0.9, 0.999, 1e-08, 0.01, 10

LANE = 128
SUB = 8
VMEM_LIMIT = 56 * 1024 * 1024
R_SEQ = 256
TM_ROW = 512


def _cp(n_axes):
    return pltpu.CompilerParams(dimension_semantics=("arbitrary",) * n_axes, vmem_limit_bytes=VMEM_LIMIT)


def _tile(n, pref):
    if n <= pref:
        return n
    best = None
    for t in range(LANE, pref + 1, LANE):
        if n % t == 0:
            best = t
    assert best is not None, (n, pref)
    return best


def _perm2(n):
    return (n % 2) * 4 + n // 2


_NN = (((1,), (0,)), ((), ()))
_NT = (((1,), (1,)), ((), ()))
_TN = (((0,), (0,)), ((), ()))


def _mm_call(name, grid, ab, ab_specs, dims, acc_shape, extras, outs, finish, from_ref=False):
    nk = grid[2]
    n_ab, n_ex, n_out = len(ab), len(extras), len(outs)
    use_acc = nk > 1 or from_ref

    def product(refs):
        r = lax.dot_general(refs[0][...], refs[1][...], dims, preferred_element_type=F32)
        for i in range(1, n_ab):
            r = r + lax.dot_general(refs[2 * i][...], refs[2 * i + 1][...], dims, preferred_element_type=F32)
        return r

    def body(*refs):
        rest = refs[2 * n_ab:]
        ex_refs, o_refs = rest[:n_ex], rest[n_ex:n_ex + n_out]
        first_rows = pl.program_id(0) == 0
        if not use_acc:
            finish(product(refs), ex_refs, o_refs, first_rows)
            return
        acc = rest[n_ex + n_out]
        if nk == 1:
            acc[...] = product(refs)
            finish(acc, ex_refs, o_refs, first_rows)
            return
        k = pl.program_id(2)

        @pl.when(k == 0)
        def _():
            acc[...] = jnp.zeros_like(acc)

        acc[...] += product(refs)

        @pl.when(k == nk - 1)
        def _():
            finish(acc if from_ref else acc[...], ex_refs, o_refs, first_rows)

    res = pl.pallas_call(
        body, out_shape=[o for o, _ in outs], grid=grid,
        in_specs=list(ab_specs) + [s for _, s in extras], out_specs=[s for _, s in outs],
        scratch_shapes=[pltpu.VMEM(acc_shape, F32)] if use_acc else [], name=name, compiler_params=_cp(3),
    )(*[t for pair in ab for t in pair], *[e for e, _ in extras])
    return res[0] if n_out == 1 else res


def _finish_sum(r, ex_refs, o_refs, first_rows):
    del first_rows
    for e in ex_refs:
        r = r + e[...]
    o_refs[0][...] = r.astype(o_refs[0].dtype)


def _finish_sum_norm(r, ex_refs, o_refs, first_rows):
    del first_rows
    for e in ex_refs[:-1]:
        r = r + e[...]
    o_refs[0][...] = r
    o_refs[1][...] = ((r * lax.rsqrt(jnp.mean(r * r, axis=-1, keepdims=True) + EPS)) * ex_refs[-1][...]).astype(BF16)


_EPI_ROWS = 16


def _finish_rms_bwd(r_ref, ex_refs, o_refs, first_rows):
    x_ref, g_ref, dres_ref = ex_refs
    dx_ref, dxb_ref, dg_ref = o_refs

    @pl.when(first_rows)
    def _():
        dg_ref[...] = jnp.zeros_like(dg_ref)

    gv = g_ref[...]
    inv_d = 1.0 / r_ref.shape[1]

    def step(i, dg_acc):
        groups = [pl.ds(pl.multiple_of(i * (2 * _EPI_ROWS) + u * _EPI_ROWS, _EPI_ROWS), _EPI_ROWS) for u in range(2)]
        sums = []
        for rows in groups:
            r, xf = r_ref[rows, :], x_ref[rows, :]
            sums.append((jnp.sum(xf * xf, axis=-1, keepdims=True), jnp.sum((r * gv) * xf, axis=-1, keepdims=True)))
        for rows, (sxx, sax) in zip(groups, sums):
            r, xf = r_ref[rows, :], x_ref[rows, :]
            rs = lax.rsqrt(sxx * inv_d + EPS)
            dg_acc = dg_acc + _psum8(r * (xf * rs))
            dx = rs * (r * gv) - xf * (rs * rs * (sax * rs * inv_d)) + dres_ref[rows, :]
            dx_ref[rows, :] = dx
            dxb_ref[rows, :] = dx.astype(BF16)
        return dg_acc

    dg_acc = lax.fori_loop(0, r_ref.shape[0] // (2 * _EPI_ROWS), step, jnp.zeros((SUB, r_ref.shape[1]), F32))
    dg_ref[...] += jnp.sum(dg_acc, axis=0, keepdims=True)


def _rms_bwd_io(M, tm, x, g, dres):
    rows = pl.BlockSpec((tm, D), lambda m, n, k: (m, 0))
    vec = pl.BlockSpec((1, D), lambda m, n, k: (0, 0))
    return ([(x, rows), (g, vec), (dres, rows)],
            [(SDS((M, D), F32), rows), (SDS((M, D), BF16), rows), (SDS((1, D), F32), vec)])


_K_WHOLE = 3072


def _mm_nn(a, b, *, out_dtype, name, bias=None, add=None, norm=None):
    M, K = a.shape
    tk = K if K <= _K_WHOLE else _tile(K, 1024)
    tm = _tile(M, 1024 if K <= 1024 and norm is None else 512)
    if b.ndim == 3:
        nb, _, bw = b.shape
        N, tn, nn = nb * bw, bw, nb
        b_spec = pl.BlockSpec((None, tk, bw), lambda m, n, k: (_perm2(n), k, 0))
    else:
        N = b.shape[1]
        tn = _tile(N, 1024)
        nn = N // tn
        b_spec = pl.BlockSpec((tk, tn), lambda m, n, k: (k, n))
    tile = pl.BlockSpec((tm, tn), lambda m, n, k: (m, n))
    vec = pl.BlockSpec((1, tn), lambda m, n, k: (0, n))
    extras = ([] if bias is None else [(bias, vec)]) + ([] if add is None else [(add, tile)])
    outs, finish = [(SDS((M, N), out_dtype), tile)], _finish_sum
    if norm is not None:
        assert tn == N == D and out_dtype == F32
        extras.append((norm, vec))
        outs, finish = outs + [(SDS((M, N), BF16), tile)], _finish_sum_norm
    return _mm_call(name, (M // tm, nn, K // tk), [(a, b)], [pl.BlockSpec((tm, tk), lambda m, n, k: (m, k)), b_spec],
                    _NN, (tm, tn), extras, outs, finish)


def _mm_nt(a, b, *, out_dtype, name, add=None, rms=None):
    M, N = a.shape
    if b.ndim == 3:
        nb, Ko, bw = b.shape
        tm = _tile(M, 1024)
        tn, tk, nk = _tile(Ko, 1024), bw, nb
        b_spec = pl.BlockSpec((None, tn, bw), lambda m, n, k: (_perm2(k), n, 0))
    else:
        Ko = b.shape[0]
        tk = N if N <= _K_WHOLE else _tile(N, 1024)
        tm = _tile(M, 1024 if N <= 1024 and rms is None else 512)
        tn = _tile(Ko, 1024)
        nk = N // tk
        b_spec = pl.BlockSpec((tn, tk), lambda m, n, k: (n, k))
    tile = pl.BlockSpec((tm, tn), lambda m, n, k: (m, n))
    extras = [] if add is None else [(add, tile)]
    outs, finish = [(SDS((M, Ko), out_dtype), tile)], _finish_sum
    if rms is not None:
        assert tn == Ko == D and add is None
        (extras, outs), finish = _rms_bwd_io(M, tm, *rms), _finish_rms_bwd
    return _mm_call(name, (M // tm, Ko // tn, nk), [(a, b)], [pl.BlockSpec((tm, tk), lambda m, n, k: (m, k)), b_spec],
                    _NT, (tm, tn), extras, outs, finish, from_ref=rms is not None)


def _mm_nt_cols(parts, b, *, name, rms):
    M = parts[0].shape[0]
    tm = _tile(M, 512)
    specs, off = [], 0
    for p in parts:
        w = p.shape[1]
        assert off % w == 0
        specs.append(pl.BlockSpec((tm, w), lambda m, n, k: (m, 0)))
        specs.append(pl.BlockSpec((D, w), functools.partial(lambda m, n, k, o: (0, o), o=off // w)))
        off += w
    extras, outs = _rms_bwd_io(M, tm, *rms)
    return _mm_call(name, (M // tm, 1, 1), [(p, b) for p in parts], specs, _NT, (tm, D), extras, outs, _finish_rms_bwd,
                    from_ref=True)


def _mm_tn(a, b, *, out_dtype, name, blocks=None):
    S, Ka = a.shape
    Nb = b.shape[1]
    tm, tk = _tile(Ka, 1024), _tile(S, 2048)
    if blocks is not None:
        bw = blocks
        tn, nn = bw, Nb // bw
        out = (SDS((nn, Ka, bw), out_dtype), pl.BlockSpec((None, tm, bw), lambda m, n, k: (_perm2(n), m, 0)))
    else:
        tn = _tile(Nb, 1024)
        nn = Nb // tn
        out = (SDS((Ka, Nb), out_dtype), pl.BlockSpec((tm, tn), lambda m, n, k: (m, n)))
    return _mm_call(name, (Ka // tm, nn, S // tk), [(a, b)],
                    [pl.BlockSpec((tk, tm), lambda m, n, k: (k, m)), pl.BlockSpec((tk, tn), lambda m, n, k: (k, n))],
                    _TN, (tm, tn), [], [out], _finish_sum)


def _row(tm, c):
    return pl.BlockSpec((tm, c), lambda i: (i, 0))


def _full(shape):
    nd = len(shape)
    return pl.BlockSpec(shape, lambda i: (0,) * nd)


def _rms_fwd(x, g, name):
    S = x.shape[0]
    tm = min(S, TM_ROW)

    def body(x_ref, g_ref, o_ref):
        xf = x_ref[...]
        r = lax.rsqrt(jnp.mean(xf * xf, axis=-1, keepdims=True) + EPS)
        o_ref[...] = ((xf * r) * g_ref[...]).astype(BF16)

    return pl.pallas_call(body, out_shape=SDS((S, D), BF16), grid=(S // tm,), in_specs=[_row(tm, D), _full((1, D))],
                          out_specs=_row(tm, D), name=name, compiler_params=_cp(1))(x, g)


def _rms_bwd(x, g, dn, dres, name):
    S = x.shape[0]
    tm = min(S, TM_ROW)
    want_dx = dres is not None

    def body(x_ref, g_ref, dn_ref, *rest):
        i = pl.program_id(0)
        dg_ref = rest[-1]

        @pl.when(i == 0)
        def _():
            dg_ref[...] = jnp.zeros_like(dg_ref)

        xf = x_ref[...]
        r = lax.rsqrt(jnp.mean(xf * xf, axis=-1, keepdims=True) + EPS)
        y = xf * r
        dn_v = dn_ref[...]
        dg_ref[...] += jnp.sum(dn_v * y, axis=0, keepdims=True)
        if want_dx:
            dres_ref, dx_ref, dxb_ref = rest[0], rest[1], rest[2]
            dy = dn_v * g_ref[...]
            dx = r * (dy - y * jnp.mean(dy * y, axis=-1, keepdims=True)) + dres_ref[...]
            dx_ref[...] = dx
            dxb_ref[...] = dx.astype(BF16)

    ins = [x, g, dn] + ([dres] if want_dx else [])
    in_specs = [_row(tm, D), _full((1, D)), _row(tm, D)] + ([_row(tm, D)] if want_dx else [])
    outs = ([SDS((S, D), F32), SDS((S, D), BF16)] if want_dx else []) + [SDS((1, D), F32)]
    out_specs = ([_row(tm, D), _row(tm, D)] if want_dx else []) + [_full((1, D))]
    return pl.pallas_call(body, out_shape=outs, grid=(S // tm,), in_specs=in_specs, out_specs=out_specs, name=name,
                          compiler_params=_cp(1))(*ins)


def _loss_head(x, g, tgt):
    S = x.shape[0]
    tm = min(S, TM_ROW)

    def body(x_ref, g_ref, t_ref, loss_ref, dx_ref, dxb_ref, dg_ref):
        i = pl.program_id(0)

        @pl.when(i == 0)
        def _():
            loss_ref[...] = jnp.zeros_like(loss_ref)
            dg_ref[...] = jnp.zeros_like(dg_ref)

        xf = x_ref[...]
        r = lax.rsqrt(jnp.mean(xf * xf, axis=-1, keepdims=True) + EPS)
        y = xf * r
        gv = g_ref[...]
        err = y * gv - t_ref[...]
        per_row = jnp.mean(err * err, axis=-1, keepdims=True)
        loss_ref[...] += 0.5 * jnp.sum(per_row, axis=0, keepdims=True)
        dn_v = err * (1.0 / D)
        dg_ref[...] += jnp.sum(dn_v * y, axis=0, keepdims=True)
        dy = dn_v * gv
        dx = r * (dy - y * jnp.mean(dy * y, axis=-1, keepdims=True))
        dx_ref[...] = dx
        dxb_ref[...] = dx.astype(BF16)

    return pl.pallas_call(
        body, out_shape=[SDS((1, 1), F32), SDS((S, D), F32), SDS((S, D), BF16), SDS((1, D), F32)], grid=(S // tm,),
        in_specs=[_row(tm, D), _full((1, D)), _row(tm, D)],
        out_specs=[_full((1, 1)), _row(tm, D), _row(tm, D), _full((1, D))], name="loss_head", compiler_params=_cp(1),
    )(x, g, tgt)


def _softmax_rows(s):
    m = jnp.max(s, axis=-1, keepdims=True)
    e = jnp.exp(s - m)
    return e / jnp.sum(e, axis=-1, keepdims=True)


def _attn_fwd(q, k, v, name):
    S = q.shape[0]
    tm = min(S, TM_ROW)
    scale = XA_HD ** -0.5

    def body(q_ref, k_ref, v_ref, o_ref):
        for h in range(XA_HEADS):
            sl = slice(h * XA_HD, (h + 1) * XA_HD)
            s = lax.dot_general(q_ref[:, sl], k_ref[:, sl], _NT, preferred_element_type=F32) * scale
            p = _softmax_rows(s)
            o_ref[:, sl] = lax.dot_general(p.astype(BF16), v_ref[:, sl], _NN, preferred_element_type=F32).astype(BF16)

    return pl.pallas_call(body, out_shape=SDS((S, D), BF16), grid=(S // tm,),
                          in_specs=[_row(tm, D), _full((N_MEM, D)), _full((N_MEM, D))], out_specs=_row(tm, D),
                          name=name, compiler_params=_cp(1))(q, k, v)


def _attn_bwd(q, k, v, do, name):
    S = q.shape[0]
    tm = min(S, TM_ROW)
    scale = XA_HD ** -0.5

    def body(q_ref, k_ref, v_ref, do_ref, dq_ref, dk_ref, dv_ref):
        i = pl.program_id(0)

        @pl.when(i == 0)
        def _():
            dk_ref[...] = jnp.zeros_like(dk_ref)
            dv_ref[...] = jnp.zeros_like(dv_ref)

        for h in range(XA_HEADS):
            sl = slice(h * XA_HD, (h + 1) * XA_HD)
            qh, kh, vh, doh = q_ref[:, sl], k_ref[:, sl], v_ref[:, sl], do_ref[:, sl]
            s = lax.dot_general(qh, kh, _NT, preferred_element_type=F32) * scale
            p = _softmax_rows(s)
            pb = p.astype(BF16)
            dv_ref[:, sl] += lax.dot_general(pb, doh, _TN, preferred_element_type=F32)
            dp = lax.dot_general(doh, vh, _NT, preferred_element_type=F32)
            ds = (p * (dp - jnp.sum(dp * p, axis=-1, keepdims=True)) * scale).astype(BF16)
            dq_ref[:, sl] = lax.dot_general(ds, kh, _NN, preferred_element_type=F32).astype(BF16)
            dk_ref[:, sl] += lax.dot_general(ds, qh, _TN, preferred_element_type=F32)

    return pl.pallas_call(
        body, out_shape=[SDS((S, D), BF16), SDS((N_MEM, D), F32), SDS((N_MEM, D), F32)], grid=(S // tm,),
        in_specs=[_row(tm, D), _full((N_MEM, D)), _full((N_MEM, D)), _row(tm, D)],
        out_specs=[_row(tm, D), _full((N_MEM, D)), _full((N_MEM, D))], name=name, compiler_params=_cp(1),
    )(q, k, v, do)


def _sigmoid(x):
    return 1.0 / (1.0 + jnp.exp(-x))


def _ln_silu_fwd(cv, g, b):
    S = cv.shape[0]
    tm = min(S, TM_ROW)

    def body(x_ref, g_ref, b_ref, o_ref):
        xf = x_ref[...]
        mu = jnp.mean(xf, axis=-1, keepdims=True)
        xc = xf - mu
        rstd = lax.rsqrt(jnp.mean(xc * xc, axis=-1, keepdims=True) + EPS)
        ln = (xc * rstd) * g_ref[...] + b_ref[...]
        o_ref[...] = (ln * _sigmoid(ln)).astype(BF16)

    return pl.pallas_call(body, out_shape=SDS((S, D), BF16), grid=(S // tm,),
                          in_specs=[_row(tm, D), _full((1, D)), _full((1, D))], out_specs=_row(tm, D),
                          name="ln_silu_fwd", compiler_params=_cp(1))(cv, g, b)


def _ln_silu_bwd(ds, cv, g, b, dx):
    S = cv.shape[0]
    tm = min(S, TM_ROW)

    def body(ds_ref, x_ref, g_ref, b_ref, dx_ref, dcv_ref, dg_ref, db_ref, db2_ref):
        i = pl.program_id(0)

        @pl.when(i == 0)
        def _():
            dg_ref[...] = jnp.zeros_like(dg_ref)
            db_ref[...] = jnp.zeros_like(db_ref)
            db2_ref[...] = jnp.zeros_like(db2_ref)

        xf = x_ref[...]
        mu = jnp.mean(xf, axis=-1, keepdims=True)
        xc = xf - mu
        rstd = lax.rsqrt(jnp.mean(xc * xc, axis=-1, keepdims=True) + EPS)
        xhat = xc * rstd
        gv = g_ref[...]
        ln = xhat * gv + b_ref[...]
        sg = _sigmoid(ln)
        dln = ds_ref[...].astype(F32) * (sg + ln * sg * (1.0 - sg))
        dg_ref[...] += jnp.sum(dln * xhat, axis=0, keepdims=True)
        db_ref[...] += jnp.sum(dln, axis=0, keepdims=True)
        db2_ref[...] += jnp.sum(dx_ref[...], axis=0, keepdims=True)
        dxh = dln * gv
        dcv_ref[...] = rstd * (dxh - jnp.mean(dxh, axis=-1, keepdims=True)
                               - xhat * jnp.mean(dxh * xhat, axis=-1, keepdims=True))

    return pl.pallas_call(
        body, out_shape=[SDS((S, D), F32), SDS((1, D), F32), SDS((1, D), F32), SDS((1, D), F32)], grid=(S // tm,),
        in_specs=[_row(tm, D), _row(tm, D), _full((1, D)), _full((1, D)), _row(tm, D)],
        out_specs=[_row(tm, D), _full((1, D)), _full((1, D)), _full((1, D))], name="ln_silu_bwd",
        compiler_params=_cp(1),
    )(ds, cv, g, b, dx)


_GELU_C, _GELU_K = 0.7978845608028654, 0.044715


def _gelu(x, with_grad=False):
    x2 = x * x
    t = jnp.tanh(_GELU_C * (x + _GELU_K * x * x2))
    gel = 0.5 * x * (1.0 + t)
    if not with_grad:
        return gel
    return gel, 0.5 * (1.0 + t) + 0.5 * x * (1.0 - t * t) * (_GELU_C * (1.0 + 3.0 * _GELU_K * x2))


def _expm1(x):
    poly = x * (1.0 + x * (0.5 + x * (1.0 / 6.0 + x * (1.0 / 24.0 + x * (1.0 / 120.0)))))
    return jnp.where(jnp.abs(x) < 0.05, poly, jnp.exp(x) - 1.0)


def _softplus(x):
    return jnp.maximum(x, 0.0) + jnp.log1p(jnp.exp(-jnp.abs(x)))


_SCAN_UNROLL = 4
_RB = 32
_HB = 16


def _sub_blocks(n_rows, n_lanes, fn):
    def step(idx, c):
        r0 = pl.multiple_of(idx * _RB, _RB)
        for lt in range(n_lanes // LANE):
            fn(r0, lt)
        return c

    lax.fori_loop(0, n_rows // _RB, step, 0)


def _lanes(lt):
    return pl.ds(lt * LANE, LANE)


def _psum8(x):
    parts = [x[i * SUB:(i + 1) * SUB] for i in range(x.shape[0] // SUB)]
    return functools.reduce(lambda p, q: p + q, parts)


def _scan_fwd(a_s, b_s, out_ref, carry_ref, n_groups):
    row = lax.broadcasted_iota(jnp.int32, (SUB, LANE), 0)
    U = _SCAN_UNROLL

    def step(gi, carry):
        base = gi * (SUB * U)
        parts = []
        for u in range(U):
            i = pl.multiple_of(base + u * SUB, SUB)
            a8, b8 = a_s[pl.ds(i, SUB), :], b_s[pl.ds(i, SUB), :]
            for s in (1, 2, 4):
                a_sh = jnp.where(row >= s, pltpu.roll(a8, s, 0), 1.0)
                b_sh = jnp.where(row >= s, pltpu.roll(b8, s, 0), 0.0)
                b8 = a8 * b_sh + b8
                a8 = a8 * a_sh
            parts.append((i, a8, b8))
        for i, a8, b8 in parts:
            h8 = a8 * carry + b8
            out_ref[pl.ds(i, SUB), :] = h8
            carry = jnp.broadcast_to(h8[SUB - 1:SUB, :], (SUB, LANE))
        return carry

    carry_ref[...] = lax.fori_loop(0, n_groups // U, step, carry_ref[...])


def _scan_bwd(a_s, b_s, out_ref, carry_ref, n_groups):
    row = lax.broadcasted_iota(jnp.int32, (SUB, LANE), 0)
    U = _SCAN_UNROLL

    def step(gi, carry):
        base = (n_groups // U - 1 - gi) * (SUB * U)
        parts = []
        for u in reversed(range(U)):
            i = pl.multiple_of(base + u * SUB, SUB)
            a8, b8 = a_s[pl.ds(i, SUB), :], b_s[pl.ds(i, SUB), :]
            for s in (1, 2, 4):
                a_sh = jnp.where(row < SUB - s, pltpu.roll(a8, SUB - s, 0), 1.0)
                b_sh = jnp.where(row < SUB - s, pltpu.roll(b8, SUB - s, 0), 0.0)
                b8 = a8 * b_sh + b8
                a8 = a8 * a_sh
            parts.append((i, a8, b8))
        for i, a8, b8 in parts:
            h8 = a8 * carry + b8
            out_ref[pl.ds(i, SUB), :] = h8
            carry = jnp.broadcast_to(h8[0:1, :], (SUB, LANE))
        return carry

    carry_ref[...] = lax.fori_loop(0, n_groups // U, step, carry_ref[...])


def _rglru_pre(xr, wgx_ref, bgx_ref, wga_ref, bga_ref, lam_ref):
    xrb = xr.astype(BF16)
    wgx, wga = wgx_ref[0].astype(BF16), wga_ref[0].astype(BF16)
    gx = _sigmoid(lax.dot_general(xrb, wgx, _NN, preferred_element_type=F32) + bgx_ref[...])
    ga = _sigmoid(lax.dot_general(xrb, wga, _NN, preferred_element_type=F32) + bga_ref[...])
    sp = _softplus(-lam_ref[...])
    log_a = -C_RG * ga * sp
    a = jnp.exp(log_a)
    mult = jnp.sqrt(-_expm1(2.0 * log_a))
    return gx, ga, sp, a, mult, xrb, wgx, wga


def _a_specs():
    vec = pl.BlockSpec((1, HD_A), lambda c, j: (0, c))
    mat = pl.BlockSpec((1, HD_A, HD_A), lambda c, j: (c, 0, 0))
    return [pl.BlockSpec((CONV_A, HD_A), lambda c, j: (0, c)), vec, mat, vec, mat, vec, vec]


def _a_fwd(zp, conv_w, conv_b, wgx, bgx, wga, bga, lam):
    S = zp.shape[0]
    R, nt = R_SEQ, D // HD_A
    H = SUB

    def body(zg_ref, zr_ref, cw_ref, cb_ref, wgx_ref, bgx_ref, wga_ref, bga_ref, lam_ref, ya_ref, h_ref,
             ext, a_s, b_s, hc):
        j = pl.program_id(1)

        @pl.when(j == 0)
        def _():
            ext[0:H, :] = jnp.zeros((H, HD_A), F32)
            hc[...] = jnp.zeros_like(hc)

        ext[H:H + R, :] = zr_ref[...].astype(F32)
        xr = cb_ref[...]
        for k in range(CONV_A):
            xr = xr + cw_ref[k:k + 1, :] * ext[pl.ds(H - (CONV_A - 1 - k), R), :]
        gx, _, _, a, mult, _, _, _ = _rglru_pre(xr, wgx_ref, bgx_ref, wga_ref, bga_ref, lam_ref)
        a_s[...] = a
        b_s[...] = mult * (gx * xr)
        _scan_fwd(a_s, b_s, h_ref, hc, R // SUB)
        ya_ref[...] = (_gelu(zg_ref[...].astype(F32)) * h_ref[...]).astype(BF16)
        ext[0:H, :] = ext[R:R + H, :]

    return pl.pallas_call(
        body, out_shape=[SDS((S, D + D // 2), BF16), SDS((S, D), F32)], grid=(nt, S // R),
        in_specs=[pl.BlockSpec((R, HD_A), lambda c, j: (j, c)), pl.BlockSpec((R, HD_A), lambda c, j: (j, nt + c))]
        + _a_specs(),
        out_specs=[pl.BlockSpec((R, HD_A), lambda c, j: (j, c)), pl.BlockSpec((R, HD_A), lambda c, j: (j, c))],
        scratch_shapes=[pltpu.VMEM((H + R, HD_A), F32), pltpu.VMEM((R, HD_A), F32), pltpu.VMEM((R, HD_A), F32),
                        pltpu.VMEM((SUB, HD_A), F32)],
        name="rglru_fwd", compiler_params=_cp(2),
    )(zp, zp, conv_w, conv_b, wgx, bgx, wga, bga, lam)


def _a_bwd(dyab, zp, h, conv_w, conv_b, wgx, bgx, wga, bga, lam):
    S = zp.shape[0]
    R, nt, nch = R_SEQ, D // HD_A, S // R_SEQ
    H = SUB

    def rows(c, j):
        return (nch - 1 - j, c)

    def rows_rec(c, j):
        return (nch - 1 - j, nt + c)

    def halo(c, j):
        return (jnp.maximum((nch - 1 - j) * (R // H) - 1, 0), c)

    def halo_z(c, j):
        return (jnp.maximum((nch - 1 - j) * (R // _HB) - 1, 0), nt + c)

    def body(dy_ref, zg_ref, zr_ref, zh_ref, h_ref, hh_ref, cw_ref, cb_ref, wgx_ref, bgx_ref, wga_ref, bga_ref,
             lam_ref, dzg_ref, dzr_ref, dcw_ref, dcb_ref, dwgx_ref, dbgx_ref, dwga_ref, dbga_ref, dlam_ref,
             ext_z, ext_h, ext_mu, ext_d, a_s, b_s, muc):
        j = pl.program_id(1)
        first_chunk = (nch - 1 - j) == 0

        @pl.when(j == 0)
        def _():
            ext_mu[R:R + H, :] = jnp.zeros((H, HD_A), F32)
            ext_d[R:R + H, :] = jnp.zeros((H, HD_A), F32)
            muc[...] = jnp.zeros_like(muc)
            for r in (dcw_ref, dcb_ref, dwgx_ref, dbgx_ref, dwga_ref, dbga_ref, dlam_ref):
                r[...] = jnp.zeros_like(r)

        zg = zg_ref[...].astype(F32)
        ext_z[0:H, :] = jnp.where(first_chunk, 0.0, zh_ref[_HB - H:_HB, :].astype(F32))
        ext_z[H:H + R, :] = zr_ref[...].astype(F32)
        ext_h[0:H, :] = jnp.where(first_chunk, 0.0, hh_ref[...])
        ext_h[H:H + R, :] = h_ref[...]
        xr = cb_ref[...]
        for k in range(CONV_A):
            xr = xr + cw_ref[k:k + 1, :] * ext_z[pl.ds(H - (CONV_A - 1 - k), R), :]
        gx, ga, sp, a, mult, xrb, wgxb, wgab = _rglru_pre(xr, wgx_ref, bgx_ref, wga_ref, bga_ref, lam_ref)
        gel, dgel = _gelu(zg, with_grad=True)
        dy = dy_ref[...].astype(F32)
        dh = dy * gel
        dzg_ref[...] = (dy * h_ref[...] * dgel).astype(BF16)
        a_s[...] = a
        b_s[...] = a * dh
        _scan_bwd(a_s, b_s, ext_mu, muc, R // SUB)
        lam_t = dh + ext_mu[pl.ds(1, R), :]
        ext_mu[R:R + H, :] = ext_mu[0:H, :]
        da = lam_t * ext_h[pl.ds(H - 1, R), :]
        gxr = gx * xr
        dlog_a = da * a - (lam_t * gxr) * (a * a) / mult
        dgx = lam_t * mult * xr
        dxr = lam_t * mult * gx
        lam_v = lam_ref[...]
        dlam_ref[...] += jnp.sum(dlog_a * ga, axis=0, keepdims=True) * (C_RG * _sigmoid(-lam_v))
        dpa = (dlog_a * (-C_RG * sp)) * ga * (1.0 - ga)
        dpx = dgx * gx * (1.0 - gx)
        dbga_ref[...] += jnp.sum(dpa, axis=0, keepdims=True)
        dbgx_ref[...] += jnp.sum(dpx, axis=0, keepdims=True)
        dpab, dpxb = dpa.astype(BF16), dpx.astype(BF16)
        dwga_ref[0] += lax.dot_general(xrb, dpab, _TN, preferred_element_type=F32)
        dwgx_ref[0] += lax.dot_general(xrb, dpxb, _TN, preferred_element_type=F32)
        dxr = (dxr + lax.dot_general(dpab, wgab, _NT, preferred_element_type=F32)
               + lax.dot_general(dpxb, wgxb, _NT, preferred_element_type=F32))
        dcb_ref[...] += jnp.sum(dxr, axis=0, keepdims=True)
        ext_d[0:R, :] = dxr
        dzr = jnp.zeros((R, HD_A), F32)
        for k in range(CONV_A):
            sh = CONV_A - 1 - k
            dcw_ref[k:k + 1, :] += jnp.sum(dxr * ext_z[pl.ds(H - sh, R), :], axis=0, keepdims=True)
            dzr = dzr + cw_ref[k:k + 1, :] * ext_d[pl.ds(sh, R), :]
        dzr_ref[...] = dzr.astype(BF16)
        ext_d[R:R + H, :] = ext_d[0:H, :]

    vec_o = pl.BlockSpec((1, HD_A), lambda c, j: (0, c))
    mat_o = pl.BlockSpec((1, HD_A, HD_A), lambda c, j: (c, 0, 0))
    return pl.pallas_call(
        body,
        out_shape=[SDS((S, D), BF16), SDS((S, D), BF16), SDS((CONV_A, D), F32), SDS((1, D), F32),
                   SDS((nt, HD_A, HD_A), F32), SDS((1, D), F32), SDS((nt, HD_A, HD_A), F32), SDS((1, D), F32),
                   SDS((1, D), F32)],
        grid=(nt, nch),
        in_specs=[pl.BlockSpec((R, HD_A), rows), pl.BlockSpec((R, HD_A), rows), pl.BlockSpec((R, HD_A), rows_rec),
                  pl.BlockSpec((_HB, HD_A), halo_z), pl.BlockSpec((R, HD_A), rows),
                  pl.BlockSpec((H, HD_A), halo)] + _a_specs(),
        out_specs=[pl.BlockSpec((R, HD_A), rows), pl.BlockSpec((R, HD_A), rows),
                   pl.BlockSpec((CONV_A, HD_A), lambda c, j: (0, c)), vec_o, mat_o, vec_o, mat_o, vec_o, vec_o],
        scratch_shapes=[pltpu.VMEM((H + R, HD_A), F32), pltpu.VMEM((H + R, HD_A), F32), pltpu.VMEM((R + H, HD_A), F32),
                        pltpu.VMEM((R + H, HD_A), F32), pltpu.VMEM((R, HD_A), F32), pltpu.VMEM((R, HD_A), F32),
                        pltpu.VMEM((SUB, HD_A), F32)],
        name="rglru_bwd", compiler_params=_cp(2),
    )(dyab, zp, zp, zp, h, h, conv_w, conv_b, wgx, bgx, wga, bga, lam)


_POOL_H = 16
_POOL_T0 = 2 * D // HD_A
_POOL_Y0 = D // HD_A


def _window_sum(lv, n, lo, rows, g, ahead):
    base = 0 if ahead else SUB
    cur, win = lv[0], None
    for i, s in enumerate((1, 2, 4, 8)):
        val = cur[pl.ds(base, n), :] + cur[pl.ds(base + (s if ahead else -s), n), :]
        sel = val[lo:lo + rows]
        win = sel if win is None else jnp.where(g >= i, sel, win)
        if i < 3:
            lv[i + 1][pl.ds(base, n), :] = val
            cur = lv[i + 1]
    return win


def _pool_width(g):
    return jnp.where(g == 0, 2.0, jnp.where(g == 1, 4.0, jnp.where(g == 2, 8.0, 16.0)))


def _b_fwd(zp, yab, wg, bg, sc):
    S = zp.shape[0]
    R, H = R_SEQ, _POOL_H

    def body(z_ref, wg_ref, bg_ref, sc_ref, yab_in, yb_ref, *lv):
        del yab_in
        g, j = pl.program_id(0), pl.program_id(1)

        @pl.when(j == 0)
        def _():
            for r in lv:
                r[0:SUB, :] = jnp.zeros((SUB, HD_A), F32)
            lv[0][SUB:SUB + H, :] = jnp.zeros((H, HD_A), F32)

        u = z_ref[...].astype(F32)
        lv[0][SUB + H:SUB + H + R, :] = u
        t1 = (j * R + 1 + lax.broadcasted_iota(jnp.int32, (R, HD_A), 0)).astype(F32)
        p = _window_sum(lv, H + R, H, R, g, False) / jnp.minimum(t1, _pool_width(g)) - u
        lin = lax.dot_general(p.astype(BF16), wg_ref[0].astype(BF16), _NN, preferred_element_type=F32) + bg_ref[...]
        yb_ref[...] = (lin * sc_ref[...]).astype(BF16)
        lv[0][SUB:SUB + H, :] = lv[0][SUB + R:SUB + R + H, :]

    vec = pl.BlockSpec((1, HD_A), lambda g, j: (0, g))
    return pl.pallas_call(
        body, out_shape=SDS(yab.shape, yab.dtype), grid=(len(POOL_WINDOWS), S // R),
        in_specs=[pl.BlockSpec((R, HD_A), lambda g, j: (j, _POOL_T0 + g)),
                  pl.BlockSpec((1, HD_A, HD_A), lambda g, j: (g, 0, 0)), vec, vec, pl.BlockSpec(memory_space=pl.ANY)],
        out_specs=pl.BlockSpec((R, HD_A), lambda g, j: (j, _POOL_Y0 + g)),
        scratch_shapes=[pltpu.VMEM((SUB + H + R, HD_A), F32)] * 4, input_output_aliases={4: 0},
        name="pool_fwd", compiler_params=_cp(2),
    )(zp, wg, bg, sc, yab)


def _b_bwd(dyab, zp, wg, bg, sc):
    S = zp.shape[0]
    R, H, nch, ng = R_SEQ, _POOL_H, S // R_SEQ, len(POOL_WINDOWS)

    def body(dy_ref, z_ref, zh_ref, wg_ref, bg_ref, sc_ref, dz_ref, dwg_ref, dbg_ref, dsc_ref, *scratch):
        lu, lq = scratch[:4], scratch[4:]
        g, j = pl.program_id(0), pl.program_id(1)
        jj = nch - 1 - j

        @pl.when(j == 0)
        def _():
            for r in lu:
                r[0:SUB, :] = jnp.zeros((SUB, HD_A), F32)
            for r in lq:
                r[R + H:R + H + SUB, :] = jnp.zeros((SUB, HD_A), F32)
            lq[0][R:R + H, :] = jnp.zeros((H, HD_A), F32)
            for r in (dwg_ref, dbg_ref, dsc_ref):
                r[...] = jnp.zeros_like(r)

        u = z_ref[...].astype(F32)
        lu[0][SUB:SUB + H, :] = jnp.where(jj == 0, 0.0, zh_ref[...].astype(F32))
        lu[0][SUB + H:SUB + H + R, :] = u
        t1 = (jj * R + 1 + lax.broadcasted_iota(jnp.int32, (R, HD_A), 0)).astype(F32)
        cnt = jnp.minimum(t1, _pool_width(g))
        pb = (_window_sum(lu, H + R, H, R, g, False) / cnt - u).astype(BF16)
        wgb = wg_ref[0].astype(BF16)
        lin = lax.dot_general(pb, wgb, _NN, preferred_element_type=F32) + bg_ref[...]
        dy = dy_ref[...].astype(F32)
        dsc_ref[...] += jnp.sum(dy * lin, axis=0, keepdims=True)
        dlin = dy * sc_ref[...]
        dbg_ref[...] += jnp.sum(dlin, axis=0, keepdims=True)
        dlb = dlin.astype(BF16)
        dwg_ref[0] += lax.dot_general(pb, dlb, _TN, preferred_element_type=F32)
        dp = lax.dot_general(dlb, wgb, _NT, preferred_element_type=F32)
        lq[0][0:R, :] = dp / cnt
        dz_ref[...] = (_window_sum(lq, R + H, 0, R, g, True) - dp).astype(BF16)
        lq[0][R:R + H, :] = lq[0][0:H, :]

    vec = pl.BlockSpec((1, HD_A), lambda g, j: (0, g))
    mat = pl.BlockSpec((1, HD_A, HD_A), lambda g, j: (g, 0, 0))
    return pl.pallas_call(
        body, out_shape=[SDS((S, D // 2), BF16), SDS((ng, HD_A, HD_A), F32), SDS((1, D // 2), F32),
                         SDS((1, D // 2), F32)],
        grid=(ng, nch),
        in_specs=[pl.BlockSpec((R, HD_A), lambda g, j: (nch - 1 - j, _POOL_Y0 + g)),
                  pl.BlockSpec((R, HD_A), lambda g, j: (nch - 1 - j, _POOL_T0 + g)),
                  pl.BlockSpec((H, HD_A), lambda g, j: (jnp.maximum((nch - 1 - j) * (R // H) - 1, 0), _POOL_T0 + g)),
                  mat, vec, vec],
        out_specs=[pl.BlockSpec((R, HD_A), lambda g, j: (nch - 1 - j, g)), mat, vec, vec],
        scratch_shapes=[pltpu.VMEM((SUB + H + R, HD_A), F32)] * 8,
        name="pool_bwd", compiler_params=_cp(2),
    )(dyab, zp, zp, wg, bg, sc)


_CW_F = 768


def _f_fwd(hp, w, b, name):
    S = hp.shape[0]
    R, H, cw = R_SEQ, SUB, _CW_F
    nlt = cw // LANE

    def body(h_ref, w_ref, b_ref, o_ref, gel_ref, ud_ref, ext):
        j = pl.program_id(1)

        @pl.when(j == 0)
        def _():
            ext[:, 0:H, :] = jnp.zeros((nlt, H, LANE), F32)

        def stage(r0, lt):
            ext[lt, pl.ds(pl.multiple_of(r0 + H, SUB), _RB), :] = h_ref[pl.ds(r0, _RB), _lanes(lt)].astype(F32)

        def main(r0, lt):
            ls = _lanes(lt)
            gp = b_ref[:, ls]
            for k in range(CONV_F):
                gp = gp + w_ref[k:k + 1, ls] * ext[lt, pl.ds(r0 + (H - (CONV_F - 1 - k)), _RB), :]
            up = h_ref[pl.ds(r0, _RB), _lanes(lt + nlt)].astype(F32)
            gel, dgel = _gelu(gp, with_grad=True)
            rs = pl.ds(r0, _RB)
            o_ref[rs, ls] = (gel * up).astype(BF16)
            gel_ref[rs, ls] = gel.astype(BF16)
            ud_ref[rs, ls] = (up * dgel).astype(BF16)

        _sub_blocks(R, cw, stage)
        _sub_blocks(R, cw, main)
        ext[:, 0:H, :] = ext[:, R:R + H, :]

    tile = pl.BlockSpec((R, cw), lambda c, j: (j, c))
    return pl.pallas_call(
        body, out_shape=[SDS((S, D_FF), BF16)] * 3, grid=(D_FF // cw, S // R),
        in_specs=[pl.BlockSpec((R, 2 * cw), lambda c, j: (j, c)), pl.BlockSpec((CONV_F, cw), lambda c, j: (0, c)),
                  pl.BlockSpec((1, cw), lambda c, j: (0, c))],
        out_specs=[tile] * 3,
        scratch_shapes=[pltpu.VMEM((nlt, H + R, LANE), F32)], name=name, compiler_params=_cp(2),
    )(hp, w, b)


def _f_bwd(dact, hp, gel, ud, w, name):
    S = hp.shape[0]
    R, H, cw, nch = R_SEQ, SUB, _CW_F, S // R_SEQ
    nlt = cw // LANE

    def body(da_ref, h_ref, hh_ref, gel_ref, ud_ref, w_ref, dh_ref, dw_ref, db_ref, ext_g, ext_d, acc):
        j = pl.program_id(1)
        jj = nch - 1 - j

        @pl.when(j == 0)
        def _():
            ext_d[:, R:R + H, :] = jnp.zeros((nlt, H, LANE), F32)
            acc[...] = jnp.zeros_like(acc)

        for lt in range(nlt):
            ext_g[lt, 0:H, :] = jnp.where(jj == 0, 0.0, hh_ref[_HB - H:_HB, lt * LANE:(lt + 1) * LANE].astype(F32))

        def stage(r0, lt):
            ext_g[lt, pl.ds(pl.multiple_of(r0 + H, SUB), _RB), :] = h_ref[pl.ds(r0, _RB), _lanes(lt)].astype(F32)

        def first(r0, lt):
            ls, lu, rs = _lanes(lt), _lanes(lt + nlt), pl.ds(r0, _RB)
            da = da_ref[rs, ls].astype(F32)
            dh_ref[rs, lu] = (da * gel_ref[rs, ls].astype(F32)).astype(BF16)
            dgp = da * ud_ref[rs, ls].astype(F32)
            ext_d[lt, rs, :] = dgp
            acc[CONV_F * SUB:(CONV_F + 1) * SUB, ls] += _psum8(dgp)
            for k in range(CONV_F):
                tap = ext_g[lt, pl.ds(r0 + (H - (CONV_F - 1 - k)), _RB), :]
                acc[k * SUB:(k + 1) * SUB, ls] += _psum8(dgp * tap)

        def second(r0, lt):
            ls = _lanes(lt)
            dhg = w_ref[CONV_F - 1:CONV_F, ls] * ext_d[lt, pl.ds(r0, _RB), :]
            for k in range(CONV_F - 1):
                dhg = dhg + w_ref[k:k + 1, ls] * ext_d[lt, pl.ds(r0 + (CONV_F - 1 - k), _RB), :]
            dh_ref[pl.ds(r0, _RB), ls] = dhg.astype(BF16)

        _sub_blocks(R, cw, stage)
        _sub_blocks(R, cw, first)
        _sub_blocks(R, cw, second)
        ext_d[:, R:R + H, :] = ext_d[:, 0:H, :]

        @pl.when(j == nch - 1)
        def _():
            for k in range(CONV_F):
                dw_ref[k:k + 1, :] = jnp.sum(acc[k * SUB:(k + 1) * SUB, :], axis=0, keepdims=True)
            db_ref[...] = jnp.sum(acc[CONV_F * SUB:(CONV_F + 1) * SUB, :], axis=0, keepdims=True)

    rows = lambda c, j: (nch - 1 - j, c)
    return pl.pallas_call(
        body, out_shape=[SDS((S, 2 * D_FF), BF16), SDS((CONV_F, D_FF), F32), SDS((1, D_FF), F32)],
        grid=(D_FF // cw, nch),
        in_specs=[pl.BlockSpec((R, cw), rows), pl.BlockSpec((R, cw), lambda c, j: (nch - 1 - j, 2 * c)),
                  pl.BlockSpec((_HB, cw), lambda c, j: (jnp.maximum((nch - 1 - j) * (R // _HB) - 1, 0), 2 * c)),
                  pl.BlockSpec((R, cw), rows), pl.BlockSpec((R, cw), rows),
                  pl.BlockSpec((CONV_F, cw), lambda c, j: (0, c))],
        out_specs=[pl.BlockSpec((R, 2 * cw), rows), pl.BlockSpec((CONV_F, cw), lambda c, j: (0, c)),
                   pl.BlockSpec((1, cw), lambda c, j: (0, c))],
        scratch_shapes=[pltpu.VMEM((nlt, H + R, LANE), F32), pltpu.VMEM((nlt, R + H, LANE), F32),
                        pltpu.VMEM(((CONV_F + 1) * SUB, cw), F32)], name=name,
        compiler_params=_cp(2),
    )(dact, hp, hp, gel, ud, w)


_CW_C = 256
_H_C = 32


def _c_fwd(h1p, w, b):
    S = h1p.shape[0]
    R, H, cw = R_SEQ, _H_C, _CW_C
    nlt = cw // LANE

    def body(h_ref, w_ref, b_ref, o_ref, ext):
        j = pl.program_id(1)

        @pl.when(j == 0)
        def _():
            ext[:, 0:H, :] = jnp.zeros((nlt, H, LANE), F32)

        def stage(r0, lt):
            rs = pl.ds(r0, _RB)
            gate = h_ref[rs, _lanes(lt + nlt)].astype(F32)
            ext[lt, pl.ds(pl.multiple_of(r0 + H, SUB), _RB), :] = h_ref[rs, _lanes(lt)].astype(F32) * _sigmoid(gate)

        def main(r0, lt):
            ls = _lanes(lt)
            cv = b_ref[:, ls]
            for k in range(CONV_C):
                cv = cv + w_ref[k:k + 1, ls] * ext[lt, pl.ds(r0 + (H - (CONV_C - 1 - k)), _RB), :]
            o_ref[pl.ds(r0, _RB), ls] = cv

        _sub_blocks(R, cw, stage)
        _sub_blocks(R, cw, main)
        ext[:, 0:H, :] = ext[:, R:R + H, :]

    return pl.pallas_call(
        body, out_shape=SDS((S, D), F32), grid=(D // cw, S // R),
        in_specs=[pl.BlockSpec((R, 2 * cw), lambda c, j: (j, c)), pl.BlockSpec((CONV_C, cw), lambda c, j: (0, c)),
                  pl.BlockSpec((1, cw), lambda c, j: (0, c))],
        out_specs=pl.BlockSpec((R, cw), lambda c, j: (j, c)),
        scratch_shapes=[pltpu.VMEM((nlt, H + R, LANE), F32)], name="conf_conv_fwd", compiler_params=_cp(2),
    )(h1p, w, b)


def _c_bwd(dcv, h1p, w):
    S = h1p.shape[0]
    R, H, cw, nch = R_SEQ, _H_C, _CW_C, S // R_SEQ
    nlt = cw // LANE
    a_b, a_val, a_gate = CONV_C * SUB, (CONV_C + 1) * SUB, (CONV_C + 2) * SUB

    def body(dc_ref, h_ref, hh_ref, w_ref, dh_ref, dw_ref, db_ref, db1_ref, ext_u, ext_d, acc):
        j = pl.program_id(1)
        jj = nch - 1 - j

        @pl.when(j == 0)
        def _():
            ext_d[:, R:R + H, :] = jnp.zeros((nlt, H, LANE), F32)
            acc[...] = jnp.zeros_like(acc)

        for lt in range(nlt):
            ext_u[lt, 0:H, :] = jnp.where(
                jj == 0, 0.0, hh_ref[:, lt * LANE:(lt + 1) * LANE].astype(F32)
                * _sigmoid(hh_ref[:, cw + lt * LANE:cw + (lt + 1) * LANE].astype(F32)))

        def stage(r0, lt):
            rs, ls = pl.ds(r0, _RB), _lanes(lt)
            gate = h_ref[rs, _lanes(lt + nlt)].astype(F32)
            ext_u[lt, pl.ds(pl.multiple_of(r0 + H, SUB), _RB), :] = h_ref[rs, ls].astype(F32) * _sigmoid(gate)
            ext_d[lt, rs, :] = dc_ref[rs, ls]

        def first(r0, lt):
            ls = _lanes(lt)
            dc = dc_ref[pl.ds(r0, _RB), ls]
            acc[a_b:a_b + SUB, ls] += _psum8(dc)
            for k in range(CONV_C):
                tap = ext_u[lt, pl.ds(r0 + (H - (CONV_C - 1 - k)), _RB), :]
                acc[k * SUB:(k + 1) * SUB, ls] += _psum8(dc * tap)

        def second(r0, lt):
            rs, ls, lg = pl.ds(r0, _RB), _lanes(lt), _lanes(lt + nlt)
            du = w_ref[CONV_C - 1:CONV_C, ls] * ext_d[lt, rs, :]
            for k in range(CONV_C - 1):
                du = du + w_ref[k:k + 1, ls] * ext_d[lt, pl.ds(r0 + (CONV_C - 1 - k), _RB), :]
            val = h_ref[rs, ls].astype(F32)
            sg = _sigmoid(h_ref[rs, lg].astype(F32))
            dval = du * sg
            dgate = du * val * sg * (1.0 - sg)
            acc[a_val:a_val + SUB, ls] += _psum8(dval)
            acc[a_gate:a_gate + SUB, ls] += _psum8(dgate)
            dh_ref[rs, ls] = dval.astype(BF16)
            dh_ref[rs, lg] = dgate.astype(BF16)

        _sub_blocks(R, cw, stage)
        _sub_blocks(R, cw, first)
        _sub_blocks(R, cw, second)
        ext_d[:, R:R + H, :] = ext_d[:, 0:H, :]

        @pl.when(j == nch - 1)
        def _():
            for k in range(CONV_C):
                dw_ref[k:k + 1, :] = jnp.sum(acc[k * SUB:(k + 1) * SUB, :], axis=0, keepdims=True)
            db_ref[...] = jnp.sum(acc[a_b:a_b + SUB, :], axis=0, keepdims=True)
            db1_ref[:, 0:cw] = jnp.sum(acc[a_val:a_val + SUB, :], axis=0, keepdims=True)
            db1_ref[:, cw:2 * cw] = jnp.sum(acc[a_gate:a_gate + SUB, :], axis=0, keepdims=True)

    rows = lambda c, j: (nch - 1 - j, c)
    return pl.pallas_call(
        body, out_shape=[SDS((S, 2 * D), BF16), SDS((CONV_C, D), F32), SDS((1, D), F32), SDS((1, 2 * D), F32)],
        grid=(D // cw, nch),
        in_specs=[pl.BlockSpec((R, cw), rows), pl.BlockSpec((R, 2 * cw), rows),
                  pl.BlockSpec((H, 2 * cw), lambda c, j: (jnp.maximum((nch - 1 - j) * (R // H) - 1, 0), c)),
                  pl.BlockSpec((CONV_C, cw), lambda c, j: (0, c))],
        out_specs=[pl.BlockSpec((R, 2 * cw), rows), pl.BlockSpec((CONV_C, cw), lambda c, j: (0, c)),
                   pl.BlockSpec((1, cw), lambda c, j: (0, c)), pl.BlockSpec((1, 2 * cw), lambda c, j: (0, c))],
        scratch_shapes=[pltpu.VMEM((nlt, H + R, LANE), F32), pltpu.VMEM((nlt, R + H, LANE), F32),
                        pltpu.VMEM(((CONV_C + 3) * SUB, cw), F32)], name="conf_conv_bwd",
        compiler_params=_cp(2),
    )(dcv, h1p, h1p, w)


def _local_step(x, mem, tgt, W, fetch=None, send=None):
    G = {}
    W = dict(W)

    def arrive(group, after):
        if fetch is None:
            return None
        got, tok = fetch(group, after)
        for key, val in got.items():
            W[key] = {**W.get(key, {}), **val} if isinstance(val, dict) else val
        return tok

    def gain(g, tok):
        return g if tok is None else g + tok

    def sent(group):
        return None if send is None else send(group, G)

    def xattn_fwd(xin, n, l):
        tok = arrive(("xa", l), n)
        mn = _rms_fwd(mem, gain(W["xa_mem_norm"][l:l + 1], tok), f"xa_memnorm_fwd{l}")
        q = _mm_nn(n, W["xa_wq"][l], out_dtype=BF16, name=f"xa_q{l}")
        k = _mm_nn(mn, W["xa_wk"][l], out_dtype=BF16, name=f"xa_k{l}")
        v = _mm_nn(mn, W["xa_wv"][l], out_dtype=BF16, name=f"xa_v{l}")
        o = _attn_fwd(q, k, v, f"xa_attn_fwd{l}")
        xout, nout = _mm_nn(o, W["xa_wo"][l], out_dtype=F32, name=f"xa_o{l}", add=xin, norm=W["f_norm"][l:l + 1])
        return xout, nout, (xin, n, q, mn, k, v, o)

    def xattn_bwd(dx, dxb, saved, l):
        xin, n, q, mn, k, v, o = saved
        do = _mm_nt(dxb, W["xa_wo"][l], out_dtype=BF16, name=f"xa_do{l}")
        G[f"xa_wo{l}"] = _mm_tn(o, dxb, out_dtype=BF16, name=f"xa_dwo{l}")
        dq, dk, dv = _attn_bwd(q, k, v, do, f"xa_attn_bwd{l}")
        dkb, dvb = dk.astype(BF16), dv.astype(BF16)
        G[f"xa_wq{l}"] = _mm_tn(n, dq, out_dtype=BF16, name=f"xa_dwq{l}")
        G[f"xa_wk{l}"] = _mm_tn(mn, dkb, out_dtype=BF16, name=f"xa_dwk{l}")
        G[f"xa_wv{l}"] = _mm_tn(mn, dvb, out_dtype=BF16, name=f"xa_dwv{l}")
        tok = sent(("xa", l))
        dmn = _mm_nt(dkb, W["xa_wk"][l], out_dtype=F32, name=f"xa_dmn_k{l}")
        dmn = _mm_nt(dvb, W["xa_wv"][l], out_dtype=F32, name=f"xa_dmn_v{l}", add=dmn)
        (G[f"xa_mem_norm{l}"],) = _rms_bwd(mem, W["xa_mem_norm"][l:l + 1], dmn, None, f"xa_memnorm_bwd{l}")
        dx, dxb, G[f"xa_norm{l}"] = _mm_nt(dq, W["xa_wq"][l], out_dtype=F32, name=f"xa_dn{l}",
                                           rms=(xin, gain(W["xa_norm"][l:l + 1], tok), dx))
        return dx, dxb

    def ffn_fwd(xin, n, l, next_gain):
        tok = arrive(("f", l), n)
        hp = _mm_nn(n, W["f_w_up"][l], out_dtype=BF16, name=f"f_up{l}")
        act, gel, ud = _f_fwd(hp, W["f_dw_w"][l], gain(W["f_dw_b"][l:l + 1], tok), f"f_conv_fwd{l}")
        res = _mm_nn(act, W["f_w_down"][l], out_dtype=F32, name=f"f_down{l}", add=xin, norm=next_gain)
        xout, nout = res if next_gain is not None else (res, None)
        return xout, nout, (xin, n, hp, act, gel, ud)

    def ffn_bwd(dx, dxb, saved, l):
        xin, n, hp, act, gel, ud = saved
        dact = _mm_nt(dxb, W["f_w_down"][l], out_dtype=BF16, name=f"f_dact{l}")
        G[f"f_w_down{l}"] = _mm_tn(act, dxb, out_dtype=BF16, name=f"f_dwdown{l}")
        dhp, G[f"f_dw_w{l}"], G[f"f_dw_b{l}"] = _f_bwd(dact, hp, gel, ud, W["f_dw_w"][l], f"f_conv_bwd{l}")
        G[f"f_w_up{l}"] = _mm_tn(n, dhp, out_dtype=BF16, name=f"f_dwup{l}", blocks=_CW_F)
        tok = sent(("f", l))
        dx, dxb, G[f"f_norm{l}"] = _mm_nt(dhp, W["f_w_up"][l], out_dtype=F32, name=f"f_dn{l}",
                                          rms=(xin, gain(W["f_norm"][l:l + 1], tok), dx))
        return dx, dxb

    n0 = _rms_fwd(x, W["ab_norm"], "ab_norm_fwd")
    tok = arrive(("ab", 0), n0)
    a_par = (W["a_conv_w"], gain(W["a_conv_b"], tok), W["a_gate_x_w"], W["a_gate_x_b"], W["a_gate_a_w"],
             W["a_gate_a_b"], W["a_lambda"])
    b_par = (W["b_group_w"], W["b_group_b"], W["b_scale"])
    zp = _mm_nn(n0, W["ab_w_in"], out_dtype=BF16, name="ab_in")
    yab, h_a = _a_fwd(zp, *a_par)
    yab = _b_fwd(zp, yab, *b_par)
    arrive(("ab", 1), yab)
    x1, n1 = _mm_nn(yab, W["ab_w_out"], out_dtype=F32, name="ab_out", add=x, norm=W["xa_norm"][0:1])
    x2, n2, s_xa0 = xattn_fwd(x1, n1, 0)
    x3, n3, s_f0 = ffn_fwd(x2, n2, 0, W["c_norm"])
    tok = arrive(("c", 0), n3)
    h1p = _mm_nn(n3, W["c_w_pw1"], out_dtype=BF16, name="c_pw1", bias=gain(W["c_b_pw1"], tok))
    cv = _c_fwd(h1p, W["c_dw_w"], W["c_dw_b"])
    sc = _ln_silu_fwd(cv, W["c_ln_g"], W["c_ln_b"])
    x4, n4 = _mm_nn(sc, W["c_w_pw2"], out_dtype=F32, name="c_pw2", bias=W["c_b_pw2"], add=x3, norm=W["xa_norm"][1:2])
    x5, n5, s_xa1 = xattn_fwd(x4, n4, 1)
    x6, _, s_f1 = ffn_fwd(x5, n5, 1, None)
    loss, dx, dxb, G["final_norm"] = _loss_head(x6, W["final_norm"], tgt)

    dx, dxb = ffn_bwd(dx, dxb, s_f1, 1)
    dx, dxb = xattn_bwd(dx, dxb, s_xa1, 1)
    dsc = _mm_nt(dxb, W["c_w_pw2"], out_dtype=BF16, name="c_dsc")
    G["c_w_pw2"] = _mm_tn(sc, dxb, out_dtype=BF16, name="c_dwpw2")
    dcv, G["c_ln_g"], G["c_ln_b"], G["c_b_pw2"] = _ln_silu_bwd(dsc, cv, W["c_ln_g"], W["c_ln_b"], dx)
    dh1p, G["c_dw_w"], G["c_dw_b"], G["c_b_pw1"] = _c_bwd(dcv, h1p, W["c_dw_w"])
    G["c_w_pw1"] = _mm_tn(n3, dh1p, out_dtype=BF16, name="c_dwpw1", blocks=_CW_C)
    tok = sent(("c", 0))
    dx, dxb, G["c_norm"] = _mm_nt(dh1p, W["c_w_pw1"], out_dtype=F32, name="c_dn",
                                  rms=(x3, gain(W["c_norm"], tok), dx))
    dx, dxb = ffn_bwd(dx, dxb, s_f0, 0)
    dx, dxb = xattn_bwd(dx, dxb, s_xa0, 0)
    dyab = _mm_nt(dxb, W["ab_w_out"], out_dtype=BF16, name="ab_dyab")
    G["ab_w_out"] = _mm_tn(yab, dxb, out_dtype=BF16, name="ab_dwout")
    tok = sent(("ab", 1))
    a_par = (a_par[0], gain(a_par[1], tok)) + a_par[2:]
    (dzg, dzr, G["a_conv_w"], G["a_conv_b"], G["a_gate_x_w"], G["a_gate_x_b"], G["a_gate_a_w"], G["a_gate_a_b"],
     G["a_lambda"]) = _a_bwd(dyab, zp, h_a, *a_par)
    dzq, G["b_group_w"], G["b_group_b"], G["b_scale"] = _b_bwd(dyab, zp, *b_par)
    G["ab_w_in"] = jnp.concatenate(
        [_mm_tn(n0, dz, out_dtype=BF16, name=f"ab_dwin_{part}")
         for part, dz in (("gate", dzg), ("rec", dzr), ("pool", dzq))], axis=1)
    tok = sent(("ab", 0))
    dx, _, G["ab_norm"] = _mm_nt_cols([dzg, dzr, dzq], W["ab_w_in"], name="ab_dn",
                                      rms=(x, gain(W["ab_norm"], tok), dx))
    return loss, dx, G


def _my_place():
    x, y, c = lax.axis_index("x"), lax.axis_index("y"), lax.axis_index("c")
    return x, y, c


def _all_gather(shards, name):
    n = len(shards)

    def body(*refs):
        ins, outs = refs[:n], refs[n:2 * n]
        send_sems, recv_sems, local_sems = refs[2 * n:]
        x, y, c = _my_place()
        me, sibling = (x, y, c), (x, y, 1 - c)
        chips = [(1 - x, y), (x, 1 - y), (1 - x, 1 - y)]

        def slab(a, place):
            px, py, pc = place
            return outs[a].at[4 * px + 2 * py + pc]

        def copy(a, k, block, to, src=None):
            return pltpu.make_async_remote_copy(
                src_ref=slab(a, block) if src is None else src, dst_ref=slab(a, block),
                send_sem=send_sems.at[a, k], recv_sem=recv_sems.at[a, k], device_id=to, device_id_type=MESH)

        mine = [pltpu.make_async_copy(ins[a], slab(a, me), local_sems.at[a]) for a in range(n)]
        for cp in mine:
            cp.start()
        first = []
        for j, chip in enumerate(chips):
            first += [copy(a, 1 + j, me, (*chip, c), src=ins[a]) for a in range(n)]
        first += [copy(a, 0, me, sibling, src=ins[a]) for a in range(n)]
        for cp in first:
            cp.start()
        passed = []
        for j, chip in enumerate(chips):
            for a in range(n):
                copy(a, 1 + j, (*chip, c), me).wait_recv()
                cp = copy(a, 4 + j, (*chip, c), sibling)
                cp.start()
                passed.append(cp)
        for a in range(n):
            copy(a, 0, sibling, me).wait_recv()
        for j, chip in enumerate(chips):
            for a in range(n):
                copy(a, 4 + j, (*chip, 1 - c), me).wait_recv()
        for cp in first + passed:
            cp.wait_send()
        for cp in mine:
            cp.wait()

    any_spec = pl.BlockSpec(memory_space=pl.ANY)
    return pl.pallas_call(
        body, out_shape=[SDS((N_DEV,) + s.shape, s.dtype) for s in shards], in_specs=[any_spec] * n,
        out_specs=[any_spec] * n,
        scratch_shapes=[pltpu.SemaphoreType.DMA((n, 7)), pltpu.SemaphoreType.DMA((n, 7)), pltpu.SemaphoreType.DMA((n,))],
        name=name,
    )(*shards)


_HBM = pl.BlockSpec(memory_space=pltpu.HBM)
_SEM = pl.BlockSpec(memory_space=pltpu.SEMAPHORE)
_EFFECT = pltpu.SideEffectType.DATAFLOW_SIDE_EFFECTING


def _peer_places():
    x, y, c = _my_place()
    peers = []
    for k in range(1, N_DEV):
        px = 1 - x if (k >> 2) & 1 else x
        py = 1 - y if (k >> 1) & 1 else y
        pc = 1 - c if k & 1 else c
        peers.append(((px, py, pc), 4 * px + 2 * py + pc))
    return (x, y, c), 4 * x + 2 * y + c, peers


def _send_start(srcs, per_dest, name):
    n = len(srcs)
    lands = [lax.empty((N_DEV,) + (s.shape[1:] if per_dest else s.shape), s.dtype) for s in srcs]

    def body(*refs):
        src, land = refs[:n], refs[n:2 * n]
        outs = refs[2 * n:]
        send, recv, token = outs[:n], outs[n:2 * n], outs[4 * n]
        _, me, peers = _peer_places()
        for a in range(n):
            for peer, pidx in peers:
                pltpu.make_async_remote_copy(
                    src_ref=src[a].at[pidx] if per_dest else src[a], dst_ref=land[a].at[me], send_sem=send[a],
                    recv_sem=recv[a], device_id=peer, device_id_type=MESH).start()
        token[...] = jnp.zeros_like(token)

    hbm = lambda a: pltpu.HBM(a.shape, a.dtype)
    sem = pltpu.SemaphoreType.DMA(())
    res = pl.pallas_call(
        body, name=name,
        out_shape=tuple([sem] * (2 * n) + [hbm(s) for s in srcs] + [hbm(l) for l in lands]
                        + [SDS((SUB, LANE), F32)]),
        in_specs=[_HBM] * (2 * n),
        out_specs=tuple([_SEM] * (2 * n) + [_HBM] * (2 * n) + [pl.BlockSpec(memory_space=pltpu.VMEM)]),
        input_output_aliases={i: 2 * n + i for i in range(2 * n)},
        compiler_params=pltpu.CompilerParams(has_side_effects=_EFFECT),
    )(*[pltpu.with_memory_space_constraint(s, pltpu.HBM) for s in srcs],
      *[pltpu.with_memory_space_constraint(l, pltpu.HBM) for l in lands])
    return res[:n], res[n:2 * n], res[2 * n:3 * n], res[3 * n:4 * n], res[4 * n]


def _send_wait(send, recv, srcs, lands, after, per_dest, name):
    n = len(srcs)

    def body(*refs):
        src, land = refs[:n], refs[n:2 * n]
        send_s, recv_s = refs[2 * n:3 * n], refs[3 * n:4 * n]
        token = refs[-1]
        place, _, _ = _peer_places()
        for a in range(n):
            seven = land[a].at[pl.ds(0, N_DEV - 1)]
            copy = pltpu.make_async_remote_copy(
                src_ref=src[a].at[pl.ds(0, N_DEV - 1)] if per_dest else seven, dst_ref=seven, send_sem=send_s[a],
                recv_sem=recv_s[a], device_id=place, device_id_type=MESH)
            copy.wait_send()
            copy.wait_recv()
        token[...] = jnp.zeros_like(token)

    hbm = lambda a: pltpu.HBM(a.shape, a.dtype)
    res = pl.pallas_call(
        body, name=name,
        out_shape=tuple([hbm(s) for s in srcs] + [hbm(l) for l in lands] + [SDS((SUB, LANE), F32)]),
        in_specs=[_HBM] * (2 * n) + [_SEM] * (2 * n) + [pl.BlockSpec(memory_space=pl.ANY)],
        out_specs=tuple([_HBM] * (2 * n) + [pl.BlockSpec(memory_space=pltpu.VMEM)]),
        input_output_aliases={i: i for i in range(2 * n)},
        compiler_params=pltpu.CompilerParams(has_side_effects=_EFFECT),
    )(*srcs, *lands, *send, *recv, after)
    return res[:n], res[n:2 * n], res[2 * n]


def _adamw_math(w, g, m, v):
    m = ADAM_B1 * m + (1.0 - ADAM_B1) * g
    v = ADAM_B2 * v + (1.0 - ADAM_B2) * (g * g)
    m_hat = m / (1.0 - ADAM_B1 ** ADAM_STEP)
    v_hat = v / (1.0 - ADAM_B2 ** ADAM_STEP)
    delta = -ADAM_LR * (m_hat / (jnp.sqrt(v_hat) + ADAM_EPS) + ADAM_WD * w)
    return delta, m, v


def _row_tile(r, c, itemsize_rows):
    cap = max(SUB, (itemsize_rows // (4 * c)) // SUB * SUB)
    if r <= cap:
        return r
    best = None
    for t in range(SUB, cap + 1, SUB):
        if r % t == 0:
            best = t
    return best if best is not None else r


def _sum_adamw(landing, w, m, v, name, layer=0, prev=None):
    _, r, c = landing.shape
    tr = _row_tile(r, c, 2 << 20)
    off = layer * (r // tr)

    def body(l_ref, w_ref, m_ref, v_ref, *rest):
        g_ref, d_ref, mo_ref, vo_ref = rest[-4:]
        g = l_ref[0].astype(F32)
        for s in range(1, N_DEV):
            g = g + l_ref[s].astype(F32)
        g_ref[...] = g
        d_ref[...], mo_ref[...], vo_ref[...] = _adamw_math(w_ref[...], g, m_ref[...], v_ref[...])

    blk = pl.BlockSpec((tr, c), lambda i: (i + off, 0))
    n_prev = 0 if prev is None else 4
    return pl.pallas_call(
        body, out_shape=[SDS(w.shape, F32)] * 4, grid=(r // tr,),
        in_specs=[pl.BlockSpec((N_DEV, tr, c), lambda i: (0, i, 0)), blk, blk, blk]
        + [pl.BlockSpec(memory_space=pl.ANY)] * n_prev,
        out_specs=[blk] * 4, input_output_aliases={4 + i: i for i in range(n_prev)}, name=name,
        compiler_params=_cp(1),
    )(landing, w, m, v, *([] if prev is None else prev))


def _sum8(landing, name):
    _, r, c = landing.shape

    def body(l_ref, g_ref):
        g = l_ref[0]
        for s in range(1, N_DEV):
            g = g + l_ref[s]
        g_ref[...] = g

    return pl.pallas_call(body, out_shape=SDS((r, c), F32), name=name, compiler_params=_cp(0))(landing)


def _adamw_small(repl_pack, own_pack, P, M, V):
    table, off = [], 0
    for name, shape in _REPL.items():
        table.append((name, shape if len(shape) > 1 else (1,) + shape, 0, off // LANE))
        off += _size(shape)
    off = _REPL_ROWS * LANE
    for name, shape in _SMALL_SHARDED.items():
        table.append((name, shape, 1, off // LANE))
        off += _size(shape)
    n = len(table)

    def body(*refs):
        packs, ins, outs = refs[:2], refs[2:2 + 3 * n], refs[2 + 3 * n:]
        for p, (_, shape, which, r0) in enumerate(table):
            w_ref, m_ref, v_ref = ins[3 * p:3 * p + 3]
            g_ref, d_ref, mo_ref, vo_ref = outs[4 * p:4 * p + 4]
            pack, rows, q = packs[which], shape[-2], shape[-1] // LANE
            lead = [()]
            for dim in shape[:-2]:
                lead = [t + (i,) for t in lead for i in range(dim)]
            for li, idx in enumerate(lead):
                if q == 1:
                    dst = g_ref.at[idx] if idx else g_ref
                    dst[...] = pack[r0 + li * rows:r0 + (li + 1) * rows, :]
                    continue
                for i in range(rows):
                    for k in range(q):
                        row = r0 + (li * rows + i) * q + k
                        g_ref[idx + (slice(i, i + 1), slice(k * LANE, (k + 1) * LANE))] = pack[row:row + 1, :]
            d_ref[...], mo_ref[...], vo_ref[...] = _adamw_math(w_ref[...], g_ref[...], m_ref[...], v_ref[...])

    ins, out_shape = [], []
    for name, shape, _, _ in table:
        ins += [t[name].reshape(shape) for t in (P, M, V)]
        out_shape += [SDS(shape, F32)] * 4
    res = pl.pallas_call(body, out_shape=out_shape, name="adamw_small", compiler_params=_cp(0))(
        repl_pack, own_pack, *ins)
    dicts = ({}, {}, {}, {})
    for p, (name, shape, _, _) in enumerate(table):
        for d, arr in zip(dicts, res[4 * p:4 * p + 4]):
            d[name] = arr.reshape(P[name].shape)
    return dicts


_BIG = {
    "ab_w_in": (1, D, 320), "ab_w_out": (1, 192, D), "c_w_pw1": (1, D, 256), "c_w_pw2": (1, 128, D),
    "xa_wq": (2, 128, D), "xa_wk": (2, 128, D), "xa_wv": (2, 128, D), "xa_wo": (2, 128, D),
    "f_w_up": (2, D, 768), "f_w_down": (2, 384, D),
}
_SMALL_SHARDED = {
    "a_conv_w": (1, 4, 128), "c_norm": (1, 128), "c_b_pw1": (1, 256), "c_dw_w": (1, 31, 128), "c_dw_b": (1, 128),
    "c_ln_g": (1, 128), "c_ln_b": (1, 128), "c_b_pw2": (1, 128), "f_dw_w": (2, 3, 384),
}
_REPL = {
    "ab_norm": (1, D), "a_conv_b": (1, D), "a_gate_x_w": (1, 8, 128, 128), "a_gate_x_b": (1, D),
    "a_gate_a_w": (1, 8, 128, 128), "a_gate_a_b": (1, D), "a_lambda": (1, D), "b_group_w": (1, 4, 128, 128),
    "b_group_b": (1, 512), "b_scale": (1, 512), "xa_norm": (2, D), "xa_mem_norm": (2, D), "f_norm": (2, D),
    "f_dw_b": (2, D_FF), "final_norm": (D,),
}


def _size(shape):
    n = 1
    for s in shape:
        n *= s
    return n


_N_SS = sum(_size(s) for s in _SMALL_SHARDED.values())
_N_REPL = sum(_size(s) for s in _REPL.values())
_REPL_ROWS = -(-_N_REPL // (N_DEV * SUB * LANE)) * SUB
_SS_ROWS = _N_SS // LANE
_SMALL_ROWS = -(-(_REPL_ROWS + _SS_ROWS) // SUB) * SUB


def _pack(parts, rows):
    flat = jnp.concatenate([p.reshape(-1).astype(F32) for p in parts])
    return jnp.pad(flat, (0, rows * LANE - flat.shape[0])).reshape(rows, LANE)


def _pair_blocks(v, bw):
    lead, n = v.shape[:-1], v.shape[-1]
    return jnp.swapaxes(v.reshape(lead + (2, n // (2 * bw), bw)), -3, -2).reshape(lead + (n,))


def _unpair_blocks(v, bw):
    lead, n = v.shape[:-1], v.shape[-1]
    return jnp.swapaxes(v.reshape(lead + (n // (2 * bw), 2, bw)), -3, -2).reshape(lead + (n,))


_GROUPS = {
    ("ab", 0): (("ab_w_in", 0),),
    ("ab", 1): (("ab_w_out", 0),),
    ("xa", 0): (("xa_wq", 0), ("xa_wk", 0), ("xa_wv", 0), ("xa_wo", 0)),
    ("f", 0): (("f_w_up", 0), ("f_w_down", 0)),
    ("c", 0): (("c_w_pw1", 0), ("c_w_pw2", 0)),
    ("xa", 1): (("xa_wq", 1), ("xa_wk", 1), ("xa_wv", 1), ("xa_wo", 1)),
    ("f", 1): (("f_w_up", 1), ("f_w_down", 1)),
}
_SEND_GROUPS = _GROUPS


def _weight_layout(name, g):
    if name == "ab_w_in":
        return jnp.swapaxes(g, 0, 1).reshape(D, N_DEV * 320)
    if name in ("c_w_pw1", "f_w_up"):
        return g
    return g.reshape(N_DEV * g.shape[1], D)


def _grad_blocks(name, l, G):
    _, r, c = _BIG[name]
    if name == "ab_w_in":
        return jnp.swapaxes(G[name].reshape(D, N_DEV, 320), 0, 1)
    if name == "c_w_pw1":
        return G[name]
    if name == "f_w_up":
        return G[f"{name}{l}"]
    return (G[name] if _BIG[name][0] == 1 else G[f"{name}{l}"]).reshape(N_DEV, r, c)


def _small_layouts(sm):
    W = {}
    sm = sm.reshape(N_DEV, -1)
    off = 0
    for name, shape in _SMALL_SHARDED.items():
        n = _size(shape)
        blocks = sm[:, off:off + n].reshape((N_DEV,) + shape)
        off += n
        W[name] = jnp.moveaxis(blocks, 0, -2).reshape(shape[:-1] + (N_DEV * shape[-1],))
    W["a_conv_w"], W["c_dw_w"] = W["a_conv_w"][0], W["c_dw_w"][0]
    W["c_b_pw1"] = _pair_blocks(W["c_b_pw1"], _CW_C)
    return W


def _with_own(land, src, me, per_dest):
    own = lax.dynamic_slice_in_dim(src, me, 1, 0) if per_dest else src[None]
    return lax.dynamic_update_slice_in_dim(land, own, me, 0)


def _to_dest_major(g, shape):
    full = g.reshape(shape[:-1] + (N_DEV, shape[-1]))
    return jnp.moveaxis(full, -2, 0).reshape(N_DEV, -1)


def kernel(x, mem, ab_norm, ab_w_in, a_conv_w, a_conv_b, a_gate_x_w, a_gate_x_b, a_gate_a_w, a_gate_a_b, a_lambda, b_group_w, b_group_b, b_scale, ab_w_out, c_norm, c_w_pw1, c_b_pw1, c_dw_w, c_dw_b, c_ln_g, c_ln_b, c_w_pw2, c_b_pw2, xa_norm, xa_mem_norm, xa_wq, xa_wk, xa_wv, xa_wo, f_norm, f_w_up, f_dw_w, f_dw_b, f_w_down, final_norm, loss_target, m_ab_norm, m_ab_w_in, m_a_conv_w, m_a_conv_b, m_a_gate_x_w, m_a_gate_x_b, m_a_gate_a_w, m_a_gate_a_b, m_a_lambda, m_b_group_w, m_b_group_b, m_b_scale, m_ab_w_out, m_c_norm, m_c_w_pw1, m_c_b_pw1, m_c_dw_w, m_c_dw_b, m_c_ln_g, m_c_ln_b, m_c_w_pw2, m_c_b_pw2, m_xa_norm, m_xa_mem_norm, m_xa_wq, m_xa_wk, m_xa_wv, m_xa_wo, m_f_norm, m_f_w_up, m_f_dw_w, m_f_dw_b, m_f_w_down, m_final_norm, v_ab_norm, v_ab_w_in, v_a_conv_w, v_a_conv_b, v_a_gate_x_w, v_a_gate_x_b, v_a_gate_a_w, v_a_gate_a_b, v_a_lambda, v_b_group_w, v_b_group_b, v_b_scale, v_ab_w_out, v_c_norm, v_c_w_pw1, v_c_b_pw1, v_c_dw_w, v_c_dw_b, v_c_ln_g, v_c_ln_b, v_c_w_pw2, v_c_b_pw2, v_xa_norm, v_xa_mem_norm, v_xa_wq, v_xa_wk, v_xa_wv, v_xa_wo, v_f_norm, v_f_w_up, v_f_dw_w, v_f_dw_b, v_f_w_down, v_final_norm):
    args = dict(locals())
    P = {n: args[n] for n in _NAMES}
    M = {n: args["m_" + n] for n in _NAMES}
    V = {n: args["v_" + n] for n in _NAMES}

    me = 4 * lax.axis_index("x") + 2 * lax.axis_index("y") + lax.axis_index("c")

    in_flight = {}

    def launch(groups, tok):
        shards, n_of = [], {}
        for grp in groups:
            for name, l in _GROUPS[grp]:
                w = P[name][l] if tok is None else P[name][l] + tok
                shards.append(w.astype(BF16))
            if grp == ("ab", 0):
                shards.append(_pack([P[n] for n in _SMALL_SHARDED], _SS_ROWS + 4))
            n_of[grp] = len(shards)
        res = _send_start(shards, False, "gather_start_" + "_".join(g[0] + str(g[1]) for g in groups))
        lo = 0
        for grp in groups:
            in_flight[grp] = [r[lo:n_of[grp]] for r in res[:4]]
            lo = n_of[grp]
        return res[4][:1, :1]

    follow = {("ab", 0): [("ab", 1), ("xa", 0), ("f", 0)], ("xa", 0): [("c", 0)], ("f", 0): [("xa", 1)],
              ("c", 0): [("f", 1)]}

    def fetch(grp, after):
        send_s, recv_s, srcs, lands = in_flight.pop(grp)
        srcs, lands, tok = _send_wait(send_s, recv_s, srcs, lands, after, False, f"gather_wait_{grp[0]}{grp[1]}")
        tok = launch(follow[grp], tok[:1, :1]) if grp in follow else None
        full = [_with_own(land, src, me, False) for land, src in zip(lands, srcs)]
        out = {}
        for (name, l), g in zip(_GROUPS[grp], full):
            w = _weight_layout(name, g)
            if _BIG[name][0] == 1:
                out[name] = w
            else:
                out[name] = {l: w}
        if grp == ("ab", 0):
            out.update(_small_layouts(full[-1]))
        return out, tok

    zero = launch([("ab", 0)], None)

    pending = []

    def send(grp, G):
        members = _SEND_GROUPS[grp]
        res = _send_start([_grad_blocks(name, l, G) for name, l in members], True, f"send_{grp[0]}{grp[1]}")
        pending.append((members, res))
        return res[4][:1, :1]

    W = {n: P[n] for n in _REPL}
    W["ab_norm"] = P["ab_norm"] + zero
    W["final_norm"] = P["final_norm"].reshape(1, D)
    W["a_gate_x_w"], W["a_gate_a_w"], W["b_group_w"] = P["a_gate_x_w"][0], P["a_gate_a_w"][0], P["b_group_w"][0]
    loss, grad_x, G = _local_step(x[0], mem[0], loss_target[0], W, fetch, send)
    loss = lax.psum(loss[0, 0], ("x", "y", "c"))

    Gs = dict(G)
    Gs["c_b_pw1"] = _unpair_blocks(G["c_b_pw1"], _CW_C)
    Gs["f_dw_w"] = jnp.stack([G["f_dw_w0"], G["f_dw_w1"]])
    Gs["a_conv_w"], Gs["c_dw_w"] = G["a_conv_w"][None], G["c_dw_w"][None]
    for n in ("xa_norm", "xa_mem_norm", "f_norm", "f_dw_b"):
        Gs[n] = jnp.concatenate([G[f"{n}0"], G[f"{n}1"]], axis=0)
    for n in ("a_gate_x_w", "a_gate_a_w", "b_group_w"):
        Gs[n] = G[n][None]
    repl_flat = jnp.concatenate([Gs[n].reshape(-1) for n in _REPL])
    repl_rows = jnp.pad(repl_flat, (0, N_DEV * _REPL_ROWS * LANE - _N_REPL)).reshape(N_DEV, _REPL_ROWS, LANE)
    ss_rows = jnp.concatenate([_to_dest_major(Gs[n], s) for n, s in _SMALL_SHARDED.items()], axis=1)
    ss_rows = ss_rows.reshape(N_DEV, _SS_ROWS, LANE)
    small_pack = jnp.concatenate(
        [repl_rows, ss_rows, jnp.zeros((N_DEV, _SMALL_ROWS - _REPL_ROWS - _SS_ROWS, LANE), F32)], axis=1)
    last = _send_start([small_pack], True, "send_small")
    pending.append(((("small", 0),), last))

    members = [m for mem_, _ in pending for m in mem_]
    cat = [[a for _, res in pending for a in res[i]] for i in range(4)]
    srcs, lands, _ = _send_wait(cat[0], cat[1], cat[2], cat[3], grad_x, True, "send_wait")
    landed = {m: _with_own(land, src, me, True) for m, land, src in zip(members, lands, srcs)}

    out_g, out_d, out_m, out_v = {}, {}, {}, {}
    for name, (layers, r, c) in _BIG.items():
        shape = P[name].shape
        w2, m2, v2 = [t[name].reshape(layers * r, c) for t in (P, M, V)]
        res = None
        for l in range(layers):
            res = _sum_adamw(landed[(name, l)], w2, m2, v2, f"adamw_{name}{l}", layer=l, prev=res)
        out_g[name], out_d[name], out_m[name], out_v[name] = [t.reshape(shape) for t in res]

    small_sum = _sum8(landed[("small", 0)], "sum_small")
    (repl_all,) = _all_gather([small_sum[:_REPL_ROWS]], "gather_small_grads")
    for out, got in zip((out_g, out_d, out_m, out_v),
                        _adamw_small(repl_all.reshape(N_DEV * _REPL_ROWS, LANE), small_sum, P, M, V)):
        out.update(got)

    return (loss, grad_x[None], *[out_g[n] for n in _NAMES], *[out_d[n] for n in _NAMES],
            *[out_m[n] for n in _NAMES], *[out_v[n] for n in _NAMES])


_NAMES = ("ab_norm", "ab_w_in", "a_conv_w", "a_conv_b", "a_gate_x_w", "a_gate_x_b", "a_gate_a_w", "a_gate_a_b",
          "a_lambda", "b_group_w", "b_group_b", "b_scale", "ab_w_out", "c_norm", "c_w_pw1", "c_b_pw1", "c_dw_w",
          "c_dw_b", "c_ln_g", "c_ln_b", "c_w_pw2", "c_b_pw2", "xa_norm", "xa_mem_norm", "xa_wq", "xa_wk", "xa_wv",
          "xa_wo", "f_norm", "f_w_up", "f_dw_w", "f_dw_b", "f_w_down", "final_norm")
```

```python
import functools

import jax
import jax.numpy as jnp
from jax import lax
from jax.experimental import pallas as pl
from jax.experimental.pallas import tpu as pltpu

F32, BF16 = jnp.float32, jnp.bfloat16
SDS = jax.ShapeDtypeStruct
MESH = pl.DeviceIdType.MESH

N_DEV = 8
D = 1024
N_MEM = 256
XA_HEADS, XA_HD = 4, 256
HD_A = 128
CONV_A, CONV_C, CONV_F = 4, 31, 3
C_RG = 8.0
POOL_WINDOWS = (2, 4, 8, 16)
D_FF = 3 * D
EPS = 1e-6
ADAM_LR, ADAM_B1, ADAM_B2, ADAM_EPS, ADAM_WD, ADAM_STEP = 0.001, 0.9, 0.999, 1e-08, 0.01, 10

LANE = 128
SUB = 8
VMEM_LIMIT = 56 * 1024 * 1024
R_SEQ = 512
R_RGLRU = 256
TM_ROW = 512


def _cp(n_axes):
    return pltpu.CompilerParams(dimension_semantics=("arbitrary",) * n_axes, vmem_limit_bytes=VMEM_LIMIT)


def _tile(n, pref):
    if n <= pref:
        return n
    best = None
    for t in range(LANE, pref + 1, LANE):
        if n % t == 0:
            best = t
    assert best is not None, (n, pref)
    return best


def _perm2(n):
    return (n % 2) * 4 + n // 2


_NN = (((1,), (0,)), ((), ()))
_NT = (((1,), (1,)), ((), ()))
_TN = (((0,), (0,)), ((), ()))


def _mm_call(name, grid, ab, ab_specs, dims, acc_shape, extras, outs, finish, from_ref=False):
    nk = grid[2]
    n_ab, n_ex, n_out = len(ab), len(extras), len(outs)
    use_acc = nk > 1 or from_ref

    def product(refs):
        r = lax.dot_general(refs[0][...], refs[1][...], dims, preferred_element_type=F32)
        for i in range(1, n_ab):
            r = r + lax.dot_general(refs[2 * i][...], refs[2 * i + 1][...], dims, preferred_element_type=F32)
        return r

    def body(*refs):
        rest = refs[2 * n_ab:]
        ex_refs, o_refs = rest[:n_ex], rest[n_ex:n_ex + n_out]
        first_rows = pl.program_id(0) == 0
        if not use_acc:
            finish(product(refs), ex_refs, o_refs, first_rows)
            return
        acc = rest[n_ex + n_out]
        if nk == 1:
            acc[...] = product(refs)
            finish(acc, ex_refs, o_refs, first_rows)
            return
        k = pl.program_id(2)

        @pl.when(k == 0)
        def _():
            acc[...] = jnp.zeros_like(acc)

        acc[...] += product(refs)

        @pl.when(k == nk - 1)
        def _():
            finish(acc if from_ref else acc[...], ex_refs, o_refs, first_rows)

    res = pl.pallas_call(
        body, out_shape=[o for o, _ in outs], grid=grid,
        in_specs=list(ab_specs) + [s for _, s in extras], out_specs=[s for _, s in outs],
        scratch_shapes=[pltpu.VMEM(acc_shape, F32)] if use_acc else [], name=name, compiler_params=_cp(3),
    )(*[t for pair in ab for t in pair], *[e for e, _ in extras])
    return res[0] if n_out == 1 else res


def _finish_sum(r, ex_refs, o_refs, first_rows):
    del first_rows
    for e in ex_refs:
        r = r + e[...]
    o_refs[0][...] = r.astype(o_refs[0].dtype)


def _finish_sum_norm(r, ex_refs, o_refs, first_rows):
    del first_rows
    for e in ex_refs[:-1]:
        r = r + e[...]
    o_refs[0][...] = r
    o_refs[1][...] = ((r * lax.rsqrt(jnp.mean(r * r, axis=-1, keepdims=True) + EPS)) * ex_refs[-1][...]).astype(BF16)


_EPI_ROWS = 16


def _finish_rms_bwd(r_ref, ex_refs, o_refs, first_rows):
    x_ref, g_ref, dres_ref = ex_refs
    dx_ref, dxb_ref, dg_ref = o_refs

    @pl.when(first_rows)
    def _():
        dg_ref[...] = jnp.zeros_like(dg_ref)

    gv = g_ref[...]
    inv_d = 1.0 / r_ref.shape[1]

    def step(i, dg_acc):
        groups = [pl.ds(pl.multiple_of(i * (2 * _EPI_ROWS) + u * _EPI_ROWS, _EPI_ROWS), _EPI_ROWS) for u in range(2)]
        sums = []
        for rows in groups:
            r, xf = r_ref[rows, :], x_ref[rows, :]
            sums.append((jnp.sum(xf * xf, axis=-1, keepdims=True), jnp.sum((r * gv) * xf, axis=-1, keepdims=True)))
        for rows, (sxx, sax) in zip(groups, sums):
            r, xf = r_ref[rows, :], x_ref[rows, :]
            rs = lax.rsqrt(sxx * inv_d + EPS)
            dg_acc = dg_acc + _psum8(r * (xf * rs))
            dx = rs * (r * gv) - xf * (rs * rs * (sax * rs * inv_d)) + dres_ref[rows, :]
            dx_ref[rows, :] = dx
            dxb_ref[rows, :] = dx.astype(BF16)
        return dg_acc

    dg_acc = lax.fori_loop(0, r_ref.shape[0] // (2 * _EPI_ROWS), step, jnp.zeros((SUB, r_ref.shape[1]), F32))
    dg_ref[...] += jnp.sum(dg_acc, axis=0, keepdims=True)


def _rms_bwd_io(M, tm, x, g, dres):
    rows = pl.BlockSpec((tm, D), lambda m, n, k: (m, 0))
    vec = pl.BlockSpec((1, D), lambda m, n, k: (0, 0))
    return ([(x, rows), (g, vec), (dres, rows)],
            [(SDS((M, D), F32), rows), (SDS((M, D), BF16), rows), (SDS((1, D), F32), vec)])


_K_WHOLE = 3072


def _mm_nn(a, b, *, out_dtype, name, bias=None, add=None, norm=None):
    M, K = a.shape
    tk = K if K <= _K_WHOLE else _tile(K, 1024)
    tm = _tile(M, 1024 if K <= 1024 and norm is None else 512)
    if b.ndim == 3:
        nb, _, bw = b.shape
        N, tn, nn = nb * bw, bw, nb
        b_spec = pl.BlockSpec((None, tk, bw), lambda m, n, k: (_perm2(n), k, 0))
    else:
        N = b.shape[1]
        tn = _tile(N, 1024)
        nn = N // tn
        b_spec = pl.BlockSpec((tk, tn), lambda m, n, k: (k, n))
    tile = pl.BlockSpec((tm, tn), lambda m, n, k: (m, n))
    vec = pl.BlockSpec((1, tn), lambda m, n, k: (0, n))
    extras = ([] if bias is None else [(bias, vec)]) + ([] if add is None else [(add, tile)])
    outs, finish = [(SDS((M, N), out_dtype), tile)], _finish_sum
    if norm is not None:
        assert tn == N == D and out_dtype == F32
        extras.append((norm, vec))
        outs, finish = outs + [(SDS((M, N), BF16), tile)], _finish_sum_norm
    return _mm_call(name, (M // tm, nn, K // tk), [(a, b)], [pl.BlockSpec((tm, tk), lambda m, n, k: (m, k)), b_spec],
                    _NN, (tm, tn), extras, outs, finish)


def _mm_nt(a, b, *, out_dtype, name, add=None, rms=None):
    M, N = a.shape
    if b.ndim == 3:
        nb, Ko, bw = b.shape
        tm = _tile(M, 1024)
        tn, tk, nk = _tile(Ko, 1024), bw, nb
        b_spec = pl.BlockSpec((None, tn, bw), lambda m, n, k: (_perm2(k), n, 0))
    else:
        Ko = b.shape[0]
        tk = N if N <= _K_WHOLE else _tile(N, 1024)
        tm = _tile(M, 1024 if N <= 1024 and rms is None else 512)
        tn = _tile(Ko, 1024)
        nk = N // tk
        b_spec = pl.BlockSpec((tn, tk), lambda m, n, k: (n, k))
    tile = pl.BlockSpec((tm, tn), lambda m, n, k: (m, n))
    extras = [] if add is None else [(add, tile)]
    outs, finish = [(SDS((M, Ko), out_dtype), tile)], _finish_sum
    if rms is not None:
        assert tn == Ko == D and add is None
        (extras, outs), finish = _rms_bwd_io(M, tm, *rms), _finish_rms_bwd
    return _mm_call(name, (M // tm, Ko // tn, nk), [(a, b)], [pl.BlockSpec((tm, tk), lambda m, n, k: (m, k)), b_spec],
                    _NT, (tm, tn), extras, outs, finish, from_ref=rms is not None)


def _mm_nt_cols(parts, b, *, name, rms):
    M = parts[0].shape[0]
    tm = _tile(M, 512)
    specs, off = [], 0
    for p in parts:
        w = p.shape[1]
        assert off % w == 0
        specs.append(pl.BlockSpec((tm, w), lambda m, n, k: (m, 0)))
        specs.append(pl.BlockSpec((D, w), functools.partial(lambda m, n, k, o: (0, o), o=off // w)))
        off += w
    extras, outs = _rms_bwd_io(M, tm, *rms)
    return _mm_call(name, (M // tm, 1, 1), [(p, b) for p in parts], specs, _NT, (tm, D), extras, outs, _finish_rms_bwd,
                    from_ref=True)


def _mm_tn(a, b, *, out_dtype, name, blocks=None):
    S, Ka = a.shape
    Nb = b.shape[1]
    tm, tk = _tile(Ka, 1024), _tile(S, 2048)
    if blocks is not None:
        bw = blocks
        tn, nn = bw, Nb // bw
        out = (SDS((nn, Ka, bw), out_dtype), pl.BlockSpec((None, tm, bw), lambda m, n, k: (_perm2(n), m, 0)))
    else:
        tn = _tile(Nb, 1024)
        nn = Nb // tn
        out = (SDS((Ka, Nb), out_dtype), pl.BlockSpec((tm, tn), lambda m, n, k: (m, n)))
    return _mm_call(name, (Ka // tm, nn, S // tk), [(a, b)],
                    [pl.BlockSpec((tk, tm), lambda m, n, k: (k, m)), pl.BlockSpec((tk, tn), lambda m, n, k: (k, n))],
                    _TN, (tm, tn), [], [out], _finish_sum)


def _row(tm, c):
    return pl.BlockSpec((tm, c), lambda i: (i, 0))


def _full(shape):
    nd = len(shape)
    return pl.BlockSpec(shape, lambda i: (0,) * nd)


def _rms_fwd(x, g, name):
    S = x.shape[0]
    tm = min(S, TM_ROW)

    def body(x_ref, g_ref, o_ref):
        xf = x_ref[...]
        r = lax.rsqrt(jnp.mean(xf * xf, axis=-1, keepdims=True) + EPS)
        o_ref[...] = ((xf * r) * g_ref[...]).astype(BF16)

    return pl.pallas_call(body, out_shape=SDS((S, D), BF16), grid=(S // tm,), in_specs=[_row(tm, D), _full((1, D))],
                          out_specs=_row(tm, D), name=name, compiler_params=_cp(1))(x, g)


def _rms_bwd(x, g, dn, dres, name):
    S = x.shape[0]
    tm = min(S, TM_ROW)
    want_dx = dres is not None

    def body(x_ref, g_ref, dn_ref, *rest):
        i = pl.program_id(0)
        dg_ref = rest[-1]

        @pl.when(i == 0)
        def _():
            dg_ref[...] = jnp.zeros_like(dg_ref)

        xf = x_ref[...]
        r = lax.rsqrt(jnp.mean(xf * xf, axis=-1, keepdims=True) + EPS)
        y = xf * r
        dn_v = dn_ref[...]
        dg_ref[...] += jnp.sum(dn_v * y, axis=0, keepdims=True)
        if want_dx:
            dres_ref, dx_ref, dxb_ref = rest[0], rest[1], rest[2]
            dy = dn_v * g_ref[...]
            dx = r * (dy - y * jnp.mean(dy * y, axis=-1, keepdims=True)) + dres_ref[...]
            dx_ref[...] = dx
            dxb_ref[...] = dx.astype(BF16)

    ins = [x, g, dn] + ([dres] if want_dx else [])
    in_specs = [_row(tm, D), _full((1, D)), _row(tm, D)] + ([_row(tm, D)] if want_dx else [])
    outs = ([SDS((S, D), F32), SDS((S, D), BF16)] if want_dx else []) + [SDS((1, D), F32)]
    out_specs = ([_row(tm, D), _row(tm, D)] if want_dx else []) + [_full((1, D))]
    return pl.pallas_call(body, out_shape=outs, grid=(S // tm,), in_specs=in_specs, out_specs=out_specs, name=name,
                          compiler_params=_cp(1))(*ins)


def _loss_head(x, g, tgt):
    S = x.shape[0]
    tm = min(S, TM_ROW)

    def body(x_ref, g_ref, t_ref, loss_ref, dx_ref, dxb_ref, dg_ref):
        i = pl.program_id(0)

        @pl.when(i == 0)
        def _():
            loss_ref[...] = jnp.zeros_like(loss_ref)
            dg_ref[...] = jnp.zeros_like(dg_ref)

        xf = x_ref[...]
        r = lax.rsqrt(jnp.mean(xf * xf, axis=-1, keepdims=True) + EPS)
        y = xf * r
        gv = g_ref[...]
        err = y * gv - t_ref[...]
        per_row = jnp.mean(err * err, axis=-1, keepdims=True)
        loss_ref[...] += 0.5 * jnp.sum(per_row, axis=0, keepdims=True)
        dn_v = err * (1.0 / D)
        dg_ref[...] += jnp.sum(dn_v * y, axis=0, keepdims=True)
        dy = dn_v * gv
        dx = r * (dy - y * jnp.mean(dy * y, axis=-1, keepdims=True))
        dx_ref[...] = dx
        dxb_ref[...] = dx.astype(BF16)

    return pl.pallas_call(
        body, out_shape=[SDS((1, 1), F32), SDS((S, D), F32), SDS((S, D), BF16), SDS((1, D), F32)], grid=(S // tm,),
        in_specs=[_row(tm, D), _full((1, D)), _row(tm, D)],
        out_specs=[_full((1, 1)), _row(tm, D), _row(tm, D), _full((1, D))], name="loss_head", compiler_params=_cp(1),
    )(x, g, tgt)


def _softmax_rows(s):
    m = jnp.max(s, axis=-1, keepdims=True)
    e = jnp.exp(s - m)
    return e / jnp.sum(e, axis=-1, keepdims=True)


def _attn_fwd(q, k, v, name):
    S = q.shape[0]
    tm = min(S, TM_ROW)
    scale = XA_HD ** -0.5

    def body(q_ref, k_ref, v_ref, o_ref):
        for h in range(XA_HEADS):
            sl = slice(h * XA_HD, (h + 1) * XA_HD)
            s = lax.dot_general(q_ref[:, sl], k_ref[:, sl], _NT, preferred_element_type=F32) * scale
            p = _softmax_rows(s)
            o_ref[:, sl] = lax.dot_general(p.astype(BF16), v_ref[:, sl], _NN, preferred_element_type=F32).astype(BF16)

    return pl.pallas_call(body, out_shape=SDS((S, D), BF16), grid=(S // tm,),
                          in_specs=[_row(tm, D), _full((N_MEM, D)), _full((N_MEM, D))], out_specs=_row(tm, D),
                          name=name, compiler_params=_cp(1))(q, k, v)


def _attn_bwd(q, k, v, do, name):
    S = q.shape[0]
    tm = min(S, TM_ROW)
    scale = XA_HD ** -0.5

    def body(q_ref, k_ref, v_ref, do_ref, dq_ref, dk_ref, dv_ref):
        i = pl.program_id(0)

        @pl.when(i == 0)
        def _():
            dk_ref[...] = jnp.zeros_like(dk_ref)
            dv_ref[...] = jnp.zeros_like(dv_ref)

        for h in range(XA_HEADS):
            sl = slice(h * XA_HD, (h + 1) * XA_HD)
            qh, kh, vh, doh = q_ref[:, sl], k_ref[:, sl], v_ref[:, sl], do_ref[:, sl]
            s = lax.dot_general(qh, kh, _NT, preferred_element_type=F32) * scale
            p = _softmax_rows(s)
            pb = p.astype(BF16)
            dv_ref[:, sl] += lax.dot_general(pb, doh, _TN, preferred_element_type=F32)
            dp = lax.dot_general(doh, vh, _NT, preferred_element_type=F32)
            ds = (p * (dp - jnp.sum(dp * p, axis=-1, keepdims=True)) * scale).astype(BF16)
            dq_ref[:, sl] = lax.dot_general(ds, kh, _NN, preferred_element_type=F32).astype(BF16)
            dk_ref[:, sl] += lax.dot_general(ds, qh, _TN, preferred_element_type=F32)

    return pl.pallas_call(
        body, out_shape=[SDS((S, D), BF16), SDS((N_MEM, D), F32), SDS((N_MEM, D), F32)], grid=(S // tm,),
        in_specs=[_row(tm, D), _full((N_MEM, D)), _full((N_MEM, D)), _row(tm, D)],
        out_specs=[_row(tm, D), _full((N_MEM, D)), _full((N_MEM, D))], name=name, compiler_params=_cp(1),
    )(q, k, v, do)


def _sigmoid(x):
    return 1.0 / (1.0 + jnp.exp(-x))


def _ln_silu_fwd(cv, g, b):
    S = cv.shape[0]
    tm = min(S, TM_ROW)

    def body(x_ref, g_ref, b_ref, o_ref):
        xf = x_ref[...]
        mu = jnp.mean(xf, axis=-1, keepdims=True)
        xc = xf - mu
        rstd = lax.rsqrt(jnp.mean(xc * xc, axis=-1, keepdims=True) + EPS)
        ln = (xc * rstd) * g_ref[...] + b_ref[...]
        o_ref[...] = (ln * _sigmoid(ln)).astype(BF16)

    return pl.pallas_call(body, out_shape=SDS((S, D), BF16), grid=(S // tm,),
                          in_specs=[_row(tm, D), _full((1, D)), _full((1, D))], out_specs=_row(tm, D),
                          name="ln_silu_fwd", compiler_params=_cp(1))(cv, g, b)


def _ln_silu_bwd(ds, cv, g, b, dx):
    S = cv.shape[0]
    tm = min(S, TM_ROW)

    def body(ds_ref, x_ref, g_ref, b_ref, dx_ref, dcv_ref, dg_ref, db_ref, db2_ref):
        i = pl.program_id(0)

        @pl.when(i == 0)
        def _():
            dg_ref[...] = jnp.zeros_like(dg_ref)
            db_ref[...] = jnp.zeros_like(db_ref)
            db2_ref[...] = jnp.zeros_like(db2_ref)

        xf = x_ref[...]
        mu = jnp.mean(xf, axis=-1, keepdims=True)
        xc = xf - mu
        rstd = lax.rsqrt(jnp.mean(xc * xc, axis=-1, keepdims=True) + EPS)
        xhat = xc * rstd
        gv = g_ref[...]
        ln = xhat * gv + b_ref[...]
        sg = _sigmoid(ln)
        dln = ds_ref[...].astype(F32) * (sg + ln * sg * (1.0 - sg))
        dg_ref[...] += jnp.sum(dln * xhat, axis=0, keepdims=True)
        db_ref[...] += jnp.sum(dln, axis=0, keepdims=True)
        db2_ref[...] += jnp.sum(dx_ref[...], axis=0, keepdims=True)
        dxh = dln * gv
        dcv_ref[...] = rstd * (dxh - jnp.mean(dxh, axis=-1, keepdims=True)
                               - xhat * jnp.mean(dxh * xhat, axis=-1, keepdims=True))

    return pl.pallas_call(
        body, out_shape=[SDS((S, D), F32), SDS((1, D), F32), SDS((1, D), F32), SDS((1, D), F32)], grid=(S // tm,),
        in_specs=[_row(tm, D), _row(tm, D), _full((1, D)), _full((1, D)), _row(tm, D)],
        out_specs=[_row(tm, D), _full((1, D)), _full((1, D)), _full((1, D))], name="ln_silu_bwd",
        compiler_params=_cp(1),
    )(ds, cv, g, b, dx)


_GELU_C, _GELU_K = 0.7978845608028654, 0.044715


def _gelu(x, with_grad=False):
    x2 = x * x
    t = jnp.tanh(_GELU_C * (x + _GELU_K * x * x2))
    gel = 0.5 * x * (1.0 + t)
    if not with_grad:
        return gel
    return gel, 0.5 * (1.0 + t) + 0.5 * x * (1.0 - t * t) * (_GELU_C * (1.0 + 3.0 * _GELU_K * x2))


def _expm1(x):
    poly = x * (1.0 + x * (0.5 + x * (1.0 / 6.0 + x * (1.0 / 24.0 + x * (1.0 / 120.0)))))
    return jnp.where(jnp.abs(x) < 0.05, poly, jnp.exp(x) - 1.0)


def _softplus(x):
    return jnp.maximum(x, 0.0) + jnp.log1p(jnp.exp(-jnp.abs(x)))


_SCAN_UNROLL = 4
_RB = 32
_HB = 16


def _sub_blocks(n_rows, n_lanes, fn):
    def step(idx, c):
        r0 = pl.multiple_of(idx * _RB, _RB)
        for lt in range(n_lanes // LANE):
            fn(r0, lt)
        return c

    lax.fori_loop(0, n_rows // _RB, step, 0)


def _lanes(lt):
    return pl.ds(lt * LANE, LANE)


def _psum8(x):
    parts = [x[i * SUB:(i + 1) * SUB] for i in range(x.shape[0] // SUB)]
    return functools.reduce(lambda p, q: p + q, parts)


def _scan_fwd(a_s, b_s, out_ref, carry_ref, n_groups):
    row = lax.broadcasted_iota(jnp.int32, (SUB, LANE), 0)
    U = _SCAN_UNROLL

    def step(gi, carry):
        base = gi * (SUB * U)
        parts = []
        for u in range(U):
            i = pl.multiple_of(base + u * SUB, SUB)
            a8, b8 = a_s[pl.ds(i, SUB), :], b_s[pl.ds(i, SUB), :]
            for s in (1, 2, 4):
                a_sh = jnp.where(row >= s, pltpu.roll(a8, s, 0), 1.0)
                b_sh = jnp.where(row >= s, pltpu.roll(b8, s, 0), 0.0)
                b8 = a8 * b_sh + b8
                a8 = a8 * a_sh
            parts.append((i, a8, b8))
        for i, a8, b8 in parts:
            h8 = a8 * carry + b8
            out_ref[pl.ds(i, SUB), :] = h8
            carry = jnp.broadcast_to(h8[SUB - 1:SUB, :], (SUB, LANE))
        return carry

    carry_ref[...] = lax.fori_loop(0, n_groups // U, step, carry_ref[...])


def _scan_bwd(a_s, b_s, out_ref, carry_ref, n_groups):
    row = lax.broadcasted_iota(jnp.int32, (SUB, LANE), 0)
    U = _SCAN_UNROLL

    def step(gi, carry):
        base = (n_groups // U - 1 - gi) * (SUB * U)
        parts = []
        for u in reversed(range(U)):
            i = pl.multiple_of(base + u * SUB, SUB)
            a8, b8 = a_s[pl.ds(i, SUB), :], b_s[pl.ds(i, SUB), :]
            for s in (1, 2, 4):
                a_sh = jnp.where(row < SUB - s, pltpu.roll(a8, SUB - s, 0), 1.0)
                b_sh = jnp.where(row < SUB - s, pltpu.roll(b8, SUB - s, 0), 0.0)
                b8 = a8 * b_sh + b8
                a8 = a8 * a_sh
            parts.append((i, a8, b8))
        for i, a8, b8 in parts:
            h8 = a8 * carry + b8
            out_ref[pl.ds(i, SUB), :] = h8
            carry = jnp.broadcast_to(h8[0:1, :], (SUB, LANE))
        return carry

    carry_ref[...] = lax.fori_loop(0, n_groups // U, step, carry_ref[...])


def _rglru_pre(xr, wgx_ref, bgx_ref, wga_ref, bga_ref, lam_ref):
    xrb = xr.astype(BF16)
    wgx, wga = wgx_ref[0].astype(BF16), wga_ref[0].astype(BF16)
    gx = _sigmoid(lax.dot_general(xrb, wgx, _NN, preferred_element_type=F32) + bgx_ref[...])
    ga = _sigmoid(lax.dot_general(xrb, wga, _NN, preferred_element_type=F32) + bga_ref[...])
    sp = _softplus(-lam_ref[...])
    log_a = -C_RG * ga * sp
    a = jnp.exp(log_a)
    mult = jnp.sqrt(-_expm1(2.0 * log_a))
    return gx, ga, sp, a, mult, xrb, wgx, wga


def _a_specs():
    vec = pl.BlockSpec((1, HD_A), lambda c, j: (0, c))
    mat = pl.BlockSpec((1, HD_A, HD_A), lambda c, j: (c, 0, 0))
    return [pl.BlockSpec((CONV_A, HD_A), lambda c, j: (0, c)), vec, mat, vec, mat, vec, vec]


def _a_fwd(zp, conv_w, conv_b, wgx, bgx, wga, bga, lam):
    S = zp.shape[0]
    R, nt = R_RGLRU, D // HD_A
    H = SUB

    def body(zg_ref, zr_ref, cw_ref, cb_ref, wgx_ref, bgx_ref, wga_ref, bga_ref, lam_ref, ya_ref, h_ref,
             ext, a_s, b_s, hc):
        j = pl.program_id(1)

        @pl.when(j == 0)
        def _():
            ext[0:H, :] = jnp.zeros((H, HD_A), F32)
            hc[...] = jnp.zeros_like(hc)

        ext[H:H + R, :] = zr_ref[...].astype(F32)
        xr = cb_ref[...]
        for k in range(CONV_A):
            xr = xr + cw_ref[k:k + 1, :] * ext[pl.ds(H - (CONV_A - 1 - k), R), :]
        gx, _, _, a, mult, _, _, _ = _rglru_pre(xr, wgx_ref, bgx_ref, wga_ref, bga_ref, lam_ref)
        a_s[...] = a
        b_s[...] = mult * (gx * xr)
        _scan_fwd(a_s, b_s, h_ref, hc, R // SUB)
        ya_ref[...] = (_gelu(zg_ref[...].astype(F32)) * h_ref[...]).astype(BF16)
        ext[0:H, :] = ext[R:R + H, :]

    return pl.pallas_call(
        body, out_shape=[SDS((S, D + D // 2), BF16), SDS((S, D), F32)], grid=(nt, S // R),
        in_specs=[pl.BlockSpec((R, HD_A), lambda c, j: (j, c)), pl.BlockSpec((R, HD_A), lambda c, j: (j, nt + c))]
        + _a_specs(),
        out_specs=[pl.BlockSpec((R, HD_A), lambda c, j: (j, c)), pl.BlockSpec((R, HD_A), lambda c, j: (j, c))],
        scratch_shapes=[pltpu.VMEM((H + R, HD_A), F32), pltpu.VMEM((R, HD_A), F32), pltpu.VMEM((R, HD_A), F32),
                        pltpu.VMEM((SUB, HD_A), F32)],
        name="rglru_fwd", compiler_params=_cp(2),
    )(zp, zp, conv_w, conv_b, wgx, bgx, wga, bga, lam)


def _a_bwd(dyab, zp, h, conv_w, conv_b, wgx, bgx, wga, bga, lam):
    S = zp.shape[0]
    R, nt, nch = R_RGLRU, D // HD_A, S // R_RGLRU
    H = SUB

    def rows(c, j):
        return (nch - 1 - j, c)

    def rows_rec(c, j):
        return (nch - 1 - j, nt + c)

    def halo(c, j):
        return (jnp.maximum((nch - 1 - j) * (R // H) - 1, 0), c)

    def halo_z(c, j):
        return (jnp.maximum((nch - 1 - j) * (R // _HB) - 1, 0), nt + c)

    def body(dy_ref, zg_ref, zr_ref, zh_ref, h_ref, hh_ref, cw_ref, cb_ref, wgx_ref, bgx_ref, wga_ref, bga_ref,
             lam_ref, dzg_ref, dzr_ref, dcw_ref, dcb_ref, dwgx_ref, dbgx_ref, dwga_ref, dbga_ref, dlam_ref,
             ext_z, ext_h, ext_mu, ext_d, a_s, b_s, muc):
        j = pl.program_id(1)
        first_chunk = (nch - 1 - j) == 0

        @pl.when(j == 0)
        def _():
            ext_mu[R:R + H, :] = jnp.zeros((H, HD_A), F32)
            ext_d[R:R + H, :] = jnp.zeros((H, HD_A), F32)
            muc[...] = jnp.zeros_like(muc)
            for r in (dcw_ref, dcb_ref, dwgx_ref, dbgx_ref, dwga_ref, dbga_ref, dlam_ref):
                r[...] = jnp.zeros_like(r)

        zg = zg_ref[...].astype(F32)
        ext_z[0:H, :] = jnp.where(first_chunk, 0.0, zh_ref[_HB - H:_HB, :].astype(F32))
        ext_z[H:H + R, :] = zr_ref[...].astype(F32)
        ext_h[0:H, :] = jnp.where(first_chunk, 0.0, hh_ref[...])
        ext_h[H:H + R, :] = h_ref[...]
        xr = cb_ref[...]
        for k in range(CONV_A):
            xr = xr + cw_ref[k:k + 1, :] * ext_z[pl.ds(H - (CONV_A - 1 - k), R), :]
        gx, ga, sp, a, mult, xrb, wgxb, wgab = _rglru_pre(xr, wgx_ref, bgx_ref, wga_ref, bga_ref, lam_ref)
        gel, dgel = _gelu(zg, with_grad=True)
        dy = dy_ref[...].astype(F32)
        dh = dy * gel
        dzg_ref[...] = (dy * h_ref[...] * dgel).astype(BF16)
        a_s[...] = a
        b_s[...] = a * dh
        _scan_bwd(a_s, b_s, ext_mu, muc, R // SUB)
        lam_t = dh + ext_mu[pl.ds(1, R), :]
        ext_mu[R:R + H, :] = ext_mu[0:H, :]
        da = lam_t * ext_h[pl.ds(H - 1, R), :]
        gxr = gx * xr
        dlog_a = da * a - (lam_t * gxr) * (a * a) / mult
        dgx = lam_t * mult * xr
        dxr = lam_t * mult * gx
        lam_v = lam_ref[...]
        dlam_ref[...] += jnp.sum(dlog_a * ga, axis=0, keepdims=True) * (C_RG * _sigmoid(-lam_v))
        dpa = (dlog_a * (-C_RG * sp)) * ga * (1.0 - ga)
        dpx = dgx * gx * (1.0 - gx)
        dbga_ref[...] += jnp.sum(dpa, axis=0, keepdims=True)
        dbgx_ref[...] += jnp.sum(dpx, axis=0, keepdims=True)
        dpab, dpxb = dpa.astype(BF16), dpx.astype(BF16)
        dwga_ref[0] += lax.dot_general(xrb, dpab, _TN, preferred_element_type=F32)
        dwgx_ref[0] += lax.dot_general(xrb, dpxb, _TN, preferred_element_type=F32)
        dxr = (dxr + lax.dot_general(dpab, wgab, _NT, preferred_element_type=F32)
               + lax.dot_general(dpxb, wgxb, _NT, preferred_element_type=F32))
        dcb_ref[...] += jnp.sum(dxr, axis=0, keepdims=True)
        ext_d[0:R, :] = dxr
        dzr = jnp.zeros((R, HD_A), F32)
        for k in range(CONV_A):
            sh = CONV_A - 1 - k
            dcw_ref[k:k + 1, :] += jnp.sum(dxr * ext_z[pl.ds(H - sh, R), :], axis=0, keepdims=True)
            dzr = dzr + cw_ref[k:k + 1, :] * ext_d[pl.ds(sh, R), :]
        dzr_ref[...] = dzr.astype(BF16)
        ext_d[R:R + H, :] = ext_d[0:H, :]

    vec_o = pl.BlockSpec((1, HD_A), lambda c, j: (0, c))
    mat_o = pl.BlockSpec((1, HD_A, HD_A), lambda c, j: (c, 0, 0))
    return pl.pallas_call(
        body,
        out_shape=[SDS((S, D), BF16), SDS((S, D), BF16), SDS((CONV_A, D), F32), SDS((1, D), F32),
                   SDS((nt, HD_A, HD_A), F32), SDS((1, D), F32), SDS((nt, HD_A, HD_A), F32), SDS((1, D), F32),
                   SDS((1, D), F32)],
        grid=(nt, nch),
        in_specs=[pl.BlockSpec((R, HD_A), rows), pl.BlockSpec((R, HD_A), rows), pl.BlockSpec((R, HD_A), rows_rec),
                  pl.BlockSpec((_HB, HD_A), halo_z), pl.BlockSpec((R, HD_A), rows),
                  pl.BlockSpec((H, HD_A), halo)] + _a_specs(),
        out_specs=[pl.BlockSpec((R, HD_A), rows), pl.BlockSpec((R, HD_A), rows),
                   pl.BlockSpec((CONV_A, HD_A), lambda c, j: (0, c)), vec_o, mat_o, vec_o, mat_o, vec_o, vec_o],
        scratch_shapes=[pltpu.VMEM((H + R, HD_A), F32), pltpu.VMEM((H + R, HD_A), F32), pltpu.VMEM((R + H, HD_A), F32),
                        pltpu.VMEM((R + H, HD_A), F32), pltpu.VMEM((R, HD_A), F32), pltpu.VMEM((R, HD_A), F32),
                        pltpu.VMEM((SUB, HD_A), F32)],
        name="rglru_bwd", compiler_params=_cp(2),
    )(dyab, zp, zp, zp, h, h, conv_w, conv_b, wgx, bgx, wga, bga, lam)


_POOL_H = 16
_POOL_T0 = 2 * D // HD_A
_POOL_Y0 = D // HD_A


def _window_sum(lv, n, lo, rows, g, ahead):
    base = 0 if ahead else SUB
    cur, win = lv[0], None
    for i, s in enumerate((1, 2, 4, 8)):
        val = cur[pl.ds(base, n), :] + cur[pl.ds(base + (s if ahead else -s), n), :]
        sel = val[lo:lo + rows]
        win = sel if win is None else jnp.where(g >= i, sel, win)
        if i < 3:
            lv[i + 1][pl.ds(base, n), :] = val
            cur = lv[i + 1]
    return win


def _pool_width(g):
    return jnp.where(g == 0, 2.0, jnp.where(g == 1, 4.0, jnp.where(g == 2, 8.0, 16.0)))


def _b_fwd(zp, yab, wg, bg, sc):
    S = zp.shape[0]
    R, H = R_SEQ, _POOL_H

    def body(z_ref, wg_ref, bg_ref, sc_ref, yab_in, yb_ref, *lv):
        del yab_in
        g, j = pl.program_id(0), pl.program_id(1)

        @pl.when(j == 0)
        def _():
            for r in lv:
                r[0:SUB, :] = jnp.zeros((SUB, HD_A), F32)
            lv[0][SUB:SUB + H, :] = jnp.zeros((H, HD_A), F32)

        u = z_ref[...].astype(F32)
        lv[0][SUB + H:SUB + H + R, :] = u
        t1 = (j * R + 1 + lax.broadcasted_iota(jnp.int32, (R, HD_A), 0)).astype(F32)
        p = _window_sum(lv, H + R, H, R, g, False) / jnp.minimum(t1, _pool_width(g)) - u
        lin = lax.dot_general(p.astype(BF16), wg_ref[0].astype(BF16), _NN, preferred_element_type=F32) + bg_ref[...]
        yb_ref[...] = (lin * sc_ref[...]).astype(BF16)
        lv[0][SUB:SUB + H, :] = lv[0][SUB + R:SUB + R + H, :]

    vec = pl.BlockSpec((1, HD_A), lambda g, j: (0, g))
    return pl.pallas_call(
        body, out_shape=SDS(yab.shape, yab.dtype), grid=(len(POOL_WINDOWS), S // R),
        in_specs=[pl.BlockSpec((R, HD_A), lambda g, j: (j, _POOL_T0 + g)),
                  pl.BlockSpec((1, HD_A, HD_A), lambda g, j: (g, 0, 0)), vec, vec, pl.BlockSpec(memory_space=pl.ANY)],
        out_specs=pl.BlockSpec((R, HD_A), lambda g, j: (j, _POOL_Y0 + g)),
        scratch_shapes=[pltpu.VMEM((SUB + H + R, HD_A), F32)] * 4, input_output_aliases={4: 0},
        name="pool_fwd", compiler_params=_cp(2),
    )(zp, wg, bg, sc, yab)


def _b_bwd(dyab, zp, wg, bg, sc):
    S = zp.shape[0]
    R, H, nch, ng = R_SEQ, _POOL_H, S // R_SEQ, len(POOL_WINDOWS)

    def body(dy_ref, z_ref, zh_ref, wg_ref, bg_ref, sc_ref, dz_ref, dwg_ref, dbg_ref, dsc_ref, *scratch):
        lu, lq = scratch[:4], scratch[4:]
        g, j = pl.program_id(0), pl.program_id(1)
        jj = nch - 1 - j

        @pl.when(j == 0)
        def _():
            for r in lu:
                r[0:SUB, :] = jnp.zeros((SUB, HD_A), F32)
            for r in lq:
                r[R + H:R + H + SUB, :] = jnp.zeros((SUB, HD_A), F32)
            lq[0][R:R + H, :] = jnp.zeros((H, HD_A), F32)
            for r in (dwg_ref, dbg_ref, dsc_ref):
                r[...] = jnp.zeros_like(r)

        u = z_ref[...].astype(F32)
        lu[0][SUB:SUB + H, :] = jnp.where(jj == 0, 0.0, zh_ref[...].astype(F32))
        lu[0][SUB + H:SUB + H + R, :] = u
        t1 = (jj * R + 1 + lax.broadcasted_iota(jnp.int32, (R, HD_A), 0)).astype(F32)
        cnt = jnp.minimum(t1, _pool_width(g))
        pb = (_window_sum(lu, H + R, H, R, g, False) / cnt - u).astype(BF16)
        wgb = wg_ref[0].astype(BF16)
        lin = lax.dot_general(pb, wgb, _NN, preferred_element_type=F32) + bg_ref[...]
        dy = dy_ref[...].astype(F32)
        dsc_ref[...] += jnp.sum(dy * lin, axis=0, keepdims=True)
        dlin = dy * sc_ref[...]
        dbg_ref[...] += jnp.sum(dlin, axis=0, keepdims=True)
        dlb = dlin.astype(BF16)
        dwg_ref[0] += lax.dot_general(pb, dlb, _TN, preferred_element_type=F32)
        dp = lax.dot_general(dlb, wgb, _NT, preferred_element_type=F32)
        lq[0][0:R, :] = dp / cnt
        dz_ref[...] = (_window_sum(lq, R + H, 0, R, g, True) - dp).astype(BF16)
        lq[0][R:R + H, :] = lq[0][0:H, :]

    vec = pl.BlockSpec((1, HD_A), lambda g, j: (0, g))
    mat = pl.BlockSpec((1, HD_A, HD_A), lambda g, j: (g, 0, 0))
    return pl.pallas_call(
        body, out_shape=[SDS((S, D // 2), BF16), SDS((ng, HD_A, HD_A), F32), SDS((1, D // 2), F32),
                         SDS((1, D // 2), F32)],
        grid=(ng, nch),
        in_specs=[pl.BlockSpec((R, HD_A), lambda g, j: (nch - 1 - j, _POOL_Y0 + g)),
                  pl.BlockSpec((R, HD_A), lambda g, j: (nch - 1 - j, _POOL_T0 + g)),
                  pl.BlockSpec((H, HD_A), lambda g, j: (jnp.maximum((nch - 1 - j) * (R // H) - 1, 0), _POOL_T0 + g)),
                  mat, vec, vec],
        out_specs=[pl.BlockSpec((R, HD_A), lambda g, j: (nch - 1 - j, g)), mat, vec, vec],
        scratch_shapes=[pltpu.VMEM((SUB + H + R, HD_A), F32)] * 8,
        name="pool_bwd", compiler_params=_cp(2),
    )(dyab, zp, zp, wg, bg, sc)


_CW_F = 768


def _f_fwd(hp, w, b, name):
    S = hp.shape[0]
    R, H, cw = R_SEQ, SUB, _CW_F
    nlt = cw // LANE

    def body(h_ref, w_ref, b_ref, o_ref, gel_ref, ud_ref, ext):
        j = pl.program_id(1)

        @pl.when(j == 0)
        def _():
            ext[:, 0:H, :] = jnp.zeros((nlt, H, LANE), F32)

        def stage(r0, lt):
            ext[lt, pl.ds(pl.multiple_of(r0 + H, SUB), _RB), :] = h_ref[pl.ds(r0, _RB), _lanes(lt)].astype(F32)

        def main(r0, lt):
            ls = _lanes(lt)
            gp = b_ref[:, ls]
            for k in range(CONV_F):
                gp = gp + w_ref[k:k + 1, ls] * ext[lt, pl.ds(r0 + (H - (CONV_F - 1 - k)), _RB), :]
            up = h_ref[pl.ds(r0, _RB), _lanes(lt + nlt)].astype(F32)
            gel, dgel = _gelu(gp, with_grad=True)
            rs = pl.ds(r0, _RB)
            o_ref[rs, ls] = (gel * up).astype(BF16)
            gel_ref[rs, ls] = gel.astype(BF16)
            ud_ref[rs, ls] = (up * dgel).astype(BF16)

        _sub_blocks(R, cw, stage)
        _sub_blocks(R, cw, main)
        ext[:, 0:H, :] = ext[:, R:R + H, :]

    tile = pl.BlockSpec((R, cw), lambda c, j: (j, c))
    return pl.pallas_call(
        body, out_shape=[SDS((S, D_FF), BF16)] * 3, grid=(D_FF // cw, S // R),
        in_specs=[pl.BlockSpec((R, 2 * cw), lambda c, j: (j, c)), pl.BlockSpec((CONV_F, cw), lambda c, j: (0, c)),
                  pl.BlockSpec((1, cw), lambda c, j: (0, c))],
        out_specs=[tile] * 3,
        scratch_shapes=[pltpu.VMEM((nlt, H + R, LANE), F32)], name=name, compiler_params=_cp(2),
    )(hp, w, b)


def _f_bwd(dact, hp, gel, ud, w, name):
    S = hp.shape[0]
    R, H, cw, nch = R_SEQ, SUB, _CW_F, S // R_SEQ
    nlt = cw // LANE

    def body(da_ref, h_ref, hh_ref, gel_ref, ud_ref, w_ref, dh_ref, dw_ref, db_ref, ext_g, ext_d, acc):
        j = pl.program_id(1)
        jj = nch - 1 - j

        @pl.when(j == 0)
        def _():
            ext_d[:, R:R + H, :] = jnp.zeros((nlt, H, LANE), F32)
            acc[...] = jnp.zeros_like(acc)

        for lt in range(nlt):
            ext_g[lt, 0:H, :] = jnp.where(jj == 0, 0.0, hh_ref[_HB - H:_HB, lt * LANE:(lt + 1) * LANE].astype(F32))

        def stage(r0, lt):
            ext_g[lt, pl.ds(pl.multiple_of(r0 + H, SUB), _RB), :] = h_ref[pl.ds(r0, _RB), _lanes(lt)].astype(F32)

        def first(r0, lt):
            ls, lu, rs = _lanes(lt), _lanes(lt + nlt), pl.ds(r0, _RB)
            da = da_ref[rs, ls].astype(F32)
            dh_ref[rs, lu] = (da * gel_ref[rs, ls].astype(F32)).astype(BF16)
            dgp = da * ud_ref[rs, ls].astype(F32)
            ext_d[lt, rs, :] = dgp
            acc[CONV_F * SUB:(CONV_F + 1) * SUB, ls] += _psum8(dgp)
            for k in range(CONV_F):
                tap = ext_g[lt, pl.ds(r0 + (H - (CONV_F - 1 - k)), _RB), :]
                acc[k * SUB:(k + 1) * SUB, ls] += _psum8(dgp * tap)

        def second(r0, lt):
            ls = _lanes(lt)
            dhg = w_ref[CONV_F - 1:CONV_F, ls] * ext_d[lt, pl.ds(r0, _RB), :]
            for k in range(CONV_F - 1):
                dhg = dhg + w_ref[k:k + 1, ls] * ext_d[lt, pl.ds(r0 + (CONV_F - 1 - k), _RB), :]
            dh_ref[pl.ds(r0, _RB), ls] = dhg.astype(BF16)

        _sub_blocks(R, cw, stage)
        _sub_blocks(R, cw, first)
        _sub_blocks(R, cw, second)
        ext_d[:, R:R + H, :] = ext_d[:, 0:H, :]

        @pl.when(j == nch - 1)
        def _():
            for k in range(CONV_F):
                dw_ref[k:k + 1, :] = jnp.sum(acc[k * SUB:(k + 1) * SUB, :], axis=0, keepdims=True)
            db_ref[...] = jnp.sum(acc[CONV_F * SUB:(CONV_F + 1) * SUB, :], axis=0, keepdims=True)

    rows = lambda c, j: (nch - 1 - j, c)
    return pl.pallas_call(
        body, out_shape=[SDS((S, 2 * D_FF), BF16), SDS((CONV_F, D_FF), F32), SDS((1, D_FF), F32)],
        grid=(D_FF // cw, nch),
        in_specs=[pl.BlockSpec((R, cw), rows), pl.BlockSpec((R, cw), lambda c, j: (nch - 1 - j, 2 * c)),
                  pl.BlockSpec((_HB, cw), lambda c, j: (jnp.maximum((nch - 1 - j) * (R // _HB) - 1, 0), 2 * c)),
                  pl.BlockSpec((R, cw), rows), pl.BlockSpec((R, cw), rows),
                  pl.BlockSpec((CONV_F, cw), lambda c, j: (0, c))],
        out_specs=[pl.BlockSpec((R, 2 * cw), rows), pl.BlockSpec((CONV_F, cw), lambda c, j: (0, c)),
                   pl.BlockSpec((1, cw), lambda c, j: (0, c))],
        scratch_shapes=[pltpu.VMEM((nlt, H + R, LANE), F32), pltpu.VMEM((nlt, R + H, LANE), F32),
                        pltpu.VMEM(((CONV_F + 1) * SUB, cw), F32)], name=name,
        compiler_params=_cp(2),
    )(dact, hp, hp, gel, ud, w)


_CW_C = 256
_H_C = 32


def _c_fwd(h1p, w, b):
    S = h1p.shape[0]
    R, H, cw = R_SEQ, _H_C, _CW_C
    nlt = cw // LANE

    def body(h_ref, w_ref, b_ref, o_ref, ext):
        j = pl.program_id(1)

        @pl.when(j == 0)
        def _():
            ext[:, 0:H, :] = jnp.zeros((nlt, H, LANE), F32)

        def stage(r0, lt):
            rs = pl.ds(r0, _RB)
            gate = h_ref[rs, _lanes(lt + nlt)].astype(F32)
            ext[lt, pl.ds(pl.multiple_of(r0 + H, SUB), _RB), :] = h_ref[rs, _lanes(lt)].astype(F32) * _sigmoid(gate)

        def main(r0, lt):
            ls = _lanes(lt)
            cv = b_ref[:, ls]
            for k in range(CONV_C):
                cv = cv + w_ref[k:k + 1, ls] * ext[lt, pl.ds(r0 + (H - (CONV_C - 1 - k)), _RB), :]
            o_ref[pl.ds(r0, _RB), ls] = cv

        _sub_blocks(R, cw, stage)
        _sub_blocks(R, cw, main)
        ext[:, 0:H, :] = ext[:, R:R + H, :]

    return pl.pallas_call(
        body, out_shape=SDS((S, D), F32), grid=(D // cw, S // R),
        in_specs=[pl.BlockSpec((R, 2 * cw), lambda c, j: (j, c)), pl.BlockSpec((CONV_C, cw), lambda c, j: (0, c)),
                  pl.BlockSpec((1, cw), lambda c, j: (0, c))],
        out_specs=pl.BlockSpec((R, cw), lambda c, j: (j, c)),
        scratch_shapes=[pltpu.VMEM((nlt, H + R, LANE), F32)], name="conf_conv_fwd", compiler_params=_cp(2),
    )(h1p, w, b)


def _c_bwd(dcv, h1p, w):
    S = h1p.shape[0]
    R, H, cw, nch = R_SEQ, _H_C, _CW_C, S // R_SEQ
    nlt = cw // LANE
    a_b, a_val, a_gate = CONV_C * SUB, (CONV_C + 1) * SUB, (CONV_C + 2) * SUB

    def body(dc_ref, h_ref, hh_ref, w_ref, dh_ref, dw_ref, db_ref, db1_ref, ext_u, ext_d, acc):
        j = pl.program_id(1)
        jj = nch - 1 - j

        @pl.when(j == 0)
        def _():
            ext_d[:, R:R + H, :] = jnp.zeros((nlt, H, LANE), F32)
            acc[...] = jnp.zeros_like(acc)

        for lt in range(nlt):
            ext_u[lt, 0:H, :] = jnp.where(
                jj == 0, 0.0, hh_ref[:, lt * LANE:(lt + 1) * LANE].astype(F32)
                * _sigmoid(hh_ref[:, cw + lt * LANE:cw + (lt + 1) * LANE].astype(F32)))

        def stage(r0, lt):
            rs, ls = pl.ds(r0, _RB), _lanes(lt)
            gate = h_ref[rs, _lanes(lt + nlt)].astype(F32)
            ext_u[lt, pl.ds(pl.multiple_of(r0 + H, SUB), _RB), :] = h_ref[rs, ls].astype(F32) * _sigmoid(gate)
            ext_d[lt, rs, :] = dc_ref[rs, ls]

        def first(r0, lt):
            ls = _lanes(lt)
            dc = dc_ref[pl.ds(r0, _RB), ls]
            acc[a_b:a_b + SUB, ls] += _psum8(dc)
            for k in range(CONV_C):
                tap = ext_u[lt, pl.ds(r0 + (H - (CONV_C - 1 - k)), _RB), :]
                acc[k * SUB:(k + 1) * SUB, ls] += _psum8(dc * tap)

        def second(r0, lt):
            rs, ls, lg = pl.ds(r0, _RB), _lanes(lt), _lanes(lt + nlt)
            du = w_ref[CONV_C - 1:CONV_C, ls] * ext_d[lt, rs, :]
            for k in range(CONV_C - 1):
                du = du + w_ref[k:k + 1, ls] * ext_d[lt, pl.ds(r0 + (CONV_C - 1 - k), _RB), :]
            val = h_ref[rs, ls].astype(F32)
            sg = _sigmoid(h_ref[rs, lg].astype(F32))
            dval = du * sg
            dgate = du * val * sg * (1.0 - sg)
            acc[a_val:a_val + SUB, ls] += _psum8(dval)
            acc[a_gate:a_gate + SUB, ls] += _psum8(dgate)
            dh_ref[rs, ls] = dval.astype(BF16)
            dh_ref[rs, lg] = dgate.astype(BF16)

        _sub_blocks(R, cw, stage)
        _sub_blocks(R, cw, first)
        _sub_blocks(R, cw, second)
        ext_d[:, R:R + H, :] = ext_d[:, 0:H, :]

        @pl.when(j == nch - 1)
        def _():
            for k in range(CONV_C):
                dw_ref[k:k + 1, :] = jnp.sum(acc[k * SUB:(k + 1) * SUB, :], axis=0, keepdims=True)
            db_ref[...] = jnp.sum(acc[a_b:a_b + SUB, :], axis=0, keepdims=True)
            db1_ref[:, 0:cw] = jnp.sum(acc[a_val:a_val + SUB, :], axis=0, keepdims=True)
            db1_ref[:, cw:2 * cw] = jnp.sum(acc[a_gate:a_gate + SUB, :], axis=0, keepdims=True)

    rows = lambda c, j: (nch - 1 - j, c)
    return pl.pallas_call(
        body, out_shape=[SDS((S, 2 * D), BF16), SDS((CONV_C, D), F32), SDS((1, D), F32), SDS((1, 2 * D), F32)],
        grid=(D // cw, nch),
        in_specs=[pl.BlockSpec((R, cw), rows), pl.BlockSpec((R, 2 * cw), rows),
                  pl.BlockSpec((H, 2 * cw), lambda c, j: (jnp.maximum((nch - 1 - j) * (R // H) - 1, 0), c)),
                  pl.BlockSpec((CONV_C, cw), lambda c, j: (0, c))],
        out_specs=[pl.BlockSpec((R, 2 * cw), rows), pl.BlockSpec((CONV_C, cw), lambda c, j: (0, c)),
                   pl.BlockSpec((1, cw), lambda c, j: (0, c)), pl.BlockSpec((1, 2 * cw), lambda c, j: (0, c))],
        scratch_shapes=[pltpu.VMEM((nlt, H + R, LANE), F32), pltpu.VMEM((nlt, R + H, LANE), F32),
                        pltpu.VMEM(((CONV_C + 3) * SUB, cw), F32)], name="conf_conv_bwd",
        compiler_params=_cp(2),
    )(dcv, h1p, h1p, w)


def _local_step(x, mem, tgt, W, fetch=None, send=None):
    G = {}
    W = dict(W)

    def arrive(group, after):
        if fetch is None:
            return None
        got, tok = fetch(group, after)
        for key, val in got.items():
            W[key] = {**W.get(key, {}), **val} if isinstance(val, dict) else val
        return tok

    def gain(g, tok):
        return g if tok is None else g + tok

    def sent(group):
        return None if send is None else send(group, G)

    def xattn_fwd(xin, n, l):
        tok = arrive(("xa", l), n)
        mn = _rms_fwd(mem, gain(W["xa_mem_norm"][l:l + 1], tok), f"xa_memnorm_fwd{l}")
        q = _mm_nn(n, W["xa_wq"][l], out_dtype=BF16, name=f"xa_q{l}")
        k = _mm_nn(mn, W["xa_wk"][l], out_dtype=BF16, name=f"xa_k{l}")
        v = _mm_nn(mn, W["xa_wv"][l], out_dtype=BF16, name=f"xa_v{l}")
        o = _attn_fwd(q, k, v, f"xa_attn_fwd{l}")
        xout, nout = _mm_nn(o, W["xa_wo"][l], out_dtype=F32, name=f"xa_o{l}", add=xin, norm=W["f_norm"][l:l + 1])
        return xout, nout, (xin, n, q, mn, k, v, o)

    def xattn_bwd(dx, dxb, saved, l):
        xin, n, q, mn, k, v, o = saved
        do = _mm_nt(dxb, W["xa_wo"][l], out_dtype=BF16, name=f"xa_do{l}")
        G[f"xa_wo{l}"] = _mm_tn(o, dxb, out_dtype=BF16, name=f"xa_dwo{l}")
        dq, dk, dv = _attn_bwd(q, k, v, do, f"xa_attn_bwd{l}")
        dkb, dvb = dk.astype(BF16), dv.astype(BF16)
        G[f"xa_wq{l}"] = _mm_tn(n, dq, out_dtype=BF16, name=f"xa_dwq{l}")
        G[f"xa_wk{l}"] = _mm_tn(mn, dkb, out_dtype=BF16, name=f"xa_dwk{l}")
        G[f"xa_wv{l}"] = _mm_tn(mn, dvb, out_dtype=BF16, name=f"xa_dwv{l}")
        tok = sent(("xa", l))
        dmn = _mm_nt(dkb, W["xa_wk"][l], out_dtype=F32, name=f"xa_dmn_k{l}")
        dmn = _mm_nt(dvb, W["xa_wv"][l], out_dtype=F32, name=f"xa_dmn_v{l}", add=dmn)
        (G[f"xa_mem_norm{l}"],) = _rms_bwd(mem, W["xa_mem_norm"][l:l + 1], dmn, None, f"xa_memnorm_bwd{l}")
        dx, dxb, G[f"xa_norm{l}"] = _mm_nt(dq, W["xa_wq"][l], out_dtype=F32, name=f"xa_dn{l}",
                                           rms=(xin, gain(W["xa_norm"][l:l + 1], tok), dx))
        return dx, dxb

    def ffn_fwd(xin, n, l, next_gain):
        tok = arrive(("f", l), n)
        hp = _mm_nn(n, W["f_w_up"][l], out_dtype=BF16, name=f"f_up{l}")
        act, gel, ud = _f_fwd(hp, W["f_dw_w"][l], gain(W["f_dw_b"][l:l + 1], tok), f"f_conv_fwd{l}")
        res = _mm_nn(act, W["f_w_down"][l], out_dtype=F32, name=f"f_down{l}", add=xin, norm=next_gain)
        xout, nout = res if next_gain is not None else (res, None)
        return xout, nout, (xin, n, hp, act, gel, ud)

    def ffn_bwd(dx, dxb, saved, l):
        xin, n, hp, act, gel, ud = saved
        dact = _mm_nt(dxb, W["f_w_down"][l], out_dtype=BF16, name=f"f_dact{l}")
        G[f"f_w_down{l}"] = _mm_tn(act, dxb, out_dtype=BF16, name=f"f_dwdown{l}")
        dhp, G[f"f_dw_w{l}"], G[f"f_dw_b{l}"] = _f_bwd(dact, hp, gel, ud, W["f_dw_w"][l], f"f_conv_bwd{l}")
        G[f"f_w_up{l}"] = _mm_tn(n, dhp, out_dtype=BF16, name=f"f_dwup{l}", blocks=_CW_F)
        tok = sent(("f", l))
        dx, dxb, G[f"f_norm{l}"] = _mm_nt(dhp, W["f_w_up"][l], out_dtype=F32, name=f"f_dn{l}",
                                          rms=(xin, gain(W["f_norm"][l:l + 1], tok), dx))
        return dx, dxb

    n0 = _rms_fwd(x, W["ab_norm"], "ab_norm_fwd")
    tok = arrive(("ab", 0), n0)
    a_par = (W["a_conv_w"], gain(W["a_conv_b"], tok), W["a_gate_x_w"], W["a_gate_x_b"], W["a_gate_a_w"],
             W["a_gate_a_b"], W["a_lambda"])
    b_par = (W["b_group_w"], W["b_group_b"], W["b_scale"])
    zp = _mm_nn(n0, W["ab_w_in"], out_dtype=BF16, name="ab_in")
    yab, h_a = _a_fwd(zp, *a_par)
    yab = _b_fwd(zp, yab, *b_par)
    arrive(("ab", 1), yab)
    x1, n1 = _mm_nn(yab, W["ab_w_out"], out_dtype=F32, name="ab_out", add=x, norm=W["xa_norm"][0:1])
    x2, n2, s_xa0 = xattn_fwd(x1, n1, 0)
    x3, n3, s_f0 = ffn_fwd(x2, n2, 0, W["c_norm"])
    tok = arrive(("c", 0), n3)
    h1p = _mm_nn(n3, W["c_w_pw1"], out_dtype=BF16, name="c_pw1", bias=gain(W["c_b_pw1"], tok))
    cv = _c_fwd(h1p, W["c_dw_w"], W["c_dw_b"])
    sc = _ln_silu_fwd(cv, W["c_ln_g"], W["c_ln_b"])
    x4, n4 = _mm_nn(sc, W["c_w_pw2"], out_dtype=F32, name="c_pw2", bias=W["c_b_pw2"], add=x3, norm=W["xa_norm"][1:2])
    x5, n5, s_xa1 = xattn_fwd(x4, n4, 1)
    x6, _, s_f1 = ffn_fwd(x5, n5, 1, None)
    loss, dx, dxb, G["final_norm"] = _loss_head(x6, W["final_norm"], tgt)

    dx, dxb = ffn_bwd(dx, dxb, s_f1, 1)
    dx, dxb = xattn_bwd(dx, dxb, s_xa1, 1)
    dsc = _mm_nt(dxb, W["c_w_pw2"], out_dtype=BF16, name="c_dsc")
    G["c_w_pw2"] = _mm_tn(sc, dxb, out_dtype=BF16, name="c_dwpw2")
    dcv, G["c_ln_g"], G["c_ln_b"], G["c_b_pw2"] = _ln_silu_bwd(dsc, cv, W["c_ln_g"], W["c_ln_b"], dx)
    dh1p, G["c_dw_w"], G["c_dw_b"], G["c_b_pw1"] = _c_bwd(dcv, h1p, W["c_dw_w"])
    G["c_w_pw1"] = _mm_tn(n3, dh1p, out_dtype=BF16, name="c_dwpw1", blocks=_CW_C)
    tok = sent(("c", 0))
    dx, dxb, G["c_norm"] = _mm_nt(dh1p, W["c_w_pw1"], out_dtype=F32, name="c_dn",
                                  rms=(x3, gain(W["c_norm"], tok), dx))
    dx, dxb = ffn_bwd(dx, dxb, s_f0, 0)
    dx, dxb = xattn_bwd(dx, dxb, s_xa0, 0)
    dyab = _mm_nt(dxb, W["ab_w_out"], out_dtype=BF16, name="ab_dyab")
    G["ab_w_out"] = _mm_tn(yab, dxb, out_dtype=BF16, name="ab_dwout")
    tok = sent(("ab", 1))
    a_par = (a_par[0], gain(a_par[1], tok)) + a_par[2:]
    (dzg, dzr, G["a_conv_w"], G["a_conv_b"], G["a_gate_x_w"], G["a_gate_x_b"], G["a_gate_a_w"], G["a_gate_a_b"],
     G["a_lambda"]) = _a_bwd(dyab, zp, h_a, *a_par)
    dzq, G["b_group_w"], G["b_group_b"], G["b_scale"] = _b_bwd(dyab, zp, *b_par)
    G["ab_w_in"] = jnp.concatenate(
        [_mm_tn(n0, dz, out_dtype=BF16, name=f"ab_dwin_{part}")
         for part, dz in (("gate", dzg), ("rec", dzr), ("pool", dzq))], axis=1)
    tok = sent(("ab", 0))
    dx, _, G["ab_norm"] = _mm_nt_cols([dzg, dzr, dzq], W["ab_w_in"], name="ab_dn",
                                      rms=(x, gain(W["ab_norm"], tok), dx))
    return loss, dx, G


def _my_place():
    x, y, c = lax.axis_index("x"), lax.axis_index("y"), lax.axis_index("c")
    return x, y, c


def _all_gather(shards, name):
    n = len(shards)

    def body(*refs):
        ins, outs = refs[:n], refs[n:2 * n]
        send_sems, recv_sems, local_sems = refs[2 * n:]
        x, y, c = _my_place()
        me, sibling = (x, y, c), (x, y, 1 - c)
        chips = [(1 - x, y), (x, 1 - y), (1 - x, 1 - y)]

        def slab(a, place):
            px, py, pc = place
            return outs[a].at[4 * px + 2 * py + pc]

        def copy(a, k, block, to, src=None):
            return pltpu.make_async_remote_copy(
                src_ref=slab(a, block) if src is None else src, dst_ref=slab(a, block),
                send_sem=send_sems.at[a, k], recv_sem=recv_sems.at[a, k], device_id=to, device_id_type=MESH)

        mine = [pltpu.make_async_copy(ins[a], slab(a, me), local_sems.at[a]) for a in range(n)]
        for cp in mine:
            cp.start()
        first = []
        for j, chip in enumerate(chips):
            first += [copy(a, 1 + j, me, (*chip, c), src=ins[a]) for a in range(n)]
        first += [copy(a, 0, me, sibling, src=ins[a]) for a in range(n)]
        for cp in first:
            cp.start()
        passed = []
        for j, chip in enumerate(chips):
            for a in range(n):
                copy(a, 1 + j, (*chip, c), me).wait_recv()
                cp = copy(a, 4 + j, (*chip, c), sibling)
                cp.start()
                passed.append(cp)
        for a in range(n):
            copy(a, 0, sibling, me).wait_recv()
        for j, chip in enumerate(chips):
            for a in range(n):
                copy(a, 4 + j, (*chip, 1 - c), me).wait_recv()
        for cp in first + passed:
            cp.wait_send()
        for cp in mine:
            cp.wait()

    any_spec = pl.BlockSpec(memory_space=pl.ANY)
    return pl.pallas_call(
        body, out_shape=[SDS((N_DEV,) + s.shape, s.dtype) for s in shards], in_specs=[any_spec] * n,
        out_specs=[any_spec] * n,
        scratch_shapes=[pltpu.SemaphoreType.DMA((n, 7)), pltpu.SemaphoreType.DMA((n, 7)), pltpu.SemaphoreType.DMA((n,))],
        name=name,
    )(*shards)


_HBM = pl.BlockSpec(memory_space=pltpu.HBM)
_SEM = pl.BlockSpec(memory_space=pltpu.SEMAPHORE)
_EFFECT = pltpu.SideEffectType.DATAFLOW_SIDE_EFFECTING


def _peer_places():
    x, y, c = _my_place()
    peers = []
    for k in range(1, N_DEV):
        px = 1 - x if (k >> 2) & 1 else x
        py = 1 - y if (k >> 1) & 1 else y
        pc = 1 - c if k & 1 else c
        peers.append(((px, py, pc), 4 * px + 2 * py + pc))
    return (x, y, c), 4 * x + 2 * y + c, peers


def _send_start(srcs, per_dest, name):
    n = len(srcs)
    lands = [lax.empty((N_DEV,) + (s.shape[1:] if per_dest else s.shape), s.dtype) for s in srcs]

    def body(*refs):
        src, land = refs[:n], refs[n:2 * n]
        outs = refs[2 * n:]
        send, recv, token = outs[:n], outs[n:2 * n], outs[4 * n]
        _, me, peers = _peer_places()
        for a in range(n):
            for peer, pidx in peers:
                pltpu.make_async_remote_copy(
                    src_ref=src[a].at[pidx] if per_dest else src[a], dst_ref=land[a].at[me], send_sem=send[a],
                    recv_sem=recv[a], device_id=peer, device_id_type=MESH).start()
        token[...] = jnp.zeros_like(token)

    hbm = lambda a: pltpu.HBM(a.shape, a.dtype)
    sem = pltpu.SemaphoreType.DMA(())
    res = pl.pallas_call(
        body, name=name,
        out_shape=tuple([sem] * (2 * n) + [hbm(s) for s in srcs] + [hbm(l) for l in lands]
                        + [SDS((SUB, LANE), F32)]),
        in_specs=[_HBM] * (2 * n),
        out_specs=tuple([_SEM] * (2 * n) + [_HBM] * (2 * n) + [pl.BlockSpec(memory_space=pltpu.VMEM)]),
        input_output_aliases={i: 2 * n + i for i in range(2 * n)},
        compiler_params=pltpu.CompilerParams(has_side_effects=_EFFECT),
    )(*[pltpu.with_memory_space_constraint(s, pltpu.HBM) for s in srcs],
      *[pltpu.with_memory_space_constraint(l, pltpu.HBM) for l in lands])
    return res[:n], res[n:2 * n], res[2 * n:3 * n], res[3 * n:4 * n], res[4 * n]


def _send_wait(send, recv, srcs, lands, after, per_dest, name):
    n = len(srcs)

    def body(*refs):
        src, land = refs[:n], refs[n:2 * n]
        send_s, recv_s = refs[2 * n:3 * n], refs[3 * n:4 * n]
        token = refs[-1]
        place, _, _ = _peer_places()
        for a in range(n):
            seven = land[a].at[pl.ds(0, N_DEV - 1)]
            copy = pltpu.make_async_remote_copy(
                src_ref=src[a].at[pl.ds(0, N_DEV - 1)] if per_dest else seven, dst_ref=seven, send_sem=send_s[a],
                recv_sem=recv_s[a], device_id=place, device_id_type=MESH)
            copy.wait_send()
            copy.wait_recv()
        token[...] = jnp.zeros_like(token)

    hbm = lambda a: pltpu.HBM(a.shape, a.dtype)
    res = pl.pallas_call(
        body, name=name,
        out_shape=tuple([hbm(s) for s in srcs] + [hbm(l) for l in lands] + [SDS((SUB, LANE), F32)]),
        in_specs=[_HBM] * (2 * n) + [_SEM] * (2 * n) + [pl.BlockSpec(memory_space=pl.ANY)],
        out_specs=tuple([_HBM] * (2 * n) + [pl.BlockSpec(memory_space=pltpu.VMEM)]),
        input_output_aliases={i: i for i in range(2 * n)},
        compiler_params=pltpu.CompilerParams(has_side_effects=_EFFECT),
    )(*srcs, *lands, *send, *recv, after)
    return res[:n], res[n:2 * n], res[2 * n]


def _adamw_math(w, g, m, v):
    m = ADAM_B1 * m + (1.0 - ADAM_B1) * g
    v = ADAM_B2 * v + (1.0 - ADAM_B2) * (g * g)
    m_hat = m / (1.0 - ADAM_B1 ** ADAM_STEP)
    v_hat = v / (1.0 - ADAM_B2 ** ADAM_STEP)
    delta = -ADAM_LR * (m_hat / (jnp.sqrt(v_hat) + ADAM_EPS) + ADAM_WD * w)
    return delta, m, v


def _row_tile(r, c, itemsize_rows):
    cap = max(SUB, (itemsize_rows // (4 * c)) // SUB * SUB)
    if r <= cap:
        return r
    best = None
    for t in range(SUB, cap + 1, SUB):
        if r % t == 0:
            best = t
    return best if best is not None else r


def _sum_adamw(landing, w, m, v, name, layer=0, prev=None):
    _, r, c = landing.shape
    tr = _row_tile(r, c, 2 << 20)
    off = layer * (r // tr)

    def body(l_ref, w_ref, m_ref, v_ref, *rest):
        g_ref, d_ref, mo_ref, vo_ref = rest[-4:]
        g = l_ref[0].astype(F32)
        for s in range(1, N_DEV):
            g = g + l_ref[s].astype(F32)
        g_ref[...] = g
        d_ref[...], mo_ref[...], vo_ref[...] = _adamw_math(w_ref[...], g, m_ref[...], v_ref[...])

    blk = pl.BlockSpec((tr, c), lambda i: (i + off, 0))
    n_prev = 0 if prev is None else 4
    return pl.pallas_call(
        body, out_shape=[SDS(w.shape, F32)] * 4, grid=(r // tr,),
        in_specs=[pl.BlockSpec((N_DEV, tr, c), lambda i: (0, i, 0)), blk, blk, blk]
        + [pl.BlockSpec(memory_space=pl.ANY)] * n_prev,
        out_specs=[blk] * 4, input_output_aliases={4 + i: i for i in range(n_prev)}, name=name,
        compiler_params=_cp(1),
    )(landing, w, m, v, *([] if prev is None else prev))


def _sum8(landing, name):
    _, r, c = landing.shape

    def body(l_ref, g_ref):
        g = l_ref[0]
        for s in range(1, N_DEV):
            g = g + l_ref[s]
        g_ref[...] = g

    return pl.pallas_call(body, out_shape=SDS((r, c), F32), name=name, compiler_params=_cp(0))(landing)


def _adamw_small(repl_pack, own_pack, P, M, V):
    table, off = [], 0
    for name, shape in _REPL.items():
        table.append((name, shape if len(shape) > 1 else (1,) + shape, 0, off // LANE))
        off += _size(shape)
    off = _REPL_ROWS * LANE
    for name, shape in _SMALL_SHARDED.items():
        table.append((name, shape, 1, off // LANE))
        off += _size(shape)
    n = len(table)

    def body(*refs):
        packs, ins, outs = refs[:2], refs[2:2 + 3 * n], refs[2 + 3 * n:]
        for p, (_, shape, which, r0) in enumerate(table):
            w_ref, m_ref, v_ref = ins[3 * p:3 * p + 3]
            g_ref, d_ref, mo_ref, vo_ref = outs[4 * p:4 * p + 4]
            pack, rows, q = packs[which], shape[-2], shape[-1] // LANE
            lead = [()]
            for dim in shape[:-2]:
                lead = [t + (i,) for t in lead for i in range(dim)]
            for li, idx in enumerate(lead):
                if q == 1:
                    dst = g_ref.at[idx] if idx else g_ref
                    dst[...] = pack[r0 + li * rows:r0 + (li + 1) * rows, :]
                    continue
                for i in range(rows):
                    for k in range(q):
                        row = r0 + (li * rows + i) * q + k
                        g_ref[idx + (slice(i, i + 1), slice(k * LANE, (k + 1) * LANE))] = pack[row:row + 1, :]
            d_ref[...], mo_ref[...], vo_ref[...] = _adamw_math(w_ref[...], g_ref[...], m_ref[...], v_ref[...])

    ins, out_shape = [], []
    for name, shape, _, _ in table:
        ins += [t[name].reshape(shape) for t in (P, M, V)]
        out_shape += [SDS(shape, F32)] * 4
    res = pl.pallas_call(body, out_shape=out_shape, name="adamw_small", compiler_params=_cp(0))(
        repl_pack, own_pack, *ins)
    dicts = ({}, {}, {}, {})
    for p, (name, shape, _, _) in enumerate(table):
        for d, arr in zip(dicts, res[4 * p:4 * p + 4]):
            d[name] = arr.reshape(P[name].shape)
    return dicts


_BIG = {
    "ab_w_in": (1, D, 320), "ab_w_out": (1, 192, D), "c_w_pw1": (1, D, 256), "c_w_pw2": (1, 128, D),
    "xa_wq": (2, 128, D), "xa_wk": (2, 128, D), "xa_wv": (2, 128, D), "xa_wo": (2, 128, D),
    "f_w_up": (2, D, 768), "f_w_down": (2, 384, D),
}
_SMALL_SHARDED = {
    "a_conv_w": (1, 4, 128), "c_norm": (1, 128), "c_b_pw1": (1, 256), "c_dw_w": (1, 31, 128), "c_dw_b": (1, 128),
    "c_ln_g": (1, 128), "c_ln_b": (1, 128), "c_b_pw2": (1, 128), "f_dw_w": (2, 3, 384),
}
_REPL = {
    "ab_norm": (1, D), "a_conv_b": (1, D), "a_gate_x_w": (1, 8, 128, 128), "a_gate_x_b": (1, D),
    "a_gate_a_w": (1, 8, 128, 128), "a_gate_a_b": (1, D), "a_lambda": (1, D), "b_group_w": (1, 4, 128, 128),
    "b_group_b": (1, 512), "b_scale": (1, 512), "xa_norm": (2, D), "xa_mem_norm": (2, D), "f_norm": (2, D),
    "f_dw_b": (2, D_FF), "final_norm": (D,),
}


def _size(shape):
    n = 1
    for s in shape:
        n *= s
    return n


_N_SS = sum(_size(s) for s in _SMALL_SHARDED.values())
_N_REPL = sum(_size(s) for s in _REPL.values())
_REPL_ROWS = -(-_N_REPL // (N_DEV * SUB * LANE)) * SUB
_SS_ROWS = _N_SS // LANE
_SMALL_ROWS = -(-(_REPL_ROWS + _SS_ROWS) // SUB) * SUB


def _pack(parts, rows):
    flat = jnp.concatenate([p.reshape(-1).astype(F32) for p in parts])
    return jnp.pad(flat, (0, rows * LANE - flat.shape[0])).reshape(rows, LANE)


def _pair_blocks(v, bw):
    lead, n = v.shape[:-1], v.shape[-1]
    return jnp.swapaxes(v.reshape(lead + (2, n // (2 * bw), bw)), -3, -2).reshape(lead + (n,))


def _unpair_blocks(v, bw):
    lead, n = v.shape[:-1], v.shape[-1]
    return jnp.swapaxes(v.reshape(lead + (n // (2 * bw), 2, bw)), -3, -2).reshape(lead + (n,))


_GROUPS = {
    ("ab", 0): (("ab_w_in", 0),),
    ("ab", 1): (("ab_w_out", 0),),
    ("xa", 0): (("xa_wq", 0), ("xa_wk", 0), ("xa_wv", 0), ("xa_wo", 0)),
    ("f", 0): (("f_w_up", 0), ("f_w_down", 0)),
    ("c", 0): (("c_w_pw1", 0), ("c_w_pw2", 0)),
    ("xa", 1): (("xa_wq", 1), ("xa_wk", 1), ("xa_wv", 1), ("xa_wo", 1)),
    ("f", 1): (("f_w_up", 1), ("f_w_down", 1)),
}
_SEND_GROUPS = _GROUPS


def _weight_layout(name, g):
    if name == "ab_w_in":
        return jnp.swapaxes(g, 0, 1).reshape(D, N_DEV * 320)
    if name in ("c_w_pw1", "f_w_up"):
        return g
    return g.reshape(N_DEV * g.shape[1], D)


def _grad_blocks(name, l, G):
    _, r, c = _BIG[name]
    if name == "ab_w_in":
        return jnp.swapaxes(G[name].reshape(D, N_DEV, 320), 0, 1)
    if name == "c_w_pw1":
        return G[name]
    if name == "f_w_up":
        return G[f"{name}{l}"]
    return (G[name] if _BIG[name][0] == 1 else G[f"{name}{l}"]).reshape(N_DEV, r, c)


def _small_layouts(sm):
    W = {}
    sm = sm.reshape(N_DEV, -1)
    off = 0
    for name, shape in _SMALL_SHARDED.items():
        n = _size(shape)
        blocks = sm[:, off:off + n].reshape((N_DEV,) + shape)
        off += n
        W[name] = jnp.moveaxis(blocks, 0, -2).reshape(shape[:-1] + (N_DEV * shape[-1],))
    W["a_conv_w"], W["c_dw_w"] = W["a_conv_w"][0], W["c_dw_w"][0]
    W["c_b_pw1"] = _pair_blocks(W["c_b_pw1"], _CW_C)
    return W


def _with_own(land, src, me, per_dest):
    own = lax.dynamic_slice_in_dim(src, me, 1, 0) if per_dest else src[None]
    return lax.dynamic_update_slice_in_dim(land, own, me, 0)


def _to_dest_major(g, shape):
    full = g.reshape(shape[:-1] + (N_DEV, shape[-1]))
    return jnp.moveaxis(full, -2, 0).reshape(N_DEV, -1)


def kernel(x, mem, ab_norm, ab_w_in, a_conv_w, a_conv_b, a_gate_x_w, a_gate_x_b, a_gate_a_w, a_gate_a_b, a_lambda, b_group_w, b_group_b, b_scale, ab_w_out, c_norm, c_w_pw1, c_b_pw1, c_dw_w, c_dw_b, c_ln_g, c_ln_b, c_w_pw2, c_b_pw2, xa_norm, xa_mem_norm, xa_wq, xa_wk, xa_wv, xa_wo, f_norm, f_w_up, f_dw_w, f_dw_b, f_w_down, final_norm, loss_target, m_ab_norm, m_ab_w_in, m_a_conv_w, m_a_conv_b, m_a_gate_x_w, m_a_gate_x_b, m_a_gate_a_w, m_a_gate_a_b, m_a_lambda, m_b_group_w, m_b_group_b, m_b_scale, m_ab_w_out, m_c_norm, m_c_w_pw1, m_c_b_pw1, m_c_dw_w, m_c_dw_b, m_c_ln_g, m_c_ln_b, m_c_w_pw2, m_c_b_pw2, m_xa_norm, m_xa_mem_norm, m_xa_wq, m_xa_wk, m_xa_wv, m_xa_wo, m_f_norm, m_f_w_up, m_f_dw_w, m_f_dw_b, m_f_w_down, m_final_norm, v_ab_norm, v_ab_w_in, v_a_conv_w, v_a_conv_b, v_a_gate_x_w, v_a_gate_x_b, v_a_gate_a_w, v_a_gate_a_b, v_a_lambda, v_b_group_w, v_b_group_b, v_b_scale, v_ab_w_out, v_c_norm, v_c_w_pw1, v_c_b_pw1, v_c_dw_w, v_c_dw_b, v_c_ln_g, v_c_ln_b, v_c_w_pw2, v_c_b_pw2, v_xa_norm, v_xa_mem_norm, v_xa_wq, v_xa_wk, v_xa_wv, v_xa_wo, v_f_norm, v_f_w_up, v_f_dw_w, v_f_dw_b, v_f_w_down, v_final_norm):
    args = dict(locals())
    P = {n: args[n] for n in _NAMES}
    M = {n: args["m_" + n] for n in _NAMES}
    V = {n: args["v_" + n] for n in _NAMES}

    me = 4 * lax.axis_index("x") + 2 * lax.axis_index("y") + lax.axis_index("c")

    in_flight = {}

    def launch(groups, tok):
        shards, n_of = [], {}
        for grp in groups:
            for name, l in _GROUPS[grp]:
                w = P[name][l] if tok is None else P[name][l] + tok
                shards.append(w.astype(BF16))
            if grp == ("ab", 0):
                shards.append(_pack([P[n] for n in _SMALL_SHARDED], _SS_ROWS + 4))
            n_of[grp] = len(shards)
        res = _send_start(shards, False, "gather_start_" + "_".join(g[0] + str(g[1]) for g in groups))
        lo = 0
        for grp in groups:
            in_flight[grp] = [r[lo:n_of[grp]] for r in res[:4]]
            lo = n_of[grp]
        return res[4][:1, :1]

    follow = {("ab", 0): [("ab", 1), ("xa", 0), ("f", 0)], ("xa", 0): [("c", 0)], ("f", 0): [("xa", 1)],
              ("c", 0): [("f", 1)]}

    def fetch(grp, after):
        send_s, recv_s, srcs, lands = in_flight.pop(grp)
        srcs, lands, tok = _send_wait(send_s, recv_s, srcs, lands, after, False, f"gather_wait_{grp[0]}{grp[1]}")
        tok = launch(follow[grp], tok[:1, :1]) if grp in follow else None
        full = [_with_own(land, src, me, False) for land, src in zip(lands, srcs)]
        out = {}
        for (name, l), g in zip(_GROUPS[grp], full):
            w = _weight_layout(name, g)
            if _BIG[name][0] == 1:
                out[name] = w
            else:
                out[name] = {l: w}
        if grp == ("ab", 0):
            out.update(_small_layouts(full[-1]))
        return out, tok

    zero = launch([("ab", 0)], None)

    pending = []

    def send(grp, G):
        members = _SEND_GROUPS[grp]
        res = _send_start([_grad_blocks(name, l, G) for name, l in members], True, f"send_{grp[0]}{grp[1]}")
        pending.append((members, res))
        return res[4][:1, :1]

    W = {n: P[n] for n in _REPL}
    W["ab_norm"] = P["ab_norm"] + zero
    W["final_norm"] = P["final_norm"].reshape(1, D)
    W["a_gate_x_w"], W["a_gate_a_w"], W["b_group_w"] = P["a_gate_x_w"][0], P["a_gate_a_w"][0], P["b_group_w"][0]
    loss, grad_x, G = _local_step(x[0], mem[0], loss_target[0], W, fetch, send)
    loss = lax.psum(loss[0, 0], ("x", "y", "c"))

    Gs = dict(G)
    Gs["c_b_pw1"] = _unpair_blocks(G["c_b_pw1"], _CW_C)
    Gs["f_dw_w"] = jnp.stack([G["f_dw_w0"], G["f_dw_w1"]])
    Gs["a_conv_w"], Gs["c_dw_w"] = G["a_conv_w"][None], G["c_dw_w"][None]
    for n in ("xa_norm", "xa_mem_norm", "f_norm", "f_dw_b"):
        Gs[n] = jnp.concatenate([G[f"{n}0"], G[f"{n}1"]], axis=0)
    for n in ("a_gate_x_w", "a_gate_a_w", "b_group_w"):
        Gs[n] = G[n][None]
    repl_flat = jnp.concatenate([Gs[n].reshape(-1) for n in _REPL])
    repl_rows = jnp.pad(repl_flat, (0, N_DEV * _REPL_ROWS * LANE - _N_REPL)).reshape(N_DEV, _REPL_ROWS, LANE)
    ss_rows = jnp.concatenate([_to_dest_major(Gs[n], s) for n, s in _SMALL_SHARDED.items()], axis=1)
    ss_rows = ss_rows.reshape(N_DEV, _SS_ROWS, LANE)
    small_pack = jnp.concatenate(
        [repl_rows, ss_rows, jnp.zeros((N_DEV, _SMALL_ROWS - _REPL_ROWS - _SS_ROWS, LANE), F32)], axis=1)
    last = _send_start([small_pack], True, "send_small")
    pending.append(((("small", 0),), last))

    members = [m for mem_, _ in pending for m in mem_]
    cat = [[a for _, res in pending for a in res[i]] for i in range(4)]
    srcs, lands, _ = _send_wait(cat[0], cat[1], cat[2], cat[3], grad_x, True, "send_wait")
    landed = {m: _with_own(land, src, me, True) for m, land, src in zip(members, lands, srcs)}

    out_g, out_d, out_m, out_v = {}, {}, {}, {}
    for name, (layers, r, c) in _BIG.items():
        shape = P[name].shape
        w2, m2, v2 = [t[name].reshape(layers * r, c) for t in (P, M, V)]
        res = None
        for l in range(layers):
            res = _sum_adamw(landed[(name, l)], w2, m2, v2, f"adamw_{name}{l}", layer=l, prev=res)
        out_g[name], out_d[name], out_m[name], out_v[name] = [t.reshape(shape) for t in res]

    small_sum = _sum8(landed[("small", 0)], "sum_small")
    (repl_all,) = _all_gather([small_sum[:_REPL_ROWS]], "gather_small_grads")
    for out, got in zip((out_g, out_d, out_m, out_v),
                        _adamw_small(repl_all.reshape(N_DEV * _REPL_ROWS, LANE), small_sum, P, M, V)):
        out.update(got)

    return (loss, grad_x[None], *[out_g[n] for n in _NAMES], *[out_d[n] for n in _NAMES],
            *[out_m[n] for n in _NAMES], *[out_v[n] for n in _NAMES])


_NAMES = ("ab_norm", "ab_w_in", "a_conv_w", "a_conv_b", "a_gate_x_w", "a_gate_x_b", "a_gate_a_w", "a_gate_a_b",
          "a_lambda", "b_group_w", "b_group_b", "b_scale", "ab_w_out", "c_norm", "c_w_pw1", "c_b_pw1", "c_dw_w",
          "c_dw_b", "c_ln_g", "c_ln_b", "c_w_pw2", "c_b_pw2", "xa_norm", "xa_mem_norm", "xa_wq", "xa_wk", "xa_wv",
          "xa_wo", "f_norm", "f_w_up", "f_dw_w", "f_dw_b", "f_w_down", "final_norm")
```

```python
import functools

import jax
import jax.numpy as jnp
from jax import lax
from jax.experimental import pallas as pl
from jax.experimental.pallas import tpu as pltpu

F32, BF16 = jnp.float32, jnp.bfloat16
SDS = jax.ShapeDtypeStruct
MESH = pl.DeviceIdType.MESH

N_DEV = 8
D = 1024
N_MEM = 256
XA_HEADS, XA_HD = 4, 256
HD_A = 128
CONV_A, CONV_C, CONV_F = 4, 31, 3
C_RG = 8.0
POOL_WINDOWS = (2, 4, 8, 16)
D_FF = 3 * D
EPS = 1e-6
ADAM_LR, ADAM_B1, ADAM_B2, ADAM_EPS, ADAM_WD, ADAM_STEP = 0.001, 0.9, 0.999, 1e-08, 0.01, 10

LANE = 128
SUB = 8
VMEM_LIMIT = 56 * 1024 * 1024
R_SEQ = 512
R_RGLRU = 256
R_FFN = 1024
TM_ROW = 512


def _cp(n_axes):
    return pltpu.CompilerParams(dimension_semantics=("arbitrary",) * n_axes, vmem_limit_bytes=VMEM_LIMIT)


def _tile(n, pref):
    if n <= pref:
        return n
    best = None
    for t in range(LANE, pref + 1, LANE):
        if n % t == 0:
            best = t
    assert best is not None, (n, pref)
    return best


def _perm2(n):
    return (n % 2) * 4 + n // 2


_NN = (((1,), (0,)), ((), ()))
_NT = (((1,), (1,)), ((), ()))
_TN = (((0,), (0,)), ((), ()))


def _mm_call(name, grid, ab, ab_specs, dims, acc_shape, extras, outs, finish, from_ref=False):
    nk = grid[2]
    n_ab, n_ex, n_out = len(ab), len(extras), len(outs)
    use_acc = nk > 1 or from_ref

    def product(refs):
        r = lax.dot_general(refs[0][...], refs[1][...], dims, preferred_element_type=F32)
        for i in range(1, n_ab):
            r = r + lax.dot_general(refs[2 * i][...], refs[2 * i + 1][...], dims, preferred_element_type=F32)
        return r

    def body(*refs):
        rest = refs[2 * n_ab:]
        ex_refs, o_refs = rest[:n_ex], rest[n_ex:n_ex + n_out]
        first_rows = pl.program_id(0) == 0
        if not use_acc:
            finish(product(refs), ex_refs, o_refs, first_rows)
            return
        acc = rest[n_ex + n_out]
        if nk == 1:
            acc[...] = product(refs)
            finish(acc, ex_refs, o_refs, first_rows)
            return
        k = pl.program_id(2)

        @pl.when(k == 0)
        def _():
            acc[...] = jnp.zeros_like(acc)

        acc[...] += product(refs)

        @pl.when(k == nk - 1)
        def _():
            finish(acc if from_ref else acc[...], ex_refs, o_refs, first_rows)

    res = pl.pallas_call(
        body, out_shape=[o for o, _ in outs], grid=grid,
        in_specs=list(ab_specs) + [s for _, s in extras], out_specs=[s for _, s in outs],
        scratch_shapes=[pltpu.VMEM(acc_shape, F32)] if use_acc else [], name=name, compiler_params=_cp(3),
    )(*[t for pair in ab for t in pair], *[e for e, _ in extras])
    return res[0] if n_out == 1 else res


def _finish_sum(r, ex_refs, o_refs, first_rows):
    del first_rows
    for e in ex_refs:
        r = r + e[...]
    o_refs[0][...] = r.astype(o_refs[0].dtype)


def _finish_sum_norm(r, ex_refs, o_refs, first_rows):
    del first_rows
    for e in ex_refs[:-1]:
        r = r + e[...]
    o_refs[0][...] = r
    o_refs[1][...] = ((r * lax.rsqrt(jnp.mean(r * r, axis=-1, keepdims=True) + EPS)) * ex_refs[-1][...]).astype(BF16)


_EPI_ROWS = 16


def _finish_rms_bwd(r_ref, ex_refs, o_refs, first_rows):
    x_ref, g_ref, dres_ref = ex_refs
    dx_ref, dxb_ref, dg_ref = o_refs

    @pl.when(first_rows)
    def _():
        dg_ref[...] = jnp.zeros_like(dg_ref)

    gv = g_ref[...]
    inv_d = 1.0 / r_ref.shape[1]

    def step(i, dg_acc):
        groups = [pl.ds(pl.multiple_of(i * (2 * _EPI_ROWS) + u * _EPI_ROWS, _EPI_ROWS), _EPI_ROWS) for u in range(2)]
        sums = []
        for rows in groups:
            r, xf = r_ref[rows, :], x_ref[rows, :]
            sums.append((jnp.sum(xf * xf, axis=-1, keepdims=True), jnp.sum((r * gv) * xf, axis=-1, keepdims=True)))
        for rows, (sxx, sax) in zip(groups, sums):
            r, xf = r_ref[rows, :], x_ref[rows, :]
            rs = lax.rsqrt(sxx * inv_d + EPS)
            dg_acc = dg_acc + _psum8(r * (xf * rs))
            dx = rs * (r * gv) - xf * (rs * rs * (sax * rs * inv_d)) + dres_ref[rows, :]
            dx_ref[rows, :] = dx
            dxb_ref[rows, :] = dx.astype(BF16)
        return dg_acc

    dg_acc = lax.fori_loop(0, r_ref.shape[0] // (2 * _EPI_ROWS), step, jnp.zeros((SUB, r_ref.shape[1]), F32))
    dg_ref[...] += jnp.sum(dg_acc, axis=0, keepdims=True)


def _rms_bwd_io(M, tm, x, g, dres):
    rows = pl.BlockSpec((tm, D), lambda m, n, k: (m, 0))
    vec = pl.BlockSpec((1, D), lambda m, n, k: (0, 0))
    return ([(x, rows), (g, vec), (dres, rows)],
            [(SDS((M, D), F32), rows), (SDS((M, D), BF16), rows), (SDS((1, D), F32), vec)])


_K_WHOLE = 3072


def _mm_nn(a, b, *, out_dtype, name, bias=None, add=None, norm=None):
    M, K = a.shape
    tk = K if K <= _K_WHOLE else _tile(K, 1024)
    tm = _tile(M, 1024 if K <= 1024 and norm is None else 512)
    if b.ndim == 3:
        nb, _, bw = b.shape
        N, tn, nn = nb * bw, bw, nb
        b_spec = pl.BlockSpec((None, tk, bw), lambda m, n, k: (_perm2(n), k, 0))
    else:
        N = b.shape[1]
        tn = _tile(N, 1024)
        nn = N // tn
        b_spec = pl.BlockSpec((tk, tn), lambda m, n, k: (k, n))
    tile = pl.BlockSpec((tm, tn), lambda m, n, k: (m, n))
    vec = pl.BlockSpec((1, tn), lambda m, n, k: (0, n))
    extras = ([] if bias is None else [(bias, vec)]) + ([] if add is None else [(add, tile)])
    outs, finish = [(SDS((M, N), out_dtype), tile)], _finish_sum
    if norm is not None:
        assert tn == N == D and out_dtype == F32
        extras.append((norm, vec))
        outs, finish = outs + [(SDS((M, N), BF16), tile)], _finish_sum_norm
    return _mm_call(name, (M // tm, nn, K // tk), [(a, b)], [pl.BlockSpec((tm, tk), lambda m, n, k: (m, k)), b_spec],
                    _NN, (tm, tn), extras, outs, finish)


def _mm_nt(a, b, *, out_dtype, name, add=None, rms=None):
    M, N = a.shape
    if b.ndim == 3:
        nb, Ko, bw = b.shape
        tm = _tile(M, 1024)
        tn, tk, nk = _tile(Ko, 1024), bw, nb
        b_spec = pl.BlockSpec((None, tn, bw), lambda m, n, k: (_perm2(k), n, 0))
    else:
        Ko = b.shape[0]
        tk = N if N <= _K_WHOLE else _tile(N, 1024)
        tm = _tile(M, 1024 if N <= 1024 and rms is None else 512)
        tn = _tile(Ko, 1024)
        nk = N // tk
        b_spec = pl.BlockSpec((tn, tk), lambda m, n, k: (n, k))
    tile = pl.BlockSpec((tm, tn), lambda m, n, k: (m, n))
    extras = [] if add is None else [(add, tile)]
    outs, finish = [(SDS((M, Ko), out_dtype), tile)], _finish_sum
    if rms is not None:
        assert tn == Ko == D and add is None
        (extras, outs), finish = _rms_bwd_io(M, tm, *rms), _finish_rms_bwd
    return _mm_call(name, (M // tm, Ko // tn, nk), [(a, b)], [pl.BlockSpec((tm, tk), lambda m, n, k: (m, k)), b_spec],
                    _NT, (tm, tn), extras, outs, finish, from_ref=rms is not None)


def _mm_nt_cols(parts, b, *, name, rms):
    M = parts[0].shape[0]
    tm = _tile(M, 512)
    specs, off = [], 0
    for p in parts:
        w = p.shape[1]
        assert off % w == 0
        specs.append(pl.BlockSpec((tm, w), lambda m, n, k: (m, 0)))
        specs.append(pl.BlockSpec((D, w), functools.partial(lambda m, n, k, o: (0, o), o=off // w)))
        off += w
    extras, outs = _rms_bwd_io(M, tm, *rms)
    return _mm_call(name, (M // tm, 1, 1), [(p, b) for p in parts], specs, _NT, (tm, D), extras, outs, _finish_rms_bwd,
                    from_ref=True)


def _mm_tn(a, b, *, out_dtype, name, blocks=None):
    S, Ka = a.shape
    Nb = b.shape[1]
    tm, tk = _tile(Ka, 1024), _tile(S, 2048)
    if blocks is not None:
        bw = blocks
        tn, nn = bw, Nb // bw
        out = (SDS((nn, Ka, bw), out_dtype), pl.BlockSpec((None, tm, bw), lambda m, n, k: (_perm2(n), m, 0)))
    else:
        tn = _tile(Nb, 1024)
        nn = Nb // tn
        out = (SDS((Ka, Nb), out_dtype), pl.BlockSpec((tm, tn), lambda m, n, k: (m, n)))
    return _mm_call(name, (Ka // tm, nn, S // tk), [(a, b)],
                    [pl.BlockSpec((tk, tm), lambda m, n, k: (k, m)), pl.BlockSpec((tk, tn), lambda m, n, k: (k, n))],
                    _TN, (tm, tn), [], [out], _finish_sum)


def _row(tm, c):
    return pl.BlockSpec((tm, c), lambda i: (i, 0))


def _full(shape):
    nd = len(shape)
    return pl.BlockSpec(shape, lambda i: (0,) * nd)


def _rms_fwd(x, g, name):
    S = x.shape[0]
    tm = min(S, TM_ROW)

    def body(x_ref, g_ref, o_ref):
        xf = x_ref[...]
        r = lax.rsqrt(jnp.mean(xf * xf, axis=-1, keepdims=True) + EPS)
        o_ref[...] = ((xf * r) * g_ref[...]).astype(BF16)

    return pl.pallas_call(body, out_shape=SDS((S, D), BF16), grid=(S // tm,), in_specs=[_row(tm, D), _full((1, D))],
                          out_specs=_row(tm, D), name=name, compiler_params=_cp(1))(x, g)


def _rms_bwd(x, g, dn, dres, name):
    S = x.shape[0]
    tm = min(S, TM_ROW)
    want_dx = dres is not None

    def body(x_ref, g_ref, dn_ref, *rest):
        i = pl.program_id(0)
        dg_ref = rest[-1]

        @pl.when(i == 0)
        def _():
            dg_ref[...] = jnp.zeros_like(dg_ref)

        xf = x_ref[...]
        r = lax.rsqrt(jnp.mean(xf * xf, axis=-1, keepdims=True) + EPS)
        y = xf * r
        dn_v = dn_ref[...]
        dg_ref[...] += jnp.sum(dn_v * y, axis=0, keepdims=True)
        if want_dx:
            dres_ref, dx_ref, dxb_ref = rest[0], rest[1], rest[2]
            dy = dn_v * g_ref[...]
            dx = r * (dy - y * jnp.mean(dy * y, axis=-1, keepdims=True)) + dres_ref[...]
            dx_ref[...] = dx
            dxb_ref[...] = dx.astype(BF16)

    ins = [x, g, dn] + ([dres] if want_dx else [])
    in_specs = [_row(tm, D), _full((1, D)), _row(tm, D)] + ([_row(tm, D)] if want_dx else [])
    outs = ([SDS((S, D), F32), SDS((S, D), BF16)] if want_dx else []) + [SDS((1, D), F32)]
    out_specs = ([_row(tm, D), _row(tm, D)] if want_dx else []) + [_full((1, D))]
    return pl.pallas_call(body, out_shape=outs, grid=(S // tm,), in_specs=in_specs, out_specs=out_specs, name=name,
                          compiler_params=_cp(1))(*ins)


def _loss_head(x, g, tgt):
    S = x.shape[0]
    tm = min(S, TM_ROW)

    def body(x_ref, g_ref, t_ref, loss_ref, dx_ref, dxb_ref, dg_ref):
        i = pl.program_id(0)

        @pl.when(i == 0)
        def _():
            loss_ref[...] = jnp.zeros_like(loss_ref)
            dg_ref[...] = jnp.zeros_like(dg_ref)

        xf = x_ref[...]
        r = lax.rsqrt(jnp.mean(xf * xf, axis=-1, keepdims=True) + EPS)
        y = xf * r
        gv = g_ref[...]
        err = y * gv - t_ref[...]
        per_row = jnp.mean(err * err, axis=-1, keepdims=True)
        loss_ref[...] += 0.5 * jnp.sum(per_row, axis=0, keepdims=True)
        dn_v = err * (1.0 / D)
        dg_ref[...] += jnp.sum(dn_v * y, axis=0, keepdims=True)
        dy = dn_v * gv
        dx = r * (dy - y * jnp.mean(dy * y, axis=-1, keepdims=True))
        dx_ref[...] = dx
        dxb_ref[...] = dx.astype(BF16)

    return pl.pallas_call(
        body, out_shape=[SDS((1, 1), F32), SDS((S, D), F32), SDS((S, D), BF16), SDS((1, D), F32)], grid=(S // tm,),
        in_specs=[_row(tm, D), _full((1, D)), _row(tm, D)],
        out_specs=[_full((1, 1)), _row(tm, D), _row(tm, D), _full((1, D))], name="loss_head", compiler_params=_cp(1),
    )(x, g, tgt)


def _softmax_rows(s):
    m = jnp.max(s, axis=-1, keepdims=True)
    e = jnp.exp(s - m)
    return e / jnp.sum(e, axis=-1, keepdims=True)


def _attn_fwd(q, k, v, name):
    S = q.shape[0]
    tm = min(S, TM_ROW)
    scale = XA_HD ** -0.5

    def body(q_ref, k_ref, v_ref, o_ref):
        for h in range(XA_HEADS):
            sl = slice(h * XA_HD, (h + 1) * XA_HD)
            s = lax.dot_general(q_ref[:, sl], k_ref[:, sl], _NT, preferred_element_type=F32) * scale
            p = _softmax_rows(s)
            o_ref[:, sl] = lax.dot_general(p.astype(BF16), v_ref[:, sl], _NN, preferred_element_type=F32).astype(BF16)

    return pl.pallas_call(body, out_shape=SDS((S, D), BF16), grid=(S // tm,),
                          in_specs=[_row(tm, D), _full((N_MEM, D)), _full((N_MEM, D))], out_specs=_row(tm, D),
                          name=name, compiler_params=_cp(1))(q, k, v)


def _attn_bwd(q, k, v, do, name):
    S = q.shape[0]
    tm = min(S, TM_ROW)
    scale = XA_HD ** -0.5

    def body(q_ref, k_ref, v_ref, do_ref, dq_ref, dk_ref, dv_ref):
        i = pl.program_id(0)

        @pl.when(i == 0)
        def _():
            dk_ref[...] = jnp.zeros_like(dk_ref)
            dv_ref[...] = jnp.zeros_like(dv_ref)

        for h in range(XA_HEADS):
            sl = slice(h * XA_HD, (h + 1) * XA_HD)
            qh, kh, vh, doh = q_ref[:, sl], k_ref[:, sl], v_ref[:, sl], do_ref[:, sl]
            s = lax.dot_general(qh, kh, _NT, preferred_element_type=F32) * scale
            p = _softmax_rows(s)
            pb = p.astype(BF16)
            dv_ref[:, sl] += lax.dot_general(pb, doh, _TN, preferred_element_type=F32)
            dp = lax.dot_general(doh, vh, _NT, preferred_element_type=F32)
            ds = (p * (dp - jnp.sum(dp * p, axis=-1, keepdims=True)) * scale).astype(BF16)
            dq_ref[:, sl] = lax.dot_general(ds, kh, _NN, preferred_element_type=F32).astype(BF16)
            dk_ref[:, sl] += lax.dot_general(ds, qh, _TN, preferred_element_type=F32)

    return pl.pallas_call(
        body, out_shape=[SDS((S, D), BF16), SDS((N_MEM, D), F32), SDS((N_MEM, D), F32)], grid=(S // tm,),
        in_specs=[_row(tm, D), _full((N_MEM, D)), _full((N_MEM, D)), _row(tm, D)],
        out_specs=[_row(tm, D), _full((N_MEM, D)), _full((N_MEM, D))], name=name, compiler_params=_cp(1),
    )(q, k, v, do)


def _sigmoid(x):
    return 1.0 / (1.0 + jnp.exp(-x))


def _ln_silu_fwd(cv, g, b):
    S = cv.shape[0]
    tm = min(S, TM_ROW)

    def body(x_ref, g_ref, b_ref, o_ref):
        xf = x_ref[...]
        mu = jnp.mean(xf, axis=-1, keepdims=True)
        xc = xf - mu
        rstd = lax.rsqrt(jnp.mean(xc * xc, axis=-1, keepdims=True) + EPS)
        ln = (xc * rstd) * g_ref[...] + b_ref[...]
        o_ref[...] = (ln * _sigmoid(ln)).astype(BF16)

    return pl.pallas_call(body, out_shape=SDS((S, D), BF16), grid=(S // tm,),
                          in_specs=[_row(tm, D), _full((1, D)), _full((1, D))], out_specs=_row(tm, D),
                          name="ln_silu_fwd", compiler_params=_cp(1))(cv, g, b)


def _ln_silu_bwd(ds, cv, g, b, dx):
    S = cv.shape[0]
    tm = min(S, TM_ROW)

    def body(ds_ref, x_ref, g_ref, b_ref, dx_ref, dcv_ref, dg_ref, db_ref, db2_ref):
        i = pl.program_id(0)

        @pl.when(i == 0)
        def _():
            dg_ref[...] = jnp.zeros_like(dg_ref)
            db_ref[...] = jnp.zeros_like(db_ref)
            db2_ref[...] = jnp.zeros_like(db2_ref)

        xf = x_ref[...]
        mu = jnp.mean(xf, axis=-1, keepdims=True)
        xc = xf - mu
        rstd = lax.rsqrt(jnp.mean(xc * xc, axis=-1, keepdims=True) + EPS)
        xhat = xc * rstd
        gv = g_ref[...]
        ln = xhat * gv + b_ref[...]
        sg = _sigmoid(ln)
        dln = ds_ref[...].astype(F32) * (sg + ln * sg * (1.0 - sg))
        dg_ref[...] += jnp.sum(dln * xhat, axis=0, keepdims=True)
        db_ref[...] += jnp.sum(dln, axis=0, keepdims=True)
        db2_ref[...] += jnp.sum(dx_ref[...], axis=0, keepdims=True)
        dxh = dln * gv
        dcv_ref[...] = rstd * (dxh - jnp.mean(dxh, axis=-1, keepdims=True)
                               - xhat * jnp.mean(dxh * xhat, axis=-1, keepdims=True))

    return pl.pallas_call(
        body, out_shape=[SDS((S, D), F32), SDS((1, D), F32), SDS((1, D), F32), SDS((1, D), F32)], grid=(S // tm,),
        in_specs=[_row(tm, D), _row(tm, D), _full((1, D)), _full((1, D)), _row(tm, D)],
        out_specs=[_row(tm, D), _full((1, D)), _full((1, D)), _full((1, D))], name="ln_silu_bwd",
        compiler_params=_cp(1),
    )(ds, cv, g, b, dx)


_GELU_C, _GELU_K = 0.7978845608028654, 0.044715


def _gelu(x, with_grad=False):
    x2 = x * x
    t = jnp.tanh(_GELU_C * (x + _GELU_K * x * x2))
    gel = 0.5 * x * (1.0 + t)
    if not with_grad:
        return gel
    return gel, 0.5 * (1.0 + t) + 0.5 * x * (1.0 - t * t) * (_GELU_C * (1.0 + 3.0 * _GELU_K * x2))


def _expm1(x):
    poly = x * (1.0 + x * (0.5 + x * (1.0 / 6.0 + x * (1.0 / 24.0 + x * (1.0 / 120.0)))))
    return jnp.where(jnp.abs(x) < 0.05, poly, jnp.exp(x) - 1.0)


def _softplus(x):
    return jnp.maximum(x, 0.0) + jnp.log1p(jnp.exp(-jnp.abs(x)))


_SCAN_UNROLL = 4
_RB = 32
_HB = 16


def _sub_blocks(n_rows, n_lanes, fn):
    def step(idx, c):
        r0 = pl.multiple_of(idx * _RB, _RB)
        for lt in range(n_lanes // LANE):
            fn(r0, lt)
        return c

    lax.fori_loop(0, n_rows // _RB, step, 0)


def _lanes(lt):
    return pl.ds(lt * LANE, LANE)


def _psum8(x):
    parts = [x[i * SUB:(i + 1) * SUB] for i in range(x.shape[0] // SUB)]
    return functools.reduce(lambda p, q: p + q, parts)


def _scan_fwd(a_s, b_s, out_ref, carry_ref, n_groups):
    row = lax.broadcasted_iota(jnp.int32, (SUB, LANE), 0)
    U = _SCAN_UNROLL

    def step(gi, carry):
        base = gi * (SUB * U)
        parts = []
        for u in range(U):
            i = pl.multiple_of(base + u * SUB, SUB)
            a8, b8 = a_s[pl.ds(i, SUB), :], b_s[pl.ds(i, SUB), :]
            for s in (1, 2, 4):
                a_sh = jnp.where(row >= s, pltpu.roll(a8, s, 0), 1.0)
                b_sh = jnp.where(row >= s, pltpu.roll(b8, s, 0), 0.0)
                b8 = a8 * b_sh + b8
                a8 = a8 * a_sh
            parts.append((i, a8, b8))
        for i, a8, b8 in parts:
            h8 = a8 * carry + b8
            out_ref[pl.ds(i, SUB), :] = h8
            carry = jnp.broadcast_to(h8[SUB - 1:SUB, :], (SUB, LANE))
        return carry

    carry_ref[...] = lax.fori_loop(0, n_groups // U, step, carry_ref[...])


def _scan_bwd(a_s, b_s, out_ref, carry_ref, n_groups):
    row = lax.broadcasted_iota(jnp.int32, (SUB, LANE), 0)
    U = _SCAN_UNROLL

    def step(gi, carry):
        base = (n_groups // U - 1 - gi) * (SUB * U)
        parts = []
        for u in reversed(range(U)):
            i = pl.multiple_of(base + u * SUB, SUB)
            a8, b8 = a_s[pl.ds(i, SUB), :], b_s[pl.ds(i, SUB), :]
            for s in (1, 2, 4):
                a_sh = jnp.where(row < SUB - s, pltpu.roll(a8, SUB - s, 0), 1.0)
                b_sh = jnp.where(row < SUB - s, pltpu.roll(b8, SUB - s, 0), 0.0)
                b8 = a8 * b_sh + b8
                a8 = a8 * a_sh
            parts.append((i, a8, b8))
        for i, a8, b8 in parts:
            h8 = a8 * carry + b8
            out_ref[pl.ds(i, SUB), :] = h8
            carry = jnp.broadcast_to(h8[0:1, :], (SUB, LANE))
        return carry

    carry_ref[...] = lax.fori_loop(0, n_groups // U, step, carry_ref[...])


def _rglru_pre(xr, wgx_ref, bgx_ref, wga_ref, bga_ref, lam_ref):
    xrb = xr.astype(BF16)
    wgx, wga = wgx_ref[0].astype(BF16), wga_ref[0].astype(BF16)
    gx = _sigmoid(lax.dot_general(xrb, wgx, _NN, preferred_element_type=F32) + bgx_ref[...])
    ga = _sigmoid(lax.dot_general(xrb, wga, _NN, preferred_element_type=F32) + bga_ref[...])
    sp = _softplus(-lam_ref[...])
    log_a = -C_RG * ga * sp
    a = jnp.exp(log_a)
    mult = jnp.sqrt(-_expm1(2.0 * log_a))
    return gx, ga, sp, a, mult, xrb, wgx, wga


def _a_specs():
    vec = pl.BlockSpec((1, HD_A), lambda c, j: (0, c))
    mat = pl.BlockSpec((1, HD_A, HD_A), lambda c, j: (c, 0, 0))
    return [pl.BlockSpec((CONV_A, HD_A), lambda c, j: (0, c)), vec, mat, vec, mat, vec, vec]


def _a_fwd(zp, conv_w, conv_b, wgx, bgx, wga, bga, lam):
    S = zp.shape[0]
    R, nt = R_RGLRU, D // HD_A
    H = SUB

    def body(zg_ref, zr_ref, cw_ref, cb_ref, wgx_ref, bgx_ref, wga_ref, bga_ref, lam_ref, ya_ref, h_ref,
             ext, a_s, b_s, hc):
        j = pl.program_id(1)

        @pl.when(j == 0)
        def _():
            ext[0:H, :] = jnp.zeros((H, HD_A), F32)
            hc[...] = jnp.zeros_like(hc)

        ext[H:H + R, :] = zr_ref[...].astype(F32)
        xr = cb_ref[...]
        for k in range(CONV_A):
            xr = xr + cw_ref[k:k + 1, :] * ext[pl.ds(H - (CONV_A - 1 - k), R), :]
        gx, _, _, a, mult, _, _, _ = _rglru_pre(xr, wgx_ref, bgx_ref, wga_ref, bga_ref, lam_ref)
        a_s[...] = a
        b_s[...] = mult * (gx * xr)
        _scan_fwd(a_s, b_s, h_ref, hc, R // SUB)
        ya_ref[...] = (_gelu(zg_ref[...].astype(F32)) * h_ref[...]).astype(BF16)
        ext[0:H, :] = ext[R:R + H, :]

    return pl.pallas_call(
        body, out_shape=[SDS((S, D + D // 2), BF16), SDS((S, D), F32)], grid=(nt, S // R),
        in_specs=[pl.BlockSpec((R, HD_A), lambda c, j: (j, c)), pl.BlockSpec((R, HD_A), lambda c, j: (j, nt + c))]
        + _a_specs(),
        out_specs=[pl.BlockSpec((R, HD_A), lambda c, j: (j, c)), pl.BlockSpec((R, HD_A), lambda c, j: (j, c))],
        scratch_shapes=[pltpu.VMEM((H + R, HD_A), F32), pltpu.VMEM((R, HD_A), F32), pltpu.VMEM((R, HD_A), F32),
                        pltpu.VMEM((SUB, HD_A), F32)],
        name="rglru_fwd", compiler_params=_cp(2),
    )(zp, zp, conv_w, conv_b, wgx, bgx, wga, bga, lam)


def _a_bwd(dyab, zp, h, conv_w, conv_b, wgx, bgx, wga, bga, lam):
    S = zp.shape[0]
    R, nt, nch = R_RGLRU, D // HD_A, S // R_RGLRU
    H = SUB

    def rows(c, j):
        return (nch - 1 - j, c)

    def rows_rec(c, j):
        return (nch - 1 - j, nt + c)

    def halo(c, j):
        return (jnp.maximum((nch - 1 - j) * (R // H) - 1, 0), c)

    def halo_z(c, j):
        return (jnp.maximum((nch - 1 - j) * (R // _HB) - 1, 0), nt + c)

    def body(dy_ref, zg_ref, zr_ref, zh_ref, h_ref, hh_ref, cw_ref, cb_ref, wgx_ref, bgx_ref, wga_ref, bga_ref,
             lam_ref, dzg_ref, dzr_ref, dcw_ref, dcb_ref, dwgx_ref, dbgx_ref, dwga_ref, dbga_ref, dlam_ref,
             ext_z, ext_h, ext_mu, ext_d, a_s, b_s, muc):
        j = pl.program_id(1)
        first_chunk = (nch - 1 - j) == 0

        @pl.when(j == 0)
        def _():
            ext_mu[R:R + H, :] = jnp.zeros((H, HD_A), F32)
            ext_d[R:R + H, :] = jnp.zeros((H, HD_A), F32)
            muc[...] = jnp.zeros_like(muc)
            for r in (dcw_ref, dcb_ref, dwgx_ref, dbgx_ref, dwga_ref, dbga_ref, dlam_ref):
                r[...] = jnp.zeros_like(r)

        zg = zg_ref[...].astype(F32)
        ext_z[0:H, :] = jnp.where(first_chunk, 0.0, zh_ref[_HB - H:_HB, :].astype(F32))
        ext_z[H:H + R, :] = zr_ref[...].astype(F32)
        ext_h[0:H, :] = jnp.where(first_chunk, 0.0, hh_ref[...])
        ext_h[H:H + R, :] = h_ref[...]
        xr = cb_ref[...]
        for k in range(CONV_A):
            xr = xr + cw_ref[k:k + 1, :] * ext_z[pl.ds(H - (CONV_A - 1 - k), R), :]
        gx, ga, sp, a, mult, xrb, wgxb, wgab = _rglru_pre(xr, wgx_ref, bgx_ref, wga_ref, bga_ref, lam_ref)
        gel, dgel = _gelu(zg, with_grad=True)
        dy = dy_ref[...].astype(F32)
        dh = dy * gel
        dzg_ref[...] = (dy * h_ref[...] * dgel).astype(BF16)
        a_s[...] = a
        b_s[...] = a * dh
        _scan_bwd(a_s, b_s, ext_mu, muc, R // SUB)
        lam_t = dh + ext_mu[pl.ds(1, R), :]
        ext_mu[R:R + H, :] = ext_mu[0:H, :]
        da = lam_t * ext_h[pl.ds(H - 1, R), :]
        gxr = gx * xr
        dlog_a = da * a - (lam_t * gxr) * (a * a) / mult
        dgx = lam_t * mult * xr
        dxr = lam_t * mult * gx
        lam_v = lam_ref[...]
        dlam_ref[...] += jnp.sum(dlog_a * ga, axis=0, keepdims=True) * (C_RG * _sigmoid(-lam_v))
        dpa = (dlog_a * (-C_RG * sp)) * ga * (1.0 - ga)
        dpx = dgx * gx * (1.0 - gx)
        dbga_ref[...] += jnp.sum(dpa, axis=0, keepdims=True)
        dbgx_ref[...] += jnp.sum(dpx, axis=0, keepdims=True)
        dpab, dpxb = dpa.astype(BF16), dpx.astype(BF16)
        dwga_ref[0] += lax.dot_general(xrb, dpab, _TN, preferred_element_type=F32)
        dwgx_ref[0] += lax.dot_general(xrb, dpxb, _TN, preferred_element_type=F32)
        dxr = (dxr + lax.dot_general(dpab, wgab, _NT, preferred_element_type=F32)
               + lax.dot_general(dpxb, wgxb, _NT, preferred_element_type=F32))
        dcb_ref[...] += jnp.sum(dxr, axis=0, keepdims=True)
        ext_d[0:R, :] = dxr
        dzr = jnp.zeros((R, HD_A), F32)
        for k in range(CONV_A):
            sh = CONV_A - 1 - k
            dcw_ref[k:k + 1, :] += jnp.sum(dxr * ext_z[pl.ds(H - sh, R), :], axis=0, keepdims=True)
            dzr = dzr + cw_ref[k:k + 1, :] * ext_d[pl.ds(sh, R), :]
        dzr_ref[...] = dzr.astype(BF16)
        ext_d[R:R + H, :] = ext_d[0:H, :]

    vec_o = pl.BlockSpec((1, HD_A), lambda c, j: (0, c))
    mat_o = pl.BlockSpec((1, HD_A, HD_A), lambda c, j: (c, 0, 0))
    return pl.pallas_call(
        body,
        out_shape=[SDS((S, D), BF16), SDS((S, D), BF16), SDS((CONV_A, D), F32), SDS((1, D), F32),
                   SDS((nt, HD_A, HD_A), F32), SDS((1, D), F32), SDS((nt, HD_A, HD_A), F32), SDS((1, D), F32),
                   SDS((1, D), F32)],
        grid=(nt, nch),
        in_specs=[pl.BlockSpec((R, HD_A), rows), pl.BlockSpec((R, HD_A), rows), pl.BlockSpec((R, HD_A), rows_rec),
                  pl.BlockSpec((_HB, HD_A), halo_z), pl.BlockSpec((R, HD_A), rows),
                  pl.BlockSpec((H, HD_A), halo)] + _a_specs(),
        out_specs=[pl.BlockSpec((R, HD_A), rows), pl.BlockSpec((R, HD_A), rows),
                   pl.BlockSpec((CONV_A, HD_A), lambda c, j: (0, c)), vec_o, mat_o, vec_o, mat_o, vec_o, vec_o],
        scratch_shapes=[pltpu.VMEM((H + R, HD_A), F32), pltpu.VMEM((H + R, HD_A), F32), pltpu.VMEM((R + H, HD_A), F32),
                        pltpu.VMEM((R + H, HD_A), F32), pltpu.VMEM((R, HD_A), F32), pltpu.VMEM((R, HD_A), F32),
                        pltpu.VMEM((SUB, HD_A), F32)],
        name="rglru_bwd", compiler_params=_cp(2),
    )(dyab, zp, zp, zp, h, h, conv_w, conv_b, wgx, bgx, wga, bga, lam)


_POOL_H = 16
_POOL_T0 = 2 * D // HD_A
_POOL_Y0 = D // HD_A


def _window_sum(lv, n, lo, rows, g, ahead):
    base = 0 if ahead else SUB
    cur, win = lv[0], None
    for i, s in enumerate((1, 2, 4, 8)):
        val = cur[pl.ds(base, n), :] + cur[pl.ds(base + (s if ahead else -s), n), :]
        sel = val[lo:lo + rows]
        win = sel if win is None else jnp.where(g >= i, sel, win)
        if i < 3:
            lv[i + 1][pl.ds(base, n), :] = val
            cur = lv[i + 1]
    return win


def _pool_width(g):
    return jnp.where(g == 0, 2.0, jnp.where(g == 1, 4.0, jnp.where(g == 2, 8.0, 16.0)))


def _b_fwd(zp, yab, wg, bg, sc):
    S = zp.shape[0]
    R, H = R_SEQ, _POOL_H

    def body(z_ref, wg_ref, bg_ref, sc_ref, yab_in, yb_ref, *lv):
        del yab_in
        g, j = pl.program_id(0), pl.program_id(1)

        @pl.when(j == 0)
        def _():
            for r in lv:
                r[0:SUB, :] = jnp.zeros((SUB, HD_A), F32)
            lv[0][SUB:SUB + H, :] = jnp.zeros((H, HD_A), F32)

        u = z_ref[...].astype(F32)
        lv[0][SUB + H:SUB + H + R, :] = u
        t1 = (j * R + 1 + lax.broadcasted_iota(jnp.int32, (R, HD_A), 0)).astype(F32)
        p = _window_sum(lv, H + R, H, R, g, False) / jnp.minimum(t1, _pool_width(g)) - u
        lin = lax.dot_general(p.astype(BF16), wg_ref[0].astype(BF16), _NN, preferred_element_type=F32) + bg_ref[...]
        yb_ref[...] = (lin * sc_ref[...]).astype(BF16)
        lv[0][SUB:SUB + H, :] = lv[0][SUB + R:SUB + R + H, :]

    vec = pl.BlockSpec((1, HD_A), lambda g, j: (0, g))
    return pl.pallas_call(
        body, out_shape=SDS(yab.shape, yab.dtype), grid=(len(POOL_WINDOWS), S // R),
        in_specs=[pl.BlockSpec((R, HD_A), lambda g, j: (j, _POOL_T0 + g)),
                  pl.BlockSpec((1, HD_A, HD_A), lambda g, j: (g, 0, 0)), vec, vec, pl.BlockSpec(memory_space=pl.ANY)],
        out_specs=pl.BlockSpec((R, HD_A), lambda g, j: (j, _POOL_Y0 + g)),
        scratch_shapes=[pltpu.VMEM((SUB + H + R, HD_A), F32)] * 4, input_output_aliases={4: 0},
        name="pool_fwd", compiler_params=_cp(2),
    )(zp, wg, bg, sc, yab)


def _b_bwd(dyab, zp, wg, bg, sc):
    S = zp.shape[0]
    R, H, nch, ng = R_SEQ, _POOL_H, S // R_SEQ, len(POOL_WINDOWS)

    def body(dy_ref, z_ref, zh_ref, wg_ref, bg_ref, sc_ref, dz_ref, dwg_ref, dbg_ref, dsc_ref, *scratch):
        lu, lq = scratch[:4], scratch[4:]
        g, j = pl.program_id(0), pl.program_id(1)
        jj = nch - 1 - j

        @pl.when(j == 0)
        def _():
            for r in lu:
                r[0:SUB, :] = jnp.zeros((SUB, HD_A), F32)
            for r in lq:
                r[R + H:R + H + SUB, :] = jnp.zeros((SUB, HD_A), F32)
            lq[0][R:R + H, :] = jnp.zeros((H, HD_A), F32)
            for r in (dwg_ref, dbg_ref, dsc_ref):
                r[...] = jnp.zeros_like(r)

        u = z_ref[...].astype(F32)
        lu[0][SUB:SUB + H, :] = jnp.where(jj == 0, 0.0, zh_ref[...].astype(F32))
        lu[0][SUB + H:SUB + H + R, :] = u
        t1 = (jj * R + 1 + lax.broadcasted_iota(jnp.int32, (R, HD_A), 0)).astype(F32)
        cnt = jnp.minimum(t1, _pool_width(g))
        pb = (_window_sum(lu, H + R, H, R, g, False) / cnt - u).astype(BF16)
        wgb = wg_ref[0].astype(BF16)
        lin = lax.dot_general(pb, wgb, _NN, preferred_element_type=F32) + bg_ref[...]
        dy = dy_ref[...].astype(F32)
        dsc_ref[...] += jnp.sum(dy * lin, axis=0, keepdims=True)
        dlin = dy * sc_ref[...]
        dbg_ref[...] += jnp.sum(dlin, axis=0, keepdims=True)
        dlb = dlin.astype(BF16)
        dwg_ref[0] += lax.dot_general(pb, dlb, _TN, preferred_element_type=F32)
        dp = lax.dot_general(dlb, wgb, _NT, preferred_element_type=F32)
        lq[0][0:R, :] = dp / cnt
        dz_ref[...] = (_window_sum(lq, R + H, 0, R, g, True) - dp).astype(BF16)
        lq[0][R:R + H, :] = lq[0][0:H, :]

    vec = pl.BlockSpec((1, HD_A), lambda g, j: (0, g))
    mat = pl.BlockSpec((1, HD_A, HD_A), lambda g, j: (g, 0, 0))
    return pl.pallas_call(
        body, out_shape=[SDS((S, D // 2), BF16), SDS((ng, HD_A, HD_A), F32), SDS((1, D // 2), F32),
                         SDS((1, D // 2), F32)],
        grid=(ng, nch),
        in_specs=[pl.BlockSpec((R, HD_A), lambda g, j: (nch - 1 - j, _POOL_Y0 + g)),
                  pl.BlockSpec((R, HD_A), lambda g, j: (nch - 1 - j, _POOL_T0 + g)),
                  pl.BlockSpec((H, HD_A), lambda g, j: (jnp.maximum((nch - 1 - j) * (R // H) - 1, 0), _POOL_T0 + g)),
                  mat, vec, vec],
        out_specs=[pl.BlockSpec((R, HD_A), lambda g, j: (nch - 1 - j, g)), mat, vec, vec],
        scratch_shapes=[pltpu.VMEM((SUB + H + R, HD_A), F32)] * 8,
        name="pool_bwd", compiler_params=_cp(2),
    )(dyab, zp, zp, wg, bg, sc)


_CW_F = 768


def _f_fwd(hp, w, b, name):
    S = hp.shape[0]
    R, H, cw = min(S, R_FFN), SUB, _CW_F
    nlt = cw // LANE

    def body(h_ref, w_ref, b_ref, o_ref, gel_ref, ud_ref, ext):
        j = pl.program_id(1)

        @pl.when(j == 0)
        def _():
            ext[:, 0:H, :] = jnp.zeros((nlt, H, LANE), F32)

        def stage(r0, lt):
            ext[lt, pl.ds(pl.multiple_of(r0 + H, SUB), _RB), :] = h_ref[pl.ds(r0, _RB), _lanes(lt)].astype(F32)

        def main(r0, lt):
            ls = _lanes(lt)
            gp = b_ref[:, ls]
            for k in range(CONV_F):
                gp = gp + w_ref[k:k + 1, ls] * ext[lt, pl.ds(r0 + (H - (CONV_F - 1 - k)), _RB), :]
            up = h_ref[pl.ds(r0, _RB), _lanes(lt + nlt)].astype(F32)
            gel, dgel = _gelu(gp, with_grad=True)
            rs = pl.ds(r0, _RB)
            o_ref[rs, ls] = (gel * up).astype(BF16)
            gel_ref[rs, ls] = gel.astype(BF16)
            ud_ref[rs, ls] = (up * dgel).astype(BF16)

        _sub_blocks(R, cw, stage)
        _sub_blocks(R, cw, main)
        ext[:, 0:H, :] = ext[:, R:R + H, :]

    tile = pl.BlockSpec((R, cw), lambda c, j: (j, c))
    return pl.pallas_call(
        body, out_shape=[SDS((S, D_FF), BF16)] * 3, grid=(D_FF // cw, S // R),
        in_specs=[pl.BlockSpec((R, 2 * cw), lambda c, j: (j, c)), pl.BlockSpec((CONV_F, cw), lambda c, j: (0, c)),
                  pl.BlockSpec((1, cw), lambda c, j: (0, c))],
        out_specs=[tile] * 3,
        scratch_shapes=[pltpu.VMEM((nlt, H + R, LANE), F32)], name=name, compiler_params=_cp(2),
    )(hp, w, b)


def _f_bwd(dact, hp, gel, ud, w, name):
    S = hp.shape[0]
    R, H, cw = min(S, R_FFN), SUB, _CW_F
    nch = S // R
    nlt = cw // LANE

    def body(da_ref, h_ref, hh_ref, gel_ref, ud_ref, w_ref, dh_ref, dw_ref, db_ref, ext_g, ext_d, acc):
        j = pl.program_id(1)
        jj = nch - 1 - j

        @pl.when(j == 0)
        def _():
            ext_d[:, R:R + H, :] = jnp.zeros((nlt, H, LANE), F32)
            acc[...] = jnp.zeros_like(acc)

        for lt in range(nlt):
            ext_g[lt, 0:H, :] = jnp.where(jj == 0, 0.0, hh_ref[_HB - H:_HB, lt * LANE:(lt + 1) * LANE].astype(F32))

        def stage(r0, lt):
            ext_g[lt, pl.ds(pl.multiple_of(r0 + H, SUB), _RB), :] = h_ref[pl.ds(r0, _RB), _lanes(lt)].astype(F32)

        def first(r0, lt):
            ls, lu, rs = _lanes(lt), _lanes(lt + nlt), pl.ds(r0, _RB)
            da = da_ref[rs, ls].astype(F32)
            dh_ref[rs, lu] = (da * gel_ref[rs, ls].astype(F32)).astype(BF16)
            dgp = da * ud_ref[rs, ls].astype(F32)
            ext_d[lt, rs, :] = dgp
            acc[CONV_F * SUB:(CONV_F + 1) * SUB, ls] += _psum8(dgp)
            for k in range(CONV_F):
                tap = ext_g[lt, pl.ds(r0 + (H - (CONV_F - 1 - k)), _RB), :]
                acc[k * SUB:(k + 1) * SUB, ls] += _psum8(dgp * tap)

        def second(r0, lt):
            ls = _lanes(lt)
            dhg = w_ref[CONV_F - 1:CONV_F, ls] * ext_d[lt, pl.ds(r0, _RB), :]
            for k in range(CONV_F - 1):
                dhg = dhg + w_ref[k:k + 1, ls] * ext_d[lt, pl.ds(r0 + (CONV_F - 1 - k), _RB), :]
            dh_ref[pl.ds(r0, _RB), ls] = dhg.astype(BF16)

        _sub_blocks(R, cw, stage)
        _sub_blocks(R, cw, first)
        _sub_blocks(R, cw, second)
        ext_d[:, R:R + H, :] = ext_d[:, 0:H, :]

        @pl.when(j == nch - 1)
        def _():
            for k in range(CONV_F):
                dw_ref[k:k + 1, :] = jnp.sum(acc[k * SUB:(k + 1) * SUB, :], axis=0, keepdims=True)
            db_ref[...] = jnp.sum(acc[CONV_F * SUB:(CONV_F + 1) * SUB, :], axis=0, keepdims=True)

    rows = lambda c, j: (nch - 1 - j, c)
    return pl.pallas_call(
        body, out_shape=[SDS((S, 2 * D_FF), BF16), SDS((CONV_F, D_FF), F32), SDS((1, D_FF), F32)],
        grid=(D_FF // cw, nch),
        in_specs=[pl.BlockSpec((R, cw), rows), pl.BlockSpec((R, cw), lambda c, j: (nch - 1 - j, 2 * c)),
                  pl.BlockSpec((_HB, cw), lambda c, j: (jnp.maximum((nch - 1 - j) * (R // _HB) - 1, 0), 2 * c)),
                  pl.BlockSpec((R, cw), rows), pl.BlockSpec((R, cw), rows),
                  pl.BlockSpec((CONV_F, cw), lambda c, j: (0, c))],
        out_specs=[pl.BlockSpec((R, 2 * cw), rows), pl.BlockSpec((CONV_F, cw), lambda c, j: (0, c)),
                   pl.BlockSpec((1, cw), lambda c, j: (0, c))],
        scratch_shapes=[pltpu.VMEM((nlt, H + R, LANE), F32), pltpu.VMEM((nlt, R + H, LANE), F32),
                        pltpu.VMEM(((CONV_F + 1) * SUB, cw), F32)], name=name,
        compiler_params=_cp(2),
    )(dact, hp, hp, gel, ud, w)


_CW_C = 256
_H_C = 32


def _shifted_copies(ext, nlt, H, R, ahead):
    margin = H - SUB
    for lt in range(nlt):
        for b in range(1, SUB):
            if ahead:
                ext[b * nlt + lt, R:R + margin, :] = ext[lt, R + b:R + margin + b, :]
            else:
                ext[b * nlt + lt, SUB:H, :] = ext[lt, SUB - b:H - b, :]

    def copy(r0, lt):
        for b in range(1, SUB):
            if ahead:
                ext[b * nlt + lt, pl.ds(r0, _RB), :] = ext[lt, pl.ds(r0 + b, _RB), :]
            else:
                ext[b * nlt + lt, pl.ds(pl.multiple_of(r0 + H, SUB), _RB), :] = ext[lt, pl.ds(r0 + (H - b), _RB), :]

    _sub_blocks(R, nlt * LANE, copy)


def _c_fwd(h1p, w, b):
    S = h1p.shape[0]
    R, H, cw = R_SEQ, _H_C, _CW_C
    nlt = cw // LANE

    def body(h_ref, w_ref, b_ref, o_ref, ext):
        j = pl.program_id(1)

        @pl.when(j == 0)
        def _():
            ext[0:nlt, 0:H, :] = jnp.zeros((nlt, H, LANE), F32)

        def stage(r0, lt):
            rs = pl.ds(r0, _RB)
            gate = h_ref[rs, _lanes(lt + nlt)].astype(F32)
            ext[lt, pl.ds(pl.multiple_of(r0 + H, SUB), _RB), :] = h_ref[rs, _lanes(lt)].astype(F32) * _sigmoid(gate)

        def main(r0, lt):
            ls = _lanes(lt)
            cv = b_ref[:, ls]
            for k in range(CONV_C):
                a8, ph = divmod(CONV_C - 1 - k, SUB)
                rows = pl.ds(pl.multiple_of(r0 + (H - a8 * SUB), SUB), _RB)
                cv = cv + w_ref[k:k + 1, ls] * ext[ph * nlt + lt, rows, :]
            o_ref[pl.ds(r0, _RB), ls] = cv

        _sub_blocks(R, cw, stage)
        _shifted_copies(ext, nlt, H, R, False)
        _sub_blocks(R, cw, main)
        ext[0:nlt, 0:H, :] = ext[0:nlt, R:R + H, :]

    return pl.pallas_call(
        body, out_shape=SDS((S, D), F32), grid=(D // cw, S // R),
        in_specs=[pl.BlockSpec((R, 2 * cw), lambda c, j: (j, c)), pl.BlockSpec((CONV_C, cw), lambda c, j: (0, c)),
                  pl.BlockSpec((1, cw), lambda c, j: (0, c))],
        out_specs=pl.BlockSpec((R, cw), lambda c, j: (j, c)),
        scratch_shapes=[pltpu.VMEM((SUB * nlt, H + R, LANE), F32)], name="conf_conv_fwd", compiler_params=_cp(2),
    )(h1p, w, b)


def _c_bwd(dcv, h1p, w):
    S = h1p.shape[0]
    R, H, cw, nch = R_SEQ, _H_C, _CW_C, S // R_SEQ
    nlt = cw // LANE
    a_b, a_val, a_gate = CONV_C * SUB, (CONV_C + 1) * SUB, (CONV_C + 2) * SUB

    def body(dc_ref, h_ref, hh_ref, w_ref, dh_ref, dw_ref, db_ref, db1_ref, ext_u, ext_d, acc):
        j = pl.program_id(1)
        jj = nch - 1 - j

        @pl.when(j == 0)
        def _():
            ext_d[0:nlt, R:R + H, :] = jnp.zeros((nlt, H, LANE), F32)
            acc[...] = jnp.zeros_like(acc)

        for lt in range(nlt):
            ext_u[lt, 0:H, :] = jnp.where(
                jj == 0, 0.0, hh_ref[:, lt * LANE:(lt + 1) * LANE].astype(F32)
                * _sigmoid(hh_ref[:, cw + lt * LANE:cw + (lt + 1) * LANE].astype(F32)))

        def stage(r0, lt):
            rs, ls = pl.ds(r0, _RB), _lanes(lt)
            gate = h_ref[rs, _lanes(lt + nlt)].astype(F32)
            ext_u[lt, pl.ds(pl.multiple_of(r0 + H, SUB), _RB), :] = h_ref[rs, ls].astype(F32) * _sigmoid(gate)
            ext_d[lt, rs, :] = dc_ref[rs, ls]

        def first(r0, lt):
            ls = _lanes(lt)
            dc = dc_ref[pl.ds(r0, _RB), ls]
            acc[a_b:a_b + SUB, ls] += _psum8(dc)
            for k in range(CONV_C):
                a8, ph = divmod(CONV_C - 1 - k, SUB)
                tap = ext_u[ph * nlt + lt, pl.ds(pl.multiple_of(r0 + (H - a8 * SUB), SUB), _RB), :]
                acc[k * SUB:(k + 1) * SUB, ls] += _psum8(dc * tap)

        def second(r0, lt):
            rs, ls, lg = pl.ds(r0, _RB), _lanes(lt), _lanes(lt + nlt)
            du = w_ref[CONV_C - 1:CONV_C, ls] * ext_d[lt, rs, :]
            for k in range(CONV_C - 1):
                a8, ph = divmod(CONV_C - 1 - k, SUB)
                du = du + w_ref[k:k + 1, ls] * ext_d[ph * nlt + lt, pl.ds(pl.multiple_of(r0 + a8 * SUB, SUB), _RB), :]
            val = h_ref[rs, ls].astype(F32)
            sg = _sigmoid(h_ref[rs, lg].astype(F32))
            dval = du * sg
            dgate = du * val * sg * (1.0 - sg)
            acc[a_val:a_val + SUB, ls] += _psum8(dval)
            acc[a_gate:a_gate + SUB, ls] += _psum8(dgate)
            dh_ref[rs, ls] = dval.astype(BF16)
            dh_ref[rs, lg] = dgate.astype(BF16)

        _sub_blocks(R, cw, stage)
        _shifted_copies(ext_u, nlt, H, R, False)
        _shifted_copies(ext_d, nlt, H, R, True)
        _sub_blocks(R, cw, first)
        _sub_blocks(R, cw, second)
        ext_d[0:nlt, R:R + H, :] = ext_d[0:nlt, 0:H, :]

        @pl.when(j == nch - 1)
        def _():
            for k in range(CONV_C):
                dw_ref[k:k + 1, :] = jnp.sum(acc[k * SUB:(k + 1) * SUB, :], axis=0, keepdims=True)
            db_ref[...] = jnp.sum(acc[a_b:a_b + SUB, :], axis=0, keepdims=True)
            db1_ref[:, 0:cw] = jnp.sum(acc[a_val:a_val + SUB, :], axis=0, keepdims=True)
            db1_ref[:, cw:2 * cw] = jnp.sum(acc[a_gate:a_gate + SUB, :], axis=0, keepdims=True)

    rows = lambda c, j: (nch - 1 - j, c)
    return pl.pallas_call(
        body, out_shape=[SDS((S, 2 * D), BF16), SDS((CONV_C, D), F32), SDS((1, D), F32), SDS((1, 2 * D), F32)],
        grid=(D // cw, nch),
        in_specs=[pl.BlockSpec((R, cw), rows), pl.BlockSpec((R, 2 * cw), rows),
                  pl.BlockSpec((H, 2 * cw), lambda c, j: (jnp.maximum((nch - 1 - j) * (R // H) - 1, 0), c)),
                  pl.BlockSpec((CONV_C, cw), lambda c, j: (0, c))],
        out_specs=[pl.BlockSpec((R, 2 * cw), rows), pl.BlockSpec((CONV_C, cw), lambda c, j: (0, c)),
                   pl.BlockSpec((1, cw), lambda c, j: (0, c)), pl.BlockSpec((1, 2 * cw), lambda c, j: (0, c))],
        scratch_shapes=[pltpu.VMEM((SUB * nlt, H + R, LANE), F32), pltpu.VMEM((SUB * nlt, R + H, LANE), F32),
                        pltpu.VMEM(((CONV_C + 3) * SUB, cw), F32)], name="conf_conv_bwd",
        compiler_params=_cp(2),
    )(dcv, h1p, h1p, w)


def _local_step(x, mem, tgt, W, fetch=None, send=None):
    G = {}
    W = dict(W)

    def arrive(group, after):
        if fetch is None:
            return None
        got, tok = fetch(group, after)
        for key, val in got.items():
            W[key] = {**W.get(key, {}), **val} if isinstance(val, dict) else val
        return tok

    def gain(g, tok):
        return g if tok is None else g + tok

    def sent(group):
        return None if send is None else send(group, G)

    def xattn_fwd(xin, n, l):
        tok = arrive(("xa", l), n)
        mn = _rms_fwd(mem, gain(W["xa_mem_norm"][l:l + 1], tok), f"xa_memnorm_fwd{l}")
        q = _mm_nn(n, W["xa_wq"][l], out_dtype=BF16, name=f"xa_q{l}")
        k = _mm_nn(mn, W["xa_wk"][l], out_dtype=BF16, name=f"xa_k{l}")
        v = _mm_nn(mn, W["xa_wv"][l], out_dtype=BF16, name=f"xa_v{l}")
        o = _attn_fwd(q, k, v, f"xa_attn_fwd{l}")
        xout, nout = _mm_nn(o, W["xa_wo"][l], out_dtype=F32, name=f"xa_o{l}", add=xin, norm=W["f_norm"][l:l + 1])
        return xout, nout, (xin, n, q, mn, k, v, o)

    def xattn_bwd(dx, dxb, saved, l):
        xin, n, q, mn, k, v, o = saved
        do = _mm_nt(dxb, W["xa_wo"][l], out_dtype=BF16, name=f"xa_do{l}")
        G[f"xa_wo{l}"] = _mm_tn(o, dxb, out_dtype=BF16, name=f"xa_dwo{l}")
        dq, dk, dv = _attn_bwd(q, k, v, do, f"xa_attn_bwd{l}")
        dkb, dvb = dk.astype(BF16), dv.astype(BF16)
        G[f"xa_wq{l}"] = _mm_tn(n, dq, out_dtype=BF16, name=f"xa_dwq{l}")
        G[f"xa_wk{l}"] = _mm_tn(mn, dkb, out_dtype=BF16, name=f"xa_dwk{l}")
        G[f"xa_wv{l}"] = _mm_tn(mn, dvb, out_dtype=BF16, name=f"xa_dwv{l}")
        tok = sent(("xa", l))
        dmn = _mm_nt(dkb, W["xa_wk"][l], out_dtype=F32, name=f"xa_dmn_k{l}")
        dmn = _mm_nt(dvb, W["xa_wv"][l], out_dtype=F32, name=f"xa_dmn_v{l}", add=dmn)
        (G[f"xa_mem_norm{l}"],) = _rms_bwd(mem, W["xa_mem_norm"][l:l + 1], dmn, None, f"xa_memnorm_bwd{l}")
        dx, dxb, G[f"xa_norm{l}"] = _mm_nt(dq, W["xa_wq"][l], out_dtype=F32, name=f"xa_dn{l}",
                                           rms=(xin, gain(W["xa_norm"][l:l + 1], tok), dx))
        return dx, dxb

    def ffn_fwd(xin, n, l, next_gain):
        tok = arrive(("f", l), n)
        hp = _mm_nn(n, W["f_w_up"][l], out_dtype=BF16, name=f"f_up{l}")
        act, gel, ud = _f_fwd(hp, W["f_dw_w"][l], gain(W["f_dw_b"][l:l + 1], tok), f"f_conv_fwd{l}")
        res = _mm_nn(act, W["f_w_down"][l], out_dtype=F32, name=f"f_down{l}", add=xin, norm=next_gain)
        xout, nout = res if next_gain is not None else (res, None)
        return xout, nout, (xin, n, hp, act, gel, ud)

    def ffn_bwd(dx, dxb, saved, l):
        xin, n, hp, act, gel, ud = saved
        dact = _mm_nt(dxb, W["f_w_down"][l], out_dtype=BF16, name=f"f_dact{l}")
        G[f"f_w_down{l}"] = _mm_tn(act, dxb, out_dtype=BF16, name=f"f_dwdown{l}")
        dhp, G[f"f_dw_w{l}"], G[f"f_dw_b{l}"] = _f_bwd(dact, hp, gel, ud, W["f_dw_w"][l], f"f_conv_bwd{l}")
        G[f"f_w_up{l}"] = _mm_tn(n, dhp, out_dtype=BF16, name=f"f_dwup{l}", blocks=_CW_F)
        tok = sent(("f", l))
        dx, dxb, G[f"f_norm{l}"] = _mm_nt(dhp, W["f_w_up"][l], out_dtype=F32, name=f"f_dn{l}",
                                          rms=(xin, gain(W["f_norm"][l:l + 1], tok), dx))
        return dx, dxb

    n0 = _rms_fwd(x, W["ab_norm"], "ab_norm_fwd")
    tok = arrive(("ab", 0), n0)
    a_par = (W["a_conv_w"], gain(W["a_conv_b"], tok), W["a_gate_x_w"], W["a_gate_x_b"], W["a_gate_a_w"],
             W["a_gate_a_b"], W["a_lambda"])
    b_par = (W["b_group_w"], W["b_group_b"], W["b_scale"])
    zp = _mm_nn(n0, W["ab_w_in"], out_dtype=BF16, name="ab_in")
    yab, h_a = _a_fwd(zp, *a_par)
    yab = _b_fwd(zp, yab, *b_par)
    arrive(("ab", 1), yab)
    x1, n1 = _mm_nn(yab, W["ab_w_out"], out_dtype=F32, name="ab_out", add=x, norm=W["xa_norm"][0:1])
    x2, n2, s_xa0 = xattn_fwd(x1, n1, 0)
    x3, n3, s_f0 = ffn_fwd(x2, n2, 0, W["c_norm"])
    tok = arrive(("c", 0), n3)
    h1p = _mm_nn(n3, W["c_w_pw1"], out_dtype=BF16, name="c_pw1", bias=gain(W["c_b_pw1"], tok))
    cv = _c_fwd(h1p, W["c_dw_w"], W["c_dw_b"])
    sc = _ln_silu_fwd(cv, W["c_ln_g"], W["c_ln_b"])
    x4, n4 = _mm_nn(sc, W["c_w_pw2"], out_dtype=F32, name="c_pw2", bias=W["c_b_pw2"], add=x3, norm=W["xa_norm"][1:2])
    x5, n5, s_xa1 = xattn_fwd(x4, n4, 1)
    x6, _, s_f1 = ffn_fwd(x5, n5, 1, None)
    loss, dx, dxb, G["final_norm"] = _loss_head(x6, W["final_norm"], tgt)

    dx, dxb = ffn_bwd(dx, dxb, s_f1, 1)
    dx, dxb = xattn_bwd(dx, dxb, s_xa1, 1)
    dsc = _mm_nt(dxb, W["c_w_pw2"], out_dtype=BF16, name="c_dsc")
    G["c_w_pw2"] = _mm_tn(sc, dxb, out_dtype=BF16, name="c_dwpw2")
    dcv, G["c_ln_g"], G["c_ln_b"], G["c_b_pw2"] = _ln_silu_bwd(dsc, cv, W["c_ln_g"], W["c_ln_b"], dx)
    dh1p, G["c_dw_w"], G["c_dw_b"], G["c_b_pw1"] = _c_bwd(dcv, h1p, W["c_dw_w"])
    G["c_w_pw1"] = _mm_tn(n3, dh1p, out_dtype=BF16, name="c_dwpw1", blocks=_CW_C)
    tok = sent(("c", 0))
    dx, dxb, G["c_norm"] = _mm_nt(dh1p, W["c_w_pw1"], out_dtype=F32, name="c_dn",
                                  rms=(x3, gain(W["c_norm"], tok), dx))
    dx, dxb = ffn_bwd(dx, dxb, s_f0, 0)
    dx, dxb = xattn_bwd(dx, dxb, s_xa0, 0)
    dyab = _mm_nt(dxb, W["ab_w_out"], out_dtype=BF16, name="ab_dyab")
    G["ab_w_out"] = _mm_tn(yab, dxb, out_dtype=BF16, name="ab_dwout")
    tok = sent(("ab", 1))
    a_par = (a_par[0], gain(a_par[1], tok)) + a_par[2:]
    (dzg, dzr, G["a_conv_w"], G["a_conv_b"], G["a_gate_x_w"], G["a_gate_x_b"], G["a_gate_a_w"], G["a_gate_a_b"],
     G["a_lambda"]) = _a_bwd(dyab, zp, h_a, *a_par)
    dzq, G["b_group_w"], G["b_group_b"], G["b_scale"] = _b_bwd(dyab, zp, *b_par)
    G["ab_w_in"] = jnp.concatenate(
        [_mm_tn(n0, dz, out_dtype=BF16, name=f"ab_dwin_{part}")
         for part, dz in (("gate", dzg), ("rec", dzr), ("pool", dzq))], axis=1)
    tok = sent(("ab", 0))
    dx, _, G["ab_norm"] = _mm_nt_cols([dzg, dzr, dzq], W["ab_w_in"], name="ab_dn",
                                      rms=(x, gain(W["ab_norm"], tok), dx))
    return loss, dx, G


def _my_place():
    x, y, c = lax.axis_index("x"), lax.axis_index("y"), lax.axis_index("c")
    return x, y, c


def _all_gather(shards, name):
    n = len(shards)

    def body(*refs):
        ins, outs = refs[:n], refs[n:2 * n]
        send_sems, recv_sems, local_sems = refs[2 * n:]
        x, y, c = _my_place()
        me, sibling = (x, y, c), (x, y, 1 - c)
        chips = [(1 - x, y), (x, 1 - y), (1 - x, 1 - y)]

        def slab(a, place):
            px, py, pc = place
            return outs[a].at[4 * px + 2 * py + pc]

        def copy(a, k, block, to, src=None):
            return pltpu.make_async_remote_copy(
                src_ref=slab(a, block) if src is None else src, dst_ref=slab(a, block),
                send_sem=send_sems.at[a, k], recv_sem=recv_sems.at[a, k], device_id=to, device_id_type=MESH)

        mine = [pltpu.make_async_copy(ins[a], slab(a, me), local_sems.at[a]) for a in range(n)]
        for cp in mine:
            cp.start()
        first = []
        for j, chip in enumerate(chips):
            first += [copy(a, 1 + j, me, (*chip, c), src=ins[a]) for a in range(n)]
        first += [copy(a, 0, me, sibling, src=ins[a]) for a in range(n)]
        for cp in first:
            cp.start()
        passed = []
        for j, chip in enumerate(chips):
            for a in range(n):
                copy(a, 1 + j, (*chip, c), me).wait_recv()
                cp = copy(a, 4 + j, (*chip, c), sibling)
                cp.start()
                passed.append(cp)
        for a in range(n):
            copy(a, 0, sibling, me).wait_recv()
        for j, chip in enumerate(chips):
            for a in range(n):
                copy(a, 4 + j, (*chip, 1 - c), me).wait_recv()
        for cp in first + passed:
            cp.wait_send()
        for cp in mine:
            cp.wait()

    any_spec = pl.BlockSpec(memory_space=pl.ANY)
    return pl.pallas_call(
        body, out_shape=[SDS((N_DEV,) + s.shape, s.dtype) for s in shards], in_specs=[any_spec] * n,
        out_specs=[any_spec] * n,
        scratch_shapes=[pltpu.SemaphoreType.DMA((n, 7)), pltpu.SemaphoreType.DMA((n, 7)), pltpu.SemaphoreType.DMA((n,))],
        name=name,
    )(*shards)


_HBM = pl.BlockSpec(memory_space=pltpu.HBM)
_SEM = pl.BlockSpec(memory_space=pltpu.SEMAPHORE)
_EFFECT = pltpu.SideEffectType.DATAFLOW_SIDE_EFFECTING


def _peer_places():
    x, y, c = _my_place()
    peers = []
    for k in range(1, N_DEV):
        px = 1 - x if (k >> 2) & 1 else x
        py = 1 - y if (k >> 1) & 1 else y
        pc = 1 - c if k & 1 else c
        peers.append(((px, py, pc), 4 * px + 2 * py + pc))
    return (x, y, c), 4 * x + 2 * y + c, peers


def _send_start(srcs, per_dest, name):
    n = len(srcs)
    lands = [lax.empty((N_DEV,) + (s.shape[1:] if per_dest else s.shape), s.dtype) for s in srcs]

    def body(*refs):
        src, land = refs[:n], refs[n:2 * n]
        outs = refs[2 * n:]
        send, recv, token = outs[:n], outs[n:2 * n], outs[4 * n]
        _, me, peers = _peer_places()
        for a in range(n):
            for peer, pidx in peers:
                pltpu.make_async_remote_copy(
                    src_ref=src[a].at[pidx] if per_dest else src[a], dst_ref=land[a].at[me], send_sem=send[a],
                    recv_sem=recv[a], device_id=peer, device_id_type=MESH).start()
        token[...] = jnp.zeros_like(token)

    hbm = lambda a: pltpu.HBM(a.shape, a.dtype)
    sem = pltpu.SemaphoreType.DMA(())
    res = pl.pallas_call(
        body, name=name,
        out_shape=tuple([sem] * (2 * n) + [hbm(s) for s in srcs] + [hbm(l) for l in lands]
                        + [SDS((SUB, LANE), F32)]),
        in_specs=[_HBM] * (2 * n),
        out_specs=tuple([_SEM] * (2 * n) + [_HBM] * (2 * n) + [pl.BlockSpec(memory_space=pltpu.VMEM)]),
        input_output_aliases={i: 2 * n + i for i in range(2 * n)},
        compiler_params=pltpu.CompilerParams(has_side_effects=_EFFECT),
    )(*[pltpu.with_memory_space_constraint(s, pltpu.HBM) for s in srcs],
      *[pltpu.with_memory_space_constraint(l, pltpu.HBM) for l in lands])
    return res[:n], res[n:2 * n], res[2 * n:3 * n], res[3 * n:4 * n], res[4 * n]


def _send_wait(send, recv, srcs, lands, after, per_dest, name):
    n = len(srcs)

    def body(*refs):
        src, land = refs[:n], refs[n:2 * n]
        send_s, recv_s = refs[2 * n:3 * n], refs[3 * n:4 * n]
        token = refs[-1]
        place, _, _ = _peer_places()
        for a in range(n):
            seven = land[a].at[pl.ds(0, N_DEV - 1)]
            copy = pltpu.make_async_remote_copy(
                src_ref=src[a].at[pl.ds(0, N_DEV - 1)] if per_dest else seven, dst_ref=seven, send_sem=send_s[a],
                recv_sem=recv_s[a], device_id=place, device_id_type=MESH)
            copy.wait_send()
            copy.wait_recv()
        token[...] = jnp.zeros_like(token)

    hbm = lambda a: pltpu.HBM(a.shape, a.dtype)
    res = pl.pallas_call(
        body, name=name,
        out_shape=tuple([hbm(s) for s in srcs] + [hbm(l) for l in lands] + [SDS((SUB, LANE), F32)]),
        in_specs=[_HBM] * (2 * n) + [_SEM] * (2 * n) + [pl.BlockSpec(memory_space=pl.ANY)],
        out_specs=tuple([_HBM] * (2 * n) + [pl.BlockSpec(memory_space=pltpu.VMEM)]),
        input_output_aliases={i: i for i in range(2 * n)},
        compiler_params=pltpu.CompilerParams(has_side_effects=_EFFECT),
    )(*srcs, *lands, *send, *recv, after)
    return res[:n], res[n:2 * n], res[2 * n]


def _adamw_math(w, g, m, v):
    m = ADAM_B1 * m + (1.0 - ADAM_B1) * g
    v = ADAM_B2 * v + (1.0 - ADAM_B2) * (g * g)
    m_hat = m / (1.0 - ADAM_B1 ** ADAM_STEP)
    v_hat = v / (1.0 - ADAM_B2 ** ADAM_STEP)
    delta = -ADAM_LR * (m_hat / (jnp.sqrt(v_hat) + ADAM_EPS) + ADAM_WD * w)
    return delta, m, v


def _row_tile(r, c, itemsize_rows):
    cap = max(SUB, (itemsize_rows // (4 * c)) // SUB * SUB)
    if r <= cap:
        return r
    best = None
    for t in range(SUB, cap + 1, SUB):
        if r % t == 0:
            best = t
    return best if best is not None else r


def _sum_adamw(landing, w, m, v, name, layer=0, prev=None):
    _, r, c = landing.shape
    tr = _row_tile(r, c, 2 << 20)
    off = layer * (r // tr)

    def body(l_ref, w_ref, m_ref, v_ref, *rest):
        g_ref, d_ref, mo_ref, vo_ref = rest[-4:]
        g = l_ref[0].astype(F32)
        for s in range(1, N_DEV):
            g = g + l_ref[s].astype(F32)
        g_ref[...] = g
        d_ref[...], mo_ref[...], vo_ref[...] = _adamw_math(w_ref[...], g, m_ref[...], v_ref[...])

    blk = pl.BlockSpec((tr, c), lambda i: (i + off, 0))
    n_prev = 0 if prev is None else 4
    return pl.pallas_call(
        body, out_shape=[SDS(w.shape, F32)] * 4, grid=(r // tr,),
        in_specs=[pl.BlockSpec((N_DEV, tr, c), lambda i: (0, i, 0)), blk, blk, blk]
        + [pl.BlockSpec(memory_space=pl.ANY)] * n_prev,
        out_specs=[blk] * 4, input_output_aliases={4 + i: i for i in range(n_prev)}, name=name,
        compiler_params=_cp(1),
    )(landing, w, m, v, *([] if prev is None else prev))


def _sum8(landing, name):
    _, r, c = landing.shape

    def body(l_ref, g_ref):
        g = l_ref[0]
        for s in range(1, N_DEV):
            g = g + l_ref[s]
        g_ref[...] = g

    return pl.pallas_call(body, out_shape=SDS((r, c), F32), name=name, compiler_params=_cp(0))(landing)


def _adamw_small(repl_pack, own_pack, P, M, V):
    table, off = [], 0
    for name, shape in _REPL.items():
        table.append((name, shape if len(shape) > 1 else (1,) + shape, 0, off // LANE))
        off += _size(shape)
    off = _REPL_ROWS * LANE
    for name, shape in _SMALL_SHARDED.items():
        table.append((name, shape, 1, off // LANE))
        off += _size(shape)
    n = len(table)

    def body(*refs):
        packs, ins, outs = refs[:2], refs[2:2 + 3 * n], refs[2 + 3 * n:]
        for p, (_, shape, which, r0) in enumerate(table):
            w_ref, m_ref, v_ref = ins[3 * p:3 * p + 3]
            g_ref, d_ref, mo_ref, vo_ref = outs[4 * p:4 * p + 4]
            pack, rows, q = packs[which], shape[-2], shape[-1] // LANE
            lead = [()]
            for dim in shape[:-2]:
                lead = [t + (i,) for t in lead for i in range(dim)]
            for li, idx in enumerate(lead):
                if q == 1:
                    dst = g_ref.at[idx] if idx else g_ref
                    dst[...] = pack[r0 + li * rows:r0 + (li + 1) * rows, :]
                    continue
                for i in range(rows):
                    for k in range(q):
                        row = r0 + (li * rows + i) * q + k
                        g_ref[idx + (slice(i, i + 1), slice(k * LANE, (k + 1) * LANE))] = pack[row:row + 1, :]
            d_ref[...], mo_ref[...], vo_ref[...] = _adamw_math(w_ref[...], g_ref[...], m_ref[...], v_ref[...])

    ins, out_shape = [], []
    for name, shape, _, _ in table:
        ins += [t[name].reshape(shape) for t in (P, M, V)]
        out_shape += [SDS(shape, F32)] * 4
    res = pl.pallas_call(body, out_shape=out_shape, name="adamw_small", compiler_params=_cp(0))(
        repl_pack, own_pack, *ins)
    dicts = ({}, {}, {}, {})
    for p, (name, shape, _, _) in enumerate(table):
        for d, arr in zip(dicts, res[4 * p:4 * p + 4]):
            d[name] = arr.reshape(P[name].shape)
    return dicts


_BIG = {
    "ab_w_in": (1, D, 320), "ab_w_out": (1, 192, D), "c_w_pw1": (1, D, 256), "c_w_pw2": (1, 128, D),
    "xa_wq": (2, 128, D), "xa_wk": (2, 128, D), "xa_wv": (2, 128, D), "xa_wo": (2, 128, D),
    "f_w_up": (2, D, 768), "f_w_down": (2, 384, D),
}
_SMALL_SHARDED = {
    "a_conv_w": (1, 4, 128), "c_norm": (1, 128), "c_b_pw1": (1, 256), "c_dw_w": (1, 31, 128), "c_dw_b": (1, 128),
    "c_ln_g": (1, 128), "c_ln_b": (1, 128), "c_b_pw2": (1, 128), "f_dw_w": (2, 3, 384),
}
_REPL = {
    "ab_norm": (1, D), "a_conv_b": (1, D), "a_gate_x_w": (1, 8, 128, 128), "a_gate_x_b": (1, D),
    "a_gate_a_w": (1, 8, 128, 128), "a_gate_a_b": (1, D), "a_lambda": (1, D), "b_group_w": (1, 4, 128, 128),
    "b_group_b": (1, 512), "b_scale": (1, 512), "xa_norm": (2, D), "xa_mem_norm": (2, D), "f_norm": (2, D),
    "f_dw_b": (2, D_FF), "final_norm": (D,),
}


def _size(shape):
    n = 1
    for s in shape:
        n *= s
    return n


_N_SS = sum(_size(s) for s in _SMALL_SHARDED.values())
_N_REPL = sum(_size(s) for s in _REPL.values())
_REPL_ROWS = -(-_N_REPL // (N_DEV * SUB * LANE)) * SUB
_SS_ROWS = _N_SS // LANE
_SMALL_ROWS = -(-(_REPL_ROWS + _SS_ROWS) // SUB) * SUB


def _pack(parts, rows):
    flat = jnp.concatenate([p.reshape(-1).astype(F32) for p in parts])
    return jnp.pad(flat, (0, rows * LANE - flat.shape[0])).reshape(rows, LANE)


def _pair_blocks(v, bw):
    lead, n = v.shape[:-1], v.shape[-1]
    return jnp.swapaxes(v.reshape(lead + (2, n // (2 * bw), bw)), -3, -2).reshape(lead + (n,))


def _unpair_blocks(v, bw):
    lead, n = v.shape[:-1], v.shape[-1]
    return jnp.swapaxes(v.reshape(lead + (n // (2 * bw), 2, bw)), -3, -2).reshape(lead + (n,))


_GROUPS = {
    ("ab", 0): (("ab_w_in", 0),),
    ("ab", 1): (("ab_w_out", 0),),
    ("xa", 0): (("xa_wq", 0), ("xa_wk", 0), ("xa_wv", 0), ("xa_wo", 0)),
    ("f", 0): (("f_w_up", 0), ("f_w_down", 0)),
    ("c", 0): (("c_w_pw1", 0), ("c_w_pw2", 0)),
    ("xa", 1): (("xa_wq", 1), ("xa_wk", 1), ("xa_wv", 1), ("xa_wo", 1)),
    ("f", 1): (("f_w_up", 1), ("f_w_down", 1)),
}
_SEND_GROUPS = _GROUPS


def _weight_layout(name, g):
    if name == "ab_w_in":
        return jnp.swapaxes(g, 0, 1).reshape(D, N_DEV * 320)
    if name in ("c_w_pw1", "f_w_up"):
        return g
    return g.reshape(N_DEV * g.shape[1], D)


def _grad_blocks(name, l, G):
    _, r, c = _BIG[name]
    if name == "ab_w_in":
        return jnp.swapaxes(G[name].reshape(D, N_DEV, 320), 0, 1)
    if name == "c_w_pw1":
        return G[name]
    if name == "f_w_up":
        return G[f"{name}{l}"]
    return (G[name] if _BIG[name][0] == 1 else G[f"{name}{l}"]).reshape(N_DEV, r, c)


def _small_layouts(sm):
    W = {}
    sm = sm.reshape(N_DEV, -1)
    off = 0
    for name, shape in _SMALL_SHARDED.items():
        n = _size(shape)
        blocks = sm[:, off:off + n].reshape((N_DEV,) + shape)
        off += n
        W[name] = jnp.moveaxis(blocks, 0, -2).reshape(shape[:-1] + (N_DEV * shape[-1],))
    W["a_conv_w"], W["c_dw_w"] = W["a_conv_w"][0], W["c_dw_w"][0]
    W["c_b_pw1"] = _pair_blocks(W["c_b_pw1"], _CW_C)
    return W


def _with_own(land, src, me, per_dest):
    own = lax.dynamic_slice_in_dim(src, me, 1, 0) if per_dest else src[None]
    return lax.dynamic_update_slice_in_dim(land, own, me, 0)


def _to_dest_major(g, shape):
    full = g.reshape(shape[:-1] + (N_DEV, shape[-1]))
    return jnp.moveaxis(full, -2, 0).reshape(N_DEV, -1)


def kernel(x, mem, ab_norm, ab_w_in, a_conv_w, a_conv_b, a_gate_x_w, a_gate_x_b, a_gate_a_w, a_gate_a_b, a_lambda, b_group_w, b_group_b, b_scale, ab_w_out, c_norm, c_w_pw1, c_b_pw1, c_dw_w, c_dw_b, c_ln_g, c_ln_b, c_w_pw2, c_b_pw2, xa_norm, xa_mem_norm, xa_wq, xa_wk, xa_wv, xa_wo, f_norm, f_w_up, f_dw_w, f_dw_b, f_w_down, final_norm, loss_target, m_ab_norm, m_ab_w_in, m_a_conv_w, m_a_conv_b, m_a_gate_x_w, m_a_gate_x_b, m_a_gate_a_w, m_a_gate_a_b, m_a_lambda, m_b_group_w, m_b_group_b, m_b_scale, m_ab_w_out, m_c_norm, m_c_w_pw1, m_c_b_pw1, m_c_dw_w, m_c_dw_b, m_c_ln_g, m_c_ln_b, m_c_w_pw2, m_c_b_pw2, m_xa_norm, m_xa_mem_norm, m_xa_wq, m_xa_wk, m_xa_wv, m_xa_wo, m_f_norm, m_f_w_up, m_f_dw_w, m_f_dw_b, m_f_w_down, m_final_norm, v_ab_norm, v_ab_w_in, v_a_conv_w, v_a_conv_b, v_a_gate_x_w, v_a_gate_x_b, v_a_gate_a_w, v_a_gate_a_b, v_a_lambda, v_b_group_w, v_b_group_b, v_b_scale, v_ab_w_out, v_c_norm, v_c_w_pw1, v_c_b_pw1, v_c_dw_w, v_c_dw_b, v_c_ln_g, v_c_ln_b, v_c_w_pw2, v_c_b_pw2, v_xa_norm, v_xa_mem_norm, v_xa_wq, v_xa_wk, v_xa_wv, v_xa_wo, v_f_norm, v_f_w_up, v_f_dw_w, v_f_dw_b, v_f_w_down, v_final_norm):
    args = dict(locals())
    P = {n: args[n] for n in _NAMES}
    M = {n: args["m_" + n] for n in _NAMES}
    V = {n: args["v_" + n] for n in _NAMES}

    me = 4 * lax.axis_index("x") + 2 * lax.axis_index("y") + lax.axis_index("c")

    in_flight = {}

    def launch(groups, tok):
        shards, n_of = [], {}
        for grp in groups:
            for name, l in _GROUPS[grp]:
                w = P[name][l] if tok is None else P[name][l] + tok
                shards.append(w.astype(BF16))
            if grp == ("ab", 0):
                shards.append(_pack([P[n] for n in _SMALL_SHARDED], _SS_ROWS + 4))
            n_of[grp] = len(shards)
        res = _send_start(shards, False, "gather_start_" + "_".join(g[0] + str(g[1]) for g in groups))
        lo = 0
        for grp in groups:
            in_flight[grp] = [r[lo:n_of[grp]] for r in res[:4]]
            lo = n_of[grp]
        return res[4][:1, :1]

    follow = {("ab", 0): [("ab", 1), ("xa", 0), ("f", 0)], ("xa", 0): [("c", 0)], ("f", 0): [("xa", 1)],
              ("c", 0): [("f", 1)]}

    def fetch(grp, after):
        send_s, recv_s, srcs, lands = in_flight.pop(grp)
        srcs, lands, tok = _send_wait(send_s, recv_s, srcs, lands, after, False, f"gather_wait_{grp[0]}{grp[1]}")
        tok = launch(follow[grp], tok[:1, :1]) if grp in follow else None
        full = [_with_own(land, src, me, False) for land, src in zip(lands, srcs)]
        out = {}
        for (name, l), g in zip(_GROUPS[grp], full):
            w = _weight_layout(name, g)
            if _BIG[name][0] == 1:
                out[name] = w
            else:
                out[name] = {l: w}
        if grp == ("ab", 0):
            out.update(_small_layouts(full[-1]))
        return out, tok

    zero = launch([("ab", 0)], None)

    pending = []

    def send(grp, G):
        members = _SEND_GROUPS[grp]
        res = _send_start([_grad_blocks(name, l, G) for name, l in members], True, f"send_{grp[0]}{grp[1]}")
        pending.append((members, res))
        return res[4][:1, :1]

    W = {n: P[n] for n in _REPL}
    W["ab_norm"] = P["ab_norm"] + zero
    W["final_norm"] = P["final_norm"].reshape(1, D)
    W["a_gate_x_w"], W["a_gate_a_w"], W["b_group_w"] = P["a_gate_x_w"][0], P["a_gate_a_w"][0], P["b_group_w"][0]
    loss, grad_x, G = _local_step(x[0], mem[0], loss_target[0], W, fetch, send)
    loss = lax.psum(loss[0, 0], ("x", "y", "c"))

    Gs = dict(G)
    Gs["c_b_pw1"] = _unpair_blocks(G["c_b_pw1"], _CW_C)
    Gs["f_dw_w"] = jnp.stack([G["f_dw_w0"], G["f_dw_w1"]])
    Gs["a_conv_w"], Gs["c_dw_w"] = G["a_conv_w"][None], G["c_dw_w"][None]
    for n in ("xa_norm", "xa_mem_norm", "f_norm", "f_dw_b"):
        Gs[n] = jnp.concatenate([G[f"{n}0"], G[f"{n}1"]], axis=0)
    for n in ("a_gate_x_w", "a_gate_a_w", "b_group_w"):
        Gs[n] = G[n][None]
    repl_flat = jnp.concatenate([Gs[n].reshape(-1) for n in _REPL])
    repl_rows = jnp.pad(repl_flat, (0, N_DEV * _REPL_ROWS * LANE - _N_REPL)).reshape(N_DEV, _REPL_ROWS, LANE)
    ss_rows = jnp.concatenate([_to_dest_major(Gs[n], s) for n, s in _SMALL_SHARDED.items()], axis=1)
    ss_rows = ss_rows.reshape(N_DEV, _SS_ROWS, LANE)
    small_pack = jnp.concatenate(
        [repl_rows, ss_rows, jnp.zeros((N_DEV, _SMALL_ROWS - _REPL_ROWS - _SS_ROWS, LANE), F32)], axis=1)
    last = _send_start([small_pack], True, "send_small")
    pending.append(((("small", 0),), last))

    members = [m for mem_, _ in pending for m in mem_]
    cat = [[a for _, res in pending for a in res[i]] for i in range(4)]
    srcs, lands, _ = _send_wait(cat[0], cat[1], cat[2], cat[3], grad_x, True, "send_wait")
    landed = {m: _with_own(land, src, me, True) for m, land, src in zip(members, lands, srcs)}

    out_g, out_d, out_m, out_v = {}, {}, {}, {}
    for name, (layers, r, c) in _BIG.items():
        shape = P[name].shape
        w2, m2, v2 = [t[name].reshape(layers * r, c) for t in (P, M, V)]
        res = None
        for l in range(layers):
            res = _sum_adamw(landed[(name, l)], w2, m2, v2, f"adamw_{name}{l}", layer=l, prev=res)
        out_g[name], out_d[name], out_m[name], out_v[name] = [t.reshape(shape) for t in res]

    small_sum = _sum8(landed[("small", 0)], "sum_small")
    (repl_all,) = _all_gather([small_sum[:_REPL_ROWS]], "gather_small_grads")
    for out, got in zip((out_g, out_d, out_m, out_v),
                        _adamw_small(repl_all.reshape(N_DEV * _REPL_ROWS, LANE), small_sum, P, M, V)):
        out.update(got)

    return (loss, grad_x[None], *[out_g[n] for n in _NAMES], *[out_d[n] for n in _NAMES],
            *[out_m[n] for n in _NAMES], *[out_v[n] for n in _NAMES])


_NAMES = ("ab_norm", "ab_w_in", "a_conv_w", "a_conv_b", "a_gate_x_w", "a_gate_x_b", "a_gate_a_w", "a_gate_a_b",
          "a_lambda", "b_group_w", "b_group_b", "b_scale", "ab_w_out", "c_norm", "c_w_pw1", "c_b_pw1", "c_dw_w",
          "c_dw_b", "c_ln_g", "c_ln_b", "c_w_pw2", "c_b_pw2", "xa_norm", "xa_mem_norm", "xa_wq", "xa_wk", "xa_wv",
          "xa_wo", "f_norm", "f_w_up", "f_dw_w", "f_dw_b", "f_w_down", "final_norm")
```

```python
import functools

import jax
import jax.numpy as jnp
from jax import lax
from jax.experimental import pallas as pl
from jax.experimental.pallas import tpu as pltpu

F32, BF16 = jnp.float32, jnp.bfloat16
SDS = jax.ShapeDtypeStruct
MESH = pl.DeviceIdType.MESH

N_DEV = 8
D = 1024
N_MEM = 256
XA_HEADS, XA_HD = 4, 256
HD_A = 128
CONV_A, CONV_C, CONV_F = 4, 31, 3
C_RG = 8.0
POOL_WINDOWS = (2, 4, 8, 16)
D_FF = 3 * D
EPS = 1e-6
ADAM_LR, ADAM_B1, ADAM_B2, ADAM_EPS, ADAM_WD, ADAM_STEP = 0.001, 0.9, 0.999, 1e-08, 0.01, 10

LANE = 128
SUB = 8
VMEM_LIMIT = 56 * 1024 * 1024
R_SEQ = 512
R_RGLRU = 256
R_FFN = 1024
TM_ROW = 512


def _cp(n_axes):
    return pltpu.CompilerParams(dimension_semantics=("arbitrary",) * n_axes, vmem_limit_bytes=VMEM_LIMIT)


def _tile(n, pref):
    if n <= pref:
        return n
    best = None
    for t in range(LANE, pref + 1, LANE):
        if n % t == 0:
            best = t
    assert best is not None, (n, pref)
    return best


def _perm2(n):
    return (n % 2) * 4 + n // 2


_NN = (((1,), (0,)), ((), ()))
_NT = (((1,), (1,)), ((), ()))
_TN = (((0,), (0,)), ((), ()))


def _mm_call(name, grid, ab, ab_specs, dims, acc_shape, extras, outs, finish, from_ref=False):
    nk = grid[2]
    n_ab, n_ex, n_out = len(ab), len(extras), len(outs)
    use_acc = nk > 1 or from_ref

    def product(refs):
        r = lax.dot_general(refs[0][...], refs[1][...], dims, preferred_element_type=F32)
        for i in range(1, n_ab):
            r = r + lax.dot_general(refs[2 * i][...], refs[2 * i + 1][...], dims, preferred_element_type=F32)
        return r

    def body(*refs):
        rest = refs[2 * n_ab:]
        ex_refs, o_refs = rest[:n_ex], rest[n_ex:n_ex + n_out]
        first_rows = pl.program_id(0) == 0
        if not use_acc:
            finish(product(refs), ex_refs, o_refs, first_rows)
            return
        acc = rest[n_ex + n_out]
        if nk == 1:
            acc[...] = product(refs)
            finish(acc, ex_refs, o_refs, first_rows)
            return
        k = pl.program_id(2)

        @pl.when(k == 0)
        def _():
            acc[...] = jnp.zeros_like(acc)

        acc[...] += product(refs)

        @pl.when(k == nk - 1)
        def _():
            finish(acc if from_ref else acc[...], ex_refs, o_refs, first_rows)

    res = pl.pallas_call(
        body, out_shape=[o for o, _ in outs], grid=grid,
        in_specs=list(ab_specs) + [s for _, s in extras], out_specs=[s for _, s in outs],
        scratch_shapes=[pltpu.VMEM(acc_shape, F32)] if use_acc else [], name=name, compiler_params=_cp(3),
    )(*[t for pair in ab for t in pair], *[e for e, _ in extras])
    return res[0] if n_out == 1 else res


def _finish_sum(r, ex_refs, o_refs, first_rows):
    del first_rows
    for e in ex_refs:
        r = r + e[...]
    o_refs[0][...] = r.astype(o_refs[0].dtype)


def _finish_sum_norm(r, ex_refs, o_refs, first_rows):
    del first_rows
    for e in ex_refs[:-1]:
        r = r + e[...]
    o_refs[0][...] = r
    o_refs[1][...] = ((r * lax.rsqrt(jnp.mean(r * r, axis=-1, keepdims=True) + EPS)) * ex_refs[-1][...]).astype(BF16)


_EPI_ROWS = 16


def _finish_rms_bwd(r_ref, ex_refs, o_refs, first_rows):
    x_ref, g_ref, dres_ref = ex_refs
    dx_ref, dxb_ref, dg_ref = o_refs

    @pl.when(first_rows)
    def _():
        dg_ref[...] = jnp.zeros_like(dg_ref)

    gv = g_ref[...]
    inv_d = 1.0 / r_ref.shape[1]

    def step(i, dg_acc):
        groups = [pl.ds(pl.multiple_of(i * (2 * _EPI_ROWS) + u * _EPI_ROWS, _EPI_ROWS), _EPI_ROWS) for u in range(2)]
        sums = []
        for rows in groups:
            r, xf = r_ref[rows, :], x_ref[rows, :]
            sums.append((jnp.sum(xf * xf, axis=-1, keepdims=True), jnp.sum((r * gv) * xf, axis=-1, keepdims=True)))
        for rows, (sxx, sax) in zip(groups, sums):
            r, xf = r_ref[rows, :], x_ref[rows, :]
            rs = lax.rsqrt(sxx * inv_d + EPS)
            dg_acc = dg_acc + _psum8(r * (xf * rs))
            dx = rs * (r * gv) - xf * (rs * rs * (sax * rs * inv_d)) + dres_ref[rows, :]
            dx_ref[rows, :] = dx
            dxb_ref[rows, :] = dx.astype(BF16)
        return dg_acc

    dg_acc = lax.fori_loop(0, r_ref.shape[0] // (2 * _EPI_ROWS), step, jnp.zeros((SUB, r_ref.shape[1]), F32))
    dg_ref[...] += jnp.sum(dg_acc, axis=0, keepdims=True)


def _rms_bwd_io(M, tm, x, g, dres):
    rows = pl.BlockSpec((tm, D), lambda m, n, k: (m, 0))
    vec = pl.BlockSpec((1, D), lambda m, n, k: (0, 0))
    return ([(x, rows), (g, vec), (dres, rows)],
            [(SDS((M, D), F32), rows), (SDS((M, D), BF16), rows), (SDS((1, D), F32), vec)])


_K_WHOLE = 3072


def _mm_nn(a, b, *, out_dtype, name, bias=None, add=None, norm=None):
    M, K = a.shape
    tk = K if K <= _K_WHOLE else _tile(K, 1024)
    if K <= 1024 and norm is None:
        tm = _tile(M, 2048 if add is None and out_dtype == BF16 else 1024)
    else:
        tm = _tile(M, 512)
    if b.ndim == 3:
        nb, _, bw = b.shape
        N, tn, nn = nb * bw, bw, nb
        b_spec = pl.BlockSpec((None, tk, bw), lambda m, n, k: (_perm2(n), k, 0))
    else:
        N = b.shape[1]
        tn = _tile(N, 1024)
        nn = N // tn
        b_spec = pl.BlockSpec((tk, tn), lambda m, n, k: (k, n))
    tile = pl.BlockSpec((tm, tn), lambda m, n, k: (m, n))
    vec = pl.BlockSpec((1, tn), lambda m, n, k: (0, n))
    extras = ([] if bias is None else [(bias, vec)]) + ([] if add is None else [(add, tile)])
    outs, finish = [(SDS((M, N), out_dtype), tile)], _finish_sum
    if norm is not None:
        assert tn == N == D and out_dtype == F32
        extras.append((norm, vec))
        outs, finish = outs + [(SDS((M, N), BF16), tile)], _finish_sum_norm
    return _mm_call(name, (M // tm, nn, K // tk), [(a, b)], [pl.BlockSpec((tm, tk), lambda m, n, k: (m, k)), b_spec],
                    _NN, (tm, tn), extras, outs, finish)


def _mm_nt(a, b, *, out_dtype, name, add=None, rms=None):
    M, N = a.shape
    if b.ndim == 3:
        nb, Ko, bw = b.shape
        tm = _tile(M, 1024)
        tn, tk, nk = _tile(Ko, 1024), bw, nb
        b_spec = pl.BlockSpec((None, tn, bw), lambda m, n, k: (_perm2(k), n, 0))
    else:
        Ko = b.shape[0]
        tk = N if N <= _K_WHOLE else _tile(N, 1024)
        tm = _tile(M, 1024 if N <= 1024 and rms is None else 512)
        tn = _tile(Ko, 1024)
        nk = N // tk
        b_spec = pl.BlockSpec((tn, tk), lambda m, n, k: (n, k))
    tile = pl.BlockSpec((tm, tn), lambda m, n, k: (m, n))
    extras = [] if add is None else [(add, tile)]
    outs, finish = [(SDS((M, Ko), out_dtype), tile)], _finish_sum
    if rms is not None:
        assert tn == Ko == D and add is None
        (extras, outs), finish = _rms_bwd_io(M, tm, *rms), _finish_rms_bwd
    return _mm_call(name, (M // tm, Ko // tn, nk), [(a, b)], [pl.BlockSpec((tm, tk), lambda m, n, k: (m, k)), b_spec],
                    _NT, (tm, tn), extras, outs, finish, from_ref=rms is not None)


def _mm_nt_cols(parts, b, *, name, rms):
    M = parts[0].shape[0]
    tm = _tile(M, 512)
    specs, off = [], 0
    for p in parts:
        w = p.shape[1]
        assert off % w == 0
        specs.append(pl.BlockSpec((tm, w), lambda m, n, k: (m, 0)))
        specs.append(pl.BlockSpec((D, w), functools.partial(lambda m, n, k, o: (0, o), o=off // w)))
        off += w
    extras, outs = _rms_bwd_io(M, tm, *rms)
    return _mm_call(name, (M // tm, 1, 1), [(p, b) for p in parts], specs, _NT, (tm, D), extras, outs, _finish_rms_bwd,
                    from_ref=True)


def _mm_tn(a, b, *, out_dtype, name, blocks=None):
    S, Ka = a.shape
    Nb = b.shape[1]
    tm, tk = _tile(Ka, 1024), _tile(S, 2048)
    if blocks is not None:
        bw = blocks
        tn, nn = bw, Nb // bw
        out = (SDS((nn, Ka, bw), out_dtype), pl.BlockSpec((None, tm, bw), lambda m, n, k: (_perm2(n), m, 0)))
    else:
        tn = _tile(Nb, 1024)
        nn = Nb // tn
        out = (SDS((Ka, Nb), out_dtype), pl.BlockSpec((tm, tn), lambda m, n, k: (m, n)))
    return _mm_call(name, (Ka // tm, nn, S // tk), [(a, b)],
                    [pl.BlockSpec((tk, tm), lambda m, n, k: (k, m)), pl.BlockSpec((tk, tn), lambda m, n, k: (k, n))],
                    _TN, (tm, tn), [], [out], _finish_sum)


def _row(tm, c):
    return pl.BlockSpec((tm, c), lambda i: (i, 0))


def _full(shape):
    nd = len(shape)
    return pl.BlockSpec(shape, lambda i: (0,) * nd)


def _rms_fwd(x, g, name):
    S = x.shape[0]
    tm = min(S, TM_ROW)

    def body(x_ref, g_ref, o_ref):
        xf = x_ref[...]
        r = lax.rsqrt(jnp.mean(xf * xf, axis=-1, keepdims=True) + EPS)
        o_ref[...] = ((xf * r) * g_ref[...]).astype(BF16)

    return pl.pallas_call(body, out_shape=SDS((S, D), BF16), grid=(S // tm,), in_specs=[_row(tm, D), _full((1, D))],
                          out_specs=_row(tm, D), name=name, compiler_params=_cp(1))(x, g)


def _rms_bwd(x, g, dn, dres, name):
    S = x.shape[0]
    tm = min(S, TM_ROW)
    want_dx = dres is not None

    def body(x_ref, g_ref, dn_ref, *rest):
        i = pl.program_id(0)
        dg_ref = rest[-1]

        @pl.when(i == 0)
        def _():
            dg_ref[...] = jnp.zeros_like(dg_ref)

        xf = x_ref[...]
        r = lax.rsqrt(jnp.mean(xf * xf, axis=-1, keepdims=True) + EPS)
        y = xf * r
        dn_v = dn_ref[...]
        dg_ref[...] += jnp.sum(dn_v * y, axis=0, keepdims=True)
        if want_dx:
            dres_ref, dx_ref, dxb_ref = rest[0], rest[1], rest[2]
            dy = dn_v * g_ref[...]
            dx = r * (dy - y * jnp.mean(dy * y, axis=-1, keepdims=True)) + dres_ref[...]
            dx_ref[...] = dx
            dxb_ref[...] = dx.astype(BF16)

    ins = [x, g, dn] + ([dres] if want_dx else [])
    in_specs = [_row(tm, D), _full((1, D)), _row(tm, D)] + ([_row(tm, D)] if want_dx else [])
    outs = ([SDS((S, D), F32), SDS((S, D), BF16)] if want_dx else []) + [SDS((1, D), F32)]
    out_specs = ([_row(tm, D), _row(tm, D)] if want_dx else []) + [_full((1, D))]
    return pl.pallas_call(body, out_shape=outs, grid=(S // tm,), in_specs=in_specs, out_specs=out_specs, name=name,
                          compiler_params=_cp(1))(*ins)


def _loss_head(x, g, tgt):
    S = x.shape[0]
    tm = min(S, TM_ROW)

    def body(x_ref, g_ref, t_ref, loss_ref, dx_ref, dxb_ref, dg_ref):
        i = pl.program_id(0)

        @pl.when(i == 0)
        def _():
            loss_ref[...] = jnp.zeros_like(loss_ref)
            dg_ref[...] = jnp.zeros_like(dg_ref)

        xf = x_ref[...]
        r = lax.rsqrt(jnp.mean(xf * xf, axis=-1, keepdims=True) + EPS)
        y = xf * r
        gv = g_ref[...]
        err = y * gv - t_ref[...]
        per_row = jnp.mean(err * err, axis=-1, keepdims=True)
        loss_ref[...] += 0.5 * jnp.sum(per_row, axis=0, keepdims=True)
        dn_v = err * (1.0 / D)
        dg_ref[...] += jnp.sum(dn_v * y, axis=0, keepdims=True)
        dy = dn_v * gv
        dx = r * (dy - y * jnp.mean(dy * y, axis=-1, keepdims=True))
        dx_ref[...] = dx
        dxb_ref[...] = dx.astype(BF16)

    return pl.pallas_call(
        body, out_shape=[SDS((1, 1), F32), SDS((S, D), F32), SDS((S, D), BF16), SDS((1, D), F32)], grid=(S // tm,),
        in_specs=[_row(tm, D), _full((1, D)), _row(tm, D)],
        out_specs=[_full((1, 1)), _row(tm, D), _row(tm, D), _full((1, D))], name="loss_head", compiler_params=_cp(1),
    )(x, g, tgt)


def _softmax_rows(s):
    m = jnp.max(s, axis=-1, keepdims=True)
    e = jnp.exp(s - m)
    return e / jnp.sum(e, axis=-1, keepdims=True)


def _attn_fwd(q, k, v, name):
    S = q.shape[0]
    tm = min(S, TM_ROW)
    scale = XA_HD ** -0.5

    def body(q_ref, k_ref, v_ref, o_ref):
        for h in range(XA_HEADS):
            sl = slice(h * XA_HD, (h + 1) * XA_HD)
            s = lax.dot_general(q_ref[:, sl], k_ref[:, sl], _NT, preferred_element_type=F32) * scale
            p = _softmax_rows(s)
            o_ref[:, sl] = lax.dot_general(p.astype(BF16), v_ref[:, sl], _NN, preferred_element_type=F32).astype(BF16)

    return pl.pallas_call(body, out_shape=SDS((S, D), BF16), grid=(S // tm,),
                          in_specs=[_row(tm, D), _full((N_MEM, D)), _full((N_MEM, D))], out_specs=_row(tm, D),
                          name=name, compiler_params=_cp(1))(q, k, v)


def _attn_bwd(q, k, v, do, name):
    S = q.shape[0]
    tm = min(S, TM_ROW)
    scale = XA_HD ** -0.5

    def body(q_ref, k_ref, v_ref, do_ref, dq_ref, dk_ref, dv_ref):
        i = pl.program_id(0)

        @pl.when(i == 0)
        def _():
            dk_ref[...] = jnp.zeros_like(dk_ref)
            dv_ref[...] = jnp.zeros_like(dv_ref)

        for h in range(XA_HEADS):
            sl = slice(h * XA_HD, (h + 1) * XA_HD)
            qh, kh, vh, doh = q_ref[:, sl], k_ref[:, sl], v_ref[:, sl], do_ref[:, sl]
            s = lax.dot_general(qh, kh, _NT, preferred_element_type=F32) * scale
            p = _softmax_rows(s)
            pb = p.astype(BF16)
            dv_ref[:, sl] += lax.dot_general(pb, doh, _TN, preferred_element_type=F32)
            dp = lax.dot_general(doh, vh, _NT, preferred_element_type=F32)
            ds = (p * (dp - jnp.sum(dp * p, axis=-1, keepdims=True)) * scale).astype(BF16)
            dq_ref[:, sl] = lax.dot_general(ds, kh, _NN, preferred_element_type=F32).astype(BF16)
            dk_ref[:, sl] += lax.dot_general(ds, qh, _TN, preferred_element_type=F32)

    return pl.pallas_call(
        body, out_shape=[SDS((S, D), BF16), SDS((N_MEM, D), F32), SDS((N_MEM, D), F32)], grid=(S // tm,),
        in_specs=[_row(tm, D), _full((N_MEM, D)), _full((N_MEM, D)), _row(tm, D)],
        out_specs=[_row(tm, D), _full((N_MEM, D)), _full((N_MEM, D))], name=name, compiler_params=_cp(1),
    )(q, k, v, do)


def _sigmoid(x):
    return 1.0 / (1.0 + jnp.exp(-x))


def _ln_silu_fwd(cv, g, b):
    S = cv.shape[0]
    tm = min(S, TM_ROW)

    def body(x_ref, g_ref, b_ref, o_ref):
        xf = x_ref[...]
        mu = jnp.mean(xf, axis=-1, keepdims=True)
        xc = xf - mu
        rstd = lax.rsqrt(jnp.mean(xc * xc, axis=-1, keepdims=True) + EPS)
        ln = (xc * rstd) * g_ref[...] + b_ref[...]
        o_ref[...] = (ln * _sigmoid(ln)).astype(BF16)

    return pl.pallas_call(body, out_shape=SDS((S, D), BF16), grid=(S // tm,),
                          in_specs=[_row(tm, D), _full((1, D)), _full((1, D))], out_specs=_row(tm, D),
                          name="ln_silu_fwd", compiler_params=_cp(1))(cv, g, b)


def _ln_silu_bwd(ds, cv, g, b, dx):
    S = cv.shape[0]
    tm = min(S, TM_ROW)

    def body(ds_ref, x_ref, g_ref, b_ref, dx_ref, dcv_ref, dg_ref, db_ref, db2_ref):
        i = pl.program_id(0)

        @pl.when(i == 0)
        def _():
            dg_ref[...] = jnp.zeros_like(dg_ref)
            db_ref[...] = jnp.zeros_like(db_ref)
            db2_ref[...] = jnp.zeros_like(db2_ref)

        xf = x_ref[...]
        mu = jnp.mean(xf, axis=-1, keepdims=True)
        xc = xf - mu
        rstd = lax.rsqrt(jnp.mean(xc * xc, axis=-1, keepdims=True) + EPS)
        xhat = xc * rstd
        gv = g_ref[...]
        ln = xhat * gv + b_ref[...]
        sg = _sigmoid(ln)
        dln = ds_ref[...].astype(F32) * (sg + ln * sg * (1.0 - sg))
        dg_ref[...] += jnp.sum(dln * xhat, axis=0, keepdims=True)
        db_ref[...] += jnp.sum(dln, axis=0, keepdims=True)
        db2_ref[...] += jnp.sum(dx_ref[...], axis=0, keepdims=True)
        dxh = dln * gv
        dcv_ref[...] = rstd * (dxh - jnp.mean(dxh, axis=-1, keepdims=True)
                               - xhat * jnp.mean(dxh * xhat, axis=-1, keepdims=True))

    return pl.pallas_call(
        body, out_shape=[SDS((S, D), F32), SDS((1, D), F32), SDS((1, D), F32), SDS((1, D), F32)], grid=(S // tm,),
        in_specs=[_row(tm, D), _row(tm, D), _full((1, D)), _full((1, D)), _row(tm, D)],
        out_specs=[_row(tm, D), _full((1, D)), _full((1, D)), _full((1, D))], name="ln_silu_bwd",
        compiler_params=_cp(1),
    )(ds, cv, g, b, dx)


_GELU_C, _GELU_K = 0.7978845608028654, 0.044715


def _gelu(x, with_grad=False):
    x2 = x * x
    t = jnp.tanh(_GELU_C * (x + _GELU_K * x * x2))
    gel = 0.5 * x * (1.0 + t)
    if not with_grad:
        return gel
    return gel, 0.5 * (1.0 + t) + 0.5 * x * (1.0 - t * t) * (_GELU_C * (1.0 + 3.0 * _GELU_K * x2))


def _expm1(x):
    poly = x * (1.0 + x * (0.5 + x * (1.0 / 6.0 + x * (1.0 / 24.0 + x * (1.0 / 120.0)))))
    return jnp.where(jnp.abs(x) < 0.05, poly, jnp.exp(x) - 1.0)


def _softplus(x):
    return jnp.maximum(x, 0.0) + jnp.log1p(jnp.exp(-jnp.abs(x)))


_SCAN_UNROLL = 4
_RB = 32
_HB = 16


def _sub_blocks(n_rows, n_lanes, fn):
    def step(idx, c):
        r0 = pl.multiple_of(idx * _RB, _RB)
        for lt in range(n_lanes // LANE):
            fn(r0, lt)
        return c

    lax.fori_loop(0, n_rows // _RB, step, 0)


def _lanes(lt):
    return pl.ds(lt * LANE, LANE)


def _psum8(x):
    parts = [x[i * SUB:(i + 1) * SUB] for i in range(x.shape[0] // SUB)]
    return functools.reduce(lambda p, q: p + q, parts)


def _scan_fwd(a_s, b_s, out_ref, carry_ref, n_groups):
    row = lax.broadcasted_iota(jnp.int32, (SUB, LANE), 0)
    U = _SCAN_UNROLL

    def step(gi, carry):
        base = gi * (SUB * U)
        parts = []
        for u in range(U):
            i = pl.multiple_of(base + u * SUB, SUB)
            a8, b8 = a_s[pl.ds(i, SUB), :], b_s[pl.ds(i, SUB), :]
            for s in (1, 2, 4):
                a_sh = jnp.where(row >= s, pltpu.roll(a8, s, 0), 1.0)
                b_sh = jnp.where(row >= s, pltpu.roll(b8, s, 0), 0.0)
                b8 = a8 * b_sh + b8
                a8 = a8 * a_sh
            parts.append((i, a8, b8))
        for i, a8, b8 in parts:
            h8 = a8 * carry + b8
            out_ref[pl.ds(i, SUB), :] = h8
            carry = jnp.broadcast_to(h8[SUB - 1:SUB, :], (SUB, LANE))
        return carry

    carry_ref[...] = lax.fori_loop(0, n_groups // U, step, carry_ref[...])


def _scan_bwd(a_s, b_s, out_ref, carry_ref, n_groups):
    row = lax.broadcasted_iota(jnp.int32, (SUB, LANE), 0)
    U = _SCAN_UNROLL

    def step(gi, carry):
        base = (n_groups // U - 1 - gi) * (SUB * U)
        parts = []
        for u in reversed(range(U)):
            i = pl.multiple_of(base + u * SUB, SUB)
            a8, b8 = a_s[pl.ds(i, SUB), :], b_s[pl.ds(i, SUB), :]
            for s in (1, 2, 4):
                a_sh = jnp.where(row < SUB - s, pltpu.roll(a8, SUB - s, 0), 1.0)
                b_sh = jnp.where(row < SUB - s, pltpu.roll(b8, SUB - s, 0), 0.0)
                b8 = a8 * b_sh + b8
                a8 = a8 * a_sh
            parts.append((i, a8, b8))
        for i, a8, b8 in parts:
            h8 = a8 * carry + b8
            out_ref[pl.ds(i, SUB), :] = h8
            carry = jnp.broadcast_to(h8[0:1, :], (SUB, LANE))
        return carry

    carry_ref[...] = lax.fori_loop(0, n_groups // U, step, carry_ref[...])


def _rglru_pre(xr, wgx_ref, bgx_ref, wga_ref, bga_ref, lam_ref):
    xrb = xr.astype(BF16)
    wgx, wga = wgx_ref[0].astype(BF16), wga_ref[0].astype(BF16)
    gx = _sigmoid(lax.dot_general(xrb, wgx, _NN, preferred_element_type=F32) + bgx_ref[...])
    ga = _sigmoid(lax.dot_general(xrb, wga, _NN, preferred_element_type=F32) + bga_ref[...])
    sp = _softplus(-lam_ref[...])
    log_a = -C_RG * ga * sp
    a = jnp.exp(log_a)
    mult = jnp.sqrt(-_expm1(2.0 * log_a))
    return gx, ga, sp, a, mult, xrb, wgx, wga


def _a_specs():
    vec = pl.BlockSpec((1, HD_A), lambda c, j: (0, c))
    mat = pl.BlockSpec((1, HD_A, HD_A), lambda c, j: (c, 0, 0))
    return [pl.BlockSpec((CONV_A, HD_A), lambda c, j: (0, c)), vec, mat, vec, mat, vec, vec]


def _a_fwd(zp, conv_w, conv_b, wgx, bgx, wga, bga, lam):
    S = zp.shape[0]
    R, nt = R_RGLRU, D // HD_A
    H = SUB

    def body(zg_ref, zr_ref, cw_ref, cb_ref, wgx_ref, bgx_ref, wga_ref, bga_ref, lam_ref, ya_ref, h_ref,
             ext, a_s, b_s, hc):
        j = pl.program_id(1)

        @pl.when(j == 0)
        def _():
            ext[0:H, :] = jnp.zeros((H, HD_A), F32)
            hc[...] = jnp.zeros_like(hc)

        ext[H:H + R, :] = zr_ref[...].astype(F32)
        xr = cb_ref[...]
        for k in range(CONV_A):
            xr = xr + cw_ref[k:k + 1, :] * ext[pl.ds(H - (CONV_A - 1 - k), R), :]
        gx, _, _, a, mult, _, _, _ = _rglru_pre(xr, wgx_ref, bgx_ref, wga_ref, bga_ref, lam_ref)
        a_s[...] = a
        b_s[...] = mult * (gx * xr)
        _scan_fwd(a_s, b_s, h_ref, hc, R // SUB)
        ya_ref[...] = (_gelu(zg_ref[...].astype(F32)) * h_ref[...]).astype(BF16)
        ext[0:H, :] = ext[R:R + H, :]

    return pl.pallas_call(
        body, out_shape=[SDS((S, D + D // 2), BF16), SDS((S, D), F32)], grid=(nt, S // R),
        in_specs=[pl.BlockSpec((R, HD_A), lambda c, j: (j, c)), pl.BlockSpec((R, HD_A), lambda c, j: (j, nt + c))]
        + _a_specs(),
        out_specs=[pl.BlockSpec((R, HD_A), lambda c, j: (j, c)), pl.BlockSpec((R, HD_A), lambda c, j: (j, c))],
        scratch_shapes=[pltpu.VMEM((H + R, HD_A), F32), pltpu.VMEM((R, HD_A), F32), pltpu.VMEM((R, HD_A), F32),
                        pltpu.VMEM((SUB, HD_A), F32)],
        name="rglru_fwd", compiler_params=_cp(2),
    )(zp, zp, conv_w, conv_b, wgx, bgx, wga, bga, lam)


def _a_bwd(dyab, zp, h, conv_w, conv_b, wgx, bgx, wga, bga, lam):
    S = zp.shape[0]
    R, nt, nch = R_RGLRU, D // HD_A, S // R_RGLRU
    H = SUB

    def rows(c, j):
        return (nch - 1 - j, c)

    def rows_rec(c, j):
        return (nch - 1 - j, nt + c)

    def halo(c, j):
        return (jnp.maximum((nch - 1 - j) * (R // H) - 1, 0), c)

    def halo_z(c, j):
        return (jnp.maximum((nch - 1 - j) * (R // _HB) - 1, 0), nt + c)

    def body(dy_ref, zg_ref, zr_ref, zh_ref, h_ref, hh_ref, cw_ref, cb_ref, wgx_ref, bgx_ref, wga_ref, bga_ref,
             lam_ref, dzg_ref, dzr_ref, dcw_ref, dcb_ref, dwgx_ref, dbgx_ref, dwga_ref, dbga_ref, dlam_ref,
             ext_z, ext_h, ext_mu, ext_d, a_s, b_s, muc):
        j = pl.program_id(1)
        first_chunk = (nch - 1 - j) == 0

        @pl.when(j == 0)
        def _():
            ext_mu[R:R + H, :] = jnp.zeros((H, HD_A), F32)
            ext_d[R:R + H, :] = jnp.zeros((H, HD_A), F32)
            muc[...] = jnp.zeros_like(muc)
            for r in (dcw_ref, dcb_ref, dwgx_ref, dbgx_ref, dwga_ref, dbga_ref, dlam_ref):
                r[...] = jnp.zeros_like(r)

        zg = zg_ref[...].astype(F32)
        ext_z[0:H, :] = jnp.where(first_chunk, 0.0, zh_ref[_HB - H:_HB, :].astype(F32))
        ext_z[H:H + R, :] = zr_ref[...].astype(F32)
        ext_h[0:H, :] = jnp.where(first_chunk, 0.0, hh_ref[...])
        ext_h[H:H + R, :] = h_ref[...]
        xr = cb_ref[...]
        for k in range(CONV_A):
            xr = xr + cw_ref[k:k + 1, :] * ext_z[pl.ds(H - (CONV_A - 1 - k), R), :]
        gx, ga, sp, a, mult, xrb, wgxb, wgab = _rglru_pre(xr, wgx_ref, bgx_ref, wga_ref, bga_ref, lam_ref)
        gel, dgel = _gelu(zg, with_grad=True)
        dy = dy_ref[...].astype(F32)
        dh = dy * gel
        dzg_ref[...] = (dy * h_ref[...] * dgel).astype(BF16)
        a_s[...] = a
        b_s[...] = a * dh
        _scan_bwd(a_s, b_s, ext_mu, muc, R // SUB)
        lam_t = dh + ext_mu[pl.ds(1, R), :]
        ext_mu[R:R + H, :] = ext_mu[0:H, :]
        da = lam_t * ext_h[pl.ds(H - 1, R), :]
        gxr = gx * xr
        dlog_a = da * a - (lam_t * gxr) * (a * a) / mult
        dgx = lam_t * mult * xr
        dxr = lam_t * mult * gx
        lam_v = lam_ref[...]
        dlam_ref[...] += jnp.sum(dlog_a * ga, axis=0, keepdims=True) * (C_RG * _sigmoid(-lam_v))
        dpa = (dlog_a * (-C_RG * sp)) * ga * (1.0 - ga)
        dpx = dgx * gx * (1.0 - gx)
        dbga_ref[...] += jnp.sum(dpa, axis=0, keepdims=True)
        dbgx_ref[...] += jnp.sum(dpx, axis=0, keepdims=True)
        dpab, dpxb = dpa.astype(BF16), dpx.astype(BF16)
        dwga_ref[0] += lax.dot_general(xrb, dpab, _TN, preferred_element_type=F32)
        dwgx_ref[0] += lax.dot_general(xrb, dpxb, _TN, preferred_element_type=F32)
        dxr = (dxr + lax.dot_general(dpab, wgab, _NT, preferred_element_type=F32)
               + lax.dot_general(dpxb, wgxb, _NT, preferred_element_type=F32))
        dcb_ref[...] += jnp.sum(dxr, axis=0, keepdims=True)
        ext_d[0:R, :] = dxr
        dzr = jnp.zeros((R, HD_A), F32)
        for k in range(CONV_A):
            sh = CONV_A - 1 - k
            dcw_ref[k:k + 1, :] += jnp.sum(dxr * ext_z[pl.ds(H - sh, R), :], axis=0, keepdims=True)
            dzr = dzr + cw_ref[k:k + 1, :] * ext_d[pl.ds(sh, R), :]
        dzr_ref[...] = dzr.astype(BF16)
        ext_d[R:R + H, :] = ext_d[0:H, :]

    vec_o = pl.BlockSpec((1, HD_A), lambda c, j: (0, c))
    mat_o = pl.BlockSpec((1, HD_A, HD_A), lambda c, j: (c, 0, 0))
    return pl.pallas_call(
        body,
        out_shape=[SDS((S, D), BF16), SDS((S, D), BF16), SDS((CONV_A, D), F32), SDS((1, D), F32),
                   SDS((nt, HD_A, HD_A), F32), SDS((1, D), F32), SDS((nt, HD_A, HD_A), F32), SDS((1, D), F32),
                   SDS((1, D), F32)],
        grid=(nt, nch),
        in_specs=[pl.BlockSpec((R, HD_A), rows), pl.BlockSpec((R, HD_A), rows), pl.BlockSpec((R, HD_A), rows_rec),
                  pl.BlockSpec((_HB, HD_A), halo_z), pl.BlockSpec((R, HD_A), rows),
                  pl.BlockSpec((H, HD_A), halo)] + _a_specs(),
        out_specs=[pl.BlockSpec((R, HD_A), rows), pl.BlockSpec((R, HD_A), rows),
                   pl.BlockSpec((CONV_A, HD_A), lambda c, j: (0, c)), vec_o, mat_o, vec_o, mat_o, vec_o, vec_o],
        scratch_shapes=[pltpu.VMEM((H + R, HD_A), F32), pltpu.VMEM((H + R, HD_A), F32), pltpu.VMEM((R + H, HD_A), F32),
                        pltpu.VMEM((R + H, HD_A), F32), pltpu.VMEM((R, HD_A), F32), pltpu.VMEM((R, HD_A), F32),
                        pltpu.VMEM((SUB, HD_A), F32)],
        name="rglru_bwd", compiler_params=_cp(2),
    )(dyab, zp, zp, zp, h, h, conv_w, conv_b, wgx, bgx, wga, bga, lam)


_POOL_H = 16
_POOL_T0 = 2 * D // HD_A
_POOL_Y0 = D // HD_A


def _window_sum(lv, n, lo, rows, g, ahead):
    base = 0 if ahead else SUB
    cur, win = lv[0], None
    for i, s in enumerate((1, 2, 4, 8)):
        val = cur[pl.ds(base, n), :] + cur[pl.ds(base + (s if ahead else -s), n), :]
        sel = val[lo:lo + rows]
        win = sel if win is None else jnp.where(g >= i, sel, win)
        if i < 3:
            lv[i + 1][pl.ds(base, n), :] = val
            cur = lv[i + 1]
    return win


def _pool_width(g):
    return jnp.where(g == 0, 2.0, jnp.where(g == 1, 4.0, jnp.where(g == 2, 8.0, 16.0)))


def _b_fwd(zp, yab, wg, bg, sc):
    S = zp.shape[0]
    R, H = R_SEQ, _POOL_H

    def body(z_ref, wg_ref, bg_ref, sc_ref, yab_in, yb_ref, *lv):
        del yab_in
        g, j = pl.program_id(0), pl.program_id(1)

        @pl.when(j == 0)
        def _():
            for r in lv:
                r[0:SUB, :] = jnp.zeros((SUB, HD_A), F32)
            lv[0][SUB:SUB + H, :] = jnp.zeros((H, HD_A), F32)

        u = z_ref[...].astype(F32)
        lv[0][SUB + H:SUB + H + R, :] = u
        t1 = (j * R + 1 + lax.broadcasted_iota(jnp.int32, (R, HD_A), 0)).astype(F32)
        p = _window_sum(lv, H + R, H, R, g, False) / jnp.minimum(t1, _pool_width(g)) - u
        lin = lax.dot_general(p.astype(BF16), wg_ref[0].astype(BF16), _NN, preferred_element_type=F32) + bg_ref[...]
        yb_ref[...] = (lin * sc_ref[...]).astype(BF16)
        lv[0][SUB:SUB + H, :] = lv[0][SUB + R:SUB + R + H, :]

    vec = pl.BlockSpec((1, HD_A), lambda g, j: (0, g))
    return pl.pallas_call(
        body, out_shape=SDS(yab.shape, yab.dtype), grid=(len(POOL_WINDOWS), S // R),
        in_specs=[pl.BlockSpec((R, HD_A), lambda g, j: (j, _POOL_T0 + g)),
                  pl.BlockSpec((1, HD_A, HD_A), lambda g, j: (g, 0, 0)), vec, vec, pl.BlockSpec(memory_space=pl.ANY)],
        out_specs=pl.BlockSpec((R, HD_A), lambda g, j: (j, _POOL_Y0 + g)),
        scratch_shapes=[pltpu.VMEM((SUB + H + R, HD_A), F32)] * 4, input_output_aliases={4: 0},
        name="pool_fwd", compiler_params=_cp(2),
    )(zp, wg, bg, sc, yab)


def _b_bwd(dyab, zp, wg, bg, sc):
    S = zp.shape[0]
    R, H, nch, ng = R_SEQ, _POOL_H, S // R_SEQ, len(POOL_WINDOWS)

    def body(dy_ref, z_ref, zh_ref, wg_ref, bg_ref, sc_ref, dz_ref, dwg_ref, dbg_ref, dsc_ref, *scratch):
        lu, lq = scratch[:4], scratch[4:]
        g, j = pl.program_id(0), pl.program_id(1)
        jj = nch - 1 - j

        @pl.when(j == 0)
        def _():
            for r in lu:
                r[0:SUB, :] = jnp.zeros((SUB, HD_A), F32)
            for r in lq:
                r[R + H:R + H + SUB, :] = jnp.zeros((SUB, HD_A), F32)
            lq[0][R:R + H, :] = jnp.zeros((H, HD_A), F32)
            for r in (dwg_ref, dbg_ref, dsc_ref):
                r[...] = jnp.zeros_like(r)

        u = z_ref[...].astype(F32)
        lu[0][SUB:SUB + H, :] = jnp.where(jj == 0, 0.0, zh_ref[...].astype(F32))
        lu[0][SUB + H:SUB + H + R, :] = u
        t1 = (jj * R + 1 + lax.broadcasted_iota(jnp.int32, (R, HD_A), 0)).astype(F32)
        cnt = jnp.minimum(t1, _pool_width(g))
        pb = (_window_sum(lu, H + R, H, R, g, False) / cnt - u).astype(BF16)
        wgb = wg_ref[0].astype(BF16)
        lin = lax.dot_general(pb, wgb, _NN, preferred_element_type=F32) + bg_ref[...]
        dy = dy_ref[...].astype(F32)
        dsc_ref[...] += jnp.sum(dy * lin, axis=0, keepdims=True)
        dlin = dy * sc_ref[...]
        dbg_ref[...] += jnp.sum(dlin, axis=0, keepdims=True)
        dlb = dlin.astype(BF16)
        dwg_ref[0] += lax.dot_general(pb, dlb, _TN, preferred_element_type=F32)
        dp = lax.dot_general(dlb, wgb, _NT, preferred_element_type=F32)
        lq[0][0:R, :] = dp / cnt
        dz_ref[...] = (_window_sum(lq, R + H, 0, R, g, True) - dp).astype(BF16)
        lq[0][R:R + H, :] = lq[0][0:H, :]

    vec = pl.BlockSpec((1, HD_A), lambda g, j: (0, g))
    mat = pl.BlockSpec((1, HD_A, HD_A), lambda g, j: (g, 0, 0))
    return pl.pallas_call(
        body, out_shape=[SDS((S, D // 2), BF16), SDS((ng, HD_A, HD_A), F32), SDS((1, D // 2), F32),
                         SDS((1, D // 2), F32)],
        grid=(ng, nch),
        in_specs=[pl.BlockSpec((R, HD_A), lambda g, j: (nch - 1 - j, _POOL_Y0 + g)),
                  pl.BlockSpec((R, HD_A), lambda g, j: (nch - 1 - j, _POOL_T0 + g)),
                  pl.BlockSpec((H, HD_A), lambda g, j: (jnp.maximum((nch - 1 - j) * (R // H) - 1, 0), _POOL_T0 + g)),
                  mat, vec, vec],
        out_specs=[pl.BlockSpec((R, HD_A), lambda g, j: (nch - 1 - j, g)), mat, vec, vec],
        scratch_shapes=[pltpu.VMEM((SUB + H + R, HD_A), F32)] * 8,
        name="pool_bwd", compiler_params=_cp(2),
    )(dyab, zp, zp, wg, bg, sc)


_CW_F = 768


def _f_fwd(hp, w, b, name):
    S = hp.shape[0]
    R, H, cw = min(S, R_FFN), SUB, _CW_F
    nlt = cw // LANE

    def body(h_ref, w_ref, b_ref, o_ref, gel_ref, ud_ref, ext):
        j = pl.program_id(1)

        @pl.when(j == 0)
        def _():
            ext[:, 0:H, :] = jnp.zeros((nlt, H, LANE), F32)

        def stage(r0, lt):
            ext[lt, pl.ds(pl.multiple_of(r0 + H, SUB), _RB), :] = h_ref[pl.ds(r0, _RB), _lanes(lt)].astype(F32)

        def main(r0, lt):
            ls = _lanes(lt)
            gp = b_ref[:, ls]
            for k in range(CONV_F):
                gp = gp + w_ref[k:k + 1, ls] * ext[lt, pl.ds(r0 + (H - (CONV_F - 1 - k)), _RB), :]
            up = h_ref[pl.ds(r0, _RB), _lanes(lt + nlt)].astype(F32)
            gel, dgel = _gelu(gp, with_grad=True)
            rs = pl.ds(r0, _RB)
            o_ref[rs, ls] = (gel * up).astype(BF16)
            gel_ref[rs, ls] = gel.astype(BF16)
            ud_ref[rs, ls] = (up * dgel).astype(BF16)

        _sub_blocks(R, cw, stage)
        _sub_blocks(R, cw, main)
        ext[:, 0:H, :] = ext[:, R:R + H, :]

    tile = pl.BlockSpec((R, cw), lambda c, j: (j, c))
    return pl.pallas_call(
        body, out_shape=[SDS((S, D_FF), BF16)] * 3, grid=(D_FF // cw, S // R),
        in_specs=[pl.BlockSpec((R, 2 * cw), lambda c, j: (j, c)), pl.BlockSpec((CONV_F, cw), lambda c, j: (0, c)),
                  pl.BlockSpec((1, cw), lambda c, j: (0, c))],
        out_specs=[tile] * 3,
        scratch_shapes=[pltpu.VMEM((nlt, H + R, LANE), F32)], name=name, compiler_params=_cp(2),
    )(hp, w, b)


def _f_bwd(dact, hp, gel, ud, w, name):
    S = hp.shape[0]
    R, H, cw = min(S, R_FFN), SUB, _CW_F
    nch = S // R
    nlt = cw // LANE

    def body(da_ref, h_ref, hh_ref, gel_ref, ud_ref, w_ref, dh_ref, dw_ref, db_ref, ext_g, ext_d, acc):
        j = pl.program_id(1)
        jj = nch - 1 - j

        @pl.when(j == 0)
        def _():
            ext_d[:, R:R + H, :] = jnp.zeros((nlt, H, LANE), F32)
            acc[...] = jnp.zeros_like(acc)

        for lt in range(nlt):
            ext_g[lt, 0:H, :] = jnp.where(jj == 0, 0.0, hh_ref[_HB - H:_HB, lt * LANE:(lt + 1) * LANE].astype(F32))

        def stage(r0, lt):
            ext_g[lt, pl.ds(pl.multiple_of(r0 + H, SUB), _RB), :] = h_ref[pl.ds(r0, _RB), _lanes(lt)].astype(F32)

        def first(r0, lt):
            ls, lu, rs = _lanes(lt), _lanes(lt + nlt), pl.ds(r0, _RB)
            da = da_ref[rs, ls].astype(F32)
            dh_ref[rs, lu] = (da * gel_ref[rs, ls].astype(F32)).astype(BF16)
            dgp = da * ud_ref[rs, ls].astype(F32)
            ext_d[lt, rs, :] = dgp
            acc[CONV_F * SUB:(CONV_F + 1) * SUB, ls] += _psum8(dgp)
            for k in range(CONV_F):
                tap = ext_g[lt, pl.ds(r0 + (H - (CONV_F - 1 - k)), _RB), :]
                acc[k * SUB:(k + 1) * SUB, ls] += _psum8(dgp * tap)

        def second(r0, lt):
            ls = _lanes(lt)
            dhg = w_ref[CONV_F - 1:CONV_F, ls] * ext_d[lt, pl.ds(r0, _RB), :]
            for k in range(CONV_F - 1):
                dhg = dhg + w_ref[k:k + 1, ls] * ext_d[lt, pl.ds(r0 + (CONV_F - 1 - k), _RB), :]
            dh_ref[pl.ds(r0, _RB), ls] = dhg.astype(BF16)

        _sub_blocks(R, cw, stage)
        _sub_blocks(R, cw, first)
        _sub_blocks(R, cw, second)
        ext_d[:, R:R + H, :] = ext_d[:, 0:H, :]

        @pl.when(j == nch - 1)
        def _():
            for k in range(CONV_F):
                dw_ref[k:k + 1, :] = jnp.sum(acc[k * SUB:(k + 1) * SUB, :], axis=0, keepdims=True)
            db_ref[...] = jnp.sum(acc[CONV_F * SUB:(CONV_F + 1) * SUB, :], axis=0, keepdims=True)

    rows = lambda c, j: (nch - 1 - j, c)
    return pl.pallas_call(
        body, out_shape=[SDS((S, 2 * D_FF), BF16), SDS((CONV_F, D_FF), F32), SDS((1, D_FF), F32)],
        grid=(D_FF // cw, nch),
        in_specs=[pl.BlockSpec((R, cw), rows), pl.BlockSpec((R, cw), lambda c, j: (nch - 1 - j, 2 * c)),
                  pl.BlockSpec((_HB, cw), lambda c, j: (jnp.maximum((nch - 1 - j) * (R // _HB) - 1, 0), 2 * c)),
                  pl.BlockSpec((R, cw), rows), pl.BlockSpec((R, cw), rows),
                  pl.BlockSpec((CONV_F, cw), lambda c, j: (0, c))],
        out_specs=[pl.BlockSpec((R, 2 * cw), rows), pl.BlockSpec((CONV_F, cw), lambda c, j: (0, c)),
                   pl.BlockSpec((1, cw), lambda c, j: (0, c))],
        scratch_shapes=[pltpu.VMEM((nlt, H + R, LANE), F32), pltpu.VMEM((nlt, R + H, LANE), F32),
                        pltpu.VMEM(((CONV_F + 1) * SUB, cw), F32)], name=name,
        compiler_params=_cp(2),
    )(dact, hp, hp, gel, ud, w)


_CW_C = 256
_H_C = 32


def _c_fwd(h1p, w, b):
    S = h1p.shape[0]
    R, H, cw = R_SEQ, _H_C, _CW_C
    nlt = cw // LANE

    def body(h_ref, w_ref, b_ref, o_ref, ext):
        j = pl.program_id(1)

        @pl.when(j == 0)
        def _():
            ext[:, 0:H, :] = jnp.zeros((nlt, H, LANE), F32)

        def stage(r0, lt):
            rs = pl.ds(r0, _RB)
            gate = h_ref[rs, _lanes(lt + nlt)].astype(F32)
            ext[lt, pl.ds(pl.multiple_of(r0 + H, SUB), _RB), :] = h_ref[rs, _lanes(lt)].astype(F32) * _sigmoid(gate)

        def main(r0, lt):
            ls = _lanes(lt)
            cv = b_ref[:, ls]
            for k in range(CONV_C):
                cv = cv + w_ref[k:k + 1, ls] * ext[lt, pl.ds(r0 + (H - (CONV_C - 1 - k)), _RB), :]
            o_ref[pl.ds(r0, _RB), ls] = cv

        _sub_blocks(R, cw, stage)
        _sub_blocks(R, cw, main)
        ext[:, 0:H, :] = ext[:, R:R + H, :]

    return pl.pallas_call(
        body, out_shape=SDS((S, D), F32), grid=(D // cw, S // R),
        in_specs=[pl.BlockSpec((R, 2 * cw), lambda c, j: (j, c)), pl.BlockSpec((CONV_C, cw), lambda c, j: (0, c)),
                  pl.BlockSpec((1, cw), lambda c, j: (0, c))],
        out_specs=pl.BlockSpec((R, cw), lambda c, j: (j, c)),
        scratch_shapes=[pltpu.VMEM((nlt, H + R, LANE), F32)], name="conf_conv_fwd", compiler_params=_cp(2),
    )(h1p, w, b)


def _c_bwd(dcv, h1p, w):
    S = h1p.shape[0]
    R, H, cw, nch = R_SEQ, _H_C, _CW_C, S // R_SEQ
    nlt = cw // LANE
    a_b, a_val, a_gate = CONV_C * SUB, (CONV_C + 1) * SUB, (CONV_C + 2) * SUB

    def body(dc_ref, h_ref, hh_ref, w_ref, dh_ref, dw_ref, db_ref, db1_ref, ext_u, ext_d, acc):
        j = pl.program_id(1)
        jj = nch - 1 - j

        @pl.when(j == 0)
        def _():
            ext_d[:, R:R + H, :] = jnp.zeros((nlt, H, LANE), F32)
            acc[...] = jnp.zeros_like(acc)

        for lt in range(nlt):
            ext_u[lt, 0:H, :] = jnp.where(
                jj == 0, 0.0, hh_ref[:, lt * LANE:(lt + 1) * LANE].astype(F32)
                * _sigmoid(hh_ref[:, cw + lt * LANE:cw + (lt + 1) * LANE].astype(F32)))

        def stage(r0, lt):
            rs, ls = pl.ds(r0, _RB), _lanes(lt)
            gate = h_ref[rs, _lanes(lt + nlt)].astype(F32)
            ext_u[lt, pl.ds(pl.multiple_of(r0 + H, SUB), _RB), :] = h_ref[rs, ls].astype(F32) * _sigmoid(gate)
            ext_d[lt, rs, :] = dc_ref[rs, ls]

        def first(r0, lt):
            ls = _lanes(lt)
            dc = dc_ref[pl.ds(r0, _RB), ls]
            acc[a_b:a_b + SUB, ls] += _psum8(dc)
            for k in range(CONV_C):
                tap = ext_u[lt, pl.ds(r0 + (H - (CONV_C - 1 - k)), _RB), :]
                acc[k * SUB:(k + 1) * SUB, ls] += _psum8(dc * tap)

        def second(r0, lt):
            rs, ls, lg = pl.ds(r0, _RB), _lanes(lt), _lanes(lt + nlt)
            du = w_ref[CONV_C - 1:CONV_C, ls] * ext_d[lt, rs, :]
            for k in range(CONV_C - 1):
                du = du + w_ref[k:k + 1, ls] * ext_d[lt, pl.ds(r0 + (CONV_C - 1 - k), _RB), :]
            val = h_ref[rs, ls].astype(F32)
            sg = _sigmoid(h_ref[rs, lg].astype(F32))
            dval = du * sg
            dgate = du * val * sg * (1.0 - sg)
            acc[a_val:a_val + SUB, ls] += _psum8(dval)
            acc[a_gate:a_gate + SUB, ls] += _psum8(dgate)
            dh_ref[rs, ls] = dval.astype(BF16)
            dh_ref[rs, lg] = dgate.astype(BF16)

        _sub_blocks(R, cw, stage)
        _sub_blocks(R, cw, first)
        _sub_blocks(R, cw, second)
        ext_d[:, R:R + H, :] = ext_d[:, 0:H, :]

        @pl.when(j == nch - 1)
        def _():
            for k in range(CONV_C):
                dw_ref[k:k + 1, :] = jnp.sum(acc[k * SUB:(k + 1) * SUB, :], axis=0, keepdims=True)
            db_ref[...] = jnp.sum(acc[a_b:a_b + SUB, :], axis=0, keepdims=True)
            db1_ref[:, 0:cw] = jnp.sum(acc[a_val:a_val + SUB, :], axis=0, keepdims=True)
            db1_ref[:, cw:2 * cw] = jnp.sum(acc[a_gate:a_gate + SUB, :], axis=0, keepdims=True)

    rows = lambda c, j: (nch - 1 - j, c)
    return pl.pallas_call(
        body, out_shape=[SDS((S, 2 * D), BF16), SDS((CONV_C, D), F32), SDS((1, D), F32), SDS((1, 2 * D), F32)],
        grid=(D // cw, nch),
        in_specs=[pl.BlockSpec((R, cw), rows), pl.BlockSpec((R, 2 * cw), rows),
                  pl.BlockSpec((H, 2 * cw), lambda c, j: (jnp.maximum((nch - 1 - j) * (R // H) - 1, 0), c)),
                  pl.BlockSpec((CONV_C, cw), lambda c, j: (0, c))],
        out_specs=[pl.BlockSpec((R, 2 * cw), rows), pl.BlockSpec((CONV_C, cw), lambda c, j: (0, c)),
                   pl.BlockSpec((1, cw), lambda c, j: (0, c)), pl.BlockSpec((1, 2 * cw), lambda c, j: (0, c))],
        scratch_shapes=[pltpu.VMEM((nlt, H + R, LANE), F32), pltpu.VMEM((nlt, R + H, LANE), F32),
                        pltpu.VMEM(((CONV_C + 3) * SUB, cw), F32)], name="conf_conv_bwd",
        compiler_params=_cp(2),
    )(dcv, h1p, h1p, w)


def _local_step(x, mem, tgt, W, fetch=None, send=None):
    G = {}
    W = dict(W)

    def arrive(group, after):
        if fetch is None:
            return None
        got, tok = fetch(group, after)
        for key, val in got.items():
            W[key] = {**W.get(key, {}), **val} if isinstance(val, dict) else val
        return tok

    def gain(g, tok):
        return g if tok is None else g + tok

    def sent(group):
        return None if send is None else send(group, G)

    def xattn_fwd(xin, n, l):
        tok = arrive(("xa", l), n)
        mn = _rms_fwd(mem, gain(W["xa_mem_norm"][l:l + 1], tok), f"xa_memnorm_fwd{l}")
        q = _mm_nn(n, W["xa_wq"][l], out_dtype=BF16, name=f"xa_q{l}")
        k = _mm_nn(mn, W["xa_wk"][l], out_dtype=BF16, name=f"xa_k{l}")
        v = _mm_nn(mn, W["xa_wv"][l], out_dtype=BF16, name=f"xa_v{l}")
        o = _attn_fwd(q, k, v, f"xa_attn_fwd{l}")
        xout, nout = _mm_nn(o, W["xa_wo"][l], out_dtype=F32, name=f"xa_o{l}", add=xin, norm=W["f_norm"][l:l + 1])
        return xout, nout, (xin, n, q, mn, k, v, o)

    def xattn_bwd(dx, dxb, saved, l):
        xin, n, q, mn, k, v, o = saved
        do = _mm_nt(dxb, W["xa_wo"][l], out_dtype=BF16, name=f"xa_do{l}")
        G[f"xa_wo{l}"] = _mm_tn(o, dxb, out_dtype=BF16, name=f"xa_dwo{l}")
        dq, dk, dv = _attn_bwd(q, k, v, do, f"xa_attn_bwd{l}")
        dkb, dvb = dk.astype(BF16), dv.astype(BF16)
        G[f"xa_wq{l}"] = _mm_tn(n, dq, out_dtype=BF16, name=f"xa_dwq{l}")
        G[f"xa_wk{l}"] = _mm_tn(mn, dkb, out_dtype=BF16, name=f"xa_dwk{l}")
        G[f"xa_wv{l}"] = _mm_tn(mn, dvb, out_dtype=BF16, name=f"xa_dwv{l}")
        tok = sent(("xa", l))
        dmn = _mm_nt(dkb, W["xa_wk"][l], out_dtype=F32, name=f"xa_dmn_k{l}")
        dmn = _mm_nt(dvb, W["xa_wv"][l], out_dtype=F32, name=f"xa_dmn_v{l}", add=dmn)
        (G[f"xa_mem_norm{l}"],) = _rms_bwd(mem, W["xa_mem_norm"][l:l + 1], dmn, None, f"xa_memnorm_bwd{l}")
        dx, dxb, G[f"xa_norm{l}"] = _mm_nt(dq, W["xa_wq"][l], out_dtype=F32, name=f"xa_dn{l}",
                                           rms=(xin, gain(W["xa_norm"][l:l + 1], tok), dx))
        return dx, dxb

    def ffn_fwd(xin, n, l, next_gain):
        tok = arrive(("f", l), n)
        hp = _mm_nn(n, W["f_w_up"][l], out_dtype=BF16, name=f"f_up{l}")
        act, gel, ud = _f_fwd(hp, W["f_dw_w"][l], gain(W["f_dw_b"][l:l + 1], tok), f"f_conv_fwd{l}")
        res = _mm_nn(act, W["f_w_down"][l], out_dtype=F32, name=f"f_down{l}", add=xin, norm=next_gain)
        xout, nout = res if next_gain is not None else (res, None)
        return xout, nout, (xin, n, hp, act, gel, ud)

    def ffn_bwd(dx, dxb, saved, l):
        xin, n, hp, act, gel, ud = saved
        dact = _mm_nt(dxb, W["f_w_down"][l], out_dtype=BF16, name=f"f_dact{l}")
        G[f"f_w_down{l}"] = _mm_tn(act, dxb, out_dtype=BF16, name=f"f_dwdown{l}")
        dhp, G[f"f_dw_w{l}"], G[f"f_dw_b{l}"] = _f_bwd(dact, hp, gel, ud, W["f_dw_w"][l], f"f_conv_bwd{l}")
        G[f"f_w_up{l}"] = _mm_tn(n, dhp, out_dtype=BF16, name=f"f_dwup{l}", blocks=_CW_F)
        tok = sent(("f", l))
        dx, dxb, G[f"f_norm{l}"] = _mm_nt(dhp, W["f_w_up"][l], out_dtype=F32, name=f"f_dn{l}",
                                          rms=(xin, gain(W["f_norm"][l:l + 1], tok), dx))
        return dx, dxb

    n0 = _rms_fwd(x, W["ab_norm"], "ab_norm_fwd")
    tok = arrive(("ab", 0), n0)
    a_par = (W["a_conv_w"], gain(W["a_conv_b"], tok), W["a_gate_x_w"], W["a_gate_x_b"], W["a_gate_a_w"],
             W["a_gate_a_b"], W["a_lambda"])
    b_par = (W["b_group_w"], W["b_group_b"], W["b_scale"])
    zp = _mm_nn(n0, W["ab_w_in"], out_dtype=BF16, name="ab_in")
    yab, h_a = _a_fwd(zp, *a_par)
    yab = _b_fwd(zp, yab, *b_par)
    arrive(("ab", 1), yab)
    x1, n1 = _mm_nn(yab, W["ab_w_out"], out_dtype=F32, name="ab_out", add=x, norm=W["xa_norm"][0:1])
    x2, n2, s_xa0 = xattn_fwd(x1, n1, 0)
    x3, n3, s_f0 = ffn_fwd(x2, n2, 0, W["c_norm"])
    tok = arrive(("c", 0), n3)
    h1p = _mm_nn(n3, W["c_w_pw1"], out_dtype=BF16, name="c_pw1", bias=gain(W["c_b_pw1"], tok))
    cv = _c_fwd(h1p, W["c_dw_w"], W["c_dw_b"])
    sc = _ln_silu_fwd(cv, W["c_ln_g"], W["c_ln_b"])
    x4, n4 = _mm_nn(sc, W["c_w_pw2"], out_dtype=F32, name="c_pw2", bias=W["c_b_pw2"], add=x3, norm=W["xa_norm"][1:2])
    x5, n5, s_xa1 = xattn_fwd(x4, n4, 1)
    x6, _, s_f1 = ffn_fwd(x5, n5, 1, None)
    loss, dx, dxb, G["final_norm"] = _loss_head(x6, W["final_norm"], tgt)

    dx, dxb = ffn_bwd(dx, dxb, s_f1, 1)
    dx, dxb = xattn_bwd(dx, dxb, s_xa1, 1)
    dsc = _mm_nt(dxb, W["c_w_pw2"], out_dtype=BF16, name="c_dsc")
    G["c_w_pw2"] = _mm_tn(sc, dxb, out_dtype=BF16, name="c_dwpw2")
    dcv, G["c_ln_g"], G["c_ln_b"], G["c_b_pw2"] = _ln_silu_bwd(dsc, cv, W["c_ln_g"], W["c_ln_b"], dx)
    dh1p, G["c_dw_w"], G["c_dw_b"], G["c_b_pw1"] = _c_bwd(dcv, h1p, W["c_dw_w"])
    G["c_w_pw1"] = _mm_tn(n3, dh1p, out_dtype=BF16, name="c_dwpw1", blocks=_CW_C)
    tok = sent(("c", 0))
    dx, dxb, G["c_norm"] = _mm_nt(dh1p, W["c_w_pw1"], out_dtype=F32, name="c_dn",
                                  rms=(x3, gain(W["c_norm"], tok), dx))
    dx, dxb = ffn_bwd(dx, dxb, s_f0, 0)
    dx, dxb = xattn_bwd(dx, dxb, s_xa0, 0)
    dyab = _mm_nt(dxb, W["ab_w_out"], out_dtype=BF16, name="ab_dyab")
    G["ab_w_out"] = _mm_tn(yab, dxb, out_dtype=BF16, name="ab_dwout")
    tok = sent(("ab", 1))
    a_par = (a_par[0], gain(a_par[1], tok)) + a_par[2:]
    (dzg, dzr, G["a_conv_w"], G["a_conv_b"], G["a_gate_x_w"], G["a_gate_x_b"], G["a_gate_a_w"], G["a_gate_a_b"],
     G["a_lambda"]) = _a_bwd(dyab, zp, h_a, *a_par)
    dzq, G["b_group_w"], G["b_group_b"], G["b_scale"] = _b_bwd(dyab, zp, *b_par)
    G["ab_w_in"] = jnp.concatenate(
        [_mm_tn(n0, dz, out_dtype=BF16, name=f"ab_dwin_{part}")
         for part, dz in (("gate", dzg), ("rec", dzr), ("pool", dzq))], axis=1)
    tok = sent(("ab", 0))
    dx, _, G["ab_norm"] = _mm_nt_cols([dzg, dzr, dzq], W["ab_w_in"], name="ab_dn",
                                      rms=(x, gain(W["ab_norm"], tok), dx))
    return loss, dx, G


def _my_place():
    x, y, c = lax.axis_index("x"), lax.axis_index("y"), lax.axis_index("c")
    return x, y, c


def _all_gather(shards, name):
    n = len(shards)

    def body(*refs):
        ins, outs = refs[:n], refs[n:2 * n]
        send_sems, recv_sems, local_sems = refs[2 * n:]
        x, y, c = _my_place()
        me, sibling = (x, y, c), (x, y, 1 - c)
        chips = [(1 - x, y), (x, 1 - y), (1 - x, 1 - y)]

        def slab(a, place):
            px, py, pc = place
            return outs[a].at[4 * px + 2 * py + pc]

        def copy(a, k, block, to, src=None):
            return pltpu.make_async_remote_copy(
                src_ref=slab(a, block) if src is None else src, dst_ref=slab(a, block),
                send_sem=send_sems.at[a, k], recv_sem=recv_sems.at[a, k], device_id=to, device_id_type=MESH)

        mine = [pltpu.make_async_copy(ins[a], slab(a, me), local_sems.at[a]) for a in range(n)]
        for cp in mine:
            cp.start()
        first = []
        for j, chip in enumerate(chips):
            first += [copy(a, 1 + j, me, (*chip, c), src=ins[a]) for a in range(n)]
        first += [copy(a, 0, me, sibling, src=ins[a]) for a in range(n)]
        for cp in first:
            cp.start()
        passed = []
        for j, chip in enumerate(chips):
            for a in range(n):
                copy(a, 1 + j, (*chip, c), me).wait_recv()
                cp = copy(a, 4 + j, (*chip, c), sibling)
                cp.start()
                passed.append(cp)
        for a in range(n):
            copy(a, 0, sibling, me).wait_recv()
        for j, chip in enumerate(chips):
            for a in range(n):
                copy(a, 4 + j, (*chip, 1 - c), me).wait_recv()
        for cp in first + passed:
            cp.wait_send()
        for cp in mine:
            cp.wait()

    any_spec = pl.BlockSpec(memory_space=pl.ANY)
    return pl.pallas_call(
        body, out_shape=[SDS((N_DEV,) + s.shape, s.dtype) for s in shards], in_specs=[any_spec] * n,
        out_specs=[any_spec] * n,
        scratch_shapes=[pltpu.SemaphoreType.DMA((n, 7)), pltpu.SemaphoreType.DMA((n, 7)), pltpu.SemaphoreType.DMA((n,))],
        name=name,
    )(*shards)


_HBM = pl.BlockSpec(memory_space=pltpu.HBM)
_SEM = pl.BlockSpec(memory_space=pltpu.SEMAPHORE)
_EFFECT = pltpu.SideEffectType.DATAFLOW_SIDE_EFFECTING


def _peer_places():
    x, y, c = _my_place()
    peers = []
    for k in range(1, N_DEV):
        px = 1 - x if (k >> 2) & 1 else x
        py = 1 - y if (k >> 1) & 1 else y
        pc = 1 - c if k & 1 else c
        peers.append(((px, py, pc), 4 * px + 2 * py + pc))
    return (x, y, c), 4 * x + 2 * y + c, peers


def _send_start(srcs, per_dest, name):
    n = len(srcs)
    lands = [lax.empty((N_DEV,) + (s.shape[1:] if per_dest else s.shape), s.dtype) for s in srcs]

    def body(*refs):
        src, land = refs[:n], refs[n:2 * n]
        outs = refs[2 * n:]
        send, recv, token = outs[:n], outs[n:2 * n], outs[4 * n]
        _, me, peers = _peer_places()
        for a in range(n):
            for peer, pidx in peers:
                pltpu.make_async_remote_copy(
                    src_ref=src[a].at[pidx] if per_dest else src[a], dst_ref=land[a].at[me], send_sem=send[a],
                    recv_sem=recv[a], device_id=peer, device_id_type=MESH).start()
        token[...] = jnp.zeros_like(token)

    hbm = lambda a: pltpu.HBM(a.shape, a.dtype)
    sem = pltpu.SemaphoreType.DMA(())
    res = pl.pallas_call(
        body, name=name,
        out_shape=tuple([sem] * (2 * n) + [hbm(s) for s in srcs] + [hbm(l) for l in lands]
                        + [SDS((SUB, LANE), F32)]),
        in_specs=[_HBM] * (2 * n),
        out_specs=tuple([_SEM] * (2 * n) + [_HBM] * (2 * n) + [pl.BlockSpec(memory_space=pltpu.VMEM)]),
        input_output_aliases={i: 2 * n + i for i in range(2 * n)},
        compiler_params=pltpu.CompilerParams(has_side_effects=_EFFECT),
    )(*[pltpu.with_memory_space_constraint(s, pltpu.HBM) for s in srcs],
      *[pltpu.with_memory_space_constraint(l, pltpu.HBM) for l in lands])
    return res[:n], res[n:2 * n], res[2 * n:3 * n], res[3 * n:4 * n], res[4 * n]


def _send_wait(send, recv, srcs, lands, after, per_dest, name):
    n = len(srcs)

    def body(*refs):
        src, land = refs[:n], refs[n:2 * n]
        send_s, recv_s = refs[2 * n:3 * n], refs[3 * n:4 * n]
        token = refs[-1]
        place, _, _ = _peer_places()
        for a in range(n):
            seven = land[a].at[pl.ds(0, N_DEV - 1)]
            copy = pltpu.make_async_remote_copy(
                src_ref=src[a].at[pl.ds(0, N_DEV - 1)] if per_dest else seven, dst_ref=seven, send_sem=send_s[a],
                recv_sem=recv_s[a], device_id=place, device_id_type=MESH)
            copy.wait_send()
            copy.wait_recv()
        token[...] = jnp.zeros_like(token)

    hbm = lambda a: pltpu.HBM(a.shape, a.dtype)
    res = pl.pallas_call(
        body, name=name,
        out_shape=tuple([hbm(s) for s in srcs] + [hbm(l) for l in lands] + [SDS((SUB, LANE), F32)]),
        in_specs=[_HBM] * (2 * n) + [_SEM] * (2 * n) + [pl.BlockSpec(memory_space=pl.ANY)],
        out_specs=tuple([_HBM] * (2 * n) + [pl.BlockSpec(memory_space=pltpu.VMEM)]),
        input_output_aliases={i: i for i in range(2 * n)},
        compiler_params=pltpu.CompilerParams(has_side_effects=_EFFECT),
    )(*srcs, *lands, *send, *recv, after)
    return res[:n], res[n:2 * n], res[2 * n]


def _adamw_math(w, g, m, v):
    m = ADAM_B1 * m + (1.0 - ADAM_B1) * g
    v = ADAM_B2 * v + (1.0 - ADAM_B2) * (g * g)
    m_hat = m / (1.0 - ADAM_B1 ** ADAM_STEP)
    v_hat = v / (1.0 - ADAM_B2 ** ADAM_STEP)
    delta = -ADAM_LR * (m_hat / (jnp.sqrt(v_hat) + ADAM_EPS) + ADAM_WD * w)
    return delta, m, v


def _row_tile(r, c, itemsize_rows):
    cap = max(SUB, (itemsize_rows // (4 * c)) // SUB * SUB)
    if r <= cap:
        return r
    best = None
    for t in range(SUB, cap + 1, SUB):
        if r % t == 0:
            best = t
    return best if best is not None else r


def _sum_adamw(landing, w, m, v, name, layer=0, prev=None, after=None):
    _, r, c = landing.shape
    tr = _row_tile(r, c, 2 << 20)
    off = layer * (r // tr)
    tail = ([] if prev is None else list(prev)) + ([] if after is None else [after])

    def body(l_ref, w_ref, m_ref, v_ref, *rest):
        g_ref, d_ref, mo_ref, vo_ref = rest[-4:]
        g = l_ref[0].astype(F32)
        for s in range(1, N_DEV):
            g = g + l_ref[s].astype(F32)
        g_ref[...] = g
        d_ref[...], mo_ref[...], vo_ref[...] = _adamw_math(w_ref[...], g, m_ref[...], v_ref[...])

    blk = pl.BlockSpec((tr, c), lambda i: (i + off, 0))
    n_prev = 0 if prev is None else 4
    return pl.pallas_call(
        body, out_shape=[SDS(w.shape, F32)] * 4, grid=(r // tr,),
        in_specs=[pl.BlockSpec((N_DEV, tr, c), lambda i: (0, i, 0)), blk, blk, blk]
        + [pl.BlockSpec(memory_space=pl.ANY)] * len(tail),
        out_specs=[blk] * 4, input_output_aliases={4 + i: i for i in range(n_prev)}, name=name,
        compiler_params=_cp(1),
    )(landing, w, m, v, *tail)


def _sum8(landing, name):
    _, r, c = landing.shape

    def body(l_ref, g_ref):
        g = l_ref[0]
        for s in range(1, N_DEV):
            g = g + l_ref[s]
        g_ref[...] = g

    return pl.pallas_call(body, out_shape=SDS((r, c), F32), name=name, compiler_params=_cp(0))(landing)


def _adamw_small(repl_pack, own_pack, P, M, V):
    table, off = [], 0
    for name, shape in _REPL.items():
        table.append((name, shape if len(shape) > 1 else (1,) + shape, 0, off // LANE))
        off += _size(shape)
    off = _REPL_ROWS * LANE
    for name, shape in _SMALL_SHARDED.items():
        table.append((name, shape, 1, off // LANE))
        off += _size(shape)
    n = len(table)

    def body(*refs):
        packs, ins, outs = refs[:2], refs[2:2 + 3 * n], refs[2 + 3 * n:]
        for p, (_, shape, which, r0) in enumerate(table):
            w_ref, m_ref, v_ref = ins[3 * p:3 * p + 3]
            g_ref, d_ref, mo_ref, vo_ref = outs[4 * p:4 * p + 4]
            pack, rows, q = packs[which], shape[-2], shape[-1] // LANE
            lead = [()]
            for dim in shape[:-2]:
                lead = [t + (i,) for t in lead for i in range(dim)]
            for li, idx in enumerate(lead):
                if q == 1:
                    dst = g_ref.at[idx] if idx else g_ref
                    dst[...] = pack[r0 + li * rows:r0 + (li + 1) * rows, :]
                    continue
                for i in range(rows):
                    for k in range(q):
                        row = r0 + (li * rows + i) * q + k
                        g_ref[idx + (slice(i, i + 1), slice(k * LANE, (k + 1) * LANE))] = pack[row:row + 1, :]
            d_ref[...], mo_ref[...], vo_ref[...] = _adamw_math(w_ref[...], g_ref[...], m_ref[...], v_ref[...])

    ins, out_shape = [], []
    for name, shape, _, _ in table:
        ins += [t[name].reshape(shape) for t in (P, M, V)]
        out_shape += [SDS(shape, F32)] * 4
    res = pl.pallas_call(body, out_shape=out_shape, name="adamw_small", compiler_params=_cp(0))(
        repl_pack, own_pack, *ins)
    dicts = ({}, {}, {}, {})
    for p, (name, shape, _, _) in enumerate(table):
        for d, arr in zip(dicts, res[4 * p:4 * p + 4]):
            d[name] = arr.reshape(P[name].shape)
    return dicts


_BIG = {
    "ab_w_in": (1, D, 320), "ab_w_out": (1, 192, D), "c_w_pw1": (1, D, 256), "c_w_pw2": (1, 128, D),
    "xa_wq": (2, 128, D), "xa_wk": (2, 128, D), "xa_wv": (2, 128, D), "xa_wo": (2, 128, D),
    "f_w_up": (2, D, 768), "f_w_down": (2, 384, D),
}
_SMALL_SHARDED = {
    "a_conv_w": (1, 4, 128), "c_norm": (1, 128), "c_b_pw1": (1, 256), "c_dw_w": (1, 31, 128), "c_dw_b": (1, 128),
    "c_ln_g": (1, 128), "c_ln_b": (1, 128), "c_b_pw2": (1, 128), "f_dw_w": (2, 3, 384),
}
_REPL = {
    "ab_norm": (1, D), "a_conv_b": (1, D), "a_gate_x_w": (1, 8, 128, 128), "a_gate_x_b": (1, D),
    "a_gate_a_w": (1, 8, 128, 128), "a_gate_a_b": (1, D), "a_lambda": (1, D), "b_group_w": (1, 4, 128, 128),
    "b_group_b": (1, 512), "b_scale": (1, 512), "xa_norm": (2, D), "xa_mem_norm": (2, D), "f_norm": (2, D),
    "f_dw_b": (2, D_FF), "final_norm": (D,),
}


def _size(shape):
    n = 1
    for s in shape:
        n *= s
    return n


_N_SS = sum(_size(s) for s in _SMALL_SHARDED.values())
_N_REPL = sum(_size(s) for s in _REPL.values())
_REPL_ROWS = -(-_N_REPL // (N_DEV * SUB * LANE)) * SUB
_SS_ROWS = _N_SS // LANE
_SMALL_ROWS = -(-(_REPL_ROWS + _SS_ROWS) // SUB) * SUB


def _pack(parts, rows):
    flat = jnp.concatenate([p.reshape(-1).astype(F32) for p in parts])
    return jnp.pad(flat, (0, rows * LANE - flat.shape[0])).reshape(rows, LANE)


def _pair_blocks(v, bw):
    lead, n = v.shape[:-1], v.shape[-1]
    return jnp.swapaxes(v.reshape(lead + (2, n // (2 * bw), bw)), -3, -2).reshape(lead + (n,))


def _unpair_blocks(v, bw):
    lead, n = v.shape[:-1], v.shape[-1]
    return jnp.swapaxes(v.reshape(lead + (n // (2 * bw), 2, bw)), -3, -2).reshape(lead + (n,))


_GROUPS = {
    ("ab", 0): (("ab_w_in", 0),),
    ("ab", 1): (("ab_w_out", 0),),
    ("xa", 0): (("xa_wq", 0), ("xa_wk", 0), ("xa_wv", 0), ("xa_wo", 0)),
    ("f", 0): (("f_w_up", 0), ("f_w_down", 0)),
    ("c", 0): (("c_w_pw1", 0), ("c_w_pw2", 0)),
    ("xa", 1): (("xa_wq", 1), ("xa_wk", 1), ("xa_wv", 1), ("xa_wo", 1)),
    ("f", 1): (("f_w_up", 1), ("f_w_down", 1)),
}
_SEND_GROUPS = _GROUPS


def _weight_layout(name, g):
    if name == "ab_w_in":
        return jnp.swapaxes(g, 0, 1).reshape(D, N_DEV * 320)
    if name in ("c_w_pw1", "f_w_up"):
        return g
    return g.reshape(N_DEV * g.shape[1], D)


def _grad_blocks(name, l, G):
    _, r, c = _BIG[name]
    if name == "ab_w_in":
        return jnp.swapaxes(G[name].reshape(D, N_DEV, 320), 0, 1)
    if name == "c_w_pw1":
        return G[name]
    if name == "f_w_up":
        return G[f"{name}{l}"]
    return (G[name] if _BIG[name][0] == 1 else G[f"{name}{l}"]).reshape(N_DEV, r, c)


def _small_layouts(sm):
    W = {}
    sm = sm.reshape(N_DEV, -1)
    off = 0
    for name, shape in _SMALL_SHARDED.items():
        n = _size(shape)
        blocks = sm[:, off:off + n].reshape((N_DEV,) + shape)
        off += n
        W[name] = jnp.moveaxis(blocks, 0, -2).reshape(shape[:-1] + (N_DEV * shape[-1],))
    W["a_conv_w"], W["c_dw_w"] = W["a_conv_w"][0], W["c_dw_w"][0]
    W["c_b_pw1"] = _pair_blocks(W["c_b_pw1"], _CW_C)
    return W


def _with_own(land, src, me, per_dest):
    own = lax.dynamic_slice_in_dim(src, me, 1, 0) if per_dest else src[None]
    return lax.dynamic_update_slice_in_dim(land, own, me, 0)


def _to_dest_major(g, shape):
    full = g.reshape(shape[:-1] + (N_DEV, shape[-1]))
    return jnp.moveaxis(full, -2, 0).reshape(N_DEV, -1)


def kernel(x, mem, ab_norm, ab_w_in, a_conv_w, a_conv_b, a_gate_x_w, a_gate_x_b, a_gate_a_w, a_gate_a_b, a_lambda, b_group_w, b_group_b, b_scale, ab_w_out, c_norm, c_w_pw1, c_b_pw1, c_dw_w, c_dw_b, c_ln_g, c_ln_b, c_w_pw2, c_b_pw2, xa_norm, xa_mem_norm, xa_wq, xa_wk, xa_wv, xa_wo, f_norm, f_w_up, f_dw_w, f_dw_b, f_w_down, final_norm, loss_target, m_ab_norm, m_ab_w_in, m_a_conv_w, m_a_conv_b, m_a_gate_x_w, m_a_gate_x_b, m_a_gate_a_w, m_a_gate_a_b, m_a_lambda, m_b_group_w, m_b_group_b, m_b_scale, m_ab_w_out, m_c_norm, m_c_w_pw1, m_c_b_pw1, m_c_dw_w, m_c_dw_b, m_c_ln_g, m_c_ln_b, m_c_w_pw2, m_c_b_pw2, m_xa_norm, m_xa_mem_norm, m_xa_wq, m_xa_wk, m_xa_wv, m_xa_wo, m_f_norm, m_f_w_up, m_f_dw_w, m_f_dw_b, m_f_w_down, m_final_norm, v_ab_norm, v_ab_w_in, v_a_conv_w, v_a_conv_b, v_a_gate_x_w, v_a_gate_x_b, v_a_gate_a_w, v_a_gate_a_b, v_a_lambda, v_b_group_w, v_b_group_b, v_b_scale, v_ab_w_out, v_c_norm, v_c_w_pw1, v_c_b_pw1, v_c_dw_w, v_c_dw_b, v_c_ln_g, v_c_ln_b, v_c_w_pw2, v_c_b_pw2, v_xa_norm, v_xa_mem_norm, v_xa_wq, v_xa_wk, v_xa_wv, v_xa_wo, v_f_norm, v_f_w_up, v_f_dw_w, v_f_dw_b, v_f_w_down, v_final_norm):
    args = dict(locals())
    P = {n: args[n] for n in _NAMES}
    M = {n: args["m_" + n] for n in _NAMES}
    V = {n: args["v_" + n] for n in _NAMES}

    me = 4 * lax.axis_index("x") + 2 * lax.axis_index("y") + lax.axis_index("c")

    in_flight = {}

    def launch(groups, tok):
        shards, n_of = [], {}
        for grp in groups:
            for name, l in _GROUPS[grp]:
                w = P[name][l] if tok is None else P[name][l] + tok
                shards.append(w.astype(BF16))
            if grp == ("ab", 0):
                shards.append(_pack([P[n] for n in _SMALL_SHARDED], _SS_ROWS + 4))
            n_of[grp] = len(shards)
        res = _send_start(shards, False, "gather_start_" + "_".join(g[0] + str(g[1]) for g in groups))
        lo = 0
        for grp in groups:
            in_flight[grp] = [r[lo:n_of[grp]] for r in res[:4]]
            lo = n_of[grp]
        return res[4][:1, :1]

    follow = {("ab", 0): [("ab", 1), ("xa", 0), ("f", 0)], ("xa", 0): [("c", 0)], ("f", 0): [("xa", 1)],
              ("c", 0): [("f", 1)]}

    def fetch(grp, after):
        send_s, recv_s, srcs, lands = in_flight.pop(grp)
        srcs, lands, tok = _send_wait(send_s, recv_s, srcs, lands, after, False, f"gather_wait_{grp[0]}{grp[1]}")
        tok = launch(follow[grp], tok[:1, :1]) if grp in follow else None
        full = [_with_own(land, src, me, False) for land, src in zip(lands, srcs)]
        out = {}
        for (name, l), g in zip(_GROUPS[grp], full):
            w = _weight_layout(name, g)
            if _BIG[name][0] == 1:
                out[name] = w
            else:
                out[name] = {l: w}
        if grp == ("ab", 0):
            out.update(_small_layouts(full[-1]))
        return out, tok

    zero = launch([("ab", 0)], None)

    pending = []

    def send(grp, G):
        members = _SEND_GROUPS[grp]
        res = _send_start([_grad_blocks(name, l, G) for name, l in members], True, f"send_{grp[0]}{grp[1]}")
        pending.append((members, res))
        return res[4][:1, :1]

    W = {n: P[n] for n in _REPL}
    W["ab_norm"] = P["ab_norm"] + zero
    W["final_norm"] = P["final_norm"].reshape(1, D)
    W["a_gate_x_w"], W["a_gate_a_w"], W["b_group_w"] = P["a_gate_x_w"][0], P["a_gate_a_w"][0], P["b_group_w"][0]
    loss, grad_x, G = _local_step(x[0], mem[0], loss_target[0], W, fetch, send)
    loss = lax.psum(loss[0, 0], ("x", "y", "c"))

    Gs = dict(G)
    Gs["c_b_pw1"] = _unpair_blocks(G["c_b_pw1"], _CW_C)
    Gs["f_dw_w"] = jnp.stack([G["f_dw_w0"], G["f_dw_w1"]])
    Gs["a_conv_w"], Gs["c_dw_w"] = G["a_conv_w"][None], G["c_dw_w"][None]
    for n in ("xa_norm", "xa_mem_norm", "f_norm", "f_dw_b"):
        Gs[n] = jnp.concatenate([G[f"{n}0"], G[f"{n}1"]], axis=0)
    for n in ("a_gate_x_w", "a_gate_a_w", "b_group_w"):
        Gs[n] = G[n][None]
    repl_flat = jnp.concatenate([Gs[n].reshape(-1) for n in _REPL])
    repl_rows = jnp.pad(repl_flat, (0, N_DEV * _REPL_ROWS * LANE - _N_REPL)).reshape(N_DEV, _REPL_ROWS, LANE)
    ss_rows = jnp.concatenate([_to_dest_major(Gs[n], s) for n, s in _SMALL_SHARDED.items()], axis=1)
    ss_rows = ss_rows.reshape(N_DEV, _SS_ROWS, LANE)
    small_pack = jnp.concatenate(
        [repl_rows, ss_rows, jnp.zeros((N_DEV, _SMALL_ROWS - _REPL_ROWS - _SS_ROWS, LANE), F32)], axis=1)
    last = _send_start([small_pack], True, "send_small")
    pending.append(((("small", 0),), last))

    def arrived(some, after, name):
        members = [m for mem_, _ in some for m in mem_]
        cat = [[a for _, res in some for a in res[i]] for i in range(4)]
        srcs, lands, _ = _send_wait(cat[0], cat[1], cat[2], cat[3], after, True, name)
        return {m: _with_own(land, src, me, True) for m, land, src in zip(members, lands, srcs)}

    out_g, out_d, out_m, out_v = {}, {}, {}, {}
    chain = [None]

    def update(name, landed):
        layers, r, c = _BIG[name]
        w2, m2, v2 = [t[name].reshape(layers * r, c) for t in (P, M, V)]
        res = None
        for l in range(layers):
            res = _sum_adamw(landed[(name, l)], w2, m2, v2, f"adamw_{name}{l}", layer=l, prev=res,
                             after=chain[0] if l == 0 else None)
        chain[0] = res[1]
        out_g[name], out_d[name], out_m[name], out_v[name] = [t.reshape(P[name].shape) for t in res]

    landed = arrived(pending[:-2], grad_x, "send_wait_early")
    for name in _BIG:
        if name != "ab_w_in":
            update(name, landed)
    landed = arrived(pending[-2:], out_v["f_w_down"], "send_wait_late")
    update("ab_w_in", landed)

    small_sum = _sum8(landed[("small", 0)], "sum_small")
    (repl_all,) = _all_gather([small_sum[:_REPL_ROWS]], "gather_small_grads")
    for out, got in zip((out_g, out_d, out_m, out_v),
                        _adamw_small(repl_all.reshape(N_DEV * _REPL_ROWS, LANE), small_sum, P, M, V)):
        out.update(got)

    return (loss, grad_x[None], *[out_g[n] for n in _NAMES], *[out_d[n] for n in _NAMES],
            *[out_m[n] for n in _NAMES], *[out_v[n] for n in _NAMES])


_NAMES = ("ab_norm", "ab_w_in", "a_conv_w", "a_conv_b", "a_gate_x_w", "a_gate_x_b", "a_gate_a_w", "a_gate_a_b",
          "a_lambda", "b_group_w", "b_group_b", "b_scale", "ab_w_out", "c_norm", "c_w_pw1", "c_b_pw1", "c_dw_w",
          "c_dw_b", "c_ln_g", "c_ln_b", "c_w_pw2", "c_b_pw2", "xa_norm", "xa_mem_norm", "xa_wq", "xa_wk", "xa_wv",
          "xa_wo", "f_norm", "f_w_up", "f_dw_w", "f_dw_b", "f_w_down", "final_norm")
```

```python
import functools

import jax
import jax.numpy as jnp
from jax import lax
from jax.experimental import pallas as pl
from jax.experimental.pallas import tpu as pltpu

F32, BF16 = jnp.float32, jnp.bfloat16
SDS = jax.ShapeDtypeStruct
MESH = pl.DeviceIdType.MESH

N_DEV = 8
D = 1024
N_MEM = 256
XA_HEADS, XA_HD = 4, 256
HD_A = 128
CONV_A, CONV_C, CONV_F = 4, 31, 3
C_RG = 8.0
POOL_WINDOWS = (2, 4, 8, 16)
D_FF = 3 * D
EPS = 1e-6
ADAM_LR, ADAM_B1, ADAM_B2, ADAM_EPS, ADAM_WD, ADAM_STEP = 0.001, 0.9, 0.999, 1e-08, 0.01, 10

LANE = 128
SUB = 8
VMEM_LIMIT = 56 * 1024 * 1024
R_SEQ = 512
R_RGLRU = 256
R_FFN = 1024
TM_ROW = 512


def _cp(n_axes):
    return pltpu.CompilerParams(dimension_semantics=("arbitrary",) * n_axes, vmem_limit_bytes=VMEM_LIMIT)


def _tile(n, pref):
    if n <= pref:
        return n
    best = None
    for t in range(LANE, pref + 1, LANE):
        if n % t == 0:
            best = t
    assert best is not None, (n, pref)
    return best


def _perm2(n):
    return (n % 2) * 4 + n // 2


_NN = (((1,), (0,)), ((), ()))
_NT = (((1,), (1,)), ((), ()))
_TN = (((0,), (0,)), ((), ()))


def _mm_call(name, grid, ab, ab_specs, dims, acc_shape, extras, outs, finish, from_ref=False):
    nk = grid[2]
    n_ab, n_ex, n_out = len(ab), len(extras), len(outs)
    use_acc = nk > 1 or from_ref

    def product(refs):
        r = lax.dot_general(refs[0][...], refs[1][...], dims, preferred_element_type=F32)
        for i in range(1, n_ab):
            r = r + lax.dot_general(refs[2 * i][...], refs[2 * i + 1][...], dims, preferred_element_type=F32)
        return r

    def body(*refs):
        rest = refs[2 * n_ab:]
        ex_refs, o_refs = rest[:n_ex], rest[n_ex:n_ex + n_out]
        first_rows = pl.program_id(0) == 0
        if not use_acc:
            finish(product(refs), ex_refs, o_refs, first_rows)
            return
        acc = rest[n_ex + n_out]
        if nk == 1:
            acc[...] = product(refs)
            finish(acc, ex_refs, o_refs, first_rows)
            return
        k = pl.program_id(2)

        @pl.when(k == 0)
        def _():
            acc[...] = jnp.zeros_like(acc)

        acc[...] += product(refs)

        @pl.when(k == nk - 1)
        def _():
            finish(acc if from_ref else acc[...], ex_refs, o_refs, first_rows)

    res = pl.pallas_call(
        body, out_shape=[o for o, _ in outs], grid=grid,
        in_specs=list(ab_specs) + [s for _, s in extras], out_specs=[s for _, s in outs],
        scratch_shapes=[pltpu.VMEM(acc_shape, F32)] if use_acc else [], name=name, compiler_params=_cp(3),
    )(*[t for pair in ab for t in pair], *[e for e, _ in extras])
    return res[0] if n_out == 1 else res


def _finish_sum(r, ex_refs, o_refs, first_rows):
    del first_rows
    for e in ex_refs:
        r = r + e[...]
    o_refs[0][...] = r.astype(o_refs[0].dtype)


def _finish_sum_norm(r, ex_refs, o_refs, first_rows):
    del first_rows
    for e in ex_refs[:-1]:
        r = r + e[...]
    o_refs[0][...] = r
    o_refs[1][...] = ((r * lax.rsqrt(jnp.mean(r * r, axis=-1, keepdims=True) + EPS)) * ex_refs[-1][...]).astype(BF16)


_EPI_ROWS = 16


def _finish_rms_bwd(r_ref, ex_refs, o_refs, first_rows):
    x_ref, g_ref, dres_ref = ex_refs
    dx_ref, dxb_ref, dg_ref = o_refs

    @pl.when(first_rows)
    def _():
        dg_ref[...] = jnp.zeros_like(dg_ref)

    gv = g_ref[...]
    inv_d = 1.0 / r_ref.shape[1]

    def step(i, dg_acc):
        groups = [pl.ds(pl.multiple_of(i * (2 * _EPI_ROWS) + u * _EPI_ROWS, _EPI_ROWS), _EPI_ROWS) for u in range(2)]
        sums = []
        for rows in groups:
            r, xf = r_ref[rows, :], x_ref[rows, :]
            sums.append((jnp.sum(xf * xf, axis=-1, keepdims=True), jnp.sum((r * gv) * xf, axis=-1, keepdims=True)))
        for rows, (sxx, sax) in zip(groups, sums):
            r, xf = r_ref[rows, :], x_ref[rows, :]
            rs = lax.rsqrt(sxx * inv_d + EPS)
            dg_acc = dg_acc + _psum8(r * (xf * rs))
            dx = rs * (r * gv) - xf * (rs * rs * (sax * rs * inv_d)) + dres_ref[rows, :]
            dx_ref[rows, :] = dx
            dxb_ref[rows, :] = dx.astype(BF16)
        return dg_acc

    dg_acc = lax.fori_loop(0, r_ref.shape[0] // (2 * _EPI_ROWS), step, jnp.zeros((SUB, r_ref.shape[1]), F32))
    dg_ref[...] += jnp.sum(dg_acc, axis=0, keepdims=True)


def _rms_bwd_io(M, tm, x, g, dres):
    rows = pl.BlockSpec((tm, D), lambda m, n, k: (m, 0))
    vec = pl.BlockSpec((1, D), lambda m, n, k: (0, 0))
    return ([(x, rows), (g, vec), (dres, rows)],
            [(SDS((M, D), F32), rows), (SDS((M, D), BF16), rows), (SDS((1, D), F32), vec)])


_K_WHOLE = 3072


def _mm_nn(a, b, *, out_dtype, name, bias=None, add=None, norm=None):
    M, K = a.shape
    tk = K if K <= _K_WHOLE else _tile(K, 1024)
    if K <= 1024 and norm is None:
        tm = _tile(M, 2048 if add is None and out_dtype == BF16 else 1024)
    else:
        tm = _tile(M, 512)
    if b.ndim == 3:
        nb, _, bw = b.shape
        N, tn, nn = nb * bw, bw, nb
        b_spec = pl.BlockSpec((None, tk, bw), lambda m, n, k: (_perm2(n), k, 0))
    else:
        N = b.shape[1]
        tn = _tile(N, 1024)
        nn = N // tn
        b_spec = pl.BlockSpec((tk, tn), lambda m, n, k: (k, n))
    tile = pl.BlockSpec((tm, tn), lambda m, n, k: (m, n))
    vec = pl.BlockSpec((1, tn), lambda m, n, k: (0, n))
    extras = ([] if bias is None else [(bias, vec)]) + ([] if add is None else [(add, tile)])
    outs, finish = [(SDS((M, N), out_dtype), tile)], _finish_sum
    if norm is not None:
        assert tn == N == D and out_dtype == F32
        extras.append((norm, vec))
        outs, finish = outs + [(SDS((M, N), BF16), tile)], _finish_sum_norm
    return _mm_call(name, (M // tm, nn, K // tk), [(a, b)], [pl.BlockSpec((tm, tk), lambda m, n, k: (m, k)), b_spec],
                    _NN, (tm, tn), extras, outs, finish)


def _mm_nt(a, b, *, out_dtype, name, add=None, rms=None):
    M, N = a.shape
    if b.ndim == 3:
        nb, Ko, bw = b.shape
        tm = _tile(M, 1024)
        tn, tk, nk = _tile(Ko, 1024), bw, nb
        b_spec = pl.BlockSpec((None, tn, bw), lambda m, n, k: (_perm2(k), n, 0))
    else:
        Ko = b.shape[0]
        tk = N if N <= _K_WHOLE else _tile(N, 1024)
        if N <= 1024 and rms is None:
            tm = _tile(M, 2048 if add is None and out_dtype == BF16 else 1024)
        else:
            tm = _tile(M, 512)
        tn = _tile(Ko, 1024)
        nk = N // tk
        b_spec = pl.BlockSpec((tn, tk), lambda m, n, k: (n, k))
    tile = pl.BlockSpec((tm, tn), lambda m, n, k: (m, n))
    extras = [] if add is None else [(add, tile)]
    outs, finish = [(SDS((M, Ko), out_dtype), tile)], _finish_sum
    if rms is not None:
        assert tn == Ko == D and add is None
        (extras, outs), finish = _rms_bwd_io(M, tm, *rms), _finish_rms_bwd
    return _mm_call(name, (M // tm, Ko // tn, nk), [(a, b)], [pl.BlockSpec((tm, tk), lambda m, n, k: (m, k)), b_spec],
                    _NT, (tm, tn), extras, outs, finish, from_ref=rms is not None)


def _mm_nt_cols(parts, b, *, name, rms):
    M = parts[0].shape[0]
    tm = _tile(M, 512)
    specs, off = [], 0
    for p in parts:
        w = p.shape[1]
        assert off % w == 0
        specs.append(pl.BlockSpec((tm, w), lambda m, n, k: (m, 0)))
        specs.append(pl.BlockSpec((D, w), functools.partial(lambda m, n, k, o: (0, o), o=off // w)))
        off += w
    extras, outs = _rms_bwd_io(M, tm, *rms)
    return _mm_call(name, (M // tm, 1, 1), [(p, b) for p in parts], specs, _NT, (tm, D), extras, outs, _finish_rms_bwd,
                    from_ref=True)


def _mm_tn(a, b, *, out_dtype, name, blocks=None):
    S, Ka = a.shape
    Nb = b.shape[1]
    tm, tk = _tile(Ka, 1024), _tile(S, 4096)
    if blocks is not None:
        bw = blocks
        tn, nn = bw, Nb // bw
        out = (SDS((nn, Ka, bw), out_dtype), pl.BlockSpec((None, tm, bw), lambda m, n, k: (_perm2(n), m, 0)))
    else:
        tn = _tile(Nb, 1024)
        nn = Nb // tn
        out = (SDS((Ka, Nb), out_dtype), pl.BlockSpec((tm, tn), lambda m, n, k: (m, n)))
    return _mm_call(name, (Ka // tm, nn, S // tk), [(a, b)],
                    [pl.BlockSpec((tk, tm), lambda m, n, k: (k, m)), pl.BlockSpec((tk, tn), lambda m, n, k: (k, n))],
                    _TN, (tm, tn), [], [out], _finish_sum)


def _row(tm, c):
    return pl.BlockSpec((tm, c), lambda i: (i, 0))


def _full(shape):
    nd = len(shape)
    return pl.BlockSpec(shape, lambda i: (0,) * nd)


def _rms_fwd(x, g, name):
    S = x.shape[0]
    tm = min(S, TM_ROW)

    def body(x_ref, g_ref, o_ref):
        xf = x_ref[...]
        r = lax.rsqrt(jnp.mean(xf * xf, axis=-1, keepdims=True) + EPS)
        o_ref[...] = ((xf * r) * g_ref[...]).astype(BF16)

    return pl.pallas_call(body, out_shape=SDS((S, D), BF16), grid=(S // tm,), in_specs=[_row(tm, D), _full((1, D))],
                          out_specs=_row(tm, D), name=name, compiler_params=_cp(1))(x, g)


def _rms_bwd(x, g, dn, dres, name):
    S = x.shape[0]
    tm = min(S, TM_ROW)
    want_dx = dres is not None

    def body(x_ref, g_ref, dn_ref, *rest):
        i = pl.program_id(0)
        dg_ref = rest[-1]

        @pl.when(i == 0)
        def _():
            dg_ref[...] = jnp.zeros_like(dg_ref)

        xf = x_ref[...]
        r = lax.rsqrt(jnp.mean(xf * xf, axis=-1, keepdims=True) + EPS)
        y = xf * r
        dn_v = dn_ref[...]
        dg_ref[...] += jnp.sum(dn_v * y, axis=0, keepdims=True)
        if want_dx:
            dres_ref, dx_ref, dxb_ref = rest[0], rest[1], rest[2]
            dy = dn_v * g_ref[...]
            dx = r * (dy - y * jnp.mean(dy * y, axis=-1, keepdims=True)) + dres_ref[...]
            dx_ref[...] = dx
            dxb_ref[...] = dx.astype(BF16)

    ins = [x, g, dn] + ([dres] if want_dx else [])
    in_specs = [_row(tm, D), _full((1, D)), _row(tm, D)] + ([_row(tm, D)] if want_dx else [])
    outs = ([SDS((S, D), F32), SDS((S, D), BF16)] if want_dx else []) + [SDS((1, D), F32)]
    out_specs = ([_row(tm, D), _row(tm, D)] if want_dx else []) + [_full((1, D))]
    return pl.pallas_call(body, out_shape=outs, grid=(S // tm,), in_specs=in_specs, out_specs=out_specs, name=name,
                          compiler_params=_cp(1))(*ins)


def _loss_head(x, g, tgt):
    S = x.shape[0]
    tm = min(S, TM_ROW)

    def body(x_ref, g_ref, t_ref, loss_ref, dx_ref, dxb_ref, dg_ref):
        i = pl.program_id(0)

        @pl.when(i == 0)
        def _():
            loss_ref[...] = jnp.zeros_like(loss_ref)
            dg_ref[...] = jnp.zeros_like(dg_ref)

        xf = x_ref[...]
        r = lax.rsqrt(jnp.mean(xf * xf, axis=-1, keepdims=True) + EPS)
        y = xf * r
        gv = g_ref[...]
        err = y * gv - t_ref[...]
        per_row = jnp.mean(err * err, axis=-1, keepdims=True)
        loss_ref[...] += 0.5 * jnp.sum(per_row, axis=0, keepdims=True)
        dn_v = err * (1.0 / D)
        dg_ref[...] += jnp.sum(dn_v * y, axis=0, keepdims=True)
        dy = dn_v * gv
        dx = r * (dy - y * jnp.mean(dy * y, axis=-1, keepdims=True))
        dx_ref[...] = dx
        dxb_ref[...] = dx.astype(BF16)

    return pl.pallas_call(
        body, out_shape=[SDS((1, 1), F32), SDS((S, D), F32), SDS((S, D), BF16), SDS((1, D), F32)], grid=(S // tm,),
        in_specs=[_row(tm, D), _full((1, D)), _row(tm, D)],
        out_specs=[_full((1, 1)), _row(tm, D), _row(tm, D), _full((1, D))], name="loss_head", compiler_params=_cp(1),
    )(x, g, tgt)


def _softmax_rows(s):
    m = jnp.max(s, axis=-1, keepdims=True)
    e = jnp.exp(s - m)
    return e / jnp.sum(e, axis=-1, keepdims=True)


def _attn_fwd(q, k, v, name):
    S = q.shape[0]
    tm = min(S, TM_ROW)
    scale = XA_HD ** -0.5

    def body(q_ref, k_ref, v_ref, o_ref):
        for h in range(XA_HEADS):
            sl = slice(h * XA_HD, (h + 1) * XA_HD)
            s = lax.dot_general(q_ref[:, sl], k_ref[:, sl], _NT, preferred_element_type=F32) * scale
            p = _softmax_rows(s)
            o_ref[:, sl] = lax.dot_general(p.astype(BF16), v_ref[:, sl], _NN, preferred_element_type=F32).astype(BF16)

    return pl.pallas_call(body, out_shape=SDS((S, D), BF16), grid=(S // tm,),
                          in_specs=[_row(tm, D), _full((N_MEM, D)), _full((N_MEM, D))], out_specs=_row(tm, D),
                          name=name, compiler_params=_cp(1))(q, k, v)


def _attn_bwd(q, k, v, do, name):
    S = q.shape[0]
    tm = min(S, TM_ROW)
    scale = XA_HD ** -0.5

    def body(q_ref, k_ref, v_ref, do_ref, dq_ref, dk_ref, dv_ref):
        i = pl.program_id(0)

        @pl.when(i == 0)
        def _():
            dk_ref[...] = jnp.zeros_like(dk_ref)
            dv_ref[...] = jnp.zeros_like(dv_ref)

        for h in range(XA_HEADS):
            sl = slice(h * XA_HD, (h + 1) * XA_HD)
            qh, kh, vh, doh = q_ref[:, sl], k_ref[:, sl], v_ref[:, sl], do_ref[:, sl]
            s = lax.dot_general(qh, kh, _NT, preferred_element_type=F32) * scale
            p = _softmax_rows(s)
            pb = p.astype(BF16)
            dv_ref[:, sl] += lax.dot_general(pb, doh, _TN, preferred_element_type=F32)
            dp = lax.dot_general(doh, vh, _NT, preferred_element_type=F32)
            ds = (p * (dp - jnp.sum(dp * p, axis=-1, keepdims=True)) * scale).astype(BF16)
            dq_ref[:, sl] = lax.dot_general(ds, kh, _NN, preferred_element_type=F32).astype(BF16)
            dk_ref[:, sl] += lax.dot_general(ds, qh, _TN, preferred_element_type=F32)

    return pl.pallas_call(
        body, out_shape=[SDS((S, D), BF16), SDS((N_MEM, D), F32), SDS((N_MEM, D), F32)], grid=(S // tm,),
        in_specs=[_row(tm, D), _full((N_MEM, D)), _full((N_MEM, D)), _row(tm, D)],
        out_specs=[_row(tm, D), _full((N_MEM, D)), _full((N_MEM, D))], name=name, compiler_params=_cp(1),
    )(q, k, v, do)


def _sigmoid(x):
    return 1.0 / (1.0 + jnp.exp(-x))


def _ln_silu_fwd(cv, g, b):
    S = cv.shape[0]
    tm = min(S, TM_ROW)

    def body(x_ref, g_ref, b_ref, o_ref):
        xf = x_ref[...]
        mu = jnp.mean(xf, axis=-1, keepdims=True)
        xc = xf - mu
        rstd = lax.rsqrt(jnp.mean(xc * xc, axis=-1, keepdims=True) + EPS)
        ln = (xc * rstd) * g_ref[...] + b_ref[...]
        o_ref[...] = (ln * _sigmoid(ln)).astype(BF16)

    return pl.pallas_call(body, out_shape=SDS((S, D), BF16), grid=(S // tm,),
                          in_specs=[_row(tm, D), _full((1, D)), _full((1, D))], out_specs=_row(tm, D),
                          name="ln_silu_fwd", compiler_params=_cp(1))(cv, g, b)


def _ln_silu_bwd(ds, cv, g, b, dx):
    S = cv.shape[0]
    tm = min(S, TM_ROW)

    def body(ds_ref, x_ref, g_ref, b_ref, dx_ref, dcv_ref, dg_ref, db_ref, db2_ref):
        i = pl.program_id(0)

        @pl.when(i == 0)
        def _():
            dg_ref[...] = jnp.zeros_like(dg_ref)
            db_ref[...] = jnp.zeros_like(db_ref)
            db2_ref[...] = jnp.zeros_like(db2_ref)

        xf = x_ref[...]
        mu = jnp.mean(xf, axis=-1, keepdims=True)
        xc = xf - mu
        rstd = lax.rsqrt(jnp.mean(xc * xc, axis=-1, keepdims=True) + EPS)
        xhat = xc * rstd
        gv = g_ref[...]
        ln = xhat * gv + b_ref[...]
        sg = _sigmoid(ln)
        dln = ds_ref[...].astype(F32) * (sg + ln * sg * (1.0 - sg))
        dg_ref[...] += jnp.sum(dln * xhat, axis=0, keepdims=True)
        db_ref[...] += jnp.sum(dln, axis=0, keepdims=True)
        db2_ref[...] += jnp.sum(dx_ref[...], axis=0, keepdims=True)
        dxh = dln * gv
        dcv_ref[...] = rstd * (dxh - jnp.mean(dxh, axis=-1, keepdims=True)
                               - xhat * jnp.mean(dxh * xhat, axis=-1, keepdims=True))

    return pl.pallas_call(
        body, out_shape=[SDS((S, D), F32), SDS((1, D), F32), SDS((1, D), F32), SDS((1, D), F32)], grid=(S // tm,),
        in_specs=[_row(tm, D), _row(tm, D), _full((1, D)), _full((1, D)), _row(tm, D)],
        out_specs=[_row(tm, D), _full((1, D)), _full((1, D)), _full((1, D))], name="ln_silu_bwd",
        compiler_params=_cp(1),
    )(ds, cv, g, b, dx)


_GELU_C, _GELU_K = 0.7978845608028654, 0.044715


def _gelu(x, with_grad=False):
    x2 = x * x
    t = jnp.tanh(_GELU_C * (x + _GELU_K * x * x2))
    gel = 0.5 * x * (1.0 + t)
    if not with_grad:
        return gel
    return gel, 0.5 * (1.0 + t) + 0.5 * x * (1.0 - t * t) * (_GELU_C * (1.0 + 3.0 * _GELU_K * x2))


def _expm1(x):
    poly = x * (1.0 + x * (0.5 + x * (1.0 / 6.0 + x * (1.0 / 24.0 + x * (1.0 / 120.0)))))
    return jnp.where(jnp.abs(x) < 0.05, poly, jnp.exp(x) - 1.0)


def _softplus(x):
    return jnp.maximum(x, 0.0) + jnp.log1p(jnp.exp(-jnp.abs(x)))


_SCAN_UNROLL = 4
_RB = 32
_HB = 16


def _sub_blocks(n_rows, n_lanes, fn):
    def step(idx, c):
        r0 = pl.multiple_of(idx * _RB, _RB)
        for lt in range(n_lanes // LANE):
            fn(r0, lt)
        return c

    lax.fori_loop(0, n_rows // _RB, step, 0)


def _lanes(lt):
    return pl.ds(lt * LANE, LANE)


def _psum8(x):
    parts = [x[i * SUB:(i + 1) * SUB] for i in range(x.shape[0] // SUB)]
    return functools.reduce(lambda p, q: p + q, parts)


def _scan_fwd(a_s, b_s, out_ref, carry_ref, n_groups):
    row = lax.broadcasted_iota(jnp.int32, (SUB, LANE), 0)
    U = _SCAN_UNROLL

    def step(gi, carry):
        base = gi * (SUB * U)
        parts = []
        for u in range(U):
            i = pl.multiple_of(base + u * SUB, SUB)
            a8, b8 = a_s[pl.ds(i, SUB), :], b_s[pl.ds(i, SUB), :]
            for s in (1, 2, 4):
                a_sh = jnp.where(row >= s, pltpu.roll(a8, s, 0), 1.0)
                b_sh = jnp.where(row >= s, pltpu.roll(b8, s, 0), 0.0)
                b8 = a8 * b_sh + b8
                a8 = a8 * a_sh
            parts.append((i, a8, b8))
        for i, a8, b8 in parts:
            h8 = a8 * carry + b8
            out_ref[pl.ds(i, SUB), :] = h8
            carry = jnp.broadcast_to(h8[SUB - 1:SUB, :], (SUB, LANE))
        return carry

    carry_ref[...] = lax.fori_loop(0, n_groups // U, step, carry_ref[...])


def _scan_bwd(a_s, b_s, out_ref, carry_ref, n_groups):
    row = lax.broadcasted_iota(jnp.int32, (SUB, LANE), 0)
    U = _SCAN_UNROLL

    def step(gi, carry):
        base = (n_groups // U - 1 - gi) * (SUB * U)
        parts = []
        for u in reversed(range(U)):
            i = pl.multiple_of(base + u * SUB, SUB)
            a8, b8 = a_s[pl.ds(i, SUB), :], b_s[pl.ds(i, SUB), :]
            for s in (1, 2, 4):
                a_sh = jnp.where(row < SUB - s, pltpu.roll(a8, SUB - s, 0), 1.0)
                b_sh = jnp.where(row < SUB - s, pltpu.roll(b8, SUB - s, 0), 0.0)
                b8 = a8 * b_sh + b8
                a8 = a8 * a_sh
            parts.append((i, a8, b8))
        for i, a8, b8 in parts:
            h8 = a8 * carry + b8
            out_ref[pl.ds(i, SUB), :] = h8
            carry = jnp.broadcast_to(h8[0:1, :], (SUB, LANE))
        return carry

    carry_ref[...] = lax.fori_loop(0, n_groups // U, step, carry_ref[...])


def _rglru_pre(xr, wgx_ref, bgx_ref, wga_ref, bga_ref, lam_ref):
    xrb = xr.astype(BF16)
    wgx, wga = wgx_ref[0].astype(BF16), wga_ref[0].astype(BF16)
    gx = _sigmoid(lax.dot_general(xrb, wgx, _NN, preferred_element_type=F32) + bgx_ref[...])
    ga = _sigmoid(lax.dot_general(xrb, wga, _NN, preferred_element_type=F32) + bga_ref[...])
    sp = _softplus(-lam_ref[...])
    log_a = -C_RG * ga * sp
    a = jnp.exp(log_a)
    mult = jnp.sqrt(-_expm1(2.0 * log_a))
    return gx, ga, sp, a, mult, xrb, wgx, wga


def _a_specs():
    vec = pl.BlockSpec((1, HD_A), lambda c, j: (0, c))
    mat = pl.BlockSpec((1, HD_A, HD_A), lambda c, j: (c, 0, 0))
    return [pl.BlockSpec((CONV_A, HD_A), lambda c, j: (0, c)), vec, mat, vec, mat, vec, vec]


def _a_fwd(zp, conv_w, conv_b, wgx, bgx, wga, bga, lam):
    S = zp.shape[0]
    R, nt = R_RGLRU, D // HD_A
    H = SUB

    def body(zg_ref, zr_ref, cw_ref, cb_ref, wgx_ref, bgx_ref, wga_ref, bga_ref, lam_ref, ya_ref, h_ref,
             ext, a_s, b_s, hc):
        j = pl.program_id(1)

        @pl.when(j == 0)
        def _():
            ext[0:H, :] = jnp.zeros((H, HD_A), F32)
            hc[...] = jnp.zeros_like(hc)

        ext[H:H + R, :] = zr_ref[...].astype(F32)
        xr = cb_ref[...]
        for k in range(CONV_A):
            xr = xr + cw_ref[k:k + 1, :] * ext[pl.ds(H - (CONV_A - 1 - k), R), :]
        gx, _, _, a, mult, _, _, _ = _rglru_pre(xr, wgx_ref, bgx_ref, wga_ref, bga_ref, lam_ref)
        a_s[...] = a
        b_s[...] = mult * (gx * xr)
        _scan_fwd(a_s, b_s, h_ref, hc, R // SUB)
        ya_ref[...] = (_gelu(zg_ref[...].astype(F32)) * h_ref[...]).astype(BF16)
        ext[0:H, :] = ext[R:R + H, :]

    return pl.pallas_call(
        body, out_shape=[SDS((S, D + D // 2), BF16), SDS((S, D), F32)], grid=(nt, S // R),
        in_specs=[pl.BlockSpec((R, HD_A), lambda c, j: (j, c)), pl.BlockSpec((R, HD_A), lambda c, j: (j, nt + c))]
        + _a_specs(),
        out_specs=[pl.BlockSpec((R, HD_A), lambda c, j: (j, c)), pl.BlockSpec((R, HD_A), lambda c, j: (j, c))],
        scratch_shapes=[pltpu.VMEM((H + R, HD_A), F32), pltpu.VMEM((R, HD_A), F32), pltpu.VMEM((R, HD_A), F32),
                        pltpu.VMEM((SUB, HD_A), F32)],
        name="rglru_fwd", compiler_params=_cp(2),
    )(zp, zp, conv_w, conv_b, wgx, bgx, wga, bga, lam)


def _a_bwd(dyab, zp, h, conv_w, conv_b, wgx, bgx, wga, bga, lam):
    S = zp.shape[0]
    R, nt, nch = R_RGLRU, D // HD_A, S // R_RGLRU
    H = SUB

    def rows(c, j):
        return (nch - 1 - j, c)

    def rows_rec(c, j):
        return (nch - 1 - j, nt + c)

    def halo(c, j):
        return (jnp.maximum((nch - 1 - j) * (R // H) - 1, 0), c)

    def halo_z(c, j):
        return (jnp.maximum((nch - 1 - j) * (R // _HB) - 1, 0), nt + c)

    def body(dy_ref, zg_ref, zr_ref, zh_ref, h_ref, hh_ref, cw_ref, cb_ref, wgx_ref, bgx_ref, wga_ref, bga_ref,
             lam_ref, dzg_ref, dzr_ref, dcw_ref, dcb_ref, dwgx_ref, dbgx_ref, dwga_ref, dbga_ref, dlam_ref,
             ext_z, ext_h, ext_mu, ext_d, a_s, b_s, muc):
        j = pl.program_id(1)
        first_chunk = (nch - 1 - j) == 0

        @pl.when(j == 0)
        def _():
            ext_mu[R:R + H, :] = jnp.zeros((H, HD_A), F32)
            ext_d[R:R + H, :] = jnp.zeros((H, HD_A), F32)
            muc[...] = jnp.zeros_like(muc)
            for r in (dcw_ref, dcb_ref, dwgx_ref, dbgx_ref, dwga_ref, dbga_ref, dlam_ref):
                r[...] = jnp.zeros_like(r)

        zg = zg_ref[...].astype(F32)
        ext_z[0:H, :] = jnp.where(first_chunk, 0.0, zh_ref[_HB - H:_HB, :].astype(F32))
        ext_z[H:H + R, :] = zr_ref[...].astype(F32)
        ext_h[0:H, :] = jnp.where(first_chunk, 0.0, hh_ref[...])
        ext_h[H:H + R, :] = h_ref[...]
        xr = cb_ref[...]
        for k in range(CONV_A):
            xr = xr + cw_ref[k:k + 1, :] * ext_z[pl.ds(H - (CONV_A - 1 - k), R), :]
        gx, ga, sp, a, mult, xrb, wgxb, wgab = _rglru_pre(xr, wgx_ref, bgx_ref, wga_ref, bga_ref, lam_ref)
        gel, dgel = _gelu(zg, with_grad=True)
        dy = dy_ref[...].astype(F32)
        dh = dy * gel
        dzg_ref[...] = (dy * h_ref[...] * dgel).astype(BF16)
        a_s[...] = a
        b_s[...] = a * dh
        _scan_bwd(a_s, b_s, ext_mu, muc, R // SUB)
        lam_t = dh + ext_mu[pl.ds(1, R), :]
        ext_mu[R:R + H, :] = ext_mu[0:H, :]
        da = lam_t * ext_h[pl.ds(H - 1, R), :]
        gxr = gx * xr
        dlog_a = da * a - (lam_t * gxr) * (a * a) / mult
        dgx = lam_t * mult * xr
        dxr = lam_t * mult * gx
        lam_v = lam_ref[...]
        dlam_ref[...] += jnp.sum(dlog_a * ga, axis=0, keepdims=True) * (C_RG * _sigmoid(-lam_v))
        dpa = (dlog_a * (-C_RG * sp)) * ga * (1.0 - ga)
        dpx = dgx * gx * (1.0 - gx)
        dbga_ref[...] += jnp.sum(dpa, axis=0, keepdims=True)
        dbgx_ref[...] += jnp.sum(dpx, axis=0, keepdims=True)
        dpab, dpxb = dpa.astype(BF16), dpx.astype(BF16)
        dwga_ref[0] += lax.dot_general(xrb, dpab, _TN, preferred_element_type=F32)
        dwgx_ref[0] += lax.dot_general(xrb, dpxb, _TN, preferred_element_type=F32)
        dxr = (dxr + lax.dot_general(dpab, wgab, _NT, preferred_element_type=F32)
               + lax.dot_general(dpxb, wgxb, _NT, preferred_element_type=F32))
        dcb_ref[...] += jnp.sum(dxr, axis=0, keepdims=True)
        ext_d[0:R, :] = dxr
        dzr = jnp.zeros((R, HD_A), F32)
        for k in range(CONV_A):
            sh = CONV_A - 1 - k
            dcw_ref[k:k + 1, :] += jnp.sum(dxr * ext_z[pl.ds(H - sh, R), :], axis=0, keepdims=True)
            dzr = dzr + cw_ref[k:k + 1, :] * ext_d[pl.ds(sh, R), :]
        dzr_ref[...] = dzr.astype(BF16)
        ext_d[R:R + H, :] = ext_d[0:H, :]

    vec_o = pl.BlockSpec((1, HD_A), lambda c, j: (0, c))
    mat_o = pl.BlockSpec((1, HD_A, HD_A), lambda c, j: (c, 0, 0))
    return pl.pallas_call(
        body,
        out_shape=[SDS((S, D), BF16), SDS((S, D), BF16), SDS((CONV_A, D), F32), SDS((1, D), F32),
                   SDS((nt, HD_A, HD_A), F32), SDS((1, D), F32), SDS((nt, HD_A, HD_A), F32), SDS((1, D), F32),
                   SDS((1, D), F32)],
        grid=(nt, nch),
        in_specs=[pl.BlockSpec((R, HD_A), rows), pl.BlockSpec((R, HD_A), rows), pl.BlockSpec((R, HD_A), rows_rec),
                  pl.BlockSpec((_HB, HD_A), halo_z), pl.BlockSpec((R, HD_A), rows),
                  pl.BlockSpec((H, HD_A), halo)] + _a_specs(),
        out_specs=[pl.BlockSpec((R, HD_A), rows), pl.BlockSpec((R, HD_A), rows),
                   pl.BlockSpec((CONV_A, HD_A), lambda c, j: (0, c)), vec_o, mat_o, vec_o, mat_o, vec_o, vec_o],
        scratch_shapes=[pltpu.VMEM((H + R, HD_A), F32), pltpu.VMEM((H + R, HD_A), F32), pltpu.VMEM((R + H, HD_A), F32),
                        pltpu.VMEM((R + H, HD_A), F32), pltpu.VMEM((R, HD_A), F32), pltpu.VMEM((R, HD_A), F32),
                        pltpu.VMEM((SUB, HD_A), F32)],
        name="rglru_bwd", compiler_params=_cp(2),
    )(dyab, zp, zp, zp, h, h, conv_w, conv_b, wgx, bgx, wga, bga, lam)


_POOL_H = 16
_POOL_T0 = 2 * D // HD_A
_POOL_Y0 = D // HD_A


def _window_sum(lv, n, lo, rows, g, ahead):
    base = 0 if ahead else SUB
    cur, win = lv[0], None
    for i, s in enumerate((1, 2, 4, 8)):
        val = cur[pl.ds(base, n), :] + cur[pl.ds(base + (s if ahead else -s), n), :]
        sel = val[lo:lo + rows]
        win = sel if win is None else jnp.where(g >= i, sel, win)
        if i < 3:
            lv[i + 1][pl.ds(base, n), :] = val
            cur = lv[i + 1]
    return win


def _pool_width(g):
    return jnp.where(g == 0, 2.0, jnp.where(g == 1, 4.0, jnp.where(g == 2, 8.0, 16.0)))


def _b_fwd(zp, yab, wg, bg, sc):
    S = zp.shape[0]
    R, H = R_SEQ, _POOL_H

    def body(z_ref, wg_ref, bg_ref, sc_ref, yab_in, yb_ref, *lv):
        del yab_in
        g, j = pl.program_id(0), pl.program_id(1)

        @pl.when(j == 0)
        def _():
            for r in lv:
                r[0:SUB, :] = jnp.zeros((SUB, HD_A), F32)
            lv[0][SUB:SUB + H, :] = jnp.zeros((H, HD_A), F32)

        u = z_ref[...].astype(F32)
        lv[0][SUB + H:SUB + H + R, :] = u
        t1 = (j * R + 1 + lax.broadcasted_iota(jnp.int32, (R, HD_A), 0)).astype(F32)
        p = _window_sum(lv, H + R, H, R, g, False) / jnp.minimum(t1, _pool_width(g)) - u
        lin = lax.dot_general(p.astype(BF16), wg_ref[0].astype(BF16), _NN, preferred_element_type=F32) + bg_ref[...]
        yb_ref[...] = (lin * sc_ref[...]).astype(BF16)
        lv[0][SUB:SUB + H, :] = lv[0][SUB + R:SUB + R + H, :]

    vec = pl.BlockSpec((1, HD_A), lambda g, j: (0, g))
    return pl.pallas_call(
        body, out_shape=SDS(yab.shape, yab.dtype), grid=(len(POOL_WINDOWS), S // R),
        in_specs=[pl.BlockSpec((R, HD_A), lambda g, j: (j, _POOL_T0 + g)),
                  pl.BlockSpec((1, HD_A, HD_A), lambda g, j: (g, 0, 0)), vec, vec, pl.BlockSpec(memory_space=pl.ANY)],
        out_specs=pl.BlockSpec((R, HD_A), lambda g, j: (j, _POOL_Y0 + g)),
        scratch_shapes=[pltpu.VMEM((SUB + H + R, HD_A), F32)] * 4, input_output_aliases={4: 0},
        name="pool_fwd", compiler_params=_cp(2),
    )(zp, wg, bg, sc, yab)


def _b_bwd(dyab, zp, wg, bg, sc):
    S = zp.shape[0]
    R, H, nch, ng = R_SEQ, _POOL_H, S // R_SEQ, len(POOL_WINDOWS)

    def body(dy_ref, z_ref, zh_ref, wg_ref, bg_ref, sc_ref, dz_ref, dwg_ref, dbg_ref, dsc_ref, *scratch):
        lu, lq = scratch[:4], scratch[4:]
        g, j = pl.program_id(0), pl.program_id(1)
        jj = nch - 1 - j

        @pl.when(j == 0)
        def _():
            for r in lu:
                r[0:SUB, :] = jnp.zeros((SUB, HD_A), F32)
            for r in lq:
                r[R + H:R + H + SUB, :] = jnp.zeros((SUB, HD_A), F32)
            lq[0][R:R + H, :] = jnp.zeros((H, HD_A), F32)
            for r in (dwg_ref, dbg_ref, dsc_ref):
                r[...] = jnp.zeros_like(r)

        u = z_ref[...].astype(F32)
        lu[0][SUB:SUB + H, :] = jnp.where(jj == 0, 0.0, zh_ref[...].astype(F32))
        lu[0][SUB + H:SUB + H + R, :] = u
        t1 = (jj * R + 1 + lax.broadcasted_iota(jnp.int32, (R, HD_A), 0)).astype(F32)
        cnt = jnp.minimum(t1, _pool_width(g))
        pb = (_window_sum(lu, H + R, H, R, g, False) / cnt - u).astype(BF16)
        wgb = wg_ref[0].astype(BF16)
        lin = lax.dot_general(pb, wgb, _NN, preferred_element_type=F32) + bg_ref[...]
        dy = dy_ref[...].astype(F32)
        dsc_ref[...] += jnp.sum(dy * lin, axis=0, keepdims=True)
        dlin = dy * sc_ref[...]
        dbg_ref[...] += jnp.sum(dlin, axis=0, keepdims=True)
        dlb = dlin.astype(BF16)
        dwg_ref[0] += lax.dot_general(pb, dlb, _TN, preferred_element_type=F32)
        dp = lax.dot_general(dlb, wgb, _NT, preferred_element_type=F32)
        lq[0][0:R, :] = dp / cnt
        dz_ref[...] = (_window_sum(lq, R + H, 0, R, g, True) - dp).astype(BF16)
        lq[0][R:R + H, :] = lq[0][0:H, :]

    vec = pl.BlockSpec((1, HD_A), lambda g, j: (0, g))
    mat = pl.BlockSpec((1, HD_A, HD_A), lambda g, j: (g, 0, 0))
    return pl.pallas_call(
        body, out_shape=[SDS((S, D // 2), BF16), SDS((ng, HD_A, HD_A), F32), SDS((1, D // 2), F32),
                         SDS((1, D // 2), F32)],
        grid=(ng, nch),
        in_specs=[pl.BlockSpec((R, HD_A), lambda g, j: (nch - 1 - j, _POOL_Y0 + g)),
                  pl.BlockSpec((R, HD_A), lambda g, j: (nch - 1 - j, _POOL_T0 + g)),
                  pl.BlockSpec((H, HD_A), lambda g, j: (jnp.maximum((nch - 1 - j) * (R // H) - 1, 0), _POOL_T0 + g)),
                  mat, vec, vec],
        out_specs=[pl.BlockSpec((R, HD_A), lambda g, j: (nch - 1 - j, g)), mat, vec, vec],
        scratch_shapes=[pltpu.VMEM((SUB + H + R, HD_A), F32)] * 8,
        name="pool_bwd", compiler_params=_cp(2),
    )(dyab, zp, zp, wg, bg, sc)


_CW_F = 768


def _f_fwd(hp, w, b, name):
    S = hp.shape[0]
    R, H, cw = min(S, R_FFN), SUB, _CW_F
    nlt = cw // LANE

    def body(h_ref, w_ref, b_ref, o_ref, gel_ref, ud_ref, ext):
        j = pl.program_id(1)

        @pl.when(j == 0)
        def _():
            ext[:, 0:H, :] = jnp.zeros((nlt, H, LANE), F32)

        def stage(r0, lt):
            ext[lt, pl.ds(pl.multiple_of(r0 + H, SUB), _RB), :] = h_ref[pl.ds(r0, _RB), _lanes(lt)].astype(F32)

        def main(r0, lt):
            ls = _lanes(lt)
            gp = b_ref[:, ls]
            for k in range(CONV_F):
                gp = gp + w_ref[k:k + 1, ls] * ext[lt, pl.ds(r0 + (H - (CONV_F - 1 - k)), _RB), :]
            up = h_ref[pl.ds(r0, _RB), _lanes(lt + nlt)].astype(F32)
            gel, dgel = _gelu(gp, with_grad=True)
            rs = pl.ds(r0, _RB)
            o_ref[rs, ls] = (gel * up).astype(BF16)
            gel_ref[rs, ls] = gel.astype(BF16)
            ud_ref[rs, ls] = (up * dgel).astype(BF16)

        _sub_blocks(R, cw, stage)
        _sub_blocks(R, cw, main)
        ext[:, 0:H, :] = ext[:, R:R + H, :]

    tile = pl.BlockSpec((R, cw), lambda c, j: (j, c))
    return pl.pallas_call(
        body, out_shape=[SDS((S, D_FF), BF16)] * 3, grid=(D_FF // cw, S // R),
        in_specs=[pl.BlockSpec((R, 2 * cw), lambda c, j: (j, c)), pl.BlockSpec((CONV_F, cw), lambda c, j: (0, c)),
                  pl.BlockSpec((1, cw), lambda c, j: (0, c))],
        out_specs=[tile] * 3,
        scratch_shapes=[pltpu.VMEM((nlt, H + R, LANE), F32)], name=name, compiler_params=_cp(2),
    )(hp, w, b)


def _f_bwd(dact, hp, gel, ud, w, name):
    S = hp.shape[0]
    R, H, cw = min(S, R_FFN), SUB, _CW_F
    nch = S // R
    nlt = cw // LANE

    def body(da_ref, h_ref, hh_ref, gel_ref, ud_ref, w_ref, dh_ref, dw_ref, db_ref, ext_g, ext_d, acc):
        j = pl.program_id(1)
        jj = nch - 1 - j

        @pl.when(j == 0)
        def _():
            ext_d[:, R:R + H, :] = jnp.zeros((nlt, H, LANE), F32)
            acc[...] = jnp.zeros_like(acc)

        for lt in range(nlt):
            ext_g[lt, 0:H, :] = jnp.where(jj == 0, 0.0, hh_ref[_HB - H:_HB, lt * LANE:(lt + 1) * LANE].astype(F32))

        def stage(r0, lt):
            ext_g[lt, pl.ds(pl.multiple_of(r0 + H, SUB), _RB), :] = h_ref[pl.ds(r0, _RB), _lanes(lt)].astype(F32)

        def first(r0, lt):
            ls, lu, rs = _lanes(lt), _lanes(lt + nlt), pl.ds(r0, _RB)
            da = da_ref[rs, ls].astype(F32)
            dh_ref[rs, lu] = (da * gel_ref[rs, ls].astype(F32)).astype(BF16)
            dgp = da * ud_ref[rs, ls].astype(F32)
            ext_d[lt, rs, :] = dgp
            acc[CONV_F * SUB:(CONV_F + 1) * SUB, ls] += _psum8(dgp)
            for k in range(CONV_F):
                tap = ext_g[lt, pl.ds(r0 + (H - (CONV_F - 1 - k)), _RB), :]
                acc[k * SUB:(k + 1) * SUB, ls] += _psum8(dgp * tap)

        def second(r0, lt):
            ls = _lanes(lt)
            dhg = w_ref[CONV_F - 1:CONV_F, ls] * ext_d[lt, pl.ds(r0, _RB), :]
            for k in range(CONV_F - 1):
                dhg = dhg + w_ref[k:k + 1, ls] * ext_d[lt, pl.ds(r0 + (CONV_F - 1 - k), _RB), :]
            dh_ref[pl.ds(r0, _RB), ls] = dhg.astype(BF16)

        _sub_blocks(R, cw, stage)
        _sub_blocks(R, cw, first)
        _sub_blocks(R, cw, second)
        ext_d[:, R:R + H, :] = ext_d[:, 0:H, :]

        @pl.when(j == nch - 1)
        def _():
            for k in range(CONV_F):
                dw_ref[k:k + 1, :] = jnp.sum(acc[k * SUB:(k + 1) * SUB, :], axis=0, keepdims=True)
            db_ref[...] = jnp.sum(acc[CONV_F * SUB:(CONV_F + 1) * SUB, :], axis=0, keepdims=True)

    rows = lambda c, j: (nch - 1 - j, c)
    return pl.pallas_call(
        body, out_shape=[SDS((S, 2 * D_FF), BF16), SDS((CONV_F, D_FF), F32), SDS((1, D_FF), F32)],
        grid=(D_FF // cw, nch),
        in_specs=[pl.BlockSpec((R, cw), rows), pl.BlockSpec((R, cw), lambda c, j: (nch - 1 - j, 2 * c)),
                  pl.BlockSpec((_HB, cw), lambda c, j: (jnp.maximum((nch - 1 - j) * (R // _HB) - 1, 0), 2 * c)),
                  pl.BlockSpec((R, cw), rows), pl.BlockSpec((R, cw), rows),
                  pl.BlockSpec((CONV_F, cw), lambda c, j: (0, c))],
        out_specs=[pl.BlockSpec((R, 2 * cw), rows), pl.BlockSpec((CONV_F, cw), lambda c, j: (0, c)),
                   pl.BlockSpec((1, cw), lambda c, j: (0, c))],
        scratch_shapes=[pltpu.VMEM((nlt, H + R, LANE), F32), pltpu.VMEM((nlt, R + H, LANE), F32),
                        pltpu.VMEM(((CONV_F + 1) * SUB, cw), F32)], name=name,
        compiler_params=_cp(2),
    )(dact, hp, hp, gel, ud, w)


_CW_C = 256
_H_C = 32


def _c_fwd(h1p, w, b):
    S = h1p.shape[0]
    R, H, cw = R_SEQ, _H_C, _CW_C
    nlt = cw // LANE

    def body(h_ref, w_ref, b_ref, o_ref, ext):
        j = pl.program_id(1)

        @pl.when(j == 0)
        def _():
            ext[:, 0:H, :] = jnp.zeros((nlt, H, LANE), F32)

        def stage(r0, lt):
            rs = pl.ds(r0, _RB)
            gate = h_ref[rs, _lanes(lt + nlt)].astype(F32)
            ext[lt, pl.ds(pl.multiple_of(r0 + H, SUB), _RB), :] = h_ref[rs, _lanes(lt)].astype(F32) * _sigmoid(gate)

        def main(r0, lt):
            ls = _lanes(lt)
            cv = b_ref[:, ls]
            for k in range(CONV_C):
                cv = cv + w_ref[k:k + 1, ls] * ext[lt, pl.ds(r0 + (H - (CONV_C - 1 - k)), _RB), :]
            o_ref[pl.ds(r0, _RB), ls] = cv

        _sub_blocks(R, cw, stage)
        _sub_blocks(R, cw, main)
        ext[:, 0:H, :] = ext[:, R:R + H, :]

    return pl.pallas_call(
        body, out_shape=SDS((S, D), F32), grid=(D // cw, S // R),
        in_specs=[pl.BlockSpec((R, 2 * cw), lambda c, j: (j, c)), pl.BlockSpec((CONV_C, cw), lambda c, j: (0, c)),
                  pl.BlockSpec((1, cw), lambda c, j: (0, c))],
        out_specs=pl.BlockSpec((R, cw), lambda c, j: (j, c)),
        scratch_shapes=[pltpu.VMEM((nlt, H + R, LANE), F32)], name="conf_conv_fwd", compiler_params=_cp(2),
    )(h1p, w, b)


def _c_bwd(dcv, h1p, w):
    S = h1p.shape[0]
    R, H, cw, nch = R_SEQ, _H_C, _CW_C, S // R_SEQ
    nlt = cw // LANE
    a_b, a_val, a_gate = CONV_C * SUB, (CONV_C + 1) * SUB, (CONV_C + 2) * SUB

    def body(dc_ref, h_ref, hh_ref, w_ref, dh_ref, dw_ref, db_ref, db1_ref, ext_u, ext_d, acc):
        j = pl.program_id(1)
        jj = nch - 1 - j

        @pl.when(j == 0)
        def _():
            ext_d[:, R:R + H, :] = jnp.zeros((nlt, H, LANE), F32)
            acc[...] = jnp.zeros_like(acc)

        for lt in range(nlt):
            ext_u[lt, 0:H, :] = jnp.where(
                jj == 0, 0.0, hh_ref[:, lt * LANE:(lt + 1) * LANE].astype(F32)
                * _sigmoid(hh_ref[:, cw + lt * LANE:cw + (lt + 1) * LANE].astype(F32)))

        def stage(r0, lt):
            rs, ls = pl.ds(r0, _RB), _lanes(lt)
            gate = h_ref[rs, _lanes(lt + nlt)].astype(F32)
            ext_u[lt, pl.ds(pl.multiple_of(r0 + H, SUB), _RB), :] = h_ref[rs, ls].astype(F32) * _sigmoid(gate)
            ext_d[lt, rs, :] = dc_ref[rs, ls]

        def first(r0, lt):
            ls = _lanes(lt)
            dc = dc_ref[pl.ds(r0, _RB), ls]
            acc[a_b:a_b + SUB, ls] += _psum8(dc)
            for k in range(CONV_C):
                tap = ext_u[lt, pl.ds(r0 + (H - (CONV_C - 1 - k)), _RB), :]
                acc[k * SUB:(k + 1) * SUB, ls] += _psum8(dc * tap)

        def second(r0, lt):
            rs, ls, lg = pl.ds(r0, _RB), _lanes(lt), _lanes(lt + nlt)
            du = w_ref[CONV_C - 1:CONV_C, ls] * ext_d[lt, rs, :]
            for k in range(CONV_C - 1):
                du = du + w_ref[k:k + 1, ls] * ext_d[lt, pl.ds(r0 + (CONV_C - 1 - k), _RB), :]
            val = h_ref[rs, ls].astype(F32)
            sg = _sigmoid(h_ref[rs, lg].astype(F32))
            dval = du * sg
            dgate = du * val * sg * (1.0 - sg)
            acc[a_val:a_val + SUB, ls] += _psum8(dval)
            acc[a_gate:a_gate + SUB, ls] += _psum8(dgate)
            dh_ref[rs, ls] = dval.astype(BF16)
            dh_ref[rs, lg] = dgate.astype(BF16)

        _sub_blocks(R, cw, stage)
        _sub_blocks(R, cw, first)
        _sub_blocks(R, cw, second)
        ext_d[:, R:R + H, :] = ext_d[:, 0:H, :]

        @pl.when(j == nch - 1)
        def _():
            for k in range(CONV_C):
                dw_ref[k:k + 1, :] = jnp.sum(acc[k * SUB:(k + 1) * SUB, :], axis=0, keepdims=True)
            db_ref[...] = jnp.sum(acc[a_b:a_b + SUB, :], axis=0, keepdims=True)
            db1_ref[:, 0:cw] = jnp.sum(acc[a_val:a_val + SUB, :], axis=0, keepdims=True)
            db1_ref[:, cw:2 * cw] = jnp.sum(acc[a_gate:a_gate + SUB, :], axis=0, keepdims=True)

    rows = lambda c, j: (nch - 1 - j, c)
    return pl.pallas_call(
        body, out_shape=[SDS((S, 2 * D), BF16), SDS((CONV_C, D), F32), SDS((1, D), F32), SDS((1, 2 * D), F32)],
        grid=(D // cw, nch),
        in_specs=[pl.BlockSpec((R, cw), rows), pl.BlockSpec((R, 2 * cw), rows),
                  pl.BlockSpec((H, 2 * cw), lambda c, j: (jnp.maximum((nch - 1 - j) * (R // H) - 1, 0), c)),
                  pl.BlockSpec((CONV_C, cw), lambda c, j: (0, c))],
        out_specs=[pl.BlockSpec((R, 2 * cw), rows), pl.BlockSpec((CONV_C, cw), lambda c, j: (0, c)),
                   pl.BlockSpec((1, cw), lambda c, j: (0, c)), pl.BlockSpec((1, 2 * cw), lambda c, j: (0, c))],
        scratch_shapes=[pltpu.VMEM((nlt, H + R, LANE), F32), pltpu.VMEM((nlt, R + H, LANE), F32),
                        pltpu.VMEM(((CONV_C + 3) * SUB, cw), F32)], name="conf_conv_bwd",
        compiler_params=_cp(2),
    )(dcv, h1p, h1p, w)


def _local_step(x, mem, tgt, W, fetch=None, send=None):
    G = {}
    W = dict(W)

    def arrive(group, after):
        if fetch is None:
            return None
        got, tok = fetch(group, after)
        for key, val in got.items():
            W[key] = {**W.get(key, {}), **val} if isinstance(val, dict) else val
        return tok

    def gain(g, tok):
        return g if tok is None else g + tok

    def sent(group):
        return None if send is None else send(group, G)

    def xattn_fwd(xin, n, l):
        tok = arrive(("xa", l), n)
        mn = _rms_fwd(mem, gain(W["xa_mem_norm"][l:l + 1], tok), f"xa_memnorm_fwd{l}")
        q = _mm_nn(n, W["xa_wq"][l], out_dtype=BF16, name=f"xa_q{l}")
        k = _mm_nn(mn, W["xa_wk"][l], out_dtype=BF16, name=f"xa_k{l}")
        v = _mm_nn(mn, W["xa_wv"][l], out_dtype=BF16, name=f"xa_v{l}")
        o = _attn_fwd(q, k, v, f"xa_attn_fwd{l}")
        xout, nout = _mm_nn(o, W["xa_wo"][l], out_dtype=F32, name=f"xa_o{l}", add=xin, norm=W["f_norm"][l:l + 1])
        return xout, nout, (xin, n, q, mn, k, v, o)

    def xattn_bwd(dx, dxb, saved, l):
        xin, n, q, mn, k, v, o = saved
        do = _mm_nt(dxb, W["xa_wo"][l], out_dtype=BF16, name=f"xa_do{l}")
        G[f"xa_wo{l}"] = _mm_tn(o, dxb, out_dtype=BF16, name=f"xa_dwo{l}")
        dq, dk, dv = _attn_bwd(q, k, v, do, f"xa_attn_bwd{l}")
        dkb, dvb = dk.astype(BF16), dv.astype(BF16)
        G[f"xa_wq{l}"] = _mm_tn(n, dq, out_dtype=BF16, name=f"xa_dwq{l}")
        G[f"xa_wk{l}"] = _mm_tn(mn, dkb, out_dtype=BF16, name=f"xa_dwk{l}")
        G[f"xa_wv{l}"] = _mm_tn(mn, dvb, out_dtype=BF16, name=f"xa_dwv{l}")
        tok = sent(("xa", l))
        dmn = _mm_nt(dkb, W["xa_wk"][l], out_dtype=F32, name=f"xa_dmn_k{l}")
        dmn = _mm_nt(dvb, W["xa_wv"][l], out_dtype=F32, name=f"xa_dmn_v{l}", add=dmn)
        (G[f"xa_mem_norm{l}"],) = _rms_bwd(mem, W["xa_mem_norm"][l:l + 1], dmn, None, f"xa_memnorm_bwd{l}")
        dx, dxb, G[f"xa_norm{l}"] = _mm_nt(dq, W["xa_wq"][l], out_dtype=F32, name=f"xa_dn{l}",
                                           rms=(xin, gain(W["xa_norm"][l:l + 1], tok), dx))
        return dx, dxb

    def ffn_fwd(xin, n, l, next_gain):
        tok = arrive(("f", l), n)
        hp = _mm_nn(n, W["f_w_up"][l], out_dtype=BF16, name=f"f_up{l}")
        act, gel, ud = _f_fwd(hp, W["f_dw_w"][l], gain(W["f_dw_b"][l:l + 1], tok), f"f_conv_fwd{l}")
        res = _mm_nn(act, W["f_w_down"][l], out_dtype=F32, name=f"f_down{l}", add=xin, norm=next_gain)
        xout, nout = res if next_gain is not None else (res, None)
        return xout, nout, (xin, n, hp, act, gel, ud)

    def ffn_bwd(dx, dxb, saved, l):
        xin, n, hp, act, gel, ud = saved
        dact = _mm_nt(dxb, W["f_w_down"][l], out_dtype=BF16, name=f"f_dact{l}")
        G[f"f_w_down{l}"] = _mm_tn(act, dxb, out_dtype=BF16, name=f"f_dwdown{l}")
        dhp, G[f"f_dw_w{l}"], G[f"f_dw_b{l}"] = _f_bwd(dact, hp, gel, ud, W["f_dw_w"][l], f"f_conv_bwd{l}")
        G[f"f_w_up{l}"] = _mm_tn(n, dhp, out_dtype=BF16, name=f"f_dwup{l}", blocks=_CW_F)
        tok = sent(("f", l))
        dx, dxb, G[f"f_norm{l}"] = _mm_nt(dhp, W["f_w_up"][l], out_dtype=F32, name=f"f_dn{l}",
                                          rms=(xin, gain(W["f_norm"][l:l + 1], tok), dx))
        return dx, dxb

    n0 = _rms_fwd(x, W["ab_norm"], "ab_norm_fwd")
    tok = arrive(("ab", 0), n0)
    a_par = (W["a_conv_w"], gain(W["a_conv_b"], tok), W["a_gate_x_w"], W["a_gate_x_b"], W["a_gate_a_w"],
             W["a_gate_a_b"], W["a_lambda"])
    b_par = (W["b_group_w"], W["b_group_b"], W["b_scale"])
    zp = _mm_nn(n0, W["ab_w_in"], out_dtype=BF16, name="ab_in")
    yab, h_a = _a_fwd(zp, *a_par)
    yab = _b_fwd(zp, yab, *b_par)
    arrive(("ab", 1), yab)
    x1, n1 = _mm_nn(yab, W["ab_w_out"], out_dtype=F32, name="ab_out", add=x, norm=W["xa_norm"][0:1])
    x2, n2, s_xa0 = xattn_fwd(x1, n1, 0)
    x3, n3, s_f0 = ffn_fwd(x2, n2, 0, W["c_norm"])
    tok = arrive(("c", 0), n3)
    h1p = _mm_nn(n3, W["c_w_pw1"], out_dtype=BF16, name="c_pw1", bias=gain(W["c_b_pw1"], tok))
    cv = _c_fwd(h1p, W["c_dw_w"], W["c_dw_b"])
    sc = _ln_silu_fwd(cv, W["c_ln_g"], W["c_ln_b"])
    x4, n4 = _mm_nn(sc, W["c_w_pw2"], out_dtype=F32, name="c_pw2", bias=W["c_b_pw2"], add=x3, norm=W["xa_norm"][1:2])
    x5, n5, s_xa1 = xattn_fwd(x4, n4, 1)
    x6, _, s_f1 = ffn_fwd(x5, n5, 1, None)
    loss, dx, dxb, G["final_norm"] = _loss_head(x6, W["final_norm"], tgt)

    dx, dxb = ffn_bwd(dx, dxb, s_f1, 1)
    dx, dxb = xattn_bwd(dx, dxb, s_xa1, 1)
    dsc = _mm_nt(dxb, W["c_w_pw2"], out_dtype=BF16, name="c_dsc")
    G["c_w_pw2"] = _mm_tn(sc, dxb, out_dtype=BF16, name="c_dwpw2")
    dcv, G["c_ln_g"], G["c_ln_b"], G["c_b_pw2"] = _ln_silu_bwd(dsc, cv, W["c_ln_g"], W["c_ln_b"], dx)
    dh1p, G["c_dw_w"], G["c_dw_b"], G["c_b_pw1"] = _c_bwd(dcv, h1p, W["c_dw_w"])
    G["c_w_pw1"] = _mm_tn(n3, dh1p, out_dtype=BF16, name="c_dwpw1", blocks=_CW_C)
    tok = sent(("c", 0))
    dx, dxb, G["c_norm"] = _mm_nt(dh1p, W["c_w_pw1"], out_dtype=F32, name="c_dn",
                                  rms=(x3, gain(W["c_norm"], tok), dx))
    dx, dxb = ffn_bwd(dx, dxb, s_f0, 0)
    dx, dxb = xattn_bwd(dx, dxb, s_xa0, 0)
    dyab = _mm_nt(dxb, W["ab_w_out"], out_dtype=BF16, name="ab_dyab")
    G["ab_w_out"] = _mm_tn(yab, dxb, out_dtype=BF16, name="ab_dwout")
    tok = sent(("ab", 1))
    a_par = (a_par[0], gain(a_par[1], tok)) + a_par[2:]
    (dzg, dzr, G["a_conv_w"], G["a_conv_b"], G["a_gate_x_w"], G["a_gate_x_b"], G["a_gate_a_w"], G["a_gate_a_b"],
     G["a_lambda"]) = _a_bwd(dyab, zp, h_a, *a_par)
    dzq, G["b_group_w"], G["b_group_b"], G["b_scale"] = _b_bwd(dyab, zp, *b_par)
    G["ab_w_in"] = jnp.concatenate(
        [_mm_tn(n0, dz, out_dtype=BF16, name=f"ab_dwin_{part}")
         for part, dz in (("gate", dzg), ("rec", dzr), ("pool", dzq))], axis=1)
    tok = sent(("ab", 0))
    dx, _, G["ab_norm"] = _mm_nt_cols([dzg, dzr, dzq], W["ab_w_in"], name="ab_dn",
                                      rms=(x, gain(W["ab_norm"], tok), dx))
    return loss, dx, G


def _my_place():
    x, y, c = lax.axis_index("x"), lax.axis_index("y"), lax.axis_index("c")
    return x, y, c


def _all_gather(shards, name):
    n = len(shards)

    def body(*refs):
        ins, outs = refs[:n], refs[n:2 * n]
        send_sems, recv_sems, local_sems = refs[2 * n:]
        x, y, c = _my_place()
        me, sibling = (x, y, c), (x, y, 1 - c)
        chips = [(1 - x, y), (x, 1 - y), (1 - x, 1 - y)]

        def slab(a, place):
            px, py, pc = place
            return outs[a].at[4 * px + 2 * py + pc]

        def copy(a, k, block, to, src=None):
            return pltpu.make_async_remote_copy(
                src_ref=slab(a, block) if src is None else src, dst_ref=slab(a, block),
                send_sem=send_sems.at[a, k], recv_sem=recv_sems.at[a, k], device_id=to, device_id_type=MESH)

        mine = [pltpu.make_async_copy(ins[a], slab(a, me), local_sems.at[a]) for a in range(n)]
        for cp in mine:
            cp.start()
        first = []
        for j, chip in enumerate(chips):
            first += [copy(a, 1 + j, me, (*chip, c), src=ins[a]) for a in range(n)]
        first += [copy(a, 0, me, sibling, src=ins[a]) for a in range(n)]
        for cp in first:
            cp.start()
        passed = []
        for j, chip in enumerate(chips):
            for a in range(n):
                copy(a, 1 + j, (*chip, c), me).wait_recv()
                cp = copy(a, 4 + j, (*chip, c), sibling)
                cp.start()
                passed.append(cp)
        for a in range(n):
            copy(a, 0, sibling, me).wait_recv()
        for j, chip in enumerate(chips):
            for a in range(n):
                copy(a, 4 + j, (*chip, 1 - c), me).wait_recv()
        for cp in first + passed:
            cp.wait_send()
        for cp in mine:
            cp.wait()

    any_spec = pl.BlockSpec(memory_space=pl.ANY)
    return pl.pallas_call(
        body, out_shape=[SDS((N_DEV,) + s.shape, s.dtype) for s in shards], in_specs=[any_spec] * n,
        out_specs=[any_spec] * n,
        scratch_shapes=[pltpu.SemaphoreType.DMA((n, 7)), pltpu.SemaphoreType.DMA((n, 7)), pltpu.SemaphoreType.DMA((n,))],
        name=name,
    )(*shards)


_HBM = pl.BlockSpec(memory_space=pltpu.HBM)
_SEM = pl.BlockSpec(memory_space=pltpu.SEMAPHORE)
_EFFECT = pltpu.SideEffectType.DATAFLOW_SIDE_EFFECTING


def _peer_places():
    x, y, c = _my_place()
    peers = []
    for k in range(1, N_DEV):
        px = 1 - x if (k >> 2) & 1 else x
        py = 1 - y if (k >> 1) & 1 else y
        pc = 1 - c if k & 1 else c
        peers.append(((px, py, pc), 4 * px + 2 * py + pc))
    return (x, y, c), 4 * x + 2 * y + c, peers


def _send_start(srcs, per_dest, name):
    n = len(srcs)
    lands = [lax.empty((N_DEV,) + (s.shape[1:] if per_dest else s.shape), s.dtype) for s in srcs]

    def body(*refs):
        src, land = refs[:n], refs[n:2 * n]
        outs = refs[2 * n:]
        send, recv, token = outs[:n], outs[n:2 * n], outs[4 * n]
        _, me, peers = _peer_places()
        for a in range(n):
            for peer, pidx in peers:
                pltpu.make_async_remote_copy(
                    src_ref=src[a].at[pidx] if per_dest else src[a], dst_ref=land[a].at[me], send_sem=send[a],
                    recv_sem=recv[a], device_id=peer, device_id_type=MESH).start()
        token[...] = jnp.zeros_like(token)

    hbm = lambda a: pltpu.HBM(a.shape, a.dtype)
    sem = pltpu.SemaphoreType.DMA(())
    res = pl.pallas_call(
        body, name=name,
        out_shape=tuple([sem] * (2 * n) + [hbm(s) for s in srcs] + [hbm(l) for l in lands]
                        + [SDS((SUB, LANE), F32)]),
        in_specs=[_HBM] * (2 * n),
        out_specs=tuple([_SEM] * (2 * n) + [_HBM] * (2 * n) + [pl.BlockSpec(memory_space=pltpu.VMEM)]),
        input_output_aliases={i: 2 * n + i for i in range(2 * n)},
        compiler_params=pltpu.CompilerParams(has_side_effects=_EFFECT),
    )(*[pltpu.with_memory_space_constraint(s, pltpu.HBM) for s in srcs],
      *[pltpu.with_memory_space_constraint(l, pltpu.HBM) for l in lands])
    return res[:n], res[n:2 * n], res[2 * n:3 * n], res[3 * n:4 * n], res[4 * n]


def _send_wait(send, recv, srcs, lands, after, per_dest, name):
    n = len(srcs)

    def body(*refs):
        src, land = refs[:n], refs[n:2 * n]
        send_s, recv_s = refs[2 * n:3 * n], refs[3 * n:4 * n]
        token = refs[-1]
        place, _, _ = _peer_places()
        for a in range(n):
            seven = land[a].at[pl.ds(0, N_DEV - 1)]
            copy = pltpu.make_async_remote_copy(
                src_ref=src[a].at[pl.ds(0, N_DEV - 1)] if per_dest else seven, dst_ref=seven, send_sem=send_s[a],
                recv_sem=recv_s[a], device_id=place, device_id_type=MESH)
            copy.wait_send()
            copy.wait_recv()
        token[...] = jnp.zeros_like(token)

    hbm = lambda a: pltpu.HBM(a.shape, a.dtype)
    res = pl.pallas_call(
        body, name=name,
        out_shape=tuple([hbm(s) for s in srcs] + [hbm(l) for l in lands] + [SDS((SUB, LANE), F32)]),
        in_specs=[_HBM] * (2 * n) + [_SEM] * (2 * n) + [pl.BlockSpec(memory_space=pl.ANY)],
        out_specs=tuple([_HBM] * (2 * n) + [pl.BlockSpec(memory_space=pltpu.VMEM)]),
        input_output_aliases={i: i for i in range(2 * n)},
        compiler_params=pltpu.CompilerParams(has_side_effects=_EFFECT),
    )(*srcs, *lands, *send, *recv, after)
    return res[:n], res[n:2 * n], res[2 * n]


def _adamw_math(w, g, m, v):
    m = ADAM_B1 * m + (1.0 - ADAM_B1) * g
    v = ADAM_B2 * v + (1.0 - ADAM_B2) * (g * g)
    m_hat = m / (1.0 - ADAM_B1 ** ADAM_STEP)
    v_hat = v / (1.0 - ADAM_B2 ** ADAM_STEP)
    delta = -ADAM_LR * (m_hat / (jnp.sqrt(v_hat) + ADAM_EPS) + ADAM_WD * w)
    return delta, m, v


def _row_tile(r, c, itemsize_rows):
    cap = max(SUB, (itemsize_rows // (4 * c)) // SUB * SUB)
    if r <= cap:
        return r
    best = None
    for t in range(SUB, cap + 1, SUB):
        if r % t == 0:
            best = t
    return best if best is not None else r


def _sum_adamw(landing, w, m, v, name, layer=0, prev=None, after=None):
    _, r, c = landing.shape
    tr = _row_tile(r, c, 2 << 20)
    off = layer * (r // tr)
    tail = ([] if prev is None else list(prev)) + ([] if after is None else [after])

    def body(l_ref, w_ref, m_ref, v_ref, *rest):
        g_ref, d_ref, mo_ref, vo_ref = rest[-4:]
        g = l_ref[0].astype(F32)
        for s in range(1, N_DEV):
            g = g + l_ref[s].astype(F32)
        g_ref[...] = g
        d_ref[...], mo_ref[...], vo_ref[...] = _adamw_math(w_ref[...], g, m_ref[...], v_ref[...])

    blk = pl.BlockSpec((tr, c), lambda i: (i + off, 0))
    n_prev = 0 if prev is None else 4
    return pl.pallas_call(
        body, out_shape=[SDS(w.shape, F32)] * 4, grid=(r // tr,),
        in_specs=[pl.BlockSpec((N_DEV, tr, c), lambda i: (0, i, 0)), blk, blk, blk]
        + [pl.BlockSpec(memory_space=pl.ANY)] * len(tail),
        out_specs=[blk] * 4, input_output_aliases={4 + i: i for i in range(n_prev)}, name=name,
        compiler_params=_cp(1),
    )(landing, w, m, v, *tail)


def _sum8(landing, name):
    _, r, c = landing.shape

    def body(l_ref, g_ref):
        g = l_ref[0]
        for s in range(1, N_DEV):
            g = g + l_ref[s]
        g_ref[...] = g

    return pl.pallas_call(body, out_shape=SDS((r, c), F32), name=name, compiler_params=_cp(0))(landing)


def _adamw_small(repl_pack, own_pack, P, M, V):
    table, off = [], 0
    for name, shape in _REPL.items():
        table.append((name, shape if len(shape) > 1 else (1,) + shape, 0, off // LANE))
        off += _size(shape)
    off = _REPL_ROWS * LANE
    for name, shape in _SMALL_SHARDED.items():
        table.append((name, shape, 1, off // LANE))
        off += _size(shape)
    n = len(table)

    def body(*refs):
        packs, ins, outs = refs[:2], refs[2:2 + 3 * n], refs[2 + 3 * n:]
        for p, (_, shape, which, r0) in enumerate(table):
            w_ref, m_ref, v_ref = ins[3 * p:3 * p + 3]
            g_ref, d_ref, mo_ref, vo_ref = outs[4 * p:4 * p + 4]
            pack, rows, q = packs[which], shape[-2], shape[-1] // LANE
            lead = [()]
            for dim in shape[:-2]:
                lead = [t + (i,) for t in lead for i in range(dim)]
            for li, idx in enumerate(lead):
                if q == 1:
                    dst = g_ref.at[idx] if idx else g_ref
                    dst[...] = pack[r0 + li * rows:r0 + (li + 1) * rows, :]
                    continue
                for i in range(rows):
                    for k in range(q):
                        row = r0 + (li * rows + i) * q + k
                        g_ref[idx + (slice(i, i + 1), slice(k * LANE, (k + 1) * LANE))] = pack[row:row + 1, :]
            d_ref[...], mo_ref[...], vo_ref[...] = _adamw_math(w_ref[...], g_ref[...], m_ref[...], v_ref[...])

    ins, out_shape = [], []
    for name, shape, _, _ in table:
        ins += [t[name].reshape(shape) for t in (P, M, V)]
        out_shape += [SDS(shape, F32)] * 4
    res = pl.pallas_call(body, out_shape=out_shape, name="adamw_small", compiler_params=_cp(0))(
        repl_pack, own_pack, *ins)
    dicts = ({}, {}, {}, {})
    for p, (name, shape, _, _) in enumerate(table):
        for d, arr in zip(dicts, res[4 * p:4 * p + 4]):
            d[name] = arr.reshape(P[name].shape)
    return dicts


_BIG = {
    "ab_w_in": (1, D, 320), "ab_w_out": (1, 192, D), "c_w_pw1": (1, D, 256), "c_w_pw2": (1, 128, D),
    "xa_wq": (2, 128, D), "xa_wk": (2, 128, D), "xa_wv": (2, 128, D), "xa_wo": (2, 128, D),
    "f_w_up": (2, D, 768), "f_w_down": (2, 384, D),
}
_SMALL_SHARDED = {
    "a_conv_w": (1, 4, 128), "c_norm": (1, 128), "c_b_pw1": (1, 256), "c_dw_w": (1, 31, 128), "c_dw_b": (1, 128),
    "c_ln_g": (1, 128), "c_ln_b": (1, 128), "c_b_pw2": (1, 128), "f_dw_w": (2, 3, 384),
}
_REPL = {
    "ab_norm": (1, D), "a_conv_b": (1, D), "a_gate_x_w": (1, 8, 128, 128), "a_gate_x_b": (1, D),
    "a_gate_a_w": (1, 8, 128, 128), "a_gate_a_b": (1, D), "a_lambda": (1, D), "b_group_w": (1, 4, 128, 128),
    "b_group_b": (1, 512), "b_scale": (1, 512), "xa_norm": (2, D), "xa_mem_norm": (2, D), "f_norm": (2, D),
    "f_dw_b": (2, D_FF), "final_norm": (D,),
}


def _size(shape):
    n = 1
    for s in shape:
        n *= s
    return n


_N_SS = sum(_size(s) for s in _SMALL_SHARDED.values())
_N_REPL = sum(_size(s) for s in _REPL.values())
_REPL_ROWS = -(-_N_REPL // (N_DEV * SUB * LANE)) * SUB
_SS_ROWS = _N_SS // LANE
_SMALL_ROWS = -(-(_REPL_ROWS + _SS_ROWS) // SUB) * SUB


def _pack(parts, rows):
    flat = jnp.concatenate([p.reshape(-1).astype(F32) for p in parts])
    return jnp.pad(flat, (0, rows * LANE - flat.shape[0])).reshape(rows, LANE)


def _pair_blocks(v, bw):
    lead, n = v.shape[:-1], v.shape[-1]
    return jnp.swapaxes(v.reshape(lead + (2, n // (2 * bw), bw)), -3, -2).reshape(lead + (n,))


def _unpair_blocks(v, bw):
    lead, n = v.shape[:-1], v.shape[-1]
    return jnp.swapaxes(v.reshape(lead + (n // (2 * bw), 2, bw)), -3, -2).reshape(lead + (n,))


_GROUPS = {
    ("ab", 0): (("ab_w_in", 0),),
    ("ab", 1): (("ab_w_out", 0),),
    ("xa", 0): (("xa_wq", 0), ("xa_wk", 0), ("xa_wv", 0), ("xa_wo", 0)),
    ("f", 0): (("f_w_up", 0), ("f_w_down", 0)),
    ("c", 0): (("c_w_pw1", 0), ("c_w_pw2", 0)),
    ("xa", 1): (("xa_wq", 1), ("xa_wk", 1), ("xa_wv", 1), ("xa_wo", 1)),
    ("f", 1): (("f_w_up", 1), ("f_w_down", 1)),
}
_SEND_GROUPS = _GROUPS


def _weight_layout(name, g):
    if name == "ab_w_in":
        return jnp.swapaxes(g, 0, 1).reshape(D, N_DEV * 320)
    if name in ("c_w_pw1", "f_w_up"):
        return g
    return g.reshape(N_DEV * g.shape[1], D)


def _grad_blocks(name, l, G):
    _, r, c = _BIG[name]
    if name == "ab_w_in":
        return jnp.swapaxes(G[name].reshape(D, N_DEV, 320), 0, 1)
    if name == "c_w_pw1":
        return G[name]
    if name == "f_w_up":
        return G[f"{name}{l}"]
    return (G[name] if _BIG[name][0] == 1 else G[f"{name}{l}"]).reshape(N_DEV, r, c)


def _small_layouts(sm):
    W = {}
    sm = sm.reshape(N_DEV, -1)
    off = 0
    for name, shape in _SMALL_SHARDED.items():
        n = _size(shape)
        blocks = sm[:, off:off + n].reshape((N_DEV,) + shape)
        off += n
        W[name] = jnp.moveaxis(blocks, 0, -2).reshape(shape[:-1] + (N_DEV * shape[-1],))
    W["a_conv_w"], W["c_dw_w"] = W["a_conv_w"][0], W["c_dw_w"][0]
    W["c_b_pw1"] = _pair_blocks(W["c_b_pw1"], _CW_C)
    return W


def _with_own(land, src, me, per_dest):
    own = lax.dynamic_slice_in_dim(src, me, 1, 0) if per_dest else src[None]
    return lax.dynamic_update_slice_in_dim(land, own, me, 0)


def _to_dest_major(g, shape):
    full = g.reshape(shape[:-1] + (N_DEV, shape[-1]))
    return jnp.moveaxis(full, -2, 0).reshape(N_DEV, -1)


def kernel(x, mem, ab_norm, ab_w_in, a_conv_w, a_conv_b, a_gate_x_w, a_gate_x_b, a_gate_a_w, a_gate_a_b, a_lambda, b_group_w, b_group_b, b_scale, ab_w_out, c_norm, c_w_pw1, c_b_pw1, c_dw_w, c_dw_b, c_ln_g, c_ln_b, c_w_pw2, c_b_pw2, xa_norm, xa_mem_norm, xa_wq, xa_wk, xa_wv, xa_wo, f_norm, f_w_up, f_dw_w, f_dw_b, f_w_down, final_norm, loss_target, m_ab_norm, m_ab_w_in, m_a_conv_w, m_a_conv_b, m_a_gate_x_w, m_a_gate_x_b, m_a_gate_a_w, m_a_gate_a_b, m_a_lambda, m_b_group_w, m_b_group_b, m_b_scale, m_ab_w_out, m_c_norm, m_c_w_pw1, m_c_b_pw1, m_c_dw_w, m_c_dw_b, m_c_ln_g, m_c_ln_b, m_c_w_pw2, m_c_b_pw2, m_xa_norm, m_xa_mem_norm, m_xa_wq, m_xa_wk, m_xa_wv, m_xa_wo, m_f_norm, m_f_w_up, m_f_dw_w, m_f_dw_b, m_f_w_down, m_final_norm, v_ab_norm, v_ab_w_in, v_a_conv_w, v_a_conv_b, v_a_gate_x_w, v_a_gate_x_b, v_a_gate_a_w, v_a_gate_a_b, v_a_lambda, v_b_group_w, v_b_group_b, v_b_scale, v_ab_w_out, v_c_norm, v_c_w_pw1, v_c_b_pw1, v_c_dw_w, v_c_dw_b, v_c_ln_g, v_c_ln_b, v_c_w_pw2, v_c_b_pw2, v_xa_norm, v_xa_mem_norm, v_xa_wq, v_xa_wk, v_xa_wv, v_xa_wo, v_f_norm, v_f_w_up, v_f_dw_w, v_f_dw_b, v_f_w_down, v_final_norm):
    args = dict(locals())
    P = {n: args[n] for n in _NAMES}
    M = {n: args["m_" + n] for n in _NAMES}
    V = {n: args["v_" + n] for n in _NAMES}

    me = 4 * lax.axis_index("x") + 2 * lax.axis_index("y") + lax.axis_index("c")

    in_flight = {}

    def launch(groups, tok):
        shards, n_of = [], {}
        for grp in groups:
            for name, l in _GROUPS[grp]:
                w = P[name][l] if tok is None else P[name][l] + tok
                shards.append(w.astype(BF16))
            if grp == ("ab", 0):
                shards.append(_pack([P[n] for n in _SMALL_SHARDED], _SS_ROWS + 4))
            n_of[grp] = len(shards)
        res = _send_start(shards, False, "gather_start_" + "_".join(g[0] + str(g[1]) for g in groups))
        lo = 0
        for grp in groups:
            in_flight[grp] = [r[lo:n_of[grp]] for r in res[:4]]
            lo = n_of[grp]
        return res[4][:1, :1]

    follow = {("ab", 0): [("ab", 1), ("xa", 0), ("f", 0)], ("xa", 0): [("c", 0)], ("f", 0): [("xa", 1)],
              ("c", 0): [("f", 1)]}

    def fetch(grp, after):
        send_s, recv_s, srcs, lands = in_flight.pop(grp)
        srcs, lands, tok = _send_wait(send_s, recv_s, srcs, lands, after, False, f"gather_wait_{grp[0]}{grp[1]}")
        tok = launch(follow[grp], tok[:1, :1]) if grp in follow else None
        full = [_with_own(land, src, me, False) for land, src in zip(lands, srcs)]
        out = {}
        for (name, l), g in zip(_GROUPS[grp], full):
            w = _weight_layout(name, g)
            if _BIG[name][0] == 1:
                out[name] = w
            else:
                out[name] = {l: w}
        if grp == ("ab", 0):
            out.update(_small_layouts(full[-1]))
        return out, tok

    zero = launch([("ab", 0)], None)

    pending = []

    def send(grp, G):
        members = _SEND_GROUPS[grp]
        res = _send_start([_grad_blocks(name, l, G) for name, l in members], True, f"send_{grp[0]}{grp[1]}")
        pending.append((members, res))
        return res[4][:1, :1]

    W = {n: P[n] for n in _REPL}
    W["ab_norm"] = P["ab_norm"] + zero
    W["final_norm"] = P["final_norm"].reshape(1, D)
    W["a_gate_x_w"], W["a_gate_a_w"], W["b_group_w"] = P["a_gate_x_w"][0], P["a_gate_a_w"][0], P["b_group_w"][0]
    loss, grad_x, G = _local_step(x[0], mem[0], loss_target[0], W, fetch, send)
    loss = lax.psum(loss[0, 0], ("x", "y", "c"))

    Gs = dict(G)
    Gs["c_b_pw1"] = _unpair_blocks(G["c_b_pw1"], _CW_C)
    Gs["f_dw_w"] = jnp.stack([G["f_dw_w0"], G["f_dw_w1"]])
    Gs["a_conv_w"], Gs["c_dw_w"] = G["a_conv_w"][None], G["c_dw_w"][None]
    for n in ("xa_norm", "xa_mem_norm", "f_norm", "f_dw_b"):
        Gs[n] = jnp.concatenate([G[f"{n}0"], G[f"{n}1"]], axis=0)
    for n in ("a_gate_x_w", "a_gate_a_w", "b_group_w"):
        Gs[n] = G[n][None]
    repl_flat = jnp.concatenate([Gs[n].reshape(-1) for n in _REPL])
    repl_rows = jnp.pad(repl_flat, (0, N_DEV * _REPL_ROWS * LANE - _N_REPL)).reshape(N_DEV, _REPL_ROWS, LANE)
    ss_rows = jnp.concatenate([_to_dest_major(Gs[n], s) for n, s in _SMALL_SHARDED.items()], axis=1)
    ss_rows = ss_rows.reshape(N_DEV, _SS_ROWS, LANE)
    small_pack = jnp.concatenate(
        [repl_rows, ss_rows, jnp.zeros((N_DEV, _SMALL_ROWS - _REPL_ROWS - _SS_ROWS, LANE), F32)], axis=1)
    last = _send_start([small_pack], True, "send_small")
    pending.append(((("small", 0),), last))

    def arrived(some, after, name):
        members = [m for mem_, _ in some for m in mem_]
        cat = [[a for _, res in some for a in res[i]] for i in range(4)]
        srcs, lands, _ = _send_wait(cat[0], cat[1], cat[2], cat[3], after, True, name)
        return {m: _with_own(land, src, me, True) for m, land, src in zip(members, lands, srcs)}

    out_g, out_d, out_m, out_v = {}, {}, {}, {}
    chain = [None]

    def update(name, landed):
        layers, r, c = _BIG[name]
        w2, m2, v2 = [t[name].reshape(layers * r, c) for t in (P, M, V)]
        res = None
        for l in range(layers):
            res = _sum_adamw(landed[(name, l)], w2, m2, v2, f"adamw_{name}{l}", layer=l, prev=res,
                             after=chain[0] if l == 0 else None)
        chain[0] = res[1]
        out_g[name], out_d[name], out_m[name], out_v[name] = [t.reshape(P[name].shape) for t in res]

    landed = arrived(pending[:-2], grad_x, "send_wait_early")
    for name in _BIG:
        if name != "ab_w_in":
            update(name, landed)
    landed = arrived(pending[-2:], out_v["f_w_down"], "send_wait_late")
    update("ab_w_in", landed)

    small_sum = _sum8(landed[("small", 0)], "sum_small")
    (repl_all,) = _all_gather([small_sum[:_REPL_ROWS]], "gather_small_grads")
    for out, got in zip((out_g, out_d, out_m, out_v),
                        _adamw_small(repl_all.reshape(N_DEV * _REPL_ROWS, LANE), small_sum, P, M, V)):
        out.update(got)

    return (loss, grad_x[None], *[out_g[n] for n in _NAMES], *[out_d[n] for n in _NAMES],
            *[out_m[n] for n in _NAMES], *[out_v[n] for n in _NAMES])


_NAMES = ("ab_norm", "ab_w_in", "a_conv_w", "a_conv_b", "a_gate_x_w", "a_gate_x_b", "a_gate_a_w", "a_gate_a_b",
          "a_lambda", "b_group_w", "b_group_b", "b_scale", "ab_w_out", "c_norm", "c_w_pw1", "c_b_pw1", "c_dw_w",
          "c_dw_b", "c_ln_g", "c_ln_b", "c_w_pw2", "c_b_pw2", "xa_norm", "xa_mem_norm", "xa_wq", "xa_wk", "xa_wv",
          "xa_wo", "f_norm", "f_w_up", "f_dw_w", "f_dw_b", "f_w_down", "final_norm")
```

```python
import functools

import jax
import jax.numpy as jnp
from jax import lax
from jax.experimental import pallas as pl
from jax.experimental.pallas import tpu as pltpu

F32, BF16 = jnp.float32, jnp.bfloat16
SDS = jax.ShapeDtypeStruct
MESH = pl.DeviceIdType.MESH

N_DEV = 8
D = 1024
N_MEM = 256
XA_HEADS, XA_HD = 4, 256
HD_A = 128
CONV_A, CONV_C, CONV_F = 4, 31, 3
C_RG = 8.0
POOL_WINDOWS = (2, 4, 8, 16)
D_FF = 3 * D
EPS = 1e-6
ADAM_LR, ADAM_B1, ADAM_B2, ADAM_EPS, ADAM_WD, ADAM_STEP = 0.001, 0.9, 0.999, 1e-08, 0.01, 10

LANE = 128
SUB = 8
VMEM_LIMIT = 56 * 1024 * 1024
R_SEQ = 512
R_RGLRU = 256
R_FFN = 1024
TM_ROW = 512


def _cp(n_axes):
    return pltpu.CompilerParams(dimension_semantics=("arbitrary",) * n_axes, vmem_limit_bytes=VMEM_LIMIT)


def _tile(n, pref):
    if n <= pref:
        return n
    best = None
    for t in range(LANE, pref + 1, LANE):
        if n % t == 0:
            best = t
    assert best is not None, (n, pref)
    return best


def _perm2(n):
    return (n % 2) * 4 + n // 2


_NN = (((1,), (0,)), ((), ()))
_NT = (((1,), (1,)), ((), ()))
_TN = (((0,), (0,)), ((), ()))


def _mm_call(name, grid, ab, ab_specs, dims, acc_shape, extras, outs, finish, from_ref=False):
    nk = grid[2]
    n_ab, n_ex, n_out = len(ab), len(extras), len(outs)
    use_acc = nk > 1 or from_ref

    def product(refs):
        r = lax.dot_general(refs[0][...], refs[1][...], dims, preferred_element_type=F32)
        for i in range(1, n_ab):
            r = r + lax.dot_general(refs[2 * i][...], refs[2 * i + 1][...], dims, preferred_element_type=F32)
        return r

    def body(*refs):
        rest = refs[2 * n_ab:]
        ex_refs, o_refs = rest[:n_ex], rest[n_ex:n_ex + n_out]
        first_rows = pl.program_id(0) == 0
        if not use_acc:
            finish(product(refs), ex_refs, o_refs, first_rows)
            return
        acc = rest[n_ex + n_out]
        if nk == 1:
            acc[...] = product(refs)
            finish(acc, ex_refs, o_refs, first_rows)
            return
        k = pl.program_id(2)

        @pl.when(k == 0)
        def _():
            acc[...] = jnp.zeros_like(acc)

        acc[...] += product(refs)

        @pl.when(k == nk - 1)
        def _():
            finish(acc if from_ref else acc[...], ex_refs, o_refs, first_rows)

    res = pl.pallas_call(
        body, out_shape=[o for o, _ in outs], grid=grid,
        in_specs=list(ab_specs) + [s for _, s in extras], out_specs=[s for _, s in outs],
        scratch_shapes=[pltpu.VMEM(acc_shape, F32)] if use_acc else [], name=name, compiler_params=_cp(3),
    )(*[t for pair in ab for t in pair], *[e for e, _ in extras])
    return res[0] if n_out == 1 else res


def _finish_sum(r, ex_refs, o_refs, first_rows):
    del first_rows
    for e in ex_refs:
        r = r + e[...]
    o_refs[0][...] = r.astype(o_refs[0].dtype)


def _finish_sum_norm(r, ex_refs, o_refs, first_rows):
    del first_rows
    for e in ex_refs[:-1]:
        r = r + e[...]
    o_refs[0][...] = r
    o_refs[1][...] = ((r * lax.rsqrt(jnp.mean(r * r, axis=-1, keepdims=True) + EPS)) * ex_refs[-1][...]).astype(BF16)


_EPI_ROWS = 16


def _finish_rms_bwd(r_ref, ex_refs, o_refs, first_rows):
    x_ref, g_ref, dres_ref = ex_refs
    dx_ref, dxb_ref, dg_ref = o_refs

    @pl.when(first_rows)
    def _():
        dg_ref[...] = jnp.zeros_like(dg_ref)

    gv = g_ref[...]
    inv_d = 1.0 / r_ref.shape[1]

    def step(i, dg_acc):
        groups = [pl.ds(pl.multiple_of(i * (2 * _EPI_ROWS) + u * _EPI_ROWS, _EPI_ROWS), _EPI_ROWS) for u in range(2)]
        sums = []
        for rows in groups:
            r, xf = r_ref[rows, :], x_ref[rows, :]
            sums.append((jnp.sum(xf * xf, axis=-1, keepdims=True), jnp.sum((r * gv) * xf, axis=-1, keepdims=True)))
        for rows, (sxx, sax) in zip(groups, sums):
            r, xf = r_ref[rows, :], x_ref[rows, :]
            rs = lax.rsqrt(sxx * inv_d + EPS)
            dg_acc = dg_acc + _psum8(r * (xf * rs))
            dx = rs * (r * gv) - xf * (rs * rs * (sax * rs * inv_d)) + dres_ref[rows, :]
            dx_ref[rows, :] = dx
            dxb_ref[rows, :] = dx.astype(BF16)
        return dg_acc

    dg_acc = lax.fori_loop(0, r_ref.shape[0] // (2 * _EPI_ROWS), step, jnp.zeros((SUB, r_ref.shape[1]), F32))
    dg_ref[...] += jnp.sum(dg_acc, axis=0, keepdims=True)


def _rms_bwd_io(M, tm, x, g, dres):
    rows = pl.BlockSpec((tm, D), lambda m, n, k: (m, 0))
    vec = pl.BlockSpec((1, D), lambda m, n, k: (0, 0))
    return ([(x, rows), (g, vec), (dres, rows)],
            [(SDS((M, D), F32), rows), (SDS((M, D), BF16), rows), (SDS((1, D), F32), vec)])


_K_WHOLE = 3072


def _mm_nn(a, b, *, out_dtype, name, bias=None, add=None, norm=None):
    M, K = a.shape
    tk = K if K <= _K_WHOLE else _tile(K, 1024)
    if K <= 1024 and norm is None:
        tm = _tile(M, 2048 if add is None and out_dtype == BF16 else 1024)
    else:
        tm = _tile(M, 512)
    if b.ndim == 3:
        nb, _, bw = b.shape
        N, tn, nn = nb * bw, bw, nb
        b_spec = pl.BlockSpec((None, tk, bw), lambda m, n, k: (_perm2(n), k, 0))
    else:
        N = b.shape[1]
        tn = _tile(N, 1024)
        nn = N // tn
        b_spec = pl.BlockSpec((tk, tn), lambda m, n, k: (k, n))
    tile = pl.BlockSpec((tm, tn), lambda m, n, k: (m, n))
    vec = pl.BlockSpec((1, tn), lambda m, n, k: (0, n))
    extras = ([] if bias is None else [(bias, vec)]) + ([] if add is None else [(add, tile)])
    outs, finish = [(SDS((M, N), out_dtype), tile)], _finish_sum
    if norm is not None:
        assert tn == N == D and out_dtype == F32
        extras.append((norm, vec))
        outs, finish = outs + [(SDS((M, N), BF16), tile)], _finish_sum_norm
    return _mm_call(name, (M // tm, nn, K // tk), [(a, b)], [pl.BlockSpec((tm, tk), lambda m, n, k: (m, k)), b_spec],
                    _NN, (tm, tn), extras, outs, finish)


def _mm_nt(a, b, *, out_dtype, name, add=None, rms=None):
    M, N = a.shape
    if b.ndim == 3:
        nb, Ko, bw = b.shape
        tm = _tile(M, 1024)
        tn, tk, nk = _tile(Ko, 1024), bw, nb
        b_spec = pl.BlockSpec((None, tn, bw), lambda m, n, k: (_perm2(k), n, 0))
    else:
        Ko = b.shape[0]
        tk = N if N <= _K_WHOLE else _tile(N, 1024)
        if N <= 1024 and rms is None:
            tm = _tile(M, 2048 if add is None and out_dtype == BF16 else 1024)
        else:
            tm = _tile(M, 512)
        tn = _tile(Ko, 1024)
        nk = N // tk
        b_spec = pl.BlockSpec((tn, tk), lambda m, n, k: (n, k))
    tile = pl.BlockSpec((tm, tn), lambda m, n, k: (m, n))
    extras = [] if add is None else [(add, tile)]
    outs, finish = [(SDS((M, Ko), out_dtype), tile)], _finish_sum
    if rms is not None:
        assert tn == Ko == D and add is None
        (extras, outs), finish = _rms_bwd_io(M, tm, *rms), _finish_rms_bwd
    return _mm_call(name, (M // tm, Ko // tn, nk), [(a, b)], [pl.BlockSpec((tm, tk), lambda m, n, k: (m, k)), b_spec],
                    _NT, (tm, tn), extras, outs, finish, from_ref=rms is not None)


def _mm_nt_cols(parts, b, *, name, rms):
    M = parts[0].shape[0]
    tm = _tile(M, 512)
    specs, off = [], 0
    for p in parts:
        w = p.shape[1]
        assert off % w == 0
        specs.append(pl.BlockSpec((tm, w), lambda m, n, k: (m, 0)))
        specs.append(pl.BlockSpec((D, w), functools.partial(lambda m, n, k, o: (0, o), o=off // w)))
        off += w
    extras, outs = _rms_bwd_io(M, tm, *rms)
    return _mm_call(name, (M // tm, 1, 1), [(p, b) for p in parts], specs, _NT, (tm, D), extras, outs, _finish_rms_bwd,
                    from_ref=True)


def _mm_tn(a, b, *, out_dtype, name, blocks=None):
    S, Ka = a.shape
    Nb = b.shape[1]
    tm = _tile(Ka, 1024)
    if blocks is not None:
        bw = blocks
        tn, nn = bw, Nb // bw
        out = (SDS((nn, Ka, bw), out_dtype), pl.BlockSpec((None, tm, bw), lambda m, n, k: (_perm2(n), m, 0)))
    else:
        tn = _tile(Nb, 1024)
        nn = Nb // tn
        out = (SDS((Ka, Nb), out_dtype), pl.BlockSpec((tm, tn), lambda m, n, k: (m, n)))
    steps = (Ka // tm) * nn
    tk = _tile(S, 4096 if steps >= 4 else 2048 if steps >= 2 else 1024)
    return _mm_call(name, (Ka // tm, nn, S // tk), [(a, b)],
                    [pl.BlockSpec((tk, tm), lambda m, n, k: (k, m)), pl.BlockSpec((tk, tn), lambda m, n, k: (k, n))],
                    _TN, (tm, tn), [], [out], _finish_sum)


def _row(tm, c):
    return pl.BlockSpec((tm, c), lambda i: (i, 0))


def _full(shape):
    nd = len(shape)
    return pl.BlockSpec(shape, lambda i: (0,) * nd)


def _rms_fwd(x, g, name):
    S = x.shape[0]
    tm = min(S, TM_ROW)

    def body(x_ref, g_ref, o_ref):
        xf = x_ref[...]
        r = lax.rsqrt(jnp.mean(xf * xf, axis=-1, keepdims=True) + EPS)
        o_ref[...] = ((xf * r) * g_ref[...]).astype(BF16)

    return pl.pallas_call(body, out_shape=SDS((S, D), BF16), grid=(S // tm,), in_specs=[_row(tm, D), _full((1, D))],
                          out_specs=_row(tm, D), name=name, compiler_params=_cp(1))(x, g)


def _rms_bwd(x, g, dn, dres, name):
    S = x.shape[0]
    tm = min(S, TM_ROW)
    want_dx = dres is not None

    def body(x_ref, g_ref, dn_ref, *rest):
        i = pl.program_id(0)
        dg_ref = rest[-1]

        @pl.when(i == 0)
        def _():
            dg_ref[...] = jnp.zeros_like(dg_ref)

        xf = x_ref[...]
        r = lax.rsqrt(jnp.mean(xf * xf, axis=-1, keepdims=True) + EPS)
        y = xf * r
        dn_v = dn_ref[...]
        dg_ref[...] += jnp.sum(dn_v * y, axis=0, keepdims=True)
        if want_dx:
            dres_ref, dx_ref, dxb_ref = rest[0], rest[1], rest[2]
            dy = dn_v * g_ref[...]
            dx = r * (dy - y * jnp.mean(dy * y, axis=-1, keepdims=True)) + dres_ref[...]
            dx_ref[...] = dx
            dxb_ref[...] = dx.astype(BF16)

    ins = [x, g, dn] + ([dres] if want_dx else [])
    in_specs = [_row(tm, D), _full((1, D)), _row(tm, D)] + ([_row(tm, D)] if want_dx else [])
    outs = ([SDS((S, D), F32), SDS((S, D), BF16)] if want_dx else []) + [SDS((1, D), F32)]
    out_specs = ([_row(tm, D), _row(tm, D)] if want_dx else []) + [_full((1, D))]
    return pl.pallas_call(body, out_shape=outs, grid=(S // tm,), in_specs=in_specs, out_specs=out_specs, name=name,
                          compiler_params=_cp(1))(*ins)


def _loss_head(x, g, tgt):
    S = x.shape[0]
    tm = min(S, TM_ROW)

    def body(x_ref, g_ref, t_ref, loss_ref, dx_ref, dxb_ref, dg_ref):
        i = pl.program_id(0)

        @pl.when(i == 0)
        def _():
            loss_ref[...] = jnp.zeros_like(loss_ref)
            dg_ref[...] = jnp.zeros_like(dg_ref)

        xf = x_ref[...]
        r = lax.rsqrt(jnp.mean(xf * xf, axis=-1, keepdims=True) + EPS)
        y = xf * r
        gv = g_ref[...]
        err = y * gv - t_ref[...]
        per_row = jnp.mean(err * err, axis=-1, keepdims=True)
        loss_ref[...] += 0.5 * jnp.sum(per_row, axis=0, keepdims=True)
        dn_v = err * (1.0 / D)
        dg_ref[...] += jnp.sum(dn_v * y, axis=0, keepdims=True)
        dy = dn_v * gv
        dx = r * (dy - y * jnp.mean(dy * y, axis=-1, keepdims=True))
        dx_ref[...] = dx
        dxb_ref[...] = dx.astype(BF16)

    return pl.pallas_call(
        body, out_shape=[SDS((1, 1), F32), SDS((S, D), F32), SDS((S, D), BF16), SDS((1, D), F32)], grid=(S // tm,),
        in_specs=[_row(tm, D), _full((1, D)), _row(tm, D)],
        out_specs=[_full((1, 1)), _row(tm, D), _row(tm, D), _full((1, D))], name="loss_head", compiler_params=_cp(1),
    )(x, g, tgt)


def _softmax_rows(s):
    m = jnp.max(s, axis=-1, keepdims=True)
    e = jnp.exp(s - m)
    return e / jnp.sum(e, axis=-1, keepdims=True)


def _attn_fwd(q, k, v, name):
    S = q.shape[0]
    tm = min(S, TM_ROW)
    scale = XA_HD ** -0.5

    def body(q_ref, k_ref, v_ref, o_ref):
        for h in range(XA_HEADS):
            sl = slice(h * XA_HD, (h + 1) * XA_HD)
            s = lax.dot_general(q_ref[:, sl], k_ref[:, sl], _NT, preferred_element_type=F32) * scale
            p = _softmax_rows(s)
            o_ref[:, sl] = lax.dot_general(p.astype(BF16), v_ref[:, sl], _NN, preferred_element_type=F32).astype(BF16)

    return pl.pallas_call(body, out_shape=SDS((S, D), BF16), grid=(S // tm,),
                          in_specs=[_row(tm, D), _full((N_MEM, D)), _full((N_MEM, D))], out_specs=_row(tm, D),
                          name=name, compiler_params=_cp(1))(q, k, v)


def _attn_bwd(q, k, v, do, name):
    S = q.shape[0]
    tm = min(S, TM_ROW)
    scale = XA_HD ** -0.5

    def body(q_ref, k_ref, v_ref, do_ref, dq_ref, dk_ref, dv_ref):
        i = pl.program_id(0)

        @pl.when(i == 0)
        def _():
            dk_ref[...] = jnp.zeros_like(dk_ref)
            dv_ref[...] = jnp.zeros_like(dv_ref)

        for h in range(XA_HEADS):
            sl = slice(h * XA_HD, (h + 1) * XA_HD)
            qh, kh, vh, doh = q_ref[:, sl], k_ref[:, sl], v_ref[:, sl], do_ref[:, sl]
            s = lax.dot_general(qh, kh, _NT, preferred_element_type=F32) * scale
            p = _softmax_rows(s)
            pb = p.astype(BF16)
            dv_ref[:, sl] += lax.dot_general(pb, doh, _TN, preferred_element_type=F32)
            dp = lax.dot_general(doh, vh, _NT, preferred_element_type=F32)
            ds = (p * (dp - jnp.sum(dp * p, axis=-1, keepdims=True)) * scale).astype(BF16)
            dq_ref[:, sl] = lax.dot_general(ds, kh, _NN, preferred_element_type=F32).astype(BF16)
            dk_ref[:, sl] += lax.dot_general(ds, qh, _TN, preferred_element_type=F32)

    return pl.pallas_call(
        body, out_shape=[SDS((S, D), BF16), SDS((N_MEM, D), F32), SDS((N_MEM, D), F32)], grid=(S // tm,),
        in_specs=[_row(tm, D), _full((N_MEM, D)), _full((N_MEM, D)), _row(tm, D)],
        out_specs=[_row(tm, D), _full((N_MEM, D)), _full((N_MEM, D))], name=name, compiler_params=_cp(1),
    )(q, k, v, do)


def _sigmoid(x):
    return 1.0 / (1.0 + jnp.exp(-x))


def _ln_silu_fwd(cv, g, b):
    S = cv.shape[0]
    tm = min(S, TM_ROW)

    def body(x_ref, g_ref, b_ref, o_ref):
        xf = x_ref[...]
        mu = jnp.mean(xf, axis=-1, keepdims=True)
        xc = xf - mu
        rstd = lax.rsqrt(jnp.mean(xc * xc, axis=-1, keepdims=True) + EPS)
        ln = (xc * rstd) * g_ref[...] + b_ref[...]
        o_ref[...] = (ln * _sigmoid(ln)).astype(BF16)

    return pl.pallas_call(body, out_shape=SDS((S, D), BF16), grid=(S // tm,),
                          in_specs=[_row(tm, D), _full((1, D)), _full((1, D))], out_specs=_row(tm, D),
                          name="ln_silu_fwd", compiler_params=_cp(1))(cv, g, b)


def _ln_silu_bwd(ds, cv, g, b, dx):
    S = cv.shape[0]
    tm = min(S, TM_ROW)

    def body(ds_ref, x_ref, g_ref, b_ref, dx_ref, dcv_ref, dg_ref, db_ref, db2_ref):
        i = pl.program_id(0)

        @pl.when(i == 0)
        def _():
            dg_ref[...] = jnp.zeros_like(dg_ref)
            db_ref[...] = jnp.zeros_like(db_ref)
            db2_ref[...] = jnp.zeros_like(db2_ref)

        xf = x_ref[...]
        mu = jnp.mean(xf, axis=-1, keepdims=True)
        xc = xf - mu
        rstd = lax.rsqrt(jnp.mean(xc * xc, axis=-1, keepdims=True) + EPS)
        xhat = xc * rstd
        gv = g_ref[...]
        ln = xhat * gv + b_ref[...]
        sg = _sigmoid(ln)
        dln = ds_ref[...].astype(F32) * (sg + ln * sg * (1.0 - sg))
        dg_ref[...] += jnp.sum(dln * xhat, axis=0, keepdims=True)
        db_ref[...] += jnp.sum(dln, axis=0, keepdims=True)
        db2_ref[...] += jnp.sum(dx_ref[...], axis=0, keepdims=True)
        dxh = dln * gv
        dcv_ref[...] = rstd * (dxh - jnp.mean(dxh, axis=-1, keepdims=True)
                               - xhat * jnp.mean(dxh * xhat, axis=-1, keepdims=True))

    return pl.pallas_call(
        body, out_shape=[SDS((S, D), F32), SDS((1, D), F32), SDS((1, D), F32), SDS((1, D), F32)], grid=(S // tm,),
        in_specs=[_row(tm, D), _row(tm, D), _full((1, D)), _full((1, D)), _row(tm, D)],
        out_specs=[_row(tm, D), _full((1, D)), _full((1, D)), _full((1, D))], name="ln_silu_bwd",
        compiler_params=_cp(1),
    )(ds, cv, g, b, dx)


_GELU_C, _GELU_K = 0.7978845608028654, 0.044715


def _gelu(x, with_grad=False):
    x2 = x * x
    t = jnp.tanh(_GELU_C * (x + _GELU_K * x * x2))
    gel = 0.5 * x * (1.0 + t)
    if not with_grad:
        return gel
    return gel, 0.5 * (1.0 + t) + 0.5 * x * (1.0 - t * t) * (_GELU_C * (1.0 + 3.0 * _GELU_K * x2))


def _expm1(x):
    poly = x * (1.0 + x * (0.5 + x * (1.0 / 6.0 + x * (1.0 / 24.0 + x * (1.0 / 120.0)))))
    return jnp.where(jnp.abs(x) < 0.05, poly, jnp.exp(x) - 1.0)


def _softplus(x):
    return jnp.maximum(x, 0.0) + jnp.log1p(jnp.exp(-jnp.abs(x)))


_SCAN_UNROLL = 8
_RB = 32
_HB = 16


def _sub_blocks(n_rows, n_lanes, fn):
    def step(idx, c):
        r0 = pl.multiple_of(idx * _RB, _RB)
        for lt in range(n_lanes // LANE):
            fn(r0, lt)
        return c

    lax.fori_loop(0, n_rows // _RB, step, 0)


def _lanes(lt):
    return pl.ds(lt * LANE, LANE)


def _psum8(x):
    parts = [x[i * SUB:(i + 1) * SUB] for i in range(x.shape[0] // SUB)]
    return functools.reduce(lambda p, q: p + q, parts)


def _scan_fwd(a_s, b_s, out_ref, carry_ref, n_groups):
    row = lax.broadcasted_iota(jnp.int32, (SUB, LANE), 0)
    U = _SCAN_UNROLL

    def step(gi, carry):
        base = gi * (SUB * U)
        parts = []
        for u in range(U):
            i = pl.multiple_of(base + u * SUB, SUB)
            a8, b8 = a_s[pl.ds(i, SUB), :], b_s[pl.ds(i, SUB), :]
            for s in (1, 2, 4):
                a_sh = jnp.where(row >= s, pltpu.roll(a8, s, 0), 1.0)
                b_sh = jnp.where(row >= s, pltpu.roll(b8, s, 0), 0.0)
                b8 = a8 * b_sh + b8
                a8 = a8 * a_sh
            parts.append((i, a8, b8))
        for i, a8, b8 in parts:
            h8 = a8 * carry + b8
            out_ref[pl.ds(i, SUB), :] = h8
            carry = jnp.broadcast_to(h8[SUB - 1:SUB, :], (SUB, LANE))
        return carry

    carry_ref[...] = lax.fori_loop(0, n_groups // U, step, carry_ref[...])


def _scan_bwd(a_s, b_s, out_ref, carry_ref, n_groups):
    row = lax.broadcasted_iota(jnp.int32, (SUB, LANE), 0)
    U = _SCAN_UNROLL

    def step(gi, carry):
        base = (n_groups // U - 1 - gi) * (SUB * U)
        parts = []
        for u in reversed(range(U)):
            i = pl.multiple_of(base + u * SUB, SUB)
            a8, b8 = a_s[pl.ds(i, SUB), :], b_s[pl.ds(i, SUB), :]
            for s in (1, 2, 4):
                a_sh = jnp.where(row < SUB - s, pltpu.roll(a8, SUB - s, 0), 1.0)
                b_sh = jnp.where(row < SUB - s, pltpu.roll(b8, SUB - s, 0), 0.0)
                b8 = a8 * b_sh + b8
                a8 = a8 * a_sh
            parts.append((i, a8, b8))
        for i, a8, b8 in parts:
            h8 = a8 * carry + b8
            out_ref[pl.ds(i, SUB), :] = h8
            carry = jnp.broadcast_to(h8[0:1, :], (SUB, LANE))
        return carry

    carry_ref[...] = lax.fori_loop(0, n_groups // U, step, carry_ref[...])


def _rglru_pre(xr, wgx_ref, bgx_ref, wga_ref, bga_ref, lam_ref):
    xrb = xr.astype(BF16)
    wgx, wga = wgx_ref[0].astype(BF16), wga_ref[0].astype(BF16)
    gx = _sigmoid(lax.dot_general(xrb, wgx, _NN, preferred_element_type=F32) + bgx_ref[...])
    ga = _sigmoid(lax.dot_general(xrb, wga, _NN, preferred_element_type=F32) + bga_ref[...])
    sp = _softplus(-lam_ref[...])
    log_a = -C_RG * ga * sp
    a = jnp.exp(log_a)
    mult = jnp.sqrt(-_expm1(2.0 * log_a))
    return gx, ga, sp, a, mult, xrb, wgx, wga


def _a_specs():
    vec = pl.BlockSpec((1, HD_A), lambda c, j: (0, c))
    mat = pl.BlockSpec((1, HD_A, HD_A), lambda c, j: (c, 0, 0))
    return [pl.BlockSpec((CONV_A, HD_A), lambda c, j: (0, c)), vec, mat, vec, mat, vec, vec]


def _a_fwd(zp, conv_w, conv_b, wgx, bgx, wga, bga, lam):
    S = zp.shape[0]
    R, nt = R_RGLRU, D // HD_A
    H = SUB

    def body(zg_ref, zr_ref, cw_ref, cb_ref, wgx_ref, bgx_ref, wga_ref, bga_ref, lam_ref, ya_ref, h_ref,
             ext, a_s, b_s, hc):
        j = pl.program_id(1)

        @pl.when(j == 0)
        def _():
            ext[0:H, :] = jnp.zeros((H, HD_A), F32)
            hc[...] = jnp.zeros_like(hc)

        ext[H:H + R, :] = zr_ref[...].astype(F32)
        xr = cb_ref[...]
        for k in range(CONV_A):
            xr = xr + cw_ref[k:k + 1, :] * ext[pl.ds(H - (CONV_A - 1 - k), R), :]
        gx, _, _, a, mult, _, _, _ = _rglru_pre(xr, wgx_ref, bgx_ref, wga_ref, bga_ref, lam_ref)
        a_s[...] = a
        b_s[...] = mult * (gx * xr)
        _scan_fwd(a_s, b_s, h_ref, hc, R // SUB)
        ya_ref[...] = (_gelu(zg_ref[...].astype(F32)) * h_ref[...]).astype(BF16)
        ext[0:H, :] = ext[R:R + H, :]

    return pl.pallas_call(
        body, out_shape=[SDS((S, D + D // 2), BF16), SDS((S, D), F32)], grid=(nt, S // R),
        in_specs=[pl.BlockSpec((R, HD_A), lambda c, j: (j, c)), pl.BlockSpec((R, HD_A), lambda c, j: (j, nt + c))]
        + _a_specs(),
        out_specs=[pl.BlockSpec((R, HD_A), lambda c, j: (j, c)), pl.BlockSpec((R, HD_A), lambda c, j: (j, c))],
        scratch_shapes=[pltpu.VMEM((H + R, HD_A), F32), pltpu.VMEM((R, HD_A), F32), pltpu.VMEM((R, HD_A), F32),
                        pltpu.VMEM((SUB, HD_A), F32)],
        name="rglru_fwd", compiler_params=_cp(2),
    )(zp, zp, conv_w, conv_b, wgx, bgx, wga, bga, lam)


def _a_bwd(dyab, zp, h, conv_w, conv_b, wgx, bgx, wga, bga, lam):
    S = zp.shape[0]
    R, nt, nch = R_RGLRU, D // HD_A, S // R_RGLRU
    H = SUB

    def rows(c, j):
        return (nch - 1 - j, c)

    def rows_rec(c, j):
        return (nch - 1 - j, nt + c)

    def halo(c, j):
        return (jnp.maximum((nch - 1 - j) * (R // H) - 1, 0), c)

    def halo_z(c, j):
        return (jnp.maximum((nch - 1 - j) * (R // _HB) - 1, 0), nt + c)

    def body(dy_ref, zg_ref, zr_ref, zh_ref, h_ref, hh_ref, cw_ref, cb_ref, wgx_ref, bgx_ref, wga_ref, bga_ref,
             lam_ref, dzg_ref, dzr_ref, dcw_ref, dcb_ref, dwgx_ref, dbgx_ref, dwga_ref, dbga_ref, dlam_ref,
             ext_z, ext_h, ext_mu, ext_d, a_s, b_s, muc):
        j = pl.program_id(1)
        first_chunk = (nch - 1 - j) == 0

        @pl.when(j == 0)
        def _():
            ext_mu[R:R + H, :] = jnp.zeros((H, HD_A), F32)
            ext_d[R:R + H, :] = jnp.zeros((H, HD_A), F32)
            muc[...] = jnp.zeros_like(muc)
            for r in (dcw_ref, dcb_ref, dwgx_ref, dbgx_ref, dwga_ref, dbga_ref, dlam_ref):
                r[...] = jnp.zeros_like(r)

        zg = zg_ref[...].astype(F32)
        ext_z[0:H, :] = jnp.where(first_chunk, 0.0, zh_ref[_HB - H:_HB, :].astype(F32))
        ext_z[H:H + R, :] = zr_ref[...].astype(F32)
        ext_h[0:H, :] = jnp.where(first_chunk, 0.0, hh_ref[...])
        ext_h[H:H + R, :] = h_ref[...]
        xr = cb_ref[...]
        for k in range(CONV_A):
            xr = xr + cw_ref[k:k + 1, :] * ext_z[pl.ds(H - (CONV_A - 1 - k), R), :]
        gx, ga, sp, a, mult, xrb, wgxb, wgab = _rglru_pre(xr, wgx_ref, bgx_ref, wga_ref, bga_ref, lam_ref)
        gel, dgel = _gelu(zg, with_grad=True)
        dy = dy_ref[...].astype(F32)
        dh = dy * gel
        dzg_ref[...] = (dy * h_ref[...] * dgel).astype(BF16)
        a_s[...] = a
        b_s[...] = a * dh
        _scan_bwd(a_s, b_s, ext_mu, muc, R // SUB)
        lam_t = dh + ext_mu[pl.ds(1, R), :]
        ext_mu[R:R + H, :] = ext_mu[0:H, :]
        da = lam_t * ext_h[pl.ds(H - 1, R), :]
        gxr = gx * xr
        dlog_a = da * a - (lam_t * gxr) * (a * a) / mult
        dgx = lam_t * mult * xr
        dxr = lam_t * mult * gx
        lam_v = lam_ref[...]
        dlam_ref[...] += jnp.sum(dlog_a * ga, axis=0, keepdims=True) * (C_RG * _sigmoid(-lam_v))
        dpa = (dlog_a * (-C_RG * sp)) * ga * (1.0 - ga)
        dpx = dgx * gx * (1.0 - gx)
        dbga_ref[...] += jnp.sum(dpa, axis=0, keepdims=True)
        dbgx_ref[...] += jnp.sum(dpx, axis=0, keepdims=True)
        dpab, dpxb = dpa.astype(BF16), dpx.astype(BF16)
        dwga_ref[0] += lax.dot_general(xrb, dpab, _TN, preferred_element_type=F32)
        dwgx_ref[0] += lax.dot_general(xrb, dpxb, _TN, preferred_element_type=F32)
        dxr = (dxr + lax.dot_general(dpab, wgab, _NT, preferred_element_type=F32)
               + lax.dot_general(dpxb, wgxb, _NT, preferred_element_type=F32))
        dcb_ref[...] += jnp.sum(dxr, axis=0, keepdims=True)
        ext_d[0:R, :] = dxr
        dzr = jnp.zeros((R, HD_A), F32)
        for k in range(CONV_A):
            sh = CONV_A - 1 - k
            dcw_ref[k:k + 1, :] += jnp.sum(dxr * ext_z[pl.ds(H - sh, R), :], axis=0, keepdims=True)
            dzr = dzr + cw_ref[k:k + 1, :] * ext_d[pl.ds(sh, R), :]
        dzr_ref[...] = dzr.astype(BF16)
        ext_d[R:R + H, :] = ext_d[0:H, :]

    vec_o = pl.BlockSpec((1, HD_A), lambda c, j: (0, c))
    mat_o = pl.BlockSpec((1, HD_A, HD_A), lambda c, j: (c, 0, 0))
    return pl.pallas_call(
        body,
        out_shape=[SDS((S, D), BF16), SDS((S, D), BF16), SDS((CONV_A, D), F32), SDS((1, D), F32),
                   SDS((nt, HD_A, HD_A), F32), SDS((1, D), F32), SDS((nt, HD_A, HD_A), F32), SDS((1, D), F32),
                   SDS((1, D), F32)],
        grid=(nt, nch),
        in_specs=[pl.BlockSpec((R, HD_A), rows), pl.BlockSpec((R, HD_A), rows), pl.BlockSpec((R, HD_A), rows_rec),
                  pl.BlockSpec((_HB, HD_A), halo_z), pl.BlockSpec((R, HD_A), rows),
                  pl.BlockSpec((H, HD_A), halo)] + _a_specs(),
        out_specs=[pl.BlockSpec((R, HD_A), rows), pl.BlockSpec((R, HD_A), rows),
                   pl.BlockSpec((CONV_A, HD_A), lambda c, j: (0, c)), vec_o, mat_o, vec_o, mat_o, vec_o, vec_o],
        scratch_shapes=[pltpu.VMEM((H + R, HD_A), F32), pltpu.VMEM((H + R, HD_A), F32), pltpu.VMEM((R + H, HD_A), F32),
                        pltpu.VMEM((R + H, HD_A), F32), pltpu.VMEM((R, HD_A), F32), pltpu.VMEM((R, HD_A), F32),
                        pltpu.VMEM((SUB, HD_A), F32)],
        name="rglru_bwd", compiler_params=_cp(2),
    )(dyab, zp, zp, zp, h, h, conv_w, conv_b, wgx, bgx, wga, bga, lam)


_POOL_H = 16
_POOL_T0 = 2 * D // HD_A
_POOL_Y0 = D // HD_A


def _window_sum(lv, n, lo, rows, g, ahead):
    base = 0 if ahead else SUB
    cur, win = lv[0], None
    for i, s in enumerate((1, 2, 4, 8)):
        val = cur[pl.ds(base, n), :] + cur[pl.ds(base + (s if ahead else -s), n), :]
        sel = val[lo:lo + rows]
        win = sel if win is None else jnp.where(g >= i, sel, win)
        if i < 3:
            lv[i + 1][pl.ds(base, n), :] = val
            cur = lv[i + 1]
    return win


def _pool_width(g):
    return jnp.where(g == 0, 2.0, jnp.where(g == 1, 4.0, jnp.where(g == 2, 8.0, 16.0)))


def _b_fwd(zp, yab, wg, bg, sc):
    S = zp.shape[0]
    R, H = R_SEQ, _POOL_H

    def body(z_ref, wg_ref, bg_ref, sc_ref, yab_in, yb_ref, *lv):
        del yab_in
        g, j = pl.program_id(0), pl.program_id(1)

        @pl.when(j == 0)
        def _():
            for r in lv:
                r[0:SUB, :] = jnp.zeros((SUB, HD_A), F32)
            lv[0][SUB:SUB + H, :] = jnp.zeros((H, HD_A), F32)

        u = z_ref[...].astype(F32)
        lv[0][SUB + H:SUB + H + R, :] = u
        t1 = (j * R + 1 + lax.broadcasted_iota(jnp.int32, (R, HD_A), 0)).astype(F32)
        p = _window_sum(lv, H + R, H, R, g, False) / jnp.minimum(t1, _pool_width(g)) - u
        lin = lax.dot_general(p.astype(BF16), wg_ref[0].astype(BF16), _NN, preferred_element_type=F32) + bg_ref[...]
        yb_ref[...] = (lin * sc_ref[...]).astype(BF16)
        lv[0][SUB:SUB + H, :] = lv[0][SUB + R:SUB + R + H, :]

    vec = pl.BlockSpec((1, HD_A), lambda g, j: (0, g))
    return pl.pallas_call(
        body, out_shape=SDS(yab.shape, yab.dtype), grid=(len(POOL_WINDOWS), S // R),
        in_specs=[pl.BlockSpec((R, HD_A), lambda g, j: (j, _POOL_T0 + g)),
                  pl.BlockSpec((1, HD_A, HD_A), lambda g, j: (g, 0, 0)), vec, vec, pl.BlockSpec(memory_space=pl.ANY)],
        out_specs=pl.BlockSpec((R, HD_A), lambda g, j: (j, _POOL_Y0 + g)),
        scratch_shapes=[pltpu.VMEM((SUB + H + R, HD_A), F32)] * 4, input_output_aliases={4: 0},
        name="pool_fwd", compiler_params=_cp(2),
    )(zp, wg, bg, sc, yab)


def _b_bwd(dyab, zp, wg, bg, sc):
    S = zp.shape[0]
    R, H, nch, ng = R_SEQ, _POOL_H, S // R_SEQ, len(POOL_WINDOWS)

    def body(dy_ref, z_ref, zh_ref, wg_ref, bg_ref, sc_ref, dz_ref, dwg_ref, dbg_ref, dsc_ref, *scratch):
        lu, lq = scratch[:4], scratch[4:]
        g, j = pl.program_id(0), pl.program_id(1)
        jj = nch - 1 - j

        @pl.when(j == 0)
        def _():
            for r in lu:
                r[0:SUB, :] = jnp.zeros((SUB, HD_A), F32)
            for r in lq:
                r[R + H:R + H + SUB, :] = jnp.zeros((SUB, HD_A), F32)
            lq[0][R:R + H, :] = jnp.zeros((H, HD_A), F32)
            for r in (dwg_ref, dbg_ref, dsc_ref):
                r[...] = jnp.zeros_like(r)

        u = z_ref[...].astype(F32)
        lu[0][SUB:SUB + H, :] = jnp.where(jj == 0, 0.0, zh_ref[...].astype(F32))
        lu[0][SUB + H:SUB + H + R, :] = u
        t1 = (jj * R + 1 + lax.broadcasted_iota(jnp.int32, (R, HD_A), 0)).astype(F32)
        cnt = jnp.minimum(t1, _pool_width(g))
        pb = (_window_sum(lu, H + R, H, R, g, False) / cnt - u).astype(BF16)
        wgb = wg_ref[0].astype(BF16)
        lin = lax.dot_general(pb, wgb, _NN, preferred_element_type=F32) + bg_ref[...]
        dy = dy_ref[...].astype(F32)
        dsc_ref[...] += jnp.sum(dy * lin, axis=0, keepdims=True)
        dlin = dy * sc_ref[...]
        dbg_ref[...] += jnp.sum(dlin, axis=0, keepdims=True)
        dlb = dlin.astype(BF16)
        dwg_ref[0] += lax.dot_general(pb, dlb, _TN, preferred_element_type=F32)
        dp = lax.dot_general(dlb, wgb, _NT, preferred_element_type=F32)
        lq[0][0:R, :] = dp / cnt
        dz_ref[...] = (_window_sum(lq, R + H, 0, R, g, True) - dp).astype(BF16)
        lq[0][R:R + H, :] = lq[0][0:H, :]

    vec = pl.BlockSpec((1, HD_A), lambda g, j: (0, g))
    mat = pl.BlockSpec((1, HD_A, HD_A), lambda g, j: (g, 0, 0))
    return pl.pallas_call(
        body, out_shape=[SDS((S, D // 2), BF16), SDS((ng, HD_A, HD_A), F32), SDS((1, D // 2), F32),
                         SDS((1, D // 2), F32)],
        grid=(ng, nch),
        in_specs=[pl.BlockSpec((R, HD_A), lambda g, j: (nch - 1 - j, _POOL_Y0 + g)),
                  pl.BlockSpec((R, HD_A), lambda g, j: (nch - 1 - j, _POOL_T0 + g)),
                  pl.BlockSpec((H, HD_A), lambda g, j: (jnp.maximum((nch - 1 - j) * (R // H) - 1, 0), _POOL_T0 + g)),
                  mat, vec, vec],
        out_specs=[pl.BlockSpec((R, HD_A), lambda g, j: (nch - 1 - j, g)), mat, vec, vec],
        scratch_shapes=[pltpu.VMEM((SUB + H + R, HD_A), F32)] * 8,
        name="pool_bwd", compiler_params=_cp(2),
    )(dyab, zp, zp, wg, bg, sc)


_CW_F = 768


def _f_fwd(hp, w, b, name):
    S = hp.shape[0]
    R, H, cw = min(S, R_FFN), SUB, _CW_F
    nlt = cw // LANE

    def body(h_ref, w_ref, b_ref, o_ref, gel_ref, ud_ref, ext):
        j = pl.program_id(1)

        @pl.when(j == 0)
        def _():
            ext[:, 0:H, :] = jnp.zeros((nlt, H, LANE), F32)

        def stage(r0, lt):
            ext[lt, pl.ds(pl.multiple_of(r0 + H, SUB), _RB), :] = h_ref[pl.ds(r0, _RB), _lanes(lt)].astype(F32)

        def main(r0, lt):
            ls = _lanes(lt)
            gp = b_ref[:, ls]
            for k in range(CONV_F):
                gp = gp + w_ref[k:k + 1, ls] * ext[lt, pl.ds(r0 + (H - (CONV_F - 1 - k)), _RB), :]
            up = h_ref[pl.ds(r0, _RB), _lanes(lt + nlt)].astype(F32)
            gel, dgel = _gelu(gp, with_grad=True)
            rs = pl.ds(r0, _RB)
            o_ref[rs, ls] = (gel * up).astype(BF16)
            gel_ref[rs, ls] = gel.astype(BF16)
            ud_ref[rs, ls] = (up * dgel).astype(BF16)

        _sub_blocks(R, cw, stage)
        _sub_blocks(R, cw, main)
        ext[:, 0:H, :] = ext[:, R:R + H, :]

    tile = pl.BlockSpec((R, cw), lambda c, j: (j, c))
    return pl.pallas_call(
        body, out_shape=[SDS((S, D_FF), BF16)] * 3, grid=(D_FF // cw, S // R),
        in_specs=[pl.BlockSpec((R, 2 * cw), lambda c, j: (j, c)), pl.BlockSpec((CONV_F, cw), lambda c, j: (0, c)),
                  pl.BlockSpec((1, cw), lambda c, j: (0, c))],
        out_specs=[tile] * 3,
        scratch_shapes=[pltpu.VMEM((nlt, H + R, LANE), F32)], name=name, compiler_params=_cp(2),
    )(hp, w, b)


def _f_bwd(dact, hp, gel, ud, w, name):
    S = hp.shape[0]
    R, H, cw = min(S, R_FFN), SUB, _CW_F
    nch = S // R
    nlt = cw // LANE

    def body(da_ref, h_ref, hh_ref, gel_ref, ud_ref, w_ref, dh_ref, dw_ref, db_ref, ext_g, ext_d, acc):
        j = pl.program_id(1)
        jj = nch - 1 - j

        @pl.when(j == 0)
        def _():
            ext_d[:, R:R + H, :] = jnp.zeros((nlt, H, LANE), F32)
            acc[...] = jnp.zeros_like(acc)

        for lt in range(nlt):
            ext_g[lt, 0:H, :] = jnp.where(jj == 0, 0.0, hh_ref[_HB - H:_HB, lt * LANE:(lt + 1) * LANE].astype(F32))

        def stage(r0, lt):
            ext_g[lt, pl.ds(pl.multiple_of(r0 + H, SUB), _RB), :] = h_ref[pl.ds(r0, _RB), _lanes(lt)].astype(F32)

        def first(r0, lt):
            ls, lu, rs = _lanes(lt), _lanes(lt + nlt), pl.ds(r0, _RB)
            da = da_ref[rs, ls].astype(F32)
            dh_ref[rs, lu] = (da * gel_ref[rs, ls].astype(F32)).astype(BF16)
            dgp = da * ud_ref[rs, ls].astype(F32)
            ext_d[lt, rs, :] = dgp
            acc[CONV_F * SUB:(CONV_F + 1) * SUB, ls] += _psum8(dgp)
            for k in range(CONV_F):
                tap = ext_g[lt, pl.ds(r0 + (H - (CONV_F - 1 - k)), _RB), :]
                acc[k * SUB:(k + 1) * SUB, ls] += _psum8(dgp * tap)

        def second(r0, lt):
            ls = _lanes(lt)
            dhg = w_ref[CONV_F - 1:CONV_F, ls] * ext_d[lt, pl.ds(r0, _RB), :]
            for k in range(CONV_F - 1):
                dhg = dhg + w_ref[k:k + 1, ls] * ext_d[lt, pl.ds(r0 + (CONV_F - 1 - k), _RB), :]
            dh_ref[pl.ds(r0, _RB), ls] = dhg.astype(BF16)

        _sub_blocks(R, cw, stage)
        _sub_blocks(R, cw, first)
        _sub_blocks(R, cw, second)
        ext_d[:, R:R + H, :] = ext_d[:, 0:H, :]

        @pl.when(j == nch - 1)
        def _():
            for k in range(CONV_F):
                dw_ref[k:k + 1, :] = jnp.sum(acc[k * SUB:(k + 1) * SUB, :], axis=0, keepdims=True)
            db_ref[...] = jnp.sum(acc[CONV_F * SUB:(CONV_F + 1) * SUB, :], axis=0, keepdims=True)

    rows = lambda c, j: (nch - 1 - j, c)
    return pl.pallas_call(
        body, out_shape=[SDS((S, 2 * D_FF), BF16), SDS((CONV_F, D_FF), F32), SDS((1, D_FF), F32)],
        grid=(D_FF // cw, nch),
        in_specs=[pl.BlockSpec((R, cw), rows), pl.BlockSpec((R, cw), lambda c, j: (nch - 1 - j, 2 * c)),
                  pl.BlockSpec((_HB, cw), lambda c, j: (jnp.maximum((nch - 1 - j) * (R // _HB) - 1, 0), 2 * c)),
                  pl.BlockSpec((R, cw), rows), pl.BlockSpec((R, cw), rows),
                  pl.BlockSpec((CONV_F, cw), lambda c, j: (0, c))],
        out_specs=[pl.BlockSpec((R, 2 * cw), rows), pl.BlockSpec((CONV_F, cw), lambda c, j: (0, c)),
                   pl.BlockSpec((1, cw), lambda c, j: (0, c))],
        scratch_shapes=[pltpu.VMEM((nlt, H + R, LANE), F32), pltpu.VMEM((nlt, R + H, LANE), F32),
                        pltpu.VMEM(((CONV_F + 1) * SUB, cw), F32)], name=name,
        compiler_params=_cp(2),
    )(dact, hp, hp, gel, ud, w)


_CW_C = 256
_H_C = 32


def _c_fwd(h1p, w, b):
    S = h1p.shape[0]
    R, H, cw = R_SEQ, _H_C, _CW_C
    nlt = cw // LANE

    def body(h_ref, w_ref, b_ref, o_ref, ext):
        j = pl.program_id(1)

        @pl.when(j == 0)
        def _():
            ext[:, 0:H, :] = jnp.zeros((nlt, H, LANE), F32)

        def stage(r0, lt):
            rs = pl.ds(r0, _RB)
            gate = h_ref[rs, _lanes(lt + nlt)].astype(F32)
            ext[lt, pl.ds(pl.multiple_of(r0 + H, SUB), _RB), :] = h_ref[rs, _lanes(lt)].astype(F32) * _sigmoid(gate)

        def main(r0, lt):
            ls = _lanes(lt)
            cv = b_ref[:, ls]
            for k in range(CONV_C):
                cv = cv + w_ref[k:k + 1, ls] * ext[lt, pl.ds(r0 + (H - (CONV_C - 1 - k)), _RB), :]
            o_ref[pl.ds(r0, _RB), ls] = cv

        _sub_blocks(R, cw, stage)
        _sub_blocks(R, cw, main)
        ext[:, 0:H, :] = ext[:, R:R + H, :]

    return pl.pallas_call(
        body, out_shape=SDS((S, D), F32), grid=(D // cw, S // R),
        in_specs=[pl.BlockSpec((R, 2 * cw), lambda c, j: (j, c)), pl.BlockSpec((CONV_C, cw), lambda c, j: (0, c)),
                  pl.BlockSpec((1, cw), lambda c, j: (0, c))],
        out_specs=pl.BlockSpec((R, cw), lambda c, j: (j, c)),
        scratch_shapes=[pltpu.VMEM((nlt, H + R, LANE), F32)], name="conf_conv_fwd", compiler_params=_cp(2),
    )(h1p, w, b)


def _c_bwd(dcv, h1p, w):
    S = h1p.shape[0]
    R, H, cw, nch = R_SEQ, _H_C, _CW_C, S // R_SEQ
    nlt = cw // LANE
    a_b, a_val, a_gate = CONV_C * SUB, (CONV_C + 1) * SUB, (CONV_C + 2) * SUB

    def body(dc_ref, h_ref, hh_ref, w_ref, dh_ref, dw_ref, db_ref, db1_ref, ext_u, ext_d, acc):
        j = pl.program_id(1)
        jj = nch - 1 - j

        @pl.when(j == 0)
        def _():
            ext_d[:, R:R + H, :] = jnp.zeros((nlt, H, LANE), F32)
            acc[...] = jnp.zeros_like(acc)

        for lt in range(nlt):
            ext_u[lt, 0:H, :] = jnp.where(
                jj == 0, 0.0, hh_ref[:, lt * LANE:(lt + 1) * LANE].astype(F32)
                * _sigmoid(hh_ref[:, cw + lt * LANE:cw + (lt + 1) * LANE].astype(F32)))

        def stage(r0, lt):
            rs, ls = pl.ds(r0, _RB), _lanes(lt)
            gate = h_ref[rs, _lanes(lt + nlt)].astype(F32)
            ext_u[lt, pl.ds(pl.multiple_of(r0 + H, SUB), _RB), :] = h_ref[rs, ls].astype(F32) * _sigmoid(gate)
            ext_d[lt, rs, :] = dc_ref[rs, ls]

        def first(r0, lt):
            ls = _lanes(lt)
            dc = dc_ref[pl.ds(r0, _RB), ls]
            acc[a_b:a_b + SUB, ls] += _psum8(dc)
            for k in range(CONV_C):
                tap = ext_u[lt, pl.ds(r0 + (H - (CONV_C - 1 - k)), _RB), :]
                acc[k * SUB:(k + 1) * SUB, ls] += _psum8(dc * tap)

        def second(r0, lt):
            rs, ls, lg = pl.ds(r0, _RB), _lanes(lt), _lanes(lt + nlt)
            du = w_ref[CONV_C - 1:CONV_C, ls] * ext_d[lt, rs, :]
            for k in range(CONV_C - 1):
                du = du + w_ref[k:k + 1, ls] * ext_d[lt, pl.ds(r0 + (CONV_C - 1 - k), _RB), :]
            val = h_ref[rs, ls].astype(F32)
            sg = _sigmoid(h_ref[rs, lg].astype(F32))
            dval = du * sg
            dgate = du * val * sg * (1.0 - sg)
            acc[a_val:a_val + SUB, ls] += _psum8(dval)
            acc[a_gate:a_gate + SUB, ls] += _psum8(dgate)
            dh_ref[rs, ls] = dval.astype(BF16)
            dh_ref[rs, lg] = dgate.astype(BF16)

        _sub_blocks(R, cw, stage)
        _sub_blocks(R, cw, first)
        _sub_blocks(R, cw, second)
        ext_d[:, R:R + H, :] = ext_d[:, 0:H, :]

        @pl.when(j == nch - 1)
        def _():
            for k in range(CONV_C):
                dw_ref[k:k + 1, :] = jnp.sum(acc[k * SUB:(k + 1) * SUB, :], axis=0, keepdims=True)
            db_ref[...] = jnp.sum(acc[a_b:a_b + SUB, :], axis=0, keepdims=True)
            db1_ref[:, 0:cw] = jnp.sum(acc[a_val:a_val + SUB, :], axis=0, keepdims=True)
            db1_ref[:, cw:2 * cw] = jnp.sum(acc[a_gate:a_gate + SUB, :], axis=0, keepdims=True)

    rows = lambda c, j: (nch - 1 - j, c)
    return pl.pallas_call(
        body, out_shape=[SDS((S, 2 * D), BF16), SDS((CONV_C, D), F32), SDS((1, D), F32), SDS((1, 2 * D), F32)],
        grid=(D // cw, nch),
        in_specs=[pl.BlockSpec((R, cw), rows), pl.BlockSpec((R, 2 * cw), rows),
                  pl.BlockSpec((H, 2 * cw), lambda c, j: (jnp.maximum((nch - 1 - j) * (R // H) - 1, 0), c)),
                  pl.BlockSpec((CONV_C, cw), lambda c, j: (0, c))],
        out_specs=[pl.BlockSpec((R, 2 * cw), rows), pl.BlockSpec((CONV_C, cw), lambda c, j: (0, c)),
                   pl.BlockSpec((1, cw), lambda c, j: (0, c)), pl.BlockSpec((1, 2 * cw), lambda c, j: (0, c))],
        scratch_shapes=[pltpu.VMEM((nlt, H + R, LANE), F32), pltpu.VMEM((nlt, R + H, LANE), F32),
                        pltpu.VMEM(((CONV_C + 3) * SUB, cw), F32)], name="conf_conv_bwd",
        compiler_params=_cp(2),
    )(dcv, h1p, h1p, w)


def _local_step(x, mem, tgt, W, fetch=None, send=None):
    G = {}
    W = dict(W)

    def arrive(group, after):
        if fetch is None:
            return None
        got, tok = fetch(group, after)
        for key, val in got.items():
            W[key] = {**W.get(key, {}), **val} if isinstance(val, dict) else val
        return tok

    def gain(g, tok):
        return g if tok is None else g + tok

    def sent(group):
        return None if send is None else send(group, G)

    def xattn_fwd(xin, n, l):
        tok = arrive(("xa", l), n)
        mn = _rms_fwd(mem, gain(W["xa_mem_norm"][l:l + 1], tok), f"xa_memnorm_fwd{l}")
        q = _mm_nn(n, W["xa_wq"][l], out_dtype=BF16, name=f"xa_q{l}")
        k = _mm_nn(mn, W["xa_wk"][l], out_dtype=BF16, name=f"xa_k{l}")
        v = _mm_nn(mn, W["xa_wv"][l], out_dtype=BF16, name=f"xa_v{l}")
        o = _attn_fwd(q, k, v, f"xa_attn_fwd{l}")
        xout, nout = _mm_nn(o, W["xa_wo"][l], out_dtype=F32, name=f"xa_o{l}", add=xin, norm=W["f_norm"][l:l + 1])
        return xout, nout, (xin, n, q, mn, k, v, o)

    def xattn_bwd(dx, dxb, saved, l):
        xin, n, q, mn, k, v, o = saved
        do = _mm_nt(dxb, W["xa_wo"][l], out_dtype=BF16, name=f"xa_do{l}")
        G[f"xa_wo{l}"] = _mm_tn(o, dxb, out_dtype=BF16, name=f"xa_dwo{l}")
        dq, dk, dv = _attn_bwd(q, k, v, do, f"xa_attn_bwd{l}")
        dkb, dvb = dk.astype(BF16), dv.astype(BF16)
        G[f"xa_wq{l}"] = _mm_tn(n, dq, out_dtype=BF16, name=f"xa_dwq{l}")
        G[f"xa_wk{l}"] = _mm_tn(mn, dkb, out_dtype=BF16, name=f"xa_dwk{l}")
        G[f"xa_wv{l}"] = _mm_tn(mn, dvb, out_dtype=BF16, name=f"xa_dwv{l}")
        tok = sent(("xa", l))
        dmn = _mm_nt(dkb, W["xa_wk"][l], out_dtype=F32, name=f"xa_dmn_k{l}")
        dmn = _mm_nt(dvb, W["xa_wv"][l], out_dtype=F32, name=f"xa_dmn_v{l}", add=dmn)
        (G[f"xa_mem_norm{l}"],) = _rms_bwd(mem, W["xa_mem_norm"][l:l + 1], dmn, None, f"xa_memnorm_bwd{l}")
        dx, dxb, G[f"xa_norm{l}"] = _mm_nt(dq, W["xa_wq"][l], out_dtype=F32, name=f"xa_dn{l}",
                                           rms=(xin, gain(W["xa_norm"][l:l + 1], tok), dx))
        return dx, dxb

    def ffn_fwd(xin, n, l, next_gain):
        tok = arrive(("f", l), n)
        hp = _mm_nn(n, W["f_w_up"][l], out_dtype=BF16, name=f"f_up{l}")
        act, gel, ud = _f_fwd(hp, W["f_dw_w"][l], gain(W["f_dw_b"][l:l + 1], tok), f"f_conv_fwd{l}")
        res = _mm_nn(act, W["f_w_down"][l], out_dtype=F32, name=f"f_down{l}", add=xin, norm=next_gain)
        xout, nout = res if next_gain is not None else (res, None)
        return xout, nout, (xin, n, hp, act, gel, ud)

    def ffn_bwd(dx, dxb, saved, l):
        xin, n, hp, act, gel, ud = saved
        dact = _mm_nt(dxb, W["f_w_down"][l], out_dtype=BF16, name=f"f_dact{l}")
        G[f"f_w_down{l}"] = _mm_tn(act, dxb, out_dtype=BF16, name=f"f_dwdown{l}")
        dhp, G[f"f_dw_w{l}"], G[f"f_dw_b{l}"] = _f_bwd(dact, hp, gel, ud, W["f_dw_w"][l], f"f_conv_bwd{l}")
        G[f"f_w_up{l}"] = _mm_tn(n, dhp, out_dtype=BF16, name=f"f_dwup{l}", blocks=_CW_F)
        tok = sent(("f", l))
        dx, dxb, G[f"f_norm{l}"] = _mm_nt(dhp, W["f_w_up"][l], out_dtype=F32, name=f"f_dn{l}",
                                          rms=(xin, gain(W["f_norm"][l:l + 1], tok), dx))
        return dx, dxb

    n0 = _rms_fwd(x, W["ab_norm"], "ab_norm_fwd")
    tok = arrive(("ab", 0), n0)
    a_par = (W["a_conv_w"], gain(W["a_conv_b"], tok), W["a_gate_x_w"], W["a_gate_x_b"], W["a_gate_a_w"],
             W["a_gate_a_b"], W["a_lambda"])
    b_par = (W["b_group_w"], W["b_group_b"], W["b_scale"])
    zp = _mm_nn(n0, W["ab_w_in"], out_dtype=BF16, name="ab_in")
    yab, h_a = _a_fwd(zp, *a_par)
    yab = _b_fwd(zp, yab, *b_par)
    arrive(("ab", 1), yab)
    x1, n1 = _mm_nn(yab, W["ab_w_out"], out_dtype=F32, name="ab_out", add=x, norm=W["xa_norm"][0:1])
    x2, n2, s_xa0 = xattn_fwd(x1, n1, 0)
    x3, n3, s_f0 = ffn_fwd(x2, n2, 0, W["c_norm"])
    tok = arrive(("c", 0), n3)
    h1p = _mm_nn(n3, W["c_w_pw1"], out_dtype=BF16, name="c_pw1", bias=gain(W["c_b_pw1"], tok))
    cv = _c_fwd(h1p, W["c_dw_w"], W["c_dw_b"])
    sc = _ln_silu_fwd(cv, W["c_ln_g"], W["c_ln_b"])
    x4, n4 = _mm_nn(sc, W["c_w_pw2"], out_dtype=F32, name="c_pw2", bias=W["c_b_pw2"], add=x3, norm=W["xa_norm"][1:2])
    x5, n5, s_xa1 = xattn_fwd(x4, n4, 1)
    x6, _, s_f1 = ffn_fwd(x5, n5, 1, None)
    loss, dx, dxb, G["final_norm"] = _loss_head(x6, W["final_norm"], tgt)

    dx, dxb = ffn_bwd(dx, dxb, s_f1, 1)
    dx, dxb = xattn_bwd(dx, dxb, s_xa1, 1)
    dsc = _mm_nt(dxb, W["c_w_pw2"], out_dtype=BF16, name="c_dsc")
    G["c_w_pw2"] = _mm_tn(sc, dxb, out_dtype=BF16, name="c_dwpw2")
    dcv, G["c_ln_g"], G["c_ln_b"], G["c_b_pw2"] = _ln_silu_bwd(dsc, cv, W["c_ln_g"], W["c_ln_b"], dx)
    dh1p, G["c_dw_w"], G["c_dw_b"], G["c_b_pw1"] = _c_bwd(dcv, h1p, W["c_dw_w"])
    G["c_w_pw1"] = _mm_tn(n3, dh1p, out_dtype=BF16, name="c_dwpw1", blocks=_CW_C)
    tok = sent(("c", 0))
    dx, dxb, G["c_norm"] = _mm_nt(dh1p, W["c_w_pw1"], out_dtype=F32, name="c_dn",
                                  rms=(x3, gain(W["c_norm"], tok), dx))
    dx, dxb = ffn_bwd(dx, dxb, s_f0, 0)
    dx, dxb = xattn_bwd(dx, dxb, s_xa0, 0)
    dyab = _mm_nt(dxb, W["ab_w_out"], out_dtype=BF16, name="ab_dyab")
    G["ab_w_out"] = _mm_tn(yab, dxb, out_dtype=BF16, name="ab_dwout")
    tok = sent(("ab", 1))
    a_par = (a_par[0], gain(a_par[1], tok)) + a_par[2:]
    (dzg, dzr, G["a_conv_w"], G["a_conv_b"], G["a_gate_x_w"], G["a_gate_x_b"], G["a_gate_a_w"], G["a_gate_a_b"],
     G["a_lambda"]) = _a_bwd(dyab, zp, h_a, *a_par)
    dzq, G["b_group_w"], G["b_group_b"], G["b_scale"] = _b_bwd(dyab, zp, *b_par)
    G["ab_w_in"] = jnp.concatenate(
        [_mm_tn(n0, dz, out_dtype=BF16, name=f"ab_dwin_{part}")
         for part, dz in (("gate", dzg), ("rec", dzr), ("pool", dzq))], axis=1)
    tok = sent(("ab", 0))
    dx, _, G["ab_norm"] = _mm_nt_cols([dzg, dzr, dzq], W["ab_w_in"], name="ab_dn",
                                      rms=(x, gain(W["ab_norm"], tok), dx))
    return loss, dx, G


def _my_place():
    x, y, c = lax.axis_index("x"), lax.axis_index("y"), lax.axis_index("c")
    return x, y, c


def _all_gather(shards, name):
    n = len(shards)

    def body(*refs):
        ins, outs = refs[:n], refs[n:2 * n]
        send_sems, recv_sems, local_sems = refs[2 * n:]
        x, y, c = _my_place()
        me, sibling = (x, y, c), (x, y, 1 - c)
        chips = [(1 - x, y), (x, 1 - y), (1 - x, 1 - y)]

        def slab(a, place):
            px, py, pc = place
            return outs[a].at[4 * px + 2 * py + pc]

        def copy(a, k, block, to, src=None):
            return pltpu.make_async_remote_copy(
                src_ref=slab(a, block) if src is None else src, dst_ref=slab(a, block),
                send_sem=send_sems.at[a, k], recv_sem=recv_sems.at[a, k], device_id=to, device_id_type=MESH)

        mine = [pltpu.make_async_copy(ins[a], slab(a, me), local_sems.at[a]) for a in range(n)]
        for cp in mine:
            cp.start()
        first = []
        for j, chip in enumerate(chips):
            first += [copy(a, 1 + j, me, (*chip, c), src=ins[a]) for a in range(n)]
        first += [copy(a, 0, me, sibling, src=ins[a]) for a in range(n)]
        for cp in first:
            cp.start()
        passed = []
        for j, chip in enumerate(chips):
            for a in range(n):
                copy(a, 1 + j, (*chip, c), me).wait_recv()
                cp = copy(a, 4 + j, (*chip, c), sibling)
                cp.start()
                passed.append(cp)
        for a in range(n):
            copy(a, 0, sibling, me).wait_recv()
        for j, chip in enumerate(chips):
            for a in range(n):
                copy(a, 4 + j, (*chip, 1 - c), me).wait_recv()
        for cp in first + passed:
            cp.wait_send()
        for cp in mine:
            cp.wait()

    any_spec = pl.BlockSpec(memory_space=pl.ANY)
    return pl.pallas_call(
        body, out_shape=[SDS((N_DEV,) + s.shape, s.dtype) for s in shards], in_specs=[any_spec] * n,
        out_specs=[any_spec] * n,
        scratch_shapes=[pltpu.SemaphoreType.DMA((n, 7)), pltpu.SemaphoreType.DMA((n, 7)), pltpu.SemaphoreType.DMA((n,))],
        name=name,
    )(*shards)


_HBM = pl.BlockSpec(memory_space=pltpu.HBM)
_SEM = pl.BlockSpec(memory_space=pltpu.SEMAPHORE)
_EFFECT = pltpu.SideEffectType.DATAFLOW_SIDE_EFFECTING


def _peer_places():
    x, y, c = _my_place()
    peers = []
    for k in range(1, N_DEV):
        px = 1 - x if (k >> 2) & 1 else x
        py = 1 - y if (k >> 1) & 1 else y
        pc = 1 - c if k & 1 else c
        peers.append(((px, py, pc), 4 * px + 2 * py + pc))
    return (x, y, c), 4 * x + 2 * y + c, peers


def _send_start(srcs, per_dest, name):
    n = len(srcs)
    lands = [lax.empty((N_DEV,) + (s.shape[1:] if per_dest else s.shape), s.dtype) for s in srcs]

    def body(*refs):
        src, land = refs[:n], refs[n:2 * n]
        outs = refs[2 * n:]
        send, recv, token = outs[:n], outs[n:2 * n], outs[4 * n]
        _, me, peers = _peer_places()
        for a in range(n):
            for peer, pidx in peers:
                pltpu.make_async_remote_copy(
                    src_ref=src[a].at[pidx] if per_dest else src[a], dst_ref=land[a].at[me], send_sem=send[a],
                    recv_sem=recv[a], device_id=peer, device_id_type=MESH).start()
        token[...] = jnp.zeros_like(token)

    hbm = lambda a: pltpu.HBM(a.shape, a.dtype)
    sem = pltpu.SemaphoreType.DMA(())
    res = pl.pallas_call(
        body, name=name,
        out_shape=tuple([sem] * (2 * n) + [hbm(s) for s in srcs] + [hbm(l) for l in lands]
                        + [SDS((SUB, LANE), F32)]),
        in_specs=[_HBM] * (2 * n),
        out_specs=tuple([_SEM] * (2 * n) + [_HBM] * (2 * n) + [pl.BlockSpec(memory_space=pltpu.VMEM)]),
        input_output_aliases={i: 2 * n + i for i in range(2 * n)},
        compiler_params=pltpu.CompilerParams(has_side_effects=_EFFECT),
    )(*[pltpu.with_memory_space_constraint(s, pltpu.HBM) for s in srcs],
      *[pltpu.with_memory_space_constraint(l, pltpu.HBM) for l in lands])
    return res[:n], res[n:2 * n], res[2 * n:3 * n], res[3 * n:4 * n], res[4 * n]


def _send_wait(send, recv, srcs, lands, after, per_dest, name):
    n = len(srcs)

    def body(*refs):
        src, land = refs[:n], refs[n:2 * n]
        send_s, recv_s = refs[2 * n:3 * n], refs[3 * n:4 * n]
        token = refs[-1]
        place, _, _ = _peer_places()
        for a in range(n):
            seven = land[a].at[pl.ds(0, N_DEV - 1)]
            copy = pltpu.make_async_remote_copy(
                src_ref=src[a].at[pl.ds(0, N_DEV - 1)] if per_dest else seven, dst_ref=seven, send_sem=send_s[a],
                recv_sem=recv_s[a], device_id=place, device_id_type=MESH)
            copy.wait_send()
            copy.wait_recv()
        token[...] = jnp.zeros_like(token)

    hbm = lambda a: pltpu.HBM(a.shape, a.dtype)
    res = pl.pallas_call(
        body, name=name,
        out_shape=tuple([hbm(s) for s in srcs] + [hbm(l) for l in lands] + [SDS((SUB, LANE), F32)]),
        in_specs=[_HBM] * (2 * n) + [_SEM] * (2 * n) + [pl.BlockSpec(memory_space=pl.ANY)],
        out_specs=tuple([_HBM] * (2 * n) + [pl.BlockSpec(memory_space=pltpu.VMEM)]),
        input_output_aliases={i: i for i in range(2 * n)},
        compiler_params=pltpu.CompilerParams(has_side_effects=_EFFECT),
    )(*srcs, *lands, *send, *recv, after)
    return res[:n], res[n:2 * n], res[2 * n]


def _adamw_math(w, g, m, v):
    m = ADAM_B1 * m + (1.0 - ADAM_B1) * g
    v = ADAM_B2 * v + (1.0 - ADAM_B2) * (g * g)
    m_hat = m / (1.0 - ADAM_B1 ** ADAM_STEP)
    v_hat = v / (1.0 - ADAM_B2 ** ADAM_STEP)
    delta = -ADAM_LR * (m_hat / (jnp.sqrt(v_hat) + ADAM_EPS) + ADAM_WD * w)
    return delta, m, v


def _row_tile(r, c, itemsize_rows):
    cap = max(SUB, (itemsize_rows // (4 * c)) // SUB * SUB)
    if r <= cap:
        return r
    best = None
    for t in range(SUB, cap + 1, SUB):
        if r % t == 0:
            best = t
    return best if best is not None else r


def _sum_adamw(landing, w, m, v, name, layer=0, prev=None, after=None):
    _, r, c = landing.shape
    tr = _row_tile(r, c, 2 << 20)
    off = layer * (r // tr)
    tail = ([] if prev is None else list(prev)) + ([] if after is None else [after])

    def body(l_ref, w_ref, m_ref, v_ref, *rest):
        g_ref, d_ref, mo_ref, vo_ref = rest[-4:]
        g = l_ref[0].astype(F32)
        for s in range(1, N_DEV):
            g = g + l_ref[s].astype(F32)
        g_ref[...] = g
        d_ref[...], mo_ref[...], vo_ref[...] = _adamw_math(w_ref[...], g, m_ref[...], v_ref[...])

    blk = pl.BlockSpec((tr, c), lambda i: (i + off, 0))
    n_prev = 0 if prev is None else 4
    return pl.pallas_call(
        body, out_shape=[SDS(w.shape, F32)] * 4, grid=(r // tr,),
        in_specs=[pl.BlockSpec((N_DEV, tr, c), lambda i: (0, i, 0)), blk, blk, blk]
        + [pl.BlockSpec(memory_space=pl.ANY)] * len(tail),
        out_specs=[blk] * 4, input_output_aliases={4 + i: i for i in range(n_prev)}, name=name,
        compiler_params=_cp(1),
    )(landing, w, m, v, *tail)


def _sum8(landing, name):
    _, r, c = landing.shape

    def body(l_ref, g_ref):
        g = l_ref[0]
        for s in range(1, N_DEV):
            g = g + l_ref[s]
        g_ref[...] = g

    return pl.pallas_call(body, out_shape=SDS((r, c), F32), name=name, compiler_params=_cp(0))(landing)


def _adamw_small(repl_pack, own_pack, P, M, V):
    table, off = [], 0
    for name, shape in _REPL.items():
        table.append((name, shape if len(shape) > 1 else (1,) + shape, 0, off // LANE))
        off += _size(shape)
    off = _REPL_ROWS * LANE
    for name, shape in _SMALL_SHARDED.items():
        table.append((name, shape, 1, off // LANE))
        off += _size(shape)
    n = len(table)

    def body(*refs):
        packs, ins, outs = refs[:2], refs[2:2 + 3 * n], refs[2 + 3 * n:]
        for p, (_, shape, which, r0) in enumerate(table):
            w_ref, m_ref, v_ref = ins[3 * p:3 * p + 3]
            g_ref, d_ref, mo_ref, vo_ref = outs[4 * p:4 * p + 4]
            pack, rows, q = packs[which], shape[-2], shape[-1] // LANE
            lead = [()]
            for dim in shape[:-2]:
                lead = [t + (i,) for t in lead for i in range(dim)]
            for li, idx in enumerate(lead):
                if q == 1:
                    dst = g_ref.at[idx] if idx else g_ref
                    dst[...] = pack[r0 + li * rows:r0 + (li + 1) * rows, :]
                    continue
                for i in range(rows):
                    for k in range(q):
                        row = r0 + (li * rows + i) * q + k
                        g_ref[idx + (slice(i, i + 1), slice(k * LANE, (k + 1) * LANE))] = pack[row:row + 1, :]
            d_ref[...], mo_ref[...], vo_ref[...] = _adamw_math(w_ref[...], g_ref[...], m_ref[...], v_ref[...])

    ins, out_shape = [], []
    for name, shape, _, _ in table:
        ins += [t[name].reshape(shape) for t in (P, M, V)]
        out_shape += [SDS(shape, F32)] * 4
    res = pl.pallas_call(body, out_shape=out_shape, name="adamw_small", compiler_params=_cp(0))(
        repl_pack, own_pack, *ins)
    dicts = ({}, {}, {}, {})
    for p, (name, shape, _, _) in enumerate(table):
        for d, arr in zip(dicts, res[4 * p:4 * p + 4]):
            d[name] = arr.reshape(P[name].shape)
    return dicts


_BIG = {
    "ab_w_in": (1, D, 320), "ab_w_out": (1, 192, D), "c_w_pw1": (1, D, 256), "c_w_pw2": (1, 128, D),
    "xa_wq": (2, 128, D), "xa_wk": (2, 128, D), "xa_wv": (2, 128, D), "xa_wo": (2, 128, D),
    "f_w_up": (2, D, 768), "f_w_down": (2, 384, D),
}
_SMALL_SHARDED = {
    "a_conv_w": (1, 4, 128), "c_norm": (1, 128), "c_b_pw1": (1, 256), "c_dw_w": (1, 31, 128), "c_dw_b": (1, 128),
    "c_ln_g": (1, 128), "c_ln_b": (1, 128), "c_b_pw2": (1, 128), "f_dw_w": (2, 3, 384),
}
_REPL = {
    "ab_norm": (1, D), "a_conv_b": (1, D), "a_gate_x_w": (1, 8, 128, 128), "a_gate_x_b": (1, D),
    "a_gate_a_w": (1, 8, 128, 128), "a_gate_a_b": (1, D), "a_lambda": (1, D), "b_group_w": (1, 4, 128, 128),
    "b_group_b": (1, 512), "b_scale": (1, 512), "xa_norm": (2, D), "xa_mem_norm": (2, D), "f_norm": (2, D),
    "f_dw_b": (2, D_FF), "final_norm": (D,),
}


def _size(shape):
    n = 1
    for s in shape:
        n *= s
    return n


_N_SS = sum(_size(s) for s in _SMALL_SHARDED.values())
_N_REPL = sum(_size(s) for s in _REPL.values())
_REPL_ROWS = -(-_N_REPL // (N_DEV * SUB * LANE)) * SUB
_SS_ROWS = _N_SS // LANE
_SMALL_ROWS = -(-(_REPL_ROWS + _SS_ROWS) // SUB) * SUB


def _pack(parts, rows):
    flat = jnp.concatenate([p.reshape(-1).astype(F32) for p in parts])
    return jnp.pad(flat, (0, rows * LANE - flat.shape[0])).reshape(rows, LANE)


def _pair_blocks(v, bw):
    lead, n = v.shape[:-1], v.shape[-1]
    return jnp.swapaxes(v.reshape(lead + (2, n // (2 * bw), bw)), -3, -2).reshape(lead + (n,))


def _unpair_blocks(v, bw):
    lead, n = v.shape[:-1], v.shape[-1]
    return jnp.swapaxes(v.reshape(lead + (n // (2 * bw), 2, bw)), -3, -2).reshape(lead + (n,))


_GROUPS = {
    ("ab", 0): (("ab_w_in", 0),),
    ("ab", 1): (("ab_w_out", 0),),
    ("xa", 0): (("xa_wq", 0), ("xa_wk", 0), ("xa_wv", 0), ("xa_wo", 0)),
    ("f", 0): (("f_w_up", 0), ("f_w_down", 0)),
    ("c", 0): (("c_w_pw1", 0), ("c_w_pw2", 0)),
    ("xa", 1): (("xa_wq", 1), ("xa_wk", 1), ("xa_wv", 1), ("xa_wo", 1)),
    ("f", 1): (("f_w_up", 1), ("f_w_down", 1)),
}
_SEND_GROUPS = _GROUPS


def _weight_layout(name, g):
    if name == "ab_w_in":
        return jnp.swapaxes(g, 0, 1).reshape(D, N_DEV * 320)
    if name in ("c_w_pw1", "f_w_up"):
        return g
    return g.reshape(N_DEV * g.shape[1], D)


def _grad_blocks(name, l, G):
    _, r, c = _BIG[name]
    if name == "ab_w_in":
        return jnp.swapaxes(G[name].reshape(D, N_DEV, 320), 0, 1)
    if name == "c_w_pw1":
        return G[name]
    if name == "f_w_up":
        return G[f"{name}{l}"]
    return (G[name] if _BIG[name][0] == 1 else G[f"{name}{l}"]).reshape(N_DEV, r, c)


def _small_layouts(sm):
    W = {}
    sm = sm.reshape(N_DEV, -1)
    off = 0
    for name, shape in _SMALL_SHARDED.items():
        n = _size(shape)
        blocks = sm[:, off:off + n].reshape((N_DEV,) + shape)
        off += n
        W[name] = jnp.moveaxis(blocks, 0, -2).reshape(shape[:-1] + (N_DEV * shape[-1],))
    W["a_conv_w"], W["c_dw_w"] = W["a_conv_w"][0], W["c_dw_w"][0]
    W["c_b_pw1"] = _pair_blocks(W["c_b_pw1"], _CW_C)
    return W


def _with_own(land, src, me, per_dest):
    own = lax.dynamic_slice_in_dim(src, me, 1, 0) if per_dest else src[None]
    return lax.dynamic_update_slice_in_dim(land, own, me, 0)


def _to_dest_major(g, shape):
    full = g.reshape(shape[:-1] + (N_DEV, shape[-1]))
    return jnp.moveaxis(full, -2, 0).reshape(N_DEV, -1)


def kernel(x, mem, ab_norm, ab_w_in, a_conv_w, a_conv_b, a_gate_x_w, a_gate_x_b, a_gate_a_w, a_gate_a_b, a_lambda, b_group_w, b_group_b, b_scale, ab_w_out, c_norm, c_w_pw1, c_b_pw1, c_dw_w, c_dw_b, c_ln_g, c_ln_b, c_w_pw2, c_b_pw2, xa_norm, xa_mem_norm, xa_wq, xa_wk, xa_wv, xa_wo, f_norm, f_w_up, f_dw_w, f_dw_b, f_w_down, final_norm, loss_target, m_ab_norm, m_ab_w_in, m_a_conv_w, m_a_conv_b, m_a_gate_x_w, m_a_gate_x_b, m_a_gate_a_w, m_a_gate_a_b, m_a_lambda, m_b_group_w, m_b_group_b, m_b_scale, m_ab_w_out, m_c_norm, m_c_w_pw1, m_c_b_pw1, m_c_dw_w, m_c_dw_b, m_c_ln_g, m_c_ln_b, m_c_w_pw2, m_c_b_pw2, m_xa_norm, m_xa_mem_norm, m_xa_wq, m_xa_wk, m_xa_wv, m_xa_wo, m_f_norm, m_f_w_up, m_f_dw_w, m_f_dw_b, m_f_w_down, m_final_norm, v_ab_norm, v_ab_w_in, v_a_conv_w, v_a_conv_b, v_a_gate_x_w, v_a_gate_x_b, v_a_gate_a_w, v_a_gate_a_b, v_a_lambda, v_b_group_w, v_b_group_b, v_b_scale, v_ab_w_out, v_c_norm, v_c_w_pw1, v_c_b_pw1, v_c_dw_w, v_c_dw_b, v_c_ln_g, v_c_ln_b, v_c_w_pw2, v_c_b_pw2, v_xa_norm, v_xa_mem_norm, v_xa_wq, v_xa_wk, v_xa_wv, v_xa_wo, v_f_norm, v_f_w_up, v_f_dw_w, v_f_dw_b, v_f_w_down, v_final_norm):
    args = dict(locals())
    P = {n: args[n] for n in _NAMES}
    M = {n: args["m_" + n] for n in _NAMES}
    V = {n: args["v_" + n] for n in _NAMES}

    me = 4 * lax.axis_index("x") + 2 * lax.axis_index("y") + lax.axis_index("c")

    in_flight = {}

    def launch(groups, tok):
        shards, n_of = [], {}
        for grp in groups:
            for name, l in _GROUPS[grp]:
                w = P[name][l] if tok is None else P[name][l] + tok
                shards.append(w.astype(BF16))
            if grp == ("ab", 0):
                shards.append(_pack([P[n] for n in _SMALL_SHARDED], _SS_ROWS + 4))
            n_of[grp] = len(shards)
        res = _send_start(shards, False, "gather_start_" + "_".join(g[0] + str(g[1]) for g in groups))
        lo = 0
        for grp in groups:
            in_flight[grp] = [r[lo:n_of[grp]] for r in res[:4]]
            lo = n_of[grp]
        return res[4][:1, :1]

    follow = {("ab", 0): [("ab", 1), ("xa", 0), ("f", 0)], ("xa", 0): [("c", 0), ("xa", 1)], ("f", 0): [("f", 1)]}

    def fetch(grp, after):
        send_s, recv_s, srcs, lands = in_flight.pop(grp)
        srcs, lands, tok = _send_wait(send_s, recv_s, srcs, lands, after, False, f"gather_wait_{grp[0]}{grp[1]}")
        tok = launch(follow[grp], tok[:1, :1]) if grp in follow else None
        full = [_with_own(land, src, me, False) for land, src in zip(lands, srcs)]
        out = {}
        for (name, l), g in zip(_GROUPS[grp], full):
            w = _weight_layout(name, g)
            if _BIG[name][0] == 1:
                out[name] = w
            else:
                out[name] = {l: w}
        if grp == ("ab", 0):
            out.update(_small_layouts(full[-1]))
        return out, tok

    zero = launch([("ab", 0)], None)

    pending, held = [], []
    rides_with_next = {("xa", 1), ("f", 0)}

    def send(grp, G):
        held.extend(_SEND_GROUPS[grp])
        if grp in rides_with_next:
            return None
        members = tuple(held)
        del held[:]
        res = _send_start([_grad_blocks(name, l, G) for name, l in members], True, f"send_{grp[0]}{grp[1]}")
        pending.append((members, res))
        return res[4][:1, :1]

    W = {n: P[n] for n in _REPL}
    W["ab_norm"] = P["ab_norm"] + zero
    W["final_norm"] = P["final_norm"].reshape(1, D)
    W["a_gate_x_w"], W["a_gate_a_w"], W["b_group_w"] = P["a_gate_x_w"][0], P["a_gate_a_w"][0], P["b_group_w"][0]
    loss, grad_x, G = _local_step(x[0], mem[0], loss_target[0], W, fetch, send)
    loss = lax.psum(loss[0, 0], ("x", "y", "c"))

    Gs = dict(G)
    Gs["c_b_pw1"] = _unpair_blocks(G["c_b_pw1"], _CW_C)
    Gs["f_dw_w"] = jnp.stack([G["f_dw_w0"], G["f_dw_w1"]])
    Gs["a_conv_w"], Gs["c_dw_w"] = G["a_conv_w"][None], G["c_dw_w"][None]
    for n in ("xa_norm", "xa_mem_norm", "f_norm", "f_dw_b"):
        Gs[n] = jnp.concatenate([G[f"{n}0"], G[f"{n}1"]], axis=0)
    for n in ("a_gate_x_w", "a_gate_a_w", "b_group_w"):
        Gs[n] = G[n][None]
    repl_flat = jnp.concatenate([Gs[n].reshape(-1) for n in _REPL])
    repl_rows = jnp.pad(repl_flat, (0, N_DEV * _REPL_ROWS * LANE - _N_REPL)).reshape(N_DEV, _REPL_ROWS, LANE)
    ss_rows = jnp.concatenate([_to_dest_major(Gs[n], s) for n, s in _SMALL_SHARDED.items()], axis=1)
    ss_rows = ss_rows.reshape(N_DEV, _SS_ROWS, LANE)
    small_pack = jnp.concatenate(
        [repl_rows, ss_rows, jnp.zeros((N_DEV, _SMALL_ROWS - _REPL_ROWS - _SS_ROWS, LANE), F32)], axis=1)
    last = _send_start([small_pack], True, "send_small")
    pending.append(((("small", 0),), last))

    def arrived(some, after, name):
        members = [m for mem_, _ in some for m in mem_]
        cat = [[a for _, res in some for a in res[i]] for i in range(4)]
        srcs, lands, _ = _send_wait(cat[0], cat[1], cat[2], cat[3], after, True, name)
        return {m: _with_own(land, src, me, True) for m, land, src in zip(members, lands, srcs)}

    out_g, out_d, out_m, out_v = {}, {}, {}, {}
    chain = [None]

    def update(name, landed):
        layers, r, c = _BIG[name]
        w2, m2, v2 = [t[name].reshape(layers * r, c) for t in (P, M, V)]
        res = None
        for l in range(layers):
            res = _sum_adamw(landed[(name, l)], w2, m2, v2, f"adamw_{name}{l}", layer=l, prev=res,
                             after=chain[0] if l == 0 else None)
        chain[0] = res[1]
        out_g[name], out_d[name], out_m[name], out_v[name] = [t.reshape(P[name].shape) for t in res]

    landed = arrived(pending[:-2], grad_x, "send_wait_early")
    for name in _BIG:
        if name != "ab_w_in":
            update(name, landed)
    landed = arrived(pending[-2:], out_v["f_w_down"], "send_wait_late")
    update("ab_w_in", landed)

    small_sum = _sum8(landed[("small", 0)], "sum_small")
    (repl_all,) = _all_gather([small_sum[:_REPL_ROWS]], "gather_small_grads")
    for out, got in zip((out_g, out_d, out_m, out_v),
                        _adamw_small(repl_all.reshape(N_DEV * _REPL_ROWS, LANE), small_sum, P, M, V)):
        out.update(got)

    return (loss, grad_x[None], *[out_g[n] for n in _NAMES], *[out_d[n] for n in _NAMES],
            *[out_m[n] for n in _NAMES], *[out_v[n] for n in _NAMES])


_NAMES = ("ab_norm", "ab_w_in", "a_conv_w", "a_conv_b", "a_gate_x_w", "a_gate_x_b", "a_gate_a_w", "a_gate_a_b",
          "a_lambda", "b_group_w", "b_group_b", "b_scale", "ab_w_out", "c_norm", "c_w_pw1", "c_b_pw1", "c_dw_w",
          "c_dw_b", "c_ln_g", "c_ln_b", "c_w_pw2", "c_b_pw2", "xa_norm", "xa_mem_norm", "xa_wq", "xa_wk", "xa_wv",
          "xa_wo", "f_norm", "f_w_up", "f_dw_w", "f_dw_b", "f_w_down", "final_norm")
```

```python
import functools

import jax
import jax.numpy as jnp
from jax import lax
from jax.experimental import pallas as pl
from jax.experimental.pallas import tpu as pltpu

F32, BF16 = jnp.float32, jnp.bfloat16
SDS = jax.ShapeDtypeStruct
MESH = pl.DeviceIdType.MESH

N_DEV = 8
D = 1024
N_MEM = 256
XA_HEADS, XA_HD = 4, 256
HD_A = 128
CONV_A, CONV_C, CONV_F = 4, 31, 3
C_RG = 8.0
POOL_WINDOWS = (2, 4, 8, 16)
D_FF = 3 * D
EPS = 1e-6
ADAM_LR, ADAM_B1, ADAM_B2, ADAM_EPS, ADAM_WD, ADAM_STEP = 0.001, 0.9, 0.999, 1e-08, 0.01, 10

LANE = 128
SUB = 8
VMEM_LIMIT = 56 * 1024 * 1024
R_SEQ = 512
R_RGLRU = 256
R_FFN = 1024
TM_ROW = 512


def _cp(n_axes):
    return pltpu.CompilerParams(dimension_semantics=("arbitrary",) * n_axes, vmem_limit_bytes=VMEM_LIMIT)


def _tile(n, pref):
    if n <= pref:
        return n
    best = None
    for t in range(LANE, pref + 1, LANE):
        if n % t == 0:
            best = t
    assert best is not None, (n, pref)
    return best


def _perm2(n):
    return (n % 2) * 4 + n // 2


_NN = (((1,), (0,)), ((), ()))
_NT = (((1,), (1,)), ((), ()))
_TN = (((0,), (0,)), ((), ()))


def _mm_call(name, grid, ab, ab_specs, dims, acc_shape, extras, outs, finish, from_ref=False):
    nk = grid[2]
    n_ab, n_ex, n_out = len(ab), len(extras), len(outs)
    use_acc = nk > 1 or from_ref

    def product(refs):
        r = lax.dot_general(refs[0][...], refs[1][...], dims, preferred_element_type=F32)
        for i in range(1, n_ab):
            r = r + lax.dot_general(refs[2 * i][...], refs[2 * i + 1][...], dims, preferred_element_type=F32)
        return r

    def body(*refs):
        rest = refs[2 * n_ab:]
        ex_refs, o_refs = rest[:n_ex], rest[n_ex:n_ex + n_out]
        first_rows = pl.program_id(0) == 0
        if not use_acc:
            finish(product(refs), ex_refs, o_refs, first_rows)
            return
        acc = rest[n_ex + n_out]
        if nk == 1:
            acc[...] = product(refs)
            finish(acc, ex_refs, o_refs, first_rows)
            return
        k = pl.program_id(2)

        @pl.when(k == 0)
        def _():
            acc[...] = jnp.zeros_like(acc)

        acc[...] += product(refs)

        @pl.when(k == nk - 1)
        def _():
            finish(acc if from_ref else acc[...], ex_refs, o_refs, first_rows)

    res = pl.pallas_call(
        body, out_shape=[o for o, _ in outs], grid=grid,
        in_specs=list(ab_specs) + [s for _, s in extras], out_specs=[s for _, s in outs],
        scratch_shapes=[pltpu.VMEM(acc_shape, F32)] if use_acc else [], name=name, compiler_params=_cp(3),
    )(*[t for pair in ab for t in pair], *[e for e, _ in extras])
    return res[0] if n_out == 1 else res


def _finish_sum(r, ex_refs, o_refs, first_rows):
    del first_rows
    for e in ex_refs:
        r = r + e[...]
    o_refs[0][...] = r.astype(o_refs[0].dtype)


def _finish_sum_norm(r, ex_refs, o_refs, first_rows):
    del first_rows
    for e in ex_refs[:-1]:
        r = r + e[...]
    o_refs[0][...] = r
    o_refs[1][...] = ((r * lax.rsqrt(jnp.mean(r * r, axis=-1, keepdims=True) + EPS)) * ex_refs[-1][...]).astype(BF16)


_EPI_ROWS = 16


def _finish_rms_bwd(r_ref, ex_refs, o_refs, first_rows):
    x_ref, g_ref, dres_ref = ex_refs
    dx_ref, dxb_ref, dg_ref = o_refs

    @pl.when(first_rows)
    def _():
        dg_ref[...] = jnp.zeros_like(dg_ref)

    gv = g_ref[...]
    inv_d = 1.0 / r_ref.shape[1]

    def step(i, dg_acc):
        groups = [pl.ds(pl.multiple_of(i * (2 * _EPI_ROWS) + u * _EPI_ROWS, _EPI_ROWS), _EPI_ROWS) for u in range(2)]
        sums = []
        for rows in groups:
            r, xf = r_ref[rows, :], x_ref[rows, :]
            sums.append((jnp.sum(xf * xf, axis=-1, keepdims=True), jnp.sum((r * gv) * xf, axis=-1, keepdims=True)))
        for rows, (sxx, sax) in zip(groups, sums):
            r, xf = r_ref[rows, :], x_ref[rows, :]
            rs = lax.rsqrt(sxx * inv_d + EPS)
            dg_acc = dg_acc + _psum8(r * (xf * rs))
            dx = rs * (r * gv) - xf * (rs * rs * (sax * rs * inv_d)) + dres_ref[rows, :]
            dx_ref[rows, :] = dx
            dxb_ref[rows, :] = dx.astype(BF16)
        return dg_acc

    dg_acc = lax.fori_loop(0, r_ref.shape[0] // (2 * _EPI_ROWS), step, jnp.zeros((SUB, r_ref.shape[1]), F32))
    dg_ref[...] += jnp.sum(dg_acc, axis=0, keepdims=True)


def _rms_bwd_io(M, tm, x, g, dres):
    rows = pl.BlockSpec((tm, D), lambda m, n, k: (m, 0))
    vec = pl.BlockSpec((1, D), lambda m, n, k: (0, 0))
    return ([(x, rows), (g, vec), (dres, rows)],
            [(SDS((M, D), F32), rows), (SDS((M, D), BF16), rows), (SDS((1, D), F32), vec)])


_K_WHOLE = 3072


def _mm_nn(a, b, *, out_dtype, name, bias=None, add=None, norm=None):
    M, K = a.shape
    tk = K if K <= _K_WHOLE else _tile(K, 1024)
    if K <= 1024 and norm is None:
        tm = _tile(M, 2048 if add is None and out_dtype == BF16 else 1024)
    else:
        tm = _tile(M, 512)
    if b.ndim == 3:
        nb, _, bw = b.shape
        N, tn, nn = nb * bw, bw, nb
        b_spec = pl.BlockSpec((None, tk, bw), lambda m, n, k: (_perm2(n), k, 0))
    else:
        N = b.shape[1]
        tn = _tile(N, 1024)
        nn = N // tn
        b_spec = pl.BlockSpec((tk, tn), lambda m, n, k: (k, n))
    tile = pl.BlockSpec((tm, tn), lambda m, n, k: (m, n))
    vec = pl.BlockSpec((1, tn), lambda m, n, k: (0, n))
    extras = ([] if bias is None else [(bias, vec)]) + ([] if add is None else [(add, tile)])
    outs, finish = [(SDS((M, N), out_dtype), tile)], _finish_sum
    if norm is not None:
        assert tn == N == D and out_dtype == F32
        extras.append((norm, vec))
        outs, finish = outs + [(SDS((M, N), BF16), tile)], _finish_sum_norm
    return _mm_call(name, (M // tm, nn, K // tk), [(a, b)], [pl.BlockSpec((tm, tk), lambda m, n, k: (m, k)), b_spec],
                    _NN, (tm, tn), extras, outs, finish)


def _mm_nt(a, b, *, out_dtype, name, add=None, rms=None):
    M, N = a.shape
    if b.ndim == 3:
        nb, Ko, bw = b.shape
        tm = _tile(M, 1024)
        tn, tk, nk = _tile(Ko, 1024), bw, nb
        b_spec = pl.BlockSpec((None, tn, bw), lambda m, n, k: (_perm2(k), n, 0))
    else:
        Ko = b.shape[0]
        tk = N if N <= _K_WHOLE else _tile(N, 1024)
        if N <= 1024 and rms is None:
            tm = _tile(M, 2048 if add is None and out_dtype == BF16 else 1024)
        else:
            tm = _tile(M, 512)
        tn = _tile(Ko, 1024)
        nk = N // tk
        b_spec = pl.BlockSpec((tn, tk), lambda m, n, k: (n, k))
    tile = pl.BlockSpec((tm, tn), lambda m, n, k: (m, n))
    extras = [] if add is None else [(add, tile)]
    outs, finish = [(SDS((M, Ko), out_dtype), tile)], _finish_sum
    if rms is not None:
        assert tn == Ko == D and add is None
        (extras, outs), finish = _rms_bwd_io(M, tm, *rms), _finish_rms_bwd
    return _mm_call(name, (M // tm, Ko // tn, nk), [(a, b)], [pl.BlockSpec((tm, tk), lambda m, n, k: (m, k)), b_spec],
                    _NT, (tm, tn), extras, outs, finish, from_ref=rms is not None)


def _mm_nt_cols(parts, b, *, name, rms):
    M = parts[0].shape[0]
    tm = _tile(M, 512)
    specs, off = [], 0
    for p in parts:
        w = p.shape[1]
        assert off % w == 0
        specs.append(pl.BlockSpec((tm, w), lambda m, n, k: (m, 0)))
        specs.append(pl.BlockSpec((D, w), functools.partial(lambda m, n, k, o: (0, o), o=off // w)))
        off += w
    extras, outs = _rms_bwd_io(M, tm, *rms)
    return _mm_call(name, (M // tm, 1, 1), [(p, b) for p in parts], specs, _NT, (tm, D), extras, outs, _finish_rms_bwd,
                    from_ref=True)


def _mm_tn(a, b, *, out_dtype, name, blocks=None):
    S, Ka = a.shape
    Nb = b.shape[1]
    tm = _tile(Ka, 1024)
    if blocks is not None:
        bw = blocks
        tn, nn = bw, Nb // bw
        out = (SDS((nn, Ka, bw), out_dtype), pl.BlockSpec((None, tm, bw), lambda m, n, k: (_perm2(n), m, 0)))
    else:
        tn = _tile(Nb, 1024)
        nn = Nb // tn
        out = (SDS((Ka, Nb), out_dtype), pl.BlockSpec((tm, tn), lambda m, n, k: (m, n)))
    steps = (Ka // tm) * nn
    tk = _tile(S, 4096 if steps >= 4 else 2048 if steps >= 2 else 1024)
    return _mm_call(name, (Ka // tm, nn, S // tk), [(a, b)],
                    [pl.BlockSpec((tk, tm), lambda m, n, k: (k, m)), pl.BlockSpec((tk, tn), lambda m, n, k: (k, n))],
                    _TN, (tm, tn), [], [out], _finish_sum)


def _row(tm, c):
    return pl.BlockSpec((tm, c), lambda i: (i, 0))


def _full(shape):
    nd = len(shape)
    return pl.BlockSpec(shape, lambda i: (0,) * nd)


def _rms_fwd(x, g, name):
    S = x.shape[0]
    tm = min(S, TM_ROW)

    def body(x_ref, g_ref, o_ref):
        xf = x_ref[...]
        r = lax.rsqrt(jnp.mean(xf * xf, axis=-1, keepdims=True) + EPS)
        o_ref[...] = ((xf * r) * g_ref[...]).astype(BF16)

    return pl.pallas_call(body, out_shape=SDS((S, D), BF16), grid=(S // tm,), in_specs=[_row(tm, D), _full((1, D))],
                          out_specs=_row(tm, D), name=name, compiler_params=_cp(1))(x, g)


def _rms_bwd(x, g, dn, dres, name):
    S = x.shape[0]
    tm = min(S, TM_ROW)
    want_dx = dres is not None

    def body(x_ref, g_ref, dn_ref, *rest):
        i = pl.program_id(0)
        dg_ref = rest[-1]

        @pl.when(i == 0)
        def _():
            dg_ref[...] = jnp.zeros_like(dg_ref)

        xf = x_ref[...]
        r = lax.rsqrt(jnp.mean(xf * xf, axis=-1, keepdims=True) + EPS)
        y = xf * r
        dn_v = dn_ref[...]
        dg_ref[...] += jnp.sum(dn_v * y, axis=0, keepdims=True)
        if want_dx:
            dres_ref, dx_ref, dxb_ref = rest[0], rest[1], rest[2]
            dy = dn_v * g_ref[...]
            dx = r * (dy - y * jnp.mean(dy * y, axis=-1, keepdims=True)) + dres_ref[...]
            dx_ref[...] = dx
            dxb_ref[...] = dx.astype(BF16)

    ins = [x, g, dn] + ([dres] if want_dx else [])
    in_specs = [_row(tm, D), _full((1, D)), _row(tm, D)] + ([_row(tm, D)] if want_dx else [])
    outs = ([SDS((S, D), F32), SDS((S, D), BF16)] if want_dx else []) + [SDS((1, D), F32)]
    out_specs = ([_row(tm, D), _row(tm, D)] if want_dx else []) + [_full((1, D))]
    return pl.pallas_call(body, out_shape=outs, grid=(S // tm,), in_specs=in_specs, out_specs=out_specs, name=name,
                          compiler_params=_cp(1))(*ins)


def _loss_head(x, g, tgt):
    S = x.shape[0]
    tm = min(S, TM_ROW)

    def body(x_ref, g_ref, t_ref, loss_ref, dx_ref, dxb_ref, dg_ref):
        i = pl.program_id(0)

        @pl.when(i == 0)
        def _():
            loss_ref[...] = jnp.zeros_like(loss_ref)
            dg_ref[...] = jnp.zeros_like(dg_ref)

        xf = x_ref[...]
        r = lax.rsqrt(jnp.mean(xf * xf, axis=-1, keepdims=True) + EPS)
        y = xf * r
        gv = g_ref[...]
        err = y * gv - t_ref[...]
        per_row = jnp.mean(err * err, axis=-1, keepdims=True)
        loss_ref[...] += 0.5 * jnp.sum(per_row, axis=0, keepdims=True)
        dn_v = err * (1.0 / D)
        dg_ref[...] += jnp.sum(dn_v * y, axis=0, keepdims=True)
        dy = dn_v * gv
        dx = r * (dy - y * jnp.mean(dy * y, axis=-1, keepdims=True))
        dx_ref[...] = dx
        dxb_ref[...] = dx.astype(BF16)

    return pl.pallas_call(
        body, out_shape=[SDS((1, 1), F32), SDS((S, D), F32), SDS((S, D), BF16), SDS((1, D), F32)], grid=(S // tm,),
        in_specs=[_row(tm, D), _full((1, D)), _row(tm, D)],
        out_specs=[_full((1, 1)), _row(tm, D), _row(tm, D), _full((1, D))], name="loss_head", compiler_params=_cp(1),
    )(x, g, tgt)


def _softmax_rows(s):
    m = jnp.max(s, axis=-1, keepdims=True)
    e = jnp.exp(s - m)
    return e / jnp.sum(e, axis=-1, keepdims=True)


def _attn_fwd(q, k, v, name):
    S = q.shape[0]
    tm = min(S, TM_ROW)
    scale = XA_HD ** -0.5

    def body(q_ref, k_ref, v_ref, o_ref):
        for h in range(XA_HEADS):
            sl = slice(h * XA_HD, (h + 1) * XA_HD)
            s = lax.dot_general(q_ref[:, sl], k_ref[:, sl], _NT, preferred_element_type=F32) * scale
            p = _softmax_rows(s)
            o_ref[:, sl] = lax.dot_general(p.astype(BF16), v_ref[:, sl], _NN, preferred_element_type=F32).astype(BF16)

    return pl.pallas_call(body, out_shape=SDS((S, D), BF16), grid=(S // tm,),
                          in_specs=[_row(tm, D), _full((N_MEM, D)), _full((N_MEM, D))], out_specs=_row(tm, D),
                          name=name, compiler_params=_cp(1))(q, k, v)


def _attn_bwd(q, k, v, do, name):
    S = q.shape[0]
    tm = min(S, TM_ROW)
    scale = XA_HD ** -0.5

    def body(q_ref, k_ref, v_ref, do_ref, dq_ref, dk_ref, dv_ref):
        i = pl.program_id(0)

        @pl.when(i == 0)
        def _():
            dk_ref[...] = jnp.zeros_like(dk_ref)
            dv_ref[...] = jnp.zeros_like(dv_ref)

        for h in range(XA_HEADS):
            sl = slice(h * XA_HD, (h + 1) * XA_HD)
            qh, kh, vh, doh = q_ref[:, sl], k_ref[:, sl], v_ref[:, sl], do_ref[:, sl]
            s = lax.dot_general(qh, kh, _NT, preferred_element_type=F32) * scale
            p = _softmax_rows(s)
            pb = p.astype(BF16)
            dv_ref[:, sl] += lax.dot_general(pb, doh, _TN, preferred_element_type=F32)
            dp = lax.dot_general(doh, vh, _NT, preferred_element_type=F32)
            ds = (p * (dp - jnp.sum(dp * p, axis=-1, keepdims=True)) * scale).astype(BF16)
            dq_ref[:, sl] = lax.dot_general(ds, kh, _NN, preferred_element_type=F32).astype(BF16)
            dk_ref[:, sl] += lax.dot_general(ds, qh, _TN, preferred_element_type=F32)

    return pl.pallas_call(
        body, out_shape=[SDS((S, D), BF16), SDS((N_MEM, D), F32), SDS((N_MEM, D), F32)], grid=(S // tm,),
        in_specs=[_row(tm, D), _full((N_MEM, D)), _full((N_MEM, D)), _row(tm, D)],
        out_specs=[_row(tm, D), _full((N_MEM, D)), _full((N_MEM, D))], name=name, compiler_params=_cp(1),
    )(q, k, v, do)


def _sigmoid(x):
    return 1.0 / (1.0 + jnp.exp(-x))


def _ln_silu_fwd(cv, g, b):
    S = cv.shape[0]
    tm = min(S, TM_ROW)

    def body(x_ref, g_ref, b_ref, o_ref):
        xf = x_ref[...]
        mu = jnp.mean(xf, axis=-1, keepdims=True)
        xc = xf - mu
        rstd = lax.rsqrt(jnp.mean(xc * xc, axis=-1, keepdims=True) + EPS)
        ln = (xc * rstd) * g_ref[...] + b_ref[...]
        o_ref[...] = (ln * _sigmoid(ln)).astype(BF16)

    return pl.pallas_call(body, out_shape=SDS((S, D), BF16), grid=(S // tm,),
                          in_specs=[_row(tm, D), _full((1, D)), _full((1, D))], out_specs=_row(tm, D),
                          name="ln_silu_fwd", compiler_params=_cp(1))(cv, g, b)


def _ln_silu_bwd(ds, cv, g, b, dx):
    S = cv.shape[0]
    tm = min(S, TM_ROW)

    def body(ds_ref, x_ref, g_ref, b_ref, dx_ref, dcv_ref, dg_ref, db_ref, db2_ref):
        i = pl.program_id(0)

        @pl.when(i == 0)
        def _():
            dg_ref[...] = jnp.zeros_like(dg_ref)
            db_ref[...] = jnp.zeros_like(db_ref)
            db2_ref[...] = jnp.zeros_like(db2_ref)

        xf = x_ref[...]
        mu = jnp.mean(xf, axis=-1, keepdims=True)
        xc = xf - mu
        rstd = lax.rsqrt(jnp.mean(xc * xc, axis=-1, keepdims=True) + EPS)
        xhat = xc * rstd
        gv = g_ref[...]
        ln = xhat * gv + b_ref[...]
        sg = _sigmoid(ln)
        dln = ds_ref[...].astype(F32) * (sg + ln * sg * (1.0 - sg))
        dg_ref[...] += jnp.sum(dln * xhat, axis=0, keepdims=True)
        db_ref[...] += jnp.sum(dln, axis=0, keepdims=True)
        db2_ref[...] += jnp.sum(dx_ref[...], axis=0, keepdims=True)
        dxh = dln * gv
        dcv_ref[...] = rstd * (dxh - jnp.mean(dxh, axis=-1, keepdims=True)
                               - xhat * jnp.mean(dxh * xhat, axis=-1, keepdims=True))

    return pl.pallas_call(
        body, out_shape=[SDS((S, D), F32), SDS((1, D), F32), SDS((1, D), F32), SDS((1, D), F32)], grid=(S // tm,),
        in_specs=[_row(tm, D), _row(tm, D), _full((1, D)), _full((1, D)), _row(tm, D)],
        out_specs=[_row(tm, D), _full((1, D)), _full((1, D)), _full((1, D))], name="ln_silu_bwd",
        compiler_params=_cp(1),
    )(ds, cv, g, b, dx)


_GELU_C, _GELU_K = 0.7978845608028654, 0.044715


def _gelu(x, with_grad=False):
    x2 = x * x
    t = jnp.tanh(_GELU_C * (x + _GELU_K * x * x2))
    gel = 0.5 * x * (1.0 + t)
    if not with_grad:
        return gel
    return gel, 0.5 * (1.0 + t) + 0.5 * x * (1.0 - t * t) * (_GELU_C * (1.0 + 3.0 * _GELU_K * x2))


def _expm1(x):
    poly = x * (1.0 + x * (0.5 + x * (1.0 / 6.0 + x * (1.0 / 24.0 + x * (1.0 / 120.0)))))
    return jnp.where(jnp.abs(x) < 0.05, poly, jnp.exp(x) - 1.0)


def _softplus(x):
    return jnp.maximum(x, 0.0) + jnp.log1p(jnp.exp(-jnp.abs(x)))


_SCAN_UNROLL = 8
_RB = 32
_HB = 16


def _sub_blocks(n_rows, n_lanes, fn):
    def step(idx, c):
        r0 = pl.multiple_of(idx * _RB, _RB)
        for lt in range(n_lanes // LANE):
            fn(r0, lt)
        return c

    lax.fori_loop(0, n_rows // _RB, step, 0)


def _lanes(lt):
    return pl.ds(lt * LANE, LANE)


def _psum8(x):
    parts = [x[i * SUB:(i + 1) * SUB] for i in range(x.shape[0] // SUB)]
    return functools.reduce(lambda p, q: p + q, parts)


def _scan_fwd(a_s, b_s, out_ref, carry_ref, n_groups):
    row = lax.broadcasted_iota(jnp.int32, (SUB, LANE), 0)
    U = _SCAN_UNROLL

    def step(gi, carry):
        base = gi * (SUB * U)
        parts = []
        for u in range(U):
            i = pl.multiple_of(base + u * SUB, SUB)
            a8, b8 = a_s[pl.ds(i, SUB), :], b_s[pl.ds(i, SUB), :]
            for s in (1, 2, 4):
                a_sh = jnp.where(row >= s, pltpu.roll(a8, s, 0), 1.0)
                b_sh = jnp.where(row >= s, pltpu.roll(b8, s, 0), 0.0)
                b8 = a8 * b_sh + b8
                a8 = a8 * a_sh
            parts.append((i, a8, b8))
        for i, a8, b8 in parts:
            h8 = a8 * carry + b8
            out_ref[pl.ds(i, SUB), :] = h8
            carry = jnp.broadcast_to(h8[SUB - 1:SUB, :], (SUB, LANE))
        return carry

    carry_ref[...] = lax.fori_loop(0, n_groups // U, step, carry_ref[...])


def _scan_bwd(a_s, b_s, out_ref, carry_ref, n_groups):
    row = lax.broadcasted_iota(jnp.int32, (SUB, LANE), 0)
    U = _SCAN_UNROLL

    def step(gi, carry):
        base = (n_groups // U - 1 - gi) * (SUB * U)
        parts = []
        for u in reversed(range(U)):
            i = pl.multiple_of(base + u * SUB, SUB)
            a8, b8 = a_s[pl.ds(i, SUB), :], b_s[pl.ds(i, SUB), :]
            for s in (1, 2, 4):
                a_sh = jnp.where(row < SUB - s, pltpu.roll(a8, SUB - s, 0), 1.0)
                b_sh = jnp.where(row < SUB - s, pltpu.roll(b8, SUB - s, 0), 0.0)
                b8 = a8 * b_sh + b8
                a8 = a8 * a_sh
            parts.append((i, a8, b8))
        for i, a8, b8 in parts:
            h8 = a8 * carry + b8
            out_ref[pl.ds(i, SUB), :] = h8
            carry = jnp.broadcast_to(h8[0:1, :], (SUB, LANE))
        return carry

    carry_ref[...] = lax.fori_loop(0, n_groups // U, step, carry_ref[...])


def _rglru_pre(xr, wgx_ref, bgx_ref, wga_ref, bga_ref, lam_ref):
    xrb = xr.astype(BF16)
    wgx, wga = wgx_ref[0].astype(BF16), wga_ref[0].astype(BF16)
    gx = _sigmoid(lax.dot_general(xrb, wgx, _NN, preferred_element_type=F32) + bgx_ref[...])
    ga = _sigmoid(lax.dot_general(xrb, wga, _NN, preferred_element_type=F32) + bga_ref[...])
    sp = _softplus(-lam_ref[...])
    log_a = -C_RG * ga * sp
    a = jnp.exp(log_a)
    mult = jnp.sqrt(-_expm1(2.0 * log_a))
    return gx, ga, sp, a, mult, xrb, wgx, wga


def _a_specs():
    vec = pl.BlockSpec((1, HD_A), lambda c, j: (0, c))
    mat = pl.BlockSpec((1, HD_A, HD_A), lambda c, j: (c, 0, 0))
    return [pl.BlockSpec((CONV_A, HD_A), lambda c, j: (0, c)), vec, mat, vec, mat, vec, vec]


def _a_fwd(zp, conv_w, conv_b, wgx, bgx, wga, bga, lam):
    S = zp.shape[0]
    R, nt = R_RGLRU, D // HD_A
    H = SUB

    def body(zg_ref, zr_ref, cw_ref, cb_ref, wgx_ref, bgx_ref, wga_ref, bga_ref, lam_ref, ya_ref, h_ref,
             ext, a_s, b_s, hc):
        j = pl.program_id(1)

        @pl.when(j == 0)
        def _():
            ext[0:H, :] = jnp.zeros((H, HD_A), F32)
            hc[...] = jnp.zeros_like(hc)

        ext[H:H + R, :] = zr_ref[...].astype(F32)
        xr = cb_ref[...]
        for k in range(CONV_A):
            xr = xr + cw_ref[k:k + 1, :] * ext[pl.ds(H - (CONV_A - 1 - k), R), :]
        gx, _, _, a, mult, _, _, _ = _rglru_pre(xr, wgx_ref, bgx_ref, wga_ref, bga_ref, lam_ref)
        a_s[...] = a
        b_s[...] = mult * (gx * xr)
        _scan_fwd(a_s, b_s, h_ref, hc, R // SUB)
        ya_ref[...] = (_gelu(zg_ref[...].astype(F32)) * h_ref[...]).astype(BF16)
        ext[0:H, :] = ext[R:R + H, :]

    return pl.pallas_call(
        body, out_shape=[SDS((S, D + D // 2), BF16), SDS((S, D), F32)], grid=(nt, S // R),
        in_specs=[pl.BlockSpec((R, HD_A), lambda c, j: (j, c)), pl.BlockSpec((R, HD_A), lambda c, j: (j, nt + c))]
        + _a_specs(),
        out_specs=[pl.BlockSpec((R, HD_A), lambda c, j: (j, c)), pl.BlockSpec((R, HD_A), lambda c, j: (j, c))],
        scratch_shapes=[pltpu.VMEM((H + R, HD_A), F32), pltpu.VMEM((R, HD_A), F32), pltpu.VMEM((R, HD_A), F32),
                        pltpu.VMEM((SUB, HD_A), F32)],
        name="rglru_fwd", compiler_params=_cp(2),
    )(zp, zp, conv_w, conv_b, wgx, bgx, wga, bga, lam)


def _a_bwd(dyab, zp, h, conv_w, conv_b, wgx, bgx, wga, bga, lam):
    S = zp.shape[0]
    R, nt, nch = R_RGLRU, D // HD_A, S // R_RGLRU
    H = SUB

    def rows(c, j):
        return (nch - 1 - j, c)

    def rows_rec(c, j):
        return (nch - 1 - j, nt + c)

    def halo(c, j):
        return (jnp.maximum((nch - 1 - j) * (R // H) - 1, 0), c)

    def halo_z(c, j):
        return (jnp.maximum((nch - 1 - j) * (R // _HB) - 1, 0), nt + c)

    def body(dy_ref, zg_ref, zr_ref, zh_ref, h_ref, hh_ref, cw_ref, cb_ref, wgx_ref, bgx_ref, wga_ref, bga_ref,
             lam_ref, dzg_ref, dzr_ref, dcw_ref, dcb_ref, dwgx_ref, dbgx_ref, dwga_ref, dbga_ref, dlam_ref,
             ext_z, ext_h, ext_mu, ext_d, a_s, b_s, muc):
        j = pl.program_id(1)
        first_chunk = (nch - 1 - j) == 0

        @pl.when(j == 0)
        def _():
            ext_mu[R:R + H, :] = jnp.zeros((H, HD_A), F32)
            ext_d[R:R + H, :] = jnp.zeros((H, HD_A), F32)
            muc[...] = jnp.zeros_like(muc)
            for r in (dcw_ref, dcb_ref, dwgx_ref, dbgx_ref, dwga_ref, dbga_ref, dlam_ref):
                r[...] = jnp.zeros_like(r)

        zg = zg_ref[...].astype(F32)
        ext_z[0:H, :] = jnp.where(first_chunk, 0.0, zh_ref[_HB - H:_HB, :].astype(F32))
        ext_z[H:H + R, :] = zr_ref[...].astype(F32)
        ext_h[0:H, :] = jnp.where(first_chunk, 0.0, hh_ref[...])
        ext_h[H:H + R, :] = h_ref[...]
        xr = cb_ref[...]
        for k in range(CONV_A):
            xr = xr + cw_ref[k:k + 1, :] * ext_z[pl.ds(H - (CONV_A - 1 - k), R), :]
        gx, ga, sp, a, mult, xrb, wgxb, wgab = _rglru_pre(xr, wgx_ref, bgx_ref, wga_ref, bga_ref, lam_ref)
        gel, dgel = _gelu(zg, with_grad=True)
        dy = dy_ref[...].astype(F32)
        dh = dy * gel
        dzg_ref[...] = (dy * h_ref[...] * dgel).astype(BF16)
        a_s[...] = a
        b_s[...] = a * dh
        _scan_bwd(a_s, b_s, ext_mu, muc, R // SUB)
        lam_t = dh + ext_mu[pl.ds(1, R), :]
        ext_mu[R:R + H, :] = ext_mu[0:H, :]
        da = lam_t * ext_h[pl.ds(H - 1, R), :]
        gxr = gx * xr
        dlog_a = da * a - (lam_t * gxr) * (a * a) / mult
        dgx = lam_t * mult * xr
        dxr = lam_t * mult * gx
        lam_v = lam_ref[...]
        dlam_ref[...] += jnp.sum(dlog_a * ga, axis=0, keepdims=True) * (C_RG * _sigmoid(-lam_v))
        dpa = (dlog_a * (-C_RG * sp)) * ga * (1.0 - ga)
        dpx = dgx * gx * (1.0 - gx)
        dbga_ref[...] += jnp.sum(dpa, axis=0, keepdims=True)
        dbgx_ref[...] += jnp.sum(dpx, axis=0, keepdims=True)
        dpab, dpxb = dpa.astype(BF16), dpx.astype(BF16)
        dwga_ref[0] += lax.dot_general(xrb, dpab, _TN, preferred_element_type=F32)
        dwgx_ref[0] += lax.dot_general(xrb, dpxb, _TN, preferred_element_type=F32)
        dxr = (dxr + lax.dot_general(dpab, wgab, _NT, preferred_element_type=F32)
               + lax.dot_general(dpxb, wgxb, _NT, preferred_element_type=F32))
        dcb_ref[...] += jnp.sum(dxr, axis=0, keepdims=True)
        ext_d[0:R, :] = dxr
        dzr = jnp.zeros((R, HD_A), F32)
        for k in range(CONV_A):
            sh = CONV_A - 1 - k
            dcw_ref[k:k + 1, :] += jnp.sum(dxr * ext_z[pl.ds(H - sh, R), :], axis=0, keepdims=True)
            dzr = dzr + cw_ref[k:k + 1, :] * ext_d[pl.ds(sh, R), :]
        dzr_ref[...] = dzr.astype(BF16)
        ext_d[R:R + H, :] = ext_d[0:H, :]

    vec_o = pl.BlockSpec((1, HD_A), lambda c, j: (0, c))
    mat_o = pl.BlockSpec((1, HD_A, HD_A), lambda c, j: (c, 0, 0))
    return pl.pallas_call(
        body,
        out_shape=[SDS((S, D), BF16), SDS((S, D), BF16), SDS((CONV_A, D), F32), SDS((1, D), F32),
                   SDS((nt, HD_A, HD_A), F32), SDS((1, D), F32), SDS((nt, HD_A, HD_A), F32), SDS((1, D), F32),
                   SDS((1, D), F32)],
        grid=(nt, nch),
        in_specs=[pl.BlockSpec((R, HD_A), rows), pl.BlockSpec((R, HD_A), rows), pl.BlockSpec((R, HD_A), rows_rec),
                  pl.BlockSpec((_HB, HD_A), halo_z), pl.BlockSpec((R, HD_A), rows),
                  pl.BlockSpec((H, HD_A), halo)] + _a_specs(),
        out_specs=[pl.BlockSpec((R, HD_A), rows), pl.BlockSpec((R, HD_A), rows),
                   pl.BlockSpec((CONV_A, HD_A), lambda c, j: (0, c)), vec_o, mat_o, vec_o, mat_o, vec_o, vec_o],
        scratch_shapes=[pltpu.VMEM((H + R, HD_A), F32), pltpu.VMEM((H + R, HD_A), F32), pltpu.VMEM((R + H, HD_A), F32),
                        pltpu.VMEM((R + H, HD_A), F32), pltpu.VMEM((R, HD_A), F32), pltpu.VMEM((R, HD_A), F32),
                        pltpu.VMEM((SUB, HD_A), F32)],
        name="rglru_bwd", compiler_params=_cp(2),
    )(dyab, zp, zp, zp, h, h, conv_w, conv_b, wgx, bgx, wga, bga, lam)


_POOL_H = 16
_POOL_T0 = 2 * D // HD_A
_POOL_Y0 = D // HD_A


def _window_sum(lv, n, lo, rows, g, ahead):
    base = 0 if ahead else SUB
    cur, win = lv[0], None
    for i, s in enumerate((1, 2, 4, 8)):
        val = cur[pl.ds(base, n), :] + cur[pl.ds(base + (s if ahead else -s), n), :]
        sel = val[lo:lo + rows]
        win = sel if win is None else jnp.where(g >= i, sel, win)
        if i < 3:
            lv[i + 1][pl.ds(base, n), :] = val
            cur = lv[i + 1]
    return win


def _pool_width(g):
    return jnp.where(g == 0, 2.0, jnp.where(g == 1, 4.0, jnp.where(g == 2, 8.0, 16.0)))


def _b_fwd(zp, yab, wg, bg, sc):
    S = zp.shape[0]
    R, H = R_SEQ, _POOL_H

    def body(z_ref, wg_ref, bg_ref, sc_ref, yab_in, yb_ref, *lv):
        del yab_in
        g, j = pl.program_id(0), pl.program_id(1)

        @pl.when(j == 0)
        def _():
            for r in lv:
                r[0:SUB, :] = jnp.zeros((SUB, HD_A), F32)
            lv[0][SUB:SUB + H, :] = jnp.zeros((H, HD_A), F32)

        u = z_ref[...].astype(F32)
        lv[0][SUB + H:SUB + H + R, :] = u
        t1 = (j * R + 1 + lax.broadcasted_iota(jnp.int32, (R, HD_A), 0)).astype(F32)
        p = _window_sum(lv, H + R, H, R, g, False) / jnp.minimum(t1, _pool_width(g)) - u
        lin = lax.dot_general(p.astype(BF16), wg_ref[0].astype(BF16), _NN, preferred_element_type=F32) + bg_ref[...]
        yb_ref[...] = (lin * sc_ref[...]).astype(BF16)
        lv[0][SUB:SUB + H, :] = lv[0][SUB + R:SUB + R + H, :]

    vec = pl.BlockSpec((1, HD_A), lambda g, j: (0, g))
    return pl.pallas_call(
        body, out_shape=SDS(yab.shape, yab.dtype), grid=(len(POOL_WINDOWS), S // R),
        in_specs=[pl.BlockSpec((R, HD_A), lambda g, j: (j, _POOL_T0 + g)),
                  pl.BlockSpec((1, HD_A, HD_A), lambda g, j: (g, 0, 0)), vec, vec, pl.BlockSpec(memory_space=pl.ANY)],
        out_specs=pl.BlockSpec((R, HD_A), lambda g, j: (j, _POOL_Y0 + g)),
        scratch_shapes=[pltpu.VMEM((SUB + H + R, HD_A), F32)] * 4, input_output_aliases={4: 0},
        name="pool_fwd", compiler_params=_cp(2),
    )(zp, wg, bg, sc, yab)


def _b_bwd(dyab, zp, wg, bg, sc):
    S = zp.shape[0]
    R, H, nch, ng = R_SEQ, _POOL_H, S // R_SEQ, len(POOL_WINDOWS)

    def body(dy_ref, z_ref, zh_ref, wg_ref, bg_ref, sc_ref, dz_ref, dwg_ref, dbg_ref, dsc_ref, *scratch):
        lu, lq = scratch[:4], scratch[4:]
        g, j = pl.program_id(0), pl.program_id(1)
        jj = nch - 1 - j

        @pl.when(j == 0)
        def _():
            for r in lu:
                r[0:SUB, :] = jnp.zeros((SUB, HD_A), F32)
            for r in lq:
                r[R + H:R + H + SUB, :] = jnp.zeros((SUB, HD_A), F32)
            lq[0][R:R + H, :] = jnp.zeros((H, HD_A), F32)
            for r in (dwg_ref, dbg_ref, dsc_ref):
                r[...] = jnp.zeros_like(r)

        u = z_ref[...].astype(F32)
        lu[0][SUB:SUB + H, :] = jnp.where(jj == 0, 0.0, zh_ref[...].astype(F32))
        lu[0][SUB + H:SUB + H + R, :] = u
        t1 = (jj * R + 1 + lax.broadcasted_iota(jnp.int32, (R, HD_A), 0)).astype(F32)
        cnt = jnp.minimum(t1, _pool_width(g))
        pb = (_window_sum(lu, H + R, H, R, g, False) / cnt - u).astype(BF16)
        wgb = wg_ref[0].astype(BF16)
        lin = lax.dot_general(pb, wgb, _NN, preferred_element_type=F32) + bg_ref[...]
        dy = dy_ref[...].astype(F32)
        dsc_ref[...] += jnp.sum(dy * lin, axis=0, keepdims=True)
        dlin = dy * sc_ref[...]
        dbg_ref[...] += jnp.sum(dlin, axis=0, keepdims=True)
        dlb = dlin.astype(BF16)
        dwg_ref[0] += lax.dot_general(pb, dlb, _TN, preferred_element_type=F32)
        dp = lax.dot_general(dlb, wgb, _NT, preferred_element_type=F32)
        lq[0][0:R, :] = dp / cnt
        dz_ref[...] = (_window_sum(lq, R + H, 0, R, g, True) - dp).astype(BF16)
        lq[0][R:R + H, :] = lq[0][0:H, :]

    vec = pl.BlockSpec((1, HD_A), lambda g, j: (0, g))
    mat = pl.BlockSpec((1, HD_A, HD_A), lambda g, j: (g, 0, 0))
    return pl.pallas_call(
        body, out_shape=[SDS((S, D // 2), BF16), SDS((ng, HD_A, HD_A), F32), SDS((1, D // 2), F32),
                         SDS((1, D // 2), F32)],
        grid=(ng, nch),
        in_specs=[pl.BlockSpec((R, HD_A), lambda g, j: (nch - 1 - j, _POOL_Y0 + g)),
                  pl.BlockSpec((R, HD_A), lambda g, j: (nch - 1 - j, _POOL_T0 + g)),
                  pl.BlockSpec((H, HD_A), lambda g, j: (jnp.maximum((nch - 1 - j) * (R // H) - 1, 0), _POOL_T0 + g)),
                  mat, vec, vec],
        out_specs=[pl.BlockSpec((R, HD_A), lambda g, j: (nch - 1 - j, g)), mat, vec, vec],
        scratch_shapes=[pltpu.VMEM((SUB + H + R, HD_A), F32)] * 8,
        name="pool_bwd", compiler_params=_cp(2),
    )(dyab, zp, zp, wg, bg, sc)


_CW_F = 768


def _f_fwd(hp, w, b, name):
    S = hp.shape[0]
    R, H, cw = min(S, R_FFN), SUB, _CW_F
    nlt = cw // LANE

    def body(h_ref, w_ref, b_ref, o_ref, gel_ref, ud_ref, ext):
        j = pl.program_id(1)

        @pl.when(j == 0)
        def _():
            ext[:, 0:H, :] = jnp.zeros((nlt, H, LANE), F32)

        def stage(r0, lt):
            ext[lt, pl.ds(pl.multiple_of(r0 + H, SUB), _RB), :] = h_ref[pl.ds(r0, _RB), _lanes(lt)].astype(F32)

        def main(r0, lt):
            ls = _lanes(lt)
            gp = b_ref[:, ls]
            for k in range(CONV_F):
                gp = gp + w_ref[k:k + 1, ls] * ext[lt, pl.ds(r0 + (H - (CONV_F - 1 - k)), _RB), :]
            up = h_ref[pl.ds(r0, _RB), _lanes(lt + nlt)].astype(F32)
            gel, dgel = _gelu(gp, with_grad=True)
            rs = pl.ds(r0, _RB)
            o_ref[rs, ls] = (gel * up).astype(BF16)
            gel_ref[rs, ls] = gel.astype(BF16)
            ud_ref[rs, ls] = (up * dgel).astype(BF16)

        _sub_blocks(R, cw, stage)
        _sub_blocks(R, cw, main)
        ext[:, 0:H, :] = ext[:, R:R + H, :]

    tile = pl.BlockSpec((R, cw), lambda c, j: (j, c))
    return pl.pallas_call(
        body, out_shape=[SDS((S, D_FF), BF16)] * 3, grid=(D_FF // cw, S // R),
        in_specs=[pl.BlockSpec((R, 2 * cw), lambda c, j: (j, c)), pl.BlockSpec((CONV_F, cw), lambda c, j: (0, c)),
                  pl.BlockSpec((1, cw), lambda c, j: (0, c))],
        out_specs=[tile] * 3,
        scratch_shapes=[pltpu.VMEM((nlt, H + R, LANE), F32)], name=name, compiler_params=_cp(2),
    )(hp, w, b)


def _f_bwd(dact, hp, gel, ud, w, name):
    S = hp.shape[0]
    R, H, cw = min(S, R_FFN), SUB, _CW_F
    nch = S // R
    nlt = cw // LANE

    def body(da_ref, h_ref, hh_ref, gel_ref, ud_ref, w_ref, dh_ref, dw_ref, db_ref, ext_g, ext_d, acc):
        j = pl.program_id(1)
        jj = nch - 1 - j

        @pl.when(j == 0)
        def _():
            ext_d[:, R:R + H, :] = jnp.zeros((nlt, H, LANE), F32)
            acc[...] = jnp.zeros_like(acc)

        for lt in range(nlt):
            ext_g[lt, 0:H, :] = jnp.where(jj == 0, 0.0, hh_ref[_HB - H:_HB, lt * LANE:(lt + 1) * LANE].astype(F32))

        def stage(r0, lt):
            ext_g[lt, pl.ds(pl.multiple_of(r0 + H, SUB), _RB), :] = h_ref[pl.ds(r0, _RB), _lanes(lt)].astype(F32)

        def first(r0, lt):
            ls, lu, rs = _lanes(lt), _lanes(lt + nlt), pl.ds(r0, _RB)
            da = da_ref[rs, ls].astype(F32)
            dh_ref[rs, lu] = (da * gel_ref[rs, ls].astype(F32)).astype(BF16)
            dgp = da * ud_ref[rs, ls].astype(F32)
            ext_d[lt, rs, :] = dgp
            acc[CONV_F * SUB:(CONV_F + 1) * SUB, ls] += _psum8(dgp)
            for k in range(CONV_F):
                tap = ext_g[lt, pl.ds(r0 + (H - (CONV_F - 1 - k)), _RB), :]
                acc[k * SUB:(k + 1) * SUB, ls] += _psum8(dgp * tap)

        def second(r0, lt):
            ls = _lanes(lt)
            dhg = w_ref[CONV_F - 1:CONV_F, ls] * ext_d[lt, pl.ds(r0, _RB), :]
            for k in range(CONV_F - 1):
                dhg = dhg + w_ref[k:k + 1, ls] * ext_d[lt, pl.ds(r0 + (CONV_F - 1 - k), _RB), :]
            dh_ref[pl.ds(r0, _RB), ls] = dhg.astype(BF16)

        _sub_blocks(R, cw, stage)
        _sub_blocks(R, cw, first)
        _sub_blocks(R, cw, second)
        ext_d[:, R:R + H, :] = ext_d[:, 0:H, :]

        @pl.when(j == nch - 1)
        def _():
            for k in range(CONV_F):
                dw_ref[k:k + 1, :] = jnp.sum(acc[k * SUB:(k + 1) * SUB, :], axis=0, keepdims=True)
            db_ref[...] = jnp.sum(acc[CONV_F * SUB:(CONV_F + 1) * SUB, :], axis=0, keepdims=True)

    rows = lambda c, j: (nch - 1 - j, c)
    return pl.pallas_call(
        body, out_shape=[SDS((S, 2 * D_FF), BF16), SDS((CONV_F, D_FF), F32), SDS((1, D_FF), F32)],
        grid=(D_FF // cw, nch),
        in_specs=[pl.BlockSpec((R, cw), rows), pl.BlockSpec((R, cw), lambda c, j: (nch - 1 - j, 2 * c)),
                  pl.BlockSpec((_HB, cw), lambda c, j: (jnp.maximum((nch - 1 - j) * (R // _HB) - 1, 0), 2 * c)),
                  pl.BlockSpec((R, cw), rows), pl.BlockSpec((R, cw), rows),
                  pl.BlockSpec((CONV_F, cw), lambda c, j: (0, c))],
        out_specs=[pl.BlockSpec((R, 2 * cw), rows), pl.BlockSpec((CONV_F, cw), lambda c, j: (0, c)),
                   pl.BlockSpec((1, cw), lambda c, j: (0, c))],
        scratch_shapes=[pltpu.VMEM((nlt, H + R, LANE), F32), pltpu.VMEM((nlt, R + H, LANE), F32),
                        pltpu.VMEM(((CONV_F + 1) * SUB, cw), F32)], name=name,
        compiler_params=_cp(2),
    )(dact, hp, hp, gel, ud, w)


_CW_C = 256
_H_C = 32


def _c_fwd(h1p, w, b):
    S = h1p.shape[0]
    R, H, cw = R_SEQ, _H_C, _CW_C
    nlt = cw // LANE

    def body(h_ref, w_ref, b_ref, o_ref, ext):
        j = pl.program_id(1)

        @pl.when(j == 0)
        def _():
            ext[:, 0:H, :] = jnp.zeros((nlt, H, LANE), F32)

        def stage(r0, lt):
            rs = pl.ds(r0, _RB)
            gate = h_ref[rs, _lanes(lt + nlt)].astype(F32)
            ext[lt, pl.ds(pl.multiple_of(r0 + H, SUB), _RB), :] = h_ref[rs, _lanes(lt)].astype(F32) * _sigmoid(gate)

        def main(r0, lt):
            ls = _lanes(lt)
            cv = b_ref[:, ls]
            for k in range(CONV_C):
                cv = cv + w_ref[k:k + 1, ls] * ext[lt, pl.ds(r0 + (H - (CONV_C - 1 - k)), _RB), :]
            o_ref[pl.ds(r0, _RB), ls] = cv

        _sub_blocks(R, cw, stage)
        _sub_blocks(R, cw, main)
        ext[:, 0:H, :] = ext[:, R:R + H, :]

    return pl.pallas_call(
        body, out_shape=SDS((S, D), F32), grid=(D // cw, S // R),
        in_specs=[pl.BlockSpec((R, 2 * cw), lambda c, j: (j, c)), pl.BlockSpec((CONV_C, cw), lambda c, j: (0, c)),
                  pl.BlockSpec((1, cw), lambda c, j: (0, c))],
        out_specs=pl.BlockSpec((R, cw), lambda c, j: (j, c)),
        scratch_shapes=[pltpu.VMEM((nlt, H + R, LANE), F32)], name="conf_conv_fwd", compiler_params=_cp(2),
    )(h1p, w, b)


def _c_bwd(dcv, h1p, w):
    S = h1p.shape[0]
    R, H, cw, nch = R_SEQ, _H_C, _CW_C, S // R_SEQ
    nlt = cw // LANE
    a_b, a_val, a_gate = CONV_C * SUB, (CONV_C + 1) * SUB, (CONV_C + 2) * SUB

    def body(dc_ref, h_ref, hh_ref, w_ref, dh_ref, dw_ref, db_ref, db1_ref, ext_u, ext_d, acc):
        j = pl.program_id(1)
        jj = nch - 1 - j

        @pl.when(j == 0)
        def _():
            ext_d[:, R:R + H, :] = jnp.zeros((nlt, H, LANE), F32)
            acc[...] = jnp.zeros_like(acc)

        for lt in range(nlt):
            ext_u[lt, 0:H, :] = jnp.where(
                jj == 0, 0.0, hh_ref[:, lt * LANE:(lt + 1) * LANE].astype(F32)
                * _sigmoid(hh_ref[:, cw + lt * LANE:cw + (lt + 1) * LANE].astype(F32)))

        def stage(r0, lt):
            rs, ls = pl.ds(r0, _RB), _lanes(lt)
            gate = h_ref[rs, _lanes(lt + nlt)].astype(F32)
            ext_u[lt, pl.ds(pl.multiple_of(r0 + H, SUB), _RB), :] = h_ref[rs, ls].astype(F32) * _sigmoid(gate)
            ext_d[lt, rs, :] = dc_ref[rs, ls]

        def first(r0, lt):
            ls = _lanes(lt)
            dc = dc_ref[pl.ds(r0, _RB), ls]
            acc[a_b:a_b + SUB, ls] += _psum8(dc)
            for k in range(CONV_C):
                tap = ext_u[lt, pl.ds(r0 + (H - (CONV_C - 1 - k)), _RB), :]
                acc[k * SUB:(k + 1) * SUB, ls] += _psum8(dc * tap)

        def second(r0, lt):
            rs, ls, lg = pl.ds(r0, _RB), _lanes(lt), _lanes(lt + nlt)
            du = w_ref[CONV_C - 1:CONV_C, ls] * ext_d[lt, rs, :]
            for k in range(CONV_C - 1):
                du = du + w_ref[k:k + 1, ls] * ext_d[lt, pl.ds(r0 + (CONV_C - 1 - k), _RB), :]
            val = h_ref[rs, ls].astype(F32)
            sg = _sigmoid(h_ref[rs, lg].astype(F32))
            dval = du * sg
            dgate = du * val * sg * (1.0 - sg)
            acc[a_val:a_val + SUB, ls] += _psum8(dval)
            acc[a_gate:a_gate + SUB, ls] += _psum8(dgate)
            dh_ref[rs, ls] = dval.astype(BF16)
            dh_ref[rs, lg] = dgate.astype(BF16)

        _sub_blocks(R, cw, stage)
        _sub_blocks(R, cw, first)
        _sub_blocks(R, cw, second)
        ext_d[:, R:R + H, :] = ext_d[:, 0:H, :]

        @pl.when(j == nch - 1)
        def _():
            for k in range(CONV_C):
                dw_ref[k:k + 1, :] = jnp.sum(acc[k * SUB:(k + 1) * SUB, :], axis=0, keepdims=True)
            db_ref[...] = jnp.sum(acc[a_b:a_b + SUB, :], axis=0, keepdims=True)
            db1_ref[:, 0:cw] = jnp.sum(acc[a_val:a_val + SUB, :], axis=0, keepdims=True)
            db1_ref[:, cw:2 * cw] = jnp.sum(acc[a_gate:a_gate + SUB, :], axis=0, keepdims=True)

    rows = lambda c, j: (nch - 1 - j, c)
    return pl.pallas_call(
        body, out_shape=[SDS((S, 2 * D), BF16), SDS((CONV_C, D), F32), SDS((1, D), F32), SDS((1, 2 * D), F32)],
        grid=(D // cw, nch),
        in_specs=[pl.BlockSpec((R, cw), rows), pl.BlockSpec((R, 2 * cw), rows),
                  pl.BlockSpec((H, 2 * cw), lambda c, j: (jnp.maximum((nch - 1 - j) * (R // H) - 1, 0), c)),
                  pl.BlockSpec((CONV_C, cw), lambda c, j: (0, c))],
        out_specs=[pl.BlockSpec((R, 2 * cw), rows), pl.BlockSpec((CONV_C, cw), lambda c, j: (0, c)),
                   pl.BlockSpec((1, cw), lambda c, j: (0, c)), pl.BlockSpec((1, 2 * cw), lambda c, j: (0, c))],
        scratch_shapes=[pltpu.VMEM((nlt, H + R, LANE), F32), pltpu.VMEM((nlt, R + H, LANE), F32),
                        pltpu.VMEM(((CONV_C + 3) * SUB, cw), F32)], name="conf_conv_bwd",
        compiler_params=_cp(2),
    )(dcv, h1p, h1p, w)


def _local_step(x, mem, tgt, W, fetch=None, send=None):
    G = {}
    W = dict(W)

    def arrive(group, after):
        if fetch is None:
            return None
        got, tok = fetch(group, after)
        for key, val in got.items():
            W[key] = {**W.get(key, {}), **val} if isinstance(val, dict) else val
        return tok

    def gain(g, tok):
        return g if tok is None else g + tok

    def sent(group):
        return None if send is None else send(group, G)

    def xattn_fwd(xin, n, l):
        tok = arrive(("xa", l), n)
        mn = _rms_fwd(mem, gain(W["xa_mem_norm"][l:l + 1], tok), f"xa_memnorm_fwd{l}")
        q = _mm_nn(n, W["xa_wq"][l], out_dtype=BF16, name=f"xa_q{l}")
        k = _mm_nn(mn, W["xa_wk"][l], out_dtype=BF16, name=f"xa_k{l}")
        v = _mm_nn(mn, W["xa_wv"][l], out_dtype=BF16, name=f"xa_v{l}")
        o = _attn_fwd(q, k, v, f"xa_attn_fwd{l}")
        xout, nout = _mm_nn(o, W["xa_wo"][l], out_dtype=F32, name=f"xa_o{l}", add=xin, norm=W["f_norm"][l:l + 1])
        return xout, nout, (xin, n, q, mn, k, v, o)

    def xattn_bwd(dx, dxb, saved, l):
        xin, n, q, mn, k, v, o = saved
        do = _mm_nt(dxb, W["xa_wo"][l], out_dtype=BF16, name=f"xa_do{l}")
        G[f"xa_wo{l}"] = _mm_tn(o, dxb, out_dtype=BF16, name=f"xa_dwo{l}")
        dq, dk, dv = _attn_bwd(q, k, v, do, f"xa_attn_bwd{l}")
        dkb, dvb = dk.astype(BF16), dv.astype(BF16)
        G[f"xa_wq{l}"] = _mm_tn(n, dq, out_dtype=BF16, name=f"xa_dwq{l}")
        G[f"xa_wk{l}"] = _mm_tn(mn, dkb, out_dtype=BF16, name=f"xa_dwk{l}")
        G[f"xa_wv{l}"] = _mm_tn(mn, dvb, out_dtype=BF16, name=f"xa_dwv{l}")
        tok = sent(("xa", l))
        dmn = _mm_nt(dkb, W["xa_wk"][l], out_dtype=F32, name=f"xa_dmn_k{l}")
        dmn = _mm_nt(dvb, W["xa_wv"][l], out_dtype=F32, name=f"xa_dmn_v{l}", add=dmn)
        (G[f"xa_mem_norm{l}"],) = _rms_bwd(mem, W["xa_mem_norm"][l:l + 1], dmn, None, f"xa_memnorm_bwd{l}")
        dx, dxb, G[f"xa_norm{l}"] = _mm_nt(dq, W["xa_wq"][l], out_dtype=F32, name=f"xa_dn{l}",
                                           rms=(xin, gain(W["xa_norm"][l:l + 1], tok), dx))
        return dx, dxb

    def ffn_fwd(xin, n, l, next_gain):
        tok = arrive(("f", l), n)
        hp = _mm_nn(n, W["f_w_up"][l], out_dtype=BF16, name=f"f_up{l}")
        act, gel, ud = _f_fwd(hp, W["f_dw_w"][l], gain(W["f_dw_b"][l:l + 1], tok), f"f_conv_fwd{l}")
        arrive(("fd", l), act)
        res = _mm_nn(act, W["f_w_down"][l], out_dtype=F32, name=f"f_down{l}", add=xin, norm=next_gain)
        xout, nout = res if next_gain is not None else (res, None)
        return xout, nout, (xin, n, hp, act, gel, ud)

    def ffn_bwd(dx, dxb, saved, l):
        xin, n, hp, act, gel, ud = saved
        dact = _mm_nt(dxb, W["f_w_down"][l], out_dtype=BF16, name=f"f_dact{l}")
        G[f"f_w_down{l}"] = _mm_tn(act, dxb, out_dtype=BF16, name=f"f_dwdown{l}")
        dhp, G[f"f_dw_w{l}"], G[f"f_dw_b{l}"] = _f_bwd(dact, hp, gel, ud, W["f_dw_w"][l], f"f_conv_bwd{l}")
        G[f"f_w_up{l}"] = _mm_tn(n, dhp, out_dtype=BF16, name=f"f_dwup{l}", blocks=_CW_F)
        tok = sent(("f", l))
        dx, dxb, G[f"f_norm{l}"] = _mm_nt(dhp, W["f_w_up"][l], out_dtype=F32, name=f"f_dn{l}",
                                          rms=(xin, gain(W["f_norm"][l:l + 1], tok), dx))
        return dx, dxb

    n0 = _rms_fwd(x, W["ab_norm"], "ab_norm_fwd")
    tok = arrive(("ab", 0), n0)
    a_par = (W["a_conv_w"], gain(W["a_conv_b"], tok), W["a_gate_x_w"], W["a_gate_x_b"], W["a_gate_a_w"],
             W["a_gate_a_b"], W["a_lambda"])
    b_par = (W["b_group_w"], W["b_group_b"], W["b_scale"])
    zp = _mm_nn(n0, W["ab_w_in"], out_dtype=BF16, name="ab_in")
    yab, h_a = _a_fwd(zp, *a_par)
    yab = _b_fwd(zp, yab, *b_par)
    arrive(("ab", 1), yab)
    x1, n1 = _mm_nn(yab, W["ab_w_out"], out_dtype=F32, name="ab_out", add=x, norm=W["xa_norm"][0:1])
    x2, n2, s_xa0 = xattn_fwd(x1, n1, 0)
    x3, n3, s_f0 = ffn_fwd(x2, n2, 0, W["c_norm"])
    tok = arrive(("c", 0), n3)
    h1p = _mm_nn(n3, W["c_w_pw1"], out_dtype=BF16, name="c_pw1", bias=gain(W["c_b_pw1"], tok))
    cv = _c_fwd(h1p, W["c_dw_w"], W["c_dw_b"])
    sc = _ln_silu_fwd(cv, W["c_ln_g"], W["c_ln_b"])
    x4, n4 = _mm_nn(sc, W["c_w_pw2"], out_dtype=F32, name="c_pw2", bias=W["c_b_pw2"], add=x3, norm=W["xa_norm"][1:2])
    x5, n5, s_xa1 = xattn_fwd(x4, n4, 1)
    x6, _, s_f1 = ffn_fwd(x5, n5, 1, None)
    loss, dx, dxb, G["final_norm"] = _loss_head(x6, W["final_norm"], tgt)

    dx, dxb = ffn_bwd(dx, dxb, s_f1, 1)
    dx, dxb = xattn_bwd(dx, dxb, s_xa1, 1)
    dsc = _mm_nt(dxb, W["c_w_pw2"], out_dtype=BF16, name="c_dsc")
    G["c_w_pw2"] = _mm_tn(sc, dxb, out_dtype=BF16, name="c_dwpw2")
    dcv, G["c_ln_g"], G["c_ln_b"], G["c_b_pw2"] = _ln_silu_bwd(dsc, cv, W["c_ln_g"], W["c_ln_b"], dx)
    dh1p, G["c_dw_w"], G["c_dw_b"], G["c_b_pw1"] = _c_bwd(dcv, h1p, W["c_dw_w"])
    G["c_w_pw1"] = _mm_tn(n3, dh1p, out_dtype=BF16, name="c_dwpw1", blocks=_CW_C)
    tok = sent(("c", 0))
    dx, dxb, G["c_norm"] = _mm_nt(dh1p, W["c_w_pw1"], out_dtype=F32, name="c_dn",
                                  rms=(x3, gain(W["c_norm"], tok), dx))
    dx, dxb = ffn_bwd(dx, dxb, s_f0, 0)
    dx, dxb = xattn_bwd(dx, dxb, s_xa0, 0)
    dyab = _mm_nt(dxb, W["ab_w_out"], out_dtype=BF16, name="ab_dyab")
    G["ab_w_out"] = _mm_tn(yab, dxb, out_dtype=BF16, name="ab_dwout")
    tok = sent(("ab", 1))
    a_par = (a_par[0], gain(a_par[1], tok)) + a_par[2:]
    (dzg, dzr, G["a_conv_w"], G["a_conv_b"], G["a_gate_x_w"], G["a_gate_x_b"], G["a_gate_a_w"], G["a_gate_a_b"],
     G["a_lambda"]) = _a_bwd(dyab, zp, h_a, *a_par)
    dzq, G["b_group_w"], G["b_group_b"], G["b_scale"] = _b_bwd(dyab, zp, *b_par)
    G["ab_w_in"] = jnp.concatenate(
        [_mm_tn(n0, dz, out_dtype=BF16, name=f"ab_dwin_{part}")
         for part, dz in (("gate", dzg), ("rec", dzr), ("pool", dzq))], axis=1)
    tok = sent(("ab", 0))
    dx, _, G["ab_norm"] = _mm_nt_cols([dzg, dzr, dzq], W["ab_w_in"], name="ab_dn",
                                      rms=(x, gain(W["ab_norm"], tok), dx))
    return loss, dx, G


def _my_place():
    x, y, c = lax.axis_index("x"), lax.axis_index("y"), lax.axis_index("c")
    return x, y, c


def _all_gather(shards, name):
    n = len(shards)

    def body(*refs):
        ins, outs = refs[:n], refs[n:2 * n]
        send_sems, recv_sems, local_sems = refs[2 * n:]
        x, y, c = _my_place()
        me, sibling = (x, y, c), (x, y, 1 - c)
        chips = [(1 - x, y), (x, 1 - y), (1 - x, 1 - y)]

        def slab(a, place):
            px, py, pc = place
            return outs[a].at[4 * px + 2 * py + pc]

        def copy(a, k, block, to, src=None):
            return pltpu.make_async_remote_copy(
                src_ref=slab(a, block) if src is None else src, dst_ref=slab(a, block),
                send_sem=send_sems.at[a, k], recv_sem=recv_sems.at[a, k], device_id=to, device_id_type=MESH)

        mine = [pltpu.make_async_copy(ins[a], slab(a, me), local_sems.at[a]) for a in range(n)]
        for cp in mine:
            cp.start()
        first = []
        for j, chip in enumerate(chips):
            first += [copy(a, 1 + j, me, (*chip, c), src=ins[a]) for a in range(n)]
        first += [copy(a, 0, me, sibling, src=ins[a]) for a in range(n)]
        for cp in first:
            cp.start()
        passed = []
        for j, chip in enumerate(chips):
            for a in range(n):
                copy(a, 1 + j, (*chip, c), me).wait_recv()
                cp = copy(a, 4 + j, (*chip, c), sibling)
                cp.start()
                passed.append(cp)
        for a in range(n):
            copy(a, 0, sibling, me).wait_recv()
        for j, chip in enumerate(chips):
            for a in range(n):
                copy(a, 4 + j, (*chip, 1 - c), me).wait_recv()
        for cp in first + passed:
            cp.wait_send()
        for cp in mine:
            cp.wait()

    any_spec = pl.BlockSpec(memory_space=pl.ANY)
    return pl.pallas_call(
        body, out_shape=[SDS((N_DEV,) + s.shape, s.dtype) for s in shards], in_specs=[any_spec] * n,
        out_specs=[any_spec] * n,
        scratch_shapes=[pltpu.SemaphoreType.DMA((n, 7)), pltpu.SemaphoreType.DMA((n, 7)), pltpu.SemaphoreType.DMA((n,))],
        name=name,
    )(*shards)


_HBM = pl.BlockSpec(memory_space=pltpu.HBM)
_SEM = pl.BlockSpec(memory_space=pltpu.SEMAPHORE)
_EFFECT = pltpu.SideEffectType.DATAFLOW_SIDE_EFFECTING


def _peer_places():
    x, y, c = _my_place()
    peers = []
    for k in range(1, N_DEV):
        px = 1 - x if (k >> 2) & 1 else x
        py = 1 - y if (k >> 1) & 1 else y
        pc = 1 - c if k & 1 else c
        peers.append(((px, py, pc), 4 * px + 2 * py + pc))
    return (x, y, c), 4 * x + 2 * y + c, peers


def _send_start(srcs, per_dest, name):
    n = len(srcs)
    lands = [lax.empty((N_DEV,) + (s.shape[1:] if per_dest else s.shape), s.dtype) for s in srcs]

    def body(*refs):
        src, land = refs[:n], refs[n:2 * n]
        outs = refs[2 * n:]
        send, recv, token = outs[:n], outs[n:2 * n], outs[4 * n]
        _, me, peers = _peer_places()
        for a in range(n):
            for peer, pidx in peers:
                pltpu.make_async_remote_copy(
                    src_ref=src[a].at[pidx] if per_dest else src[a], dst_ref=land[a].at[me], send_sem=send[a],
                    recv_sem=recv[a], device_id=peer, device_id_type=MESH).start()
        token[...] = jnp.zeros_like(token)

    hbm = lambda a: pltpu.HBM(a.shape, a.dtype)
    sem = pltpu.SemaphoreType.DMA(())
    res = pl.pallas_call(
        body, name=name,
        out_shape=tuple([sem] * (2 * n) + [hbm(s) for s in srcs] + [hbm(l) for l in lands]
                        + [SDS((SUB, LANE), F32)]),
        in_specs=[_HBM] * (2 * n),
        out_specs=tuple([_SEM] * (2 * n) + [_HBM] * (2 * n) + [pl.BlockSpec(memory_space=pltpu.VMEM)]),
        input_output_aliases={i: 2 * n + i for i in range(2 * n)},
        compiler_params=pltpu.CompilerParams(has_side_effects=_EFFECT),
    )(*[pltpu.with_memory_space_constraint(s, pltpu.HBM) for s in srcs],
      *[pltpu.with_memory_space_constraint(l, pltpu.HBM) for l in lands])
    return res[:n], res[n:2 * n], res[2 * n:3 * n], res[3 * n:4 * n], res[4 * n]


def _send_wait(send, recv, srcs, lands, after, per_dest, name):
    n = len(srcs)

    def body(*refs):
        src, land = refs[:n], refs[n:2 * n]
        send_s, recv_s = refs[2 * n:3 * n], refs[3 * n:4 * n]
        token = refs[-1]
        place, _, _ = _peer_places()
        for a in range(n):
            seven = land[a].at[pl.ds(0, N_DEV - 1)]
            copy = pltpu.make_async_remote_copy(
                src_ref=src[a].at[pl.ds(0, N_DEV - 1)] if per_dest else seven, dst_ref=seven, send_sem=send_s[a],
                recv_sem=recv_s[a], device_id=place, device_id_type=MESH)
            copy.wait_send()
            copy.wait_recv()
        token[...] = jnp.zeros_like(token)

    hbm = lambda a: pltpu.HBM(a.shape, a.dtype)
    res = pl.pallas_call(
        body, name=name,
        out_shape=tuple([hbm(s) for s in srcs] + [hbm(l) for l in lands] + [SDS((SUB, LANE), F32)]),
        in_specs=[_HBM] * (2 * n) + [_SEM] * (2 * n) + [pl.BlockSpec(memory_space=pl.ANY)],
        out_specs=tuple([_HBM] * (2 * n) + [pl.BlockSpec(memory_space=pltpu.VMEM)]),
        input_output_aliases={i: i for i in range(2 * n)},
        compiler_params=pltpu.CompilerParams(has_side_effects=_EFFECT),
    )(*srcs, *lands, *send, *recv, after)
    return res[:n], res[n:2 * n], res[2 * n]


def _adamw_math(w, g, m, v):
    m = ADAM_B1 * m + (1.0 - ADAM_B1) * g
    v = ADAM_B2 * v + (1.0 - ADAM_B2) * (g * g)
    m_hat = m / (1.0 - ADAM_B1 ** ADAM_STEP)
    v_hat = v / (1.0 - ADAM_B2 ** ADAM_STEP)
    delta = -ADAM_LR * (m_hat / (jnp.sqrt(v_hat) + ADAM_EPS) + ADAM_WD * w)
    return delta, m, v


def _row_tile(r, c, itemsize_rows):
    cap = max(SUB, (itemsize_rows // (4 * c)) // SUB * SUB)
    if r <= cap:
        return r
    best = None
    for t in range(SUB, cap + 1, SUB):
        if r % t == 0:
            best = t
    return best if best is not None else r


def _sum_adamw(landing, w, m, v, name, layer=0, prev=None, after=None):
    _, r, c = landing.shape
    tr = _row_tile(r, c, 2 << 20)
    off = layer * (r // tr)
    tail = ([] if prev is None else list(prev)) + ([] if after is None else [after])

    def body(l_ref, w_ref, m_ref, v_ref, *rest):
        g_ref, d_ref, mo_ref, vo_ref = rest[-4:]
        g = l_ref[0].astype(F32)
        for s in range(1, N_DEV):
            g = g + l_ref[s].astype(F32)
        g_ref[...] = g
        d_ref[...], mo_ref[...], vo_ref[...] = _adamw_math(w_ref[...], g, m_ref[...], v_ref[...])

    blk = pl.BlockSpec((tr, c), lambda i: (i + off, 0))
    n_prev = 0 if prev is None else 4
    return pl.pallas_call(
        body, out_shape=[SDS(w.shape, F32)] * 4, grid=(r // tr,),
        in_specs=[pl.BlockSpec((N_DEV, tr, c), lambda i: (0, i, 0)), blk, blk, blk]
        + [pl.BlockSpec(memory_space=pl.ANY)] * len(tail),
        out_specs=[blk] * 4, input_output_aliases={4 + i: i for i in range(n_prev)}, name=name,
        compiler_params=_cp(1),
    )(landing, w, m, v, *tail)


def _sum8(landing, name):
    _, r, c = landing.shape

    def body(l_ref, g_ref):
        g = l_ref[0]
        for s in range(1, N_DEV):
            g = g + l_ref[s]
        g_ref[...] = g

    return pl.pallas_call(body, out_shape=SDS((r, c), F32), name=name, compiler_params=_cp(0))(landing)


def _adamw_small(repl_pack, own_pack, P, M, V):
    table, off = [], 0
    for name, shape in _REPL.items():
        table.append((name, shape if len(shape) > 1 else (1,) + shape, 0, off // LANE))
        off += _size(shape)
    off = _REPL_ROWS * LANE
    for name, shape in _SMALL_SHARDED.items():
        table.append((name, shape, 1, off // LANE))
        off += _size(shape)
    n = len(table)

    def body(*refs):
        packs, ins, outs = refs[:2], refs[2:2 + 3 * n], refs[2 + 3 * n:]
        for p, (_, shape, which, r0) in enumerate(table):
            w_ref, m_ref, v_ref = ins[3 * p:3 * p + 3]
            g_ref, d_ref, mo_ref, vo_ref = outs[4 * p:4 * p + 4]
            pack, rows, q = packs[which], shape[-2], shape[-1] // LANE
            lead = [()]
            for dim in shape[:-2]:
                lead = [t + (i,) for t in lead for i in range(dim)]
            for li, idx in enumerate(lead):
                if q == 1:
                    dst = g_ref.at[idx] if idx else g_ref
                    dst[...] = pack[r0 + li * rows:r0 + (li + 1) * rows, :]
                    continue
                for i in range(rows):
                    for k in range(q):
                        row = r0 + (li * rows + i) * q + k
                        g_ref[idx + (slice(i, i + 1), slice(k * LANE, (k + 1) * LANE))] = pack[row:row + 1, :]
            d_ref[...], mo_ref[...], vo_ref[...] = _adamw_math(w_ref[...], g_ref[...], m_ref[...], v_ref[...])

    ins, out_shape = [], []
    for name, shape, _, _ in table:
        ins += [t[name].reshape(shape) for t in (P, M, V)]
        out_shape += [SDS(shape, F32)] * 4
    res = pl.pallas_call(body, out_shape=out_shape, name="adamw_small", compiler_params=_cp(0))(
        repl_pack, own_pack, *ins)
    dicts = ({}, {}, {}, {})
    for p, (name, shape, _, _) in enumerate(table):
        for d, arr in zip(dicts, res[4 * p:4 * p + 4]):
            d[name] = arr.reshape(P[name].shape)
    return dicts


_BIG = {
    "ab_w_in": (1, D, 320), "ab_w_out": (1, 192, D), "c_w_pw1": (1, D, 256), "c_w_pw2": (1, 128, D),
    "xa_wq": (2, 128, D), "xa_wk": (2, 128, D), "xa_wv": (2, 128, D), "xa_wo": (2, 128, D),
    "f_w_up": (2, D, 768), "f_w_down": (2, 384, D),
}
_SMALL_SHARDED = {
    "a_conv_w": (1, 4, 128), "c_norm": (1, 128), "c_b_pw1": (1, 256), "c_dw_w": (1, 31, 128), "c_dw_b": (1, 128),
    "c_ln_g": (1, 128), "c_ln_b": (1, 128), "c_b_pw2": (1, 128), "f_dw_w": (2, 3, 384),
}
_REPL = {
    "ab_norm": (1, D), "a_conv_b": (1, D), "a_gate_x_w": (1, 8, 128, 128), "a_gate_x_b": (1, D),
    "a_gate_a_w": (1, 8, 128, 128), "a_gate_a_b": (1, D), "a_lambda": (1, D), "b_group_w": (1, 4, 128, 128),
    "b_group_b": (1, 512), "b_scale": (1, 512), "xa_norm": (2, D), "xa_mem_norm": (2, D), "f_norm": (2, D),
    "f_dw_b": (2, D_FF), "final_norm": (D,),
}


def _size(shape):
    n = 1
    for s in shape:
        n *= s
    return n


_N_SS = sum(_size(s) for s in _SMALL_SHARDED.values())
_N_REPL = sum(_size(s) for s in _REPL.values())
_REPL_ROWS = -(-_N_REPL // (N_DEV * SUB * LANE)) * SUB
_SS_ROWS = _N_SS // LANE
_SMALL_ROWS = -(-(_REPL_ROWS + _SS_ROWS) // SUB) * SUB


def _pack(parts, rows):
    flat = jnp.concatenate([p.reshape(-1).astype(F32) for p in parts])
    return jnp.pad(flat, (0, rows * LANE - flat.shape[0])).reshape(rows, LANE)


def _pair_blocks(v, bw):
    lead, n = v.shape[:-1], v.shape[-1]
    return jnp.swapaxes(v.reshape(lead + (2, n // (2 * bw), bw)), -3, -2).reshape(lead + (n,))


def _unpair_blocks(v, bw):
    lead, n = v.shape[:-1], v.shape[-1]
    return jnp.swapaxes(v.reshape(lead + (n // (2 * bw), 2, bw)), -3, -2).reshape(lead + (n,))


_GROUPS = {
    ("ab", 0): (("ab_w_in", 0),),
    ("ab", 1): (("ab_w_out", 0),),
    ("xa", 0): (("xa_wq", 0), ("xa_wk", 0), ("xa_wv", 0), ("xa_wo", 0)),
    ("f", 0): (("f_w_up", 0),),
    ("fd", 0): (("f_w_down", 0),),
    ("c", 0): (("c_w_pw1", 0), ("c_w_pw2", 0)),
    ("xa", 1): (("xa_wq", 1), ("xa_wk", 1), ("xa_wv", 1), ("xa_wo", 1)),
    ("f", 1): (("f_w_up", 1),),
    ("fd", 1): (("f_w_down", 1),),
}
_SEND_GROUPS = {g: m for g, m in _GROUPS.items() if g[0] != "fd"}
_SEND_GROUPS[("f", 0)] = (("f_w_up", 0), ("f_w_down", 0))
_SEND_GROUPS[("f", 1)] = (("f_w_up", 1), ("f_w_down", 1))


def _weight_layout(name, g):
    if name == "ab_w_in":
        return jnp.swapaxes(g, 0, 1).reshape(D, N_DEV * 320)
    if name in ("c_w_pw1", "f_w_up"):
        return g
    return g.reshape(N_DEV * g.shape[1], D)


def _grad_blocks(name, l, G):
    _, r, c = _BIG[name]
    if name == "ab_w_in":
        return jnp.swapaxes(G[name].reshape(D, N_DEV, 320), 0, 1)
    if name == "c_w_pw1":
        return G[name]
    if name == "f_w_up":
        return G[f"{name}{l}"]
    return (G[name] if _BIG[name][0] == 1 else G[f"{name}{l}"]).reshape(N_DEV, r, c)


def _small_layouts(sm):
    W = {}
    sm = sm.reshape(N_DEV, -1)
    off = 0
    for name, shape in _SMALL_SHARDED.items():
        n = _size(shape)
        blocks = sm[:, off:off + n].reshape((N_DEV,) + shape)
        off += n
        W[name] = jnp.moveaxis(blocks, 0, -2).reshape(shape[:-1] + (N_DEV * shape[-1],))
    W["a_conv_w"], W["c_dw_w"] = W["a_conv_w"][0], W["c_dw_w"][0]
    W["c_b_pw1"] = _pair_blocks(W["c_b_pw1"], _CW_C)
    return W


def _with_own(land, src, me, per_dest):
    own = lax.dynamic_slice_in_dim(src, me, 1, 0) if per_dest else src[None]
    return lax.dynamic_update_slice_in_dim(land, own, me, 0)


def _to_dest_major(g, shape):
    full = g.reshape(shape[:-1] + (N_DEV, shape[-1]))
    return jnp.moveaxis(full, -2, 0).reshape(N_DEV, -1)


def kernel(x, mem, ab_norm, ab_w_in, a_conv_w, a_conv_b, a_gate_x_w, a_gate_x_b, a_gate_a_w, a_gate_a_b, a_lambda, b_group_w, b_group_b, b_scale, ab_w_out, c_norm, c_w_pw1, c_b_pw1, c_dw_w, c_dw_b, c_ln_g, c_ln_b, c_w_pw2, c_b_pw2, xa_norm, xa_mem_norm, xa_wq, xa_wk, xa_wv, xa_wo, f_norm, f_w_up, f_dw_w, f_dw_b, f_w_down, final_norm, loss_target, m_ab_norm, m_ab_w_in, m_a_conv_w, m_a_conv_b, m_a_gate_x_w, m_a_gate_x_b, m_a_gate_a_w, m_a_gate_a_b, m_a_lambda, m_b_group_w, m_b_group_b, m_b_scale, m_ab_w_out, m_c_norm, m_c_w_pw1, m_c_b_pw1, m_c_dw_w, m_c_dw_b, m_c_ln_g, m_c_ln_b, m_c_w_pw2, m_c_b_pw2, m_xa_norm, m_xa_mem_norm, m_xa_wq, m_xa_wk, m_xa_wv, m_xa_wo, m_f_norm, m_f_w_up, m_f_dw_w, m_f_dw_b, m_f_w_down, m_final_norm, v_ab_norm, v_ab_w_in, v_a_conv_w, v_a_conv_b, v_a_gate_x_w, v_a_gate_x_b, v_a_gate_a_w, v_a_gate_a_b, v_a_lambda, v_b_group_w, v_b_group_b, v_b_scale, v_ab_w_out, v_c_norm, v_c_w_pw1, v_c_b_pw1, v_c_dw_w, v_c_dw_b, v_c_ln_g, v_c_ln_b, v_c_w_pw2, v_c_b_pw2, v_xa_norm, v_xa_mem_norm, v_xa_wq, v_xa_wk, v_xa_wv, v_xa_wo, v_f_norm, v_f_w_up, v_f_dw_w, v_f_dw_b, v_f_w_down, v_final_norm):
    args = dict(locals())
    P = {n: args[n] for n in _NAMES}
    M = {n: args["m_" + n] for n in _NAMES}
    V = {n: args["v_" + n] for n in _NAMES}

    me = 4 * lax.axis_index("x") + 2 * lax.axis_index("y") + lax.axis_index("c")

    in_flight = {}

    def launch(groups, tok):
        shards, n_of = [], {}
        for grp in groups:
            for name, l in _GROUPS[grp]:
                w = P[name][l] if tok is None else P[name][l] + tok
                shards.append(w.astype(BF16))
            if grp == ("ab", 0):
                shards.append(_pack([P[n] for n in _SMALL_SHARDED], _SS_ROWS + 4))
            n_of[grp] = len(shards)
        res = _send_start(shards, False, "gather_start_" + "_".join(g[0] + str(g[1]) for g in groups))
        lo = 0
        for grp in groups:
            in_flight[grp] = [r[lo:n_of[grp]] for r in res[:4]]
            lo = n_of[grp]
        return res[4][:1, :1]

    follow = {("ab", 0): [("ab", 1), ("xa", 0), ("f", 0), ("fd", 0)], ("xa", 0): [("c", 0), ("xa", 1)],
              ("f", 0): [("f", 1), ("fd", 1)]}

    def fetch(grp, after):
        send_s, recv_s, srcs, lands = in_flight.pop(grp)
        srcs, lands, tok = _send_wait(send_s, recv_s, srcs, lands, after, False, f"gather_wait_{grp[0]}{grp[1]}")
        tok = launch(follow[grp], tok[:1, :1]) if grp in follow else None
        full = [_with_own(land, src, me, False) for land, src in zip(lands, srcs)]
        out = {}
        for (name, l), g in zip(_GROUPS[grp], full):
            w = _weight_layout(name, g)
            if _BIG[name][0] == 1:
                out[name] = w
            else:
                out[name] = {l: w}
        if grp == ("ab", 0):
            out.update(_small_layouts(full[-1]))
        return out, tok

    zero = launch([("ab", 0)], None)

    pending, held = [], []
    rides_with_next = {("xa", 1), ("f", 0)}

    def send(grp, G):
        held.extend(_SEND_GROUPS[grp])
        if grp in rides_with_next:
            return None
        members = tuple(held)
        del held[:]
        res = _send_start([_grad_blocks(name, l, G) for name, l in members], True, f"send_{grp[0]}{grp[1]}")
        pending.append((members, res))
        return res[4][:1, :1]

    W = {n: P[n] for n in _REPL}
    W["ab_norm"] = P["ab_norm"] + zero
    W["final_norm"] = P["final_norm"].reshape(1, D)
    W["a_gate_x_w"], W["a_gate_a_w"], W["b_group_w"] = P["a_gate_x_w"][0], P["a_gate_a_w"][0], P["b_group_w"][0]
    loss, grad_x, G = _local_step(x[0], mem[0], loss_target[0], W, fetch, send)
    loss = lax.psum(loss[0, 0], ("x", "y", "c"))

    Gs = dict(G)
    Gs["c_b_pw1"] = _unpair_blocks(G["c_b_pw1"], _CW_C)
    Gs["f_dw_w"] = jnp.stack([G["f_dw_w0"], G["f_dw_w1"]])
    Gs["a_conv_w"], Gs["c_dw_w"] = G["a_conv_w"][None], G["c_dw_w"][None]
    for n in ("xa_norm", "xa_mem_norm", "f_norm", "f_dw_b"):
        Gs[n] = jnp.concatenate([G[f"{n}0"], G[f"{n}1"]], axis=0)
    for n in ("a_gate_x_w", "a_gate_a_w", "b_group_w"):
        Gs[n] = G[n][None]
    repl_flat = jnp.concatenate([Gs[n].reshape(-1) for n in _REPL])
    repl_rows = jnp.pad(repl_flat, (0, N_DEV * _REPL_ROWS * LANE - _N_REPL)).reshape(N_DEV, _REPL_ROWS, LANE)
    ss_rows = jnp.concatenate([_to_dest_major(Gs[n], s) for n, s in _SMALL_SHARDED.items()], axis=1)
    ss_rows = ss_rows.reshape(N_DEV, _SS_ROWS, LANE)
    small_pack = jnp.concatenate(
        [repl_rows, ss_rows, jnp.zeros((N_DEV, _SMALL_ROWS - _REPL_ROWS - _SS_ROWS, LANE), F32)], axis=1)
    last = _send_start([small_pack], True, "send_small")
    pending.append(((("small", 0),), last))

    def arrived(some, after, name):
        members = [m for mem_, _ in some for m in mem_]
        cat = [[a for _, res in some for a in res[i]] for i in range(4)]
        srcs, lands, _ = _send_wait(cat[0], cat[1], cat[2], cat[3], after, True, name)
        return {m: _with_own(land, src, me, True) for m, land, src in zip(members, lands, srcs)}

    out_g, out_d, out_m, out_v = {}, {}, {}, {}
    chain = [None]

    def update(name, landed):
        layers, r, c = _BIG[name]
        w2, m2, v2 = [t[name].reshape(layers * r, c) for t in (P, M, V)]
        res = None
        for l in range(layers):
            res = _sum_adamw(landed[(name, l)], w2, m2, v2, f"adamw_{name}{l}", layer=l, prev=res,
                             after=chain[0] if l == 0 else None)
        chain[0] = res[1]
        out_g[name], out_d[name], out_m[name], out_v[name] = [t.reshape(P[name].shape) for t in res]

    landed = arrived(pending[:-2], grad_x, "send_wait_early")
    for name in _BIG:
        if name != "ab_w_in":
            update(name, landed)
    landed = arrived(pending[-2:], out_v["f_w_down"], "send_wait_late")
    update("ab_w_in", landed)

    small_sum = _sum8(landed[("small", 0)], "sum_small")
    (repl_all,) = _all_gather([small_sum[:_REPL_ROWS]], "gather_small_grads")
    for out, got in zip((out_g, out_d, out_m, out_v),
                        _adamw_small(repl_all.reshape(N_DEV * _REPL_ROWS, LANE), small_sum, P, M, V)):
        out.update(got)

    return (loss, grad_x[None], *[out_g[n] for n in _NAMES], *[out_d[n] for n in _NAMES],
            *[out_m[n] for n in _NAMES], *[out_v[n] for n in _NAMES])


_NAMES = ("ab_norm", "ab_w_in", "a_conv_w", "a_conv_b", "a_gate_x_w", "a_gate_x_b", "a_gate_a_w", "a_gate_a_b",
          "a_lambda", "b_group_w", "b_group_b", "b_scale", "ab_w_out", "c_norm", "c_w_pw1", "c_b_pw1", "c_dw_w",
          "c_dw_b", "c_ln_g", "c_ln_b", "c_w_pw2", "c_b_pw2", "xa_norm", "xa_mem_norm", "xa_wq", "xa_wk", "xa_wv",
          "xa_wo", "f_norm", "f_w_up", "f_dw_w", "f_dw_b", "f_w_down", "final_norm")
```

```python
import functools

import jax
import jax.numpy as jnp
from jax import lax
from jax.experimental import pallas as pl
from jax.experimental.pallas import tpu as pltpu

F32, BF16 = jnp.float32, jnp.bfloat16
SDS = jax.ShapeDtypeStruct
MESH = pl.DeviceIdType.MESH

N_DEV = 8
D = 1024
N_MEM = 256
XA_HEADS, XA_HD = 4, 256
HD_A = 128
CONV_A, CONV_C, CONV_F = 4, 31, 3
C_RG = 8.0
POOL_WINDOWS = (2, 4, 8, 16)
D_FF = 3 * D
EPS = 1e-6
ADAM_LR, ADAM_B1, ADAM_B2, ADAM_EPS, ADAM_WD, ADAM_STEP = 0.001, 0.9, 0.999, 1e-08, 0.01, 10

LANE = 128
SUB = 8
VMEM_LIMIT = 56 * 1024 * 1024
R_SEQ = 512
R_RGLRU = 2048
R_FFN = 1024
TM_ROW = 512


def _cp(n_axes):
    return pltpu.CompilerParams(dimension_semantics=("arbitrary",) * n_axes, vmem_limit_bytes=VMEM_LIMIT)


def _tile(n, pref):
    if n <= pref:
        return n
    best = None
    for t in range(LANE, pref + 1, LANE):
        if n % t == 0:
            best = t
    assert best is not None, (n, pref)
    return best


def _perm2(n):
    return (n % 2) * 4 + n // 2


_NN = (((1,), (0,)), ((), ()))
_NT = (((1,), (1,)), ((), ()))
_TN = (((0,), (0,)), ((), ()))


def _mm_call(name, grid, ab, ab_specs, dims, acc_shape, extras, outs, finish, from_ref=False):
    nk = grid[2]
    n_ab, n_ex, n_out = len(ab), len(extras), len(outs)
    use_acc = nk > 1 or from_ref

    def product(refs):
        r = lax.dot_general(refs[0][...], refs[1][...], dims, preferred_element_type=F32)
        for i in range(1, n_ab):
            r = r + lax.dot_general(refs[2 * i][...], refs[2 * i + 1][...], dims, preferred_element_type=F32)
        return r

    def body(*refs):
        rest = refs[2 * n_ab:]
        ex_refs, o_refs = rest[:n_ex], rest[n_ex:n_ex + n_out]
        first_rows = pl.program_id(0) == 0
        if not use_acc:
            finish(product(refs), ex_refs, o_refs, first_rows)
            return
        acc = rest[n_ex + n_out]
        if nk == 1:
            acc[...] = product(refs)
            finish(acc, ex_refs, o_refs, first_rows)
            return
        k = pl.program_id(2)

        @pl.when(k == 0)
        def _():
            acc[...] = jnp.zeros_like(acc)

        acc[...] += product(refs)

        @pl.when(k == nk - 1)
        def _():
            finish(acc if from_ref else acc[...], ex_refs, o_refs, first_rows)

    res = pl.pallas_call(
        body, out_shape=[o for o, _ in outs], grid=grid,
        in_specs=list(ab_specs) + [s for _, s in extras], out_specs=[s for _, s in outs],
        scratch_shapes=[pltpu.VMEM(acc_shape, F32)] if use_acc else [], name=name, compiler_params=_cp(3),
    )(*[t for pair in ab for t in pair], *[e for e, _ in extras])
    return res[0] if n_out == 1 else res


def _finish_sum(r, ex_refs, o_refs, first_rows):
    del first_rows
    for e in ex_refs:
        r = r + e[...]
    o_refs[0][...] = r.astype(o_refs[0].dtype)


def _finish_sum_norm(r, ex_refs, o_refs, first_rows):
    del first_rows
    for e in ex_refs[:-1]:
        r = r + e[...]
    o_refs[0][...] = r
    o_refs[1][...] = ((r * lax.rsqrt(jnp.mean(r * r, axis=-1, keepdims=True) + EPS)) * ex_refs[-1][...]).astype(BF16)


_EPI_ROWS = 16


def _finish_rms_bwd(r_ref, ex_refs, o_refs, first_rows):
    x_ref, g_ref, dres_ref = ex_refs
    dx_ref, dxb_ref, dg_ref = o_refs

    @pl.when(first_rows)
    def _():
        dg_ref[...] = jnp.zeros_like(dg_ref)

    gv = g_ref[...]
    inv_d = 1.0 / r_ref.shape[1]

    def step(i, dg_acc):
        groups = [pl.ds(pl.multiple_of(i * (2 * _EPI_ROWS) + u * _EPI_ROWS, _EPI_ROWS), _EPI_ROWS) for u in range(2)]
        sums = []
        for rows in groups:
            r, xf = r_ref[rows, :], x_ref[rows, :]
            sums.append((jnp.sum(xf * xf, axis=-1, keepdims=True), jnp.sum((r * gv) * xf, axis=-1, keepdims=True)))
        for rows, (sxx, sax) in zip(groups, sums):
            r, xf = r_ref[rows, :], x_ref[rows, :]
            rs = lax.rsqrt(sxx * inv_d + EPS)
            dg_acc = dg_acc + _psum8(r * (xf * rs))
            dx = rs * (r * gv) - xf * (rs * rs * (sax * rs * inv_d)) + dres_ref[rows, :]
            dx_ref[rows, :] = dx
            dxb_ref[rows, :] = dx.astype(BF16)
        return dg_acc

    dg_acc = lax.fori_loop(0, r_ref.shape[0] // (2 * _EPI_ROWS), step, jnp.zeros((SUB, r_ref.shape[1]), F32))
    dg_ref[...] += jnp.sum(dg_acc, axis=0, keepdims=True)


def _rms_bwd_io(M, tm, x, g, dres):
    rows = pl.BlockSpec((tm, D), lambda m, n, k: (m, 0))
    vec = pl.BlockSpec((1, D), lambda m, n, k: (0, 0))
    return ([(x, rows), (g, vec), (dres, rows)],
            [(SDS((M, D), F32), rows), (SDS((M, D), BF16), rows), (SDS((1, D), F32), vec)])


_K_WHOLE = 3072


def _mm_nn(a, b, *, out_dtype, name, bias=None, add=None, norm=None):
    M, K = a.shape
    tk = K if K <= _K_WHOLE else _tile(K, 1024)
    if K <= 1024 and norm is None:
        tm = _tile(M, 2048 if add is None and out_dtype == BF16 else 1024)
    else:
        tm = _tile(M, 512)
    if b.ndim == 3:
        nb, _, bw = b.shape
        N, tn, nn = nb * bw, bw, nb
        b_spec = pl.BlockSpec((None, tk, bw), lambda m, n, k: (_perm2(n), k, 0))
    else:
        N = b.shape[1]
        tn = _tile(N, 1024)
        nn = N // tn
        b_spec = pl.BlockSpec((tk, tn), lambda m, n, k: (k, n))
    tile = pl.BlockSpec((tm, tn), lambda m, n, k: (m, n))
    vec = pl.BlockSpec((1, tn), lambda m, n, k: (0, n))
    extras = ([] if bias is None else [(bias, vec)]) + ([] if add is None else [(add, tile)])
    outs, finish = [(SDS((M, N), out_dtype), tile)], _finish_sum
    if norm is not None:
        assert tn == N == D and out_dtype == F32
        extras.append((norm, vec))
        outs, finish = outs + [(SDS((M, N), BF16), tile)], _finish_sum_norm
    return _mm_call(name, (M // tm, nn, K // tk), [(a, b)], [pl.BlockSpec((tm, tk), lambda m, n, k: (m, k)), b_spec],
                    _NN, (tm, tn), extras, outs, finish)


def _mm_nt(a, b, *, out_dtype, name, add=None, rms=None):
    M, N = a.shape
    if b.ndim == 3:
        nb, Ko, bw = b.shape
        tm = _tile(M, 1024)
        tn, tk, nk = _tile(Ko, 1024), bw, nb
        b_spec = pl.BlockSpec((None, tn, bw), lambda m, n, k: (_perm2(k), n, 0))
    else:
        Ko = b.shape[0]
        tk = N if N <= _K_WHOLE else _tile(N, 1024)
        if N <= 1024 and rms is None:
            tm = _tile(M, 2048 if add is None and out_dtype == BF16 else 1024)
        else:
            tm = _tile(M, 512)
        tn = _tile(Ko, 1024)
        nk = N // tk
        b_spec = pl.BlockSpec((tn, tk), lambda m, n, k: (n, k))
    tile = pl.BlockSpec((tm, tn), lambda m, n, k: (m, n))
    extras = [] if add is None else [(add, tile)]
    outs, finish = [(SDS((M, Ko), out_dtype), tile)], _finish_sum
    if rms is not None:
        assert tn == Ko == D and add is None
        (extras, outs), finish = _rms_bwd_io(M, tm, *rms), _finish_rms_bwd
    return _mm_call(name, (M // tm, Ko // tn, nk), [(a, b)], [pl.BlockSpec((tm, tk), lambda m, n, k: (m, k)), b_spec],
                    _NT, (tm, tn), extras, outs, finish, from_ref=rms is not None)


def _mm_nt_cols(parts, b, *, name, rms):
    M = parts[0].shape[0]
    tm = _tile(M, 512)
    specs, off = [], 0
    for p in parts:
        w = p.shape[1]
        assert off % w == 0
        specs.append(pl.BlockSpec((tm, w), lambda m, n, k: (m, 0)))
        specs.append(pl.BlockSpec((D, w), functools.partial(lambda m, n, k, o: (0, o), o=off // w)))
        off += w
    extras, outs = _rms_bwd_io(M, tm, *rms)
    return _mm_call(name, (M // tm, 1, 1), [(p, b) for p in parts], specs, _NT, (tm, D), extras, outs, _finish_rms_bwd,
                    from_ref=True)


def _mm_tn(a, b, *, out_dtype, name, blocks=None):
    S, Ka = a.shape
    Nb = b.shape[1]
    tm = _tile(Ka, 1024)
    if blocks is not None:
        bw = blocks
        tn, nn = bw, Nb // bw
        out = (SDS((nn, Ka, bw), out_dtype), pl.BlockSpec((None, tm, bw), lambda m, n, k: (_perm2(n), m, 0)))
    else:
        tn = _tile(Nb, 1024)
        nn = Nb // tn
        out = (SDS((Ka, Nb), out_dtype), pl.BlockSpec((tm, tn), lambda m, n, k: (m, n)))
    steps = (Ka // tm) * nn
    tk = _tile(S, 4096 if steps >= 4 else 2048 if steps >= 2 else 1024)
    return _mm_call(name, (Ka // tm, nn, S // tk), [(a, b)],
                    [pl.BlockSpec((tk, tm), lambda m, n, k: (k, m)), pl.BlockSpec((tk, tn), lambda m, n, k: (k, n))],
                    _TN, (tm, tn), [], [out], _finish_sum)


def _row(tm, c):
    return pl.BlockSpec((tm, c), lambda i: (i, 0))


def _full(shape):
    nd = len(shape)
    return pl.BlockSpec(shape, lambda i: (0,) * nd)


def _rms_fwd(x, g, name):
    S = x.shape[0]
    tm = min(S, TM_ROW)

    def body(x_ref, g_ref, o_ref):
        xf = x_ref[...]
        r = lax.rsqrt(jnp.mean(xf * xf, axis=-1, keepdims=True) + EPS)
        o_ref[...] = ((xf * r) * g_ref[...]).astype(BF16)

    return pl.pallas_call(body, out_shape=SDS((S, D), BF16), grid=(S // tm,), in_specs=[_row(tm, D), _full((1, D))],
                          out_specs=_row(tm, D), name=name, compiler_params=_cp(1))(x, g)


def _rms_bwd(x, g, dn, dres, name):
    S = x.shape[0]
    tm = min(S, TM_ROW)
    want_dx = dres is not None

    def body(x_ref, g_ref, dn_ref, *rest):
        i = pl.program_id(0)
        dg_ref = rest[-1]

        @pl.when(i == 0)
        def _():
            dg_ref[...] = jnp.zeros_like(dg_ref)

        xf = x_ref[...]
        r = lax.rsqrt(jnp.mean(xf * xf, axis=-1, keepdims=True) + EPS)
        y = xf * r
        dn_v = dn_ref[...]
        dg_ref[...] += jnp.sum(dn_v * y, axis=0, keepdims=True)
        if want_dx:
            dres_ref, dx_ref, dxb_ref = rest[0], rest[1], rest[2]
            dy = dn_v * g_ref[...]
            dx = r * (dy - y * jnp.mean(dy * y, axis=-1, keepdims=True)) + dres_ref[...]
            dx_ref[...] = dx
            dxb_ref[...] = dx.astype(BF16)

    ins = [x, g, dn] + ([dres] if want_dx else [])
    in_specs = [_row(tm, D), _full((1, D)), _row(tm, D)] + ([_row(tm, D)] if want_dx else [])
    outs = ([SDS((S, D), F32), SDS((S, D), BF16)] if want_dx else []) + [SDS((1, D), F32)]
    out_specs = ([_row(tm, D), _row(tm, D)] if want_dx else []) + [_full((1, D))]
    return pl.pallas_call(body, out_shape=outs, grid=(S // tm,), in_specs=in_specs, out_specs=out_specs, name=name,
                          compiler_params=_cp(1))(*ins)


def _loss_head(x, g, tgt):
    S = x.shape[0]
    tm = min(S, TM_ROW)

    def body(x_ref, g_ref, t_ref, loss_ref, dx_ref, dxb_ref, dg_ref):
        i = pl.program_id(0)

        @pl.when(i == 0)
        def _():
            loss_ref[...] = jnp.zeros_like(loss_ref)
            dg_ref[...] = jnp.zeros_like(dg_ref)

        xf = x_ref[...]
        r = lax.rsqrt(jnp.mean(xf * xf, axis=-1, keepdims=True) + EPS)
        y = xf * r
        gv = g_ref[...]
        err = y * gv - t_ref[...]
        per_row = jnp.mean(err * err, axis=-1, keepdims=True)
        loss_ref[...] += 0.5 * jnp.sum(per_row, axis=0, keepdims=True)
        dn_v = err * (1.0 / D)
        dg_ref[...] += jnp.sum(dn_v * y, axis=0, keepdims=True)
        dy = dn_v * gv
        dx = r * (dy - y * jnp.mean(dy * y, axis=-1, keepdims=True))
        dx_ref[...] = dx
        dxb_ref[...] = dx.astype(BF16)

    return pl.pallas_call(
        body, out_shape=[SDS((1, 1), F32), SDS((S, D), F32), SDS((S, D), BF16), SDS((1, D), F32)], grid=(S // tm,),
        in_specs=[_row(tm, D), _full((1, D)), _row(tm, D)],
        out_specs=[_full((1, 1)), _row(tm, D), _row(tm, D), _full((1, D))], name="loss_head", compiler_params=_cp(1),
    )(x, g, tgt)


def _softmax_rows(s):
    m = jnp.max(s, axis=-1, keepdims=True)
    e = jnp.exp(s - m)
    return e / jnp.sum(e, axis=-1, keepdims=True)


def _attn_fwd(q, k, v, name):
    S = q.shape[0]
    tm = min(S, TM_ROW)
    scale = XA_HD ** -0.5

    def body(q_ref, k_ref, v_ref, o_ref):
        for h in range(XA_HEADS):
            sl = slice(h * XA_HD, (h + 1) * XA_HD)
            s = lax.dot_general(q_ref[:, sl], k_ref[:, sl], _NT, preferred_element_type=F32) * scale
            p = _softmax_rows(s)
            o_ref[:, sl] = lax.dot_general(p.astype(BF16), v_ref[:, sl], _NN, preferred_element_type=F32).astype(BF16)

    return pl.pallas_call(body, out_shape=SDS((S, D), BF16), grid=(S // tm,),
                          in_specs=[_row(tm, D), _full((N_MEM, D)), _full((N_MEM, D))], out_specs=_row(tm, D),
                          name=name, compiler_params=_cp(1))(q, k, v)


def _attn_bwd(q, k, v, do, name):
    S = q.shape[0]
    tm = min(S, TM_ROW)
    scale = XA_HD ** -0.5

    def body(q_ref, k_ref, v_ref, do_ref, dq_ref, dk_ref, dv_ref):
        i = pl.program_id(0)

        @pl.when(i == 0)
        def _():
            dk_ref[...] = jnp.zeros_like(dk_ref)
            dv_ref[...] = jnp.zeros_like(dv_ref)

        for h in range(XA_HEADS):
            sl = slice(h * XA_HD, (h + 1) * XA_HD)
            qh, kh, vh, doh = q_ref[:, sl], k_ref[:, sl], v_ref[:, sl], do_ref[:, sl]
            s = lax.dot_general(qh, kh, _NT, preferred_element_type=F32) * scale
            p = _softmax_rows(s)
            pb = p.astype(BF16)
            dv_ref[:, sl] += lax.dot_general(pb, doh, _TN, preferred_element_type=F32)
            dp = lax.dot_general(doh, vh, _NT, preferred_element_type=F32)
            ds = (p * (dp - jnp.sum(dp * p, axis=-1, keepdims=True)) * scale).astype(BF16)
            dq_ref[:, sl] = lax.dot_general(ds, kh, _NN, preferred_element_type=F32).astype(BF16)
            dk_ref[:, sl] += lax.dot_general(ds, qh, _TN, preferred_element_type=F32)

    return pl.pallas_call(
        body, out_shape=[SDS((S, D), BF16), SDS((N_MEM, D), F32), SDS((N_MEM, D), F32)], grid=(S // tm,),
        in_specs=[_row(tm, D), _full((N_MEM, D)), _full((N_MEM, D)), _row(tm, D)],
        out_specs=[_row(tm, D), _full((N_MEM, D)), _full((N_MEM, D))], name=name, compiler_params=_cp(1),
    )(q, k, v, do)


def _sigmoid(x):
    return 1.0 / (1.0 + jnp.exp(-x))


def _ln_silu_fwd(cv, g, b):
    S = cv.shape[0]
    tm = min(S, TM_ROW)

    def body(x_ref, g_ref, b_ref, o_ref):
        xf = x_ref[...]
        mu = jnp.mean(xf, axis=-1, keepdims=True)
        xc = xf - mu
        rstd = lax.rsqrt(jnp.mean(xc * xc, axis=-1, keepdims=True) + EPS)
        ln = (xc * rstd) * g_ref[...] + b_ref[...]
        o_ref[...] = (ln * _sigmoid(ln)).astype(BF16)

    return pl.pallas_call(body, out_shape=SDS((S, D), BF16), grid=(S // tm,),
                          in_specs=[_row(tm, D), _full((1, D)), _full((1, D))], out_specs=_row(tm, D),
                          name="ln_silu_fwd", compiler_params=_cp(1))(cv, g, b)


def _ln_silu_bwd(ds, cv, g, b, dx):
    S = cv.shape[0]
    tm = min(S, TM_ROW)

    def body(ds_ref, x_ref, g_ref, b_ref, dx_ref, dcv_ref, dg_ref, db_ref, db2_ref):
        i = pl.program_id(0)

        @pl.when(i == 0)
        def _():
            dg_ref[...] = jnp.zeros_like(dg_ref)
            db_ref[...] = jnp.zeros_like(db_ref)
            db2_ref[...] = jnp.zeros_like(db2_ref)

        xf = x_ref[...]
        mu = jnp.mean(xf, axis=-1, keepdims=True)
        xc = xf - mu
        rstd = lax.rsqrt(jnp.mean(xc * xc, axis=-1, keepdims=True) + EPS)
        xhat = xc * rstd
        gv = g_ref[...]
        ln = xhat * gv + b_ref[...]
        sg = _sigmoid(ln)
        dln = ds_ref[...].astype(F32) * (sg + ln * sg * (1.0 - sg))
        dg_ref[...] += jnp.sum(dln * xhat, axis=0, keepdims=True)
        db_ref[...] += jnp.sum(dln, axis=0, keepdims=True)
        db2_ref[...] += jnp.sum(dx_ref[...], axis=0, keepdims=True)
        dxh = dln * gv
        dcv_ref[...] = rstd * (dxh - jnp.mean(dxh, axis=-1, keepdims=True)
                               - xhat * jnp.mean(dxh * xhat, axis=-1, keepdims=True))

    return pl.pallas_call(
        body, out_shape=[SDS((S, D), F32), SDS((1, D), F32), SDS((1, D), F32), SDS((1, D), F32)], grid=(S // tm,),
        in_specs=[_row(tm, D), _row(tm, D), _full((1, D)), _full((1, D)), _row(tm, D)],
        out_specs=[_row(tm, D), _full((1, D)), _full((1, D)), _full((1, D))], name="ln_silu_bwd",
        compiler_params=_cp(1),
    )(ds, cv, g, b, dx)


_GELU_C, _GELU_K = 0.7978845608028654, 0.044715


def _gelu(x, with_grad=False):
    x2 = x * x
    t = jnp.tanh(_GELU_C * (x + _GELU_K * x * x2))
    gel = 0.5 * x * (1.0 + t)
    if not with_grad:
        return gel
    return gel, 0.5 * (1.0 + t) + 0.5 * x * (1.0 - t * t) * (_GELU_C * (1.0 + 3.0 * _GELU_K * x2))


def _expm1(x):
    poly = x * (1.0 + x * (0.5 + x * (1.0 / 6.0 + x * (1.0 / 24.0 + x * (1.0 / 120.0)))))
    return jnp.where(jnp.abs(x) < 0.05, poly, jnp.exp(x) - 1.0)


def _softplus(x):
    return jnp.maximum(x, 0.0) + jnp.log1p(jnp.exp(-jnp.abs(x)))


_SCAN_UNROLL = 8
_RB = 32
_HB = 16


def _sub_blocks(n_rows, n_lanes, fn):
    def step(idx, c):
        r0 = pl.multiple_of(idx * _RB, _RB)
        for lt in range(n_lanes // LANE):
            fn(r0, lt)
        return c

    lax.fori_loop(0, n_rows // _RB, step, 0)


def _lanes(lt):
    return pl.ds(lt * LANE, LANE)


def _psum8(x):
    parts = [x[i * SUB:(i + 1) * SUB] for i in range(x.shape[0] // SUB)]
    return functools.reduce(lambda p, q: p + q, parts)


def _scan_fwd(a_s, b_s, out_ref, carry_ref, n_groups):
    row = lax.broadcasted_iota(jnp.int32, (SUB, LANE), 0)
    U = _SCAN_UNROLL

    def step(gi, carry):
        base = gi * (SUB * U)
        parts = []
        for u in range(U):
            i = pl.multiple_of(base + u * SUB, SUB)
            a8, b8 = a_s[pl.ds(i, SUB), :], b_s[pl.ds(i, SUB), :]
            for s in (1, 2, 4):
                a_sh = jnp.where(row >= s, pltpu.roll(a8, s, 0), 1.0)
                b_sh = jnp.where(row >= s, pltpu.roll(b8, s, 0), 0.0)
                b8 = a8 * b_sh + b8
                a8 = a8 * a_sh
            parts.append((i, a8, b8))
        for i, a8, b8 in parts:
            h8 = a8 * carry + b8
            out_ref[pl.ds(i, SUB), :] = h8
            carry = jnp.broadcast_to(h8[SUB - 1:SUB, :], (SUB, LANE))
        return carry

    carry_ref[...] = lax.fori_loop(0, n_groups // U, step, carry_ref[...])


def _scan_bwd(a_s, b_s, out_ref, carry_ref, n_groups):
    row = lax.broadcasted_iota(jnp.int32, (SUB, LANE), 0)
    U = _SCAN_UNROLL

    def step(gi, carry):
        base = (n_groups // U - 1 - gi) * (SUB * U)
        parts = []
        for u in reversed(range(U)):
            i = pl.multiple_of(base + u * SUB, SUB)
            a8, b8 = a_s[pl.ds(i, SUB), :], b_s[pl.ds(i, SUB), :]
            for s in (1, 2, 4):
                a_sh = jnp.where(row < SUB - s, pltpu.roll(a8, SUB - s, 0), 1.0)
                b_sh = jnp.where(row < SUB - s, pltpu.roll(b8, SUB - s, 0), 0.0)
                b8 = a8 * b_sh + b8
                a8 = a8 * a_sh
            parts.append((i, a8, b8))
        for i, a8, b8 in parts:
            h8 = a8 * carry + b8
            out_ref[pl.ds(i, SUB), :] = h8
            carry = jnp.broadcast_to(h8[0:1, :], (SUB, LANE))
        return carry

    carry_ref[...] = lax.fori_loop(0, n_groups // U, step, carry_ref[...])


def _rglru_pre(xr, wgx_ref, bgx_ref, wga_ref, bga_ref, lam_ref):
    xrb = xr.astype(BF16)
    wgx, wga = wgx_ref[0].astype(BF16), wga_ref[0].astype(BF16)
    gx = _sigmoid(lax.dot_general(xrb, wgx, _NN, preferred_element_type=F32) + bgx_ref[...])
    ga = _sigmoid(lax.dot_general(xrb, wga, _NN, preferred_element_type=F32) + bga_ref[...])
    sp = _softplus(-lam_ref[...])
    log_a = -C_RG * ga * sp
    a = jnp.exp(log_a)
    mult = jnp.sqrt(-_expm1(2.0 * log_a))
    return gx, ga, sp, a, mult, xrb, wgx, wga


def _a_specs():
    vec = pl.BlockSpec((1, HD_A), lambda c, j: (0, c))
    mat = pl.BlockSpec((1, HD_A, HD_A), lambda c, j: (c, 0, 0))
    return [pl.BlockSpec((CONV_A, HD_A), lambda c, j: (0, c)), vec, mat, vec, mat, vec, vec]


def _a_fwd(zp, conv_w, conv_b, wgx, bgx, wga, bga, lam):
    S = zp.shape[0]
    R, nt = R_RGLRU, D // HD_A
    H = SUB

    def body(zg_ref, zr_ref, cw_ref, cb_ref, wgx_ref, bgx_ref, wga_ref, bga_ref, lam_ref, ya_ref, h_ref,
             ext, a_s, b_s, hc):
        j = pl.program_id(1)

        @pl.when(j == 0)
        def _():
            ext[0:H, :] = jnp.zeros((H, HD_A), F32)
            hc[...] = jnp.zeros_like(hc)

        ext[H:H + R, :] = zr_ref[...].astype(F32)
        xr = cb_ref[...]
        for k in range(CONV_A):
            xr = xr + cw_ref[k:k + 1, :] * ext[pl.ds(H - (CONV_A - 1 - k), R), :]
        gx, _, _, a, mult, _, _, _ = _rglru_pre(xr, wgx_ref, bgx_ref, wga_ref, bga_ref, lam_ref)
        a_s[...] = a
        b_s[...] = mult * (gx * xr)
        _scan_fwd(a_s, b_s, h_ref, hc, R // SUB)
        ya_ref[...] = (_gelu(zg_ref[...].astype(F32)) * h_ref[...]).astype(BF16)
        ext[0:H, :] = ext[R:R + H, :]

    return pl.pallas_call(
        body, out_shape=[SDS((S, D + D // 2), BF16), SDS((S, D), F32)], grid=(nt, S // R),
        in_specs=[pl.BlockSpec((R, HD_A), lambda c, j: (j, c)), pl.BlockSpec((R, HD_A), lambda c, j: (j, nt + c))]
        + _a_specs(),
        out_specs=[pl.BlockSpec((R, HD_A), lambda c, j: (j, c)), pl.BlockSpec((R, HD_A), lambda c, j: (j, c))],
        scratch_shapes=[pltpu.VMEM((H + R, HD_A), F32), pltpu.VMEM((R, HD_A), F32), pltpu.VMEM((R, HD_A), F32),
                        pltpu.VMEM((SUB, HD_A), F32)],
        name="rglru_fwd", compiler_params=_cp(2),
    )(zp, zp, conv_w, conv_b, wgx, bgx, wga, bga, lam)


def _a_bwd(dyab, zp, h, conv_w, conv_b, wgx, bgx, wga, bga, lam):
    S = zp.shape[0]
    R, nt, nch = R_RGLRU, D // HD_A, S // R_RGLRU
    H = SUB

    def rows(c, j):
        return (nch - 1 - j, c)

    def rows_rec(c, j):
        return (nch - 1 - j, nt + c)

    def halo(c, j):
        return (jnp.maximum((nch - 1 - j) * (R // H) - 1, 0), c)

    def halo_z(c, j):
        return (jnp.maximum((nch - 1 - j) * (R // _HB) - 1, 0), nt + c)

    def body(dy_ref, zg_ref, zr_ref, zh_ref, h_ref, hh_ref, cw_ref, cb_ref, wgx_ref, bgx_ref, wga_ref, bga_ref,
             lam_ref, dzg_ref, dzr_ref, dcw_ref, dcb_ref, dwgx_ref, dbgx_ref, dwga_ref, dbga_ref, dlam_ref,
             ext_z, ext_h, ext_mu, ext_d, a_s, b_s, muc):
        j = pl.program_id(1)
        first_chunk = (nch - 1 - j) == 0

        @pl.when(j == 0)
        def _():
            ext_mu[R:R + H, :] = jnp.zeros((H, HD_A), F32)
            ext_d[R:R + H, :] = jnp.zeros((H, HD_A), F32)
            muc[...] = jnp.zeros_like(muc)
            for r in (dcw_ref, dcb_ref, dwgx_ref, dbgx_ref, dwga_ref, dbga_ref, dlam_ref):
                r[...] = jnp.zeros_like(r)

        zg = zg_ref[...].astype(F32)
        ext_z[0:H, :] = jnp.where(first_chunk, 0.0, zh_ref[_HB - H:_HB, :].astype(F32))
        ext_z[H:H + R, :] = zr_ref[...].astype(F32)
        ext_h[0:H, :] = jnp.where(first_chunk, 0.0, hh_ref[...])
        ext_h[H:H + R, :] = h_ref[...]
        xr = cb_ref[...]
        for k in range(CONV_A):
            xr = xr + cw_ref[k:k + 1, :] * ext_z[pl.ds(H - (CONV_A - 1 - k), R), :]
        gx, ga, sp, a, mult, xrb, wgxb, wgab = _rglru_pre(xr, wgx_ref, bgx_ref, wga_ref, bga_ref, lam_ref)
        gel, dgel = _gelu(zg, with_grad=True)
        dy = dy_ref[...].astype(F32)
        dh = dy * gel
        dzg_ref[...] = (dy * h_ref[...] * dgel).astype(BF16)
        a_s[...] = a
        b_s[...] = a * dh
        _scan_bwd(a_s, b_s, ext_mu, muc, R // SUB)
        lam_t = dh + ext_mu[pl.ds(1, R), :]
        ext_mu[R:R + H, :] = ext_mu[0:H, :]
        da = lam_t * ext_h[pl.ds(H - 1, R), :]
        gxr = gx * xr
        dlog_a = da * a - (lam_t * gxr) * (a * a) / mult
        dgx = lam_t * mult * xr
        dxr = lam_t * mult * gx
        lam_v = lam_ref[...]
        dlam_ref[...] += jnp.sum(dlog_a * ga, axis=0, keepdims=True) * (C_RG * _sigmoid(-lam_v))
        dpa = (dlog_a * (-C_RG * sp)) * ga * (1.0 - ga)
        dpx = dgx * gx * (1.0 - gx)
        dbga_ref[...] += jnp.sum(dpa, axis=0, keepdims=True)
        dbgx_ref[...] += jnp.sum(dpx, axis=0, keepdims=True)
        dpab, dpxb = dpa.astype(BF16), dpx.astype(BF16)
        dwga_ref[0] += lax.dot_general(xrb, dpab, _TN, preferred_element_type=F32)
        dwgx_ref[0] += lax.dot_general(xrb, dpxb, _TN, preferred_element_type=F32)
        dxr = (dxr + lax.dot_general(dpab, wgab, _NT, preferred_element_type=F32)
               + lax.dot_general(dpxb, wgxb, _NT, preferred_element_type=F32))
        dcb_ref[...] += jnp.sum(dxr, axis=0, keepdims=True)
        ext_d[0:R, :] = dxr
        dzr = jnp.zeros((R, HD_A), F32)
        for k in range(CONV_A):
            sh = CONV_A - 1 - k
            dcw_ref[k:k + 1, :] += jnp.sum(dxr * ext_z[pl.ds(H - sh, R), :], axis=0, keepdims=True)
            dzr = dzr + cw_ref[k:k + 1, :] * ext_d[pl.ds(sh, R), :]
        dzr_ref[...] = dzr.astype(BF16)
        ext_d[R:R + H, :] = ext_d[0:H, :]

    vec_o = pl.BlockSpec((1, HD_A), lambda c, j: (0, c))
    mat_o = pl.BlockSpec((1, HD_A, HD_A), lambda c, j: (c, 0, 0))
    return pl.pallas_call(
        body,
        out_shape=[SDS((S, D), BF16), SDS((S, D), BF16), SDS((CONV_A, D), F32), SDS((1, D), F32),
                   SDS((nt, HD_A, HD_A), F32), SDS((1, D), F32), SDS((nt, HD_A, HD_A), F32), SDS((1, D), F32),
                   SDS((1, D), F32)],
        grid=(nt, nch),
        in_specs=[pl.BlockSpec((R, HD_A), rows), pl.BlockSpec((R, HD_A), rows), pl.BlockSpec((R, HD_A), rows_rec),
                  pl.BlockSpec((_HB, HD_A), halo_z), pl.BlockSpec((R, HD_A), rows),
                  pl.BlockSpec((H, HD_A), halo)] + _a_specs(),
        out_specs=[pl.BlockSpec((R, HD_A), rows), pl.BlockSpec((R, HD_A), rows),
                   pl.BlockSpec((CONV_A, HD_A), lambda c, j: (0, c)), vec_o, mat_o, vec_o, mat_o, vec_o, vec_o],
        scratch_shapes=[pltpu.VMEM((H + R, HD_A), F32), pltpu.VMEM((H + R, HD_A), F32), pltpu.VMEM((R + H, HD_A), F32),
                        pltpu.VMEM((R + H, HD_A), F32), pltpu.VMEM((R, HD_A), F32), pltpu.VMEM((R, HD_A), F32),
                        pltpu.VMEM((SUB, HD_A), F32)],
        name="rglru_bwd", compiler_params=_cp(2),
    )(dyab, zp, zp, zp, h, h, conv_w, conv_b, wgx, bgx, wga, bga, lam)


_POOL_H = 16
_POOL_T0 = 2 * D // HD_A
_POOL_Y0 = D // HD_A


def _window_sum(lv, n, lo, rows, g, ahead):
    base = 0 if ahead else SUB
    cur, win = lv[0], None
    for i, s in enumerate((1, 2, 4, 8)):
        val = cur[pl.ds(base, n), :] + cur[pl.ds(base + (s if ahead else -s), n), :]
        sel = val[lo:lo + rows]
        win = sel if win is None else jnp.where(g >= i, sel, win)
        if i < 3:
            lv[i + 1][pl.ds(base, n), :] = val
            cur = lv[i + 1]
    return win


def _pool_width(g):
    return jnp.where(g == 0, 2.0, jnp.where(g == 1, 4.0, jnp.where(g == 2, 8.0, 16.0)))


def _b_fwd(zp, yab, wg, bg, sc):
    S = zp.shape[0]
    R, H = R_SEQ, _POOL_H

    def body(z_ref, wg_ref, bg_ref, sc_ref, yab_in, yb_ref, *lv):
        del yab_in
        g, j = pl.program_id(0), pl.program_id(1)

        @pl.when(j == 0)
        def _():
            for r in lv:
                r[0:SUB, :] = jnp.zeros((SUB, HD_A), F32)
            lv[0][SUB:SUB + H, :] = jnp.zeros((H, HD_A), F32)

        u = z_ref[...].astype(F32)
        lv[0][SUB + H:SUB + H + R, :] = u
        t1 = (j * R + 1 + lax.broadcasted_iota(jnp.int32, (R, HD_A), 0)).astype(F32)
        p = _window_sum(lv, H + R, H, R, g, False) / jnp.minimum(t1, _pool_width(g)) - u
        lin = lax.dot_general(p.astype(BF16), wg_ref[0].astype(BF16), _NN, preferred_element_type=F32) + bg_ref[...]
        yb_ref[...] = (lin * sc_ref[...]).astype(BF16)
        lv[0][SUB:SUB + H, :] = lv[0][SUB + R:SUB + R + H, :]

    vec = pl.BlockSpec((1, HD_A), lambda g, j: (0, g))
    return pl.pallas_call(
        body, out_shape=SDS(yab.shape, yab.dtype), grid=(len(POOL_WINDOWS), S // R),
        in_specs=[pl.BlockSpec((R, HD_A), lambda g, j: (j, _POOL_T0 + g)),
                  pl.BlockSpec((1, HD_A, HD_A), lambda g, j: (g, 0, 0)), vec, vec, pl.BlockSpec(memory_space=pl.ANY)],
        out_specs=pl.BlockSpec((R, HD_A), lambda g, j: (j, _POOL_Y0 + g)),
        scratch_shapes=[pltpu.VMEM((SUB + H + R, HD_A), F32)] * 4, input_output_aliases={4: 0},
        name="pool_fwd", compiler_params=_cp(2),
    )(zp, wg, bg, sc, yab)


def _b_bwd(dyab, zp, wg, bg, sc):
    S = zp.shape[0]
    R, H, nch, ng = R_SEQ, _POOL_H, S // R_SEQ, len(POOL_WINDOWS)

    def body(dy_ref, z_ref, zh_ref, wg_ref, bg_ref, sc_ref, dz_ref, dwg_ref, dbg_ref, dsc_ref, *scratch):
        lu, lq = scratch[:4], scratch[4:]
        g, j = pl.program_id(0), pl.program_id(1)
        jj = nch - 1 - j

        @pl.when(j == 0)
        def _():
            for r in lu:
                r[0:SUB, :] = jnp.zeros((SUB, HD_A), F32)
            for r in lq:
                r[R + H:R + H + SUB, :] = jnp.zeros((SUB, HD_A), F32)
            lq[0][R:R + H, :] = jnp.zeros((H, HD_A), F32)
            for r in (dwg_ref, dbg_ref, dsc_ref):
                r[...] = jnp.zeros_like(r)

        u = z_ref[...].astype(F32)
        lu[0][SUB:SUB + H, :] = jnp.where(jj == 0, 0.0, zh_ref[...].astype(F32))
        lu[0][SUB + H:SUB + H + R, :] = u
        t1 = (jj * R + 1 + lax.broadcasted_iota(jnp.int32, (R, HD_A), 0)).astype(F32)
        cnt = jnp.minimum(t1, _pool_width(g))
        pb = (_window_sum(lu, H + R, H, R, g, False) / cnt - u).astype(BF16)
        wgb = wg_ref[0].astype(BF16)
        lin = lax.dot_general(pb, wgb, _NN, preferred_element_type=F32) + bg_ref[...]
        dy = dy_ref[...].astype(F32)
        dsc_ref[...] += jnp.sum(dy * lin, axis=0, keepdims=True)
        dlin = dy * sc_ref[...]
        dbg_ref[...] += jnp.sum(dlin, axis=0, keepdims=True)
        dlb = dlin.astype(BF16)
        dwg_ref[0] += lax.dot_general(pb, dlb, _TN, preferred_element_type=F32)
        dp = lax.dot_general(dlb, wgb, _NT, preferred_element_type=F32)
        lq[0][0:R, :] = dp / cnt
        dz_ref[...] = (_window_sum(lq, R + H, 0, R, g, True) - dp).astype(BF16)
        lq[0][R:R + H, :] = lq[0][0:H, :]

    vec = pl.BlockSpec((1, HD_A), lambda g, j: (0, g))
    mat = pl.BlockSpec((1, HD_A, HD_A), lambda g, j: (g, 0, 0))
    return pl.pallas_call(
        body, out_shape=[SDS((S, D // 2), BF16), SDS((ng, HD_A, HD_A), F32), SDS((1, D // 2), F32),
                         SDS((1, D // 2), F32)],
        grid=(ng, nch),
        in_specs=[pl.BlockSpec((R, HD_A), lambda g, j: (nch - 1 - j, _POOL_Y0 + g)),
                  pl.BlockSpec((R, HD_A), lambda g, j: (nch - 1 - j, _POOL_T0 + g)),
                  pl.BlockSpec((H, HD_A), lambda g, j: (jnp.maximum((nch - 1 - j) * (R // H) - 1, 0), _POOL_T0 + g)),
                  mat, vec, vec],
        out_specs=[pl.BlockSpec((R, HD_A), lambda g, j: (nch - 1 - j, g)), mat, vec, vec],
        scratch_shapes=[pltpu.VMEM((SUB + H + R, HD_A), F32)] * 8,
        name="pool_bwd", compiler_params=_cp(2),
    )(dyab, zp, zp, wg, bg, sc)


_CW_F = 768


def _f_fwd(hp, w, b, name):
    S = hp.shape[0]
    R, H, cw = min(S, R_FFN), SUB, _CW_F
    nlt = cw // LANE

    def body(h_ref, w_ref, b_ref, o_ref, gel_ref, ud_ref, ext):
        j = pl.program_id(1)

        @pl.when(j == 0)
        def _():
            ext[:, 0:H, :] = jnp.zeros((nlt, H, LANE), F32)

        def stage(r0, lt):
            ext[lt, pl.ds(pl.multiple_of(r0 + H, SUB), _RB), :] = h_ref[pl.ds(r0, _RB), _lanes(lt)].astype(F32)

        def main(r0, lt):
            ls = _lanes(lt)
            gp = b_ref[:, ls]
            for k in range(CONV_F):
                gp = gp + w_ref[k:k + 1, ls] * ext[lt, pl.ds(r0 + (H - (CONV_F - 1 - k)), _RB), :]
            up = h_ref[pl.ds(r0, _RB), _lanes(lt + nlt)].astype(F32)
            gel, dgel = _gelu(gp, with_grad=True)
            rs = pl.ds(r0, _RB)
            o_ref[rs, ls] = (gel * up).astype(BF16)
            gel_ref[rs, ls] = gel.astype(BF16)
            ud_ref[rs, ls] = (up * dgel).astype(BF16)

        _sub_blocks(R, cw, stage)
        _sub_blocks(R, cw, main)
        ext[:, 0:H, :] = ext[:, R:R + H, :]

    tile = pl.BlockSpec((R, cw), lambda c, j: (j, c))
    return pl.pallas_call(
        body, out_shape=[SDS((S, D_FF), BF16)] * 3, grid=(D_FF // cw, S // R),
        in_specs=[pl.BlockSpec((R, 2 * cw), lambda c, j: (j, c)), pl.BlockSpec((CONV_F, cw), lambda c, j: (0, c)),
                  pl.BlockSpec((1, cw), lambda c, j: (0, c))],
        out_specs=[tile] * 3,
        scratch_shapes=[pltpu.VMEM((nlt, H + R, LANE), F32)], name=name, compiler_params=_cp(2),
    )(hp, w, b)


def _f_bwd(dact, hp, gel, ud, w, name):
    S = hp.shape[0]
    R, H, cw = min(S, R_FFN), SUB, _CW_F
    nch = S // R
    nlt = cw // LANE

    def body(da_ref, h_ref, hh_ref, gel_ref, ud_ref, w_ref, dh_ref, dw_ref, db_ref, ext_g, ext_d, acc):
        j = pl.program_id(1)
        jj = nch - 1 - j

        @pl.when(j == 0)
        def _():
            ext_d[:, R:R + H, :] = jnp.zeros((nlt, H, LANE), F32)
            acc[...] = jnp.zeros_like(acc)

        for lt in range(nlt):
            ext_g[lt, 0:H, :] = jnp.where(jj == 0, 0.0, hh_ref[_HB - H:_HB, lt * LANE:(lt + 1) * LANE].astype(F32))

        def stage(r0, lt):
            ext_g[lt, pl.ds(pl.multiple_of(r0 + H, SUB), _RB), :] = h_ref[pl.ds(r0, _RB), _lanes(lt)].astype(F32)

        def first(r0, lt):
            ls, lu, rs = _lanes(lt), _lanes(lt + nlt), pl.ds(r0, _RB)
            da = da_ref[rs, ls].astype(F32)
            dh_ref[rs, lu] = (da * gel_ref[rs, ls].astype(F32)).astype(BF16)
            dgp = da * ud_ref[rs, ls].astype(F32)
            ext_d[lt, rs, :] = dgp
            acc[CONV_F * SUB:(CONV_F + 1) * SUB, ls] += _psum8(dgp)
            for k in range(CONV_F):
                tap = ext_g[lt, pl.ds(r0 + (H - (CONV_F - 1 - k)), _RB), :]
                acc[k * SUB:(k + 1) * SUB, ls] += _psum8(dgp * tap)

        def second(r0, lt):
            ls = _lanes(lt)
            dhg = w_ref[CONV_F - 1:CONV_F, ls] * ext_d[lt, pl.ds(r0, _RB), :]
            for k in range(CONV_F - 1):
                dhg = dhg + w_ref[k:k + 1, ls] * ext_d[lt, pl.ds(r0 + (CONV_F - 1 - k), _RB), :]
            dh_ref[pl.ds(r0, _RB), ls] = dhg.astype(BF16)

        _sub_blocks(R, cw, stage)
        _sub_blocks(R, cw, first)
        _sub_blocks(R, cw, second)
        ext_d[:, R:R + H, :] = ext_d[:, 0:H, :]

        @pl.when(j == nch - 1)
        def _():
            for k in range(CONV_F):
                dw_ref[k:k + 1, :] = jnp.sum(acc[k * SUB:(k + 1) * SUB, :], axis=0, keepdims=True)
            db_ref[...] = jnp.sum(acc[CONV_F * SUB:(CONV_F + 1) * SUB, :], axis=0, keepdims=True)

    rows = lambda c, j: (nch - 1 - j, c)
    return pl.pallas_call(
        body, out_shape=[SDS((S, 2 * D_FF), BF16), SDS((CONV_F, D_FF), F32), SDS((1, D_FF), F32)],
        grid=(D_FF // cw, nch),
        in_specs=[pl.BlockSpec((R, cw), rows), pl.BlockSpec((R, cw), lambda c, j: (nch - 1 - j, 2 * c)),
                  pl.BlockSpec((_HB, cw), lambda c, j: (jnp.maximum((nch - 1 - j) * (R // _HB) - 1, 0), 2 * c)),
                  pl.BlockSpec((R, cw), rows), pl.BlockSpec((R, cw), rows),
                  pl.BlockSpec((CONV_F, cw), lambda c, j: (0, c))],
        out_specs=[pl.BlockSpec((R, 2 * cw), rows), pl.BlockSpec((CONV_F, cw), lambda c, j: (0, c)),
                   pl.BlockSpec((1, cw), lambda c, j: (0, c))],
        scratch_shapes=[pltpu.VMEM((nlt, H + R, LANE), F32), pltpu.VMEM((nlt, R + H, LANE), F32),
                        pltpu.VMEM(((CONV_F + 1) * SUB, cw), F32)], name=name,
        compiler_params=_cp(2),
    )(dact, hp, hp, gel, ud, w)


_CW_C = 256
_H_C = 32


def _c_fwd(h1p, w, b):
    S = h1p.shape[0]
    R, H, cw = R_SEQ, _H_C, _CW_C
    nlt = cw // LANE

    def body(h_ref, w_ref, b_ref, o_ref, ext):
        j = pl.program_id(1)

        @pl.when(j == 0)
        def _():
            ext[:, 0:H, :] = jnp.zeros((nlt, H, LANE), F32)

        def stage(r0, lt):
            rs = pl.ds(r0, _RB)
            gate = h_ref[rs, _lanes(lt + nlt)].astype(F32)
            ext[lt, pl.ds(pl.multiple_of(r0 + H, SUB), _RB), :] = h_ref[rs, _lanes(lt)].astype(F32) * _sigmoid(gate)

        def main(r0, lt):
            ls = _lanes(lt)
            cv = b_ref[:, ls]
            for k in range(CONV_C):
                cv = cv + w_ref[k:k + 1, ls] * ext[lt, pl.ds(r0 + (H - (CONV_C - 1 - k)), _RB), :]
            o_ref[pl.ds(r0, _RB), ls] = cv

        _sub_blocks(R, cw, stage)
        _sub_blocks(R, cw, main)
        ext[:, 0:H, :] = ext[:, R:R + H, :]

    return pl.pallas_call(
        body, out_shape=SDS((S, D), F32), grid=(D // cw, S // R),
        in_specs=[pl.BlockSpec((R, 2 * cw), lambda c, j: (j, c)), pl.BlockSpec((CONV_C, cw), lambda c, j: (0, c)),
                  pl.BlockSpec((1, cw), lambda c, j: (0, c))],
        out_specs=pl.BlockSpec((R, cw), lambda c, j: (j, c)),
        scratch_shapes=[pltpu.VMEM((nlt, H + R, LANE), F32)], name="conf_conv_fwd", compiler_params=_cp(2),
    )(h1p, w, b)


def _c_bwd(dcv, h1p, w):
    S = h1p.shape[0]
    R, H, cw, nch = R_SEQ, _H_C, _CW_C, S // R_SEQ
    nlt = cw // LANE
    a_b, a_val, a_gate = CONV_C * SUB, (CONV_C + 1) * SUB, (CONV_C + 2) * SUB

    def body(dc_ref, h_ref, hh_ref, w_ref, dh_ref, dw_ref, db_ref, db1_ref, ext_u, ext_d, acc):
        j = pl.program_id(1)
        jj = nch - 1 - j

        @pl.when(j == 0)
        def _():
            ext_d[:, R:R + H, :] = jnp.zeros((nlt, H, LANE), F32)
            acc[...] = jnp.zeros_like(acc)

        for lt in range(nlt):
            ext_u[lt, 0:H, :] = jnp.where(
                jj == 0, 0.0, hh_ref[:, lt * LANE:(lt + 1) * LANE].astype(F32)
                * _sigmoid(hh_ref[:, cw + lt * LANE:cw + (lt + 1) * LANE].astype(F32)))

        def stage(r0, lt):
            rs, ls = pl.ds(r0, _RB), _lanes(lt)
            gate = h_ref[rs, _lanes(lt + nlt)].astype(F32)
            ext_u[lt, pl.ds(pl.multiple_of(r0 + H, SUB), _RB), :] = h_ref[rs, ls].astype(F32) * _sigmoid(gate)
            ext_d[lt, rs, :] = dc_ref[rs, ls]

        def first(r0, lt):
            ls = _lanes(lt)
            dc = dc_ref[pl.ds(r0, _RB), ls]
            acc[a_b:a_b + SUB, ls] += _psum8(dc)
            for k in range(CONV_C):
                tap = ext_u[lt, pl.ds(r0 + (H - (CONV_C - 1 - k)), _RB), :]
                acc[k * SUB:(k + 1) * SUB, ls] += _psum8(dc * tap)

        def second(r0, lt):
            rs, ls, lg = pl.ds(r0, _RB), _lanes(lt), _lanes(lt + nlt)
            du = w_ref[CONV_C - 1:CONV_C, ls] * ext_d[lt, rs, :]
            for k in range(CONV_C - 1):
                du = du + w_ref[k:k + 1, ls] * ext_d[lt, pl.ds(r0 + (CONV_C - 1 - k), _RB), :]
            val = h_ref[rs, ls].astype(F32)
            sg = _sigmoid(h_ref[rs, lg].astype(F32))
            dval = du * sg
            dgate = du * val * sg * (1.0 - sg)
            acc[a_val:a_val + SUB, ls] += _psum8(dval)
            acc[a_gate:a_gate + SUB, ls] += _psum8(dgate)
            dh_ref[rs, ls] = dval.astype(BF16)
            dh_ref[rs, lg] = dgate.astype(BF16)

        _sub_blocks(R, cw, stage)
        _sub_blocks(R, cw, first)
        _sub_blocks(R, cw, second)
        ext_d[:, R:R + H, :] = ext_d[:, 0:H, :]

        @pl.when(j == nch - 1)
        def _():
            for k in range(CONV_C):
                dw_ref[k:k + 1, :] = jnp.sum(acc[k * SUB:(k + 1) * SUB, :], axis=0, keepdims=True)
            db_ref[...] = jnp.sum(acc[a_b:a_b + SUB, :], axis=0, keepdims=True)
            db1_ref[:, 0:cw] = jnp.sum(acc[a_val:a_val + SUB, :], axis=0, keepdims=True)
            db1_ref[:, cw:2 * cw] = jnp.sum(acc[a_gate:a_gate + SUB, :], axis=0, keepdims=True)

    rows = lambda c, j: (nch - 1 - j, c)
    return pl.pallas_call(
        body, out_shape=[SDS((S, 2 * D), BF16), SDS((CONV_C, D), F32), SDS((1, D), F32), SDS((1, 2 * D), F32)],
        grid=(D // cw, nch),
        in_specs=[pl.BlockSpec((R, cw), rows), pl.BlockSpec((R, 2 * cw), rows),
                  pl.BlockSpec((H, 2 * cw), lambda c, j: (jnp.maximum((nch - 1 - j) * (R // H) - 1, 0), c)),
                  pl.BlockSpec((CONV_C, cw), lambda c, j: (0, c))],
        out_specs=[pl.BlockSpec((R, 2 * cw), rows), pl.BlockSpec((CONV_C, cw), lambda c, j: (0, c)),
                   pl.BlockSpec((1, cw), lambda c, j: (0, c)), pl.BlockSpec((1, 2 * cw), lambda c, j: (0, c))],
        scratch_shapes=[pltpu.VMEM((nlt, H + R, LANE), F32), pltpu.VMEM((nlt, R + H, LANE), F32),
                        pltpu.VMEM(((CONV_C + 3) * SUB, cw), F32)], name="conf_conv_bwd",
        compiler_params=_cp(2),
    )(dcv, h1p, h1p, w)


def _local_step(x, mem, tgt, W, fetch=None, send=None):
    G = {}
    W = dict(W)

    def arrive(group, after):
        if fetch is None:
            return None
        got, tok = fetch(group, after)
        for key, val in got.items():
            W[key] = {**W.get(key, {}), **val} if isinstance(val, dict) else val
        return tok

    def gain(g, tok):
        return g if tok is None else g + tok

    def sent(group):
        return None if send is None else send(group, G)

    def xattn_fwd(xin, n, l):
        tok = arrive(("xa", l), n)
        mn = _rms_fwd(mem, gain(W["xa_mem_norm"][l:l + 1], tok), f"xa_memnorm_fwd{l}")
        q = _mm_nn(n, W["xa_wq"][l], out_dtype=BF16, name=f"xa_q{l}")
        k = _mm_nn(mn, W["xa_wk"][l], out_dtype=BF16, name=f"xa_k{l}")
        v = _mm_nn(mn, W["xa_wv"][l], out_dtype=BF16, name=f"xa_v{l}")
        o = _attn_fwd(q, k, v, f"xa_attn_fwd{l}")
        xout, nout = _mm_nn(o, W["xa_wo"][l], out_dtype=F32, name=f"xa_o{l}", add=xin, norm=W["f_norm"][l:l + 1])
        return xout, nout, (xin, n, q, mn, k, v, o)

    def xattn_bwd(dx, dxb, saved, l):
        xin, n, q, mn, k, v, o = saved
        do = _mm_nt(dxb, W["xa_wo"][l], out_dtype=BF16, name=f"xa_do{l}")
        G[f"xa_wo{l}"] = _mm_tn(o, dxb, out_dtype=BF16, name=f"xa_dwo{l}")
        dq, dk, dv = _attn_bwd(q, k, v, do, f"xa_attn_bwd{l}")
        dkb, dvb = dk.astype(BF16), dv.astype(BF16)
        G[f"xa_wq{l}"] = _mm_tn(n, dq, out_dtype=BF16, name=f"xa_dwq{l}")
        G[f"xa_wk{l}"] = _mm_tn(mn, dkb, out_dtype=BF16, name=f"xa_dwk{l}")
        G[f"xa_wv{l}"] = _mm_tn(mn, dvb, out_dtype=BF16, name=f"xa_dwv{l}")
        tok = sent(("xa", l))
        dmn = _mm_nt(dkb, W["xa_wk"][l], out_dtype=F32, name=f"xa_dmn_k{l}")
        dmn = _mm_nt(dvb, W["xa_wv"][l], out_dtype=F32, name=f"xa_dmn_v{l}", add=dmn)
        (G[f"xa_mem_norm{l}"],) = _rms_bwd(mem, W["xa_mem_norm"][l:l + 1], dmn, None, f"xa_memnorm_bwd{l}")
        dx, dxb, G[f"xa_norm{l}"] = _mm_nt(dq, W["xa_wq"][l], out_dtype=F32, name=f"xa_dn{l}",
                                           rms=(xin, gain(W["xa_norm"][l:l + 1], tok), dx))
        return dx, dxb

    def ffn_fwd(xin, n, l, next_gain):
        tok = arrive(("f", l), n)
        hp = _mm_nn(n, W["f_w_up"][l], out_dtype=BF16, name=f"f_up{l}")
        act, gel, ud = _f_fwd(hp, W["f_dw_w"][l], gain(W["f_dw_b"][l:l + 1], tok), f"f_conv_fwd{l}")
        arrive(("fd", l), act)
        res = _mm_nn(act, W["f_w_down"][l], out_dtype=F32, name=f"f_down{l}", add=xin, norm=next_gain)
        xout, nout = res if next_gain is not None else (res, None)
        return xout, nout, (xin, n, hp, act, gel, ud)

    def ffn_bwd(dx, dxb, saved, l):
        xin, n, hp, act, gel, ud = saved
        dact = _mm_nt(dxb, W["f_w_down"][l], out_dtype=BF16, name=f"f_dact{l}")
        G[f"f_w_down{l}"] = _mm_tn(act, dxb, out_dtype=BF16, name=f"f_dwdown{l}")
        dhp, G[f"f_dw_w{l}"], G[f"f_dw_b{l}"] = _f_bwd(dact, hp, gel, ud, W["f_dw_w"][l], f"f_conv_bwd{l}")
        G[f"f_w_up{l}"] = _mm_tn(n, dhp, out_dtype=BF16, name=f"f_dwup{l}", blocks=_CW_F)
        tok = sent(("f", l))
        dx, dxb, G[f"f_norm{l}"] = _mm_nt(dhp, W["f_w_up"][l], out_dtype=F32, name=f"f_dn{l}",
                                          rms=(xin, gain(W["f_norm"][l:l + 1], tok), dx))
        return dx, dxb

    n0 = _rms_fwd(x, W["ab_norm"], "ab_norm_fwd")
    tok = arrive(("ab", 0), n0)
    a_par = (W["a_conv_w"], gain(W["a_conv_b"], tok), W["a_gate_x_w"], W["a_gate_x_b"], W["a_gate_a_w"],
             W["a_gate_a_b"], W["a_lambda"])
    b_par = (W["b_group_w"], W["b_group_b"], W["b_scale"])
    zp = _mm_nn(n0, W["ab_w_in"], out_dtype=BF16, name="ab_in")
    yab, h_a = _a_fwd(zp, *a_par)
    yab = _b_fwd(zp, yab, *b_par)
    arrive(("ab", 1), yab)
    x1, n1 = _mm_nn(yab, W["ab_w_out"], out_dtype=F32, name="ab_out", add=x, norm=W["xa_norm"][0:1])
    x2, n2, s_xa0 = xattn_fwd(x1, n1, 0)
    x3, n3, s_f0 = ffn_fwd(x2, n2, 0, W["c_norm"])
    tok = arrive(("c", 0), n3)
    h1p = _mm_nn(n3, W["c_w_pw1"], out_dtype=BF16, name="c_pw1", bias=gain(W["c_b_pw1"], tok))
    cv = _c_fwd(h1p, W["c_dw_w"], W["c_dw_b"])
    sc = _ln_silu_fwd(cv, W["c_ln_g"], W["c_ln_b"])
    x4, n4 = _mm_nn(sc, W["c_w_pw2"], out_dtype=F32, name="c_pw2", bias=W["c_b_pw2"], add=x3, norm=W["xa_norm"][1:2])
    x5, n5, s_xa1 = xattn_fwd(x4, n4, 1)
    x6, _, s_f1 = ffn_fwd(x5, n5, 1, None)
    loss, dx, dxb, G["final_norm"] = _loss_head(x6, W["final_norm"], tgt)

    dx, dxb = ffn_bwd(dx, dxb, s_f1, 1)
    dx, dxb = xattn_bwd(dx, dxb, s_xa1, 1)
    dsc = _mm_nt(dxb, W["c_w_pw2"], out_dtype=BF16, name="c_dsc")
    G["c_w_pw2"] = _mm_tn(sc, dxb, out_dtype=BF16, name="c_dwpw2")
    dcv, G["c_ln_g"], G["c_ln_b"], G["c_b_pw2"] = _ln_silu_bwd(dsc, cv, W["c_ln_g"], W["c_ln_b"], dx)
    dh1p, G["c_dw_w"], G["c_dw_b"], G["c_b_pw1"] = _c_bwd(dcv, h1p, W["c_dw_w"])
    G["c_w_pw1"] = _mm_tn(n3, dh1p, out_dtype=BF16, name="c_dwpw1", blocks=_CW_C)
    tok = sent(("c", 0))
    dx, dxb, G["c_norm"] = _mm_nt(dh1p, W["c_w_pw1"], out_dtype=F32, name="c_dn",
                                  rms=(x3, gain(W["c_norm"], tok), dx))
    dx, dxb = ffn_bwd(dx, dxb, s_f0, 0)
    dx, dxb = xattn_bwd(dx, dxb, s_xa0, 0)
    dyab = _mm_nt(dxb, W["ab_w_out"], out_dtype=BF16, name="ab_dyab")
    G["ab_w_out"] = _mm_tn(yab, dxb, out_dtype=BF16, name="ab_dwout")
    tok = sent(("ab", 1))
    a_par = (a_par[0], gain(a_par[1], tok)) + a_par[2:]
    (dzg, dzr, G["a_conv_w"], G["a_conv_b"], G["a_gate_x_w"], G["a_gate_x_b"], G["a_gate_a_w"], G["a_gate_a_b"],
     G["a_lambda"]) = _a_bwd(dyab, zp, h_a, *a_par)
    dzq, G["b_group_w"], G["b_group_b"], G["b_scale"] = _b_bwd(dyab, zp, *b_par)
    G["ab_w_in"] = jnp.concatenate(
        [_mm_tn(n0, dz, out_dtype=BF16, name=f"ab_dwin_{part}")
         for part, dz in (("gate", dzg), ("rec", dzr), ("pool", dzq))], axis=1)
    tok = sent(("ab", 0))
    dx, _, G["ab_norm"] = _mm_nt_cols([dzg, dzr, dzq], W["ab_w_in"], name="ab_dn",
                                      rms=(x, gain(W["ab_norm"], tok), dx))
    return loss, dx, G


def _my_place():
    x, y, c = lax.axis_index("x"), lax.axis_index("y"), lax.axis_index("c")
    return x, y, c


def _all_gather(shards, name):
    n = len(shards)

    def body(*refs):
        ins, outs = refs[:n], refs[n:2 * n]
        send_sems, recv_sems, local_sems = refs[2 * n:]
        x, y, c = _my_place()
        me, sibling = (x, y, c), (x, y, 1 - c)
        chips = [(1 - x, y), (x, 1 - y), (1 - x, 1 - y)]

        def slab(a, place):
            px, py, pc = place
            return outs[a].at[4 * px + 2 * py + pc]

        def copy(a, k, block, to, src=None):
            return pltpu.make_async_remote_copy(
                src_ref=slab(a, block) if src is None else src, dst_ref=slab(a, block),
                send_sem=send_sems.at[a, k], recv_sem=recv_sems.at[a, k], device_id=to, device_id_type=MESH)

        mine = [pltpu.make_async_copy(ins[a], slab(a, me), local_sems.at[a]) for a in range(n)]
        for cp in mine:
            cp.start()
        first = []
        for j, chip in enumerate(chips):
            first += [copy(a, 1 + j, me, (*chip, c), src=ins[a]) for a in range(n)]
        first += [copy(a, 0, me, sibling, src=ins[a]) for a in range(n)]
        for cp in first:
            cp.start()
        passed = []
        for j, chip in enumerate(chips):
            for a in range(n):
                copy(a, 1 + j, (*chip, c), me).wait_recv()
                cp = copy(a, 4 + j, (*chip, c), sibling)
                cp.start()
                passed.append(cp)
        for a in range(n):
            copy(a, 0, sibling, me).wait_recv()
        for j, chip in enumerate(chips):
            for a in range(n):
                copy(a, 4 + j, (*chip, 1 - c), me).wait_recv()
        for cp in first + passed:
            cp.wait_send()
        for cp in mine:
            cp.wait()

    any_spec = pl.BlockSpec(memory_space=pl.ANY)
    return pl.pallas_call(
        body, out_shape=[SDS((N_DEV,) + s.shape, s.dtype) for s in shards], in_specs=[any_spec] * n,
        out_specs=[any_spec] * n,
        scratch_shapes=[pltpu.SemaphoreType.DMA((n, 7)), pltpu.SemaphoreType.DMA((n, 7)), pltpu.SemaphoreType.DMA((n,))],
        name=name,
    )(*shards)


_HBM = pl.BlockSpec(memory_space=pltpu.HBM)
_SEM = pl.BlockSpec(memory_space=pltpu.SEMAPHORE)
_EFFECT = pltpu.SideEffectType.DATAFLOW_SIDE_EFFECTING


def _peer_places():
    x, y, c = _my_place()
    peers = []
    for k in range(1, N_DEV):
        px = 1 - x if (k >> 2) & 1 else x
        py = 1 - y if (k >> 1) & 1 else y
        pc = 1 - c if k & 1 else c
        peers.append(((px, py, pc), 4 * px + 2 * py + pc))
    return (x, y, c), 4 * x + 2 * y + c, peers


def _send_start(srcs, per_dest, name):
    n = len(srcs)
    lands = [lax.empty((N_DEV,) + (s.shape[1:] if per_dest else s.shape), s.dtype) for s in srcs]

    def body(*refs):
        src, land = refs[:n], refs[n:2 * n]
        outs = refs[2 * n:]
        send, recv, token = outs[:n], outs[n:2 * n], outs[4 * n]
        _, me, peers = _peer_places()
        for a in range(n):
            for peer, pidx in peers:
                pltpu.make_async_remote_copy(
                    src_ref=src[a].at[pidx] if per_dest else src[a], dst_ref=land[a].at[me], send_sem=send[a],
                    recv_sem=recv[a], device_id=peer, device_id_type=MESH).start()
        token[...] = jnp.zeros_like(token)

    hbm = lambda a: pltpu.HBM(a.shape, a.dtype)
    sem = pltpu.SemaphoreType.DMA(())
    res = pl.pallas_call(
        body, name=name,
        out_shape=tuple([sem] * (2 * n) + [hbm(s) for s in srcs] + [hbm(l) for l in lands]
                        + [SDS((SUB, LANE), F32)]),
        in_specs=[_HBM] * (2 * n),
        out_specs=tuple([_SEM] * (2 * n) + [_HBM] * (2 * n) + [pl.BlockSpec(memory_space=pltpu.VMEM)]),
        input_output_aliases={i: 2 * n + i for i in range(2 * n)},
        compiler_params=pltpu.CompilerParams(has_side_effects=_EFFECT),
    )(*[pltpu.with_memory_space_constraint(s, pltpu.HBM) for s in srcs],
      *[pltpu.with_memory_space_constraint(l, pltpu.HBM) for l in lands])
    return res[:n], res[n:2 * n], res[2 * n:3 * n], res[3 * n:4 * n], res[4 * n]


def _send_wait(send, recv, srcs, lands, after, per_dest, name):
    n = len(srcs)

    def body(*refs):
        src, land = refs[:n], refs[n:2 * n]
        send_s, recv_s = refs[2 * n:3 * n], refs[3 * n:4 * n]
        token = refs[-1]
        place, _, _ = _peer_places()
        for a in range(n):
            seven = land[a].at[pl.ds(0, N_DEV - 1)]
            copy = pltpu.make_async_remote_copy(
                src_ref=src[a].at[pl.ds(0, N_DEV - 1)] if per_dest else seven, dst_ref=seven, send_sem=send_s[a],
                recv_sem=recv_s[a], device_id=place, device_id_type=MESH)
            copy.wait_send()
            copy.wait_recv()
        token[...] = jnp.zeros_like(token)

    hbm = lambda a: pltpu.HBM(a.shape, a.dtype)
    res = pl.pallas_call(
        body, name=name,
        out_shape=tuple([hbm(s) for s in srcs] + [hbm(l) for l in lands] + [SDS((SUB, LANE), F32)]),
        in_specs=[_HBM] * (2 * n) + [_SEM] * (2 * n) + [pl.BlockSpec(memory_space=pl.ANY)],
        out_specs=tuple([_HBM] * (2 * n) + [pl.BlockSpec(memory_space=pltpu.VMEM)]),
        input_output_aliases={i: i for i in range(2 * n)},
        compiler_params=pltpu.CompilerParams(has_side_effects=_EFFECT),
    )(*srcs, *lands, *send, *recv, after)
    return res[:n], res[n:2 * n], res[2 * n]


def _adamw_math(w, g, m, v):
    m = ADAM_B1 * m + (1.0 - ADAM_B1) * g
    v = ADAM_B2 * v + (1.0 - ADAM_B2) * (g * g)
    m_hat = m / (1.0 - ADAM_B1 ** ADAM_STEP)
    v_hat = v / (1.0 - ADAM_B2 ** ADAM_STEP)
    delta = -ADAM_LR * (m_hat / (jnp.sqrt(v_hat) + ADAM_EPS) + ADAM_WD * w)
    return delta, m, v


def _row_tile(r, c, itemsize_rows):
    cap = max(SUB, (itemsize_rows // (4 * c)) // SUB * SUB)
    if r <= cap:
        return r
    best = None
    for t in range(SUB, cap + 1, SUB):
        if r % t == 0:
            best = t
    return best if best is not None else r


def _sum_adamw(landing, w, m, v, name, layer=0, prev=None, after=None):
    _, r, c = landing.shape
    tr = _row_tile(r, c, 2 << 20)
    off = layer * (r // tr)
    tail = ([] if prev is None else list(prev)) + ([] if after is None else [after])

    def body(l_ref, w_ref, m_ref, v_ref, *rest):
        g_ref, d_ref, mo_ref, vo_ref = rest[-4:]
        g = l_ref[0].astype(F32)
        for s in range(1, N_DEV):
            g = g + l_ref[s].astype(F32)
        g_ref[...] = g
        d_ref[...], mo_ref[...], vo_ref[...] = _adamw_math(w_ref[...], g, m_ref[...], v_ref[...])

    blk = pl.BlockSpec((tr, c), lambda i: (i + off, 0))
    n_prev = 0 if prev is None else 4
    return pl.pallas_call(
        body, out_shape=[SDS(w.shape, F32)] * 4, grid=(r // tr,),
        in_specs=[pl.BlockSpec((N_DEV, tr, c), lambda i: (0, i, 0)), blk, blk, blk]
        + [pl.BlockSpec(memory_space=pl.ANY)] * len(tail),
        out_specs=[blk] * 4, input_output_aliases={4 + i: i for i in range(n_prev)}, name=name,
        compiler_params=_cp(1),
    )(landing, w, m, v, *tail)


def _sum8(landing, name):
    _, r, c = landing.shape

    def body(l_ref, g_ref):
        g = l_ref[0]
        for s in range(1, N_DEV):
            g = g + l_ref[s]
        g_ref[...] = g

    return pl.pallas_call(body, out_shape=SDS((r, c), F32), name=name, compiler_params=_cp(0))(landing)


def _adamw_small(repl_pack, own_pack, P, M, V):
    table, off = [], 0
    for name, shape in _REPL.items():
        table.append((name, shape if len(shape) > 1 else (1,) + shape, 0, off // LANE))
        off += _size(shape)
    off = _REPL_ROWS * LANE
    for name, shape in _SMALL_SHARDED.items():
        table.append((name, shape, 1, off // LANE))
        off += _size(shape)
    n = len(table)

    def body(*refs):
        packs, ins, outs = refs[:2], refs[2:2 + 3 * n], refs[2 + 3 * n:]
        for p, (_, shape, which, r0) in enumerate(table):
            w_ref, m_ref, v_ref = ins[3 * p:3 * p + 3]
            g_ref, d_ref, mo_ref, vo_ref = outs[4 * p:4 * p + 4]
            pack, rows, q = packs[which], shape[-2], shape[-1] // LANE
            lead = [()]
            for dim in shape[:-2]:
                lead = [t + (i,) for t in lead for i in range(dim)]
            for li, idx in enumerate(lead):
                if q == 1:
                    dst = g_ref.at[idx] if idx else g_ref
                    dst[...] = pack[r0 + li * rows:r0 + (li + 1) * rows, :]
                    continue
                for i in range(rows):
                    for k in range(q):
                        row = r0 + (li * rows + i) * q + k
                        g_ref[idx + (slice(i, i + 1), slice(k * LANE, (k + 1) * LANE))] = pack[row:row + 1, :]
            d_ref[...], mo_ref[...], vo_ref[...] = _adamw_math(w_ref[...], g_ref[...], m_ref[...], v_ref[...])

    ins, out_shape = [], []
    for name, shape, _, _ in table:
        ins += [t[name].reshape(shape) for t in (P, M, V)]
        out_shape += [SDS(shape, F32)] * 4
    res = pl.pallas_call(body, out_shape=out_shape, name="adamw_small", compiler_params=_cp(0))(
        repl_pack, own_pack, *ins)
    dicts = ({}, {}, {}, {})
    for p, (name, shape, _, _) in enumerate(table):
        for d, arr in zip(dicts, res[4 * p:4 * p + 4]):
            d[name] = arr.reshape(P[name].shape)
    return dicts


_BIG = {
    "ab_w_in": (1, D, 320), "ab_w_out": (1, 192, D), "c_w_pw1": (1, D, 256), "c_w_pw2": (1, 128, D),
    "xa_wq": (2, 128, D), "xa_wk": (2, 128, D), "xa_wv": (2, 128, D), "xa_wo": (2, 128, D),
    "f_w_up": (2, D, 768), "f_w_down": (2, 384, D),
}
_SMALL_SHARDED = {
    "a_conv_w": (1, 4, 128), "c_norm": (1, 128), "c_b_pw1": (1, 256), "c_dw_w": (1, 31, 128), "c_dw_b": (1, 128),
    "c_ln_g": (1, 128), "c_ln_b": (1, 128), "c_b_pw2": (1, 128), "f_dw_w": (2, 3, 384),
}
_REPL = {
    "ab_norm": (1, D), "a_conv_b": (1, D), "a_gate_x_w": (1, 8, 128, 128), "a_gate_x_b": (1, D),
    "a_gate_a_w": (1, 8, 128, 128), "a_gate_a_b": (1, D), "a_lambda": (1, D), "b_group_w": (1, 4, 128, 128),
    "b_group_b": (1, 512), "b_scale": (1, 512), "xa_norm": (2, D), "xa_mem_norm": (2, D), "f_norm": (2, D),
    "f_dw_b": (2, D_FF), "final_norm": (D,),
}


def _size(shape):
    n = 1
    for s in shape:
        n *= s
    return n


_N_SS = sum(_size(s) for s in _SMALL_SHARDED.values())
_N_REPL = sum(_size(s) for s in _REPL.values())
_REPL_ROWS = -(-_N_REPL // (N_DEV * SUB * LANE)) * SUB
_SS_ROWS = _N_SS // LANE
_SMALL_ROWS = -(-(_REPL_ROWS + _SS_ROWS) // SUB) * SUB


def _pack(parts, rows):
    flat = jnp.concatenate([p.reshape(-1).astype(F32) for p in parts])
    return jnp.pad(flat, (0, rows * LANE - flat.shape[0])).reshape(rows, LANE)


def _pair_blocks(v, bw):
    lead, n = v.shape[:-1], v.shape[-1]
    return jnp.swapaxes(v.reshape(lead + (2, n // (2 * bw), bw)), -3, -2).reshape(lead + (n,))


def _unpair_blocks(v, bw):
    lead, n = v.shape[:-1], v.shape[-1]
    return jnp.swapaxes(v.reshape(lead + (n // (2 * bw), 2, bw)), -3, -2).reshape(lead + (n,))


_GROUPS = {
    ("ab", 0): (("ab_w_in", 0),),
    ("ab", 1): (("ab_w_out", 0),),
    ("xa", 0): (("xa_wq", 0), ("xa_wk", 0), ("xa_wv", 0), ("xa_wo", 0)),
    ("f", 0): (("f_w_up", 0),),
    ("fd", 0): (("f_w_down", 0),),
    ("c", 0): (("c_w_pw1", 0), ("c_w_pw2", 0)),
    ("xa", 1): (("xa_wq", 1), ("xa_wk", 1), ("xa_wv", 1), ("xa_wo", 1)),
    ("f", 1): (("f_w_up", 1),),
    ("fd", 1): (("f_w_down", 1),),
}
_SEND_GROUPS = {g: m for g, m in _GROUPS.items() if g[0] != "fd"}
_SEND_GROUPS[("f", 0)] = (("f_w_up", 0), ("f_w_down", 0))
_SEND_GROUPS[("f", 1)] = (("f_w_up", 1), ("f_w_down", 1))


def _weight_layout(name, g):
    if name == "ab_w_in":
        return jnp.swapaxes(g, 0, 1).reshape(D, N_DEV * 320)
    if name in ("c_w_pw1", "f_w_up"):
        return g
    return g.reshape(N_DEV * g.shape[1], D)


def _grad_blocks(name, l, G):
    _, r, c = _BIG[name]
    if name == "ab_w_in":
        return jnp.swapaxes(G[name].reshape(D, N_DEV, 320), 0, 1)
    if name == "c_w_pw1":
        return G[name]
    if name == "f_w_up":
        return G[f"{name}{l}"]
    return (G[name] if _BIG[name][0] == 1 else G[f"{name}{l}"]).reshape(N_DEV, r, c)


def _small_layouts(sm):
    W = {}
    sm = sm.reshape(N_DEV, -1)
    off = 0
    for name, shape in _SMALL_SHARDED.items():
        n = _size(shape)
        blocks = sm[:, off:off + n].reshape((N_DEV,) + shape)
        off += n
        W[name] = jnp.moveaxis(blocks, 0, -2).reshape(shape[:-1] + (N_DEV * shape[-1],))
    W["a_conv_w"], W["c_dw_w"] = W["a_conv_w"][0], W["c_dw_w"][0]
    W["c_b_pw1"] = _pair_blocks(W["c_b_pw1"], _CW_C)
    return W


def _with_own(land, src, me, per_dest):
    own = lax.dynamic_slice_in_dim(src, me, 1, 0) if per_dest else src[None]
    return lax.dynamic_update_slice_in_dim(land, own, me, 0)


def _to_dest_major(g, shape):
    full = g.reshape(shape[:-1] + (N_DEV, shape[-1]))
    return jnp.moveaxis(full, -2, 0).reshape(N_DEV, -1)


def kernel(x, mem, ab_norm, ab_w_in, a_conv_w, a_conv_b, a_gate_x_w, a_gate_x_b, a_gate_a_w, a_gate_a_b, a_lambda, b_group_w, b_group_b, b_scale, ab_w_out, c_norm, c_w_pw1, c_b_pw1, c_dw_w, c_dw_b, c_ln_g, c_ln_b, c_w_pw2, c_b_pw2, xa_norm, xa_mem_norm, xa_wq, xa_wk, xa_wv, xa_wo, f_norm, f_w_up, f_dw_w, f_dw_b, f_w_down, final_norm, loss_target, m_ab_norm, m_ab_w_in, m_a_conv_w, m_a_conv_b, m_a_gate_x_w, m_a_gate_x_b, m_a_gate_a_w, m_a_gate_a_b, m_a_lambda, m_b_group_w, m_b_group_b, m_b_scale, m_ab_w_out, m_c_norm, m_c_w_pw1, m_c_b_pw1, m_c_dw_w, m_c_dw_b, m_c_ln_g, m_c_ln_b, m_c_w_pw2, m_c_b_pw2, m_xa_norm, m_xa_mem_norm, m_xa_wq, m_xa_wk, m_xa_wv, m_xa_wo, m_f_norm, m_f_w_up, m_f_dw_w, m_f_dw_b, m_f_w_down, m_final_norm, v_ab_norm, v_ab_w_in, v_a_conv_w, v_a_conv_b, v_a_gate_x_w, v_a_gate_x_b, v_a_gate_a_w, v_a_gate_a_b, v_a_lambda, v_b_group_w, v_b_group_b, v_b_scale, v_ab_w_out, v_c_norm, v_c_w_pw1, v_c_b_pw1, v_c_dw_w, v_c_dw_b, v_c_ln_g, v_c_ln_b, v_c_w_pw2, v_c_b_pw2, v_xa_norm, v_xa_mem_norm, v_xa_wq, v_xa_wk, v_xa_wv, v_xa_wo, v_f_norm, v_f_w_up, v_f_dw_w, v_f_dw_b, v_f_w_down, v_final_norm):
    args = dict(locals())
    P = {n: args[n] for n in _NAMES}
    M = {n: args["m_" + n] for n in _NAMES}
    V = {n: args["v_" + n] for n in _NAMES}

    me = 4 * lax.axis_index("x") + 2 * lax.axis_index("y") + lax.axis_index("c")

    in_flight = {}

    def launch(groups, tok):
        shards, n_of = [], {}
        for grp in groups:
            for name, l in _GROUPS[grp]:
                w = P[name][l] if tok is None else P[name][l] + tok
                shards.append(w.astype(BF16))
            if grp == ("ab", 0):
                shards.append(_pack([P[n] for n in _SMALL_SHARDED], _SS_ROWS + 4))
            n_of[grp] = len(shards)
        res = _send_start(shards, False, "gather_start_" + "_".join(g[0] + str(g[1]) for g in groups))
        lo = 0
        for grp in groups:
            in_flight[grp] = [r[lo:n_of[grp]] for r in res[:4]]
            lo = n_of[grp]
        return res[4][:1, :1]

    follow = {("ab", 0): [("ab", 1), ("xa", 0), ("f", 0), ("fd", 0)], ("xa", 0): [("c", 0), ("xa", 1)],
              ("f", 0): [("f", 1), ("fd", 1)]}

    def fetch(grp, after):
        send_s, recv_s, srcs, lands = in_flight.pop(grp)
        srcs, lands, tok = _send_wait(send_s, recv_s, srcs, lands, after, False, f"gather_wait_{grp[0]}{grp[1]}")
        tok = launch(follow[grp], tok[:1, :1]) if grp in follow else None
        full = [_with_own(land, src, me, False) for land, src in zip(lands, srcs)]
        out = {}
        for (name, l), g in zip(_GROUPS[grp], full):
            w = _weight_layout(name, g)
            if _BIG[name][0] == 1:
                out[name] = w
            else:
                out[name] = {l: w}
        if grp == ("ab", 0):
            out.update(_small_layouts(full[-1]))
        return out, tok

    zero = launch([("ab", 0)], None)

    pending, held = [], []
    rides_with_next = {("xa", 1), ("f", 0)}

    def send(grp, G):
        held.extend(_SEND_GROUPS[grp])
        if grp in rides_with_next:
            return None
        members = tuple(held)
        del held[:]
        res = _send_start([_grad_blocks(name, l, G) for name, l in members], True, f"send_{grp[0]}{grp[1]}")
        pending.append((members, res))
        return res[4][:1, :1]

    W = {n: P[n] for n in _REPL}
    W["ab_norm"] = P["ab_norm"] + zero
    W["final_norm"] = P["final_norm"].reshape(1, D)
    W["a_gate_x_w"], W["a_gate_a_w"], W["b_group_w"] = P["a_gate_x_w"][0], P["a_gate_a_w"][0], P["b_group_w"][0]
    loss, grad_x, G = _local_step(x[0], mem[0], loss_target[0], W, fetch, send)
    loss = lax.psum(loss[0, 0], ("x", "y", "c"))

    Gs = dict(G)
    Gs["c_b_pw1"] = _unpair_blocks(G["c_b_pw1"], _CW_C)
    Gs["f_dw_w"] = jnp.stack([G["f_dw_w0"], G["f_dw_w1"]])
    Gs["a_conv_w"], Gs["c_dw_w"] = G["a_conv_w"][None], G["c_dw_w"][None]
    for n in ("xa_norm", "xa_mem_norm", "f_norm", "f_dw_b"):
        Gs[n] = jnp.concatenate([G[f"{n}0"], G[f"{n}1"]], axis=0)
    for n in ("a_gate_x_w", "a_gate_a_w", "b_group_w"):
        Gs[n] = G[n][None]
    repl_flat = jnp.concatenate([Gs[n].reshape(-1) for n in _REPL])
    repl_rows = jnp.pad(repl_flat, (0, N_DEV * _REPL_ROWS * LANE - _N_REPL)).reshape(N_DEV, _REPL_ROWS, LANE)
    ss_rows = jnp.concatenate([_to_dest_major(Gs[n], s) for n, s in _SMALL_SHARDED.items()], axis=1)
    ss_rows = ss_rows.reshape(N_DEV, _SS_ROWS, LANE)
    small_pack = jnp.concatenate(
        [repl_rows, ss_rows, jnp.zeros((N_DEV, _SMALL_ROWS - _REPL_ROWS - _SS_ROWS, LANE), F32)], axis=1)
    last = _send_start([small_pack], True, "send_small")
    pending.append(((("small", 0),), last))

    def arrived(some, after, name):
        members = [m for mem_, _ in some for m in mem_]
        cat = [[a for _, res in some for a in res[i]] for i in range(4)]
        srcs, lands, _ = _send_wait(cat[0], cat[1], cat[2], cat[3], after, True, name)
        return {m: _with_own(land, src, me, True) for m, land, src in zip(members, lands, srcs)}

    out_g, out_d, out_m, out_v = {}, {}, {}, {}
    chain = [None]

    def update(name, landed):
        layers, r, c = _BIG[name]
        w2, m2, v2 = [t[name].reshape(layers * r, c) for t in (P, M, V)]
        res = None
        for l in range(layers):
            res = _sum_adamw(landed[(name, l)], w2, m2, v2, f"adamw_{name}{l}", layer=l, prev=res,
                             after=chain[0] if l == 0 else None)
        chain[0] = res[1]
        out_g[name], out_d[name], out_m[name], out_v[name] = [t.reshape(P[name].shape) for t in res]

    landed = arrived(pending[:-2], grad_x, "send_wait_early")
    for name in _BIG:
        if name != "ab_w_in":
            update(name, landed)
    landed = arrived(pending[-2:], out_v["f_w_down"], "send_wait_late")
    update("ab_w_in", landed)

    small_sum = _sum8(landed[("small", 0)], "sum_small")
    (repl_all,) = _all_gather([small_sum[:_REPL_ROWS]], "gather_small_grads")
    for out, got in zip((out_g, out_d, out_m, out_v),
                        _adamw_small(repl_all.reshape(N_DEV * _REPL_ROWS, LANE), small_sum, P, M, V)):
        out.update(got)

    return (loss, grad_x[None], *[out_g[n] for n in _NAMES], *[out_d[n] for n in _NAMES],
            *[out_m[n] for n in _NAMES], *[out_v[n] for n in _NAMES])


_NAMES = ("ab_norm", "ab_w_in", "a_conv_w", "a_conv_b", "a_gate_x_w", "a_gate_x_b", "a_gate_a_w", "a_gate_a_b",
          "a_lambda", "b_group_w", "b_group_b", "b_scale", "ab_w_out", "c_norm", "c_w_pw1", "c_b_pw1", "c_dw_w",
          "c_dw_b", "c_ln_g", "c_ln_b", "c_w_pw2", "c_b_pw2", "xa_norm", "xa_mem_norm", "xa_wq", "xa_wk", "xa_wv",
          "xa_wo", "f_norm", "f_w_up", "f_dw_w", "f_dw_b", "f_w_down", "final_norm")
```

```python
import functools

import jax
import jax.numpy as jnp
from jax import lax
from jax.experimental import pallas as pl
from jax.experimental.pallas import tpu as pltpu

F32, BF16 = jnp.float32, jnp.bfloat16
SDS = jax.ShapeDtypeStruct
MESH = pl.DeviceIdType.MESH

N_DEV = 8
D = 1024
N_MEM = 256
XA_HEADS, XA_HD = 4, 256
HD_A = 128
CONV_A, CONV_C, CONV_F = 4, 31, 3
C_RG = 8.0
POOL_WINDOWS = (2, 4, 8, 16)
D_FF = 3 * D
EPS = 1e-6
ADAM_LR, ADAM_B1, ADAM_B2, ADAM_EPS, ADAM_WD, ADAM_STEP = 0.001, 0.9, 0.999, 1e-08, 0.01, 10

LANE = 128
SUB = 8
VMEM_LIMIT = 56 * 1024 * 1024
R_SEQ = 1024
R_POOL = 2048
R_RGLRU = 2048
R_FFN = 1024
TM_ROW = 512


def _cp(n_axes):
    return pltpu.CompilerParams(dimension_semantics=("arbitrary",) * n_axes, vmem_limit_bytes=VMEM_LIMIT)


def _tile(n, pref):
    if n <= pref:
        return n
    best = None
    for t in range(LANE, pref + 1, LANE):
        if n % t == 0:
            best = t
    assert best is not None, (n, pref)
    return best


def _perm2(n):
    return (n % 2) * 4 + n // 2


_NN = (((1,), (0,)), ((), ()))
_NT = (((1,), (1,)), ((), ()))
_TN = (((0,), (0,)), ((), ()))


def _mm_call(name, grid, ab, ab_specs, dims, acc_shape, extras, outs, finish, from_ref=False):
    nk = grid[2]
    n_ab, n_ex, n_out = len(ab), len(extras), len(outs)
    use_acc = nk > 1 or from_ref

    def product(refs):
        r = lax.dot_general(refs[0][...], refs[1][...], dims, preferred_element_type=F32)
        for i in range(1, n_ab):
            r = r + lax.dot_general(refs[2 * i][...], refs[2 * i + 1][...], dims, preferred_element_type=F32)
        return r

    def body(*refs):
        rest = refs[2 * n_ab:]
        ex_refs, o_refs = rest[:n_ex], rest[n_ex:n_ex + n_out]
        first_rows = pl.program_id(0) == 0
        if not use_acc:
            finish(product(refs), ex_refs, o_refs, first_rows)
            return
        acc = rest[n_ex + n_out]
        if nk == 1:
            acc[...] = product(refs)
            finish(acc, ex_refs, o_refs, first_rows)
            return
        k = pl.program_id(2)

        @pl.when(k == 0)
        def _():
            acc[...] = jnp.zeros_like(acc)

        acc[...] += product(refs)

        @pl.when(k == nk - 1)
        def _():
            finish(acc if from_ref else acc[...], ex_refs, o_refs, first_rows)

    res = pl.pallas_call(
        body, out_shape=[o for o, _ in outs], grid=grid,
        in_specs=list(ab_specs) + [s for _, s in extras], out_specs=[s for _, s in outs],
        scratch_shapes=[pltpu.VMEM(acc_shape, F32)] if use_acc else [], name=name, compiler_params=_cp(3),
    )(*[t for pair in ab for t in pair], *[e for e, _ in extras])
    return res[0] if n_out == 1 else res


def _finish_sum(r, ex_refs, o_refs, first_rows):
    del first_rows
    for e in ex_refs:
        r = r + e[...]
    o_refs[0][...] = r.astype(o_refs[0].dtype)


def _finish_sum_norm(r, ex_refs, o_refs, first_rows):
    del first_rows
    for e in ex_refs[:-1]:
        r = r + e[...]
    o_refs[0][...] = r
    o_refs[1][...] = ((r * lax.rsqrt(jnp.mean(r * r, axis=-1, keepdims=True) + EPS)) * ex_refs[-1][...]).astype(BF16)


_EPI_ROWS = 16


def _finish_rms_bwd(r_ref, ex_refs, o_refs, first_rows):
    x_ref, g_ref, dres_ref = ex_refs
    dx_ref, dxb_ref, dg_ref = o_refs

    @pl.when(first_rows)
    def _():
        dg_ref[...] = jnp.zeros_like(dg_ref)

    gv = g_ref[...]
    inv_d = 1.0 / r_ref.shape[1]

    def step(i, dg_acc):
        groups = [pl.ds(pl.multiple_of(i * (2 * _EPI_ROWS) + u * _EPI_ROWS, _EPI_ROWS), _EPI_ROWS) for u in range(2)]
        sums = []
        for rows in groups:
            r, xf = r_ref[rows, :], x_ref[rows, :]
            sums.append((jnp.sum(xf * xf, axis=-1, keepdims=True), jnp.sum((r * gv) * xf, axis=-1, keepdims=True)))
        for rows, (sxx, sax) in zip(groups, sums):
            r, xf = r_ref[rows, :], x_ref[rows, :]
            rs = lax.rsqrt(sxx * inv_d + EPS)
            dg_acc = dg_acc + _psum8(r * (xf * rs))
            dx = rs * (r * gv) - xf * (rs * rs * (sax * rs * inv_d)) + dres_ref[rows, :]
            dx_ref[rows, :] = dx
            dxb_ref[rows, :] = dx.astype(BF16)
        return dg_acc

    dg_acc = lax.fori_loop(0, r_ref.shape[0] // (2 * _EPI_ROWS), step, jnp.zeros((SUB, r_ref.shape[1]), F32))
    dg_ref[...] += jnp.sum(dg_acc, axis=0, keepdims=True)


def _rms_bwd_io(M, tm, x, g, dres):
    rows = pl.BlockSpec((tm, D), lambda m, n, k: (m, 0))
    vec = pl.BlockSpec((1, D), lambda m, n, k: (0, 0))
    return ([(x, rows), (g, vec), (dres, rows)],
            [(SDS((M, D), F32), rows), (SDS((M, D), BF16), rows), (SDS((1, D), F32), vec)])


_K_WHOLE = 3072


def _mm_nn(a, b, *, out_dtype, name, bias=None, add=None, norm=None):
    M, K = a.shape
    tk = K if K <= _K_WHOLE else _tile(K, 1024)
    if K <= 1024 and norm is None:
        tm = _tile(M, 2048 if add is None and out_dtype == BF16 else 1024)
    else:
        tm = _tile(M, 512)
    if b.ndim == 3:
        nb, _, bw = b.shape
        N, tn, nn = nb * bw, bw, nb
        b_spec = pl.BlockSpec((None, tk, bw), lambda m, n, k: (_perm2(n), k, 0))
    else:
        N = b.shape[1]
        tn = _tile(N, 1024)
        nn = N // tn
        b_spec = pl.BlockSpec((tk, tn), lambda m, n, k: (k, n))
    tile = pl.BlockSpec((tm, tn), lambda m, n, k: (m, n))
    vec = pl.BlockSpec((1, tn), lambda m, n, k: (0, n))
    extras = ([] if bias is None else [(bias, vec)]) + ([] if add is None else [(add, tile)])
    outs, finish = [(SDS((M, N), out_dtype), tile)], _finish_sum
    if norm is not None:
        assert tn == N == D and out_dtype == F32
        extras.append((norm, vec))
        outs, finish = outs + [(SDS((M, N), BF16), tile)], _finish_sum_norm
    return _mm_call(name, (M // tm, nn, K // tk), [(a, b)], [pl.BlockSpec((tm, tk), lambda m, n, k: (m, k)), b_spec],
                    _NN, (tm, tn), extras, outs, finish)


def _mm_nt(a, b, *, out_dtype, name, add=None, rms=None):
    M, N = a.shape
    if b.ndim == 3:
        nb, Ko, bw = b.shape
        tm = _tile(M, 1024)
        tn, tk, nk = _tile(Ko, 1024), bw, nb
        b_spec = pl.BlockSpec((None, tn, bw), lambda m, n, k: (_perm2(k), n, 0))
    else:
        Ko = b.shape[0]
        tk = N if N <= _K_WHOLE else _tile(N, 1024)
        if N <= 1024 and rms is None:
            tm = _tile(M, 2048 if add is None and out_dtype == BF16 else 1024)
        else:
            tm = _tile(M, 512)
        tn = _tile(Ko, 1024)
        nk = N // tk
        b_spec = pl.BlockSpec((tn, tk), lambda m, n, k: (n, k))
    tile = pl.BlockSpec((tm, tn), lambda m, n, k: (m, n))
    extras = [] if add is None else [(add, tile)]
    outs, finish = [(SDS((M, Ko), out_dtype), tile)], _finish_sum
    if rms is not None:
        assert tn == Ko == D and add is None
        (extras, outs), finish = _rms_bwd_io(M, tm, *rms), _finish_rms_bwd
    return _mm_call(name, (M // tm, Ko // tn, nk), [(a, b)], [pl.BlockSpec((tm, tk), lambda m, n, k: (m, k)), b_spec],
                    _NT, (tm, tn), extras, outs, finish, from_ref=rms is not None)


def _mm_nt_cols(parts, b, *, name, rms):
    M = parts[0].shape[0]
    tm = _tile(M, 512)
    specs, off = [], 0
    for p in parts:
        w = p.shape[1]
        assert off % w == 0
        specs.append(pl.BlockSpec((tm, w), lambda m, n, k: (m, 0)))
        specs.append(pl.BlockSpec((D, w), functools.partial(lambda m, n, k, o: (0, o), o=off // w)))
        off += w
    extras, outs = _rms_bwd_io(M, tm, *rms)
    return _mm_call(name, (M // tm, 1, 1), [(p, b) for p in parts], specs, _NT, (tm, D), extras, outs, _finish_rms_bwd,
                    from_ref=True)


def _mm_tn(a, b, *, out_dtype, name, blocks=None):
    S, Ka = a.shape
    Nb = b.shape[1]
    tm = _tile(Ka, 1024)
    if blocks is not None:
        bw = blocks
        tn, nn = bw, Nb // bw
        out = (SDS((nn, Ka, bw), out_dtype), pl.BlockSpec((None, tm, bw), lambda m, n, k: (_perm2(n), m, 0)))
    else:
        tn = _tile(Nb, 1024)
        nn = Nb // tn
        out = (SDS((Ka, Nb), out_dtype), pl.BlockSpec((tm, tn), lambda m, n, k: (m, n)))
    steps = (Ka // tm) * nn
    tk = _tile(S, 4096 if steps >= 4 else 2048 if steps >= 2 else 1024)
    return _mm_call(name, (Ka // tm, nn, S // tk), [(a, b)],
                    [pl.BlockSpec((tk, tm), lambda m, n, k: (k, m)), pl.BlockSpec((tk, tn), lambda m, n, k: (k, n))],
                    _TN, (tm, tn), [], [out], _finish_sum)


def _row(tm, c):
    return pl.BlockSpec((tm, c), lambda i: (i, 0))


def _full(shape):
    nd = len(shape)
    return pl.BlockSpec(shape, lambda i: (0,) * nd)


def _rms_fwd(x, g, name):
    S = x.shape[0]
    tm = min(S, TM_ROW)

    def body(x_ref, g_ref, o_ref):
        xf = x_ref[...]
        r = lax.rsqrt(jnp.mean(xf * xf, axis=-1, keepdims=True) + EPS)
        o_ref[...] = ((xf * r) * g_ref[...]).astype(BF16)

    return pl.pallas_call(body, out_shape=SDS((S, D), BF16), grid=(S // tm,), in_specs=[_row(tm, D), _full((1, D))],
                          out_specs=_row(tm, D), name=name, compiler_params=_cp(1))(x, g)


def _rms_bwd(x, g, dn, dres, name):
    S = x.shape[0]
    tm = min(S, TM_ROW)
    want_dx = dres is not None

    def body(x_ref, g_ref, dn_ref, *rest):
        i = pl.program_id(0)
        dg_ref = rest[-1]

        @pl.when(i == 0)
        def _():
            dg_ref[...] = jnp.zeros_like(dg_ref)

        xf = x_ref[...]
        r = lax.rsqrt(jnp.mean(xf * xf, axis=-1, keepdims=True) + EPS)
        y = xf * r
        dn_v = dn_ref[...]
        dg_ref[...] += jnp.sum(dn_v * y, axis=0, keepdims=True)
        if want_dx:
            dres_ref, dx_ref, dxb_ref = rest[0], rest[1], rest[2]
            dy = dn_v * g_ref[...]
            dx = r * (dy - y * jnp.mean(dy * y, axis=-1, keepdims=True)) + dres_ref[...]
            dx_ref[...] = dx
            dxb_ref[...] = dx.astype(BF16)

    ins = [x, g, dn] + ([dres] if want_dx else [])
    in_specs = [_row(tm, D), _full((1, D)), _row(tm, D)] + ([_row(tm, D)] if want_dx else [])
    outs = ([SDS((S, D), F32), SDS((S, D), BF16)] if want_dx else []) + [SDS((1, D), F32)]
    out_specs = ([_row(tm, D), _row(tm, D)] if want_dx else []) + [_full((1, D))]
    return pl.pallas_call(body, out_shape=outs, grid=(S // tm,), in_specs=in_specs, out_specs=out_specs, name=name,
                          compiler_params=_cp(1))(*ins)


def _loss_head(x, g, tgt):
    S = x.shape[0]
    tm = min(S, TM_ROW)

    def body(x_ref, g_ref, t_ref, loss_ref, dx_ref, dxb_ref, dg_ref):
        i = pl.program_id(0)

        @pl.when(i == 0)
        def _():
            loss_ref[...] = jnp.zeros_like(loss_ref)
            dg_ref[...] = jnp.zeros_like(dg_ref)

        xf = x_ref[...]
        r = lax.rsqrt(jnp.mean(xf * xf, axis=-1, keepdims=True) + EPS)
        y = xf * r
        gv = g_ref[...]
        err = y * gv - t_ref[...]
        per_row = jnp.mean(err * err, axis=-1, keepdims=True)
        loss_ref[...] += 0.5 * jnp.sum(per_row, axis=0, keepdims=True)
        dn_v = err * (1.0 / D)
        dg_ref[...] += jnp.sum(dn_v * y, axis=0, keepdims=True)
        dy = dn_v * gv
        dx = r * (dy - y * jnp.mean(dy * y, axis=-1, keepdims=True))
        dx_ref[...] = dx
        dxb_ref[...] = dx.astype(BF16)

    return pl.pallas_call(
        body, out_shape=[SDS((1, 1), F32), SDS((S, D), F32), SDS((S, D), BF16), SDS((1, D), F32)], grid=(S // tm,),
        in_specs=[_row(tm, D), _full((1, D)), _row(tm, D)],
        out_specs=[_full((1, 1)), _row(tm, D), _row(tm, D), _full((1, D))], name="loss_head", compiler_params=_cp(1),
    )(x, g, tgt)


def _softmax_rows(s):
    m = jnp.max(s, axis=-1, keepdims=True)
    e = jnp.exp(s - m)
    return e / jnp.sum(e, axis=-1, keepdims=True)


def _attn_fwd(q, k, v, name):
    S = q.shape[0]
    tm = min(S, TM_ROW)
    scale = XA_HD ** -0.5

    def body(q_ref, k_ref, v_ref, o_ref):
        for h in range(XA_HEADS):
            sl = slice(h * XA_HD, (h + 1) * XA_HD)
            s = lax.dot_general(q_ref[:, sl], k_ref[:, sl], _NT, preferred_element_type=F32) * scale
            p = _softmax_rows(s)
            o_ref[:, sl] = lax.dot_general(p.astype(BF16), v_ref[:, sl], _NN, preferred_element_type=F32).astype(BF16)

    return pl.pallas_call(body, out_shape=SDS((S, D), BF16), grid=(S // tm,),
                          in_specs=[_row(tm, D), _full((N_MEM, D)), _full((N_MEM, D))], out_specs=_row(tm, D),
                          name=name, compiler_params=_cp(1))(q, k, v)


def _attn_bwd(q, k, v, do, name):
    S = q.shape[0]
    tm = min(S, TM_ROW)
    scale = XA_HD ** -0.5

    def body(q_ref, k_ref, v_ref, do_ref, dq_ref, dk_ref, dv_ref):
        i = pl.program_id(0)

        @pl.when(i == 0)
        def _():
            dk_ref[...] = jnp.zeros_like(dk_ref)
            dv_ref[...] = jnp.zeros_like(dv_ref)

        for h in range(XA_HEADS):
            sl = slice(h * XA_HD, (h + 1) * XA_HD)
            qh, kh, vh, doh = q_ref[:, sl], k_ref[:, sl], v_ref[:, sl], do_ref[:, sl]
            s = lax.dot_general(qh, kh, _NT, preferred_element_type=F32) * scale
            p = _softmax_rows(s)
            pb = p.astype(BF16)
            dv_ref[:, sl] += lax.dot_general(pb, doh, _TN, preferred_element_type=F32)
            dp = lax.dot_general(doh, vh, _NT, preferred_element_type=F32)
            ds = (p * (dp - jnp.sum(dp * p, axis=-1, keepdims=True)) * scale).astype(BF16)
            dq_ref[:, sl] = lax.dot_general(ds, kh, _NN, preferred_element_type=F32).astype(BF16)
            dk_ref[:, sl] += lax.dot_general(ds, qh, _TN, preferred_element_type=F32)

    return pl.pallas_call(
        body, out_shape=[SDS((S, D), BF16), SDS((N_MEM, D), F32), SDS((N_MEM, D), F32)], grid=(S // tm,),
        in_specs=[_row(tm, D), _full((N_MEM, D)), _full((N_MEM, D)), _row(tm, D)],
        out_specs=[_row(tm, D), _full((N_MEM, D)), _full((N_MEM, D))], name=name, compiler_params=_cp(1),
    )(q, k, v, do)


def _sigmoid(x):
    return 1.0 / (1.0 + jnp.exp(-x))


def _ln_silu_fwd(cv, g, b):
    S = cv.shape[0]
    tm = min(S, TM_ROW)

    def body(x_ref, g_ref, b_ref, o_ref):
        xf = x_ref[...]
        mu = jnp.mean(xf, axis=-1, keepdims=True)
        xc = xf - mu
        rstd = lax.rsqrt(jnp.mean(xc * xc, axis=-1, keepdims=True) + EPS)
        ln = (xc * rstd) * g_ref[...] + b_ref[...]
        o_ref[...] = (ln * _sigmoid(ln)).astype(BF16)

    return pl.pallas_call(body, out_shape=SDS((S, D), BF16), grid=(S // tm,),
                          in_specs=[_row(tm, D), _full((1, D)), _full((1, D))], out_specs=_row(tm, D),
                          name="ln_silu_fwd", compiler_params=_cp(1))(cv, g, b)


def _ln_silu_bwd(ds, cv, g, b, dx):
    S = cv.shape[0]
    tm = min(S, TM_ROW)

    def body(ds_ref, x_ref, g_ref, b_ref, dx_ref, dcv_ref, dg_ref, db_ref, db2_ref):
        i = pl.program_id(0)

        @pl.when(i == 0)
        def _():
            dg_ref[...] = jnp.zeros_like(dg_ref)
            db_ref[...] = jnp.zeros_like(db_ref)
            db2_ref[...] = jnp.zeros_like(db2_ref)

        xf = x_ref[...]
        mu = jnp.mean(xf, axis=-1, keepdims=True)
        xc = xf - mu
        rstd = lax.rsqrt(jnp.mean(xc * xc, axis=-1, keepdims=True) + EPS)
        xhat = xc * rstd
        gv = g_ref[...]
        ln = xhat * gv + b_ref[...]
        sg = _sigmoid(ln)
        dln = ds_ref[...].astype(F32) * (sg + ln * sg * (1.0 - sg))
        dg_ref[...] += jnp.sum(dln * xhat, axis=0, keepdims=True)
        db_ref[...] += jnp.sum(dln, axis=0, keepdims=True)
        db2_ref[...] += jnp.sum(dx_ref[...], axis=0, keepdims=True)
        dxh = dln * gv
        dcv_ref[...] = rstd * (dxh - jnp.mean(dxh, axis=-1, keepdims=True)
                               - xhat * jnp.mean(dxh * xhat, axis=-1, keepdims=True))

    return pl.pallas_call(
        body, out_shape=[SDS((S, D), F32), SDS((1, D), F32), SDS((1, D), F32), SDS((1, D), F32)], grid=(S // tm,),
        in_specs=[_row(tm, D), _row(tm, D), _full((1, D)), _full((1, D)), _row(tm, D)],
        out_specs=[_row(tm, D), _full((1, D)), _full((1, D)), _full((1, D))], name="ln_silu_bwd",
        compiler_params=_cp(1),
    )(ds, cv, g, b, dx)


_GELU_C, _GELU_K = 0.7978845608028654, 0.044715


def _gelu(x, with_grad=False):
    x2 = x * x
    t = jnp.tanh(_GELU_C * (x + _GELU_K * x * x2))
    gel = 0.5 * x * (1.0 + t)
    if not with_grad:
        return gel
    return gel, 0.5 * (1.0 + t) + 0.5 * x * (1.0 - t * t) * (_GELU_C * (1.0 + 3.0 * _GELU_K * x2))


def _expm1(x):
    poly = x * (1.0 + x * (0.5 + x * (1.0 / 6.0 + x * (1.0 / 24.0 + x * (1.0 / 120.0)))))
    return jnp.where(jnp.abs(x) < 0.05, poly, jnp.exp(x) - 1.0)


def _softplus(x):
    return jnp.maximum(x, 0.0) + jnp.log1p(jnp.exp(-jnp.abs(x)))


_SCAN_UNROLL = 8
_RB = 32
_HB = 16


def _sub_blocks(n_rows, n_lanes, fn):
    def step(idx, c):
        r0 = pl.multiple_of(idx * _RB, _RB)
        for lt in range(n_lanes // LANE):
            fn(r0, lt)
        return c

    lax.fori_loop(0, n_rows // _RB, step, 0)


def _lanes(lt):
    return pl.ds(lt * LANE, LANE)


def _psum8(x):
    parts = [x[i * SUB:(i + 1) * SUB] for i in range(x.shape[0] // SUB)]
    return functools.reduce(lambda p, q: p + q, parts)


def _scan_fwd(a_s, b_s, out_ref, carry_ref, n_groups):
    row = lax.broadcasted_iota(jnp.int32, (SUB, LANE), 0)
    U = _SCAN_UNROLL

    def step(gi, carry):
        base = gi * (SUB * U)
        parts = []
        for u in range(U):
            i = pl.multiple_of(base + u * SUB, SUB)
            a8, b8 = a_s[pl.ds(i, SUB), :], b_s[pl.ds(i, SUB), :]
            for s in (1, 2, 4):
                a_sh = jnp.where(row >= s, pltpu.roll(a8, s, 0), 1.0)
                b_sh = jnp.where(row >= s, pltpu.roll(b8, s, 0), 0.0)
                b8 = a8 * b_sh + b8
                a8 = a8 * a_sh
            parts.append((i, a8, b8))
        for i, a8, b8 in parts:
            h8 = a8 * carry + b8
            out_ref[pl.ds(i, SUB), :] = h8
            carry = jnp.broadcast_to(h8[SUB - 1:SUB, :], (SUB, LANE))
        return carry

    carry_ref[...] = lax.fori_loop(0, n_groups // U, step, carry_ref[...])


def _scan_bwd(a_s, b_s, out_ref, carry_ref, n_groups):
    row = lax.broadcasted_iota(jnp.int32, (SUB, LANE), 0)
    U = _SCAN_UNROLL

    def step(gi, carry):
        base = (n_groups // U - 1 - gi) * (SUB * U)
        parts = []
        for u in reversed(range(U)):
            i = pl.multiple_of(base + u * SUB, SUB)
            a8, b8 = a_s[pl.ds(i, SUB), :], b_s[pl.ds(i, SUB), :]
            for s in (1, 2, 4):
                a_sh = jnp.where(row < SUB - s, pltpu.roll(a8, SUB - s, 0), 1.0)
                b_sh = jnp.where(row < SUB - s, pltpu.roll(b8, SUB - s, 0), 0.0)
                b8 = a8 * b_sh + b8
                a8 = a8 * a_sh
            parts.append((i, a8, b8))
        for i, a8, b8 in parts:
            h8 = a8 * carry + b8
            out_ref[pl.ds(i, SUB), :] = h8
            carry = jnp.broadcast_to(h8[0:1, :], (SUB, LANE))
        return carry

    carry_ref[...] = lax.fori_loop(0, n_groups // U, step, carry_ref[...])


def _rglru_pre(xr, wgx_ref, bgx_ref, wga_ref, bga_ref, lam_ref):
    xrb = xr.astype(BF16)
    wgx, wga = wgx_ref[0].astype(BF16), wga_ref[0].astype(BF16)
    gx = _sigmoid(lax.dot_general(xrb, wgx, _NN, preferred_element_type=F32) + bgx_ref[...])
    ga = _sigmoid(lax.dot_general(xrb, wga, _NN, preferred_element_type=F32) + bga_ref[...])
    sp = _softplus(-lam_ref[...])
    log_a = -C_RG * ga * sp
    a = jnp.exp(log_a)
    mult = jnp.sqrt(-_expm1(2.0 * log_a))
    return gx, ga, sp, a, mult, xrb, wgx, wga


def _a_specs():
    vec = pl.BlockSpec((1, HD_A), lambda c, j: (0, c))
    mat = pl.BlockSpec((1, HD_A, HD_A), lambda c, j: (c, 0, 0))
    return [pl.BlockSpec((CONV_A, HD_A), lambda c, j: (0, c)), vec, mat, vec, mat, vec, vec]


def _a_fwd(zp, conv_w, conv_b, wgx, bgx, wga, bga, lam):
    S = zp.shape[0]
    R, nt = R_RGLRU, D // HD_A
    H = SUB

    def body(zg_ref, zr_ref, cw_ref, cb_ref, wgx_ref, bgx_ref, wga_ref, bga_ref, lam_ref, ya_ref, h_ref,
             ext, a_s, b_s, hc):
        j = pl.program_id(1)

        @pl.when(j == 0)
        def _():
            ext[0:H, :] = jnp.zeros((H, HD_A), F32)
            hc[...] = jnp.zeros_like(hc)

        ext[H:H + R, :] = zr_ref[...].astype(F32)
        xr = cb_ref[...]
        for k in range(CONV_A):
            xr = xr + cw_ref[k:k + 1, :] * ext[pl.ds(H - (CONV_A - 1 - k), R), :]
        gx, _, _, a, mult, _, _, _ = _rglru_pre(xr, wgx_ref, bgx_ref, wga_ref, bga_ref, lam_ref)
        a_s[...] = a
        b_s[...] = mult * (gx * xr)
        _scan_fwd(a_s, b_s, h_ref, hc, R // SUB)
        ya_ref[...] = (_gelu(zg_ref[...].astype(F32)) * h_ref[...]).astype(BF16)
        ext[0:H, :] = ext[R:R + H, :]

    return pl.pallas_call(
        body, out_shape=[SDS((S, D + D // 2), BF16), SDS((S, D), F32)], grid=(nt, S // R),
        in_specs=[pl.BlockSpec((R, HD_A), lambda c, j: (j, c)), pl.BlockSpec((R, HD_A), lambda c, j: (j, nt + c))]
        + _a_specs(),
        out_specs=[pl.BlockSpec((R, HD_A), lambda c, j: (j, c)), pl.BlockSpec((R, HD_A), lambda c, j: (j, c))],
        scratch_shapes=[pltpu.VMEM((H + R, HD_A), F32), pltpu.VMEM((R, HD_A), F32), pltpu.VMEM((R, HD_A), F32),
                        pltpu.VMEM((SUB, HD_A), F32)],
        name="rglru_fwd", compiler_params=_cp(2),
    )(zp, zp, conv_w, conv_b, wgx, bgx, wga, bga, lam)


def _a_bwd(dyab, zp, h, conv_w, conv_b, wgx, bgx, wga, bga, lam):
    S = zp.shape[0]
    R, nt, nch = R_RGLRU, D // HD_A, S // R_RGLRU
    H = SUB

    def rows(c, j):
        return (nch - 1 - j, c)

    def rows_rec(c, j):
        return (nch - 1 - j, nt + c)

    def halo(c, j):
        return (jnp.maximum((nch - 1 - j) * (R // H) - 1, 0), c)

    def halo_z(c, j):
        return (jnp.maximum((nch - 1 - j) * (R // _HB) - 1, 0), nt + c)

    def body(dy_ref, zg_ref, zr_ref, zh_ref, h_ref, hh_ref, cw_ref, cb_ref, wgx_ref, bgx_ref, wga_ref, bga_ref,
             lam_ref, dzg_ref, dzr_ref, dcw_ref, dcb_ref, dwgx_ref, dbgx_ref, dwga_ref, dbga_ref, dlam_ref,
             ext_z, ext_h, ext_mu, ext_d, a_s, b_s, muc):
        j = pl.program_id(1)
        first_chunk = (nch - 1 - j) == 0

        @pl.when(j == 0)
        def _():
            ext_mu[R:R + H, :] = jnp.zeros((H, HD_A), F32)
            ext_d[R:R + H, :] = jnp.zeros((H, HD_A), F32)
            muc[...] = jnp.zeros_like(muc)
            for r in (dcw_ref, dcb_ref, dwgx_ref, dbgx_ref, dwga_ref, dbga_ref, dlam_ref):
                r[...] = jnp.zeros_like(r)

        zg = zg_ref[...].astype(F32)
        ext_z[0:H, :] = jnp.where(first_chunk, 0.0, zh_ref[_HB - H:_HB, :].astype(F32))
        ext_z[H:H + R, :] = zr_ref[...].astype(F32)
        ext_h[0:H, :] = jnp.where(first_chunk, 0.0, hh_ref[...])
        ext_h[H:H + R, :] = h_ref[...]
        xr = cb_ref[...]
        for k in range(CONV_A):
            xr = xr + cw_ref[k:k + 1, :] * ext_z[pl.ds(H - (CONV_A - 1 - k), R), :]
        gx, ga, sp, a, mult, xrb, wgxb, wgab = _rglru_pre(xr, wgx_ref, bgx_ref, wga_ref, bga_ref, lam_ref)
        gel, dgel = _gelu(zg, with_grad=True)
        dy = dy_ref[...].astype(F32)
        dh = dy * gel
        dzg_ref[...] = (dy * h_ref[...] * dgel).astype(BF16)
        a_s[...] = a
        b_s[...] = a * dh
        _scan_bwd(a_s, b_s, ext_mu, muc, R // SUB)
        lam_t = dh + ext_mu[pl.ds(1, R), :]
        ext_mu[R:R + H, :] = ext_mu[0:H, :]
        da = lam_t * ext_h[pl.ds(H - 1, R), :]
        gxr = gx * xr
        dlog_a = da * a - (lam_t * gxr) * (a * a) / mult
        dgx = lam_t * mult * xr
        dxr = lam_t * mult * gx
        lam_v = lam_ref[...]
        dlam_ref[...] += jnp.sum(dlog_a * ga, axis=0, keepdims=True) * (C_RG * _sigmoid(-lam_v))
        dpa = (dlog_a * (-C_RG * sp)) * ga * (1.0 - ga)
        dpx = dgx * gx * (1.0 - gx)
        dbga_ref[...] += jnp.sum(dpa, axis=0, keepdims=True)
        dbgx_ref[...] += jnp.sum(dpx, axis=0, keepdims=True)
        dpab, dpxb = dpa.astype(BF16), dpx.astype(BF16)
        dwga_ref[0] += lax.dot_general(xrb, dpab, _TN, preferred_element_type=F32)
        dwgx_ref[0] += lax.dot_general(xrb, dpxb, _TN, preferred_element_type=F32)
        dxr = (dxr + lax.dot_general(dpab, wgab, _NT, preferred_element_type=F32)
               + lax.dot_general(dpxb, wgxb, _NT, preferred_element_type=F32))
        dcb_ref[...] += jnp.sum(dxr, axis=0, keepdims=True)
        ext_d[0:R, :] = dxr
        dzr = jnp.zeros((R, HD_A), F32)
        for k in range(CONV_A):
            sh = CONV_A - 1 - k
            dcw_ref[k:k + 1, :] += jnp.sum(dxr * ext_z[pl.ds(H - sh, R), :], axis=0, keepdims=True)
            dzr = dzr + cw_ref[k:k + 1, :] * ext_d[pl.ds(sh, R), :]
        dzr_ref[...] = dzr.astype(BF16)
        ext_d[R:R + H, :] = ext_d[0:H, :]

    vec_o = pl.BlockSpec((1, HD_A), lambda c, j: (0, c))
    mat_o = pl.BlockSpec((1, HD_A, HD_A), lambda c, j: (c, 0, 0))
    return pl.pallas_call(
        body,
        out_shape=[SDS((S, D), BF16), SDS((S, D), BF16), SDS((CONV_A, D), F32), SDS((1, D), F32),
                   SDS((nt, HD_A, HD_A), F32), SDS((1, D), F32), SDS((nt, HD_A, HD_A), F32), SDS((1, D), F32),
                   SDS((1, D), F32)],
        grid=(nt, nch),
        in_specs=[pl.BlockSpec((R, HD_A), rows), pl.BlockSpec((R, HD_A), rows), pl.BlockSpec((R, HD_A), rows_rec),
                  pl.BlockSpec((_HB, HD_A), halo_z), pl.BlockSpec((R, HD_A), rows),
                  pl.BlockSpec((H, HD_A), halo)] + _a_specs(),
        out_specs=[pl.BlockSpec((R, HD_A), rows), pl.BlockSpec((R, HD_A), rows),
                   pl.BlockSpec((CONV_A, HD_A), lambda c, j: (0, c)), vec_o, mat_o, vec_o, mat_o, vec_o, vec_o],
        scratch_shapes=[pltpu.VMEM((H + R, HD_A), F32), pltpu.VMEM((H + R, HD_A), F32), pltpu.VMEM((R + H, HD_A), F32),
                        pltpu.VMEM((R + H, HD_A), F32), pltpu.VMEM((R, HD_A), F32), pltpu.VMEM((R, HD_A), F32),
                        pltpu.VMEM((SUB, HD_A), F32)],
        name="rglru_bwd", compiler_params=_cp(2),
    )(dyab, zp, zp, zp, h, h, conv_w, conv_b, wgx, bgx, wga, bga, lam)


_POOL_H = 16
_POOL_T0 = 2 * D // HD_A
_POOL_Y0 = D // HD_A


def _window_sum(lv, n, lo, rows, g, ahead):
    base = 0 if ahead else SUB
    cur, win = lv[0], None
    for i, s in enumerate((1, 2, 4, 8)):
        val = cur[pl.ds(base, n), :] + cur[pl.ds(base + (s if ahead else -s), n), :]
        sel = val[lo:lo + rows]
        win = sel if win is None else jnp.where(g >= i, sel, win)
        if i < 3:
            lv[i + 1][pl.ds(base, n), :] = val
            cur = lv[i + 1]
    return win


def _pool_width(g):
    return jnp.where(g == 0, 2.0, jnp.where(g == 1, 4.0, jnp.where(g == 2, 8.0, 16.0)))


def _b_fwd(zp, yab, wg, bg, sc):
    S = zp.shape[0]
    R, H = min(S, R_POOL), _POOL_H

    def body(z_ref, wg_ref, bg_ref, sc_ref, yab_in, yb_ref, *lv):
        del yab_in
        g, j = pl.program_id(0), pl.program_id(1)

        @pl.when(j == 0)
        def _():
            for r in lv:
                r[0:SUB, :] = jnp.zeros((SUB, HD_A), F32)
            lv[0][SUB:SUB + H, :] = jnp.zeros((H, HD_A), F32)

        u = z_ref[...].astype(F32)
        lv[0][SUB + H:SUB + H + R, :] = u
        t1 = (j * R + 1 + lax.broadcasted_iota(jnp.int32, (R, HD_A), 0)).astype(F32)
        p = _window_sum(lv, H + R, H, R, g, False) / jnp.minimum(t1, _pool_width(g)) - u
        lin = lax.dot_general(p.astype(BF16), wg_ref[0].astype(BF16), _NN, preferred_element_type=F32) + bg_ref[...]
        yb_ref[...] = (lin * sc_ref[...]).astype(BF16)
        lv[0][SUB:SUB + H, :] = lv[0][SUB + R:SUB + R + H, :]

    vec = pl.BlockSpec((1, HD_A), lambda g, j: (0, g))
    return pl.pallas_call(
        body, out_shape=SDS(yab.shape, yab.dtype), grid=(len(POOL_WINDOWS), S // R),
        in_specs=[pl.BlockSpec((R, HD_A), lambda g, j: (j, _POOL_T0 + g)),
                  pl.BlockSpec((1, HD_A, HD_A), lambda g, j: (g, 0, 0)), vec, vec, pl.BlockSpec(memory_space=pl.ANY)],
        out_specs=pl.BlockSpec((R, HD_A), lambda g, j: (j, _POOL_Y0 + g)),
        scratch_shapes=[pltpu.VMEM((SUB + H + R, HD_A), F32)] * 4, input_output_aliases={4: 0},
        name="pool_fwd", compiler_params=_cp(2),
    )(zp, wg, bg, sc, yab)


def _b_bwd(dyab, zp, wg, bg, sc):
    S = zp.shape[0]
    R, H, ng = min(S, R_POOL), _POOL_H, len(POOL_WINDOWS)
    nch = S // R

    def body(dy_ref, z_ref, zh_ref, wg_ref, bg_ref, sc_ref, dz_ref, dwg_ref, dbg_ref, dsc_ref, *scratch):
        lu, lq = scratch[:4], scratch[4:]
        g, j = pl.program_id(0), pl.program_id(1)
        jj = nch - 1 - j

        @pl.when(j == 0)
        def _():
            for r in lu:
                r[0:SUB, :] = jnp.zeros((SUB, HD_A), F32)
            for r in lq:
                r[R + H:R + H + SUB, :] = jnp.zeros((SUB, HD_A), F32)
            lq[0][R:R + H, :] = jnp.zeros((H, HD_A), F32)
            for r in (dwg_ref, dbg_ref, dsc_ref):
                r[...] = jnp.zeros_like(r)

        u = z_ref[...].astype(F32)
        lu[0][SUB:SUB + H, :] = jnp.where(jj == 0, 0.0, zh_ref[...].astype(F32))
        lu[0][SUB + H:SUB + H + R, :] = u
        t1 = (jj * R + 1 + lax.broadcasted_iota(jnp.int32, (R, HD_A), 0)).astype(F32)
        cnt = jnp.minimum(t1, _pool_width(g))
        pb = (_window_sum(lu, H + R, H, R, g, False) / cnt - u).astype(BF16)
        wgb = wg_ref[0].astype(BF16)
        lin = lax.dot_general(pb, wgb, _NN, preferred_element_type=F32) + bg_ref[...]
        dy = dy_ref[...].astype(F32)
        dsc_ref[...] += jnp.sum(dy * lin, axis=0, keepdims=True)
        dlin = dy * sc_ref[...]
        dbg_ref[...] += jnp.sum(dlin, axis=0, keepdims=True)
        dlb = dlin.astype(BF16)
        dwg_ref[0] += lax.dot_general(pb, dlb, _TN, preferred_element_type=F32)
        dp = lax.dot_general(dlb, wgb, _NT, preferred_element_type=F32)
        lq[0][0:R, :] = dp / cnt
        dz_ref[...] = (_window_sum(lq, R + H, 0, R, g, True) - dp).astype(BF16)
        lq[0][R:R + H, :] = lq[0][0:H, :]

    vec = pl.BlockSpec((1, HD_A), lambda g, j: (0, g))
    mat = pl.BlockSpec((1, HD_A, HD_A), lambda g, j: (g, 0, 0))
    return pl.pallas_call(
        body, out_shape=[SDS((S, D // 2), BF16), SDS((ng, HD_A, HD_A), F32), SDS((1, D // 2), F32),
                         SDS((1, D // 2), F32)],
        grid=(ng, nch),
        in_specs=[pl.BlockSpec((R, HD_A), lambda g, j: (nch - 1 - j, _POOL_Y0 + g)),
                  pl.BlockSpec((R, HD_A), lambda g, j: (nch - 1 - j, _POOL_T0 + g)),
                  pl.BlockSpec((H, HD_A), lambda g, j: (jnp.maximum((nch - 1 - j) * (R // H) - 1, 0), _POOL_T0 + g)),
                  mat, vec, vec],
        out_specs=[pl.BlockSpec((R, HD_A), lambda g, j: (nch - 1 - j, g)), mat, vec, vec],
        scratch_shapes=[pltpu.VMEM((SUB + H + R, HD_A), F32)] * 8,
        name="pool_bwd", compiler_params=_cp(2),
    )(dyab, zp, zp, wg, bg, sc)


_CW_F = 768


def _f_fwd(hp, w, b, name):
    S = hp.shape[0]
    R, H, cw = min(S, R_FFN), SUB, _CW_F
    nlt = cw // LANE

    def body(h_ref, w_ref, b_ref, o_ref, gel_ref, ud_ref, ext):
        j = pl.program_id(1)

        @pl.when(j == 0)
        def _():
            ext[:, 0:H, :] = jnp.zeros((nlt, H, LANE), F32)

        def stage(r0, lt):
            ext[lt, pl.ds(pl.multiple_of(r0 + H, SUB), _RB), :] = h_ref[pl.ds(r0, _RB), _lanes(lt)].astype(F32)

        def main(r0, lt):
            ls = _lanes(lt)
            gp = b_ref[:, ls]
            for k in range(CONV_F):
                gp = gp + w_ref[k:k + 1, ls] * ext[lt, pl.ds(r0 + (H - (CONV_F - 1 - k)), _RB), :]
            up = h_ref[pl.ds(r0, _RB), _lanes(lt + nlt)].astype(F32)
            gel, dgel = _gelu(gp, with_grad=True)
            rs = pl.ds(r0, _RB)
            o_ref[rs, ls] = (gel * up).astype(BF16)
            gel_ref[rs, ls] = gel.astype(BF16)
            ud_ref[rs, ls] = (up * dgel).astype(BF16)

        _sub_blocks(R, cw, stage)
        _sub_blocks(R, cw, main)
        ext[:, 0:H, :] = ext[:, R:R + H, :]

    tile = pl.BlockSpec((R, cw), lambda c, j: (j, c))
    return pl.pallas_call(
        body, out_shape=[SDS((S, D_FF), BF16)] * 3, grid=(D_FF // cw, S // R),
        in_specs=[pl.BlockSpec((R, 2 * cw), lambda c, j: (j, c)), pl.BlockSpec((CONV_F, cw), lambda c, j: (0, c)),
                  pl.BlockSpec((1, cw), lambda c, j: (0, c))],
        out_specs=[tile] * 3,
        scratch_shapes=[pltpu.VMEM((nlt, H + R, LANE), F32)], name=name, compiler_params=_cp(2),
    )(hp, w, b)


def _f_bwd(dact, hp, gel, ud, w, name):
    S = hp.shape[0]
    R, H, cw = min(S, R_FFN), SUB, _CW_F
    nch = S // R
    nlt = cw // LANE

    def body(da_ref, h_ref, hh_ref, gel_ref, ud_ref, w_ref, dh_ref, dw_ref, db_ref, ext_g, ext_d, acc):
        j = pl.program_id(1)
        jj = nch - 1 - j

        @pl.when(j == 0)
        def _():
            ext_d[:, R:R + H, :] = jnp.zeros((nlt, H, LANE), F32)
            acc[...] = jnp.zeros_like(acc)

        for lt in range(nlt):
            ext_g[lt, 0:H, :] = jnp.where(jj == 0, 0.0, hh_ref[_HB - H:_HB, lt * LANE:(lt + 1) * LANE].astype(F32))

        def stage(r0, lt):
            ext_g[lt, pl.ds(pl.multiple_of(r0 + H, SUB), _RB), :] = h_ref[pl.ds(r0, _RB), _lanes(lt)].astype(F32)

        def first(r0, lt):
            ls, lu, rs = _lanes(lt), _lanes(lt + nlt), pl.ds(r0, _RB)
            da = da_ref[rs, ls].astype(F32)
            dh_ref[rs, lu] = (da * gel_ref[rs, ls].astype(F32)).astype(BF16)
            dgp = da * ud_ref[rs, ls].astype(F32)
            ext_d[lt, rs, :] = dgp
            acc[CONV_F * SUB:(CONV_F + 1) * SUB, ls] += _psum8(dgp)
            for k in range(CONV_F):
                tap = ext_g[lt, pl.ds(r0 + (H - (CONV_F - 1 - k)), _RB), :]
                acc[k * SUB:(k + 1) * SUB, ls] += _psum8(dgp * tap)

        def second(r0, lt):
            ls = _lanes(lt)
            dhg = w_ref[CONV_F - 1:CONV_F, ls] * ext_d[lt, pl.ds(r0, _RB), :]
            for k in range(CONV_F - 1):
                dhg = dhg + w_ref[k:k + 1, ls] * ext_d[lt, pl.ds(r0 + (CONV_F - 1 - k), _RB), :]
            dh_ref[pl.ds(r0, _RB), ls] = dhg.astype(BF16)

        _sub_blocks(R, cw, stage)
        _sub_blocks(R, cw, first)
        _sub_blocks(R, cw, second)
        ext_d[:, R:R + H, :] = ext_d[:, 0:H, :]

        @pl.when(j == nch - 1)
        def _():
            for k in range(CONV_F):
                dw_ref[k:k + 1, :] = jnp.sum(acc[k * SUB:(k + 1) * SUB, :], axis=0, keepdims=True)
            db_ref[...] = jnp.sum(acc[CONV_F * SUB:(CONV_F + 1) * SUB, :], axis=0, keepdims=True)

    rows = lambda c, j: (nch - 1 - j, c)
    return pl.pallas_call(
        body, out_shape=[SDS((S, 2 * D_FF), BF16), SDS((CONV_F, D_FF), F32), SDS((1, D_FF), F32)],
        grid=(D_FF // cw, nch),
        in_specs=[pl.BlockSpec((R, cw), rows), pl.BlockSpec((R, cw), lambda c, j: (nch - 1 - j, 2 * c)),
                  pl.BlockSpec((_HB, cw), lambda c, j: (jnp.maximum((nch - 1 - j) * (R // _HB) - 1, 0), 2 * c)),
                  pl.BlockSpec((R, cw), rows), pl.BlockSpec((R, cw), rows),
                  pl.BlockSpec((CONV_F, cw), lambda c, j: (0, c))],
        out_specs=[pl.BlockSpec((R, 2 * cw), rows), pl.BlockSpec((CONV_F, cw), lambda c, j: (0, c)),
                   pl.BlockSpec((1, cw), lambda c, j: (0, c))],
        scratch_shapes=[pltpu.VMEM((nlt, H + R, LANE), F32), pltpu.VMEM((nlt, R + H, LANE), F32),
                        pltpu.VMEM(((CONV_F + 1) * SUB, cw), F32)], name=name,
        compiler_params=_cp(2),
    )(dact, hp, hp, gel, ud, w)


_CW_C = 256
_H_C = 32


def _c_fwd(h1p, w, b):
    S = h1p.shape[0]
    R, H, cw = R_SEQ, _H_C, _CW_C
    nlt = cw // LANE

    def body(h_ref, w_ref, b_ref, o_ref, ext):
        j = pl.program_id(1)

        @pl.when(j == 0)
        def _():
            ext[:, 0:H, :] = jnp.zeros((nlt, H, LANE), F32)

        def stage(r0, lt):
            rs = pl.ds(r0, _RB)
            gate = h_ref[rs, _lanes(lt + nlt)].astype(F32)
            ext[lt, pl.ds(pl.multiple_of(r0 + H, SUB), _RB), :] = h_ref[rs, _lanes(lt)].astype(F32) * _sigmoid(gate)

        def main(r0, lt):
            ls = _lanes(lt)
            cv = b_ref[:, ls]
            for k in range(CONV_C):
                cv = cv + w_ref[k:k + 1, ls] * ext[lt, pl.ds(r0 + (H - (CONV_C - 1 - k)), _RB), :]
            o_ref[pl.ds(r0, _RB), ls] = cv

        _sub_blocks(R, cw, stage)
        _sub_blocks(R, cw, main)
        ext[:, 0:H, :] = ext[:, R:R + H, :]

    return pl.pallas_call(
        body, out_shape=SDS((S, D), F32), grid=(D // cw, S // R),
        in_specs=[pl.BlockSpec((R, 2 * cw), lambda c, j: (j, c)), pl.BlockSpec((CONV_C, cw), lambda c, j: (0, c)),
                  pl.BlockSpec((1, cw), lambda c, j: (0, c))],
        out_specs=pl.BlockSpec((R, cw), lambda c, j: (j, c)),
        scratch_shapes=[pltpu.VMEM((nlt, H + R, LANE), F32)], name="conf_conv_fwd", compiler_params=_cp(2),
    )(h1p, w, b)


def _c_bwd(dcv, h1p, w):
    S = h1p.shape[0]
    R, H, cw, nch = R_SEQ, _H_C, _CW_C, S // R_SEQ
    nlt = cw // LANE
    a_b, a_val, a_gate = CONV_C * SUB, (CONV_C + 1) * SUB, (CONV_C + 2) * SUB

    def body(dc_ref, h_ref, hh_ref, w_ref, dh_ref, dw_ref, db_ref, db1_ref, ext_u, ext_d, acc):
        j = pl.program_id(1)
        jj = nch - 1 - j

        @pl.when(j == 0)
        def _():
            ext_d[:, R:R + H, :] = jnp.zeros((nlt, H, LANE), F32)
            acc[...] = jnp.zeros_like(acc)

        for lt in range(nlt):
            ext_u[lt, 0:H, :] = jnp.where(
                jj == 0, 0.0, hh_ref[:, lt * LANE:(lt + 1) * LANE].astype(F32)
                * _sigmoid(hh_ref[:, cw + lt * LANE:cw + (lt + 1) * LANE].astype(F32)))

        def stage(r0, lt):
            rs, ls = pl.ds(r0, _RB), _lanes(lt)
            gate = h_ref[rs, _lanes(lt + nlt)].astype(F32)
            ext_u[lt, pl.ds(pl.multiple_of(r0 + H, SUB), _RB), :] = h_ref[rs, ls].astype(F32) * _sigmoid(gate)
            ext_d[lt, rs, :] = dc_ref[rs, ls]

        def first(r0, lt):
            ls = _lanes(lt)
            dc = dc_ref[pl.ds(r0, _RB), ls]
            acc[a_b:a_b + SUB, ls] += _psum8(dc)
            for k in range(CONV_C):
                tap = ext_u[lt, pl.ds(r0 + (H - (CONV_C - 1 - k)), _RB), :]
                acc[k * SUB:(k + 1) * SUB, ls] += _psum8(dc * tap)

        def second(r0, lt):
            rs, ls, lg = pl.ds(r0, _RB), _lanes(lt), _lanes(lt + nlt)
            du = w_ref[CONV_C - 1:CONV_C, ls] * ext_d[lt, rs, :]
            for k in range(CONV_C - 1):
                du = du + w_ref[k:k + 1, ls] * ext_d[lt, pl.ds(r0 + (CONV_C - 1 - k), _RB), :]
            val = h_ref[rs, ls].astype(F32)
            sg = _sigmoid(h_ref[rs, lg].astype(F32))
            dval = du * sg
            dgate = du * val * sg * (1.0 - sg)
            acc[a_val:a_val + SUB, ls] += _psum8(dval)
            acc[a_gate:a_gate + SUB, ls] += _psum8(dgate)
            dh_ref[rs, ls] = dval.astype(BF16)
            dh_ref[rs, lg] = dgate.astype(BF16)

        _sub_blocks(R, cw, stage)
        _sub_blocks(R, cw, first)
        _sub_blocks(R, cw, second)
        ext_d[:, R:R + H, :] = ext_d[:, 0:H, :]

        @pl.when(j == nch - 1)
        def _():
            for k in range(CONV_C):
                dw_ref[k:k + 1, :] = jnp.sum(acc[k * SUB:(k + 1) * SUB, :], axis=0, keepdims=True)
            db_ref[...] = jnp.sum(acc[a_b:a_b + SUB, :], axis=0, keepdims=True)
            db1_ref[:, 0:cw] = jnp.sum(acc[a_val:a_val + SUB, :], axis=0, keepdims=True)
            db1_ref[:, cw:2 * cw] = jnp.sum(acc[a_gate:a_gate + SUB, :], axis=0, keepdims=True)

    rows = lambda c, j: (nch - 1 - j, c)
    return pl.pallas_call(
        body, out_shape=[SDS((S, 2 * D), BF16), SDS((CONV_C, D), F32), SDS((1, D), F32), SDS((1, 2 * D), F32)],
        grid=(D // cw, nch),
        in_specs=[pl.BlockSpec((R, cw), rows), pl.BlockSpec((R, 2 * cw), rows),
                  pl.BlockSpec((H, 2 * cw), lambda c, j: (jnp.maximum((nch - 1 - j) * (R // H) - 1, 0), c)),
                  pl.BlockSpec((CONV_C, cw), lambda c, j: (0, c))],
        out_specs=[pl.BlockSpec((R, 2 * cw), rows), pl.BlockSpec((CONV_C, cw), lambda c, j: (0, c)),
                   pl.BlockSpec((1, cw), lambda c, j: (0, c)), pl.BlockSpec((1, 2 * cw), lambda c, j: (0, c))],
        scratch_shapes=[pltpu.VMEM((nlt, H + R, LANE), F32), pltpu.VMEM((nlt, R + H, LANE), F32),
                        pltpu.VMEM(((CONV_C + 3) * SUB, cw), F32)], name="conf_conv_bwd",
        compiler_params=_cp(2),
    )(dcv, h1p, h1p, w)


def _local_step(x, mem, tgt, W, fetch=None, send=None):
    G = {}
    W = dict(W)

    def arrive(group, after):
        if fetch is None:
            return None
        got, tok = fetch(group, after)
        for key, val in got.items():
            W[key] = {**W.get(key, {}), **val} if isinstance(val, dict) else val
        return tok

    def gain(g, tok):
        return g if tok is None else g + tok

    def sent(group):
        return None if send is None else send(group, G)

    def xattn_fwd(xin, n, l):
        tok = arrive(("xa", l), n)
        mn = _rms_fwd(mem, gain(W["xa_mem_norm"][l:l + 1], tok), f"xa_memnorm_fwd{l}")
        q = _mm_nn(n, W["xa_wq"][l], out_dtype=BF16, name=f"xa_q{l}")
        k = _mm_nn(mn, W["xa_wk"][l], out_dtype=BF16, name=f"xa_k{l}")
        v = _mm_nn(mn, W["xa_wv"][l], out_dtype=BF16, name=f"xa_v{l}")
        o = _attn_fwd(q, k, v, f"xa_attn_fwd{l}")
        xout, nout = _mm_nn(o, W["xa_wo"][l], out_dtype=F32, name=f"xa_o{l}", add=xin, norm=W["f_norm"][l:l + 1])
        return xout, nout, (xin, n, q, mn, k, v, o)

    def xattn_bwd(dx, dxb, saved, l):
        xin, n, q, mn, k, v, o = saved
        do = _mm_nt(dxb, W["xa_wo"][l], out_dtype=BF16, name=f"xa_do{l}")
        G[f"xa_wo{l}"] = _mm_tn(o, dxb, out_dtype=BF16, name=f"xa_dwo{l}")
        dq, dk, dv = _attn_bwd(q, k, v, do, f"xa_attn_bwd{l}")
        dkb, dvb = dk.astype(BF16), dv.astype(BF16)
        G[f"xa_wq{l}"] = _mm_tn(n, dq, out_dtype=BF16, name=f"xa_dwq{l}")
        G[f"xa_wk{l}"] = _mm_tn(mn, dkb, out_dtype=BF16, name=f"xa_dwk{l}")
        G[f"xa_wv{l}"] = _mm_tn(mn, dvb, out_dtype=BF16, name=f"xa_dwv{l}")
        tok = sent(("xa", l))
        dmn = _mm_nt(dkb, W["xa_wk"][l], out_dtype=F32, name=f"xa_dmn_k{l}")
        dmn = _mm_nt(dvb, W["xa_wv"][l], out_dtype=F32, name=f"xa_dmn_v{l}", add=dmn)
        (G[f"xa_mem_norm{l}"],) = _rms_bwd(mem, W["xa_mem_norm"][l:l + 1], dmn, None, f"xa_memnorm_bwd{l}")
        dx, dxb, G[f"xa_norm{l}"] = _mm_nt(dq, W["xa_wq"][l], out_dtype=F32, name=f"xa_dn{l}",
                                           rms=(xin, gain(W["xa_norm"][l:l + 1], tok), dx))
        return dx, dxb

    def ffn_fwd(xin, n, l, next_gain):
        tok = arrive(("f", l), n)
        hp = _mm_nn(n, W["f_w_up"][l], out_dtype=BF16, name=f"f_up{l}")
        act, gel, ud = _f_fwd(hp, W["f_dw_w"][l], gain(W["f_dw_b"][l:l + 1], tok), f"f_conv_fwd{l}")
        arrive(("fd", l), act)
        res = _mm_nn(act, W["f_w_down"][l], out_dtype=F32, name=f"f_down{l}", add=xin, norm=next_gain)
        xout, nout = res if next_gain is not None else (res, None)
        return xout, nout, (xin, n, hp, act, gel, ud)

    def ffn_bwd(dx, dxb, saved, l):
        xin, n, hp, act, gel, ud = saved
        dact = _mm_nt(dxb, W["f_w_down"][l], out_dtype=BF16, name=f"f_dact{l}")
        G[f"f_w_down{l}"] = _mm_tn(act, dxb, out_dtype=BF16, name=f"f_dwdown{l}")
        dhp, G[f"f_dw_w{l}"], G[f"f_dw_b{l}"] = _f_bwd(dact, hp, gel, ud, W["f_dw_w"][l], f"f_conv_bwd{l}")
        G[f"f_w_up{l}"] = _mm_tn(n, dhp, out_dtype=BF16, name=f"f_dwup{l}", blocks=_CW_F)
        tok = sent(("f", l))
        dx, dxb, G[f"f_norm{l}"] = _mm_nt(dhp, W["f_w_up"][l], out_dtype=F32, name=f"f_dn{l}",
                                          rms=(xin, gain(W["f_norm"][l:l + 1], tok), dx))
        return dx, dxb

    n0 = _rms_fwd(x, W["ab_norm"], "ab_norm_fwd")
    tok = arrive(("ab", 0), n0)
    a_par = (W["a_conv_w"], gain(W["a_conv_b"], tok), W["a_gate_x_w"], W["a_gate_x_b"], W["a_gate_a_w"],
             W["a_gate_a_b"], W["a_lambda"])
    b_par = (W["b_group_w"], W["b_group_b"], W["b_scale"])
    zp = _mm_nn(n0, W["ab_w_in"], out_dtype=BF16, name="ab_in")
    yab, h_a = _a_fwd(zp, *a_par)
    yab = _b_fwd(zp, yab, *b_par)
    arrive(("ab", 1), yab)
    x1, n1 = _mm_nn(yab, W["ab_w_out"], out_dtype=F32, name="ab_out", add=x, norm=W["xa_norm"][0:1])
    x2, n2, s_xa0 = xattn_fwd(x1, n1, 0)
    x3, n3, s_f0 = ffn_fwd(x2, n2, 0, W["c_norm"])
    tok = arrive(("c", 0), n3)
    h1p = _mm_nn(n3, W["c_w_pw1"], out_dtype=BF16, name="c_pw1", bias=gain(W["c_b_pw1"], tok))
    cv = _c_fwd(h1p, W["c_dw_w"], W["c_dw_b"])
    sc = _ln_silu_fwd(cv, W["c_ln_g"], W["c_ln_b"])
    x4, n4 = _mm_nn(sc, W["c_w_pw2"], out_dtype=F32, name="c_pw2", bias=W["c_b_pw2"], add=x3, norm=W["xa_norm"][1:2])
    x5, n5, s_xa1 = xattn_fwd(x4, n4, 1)
    x6, _, s_f1 = ffn_fwd(x5, n5, 1, None)
    loss, dx, dxb, G["final_norm"] = _loss_head(x6, W["final_norm"], tgt)

    dx, dxb = ffn_bwd(dx, dxb, s_f1, 1)
    dx, dxb = xattn_bwd(dx, dxb, s_xa1, 1)
    dsc = _mm_nt(dxb, W["c_w_pw2"], out_dtype=BF16, name="c_dsc")
    G["c_w_pw2"] = _mm_tn(sc, dxb, out_dtype=BF16, name="c_dwpw2")
    dcv, G["c_ln_g"], G["c_ln_b"], G["c_b_pw2"] = _ln_silu_bwd(dsc, cv, W["c_ln_g"], W["c_ln_b"], dx)
    dh1p, G["c_dw_w"], G["c_dw_b"], G["c_b_pw1"] = _c_bwd(dcv, h1p, W["c_dw_w"])
    G["c_w_pw1"] = _mm_tn(n3, dh1p, out_dtype=BF16, name="c_dwpw1", blocks=_CW_C)
    tok = sent(("c", 0))
    dx, dxb, G["c_norm"] = _mm_nt(dh1p, W["c_w_pw1"], out_dtype=F32, name="c_dn",
                                  rms=(x3, gain(W["c_norm"], tok), dx))
    dx, dxb = ffn_bwd(dx, dxb, s_f0, 0)
    dx, dxb = xattn_bwd(dx, dxb, s_xa0, 0)
    dyab = _mm_nt(dxb, W["ab_w_out"], out_dtype=BF16, name="ab_dyab")
    G["ab_w_out"] = _mm_tn(yab, dxb, out_dtype=BF16, name="ab_dwout")
    tok = sent(("ab", 1))
    a_par = (a_par[0], gain(a_par[1], tok)) + a_par[2:]
    (dzg, dzr, G["a_conv_w"], G["a_conv_b"], G["a_gate_x_w"], G["a_gate_x_b"], G["a_gate_a_w"], G["a_gate_a_b"],
     G["a_lambda"]) = _a_bwd(dyab, zp, h_a, *a_par)
    dzq, G["b_group_w"], G["b_group_b"], G["b_scale"] = _b_bwd(dyab, zp, *b_par)
    G["ab_w_in"] = jnp.concatenate(
        [_mm_tn(n0, dz, out_dtype=BF16, name=f"ab_dwin_{part}")
         for part, dz in (("gate", dzg), ("rec", dzr), ("pool", dzq))], axis=1)
    tok = sent(("ab", 0))
    dx, _, G["ab_norm"] = _mm_nt_cols([dzg, dzr, dzq], W["ab_w_in"], name="ab_dn",
                                      rms=(x, gain(W["ab_norm"], tok), dx))
    return loss, dx, G


def _my_place():
    x, y, c = lax.axis_index("x"), lax.axis_index("y"), lax.axis_index("c")
    return x, y, c


def _all_gather(shards, name):
    n = len(shards)

    def body(*refs):
        ins, outs = refs[:n], refs[n:2 * n]
        send_sems, recv_sems, local_sems = refs[2 * n:]
        x, y, c = _my_place()
        me, sibling = (x, y, c), (x, y, 1 - c)
        chips = [(1 - x, y), (x, 1 - y), (1 - x, 1 - y)]

        def slab(a, place):
            px, py, pc = place
            return outs[a].at[4 * px + 2 * py + pc]

        def copy(a, k, block, to, src=None):
            return pltpu.make_async_remote_copy(
                src_ref=slab(a, block) if src is None else src, dst_ref=slab(a, block),
                send_sem=send_sems.at[a, k], recv_sem=recv_sems.at[a, k], device_id=to, device_id_type=MESH)

        mine = [pltpu.make_async_copy(ins[a], slab(a, me), local_sems.at[a]) for a in range(n)]
        for cp in mine:
            cp.start()
        first = []
        for j, chip in enumerate(chips):
            first += [copy(a, 1 + j, me, (*chip, c), src=ins[a]) for a in range(n)]
        first += [copy(a, 0, me, sibling, src=ins[a]) for a in range(n)]
        for cp in first:
            cp.start()
        passed = []
        for j, chip in enumerate(chips):
            for a in range(n):
                copy(a, 1 + j, (*chip, c), me).wait_recv()
                cp = copy(a, 4 + j, (*chip, c), sibling)
                cp.start()
                passed.append(cp)
        for a in range(n):
            copy(a, 0, sibling, me).wait_recv()
        for j, chip in enumerate(chips):
            for a in range(n):
                copy(a, 4 + j, (*chip, 1 - c), me).wait_recv()
        for cp in first + passed:
            cp.wait_send()
        for cp in mine:
            cp.wait()

    any_spec = pl.BlockSpec(memory_space=pl.ANY)
    return pl.pallas_call(
        body, out_shape=[SDS((N_DEV,) + s.shape, s.dtype) for s in shards], in_specs=[any_spec] * n,
        out_specs=[any_spec] * n,
        scratch_shapes=[pltpu.SemaphoreType.DMA((n, 7)), pltpu.SemaphoreType.DMA((n, 7)), pltpu.SemaphoreType.DMA((n,))],
        name=name,
    )(*shards)


_HBM = pl.BlockSpec(memory_space=pltpu.HBM)
_SEM = pl.BlockSpec(memory_space=pltpu.SEMAPHORE)
_EFFECT = pltpu.SideEffectType.DATAFLOW_SIDE_EFFECTING


def _peer_places():
    x, y, c = _my_place()
    peers = []
    for k in range(1, N_DEV):
        px = 1 - x if (k >> 2) & 1 else x
        py = 1 - y if (k >> 1) & 1 else y
        pc = 1 - c if k & 1 else c
        peers.append(((px, py, pc), 4 * px + 2 * py + pc))
    return (x, y, c), 4 * x + 2 * y + c, peers


def _send_start(srcs, per_dest, name):
    n = len(srcs)
    lands = [lax.empty((N_DEV,) + (s.shape[1:] if per_dest else s.shape), s.dtype) for s in srcs]

    def body(*refs):
        src, land = refs[:n], refs[n:2 * n]
        outs = refs[2 * n:]
        send, recv, token = outs[:n], outs[n:2 * n], outs[4 * n]
        _, me, peers = _peer_places()
        for a in range(n):
            for peer, pidx in peers:
                pltpu.make_async_remote_copy(
                    src_ref=src[a].at[pidx] if per_dest else src[a], dst_ref=land[a].at[me], send_sem=send[a],
                    recv_sem=recv[a], device_id=peer, device_id_type=MESH).start()
        token[...] = jnp.zeros_like(token)

    hbm = lambda a: pltpu.HBM(a.shape, a.dtype)
    sem = pltpu.SemaphoreType.DMA(())
    res = pl.pallas_call(
        body, name=name,
        out_shape=tuple([sem] * (2 * n) + [hbm(s) for s in srcs] + [hbm(l) for l in lands]
                        + [SDS((SUB, LANE), F32)]),
        in_specs=[_HBM] * (2 * n),
        out_specs=tuple([_SEM] * (2 * n) + [_HBM] * (2 * n) + [pl.BlockSpec(memory_space=pltpu.VMEM)]),
        input_output_aliases={i: 2 * n + i for i in range(2 * n)},
        compiler_params=pltpu.CompilerParams(has_side_effects=_EFFECT),
    )(*[pltpu.with_memory_space_constraint(s, pltpu.HBM) for s in srcs],
      *[pltpu.with_memory_space_constraint(l, pltpu.HBM) for l in lands])
    return res[:n], res[n:2 * n], res[2 * n:3 * n], res[3 * n:4 * n], res[4 * n]


def _send_wait(send, recv, srcs, lands, after, per_dest, name):
    n = len(srcs)

    def body(*refs):
        src, land = refs[:n], refs[n:2 * n]
        send_s, recv_s = refs[2 * n:3 * n], refs[3 * n:4 * n]
        token = refs[-1]
        place, _, _ = _peer_places()
        for a in range(n):
            seven = land[a].at[pl.ds(0, N_DEV - 1)]
            copy = pltpu.make_async_remote_copy(
                src_ref=src[a].at[pl.ds(0, N_DEV - 1)] if per_dest else seven, dst_ref=seven, send_sem=send_s[a],
                recv_sem=recv_s[a], device_id=place, device_id_type=MESH)
            copy.wait_send()
            copy.wait_recv()
        token[...] = jnp.zeros_like(token)

    hbm = lambda a: pltpu.HBM(a.shape, a.dtype)
    res = pl.pallas_call(
        body, name=name,
        out_shape=tuple([hbm(s) for s in srcs] + [hbm(l) for l in lands] + [SDS((SUB, LANE), F32)]),
        in_specs=[_HBM] * (2 * n) + [_SEM] * (2 * n) + [pl.BlockSpec(memory_space=pl.ANY)],
        out_specs=tuple([_HBM] * (2 * n) + [pl.BlockSpec(memory_space=pltpu.VMEM)]),
        input_output_aliases={i: i for i in range(2 * n)},
        compiler_params=pltpu.CompilerParams(has_side_effects=_EFFECT),
    )(*srcs, *lands, *send, *recv, after)
    return res[:n], res[n:2 * n], res[2 * n]


def _adamw_math(w, g, m, v):
    m = ADAM_B1 * m + (1.0 - ADAM_B1) * g
    v = ADAM_B2 * v + (1.0 - ADAM_B2) * (g * g)
    m_hat = m / (1.0 - ADAM_B1 ** ADAM_STEP)
    v_hat = v / (1.0 - ADAM_B2 ** ADAM_STEP)
    delta = -ADAM_LR * (m_hat / (jnp.sqrt(v_hat) + ADAM_EPS) + ADAM_WD * w)
    return delta, m, v


def _row_tile(r, c, itemsize_rows):
    cap = max(SUB, (itemsize_rows // (4 * c)) // SUB * SUB)
    if r <= cap:
        return r
    best = None
    for t in range(SUB, cap + 1, SUB):
        if r % t == 0:
            best = t
    return best if best is not None else r


def _sum_adamw(landing, w, m, v, name, layer=0, prev=None, after=None):
    _, r, c = landing.shape
    tr = _row_tile(r, c, 2 << 20)
    off = layer * (r // tr)
    tail = ([] if prev is None else list(prev)) + ([] if after is None else [after])

    def body(l_ref, w_ref, m_ref, v_ref, *rest):
        g_ref, d_ref, mo_ref, vo_ref = rest[-4:]
        g = l_ref[0].astype(F32)
        for s in range(1, N_DEV):
            g = g + l_ref[s].astype(F32)
        g_ref[...] = g
        d_ref[...], mo_ref[...], vo_ref[...] = _adamw_math(w_ref[...], g, m_ref[...], v_ref[...])

    blk = pl.BlockSpec((tr, c), lambda i: (i + off, 0))
    n_prev = 0 if prev is None else 4
    return pl.pallas_call(
        body, out_shape=[SDS(w.shape, F32)] * 4, grid=(r // tr,),
        in_specs=[pl.BlockSpec((N_DEV, tr, c), lambda i: (0, i, 0)), blk, blk, blk]
        + [pl.BlockSpec(memory_space=pl.ANY)] * len(tail),
        out_specs=[blk] * 4, input_output_aliases={4 + i: i for i in range(n_prev)}, name=name,
        compiler_params=_cp(1),
    )(landing, w, m, v, *tail)


def _sum8(landing, name):
    _, r, c = landing.shape

    def body(l_ref, g_ref):
        g = l_ref[0]
        for s in range(1, N_DEV):
            g = g + l_ref[s]
        g_ref[...] = g

    return pl.pallas_call(body, out_shape=SDS((r, c), F32), name=name, compiler_params=_cp(0))(landing)


def _adamw_small(repl_pack, own_pack, P, M, V):
    table, off = [], 0
    for name, shape in _REPL.items():
        table.append((name, shape if len(shape) > 1 else (1,) + shape, 0, off // LANE))
        off += _size(shape)
    off = _REPL_ROWS * LANE
    for name, shape in _SMALL_SHARDED.items():
        table.append((name, shape, 1, off // LANE))
        off += _size(shape)
    n = len(table)

    def body(*refs):
        packs, ins, outs = refs[:2], refs[2:2 + 3 * n], refs[2 + 3 * n:]
        for p, (_, shape, which, r0) in enumerate(table):
            w_ref, m_ref, v_ref = ins[3 * p:3 * p + 3]
            g_ref, d_ref, mo_ref, vo_ref = outs[4 * p:4 * p + 4]
            pack, rows, q = packs[which], shape[-2], shape[-1] // LANE
            lead = [()]
            for dim in shape[:-2]:
                lead = [t + (i,) for t in lead for i in range(dim)]
            for li, idx in enumerate(lead):
                if q == 1:
                    dst = g_ref.at[idx] if idx else g_ref
                    dst[...] = pack[r0 + li * rows:r0 + (li + 1) * rows, :]
                    continue
                for i in range(rows):
                    for k in range(q):
                        row = r0 + (li * rows + i) * q + k
                        g_ref[idx + (slice(i, i + 1), slice(k * LANE, (k + 1) * LANE))] = pack[row:row + 1, :]
            d_ref[...], mo_ref[...], vo_ref[...] = _adamw_math(w_ref[...], g_ref[...], m_ref[...], v_ref[...])

    ins, out_shape = [], []
    for name, shape, _, _ in table:
        ins += [t[name].reshape(shape) for t in (P, M, V)]
        out_shape += [SDS(shape, F32)] * 4
    res = pl.pallas_call(body, out_shape=out_shape, name="adamw_small", compiler_params=_cp(0))(
        repl_pack, own_pack, *ins)
    dicts = ({}, {}, {}, {})
    for p, (name, shape, _, _) in enumerate(table):
        for d, arr in zip(dicts, res[4 * p:4 * p + 4]):
            d[name] = arr.reshape(P[name].shape)
    return dicts


_BIG = {
    "ab_w_in": (1, D, 320), "ab_w_out": (1, 192, D), "c_w_pw1": (1, D, 256), "c_w_pw2": (1, 128, D),
    "xa_wq": (2, 128, D), "xa_wk": (2, 128, D), "xa_wv": (2, 128, D), "xa_wo": (2, 128, D),
    "f_w_up": (2, D, 768), "f_w_down": (2, 384, D),
}
_SMALL_SHARDED = {
    "a_conv_w": (1, 4, 128), "c_norm": (1, 128), "c_b_pw1": (1, 256), "c_dw_w": (1, 31, 128), "c_dw_b": (1, 128),
    "c_ln_g": (1, 128), "c_ln_b": (1, 128), "c_b_pw2": (1, 128), "f_dw_w": (2, 3, 384),
}
_REPL = {
    "ab_norm": (1, D), "a_conv_b": (1, D), "a_gate_x_w": (1, 8, 128, 128), "a_gate_x_b": (1, D),
    "a_gate_a_w": (1, 8, 128, 128), "a_gate_a_b": (1, D), "a_lambda": (1, D), "b_group_w": (1, 4, 128, 128),
    "b_group_b": (1, 512), "b_scale": (1, 512), "xa_norm": (2, D), "xa_mem_norm": (2, D), "f_norm": (2, D),
    "f_dw_b": (2, D_FF), "final_norm": (D,),
}


def _size(shape):
    n = 1
    for s in shape:
        n *= s
    return n


_N_SS = sum(_size(s) for s in _SMALL_SHARDED.values())
_N_REPL = sum(_size(s) for s in _REPL.values())
_REPL_ROWS = -(-_N_REPL // (N_DEV * SUB * LANE)) * SUB
_SS_ROWS = _N_SS // LANE
_SMALL_ROWS = -(-(_REPL_ROWS + _SS_ROWS) // SUB) * SUB


def _pack(parts, rows):
    flat = jnp.concatenate([p.reshape(-1).astype(F32) for p in parts])
    return jnp.pad(flat, (0, rows * LANE - flat.shape[0])).reshape(rows, LANE)


def _pair_blocks(v, bw):
    lead, n = v.shape[:-1], v.shape[-1]
    return jnp.swapaxes(v.reshape(lead + (2, n // (2 * bw), bw)), -3, -2).reshape(lead + (n,))


def _unpair_blocks(v, bw):
    lead, n = v.shape[:-1], v.shape[-1]
    return jnp.swapaxes(v.reshape(lead + (n // (2 * bw), 2, bw)), -3, -2).reshape(lead + (n,))


_GROUPS = {
    ("ab", 0): (("ab_w_in", 0),),
    ("ab", 1): (("ab_w_out", 0),),
    ("xa", 0): (("xa_wq", 0), ("xa_wk", 0), ("xa_wv", 0), ("xa_wo", 0)),
    ("f", 0): (("f_w_up", 0),),
    ("fd", 0): (("f_w_down", 0),),
    ("c", 0): (("c_w_pw1", 0), ("c_w_pw2", 0)),
    ("xa", 1): (("xa_wq", 1), ("xa_wk", 1), ("xa_wv", 1), ("xa_wo", 1)),
    ("f", 1): (("f_w_up", 1),),
    ("fd", 1): (("f_w_down", 1),),
}
_SEND_GROUPS = {g: m for g, m in _GROUPS.items() if g[0] != "fd"}
_SEND_GROUPS[("f", 0)] = (("f_w_up", 0), ("f_w_down", 0))
_SEND_GROUPS[("f", 1)] = (("f_w_up", 1), ("f_w_down", 1))


def _weight_layout(name, g):
    if name == "ab_w_in":
        return jnp.swapaxes(g, 0, 1).reshape(D, N_DEV * 320)
    if name in ("c_w_pw1", "f_w_up"):
        return g
    return g.reshape(N_DEV * g.shape[1], D)


def _grad_blocks(name, l, G):
    _, r, c = _BIG[name]
    if name == "ab_w_in":
        return jnp.swapaxes(G[name].reshape(D, N_DEV, 320), 0, 1)
    if name == "c_w_pw1":
        return G[name]
    if name == "f_w_up":
        return G[f"{name}{l}"]
    return (G[name] if _BIG[name][0] == 1 else G[f"{name}{l}"]).reshape(N_DEV, r, c)


def _small_layouts(sm):
    W = {}
    sm = sm.reshape(N_DEV, -1)
    off = 0
    for name, shape in _SMALL_SHARDED.items():
        n = _size(shape)
        blocks = sm[:, off:off + n].reshape((N_DEV,) + shape)
        off += n
        W[name] = jnp.moveaxis(blocks, 0, -2).reshape(shape[:-1] + (N_DEV * shape[-1],))
    W["a_conv_w"], W["c_dw_w"] = W["a_conv_w"][0], W["c_dw_w"][0]
    W["c_b_pw1"] = _pair_blocks(W["c_b_pw1"], _CW_C)
    return W


def _with_own(land, src, me, per_dest):
    own = lax.dynamic_slice_in_dim(src, me, 1, 0) if per_dest else src[None]
    return lax.dynamic_update_slice_in_dim(land, own, me, 0)


def _to_dest_major(g, shape):
    full = g.reshape(shape[:-1] + (N_DEV, shape[-1]))
    return jnp.moveaxis(full, -2, 0).reshape(N_DEV, -1)


def kernel(x, mem, ab_norm, ab_w_in, a_conv_w, a_conv_b, a_gate_x_w, a_gate_x_b, a_gate_a_w, a_gate_a_b, a_lambda, b_group_w, b_group_b, b_scale, ab_w_out, c_norm, c_w_pw1, c_b_pw1, c_dw_w, c_dw_b, c_ln_g, c_ln_b, c_w_pw2, c_b_pw2, xa_norm, xa_mem_norm, xa_wq, xa_wk, xa_wv, xa_wo, f_norm, f_w_up, f_dw_w, f_dw_b, f_w_down, final_norm, loss_target, m_ab_norm, m_ab_w_in, m_a_conv_w, m_a_conv_b, m_a_gate_x_w, m_a_gate_x_b, m_a_gate_a_w, m_a_gate_a_b, m_a_lambda, m_b_group_w, m_b_group_b, m_b_scale, m_ab_w_out, m_c_norm, m_c_w_pw1, m_c_b_pw1, m_c_dw_w, m_c_dw_b, m_c_ln_g, m_c_ln_b, m_c_w_pw2, m_c_b_pw2, m_xa_norm, m_xa_mem_norm, m_xa_wq, m_xa_wk, m_xa_wv, m_xa_wo, m_f_norm, m_f_w_up, m_f_dw_w, m_f_dw_b, m_f_w_down, m_final_norm, v_ab_norm, v_ab_w_in, v_a_conv_w, v_a_conv_b, v_a_gate_x_w, v_a_gate_x_b, v_a_gate_a_w, v_a_gate_a_b, v_a_lambda, v_b_group_w, v_b_group_b, v_b_scale, v_ab_w_out, v_c_norm, v_c_w_pw1, v_c_b_pw1, v_c_dw_w, v_c_dw_b, v_c_ln_g, v_c_ln_b, v_c_w_pw2, v_c_b_pw2, v_xa_norm, v_xa_mem_norm, v_xa_wq, v_xa_wk, v_xa_wv, v_xa_wo, v_f_norm, v_f_w_up, v_f_dw_w, v_f_dw_b, v_f_w_down, v_final_norm):
    args = dict(locals())
    P = {n: args[n] for n in _NAMES}
    M = {n: args["m_" + n] for n in _NAMES}
    V = {n: args["v_" + n] for n in _NAMES}

    me = 4 * lax.axis_index("x") + 2 * lax.axis_index("y") + lax.axis_index("c")

    in_flight = {}

    def launch(groups, tok):
        shards, n_of = [], {}
        for grp in groups:
            for name, l in _GROUPS[grp]:
                w = P[name][l] if tok is None else P[name][l] + tok
                shards.append(w.astype(BF16))
            if grp == ("ab", 0):
                shards.append(_pack([P[n] for n in _SMALL_SHARDED], _SS_ROWS + 4))
            n_of[grp] = len(shards)
        res = _send_start(shards, False, "gather_start_" + "_".join(g[0] + str(g[1]) for g in groups))
        lo = 0
        for grp in groups:
            in_flight[grp] = [r[lo:n_of[grp]] for r in res[:4]]
            lo = n_of[grp]
        return res[4][:1, :1]

    follow = {("ab", 0): [("ab", 1), ("xa", 0), ("f", 0), ("fd", 0)], ("xa", 0): [("c", 0), ("xa", 1)],
              ("f", 0): [("f", 1), ("fd", 1)]}

    def fetch(grp, after):
        send_s, recv_s, srcs, lands = in_flight.pop(grp)
        srcs, lands, tok = _send_wait(send_s, recv_s, srcs, lands, after, False, f"gather_wait_{grp[0]}{grp[1]}")
        tok = launch(follow[grp], tok[:1, :1]) if grp in follow else None
        full = [_with_own(land, src, me, False) for land, src in zip(lands, srcs)]
        out = {}
        for (name, l), g in zip(_GROUPS[grp], full):
            w = _weight_layout(name, g)
            if _BIG[name][0] == 1:
                out[name] = w
            else:
                out[name] = {l: w}
        if grp == ("ab", 0):
            out.update(_small_layouts(full[-1]))
        return out, tok

    zero = launch([("ab", 0)], None)

    pending, held = [], []
    rides_with_next = {("xa", 1), ("f", 0)}

    def send(grp, G):
        held.extend(_SEND_GROUPS[grp])
        if grp in rides_with_next:
            return None
        members = tuple(held)
        del held[:]
        res = _send_start([_grad_blocks(name, l, G) for name, l in members], True, f"send_{grp[0]}{grp[1]}")
        pending.append((members, res))
        return res[4][:1, :1]

    W = {n: P[n] for n in _REPL}
    W["ab_norm"] = P["ab_norm"] + zero
    W["final_norm"] = P["final_norm"].reshape(1, D)
    W["a_gate_x_w"], W["a_gate_a_w"], W["b_group_w"] = P["a_gate_x_w"][0], P["a_gate_a_w"][0], P["b_group_w"][0]
    loss, grad_x, G = _local_step(x[0], mem[0], loss_target[0], W, fetch, send)
    loss = lax.psum(loss[0, 0], ("x", "y", "c"))

    Gs = dict(G)
    Gs["c_b_pw1"] = _unpair_blocks(G["c_b_pw1"], _CW_C)
    Gs["f_dw_w"] = jnp.stack([G["f_dw_w0"], G["f_dw_w1"]])
    Gs["a_conv_w"], Gs["c_dw_w"] = G["a_conv_w"][None], G["c_dw_w"][None]
    for n in ("xa_norm", "xa_mem_norm", "f_norm", "f_dw_b"):
        Gs[n] = jnp.concatenate([G[f"{n}0"], G[f"{n}1"]], axis=0)
    for n in ("a_gate_x_w", "a_gate_a_w", "b_group_w"):
        Gs[n] = G[n][None]
    repl_flat = jnp.concatenate([Gs[n].reshape(-1) for n in _REPL])
    repl_rows = jnp.pad(repl_flat, (0, N_DEV * _REPL_ROWS * LANE - _N_REPL)).reshape(N_DEV, _REPL_ROWS, LANE)
    ss_rows = jnp.concatenate([_to_dest_major(Gs[n], s) for n, s in _SMALL_SHARDED.items()], axis=1)
    ss_rows = ss_rows.reshape(N_DEV, _SS_ROWS, LANE)
    small_pack = jnp.concatenate(
        [repl_rows, ss_rows, jnp.zeros((N_DEV, _SMALL_ROWS - _REPL_ROWS - _SS_ROWS, LANE), F32)], axis=1)
    last = _send_start([small_pack], True, "send_small")
    pending.append(((("small", 0),), last))

    def arrived(some, after, name):
        members = [m for mem_, _ in some for m in mem_]
        cat = [[a for _, res in some for a in res[i]] for i in range(4)]
        srcs, lands, _ = _send_wait(cat[0], cat[1], cat[2], cat[3], after, True, name)
        return {m: _with_own(land, src, me, True) for m, land, src in zip(members, lands, srcs)}

    out_g, out_d, out_m, out_v = {}, {}, {}, {}
    chain = [None]

    def update(name, landed):
        layers, r, c = _BIG[name]
        w2, m2, v2 = [t[name].reshape(layers * r, c) for t in (P, M, V)]
        res = None
        for l in range(layers):
            res = _sum_adamw(landed[(name, l)], w2, m2, v2, f"adamw_{name}{l}", layer=l, prev=res,
                             after=chain[0] if l == 0 else None)
        chain[0] = res[1]
        out_g[name], out_d[name], out_m[name], out_v[name] = [t.reshape(P[name].shape) for t in res]

    landed = arrived(pending[:-2], grad_x, "send_wait_early")
    for name in _BIG:
        if name != "ab_w_in":
            update(name, landed)
    landed = arrived(pending[-2:], out_v["f_w_down"], "send_wait_late")
    update("ab_w_in", landed)

    small_sum = _sum8(landed[("small", 0)], "sum_small")
    (repl_all,) = _all_gather([small_sum[:_REPL_ROWS]], "gather_small_grads")
    for out, got in zip((out_g, out_d, out_m, out_v),
                        _adamw_small(repl_all.reshape(N_DEV * _REPL_ROWS, LANE), small_sum, P, M, V)):
        out.update(got)

    return (loss, grad_x[None], *[out_g[n] for n in _NAMES], *[out_d[n] for n in _NAMES],
            *[out_m[n] for n in _NAMES], *[out_v[n] for n in _NAMES])


_NAMES = ("ab_norm", "ab_w_in", "a_conv_w", "a_conv_b", "a_gate_x_w", "a_gate_x_b", "a_gate_a_w", "a_gate_a_b",
          "a_lambda", "b_group_w", "b_group_b", "b_scale", "ab_w_out", "c_norm", "c_w_pw1", "c_b_pw1", "c_dw_w",
          "c_dw_b", "c_ln_g", "c_ln_b", "c_w_pw2", "c_b_pw2", "xa_norm", "xa_mem_norm", "xa_wq", "xa_wk", "xa_wv",
          "xa_wo", "f_norm", "f_w_up", "f_dw_w", "f_dw_b", "f_w_down", "final_norm")
```

```python
import functools

import jax
import jax.numpy as jnp
from jax import lax
from jax.experimental import pallas as pl
from jax.experimental.pallas import tpu as pltpu

F32, BF16 = jnp.float32, jnp.bfloat16
SDS = jax.ShapeDtypeStruct
MESH = pl.DeviceIdType.MESH

N_DEV = 8
D = 1024
N_MEM = 256
XA_HEADS, XA_HD = 4, 256
HD_A = 128
CONV_A, CONV_C, CONV_F = 4, 31, 3
C_RG = 8.0
POOL_WINDOWS = (2, 4, 8, 16)
D_FF = 3 * D
EPS = 1e-6
ADAM_LR, ADAM_B1, ADAM_B2, ADAM_EPS, ADAM_WD, ADAM_STEP = 0.001, 0.9, 0.999, 1e-08, 0.01, 10

LANE = 128
SUB = 8
VMEM_LIMIT = 56 * 1024 * 1024
R_SEQ = 1024
R_POOL = 2048
R_RGLRU = 2048
R_FFN = 2048
TM_ROW = 1024


def _cp(n_axes):
    return pltpu.CompilerParams(dimension_semantics=("arbitrary",) * n_axes, vmem_limit_bytes=VMEM_LIMIT)


def _tile(n, pref):
    if n <= pref:
        return n
    best = None
    for t in range(LANE, pref + 1, LANE):
        if n % t == 0:
            best = t
    assert best is not None, (n, pref)
    return best


def _perm2(n):
    return (n % 2) * 4 + n // 2


_NN = (((1,), (0,)), ((), ()))
_NT = (((1,), (1,)), ((), ()))
_TN = (((0,), (0,)), ((), ()))


def _mm_call(name, grid, ab, ab_specs, dims, acc_shape, extras, outs, finish, from_ref=False):
    nk = grid[2]
    n_ab, n_ex, n_out = len(ab), len(extras), len(outs)
    use_acc = nk > 1 or from_ref

    def product(refs):
        r = lax.dot_general(refs[0][...], refs[1][...], dims, preferred_element_type=F32)
        for i in range(1, n_ab):
            r = r + lax.dot_general(refs[2 * i][...], refs[2 * i + 1][...], dims, preferred_element_type=F32)
        return r

    def body(*refs):
        rest = refs[2 * n_ab:]
        ex_refs, o_refs = rest[:n_ex], rest[n_ex:n_ex + n_out]
        first_rows = pl.program_id(0) == 0
        if not use_acc:
            finish(product(refs), ex_refs, o_refs, first_rows)
            return
        acc = rest[n_ex + n_out]
        if nk == 1:
            acc[...] = product(refs)
            finish(acc, ex_refs, o_refs, first_rows)
            return
        k = pl.program_id(2)

        @pl.when(k == 0)
        def _():
            acc[...] = jnp.zeros_like(acc)

        acc[...] += product(refs)

        @pl.when(k == nk - 1)
        def _():
            finish(acc if from_ref else acc[...], ex_refs, o_refs, first_rows)

    res = pl.pallas_call(
        body, out_shape=[o for o, _ in outs], grid=grid,
        in_specs=list(ab_specs) + [s for _, s in extras], out_specs=[s for _, s in outs],
        scratch_shapes=[pltpu.VMEM(acc_shape, F32)] if use_acc else [], name=name, compiler_params=_cp(3),
    )(*[t for pair in ab for t in pair], *[e for e, _ in extras])
    return res[0] if n_out == 1 else res


def _finish_sum(r, ex_refs, o_refs, first_rows):
    del first_rows
    for e in ex_refs:
        r = r + e[...]
    o_refs[0][...] = r.astype(o_refs[0].dtype)


def _finish_sum_norm(r, ex_refs, o_refs, first_rows):
    del first_rows
    for e in ex_refs[:-1]:
        r = r + e[...]
    o_refs[0][...] = r
    o_refs[1][...] = ((r * lax.rsqrt(jnp.mean(r * r, axis=-1, keepdims=True) + EPS)) * ex_refs[-1][...]).astype(BF16)


_EPI_ROWS = 16


def _finish_rms_bwd(r_ref, ex_refs, o_refs, first_rows):
    x_ref, g_ref, dres_ref = ex_refs
    dx_ref, dxb_ref, dg_ref = o_refs

    @pl.when(first_rows)
    def _():
        dg_ref[...] = jnp.zeros_like(dg_ref)

    gv = g_ref[...]
    inv_d = 1.0 / r_ref.shape[1]

    def step(i, dg_acc):
        groups = [pl.ds(pl.multiple_of(i * (2 * _EPI_ROWS) + u * _EPI_ROWS, _EPI_ROWS), _EPI_ROWS) for u in range(2)]
        sums = []
        for rows in groups:
            r, xf = r_ref[rows, :], x_ref[rows, :]
            sums.append((jnp.sum(xf * xf, axis=-1, keepdims=True), jnp.sum((r * gv) * xf, axis=-1, keepdims=True)))
        for rows, (sxx, sax) in zip(groups, sums):
            r, xf = r_ref[rows, :], x_ref[rows, :]
            rs = lax.rsqrt(sxx * inv_d + EPS)
            dg_acc = dg_acc + _psum8(r * (xf * rs))
            dx = rs * (r * gv) - xf * (rs * rs * (sax * rs * inv_d)) + dres_ref[rows, :]
            dx_ref[rows, :] = dx
            dxb_ref[rows, :] = dx.astype(BF16)
        return dg_acc

    dg_acc = lax.fori_loop(0, r_ref.shape[0] // (2 * _EPI_ROWS), step, jnp.zeros((SUB, r_ref.shape[1]), F32))
    dg_ref[...] += jnp.sum(dg_acc, axis=0, keepdims=True)


def _rms_bwd_io(M, tm, x, g, dres):
    rows = pl.BlockSpec((tm, D), lambda m, n, k: (m, 0))
    vec = pl.BlockSpec((1, D), lambda m, n, k: (0, 0))
    return ([(x, rows), (g, vec), (dres, rows)],
            [(SDS((M, D), F32), rows), (SDS((M, D), BF16), rows), (SDS((1, D), F32), vec)])


_K_WHOLE = 3072


def _mm_nn(a, b, *, out_dtype, name, bias=None, add=None, norm=None):
    M, K = a.shape
    tk = K if K <= _K_WHOLE else _tile(K, 1024)
    if K <= 1024 and norm is None:
        tm = _tile(M, 2048 if add is None and out_dtype == BF16 else 1024)
    else:
        tm = _tile(M, 512)
    if b.ndim == 3:
        nb, _, bw = b.shape
        N, tn, nn = nb * bw, bw, nb
        b_spec = pl.BlockSpec((None, tk, bw), lambda m, n, k: (_perm2(n), k, 0))
    else:
        N = b.shape[1]
        tn = _tile(N, 1024)
        nn = N // tn
        b_spec = pl.BlockSpec((tk, tn), lambda m, n, k: (k, n))
    tile = pl.BlockSpec((tm, tn), lambda m, n, k: (m, n))
    vec = pl.BlockSpec((1, tn), lambda m, n, k: (0, n))
    extras = ([] if bias is None else [(bias, vec)]) + ([] if add is None else [(add, tile)])
    outs, finish = [(SDS((M, N), out_dtype), tile)], _finish_sum
    if norm is not None:
        assert tn == N == D and out_dtype == F32
        extras.append((norm, vec))
        outs, finish = outs + [(SDS((M, N), BF16), tile)], _finish_sum_norm
    return _mm_call(name, (M // tm, nn, K // tk), [(a, b)], [pl.BlockSpec((tm, tk), lambda m, n, k: (m, k)), b_spec],
                    _NN, (tm, tn), extras, outs, finish)


def _mm_nt(a, b, *, out_dtype, name, add=None, rms=None):
    M, N = a.shape
    if b.ndim == 3:
        nb, Ko, bw = b.shape
        tm = _tile(M, 1024)
        tn, tk, nk = _tile(Ko, 1024), bw, nb
        b_spec = pl.BlockSpec((None, tn, bw), lambda m, n, k: (_perm2(k), n, 0))
    else:
        Ko = b.shape[0]
        tk = N if N <= _K_WHOLE else _tile(N, 1024)
        if N <= 1024 and rms is None:
            tm = _tile(M, 2048 if add is None and out_dtype == BF16 else 1024)
        else:
            tm = _tile(M, 512)
        tn = _tile(Ko, 1024)
        nk = N // tk
        b_spec = pl.BlockSpec((tn, tk), lambda m, n, k: (n, k))
    tile = pl.BlockSpec((tm, tn), lambda m, n, k: (m, n))
    extras = [] if add is None else [(add, tile)]
    outs, finish = [(SDS((M, Ko), out_dtype), tile)], _finish_sum
    if rms is not None:
        assert tn == Ko == D and add is None
        (extras, outs), finish = _rms_bwd_io(M, tm, *rms), _finish_rms_bwd
    return _mm_call(name, (M // tm, Ko // tn, nk), [(a, b)], [pl.BlockSpec((tm, tk), lambda m, n, k: (m, k)), b_spec],
                    _NT, (tm, tn), extras, outs, finish, from_ref=rms is not None)


def _mm_nt_cols(parts, b, *, name, rms):
    M = parts[0].shape[0]
    tm = _tile(M, 512)
    specs, off = [], 0
    for p in parts:
        w = p.shape[1]
        assert off % w == 0
        specs.append(pl.BlockSpec((tm, w), lambda m, n, k: (m, 0)))
        specs.append(pl.BlockSpec((D, w), functools.partial(lambda m, n, k, o: (0, o), o=off // w)))
        off += w
    extras, outs = _rms_bwd_io(M, tm, *rms)
    return _mm_call(name, (M // tm, 1, 1), [(p, b) for p in parts], specs, _NT, (tm, D), extras, outs, _finish_rms_bwd,
                    from_ref=True)


def _mm_tn(a, b, *, out_dtype, name, blocks=None):
    S, Ka = a.shape
    Nb = b.shape[1]
    tm = _tile(Ka, 1024)
    if blocks is not None:
        bw = blocks
        tn, nn = bw, Nb // bw
        out = (SDS((nn, Ka, bw), out_dtype), pl.BlockSpec((None, tm, bw), lambda m, n, k: (_perm2(n), m, 0)))
    else:
        tn = _tile(Nb, 1024)
        nn = Nb // tn
        out = (SDS((Ka, Nb), out_dtype), pl.BlockSpec((tm, tn), lambda m, n, k: (m, n)))
    steps = (Ka // tm) * nn
    tk = _tile(S, 4096 if steps >= 4 else 2048 if steps >= 2 else 1024)
    return _mm_call(name, (Ka // tm, nn, S // tk), [(a, b)],
                    [pl.BlockSpec((tk, tm), lambda m, n, k: (k, m)), pl.BlockSpec((tk, tn), lambda m, n, k: (k, n))],
                    _TN, (tm, tn), [], [out], _finish_sum)


def _row(tm, c):
    return pl.BlockSpec((tm, c), lambda i: (i, 0))


def _full(shape):
    nd = len(shape)
    return pl.BlockSpec(shape, lambda i: (0,) * nd)


def _rms_fwd(x, g, name):
    S = x.shape[0]
    tm = min(S, TM_ROW)

    def body(x_ref, g_ref, o_ref):
        xf = x_ref[...]
        r = lax.rsqrt(jnp.mean(xf * xf, axis=-1, keepdims=True) + EPS)
        o_ref[...] = ((xf * r) * g_ref[...]).astype(BF16)

    return pl.pallas_call(body, out_shape=SDS((S, D), BF16), grid=(S // tm,), in_specs=[_row(tm, D), _full((1, D))],
                          out_specs=_row(tm, D), name=name, compiler_params=_cp(1))(x, g)


def _rms_bwd(x, g, dn, dres, name):
    S = x.shape[0]
    tm = min(S, TM_ROW)
    want_dx = dres is not None

    def body(x_ref, g_ref, dn_ref, *rest):
        i = pl.program_id(0)
        dg_ref = rest[-1]

        @pl.when(i == 0)
        def _():
            dg_ref[...] = jnp.zeros_like(dg_ref)

        xf = x_ref[...]
        r = lax.rsqrt(jnp.mean(xf * xf, axis=-1, keepdims=True) + EPS)
        y = xf * r
        dn_v = dn_ref[...]
        dg_ref[...] += jnp.sum(dn_v * y, axis=0, keepdims=True)
        if want_dx:
            dres_ref, dx_ref, dxb_ref = rest[0], rest[1], rest[2]
            dy = dn_v * g_ref[...]
            dx = r * (dy - y * jnp.mean(dy * y, axis=-1, keepdims=True)) + dres_ref[...]
            dx_ref[...] = dx
            dxb_ref[...] = dx.astype(BF16)

    ins = [x, g, dn] + ([dres] if want_dx else [])
    in_specs = [_row(tm, D), _full((1, D)), _row(tm, D)] + ([_row(tm, D)] if want_dx else [])
    outs = ([SDS((S, D), F32), SDS((S, D), BF16)] if want_dx else []) + [SDS((1, D), F32)]
    out_specs = ([_row(tm, D), _row(tm, D)] if want_dx else []) + [_full((1, D))]
    return pl.pallas_call(body, out_shape=outs, grid=(S // tm,), in_specs=in_specs, out_specs=out_specs, name=name,
                          compiler_params=_cp(1))(*ins)


def _loss_head(x, g, tgt):
    S = x.shape[0]
    tm = min(S, TM_ROW)

    def body(x_ref, g_ref, t_ref, loss_ref, dx_ref, dxb_ref, dg_ref):
        i = pl.program_id(0)

        @pl.when(i == 0)
        def _():
            loss_ref[...] = jnp.zeros_like(loss_ref)
            dg_ref[...] = jnp.zeros_like(dg_ref)

        xf = x_ref[...]
        r = lax.rsqrt(jnp.mean(xf * xf, axis=-1, keepdims=True) + EPS)
        y = xf * r
        gv = g_ref[...]
        err = y * gv - t_ref[...]
        per_row = jnp.mean(err * err, axis=-1, keepdims=True)
        loss_ref[...] += 0.5 * jnp.sum(per_row, axis=0, keepdims=True)
        dn_v = err * (1.0 / D)
        dg_ref[...] += jnp.sum(dn_v * y, axis=0, keepdims=True)
        dy = dn_v * gv
        dx = r * (dy - y * jnp.mean(dy * y, axis=-1, keepdims=True))
        dx_ref[...] = dx
        dxb_ref[...] = dx.astype(BF16)

    return pl.pallas_call(
        body, out_shape=[SDS((1, 1), F32), SDS((S, D), F32), SDS((S, D), BF16), SDS((1, D), F32)], grid=(S // tm,),
        in_specs=[_row(tm, D), _full((1, D)), _row(tm, D)],
        out_specs=[_full((1, 1)), _row(tm, D), _row(tm, D), _full((1, D))], name="loss_head", compiler_params=_cp(1),
    )(x, g, tgt)


def _softmax_rows(s):
    m = jnp.max(s, axis=-1, keepdims=True)
    e = jnp.exp(s - m)
    return e / jnp.sum(e, axis=-1, keepdims=True)


def _attn_fwd(q, k, v, name):
    S = q.shape[0]
    tm = min(S, TM_ROW)
    scale = XA_HD ** -0.5

    def body(q_ref, k_ref, v_ref, o_ref):
        for h in range(XA_HEADS):
            sl = slice(h * XA_HD, (h + 1) * XA_HD)
            s = lax.dot_general(q_ref[:, sl], k_ref[:, sl], _NT, preferred_element_type=F32) * scale
            p = _softmax_rows(s)
            o_ref[:, sl] = lax.dot_general(p.astype(BF16), v_ref[:, sl], _NN, preferred_element_type=F32).astype(BF16)

    return pl.pallas_call(body, out_shape=SDS((S, D), BF16), grid=(S // tm,),
                          in_specs=[_row(tm, D), _full((N_MEM, D)), _full((N_MEM, D))], out_specs=_row(tm, D),
                          name=name, compiler_params=_cp(1))(q, k, v)


def _attn_bwd(q, k, v, do, name):
    S = q.shape[0]
    tm = min(S, TM_ROW)
    scale = XA_HD ** -0.5

    def body(q_ref, k_ref, v_ref, do_ref, dq_ref, dk_ref, dv_ref):
        i = pl.program_id(0)

        @pl.when(i == 0)
        def _():
            dk_ref[...] = jnp.zeros_like(dk_ref)
            dv_ref[...] = jnp.zeros_like(dv_ref)

        for h in range(XA_HEADS):
            sl = slice(h * XA_HD, (h + 1) * XA_HD)
            qh, kh, vh, doh = q_ref[:, sl], k_ref[:, sl], v_ref[:, sl], do_ref[:, sl]
            s = lax.dot_general(qh, kh, _NT, preferred_element_type=F32) * scale
            p = _softmax_rows(s)
            pb = p.astype(BF16)
            dv_ref[:, sl] += lax.dot_general(pb, doh, _TN, preferred_element_type=F32)
            dp = lax.dot_general(doh, vh, _NT, preferred_element_type=F32)
            ds = (p * (dp - jnp.sum(dp * p, axis=-1, keepdims=True)) * scale).astype(BF16)
            dq_ref[:, sl] = lax.dot_general(ds, kh, _NN, preferred_element_type=F32).astype(BF16)
            dk_ref[:, sl] += lax.dot_general(ds, qh, _TN, preferred_element_type=F32)

    return pl.pallas_call(
        body, out_shape=[SDS((S, D), BF16), SDS((N_MEM, D), F32), SDS((N_MEM, D), F32)], grid=(S // tm,),
        in_specs=[_row(tm, D), _full((N_MEM, D)), _full((N_MEM, D)), _row(tm, D)],
        out_specs=[_row(tm, D), _full((N_MEM, D)), _full((N_MEM, D))], name=name, compiler_params=_cp(1),
    )(q, k, v, do)


def _sigmoid(x):
    return 1.0 / (1.0 + jnp.exp(-x))


def _ln_silu_fwd(cv, g, b):
    S = cv.shape[0]
    tm = min(S, TM_ROW)

    def body(x_ref, g_ref, b_ref, o_ref):
        xf = x_ref[...]
        mu = jnp.mean(xf, axis=-1, keepdims=True)
        xc = xf - mu
        rstd = lax.rsqrt(jnp.mean(xc * xc, axis=-1, keepdims=True) + EPS)
        ln = (xc * rstd) * g_ref[...] + b_ref[...]
        o_ref[...] = (ln * _sigmoid(ln)).astype(BF16)

    return pl.pallas_call(body, out_shape=SDS((S, D), BF16), grid=(S // tm,),
                          in_specs=[_row(tm, D), _full((1, D)), _full((1, D))], out_specs=_row(tm, D),
                          name="ln_silu_fwd", compiler_params=_cp(1))(cv, g, b)


def _ln_silu_bwd(ds, cv, g, b, dx):
    S = cv.shape[0]
    tm = min(S, TM_ROW)

    def body(ds_ref, x_ref, g_ref, b_ref, dx_ref, dcv_ref, dg_ref, db_ref, db2_ref):
        i = pl.program_id(0)

        @pl.when(i == 0)
        def _():
            dg_ref[...] = jnp.zeros_like(dg_ref)
            db_ref[...] = jnp.zeros_like(db_ref)
            db2_ref[...] = jnp.zeros_like(db2_ref)

        xf = x_ref[...]
        mu = jnp.mean(xf, axis=-1, keepdims=True)
        xc = xf - mu
        rstd = lax.rsqrt(jnp.mean(xc * xc, axis=-1, keepdims=True) + EPS)
        xhat = xc * rstd
        gv = g_ref[...]
        ln = xhat * gv + b_ref[...]
        sg = _sigmoid(ln)
        dln = ds_ref[...].astype(F32) * (sg + ln * sg * (1.0 - sg))
        dg_ref[...] += jnp.sum(dln * xhat, axis=0, keepdims=True)
        db_ref[...] += jnp.sum(dln, axis=0, keepdims=True)
        db2_ref[...] += jnp.sum(dx_ref[...], axis=0, keepdims=True)
        dxh = dln * gv
        dcv_ref[...] = rstd * (dxh - jnp.mean(dxh, axis=-1, keepdims=True)
                               - xhat * jnp.mean(dxh * xhat, axis=-1, keepdims=True))

    return pl.pallas_call(
        body, out_shape=[SDS((S, D), F32), SDS((1, D), F32), SDS((1, D), F32), SDS((1, D), F32)], grid=(S // tm,),
        in_specs=[_row(tm, D), _row(tm, D), _full((1, D)), _full((1, D)), _row(tm, D)],
        out_specs=[_row(tm, D), _full((1, D)), _full((1, D)), _full((1, D))], name="ln_silu_bwd",
        compiler_params=_cp(1),
    )(ds, cv, g, b, dx)


_GELU_C, _GELU_K = 0.7978845608028654, 0.044715


def _gelu(x, with_grad=False):
    x2 = x * x
    t = jnp.tanh(_GELU_C * (x + _GELU_K * x * x2))
    gel = 0.5 * x * (1.0 + t)
    if not with_grad:
        return gel
    return gel, 0.5 * (1.0 + t) + 0.5 * x * (1.0 - t * t) * (_GELU_C * (1.0 + 3.0 * _GELU_K * x2))


def _expm1(x):
    poly = x * (1.0 + x * (0.5 + x * (1.0 / 6.0 + x * (1.0 / 24.0 + x * (1.0 / 120.0)))))
    return jnp.where(jnp.abs(x) < 0.05, poly, jnp.exp(x) - 1.0)


def _softplus(x):
    return jnp.maximum(x, 0.0) + jnp.log1p(jnp.exp(-jnp.abs(x)))


_SCAN_UNROLL = 8
_RB = 32
_HB = 16


def _sub_blocks(n_rows, n_lanes, fn):
    def step(idx, c):
        r0 = pl.multiple_of(idx * _RB, _RB)
        for lt in range(n_lanes // LANE):
            fn(r0, lt)
        return c

    lax.fori_loop(0, n_rows // _RB, step, 0)


def _lanes(lt):
    return pl.ds(lt * LANE, LANE)


def _psum8(x):
    parts = [x[i * SUB:(i + 1) * SUB] for i in range(x.shape[0] // SUB)]
    return functools.reduce(lambda p, q: p + q, parts)


def _scan_fwd(a_s, b_s, out_ref, carry_ref, n_groups):
    row = lax.broadcasted_iota(jnp.int32, (SUB, LANE), 0)
    U = _SCAN_UNROLL

    def step(gi, carry):
        base = gi * (SUB * U)
        parts = []
        for u in range(U):
            i = pl.multiple_of(base + u * SUB, SUB)
            a8, b8 = a_s[pl.ds(i, SUB), :], b_s[pl.ds(i, SUB), :]
            for s in (1, 2, 4):
                a_sh = jnp.where(row >= s, pltpu.roll(a8, s, 0), 1.0)
                b_sh = jnp.where(row >= s, pltpu.roll(b8, s, 0), 0.0)
                b8 = a8 * b_sh + b8
                a8 = a8 * a_sh
            parts.append((i, a8, b8))
        for i, a8, b8 in parts:
            h8 = a8 * carry + b8
            out_ref[pl.ds(i, SUB), :] = h8
            carry = jnp.broadcast_to(h8[SUB - 1:SUB, :], (SUB, LANE))
        return carry

    carry_ref[...] = lax.fori_loop(0, n_groups // U, step, carry_ref[...])


def _scan_bwd(a_s, b_s, out_ref, carry_ref, n_groups):
    row = lax.broadcasted_iota(jnp.int32, (SUB, LANE), 0)
    U = _SCAN_UNROLL

    def step(gi, carry):
        base = (n_groups // U - 1 - gi) * (SUB * U)
        parts = []
        for u in reversed(range(U)):
            i = pl.multiple_of(base + u * SUB, SUB)
            a8, b8 = a_s[pl.ds(i, SUB), :], b_s[pl.ds(i, SUB), :]
            for s in (1, 2, 4):
                a_sh = jnp.where(row < SUB - s, pltpu.roll(a8, SUB - s, 0), 1.0)
                b_sh = jnp.where(row < SUB - s, pltpu.roll(b8, SUB - s, 0), 0.0)
                b8 = a8 * b_sh + b8
                a8 = a8 * a_sh
            parts.append((i, a8, b8))
        for i, a8, b8 in parts:
            h8 = a8 * carry + b8
            out_ref[pl.ds(i, SUB), :] = h8
            carry = jnp.broadcast_to(h8[0:1, :], (SUB, LANE))
        return carry

    carry_ref[...] = lax.fori_loop(0, n_groups // U, step, carry_ref[...])


def _rglru_pre(xr, wgx_ref, bgx_ref, wga_ref, bga_ref, lam_ref):
    xrb = xr.astype(BF16)
    wgx, wga = wgx_ref[0].astype(BF16), wga_ref[0].astype(BF16)
    gx = _sigmoid(lax.dot_general(xrb, wgx, _NN, preferred_element_type=F32) + bgx_ref[...])
    ga = _sigmoid(lax.dot_general(xrb, wga, _NN, preferred_element_type=F32) + bga_ref[...])
    sp = _softplus(-lam_ref[...])
    log_a = -C_RG * ga * sp
    a = jnp.exp(log_a)
    mult = jnp.sqrt(-_expm1(2.0 * log_a))
    return gx, ga, sp, a, mult, xrb, wgx, wga


def _a_specs():
    vec = pl.BlockSpec((1, HD_A), lambda c, j: (0, c))
    mat = pl.BlockSpec((1, HD_A, HD_A), lambda c, j: (c, 0, 0))
    return [pl.BlockSpec((CONV_A, HD_A), lambda c, j: (0, c)), vec, mat, vec, mat, vec, vec]


def _a_fwd(zp, conv_w, conv_b, wgx, bgx, wga, bga, lam):
    S = zp.shape[0]
    R, nt = R_RGLRU, D // HD_A
    H = SUB

    def body(zg_ref, zr_ref, cw_ref, cb_ref, wgx_ref, bgx_ref, wga_ref, bga_ref, lam_ref, ya_ref, h_ref,
             ext, a_s, b_s, hc):
        j = pl.program_id(1)

        @pl.when(j == 0)
        def _():
            ext[0:H, :] = jnp.zeros((H, HD_A), F32)
            hc[...] = jnp.zeros_like(hc)

        ext[H:H + R, :] = zr_ref[...].astype(F32)
        xr = cb_ref[...]
        for k in range(CONV_A):
            xr = xr + cw_ref[k:k + 1, :] * ext[pl.ds(H - (CONV_A - 1 - k), R), :]
        gx, _, _, a, mult, _, _, _ = _rglru_pre(xr, wgx_ref, bgx_ref, wga_ref, bga_ref, lam_ref)
        a_s[...] = a
        b_s[...] = mult * (gx * xr)
        _scan_fwd(a_s, b_s, h_ref, hc, R // SUB)
        ya_ref[...] = (_gelu(zg_ref[...].astype(F32)) * h_ref[...]).astype(BF16)
        ext[0:H, :] = ext[R:R + H, :]

    return pl.pallas_call(
        body, out_shape=[SDS((S, D + D // 2), BF16), SDS((S, D), F32)], grid=(nt, S // R),
        in_specs=[pl.BlockSpec((R, HD_A), lambda c, j: (j, c)), pl.BlockSpec((R, HD_A), lambda c, j: (j, nt + c))]
        + _a_specs(),
        out_specs=[pl.BlockSpec((R, HD_A), lambda c, j: (j, c)), pl.BlockSpec((R, HD_A), lambda c, j: (j, c))],
        scratch_shapes=[pltpu.VMEM((H + R, HD_A), F32), pltpu.VMEM((R, HD_A), F32), pltpu.VMEM((R, HD_A), F32),
                        pltpu.VMEM((SUB, HD_A), F32)],
        name="rglru_fwd", compiler_params=_cp(2),
    )(zp, zp, conv_w, conv_b, wgx, bgx, wga, bga, lam)


def _a_bwd(dyab, zp, h, conv_w, conv_b, wgx, bgx, wga, bga, lam):
    S = zp.shape[0]
    R, nt, nch = R_RGLRU, D // HD_A, S // R_RGLRU
    H = SUB

    def rows(c, j):
        return (nch - 1 - j, c)

    def rows_rec(c, j):
        return (nch - 1 - j, nt + c)

    def halo(c, j):
        return (jnp.maximum((nch - 1 - j) * (R // H) - 1, 0), c)

    def halo_z(c, j):
        return (jnp.maximum((nch - 1 - j) * (R // _HB) - 1, 0), nt + c)

    def body(dy_ref, zg_ref, zr_ref, zh_ref, h_ref, hh_ref, cw_ref, cb_ref, wgx_ref, bgx_ref, wga_ref, bga_ref,
             lam_ref, dzg_ref, dzr_ref, dcw_ref, dcb_ref, dwgx_ref, dbgx_ref, dwga_ref, dbga_ref, dlam_ref,
             ext_z, ext_h, ext_mu, ext_d, a_s, b_s, muc):
        j = pl.program_id(1)
        first_chunk = (nch - 1 - j) == 0

        @pl.when(j == 0)
        def _():
            ext_mu[R:R + H, :] = jnp.zeros((H, HD_A), F32)
            ext_d[R:R + H, :] = jnp.zeros((H, HD_A), F32)
            muc[...] = jnp.zeros_like(muc)
            for r in (dcw_ref, dcb_ref, dwgx_ref, dbgx_ref, dwga_ref, dbga_ref, dlam_ref):
                r[...] = jnp.zeros_like(r)

        zg = zg_ref[...].astype(F32)
        ext_z[0:H, :] = jnp.where(first_chunk, 0.0, zh_ref[_HB - H:_HB, :].astype(F32))
        ext_z[H:H + R, :] = zr_ref[...].astype(F32)
        ext_h[0:H, :] = jnp.where(first_chunk, 0.0, hh_ref[...])
        ext_h[H:H + R, :] = h_ref[...]
        xr = cb_ref[...]
        for k in range(CONV_A):
            xr = xr + cw_ref[k:k + 1, :] * ext_z[pl.ds(H - (CONV_A - 1 - k), R), :]
        gx, ga, sp, a, mult, xrb, wgxb, wgab = _rglru_pre(xr, wgx_ref, bgx_ref, wga_ref, bga_ref, lam_ref)
        gel, dgel = _gelu(zg, with_grad=True)
        dy = dy_ref[...].astype(F32)
        dh = dy * gel
        dzg_ref[...] = (dy * h_ref[...] * dgel).astype(BF16)
        a_s[...] = a
        b_s[...] = a * dh
        _scan_bwd(a_s, b_s, ext_mu, muc, R // SUB)
        lam_t = dh + ext_mu[pl.ds(1, R), :]
        ext_mu[R:R + H, :] = ext_mu[0:H, :]
        da = lam_t * ext_h[pl.ds(H - 1, R), :]
        gxr = gx * xr
        dlog_a = da * a - (lam_t * gxr) * (a * a) / mult
        dgx = lam_t * mult * xr
        dxr = lam_t * mult * gx
        lam_v = lam_ref[...]
        dlam_ref[...] += jnp.sum(dlog_a * ga, axis=0, keepdims=True) * (C_RG * _sigmoid(-lam_v))
        dpa = (dlog_a * (-C_RG * sp)) * ga * (1.0 - ga)
        dpx = dgx * gx * (1.0 - gx)
        dbga_ref[...] += jnp.sum(dpa, axis=0, keepdims=True)
        dbgx_ref[...] += jnp.sum(dpx, axis=0, keepdims=True)
        dpab, dpxb = dpa.astype(BF16), dpx.astype(BF16)
        dwga_ref[0] += lax.dot_general(xrb, dpab, _TN, preferred_element_type=F32)
        dwgx_ref[0] += lax.dot_general(xrb, dpxb, _TN, preferred_element_type=F32)
        dxr = (dxr + lax.dot_general(dpab, wgab, _NT, preferred_element_type=F32)
               + lax.dot_general(dpxb, wgxb, _NT, preferred_element_type=F32))
        dcb_ref[...] += jnp.sum(dxr, axis=0, keepdims=True)
        ext_d[0:R, :] = dxr
        dzr = jnp.zeros((R, HD_A), F32)
        for k in range(CONV_A):
            sh = CONV_A - 1 - k
            dcw_ref[k:k + 1, :] += jnp.sum(dxr * ext_z[pl.ds(H - sh, R), :], axis=0, keepdims=True)
            dzr = dzr + cw_ref[k:k + 1, :] * ext_d[pl.ds(sh, R), :]
        dzr_ref[...] = dzr.astype(BF16)
        ext_d[R:R + H, :] = ext_d[0:H, :]

    vec_o = pl.BlockSpec((1, HD_A), lambda c, j: (0, c))
    mat_o = pl.BlockSpec((1, HD_A, HD_A), lambda c, j: (c, 0, 0))
    return pl.pallas_call(
        body,
        out_shape=[SDS((S, D), BF16), SDS((S, D), BF16), SDS((CONV_A, D), F32), SDS((1, D), F32),
                   SDS((nt, HD_A, HD_A), F32), SDS((1, D), F32), SDS((nt, HD_A, HD_A), F32), SDS((1, D), F32),
                   SDS((1, D), F32)],
        grid=(nt, nch),
        in_specs=[pl.BlockSpec((R, HD_A), rows), pl.BlockSpec((R, HD_A), rows), pl.BlockSpec((R, HD_A), rows_rec),
                  pl.BlockSpec((_HB, HD_A), halo_z), pl.BlockSpec((R, HD_A), rows),
                  pl.BlockSpec((H, HD_A), halo)] + _a_specs(),
        out_specs=[pl.BlockSpec((R, HD_A), rows), pl.BlockSpec((R, HD_A), rows),
                   pl.BlockSpec((CONV_A, HD_A), lambda c, j: (0, c)), vec_o, mat_o, vec_o, mat_o, vec_o, vec_o],
        scratch_shapes=[pltpu.VMEM((H + R, HD_A), F32), pltpu.VMEM((H + R, HD_A), F32), pltpu.VMEM((R + H, HD_A), F32),
                        pltpu.VMEM((R + H, HD_A), F32), pltpu.VMEM((R, HD_A), F32), pltpu.VMEM((R, HD_A), F32),
                        pltpu.VMEM((SUB, HD_A), F32)],
        name="rglru_bwd", compiler_params=_cp(2),
    )(dyab, zp, zp, zp, h, h, conv_w, conv_b, wgx, bgx, wga, bga, lam)


_POOL_H = 16
_POOL_T0 = 2 * D // HD_A
_POOL_Y0 = D // HD_A


def _window_sum(lv, n, lo, rows, g, ahead):
    base = 0 if ahead else SUB
    cur, win = lv[0], None
    for i, s in enumerate((1, 2, 4, 8)):
        val = cur[pl.ds(base, n), :] + cur[pl.ds(base + (s if ahead else -s), n), :]
        sel = val[lo:lo + rows]
        win = sel if win is None else jnp.where(g >= i, sel, win)
        if i < 3:
            lv[i + 1][pl.ds(base, n), :] = val
            cur = lv[i + 1]
    return win


def _pool_width(g):
    return jnp.where(g == 0, 2.0, jnp.where(g == 1, 4.0, jnp.where(g == 2, 8.0, 16.0)))


def _b_fwd(zp, yab, wg, bg, sc):
    S = zp.shape[0]
    R, H = min(S, R_POOL), _POOL_H

    def body(z_ref, wg_ref, bg_ref, sc_ref, yab_in, yb_ref, *lv):
        del yab_in
        g, j = pl.program_id(0), pl.program_id(1)

        @pl.when(j == 0)
        def _():
            for r in lv:
                r[0:SUB, :] = jnp.zeros((SUB, HD_A), F32)
            lv[0][SUB:SUB + H, :] = jnp.zeros((H, HD_A), F32)

        u = z_ref[...].astype(F32)
        lv[0][SUB + H:SUB + H + R, :] = u
        t1 = (j * R + 1 + lax.broadcasted_iota(jnp.int32, (R, HD_A), 0)).astype(F32)
        p = _window_sum(lv, H + R, H, R, g, False) / jnp.minimum(t1, _pool_width(g)) - u
        lin = lax.dot_general(p.astype(BF16), wg_ref[0].astype(BF16), _NN, preferred_element_type=F32) + bg_ref[...]
        yb_ref[...] = (lin * sc_ref[...]).astype(BF16)
        lv[0][SUB:SUB + H, :] = lv[0][SUB + R:SUB + R + H, :]

    vec = pl.BlockSpec((1, HD_A), lambda g, j: (0, g))
    return pl.pallas_call(
        body, out_shape=SDS(yab.shape, yab.dtype), grid=(len(POOL_WINDOWS), S // R),
        in_specs=[pl.BlockSpec((R, HD_A), lambda g, j: (j, _POOL_T0 + g)),
                  pl.BlockSpec((1, HD_A, HD_A), lambda g, j: (g, 0, 0)), vec, vec, pl.BlockSpec(memory_space=pl.ANY)],
        out_specs=pl.BlockSpec((R, HD_A), lambda g, j: (j, _POOL_Y0 + g)),
        scratch_shapes=[pltpu.VMEM((SUB + H + R, HD_A), F32)] * 4, input_output_aliases={4: 0},
        name="pool_fwd", compiler_params=_cp(2),
    )(zp, wg, bg, sc, yab)


def _b_bwd(dyab, zp, wg, bg, sc):
    S = zp.shape[0]
    R, H, ng = min(S, R_POOL), _POOL_H, len(POOL_WINDOWS)
    nch = S // R

    def body(dy_ref, z_ref, zh_ref, wg_ref, bg_ref, sc_ref, dz_ref, dwg_ref, dbg_ref, dsc_ref, *scratch):
        lu, lq = scratch[:4], scratch[4:]
        g, j = pl.program_id(0), pl.program_id(1)
        jj = nch - 1 - j

        @pl.when(j == 0)
        def _():
            for r in lu:
                r[0:SUB, :] = jnp.zeros((SUB, HD_A), F32)
            for r in lq:
                r[R + H:R + H + SUB, :] = jnp.zeros((SUB, HD_A), F32)
            lq[0][R:R + H, :] = jnp.zeros((H, HD_A), F32)
            for r in (dwg_ref, dbg_ref, dsc_ref):
                r[...] = jnp.zeros_like(r)

        u = z_ref[...].astype(F32)
        lu[0][SUB:SUB + H, :] = jnp.where(jj == 0, 0.0, zh_ref[...].astype(F32))
        lu[0][SUB + H:SUB + H + R, :] = u
        t1 = (jj * R + 1 + lax.broadcasted_iota(jnp.int32, (R, HD_A), 0)).astype(F32)
        cnt = jnp.minimum(t1, _pool_width(g))
        pb = (_window_sum(lu, H + R, H, R, g, False) / cnt - u).astype(BF16)
        wgb = wg_ref[0].astype(BF16)
        lin = lax.dot_general(pb, wgb, _NN, preferred_element_type=F32) + bg_ref[...]
        dy = dy_ref[...].astype(F32)
        dsc_ref[...] += jnp.sum(dy * lin, axis=0, keepdims=True)
        dlin = dy * sc_ref[...]
        dbg_ref[...] += jnp.sum(dlin, axis=0, keepdims=True)
        dlb = dlin.astype(BF16)
        dwg_ref[0] += lax.dot_general(pb, dlb, _TN, preferred_element_type=F32)
        dp = lax.dot_general(dlb, wgb, _NT, preferred_element_type=F32)
        lq[0][0:R, :] = dp / cnt
        dz_ref[...] = (_window_sum(lq, R + H, 0, R, g, True) - dp).astype(BF16)
        lq[0][R:R + H, :] = lq[0][0:H, :]

    vec = pl.BlockSpec((1, HD_A), lambda g, j: (0, g))
    mat = pl.BlockSpec((1, HD_A, HD_A), lambda g, j: (g, 0, 0))
    return pl.pallas_call(
        body, out_shape=[SDS((S, D // 2), BF16), SDS((ng, HD_A, HD_A), F32), SDS((1, D // 2), F32),
                         SDS((1, D // 2), F32)],
        grid=(ng, nch),
        in_specs=[pl.BlockSpec((R, HD_A), lambda g, j: (nch - 1 - j, _POOL_Y0 + g)),
                  pl.BlockSpec((R, HD_A), lambda g, j: (nch - 1 - j, _POOL_T0 + g)),
                  pl.BlockSpec((H, HD_A), lambda g, j: (jnp.maximum((nch - 1 - j) * (R // H) - 1, 0), _POOL_T0 + g)),
                  mat, vec, vec],
        out_specs=[pl.BlockSpec((R, HD_A), lambda g, j: (nch - 1 - j, g)), mat, vec, vec],
        scratch_shapes=[pltpu.VMEM((SUB + H + R, HD_A), F32)] * 8,
        name="pool_bwd", compiler_params=_cp(2),
    )(dyab, zp, zp, wg, bg, sc)


_CW_F = 768


def _f_fwd(hp, w, b, name):
    S = hp.shape[0]
    R, H, cw = min(S, R_FFN), SUB, _CW_F
    nlt = cw // LANE

    def body(h_ref, w_ref, b_ref, o_ref, gel_ref, ud_ref, ext):
        j = pl.program_id(1)

        @pl.when(j == 0)
        def _():
            ext[:, 0:H, :] = jnp.zeros((nlt, H, LANE), F32)

        def stage(r0, lt):
            ext[lt, pl.ds(pl.multiple_of(r0 + H, SUB), _RB), :] = h_ref[pl.ds(r0, _RB), _lanes(lt)].astype(F32)

        def main(r0, lt):
            ls = _lanes(lt)
            gp = b_ref[:, ls]
            for k in range(CONV_F):
                gp = gp + w_ref[k:k + 1, ls] * ext[lt, pl.ds(r0 + (H - (CONV_F - 1 - k)), _RB), :]
            up = h_ref[pl.ds(r0, _RB), _lanes(lt + nlt)].astype(F32)
            gel, dgel = _gelu(gp, with_grad=True)
            rs = pl.ds(r0, _RB)
            o_ref[rs, ls] = (gel * up).astype(BF16)
            gel_ref[rs, ls] = gel.astype(BF16)
            ud_ref[rs, ls] = (up * dgel).astype(BF16)

        _sub_blocks(R, cw, stage)
        _sub_blocks(R, cw, main)
        ext[:, 0:H, :] = ext[:, R:R + H, :]

    tile = pl.BlockSpec((R, cw), lambda c, j: (j, c))
    return pl.pallas_call(
        body, out_shape=[SDS((S, D_FF), BF16)] * 3, grid=(D_FF // cw, S // R),
        in_specs=[pl.BlockSpec((R, 2 * cw), lambda c, j: (j, c)), pl.BlockSpec((CONV_F, cw), lambda c, j: (0, c)),
                  pl.BlockSpec((1, cw), lambda c, j: (0, c))],
        out_specs=[tile] * 3,
        scratch_shapes=[pltpu.VMEM((nlt, H + R, LANE), F32)], name=name, compiler_params=_cp(2),
    )(hp, w, b)


def _f_bwd(dact, hp, gel, ud, w, name):
    S = hp.shape[0]
    R, H, cw = min(S, R_FFN), SUB, _CW_F
    nch = S // R
    nlt = cw // LANE

    def body(da_ref, h_ref, hh_ref, gel_ref, ud_ref, w_ref, dh_ref, dw_ref, db_ref, ext_g, ext_d, acc):
        j = pl.program_id(1)
        jj = nch - 1 - j

        @pl.when(j == 0)
        def _():
            ext_d[:, R:R + H, :] = jnp.zeros((nlt, H, LANE), F32)
            acc[...] = jnp.zeros_like(acc)

        for lt in range(nlt):
            ext_g[lt, 0:H, :] = jnp.where(jj == 0, 0.0, hh_ref[_HB - H:_HB, lt * LANE:(lt + 1) * LANE].astype(F32))

        def stage(r0, lt):
            ext_g[lt, pl.ds(pl.multiple_of(r0 + H, SUB), _RB), :] = h_ref[pl.ds(r0, _RB), _lanes(lt)].astype(F32)

        def first(r0, lt):
            ls, lu, rs = _lanes(lt), _lanes(lt + nlt), pl.ds(r0, _RB)
            da = da_ref[rs, ls].astype(F32)
            dh_ref[rs, lu] = (da * gel_ref[rs, ls].astype(F32)).astype(BF16)
            dgp = da * ud_ref[rs, ls].astype(F32)
            ext_d[lt, rs, :] = dgp
            acc[CONV_F * SUB:(CONV_F + 1) * SUB, ls] += _psum8(dgp)
            for k in range(CONV_F):
                tap = ext_g[lt, pl.ds(r0 + (H - (CONV_F - 1 - k)), _RB), :]
                acc[k * SUB:(k + 1) * SUB, ls] += _psum8(dgp * tap)

        def second(r0, lt):
            ls = _lanes(lt)
            dhg = w_ref[CONV_F - 1:CONV_F, ls] * ext_d[lt, pl.ds(r0, _RB), :]
            for k in range(CONV_F - 1):
                dhg = dhg + w_ref[k:k + 1, ls] * ext_d[lt, pl.ds(r0 + (CONV_F - 1 - k), _RB), :]
            dh_ref[pl.ds(r0, _RB), ls] = dhg.astype(BF16)

        _sub_blocks(R, cw, stage)
        _sub_blocks(R, cw, first)
        _sub_blocks(R, cw, second)
        ext_d[:, R:R + H, :] = ext_d[:, 0:H, :]

        @pl.when(j == nch - 1)
        def _():
            for k in range(CONV_F):
                dw_ref[k:k + 1, :] = jnp.sum(acc[k * SUB:(k + 1) * SUB, :], axis=0, keepdims=True)
            db_ref[...] = jnp.sum(acc[CONV_F * SUB:(CONV_F + 1) * SUB, :], axis=0, keepdims=True)

    rows = lambda c, j: (nch - 1 - j, c)
    return pl.pallas_call(
        body, out_shape=[SDS((S, 2 * D_FF), BF16), SDS((CONV_F, D_FF), F32), SDS((1, D_FF), F32)],
        grid=(D_FF // cw, nch),
        in_specs=[pl.BlockSpec((R, cw), rows), pl.BlockSpec((R, cw), lambda c, j: (nch - 1 - j, 2 * c)),
                  pl.BlockSpec((_HB, cw), lambda c, j: (jnp.maximum((nch - 1 - j) * (R // _HB) - 1, 0), 2 * c)),
                  pl.BlockSpec((R, cw), rows), pl.BlockSpec((R, cw), rows),
                  pl.BlockSpec((CONV_F, cw), lambda c, j: (0, c))],
        out_specs=[pl.BlockSpec((R, 2 * cw), rows), pl.BlockSpec((CONV_F, cw), lambda c, j: (0, c)),
                   pl.BlockSpec((1, cw), lambda c, j: (0, c))],
        scratch_shapes=[pltpu.VMEM((nlt, H + R, LANE), F32), pltpu.VMEM((nlt, R + H, LANE), F32),
                        pltpu.VMEM(((CONV_F + 1) * SUB, cw), F32)], name=name,
        compiler_params=_cp(2),
    )(dact, hp, hp, gel, ud, w)


_CW_C = 256
_H_C = 32


def _c_fwd(h1p, w, b):
    S = h1p.shape[0]
    R, H, cw = R_SEQ, _H_C, _CW_C
    nlt = cw // LANE

    def body(h_ref, w_ref, b_ref, o_ref, ext):
        j = pl.program_id(1)

        @pl.when(j == 0)
        def _():
            ext[:, 0:H, :] = jnp.zeros((nlt, H, LANE), F32)

        def stage(r0, lt):
            rs = pl.ds(r0, _RB)
            gate = h_ref[rs, _lanes(lt + nlt)].astype(F32)
            ext[lt, pl.ds(pl.multiple_of(r0 + H, SUB), _RB), :] = h_ref[rs, _lanes(lt)].astype(F32) * _sigmoid(gate)

        def main(r0, lt):
            ls = _lanes(lt)
            cv = b_ref[:, ls]
            for k in range(CONV_C):
                cv = cv + w_ref[k:k + 1, ls] * ext[lt, pl.ds(r0 + (H - (CONV_C - 1 - k)), _RB), :]
            o_ref[pl.ds(r0, _RB), ls] = cv

        _sub_blocks(R, cw, stage)
        _sub_blocks(R, cw, main)
        ext[:, 0:H, :] = ext[:, R:R + H, :]

    return pl.pallas_call(
        body, out_shape=SDS((S, D), F32), grid=(D // cw, S // R),
        in_specs=[pl.BlockSpec((R, 2 * cw), lambda c, j: (j, c)), pl.BlockSpec((CONV_C, cw), lambda c, j: (0, c)),
                  pl.BlockSpec((1, cw), lambda c, j: (0, c))],
        out_specs=pl.BlockSpec((R, cw), lambda c, j: (j, c)),
        scratch_shapes=[pltpu.VMEM((nlt, H + R, LANE), F32)], name="conf_conv_fwd", compiler_params=_cp(2),
    )(h1p, w, b)


def _c_bwd(dcv, h1p, w):
    S = h1p.shape[0]
    R, H, cw, nch = R_SEQ, _H_C, _CW_C, S // R_SEQ
    nlt = cw // LANE
    a_b, a_val, a_gate = CONV_C * SUB, (CONV_C + 1) * SUB, (CONV_C + 2) * SUB

    def body(dc_ref, h_ref, hh_ref, w_ref, dh_ref, dw_ref, db_ref, db1_ref, ext_u, ext_d, acc):
        j = pl.program_id(1)
        jj = nch - 1 - j

        @pl.when(j == 0)
        def _():
            ext_d[:, R:R + H, :] = jnp.zeros((nlt, H, LANE), F32)
            acc[...] = jnp.zeros_like(acc)

        for lt in range(nlt):
            ext_u[lt, 0:H, :] = jnp.where(
                jj == 0, 0.0, hh_ref[:, lt * LANE:(lt + 1) * LANE].astype(F32)
                * _sigmoid(hh_ref[:, cw + lt * LANE:cw + (lt + 1) * LANE].astype(F32)))

        def stage(r0, lt):
            rs, ls = pl.ds(r0, _RB), _lanes(lt)
            gate = h_ref[rs, _lanes(lt + nlt)].astype(F32)
            ext_u[lt, pl.ds(pl.multiple_of(r0 + H, SUB), _RB), :] = h_ref[rs, ls].astype(F32) * _sigmoid(gate)
            ext_d[lt, rs, :] = dc_ref[rs, ls]

        def first(r0, lt):
            ls = _lanes(lt)
            dc = dc_ref[pl.ds(r0, _RB), ls]
            acc[a_b:a_b + SUB, ls] += _psum8(dc)
            for k in range(CONV_C):
                tap = ext_u[lt, pl.ds(r0 + (H - (CONV_C - 1 - k)), _RB), :]
                acc[k * SUB:(k + 1) * SUB, ls] += _psum8(dc * tap)

        def second(r0, lt):
            rs, ls, lg = pl.ds(r0, _RB), _lanes(lt), _lanes(lt + nlt)
            du = w_ref[CONV_C - 1:CONV_C, ls] * ext_d[lt, rs, :]
            for k in range(CONV_C - 1):
                du = du + w_ref[k:k + 1, ls] * ext_d[lt, pl.ds(r0 + (CONV_C - 1 - k), _RB), :]
            val = h_ref[rs, ls].astype(F32)
            sg = _sigmoid(h_ref[rs, lg].astype(F32))
            dval = du * sg
            dgate = du * val * sg * (1.0 - sg)
            acc[a_val:a_val + SUB, ls] += _psum8(dval)
            acc[a_gate:a_gate + SUB, ls] += _psum8(dgate)
            dh_ref[rs, ls] = dval.astype(BF16)
            dh_ref[rs, lg] = dgate.astype(BF16)

        _sub_blocks(R, cw, stage)
        _sub_blocks(R, cw, first)
        _sub_blocks(R, cw, second)
        ext_d[:, R:R + H, :] = ext_d[:, 0:H, :]

        @pl.when(j == nch - 1)
        def _():
            for k in range(CONV_C):
                dw_ref[k:k + 1, :] = jnp.sum(acc[k * SUB:(k + 1) * SUB, :], axis=0, keepdims=True)
            db_ref[...] = jnp.sum(acc[a_b:a_b + SUB, :], axis=0, keepdims=True)
            db1_ref[:, 0:cw] = jnp.sum(acc[a_val:a_val + SUB, :], axis=0, keepdims=True)
            db1_ref[:, cw:2 * cw] = jnp.sum(acc[a_gate:a_gate + SUB, :], axis=0, keepdims=True)

    rows = lambda c, j: (nch - 1 - j, c)
    return pl.pallas_call(
        body, out_shape=[SDS((S, 2 * D), BF16), SDS((CONV_C, D), F32), SDS((1, D), F32), SDS((1, 2 * D), F32)],
        grid=(D // cw, nch),
        in_specs=[pl.BlockSpec((R, cw), rows), pl.BlockSpec((R, 2 * cw), rows),
                  pl.BlockSpec((H, 2 * cw), lambda c, j: (jnp.maximum((nch - 1 - j) * (R // H) - 1, 0), c)),
                  pl.BlockSpec((CONV_C, cw), lambda c, j: (0, c))],
        out_specs=[pl.BlockSpec((R, 2 * cw), rows), pl.BlockSpec((CONV_C, cw), lambda c, j: (0, c)),
                   pl.BlockSpec((1, cw), lambda c, j: (0, c)), pl.BlockSpec((1, 2 * cw), lambda c, j: (0, c))],
        scratch_shapes=[pltpu.VMEM((nlt, H + R, LANE), F32), pltpu.VMEM((nlt, R + H, LANE), F32),
                        pltpu.VMEM(((CONV_C + 3) * SUB, cw), F32)], name="conf_conv_bwd",
        compiler_params=_cp(2),
    )(dcv, h1p, h1p, w)


def _local_step(x, mem, tgt, W, fetch=None, send=None):
    G = {}
    W = dict(W)

    def arrive(group, after):
        if fetch is None:
            return None
        got, tok = fetch(group, after)
        for key, val in got.items():
            W[key] = {**W.get(key, {}), **val} if isinstance(val, dict) else val
        return tok

    def gain(g, tok):
        return g if tok is None else g + tok

    def sent(group):
        return None if send is None else send(group, G)

    def xattn_fwd(xin, n, l):
        tok = arrive(("xa", l), n)
        mn = _rms_fwd(mem, gain(W["xa_mem_norm"][l:l + 1], tok), f"xa_memnorm_fwd{l}")
        q = _mm_nn(n, W["xa_wq"][l], out_dtype=BF16, name=f"xa_q{l}")
        k = _mm_nn(mn, W["xa_wk"][l], out_dtype=BF16, name=f"xa_k{l}")
        v = _mm_nn(mn, W["xa_wv"][l], out_dtype=BF16, name=f"xa_v{l}")
        o = _attn_fwd(q, k, v, f"xa_attn_fwd{l}")
        xout, nout = _mm_nn(o, W["xa_wo"][l], out_dtype=F32, name=f"xa_o{l}", add=xin, norm=W["f_norm"][l:l + 1])
        return xout, nout, (xin, n, q, mn, k, v, o)

    def xattn_bwd(dx, dxb, saved, l):
        xin, n, q, mn, k, v, o = saved
        do = _mm_nt(dxb, W["xa_wo"][l], out_dtype=BF16, name=f"xa_do{l}")
        G[f"xa_wo{l}"] = _mm_tn(o, dxb, out_dtype=BF16, name=f"xa_dwo{l}")
        dq, dk, dv = _attn_bwd(q, k, v, do, f"xa_attn_bwd{l}")
        dkb, dvb = dk.astype(BF16), dv.astype(BF16)
        G[f"xa_wq{l}"] = _mm_tn(n, dq, out_dtype=BF16, name=f"xa_dwq{l}")
        G[f"xa_wk{l}"] = _mm_tn(mn, dkb, out_dtype=BF16, name=f"xa_dwk{l}")
        G[f"xa_wv{l}"] = _mm_tn(mn, dvb, out_dtype=BF16, name=f"xa_dwv{l}")
        tok = sent(("xa", l))
        dmn = _mm_nt(dkb, W["xa_wk"][l], out_dtype=F32, name=f"xa_dmn_k{l}")
        dmn = _mm_nt(dvb, W["xa_wv"][l], out_dtype=F32, name=f"xa_dmn_v{l}", add=dmn)
        (G[f"xa_mem_norm{l}"],) = _rms_bwd(mem, W["xa_mem_norm"][l:l + 1], dmn, None, f"xa_memnorm_bwd{l}")
        dx, dxb, G[f"xa_norm{l}"] = _mm_nt(dq, W["xa_wq"][l], out_dtype=F32, name=f"xa_dn{l}",
                                           rms=(xin, gain(W["xa_norm"][l:l + 1], tok), dx))
        return dx, dxb

    def ffn_fwd(xin, n, l, next_gain):
        tok = arrive(("f", l), n)
        hp = _mm_nn(n, W["f_w_up"][l], out_dtype=BF16, name=f"f_up{l}")
        act, gel, ud = _f_fwd(hp, W["f_dw_w"][l], gain(W["f_dw_b"][l:l + 1], tok), f"f_conv_fwd{l}")
        arrive(("fd", l), act)
        res = _mm_nn(act, W["f_w_down"][l], out_dtype=F32, name=f"f_down{l}", add=xin, norm=next_gain)
        xout, nout = res if next_gain is not None else (res, None)
        return xout, nout, (xin, n, hp, act, gel, ud)

    def ffn_bwd(dx, dxb, saved, l):
        xin, n, hp, act, gel, ud = saved
        dact = _mm_nt(dxb, W["f_w_down"][l], out_dtype=BF16, name=f"f_dact{l}")
        G[f"f_w_down{l}"] = _mm_tn(act, dxb, out_dtype=BF16, name=f"f_dwdown{l}")
        dhp, G[f"f_dw_w{l}"], G[f"f_dw_b{l}"] = _f_bwd(dact, hp, gel, ud, W["f_dw_w"][l], f"f_conv_bwd{l}")
        G[f"f_w_up{l}"] = _mm_tn(n, dhp, out_dtype=BF16, name=f"f_dwup{l}", blocks=_CW_F)
        tok = sent(("f", l))
        dx, dxb, G[f"f_norm{l}"] = _mm_nt(dhp, W["f_w_up"][l], out_dtype=F32, name=f"f_dn{l}",
                                          rms=(xin, gain(W["f_norm"][l:l + 1], tok), dx))
        return dx, dxb

    n0 = _rms_fwd(x, W["ab_norm"], "ab_norm_fwd")
    tok = arrive(("ab", 0), n0)
    a_par = (W["a_conv_w"], gain(W["a_conv_b"], tok), W["a_gate_x_w"], W["a_gate_x_b"], W["a_gate_a_w"],
             W["a_gate_a_b"], W["a_lambda"])
    b_par = (W["b_group_w"], W["b_group_b"], W["b_scale"])
    zp = _mm_nn(n0, W["ab_w_in"], out_dtype=BF16, name="ab_in")
    yab, h_a = _a_fwd(zp, *a_par)
    yab = _b_fwd(zp, yab, *b_par)
    arrive(("ab", 1), yab)
    x1, n1 = _mm_nn(yab, W["ab_w_out"], out_dtype=F32, name="ab_out", add=x, norm=W["xa_norm"][0:1])
    x2, n2, s_xa0 = xattn_fwd(x1, n1, 0)
    x3, n3, s_f0 = ffn_fwd(x2, n2, 0, W["c_norm"])
    tok = arrive(("c", 0), n3)
    h1p = _mm_nn(n3, W["c_w_pw1"], out_dtype=BF16, name="c_pw1", bias=gain(W["c_b_pw1"], tok))
    cv = _c_fwd(h1p, W["c_dw_w"], W["c_dw_b"])
    sc = _ln_silu_fwd(cv, W["c_ln_g"], W["c_ln_b"])
    x4, n4 = _mm_nn(sc, W["c_w_pw2"], out_dtype=F32, name="c_pw2", bias=W["c_b_pw2"], add=x3, norm=W["xa_norm"][1:2])
    x5, n5, s_xa1 = xattn_fwd(x4, n4, 1)
    x6, _, s_f1 = ffn_fwd(x5, n5, 1, None)
    loss, dx, dxb, G["final_norm"] = _loss_head(x6, W["final_norm"], tgt)

    dx, dxb = ffn_bwd(dx, dxb, s_f1, 1)
    dx, dxb = xattn_bwd(dx, dxb, s_xa1, 1)
    dsc = _mm_nt(dxb, W["c_w_pw2"], out_dtype=BF16, name="c_dsc")
    G["c_w_pw2"] = _mm_tn(sc, dxb, out_dtype=BF16, name="c_dwpw2")
    dcv, G["c_ln_g"], G["c_ln_b"], G["c_b_pw2"] = _ln_silu_bwd(dsc, cv, W["c_ln_g"], W["c_ln_b"], dx)
    dh1p, G["c_dw_w"], G["c_dw_b"], G["c_b_pw1"] = _c_bwd(dcv, h1p, W["c_dw_w"])
    G["c_w_pw1"] = _mm_tn(n3, dh1p, out_dtype=BF16, name="c_dwpw1", blocks=_CW_C)
    tok = sent(("c", 0))
    dx, dxb, G["c_norm"] = _mm_nt(dh1p, W["c_w_pw1"], out_dtype=F32, name="c_dn",
                                  rms=(x3, gain(W["c_norm"], tok), dx))
    dx, dxb = ffn_bwd(dx, dxb, s_f0, 0)
    dx, dxb = xattn_bwd(dx, dxb, s_xa0, 0)
    dyab = _mm_nt(dxb, W["ab_w_out"], out_dtype=BF16, name="ab_dyab")
    G["ab_w_out"] = _mm_tn(yab, dxb, out_dtype=BF16, name="ab_dwout")
    tok = sent(("ab", 1))
    a_par = (a_par[0], gain(a_par[1], tok)) + a_par[2:]
    (dzg, dzr, G["a_conv_w"], G["a_conv_b"], G["a_gate_x_w"], G["a_gate_x_b"], G["a_gate_a_w"], G["a_gate_a_b"],
     G["a_lambda"]) = _a_bwd(dyab, zp, h_a, *a_par)
    dzq, G["b_group_w"], G["b_group_b"], G["b_scale"] = _b_bwd(dyab, zp, *b_par)
    G["ab_w_in"] = jnp.concatenate(
        [_mm_tn(n0, dz, out_dtype=BF16, name=f"ab_dwin_{part}")
         for part, dz in (("gate", dzg), ("rec", dzr), ("pool", dzq))], axis=1)
    tok = sent(("ab", 0))
    dx, _, G["ab_norm"] = _mm_nt_cols([dzg, dzr, dzq], W["ab_w_in"], name="ab_dn",
                                      rms=(x, gain(W["ab_norm"], tok), dx))
    return loss, dx, G


def _my_place():
    x, y, c = lax.axis_index("x"), lax.axis_index("y"), lax.axis_index("c")
    return x, y, c


def _all_gather(shards, name):
    n = len(shards)

    def body(*refs):
        ins, outs = refs[:n], refs[n:2 * n]
        send_sems, recv_sems, local_sems = refs[2 * n:]
        x, y, c = _my_place()
        me, sibling = (x, y, c), (x, y, 1 - c)
        chips = [(1 - x, y), (x, 1 - y), (1 - x, 1 - y)]

        def slab(a, place):
            px, py, pc = place
            return outs[a].at[4 * px + 2 * py + pc]

        def copy(a, k, block, to, src=None):
            return pltpu.make_async_remote_copy(
                src_ref=slab(a, block) if src is None else src, dst_ref=slab(a, block),
                send_sem=send_sems.at[a, k], recv_sem=recv_sems.at[a, k], device_id=to, device_id_type=MESH)

        mine = [pltpu.make_async_copy(ins[a], slab(a, me), local_sems.at[a]) for a in range(n)]
        for cp in mine:
            cp.start()
        first = []
        for j, chip in enumerate(chips):
            first += [copy(a, 1 + j, me, (*chip, c), src=ins[a]) for a in range(n)]
        first += [copy(a, 0, me, sibling, src=ins[a]) for a in range(n)]
        for cp in first:
            cp.start()
        passed = []
        for j, chip in enumerate(chips):
            for a in range(n):
                copy(a, 1 + j, (*chip, c), me).wait_recv()
                cp = copy(a, 4 + j, (*chip, c), sibling)
                cp.start()
                passed.append(cp)
        for a in range(n):
            copy(a, 0, sibling, me).wait_recv()
        for j, chip in enumerate(chips):
            for a in range(n):
                copy(a, 4 + j, (*chip, 1 - c), me).wait_recv()
        for cp in first + passed:
            cp.wait_send()
        for cp in mine:
            cp.wait()

    any_spec = pl.BlockSpec(memory_space=pl.ANY)
    return pl.pallas_call(
        body, out_shape=[SDS((N_DEV,) + s.shape, s.dtype) for s in shards], in_specs=[any_spec] * n,
        out_specs=[any_spec] * n,
        scratch_shapes=[pltpu.SemaphoreType.DMA((n, 7)), pltpu.SemaphoreType.DMA((n, 7)), pltpu.SemaphoreType.DMA((n,))],
        name=name,
    )(*shards)


_HBM = pl.BlockSpec(memory_space=pltpu.HBM)
_SEM = pl.BlockSpec(memory_space=pltpu.SEMAPHORE)
_EFFECT = pltpu.SideEffectType.DATAFLOW_SIDE_EFFECTING


def _peer_places():
    x, y, c = _my_place()
    peers = []
    for k in range(1, N_DEV):
        px = 1 - x if (k >> 2) & 1 else x
        py = 1 - y if (k >> 1) & 1 else y
        pc = 1 - c if k & 1 else c
        peers.append(((px, py, pc), 4 * px + 2 * py + pc))
    return (x, y, c), 4 * x + 2 * y + c, peers


def _send_start(srcs, per_dest, name):
    n = len(srcs)
    lands = [lax.empty((N_DEV,) + (s.shape[1:] if per_dest else s.shape), s.dtype) for s in srcs]

    def body(*refs):
        src, land = refs[:n], refs[n:2 * n]
        outs = refs[2 * n:]
        send, recv, token = outs[:n], outs[n:2 * n], outs[4 * n]
        _, me, peers = _peer_places()
        for a in range(n):
            for peer, pidx in peers:
                pltpu.make_async_remote_copy(
                    src_ref=src[a].at[pidx] if per_dest else src[a], dst_ref=land[a].at[me], send_sem=send[a],
                    recv_sem=recv[a], device_id=peer, device_id_type=MESH).start()
        token[...] = jnp.zeros_like(token)

    hbm = lambda a: pltpu.HBM(a.shape, a.dtype)
    sem = pltpu.SemaphoreType.DMA(())
    res = pl.pallas_call(
        body, name=name,
        out_shape=tuple([sem] * (2 * n) + [hbm(s) for s in srcs] + [hbm(l) for l in lands]
                        + [SDS((SUB, LANE), F32)]),
        in_specs=[_HBM] * (2 * n),
        out_specs=tuple([_SEM] * (2 * n) + [_HBM] * (2 * n) + [pl.BlockSpec(memory_space=pltpu.VMEM)]),
        input_output_aliases={i: 2 * n + i for i in range(2 * n)},
        compiler_params=pltpu.CompilerParams(has_side_effects=_EFFECT),
    )(*[pltpu.with_memory_space_constraint(s, pltpu.HBM) for s in srcs],
      *[pltpu.with_memory_space_constraint(l, pltpu.HBM) for l in lands])
    return res[:n], res[n:2 * n], res[2 * n:3 * n], res[3 * n:4 * n], res[4 * n]


def _send_wait(send, recv, srcs, lands, after, per_dest, name):
    n = len(srcs)

    def body(*refs):
        src, land = refs[:n], refs[n:2 * n]
        send_s, recv_s = refs[2 * n:3 * n], refs[3 * n:4 * n]
        token = refs[-1]
        place, _, _ = _peer_places()
        for a in range(n):
            seven = land[a].at[pl.ds(0, N_DEV - 1)]
            copy = pltpu.make_async_remote_copy(
                src_ref=src[a].at[pl.ds(0, N_DEV - 1)] if per_dest else seven, dst_ref=seven, send_sem=send_s[a],
                recv_sem=recv_s[a], device_id=place, device_id_type=MESH)
            copy.wait_send()
            copy.wait_recv()
        token[...] = jnp.zeros_like(token)

    hbm = lambda a: pltpu.HBM(a.shape, a.dtype)
    res = pl.pallas_call(
        body, name=name,
        out_shape=tuple([hbm(s) for s in srcs] + [hbm(l) for l in lands] + [SDS((SUB, LANE), F32)]),
        in_specs=[_HBM] * (2 * n) + [_SEM] * (2 * n) + [pl.BlockSpec(memory_space=pl.ANY)],
        out_specs=tuple([_HBM] * (2 * n) + [pl.BlockSpec(memory_space=pltpu.VMEM)]),
        input_output_aliases={i: i for i in range(2 * n)},
        compiler_params=pltpu.CompilerParams(has_side_effects=_EFFECT),
    )(*srcs, *lands, *send, *recv, after)
    return res[:n], res[n:2 * n], res[2 * n]


def _adamw_math(w, g, m, v):
    m = ADAM_B1 * m + (1.0 - ADAM_B1) * g
    v = ADAM_B2 * v + (1.0 - ADAM_B2) * (g * g)
    m_hat = m / (1.0 - ADAM_B1 ** ADAM_STEP)
    v_hat = v / (1.0 - ADAM_B2 ** ADAM_STEP)
    delta = -ADAM_LR * (m_hat / (jnp.sqrt(v_hat) + ADAM_EPS) + ADAM_WD * w)
    return delta, m, v


def _row_tile(r, c, itemsize_rows):
    cap = max(SUB, (itemsize_rows // (4 * c)) // SUB * SUB)
    if r <= cap:
        return r
    best = None
    for t in range(SUB, cap + 1, SUB):
        if r % t == 0:
            best = t
    return best if best is not None else r


def _sum_adamw(landing, w, m, v, name, layer=0, prev=None, after=None):
    _, r, c = landing.shape
    tr = _row_tile(r, c, 2 << 20)
    off = layer * (r // tr)
    tail = ([] if prev is None else list(prev)) + ([] if after is None else [after])

    def body(l_ref, w_ref, m_ref, v_ref, *rest):
        g_ref, d_ref, mo_ref, vo_ref = rest[-4:]
        g = l_ref[0].astype(F32)
        for s in range(1, N_DEV):
            g = g + l_ref[s].astype(F32)
        g_ref[...] = g
        d_ref[...], mo_ref[...], vo_ref[...] = _adamw_math(w_ref[...], g, m_ref[...], v_ref[...])

    blk = pl.BlockSpec((tr, c), lambda i: (i + off, 0))
    n_prev = 0 if prev is None else 4
    return pl.pallas_call(
        body, out_shape=[SDS(w.shape, F32)] * 4, grid=(r // tr,),
        in_specs=[pl.BlockSpec((N_DEV, tr, c), lambda i: (0, i, 0)), blk, blk, blk]
        + [pl.BlockSpec(memory_space=pl.ANY)] * len(tail),
        out_specs=[blk] * 4, input_output_aliases={4 + i: i for i in range(n_prev)}, name=name,
        compiler_params=_cp(1),
    )(landing, w, m, v, *tail)


def _sum8(landing, name):
    _, r, c = landing.shape

    def body(l_ref, g_ref):
        g = l_ref[0]
        for s in range(1, N_DEV):
            g = g + l_ref[s]
        g_ref[...] = g

    return pl.pallas_call(body, out_shape=SDS((r, c), F32), name=name, compiler_params=_cp(0))(landing)


def _adamw_small(repl_pack, own_pack, P, M, V):
    table, off = [], 0
    for name, shape in _REPL.items():
        table.append((name, shape if len(shape) > 1 else (1,) + shape, 0, off // LANE))
        off += _size(shape)
    off = _REPL_ROWS * LANE
    for name, shape in _SMALL_SHARDED.items():
        table.append((name, shape, 1, off // LANE))
        off += _size(shape)
    n = len(table)

    def body(*refs):
        packs, ins, outs = refs[:2], refs[2:2 + 3 * n], refs[2 + 3 * n:]
        for p, (_, shape, which, r0) in enumerate(table):
            w_ref, m_ref, v_ref = ins[3 * p:3 * p + 3]
            g_ref, d_ref, mo_ref, vo_ref = outs[4 * p:4 * p + 4]
            pack, rows, q = packs[which], shape[-2], shape[-1] // LANE
            lead = [()]
            for dim in shape[:-2]:
                lead = [t + (i,) for t in lead for i in range(dim)]
            for li, idx in enumerate(lead):
                if q == 1:
                    dst = g_ref.at[idx] if idx else g_ref
                    dst[...] = pack[r0 + li * rows:r0 + (li + 1) * rows, :]
                    continue
                for i in range(rows):
                    for k in range(q):
                        row = r0 + (li * rows + i) * q + k
                        g_ref[idx + (slice(i, i + 1), slice(k * LANE, (k + 1) * LANE))] = pack[row:row + 1, :]
            d_ref[...], mo_ref[...], vo_ref[...] = _adamw_math(w_ref[...], g_ref[...], m_ref[...], v_ref[...])

    ins, out_shape = [], []
    for name, shape, _, _ in table:
        ins += [t[name].reshape(shape) for t in (P, M, V)]
        out_shape += [SDS(shape, F32)] * 4
    res = pl.pallas_call(body, out_shape=out_shape, name="adamw_small", compiler_params=_cp(0))(
        repl_pack, own_pack, *ins)
    dicts = ({}, {}, {}, {})
    for p, (name, shape, _, _) in enumerate(table):
        for d, arr in zip(dicts, res[4 * p:4 * p + 4]):
            d[name] = arr.reshape(P[name].shape)
    return dicts


_BIG = {
    "ab_w_in": (1, D, 320), "ab_w_out": (1, 192, D), "c_w_pw1": (1, D, 256), "c_w_pw2": (1, 128, D),
    "xa_wq": (2, 128, D), "xa_wk": (2, 128, D), "xa_wv": (2, 128, D), "xa_wo": (2, 128, D),
    "f_w_up": (2, D, 768), "f_w_down": (2, 384, D),
}
_SMALL_SHARDED = {
    "a_conv_w": (1, 4, 128), "c_norm": (1, 128), "c_b_pw1": (1, 256), "c_dw_w": (1, 31, 128), "c_dw_b": (1, 128),
    "c_ln_g": (1, 128), "c_ln_b": (1, 128), "c_b_pw2": (1, 128), "f_dw_w": (2, 3, 384),
}
_REPL = {
    "ab_norm": (1, D), "a_conv_b": (1, D), "a_gate_x_w": (1, 8, 128, 128), "a_gate_x_b": (1, D),
    "a_gate_a_w": (1, 8, 128, 128), "a_gate_a_b": (1, D), "a_lambda": (1, D), "b_group_w": (1, 4, 128, 128),
    "b_group_b": (1, 512), "b_scale": (1, 512), "xa_norm": (2, D), "xa_mem_norm": (2, D), "f_norm": (2, D),
    "f_dw_b": (2, D_FF), "final_norm": (D,),
}


def _size(shape):
    n = 1
    for s in shape:
        n *= s
    return n


_N_SS = sum(_size(s) for s in _SMALL_SHARDED.values())
_N_REPL = sum(_size(s) for s in _REPL.values())
_REPL_ROWS = -(-_N_REPL // (N_DEV * SUB * LANE)) * SUB
_SS_ROWS = _N_SS // LANE
_SMALL_ROWS = -(-(_REPL_ROWS + _SS_ROWS) // SUB) * SUB


def _pack(parts, rows):
    flat = jnp.concatenate([p.reshape(-1).astype(F32) for p in parts])
    return jnp.pad(flat, (0, rows * LANE - flat.shape[0])).reshape(rows, LANE)


def _pair_blocks(v, bw):
    lead, n = v.shape[:-1], v.shape[-1]
    return jnp.swapaxes(v.reshape(lead + (2, n // (2 * bw), bw)), -3, -2).reshape(lead + (n,))


def _unpair_blocks(v, bw):
    lead, n = v.shape[:-1], v.shape[-1]
    return jnp.swapaxes(v.reshape(lead + (n // (2 * bw), 2, bw)), -3, -2).reshape(lead + (n,))


_GROUPS = {
    ("ab", 0): (("ab_w_in", 0),),
    ("ab", 1): (("ab_w_out", 0),),
    ("xa", 0): (("xa_wq", 0), ("xa_wk", 0), ("xa_wv", 0), ("xa_wo", 0)),
    ("f", 0): (("f_w_up", 0),),
    ("fd", 0): (("f_w_down", 0),),
    ("c", 0): (("c_w_pw1", 0), ("c_w_pw2", 0)),
    ("xa", 1): (("xa_wq", 1), ("xa_wk", 1), ("xa_wv", 1), ("xa_wo", 1)),
    ("f", 1): (("f_w_up", 1),),
    ("fd", 1): (("f_w_down", 1),),
}
_SEND_GROUPS = {g: m for g, m in _GROUPS.items() if g[0] != "fd"}
_SEND_GROUPS[("f", 0)] = (("f_w_up", 0), ("f_w_down", 0))
_SEND_GROUPS[("f", 1)] = (("f_w_up", 1), ("f_w_down", 1))


def _weight_layout(name, g):
    if name == "ab_w_in":
        return jnp.swapaxes(g, 0, 1).reshape(D, N_DEV * 320)
    if name in ("c_w_pw1", "f_w_up"):
        return g
    return g.reshape(N_DEV * g.shape[1], D)


def _grad_blocks(name, l, G):
    _, r, c = _BIG[name]
    if name == "ab_w_in":
        return jnp.swapaxes(G[name].reshape(D, N_DEV, 320), 0, 1)
    if name == "c_w_pw1":
        return G[name]
    if name == "f_w_up":
        return G[f"{name}{l}"]
    return (G[name] if _BIG[name][0] == 1 else G[f"{name}{l}"]).reshape(N_DEV, r, c)


def _small_layouts(sm):
    W = {}
    sm = sm.reshape(N_DEV, -1)
    off = 0
    for name, shape in _SMALL_SHARDED.items():
        n = _size(shape)
        blocks = sm[:, off:off + n].reshape((N_DEV,) + shape)
        off += n
        W[name] = jnp.moveaxis(blocks, 0, -2).reshape(shape[:-1] + (N_DEV * shape[-1],))
    W["a_conv_w"], W["c_dw_w"] = W["a_conv_w"][0], W["c_dw_w"][0]
    W["c_b_pw1"] = _pair_blocks(W["c_b_pw1"], _CW_C)
    return W


def _with_own(land, src, me, per_dest):
    own = lax.dynamic_slice_in_dim(src, me, 1, 0) if per_dest else src[None]
    return lax.dynamic_update_slice_in_dim(land, own, me, 0)


def _to_dest_major(g, shape):
    full = g.reshape(shape[:-1] + (N_DEV, shape[-1]))
    return jnp.moveaxis(full, -2, 0).reshape(N_DEV, -1)


def kernel(x, mem, ab_norm, ab_w_in, a_conv_w, a_conv_b, a_gate_x_w, a_gate_x_b, a_gate_a_w, a_gate_a_b, a_lambda, b_group_w, b_group_b, b_scale, ab_w_out, c_norm, c_w_pw1, c_b_pw1, c_dw_w, c_dw_b, c_ln_g, c_ln_b, c_w_pw2, c_b_pw2, xa_norm, xa_mem_norm, xa_wq, xa_wk, xa_wv, xa_wo, f_norm, f_w_up, f_dw_w, f_dw_b, f_w_down, final_norm, loss_target, m_ab_norm, m_ab_w_in, m_a_conv_w, m_a_conv_b, m_a_gate_x_w, m_a_gate_x_b, m_a_gate_a_w, m_a_gate_a_b, m_a_lambda, m_b_group_w, m_b_group_b, m_b_scale, m_ab_w_out, m_c_norm, m_c_w_pw1, m_c_b_pw1, m_c_dw_w, m_c_dw_b, m_c_ln_g, m_c_ln_b, m_c_w_pw2, m_c_b_pw2, m_xa_norm, m_xa_mem_norm, m_xa_wq, m_xa_wk, m_xa_wv, m_xa_wo, m_f_norm, m_f_w_up, m_f_dw_w, m_f_dw_b, m_f_w_down, m_final_norm, v_ab_norm, v_ab_w_in, v_a_conv_w, v_a_conv_b, v_a_gate_x_w, v_a_gate_x_b, v_a_gate_a_w, v_a_gate_a_b, v_a_lambda, v_b_group_w, v_b_group_b, v_b_scale, v_ab_w_out, v_c_norm, v_c_w_pw1, v_c_b_pw1, v_c_dw_w, v_c_dw_b, v_c_ln_g, v_c_ln_b, v_c_w_pw2, v_c_b_pw2, v_xa_norm, v_xa_mem_norm, v_xa_wq, v_xa_wk, v_xa_wv, v_xa_wo, v_f_norm, v_f_w_up, v_f_dw_w, v_f_dw_b, v_f_w_down, v_final_norm):
    args = dict(locals())
    P = {n: args[n] for n in _NAMES}
    M = {n: args["m_" + n] for n in _NAMES}
    V = {n: args["v_" + n] for n in _NAMES}

    me = 4 * lax.axis_index("x") + 2 * lax.axis_index("y") + lax.axis_index("c")

    in_flight = {}

    def launch(groups, tok):
        shards, n_of = [], {}
        for grp in groups:
            for name, l in _GROUPS[grp]:
                w = P[name][l] if tok is None else P[name][l] + tok
                shards.append(w.astype(BF16))
            if grp == ("ab", 0):
                shards.append(_pack([P[n] for n in _SMALL_SHARDED], _SS_ROWS + 4))
            n_of[grp] = len(shards)
        res = _send_start(shards, False, "gather_start_" + "_".join(g[0] + str(g[1]) for g in groups))
        lo = 0
        for grp in groups:
            in_flight[grp] = [r[lo:n_of[grp]] for r in res[:4]]
            lo = n_of[grp]
        return res[4][:1, :1]

    follow = {("ab", 0): [("ab", 1), ("xa", 0), ("f", 0), ("fd", 0)], ("xa", 0): [("c", 0), ("xa", 1)],
              ("f", 0): [("f", 1), ("fd", 1)]}

    def fetch(grp, after):
        send_s, recv_s, srcs, lands = in_flight.pop(grp)
        srcs, lands, tok = _send_wait(send_s, recv_s, srcs, lands, after, False, f"gather_wait_{grp[0]}{grp[1]}")
        tok = launch(follow[grp], tok[:1, :1]) if grp in follow else None
        full = [_with_own(land, src, me, False) for land, src in zip(lands, srcs)]
        out = {}
        for (name, l), g in zip(_GROUPS[grp], full):
            w = _weight_layout(name, g)
            if _BIG[name][0] == 1:
                out[name] = w
            else:
                out[name] = {l: w}
        if grp == ("ab", 0):
            out.update(_small_layouts(full[-1]))
        return out, tok

    zero = launch([("ab", 0)], None)

    pending, held = [], []
    rides_with_next = {("xa", 1), ("f", 0)}

    def send(grp, G):
        held.extend(_SEND_GROUPS[grp])
        if grp in rides_with_next:
            return None
        members = tuple(held)
        del held[:]
        res = _send_start([_grad_blocks(name, l, G) for name, l in members], True, f"send_{grp[0]}{grp[1]}")
        pending.append((members, res))
        return res[4][:1, :1]

    W = {n: P[n] for n in _REPL}
    W["ab_norm"] = P["ab_norm"] + zero
    W["final_norm"] = P["final_norm"].reshape(1, D)
    W["a_gate_x_w"], W["a_gate_a_w"], W["b_group_w"] = P["a_gate_x_w"][0], P["a_gate_a_w"][0], P["b_group_w"][0]
    loss, grad_x, G = _local_step(x[0], mem[0], loss_target[0], W, fetch, send)
    loss = lax.psum(loss[0, 0], ("x", "y", "c"))

    Gs = dict(G)
    Gs["c_b_pw1"] = _unpair_blocks(G["c_b_pw1"], _CW_C)
    Gs["f_dw_w"] = jnp.stack([G["f_dw_w0"], G["f_dw_w1"]])
    Gs["a_conv_w"], Gs["c_dw_w"] = G["a_conv_w"][None], G["c_dw_w"][None]
    for n in ("xa_norm", "xa_mem_norm", "f_norm", "f_dw_b"):
        Gs[n] = jnp.concatenate([G[f"{n}0"], G[f"{n}1"]], axis=0)
    for n in ("a_gate_x_w", "a_gate_a_w", "b_group_w"):
        Gs[n] = G[n][None]
    repl_flat = jnp.concatenate([Gs[n].reshape(-1) for n in _REPL])
    repl_rows = jnp.pad(repl_flat, (0, N_DEV * _REPL_ROWS * LANE - _N_REPL)).reshape(N_DEV, _REPL_ROWS, LANE)
    ss_rows = jnp.concatenate([_to_dest_major(Gs[n], s) for n, s in _SMALL_SHARDED.items()], axis=1)
    ss_rows = ss_rows.reshape(N_DEV, _SS_ROWS, LANE)
    small_pack = jnp.concatenate(
        [repl_rows, ss_rows, jnp.zeros((N_DEV, _SMALL_ROWS - _REPL_ROWS - _SS_ROWS, LANE), F32)], axis=1)
    last = _send_start([small_pack], True, "send_small")
    pending.append(((("small", 0),), last))

    def arrived(some, after, name):
        members = [m for mem_, _ in some for m in mem_]
        cat = [[a for _, res in some for a in res[i]] for i in range(4)]
        srcs, lands, _ = _send_wait(cat[0], cat[1], cat[2], cat[3], after, True, name)
        return {m: _with_own(land, src, me, True) for m, land, src in zip(members, lands, srcs)}

    out_g, out_d, out_m, out_v = {}, {}, {}, {}
    chain = [None]

    def update(name, landed):
        layers, r, c = _BIG[name]
        w2, m2, v2 = [t[name].reshape(layers * r, c) for t in (P, M, V)]
        res = None
        for l in range(layers):
            res = _sum_adamw(landed[(name, l)], w2, m2, v2, f"adamw_{name}{l}", layer=l, prev=res,
                             after=chain[0] if l == 0 else None)
        chain[0] = res[1]
        out_g[name], out_d[name], out_m[name], out_v[name] = [t.reshape(P[name].shape) for t in res]

    landed = arrived(pending[:-2], grad_x, "send_wait_early")
    for name in _BIG:
        if name != "ab_w_in":
            update(name, landed)
    landed = arrived(pending[-2:], out_v["f_w_down"], "send_wait_late")
    update("ab_w_in", landed)

    small_sum = _sum8(landed[("small", 0)], "sum_small")
    (repl_all,) = _all_gather([small_sum[:_REPL_ROWS]], "gather_small_grads")
    for out, got in zip((out_g, out_d, out_m, out_v),
                        _adamw_small(repl_all.reshape(N_DEV * _REPL_ROWS, LANE), small_sum, P, M, V)):
        out.update(got)

    return (loss, grad_x[None], *[out_g[n] for n in _NAMES], *[out_d[n] for n in _NAMES],
            *[out_m[n] for n in _NAMES], *[out_v[n] for n in _NAMES])


_NAMES = ("ab_norm", "ab_w_in", "a_conv_w", "a_conv_b", "a_gate_x_w", "a_gate_x_b", "a_gate_a_w", "a_gate_a_b",
          "a_lambda", "b_group_w", "b_group_b", "b_scale", "ab_w_out", "c_norm", "c_w_pw1", "c_b_pw1", "c_dw_w",
          "c_dw_b", "c_ln_g", "c_ln_b", "c_w_pw2", "c_b_pw2", "xa_norm", "xa_mem_norm", "xa_wq", "xa_wk", "xa_wv",
          "xa_wo", "f_norm", "f_w_up", "f_dw_w", "f_dw_b", "f_w_down", "final_norm")
```

```python
import functools

import jax
import jax.numpy as jnp
from jax import lax
from jax.experimental import pallas as pl
from jax.experimental.pallas import tpu as pltpu

F32, BF16 = jnp.float32, jnp.bfloat16
SDS = jax.ShapeDtypeStruct
MESH = pl.DeviceIdType.MESH

N_DEV = 8
D = 1024
N_MEM = 256
XA_HEADS, XA_HD = 4, 256
HD_A = 128
CONV_A, CONV_C, CONV_F = 4, 31, 3
C_RG = 8.0
POOL_WINDOWS = (2, 4, 8, 16)
D_FF = 3 * D
EPS = 1e-6
ADAM_LR, ADAM_B1, ADAM_B2, ADAM_EPS, ADAM_WD, ADAM_STEP = 0.001, 0.9, 0.999, 1e-08, 0.01, 10

LANE = 128
SUB = 8
VMEM_LIMIT = 56 * 1024 * 1024
R_SEQ = 1024
R_POOL = 2048
R_RGLRU = 2048
R_FFN = 2048
TM_ROW = 1024


def _cp(n_axes):
    return pltpu.CompilerParams(dimension_semantics=("arbitrary",) * n_axes, vmem_limit_bytes=VMEM_LIMIT)


def _tile(n, pref):
    if n <= pref:
        return n
    best = None
    for t in range(LANE, pref + 1, LANE):
        if n % t == 0:
            best = t
    assert best is not None, (n, pref)
    return best


def _perm2(n):
    return (n % 2) * 4 + n // 2


_NN = (((1,), (0,)), ((), ()))
_NT = (((1,), (1,)), ((), ()))
_TN = (((0,), (0,)), ((), ()))


def _mm_call(name, grid, ab, ab_specs, dims, acc_shape, extras, outs, finish, from_ref=False):
    nk = grid[2]
    n_ab, n_ex, n_out = len(ab), len(extras), len(outs)
    use_acc = nk > 1 or from_ref

    def product(refs):
        r = lax.dot_general(refs[0][...], refs[1][...], dims, preferred_element_type=F32)
        for i in range(1, n_ab):
            r = r + lax.dot_general(refs[2 * i][...], refs[2 * i + 1][...], dims, preferred_element_type=F32)
        return r

    def body(*refs):
        rest = refs[2 * n_ab:]
        ex_refs, o_refs = rest[:n_ex], rest[n_ex:n_ex + n_out]
        first_rows = pl.program_id(0) == 0
        if not use_acc:
            finish(product(refs), ex_refs, o_refs, first_rows)
            return
        acc = rest[n_ex + n_out]
        if nk == 1:
            acc[...] = product(refs)
            finish(acc, ex_refs, o_refs, first_rows)
            return
        k = pl.program_id(2)

        @pl.when(k == 0)
        def _():
            acc[...] = jnp.zeros_like(acc)

        acc[...] += product(refs)

        @pl.when(k == nk - 1)
        def _():
            finish(acc if from_ref else acc[...], ex_refs, o_refs, first_rows)

    res = pl.pallas_call(
        body, out_shape=[o for o, _ in outs], grid=grid,
        in_specs=list(ab_specs) + [s for _, s in extras], out_specs=[s for _, s in outs],
        scratch_shapes=[pltpu.VMEM(acc_shape, F32)] if use_acc else [], name=name, compiler_params=_cp(3),
    )(*[t for pair in ab for t in pair], *[e for e, _ in extras])
    return res[0] if n_out == 1 else res


def _finish_sum(r, ex_refs, o_refs, first_rows):
    del first_rows
    for e in ex_refs:
        r = r + e[...]
    o_refs[0][...] = r.astype(o_refs[0].dtype)


def _finish_sum_norm(r, ex_refs, o_refs, first_rows):
    del first_rows
    for e in ex_refs[:-1]:
        r = r + e[...]
    o_refs[0][...] = r
    o_refs[1][...] = ((r * lax.rsqrt(jnp.mean(r * r, axis=-1, keepdims=True) + EPS)) * ex_refs[-1][...]).astype(BF16)


_EPI_ROWS = 16


def _finish_rms_bwd(r_ref, ex_refs, o_refs, first_rows):
    x_ref, g_ref, dres_ref = ex_refs
    dx_ref, dxb_ref, dg_ref = o_refs

    @pl.when(first_rows)
    def _():
        dg_ref[...] = jnp.zeros_like(dg_ref)

    gv = g_ref[...]
    inv_d = 1.0 / r_ref.shape[1]

    def step(i, dg_acc):
        groups = [pl.ds(pl.multiple_of(i * (2 * _EPI_ROWS) + u * _EPI_ROWS, _EPI_ROWS), _EPI_ROWS) for u in range(2)]
        sums = []
        for rows in groups:
            r, xf = r_ref[rows, :], x_ref[rows, :]
            sums.append((jnp.sum(xf * xf, axis=-1, keepdims=True), jnp.sum((r * gv) * xf, axis=-1, keepdims=True)))
        for rows, (sxx, sax) in zip(groups, sums):
            r, xf = r_ref[rows, :], x_ref[rows, :]
            rs = lax.rsqrt(sxx * inv_d + EPS)
            dg_acc = dg_acc + _psum8(r * (xf * rs))
            dx = rs * (r * gv) - xf * (rs * rs * (sax * rs * inv_d)) + dres_ref[rows, :]
            dx_ref[rows, :] = dx
            dxb_ref[rows, :] = dx.astype(BF16)
        return dg_acc

    dg_acc = lax.fori_loop(0, r_ref.shape[0] // (2 * _EPI_ROWS), step, jnp.zeros((SUB, r_ref.shape[1]), F32))
    dg_ref[...] += jnp.sum(dg_acc, axis=0, keepdims=True)


def _rms_bwd_io(M, tm, x, g, dres):
    rows = pl.BlockSpec((tm, D), lambda m, n, k: (m, 0))
    vec = pl.BlockSpec((1, D), lambda m, n, k: (0, 0))
    return ([(x, rows), (g, vec), (dres, rows)],
            [(SDS((M, D), F32), rows), (SDS((M, D), BF16), rows), (SDS((1, D), F32), vec)])


_K_WHOLE = 3072


def _mm_nn(a, b, *, out_dtype, name, bias=None, add=None, norm=None):
    M, K = a.shape
    tk = K if K <= _K_WHOLE else _tile(K, 1024)
    if K <= 1024 and norm is None:
        tm = _tile(M, 2048 if add is None and out_dtype == BF16 else 1024)
    else:
        tm = _tile(M, 1024 if K <= 1536 else 512)
    if b.ndim == 3:
        nb, _, bw = b.shape
        N, tn, nn = nb * bw, bw, nb
        b_spec = pl.BlockSpec((None, tk, bw), lambda m, n, k: (_perm2(n), k, 0))
    else:
        N = b.shape[1]
        tn = _tile(N, 1024)
        nn = N // tn
        b_spec = pl.BlockSpec((tk, tn), lambda m, n, k: (k, n))
    tile = pl.BlockSpec((tm, tn), lambda m, n, k: (m, n))
    vec = pl.BlockSpec((1, tn), lambda m, n, k: (0, n))
    extras = ([] if bias is None else [(bias, vec)]) + ([] if add is None else [(add, tile)])
    outs, finish = [(SDS((M, N), out_dtype), tile)], _finish_sum
    if norm is not None:
        assert tn == N == D and out_dtype == F32
        extras.append((norm, vec))
        outs, finish = outs + [(SDS((M, N), BF16), tile)], _finish_sum_norm
    return _mm_call(name, (M // tm, nn, K // tk), [(a, b)], [pl.BlockSpec((tm, tk), lambda m, n, k: (m, k)), b_spec],
                    _NN, (tm, tn), extras, outs, finish)


def _mm_nt(a, b, *, out_dtype, name, add=None, rms=None):
    M, N = a.shape
    if b.ndim == 3:
        nb, Ko, bw = b.shape
        tm = _tile(M, 1024)
        tn, tk, nk = _tile(Ko, 1024), bw, nb
        b_spec = pl.BlockSpec((None, tn, bw), lambda m, n, k: (_perm2(k), n, 0))
    else:
        Ko = b.shape[0]
        tk = N if N <= _K_WHOLE else _tile(N, 1024)
        if N <= 1024:
            tm = _tile(M, 2048 if add is None and rms is None and out_dtype == BF16 else 1024)
        else:
            tm = _tile(M, 512)
        tn = _tile(Ko, 1024)
        nk = N // tk
        b_spec = pl.BlockSpec((tn, tk), lambda m, n, k: (n, k))
    tile = pl.BlockSpec((tm, tn), lambda m, n, k: (m, n))
    extras = [] if add is None else [(add, tile)]
    outs, finish = [(SDS((M, Ko), out_dtype), tile)], _finish_sum
    if rms is not None:
        assert tn == Ko == D and add is None
        (extras, outs), finish = _rms_bwd_io(M, tm, *rms), _finish_rms_bwd
    return _mm_call(name, (M // tm, Ko // tn, nk), [(a, b)], [pl.BlockSpec((tm, tk), lambda m, n, k: (m, k)), b_spec],
                    _NT, (tm, tn), extras, outs, finish, from_ref=rms is not None)


def _mm_nt_cols(parts, b, *, name, rms):
    M = parts[0].shape[0]
    tm = _tile(M, 1024)
    specs, off = [], 0
    for p in parts:
        w = p.shape[1]
        assert off % w == 0
        specs.append(pl.BlockSpec((tm, w), lambda m, n, k: (m, 0)))
        specs.append(pl.BlockSpec((D, w), functools.partial(lambda m, n, k, o: (0, o), o=off // w)))
        off += w
    extras, outs = _rms_bwd_io(M, tm, *rms)
    return _mm_call(name, (M // tm, 1, 1), [(p, b) for p in parts], specs, _NT, (tm, D), extras, outs, _finish_rms_bwd,
                    from_ref=True)


def _mm_tn(a, b, *, out_dtype, name, blocks=None):
    S, Ka = a.shape
    Nb = b.shape[1]
    tm = _tile(Ka, 1024)
    if blocks is not None:
        bw = blocks
        tn, nn = bw, Nb // bw
        out = (SDS((nn, Ka, bw), out_dtype), pl.BlockSpec((None, tm, bw), lambda m, n, k: (_perm2(n), m, 0)))
    else:
        tn = _tile(Nb, 1024)
        nn = Nb // tn
        out = (SDS((Ka, Nb), out_dtype), pl.BlockSpec((tm, tn), lambda m, n, k: (m, n)))
    steps = (Ka // tm) * nn
    tk = _tile(S, 4096 if steps >= 4 else 2048 if steps >= 2 else 1024)
    return _mm_call(name, (Ka // tm, nn, S // tk), [(a, b)],
                    [pl.BlockSpec((tk, tm), lambda m, n, k: (k, m)), pl.BlockSpec((tk, tn), lambda m, n, k: (k, n))],
                    _TN, (tm, tn), [], [out], _finish_sum)


def _row(tm, c):
    return pl.BlockSpec((tm, c), lambda i: (i, 0))


def _full(shape):
    nd = len(shape)
    return pl.BlockSpec(shape, lambda i: (0,) * nd)


def _rms_fwd(x, g, name):
    S = x.shape[0]
    tm = min(S, TM_ROW)

    def body(x_ref, g_ref, o_ref):
        xf = x_ref[...]
        r = lax.rsqrt(jnp.mean(xf * xf, axis=-1, keepdims=True) + EPS)
        o_ref[...] = ((xf * r) * g_ref[...]).astype(BF16)

    return pl.pallas_call(body, out_shape=SDS((S, D), BF16), grid=(S // tm,), in_specs=[_row(tm, D), _full((1, D))],
                          out_specs=_row(tm, D), name=name, compiler_params=_cp(1))(x, g)


def _rms_bwd(x, g, dn, dres, name):
    S = x.shape[0]
    tm = min(S, TM_ROW)
    want_dx = dres is not None

    def body(x_ref, g_ref, dn_ref, *rest):
        i = pl.program_id(0)
        dg_ref = rest[-1]

        @pl.when(i == 0)
        def _():
            dg_ref[...] = jnp.zeros_like(dg_ref)

        xf = x_ref[...]
        r = lax.rsqrt(jnp.mean(xf * xf, axis=-1, keepdims=True) + EPS)
        y = xf * r
        dn_v = dn_ref[...]
        dg_ref[...] += jnp.sum(dn_v * y, axis=0, keepdims=True)
        if want_dx:
            dres_ref, dx_ref, dxb_ref = rest[0], rest[1], rest[2]
            dy = dn_v * g_ref[...]
            dx = r * (dy - y * jnp.mean(dy * y, axis=-1, keepdims=True)) + dres_ref[...]
            dx_ref[...] = dx
            dxb_ref[...] = dx.astype(BF16)

    ins = [x, g, dn] + ([dres] if want_dx else [])
    in_specs = [_row(tm, D), _full((1, D)), _row(tm, D)] + ([_row(tm, D)] if want_dx else [])
    outs = ([SDS((S, D), F32), SDS((S, D), BF16)] if want_dx else []) + [SDS((1, D), F32)]
    out_specs = ([_row(tm, D), _row(tm, D)] if want_dx else []) + [_full((1, D))]
    return pl.pallas_call(body, out_shape=outs, grid=(S // tm,), in_specs=in_specs, out_specs=out_specs, name=name,
                          compiler_params=_cp(1))(*ins)


def _loss_head(x, g, tgt):
    S = x.shape[0]
    tm = min(S, TM_ROW)

    def body(x_ref, g_ref, t_ref, loss_ref, dx_ref, dxb_ref, dg_ref):
        i = pl.program_id(0)

        @pl.when(i == 0)
        def _():
            loss_ref[...] = jnp.zeros_like(loss_ref)
            dg_ref[...] = jnp.zeros_like(dg_ref)

        xf = x_ref[...]
        r = lax.rsqrt(jnp.mean(xf * xf, axis=-1, keepdims=True) + EPS)
        y = xf * r
        gv = g_ref[...]
        err = y * gv - t_ref[...]
        per_row = jnp.mean(err * err, axis=-1, keepdims=True)
        loss_ref[...] += 0.5 * jnp.sum(per_row, axis=0, keepdims=True)
        dn_v = err * (1.0 / D)
        dg_ref[...] += jnp.sum(dn_v * y, axis=0, keepdims=True)
        dy = dn_v * gv
        dx = r * (dy - y * jnp.mean(dy * y, axis=-1, keepdims=True))
        dx_ref[...] = dx
        dxb_ref[...] = dx.astype(BF16)

    return pl.pallas_call(
        body, out_shape=[SDS((1, 1), F32), SDS((S, D), F32), SDS((S, D), BF16), SDS((1, D), F32)], grid=(S // tm,),
        in_specs=[_row(tm, D), _full((1, D)), _row(tm, D)],
        out_specs=[_full((1, 1)), _row(tm, D), _row(tm, D), _full((1, D))], name="loss_head", compiler_params=_cp(1),
    )(x, g, tgt)


def _softmax_rows(s):
    m = jnp.max(s, axis=-1, keepdims=True)
    e = jnp.exp(s - m)
    return e / jnp.sum(e, axis=-1, keepdims=True)


def _attn_fwd(q, k, v, name):
    S = q.shape[0]
    tm = min(S, TM_ROW)
    scale = XA_HD ** -0.5

    def body(q_ref, k_ref, v_ref, o_ref):
        for h in range(XA_HEADS):
            sl = slice(h * XA_HD, (h + 1) * XA_HD)
            s = lax.dot_general(q_ref[:, sl], k_ref[:, sl], _NT, preferred_element_type=F32) * scale
            p = _softmax_rows(s)
            o_ref[:, sl] = lax.dot_general(p.astype(BF16), v_ref[:, sl], _NN, preferred_element_type=F32).astype(BF16)

    return pl.pallas_call(body, out_shape=SDS((S, D), BF16), grid=(S // tm,),
                          in_specs=[_row(tm, D), _full((N_MEM, D)), _full((N_MEM, D))], out_specs=_row(tm, D),
                          name=name, compiler_params=_cp(1))(q, k, v)


def _attn_bwd(q, k, v, do, name):
    S = q.shape[0]
    tm = min(S, TM_ROW)
    scale = XA_HD ** -0.5

    def body(q_ref, k_ref, v_ref, do_ref, dq_ref, dk_ref, dv_ref):
        i = pl.program_id(0)

        @pl.when(i == 0)
        def _():
            dk_ref[...] = jnp.zeros_like(dk_ref)
            dv_ref[...] = jnp.zeros_like(dv_ref)

        for h in range(XA_HEADS):
            sl = slice(h * XA_HD, (h + 1) * XA_HD)
            qh, kh, vh, doh = q_ref[:, sl], k_ref[:, sl], v_ref[:, sl], do_ref[:, sl]
            s = lax.dot_general(qh, kh, _NT, preferred_element_type=F32) * scale
            p = _softmax_rows(s)
            pb = p.astype(BF16)
            dv_ref[:, sl] += lax.dot_general(pb, doh, _TN, preferred_element_type=F32)
            dp = lax.dot_general(doh, vh, _NT, preferred_element_type=F32)
            ds = (p * (dp - jnp.sum(dp * p, axis=-1, keepdims=True)) * scale).astype(BF16)
            dq_ref[:, sl] = lax.dot_general(ds, kh, _NN, preferred_element_type=F32).astype(BF16)
            dk_ref[:, sl] += lax.dot_general(ds, qh, _TN, preferred_element_type=F32)

    return pl.pallas_call(
        body, out_shape=[SDS((S, D), BF16), SDS((N_MEM, D), F32), SDS((N_MEM, D), F32)], grid=(S // tm,),
        in_specs=[_row(tm, D), _full((N_MEM, D)), _full((N_MEM, D)), _row(tm, D)],
        out_specs=[_row(tm, D), _full((N_MEM, D)), _full((N_MEM, D))], name=name, compiler_params=_cp(1),
    )(q, k, v, do)


def _sigmoid(x):
    return 1.0 / (1.0 + jnp.exp(-x))


def _ln_silu_fwd(cv, g, b):
    S = cv.shape[0]
    tm = min(S, TM_ROW)

    def body(x_ref, g_ref, b_ref, o_ref):
        xf = x_ref[...]
        mu = jnp.mean(xf, axis=-1, keepdims=True)
        xc = xf - mu
        rstd = lax.rsqrt(jnp.mean(xc * xc, axis=-1, keepdims=True) + EPS)
        ln = (xc * rstd) * g_ref[...] + b_ref[...]
        o_ref[...] = (ln * _sigmoid(ln)).astype(BF16)

    return pl.pallas_call(body, out_shape=SDS((S, D), BF16), grid=(S // tm,),
                          in_specs=[_row(tm, D), _full((1, D)), _full((1, D))], out_specs=_row(tm, D),
                          name="ln_silu_fwd", compiler_params=_cp(1))(cv, g, b)


def _ln_silu_bwd(ds, cv, g, b, dx):
    S = cv.shape[0]
    tm = min(S, TM_ROW)

    def body(ds_ref, x_ref, g_ref, b_ref, dx_ref, dcv_ref, dg_ref, db_ref, db2_ref):
        i = pl.program_id(0)

        @pl.when(i == 0)
        def _():
            dg_ref[...] = jnp.zeros_like(dg_ref)
            db_ref[...] = jnp.zeros_like(db_ref)
            db2_ref[...] = jnp.zeros_like(db2_ref)

        xf = x_ref[...]
        mu = jnp.mean(xf, axis=-1, keepdims=True)
        xc = xf - mu
        rstd = lax.rsqrt(jnp.mean(xc * xc, axis=-1, keepdims=True) + EPS)
        xhat = xc * rstd
        gv = g_ref[...]
        ln = xhat * gv + b_ref[...]
        sg = _sigmoid(ln)
        dln = ds_ref[...].astype(F32) * (sg + ln * sg * (1.0 - sg))
        dg_ref[...] += jnp.sum(dln * xhat, axis=0, keepdims=True)
        db_ref[...] += jnp.sum(dln, axis=0, keepdims=True)
        db2_ref[...] += jnp.sum(dx_ref[...], axis=0, keepdims=True)
        dxh = dln * gv
        dcv_ref[...] = rstd * (dxh - jnp.mean(dxh, axis=-1, keepdims=True)
                               - xhat * jnp.mean(dxh * xhat, axis=-1, keepdims=True))

    return pl.pallas_call(
        body, out_shape=[SDS((S, D), F32), SDS((1, D), F32), SDS((1, D), F32), SDS((1, D), F32)], grid=(S // tm,),
        in_specs=[_row(tm, D), _row(tm, D), _full((1, D)), _full((1, D)), _row(tm, D)],
        out_specs=[_row(tm, D), _full((1, D)), _full((1, D)), _full((1, D))], name="ln_silu_bwd",
        compiler_params=_cp(1),
    )(ds, cv, g, b, dx)


_GELU_C, _GELU_K = 0.7978845608028654, 0.044715


def _gelu(x, with_grad=False):
    x2 = x * x
    t = jnp.tanh(_GELU_C * (x + _GELU_K * x * x2))
    gel = 0.5 * x * (1.0 + t)
    if not with_grad:
        return gel
    return gel, 0.5 * (1.0 + t) + 0.5 * x * (1.0 - t * t) * (_GELU_C * (1.0 + 3.0 * _GELU_K * x2))


def _expm1(x):
    poly = x * (1.0 + x * (0.5 + x * (1.0 / 6.0 + x * (1.0 / 24.0 + x * (1.0 / 120.0)))))
    return jnp.where(jnp.abs(x) < 0.05, poly, jnp.exp(x) - 1.0)


def _softplus(x):
    return jnp.maximum(x, 0.0) + jnp.log1p(jnp.exp(-jnp.abs(x)))


_SCAN_UNROLL = 8
_RB = 32
_HB = 16


def _sub_blocks(n_rows, n_lanes, fn):
    def step(idx, c):
        r0 = pl.multiple_of(idx * _RB, _RB)
        for lt in range(n_lanes // LANE):
            fn(r0, lt)
        return c

    lax.fori_loop(0, n_rows // _RB, step, 0)


def _lanes(lt):
    return pl.ds(lt * LANE, LANE)


def _psum8(x):
    parts = [x[i * SUB:(i + 1) * SUB] for i in range(x.shape[0] // SUB)]
    return functools.reduce(lambda p, q: p + q, parts)


def _scan_fwd(a_s, b_s, out_ref, carry_ref, n_groups):
    row = lax.broadcasted_iota(jnp.int32, (SUB, LANE), 0)
    U = _SCAN_UNROLL

    def step(gi, carry):
        base = gi * (SUB * U)
        parts = []
        for u in range(U):
            i = pl.multiple_of(base + u * SUB, SUB)
            a8, b8 = a_s[pl.ds(i, SUB), :], b_s[pl.ds(i, SUB), :]
            for s in (1, 2, 4):
                a_sh = jnp.where(row >= s, pltpu.roll(a8, s, 0), 1.0)
                b_sh = jnp.where(row >= s, pltpu.roll(b8, s, 0), 0.0)
                b8 = a8 * b_sh + b8
                a8 = a8 * a_sh
            parts.append((i, a8, b8))
        for i, a8, b8 in parts:
            h8 = a8 * carry + b8
            out_ref[pl.ds(i, SUB), :] = h8
            carry = jnp.broadcast_to(h8[SUB - 1:SUB, :], (SUB, LANE))
        return carry

    carry_ref[...] = lax.fori_loop(0, n_groups // U, step, carry_ref[...])


def _scan_bwd(a_s, b_s, out_ref, carry_ref, n_groups):
    row = lax.broadcasted_iota(jnp.int32, (SUB, LANE), 0)
    U = _SCAN_UNROLL

    def step(gi, carry):
        base = (n_groups // U - 1 - gi) * (SUB * U)
        parts = []
        for u in reversed(range(U)):
            i = pl.multiple_of(base + u * SUB, SUB)
            a8, b8 = a_s[pl.ds(i, SUB), :], b_s[pl.ds(i, SUB), :]
            for s in (1, 2, 4):
                a_sh = jnp.where(row < SUB - s, pltpu.roll(a8, SUB - s, 0), 1.0)
                b_sh = jnp.where(row < SUB - s, pltpu.roll(b8, SUB - s, 0), 0.0)
                b8 = a8 * b_sh + b8
                a8 = a8 * a_sh
            parts.append((i, a8, b8))
        for i, a8, b8 in parts:
            h8 = a8 * carry + b8
            out_ref[pl.ds(i, SUB), :] = h8
            carry = jnp.broadcast_to(h8[0:1, :], (SUB, LANE))
        return carry

    carry_ref[...] = lax.fori_loop(0, n_groups // U, step, carry_ref[...])


def _rglru_pre(xr, wgx_ref, bgx_ref, wga_ref, bga_ref, lam_ref):
    xrb = xr.astype(BF16)
    wgx, wga = wgx_ref[0].astype(BF16), wga_ref[0].astype(BF16)
    gx = _sigmoid(lax.dot_general(xrb, wgx, _NN, preferred_element_type=F32) + bgx_ref[...])
    ga = _sigmoid(lax.dot_general(xrb, wga, _NN, preferred_element_type=F32) + bga_ref[...])
    sp = _softplus(-lam_ref[...])
    log_a = -C_RG * ga * sp
    a = jnp.exp(log_a)
    mult = jnp.sqrt(-_expm1(2.0 * log_a))
    return gx, ga, sp, a, mult, xrb, wgx, wga


def _a_specs():
    vec = pl.BlockSpec((1, HD_A), lambda c, j: (0, c))
    mat = pl.BlockSpec((1, HD_A, HD_A), lambda c, j: (c, 0, 0))
    return [pl.BlockSpec((CONV_A, HD_A), lambda c, j: (0, c)), vec, mat, vec, mat, vec, vec]


def _a_fwd(zp, conv_w, conv_b, wgx, bgx, wga, bga, lam):
    S = zp.shape[0]
    R, nt = R_RGLRU, D // HD_A
    H = SUB

    def body(zg_ref, zr_ref, cw_ref, cb_ref, wgx_ref, bgx_ref, wga_ref, bga_ref, lam_ref, ya_ref, h_ref,
             ext, a_s, b_s, hc):
        j = pl.program_id(1)

        @pl.when(j == 0)
        def _():
            ext[0:H, :] = jnp.zeros((H, HD_A), F32)
            hc[...] = jnp.zeros_like(hc)

        ext[H:H + R, :] = zr_ref[...].astype(F32)
        xr = cb_ref[...]
        for k in range(CONV_A):
            xr = xr + cw_ref[k:k + 1, :] * ext[pl.ds(H - (CONV_A - 1 - k), R), :]
        gx, _, _, a, mult, _, _, _ = _rglru_pre(xr, wgx_ref, bgx_ref, wga_ref, bga_ref, lam_ref)
        a_s[...] = a
        b_s[...] = mult * (gx * xr)
        _scan_fwd(a_s, b_s, h_ref, hc, R // SUB)
        ya_ref[...] = (_gelu(zg_ref[...].astype(F32)) * h_ref[...]).astype(BF16)
        ext[0:H, :] = ext[R:R + H, :]

    return pl.pallas_call(
        body, out_shape=[SDS((S, D + D // 2), BF16), SDS((S, D), F32)], grid=(nt, S // R),
        in_specs=[pl.BlockSpec((R, HD_A), lambda c, j: (j, c)), pl.BlockSpec((R, HD_A), lambda c, j: (j, nt + c))]
        + _a_specs(),
        out_specs=[pl.BlockSpec((R, HD_A), lambda c, j: (j, c)), pl.BlockSpec((R, HD_A), lambda c, j: (j, c))],
        scratch_shapes=[pltpu.VMEM((H + R, HD_A), F32), pltpu.VMEM((R, HD_A), F32), pltpu.VMEM((R, HD_A), F32),
                        pltpu.VMEM((SUB, HD_A), F32)],
        name="rglru_fwd", compiler_params=_cp(2),
    )(zp, zp, conv_w, conv_b, wgx, bgx, wga, bga, lam)


def _a_bwd(dyab, zp, h, conv_w, conv_b, wgx, bgx, wga, bga, lam):
    S = zp.shape[0]
    R, nt, nch = R_RGLRU, D // HD_A, S // R_RGLRU
    H = SUB

    def rows(c, j):
        return (nch - 1 - j, c)

    def rows_rec(c, j):
        return (nch - 1 - j, nt + c)

    def halo(c, j):
        return (jnp.maximum((nch - 1 - j) * (R // H) - 1, 0), c)

    def halo_z(c, j):
        return (jnp.maximum((nch - 1 - j) * (R // _HB) - 1, 0), nt + c)

    def body(dy_ref, zg_ref, zr_ref, zh_ref, h_ref, hh_ref, cw_ref, cb_ref, wgx_ref, bgx_ref, wga_ref, bga_ref,
             lam_ref, dzg_ref, dzr_ref, dcw_ref, dcb_ref, dwgx_ref, dbgx_ref, dwga_ref, dbga_ref, dlam_ref,
             ext_z, ext_h, ext_mu, ext_d, a_s, b_s, muc):
        j = pl.program_id(1)
        first_chunk = (nch - 1 - j) == 0

        @pl.when(j == 0)
        def _():
            ext_mu[R:R + H, :] = jnp.zeros((H, HD_A), F32)
            ext_d[R:R + H, :] = jnp.zeros((H, HD_A), F32)
            muc[...] = jnp.zeros_like(muc)
            for r in (dcw_ref, dcb_ref, dwgx_ref, dbgx_ref, dwga_ref, dbga_ref, dlam_ref):
                r[...] = jnp.zeros_like(r)

        zg = zg_ref[...].astype(F32)
        ext_z[0:H, :] = jnp.where(first_chunk, 0.0, zh_ref[_HB - H:_HB, :].astype(F32))
        ext_z[H:H + R, :] = zr_ref[...].astype(F32)
        ext_h[0:H, :] = jnp.where(first_chunk, 0.0, hh_ref[...])
        ext_h[H:H + R, :] = h_ref[...]
        xr = cb_ref[...]
        for k in range(CONV_A):
            xr = xr + cw_ref[k:k + 1, :] * ext_z[pl.ds(H - (CONV_A - 1 - k), R), :]
        gx, ga, sp, a, mult, xrb, wgxb, wgab = _rglru_pre(xr, wgx_ref, bgx_ref, wga_ref, bga_ref, lam_ref)
        gel, dgel = _gelu(zg, with_grad=True)
        dy = dy_ref[...].astype(F32)
        dh = dy * gel
        dzg_ref[...] = (dy * h_ref[...] * dgel).astype(BF16)
        a_s[...] = a
        b_s[...] = a * dh
        _scan_bwd(a_s, b_s, ext_mu, muc, R // SUB)
        lam_t = dh + ext_mu[pl.ds(1, R), :]
        ext_mu[R:R + H, :] = ext_mu[0:H, :]
        da = lam_t * ext_h[pl.ds(H - 1, R), :]
        gxr = gx * xr
        dlog_a = da * a - (lam_t * gxr) * (a * a) / mult
        dgx = lam_t * mult * xr
        dxr = lam_t * mult * gx
        lam_v = lam_ref[...]
        dlam_ref[...] += jnp.sum(dlog_a * ga, axis=0, keepdims=True) * (C_RG * _sigmoid(-lam_v))
        dpa = (dlog_a * (-C_RG * sp)) * ga * (1.0 - ga)
        dpx = dgx * gx * (1.0 - gx)
        dbga_ref[...] += jnp.sum(dpa, axis=0, keepdims=True)
        dbgx_ref[...] += jnp.sum(dpx, axis=0, keepdims=True)
        dpab, dpxb = dpa.astype(BF16), dpx.astype(BF16)
        dwga_ref[0] += lax.dot_general(xrb, dpab, _TN, preferred_element_type=F32)
        dwgx_ref[0] += lax.dot_general(xrb, dpxb, _TN, preferred_element_type=F32)
        dxr = (dxr + lax.dot_general(dpab, wgab, _NT, preferred_element_type=F32)
               + lax.dot_general(dpxb, wgxb, _NT, preferred_element_type=F32))
        dcb_ref[...] += jnp.sum(dxr, axis=0, keepdims=True)
        ext_d[0:R, :] = dxr
        dzr = jnp.zeros((R, HD_A), F32)
        for k in range(CONV_A):
            sh = CONV_A - 1 - k
            dcw_ref[k:k + 1, :] += jnp.sum(dxr * ext_z[pl.ds(H - sh, R), :], axis=0, keepdims=True)
            dzr = dzr + cw_ref[k:k + 1, :] * ext_d[pl.ds(sh, R), :]
        dzr_ref[...] = dzr.astype(BF16)
        ext_d[R:R + H, :] = ext_d[0:H, :]

    vec_o = pl.BlockSpec((1, HD_A), lambda c, j: (0, c))
    mat_o = pl.BlockSpec((1, HD_A, HD_A), lambda c, j: (c, 0, 0))
    return pl.pallas_call(
        body,
        out_shape=[SDS((S, D), BF16), SDS((S, D), BF16), SDS((CONV_A, D), F32), SDS((1, D), F32),
                   SDS((nt, HD_A, HD_A), F32), SDS((1, D), F32), SDS((nt, HD_A, HD_A), F32), SDS((1, D), F32),
                   SDS((1, D), F32)],
        grid=(nt, nch),
        in_specs=[pl.BlockSpec((R, HD_A), rows), pl.BlockSpec((R, HD_A), rows), pl.BlockSpec((R, HD_A), rows_rec),
                  pl.BlockSpec((_HB, HD_A), halo_z), pl.BlockSpec((R, HD_A), rows),
                  pl.BlockSpec((H, HD_A), halo)] + _a_specs(),
        out_specs=[pl.BlockSpec((R, HD_A), rows), pl.BlockSpec((R, HD_A), rows),
                   pl.BlockSpec((CONV_A, HD_A), lambda c, j: (0, c)), vec_o, mat_o, vec_o, mat_o, vec_o, vec_o],
        scratch_shapes=[pltpu.VMEM((H + R, HD_A), F32), pltpu.VMEM((H + R, HD_A), F32), pltpu.VMEM((R + H, HD_A), F32),
                        pltpu.VMEM((R + H, HD_A), F32), pltpu.VMEM((R, HD_A), F32), pltpu.VMEM((R, HD_A), F32),
                        pltpu.VMEM((SUB, HD_A), F32)],
        name="rglru_bwd", compiler_params=_cp(2),
    )(dyab, zp, zp, zp, h, h, conv_w, conv_b, wgx, bgx, wga, bga, lam)


_POOL_H = 16
_POOL_T0 = 2 * D // HD_A
_POOL_Y0 = D // HD_A


def _window_sum(lv, n, lo, rows, g, ahead):
    base = 0 if ahead else SUB
    cur, win = lv[0], None
    for i, s in enumerate((1, 2, 4, 8)):
        val = cur[pl.ds(base, n), :] + cur[pl.ds(base + (s if ahead else -s), n), :]
        sel = val[lo:lo + rows]
        win = sel if win is None else jnp.where(g >= i, sel, win)
        if i < 3:
            lv[i + 1][pl.ds(base, n), :] = val
            cur = lv[i + 1]
    return win


def _pool_width(g):
    return jnp.where(g == 0, 2.0, jnp.where(g == 1, 4.0, jnp.where(g == 2, 8.0, 16.0)))


def _b_fwd(zp, yab, wg, bg, sc):
    S = zp.shape[0]
    R, H = min(S, R_POOL), _POOL_H

    def body(z_ref, wg_ref, bg_ref, sc_ref, yab_in, yb_ref, *lv):
        del yab_in
        g, j = pl.program_id(0), pl.program_id(1)

        @pl.when(j == 0)
        def _():
            for r in lv:
                r[0:SUB, :] = jnp.zeros((SUB, HD_A), F32)
            lv[0][SUB:SUB + H, :] = jnp.zeros((H, HD_A), F32)

        u = z_ref[...].astype(F32)
        lv[0][SUB + H:SUB + H + R, :] = u
        t1 = (j * R + 1 + lax.broadcasted_iota(jnp.int32, (R, HD_A), 0)).astype(F32)
        p = _window_sum(lv, H + R, H, R, g, False) / jnp.minimum(t1, _pool_width(g)) - u
        lin = lax.dot_general(p.astype(BF16), wg_ref[0].astype(BF16), _NN, preferred_element_type=F32) + bg_ref[...]
        yb_ref[...] = (lin * sc_ref[...]).astype(BF16)
        lv[0][SUB:SUB + H, :] = lv[0][SUB + R:SUB + R + H, :]

    vec = pl.BlockSpec((1, HD_A), lambda g, j: (0, g))
    return pl.pallas_call(
        body, out_shape=SDS(yab.shape, yab.dtype), grid=(len(POOL_WINDOWS), S // R),
        in_specs=[pl.BlockSpec((R, HD_A), lambda g, j: (j, _POOL_T0 + g)),
                  pl.BlockSpec((1, HD_A, HD_A), lambda g, j: (g, 0, 0)), vec, vec, pl.BlockSpec(memory_space=pl.ANY)],
        out_specs=pl.BlockSpec((R, HD_A), lambda g, j: (j, _POOL_Y0 + g)),
        scratch_shapes=[pltpu.VMEM((SUB + H + R, HD_A), F32)] * 4, input_output_aliases={4: 0},
        name="pool_fwd", compiler_params=_cp(2),
    )(zp, wg, bg, sc, yab)


def _b_bwd(dyab, zp, wg, bg, sc):
    S = zp.shape[0]
    R, H, ng = min(S, R_POOL), _POOL_H, len(POOL_WINDOWS)
    nch = S // R

    def body(dy_ref, z_ref, zh_ref, wg_ref, bg_ref, sc_ref, dz_ref, dwg_ref, dbg_ref, dsc_ref, *scratch):
        lu, lq = scratch[:4], scratch[4:]
        g, j = pl.program_id(0), pl.program_id(1)
        jj = nch - 1 - j

        @pl.when(j == 0)
        def _():
            for r in lu:
                r[0:SUB, :] = jnp.zeros((SUB, HD_A), F32)
            for r in lq:
                r[R + H:R + H + SUB, :] = jnp.zeros((SUB, HD_A), F32)
            lq[0][R:R + H, :] = jnp.zeros((H, HD_A), F32)
            for r in (dwg_ref, dbg_ref, dsc_ref):
                r[...] = jnp.zeros_like(r)

        u = z_ref[...].astype(F32)
        lu[0][SUB:SUB + H, :] = jnp.where(jj == 0, 0.0, zh_ref[...].astype(F32))
        lu[0][SUB + H:SUB + H + R, :] = u
        t1 = (jj * R + 1 + lax.broadcasted_iota(jnp.int32, (R, HD_A), 0)).astype(F32)
        cnt = jnp.minimum(t1, _pool_width(g))
        pb = (_window_sum(lu, H + R, H, R, g, False) / cnt - u).astype(BF16)
        wgb = wg_ref[0].astype(BF16)
        lin = lax.dot_general(pb, wgb, _NN, preferred_element_type=F32) + bg_ref[...]
        dy = dy_ref[...].astype(F32)
        dsc_ref[...] += jnp.sum(dy * lin, axis=0, keepdims=True)
        dlin = dy * sc_ref[...]
        dbg_ref[...] += jnp.sum(dlin, axis=0, keepdims=True)
        dlb = dlin.astype(BF16)
        dwg_ref[0] += lax.dot_general(pb, dlb, _TN, preferred_element_type=F32)
        dp = lax.dot_general(dlb, wgb, _NT, preferred_element_type=F32)
        lq[0][0:R, :] = dp / cnt
        dz_ref[...] = (_window_sum(lq, R + H, 0, R, g, True) - dp).astype(BF16)
        lq[0][R:R + H, :] = lq[0][0:H, :]

    vec = pl.BlockSpec((1, HD_A), lambda g, j: (0, g))
    mat = pl.BlockSpec((1, HD_A, HD_A), lambda g, j: (g, 0, 0))
    return pl.pallas_call(
        body, out_shape=[SDS((S, D // 2), BF16), SDS((ng, HD_A, HD_A), F32), SDS((1, D // 2), F32),
                         SDS((1, D // 2), F32)],
        grid=(ng, nch),
        in_specs=[pl.BlockSpec((R, HD_A), lambda g, j: (nch - 1 - j, _POOL_Y0 + g)),
                  pl.BlockSpec((R, HD_A), lambda g, j: (nch - 1 - j, _POOL_T0 + g)),
                  pl.BlockSpec((H, HD_A), lambda g, j: (jnp.maximum((nch - 1 - j) * (R // H) - 1, 0), _POOL_T0 + g)),
                  mat, vec, vec],
        out_specs=[pl.BlockSpec((R, HD_A), lambda g, j: (nch - 1 - j, g)), mat, vec, vec],
        scratch_shapes=[pltpu.VMEM((SUB + H + R, HD_A), F32)] * 8,
        name="pool_bwd", compiler_params=_cp(2),
    )(dyab, zp, zp, wg, bg, sc)


_CW_F = 768


def _f_fwd(hp, w, b, name):
    S = hp.shape[0]
    R, H, cw = min(S, R_FFN), SUB, _CW_F
    nlt = cw // LANE

    def body(h_ref, w_ref, b_ref, o_ref, gel_ref, ud_ref, ext):
        j = pl.program_id(1)

        @pl.when(j == 0)
        def _():
            ext[:, 0:H, :] = jnp.zeros((nlt, H, LANE), F32)

        def stage(r0, lt):
            ext[lt, pl.ds(pl.multiple_of(r0 + H, SUB), _RB), :] = h_ref[pl.ds(r0, _RB), _lanes(lt)].astype(F32)

        def main(r0, lt):
            ls = _lanes(lt)
            gp = b_ref[:, ls]
            for k in range(CONV_F):
                gp = gp + w_ref[k:k + 1, ls] * ext[lt, pl.ds(r0 + (H - (CONV_F - 1 - k)), _RB), :]
            up = h_ref[pl.ds(r0, _RB), _lanes(lt + nlt)].astype(F32)
            gel, dgel = _gelu(gp, with_grad=True)
            rs = pl.ds(r0, _RB)
            o_ref[rs, ls] = (gel * up).astype(BF16)
            gel_ref[rs, ls] = gel.astype(BF16)
            ud_ref[rs, ls] = (up * dgel).astype(BF16)

        _sub_blocks(R, cw, stage)
        _sub_blocks(R, cw, main)
        ext[:, 0:H, :] = ext[:, R:R + H, :]

    tile = pl.BlockSpec((R, cw), lambda c, j: (j, c))
    return pl.pallas_call(
        body, out_shape=[SDS((S, D_FF), BF16)] * 3, grid=(D_FF // cw, S // R),
        in_specs=[pl.BlockSpec((R, 2 * cw), lambda c, j: (j, c)), pl.BlockSpec((CONV_F, cw), lambda c, j: (0, c)),
                  pl.BlockSpec((1, cw), lambda c, j: (0, c))],
        out_specs=[tile] * 3,
        scratch_shapes=[pltpu.VMEM((nlt, H + R, LANE), F32)], name=name, compiler_params=_cp(2),
    )(hp, w, b)


def _f_bwd(dact, hp, gel, ud, w, name):
    S = hp.shape[0]
    R, H, cw = min(S, R_FFN), SUB, _CW_F
    nch = S // R
    nlt = cw // LANE

    def body(da_ref, h_ref, hh_ref, gel_ref, ud_ref, w_ref, dh_ref, dw_ref, db_ref, ext_g, ext_d, acc):
        j = pl.program_id(1)
        jj = nch - 1 - j

        @pl.when(j == 0)
        def _():
            ext_d[:, R:R + H, :] = jnp.zeros((nlt, H, LANE), F32)
            acc[...] = jnp.zeros_like(acc)

        for lt in range(nlt):
            ext_g[lt, 0:H, :] = jnp.where(jj == 0, 0.0, hh_ref[_HB - H:_HB, lt * LANE:(lt + 1) * LANE].astype(F32))

        def stage(r0, lt):
            ext_g[lt, pl.ds(pl.multiple_of(r0 + H, SUB), _RB), :] = h_ref[pl.ds(r0, _RB), _lanes(lt)].astype(F32)

        def first(r0, lt):
            ls, lu, rs = _lanes(lt), _lanes(lt + nlt), pl.ds(r0, _RB)
            da = da_ref[rs, ls].astype(F32)
            dh_ref[rs, lu] = (da * gel_ref[rs, ls].astype(F32)).astype(BF16)
            dgp = da * ud_ref[rs, ls].astype(F32)
            ext_d[lt, rs, :] = dgp
            acc[CONV_F * SUB:(CONV_F + 1) * SUB, ls] += _psum8(dgp)
            for k in range(CONV_F):
                tap = ext_g[lt, pl.ds(r0 + (H - (CONV_F - 1 - k)), _RB), :]
                acc[k * SUB:(k + 1) * SUB, ls] += _psum8(dgp * tap)

        def second(r0, lt):
            ls = _lanes(lt)
            dhg = w_ref[CONV_F - 1:CONV_F, ls] * ext_d[lt, pl.ds(r0, _RB), :]
            for k in range(CONV_F - 1):
                dhg = dhg + w_ref[k:k + 1, ls] * ext_d[lt, pl.ds(r0 + (CONV_F - 1 - k), _RB), :]
            dh_ref[pl.ds(r0, _RB), ls] = dhg.astype(BF16)

        _sub_blocks(R, cw, stage)
        _sub_blocks(R, cw, first)
        _sub_blocks(R, cw, second)
        ext_d[:, R:R + H, :] = ext_d[:, 0:H, :]

        @pl.when(j == nch - 1)
        def _():
            for k in range(CONV_F):
                dw_ref[k:k + 1, :] = jnp.sum(acc[k * SUB:(k + 1) * SUB, :], axis=0, keepdims=True)
            db_ref[...] = jnp.sum(acc[CONV_F * SUB:(CONV_F + 1) * SUB, :], axis=0, keepdims=True)

    rows = lambda c, j: (nch - 1 - j, c)
    return pl.pallas_call(
        body, out_shape=[SDS((S, 2 * D_FF), BF16), SDS((CONV_F, D_FF), F32), SDS((1, D_FF), F32)],
        grid=(D_FF // cw, nch),
        in_specs=[pl.BlockSpec((R, cw), rows), pl.BlockSpec((R, cw), lambda c, j: (nch - 1 - j, 2 * c)),
                  pl.BlockSpec((_HB, cw), lambda c, j: (jnp.maximum((nch - 1 - j) * (R // _HB) - 1, 0), 2 * c)),
                  pl.BlockSpec((R, cw), rows), pl.BlockSpec((R, cw), rows),
                  pl.BlockSpec((CONV_F, cw), lambda c, j: (0, c))],
        out_specs=[pl.BlockSpec((R, 2 * cw), rows), pl.BlockSpec((CONV_F, cw), lambda c, j: (0, c)),
                   pl.BlockSpec((1, cw), lambda c, j: (0, c))],
        scratch_shapes=[pltpu.VMEM((nlt, H + R, LANE), F32), pltpu.VMEM((nlt, R + H, LANE), F32),
                        pltpu.VMEM(((CONV_F + 1) * SUB, cw), F32)], name=name,
        compiler_params=_cp(2),
    )(dact, hp, hp, gel, ud, w)


_CW_C = 256
_H_C = 32


def _c_fwd(h1p, w, b):
    S = h1p.shape[0]
    R, H, cw = R_SEQ, _H_C, _CW_C
    nlt = cw // LANE

    def body(h_ref, w_ref, b_ref, o_ref, ext):
        j = pl.program_id(1)

        @pl.when(j == 0)
        def _():
            ext[:, 0:H, :] = jnp.zeros((nlt, H, LANE), F32)

        def stage(r0, lt):
            rs = pl.ds(r0, _RB)
            gate = h_ref[rs, _lanes(lt + nlt)].astype(F32)
            ext[lt, pl.ds(pl.multiple_of(r0 + H, SUB), _RB), :] = h_ref[rs, _lanes(lt)].astype(F32) * _sigmoid(gate)

        def main(r0, lt):
            ls = _lanes(lt)
            cv = b_ref[:, ls]
            for k in range(CONV_C):
                cv = cv + w_ref[k:k + 1, ls] * ext[lt, pl.ds(r0 + (H - (CONV_C - 1 - k)), _RB), :]
            o_ref[pl.ds(r0, _RB), ls] = cv

        _sub_blocks(R, cw, stage)
        _sub_blocks(R, cw, main)
        ext[:, 0:H, :] = ext[:, R:R + H, :]

    return pl.pallas_call(
        body, out_shape=SDS((S, D), F32), grid=(D // cw, S // R),
        in_specs=[pl.BlockSpec((R, 2 * cw), lambda c, j: (j, c)), pl.BlockSpec((CONV_C, cw), lambda c, j: (0, c)),
                  pl.BlockSpec((1, cw), lambda c, j: (0, c))],
        out_specs=pl.BlockSpec((R, cw), lambda c, j: (j, c)),
        scratch_shapes=[pltpu.VMEM((nlt, H + R, LANE), F32)], name="conf_conv_fwd", compiler_params=_cp(2),
    )(h1p, w, b)


def _c_bwd(dcv, h1p, w):
    S = h1p.shape[0]
    R, H, cw, nch = R_SEQ, _H_C, _CW_C, S // R_SEQ
    nlt = cw // LANE
    a_b, a_val, a_gate = CONV_C * SUB, (CONV_C + 1) * SUB, (CONV_C + 2) * SUB

    def body(dc_ref, h_ref, hh_ref, w_ref, dh_ref, dw_ref, db_ref, db1_ref, ext_u, ext_d, acc):
        j = pl.program_id(1)
        jj = nch - 1 - j

        @pl.when(j == 0)
        def _():
            ext_d[:, R:R + H, :] = jnp.zeros((nlt, H, LANE), F32)
            acc[...] = jnp.zeros_like(acc)

        for lt in range(nlt):
            ext_u[lt, 0:H, :] = jnp.where(
                jj == 0, 0.0, hh_ref[:, lt * LANE:(lt + 1) * LANE].astype(F32)
                * _sigmoid(hh_ref[:, cw + lt * LANE:cw + (lt + 1) * LANE].astype(F32)))

        def stage(r0, lt):
            rs, ls = pl.ds(r0, _RB), _lanes(lt)
            gate = h_ref[rs, _lanes(lt + nlt)].astype(F32)
            ext_u[lt, pl.ds(pl.multiple_of(r0 + H, SUB), _RB), :] = h_ref[rs, ls].astype(F32) * _sigmoid(gate)
            ext_d[lt, rs, :] = dc_ref[rs, ls]

        def first(r0, lt):
            ls = _lanes(lt)
            dc = dc_ref[pl.ds(r0, _RB), ls]
            acc[a_b:a_b + SUB, ls] += _psum8(dc)
            for k in range(CONV_C):
                tap = ext_u[lt, pl.ds(r0 + (H - (CONV_C - 1 - k)), _RB), :]
                acc[k * SUB:(k + 1) * SUB, ls] += _psum8(dc * tap)

        def second(r0, lt):
            rs, ls, lg = pl.ds(r0, _RB), _lanes(lt), _lanes(lt + nlt)
            du = w_ref[CONV_C - 1:CONV_C, ls] * ext_d[lt, rs, :]
            for k in range(CONV_C - 1):
                du = du + w_ref[k:k + 1, ls] * ext_d[lt, pl.ds(r0 + (CONV_C - 1 - k), _RB), :]
            val = h_ref[rs, ls].astype(F32)
            sg = _sigmoid(h_ref[rs, lg].astype(F32))
            dval = du * sg
            dgate = du * val * sg * (1.0 - sg)
            acc[a_val:a_val + SUB, ls] += _psum8(dval)
            acc[a_gate:a_gate + SUB, ls] += _psum8(dgate)
            dh_ref[rs, ls] = dval.astype(BF16)
            dh_ref[rs, lg] = dgate.astype(BF16)

        _sub_blocks(R, cw, stage)
        _sub_blocks(R, cw, first)
        _sub_blocks(R, cw, second)
        ext_d[:, R:R + H, :] = ext_d[:, 0:H, :]

        @pl.when(j == nch - 1)
        def _():
            for k in range(CONV_C):
                dw_ref[k:k + 1, :] = jnp.sum(acc[k * SUB:(k + 1) * SUB, :], axis=0, keepdims=True)
            db_ref[...] = jnp.sum(acc[a_b:a_b + SUB, :], axis=0, keepdims=True)
            db1_ref[:, 0:cw] = jnp.sum(acc[a_val:a_val + SUB, :], axis=0, keepdims=True)
            db1_ref[:, cw:2 * cw] = jnp.sum(acc[a_gate:a_gate + SUB, :], axis=0, keepdims=True)

    rows = lambda c, j: (nch - 1 - j, c)
    return pl.pallas_call(
        body, out_shape=[SDS((S, 2 * D), BF16), SDS((CONV_C, D), F32), SDS((1, D), F32), SDS((1, 2 * D), F32)],
        grid=(D // cw, nch),
        in_specs=[pl.BlockSpec((R, cw), rows), pl.BlockSpec((R, 2 * cw), rows),
                  pl.BlockSpec((H, 2 * cw), lambda c, j: (jnp.maximum((nch - 1 - j) * (R // H) - 1, 0), c)),
                  pl.BlockSpec((CONV_C, cw), lambda c, j: (0, c))],
        out_specs=[pl.BlockSpec((R, 2 * cw), rows), pl.BlockSpec((CONV_C, cw), lambda c, j: (0, c)),
                   pl.BlockSpec((1, cw), lambda c, j: (0, c)), pl.BlockSpec((1, 2 * cw), lambda c, j: (0, c))],
        scratch_shapes=[pltpu.VMEM((nlt, H + R, LANE), F32), pltpu.VMEM((nlt, R + H, LANE), F32),
                        pltpu.VMEM(((CONV_C + 3) * SUB, cw), F32)], name="conf_conv_bwd",
        compiler_params=_cp(2),
    )(dcv, h1p, h1p, w)


def _local_step(x, mem, tgt, W, fetch=None, send=None):
    G = {}
    W = dict(W)

    def arrive(group, after):
        if fetch is None:
            return None
        got, tok = fetch(group, after)
        for key, val in got.items():
            W[key] = {**W.get(key, {}), **val} if isinstance(val, dict) else val
        return tok

    def gain(g, tok):
        return g if tok is None else g + tok

    def sent(group):
        return None if send is None else send(group, G)

    def xattn_fwd(xin, n, l):
        tok = arrive(("xa", l), n)
        mn = _rms_fwd(mem, gain(W["xa_mem_norm"][l:l + 1], tok), f"xa_memnorm_fwd{l}")
        q = _mm_nn(n, W["xa_wq"][l], out_dtype=BF16, name=f"xa_q{l}")
        k = _mm_nn(mn, W["xa_wk"][l], out_dtype=BF16, name=f"xa_k{l}")
        v = _mm_nn(mn, W["xa_wv"][l], out_dtype=BF16, name=f"xa_v{l}")
        o = _attn_fwd(q, k, v, f"xa_attn_fwd{l}")
        xout, nout = _mm_nn(o, W["xa_wo"][l], out_dtype=F32, name=f"xa_o{l}", add=xin, norm=W["f_norm"][l:l + 1])
        return xout, nout, (xin, n, q, mn, k, v, o)

    def xattn_bwd(dx, dxb, saved, l):
        xin, n, q, mn, k, v, o = saved
        do = _mm_nt(dxb, W["xa_wo"][l], out_dtype=BF16, name=f"xa_do{l}")
        G[f"xa_wo{l}"] = _mm_tn(o, dxb, out_dtype=BF16, name=f"xa_dwo{l}")
        dq, dk, dv = _attn_bwd(q, k, v, do, f"xa_attn_bwd{l}")
        dkb, dvb = dk.astype(BF16), dv.astype(BF16)
        G[f"xa_wq{l}"] = _mm_tn(n, dq, out_dtype=BF16, name=f"xa_dwq{l}")
        G[f"xa_wk{l}"] = _mm_tn(mn, dkb, out_dtype=BF16, name=f"xa_dwk{l}")
        G[f"xa_wv{l}"] = _mm_tn(mn, dvb, out_dtype=BF16, name=f"xa_dwv{l}")
        tok = sent(("xa", l))
        dmn = _mm_nt(dkb, W["xa_wk"][l], out_dtype=F32, name=f"xa_dmn_k{l}")
        dmn = _mm_nt(dvb, W["xa_wv"][l], out_dtype=F32, name=f"xa_dmn_v{l}", add=dmn)
        (G[f"xa_mem_norm{l}"],) = _rms_bwd(mem, W["xa_mem_norm"][l:l + 1], dmn, None, f"xa_memnorm_bwd{l}")
        dx, dxb, G[f"xa_norm{l}"] = _mm_nt(dq, W["xa_wq"][l], out_dtype=F32, name=f"xa_dn{l}",
                                           rms=(xin, gain(W["xa_norm"][l:l + 1], tok), dx))
        return dx, dxb

    def ffn_fwd(xin, n, l, next_gain):
        tok = arrive(("f", l), n)
        hp = _mm_nn(n, W["f_w_up"][l], out_dtype=BF16, name=f"f_up{l}")
        act, gel, ud = _f_fwd(hp, W["f_dw_w"][l], gain(W["f_dw_b"][l:l + 1], tok), f"f_conv_fwd{l}")
        arrive(("fd", l), act)
        res = _mm_nn(act, W["f_w_down"][l], out_dtype=F32, name=f"f_down{l}", add=xin, norm=next_gain)
        xout, nout = res if next_gain is not None else (res, None)
        return xout, nout, (xin, n, hp, act, gel, ud)

    def ffn_bwd(dx, dxb, saved, l):
        xin, n, hp, act, gel, ud = saved
        dact = _mm_nt(dxb, W["f_w_down"][l], out_dtype=BF16, name=f"f_dact{l}")
        G[f"f_w_down{l}"] = _mm_tn(act, dxb, out_dtype=BF16, name=f"f_dwdown{l}")
        dhp, G[f"f_dw_w{l}"], G[f"f_dw_b{l}"] = _f_bwd(dact, hp, gel, ud, W["f_dw_w"][l], f"f_conv_bwd{l}")
        G[f"f_w_up{l}"] = _mm_tn(n, dhp, out_dtype=BF16, name=f"f_dwup{l}", blocks=_CW_F)
        tok = sent(("f", l))
        dx, dxb, G[f"f_norm{l}"] = _mm_nt(dhp, W["f_w_up"][l], out_dtype=F32, name=f"f_dn{l}",
                                          rms=(xin, gain(W["f_norm"][l:l + 1], tok), dx))
        return dx, dxb

    n0 = _rms_fwd(x, W["ab_norm"], "ab_norm_fwd")
    tok = arrive(("ab", 0), n0)
    a_par = (W["a_conv_w"], gain(W["a_conv_b"], tok), W["a_gate_x_w"], W["a_gate_x_b"], W["a_gate_a_w"],
             W["a_gate_a_b"], W["a_lambda"])
    b_par = (W["b_group_w"], W["b_group_b"], W["b_scale"])
    zp = _mm_nn(n0, W["ab_w_in"], out_dtype=BF16, name="ab_in")
    yab, h_a = _a_fwd(zp, *a_par)
    yab = _b_fwd(zp, yab, *b_par)
    arrive(("ab", 1), yab)
    x1, n1 = _mm_nn(yab, W["ab_w_out"], out_dtype=F32, name="ab_out", add=x, norm=W["xa_norm"][0:1])
    x2, n2, s_xa0 = xattn_fwd(x1, n1, 0)
    x3, n3, s_f0 = ffn_fwd(x2, n2, 0, W["c_norm"])
    tok = arrive(("c", 0), n3)
    h1p = _mm_nn(n3, W["c_w_pw1"], out_dtype=BF16, name="c_pw1", bias=gain(W["c_b_pw1"], tok))
    cv = _c_fwd(h1p, W["c_dw_w"], W["c_dw_b"])
    sc = _ln_silu_fwd(cv, W["c_ln_g"], W["c_ln_b"])
    x4, n4 = _mm_nn(sc, W["c_w_pw2"], out_dtype=F32, name="c_pw2", bias=W["c_b_pw2"], add=x3, norm=W["xa_norm"][1:2])
    x5, n5, s_xa1 = xattn_fwd(x4, n4, 1)
    x6, _, s_f1 = ffn_fwd(x5, n5, 1, None)
    loss, dx, dxb, G["final_norm"] = _loss_head(x6, W["final_norm"], tgt)

    dx, dxb = ffn_bwd(dx, dxb, s_f1, 1)
    dx, dxb = xattn_bwd(dx, dxb, s_xa1, 1)
    dsc = _mm_nt(dxb, W["c_w_pw2"], out_dtype=BF16, name="c_dsc")
    G["c_w_pw2"] = _mm_tn(sc, dxb, out_dtype=BF16, name="c_dwpw2")
    dcv, G["c_ln_g"], G["c_ln_b"], G["c_b_pw2"] = _ln_silu_bwd(dsc, cv, W["c_ln_g"], W["c_ln_b"], dx)
    dh1p, G["c_dw_w"], G["c_dw_b"], G["c_b_pw1"] = _c_bwd(dcv, h1p, W["c_dw_w"])
    G["c_w_pw1"] = _mm_tn(n3, dh1p, out_dtype=BF16, name="c_dwpw1", blocks=_CW_C)
    tok = sent(("c", 0))
    dx, dxb, G["c_norm"] = _mm_nt(dh1p, W["c_w_pw1"], out_dtype=F32, name="c_dn",
                                  rms=(x3, gain(W["c_norm"], tok), dx))
    dx, dxb = ffn_bwd(dx, dxb, s_f0, 0)
    dx, dxb = xattn_bwd(dx, dxb, s_xa0, 0)
    dyab = _mm_nt(dxb, W["ab_w_out"], out_dtype=BF16, name="ab_dyab")
    G["ab_w_out"] = _mm_tn(yab, dxb, out_dtype=BF16, name="ab_dwout")
    tok = sent(("ab", 1))
    a_par = (a_par[0], gain(a_par[1], tok)) + a_par[2:]
    (dzg, dzr, G["a_conv_w"], G["a_conv_b"], G["a_gate_x_w"], G["a_gate_x_b"], G["a_gate_a_w"], G["a_gate_a_b"],
     G["a_lambda"]) = _a_bwd(dyab, zp, h_a, *a_par)
    dzq, G["b_group_w"], G["b_group_b"], G["b_scale"] = _b_bwd(dyab, zp, *b_par)
    G["ab_w_in"] = jnp.concatenate(
        [_mm_tn(n0, dz, out_dtype=BF16, name=f"ab_dwin_{part}")
         for part, dz in (("gate", dzg), ("rec", dzr), ("pool", dzq))], axis=1)
    tok = sent(("ab", 0))
    dx, _, G["ab_norm"] = _mm_nt_cols([dzg, dzr, dzq], W["ab_w_in"], name="ab_dn",
                                      rms=(x, gain(W["ab_norm"], tok), dx))
    return loss, dx, G


def _my_place():
    x, y, c = lax.axis_index("x"), lax.axis_index("y"), lax.axis_index("c")
    return x, y, c


def _all_gather(shards, name):
    n = len(shards)

    def body(*refs):
        ins, outs = refs[:n], refs[n:2 * n]
        send_sems, recv_sems, local_sems = refs[2 * n:]
        x, y, c = _my_place()
        me, sibling = (x, y, c), (x, y, 1 - c)
        chips = [(1 - x, y), (x, 1 - y), (1 - x, 1 - y)]

        def slab(a, place):
            px, py, pc = place
            return outs[a].at[4 * px + 2 * py + pc]

        def copy(a, k, block, to, src=None):
            return pltpu.make_async_remote_copy(
                src_ref=slab(a, block) if src is None else src, dst_ref=slab(a, block),
                send_sem=send_sems.at[a, k], recv_sem=recv_sems.at[a, k], device_id=to, device_id_type=MESH)

        mine = [pltpu.make_async_copy(ins[a], slab(a, me), local_sems.at[a]) for a in range(n)]
        for cp in mine:
            cp.start()
        first = []
        for j, chip in enumerate(chips):
            first += [copy(a, 1 + j, me, (*chip, c), src=ins[a]) for a in range(n)]
        first += [copy(a, 0, me, sibling, src=ins[a]) for a in range(n)]
        for cp in first:
            cp.start()
        passed = []
        for j, chip in enumerate(chips):
            for a in range(n):
                copy(a, 1 + j, (*chip, c), me).wait_recv()
                cp = copy(a, 4 + j, (*chip, c), sibling)
                cp.start()
                passed.append(cp)
        for a in range(n):
            copy(a, 0, sibling, me).wait_recv()
        for j, chip in enumerate(chips):
            for a in range(n):
                copy(a, 4 + j, (*chip, 1 - c), me).wait_recv()
        for cp in first + passed:
            cp.wait_send()
        for cp in mine:
            cp.wait()

    any_spec = pl.BlockSpec(memory_space=pl.ANY)
    return pl.pallas_call(
        body, out_shape=[SDS((N_DEV,) + s.shape, s.dtype) for s in shards], in_specs=[any_spec] * n,
        out_specs=[any_spec] * n,
        scratch_shapes=[pltpu.SemaphoreType.DMA((n, 7)), pltpu.SemaphoreType.DMA((n, 7)), pltpu.SemaphoreType.DMA((n,))],
        name=name,
    )(*shards)


_HBM = pl.BlockSpec(memory_space=pltpu.HBM)
_SEM = pl.BlockSpec(memory_space=pltpu.SEMAPHORE)
_EFFECT = pltpu.SideEffectType.DATAFLOW_SIDE_EFFECTING


def _peer_places():
    x, y, c = _my_place()
    peers = []
    for k in range(1, N_DEV):
        px = 1 - x if (k >> 2) & 1 else x
        py = 1 - y if (k >> 1) & 1 else y
        pc = 1 - c if k & 1 else c
        peers.append(((px, py, pc), 4 * px + 2 * py + pc))
    return (x, y, c), 4 * x + 2 * y + c, peers


def _send_start(srcs, per_dest, name):
    n = len(srcs)
    lands = [lax.empty((N_DEV,) + (s.shape[1:] if per_dest else s.shape), s.dtype) for s in srcs]

    def body(*refs):
        src, land = refs[:n], refs[n:2 * n]
        outs = refs[2 * n:]
        send, recv, token = outs[:n], outs[n:2 * n], outs[4 * n]
        _, me, peers = _peer_places()
        for a in range(n):
            for peer, pidx in peers:
                pltpu.make_async_remote_copy(
                    src_ref=src[a].at[pidx] if per_dest else src[a], dst_ref=land[a].at[me], send_sem=send[a],
                    recv_sem=recv[a], device_id=peer, device_id_type=MESH).start()
        token[...] = jnp.zeros_like(token)

    hbm = lambda a: pltpu.HBM(a.shape, a.dtype)
    sem = pltpu.SemaphoreType.DMA(())
    res = pl.pallas_call(
        body, name=name,
        out_shape=tuple([sem] * (2 * n) + [hbm(s) for s in srcs] + [hbm(l) for l in lands]
                        + [SDS((SUB, LANE), F32)]),
        in_specs=[_HBM] * (2 * n),
        out_specs=tuple([_SEM] * (2 * n) + [_HBM] * (2 * n) + [pl.BlockSpec(memory_space=pltpu.VMEM)]),
        input_output_aliases={i: 2 * n + i for i in range(2 * n)},
        compiler_params=pltpu.CompilerParams(has_side_effects=_EFFECT),
    )(*[pltpu.with_memory_space_constraint(s, pltpu.HBM) for s in srcs],
      *[pltpu.with_memory_space_constraint(l, pltpu.HBM) for l in lands])
    return res[:n], res[n:2 * n], res[2 * n:3 * n], res[3 * n:4 * n], res[4 * n]


def _send_wait(send, recv, srcs, lands, after, per_dest, name):
    n = len(srcs)

    def body(*refs):
        src, land = refs[:n], refs[n:2 * n]
        send_s, recv_s = refs[2 * n:3 * n], refs[3 * n:4 * n]
        token = refs[-1]
        place, _, _ = _peer_places()
        for a in range(n):
            seven = land[a].at[pl.ds(0, N_DEV - 1)]
            copy = pltpu.make_async_remote_copy(
                src_ref=src[a].at[pl.ds(0, N_DEV - 1)] if per_dest else seven, dst_ref=seven, send_sem=send_s[a],
                recv_sem=recv_s[a], device_id=place, device_id_type=MESH)
            copy.wait_send()
            copy.wait_recv()
        token[...] = jnp.zeros_like(token)

    hbm = lambda a: pltpu.HBM(a.shape, a.dtype)
    res = pl.pallas_call(
        body, name=name,
        out_shape=tuple([hbm(s) for s in srcs] + [hbm(l) for l in lands] + [SDS((SUB, LANE), F32)]),
        in_specs=[_HBM] * (2 * n) + [_SEM] * (2 * n) + [pl.BlockSpec(memory_space=pl.ANY)],
        out_specs=tuple([_HBM] * (2 * n) + [pl.BlockSpec(memory_space=pltpu.VMEM)]),
        input_output_aliases={i: i for i in range(2 * n)},
        compiler_params=pltpu.CompilerParams(has_side_effects=_EFFECT),
    )(*srcs, *lands, *send, *recv, after)
    return res[:n], res[n:2 * n], res[2 * n]


def _adamw_math(w, g, m, v):
    m = ADAM_B1 * m + (1.0 - ADAM_B1) * g
    v = ADAM_B2 * v + (1.0 - ADAM_B2) * (g * g)
    m_hat = m / (1.0 - ADAM_B1 ** ADAM_STEP)
    v_hat = v / (1.0 - ADAM_B2 ** ADAM_STEP)
    delta = -ADAM_LR * (m_hat / (jnp.sqrt(v_hat) + ADAM_EPS) + ADAM_WD * w)
    return delta, m, v


def _row_tile(r, c, itemsize_rows):
    cap = max(SUB, (itemsize_rows // (4 * c)) // SUB * SUB)
    if r <= cap:
        return r
    best = None
    for t in range(SUB, cap + 1, SUB):
        if r % t == 0:
            best = t
    return best if best is not None else r


def _sum_adamw(landing, w, m, v, name, layer=0, prev=None, after=None):
    _, r, c = landing.shape
    tr = _row_tile(r, c, 2 << 20)
    off = layer * (r // tr)
    tail = ([] if prev is None else list(prev)) + ([] if after is None else [after])

    def body(l_ref, w_ref, m_ref, v_ref, *rest):
        g_ref, d_ref, mo_ref, vo_ref = rest[-4:]
        g = l_ref[0].astype(F32)
        for s in range(1, N_DEV):
            g = g + l_ref[s].astype(F32)
        g_ref[...] = g
        d_ref[...], mo_ref[...], vo_ref[...] = _adamw_math(w_ref[...], g, m_ref[...], v_ref[...])

    blk = pl.BlockSpec((tr, c), lambda i: (i + off, 0))
    n_prev = 0 if prev is None else 4
    return pl.pallas_call(
        body, out_shape=[SDS(w.shape, F32)] * 4, grid=(r // tr,),
        in_specs=[pl.BlockSpec((N_DEV, tr, c), lambda i: (0, i, 0)), blk, blk, blk]
        + [pl.BlockSpec(memory_space=pl.ANY)] * len(tail),
        out_specs=[blk] * 4, input_output_aliases={4 + i: i for i in range(n_prev)}, name=name,
        compiler_params=_cp(1),
    )(landing, w, m, v, *tail)


def _sum8(landing, name):
    _, r, c = landing.shape

    def body(l_ref, g_ref):
        g = l_ref[0]
        for s in range(1, N_DEV):
            g = g + l_ref[s]
        g_ref[...] = g

    return pl.pallas_call(body, out_shape=SDS((r, c), F32), name=name, compiler_params=_cp(0))(landing)


def _adamw_small(repl_pack, own_pack, P, M, V):
    table, off = [], 0
    for name, shape in _REPL.items():
        table.append((name, shape if len(shape) > 1 else (1,) + shape, 0, off // LANE))
        off += _size(shape)
    off = _REPL_ROWS * LANE
    for name, shape in _SMALL_SHARDED.items():
        table.append((name, shape, 1, off // LANE))
        off += _size(shape)
    n = len(table)

    def body(*refs):
        packs, ins, outs = refs[:2], refs[2:2 + 3 * n], refs[2 + 3 * n:]
        for p, (_, shape, which, r0) in enumerate(table):
            w_ref, m_ref, v_ref = ins[3 * p:3 * p + 3]
            g_ref, d_ref, mo_ref, vo_ref = outs[4 * p:4 * p + 4]
            pack, rows, q = packs[which], shape[-2], shape[-1] // LANE
            lead = [()]
            for dim in shape[:-2]:
                lead = [t + (i,) for t in lead for i in range(dim)]
            for li, idx in enumerate(lead):
                if q == 1:
                    dst = g_ref.at[idx] if idx else g_ref
                    dst[...] = pack[r0 + li * rows:r0 + (li + 1) * rows, :]
                    continue
                for i in range(rows):
                    for k in range(q):
                        row = r0 + (li * rows + i) * q + k
                        g_ref[idx + (slice(i, i + 1), slice(k * LANE, (k + 1) * LANE))] = pack[row:row + 1, :]
            d_ref[...], mo_ref[...], vo_ref[...] = _adamw_math(w_ref[...], g_ref[...], m_ref[...], v_ref[...])

    ins, out_shape = [], []
    for name, shape, _, _ in table:
        ins += [t[name].reshape(shape) for t in (P, M, V)]
        out_shape += [SDS(shape, F32)] * 4
    res = pl.pallas_call(body, out_shape=out_shape, name="adamw_small", compiler_params=_cp(0))(
        repl_pack, own_pack, *ins)
    dicts = ({}, {}, {}, {})
    for p, (name, shape, _, _) in enumerate(table):
        for d, arr in zip(dicts, res[4 * p:4 * p + 4]):
            d[name] = arr.reshape(P[name].shape)
    return dicts


_BIG = {
    "ab_w_in": (1, D, 320), "ab_w_out": (1, 192, D), "c_w_pw1": (1, D, 256), "c_w_pw2": (1, 128, D),
    "xa_wq": (2, 128, D), "xa_wk": (2, 128, D), "xa_wv": (2, 128, D), "xa_wo": (2, 128, D),
    "f_w_up": (2, D, 768), "f_w_down": (2, 384, D),
}
_SMALL_SHARDED = {
    "a_conv_w": (1, 4, 128), "c_norm": (1, 128), "c_b_pw1": (1, 256), "c_dw_w": (1, 31, 128), "c_dw_b": (1, 128),
    "c_ln_g": (1, 128), "c_ln_b": (1, 128), "c_b_pw2": (1, 128), "f_dw_w": (2, 3, 384),
}
_REPL = {
    "ab_norm": (1, D), "a_conv_b": (1, D), "a_gate_x_w": (1, 8, 128, 128), "a_gate_x_b": (1, D),
    "a_gate_a_w": (1, 8, 128, 128), "a_gate_a_b": (1, D), "a_lambda": (1, D), "b_group_w": (1, 4, 128, 128),
    "b_group_b": (1, 512), "b_scale": (1, 512), "xa_norm": (2, D), "xa_mem_norm": (2, D), "f_norm": (2, D),
    "f_dw_b": (2, D_FF), "final_norm": (D,),
}


def _size(shape):
    n = 1
    for s in shape:
        n *= s
    return n


_N_SS = sum(_size(s) for s in _SMALL_SHARDED.values())
_N_REPL = sum(_size(s) for s in _REPL.values())
_REPL_ROWS = -(-_N_REPL // (N_DEV * SUB * LANE)) * SUB
_SS_ROWS = _N_SS // LANE
_SMALL_ROWS = -(-(_REPL_ROWS + _SS_ROWS) // SUB) * SUB


def _pack(parts, rows):
    flat = jnp.concatenate([p.reshape(-1).astype(F32) for p in parts])
    return jnp.pad(flat, (0, rows * LANE - flat.shape[0])).reshape(rows, LANE)


def _pair_blocks(v, bw):
    lead, n = v.shape[:-1], v.shape[-1]
    return jnp.swapaxes(v.reshape(lead + (2, n // (2 * bw), bw)), -3, -2).reshape(lead + (n,))


def _unpair_blocks(v, bw):
    lead, n = v.shape[:-1], v.shape[-1]
    return jnp.swapaxes(v.reshape(lead + (n // (2 * bw), 2, bw)), -3, -2).reshape(lead + (n,))


_GROUPS = {
    ("ab", 0): (("ab_w_in", 0),),
    ("ab", 1): (("ab_w_out", 0),),
    ("xa", 0): (("xa_wq", 0), ("xa_wk", 0), ("xa_wv", 0), ("xa_wo", 0)),
    ("f", 0): (("f_w_up", 0),),
    ("fd", 0): (("f_w_down", 0),),
    ("c", 0): (("c_w_pw1", 0), ("c_w_pw2", 0)),
    ("xa", 1): (("xa_wq", 1), ("xa_wk", 1), ("xa_wv", 1), ("xa_wo", 1)),
    ("f", 1): (("f_w_up", 1),),
    ("fd", 1): (("f_w_down", 1),),
}
_SEND_GROUPS = {g: m for g, m in _GROUPS.items() if g[0] != "fd"}
_SEND_GROUPS[("f", 0)] = (("f_w_up", 0), ("f_w_down", 0))
_SEND_GROUPS[("f", 1)] = (("f_w_up", 1), ("f_w_down", 1))


def _weight_layout(name, g):
    if name == "ab_w_in":
        return jnp.swapaxes(g, 0, 1).reshape(D, N_DEV * 320)
    if name in ("c_w_pw1", "f_w_up"):
        return g
    return g.reshape(N_DEV * g.shape[1], D)


def _grad_blocks(name, l, G):
    _, r, c = _BIG[name]
    if name == "ab_w_in":
        return jnp.swapaxes(G[name].reshape(D, N_DEV, 320), 0, 1)
    if name == "c_w_pw1":
        return G[name]
    if name == "f_w_up":
        return G[f"{name}{l}"]
    return (G[name] if _BIG[name][0] == 1 else G[f"{name}{l}"]).reshape(N_DEV, r, c)


def _small_layouts(sm):
    W = {}
    sm = sm.reshape(N_DEV, -1)
    off = 0
    for name, shape in _SMALL_SHARDED.items():
        n = _size(shape)
        blocks = sm[:, off:off + n].reshape((N_DEV,) + shape)
        off += n
        W[name] = jnp.moveaxis(blocks, 0, -2).reshape(shape[:-1] + (N_DEV * shape[-1],))
    W["a_conv_w"], W["c_dw_w"] = W["a_conv_w"][0], W["c_dw_w"][0]
    W["c_b_pw1"] = _pair_blocks(W["c_b_pw1"], _CW_C)
    return W


def _with_own(land, src, me, per_dest):
    own = lax.dynamic_slice_in_dim(src, me, 1, 0) if per_dest else src[None]
    return lax.dynamic_update_slice_in_dim(land, own, me, 0)


def _to_dest_major(g, shape):
    full = g.reshape(shape[:-1] + (N_DEV, shape[-1]))
    return jnp.moveaxis(full, -2, 0).reshape(N_DEV, -1)


def kernel(x, mem, ab_norm, ab_w_in, a_conv_w, a_conv_b, a_gate_x_w, a_gate_x_b, a_gate_a_w, a_gate_a_b, a_lambda, b_group_w, b_group_b, b_scale, ab_w_out, c_norm, c_w_pw1, c_b_pw1, c_dw_w, c_dw_b, c_ln_g, c_ln_b, c_w_pw2, c_b_pw2, xa_norm, xa_mem_norm, xa_wq, xa_wk, xa_wv, xa_wo, f_norm, f_w_up, f_dw_w, f_dw_b, f_w_down, final_norm, loss_target, m_ab_norm, m_ab_w_in, m_a_conv_w, m_a_conv_b, m_a_gate_x_w, m_a_gate_x_b, m_a_gate_a_w, m_a_gate_a_b, m_a_lambda, m_b_group_w, m_b_group_b, m_b_scale, m_ab_w_out, m_c_norm, m_c_w_pw1, m_c_b_pw1, m_c_dw_w, m_c_dw_b, m_c_ln_g, m_c_ln_b, m_c_w_pw2, m_c_b_pw2, m_xa_norm, m_xa_mem_norm, m_xa_wq, m_xa_wk, m_xa_wv, m_xa_wo, m_f_norm, m_f_w_up, m_f_dw_w, m_f_dw_b, m_f_w_down, m_final_norm, v_ab_norm, v_ab_w_in, v_a_conv_w, v_a_conv_b, v_a_gate_x_w, v_a_gate_x_b, v_a_gate_a_w, v_a_gate_a_b, v_a_lambda, v_b_group_w, v_b_group_b, v_b_scale, v_ab_w_out, v_c_norm, v_c_w_pw1, v_c_b_pw1, v_c_dw_w, v_c_dw_b, v_c_ln_g, v_c_ln_b, v_c_w_pw2, v_c_b_pw2, v_xa_norm, v_xa_mem_norm, v_xa_wq, v_xa_wk, v_xa_wv, v_xa_wo, v_f_norm, v_f_w_up, v_f_dw_w, v_f_dw_b, v_f_w_down, v_final_norm):
    args = dict(locals())
    P = {n: args[n] for n in _NAMES}
    M = {n: args["m_" + n] for n in _NAMES}
    V = {n: args["v_" + n] for n in _NAMES}

    me = 4 * lax.axis_index("x") + 2 * lax.axis_index("y") + lax.axis_index("c")

    in_flight = {}

    def launch(groups, tok):
        shards, n_of = [], {}
        for grp in groups:
            for name, l in _GROUPS[grp]:
                w = P[name][l] if tok is None else P[name][l] + tok
                shards.append(w.astype(BF16))
            if grp == ("ab", 0):
                shards.append(_pack([P[n] for n in _SMALL_SHARDED], _SS_ROWS + 4))
            n_of[grp] = len(shards)
        res = _send_start(shards, False, "gather_start_" + "_".join(g[0] + str(g[1]) for g in groups))
        lo = 0
        for grp in groups:
            in_flight[grp] = [r[lo:n_of[grp]] for r in res[:4]]
            lo = n_of[grp]
        return res[4][:1, :1]

    follow = {("ab", 0): [("ab", 1), ("xa", 0), ("f", 0), ("fd", 0)], ("xa", 0): [("c", 0), ("xa", 1)],
              ("f", 0): [("f", 1), ("fd", 1)]}

    def fetch(grp, after):
        send_s, recv_s, srcs, lands = in_flight.pop(grp)
        srcs, lands, tok = _send_wait(send_s, recv_s, srcs, lands, after, False, f"gather_wait_{grp[0]}{grp[1]}")
        tok = launch(follow[grp], tok[:1, :1]) if grp in follow else None
        full = [_with_own(land, src, me, False) for land, src in zip(lands, srcs)]
        out = {}
        for (name, l), g in zip(_GROUPS[grp], full):
            w = _weight_layout(name, g)
            if _BIG[name][0] == 1:
                out[name] = w
            else:
                out[name] = {l: w}
        if grp == ("ab", 0):
            out.update(_small_layouts(full[-1]))
        return out, tok

    zero = launch([("ab", 0)], None)

    pending, held = [], []
    rides_with_next = {("xa", 1), ("f", 0)}

    def send(grp, G):
        held.extend(_SEND_GROUPS[grp])
        if grp in rides_with_next:
            return None
        members = tuple(held)
        del held[:]
        res = _send_start([_grad_blocks(name, l, G) for name, l in members], True, f"send_{grp[0]}{grp[1]}")
        pending.append((members, res))
        return res[4][:1, :1]

    W = {n: P[n] for n in _REPL}
    W["ab_norm"] = P["ab_norm"] + zero
    W["final_norm"] = P["final_norm"].reshape(1, D)
    W["a_gate_x_w"], W["a_gate_a_w"], W["b_group_w"] = P["a_gate_x_w"][0], P["a_gate_a_w"][0], P["b_group_w"][0]
    loss, grad_x, G = _local_step(x[0], mem[0], loss_target[0], W, fetch, send)
    loss = lax.psum(loss[0, 0], ("x", "y", "c"))

    Gs = dict(G)
    Gs["c_b_pw1"] = _unpair_blocks(G["c_b_pw1"], _CW_C)
    Gs["f_dw_w"] = jnp.stack([G["f_dw_w0"], G["f_dw_w1"]])
    Gs["a_conv_w"], Gs["c_dw_w"] = G["a_conv_w"][None], G["c_dw_w"][None]
    for n in ("xa_norm", "xa_mem_norm", "f_norm", "f_dw_b"):
        Gs[n] = jnp.concatenate([G[f"{n}0"], G[f"{n}1"]], axis=0)
    for n in ("a_gate_x_w", "a_gate_a_w", "b_group_w"):
        Gs[n] = G[n][None]
    repl_flat = jnp.concatenate([Gs[n].reshape(-1) for n in _REPL])
    repl_rows = jnp.pad(repl_flat, (0, N_DEV * _REPL_ROWS * LANE - _N_REPL)).reshape(N_DEV, _REPL_ROWS, LANE)
    ss_rows = jnp.concatenate([_to_dest_major(Gs[n], s) for n, s in _SMALL_SHARDED.items()], axis=1)
    ss_rows = ss_rows.reshape(N_DEV, _SS_ROWS, LANE)
    small_pack = jnp.concatenate(
        [repl_rows, ss_rows, jnp.zeros((N_DEV, _SMALL_ROWS - _REPL_ROWS - _SS_ROWS, LANE), F32)], axis=1)
    last = _send_start([small_pack], True, "send_small")
    pending.append(((("small", 0),), last))

    def arrived(some, after, name):
        members = [m for mem_, _ in some for m in mem_]
        cat = [[a for _, res in some for a in res[i]] for i in range(4)]
        srcs, lands, _ = _send_wait(cat[0], cat[1], cat[2], cat[3], after, True, name)
        return {m: _with_own(land, src, me, True) for m, land, src in zip(members, lands, srcs)}

    out_g, out_d, out_m, out_v = {}, {}, {}, {}
    chain = [None]

    def update(name, landed):
        layers, r, c = _BIG[name]
        w2, m2, v2 = [t[name].reshape(layers * r, c) for t in (P, M, V)]
        res = None
        for l in range(layers):
            res = _sum_adamw(landed[(name, l)], w2, m2, v2, f"adamw_{name}{l}", layer=l, prev=res,
                             after=chain[0] if l == 0 else None)
        chain[0] = res[1]
        out_g[name], out_d[name], out_m[name], out_v[name] = [t.reshape(P[name].shape) for t in res]

    landed = arrived(pending[:-2], grad_x, "send_wait_early")
    for name in _BIG:
        if name != "ab_w_in":
            update(name, landed)
    landed = arrived(pending[-2:], out_v["f_w_down"], "send_wait_late")
    update("ab_w_in", landed)

    small_sum = _sum8(landed[("small", 0)], "sum_small")
    (repl_all,) = _all_gather([small_sum[:_REPL_ROWS]], "gather_small_grads")
    for out, got in zip((out_g, out_d, out_m, out_v),
                        _adamw_small(repl_all.reshape(N_DEV * _REPL_ROWS, LANE), small_sum, P, M, V)):
        out.update(got)

    return (loss, grad_x[None], *[out_g[n] for n in _NAMES], *[out_d[n] for n in _NAMES],
            *[out_m[n] for n in _NAMES], *[out_v[n] for n in _NAMES])


_NAMES = ("ab_norm", "ab_w_in", "a_conv_w", "a_conv_b", "a_gate_x_w", "a_gate_x_b", "a_gate_a_w", "a_gate_a_b",
          "a_lambda", "b_group_w", "b_group_b", "b_scale", "ab_w_out", "c_norm", "c_w_pw1", "c_b_pw1", "c_dw_w",
          "c_dw_b", "c_ln_g", "c_ln_b", "c_w_pw2", "c_b_pw2", "xa_norm", "xa_mem_norm", "xa_wq", "xa_wk", "xa_wv",
          "xa_wo", "f_norm", "f_w_up", "f_dw_w", "f_dw_b", "f_w_down", "final_norm")
```

```python
import functools

import jax
import jax.numpy as jnp
from jax import lax
from jax.experimental import pallas as pl
from jax.experimental.pallas import tpu as pltpu

F32, BF16 = jnp.float32, jnp.bfloat16
SDS = jax.ShapeDtypeStruct
MESH = pl.DeviceIdType.MESH

N_DEV = 8
D = 1024
N_MEM = 256
XA_HEADS, XA_HD = 4, 256
HD_A = 128
CONV_A, CONV_C, CONV_F = 4, 31, 3
C_RG = 8.0
POOL_WINDOWS = (2, 4, 8, 16)
D_FF = 3 * D
EPS = 1e-6
ADAM_LR, ADAM_B1, ADAM_B2, ADAM_EPS, ADAM_WD, ADAM_STEP = 0.001, 0.9, 0.999, 1e-08, 0.01, 10

LANE = 128
SUB = 8
VMEM_LIMIT = 56 * 1024 * 1024
R_SEQ = 1024
R_POOL = 2048
R_RGLRU = 2048
R_FFN = 2048
TM_ROW = 1024


def _cp(n_axes):
    return pltpu.CompilerParams(dimension_semantics=("arbitrary",) * n_axes, vmem_limit_bytes=VMEM_LIMIT)


def _tile(n, pref):
    if n <= pref:
        return n
    best = None
    for t in range(LANE, pref + 1, LANE):
        if n % t == 0:
            best = t
    assert best is not None, (n, pref)
    return best


def _perm2(n):
    return (n % 2) * 4 + n // 2


_NN = (((1,), (0,)), ((), ()))
_NT = (((1,), (1,)), ((), ()))
_TN = (((0,), (0,)), ((), ()))


def _mm_call(name, grid, ab, ab_specs, dims, acc_shape, extras, outs, finish, from_ref=False):
    nk = grid[2]
    n_ab, n_ex, n_out = len(ab), len(extras), len(outs)
    use_acc = nk > 1 or from_ref

    def product(refs):
        r = lax.dot_general(refs[0][...], refs[1][...], dims, preferred_element_type=F32)
        for i in range(1, n_ab):
            r = r + lax.dot_general(refs[2 * i][...], refs[2 * i + 1][...], dims, preferred_element_type=F32)
        return r

    def body(*refs):
        rest = refs[2 * n_ab:]
        ex_refs, o_refs = rest[:n_ex], rest[n_ex:n_ex + n_out]
        first_rows = pl.program_id(0) == 0
        if not use_acc:
            finish(product(refs), ex_refs, o_refs, first_rows)
            return
        acc = rest[n_ex + n_out]
        if nk == 1:
            acc[...] = product(refs)
            finish(acc, ex_refs, o_refs, first_rows)
            return
        k = pl.program_id(2)

        @pl.when(k == 0)
        def _():
            acc[...] = jnp.zeros_like(acc)

        acc[...] += product(refs)

        @pl.when(k == nk - 1)
        def _():
            finish(acc if from_ref else acc[...], ex_refs, o_refs, first_rows)

    res = pl.pallas_call(
        body, out_shape=[o for o, _ in outs], grid=grid,
        in_specs=list(ab_specs) + [s for _, s in extras], out_specs=[s for _, s in outs],
        scratch_shapes=[pltpu.VMEM(acc_shape, F32)] if use_acc else [], name=name, compiler_params=_cp(3),
    )(*[t for pair in ab for t in pair], *[e for e, _ in extras])
    return res[0] if n_out == 1 else res


def _finish_sum(r, ex_refs, o_refs, first_rows):
    del first_rows
    for e in ex_refs:
        r = r + e[...]
    o_refs[0][...] = r.astype(o_refs[0].dtype)


def _finish_sum_norm(r, ex_refs, o_refs, first_rows):
    del first_rows
    for e in ex_refs[:-1]:
        r = r + e[...]
    o_refs[0][...] = r
    o_refs[1][...] = ((r * lax.rsqrt(jnp.mean(r * r, axis=-1, keepdims=True) + EPS)) * ex_refs[-1][...]).astype(BF16)


_EPI_ROWS = 16


def _finish_rms_bwd(r_ref, ex_refs, o_refs, first_rows):
    x_ref, g_ref, dres_ref = ex_refs
    dx_ref, dxb_ref, dg_ref = o_refs

    @pl.when(first_rows)
    def _():
        dg_ref[...] = jnp.zeros_like(dg_ref)

    gv = g_ref[...]
    inv_d = 1.0 / r_ref.shape[1]

    def step(i, dg_acc):
        groups = [pl.ds(pl.multiple_of(i * (2 * _EPI_ROWS) + u * _EPI_ROWS, _EPI_ROWS), _EPI_ROWS) for u in range(2)]
        sums = []
        for rows in groups:
            r, xf = r_ref[rows, :], x_ref[rows, :]
            sums.append((jnp.sum(xf * xf, axis=-1, keepdims=True), jnp.sum((r * gv) * xf, axis=-1, keepdims=True)))
        for rows, (sxx, sax) in zip(groups, sums):
            r, xf = r_ref[rows, :], x_ref[rows, :]
            rs = lax.rsqrt(sxx * inv_d + EPS)
            dg_acc = dg_acc + _psum8(r * (xf * rs))
            dx = rs * (r * gv) - xf * (rs * rs * (sax * rs * inv_d)) + dres_ref[rows, :]
            dx_ref[rows, :] = dx
            dxb_ref[rows, :] = dx.astype(BF16)
        return dg_acc

    dg_acc = lax.fori_loop(0, r_ref.shape[0] // (2 * _EPI_ROWS), step, jnp.zeros((SUB, r_ref.shape[1]), F32))
    dg_ref[...] += jnp.sum(dg_acc, axis=0, keepdims=True)


def _rms_bwd_io(M, tm, x, g, dres):
    rows = pl.BlockSpec((tm, D), lambda m, n, k: (m, 0))
    vec = pl.BlockSpec((1, D), lambda m, n, k: (0, 0))
    return ([(x, rows), (g, vec), (dres, rows)],
            [(SDS((M, D), F32), rows), (SDS((M, D), BF16), rows), (SDS((1, D), F32), vec)])


_K_WHOLE = 3072


def _mm_nn(a, b, *, out_dtype, name, bias=None, add=None, norm=None):
    M, K = a.shape
    tk = K if K <= _K_WHOLE else _tile(K, 1024)
    if K <= 1024 and norm is None:
        tm = _tile(M, 2048 if add is None and out_dtype == BF16 else 1024)
    else:
        tm = _tile(M, 512)
    if b.ndim == 3:
        nb, _, bw = b.shape
        N, tn, nn = nb * bw, bw, nb
        b_spec = pl.BlockSpec((None, tk, bw), lambda m, n, k: (_perm2(n), k, 0))
    else:
        N = b.shape[1]
        tn = _tile(N, 1024)
        nn = N // tn
        b_spec = pl.BlockSpec((tk, tn), lambda m, n, k: (k, n))
    tile = pl.BlockSpec((tm, tn), lambda m, n, k: (m, n))
    vec = pl.BlockSpec((1, tn), lambda m, n, k: (0, n))
    extras = ([] if bias is None else [(bias, vec)]) + ([] if add is None else [(add, tile)])
    outs, finish = [(SDS((M, N), out_dtype), tile)], _finish_sum
    if norm is not None:
        assert tn == N == D and out_dtype == F32
        extras.append((norm, vec))
        outs, finish = outs + [(SDS((M, N), BF16), tile)], _finish_sum_norm
    return _mm_call(name, (M // tm, nn, K // tk), [(a, b)], [pl.BlockSpec((tm, tk), lambda m, n, k: (m, k)), b_spec],
                    _NN, (tm, tn), extras, outs, finish)


def _mm_nt(a, b, *, out_dtype, name, add=None, rms=None):
    M, N = a.shape
    if b.ndim == 3:
        nb, Ko, bw = b.shape
        tm = _tile(M, 1024)
        tn, tk, nk = _tile(Ko, 1024), bw, nb
        b_spec = pl.BlockSpec((None, tn, bw), lambda m, n, k: (_perm2(k), n, 0))
    else:
        Ko = b.shape[0]
        tk = N if N <= _K_WHOLE else _tile(N, 1024)
        if N <= 1024 and rms is None:
            tm = _tile(M, 2048 if add is None and out_dtype == BF16 else 1024)
        else:
            tm = _tile(M, 512)
        tn = _tile(Ko, 1024)
        nk = N // tk
        b_spec = pl.BlockSpec((tn, tk), lambda m, n, k: (n, k))
    tile = pl.BlockSpec((tm, tn), lambda m, n, k: (m, n))
    extras = [] if add is None else [(add, tile)]
    outs, finish = [(SDS((M, Ko), out_dtype), tile)], _finish_sum
    if rms is not None:
        assert tn == Ko == D and add is None
        (extras, outs), finish = _rms_bwd_io(M, tm, *rms), _finish_rms_bwd
    return _mm_call(name, (M // tm, Ko // tn, nk), [(a, b)], [pl.BlockSpec((tm, tk), lambda m, n, k: (m, k)), b_spec],
                    _NT, (tm, tn), extras, outs, finish, from_ref=rms is not None)


def _mm_nt_cols(parts, b, *, name, rms):
    M = parts[0].shape[0]
    tm = _tile(M, 512)
    specs, off = [], 0
    for p in parts:
        w = p.shape[1]
        assert off % w == 0
        specs.append(pl.BlockSpec((tm, w), lambda m, n, k: (m, 0)))
        specs.append(pl.BlockSpec((D, w), functools.partial(lambda m, n, k, o: (0, o), o=off // w)))
        off += w
    extras, outs = _rms_bwd_io(M, tm, *rms)
    return _mm_call(name, (M // tm, 1, 1), [(p, b) for p in parts], specs, _NT, (tm, D), extras, outs, _finish_rms_bwd,
                    from_ref=True)


def _mm_tn(a, b, *, out_dtype, name, blocks=None):
    S, Ka = a.shape
    Nb = b.shape[1]
    tm = _tile(Ka, 1024)
    if blocks is not None:
        bw = blocks
        tn, nn = bw, Nb // bw
        out = (SDS((nn, Ka, bw), out_dtype), pl.BlockSpec((None, tm, bw), lambda m, n, k: (_perm2(n), m, 0)))
    else:
        tn = _tile(Nb, 1024)
        nn = Nb // tn
        out = (SDS((Ka, Nb), out_dtype), pl.BlockSpec((tm, tn), lambda m, n, k: (m, n)))
    steps = (Ka // tm) * nn
    tk = _tile(S, 4096 if steps >= 4 else 2048 if steps >= 2 else 1024)
    return _mm_call(name, (Ka // tm, nn, S // tk), [(a, b)],
                    [pl.BlockSpec((tk, tm), lambda m, n, k: (k, m)), pl.BlockSpec((tk, tn), lambda m, n, k: (k, n))],
                    _TN, (tm, tn), [], [out], _finish_sum)


def _row(tm, c):
    return pl.BlockSpec((tm, c), lambda i: (i, 0))


def _full(shape):
    nd = len(shape)
    return pl.BlockSpec(shape, lambda i: (0,) * nd)


def _rms_fwd(x, g, name):
    S = x.shape[0]
    tm = min(S, TM_ROW)

    def body(x_ref, g_ref, o_ref):
        xf = x_ref[...]
        r = lax.rsqrt(jnp.mean(xf * xf, axis=-1, keepdims=True) + EPS)
        o_ref[...] = ((xf * r) * g_ref[...]).astype(BF16)

    return pl.pallas_call(body, out_shape=SDS((S, D), BF16), grid=(S // tm,), in_specs=[_row(tm, D), _full((1, D))],
                          out_specs=_row(tm, D), name=name, compiler_params=_cp(1))(x, g)


def _rms_bwd(x, g, dn, dres, name):
    S = x.shape[0]
    tm = min(S, TM_ROW)
    want_dx = dres is not None

    def body(x_ref, g_ref, dn_ref, *rest):
        i = pl.program_id(0)
        dg_ref = rest[-1]

        @pl.when(i == 0)
        def _():
            dg_ref[...] = jnp.zeros_like(dg_ref)

        xf = x_ref[...]
        r = lax.rsqrt(jnp.mean(xf * xf, axis=-1, keepdims=True) + EPS)
        y = xf * r
        dn_v = dn_ref[...]
        dg_ref[...] += jnp.sum(dn_v * y, axis=0, keepdims=True)
        if want_dx:
            dres_ref, dx_ref, dxb_ref = rest[0], rest[1], rest[2]
            dy = dn_v * g_ref[...]
            dx = r * (dy - y * jnp.mean(dy * y, axis=-1, keepdims=True)) + dres_ref[...]
            dx_ref[...] = dx
            dxb_ref[...] = dx.astype(BF16)

    ins = [x, g, dn] + ([dres] if want_dx else [])
    in_specs = [_row(tm, D), _full((1, D)), _row(tm, D)] + ([_row(tm, D)] if want_dx else [])
    outs = ([SDS((S, D), F32), SDS((S, D), BF16)] if want_dx else []) + [SDS((1, D), F32)]
    out_specs = ([_row(tm, D), _row(tm, D)] if want_dx else []) + [_full((1, D))]
    return pl.pallas_call(body, out_shape=outs, grid=(S // tm,), in_specs=in_specs, out_specs=out_specs, name=name,
                          compiler_params=_cp(1))(*ins)


def _loss_head(x, g, tgt):
    S = x.shape[0]
    tm = min(S, TM_ROW)

    def body(x_ref, g_ref, t_ref, loss_ref, dx_ref, dxb_ref, dg_ref):
        i = pl.program_id(0)

        @pl.when(i == 0)
        def _():
            loss_ref[...] = jnp.zeros_like(loss_ref)
            dg_ref[...] = jnp.zeros_like(dg_ref)

        xf = x_ref[...]
        r = lax.rsqrt(jnp.mean(xf * xf, axis=-1, keepdims=True) + EPS)
        y = xf * r
        gv = g_ref[...]
        err = y * gv - t_ref[...]
        per_row = jnp.mean(err * err, axis=-1, keepdims=True)
        loss_ref[...] += 0.5 * jnp.sum(per_row, axis=0, keepdims=True)
        dn_v = err * (1.0 / D)
        dg_ref[...] += jnp.sum(dn_v * y, axis=0, keepdims=True)
        dy = dn_v * gv
        dx = r * (dy - y * jnp.mean(dy * y, axis=-1, keepdims=True))
        dx_ref[...] = dx
        dxb_ref[...] = dx.astype(BF16)

    return pl.pallas_call(
        body, out_shape=[SDS((1, 1), F32), SDS((S, D), F32), SDS((S, D), BF16), SDS((1, D), F32)], grid=(S // tm,),
        in_specs=[_row(tm, D), _full((1, D)), _row(tm, D)],
        out_specs=[_full((1, 1)), _row(tm, D), _row(tm, D), _full((1, D))], name="loss_head", compiler_params=_cp(1),
    )(x, g, tgt)


def _softmax_rows(s):
    m = jnp.max(s, axis=-1, keepdims=True)
    e = jnp.exp(s - m)
    return e / jnp.sum(e, axis=-1, keepdims=True)


def _attn_fwd(q, k, v, name):
    S = q.shape[0]
    tm = min(S, TM_ROW)
    scale = XA_HD ** -0.5

    def body(q_ref, k_ref, v_ref, o_ref):
        for h in range(XA_HEADS):
            sl = slice(h * XA_HD, (h + 1) * XA_HD)
            s = lax.dot_general(q_ref[:, sl], k_ref[:, sl], _NT, preferred_element_type=F32) * scale
            p = _softmax_rows(s)
            o_ref[:, sl] = lax.dot_general(p.astype(BF16), v_ref[:, sl], _NN, preferred_element_type=F32).astype(BF16)

    return pl.pallas_call(body, out_shape=SDS((S, D), BF16), grid=(S // tm,),
                          in_specs=[_row(tm, D), _full((N_MEM, D)), _full((N_MEM, D))], out_specs=_row(tm, D),
                          name=name, compiler_params=_cp(1))(q, k, v)


def _attn_bwd(q, k, v, do, name):
    S = q.shape[0]
    tm = min(S, TM_ROW)
    scale = XA_HD ** -0.5

    def body(q_ref, k_ref, v_ref, do_ref, dq_ref, dk_ref, dv_ref):
        i = pl.program_id(0)

        @pl.when(i == 0)
        def _():
            dk_ref[...] = jnp.zeros_like(dk_ref)
            dv_ref[...] = jnp.zeros_like(dv_ref)

        for h in range(XA_HEADS):
            sl = slice(h * XA_HD, (h + 1) * XA_HD)
            qh, kh, vh, doh = q_ref[:, sl], k_ref[:, sl], v_ref[:, sl], do_ref[:, sl]
            s = lax.dot_general(qh, kh, _NT, preferred_element_type=F32) * scale
            p = _softmax_rows(s)
            pb = p.astype(BF16)
            dv_ref[:, sl] += lax.dot_general(pb, doh, _TN, preferred_element_type=F32)
            dp = lax.dot_general(doh, vh, _NT, preferred_element_type=F32)
            ds = (p * (dp - jnp.sum(dp * p, axis=-1, keepdims=True)) * scale).astype(BF16)
            dq_ref[:, sl] = lax.dot_general(ds, kh, _NN, preferred_element_type=F32).astype(BF16)
            dk_ref[:, sl] += lax.dot_general(ds, qh, _TN, preferred_element_type=F32)

    return pl.pallas_call(
        body, out_shape=[SDS((S, D), BF16), SDS((N_MEM, D), F32), SDS((N_MEM, D), F32)], grid=(S // tm,),
        in_specs=[_row(tm, D), _full((N_MEM, D)), _full((N_MEM, D)), _row(tm, D)],
        out_specs=[_row(tm, D), _full((N_MEM, D)), _full((N_MEM, D))], name=name, compiler_params=_cp(1),
    )(q, k, v, do)


def _sigmoid(x):
    return 1.0 / (1.0 + jnp.exp(-x))


def _ln_silu_fwd(cv, g, b):
    S = cv.shape[0]
    tm = min(S, TM_ROW)

    def body(x_ref, g_ref, b_ref, o_ref):
        xf = x_ref[...]
        mu = jnp.mean(xf, axis=-1, keepdims=True)
        xc = xf - mu
        rstd = lax.rsqrt(jnp.mean(xc * xc, axis=-1, keepdims=True) + EPS)
        ln = (xc * rstd) * g_ref[...] + b_ref[...]
        o_ref[...] = (ln * _sigmoid(ln)).astype(BF16)

    return pl.pallas_call(body, out_shape=SDS((S, D), BF16), grid=(S // tm,),
                          in_specs=[_row(tm, D), _full((1, D)), _full((1, D))], out_specs=_row(tm, D),
                          name="ln_silu_fwd", compiler_params=_cp(1))(cv, g, b)


def _ln_silu_bwd(ds, cv, g, b, dx):
    S = cv.shape[0]
    tm = min(S, TM_ROW)

    def body(ds_ref, x_ref, g_ref, b_ref, dx_ref, dcv_ref, dg_ref, db_ref, db2_ref):
        i = pl.program_id(0)

        @pl.when(i == 0)
        def _():
            dg_ref[...] = jnp.zeros_like(dg_ref)
            db_ref[...] = jnp.zeros_like(db_ref)
            db2_ref[...] = jnp.zeros_like(db2_ref)

        xf = x_ref[...]
        mu = jnp.mean(xf, axis=-1, keepdims=True)
        xc = xf - mu
        rstd = lax.rsqrt(jnp.mean(xc * xc, axis=-1, keepdims=True) + EPS)
        xhat = xc * rstd
        gv = g_ref[...]
        ln = xhat * gv + b_ref[...]
        sg = _sigmoid(ln)
        dln = ds_ref[...].astype(F32) * (sg + ln * sg * (1.0 - sg))
        dg_ref[...] += jnp.sum(dln * xhat, axis=0, keepdims=True)
        db_ref[...] += jnp.sum(dln, axis=0, keepdims=True)
        db2_ref[...] += jnp.sum(dx_ref[...], axis=0, keepdims=True)
        dxh = dln * gv
        dcv_ref[...] = rstd * (dxh - jnp.mean(dxh, axis=-1, keepdims=True)
                               - xhat * jnp.mean(dxh * xhat, axis=-1, keepdims=True))

    return pl.pallas_call(
        body, out_shape=[SDS((S, D), F32), SDS((1, D), F32), SDS((1, D), F32), SDS((1, D), F32)], grid=(S // tm,),
        in_specs=[_row(tm, D), _row(tm, D), _full((1, D)), _full((1, D)), _row(tm, D)],
        out_specs=[_row(tm, D), _full((1, D)), _full((1, D)), _full((1, D))], name="ln_silu_bwd",
        compiler_params=_cp(1),
    )(ds, cv, g, b, dx)


_GELU_C, _GELU_K = 0.7978845608028654, 0.044715


def _gelu(x, with_grad=False):
    x2 = x * x
    t = jnp.tanh(_GELU_C * (x + _GELU_K * x * x2))
    gel = 0.5 * x * (1.0 + t)
    if not with_grad:
        return gel
    return gel, 0.5 * (1.0 + t) + 0.5 * x * (1.0 - t * t) * (_GELU_C * (1.0 + 3.0 * _GELU_K * x2))


def _expm1(x):
    poly = x * (1.0 + x * (0.5 + x * (1.0 / 6.0 + x * (1.0 / 24.0 + x * (1.0 / 120.0)))))
    return jnp.where(jnp.abs(x) < 0.05, poly, jnp.exp(x) - 1.0)


def _softplus(x):
    return jnp.maximum(x, 0.0) + jnp.log1p(jnp.exp(-jnp.abs(x)))


_SCAN_UNROLL = 8
_RB = 32
_HB = 16


def _sub_blocks(n_rows, n_lanes, fn):
    def step(idx, c):
        r0 = pl.multiple_of(idx * _RB, _RB)
        for lt in range(n_lanes // LANE):
            fn(r0, lt)
        return c

    lax.fori_loop(0, n_rows // _RB, step, 0)


def _lanes(lt):
    return pl.ds(lt * LANE, LANE)


def _psum8(x):
    parts = [x[i * SUB:(i + 1) * SUB] for i in range(x.shape[0] // SUB)]
    return functools.reduce(lambda p, q: p + q, parts)


def _scan_fwd(a_s, b_s, out_ref, carry_ref, n_groups):
    row = lax.broadcasted_iota(jnp.int32, (SUB, LANE), 0)
    U = _SCAN_UNROLL

    def step(gi, carry):
        base = gi * (SUB * U)
        parts = []
        for u in range(U):
            i = pl.multiple_of(base + u * SUB, SUB)
            a8, b8 = a_s[pl.ds(i, SUB), :], b_s[pl.ds(i, SUB), :]
            for s in (1, 2, 4):
                a_sh = jnp.where(row >= s, pltpu.roll(a8, s, 0), 1.0)
                b_sh = jnp.where(row >= s, pltpu.roll(b8, s, 0), 0.0)
                b8 = a8 * b_sh + b8
                a8 = a8 * a_sh
            parts.append((i, a8, b8))
        for i, a8, b8 in parts:
            h8 = a8 * carry + b8
            out_ref[pl.ds(i, SUB), :] = h8
            carry = jnp.broadcast_to(h8[SUB - 1:SUB, :], (SUB, LANE))
        return carry

    carry_ref[...] = lax.fori_loop(0, n_groups // U, step, carry_ref[...])


def _scan_bwd(a_s, b_s, out_ref, carry_ref, n_groups):
    row = lax.broadcasted_iota(jnp.int32, (SUB, LANE), 0)
    U = _SCAN_UNROLL

    def step(gi, carry):
        base = (n_groups // U - 1 - gi) * (SUB * U)
        parts = []
        for u in reversed(range(U)):
            i = pl.multiple_of(base + u * SUB, SUB)
            a8, b8 = a_s[pl.ds(i, SUB), :], b_s[pl.ds(i, SUB), :]
            for s in (1, 2, 4):
                a_sh = jnp.where(row < SUB - s, pltpu.roll(a8, SUB - s, 0), 1.0)
                b_sh = jnp.where(row < SUB - s, pltpu.roll(b8, SUB - s, 0), 0.0)
                b8 = a8 * b_sh + b8
                a8 = a8 * a_sh
            parts.append((i, a8, b8))
        for i, a8, b8 in parts:
            h8 = a8 * carry + b8
            out_ref[pl.ds(i, SUB), :] = h8
            carry = jnp.broadcast_to(h8[0:1, :], (SUB, LANE))
        return carry

    carry_ref[...] = lax.fori_loop(0, n_groups // U, step, carry_ref[...])


def _rglru_pre(xr, wgx_ref, bgx_ref, wga_ref, bga_ref, lam_ref):
    xrb = xr.astype(BF16)
    wgx, wga = wgx_ref[0].astype(BF16), wga_ref[0].astype(BF16)
    gx = _sigmoid(lax.dot_general(xrb, wgx, _NN, preferred_element_type=F32) + bgx_ref[...])
    ga = _sigmoid(lax.dot_general(xrb, wga, _NN, preferred_element_type=F32) + bga_ref[...])
    sp = _softplus(-lam_ref[...])
    log_a = -C_RG * ga * sp
    a = jnp.exp(log_a)
    mult = jnp.sqrt(-_expm1(2.0 * log_a))
    return gx, ga, sp, a, mult, xrb, wgx, wga


def _a_specs():
    vec = pl.BlockSpec((1, HD_A), lambda c, j: (0, c))
    mat = pl.BlockSpec((1, HD_A, HD_A), lambda c, j: (c, 0, 0))
    return [pl.BlockSpec((CONV_A, HD_A), lambda c, j: (0, c)), vec, mat, vec, mat, vec, vec]


def _a_fwd(zp, conv_w, conv_b, wgx, bgx, wga, bga, lam):
    S = zp.shape[0]
    R, nt = R_RGLRU, D // HD_A
    H = SUB

    def body(zg_ref, zr_ref, cw_ref, cb_ref, wgx_ref, bgx_ref, wga_ref, bga_ref, lam_ref, ya_ref, h_ref,
             ext, a_s, b_s, hc):
        j = pl.program_id(1)

        @pl.when(j == 0)
        def _():
            ext[0:H, :] = jnp.zeros((H, HD_A), F32)
            hc[...] = jnp.zeros_like(hc)

        ext[H:H + R, :] = zr_ref[...].astype(F32)
        xr = cb_ref[...]
        for k in range(CONV_A):
            xr = xr + cw_ref[k:k + 1, :] * ext[pl.ds(H - (CONV_A - 1 - k), R), :]
        gx, _, _, a, mult, _, _, _ = _rglru_pre(xr, wgx_ref, bgx_ref, wga_ref, bga_ref, lam_ref)
        a_s[...] = a
        b_s[...] = mult * (gx * xr)
        _scan_fwd(a_s, b_s, h_ref, hc, R // SUB)
        ya_ref[...] = (_gelu(zg_ref[...].astype(F32)) * h_ref[...]).astype(BF16)
        ext[0:H, :] = ext[R:R + H, :]

    return pl.pallas_call(
        body, out_shape=[SDS((S, D + D // 2), BF16), SDS((S, D), F32)], grid=(nt, S // R),
        in_specs=[pl.BlockSpec((R, HD_A), lambda c, j: (j, c)), pl.BlockSpec((R, HD_A), lambda c, j: (j, nt + c))]
        + _a_specs(),
        out_specs=[pl.BlockSpec((R, HD_A), lambda c, j: (j, c)), pl.BlockSpec((R, HD_A), lambda c, j: (j, c))],
        scratch_shapes=[pltpu.VMEM((H + R, HD_A), F32), pltpu.VMEM((R, HD_A), F32), pltpu.VMEM((R, HD_A), F32),
                        pltpu.VMEM((SUB, HD_A), F32)],
        name="rglru_fwd", compiler_params=_cp(2),
    )(zp, zp, conv_w, conv_b, wgx, bgx, wga, bga, lam)


def _a_bwd(dyab, zp, h, conv_w, conv_b, wgx, bgx, wga, bga, lam):
    S = zp.shape[0]
    R, nt, nch = R_RGLRU, D // HD_A, S // R_RGLRU
    H = SUB

    def rows(c, j):
        return (nch - 1 - j, c)

    def rows_rec(c, j):
        return (nch - 1 - j, nt + c)

    def halo(c, j):
        return (jnp.maximum((nch - 1 - j) * (R // H) - 1, 0), c)

    def halo_z(c, j):
        return (jnp.maximum((nch - 1 - j) * (R // _HB) - 1, 0), nt + c)

    def body(dy_ref, zg_ref, zr_ref, zh_ref, h_ref, hh_ref, cw_ref, cb_ref, wgx_ref, bgx_ref, wga_ref, bga_ref,
             lam_ref, dzg_ref, dzr_ref, dcw_ref, dcb_ref, dwgx_ref, dbgx_ref, dwga_ref, dbga_ref, dlam_ref,
             ext_z, ext_h, ext_mu, ext_d, a_s, b_s, muc):
        j = pl.program_id(1)
        first_chunk = (nch - 1 - j) == 0

        @pl.when(j == 0)
        def _():
            ext_mu[R:R + H, :] = jnp.zeros((H, HD_A), F32)
            ext_d[R:R + H, :] = jnp.zeros((H, HD_A), F32)
            muc[...] = jnp.zeros_like(muc)
            for r in (dcw_ref, dcb_ref, dwgx_ref, dbgx_ref, dwga_ref, dbga_ref, dlam_ref):
                r[...] = jnp.zeros_like(r)

        zg = zg_ref[...].astype(F32)
        ext_z[0:H, :] = jnp.where(first_chunk, 0.0, zh_ref[_HB - H:_HB, :].astype(F32))
        ext_z[H:H + R, :] = zr_ref[...].astype(F32)
        ext_h[0:H, :] = jnp.where(first_chunk, 0.0, hh_ref[...])
        ext_h[H:H + R, :] = h_ref[...]
        xr = cb_ref[...]
        for k in range(CONV_A):
            xr = xr + cw_ref[k:k + 1, :] * ext_z[pl.ds(H - (CONV_A - 1 - k), R), :]
        gx, ga, sp, a, mult, xrb, wgxb, wgab = _rglru_pre(xr, wgx_ref, bgx_ref, wga_ref, bga_ref, lam_ref)
        gel, dgel = _gelu(zg, with_grad=True)
        dy = dy_ref[...].astype(F32)
        dh = dy * gel
        dzg_ref[...] = (dy * h_ref[...] * dgel).astype(BF16)
        a_s[...] = a
        b_s[...] = a * dh
        _scan_bwd(a_s, b_s, ext_mu, muc, R // SUB)
        lam_t = dh + ext_mu[pl.ds(1, R), :]
        ext_mu[R:R + H, :] = ext_mu[0:H, :]
        da = lam_t * ext_h[pl.ds(H - 1, R), :]
        gxr = gx * xr
        dlog_a = da * a - (lam_t * gxr) * (a * a) / mult
        dgx = lam_t * mult * xr
        dxr = lam_t * mult * gx
        lam_v = lam_ref[...]
        dlam_ref[...] += jnp.sum(dlog_a * ga, axis=0, keepdims=True) * (C_RG * _sigmoid(-lam_v))
        dpa = (dlog_a * (-C_RG * sp)) * ga * (1.0 - ga)
        dpx = dgx * gx * (1.0 - gx)
        dbga_ref[...] += jnp.sum(dpa, axis=0, keepdims=True)
        dbgx_ref[...] += jnp.sum(dpx, axis=0, keepdims=True)
        dpab, dpxb = dpa.astype(BF16), dpx.astype(BF16)
        dwga_ref[0] += lax.dot_general(xrb, dpab, _TN, preferred_element_type=F32)
        dwgx_ref[0] += lax.dot_general(xrb, dpxb, _TN, preferred_element_type=F32)
        dxr = (dxr + lax.dot_general(dpab, wgab, _NT, preferred_element_type=F32)
               + lax.dot_general(dpxb, wgxb, _NT, preferred_element_type=F32))
        dcb_ref[...] += jnp.sum(dxr, axis=0, keepdims=True)
        ext_d[0:R, :] = dxr
        dzr = jnp.zeros((R, HD_A), F32)
        for k in range(CONV_A):
            sh = CONV_A - 1 - k
            dcw_ref[k:k + 1, :] += jnp.sum(dxr * ext_z[pl.ds(H - sh, R), :], axis=0, keepdims=True)
            dzr = dzr + cw_ref[k:k + 1, :] * ext_d[pl.ds(sh, R), :]
        dzr_ref[...] = dzr.astype(BF16)
        ext_d[R:R + H, :] = ext_d[0:H, :]

    vec_o = pl.BlockSpec((1, HD_A), lambda c, j: (0, c))
    mat_o = pl.BlockSpec((1, HD_A, HD_A), lambda c, j: (c, 0, 0))
    return pl.pallas_call(
        body,
        out_shape=[SDS((S, D), BF16), SDS((S, D), BF16), SDS((CONV_A, D), F32), SDS((1, D), F32),
                   SDS((nt, HD_A, HD_A), F32), SDS((1, D), F32), SDS((nt, HD_A, HD_A), F32), SDS((1, D), F32),
                   SDS((1, D), F32)],
        grid=(nt, nch),
        in_specs=[pl.BlockSpec((R, HD_A), rows), pl.BlockSpec((R, HD_A), rows), pl.BlockSpec((R, HD_A), rows_rec),
                  pl.BlockSpec((_HB, HD_A), halo_z), pl.BlockSpec((R, HD_A), rows),
                  pl.BlockSpec((H, HD_A), halo)] + _a_specs(),
        out_specs=[pl.BlockSpec((R, HD_A), rows), pl.BlockSpec((R, HD_A), rows),
                   pl.BlockSpec((CONV_A, HD_A), lambda c, j: (0, c)), vec_o, mat_o, vec_o, mat_o, vec_o, vec_o],
        scratch_shapes=[pltpu.VMEM((H + R, HD_A), F32), pltpu.VMEM((H + R, HD_A), F32), pltpu.VMEM((R + H, HD_A), F32),
                        pltpu.VMEM((R + H, HD_A), F32), pltpu.VMEM((R, HD_A), F32), pltpu.VMEM((R, HD_A), F32),
                        pltpu.VMEM((SUB, HD_A), F32)],
        name="rglru_bwd", compiler_params=_cp(2),
    )(dyab, zp, zp, zp, h, h, conv_w, conv_b, wgx, bgx, wga, bga, lam)


_POOL_H = 16
_POOL_T0 = 2 * D // HD_A
_POOL_Y0 = D // HD_A


def _window_sum(lv, n, lo, rows, g, ahead):
    base = 0 if ahead else SUB
    cur, win = lv[0], None
    for i, s in enumerate((1, 2, 4, 8)):
        val = cur[pl.ds(base, n), :] + cur[pl.ds(base + (s if ahead else -s), n), :]
        sel = val[lo:lo + rows]
        win = sel if win is None else jnp.where(g >= i, sel, win)
        if i < 3:
            lv[i + 1][pl.ds(base, n), :] = val
            cur = lv[i + 1]
    return win


def _pool_width(g):
    return jnp.where(g == 0, 2.0, jnp.where(g == 1, 4.0, jnp.where(g == 2, 8.0, 16.0)))


def _b_fwd(zp, yab, wg, bg, sc):
    S = zp.shape[0]
    R, H = min(S, R_POOL), _POOL_H

    def body(z_ref, wg_ref, bg_ref, sc_ref, yab_in, yb_ref, *lv):
        del yab_in
        g, j = pl.program_id(0), pl.program_id(1)

        @pl.when(j == 0)
        def _():
            for r in lv:
                r[0:SUB, :] = jnp.zeros((SUB, HD_A), F32)
            lv[0][SUB:SUB + H, :] = jnp.zeros((H, HD_A), F32)

        u = z_ref[...].astype(F32)
        lv[0][SUB + H:SUB + H + R, :] = u
        t1 = (j * R + 1 + lax.broadcasted_iota(jnp.int32, (R, HD_A), 0)).astype(F32)
        p = _window_sum(lv, H + R, H, R, g, False) / jnp.minimum(t1, _pool_width(g)) - u
        lin = lax.dot_general(p.astype(BF16), wg_ref[0].astype(BF16), _NN, preferred_element_type=F32) + bg_ref[...]
        yb_ref[...] = (lin * sc_ref[...]).astype(BF16)
        lv[0][SUB:SUB + H, :] = lv[0][SUB + R:SUB + R + H, :]

    vec = pl.BlockSpec((1, HD_A), lambda g, j: (0, g))
    return pl.pallas_call(
        body, out_shape=SDS(yab.shape, yab.dtype), grid=(len(POOL_WINDOWS), S // R),
        in_specs=[pl.BlockSpec((R, HD_A), lambda g, j: (j, _POOL_T0 + g)),
                  pl.BlockSpec((1, HD_A, HD_A), lambda g, j: (g, 0, 0)), vec, vec, pl.BlockSpec(memory_space=pl.ANY)],
        out_specs=pl.BlockSpec((R, HD_A), lambda g, j: (j, _POOL_Y0 + g)),
        scratch_shapes=[pltpu.VMEM((SUB + H + R, HD_A), F32)] * 4, input_output_aliases={4: 0},
        name="pool_fwd", compiler_params=_cp(2),
    )(zp, wg, bg, sc, yab)


def _b_bwd(dyab, zp, wg, bg, sc):
    S = zp.shape[0]
    R, H, ng = min(S, R_POOL), _POOL_H, len(POOL_WINDOWS)
    nch = S // R

    def body(dy_ref, z_ref, zh_ref, wg_ref, bg_ref, sc_ref, dz_ref, dwg_ref, dbg_ref, dsc_ref, *scratch):
        lu, lq = scratch[:4], scratch[4:]
        g, j = pl.program_id(0), pl.program_id(1)
        jj = nch - 1 - j

        @pl.when(j == 0)
        def _():
            for r in lu:
                r[0:SUB, :] = jnp.zeros((SUB, HD_A), F32)
            for r in lq:
                r[R + H:R + H + SUB, :] = jnp.zeros((SUB, HD_A), F32)
            lq[0][R:R + H, :] = jnp.zeros((H, HD_A), F32)
            for r in (dwg_ref, dbg_ref, dsc_ref):
                r[...] = jnp.zeros_like(r)

        u = z_ref[...].astype(F32)
        lu[0][SUB:SUB + H, :] = jnp.where(jj == 0, 0.0, zh_ref[...].astype(F32))
        lu[0][SUB + H:SUB + H + R, :] = u
        t1 = (jj * R + 1 + lax.broadcasted_iota(jnp.int32, (R, HD_A), 0)).astype(F32)
        cnt = jnp.minimum(t1, _pool_width(g))
        pb = (_window_sum(lu, H + R, H, R, g, False) / cnt - u).astype(BF16)
        wgb = wg_ref[0].astype(BF16)
        lin = lax.dot_general(pb, wgb, _NN, preferred_element_type=F32) + bg_ref[...]
        dy = dy_ref[...].astype(F32)
        dsc_ref[...] += jnp.sum(dy * lin, axis=0, keepdims=True)
        dlin = dy * sc_ref[...]
        dbg_ref[...] += jnp.sum(dlin, axis=0, keepdims=True)
        dlb = dlin.astype(BF16)
        dwg_ref[0] += lax.dot_general(pb, dlb, _TN, preferred_element_type=F32)
        dp = lax.dot_general(dlb, wgb, _NT, preferred_element_type=F32)
        lq[0][0:R, :] = dp / cnt
        dz_ref[...] = (_window_sum(lq, R + H, 0, R, g, True) - dp).astype(BF16)
        lq[0][R:R + H, :] = lq[0][0:H, :]

    vec = pl.BlockSpec((1, HD_A), lambda g, j: (0, g))
    mat = pl.BlockSpec((1, HD_A, HD_A), lambda g, j: (g, 0, 0))
    return pl.pallas_call(
        body, out_shape=[SDS((S, D // 2), BF16), SDS((ng, HD_A, HD_A), F32), SDS((1, D // 2), F32),
                         SDS((1, D // 2), F32)],
        grid=(ng, nch),
        in_specs=[pl.BlockSpec((R, HD_A), lambda g, j: (nch - 1 - j, _POOL_Y0 + g)),
                  pl.BlockSpec((R, HD_A), lambda g, j: (nch - 1 - j, _POOL_T0 + g)),
                  pl.BlockSpec((H, HD_A), lambda g, j: (jnp.maximum((nch - 1 - j) * (R // H) - 1, 0), _POOL_T0 + g)),
                  mat, vec, vec],
        out_specs=[pl.BlockSpec((R, HD_A), lambda g, j: (nch - 1 - j, g)), mat, vec, vec],
        scratch_shapes=[pltpu.VMEM((SUB + H + R, HD_A), F32)] * 8,
        name="pool_bwd", compiler_params=_cp(2),
    )(dyab, zp, zp, wg, bg, sc)


_CW_F = 768


def _f_fwd(hp, w, b, name):
    S = hp.shape[0]
    R, H, cw = min(S, R_FFN), SUB, _CW_F
    nlt = cw // LANE

    def body(h_ref, w_ref, b_ref, o_ref, gel_ref, ud_ref, ext):
        j = pl.program_id(1)

        @pl.when(j == 0)
        def _():
            ext[:, 0:H, :] = jnp.zeros((nlt, H, LANE), F32)

        def stage(r0, lt):
            ext[lt, pl.ds(pl.multiple_of(r0 + H, SUB), _RB), :] = h_ref[pl.ds(r0, _RB), _lanes(lt)].astype(F32)

        def main(r0, lt):
            ls = _lanes(lt)
            gp = b_ref[:, ls]
            for k in range(CONV_F):
                gp = gp + w_ref[k:k + 1, ls] * ext[lt, pl.ds(r0 + (H - (CONV_F - 1 - k)), _RB), :]
            up = h_ref[pl.ds(r0, _RB), _lanes(lt + nlt)].astype(F32)
            gel, dgel = _gelu(gp, with_grad=True)
            rs = pl.ds(r0, _RB)
            o_ref[rs, ls] = (gel * up).astype(BF16)
            gel_ref[rs, ls] = gel.astype(BF16)
            ud_ref[rs, ls] = (up * dgel).astype(BF16)

        _sub_blocks(R, cw, stage)
        _sub_blocks(R, cw, main)
        ext[:, 0:H, :] = ext[:, R:R + H, :]

    tile = pl.BlockSpec((R, cw), lambda c, j: (j, c))
    return pl.pallas_call(
        body, out_shape=[SDS((S, D_FF), BF16)] * 3, grid=(D_FF // cw, S // R),
        in_specs=[pl.BlockSpec((R, 2 * cw), lambda c, j: (j, c)), pl.BlockSpec((CONV_F, cw), lambda c, j: (0, c)),
                  pl.BlockSpec((1, cw), lambda c, j: (0, c))],
        out_specs=[tile] * 3,
        scratch_shapes=[pltpu.VMEM((nlt, H + R, LANE), F32)], name=name, compiler_params=_cp(2),
    )(hp, w, b)


def _f_bwd(dact, hp, gel, ud, w, name):
    S = hp.shape[0]
    R, H, cw = min(S, R_FFN), SUB, _CW_F
    nch = S // R
    nlt = cw // LANE

    def body(da_ref, h_ref, hh_ref, gel_ref, ud_ref, w_ref, dh_ref, dw_ref, db_ref, ext_g, ext_d, acc):
        j = pl.program_id(1)
        jj = nch - 1 - j

        @pl.when(j == 0)
        def _():
            ext_d[:, R:R + H, :] = jnp.zeros((nlt, H, LANE), F32)
            acc[...] = jnp.zeros_like(acc)

        for lt in range(nlt):
            ext_g[lt, 0:H, :] = jnp.where(jj == 0, 0.0, hh_ref[_HB - H:_HB, lt * LANE:(lt + 1) * LANE].astype(F32))

        def stage(r0, lt):
            ext_g[lt, pl.ds(pl.multiple_of(r0 + H, SUB), _RB), :] = h_ref[pl.ds(r0, _RB), _lanes(lt)].astype(F32)

        def first(r0, lt):
            ls, lu, rs = _lanes(lt), _lanes(lt + nlt), pl.ds(r0, _RB)
            da = da_ref[rs, ls].astype(F32)
            dh_ref[rs, lu] = (da * gel_ref[rs, ls].astype(F32)).astype(BF16)
            dgp = da * ud_ref[rs, ls].astype(F32)
            ext_d[lt, rs, :] = dgp
            acc[CONV_F * SUB:(CONV_F + 1) * SUB, ls] += _psum8(dgp)
            for k in range(CONV_F):
                tap = ext_g[lt, pl.ds(r0 + (H - (CONV_F - 1 - k)), _RB), :]
                acc[k * SUB:(k + 1) * SUB, ls] += _psum8(dgp * tap)

        def second(r0, lt):
            ls = _lanes(lt)
            dhg = w_ref[CONV_F - 1:CONV_F, ls] * ext_d[lt, pl.ds(r0, _RB), :]
            for k in range(CONV_F - 1):
                dhg = dhg + w_ref[k:k + 1, ls] * ext_d[lt, pl.ds(r0 + (CONV_F - 1 - k), _RB), :]
            dh_ref[pl.ds(r0, _RB), ls] = dhg.astype(BF16)

        _sub_blocks(R, cw, stage)
        _sub_blocks(R, cw, first)
        _sub_blocks(R, cw, second)
        ext_d[:, R:R + H, :] = ext_d[:, 0:H, :]

        @pl.when(j == nch - 1)
        def _():
            for k in range(CONV_F):
                dw_ref[k:k + 1, :] = jnp.sum(acc[k * SUB:(k + 1) * SUB, :], axis=0, keepdims=True)
            db_ref[...] = jnp.sum(acc[CONV_F * SUB:(CONV_F + 1) * SUB, :], axis=0, keepdims=True)

    rows = lambda c, j: (nch - 1 - j, c)
    return pl.pallas_call(
        body, out_shape=[SDS((S, 2 * D_FF), BF16), SDS((CONV_F, D_FF), F32), SDS((1, D_FF), F32)],
        grid=(D_FF // cw, nch),
        in_specs=[pl.BlockSpec((R, cw), rows), pl.BlockSpec((R, cw), lambda c, j: (nch - 1 - j, 2 * c)),
                  pl.BlockSpec((_HB, cw), lambda c, j: (jnp.maximum((nch - 1 - j) * (R // _HB) - 1, 0), 2 * c)),
                  pl.BlockSpec((R, cw), rows), pl.BlockSpec((R, cw), rows),
                  pl.BlockSpec((CONV_F, cw), lambda c, j: (0, c))],
        out_specs=[pl.BlockSpec((R, 2 * cw), rows), pl.BlockSpec((CONV_F, cw), lambda c, j: (0, c)),
                   pl.BlockSpec((1, cw), lambda c, j: (0, c))],
        scratch_shapes=[pltpu.VMEM((nlt, H + R, LANE), F32), pltpu.VMEM((nlt, R + H, LANE), F32),
                        pltpu.VMEM(((CONV_F + 1) * SUB, cw), F32)], name=name,
        compiler_params=_cp(2),
    )(dact, hp, hp, gel, ud, w)


_CW_C = 256
_H_C = 32


def _c_fwd(h1p, w, b):
    S = h1p.shape[0]
    R, H, cw = R_SEQ, _H_C, _CW_C
    nlt = cw // LANE

    def body(h_ref, w_ref, b_ref, o_ref, ext):
        j = pl.program_id(1)

        @pl.when(j == 0)
        def _():
            ext[:, 0:H, :] = jnp.zeros((nlt, H, LANE), F32)

        def stage(r0, lt):
            rs = pl.ds(r0, _RB)
            gate = h_ref[rs, _lanes(lt + nlt)].astype(F32)
            ext[lt, pl.ds(pl.multiple_of(r0 + H, SUB), _RB), :] = h_ref[rs, _lanes(lt)].astype(F32) * _sigmoid(gate)

        def main(r0, lt):
            ls = _lanes(lt)
            cv = b_ref[:, ls]
            for k in range(CONV_C):
                cv = cv + w_ref[k:k + 1, ls] * ext[lt, pl.ds(r0 + (H - (CONV_C - 1 - k)), _RB), :]
            o_ref[pl.ds(r0, _RB), ls] = cv

        _sub_blocks(R, cw, stage)
        _sub_blocks(R, cw, main)
        ext[:, 0:H, :] = ext[:, R:R + H, :]

    return pl.pallas_call(
        body, out_shape=SDS((S, D), F32), grid=(D // cw, S // R),
        in_specs=[pl.BlockSpec((R, 2 * cw), lambda c, j: (j, c)), pl.BlockSpec((CONV_C, cw), lambda c, j: (0, c)),
                  pl.BlockSpec((1, cw), lambda c, j: (0, c))],
        out_specs=pl.BlockSpec((R, cw), lambda c, j: (j, c)),
        scratch_shapes=[pltpu.VMEM((nlt, H + R, LANE), F32)], name="conf_conv_fwd", compiler_params=_cp(2),
    )(h1p, w, b)


def _c_bwd(dcv, h1p, w):
    S = h1p.shape[0]
    R, H, cw, nch = R_SEQ, _H_C, _CW_C, S // R_SEQ
    nlt = cw // LANE
    a_b, a_val, a_gate = CONV_C * SUB, (CONV_C + 1) * SUB, (CONV_C + 2) * SUB

    def body(dc_ref, h_ref, hh_ref, w_ref, dh_ref, dw_ref, db_ref, db1_ref, ext_u, ext_d, acc):
        j = pl.program_id(1)
        jj = nch - 1 - j

        @pl.when(j == 0)
        def _():
            ext_d[:, R:R + H, :] = jnp.zeros((nlt, H, LANE), F32)
            acc[...] = jnp.zeros_like(acc)

        for lt in range(nlt):
            ext_u[lt, 0:H, :] = jnp.where(
                jj == 0, 0.0, hh_ref[:, lt * LANE:(lt + 1) * LANE].astype(F32)
                * _sigmoid(hh_ref[:, cw + lt * LANE:cw + (lt + 1) * LANE].astype(F32)))

        def stage(r0, lt):
            rs, ls = pl.ds(r0, _RB), _lanes(lt)
            gate = h_ref[rs, _lanes(lt + nlt)].astype(F32)
            ext_u[lt, pl.ds(pl.multiple_of(r0 + H, SUB), _RB), :] = h_ref[rs, ls].astype(F32) * _sigmoid(gate)
            ext_d[lt, rs, :] = dc_ref[rs, ls]

        def first(r0, lt):
            ls = _lanes(lt)
            dc = dc_ref[pl.ds(r0, _RB), ls]
            acc[a_b:a_b + SUB, ls] += _psum8(dc)
            for k in range(CONV_C):
                tap = ext_u[lt, pl.ds(r0 + (H - (CONV_C - 1 - k)), _RB), :]
                acc[k * SUB:(k + 1) * SUB, ls] += _psum8(dc * tap)

        def second(r0, lt):
            rs, ls, lg = pl.ds(r0, _RB), _lanes(lt), _lanes(lt + nlt)
            du = w_ref[CONV_C - 1:CONV_C, ls] * ext_d[lt, rs, :]
            for k in range(CONV_C - 1):
                du = du + w_ref[k:k + 1, ls] * ext_d[lt, pl.ds(r0 + (CONV_C - 1 - k), _RB), :]
            val = h_ref[rs, ls].astype(F32)
            sg = _sigmoid(h_ref[rs, lg].astype(F32))
            dval = du * sg
            dgate = du * val * sg * (1.0 - sg)
            acc[a_val:a_val + SUB, ls] += _psum8(dval)
            acc[a_gate:a_gate + SUB, ls] += _psum8(dgate)
            dh_ref[rs, ls] = dval.astype(BF16)
            dh_ref[rs, lg] = dgate.astype(BF16)

        _sub_blocks(R, cw, stage)
        _sub_blocks(R, cw, first)
        _sub_blocks(R, cw, second)
        ext_d[:, R:R + H, :] = ext_d[:, 0:H, :]

        @pl.when(j == nch - 1)
        def _():
            for k in range(CONV_C):
                dw_ref[k:k + 1, :] = jnp.sum(acc[k * SUB:(k + 1) * SUB, :], axis=0, keepdims=True)
            db_ref[...] = jnp.sum(acc[a_b:a_b + SUB, :], axis=0, keepdims=True)
            db1_ref[:, 0:cw] = jnp.sum(acc[a_val:a_val + SUB, :], axis=0, keepdims=True)
            db1_ref[:, cw:2 * cw] = jnp.sum(acc[a_gate:a_gate + SUB, :], axis=0, keepdims=True)

    rows = lambda c, j: (nch - 1 - j, c)
    return pl.pallas_call(
        body, out_shape=[SDS((S, 2 * D), BF16), SDS((CONV_C, D), F32), SDS((1, D), F32), SDS((1, 2 * D), F32)],
        grid=(D // cw, nch),
        in_specs=[pl.BlockSpec((R, cw), rows), pl.BlockSpec((R, 2 * cw), rows),
                  pl.BlockSpec((H, 2 * cw), lambda c, j: (jnp.maximum((nch - 1 - j) * (R // H) - 1, 0), c)),
                  pl.BlockSpec((CONV_C, cw), lambda c, j: (0, c))],
        out_specs=[pl.BlockSpec((R, 2 * cw), rows), pl.BlockSpec((CONV_C, cw), lambda c, j: (0, c)),
                   pl.BlockSpec((1, cw), lambda c, j: (0, c)), pl.BlockSpec((1, 2 * cw), lambda c, j: (0, c))],
        scratch_shapes=[pltpu.VMEM((nlt, H + R, LANE), F32), pltpu.VMEM((nlt, R + H, LANE), F32),
                        pltpu.VMEM(((CONV_C + 3) * SUB, cw), F32)], name="conf_conv_bwd",
        compiler_params=_cp(2),
    )(dcv, h1p, h1p, w)


def _local_step(x, mem, tgt, W, fetch=None, send=None):
    G = {}
    W = dict(W)

    def arrive(group, after):
        if fetch is None:
            return None
        got, tok = fetch(group, after)
        for key, val in got.items():
            W[key] = {**W.get(key, {}), **val} if isinstance(val, dict) else val
        return tok

    def gain(g, tok):
        return g if tok is None else g + tok

    def sent(group):
        return None if send is None else send(group, G)

    def xattn_fwd(xin, n, l):
        tok = arrive(("xa", l), n)
        mn = _rms_fwd(mem, gain(W["xa_mem_norm"][l:l + 1], tok), f"xa_memnorm_fwd{l}")
        q = _mm_nn(n, W["xa_wq"][l], out_dtype=BF16, name=f"xa_q{l}")
        k = _mm_nn(mn, W["xa_wk"][l], out_dtype=BF16, name=f"xa_k{l}")
        v = _mm_nn(mn, W["xa_wv"][l], out_dtype=BF16, name=f"xa_v{l}")
        o = _attn_fwd(q, k, v, f"xa_attn_fwd{l}")
        xout, nout = _mm_nn(o, W["xa_wo"][l], out_dtype=F32, name=f"xa_o{l}", add=xin, norm=W["f_norm"][l:l + 1])
        return xout, nout, (xin, n, q, mn, k, v, o)

    def xattn_bwd(dx, dxb, saved, l):
        xin, n, q, mn, k, v, o = saved
        do = _mm_nt(dxb, W["xa_wo"][l], out_dtype=BF16, name=f"xa_do{l}")
        G[f"xa_wo{l}"] = _mm_tn(o, dxb, out_dtype=BF16, name=f"xa_dwo{l}")
        dq, dk, dv = _attn_bwd(q, k, v, do, f"xa_attn_bwd{l}")
        dkb, dvb = dk.astype(BF16), dv.astype(BF16)
        G[f"xa_wq{l}"] = _mm_tn(n, dq, out_dtype=BF16, name=f"xa_dwq{l}")
        G[f"xa_wk{l}"] = _mm_tn(mn, dkb, out_dtype=BF16, name=f"xa_dwk{l}")
        G[f"xa_wv{l}"] = _mm_tn(mn, dvb, out_dtype=BF16, name=f"xa_dwv{l}")
        tok = sent(("xa", l))
        dmn = _mm_nt(dkb, W["xa_wk"][l], out_dtype=F32, name=f"xa_dmn_k{l}")
        dmn = _mm_nt(dvb, W["xa_wv"][l], out_dtype=F32, name=f"xa_dmn_v{l}", add=dmn)
        (G[f"xa_mem_norm{l}"],) = _rms_bwd(mem, W["xa_mem_norm"][l:l + 1], dmn, None, f"xa_memnorm_bwd{l}")
        dx, dxb, G[f"xa_norm{l}"] = _mm_nt(dq, W["xa_wq"][l], out_dtype=F32, name=f"xa_dn{l}",
                                           rms=(xin, gain(W["xa_norm"][l:l + 1], tok), dx))
        return dx, dxb

    def ffn_fwd(xin, n, l, next_gain):
        tok = arrive(("f", l), n)
        hp = _mm_nn(n, W["f_w_up"][l], out_dtype=BF16, name=f"f_up{l}")
        act, gel, ud = _f_fwd(hp, W["f_dw_w"][l], gain(W["f_dw_b"][l:l + 1], tok), f"f_conv_fwd{l}")
        arrive(("fd", l), act)
        res = _mm_nn(act, W["f_w_down"][l], out_dtype=F32, name=f"f_down{l}", add=xin, norm=next_gain)
        xout, nout = res if next_gain is not None else (res, None)
        return xout, nout, (xin, n, hp, act, gel, ud)

    def ffn_bwd(dx, dxb, saved, l):
        xin, n, hp, act, gel, ud = saved
        dact = _mm_nt(dxb, W["f_w_down"][l], out_dtype=BF16, name=f"f_dact{l}")
        G[f"f_w_down{l}"] = _mm_tn(act, dxb, out_dtype=BF16, name=f"f_dwdown{l}")
        dhp, G[f"f_dw_w{l}"], G[f"f_dw_b{l}"] = _f_bwd(dact, hp, gel, ud, W["f_dw_w"][l], f"f_conv_bwd{l}")
        G[f"f_w_up{l}"] = _mm_tn(n, dhp, out_dtype=BF16, name=f"f_dwup{l}", blocks=_CW_F)
        tok = sent(("f", l))
        dx, dxb, G[f"f_norm{l}"] = _mm_nt(dhp, W["f_w_up"][l], out_dtype=F32, name=f"f_dn{l}",
                                          rms=(xin, gain(W["f_norm"][l:l + 1], tok), dx))
        return dx, dxb

    n0 = _rms_fwd(x, W["ab_norm"], "ab_norm_fwd")
    tok = arrive(("ab", 0), n0)
    a_par = (W["a_conv_w"], gain(W["a_conv_b"], tok), W["a_gate_x_w"], W["a_gate_x_b"], W["a_gate_a_w"],
             W["a_gate_a_b"], W["a_lambda"])
    b_par = (W["b_group_w"], W["b_group_b"], W["b_scale"])
    zp = _mm_nn(n0, W["ab_w_in"], out_dtype=BF16, name="ab_in")
    yab, h_a = _a_fwd(zp, *a_par)
    yab = _b_fwd(zp, yab, *b_par)
    arrive(("ab", 1), yab)
    x1, n1 = _mm_nn(yab, W["ab_w_out"], out_dtype=F32, name="ab_out", add=x, norm=W["xa_norm"][0:1])
    x2, n2, s_xa0 = xattn_fwd(x1, n1, 0)
    x3, n3, s_f0 = ffn_fwd(x2, n2, 0, W["c_norm"])
    tok = arrive(("c", 0), n3)
    h1p = _mm_nn(n3, W["c_w_pw1"], out_dtype=BF16, name="c_pw1", bias=gain(W["c_b_pw1"], tok))
    cv = _c_fwd(h1p, W["c_dw_w"], W["c_dw_b"])
    sc = _ln_silu_fwd(cv, W["c_ln_g"], W["c_ln_b"])
    x4, n4 = _mm_nn(sc, W["c_w_pw2"], out_dtype=F32, name="c_pw2", bias=W["c_b_pw2"], add=x3, norm=W["xa_norm"][1:2])
    x5, n5, s_xa1 = xattn_fwd(x4, n4, 1)
    x6, _, s_f1 = ffn_fwd(x5, n5, 1, None)
    loss, dx, dxb, G["final_norm"] = _loss_head(x6, W["final_norm"], tgt)

    dx, dxb = ffn_bwd(dx, dxb, s_f1, 1)
    dx, dxb = xattn_bwd(dx, dxb, s_xa1, 1)
    dsc = _mm_nt(dxb, W["c_w_pw2"], out_dtype=BF16, name="c_dsc")
    G["c_w_pw2"] = _mm_tn(sc, dxb, out_dtype=BF16, name="c_dwpw2")
    dcv, G["c_ln_g"], G["c_ln_b"], G["c_b_pw2"] = _ln_silu_bwd(dsc, cv, W["c_ln_g"], W["c_ln_b"], dx)
    dh1p, G["c_dw_w"], G["c_dw_b"], G["c_b_pw1"] = _c_bwd(dcv, h1p, W["c_dw_w"])
    G["c_w_pw1"] = _mm_tn(n3, dh1p, out_dtype=BF16, name="c_dwpw1", blocks=_CW_C)
    tok = sent(("c", 0))
    dx, dxb, G["c_norm"] = _mm_nt(dh1p, W["c_w_pw1"], out_dtype=F32, name="c_dn",
                                  rms=(x3, gain(W["c_norm"], tok), dx))
    dx, dxb = ffn_bwd(dx, dxb, s_f0, 0)
    dx, dxb = xattn_bwd(dx, dxb, s_xa0, 0)
    dyab = _mm_nt(dxb, W["ab_w_out"], out_dtype=BF16, name="ab_dyab")
    G["ab_w_out"] = _mm_tn(yab, dxb, out_dtype=BF16, name="ab_dwout")
    tok = sent(("ab", 1))
    a_par = (a_par[0], gain(a_par[1], tok)) + a_par[2:]
    (dzg, dzr, G["a_conv_w"], G["a_conv_b"], G["a_gate_x_w"], G["a_gate_x_b"], G["a_gate_a_w"], G["a_gate_a_b"],
     G["a_lambda"]) = _a_bwd(dyab, zp, h_a, *a_par)
    dzq, G["b_group_w"], G["b_group_b"], G["b_scale"] = _b_bwd(dyab, zp, *b_par)
    G["ab_w_in"] = jnp.concatenate(
        [_mm_tn(n0, dz, out_dtype=BF16, name=f"ab_dwin_{part}")
         for part, dz in (("gate", dzg), ("rec", dzr), ("pool", dzq))], axis=1)
    tok = sent(("ab", 0))
    dx, _, G["ab_norm"] = _mm_nt_cols([dzg, dzr, dzq], W["ab_w_in"], name="ab_dn",
                                      rms=(x, gain(W["ab_norm"], tok), dx))
    return loss, dx, G


def _my_place():
    x, y, c = lax.axis_index("x"), lax.axis_index("y"), lax.axis_index("c")
    return x, y, c


def _all_gather(shards, name):
    n = len(shards)

    def body(*refs):
        ins, outs = refs[:n], refs[n:2 * n]
        send_sems, recv_sems, local_sems = refs[2 * n:]
        x, y, c = _my_place()
        me, sibling = (x, y, c), (x, y, 1 - c)
        chips = [(1 - x, y), (x, 1 - y), (1 - x, 1 - y)]

        def slab(a, place):
            px, py, pc = place
            return outs[a].at[4 * px + 2 * py + pc]

        def copy(a, k, block, to, src=None):
            return pltpu.make_async_remote_copy(
                src_ref=slab(a, block) if src is None else src, dst_ref=slab(a, block),
                send_sem=send_sems.at[a, k], recv_sem=recv_sems.at[a, k], device_id=to, device_id_type=MESH)

        mine = [pltpu.make_async_copy(ins[a], slab(a, me), local_sems.at[a]) for a in range(n)]
        for cp in mine:
            cp.start()
        first = []
        for j, chip in enumerate(chips):
            first += [copy(a, 1 + j, me, (*chip, c), src=ins[a]) for a in range(n)]
        first += [copy(a, 0, me, sibling, src=ins[a]) for a in range(n)]
        for cp in first:
            cp.start()
        passed = []
        for j, chip in enumerate(chips):
            for a in range(n):
                copy(a, 1 + j, (*chip, c), me).wait_recv()
                cp = copy(a, 4 + j, (*chip, c), sibling)
                cp.start()
                passed.append(cp)
        for a in range(n):
            copy(a, 0, sibling, me).wait_recv()
        for j, chip in enumerate(chips):
            for a in range(n):
                copy(a, 4 + j, (*chip, 1 - c), me).wait_recv()
        for cp in first + passed:
            cp.wait_send()
        for cp in mine:
            cp.wait()

    any_spec = pl.BlockSpec(memory_space=pl.ANY)
    return pl.pallas_call(
        body, out_shape=[SDS((N_DEV,) + s.shape, s.dtype) for s in shards], in_specs=[any_spec] * n,
        out_specs=[any_spec] * n,
        scratch_shapes=[pltpu.SemaphoreType.DMA((n, 7)), pltpu.SemaphoreType.DMA((n, 7)), pltpu.SemaphoreType.DMA((n,))],
        name=name,
    )(*shards)


_HBM = pl.BlockSpec(memory_space=pltpu.HBM)
_SEM = pl.BlockSpec(memory_space=pltpu.SEMAPHORE)
_EFFECT = pltpu.SideEffectType.DATAFLOW_SIDE_EFFECTING


def _peer_places():
    x, y, c = _my_place()
    peers = []
    for k in range(1, N_DEV):
        px = 1 - x if (k >> 2) & 1 else x
        py = 1 - y if (k >> 1) & 1 else y
        pc = 1 - c if k & 1 else c
        peers.append(((px, py, pc), 4 * px + 2 * py + pc))
    return (x, y, c), 4 * x + 2 * y + c, peers


def _send_start(srcs, per_dest, name):
    n = len(srcs)
    lands = [lax.empty((N_DEV,) + (s.shape[1:] if per_dest else s.shape), s.dtype) for s in srcs]

    def body(*refs):
        src, land = refs[:n], refs[n:2 * n]
        outs = refs[2 * n:]
        send, recv, token = outs[:n], outs[n:2 * n], outs[4 * n]
        place, me, peers = _peer_places()
        for a in range(n):
            for peer, pidx in peers + [(place, me)]:
                pltpu.make_async_remote_copy(
                    src_ref=src[a].at[pidx] if per_dest else src[a], dst_ref=land[a].at[me], send_sem=send[a],
                    recv_sem=recv[a], device_id=peer, device_id_type=MESH).start()
        token[...] = jnp.zeros_like(token)

    hbm = lambda a: pltpu.HBM(a.shape, a.dtype)
    sem = pltpu.SemaphoreType.DMA(())
    res = pl.pallas_call(
        body, name=name,
        out_shape=tuple([sem] * (2 * n) + [hbm(s) for s in srcs] + [hbm(l) for l in lands]
                        + [SDS((SUB, LANE), F32)]),
        in_specs=[_HBM] * (2 * n),
        out_specs=tuple([_SEM] * (2 * n) + [_HBM] * (2 * n) + [pl.BlockSpec(memory_space=pltpu.VMEM)]),
        input_output_aliases={i: 2 * n + i for i in range(2 * n)},
        compiler_params=pltpu.CompilerParams(has_side_effects=_EFFECT),
    )(*[pltpu.with_memory_space_constraint(s, pltpu.HBM) for s in srcs],
      *[pltpu.with_memory_space_constraint(l, pltpu.HBM) for l in lands])
    return res[:n], res[n:2 * n], res[2 * n:3 * n], res[3 * n:4 * n], res[4 * n]


def _send_wait(send, recv, srcs, lands, after, per_dest, name):
    n = len(srcs)

    def body(*refs):
        src, land = refs[:n], refs[n:2 * n]
        send_s, recv_s = refs[2 * n:3 * n], refs[3 * n:4 * n]
        token = refs[-1]
        place, _, _ = _peer_places()
        for a in range(n):
            copy = pltpu.make_async_remote_copy(
                src_ref=src[a] if per_dest else land[a], dst_ref=land[a], send_sem=send_s[a],
                recv_sem=recv_s[a], device_id=place, device_id_type=MESH)
            copy.wait_send()
            copy.wait_recv()
        token[...] = jnp.zeros_like(token)

    hbm = lambda a: pltpu.HBM(a.shape, a.dtype)
    res = pl.pallas_call(
        body, name=name,
        out_shape=tuple([hbm(s) for s in srcs] + [hbm(l) for l in lands] + [SDS((SUB, LANE), F32)]),
        in_specs=[_HBM] * (2 * n) + [_SEM] * (2 * n) + [pl.BlockSpec(memory_space=pl.ANY)],
        out_specs=tuple([_HBM] * (2 * n) + [pl.BlockSpec(memory_space=pltpu.VMEM)]),
        input_output_aliases={i: i for i in range(2 * n)},
        compiler_params=pltpu.CompilerParams(has_side_effects=_EFFECT),
    )(*srcs, *lands, *send, *recv, after)
    return res[:n], res[n:2 * n], res[2 * n]


def _adamw_math(w, g, m, v):
    m = ADAM_B1 * m + (1.0 - ADAM_B1) * g
    v = ADAM_B2 * v + (1.0 - ADAM_B2) * (g * g)
    m_hat = m / (1.0 - ADAM_B1 ** ADAM_STEP)
    v_hat = v / (1.0 - ADAM_B2 ** ADAM_STEP)
    delta = -ADAM_LR * (m_hat / (jnp.sqrt(v_hat) + ADAM_EPS) + ADAM_WD * w)
    return delta, m, v


def _row_tile(r, c, itemsize_rows):
    cap = max(SUB, (itemsize_rows // (4 * c)) // SUB * SUB)
    if r <= cap:
        return r
    best = None
    for t in range(SUB, cap + 1, SUB):
        if r % t == 0:
            best = t
    return best if best is not None else r


def _sum_adamw(landing, w, m, v, name, layer=0, prev=None, after=None):
    _, r, c = landing.shape
    tr = _row_tile(r, c, 2 << 20)
    off = layer * (r // tr)
    tail = ([] if prev is None else list(prev)) + ([] if after is None else [after])

    def body(l_ref, w_ref, m_ref, v_ref, *rest):
        g_ref, d_ref, mo_ref, vo_ref = rest[-4:]
        g = l_ref[0].astype(F32)
        for s in range(1, N_DEV):
            g = g + l_ref[s].astype(F32)
        g_ref[...] = g
        d_ref[...], mo_ref[...], vo_ref[...] = _adamw_math(w_ref[...], g, m_ref[...], v_ref[...])

    blk = pl.BlockSpec((tr, c), lambda i: (i + off, 0))
    n_prev = 0 if prev is None else 4
    return pl.pallas_call(
        body, out_shape=[SDS(w.shape, F32)] * 4, grid=(r // tr,),
        in_specs=[pl.BlockSpec((N_DEV, tr, c), lambda i: (0, i, 0)), blk, blk, blk]
        + [pl.BlockSpec(memory_space=pl.ANY)] * len(tail),
        out_specs=[blk] * 4, input_output_aliases={4 + i: i for i in range(n_prev)}, name=name,
        compiler_params=_cp(1),
    )(landing, w, m, v, *tail)


def _sum8(landing, name):
    _, r, c = landing.shape

    def body(l_ref, g_ref):
        g = l_ref[0]
        for s in range(1, N_DEV):
            g = g + l_ref[s]
        g_ref[...] = g

    return pl.pallas_call(body, out_shape=SDS((r, c), F32), name=name, compiler_params=_cp(0))(landing)


def _adamw_small(repl_pack, own_pack, P, M, V):
    table, off = [], 0
    for name, shape in _REPL.items():
        table.append((name, shape if len(shape) > 1 else (1,) + shape, 0, off // LANE))
        off += _size(shape)
    off = _REPL_ROWS * LANE
    for name, shape in _SMALL_SHARDED.items():
        table.append((name, shape, 1, off // LANE))
        off += _size(shape)
    n = len(table)

    def body(*refs):
        packs, ins, outs = refs[:2], refs[2:2 + 3 * n], refs[2 + 3 * n:]
        for p, (_, shape, which, r0) in enumerate(table):
            w_ref, m_ref, v_ref = ins[3 * p:3 * p + 3]
            g_ref, d_ref, mo_ref, vo_ref = outs[4 * p:4 * p + 4]
            pack, rows, q = packs[which], shape[-2], shape[-1] // LANE
            lead = [()]
            for dim in shape[:-2]:
                lead = [t + (i,) for t in lead for i in range(dim)]
            for li, idx in enumerate(lead):
                if q == 1:
                    dst = g_ref.at[idx] if idx else g_ref
                    dst[...] = pack[r0 + li * rows:r0 + (li + 1) * rows, :]
                    continue
                for i in range(rows):
                    for k in range(q):
                        row = r0 + (li * rows + i) * q + k
                        g_ref[idx + (slice(i, i + 1), slice(k * LANE, (k + 1) * LANE))] = pack[row:row + 1, :]
            d_ref[...], mo_ref[...], vo_ref[...] = _adamw_math(w_ref[...], g_ref[...], m_ref[...], v_ref[...])

    ins, out_shape = [], []
    for name, shape, _, _ in table:
        ins += [t[name].reshape(shape) for t in (P, M, V)]
        out_shape += [SDS(shape, F32)] * 4
    res = pl.pallas_call(body, out_shape=out_shape, name="adamw_small", compiler_params=_cp(0))(
        repl_pack, own_pack, *ins)
    dicts = ({}, {}, {}, {})
    for p, (name, shape, _, _) in enumerate(table):
        for d, arr in zip(dicts, res[4 * p:4 * p + 4]):
            d[name] = arr.reshape(P[name].shape)
    return dicts


_BIG = {
    "ab_w_in": (1, D, 320), "ab_w_out": (1, 192, D), "c_w_pw1": (1, D, 256), "c_w_pw2": (1, 128, D),
    "xa_wq": (2, 128, D), "xa_wk": (2, 128, D), "xa_wv": (2, 128, D), "xa_wo": (2, 128, D),
    "f_w_up": (2, D, 768), "f_w_down": (2, 384, D),
}
_SMALL_SHARDED = {
    "a_conv_w": (1, 4, 128), "c_norm": (1, 128), "c_b_pw1": (1, 256), "c_dw_w": (1, 31, 128), "c_dw_b": (1, 128),
    "c_ln_g": (1, 128), "c_ln_b": (1, 128), "c_b_pw2": (1, 128), "f_dw_w": (2, 3, 384),
}
_REPL = {
    "ab_norm": (1, D), "a_conv_b": (1, D), "a_gate_x_w": (1, 8, 128, 128), "a_gate_x_b": (1, D),
    "a_gate_a_w": (1, 8, 128, 128), "a_gate_a_b": (1, D), "a_lambda": (1, D), "b_group_w": (1, 4, 128, 128),
    "b_group_b": (1, 512), "b_scale": (1, 512), "xa_norm": (2, D), "xa_mem_norm": (2, D), "f_norm": (2, D),
    "f_dw_b": (2, D_FF), "final_norm": (D,),
}


def _size(shape):
    n = 1
    for s in shape:
        n *= s
    return n


_N_SS = sum(_size(s) for s in _SMALL_SHARDED.values())
_N_REPL = sum(_size(s) for s in _REPL.values())
_REPL_ROWS = -(-_N_REPL // (N_DEV * SUB * LANE)) * SUB
_SS_ROWS = _N_SS // LANE
_SMALL_ROWS = -(-(_REPL_ROWS + _SS_ROWS) // SUB) * SUB


def _pack(parts, rows):
    flat = jnp.concatenate([p.reshape(-1).astype(F32) for p in parts])
    return jnp.pad(flat, (0, rows * LANE - flat.shape[0])).reshape(rows, LANE)


def _pair_blocks(v, bw):
    lead, n = v.shape[:-1], v.shape[-1]
    return jnp.swapaxes(v.reshape(lead + (2, n // (2 * bw), bw)), -3, -2).reshape(lead + (n,))


def _unpair_blocks(v, bw):
    lead, n = v.shape[:-1], v.shape[-1]
    return jnp.swapaxes(v.reshape(lead + (n // (2 * bw), 2, bw)), -3, -2).reshape(lead + (n,))


_GROUPS = {
    ("ab", 0): (("ab_w_in", 0),),
    ("ab", 1): (("ab_w_out", 0),),
    ("xa", 0): (("xa_wq", 0), ("xa_wk", 0), ("xa_wv", 0), ("xa_wo", 0)),
    ("f", 0): (("f_w_up", 0),),
    ("fd", 0): (("f_w_down", 0),),
    ("c", 0): (("c_w_pw1", 0), ("c_w_pw2", 0)),
    ("xa", 1): (("xa_wq", 1), ("xa_wk", 1), ("xa_wv", 1), ("xa_wo", 1)),
    ("f", 1): (("f_w_up", 1),),
    ("fd", 1): (("f_w_down", 1),),
}
_SEND_GROUPS = {g: m for g, m in _GROUPS.items() if g[0] != "fd"}
_SEND_GROUPS[("f", 0)] = (("f_w_up", 0), ("f_w_down", 0))
_SEND_GROUPS[("f", 1)] = (("f_w_up", 1), ("f_w_down", 1))


def _weight_layout(name, g):
    if name == "ab_w_in":
        return jnp.swapaxes(g, 0, 1).reshape(D, N_DEV * 320)
    if name in ("c_w_pw1", "f_w_up"):
        return g
    return g.reshape(N_DEV * g.shape[1], D)


def _grad_blocks(name, l, G):
    _, r, c = _BIG[name]
    if name == "ab_w_in":
        return jnp.swapaxes(G[name].reshape(D, N_DEV, 320), 0, 1)
    if name == "c_w_pw1":
        return G[name]
    if name == "f_w_up":
        return G[f"{name}{l}"]
    return (G[name] if _BIG[name][0] == 1 else G[f"{name}{l}"]).reshape(N_DEV, r, c)


def _small_layouts(sm):
    W = {}
    sm = sm.reshape(N_DEV, -1)
    off = 0
    for name, shape in _SMALL_SHARDED.items():
        n = _size(shape)
        blocks = sm[:, off:off + n].reshape((N_DEV,) + shape)
        off += n
        W[name] = jnp.moveaxis(blocks, 0, -2).reshape(shape[:-1] + (N_DEV * shape[-1],))
    W["a_conv_w"], W["c_dw_w"] = W["a_conv_w"][0], W["c_dw_w"][0]
    W["c_b_pw1"] = _pair_blocks(W["c_b_pw1"], _CW_C)
    return W


def _to_dest_major(g, shape):
    full = g.reshape(shape[:-1] + (N_DEV, shape[-1]))
    return jnp.moveaxis(full, -2, 0).reshape(N_DEV, -1)


def kernel(x, mem, ab_norm, ab_w_in, a_conv_w, a_conv_b, a_gate_x_w, a_gate_x_b, a_gate_a_w, a_gate_a_b, a_lambda, b_group_w, b_group_b, b_scale, ab_w_out, c_norm, c_w_pw1, c_b_pw1, c_dw_w, c_dw_b, c_ln_g, c_ln_b, c_w_pw2, c_b_pw2, xa_norm, xa_mem_norm, xa_wq, xa_wk, xa_wv, xa_wo, f_norm, f_w_up, f_dw_w, f_dw_b, f_w_down, final_norm, loss_target, m_ab_norm, m_ab_w_in, m_a_conv_w, m_a_conv_b, m_a_gate_x_w, m_a_gate_x_b, m_a_gate_a_w, m_a_gate_a_b, m_a_lambda, m_b_group_w, m_b_group_b, m_b_scale, m_ab_w_out, m_c_norm, m_c_w_pw1, m_c_b_pw1, m_c_dw_w, m_c_dw_b, m_c_ln_g, m_c_ln_b, m_c_w_pw2, m_c_b_pw2, m_xa_norm, m_xa_mem_norm, m_xa_wq, m_xa_wk, m_xa_wv, m_xa_wo, m_f_norm, m_f_w_up, m_f_dw_w, m_f_dw_b, m_f_w_down, m_final_norm, v_ab_norm, v_ab_w_in, v_a_conv_w, v_a_conv_b, v_a_gate_x_w, v_a_gate_x_b, v_a_gate_a_w, v_a_gate_a_b, v_a_lambda, v_b_group_w, v_b_group_b, v_b_scale, v_ab_w_out, v_c_norm, v_c_w_pw1, v_c_b_pw1, v_c_dw_w, v_c_dw_b, v_c_ln_g, v_c_ln_b, v_c_w_pw2, v_c_b_pw2, v_xa_norm, v_xa_mem_norm, v_xa_wq, v_xa_wk, v_xa_wv, v_xa_wo, v_f_norm, v_f_w_up, v_f_dw_w, v_f_dw_b, v_f_w_down, v_final_norm):
    args = dict(locals())
    P = {n: args[n] for n in _NAMES}
    M = {n: args["m_" + n] for n in _NAMES}
    V = {n: args["v_" + n] for n in _NAMES}

    in_flight = {}

    def launch(groups, tok):
        shards, n_of = [], {}
        for grp in groups:
            for name, l in _GROUPS[grp]:
                w = P[name][l] if tok is None else P[name][l] + tok
                shards.append(w.astype(BF16))
            if grp == ("ab", 0):
                shards.append(_pack([P[n] for n in _SMALL_SHARDED], _SS_ROWS + 4))
            n_of[grp] = len(shards)
        res = _send_start(shards, False, "gather_start_" + "_".join(g[0] + str(g[1]) for g in groups))
        lo = 0
        for grp in groups:
            in_flight[grp] = [r[lo:n_of[grp]] for r in res[:4]]
            lo = n_of[grp]
        return res[4][:1, :1]

    follow = {("ab", 0): [("ab", 1), ("xa", 0), ("f", 0), ("fd", 0)], ("xa", 0): [("c", 0), ("xa", 1)],
              ("f", 0): [("f", 1), ("fd", 1)]}

    def fetch(grp, after):
        send_s, recv_s, srcs, lands = in_flight.pop(grp)
        srcs, lands, tok = _send_wait(send_s, recv_s, srcs, lands, after, False, f"gather_wait_{grp[0]}{grp[1]}")
        tok = launch(follow[grp], tok[:1, :1]) if grp in follow else None
        full = lands
        out = {}
        for (name, l), g in zip(_GROUPS[grp], full):
            w = _weight_layout(name, g)
            if _BIG[name][0] == 1:
                out[name] = w
            else:
                out[name] = {l: w}
        if grp == ("ab", 0):
            out.update(_small_layouts(full[-1]))
        return out, tok

    zero = launch([("ab", 0)], None)

    pending, held = [], []
    rides_with_next = {("xa", 1), ("f", 0)}

    def send(grp, G):
        held.extend(_SEND_GROUPS[grp])
        if grp in rides_with_next:
            return None
        members = tuple(held)
        del held[:]
        res = _send_start([_grad_blocks(name, l, G) for name, l in members], True, f"send_{grp[0]}{grp[1]}")
        pending.append((members, res))
        return res[4][:1, :1]

    W = {n: P[n] for n in _REPL}
    W["ab_norm"] = P["ab_norm"] + zero
    W["final_norm"] = P["final_norm"].reshape(1, D)
    W["a_gate_x_w"], W["a_gate_a_w"], W["b_group_w"] = P["a_gate_x_w"][0], P["a_gate_a_w"][0], P["b_group_w"][0]
    loss, grad_x, G = _local_step(x[0], mem[0], loss_target[0], W, fetch, send)
    loss = lax.psum(loss[0, 0], ("x", "y", "c"))

    Gs = dict(G)
    Gs["c_b_pw1"] = _unpair_blocks(G["c_b_pw1"], _CW_C)
    Gs["f_dw_w"] = jnp.stack([G["f_dw_w0"], G["f_dw_w1"]])
    Gs["a_conv_w"], Gs["c_dw_w"] = G["a_conv_w"][None], G["c_dw_w"][None]
    for n in ("xa_norm", "xa_mem_norm", "f_norm", "f_dw_b"):
        Gs[n] = jnp.concatenate([G[f"{n}0"], G[f"{n}1"]], axis=0)
    for n in ("a_gate_x_w", "a_gate_a_w", "b_group_w"):
        Gs[n] = G[n][None]
    repl_flat = jnp.concatenate([Gs[n].reshape(-1) for n in _REPL])
    repl_rows = jnp.pad(repl_flat, (0, N_DEV * _REPL_ROWS * LANE - _N_REPL)).reshape(N_DEV, _REPL_ROWS, LANE)
    ss_rows = jnp.concatenate([_to_dest_major(Gs[n], s) for n, s in _SMALL_SHARDED.items()], axis=1)
    ss_rows = ss_rows.reshape(N_DEV, _SS_ROWS, LANE)
    small_pack = jnp.concatenate(
        [repl_rows, ss_rows, jnp.zeros((N_DEV, _SMALL_ROWS - _REPL_ROWS - _SS_ROWS, LANE), F32)], axis=1)
    last = _send_start([small_pack], True, "send_small")
    pending.append(((("small", 0),), last))

    def arrived(some, after, name):
        members = [m for mem_, _ in some for m in mem_]
        cat = [[a for _, res in some for a in res[i]] for i in range(4)]
        srcs, lands, _ = _send_wait(cat[0], cat[1], cat[2], cat[3], after, True, name)
        return dict(zip(members, lands))

    out_g, out_d, out_m, out_v = {}, {}, {}, {}
    chain = [None]

    def update(name, landed):
        layers, r, c = _BIG[name]
        w2, m2, v2 = [t[name].reshape(layers * r, c) for t in (P, M, V)]
        res = None
        for l in range(layers):
            res = _sum_adamw(landed[(name, l)], w2, m2, v2, f"adamw_{name}{l}", layer=l, prev=res,
                             after=chain[0] if l == 0 else None)
        chain[0] = res[1]
        out_g[name], out_d[name], out_m[name], out_v[name] = [t.reshape(P[name].shape) for t in res]

    landed = arrived(pending[:-2], grad_x, "send_wait_early")
    for name in _BIG:
        if name != "ab_w_in":
            update(name, landed)
    landed = arrived(pending[-2:], out_v["f_w_down"], "send_wait_late")
    update("ab_w_in", landed)

    small_sum = _sum8(landed[("small", 0)], "sum_small")
    (repl_all,) = _all_gather([small_sum[:_REPL_ROWS]], "gather_small_grads")
    for out, got in zip((out_g, out_d, out_m, out_v),
                        _adamw_small(repl_all.reshape(N_DEV * _REPL_ROWS, LANE), small_sum, P, M, V)):
        out.update(got)

    return (loss, grad_x[None], *[out_g[n] for n in _NAMES], *[out_d[n] for n in _NAMES],
            *[out_m[n] for n in _NAMES], *[out_v[n] for n in _NAMES])


_NAMES = ("ab_norm", "ab_w_in", "a_conv_w", "a_conv_b", "a_gate_x_w", "a_gate_x_b", "a_gate_a_w", "a_gate_a_b",
          "a_lambda", "b_group_w", "b_group_b", "b_scale", "ab_w_out", "c_norm", "c_w_pw1", "c_b_pw1", "c_dw_w",
          "c_dw_b", "c_ln_g", "c_ln_b", "c_w_pw2", "c_b_pw2", "xa_norm", "xa_mem_norm", "xa_wq", "xa_wk", "xa_wv",
          "xa_wo", "f_norm", "f_w_up", "f_dw_w", "f_dw_b", "f_w_down", "final_norm")
```

```python
import functools

import jax
import jax.numpy as jnp
from jax import lax
from jax.experimental import pallas as pl
from jax.experimental.pallas import tpu as pltpu

F32, BF16 = jnp.float32, jnp.bfloat16
SDS = jax.ShapeDtypeStruct
MESH = pl.DeviceIdType.MESH

N_DEV = 8
D = 1024
N_MEM = 256
XA_HEADS, XA_HD = 4, 256
HD_A = 128
CONV_A, CONV_C, CONV_F = 4, 31, 3
C_RG = 8.0
POOL_WINDOWS = (2, 4, 8, 16)
D_FF = 3 * D
EPS = 1e-6
ADAM_LR, ADAM_B1, ADAM_B2, ADAM_EPS, ADAM_WD, ADAM_STEP = 0.001, 0.9, 0.999, 1e-08, 0.01, 10

LANE = 128
SUB = 8
VMEM_LIMIT = 56 * 1024 * 1024
R_SEQ = 1024
R_POOL = 2048
R_RGLRU = 2048
R_FFN = 2048
TM_ROW = 1024


def _cp(n_axes):
    return pltpu.CompilerParams(dimension_semantics=("arbitrary",) * n_axes, vmem_limit_bytes=VMEM_LIMIT)


def _tile(n, pref):
    if n <= pref:
        return n
    best = None
    for t in range(LANE, pref + 1, LANE):
        if n % t == 0:
            best = t
    assert best is not None, (n, pref)
    return best


def _perm2(n):
    return (n % 2) * 4 + n // 2


_NN = (((1,), (0,)), ((), ()))
_NT = (((1,), (1,)), ((), ()))
_TN = (((0,), (0,)), ((), ()))


def _mm_call(name, grid, ab, ab_specs, dims, acc_shape, extras, outs, finish, from_ref=False):
    nk = grid[2]
    n_ab, n_ex, n_out = len(ab), len(extras), len(outs)
    use_acc = nk > 1 or from_ref

    def product(refs):
        r = lax.dot_general(refs[0][...], refs[1][...], dims, preferred_element_type=F32)
        for i in range(1, n_ab):
            r = r + lax.dot_general(refs[2 * i][...], refs[2 * i + 1][...], dims, preferred_element_type=F32)
        return r

    def body(*refs):
        rest = refs[2 * n_ab:]
        ex_refs, o_refs = rest[:n_ex], rest[n_ex:n_ex + n_out]
        first_rows = pl.program_id(0) == 0
        if not use_acc:
            finish(product(refs), ex_refs, o_refs, first_rows)
            return
        acc = rest[n_ex + n_out]
        if nk == 1:
            acc[...] = product(refs)
            finish(acc, ex_refs, o_refs, first_rows)
            return
        k = pl.program_id(2)

        @pl.when(k == 0)
        def _():
            acc[...] = jnp.zeros_like(acc)

        acc[...] += product(refs)

        @pl.when(k == nk - 1)
        def _():
            finish(acc if from_ref else acc[...], ex_refs, o_refs, first_rows)

    res = pl.pallas_call(
        body, out_shape=[o for o, _ in outs], grid=grid,
        in_specs=list(ab_specs) + [s for _, s in extras], out_specs=[s for _, s in outs],
        scratch_shapes=[pltpu.VMEM(acc_shape, F32)] if use_acc else [], name=name, compiler_params=_cp(3),
    )(*[t for pair in ab for t in pair], *[e for e, _ in extras])
    return res[0] if n_out == 1 else res


def _finish_sum(r, ex_refs, o_refs, first_rows):
    del first_rows
    for e in ex_refs:
        r = r + e[...]
    o_refs[0][...] = r.astype(o_refs[0].dtype)


def _finish_sum_norm(r, ex_refs, o_refs, first_rows):
    del first_rows
    for e in ex_refs[:-1]:
        r = r + e[...]
    o_refs[0][...] = r
    o_refs[1][...] = ((r * lax.rsqrt(jnp.mean(r * r, axis=-1, keepdims=True) + EPS)) * ex_refs[-1][...]).astype(BF16)


_EPI_ROWS = 16


def _finish_rms_bwd(r_ref, ex_refs, o_refs, first_rows):
    x_ref, g_ref, dres_ref = ex_refs
    dx_ref, dxb_ref, dg_ref = o_refs

    @pl.when(first_rows)
    def _():
        dg_ref[...] = jnp.zeros_like(dg_ref)

    gv = g_ref[...]
    inv_d = 1.0 / r_ref.shape[1]

    def step(i, dg_acc):
        groups = [pl.ds(pl.multiple_of(i * (2 * _EPI_ROWS) + u * _EPI_ROWS, _EPI_ROWS), _EPI_ROWS) for u in range(2)]
        sums = []
        for rows in groups:
            r, xf = r_ref[rows, :], x_ref[rows, :]
            sums.append((jnp.sum(xf * xf, axis=-1, keepdims=True), jnp.sum((r * gv) * xf, axis=-1, keepdims=True)))
        for rows, (sxx, sax) in zip(groups, sums):
            r, xf = r_ref[rows, :], x_ref[rows, :]
            rs = lax.rsqrt(sxx * inv_d + EPS)
            dg_acc = dg_acc + _psum8(r * (xf * rs))
            dx = rs * (r * gv) - xf * (rs * rs * (sax * rs * inv_d)) + dres_ref[rows, :]
            dx_ref[rows, :] = dx
            dxb_ref[rows, :] = dx.astype(BF16)
        return dg_acc

    dg_acc = lax.fori_loop(0, r_ref.shape[0] // (2 * _EPI_ROWS), step, jnp.zeros((SUB, r_ref.shape[1]), F32))
    dg_ref[...] += jnp.sum(dg_acc, axis=0, keepdims=True)


def _rms_bwd_io(M, tm, x, g, dres):
    rows = pl.BlockSpec((tm, D), lambda m, n, k: (m, 0))
    vec = pl.BlockSpec((1, D), lambda m, n, k: (0, 0))
    return ([(x, rows), (g, vec), (dres, rows)],
            [(SDS((M, D), F32), rows), (SDS((M, D), BF16), rows), (SDS((1, D), F32), vec)])


_K_WHOLE = 3072


def _mm_nn(a, b, *, out_dtype, name, bias=None, add=None, norm=None):
    M, K = a.shape
    tk = K if K <= _K_WHOLE else _tile(K, 1024)
    if K <= 1024 and norm is None:
        tm = _tile(M, 2048 if add is None and out_dtype == BF16 else 1024)
    else:
        tm = _tile(M, 512)
    if b.ndim == 3:
        nb, _, bw = b.shape
        N, tn, nn = nb * bw, bw, nb
        b_spec = pl.BlockSpec((None, tk, bw), lambda m, n, k: (_perm2(n), k, 0))
    else:
        N = b.shape[1]
        tn = _tile(N, 1024)
        nn = N // tn
        b_spec = pl.BlockSpec((tk, tn), lambda m, n, k: (k, n))
    tile = pl.BlockSpec((tm, tn), lambda m, n, k: (m, n))
    vec = pl.BlockSpec((1, tn), lambda m, n, k: (0, n))
    extras = ([] if bias is None else [(bias, vec)]) + ([] if add is None else [(add, tile)])
    outs, finish = [(SDS((M, N), out_dtype), tile)], _finish_sum
    if norm is not None:
        assert tn == N == D and out_dtype == F32
        extras.append((norm, vec))
        outs, finish = outs + [(SDS((M, N), BF16), tile)], _finish_sum_norm
    return _mm_call(name, (M // tm, nn, K // tk), [(a, b)], [pl.BlockSpec((tm, tk), lambda m, n, k: (m, k)), b_spec],
                    _NN, (tm, tn), extras, outs, finish)


def _mm_nt(a, b, *, out_dtype, name, add=None, rms=None):
    M, N = a.shape
    if b.ndim == 3:
        nb, Ko, bw = b.shape
        tm = _tile(M, 1024)
        tn, tk, nk = _tile(Ko, 1024), bw, nb
        b_spec = pl.BlockSpec((None, tn, bw), lambda m, n, k: (_perm2(k), n, 0))
    else:
        Ko = b.shape[0]
        tk = N if N <= _K_WHOLE else _tile(N, 1024)
        if N <= 1024 and rms is None:
            tm = _tile(M, 2048 if add is None and out_dtype == BF16 else 1024)
        else:
            tm = _tile(M, 512)
        tn = _tile(Ko, 1024)
        nk = N // tk
        b_spec = pl.BlockSpec((tn, tk), lambda m, n, k: (n, k))
    tile = pl.BlockSpec((tm, tn), lambda m, n, k: (m, n))
    extras = [] if add is None else [(add, tile)]
    outs, finish = [(SDS((M, Ko), out_dtype), tile)], _finish_sum
    if rms is not None:
        assert tn == Ko == D and add is None
        (extras, outs), finish = _rms_bwd_io(M, tm, *rms), _finish_rms_bwd
    return _mm_call(name, (M // tm, Ko // tn, nk), [(a, b)], [pl.BlockSpec((tm, tk), lambda m, n, k: (m, k)), b_spec],
                    _NT, (tm, tn), extras, outs, finish, from_ref=rms is not None)


def _mm_nt_cols(parts, b, *, name, rms):
    M = parts[0].shape[0]
    tm = _tile(M, 512)
    specs, off = [], 0
    for p in parts:
        w = p.shape[1]
        assert off % w == 0
        specs.append(pl.BlockSpec((tm, w), lambda m, n, k: (m, 0)))
        specs.append(pl.BlockSpec((D, w), functools.partial(lambda m, n, k, o: (0, o), o=off // w)))
        off += w
    extras, outs = _rms_bwd_io(M, tm, *rms)
    return _mm_call(name, (M // tm, 1, 1), [(p, b) for p in parts], specs, _NT, (tm, D), extras, outs, _finish_rms_bwd,
                    from_ref=True)


def _mm_tn(a, b, *, out_dtype, name, blocks=None):
    S, Ka = a.shape
    Nb = b.shape[1]
    tm = _tile(Ka, 1024)
    if blocks is not None:
        bw = blocks
        tn, nn = bw, Nb // bw
        out = (SDS((nn, Ka, bw), out_dtype), pl.BlockSpec((None, tm, bw), lambda m, n, k: (_perm2(n), m, 0)))
    else:
        tn = _tile(Nb, 1024)
        nn = Nb // tn
        out = (SDS((Ka, Nb), out_dtype), pl.BlockSpec((tm, tn), lambda m, n, k: (m, n)))
    steps = (Ka // tm) * nn
    tk = _tile(S, 4096 if steps >= 4 else 2048 if steps >= 2 else 1024)
    return _mm_call(name, (Ka // tm, nn, S // tk), [(a, b)],
                    [pl.BlockSpec((tk, tm), lambda m, n, k: (k, m)), pl.BlockSpec((tk, tn), lambda m, n, k: (k, n))],
                    _TN, (tm, tn), [], [out], _finish_sum)


def _row(tm, c):
    return pl.BlockSpec((tm, c), lambda i: (i, 0))


def _full(shape):
    nd = len(shape)
    return pl.BlockSpec(shape, lambda i: (0,) * nd)


def _rms_fwd(x, g, name):
    S = x.shape[0]
    tm = min(S, TM_ROW)

    def body(x_ref, g_ref, o_ref):
        xf = x_ref[...]
        r = lax.rsqrt(jnp.mean(xf * xf, axis=-1, keepdims=True) + EPS)
        o_ref[...] = ((xf * r) * g_ref[...]).astype(BF16)

    return pl.pallas_call(body, out_shape=SDS((S, D), BF16), grid=(S // tm,), in_specs=[_row(tm, D), _full((1, D))],
                          out_specs=_row(tm, D), name=name, compiler_params=_cp(1))(x, g)


def _rms_bwd(x, g, dn, dres, name):
    S = x.shape[0]
    tm = min(S, TM_ROW)
    want_dx = dres is not None

    def body(x_ref, g_ref, dn_ref, *rest):
        i = pl.program_id(0)
        dg_ref = rest[-1]

        @pl.when(i == 0)
        def _():
            dg_ref[...] = jnp.zeros_like(dg_ref)

        xf = x_ref[...]
        r = lax.rsqrt(jnp.mean(xf * xf, axis=-1, keepdims=True) + EPS)
        y = xf * r
        dn_v = dn_ref[...]
        dg_ref[...] += jnp.sum(dn_v * y, axis=0, keepdims=True)
        if want_dx:
            dres_ref, dx_ref, dxb_ref = rest[0], rest[1], rest[2]
            dy = dn_v * g_ref[...]
            dx = r * (dy - y * jnp.mean(dy * y, axis=-1, keepdims=True)) + dres_ref[...]
            dx_ref[...] = dx
            dxb_ref[...] = dx.astype(BF16)

    ins = [x, g, dn] + ([dres] if want_dx else [])
    in_specs = [_row(tm, D), _full((1, D)), _row(tm, D)] + ([_row(tm, D)] if want_dx else [])
    outs = ([SDS((S, D), F32), SDS((S, D), BF16)] if want_dx else []) + [SDS((1, D), F32)]
    out_specs = ([_row(tm, D), _row(tm, D)] if want_dx else []) + [_full((1, D))]
    return pl.pallas_call(body, out_shape=outs, grid=(S // tm,), in_specs=in_specs, out_specs=out_specs, name=name,
                          compiler_params=_cp(1))(*ins)


def _loss_head(x, g, tgt):
    S = x.shape[0]
    tm = min(S, TM_ROW)

    def body(x_ref, g_ref, t_ref, loss_ref, dx_ref, dxb_ref, dg_ref):
        i = pl.program_id(0)

        @pl.when(i == 0)
        def _():
            loss_ref[...] = jnp.zeros_like(loss_ref)
            dg_ref[...] = jnp.zeros_like(dg_ref)

        xf = x_ref[...]
        r = lax.rsqrt(jnp.mean(xf * xf, axis=-1, keepdims=True) + EPS)
        y = xf * r
        gv = g_ref[...]
        err = y * gv - t_ref[...]
        per_row = jnp.mean(err * err, axis=-1, keepdims=True)
        loss_ref[...] += 0.5 * jnp.sum(per_row, axis=0, keepdims=True)
        dn_v = err * (1.0 / D)
        dg_ref[...] += jnp.sum(dn_v * y, axis=0, keepdims=True)
        dy = dn_v * gv
        dx = r * (dy - y * jnp.mean(dy * y, axis=-1, keepdims=True))
        dx_ref[...] = dx
        dxb_ref[...] = dx.astype(BF16)

    return pl.pallas_call(
        body, out_shape=[SDS((1, 1), F32), SDS((S, D), F32), SDS((S, D), BF16), SDS((1, D), F32)], grid=(S // tm,),
        in_specs=[_row(tm, D), _full((1, D)), _row(tm, D)],
        out_specs=[_full((1, 1)), _row(tm, D), _row(tm, D), _full((1, D))], name="loss_head", compiler_params=_cp(1),
    )(x, g, tgt)


def _softmax_rows(s):
    m = jnp.max(s, axis=-1, keepdims=True)
    e = jnp.exp(s - m)
    return e / jnp.sum(e, axis=-1, keepdims=True)


def _attn_fwd(q, k, v, name):
    S = q.shape[0]
    tm = min(S, TM_ROW)
    scale = XA_HD ** -0.5

    def body(q_ref, k_ref, v_ref, o_ref):
        for h in range(XA_HEADS):
            sl = slice(h * XA_HD, (h + 1) * XA_HD)
            s = lax.dot_general(q_ref[:, sl], k_ref[:, sl], _NT, preferred_element_type=F32) * scale
            p = _softmax_rows(s)
            o_ref[:, sl] = lax.dot_general(p.astype(BF16), v_ref[:, sl], _NN, preferred_element_type=F32).astype(BF16)

    return pl.pallas_call(body, out_shape=SDS((S, D), BF16), grid=(S // tm,),
                          in_specs=[_row(tm, D), _full((N_MEM, D)), _full((N_MEM, D))], out_specs=_row(tm, D),
                          name=name, compiler_params=_cp(1))(q, k, v)


def _attn_bwd(q, k, v, do, name):
    S = q.shape[0]
    tm = min(S, TM_ROW)
    scale = XA_HD ** -0.5

    def body(q_ref, k_ref, v_ref, do_ref, dq_ref, dk_ref, dv_ref):
        i = pl.program_id(0)

        @pl.when(i == 0)
        def _():
            dk_ref[...] = jnp.zeros_like(dk_ref)
            dv_ref[...] = jnp.zeros_like(dv_ref)

        for h in range(XA_HEADS):
            sl = slice(h * XA_HD, (h + 1) * XA_HD)
            qh, kh, vh, doh = q_ref[:, sl], k_ref[:, sl], v_ref[:, sl], do_ref[:, sl]
            s = lax.dot_general(qh, kh, _NT, preferred_element_type=F32) * scale
            p = _softmax_rows(s)
            pb = p.astype(BF16)
            dv_ref[:, sl] += lax.dot_general(pb, doh, _TN, preferred_element_type=F32)
            dp = lax.dot_general(doh, vh, _NT, preferred_element_type=F32)
            ds = (p * (dp - jnp.sum(dp * p, axis=-1, keepdims=True)) * scale).astype(BF16)
            dq_ref[:, sl] = lax.dot_general(ds, kh, _NN, preferred_element_type=F32).astype(BF16)
            dk_ref[:, sl] += lax.dot_general(ds, qh, _TN, preferred_element_type=F32)

    return pl.pallas_call(
        body, out_shape=[SDS((S, D), BF16), SDS((N_MEM, D), F32), SDS((N_MEM, D), F32)], grid=(S // tm,),
        in_specs=[_row(tm, D), _full((N_MEM, D)), _full((N_MEM, D)), _row(tm, D)],
        out_specs=[_row(tm, D), _full((N_MEM, D)), _full((N_MEM, D))], name=name, compiler_params=_cp(1),
    )(q, k, v, do)


def _sigmoid(x):
    return 1.0 / (1.0 + jnp.exp(-x))


def _ln_silu_fwd(cv, g, b):
    S = cv.shape[0]
    tm = min(S, TM_ROW)

    def body(x_ref, g_ref, b_ref, o_ref):
        xf = x_ref[...]
        mu = jnp.mean(xf, axis=-1, keepdims=True)
        xc = xf - mu
        rstd = lax.rsqrt(jnp.mean(xc * xc, axis=-1, keepdims=True) + EPS)
        ln = (xc * rstd) * g_ref[...] + b_ref[...]
        o_ref[...] = (ln * _sigmoid(ln)).astype(BF16)

    return pl.pallas_call(body, out_shape=SDS((S, D), BF16), grid=(S // tm,),
                          in_specs=[_row(tm, D), _full((1, D)), _full((1, D))], out_specs=_row(tm, D),
                          name="ln_silu_fwd", compiler_params=_cp(1))(cv, g, b)


def _ln_silu_bwd(ds, cv, g, b, dx):
    S = cv.shape[0]
    tm = min(S, TM_ROW)

    def body(ds_ref, x_ref, g_ref, b_ref, dx_ref, dcv_ref, dg_ref, db_ref, db2_ref):
        i = pl.program_id(0)

        @pl.when(i == 0)
        def _():
            dg_ref[...] = jnp.zeros_like(dg_ref)
            db_ref[...] = jnp.zeros_like(db_ref)
            db2_ref[...] = jnp.zeros_like(db2_ref)

        xf = x_ref[...]
        mu = jnp.mean(xf, axis=-1, keepdims=True)
        xc = xf - mu
        rstd = lax.rsqrt(jnp.mean(xc * xc, axis=-1, keepdims=True) + EPS)
        xhat = xc * rstd
        gv = g_ref[...]
        ln = xhat * gv + b_ref[...]
        sg = _sigmoid(ln)
        dln = ds_ref[...].astype(F32) * (sg + ln * sg * (1.0 - sg))
        dg_ref[...] += jnp.sum(dln * xhat, axis=0, keepdims=True)
        db_ref[...] += jnp.sum(dln, axis=0, keepdims=True)
        db2_ref[...] += jnp.sum(dx_ref[...], axis=0, keepdims=True)
        dxh = dln * gv
        dcv_ref[...] = rstd * (dxh - jnp.mean(dxh, axis=-1, keepdims=True)
                               - xhat * jnp.mean(dxh * xhat, axis=-1, keepdims=True))

    return pl.pallas_call(
        body, out_shape=[SDS((S, D), F32), SDS((1, D), F32), SDS((1, D), F32), SDS((1, D), F32)], grid=(S // tm,),
        in_specs=[_row(tm, D), _row(tm, D), _full((1, D)), _full((1, D)), _row(tm, D)],
        out_specs=[_row(tm, D), _full((1, D)), _full((1, D)), _full((1, D))], name="ln_silu_bwd",
        compiler_params=_cp(1),
    )(ds, cv, g, b, dx)


_GELU_C, _GELU_K = 0.7978845608028654, 0.044715


def _gelu(x, with_grad=False):
    x2 = x * x
    t = jnp.tanh(_GELU_C * (x + _GELU_K * x * x2))
    gel = 0.5 * x * (1.0 + t)
    if not with_grad:
        return gel
    return gel, 0.5 * (1.0 + t) + 0.5 * x * (1.0 - t * t) * (_GELU_C * (1.0 + 3.0 * _GELU_K * x2))


def _expm1(x):
    poly = x * (1.0 + x * (0.5 + x * (1.0 / 6.0 + x * (1.0 / 24.0 + x * (1.0 / 120.0)))))
    return jnp.where(jnp.abs(x) < 0.05, poly, jnp.exp(x) - 1.0)


def _softplus(x):
    return jnp.maximum(x, 0.0) + jnp.log1p(jnp.exp(-jnp.abs(x)))


_SCAN_UNROLL = 8
_RB = 32
_HB = 16


def _sub_blocks(n_rows, n_lanes, fn):
    def step(idx, c):
        r0 = pl.multiple_of(idx * _RB, _RB)
        for lt in range(n_lanes // LANE):
            fn(r0, lt)
        return c

    lax.fori_loop(0, n_rows // _RB, step, 0)


def _lanes(lt):
    return pl.ds(lt * LANE, LANE)


def _psum8(x):
    parts = [x[i * SUB:(i + 1) * SUB] for i in range(x.shape[0] // SUB)]
    return functools.reduce(lambda p, q: p + q, parts)


def _scan_fwd(a_s, b_s, out_ref, carry_ref, n_groups):
    row = lax.broadcasted_iota(jnp.int32, (SUB, LANE), 0)
    U = _SCAN_UNROLL

    def step(gi, carry):
        base = gi * (SUB * U)
        parts = []
        for u in range(U):
            i = pl.multiple_of(base + u * SUB, SUB)
            a8, b8 = a_s[pl.ds(i, SUB), :], b_s[pl.ds(i, SUB), :]
            for s in (1, 2, 4):
                a_sh = jnp.where(row >= s, pltpu.roll(a8, s, 0), 1.0)
                b_sh = jnp.where(row >= s, pltpu.roll(b8, s, 0), 0.0)
                b8 = a8 * b_sh + b8
                a8 = a8 * a_sh
            parts.append((i, a8, b8))
        for i, a8, b8 in parts:
            h8 = a8 * carry + b8
            out_ref[pl.ds(i, SUB), :] = h8
            carry = jnp.broadcast_to(h8[SUB - 1:SUB, :], (SUB, LANE))
        return carry

    carry_ref[...] = lax.fori_loop(0, n_groups // U, step, carry_ref[...])


def _scan_bwd(a_s, b_s, out_ref, carry_ref, n_groups):
    row = lax.broadcasted_iota(jnp.int32, (SUB, LANE), 0)
    U = _SCAN_UNROLL

    def step(gi, carry):
        base = (n_groups // U - 1 - gi) * (SUB * U)
        parts = []
        for u in reversed(range(U)):
            i = pl.multiple_of(base + u * SUB, SUB)
            a8, b8 = a_s[pl.ds(i, SUB), :], b_s[pl.ds(i, SUB), :]
            for s in (1, 2, 4):
                a_sh = jnp.where(row < SUB - s, pltpu.roll(a8, SUB - s, 0), 1.0)
                b_sh = jnp.where(row < SUB - s, pltpu.roll(b8, SUB - s, 0), 0.0)
                b8 = a8 * b_sh + b8
                a8 = a8 * a_sh
            parts.append((i, a8, b8))
        for i, a8, b8 in parts:
            h8 = a8 * carry + b8
            out_ref[pl.ds(i, SUB), :] = h8
            carry = jnp.broadcast_to(h8[0:1, :], (SUB, LANE))
        return carry

    carry_ref[...] = lax.fori_loop(0, n_groups // U, step, carry_ref[...])


def _rglru_pre(xr, wgx_ref, bgx_ref, wga_ref, bga_ref, lam_ref):
    xrb = xr.astype(BF16)
    wgx, wga = wgx_ref[0].astype(BF16), wga_ref[0].astype(BF16)
    gx = _sigmoid(lax.dot_general(xrb, wgx, _NN, preferred_element_type=F32) + bgx_ref[...])
    ga = _sigmoid(lax.dot_general(xrb, wga, _NN, preferred_element_type=F32) + bga_ref[...])
    sp = _softplus(-lam_ref[...])
    log_a = -C_RG * ga * sp
    a = jnp.exp(log_a)
    mult = jnp.sqrt(-_expm1(2.0 * log_a))
    return gx, ga, sp, a, mult, xrb, wgx, wga


def _a_specs():
    vec = pl.BlockSpec((1, HD_A), lambda c, j: (0, c))
    mat = pl.BlockSpec((1, HD_A, HD_A), lambda c, j: (c, 0, 0))
    return [pl.BlockSpec((CONV_A, HD_A), lambda c, j: (0, c)), vec, mat, vec, mat, vec, vec]


def _a_fwd(zp, conv_w, conv_b, wgx, bgx, wga, bga, lam):
    S = zp.shape[0]
    R, nt = R_RGLRU, D // HD_A
    H = SUB

    def body(zg_ref, zr_ref, cw_ref, cb_ref, wgx_ref, bgx_ref, wga_ref, bga_ref, lam_ref, ya_ref, h_ref,
             ext, a_s, b_s, hc):
        j = pl.program_id(1)

        @pl.when(j == 0)
        def _():
            ext[0:H, :] = jnp.zeros((H, HD_A), F32)
            hc[...] = jnp.zeros_like(hc)

        ext[H:H + R, :] = zr_ref[...].astype(F32)
        xr = cb_ref[...]
        for k in range(CONV_A):
            xr = xr + cw_ref[k:k + 1, :] * ext[pl.ds(H - (CONV_A - 1 - k), R), :]
        gx, _, _, a, mult, _, _, _ = _rglru_pre(xr, wgx_ref, bgx_ref, wga_ref, bga_ref, lam_ref)
        a_s[...] = a
        b_s[...] = mult * (gx * xr)
        _scan_fwd(a_s, b_s, h_ref, hc, R // SUB)
        ya_ref[...] = (_gelu(zg_ref[...].astype(F32)) * h_ref[...]).astype(BF16)
        ext[0:H, :] = ext[R:R + H, :]

    return pl.pallas_call(
        body, out_shape=[SDS((S, D + D // 2), BF16), SDS((S, D), F32)], grid=(nt, S // R),
        in_specs=[pl.BlockSpec((R, HD_A), lambda c, j: (j, c)), pl.BlockSpec((R, HD_A), lambda c, j: (j, nt + c))]
        + _a_specs(),
        out_specs=[pl.BlockSpec((R, HD_A), lambda c, j: (j, c)), pl.BlockSpec((R, HD_A), lambda c, j: (j, c))],
        scratch_shapes=[pltpu.VMEM((H + R, HD_A), F32), pltpu.VMEM((R, HD_A), F32), pltpu.VMEM((R, HD_A), F32),
                        pltpu.VMEM((SUB, HD_A), F32)],
        name="rglru_fwd", compiler_params=_cp(2),
    )(zp, zp, conv_w, conv_b, wgx, bgx, wga, bga, lam)


def _a_bwd(dyab, zp, h, conv_w, conv_b, wgx, bgx, wga, bga, lam):
    S = zp.shape[0]
    R, nt, nch = R_RGLRU, D // HD_A, S // R_RGLRU
    H = SUB

    def rows(c, j):
        return (nch - 1 - j, c)

    def rows_rec(c, j):
        return (nch - 1 - j, nt + c)

    def halo(c, j):
        return (jnp.maximum((nch - 1 - j) * (R // H) - 1, 0), c)

    def halo_z(c, j):
        return (jnp.maximum((nch - 1 - j) * (R // _HB) - 1, 0), nt + c)

    def body(dy_ref, zg_ref, zr_ref, zh_ref, h_ref, hh_ref, cw_ref, cb_ref, wgx_ref, bgx_ref, wga_ref, bga_ref,
             lam_ref, dzg_ref, dzr_ref, dcw_ref, dcb_ref, dwgx_ref, dbgx_ref, dwga_ref, dbga_ref, dlam_ref,
             ext_z, ext_h, ext_mu, ext_d, a_s, b_s, muc):
        j = pl.program_id(1)
        first_chunk = (nch - 1 - j) == 0

        @pl.when(j == 0)
        def _():
            ext_mu[R:R + H, :] = jnp.zeros((H, HD_A), F32)
            ext_d[R:R + H, :] = jnp.zeros((H, HD_A), F32)
            muc[...] = jnp.zeros_like(muc)
            for r in (dcw_ref, dcb_ref, dwgx_ref, dbgx_ref, dwga_ref, dbga_ref, dlam_ref):
                r[...] = jnp.zeros_like(r)

        zg = zg_ref[...].astype(F32)
        ext_z[0:H, :] = jnp.where(first_chunk, 0.0, zh_ref[_HB - H:_HB, :].astype(F32))
        ext_z[H:H + R, :] = zr_ref[...].astype(F32)
        ext_h[0:H, :] = jnp.where(first_chunk, 0.0, hh_ref[...])
        ext_h[H:H + R, :] = h_ref[...]
        xr = cb_ref[...]
        for k in range(CONV_A):
            xr = xr + cw_ref[k:k + 1, :] * ext_z[pl.ds(H - (CONV_A - 1 - k), R), :]
        gx, ga, sp, a, mult, xrb, wgxb, wgab = _rglru_pre(xr, wgx_ref, bgx_ref, wga_ref, bga_ref, lam_ref)
        gel, dgel = _gelu(zg, with_grad=True)
        dy = dy_ref[...].astype(F32)
        dh = dy * gel
        dzg_ref[...] = (dy * h_ref[...] * dgel).astype(BF16)
        a_s[...] = a
        b_s[...] = a * dh
        _scan_bwd(a_s, b_s, ext_mu, muc, R // SUB)
        lam_t = dh + ext_mu[pl.ds(1, R), :]
        ext_mu[R:R + H, :] = ext_mu[0:H, :]
        da = lam_t * ext_h[pl.ds(H - 1, R), :]
        gxr = gx * xr
        dlog_a = da * a - (lam_t * gxr) * (a * a) / mult
        dgx = lam_t * mult * xr
        dxr = lam_t * mult * gx
        lam_v = lam_ref[...]
        dlam_ref[...] += jnp.sum(dlog_a * ga, axis=0, keepdims=True) * (C_RG * _sigmoid(-lam_v))
        dpa = (dlog_a * (-C_RG * sp)) * ga * (1.0 - ga)
        dpx = dgx * gx * (1.0 - gx)
        dbga_ref[...] += jnp.sum(dpa, axis=0, keepdims=True)
        dbgx_ref[...] += jnp.sum(dpx, axis=0, keepdims=True)
        dpab, dpxb = dpa.astype(BF16), dpx.astype(BF16)
        dwga_ref[0] += lax.dot_general(xrb, dpab, _TN, preferred_element_type=F32)
        dwgx_ref[0] += lax.dot_general(xrb, dpxb, _TN, preferred_element_type=F32)
        dxr = (dxr + lax.dot_general(dpab, wgab, _NT, preferred_element_type=F32)
               + lax.dot_general(dpxb, wgxb, _NT, preferred_element_type=F32))
        dcb_ref[...] += jnp.sum(dxr, axis=0, keepdims=True)
        ext_d[0:R, :] = dxr
        dzr = jnp.zeros((R, HD_A), F32)
        for k in range(CONV_A):
            sh = CONV_A - 1 - k
            dcw_ref[k:k + 1, :] += jnp.sum(dxr * ext_z[pl.ds(H - sh, R), :], axis=0, keepdims=True)
            dzr = dzr + cw_ref[k:k + 1, :] * ext_d[pl.ds(sh, R), :]
        dzr_ref[...] = dzr.astype(BF16)
        ext_d[R:R + H, :] = ext_d[0:H, :]

    vec_o = pl.BlockSpec((1, HD_A), lambda c, j: (0, c))
    mat_o = pl.BlockSpec((1, HD_A, HD_A), lambda c, j: (c, 0, 0))
    return pl.pallas_call(
        body,
        out_shape=[SDS((S, D), BF16), SDS((S, D), BF16), SDS((CONV_A, D), F32), SDS((1, D), F32),
                   SDS((nt, HD_A, HD_A), F32), SDS((1, D), F32), SDS((nt, HD_A, HD_A), F32), SDS((1, D), F32),
                   SDS((1, D), F32)],
        grid=(nt, nch),
        in_specs=[pl.BlockSpec((R, HD_A), rows), pl.BlockSpec((R, HD_A), rows), pl.BlockSpec((R, HD_A), rows_rec),
                  pl.BlockSpec((_HB, HD_A), halo_z), pl.BlockSpec((R, HD_A), rows),
                  pl.BlockSpec((H, HD_A), halo)] + _a_specs(),
        out_specs=[pl.BlockSpec((R, HD_A), rows), pl.BlockSpec((R, HD_A), rows),
                   pl.BlockSpec((CONV_A, HD_A), lambda c, j: (0, c)), vec_o, mat_o, vec_o, mat_o, vec_o, vec_o],
        scratch_shapes=[pltpu.VMEM((H + R, HD_A), F32), pltpu.VMEM((H + R, HD_A), F32), pltpu.VMEM((R + H, HD_A), F32),
                        pltpu.VMEM((R + H, HD_A), F32), pltpu.VMEM((R, HD_A), F32), pltpu.VMEM((R, HD_A), F32),
                        pltpu.VMEM((SUB, HD_A), F32)],
        name="rglru_bwd", compiler_params=_cp(2),
    )(dyab, zp, zp, zp, h, h, conv_w, conv_b, wgx, bgx, wga, bga, lam)


_POOL_H = 16
_POOL_T0 = 2 * D // HD_A
_POOL_Y0 = D // HD_A


def _window_sum(lv, n, lo, rows, g, ahead):
    base = 0 if ahead else SUB
    cur, win = lv[0], None
    for i, s in enumerate((1, 2, 4, 8)):
        val = cur[pl.ds(base, n), :] + cur[pl.ds(base + (s if ahead else -s), n), :]
        sel = val[lo:lo + rows]
        win = sel if win is None else jnp.where(g >= i, sel, win)
        if i < 3:
            lv[i + 1][pl.ds(base, n), :] = val
            cur = lv[i + 1]
    return win


def _pool_width(g):
    return jnp.where(g == 0, 2.0, jnp.where(g == 1, 4.0, jnp.where(g == 2, 8.0, 16.0)))


def _b_fwd(zp, yab, wg, bg, sc):
    S = zp.shape[0]
    R, H = min(S, R_POOL), _POOL_H

    def body(z_ref, wg_ref, bg_ref, sc_ref, yab_in, yb_ref, *lv):
        del yab_in
        g, j = pl.program_id(0), pl.program_id(1)

        @pl.when(j == 0)
        def _():
            for r in lv:
                r[0:SUB, :] = jnp.zeros((SUB, HD_A), F32)
            lv[0][SUB:SUB + H, :] = jnp.zeros((H, HD_A), F32)

        u = z_ref[...].astype(F32)
        lv[0][SUB + H:SUB + H + R, :] = u
        t1 = (j * R + 1 + lax.broadcasted_iota(jnp.int32, (R, HD_A), 0)).astype(F32)
        p = _window_sum(lv, H + R, H, R, g, False) / jnp.minimum(t1, _pool_width(g)) - u
        lin = lax.dot_general(p.astype(BF16), wg_ref[0].astype(BF16), _NN, preferred_element_type=F32) + bg_ref[...]
        yb_ref[...] = (lin * sc_ref[...]).astype(BF16)
        lv[0][SUB:SUB + H, :] = lv[0][SUB + R:SUB + R + H, :]

    vec = pl.BlockSpec((1, HD_A), lambda g, j: (0, g))
    return pl.pallas_call(
        body, out_shape=SDS(yab.shape, yab.dtype), grid=(len(POOL_WINDOWS), S // R),
        in_specs=[pl.BlockSpec((R, HD_A), lambda g, j: (j, _POOL_T0 + g)),
                  pl.BlockSpec((1, HD_A, HD_A), lambda g, j: (g, 0, 0)), vec, vec, pl.BlockSpec(memory_space=pl.ANY)],
        out_specs=pl.BlockSpec((R, HD_A), lambda g, j: (j, _POOL_Y0 + g)),
        scratch_shapes=[pltpu.VMEM((SUB + H + R, HD_A), F32)] * 4, input_output_aliases={4: 0},
        name="pool_fwd", compiler_params=_cp(2),
    )(zp, wg, bg, sc, yab)


def _b_bwd(dyab, zp, wg, bg, sc):
    S = zp.shape[0]
    R, H, ng = min(S, R_POOL), _POOL_H, len(POOL_WINDOWS)
    nch = S // R

    def body(dy_ref, z_ref, zh_ref, wg_ref, bg_ref, sc_ref, dz_ref, dwg_ref, dbg_ref, dsc_ref, *scratch):
        lu, lq = scratch[:4], scratch[4:]
        g, j = pl.program_id(0), pl.program_id(1)
        jj = nch - 1 - j

        @pl.when(j == 0)
        def _():
            for r in lu:
                r[0:SUB, :] = jnp.zeros((SUB, HD_A), F32)
            for r in lq:
                r[R + H:R + H + SUB, :] = jnp.zeros((SUB, HD_A), F32)
            lq[0][R:R + H, :] = jnp.zeros((H, HD_A), F32)
            for r in (dwg_ref, dbg_ref, dsc_ref):
                r[...] = jnp.zeros_like(r)

        u = z_ref[...].astype(F32)
        lu[0][SUB:SUB + H, :] = jnp.where(jj == 0, 0.0, zh_ref[...].astype(F32))
        lu[0][SUB + H:SUB + H + R, :] = u
        t1 = (jj * R + 1 + lax.broadcasted_iota(jnp.int32, (R, HD_A), 0)).astype(F32)
        cnt = jnp.minimum(t1, _pool_width(g))
        pb = (_window_sum(lu, H + R, H, R, g, False) / cnt - u).astype(BF16)
        wgb = wg_ref[0].astype(BF16)
        lin = lax.dot_general(pb, wgb, _NN, preferred_element_type=F32) + bg_ref[...]
        dy = dy_ref[...].astype(F32)
        dsc_ref[...] += jnp.sum(dy * lin, axis=0, keepdims=True)
        dlin = dy * sc_ref[...]
        dbg_ref[...] += jnp.sum(dlin, axis=0, keepdims=True)
        dlb = dlin.astype(BF16)
        dwg_ref[0] += lax.dot_general(pb, dlb, _TN, preferred_element_type=F32)
        dp = lax.dot_general(dlb, wgb, _NT, preferred_element_type=F32)
        lq[0][0:R, :] = dp / cnt
        dz_ref[...] = (_window_sum(lq, R + H, 0, R, g, True) - dp).astype(BF16)
        lq[0][R:R + H, :] = lq[0][0:H, :]

    vec = pl.BlockSpec((1, HD_A), lambda g, j: (0, g))
    mat = pl.BlockSpec((1, HD_A, HD_A), lambda g, j: (g, 0, 0))
    return pl.pallas_call(
        body, out_shape=[SDS((S, D // 2), BF16), SDS((ng, HD_A, HD_A), F32), SDS((1, D // 2), F32),
                         SDS((1, D // 2), F32)],
        grid=(ng, nch),
        in_specs=[pl.BlockSpec((R, HD_A), lambda g, j: (nch - 1 - j, _POOL_Y0 + g)),
                  pl.BlockSpec((R, HD_A), lambda g, j: (nch - 1 - j, _POOL_T0 + g)),
                  pl.BlockSpec((H, HD_A), lambda g, j: (jnp.maximum((nch - 1 - j) * (R // H) - 1, 0), _POOL_T0 + g)),
                  mat, vec, vec],
        out_specs=[pl.BlockSpec((R, HD_A), lambda g, j: (nch - 1 - j, g)), mat, vec, vec],
        scratch_shapes=[pltpu.VMEM((SUB + H + R, HD_A), F32)] * 8,
        name="pool_bwd", compiler_params=_cp(2),
    )(dyab, zp, zp, wg, bg, sc)


_CW_F = 768


def _f_fwd(hp, w, b, name):
    S = hp.shape[0]
    R, H, cw = min(S, R_FFN), SUB, _CW_F
    nlt = cw // LANE

    def body(h_ref, w_ref, b_ref, o_ref, gel_ref, ud_ref, ext):
        j = pl.program_id(1)

        @pl.when(j == 0)
        def _():
            ext[:, 0:H, :] = jnp.zeros((nlt, H, LANE), F32)

        def stage(r0, lt):
            ext[lt, pl.ds(pl.multiple_of(r0 + H, SUB), _RB), :] = h_ref[pl.ds(r0, _RB), _lanes(lt)].astype(F32)

        def main(r0, lt):
            ls = _lanes(lt)
            gp = b_ref[:, ls]
            for k in range(CONV_F):
                gp = gp + w_ref[k:k + 1, ls] * ext[lt, pl.ds(r0 + (H - (CONV_F - 1 - k)), _RB), :]
            up = h_ref[pl.ds(r0, _RB), _lanes(lt + nlt)].astype(F32)
            gel, dgel = _gelu(gp, with_grad=True)
            rs = pl.ds(r0, _RB)
            o_ref[rs, ls] = (gel * up).astype(BF16)
            gel_ref[rs, ls] = gel.astype(BF16)
            ud_ref[rs, ls] = (up * dgel).astype(BF16)

        _sub_blocks(R, cw, stage)
        _sub_blocks(R, cw, main)
        ext[:, 0:H, :] = ext[:, R:R + H, :]

    tile = pl.BlockSpec((R, cw), lambda c, j: (j, c))
    return pl.pallas_call(
        body, out_shape=[SDS((S, D_FF), BF16)] * 3, grid=(D_FF // cw, S // R),
        in_specs=[pl.BlockSpec((R, 2 * cw), lambda c, j: (j, c)), pl.BlockSpec((CONV_F, cw), lambda c, j: (0, c)),
                  pl.BlockSpec((1, cw), lambda c, j: (0, c))],
        out_specs=[tile] * 3,
        scratch_shapes=[pltpu.VMEM((nlt, H + R, LANE), F32)], name=name, compiler_params=_cp(2),
    )(hp, w, b)


def _f_bwd(dact, hp, gel, ud, w, name):
    S = hp.shape[0]
    R, H, cw = min(S, R_FFN), SUB, _CW_F
    nch = S // R
    nlt = cw // LANE

    def body(da_ref, h_ref, hh_ref, gel_ref, ud_ref, w_ref, dh_ref, dw_ref, db_ref, ext_g, ext_d, acc):
        j = pl.program_id(1)
        jj = nch - 1 - j

        @pl.when(j == 0)
        def _():
            ext_d[:, R:R + H, :] = jnp.zeros((nlt, H, LANE), F32)
            acc[...] = jnp.zeros_like(acc)

        for lt in range(nlt):
            ext_g[lt, 0:H, :] = jnp.where(jj == 0, 0.0, hh_ref[_HB - H:_HB, lt * LANE:(lt + 1) * LANE].astype(F32))

        def stage(r0, lt):
            ext_g[lt, pl.ds(pl.multiple_of(r0 + H, SUB), _RB), :] = h_ref[pl.ds(r0, _RB), _lanes(lt)].astype(F32)

        def first(r0, lt):
            ls, lu, rs = _lanes(lt), _lanes(lt + nlt), pl.ds(r0, _RB)
            da = da_ref[rs, ls].astype(F32)
            dh_ref[rs, lu] = (da * gel_ref[rs, ls].astype(F32)).astype(BF16)
            dgp = da * ud_ref[rs, ls].astype(F32)
            ext_d[lt, rs, :] = dgp
            acc[CONV_F * SUB:(CONV_F + 1) * SUB, ls] += _psum8(dgp)
            for k in range(CONV_F):
                tap = ext_g[lt, pl.ds(r0 + (H - (CONV_F - 1 - k)), _RB), :]
                acc[k * SUB:(k + 1) * SUB, ls] += _psum8(dgp * tap)

        def second(r0, lt):
            ls = _lanes(lt)
            dhg = w_ref[CONV_F - 1:CONV_F, ls] * ext_d[lt, pl.ds(r0, _RB), :]
            for k in range(CONV_F - 1):
                dhg = dhg + w_ref[k:k + 1, ls] * ext_d[lt, pl.ds(r0 + (CONV_F - 1 - k), _RB), :]
            dh_ref[pl.ds(r0, _RB), ls] = dhg.astype(BF16)

        _sub_blocks(R, cw, stage)
        _sub_blocks(R, cw, first)
        _sub_blocks(R, cw, second)
        ext_d[:, R:R + H, :] = ext_d[:, 0:H, :]

        @pl.when(j == nch - 1)
        def _():
            for k in range(CONV_F):
                dw_ref[k:k + 1, :] = jnp.sum(acc[k * SUB:(k + 1) * SUB, :], axis=0, keepdims=True)
            db_ref[...] = jnp.sum(acc[CONV_F * SUB:(CONV_F + 1) * SUB, :], axis=0, keepdims=True)

    rows = lambda c, j: (nch - 1 - j, c)
    return pl.pallas_call(
        body, out_shape=[SDS((S, 2 * D_FF), BF16), SDS((CONV_F, D_FF), F32), SDS((1, D_FF), F32)],
        grid=(D_FF // cw, nch),
        in_specs=[pl.BlockSpec((R, cw), rows), pl.BlockSpec((R, cw), lambda c, j: (nch - 1 - j, 2 * c)),
                  pl.BlockSpec((_HB, cw), lambda c, j: (jnp.maximum((nch - 1 - j) * (R // _HB) - 1, 0), 2 * c)),
                  pl.BlockSpec((R, cw), rows), pl.BlockSpec((R, cw), rows),
                  pl.BlockSpec((CONV_F, cw), lambda c, j: (0, c))],
        out_specs=[pl.BlockSpec((R, 2 * cw), rows), pl.BlockSpec((CONV_F, cw), lambda c, j: (0, c)),
                   pl.BlockSpec((1, cw), lambda c, j: (0, c))],
        scratch_shapes=[pltpu.VMEM((nlt, H + R, LANE), F32), pltpu.VMEM((nlt, R + H, LANE), F32),
                        pltpu.VMEM(((CONV_F + 1) * SUB, cw), F32)], name=name,
        compiler_params=_cp(2),
    )(dact, hp, hp, gel, ud, w)


_CW_C = 256
_H_C = 32


def _c_fwd(h1p, w, b):
    S = h1p.shape[0]
    R, H, cw = R_SEQ, _H_C, _CW_C
    nlt = cw // LANE

    def body(h_ref, w_ref, b_ref, o_ref, ext):
        j = pl.program_id(1)

        @pl.when(j == 0)
        def _():
            ext[:, 0:H, :] = jnp.zeros((nlt, H, LANE), F32)

        def stage(r0, lt):
            rs = pl.ds(r0, _RB)
            gate = h_ref[rs, _lanes(lt + nlt)].astype(F32)
            ext[lt, pl.ds(pl.multiple_of(r0 + H, SUB), _RB), :] = h_ref[rs, _lanes(lt)].astype(F32) * _sigmoid(gate)

        def main(r0, lt):
            ls = _lanes(lt)
            cv = b_ref[:, ls]
            for k in range(CONV_C):
                cv = cv + w_ref[k:k + 1, ls] * ext[lt, pl.ds(r0 + (H - (CONV_C - 1 - k)), _RB), :]
            o_ref[pl.ds(r0, _RB), ls] = cv

        _sub_blocks(R, cw, stage)
        _sub_blocks(R, cw, main)
        ext[:, 0:H, :] = ext[:, R:R + H, :]

    return pl.pallas_call(
        body, out_shape=SDS((S, D), F32), grid=(D // cw, S // R),
        in_specs=[pl.BlockSpec((R, 2 * cw), lambda c, j: (j, c)), pl.BlockSpec((CONV_C, cw), lambda c, j: (0, c)),
                  pl.BlockSpec((1, cw), lambda c, j: (0, c))],
        out_specs=pl.BlockSpec((R, cw), lambda c, j: (j, c)),
        scratch_shapes=[pltpu.VMEM((nlt, H + R, LANE), F32)], name="conf_conv_fwd", compiler_params=_cp(2),
    )(h1p, w, b)


def _c_bwd(dcv, h1p, w):
    S = h1p.shape[0]
    R, H, cw, nch = R_SEQ, _H_C, _CW_C, S // R_SEQ
    nlt = cw // LANE
    a_b, a_val, a_gate = CONV_C * SUB, (CONV_C + 1) * SUB, (CONV_C + 2) * SUB

    def body(dc_ref, h_ref, hh_ref, w_ref, dh_ref, dw_ref, db_ref, db1_ref, ext_u, ext_d, acc):
        j = pl.program_id(1)
        jj = nch - 1 - j

        @pl.when(j == 0)
        def _():
            ext_d[:, R:R + H, :] = jnp.zeros((nlt, H, LANE), F32)
            acc[...] = jnp.zeros_like(acc)

        for lt in range(nlt):
            ext_u[lt, 0:H, :] = jnp.where(
                jj == 0, 0.0, hh_ref[:, lt * LANE:(lt + 1) * LANE].astype(F32)
                * _sigmoid(hh_ref[:, cw + lt * LANE:cw + (lt + 1) * LANE].astype(F32)))

        def stage(r0, lt):
            rs, ls = pl.ds(r0, _RB), _lanes(lt)
            gate = h_ref[rs, _lanes(lt + nlt)].astype(F32)
            ext_u[lt, pl.ds(pl.multiple_of(r0 + H, SUB), _RB), :] = h_ref[rs, ls].astype(F32) * _sigmoid(gate)
            ext_d[lt, rs, :] = dc_ref[rs, ls]

        def first(r0, lt):
            ls = _lanes(lt)
            dc = dc_ref[pl.ds(r0, _RB), ls]
            acc[a_b:a_b + SUB, ls] += _psum8(dc)
            for k in range(CONV_C):
                tap = ext_u[lt, pl.ds(r0 + (H - (CONV_C - 1 - k)), _RB), :]
                acc[k * SUB:(k + 1) * SUB, ls] += _psum8(dc * tap)

        def second(r0, lt):
            rs, ls, lg = pl.ds(r0, _RB), _lanes(lt), _lanes(lt + nlt)
            du = w_ref[CONV_C - 1:CONV_C, ls] * ext_d[lt, rs, :]
            for k in range(CONV_C - 1):
                du = du + w_ref[k:k + 1, ls] * ext_d[lt, pl.ds(r0 + (CONV_C - 1 - k), _RB), :]
            val = h_ref[rs, ls].astype(F32)
            sg = _sigmoid(h_ref[rs, lg].astype(F32))
            dval = du * sg
            dgate = du * val * sg * (1.0 - sg)
            acc[a_val:a_val + SUB, ls] += _psum8(dval)
            acc[a_gate:a_gate + SUB, ls] += _psum8(dgate)
            dh_ref[rs, ls] = dval.astype(BF16)
            dh_ref[rs, lg] = dgate.astype(BF16)

        _sub_blocks(R, cw, stage)
        _sub_blocks(R, cw, first)
        _sub_blocks(R, cw, second)
        ext_d[:, R:R + H, :] = ext_d[:, 0:H, :]

        @pl.when(j == nch - 1)
        def _():
            for k in range(CONV_C):
                dw_ref[k:k + 1, :] = jnp.sum(acc[k * SUB:(k + 1) * SUB, :], axis=0, keepdims=True)
            db_ref[...] = jnp.sum(acc[a_b:a_b + SUB, :], axis=0, keepdims=True)
            db1_ref[:, 0:cw] = jnp.sum(acc[a_val:a_val + SUB, :], axis=0, keepdims=True)
            db1_ref[:, cw:2 * cw] = jnp.sum(acc[a_gate:a_gate + SUB, :], axis=0, keepdims=True)

    rows = lambda c, j: (nch - 1 - j, c)
    return pl.pallas_call(
        body, out_shape=[SDS((S, 2 * D), BF16), SDS((CONV_C, D), F32), SDS((1, D), F32), SDS((1, 2 * D), F32)],
        grid=(D // cw, nch),
        in_specs=[pl.BlockSpec((R, cw), rows), pl.BlockSpec((R, 2 * cw), rows),
                  pl.BlockSpec((H, 2 * cw), lambda c, j: (jnp.maximum((nch - 1 - j) * (R // H) - 1, 0), c)),
                  pl.BlockSpec((CONV_C, cw), lambda c, j: (0, c))],
        out_specs=[pl.BlockSpec((R, 2 * cw), rows), pl.BlockSpec((CONV_C, cw), lambda c, j: (0, c)),
                   pl.BlockSpec((1, cw), lambda c, j: (0, c)), pl.BlockSpec((1, 2 * cw), lambda c, j: (0, c))],
        scratch_shapes=[pltpu.VMEM((nlt, H + R, LANE), F32), pltpu.VMEM((nlt, R + H, LANE), F32),
                        pltpu.VMEM(((CONV_C + 3) * SUB, cw), F32)], name="conf_conv_bwd",
        compiler_params=_cp(2),
    )(dcv, h1p, h1p, w)


def _local_step(x, mem, tgt, W, fetch=None, send=None):
    G = {}
    W = dict(W)

    def arrive(group, after):
        if fetch is None:
            return None
        got, tok = fetch(group, after)
        for key, val in got.items():
            W[key] = {**W.get(key, {}), **val} if isinstance(val, dict) else val
        return tok

    def gain(g, tok):
        return g if tok is None else g + tok

    def sent(group):
        return None if send is None else send(group, G)

    def xattn_fwd(xin, n, l):
        tok = arrive(("xa", l), n)
        mn = _rms_fwd(mem, gain(W["xa_mem_norm"][l:l + 1], tok), f"xa_memnorm_fwd{l}")
        q = _mm_nn(n, W["xa_wq"][l], out_dtype=BF16, name=f"xa_q{l}")
        k = _mm_nn(mn, W["xa_wk"][l], out_dtype=BF16, name=f"xa_k{l}")
        v = _mm_nn(mn, W["xa_wv"][l], out_dtype=BF16, name=f"xa_v{l}")
        o = _attn_fwd(q, k, v, f"xa_attn_fwd{l}")
        xout, nout = _mm_nn(o, W["xa_wo"][l], out_dtype=F32, name=f"xa_o{l}", add=xin, norm=W["f_norm"][l:l + 1])
        return xout, nout, (xin, n, q, mn, k, v, o)

    def xattn_bwd(dx, dxb, saved, l):
        xin, n, q, mn, k, v, o = saved
        do = _mm_nt(dxb, W["xa_wo"][l], out_dtype=BF16, name=f"xa_do{l}")
        G[f"xa_wo{l}"] = _mm_tn(o, dxb, out_dtype=BF16, name=f"xa_dwo{l}")
        dq, dk, dv = _attn_bwd(q, k, v, do, f"xa_attn_bwd{l}")
        dkb, dvb = dk.astype(BF16), dv.astype(BF16)
        G[f"xa_wq{l}"] = _mm_tn(n, dq, out_dtype=BF16, name=f"xa_dwq{l}")
        G[f"xa_wk{l}"] = _mm_tn(mn, dkb, out_dtype=BF16, name=f"xa_dwk{l}")
        G[f"xa_wv{l}"] = _mm_tn(mn, dvb, out_dtype=BF16, name=f"xa_dwv{l}")
        tok = sent(("xa", l))
        dmn = _mm_nt(dkb, W["xa_wk"][l], out_dtype=F32, name=f"xa_dmn_k{l}")
        dmn = _mm_nt(dvb, W["xa_wv"][l], out_dtype=F32, name=f"xa_dmn_v{l}", add=dmn)
        (G[f"xa_mem_norm{l}"],) = _rms_bwd(mem, W["xa_mem_norm"][l:l + 1], dmn, None, f"xa_memnorm_bwd{l}")
        dx, dxb, G[f"xa_norm{l}"] = _mm_nt(dq, W["xa_wq"][l], out_dtype=F32, name=f"xa_dn{l}",
                                           rms=(xin, gain(W["xa_norm"][l:l + 1], tok), dx))
        return dx, dxb

    def ffn_fwd(xin, n, l, next_gain):
        tok = arrive(("f", l), n)
        hp = _mm_nn(n, W["f_w_up"][l], out_dtype=BF16, name=f"f_up{l}")
        act, gel, ud = _f_fwd(hp, W["f_dw_w"][l], gain(W["f_dw_b"][l:l + 1], tok), f"f_conv_fwd{l}")
        arrive(("fd", l), act)
        res = _mm_nn(act, W["f_w_down"][l], out_dtype=F32, name=f"f_down{l}", add=xin, norm=next_gain)
        xout, nout = res if next_gain is not None else (res, None)
        return xout, nout, (xin, n, hp, act, gel, ud)

    def ffn_bwd(dx, dxb, saved, l):
        xin, n, hp, act, gel, ud = saved
        dact = _mm_nt(dxb, W["f_w_down"][l], out_dtype=BF16, name=f"f_dact{l}")
        G[f"f_w_down{l}"] = _mm_tn(act, dxb, out_dtype=BF16, name=f"f_dwdown{l}")
        dhp, G[f"f_dw_w{l}"], G[f"f_dw_b{l}"] = _f_bwd(dact, hp, gel, ud, W["f_dw_w"][l], f"f_conv_bwd{l}")
        G[f"f_w_up{l}"] = _mm_tn(n, dhp, out_dtype=BF16, name=f"f_dwup{l}", blocks=_CW_F)
        tok = sent(("f", l))
        dx, dxb, G[f"f_norm{l}"] = _mm_nt(dhp, W["f_w_up"][l], out_dtype=F32, name=f"f_dn{l}",
                                          rms=(xin, gain(W["f_norm"][l:l + 1], tok), dx))
        return dx, dxb

    n0 = _rms_fwd(x, W["ab_norm"], "ab_norm_fwd")
    tok = arrive(("ab", 0), n0)
    a_par = (W["a_conv_w"], gain(W["a_conv_b"], tok), W["a_gate_x_w"], W["a_gate_x_b"], W["a_gate_a_w"],
             W["a_gate_a_b"], W["a_lambda"])
    b_par = (W["b_group_w"], W["b_group_b"], W["b_scale"])
    zp = _mm_nn(n0, W["ab_w_in"], out_dtype=BF16, name="ab_in")
    yab, h_a = _a_fwd(zp, *a_par)
    yab = _b_fwd(zp, yab, *b_par)
    arrive(("ab", 1), yab)
    x1, n1 = _mm_nn(yab, W["ab_w_out"], out_dtype=F32, name="ab_out", add=x, norm=W["xa_norm"][0:1])
    x2, n2, s_xa0 = xattn_fwd(x1, n1, 0)
    x3, n3, s_f0 = ffn_fwd(x2, n2, 0, W["c_norm"])
    tok = arrive(("c", 0), n3)
    h1p = _mm_nn(n3, W["c_w_pw1"], out_dtype=BF16, name="c_pw1", bias=gain(W["c_b_pw1"], tok))
    cv = _c_fwd(h1p, W["c_dw_w"], W["c_dw_b"])
    sc = _ln_silu_fwd(cv, W["c_ln_g"], W["c_ln_b"])
    x4, n4 = _mm_nn(sc, W["c_w_pw2"], out_dtype=F32, name="c_pw2", bias=W["c_b_pw2"], add=x3, norm=W["xa_norm"][1:2])
    x5, n5, s_xa1 = xattn_fwd(x4, n4, 1)
    x6, _, s_f1 = ffn_fwd(x5, n5, 1, None)
    loss, dx, dxb, G["final_norm"] = _loss_head(x6, W["final_norm"], tgt)

    dx, dxb = ffn_bwd(dx, dxb, s_f1, 1)
    dx, dxb = xattn_bwd(dx, dxb, s_xa1, 1)
    dsc = _mm_nt(dxb, W["c_w_pw2"], out_dtype=BF16, name="c_dsc")
    G["c_w_pw2"] = _mm_tn(sc, dxb, out_dtype=BF16, name="c_dwpw2")
    dcv, G["c_ln_g"], G["c_ln_b"], G["c_b_pw2"] = _ln_silu_bwd(dsc, cv, W["c_ln_g"], W["c_ln_b"], dx)
    dh1p, G["c_dw_w"], G["c_dw_b"], G["c_b_pw1"] = _c_bwd(dcv, h1p, W["c_dw_w"])
    G["c_w_pw1"] = _mm_tn(n3, dh1p, out_dtype=BF16, name="c_dwpw1", blocks=_CW_C)
    tok = sent(("c", 0))
    dx, dxb, G["c_norm"] = _mm_nt(dh1p, W["c_w_pw1"], out_dtype=F32, name="c_dn",
                                  rms=(x3, gain(W["c_norm"], tok), dx))
    dx, dxb = ffn_bwd(dx, dxb, s_f0, 0)
    dx, dxb = xattn_bwd(dx, dxb, s_xa0, 0)
    dyab = _mm_nt(dxb, W["ab_w_out"], out_dtype=BF16, name="ab_dyab")
    G["ab_w_out"] = _mm_tn(yab, dxb, out_dtype=BF16, name="ab_dwout")
    tok = sent(("ab", 1))
    a_par = (a_par[0], gain(a_par[1], tok)) + a_par[2:]
    (dzg, dzr, G["a_conv_w"], G["a_conv_b"], G["a_gate_x_w"], G["a_gate_x_b"], G["a_gate_a_w"], G["a_gate_a_b"],
     G["a_lambda"]) = _a_bwd(dyab, zp, h_a, *a_par)
    dzq, G["b_group_w"], G["b_group_b"], G["b_scale"] = _b_bwd(dyab, zp, *b_par)
    G["ab_w_in"] = jnp.concatenate(
        [_mm_tn(n0, dz, out_dtype=BF16, name=f"ab_dwin_{part}")
         for part, dz in (("gate", dzg), ("rec", dzr), ("pool", dzq))], axis=1)
    tok = sent(("ab", 0))
    dx, _, G["ab_norm"] = _mm_nt_cols([dzg, dzr, dzq], W["ab_w_in"], name="ab_dn",
                                      rms=(x, gain(W["ab_norm"], tok), dx))
    return loss, dx, G


def _my_place():
    x, y, c = lax.axis_index("x"), lax.axis_index("y"), lax.axis_index("c")
    return x, y, c


def _all_gather(shards, name):
    n = len(shards)

    def body(*refs):
        ins, outs = refs[:n], refs[n:2 * n]
        send_sems, recv_sems, local_sems = refs[2 * n:]
        x, y, c = _my_place()
        me, sibling = (x, y, c), (x, y, 1 - c)
        chips = [(1 - x, y), (x, 1 - y), (1 - x, 1 - y)]

        def slab(a, place):
            px, py, pc = place
            return outs[a].at[4 * px + 2 * py + pc]

        def copy(a, k, block, to, src=None):
            return pltpu.make_async_remote_copy(
                src_ref=slab(a, block) if src is None else src, dst_ref=slab(a, block),
                send_sem=send_sems.at[a, k], recv_sem=recv_sems.at[a, k], device_id=to, device_id_type=MESH)

        mine = [pltpu.make_async_copy(ins[a], slab(a, me), local_sems.at[a]) for a in range(n)]
        for cp in mine:
            cp.start()
        first = []
        for j, chip in enumerate(chips):
            first += [copy(a, 1 + j, me, (*chip, c), src=ins[a]) for a in range(n)]
        first += [copy(a, 0, me, sibling, src=ins[a]) for a in range(n)]
        for cp in first:
            cp.start()
        passed = []
        for j, chip in enumerate(chips):
            for a in range(n):
                copy(a, 1 + j, (*chip, c), me).wait_recv()
                cp = copy(a, 4 + j, (*chip, c), sibling)
                cp.start()
                passed.append(cp)
        for a in range(n):
            copy(a, 0, sibling, me).wait_recv()
        for j, chip in enumerate(chips):
            for a in range(n):
                copy(a, 4 + j, (*chip, 1 - c), me).wait_recv()
        for cp in first + passed:
            cp.wait_send()
        for cp in mine:
            cp.wait()

    any_spec = pl.BlockSpec(memory_space=pl.ANY)
    return pl.pallas_call(
        body, out_shape=[SDS((N_DEV,) + s.shape, s.dtype) for s in shards], in_specs=[any_spec] * n,
        out_specs=[any_spec] * n,
        scratch_shapes=[pltpu.SemaphoreType.DMA((n, 7)), pltpu.SemaphoreType.DMA((n, 7)), pltpu.SemaphoreType.DMA((n,))],
        name=name,
    )(*shards)


_HBM = pl.BlockSpec(memory_space=pltpu.HBM)
_SEM = pl.BlockSpec(memory_space=pltpu.SEMAPHORE)
_EFFECT = pltpu.SideEffectType.DATAFLOW_SIDE_EFFECTING


def _peer_places():
    x, y, c = _my_place()
    peers = []
    for k in range(1, N_DEV):
        px = 1 - x if (k >> 2) & 1 else x
        py = 1 - y if (k >> 1) & 1 else y
        pc = 1 - c if k & 1 else c
        peers.append(((px, py, pc), 4 * px + 2 * py + pc))
    return (x, y, c), 4 * x + 2 * y + c, peers


def _send_start(srcs, per_dest, name):
    n = len(srcs)
    lands = [lax.empty((N_DEV,) + (s.shape[1:] if per_dest else s.shape), s.dtype) for s in srcs]

    def body(*refs):
        src, land = refs[:n], refs[n:2 * n]
        outs = refs[2 * n:]
        send, recv, token = outs[:n], outs[n:2 * n], outs[4 * n]
        place, me, peers = _peer_places()
        for a in range(n):
            for peer, pidx in peers + [(place, me)]:
                pltpu.make_async_remote_copy(
                    src_ref=src[a].at[pidx] if per_dest else src[a], dst_ref=land[a].at[me], send_sem=send[a],
                    recv_sem=recv[a], device_id=peer, device_id_type=MESH).start()
        token[...] = jnp.zeros_like(token)

    hbm = lambda a: pltpu.HBM(a.shape, a.dtype)
    sem = pltpu.SemaphoreType.DMA(())
    res = pl.pallas_call(
        body, name=name,
        out_shape=tuple([sem] * (2 * n) + [hbm(s) for s in srcs] + [hbm(l) for l in lands]
                        + [SDS((SUB, LANE), F32)]),
        in_specs=[_HBM] * (2 * n),
        out_specs=tuple([_SEM] * (2 * n) + [_HBM] * (2 * n) + [pl.BlockSpec(memory_space=pltpu.VMEM)]),
        input_output_aliases={i: 2 * n + i for i in range(2 * n)},
        compiler_params=pltpu.CompilerParams(has_side_effects=_EFFECT),
    )(*[pltpu.with_memory_space_constraint(s, pltpu.HBM) for s in srcs],
      *[pltpu.with_memory_space_constraint(l, pltpu.HBM) for l in lands])
    return res[:n], res[n:2 * n], res[2 * n:3 * n], res[3 * n:4 * n], res[4 * n]


def _send_wait(send, recv, srcs, lands, after, per_dest, name):
    n = len(srcs)

    def body(*refs):
        src, land = refs[:n], refs[n:2 * n]
        send_s, recv_s = refs[2 * n:3 * n], refs[3 * n:4 * n]
        token = refs[-1]
        place, _, _ = _peer_places()
        for a in range(n):
            copy = pltpu.make_async_remote_copy(
                src_ref=src[a] if per_dest else land[a], dst_ref=land[a], send_sem=send_s[a],
                recv_sem=recv_s[a], device_id=place, device_id_type=MESH)
            copy.wait_send()
            copy.wait_recv()
        token[...] = jnp.zeros_like(token)

    hbm = lambda a: pltpu.HBM(a.shape, a.dtype)
    res = pl.pallas_call(
        body, name=name,
        out_shape=tuple([hbm(s) for s in srcs] + [hbm(l) for l in lands] + [SDS((SUB, LANE), F32)]),
        in_specs=[_HBM] * (2 * n) + [_SEM] * (2 * n) + [pl.BlockSpec(memory_space=pl.ANY)],
        out_specs=tuple([_HBM] * (2 * n) + [pl.BlockSpec(memory_space=pltpu.VMEM)]),
        input_output_aliases={i: i for i in range(2 * n)},
        compiler_params=pltpu.CompilerParams(has_side_effects=_EFFECT),
    )(*srcs, *lands, *send, *recv, after)
    return res[:n], res[n:2 * n], res[2 * n]


def _adamw_math(w, g, m, v):
    m = ADAM_B1 * m + (1.0 - ADAM_B1) * g
    v = ADAM_B2 * v + (1.0 - ADAM_B2) * (g * g)
    m_hat = m / (1.0 - ADAM_B1 ** ADAM_STEP)
    v_hat = v / (1.0 - ADAM_B2 ** ADAM_STEP)
    delta = -ADAM_LR * (m_hat / (jnp.sqrt(v_hat) + ADAM_EPS) + ADAM_WD * w)
    return delta, m, v


def _row_tile(r, c, itemsize_rows):
    cap = max(SUB, (itemsize_rows // (4 * c)) // SUB * SUB)
    if r <= cap:
        return r
    best = None
    for t in range(SUB, cap + 1, SUB):
        if r % t == 0:
            best = t
    return best if best is not None else r


def _sum_adamw(landing, w, m, v, name, layer=0, prev=None, after=None):
    _, r, c = landing.shape
    tr = _row_tile(r, c, 2 << 20)
    off = layer * (r // tr)
    tail = ([] if prev is None else list(prev)) + ([] if after is None else [after])

    def body(l_ref, w_ref, m_ref, v_ref, *rest):
        g_ref, d_ref, mo_ref, vo_ref = rest[-4:]
        g = l_ref[0].astype(F32)
        for s in range(1, N_DEV):
            g = g + l_ref[s].astype(F32)
        g_ref[...] = g
        d_ref[...], mo_ref[...], vo_ref[...] = _adamw_math(w_ref[...], g, m_ref[...], v_ref[...])

    blk = pl.BlockSpec((tr, c), lambda i: (i + off, 0))
    n_prev = 0 if prev is None else 4
    return pl.pallas_call(
        body, out_shape=[SDS(w.shape, F32)] * 4, grid=(r // tr,),
        in_specs=[pl.BlockSpec((N_DEV, tr, c), lambda i: (0, i, 0)), blk, blk, blk]
        + [pl.BlockSpec(memory_space=pl.ANY)] * len(tail),
        out_specs=[blk] * 4, input_output_aliases={4 + i: i for i in range(n_prev)}, name=name,
        compiler_params=_cp(1),
    )(landing, w, m, v, *tail)


def _sum8(landing, name):
    _, r, c = landing.shape

    def body(l_ref, g_ref):
        g = l_ref[0]
        for s in range(1, N_DEV):
            g = g + l_ref[s]
        g_ref[...] = g

    return pl.pallas_call(body, out_shape=SDS((r, c), F32), name=name, compiler_params=_cp(0))(landing)


def _adamw_small(repl_pack, own_pack, P, M, V):
    table, off = [], 0
    for name, shape in _REPL.items():
        table.append((name, shape if len(shape) > 1 else (1,) + shape, 0, off // LANE))
        off += _size(shape)
    off = _REPL_ROWS * LANE
    for name, shape in _SMALL_SHARDED.items():
        table.append((name, shape, 1, off // LANE))
        off += _size(shape)
    n = len(table)

    def body(*refs):
        packs, ins, outs = refs[:2], refs[2:2 + 3 * n], refs[2 + 3 * n:]
        for p, (_, shape, which, r0) in enumerate(table):
            w_ref, m_ref, v_ref = ins[3 * p:3 * p + 3]
            g_ref, d_ref, mo_ref, vo_ref = outs[4 * p:4 * p + 4]
            pack, rows, q = packs[which], shape[-2], shape[-1] // LANE
            lead = [()]
            for dim in shape[:-2]:
                lead = [t + (i,) for t in lead for i in range(dim)]
            for li, idx in enumerate(lead):
                if q == 1:
                    dst = g_ref.at[idx] if idx else g_ref
                    dst[...] = pack[r0 + li * rows:r0 + (li + 1) * rows, :]
                    continue
                for i in range(rows):
                    for k in range(q):
                        row = r0 + (li * rows + i) * q + k
                        g_ref[idx + (slice(i, i + 1), slice(k * LANE, (k + 1) * LANE))] = pack[row:row + 1, :]
            d_ref[...], mo_ref[...], vo_ref[...] = _adamw_math(w_ref[...], g_ref[...], m_ref[...], v_ref[...])

    ins, out_shape = [], []
    for name, shape, _, _ in table:
        ins += [t[name].reshape(shape) for t in (P, M, V)]
        out_shape += [SDS(shape, F32)] * 4
    res = pl.pallas_call(body, out_shape=out_shape, name="adamw_small", compiler_params=_cp(0))(
        repl_pack, own_pack, *ins)
    dicts = ({}, {}, {}, {})
    for p, (name, shape, _, _) in enumerate(table):
        for d, arr in zip(dicts, res[4 * p:4 * p + 4]):
            d[name] = arr.reshape(P[name].shape)
    return dicts


_BIG = {
    "ab_w_in": (1, D, 320), "ab_w_out": (1, 192, D), "c_w_pw1": (1, D, 256), "c_w_pw2": (1, 128, D),
    "xa_wq": (2, 128, D), "xa_wk": (2, 128, D), "xa_wv": (2, 128, D), "xa_wo": (2, 128, D),
    "f_w_up": (2, D, 768), "f_w_down": (2, 384, D),
}
_SMALL_SHARDED = {
    "a_conv_w": (1, 4, 128), "c_norm": (1, 128), "c_b_pw1": (1, 256), "c_dw_w": (1, 31, 128), "c_dw_b": (1, 128),
    "c_ln_g": (1, 128), "c_ln_b": (1, 128), "c_b_pw2": (1, 128), "f_dw_w": (2, 3, 384),
}
_REPL = {
    "ab_norm": (1, D), "a_conv_b": (1, D), "a_gate_x_w": (1, 8, 128, 128), "a_gate_x_b": (1, D),
    "a_gate_a_w": (1, 8, 128, 128), "a_gate_a_b": (1, D), "a_lambda": (1, D), "b_group_w": (1, 4, 128, 128),
    "b_group_b": (1, 512), "b_scale": (1, 512), "xa_norm": (2, D), "xa_mem_norm": (2, D), "f_norm": (2, D),
    "f_dw_b": (2, D_FF), "final_norm": (D,),
}


def _size(shape):
    n = 1
    for s in shape:
        n *= s
    return n


_N_SS = sum(_size(s) for s in _SMALL_SHARDED.values())
_N_REPL = sum(_size(s) for s in _REPL.values())
_REPL_ROWS = -(-_N_REPL // (N_DEV * SUB * LANE)) * SUB
_SS_ROWS = _N_SS // LANE
_SMALL_ROWS = -(-(_REPL_ROWS + _SS_ROWS) // SUB) * SUB


def _pack(parts, rows):
    flat = jnp.concatenate([p.reshape(-1).astype(F32) for p in parts])
    return jnp.pad(flat, (0, rows * LANE - flat.shape[0])).reshape(rows, LANE)


def _pair_blocks(v, bw):
    lead, n = v.shape[:-1], v.shape[-1]
    return jnp.swapaxes(v.reshape(lead + (2, n // (2 * bw), bw)), -3, -2).reshape(lead + (n,))


def _unpair_blocks(v, bw):
    lead, n = v.shape[:-1], v.shape[-1]
    return jnp.swapaxes(v.reshape(lead + (n // (2 * bw), 2, bw)), -3, -2).reshape(lead + (n,))


_GROUPS = {
    ("ab", 0): (("ab_w_in", 0),),
    ("ab", 1): (("ab_w_out", 0),),
    ("xa", 0): (("xa_wq", 0), ("xa_wk", 0), ("xa_wv", 0), ("xa_wo", 0)),
    ("f", 0): (("f_w_up", 0),),
    ("fd", 0): (("f_w_down", 0),),
    ("c", 0): (("c_w_pw1", 0), ("c_w_pw2", 0)),
    ("xa", 1): (("xa_wq", 1), ("xa_wk", 1), ("xa_wv", 1), ("xa_wo", 1)),
    ("f", 1): (("f_w_up", 1),),
    ("fd", 1): (("f_w_down", 1),),
}
_SEND_GROUPS = {g: m for g, m in _GROUPS.items() if g[0] != "fd"}
_SEND_GROUPS[("f", 0)] = (("f_w_up", 0), ("f_w_down", 0))
_SEND_GROUPS[("f", 1)] = (("f_w_up", 1), ("f_w_down", 1))


def _weight_layout(name, g):
    if name == "ab_w_in":
        return jnp.swapaxes(g, 0, 1).reshape(D, N_DEV * 320)
    if name in ("c_w_pw1", "f_w_up"):
        return g
    return g.reshape(N_DEV * g.shape[1], D)


def _grad_blocks(name, l, G):
    _, r, c = _BIG[name]
    if name == "ab_w_in":
        return jnp.swapaxes(G[name].reshape(D, N_DEV, 320), 0, 1)
    if name == "c_w_pw1":
        return G[name]
    if name == "f_w_up":
        return G[f"{name}{l}"]
    return (G[name] if _BIG[name][0] == 1 else G[f"{name}{l}"]).reshape(N_DEV, r, c)


def _small_layouts(sm):
    W = {}
    sm = sm.reshape(N_DEV, -1)
    off = 0
    for name, shape in _SMALL_SHARDED.items():
        n = _size(shape)
        blocks = sm[:, off:off + n].reshape((N_DEV,) + shape)
        off += n
        W[name] = jnp.moveaxis(blocks, 0, -2).reshape(shape[:-1] + (N_DEV * shape[-1],))
    W["a_conv_w"], W["c_dw_w"] = W["a_conv_w"][0], W["c_dw_w"][0]
    W["c_b_pw1"] = _pair_blocks(W["c_b_pw1"], _CW_C)
    return W


def _to_dest_major(g, shape):
    full = g.reshape(shape[:-1] + (N_DEV, shape[-1]))
    return jnp.moveaxis(full, -2, 0).reshape(N_DEV, -1)


def kernel(x, mem, ab_norm, ab_w_in, a_conv_w, a_conv_b, a_gate_x_w, a_gate_x_b, a_gate_a_w, a_gate_a_b, a_lambda, b_group_w, b_group_b, b_scale, ab_w_out, c_norm, c_w_pw1, c_b_pw1, c_dw_w, c_dw_b, c_ln_g, c_ln_b, c_w_pw2, c_b_pw2, xa_norm, xa_mem_norm, xa_wq, xa_wk, xa_wv, xa_wo, f_norm, f_w_up, f_dw_w, f_dw_b, f_w_down, final_norm, loss_target, m_ab_norm, m_ab_w_in, m_a_conv_w, m_a_conv_b, m_a_gate_x_w, m_a_gate_x_b, m_a_gate_a_w, m_a_gate_a_b, m_a_lambda, m_b_group_w, m_b_group_b, m_b_scale, m_ab_w_out, m_c_norm, m_c_w_pw1, m_c_b_pw1, m_c_dw_w, m_c_dw_b, m_c_ln_g, m_c_ln_b, m_c_w_pw2, m_c_b_pw2, m_xa_norm, m_xa_mem_norm, m_xa_wq, m_xa_wk, m_xa_wv, m_xa_wo, m_f_norm, m_f_w_up, m_f_dw_w, m_f_dw_b, m_f_w_down, m_final_norm, v_ab_norm, v_ab_w_in, v_a_conv_w, v_a_conv_b, v_a_gate_x_w, v_a_gate_x_b, v_a_gate_a_w, v_a_gate_a_b, v_a_lambda, v_b_group_w, v_b_group_b, v_b_scale, v_ab_w_out, v_c_norm, v_c_w_pw1, v_c_b_pw1, v_c_dw_w, v_c_dw_b, v_c_ln_g, v_c_ln_b, v_c_w_pw2, v_c_b_pw2, v_xa_norm, v_xa_mem_norm, v_xa_wq, v_xa_wk, v_xa_wv, v_xa_wo, v_f_norm, v_f_w_up, v_f_dw_w, v_f_dw_b, v_f_w_down, v_final_norm):
    args = dict(locals())
    P = {n: args[n] for n in _NAMES}
    M = {n: args["m_" + n] for n in _NAMES}
    V = {n: args["v_" + n] for n in _NAMES}

    in_flight = {}

    def launch(groups, tok):
        shards, n_of = [], {}
        for grp in groups:
            for name, l in _GROUPS[grp]:
                w = P[name][l] if tok is None else P[name][l] + tok
                shards.append(w.astype(BF16))
            if grp == ("ab", 0):
                shards.append(_pack([P[n] for n in _SMALL_SHARDED], _SS_ROWS + 4))
            n_of[grp] = len(shards)
        res = _send_start(shards, False, "gather_start_" + "_".join(g[0] + str(g[1]) for g in groups))
        lo = 0
        for grp in groups:
            in_flight[grp] = [r[lo:n_of[grp]] for r in res[:4]]
            lo = n_of[grp]
        return res[4][:1, :1]

    follow = {("ab", 0): [("ab", 1), ("xa", 0), ("f", 0), ("fd", 0)], ("xa", 0): [("c", 0), ("xa", 1)],
              ("f", 0): [("f", 1), ("fd", 1)]}

    def fetch(grp, after):
        send_s, recv_s, srcs, lands = in_flight.pop(grp)
        srcs, lands, tok = _send_wait(send_s, recv_s, srcs, lands, after, False, f"gather_wait_{grp[0]}{grp[1]}")
        tok = launch(follow[grp], tok[:1, :1]) if grp in follow else None
        full = lands
        out = {}
        for (name, l), g in zip(_GROUPS[grp], full):
            w = _weight_layout(name, g)
            if _BIG[name][0] == 1:
                out[name] = w
            else:
                out[name] = {l: w}
        if grp == ("ab", 0):
            out.update(_small_layouts(full[-1]))
        return out, tok

    zero = launch([("ab", 0)], None)

    pending, held = [], []
    rides_with_next = {("xa", 1), ("f", 0)}

    def send(grp, G):
        held.extend(_SEND_GROUPS[grp])
        if grp in rides_with_next:
            return None
        members = tuple(held)
        del held[:]
        res = _send_start([_grad_blocks(name, l, G) for name, l in members], True, f"send_{grp[0]}{grp[1]}")
        pending.append((members, res))
        return res[4][:1, :1]

    W = {n: P[n] for n in _REPL}
    W["ab_norm"] = P["ab_norm"] + zero
    W["final_norm"] = P["final_norm"].reshape(1, D)
    W["a_gate_x_w"], W["a_gate_a_w"], W["b_group_w"] = P["a_gate_x_w"][0], P["a_gate_a_w"][0], P["b_group_w"][0]
    loss, grad_x, G = _local_step(x[0], mem[0], loss_target[0], W, fetch, send)

    Gs = dict(G)
    Gs["c_b_pw1"] = _unpair_blocks(G["c_b_pw1"], _CW_C)
    Gs["f_dw_w"] = jnp.stack([G["f_dw_w0"], G["f_dw_w1"]])
    Gs["a_conv_w"], Gs["c_dw_w"] = G["a_conv_w"][None], G["c_dw_w"][None]
    for n in ("xa_norm", "xa_mem_norm", "f_norm", "f_dw_b"):
        Gs[n] = jnp.concatenate([G[f"{n}0"], G[f"{n}1"]], axis=0)
    for n in ("a_gate_x_w", "a_gate_a_w", "b_group_w"):
        Gs[n] = G[n][None]
    repl_flat = jnp.concatenate([Gs[n].reshape(-1) for n in _REPL] + [loss.reshape(1)])
    repl_rows = jnp.pad(repl_flat, (0, N_DEV * _REPL_ROWS * LANE - _N_REPL - 1)).reshape(N_DEV, _REPL_ROWS, LANE)
    ss_rows = jnp.concatenate([_to_dest_major(Gs[n], s) for n, s in _SMALL_SHARDED.items()], axis=1)
    ss_rows = ss_rows.reshape(N_DEV, _SS_ROWS, LANE)
    small_pack = jnp.concatenate(
        [repl_rows, ss_rows, jnp.zeros((N_DEV, _SMALL_ROWS - _REPL_ROWS - _SS_ROWS, LANE), F32)], axis=1)
    last = _send_start([small_pack], True, "send_small")
    pending.append(((("small", 0),), last))

    def arrived(some, after, name):
        members = [m for mem_, _ in some for m in mem_]
        cat = [[a for _, res in some for a in res[i]] for i in range(4)]
        srcs, lands, _ = _send_wait(cat[0], cat[1], cat[2], cat[3], after, True, name)
        return dict(zip(members, lands))

    out_g, out_d, out_m, out_v = {}, {}, {}, {}
    chain = [None]

    def update(name, landed):
        layers, r, c = _BIG[name]
        w2, m2, v2 = [t[name].reshape(layers * r, c) for t in (P, M, V)]
        res = None
        for l in range(layers):
            res = _sum_adamw(landed[(name, l)], w2, m2, v2, f"adamw_{name}{l}", layer=l, prev=res,
                             after=chain[0] if l == 0 else None)
        chain[0] = res[1]
        out_g[name], out_d[name], out_m[name], out_v[name] = [t.reshape(P[name].shape) for t in res]

    landed = arrived(pending[:-2], grad_x, "send_wait_early")
    for name in _BIG:
        if name != "ab_w_in":
            update(name, landed)
    landed = arrived(pending[-2:], out_v["f_w_down"], "send_wait_late")
    update("ab_w_in", landed)

    small_sum = _sum8(landed[("small", 0)], "sum_small")
    (repl_all,) = _all_gather([small_sum[:_REPL_ROWS]], "gather_small_grads")
    loss = repl_all.reshape(-1)[_N_REPL]
    for out, got in zip((out_g, out_d, out_m, out_v),
                        _adamw_small(repl_all.reshape(N_DEV * _REPL_ROWS, LANE), small_sum, P, M, V)):
        out.update(got)

    return (loss, grad_x[None], *[out_g[n] for n in _NAMES], *[out_d[n] for n in _NAMES],
            *[out_m[n] for n in _NAMES], *[out_v[n] for n in _NAMES])


_NAMES = ("ab_norm", "ab_w_in", "a_conv_w", "a_conv_b", "a_gate_x_w", "a_gate_x_b", "a_gate_a_w", "a_gate_a_b",
          "a_lambda", "b_group_w", "b_group_b", "b_scale", "ab_w_out", "c_norm", "c_w_pw1", "c_b_pw1", "c_dw_w",
          "c_dw_b", "c_ln_g", "c_ln_b", "c_w_pw2", "c_b_pw2", "xa_norm", "xa_mem_norm", "xa_wq", "xa_wk", "xa_wv",
          "xa_wo", "f_norm", "f_w_up", "f_dw_w", "f_dw_b", "f_w_down", "final_norm")
```

```python
import functools

import jax
import jax.numpy as jnp
from jax import lax
from jax.experimental import pallas as pl
from jax.experimental.pallas import tpu as pltpu

F32, BF16 = jnp.float32, jnp.bfloat16
SDS = jax.ShapeDtypeStruct
MESH = pl.DeviceIdType.MESH

N_DEV = 8
D = 1024
N_MEM = 256
XA_HEADS, XA_HD = 4, 256
HD_A = 128
CONV_A, CONV_C, CONV_F = 4, 31, 3
C_RG = 8.0
POOL_WINDOWS = (2, 4, 8, 16)
D_FF = 3 * D
EPS = 1e-6
ADAM_LR, ADAM_B1, ADAM_B2, ADAM_EPS, ADAM_WD, ADAM_STEP = 0.001, 0.9, 0.999, 1e-08, 0.01, 10

LANE = 128
SUB = 8
VMEM_LIMIT = 56 * 1024 * 1024
R_SEQ = 1024
R_POOL = 2048
R_RGLRU = 2048
R_FFN = 2048
TM_ROW = 1024


def _cp(n_axes):
    return pltpu.CompilerParams(dimension_semantics=("arbitrary",) * n_axes, vmem_limit_bytes=VMEM_LIMIT)


def _tile(n, pref):
    if n <= pref:
        return n
    best = None
    for t in range(LANE, pref + 1, LANE):
        if n % t == 0:
            best = t
    assert best is not None, (n, pref)
    return best


def _perm2(n):
    return (n % 2) * 4 + n // 2


_NN = (((1,), (0,)), ((), ()))
_NT = (((1,), (1,)), ((), ()))
_TN = (((0,), (0,)), ((), ()))


def _mm_call(name, grid, ab, ab_specs, dims, acc_shape, extras, outs, finish, from_ref=False):
    nk = grid[2]
    n_ab, n_ex, n_out = len(ab), len(extras), len(outs)
    use_acc = nk > 1 or from_ref

    def product(refs):
        r = lax.dot_general(refs[0][...], refs[1][...], dims, preferred_element_type=F32)
        for i in range(1, n_ab):
            r = r + lax.dot_general(refs[2 * i][...], refs[2 * i + 1][...], dims, preferred_element_type=F32)
        return r

    def body(*refs):
        rest = refs[2 * n_ab:]
        ex_refs, o_refs = rest[:n_ex], rest[n_ex:n_ex + n_out]
        first_rows = pl.program_id(0) == 0
        if not use_acc:
            finish(product(refs), ex_refs, o_refs, first_rows)
            return
        acc = rest[n_ex + n_out]
        if nk == 1:
            acc[...] = product(refs)
            finish(acc, ex_refs, o_refs, first_rows)
            return
        k = pl.program_id(2)

        @pl.when(k == 0)
        def _():
            acc[...] = jnp.zeros_like(acc)

        acc[...] += product(refs)

        @pl.when(k == nk - 1)
        def _():
            finish(acc if from_ref else acc[...], ex_refs, o_refs, first_rows)

    res = pl.pallas_call(
        body, out_shape=[o for o, _ in outs], grid=grid,
        in_specs=list(ab_specs) + [s for _, s in extras], out_specs=[s for _, s in outs],
        scratch_shapes=[pltpu.VMEM(acc_shape, F32)] if use_acc else [], name=name, compiler_params=_cp(3),
    )(*[t for pair in ab for t in pair], *[e for e, _ in extras])
    return res[0] if n_out == 1 else res


def _finish_sum(r, ex_refs, o_refs, first_rows):
    del first_rows
    for e in ex_refs:
        r = r + e[...]
    o_refs[0][...] = r.astype(o_refs[0].dtype)


def _finish_sum_norm(r, ex_refs, o_refs, first_rows):
    del first_rows
    for e in ex_refs[:-1]:
        r = r + e[...]
    o_refs[0][...] = r
    o_refs[1][...] = ((r * lax.rsqrt(jnp.mean(r * r, axis=-1, keepdims=True) + EPS)) * ex_refs[-1][...]).astype(BF16)


_EPI_ROWS = 16


def _finish_rms_bwd(r_ref, ex_refs, o_refs, first_rows):
    x_ref, g_ref, dres_ref = ex_refs
    dx_ref, dxb_ref, dg_ref = o_refs

    @pl.when(first_rows)
    def _():
        dg_ref[...] = jnp.zeros_like(dg_ref)

    gv = g_ref[...]
    inv_d = 1.0 / r_ref.shape[1]

    def step(i, dg_acc):
        groups = [pl.ds(pl.multiple_of(i * (2 * _EPI_ROWS) + u * _EPI_ROWS, _EPI_ROWS), _EPI_ROWS) for u in range(2)]
        sums = []
        for rows in groups:
            r, xf = r_ref[rows, :], x_ref[rows, :]
            sums.append((jnp.sum(xf * xf, axis=-1, keepdims=True), jnp.sum((r * gv) * xf, axis=-1, keepdims=True)))
        for rows, (sxx, sax) in zip(groups, sums):
            r, xf = r_ref[rows, :], x_ref[rows, :]
            rs = lax.rsqrt(sxx * inv_d + EPS)
            dg_acc = dg_acc + _psum8(r * (xf * rs))
            dx = rs * (r * gv) - xf * (rs * rs * (sax * rs * inv_d)) + dres_ref[rows, :]
            dx_ref[rows, :] = dx
            dxb_ref[rows, :] = dx.astype(BF16)
        return dg_acc

    dg_acc = lax.fori_loop(0, r_ref.shape[0] // (2 * _EPI_ROWS), step, jnp.zeros((SUB, r_ref.shape[1]), F32))
    dg_ref[...] += jnp.sum(dg_acc, axis=0, keepdims=True)


def _rms_bwd_io(M, tm, x, g, dres):
    rows = pl.BlockSpec((tm, D), lambda m, n, k: (m, 0))
    vec = pl.BlockSpec((1, D), lambda m, n, k: (0, 0))
    return ([(x, rows), (g, vec), (dres, rows)],
            [(SDS((M, D), F32), rows), (SDS((M, D), BF16), rows), (SDS((1, D), F32), vec)])


_K_WHOLE = 3072


def _mm_nn(a, b, *, out_dtype, name, bias=None, add=None, norm=None):
    M, K = a.shape
    tk = K if K <= _K_WHOLE else _tile(K, 1024)
    if K <= 1024 and norm is None:
        tm = _tile(M, 2048 if add is None and out_dtype == BF16 else 1024)
    else:
        tm = _tile(M, 512)
    if b.ndim == 3:
        nb, _, bw = b.shape
        N, tn, nn = nb * bw, bw, nb
        b_spec = pl.BlockSpec((None, tk, bw), lambda m, n, k: (_perm2(n), k, 0))
    else:
        N = b.shape[1]
        tn = _tile(N, 1024)
        nn = N // tn
        b_spec = pl.BlockSpec((tk, tn), lambda m, n, k: (k, n))
    tile = pl.BlockSpec((tm, tn), lambda m, n, k: (m, n))
    vec = pl.BlockSpec((1, tn), lambda m, n, k: (0, n))
    extras = ([] if bias is None else [(bias, vec)]) + ([] if add is None else [(add, tile)])
    outs, finish = [(SDS((M, N), out_dtype), tile)], _finish_sum
    if norm is not None:
        assert tn == N == D and out_dtype == F32
        extras.append((norm, vec))
        outs, finish = outs + [(SDS((M, N), BF16), tile)], _finish_sum_norm
    return _mm_call(name, (M // tm, nn, K // tk), [(a, b)], [pl.BlockSpec((tm, tk), lambda m, n, k: (m, k)), b_spec],
                    _NN, (tm, tn), extras, outs, finish)


def _mm_nt(a, b, *, out_dtype, name, add=None, rms=None):
    M, N = a.shape
    if b.ndim == 3:
        nb, Ko, bw = b.shape
        tm = _tile(M, 1024)
        tn, tk, nk = _tile(Ko, 1024), bw, nb
        b_spec = pl.BlockSpec((None, tn, bw), lambda m, n, k: (_perm2(k), n, 0))
    else:
        Ko = b.shape[0]
        tk = N if N <= _K_WHOLE else _tile(N, 1024)
        if N <= 1024 and rms is None:
            tm = _tile(M, 2048 if add is None and out_dtype == BF16 else 1024)
        else:
            tm = _tile(M, 512)
        tn = _tile(Ko, 1024)
        nk = N // tk
        b_spec = pl.BlockSpec((tn, tk), lambda m, n, k: (n, k))
    tile = pl.BlockSpec((tm, tn), lambda m, n, k: (m, n))
    extras = [] if add is None else [(add, tile)]
    outs, finish = [(SDS((M, Ko), out_dtype), tile)], _finish_sum
    if rms is not None:
        assert tn == Ko == D and add is None
        (extras, outs), finish = _rms_bwd_io(M, tm, *rms), _finish_rms_bwd
    return _mm_call(name, (M // tm, Ko // tn, nk), [(a, b)], [pl.BlockSpec((tm, tk), lambda m, n, k: (m, k)), b_spec],
                    _NT, (tm, tn), extras, outs, finish, from_ref=rms is not None)


def _mm_nt_cols(parts, b, *, name, rms):
    M = parts[0].shape[0]
    tm = _tile(M, 512)
    specs, off = [], 0
    for p in parts:
        w = p.shape[1]
        assert off % w == 0
        specs.append(pl.BlockSpec((tm, w), lambda m, n, k: (m, 0)))
        specs.append(pl.BlockSpec((D, w), functools.partial(lambda m, n, k, o: (0, o), o=off // w)))
        off += w
    extras, outs = _rms_bwd_io(M, tm, *rms)
    return _mm_call(name, (M // tm, 1, 1), [(p, b) for p in parts], specs, _NT, (tm, D), extras, outs, _finish_rms_bwd,
                    from_ref=True)


def _mm_tn(a, b, *, out_dtype, name, blocks=None):
    S, Ka = a.shape
    Nb = b.shape[1]
    tm = _tile(Ka, 1024)
    if blocks is not None:
        bw = blocks
        tn, nn = bw, Nb // bw
        out = (SDS((nn, Ka, bw), out_dtype), pl.BlockSpec((None, tm, bw), lambda m, n, k: (_perm2(n), m, 0)))
    else:
        tn = _tile(Nb, 1024)
        nn = Nb // tn
        out = (SDS((Ka, Nb), out_dtype), pl.BlockSpec((tm, tn), lambda m, n, k: (m, n)))
    steps = (Ka // tm) * nn
    tk = _tile(S, 4096 if steps >= 4 else 2048 if steps >= 2 else 1024)
    return _mm_call(name, (Ka // tm, nn, S // tk), [(a, b)],
                    [pl.BlockSpec((tk, tm), lambda m, n, k: (k, m)), pl.BlockSpec((tk, tn), lambda m, n, k: (k, n))],
                    _TN, (tm, tn), [], [out], _finish_sum)


def _row(tm, c):
    return pl.BlockSpec((tm, c), lambda i: (i, 0))


def _full(shape):
    nd = len(shape)
    return pl.BlockSpec(shape, lambda i: (0,) * nd)


def _rms_fwd(x, g, name):
    S = x.shape[0]
    tm = min(S, TM_ROW)

    def body(x_ref, g_ref, o_ref):
        xf = x_ref[...]
        r = lax.rsqrt(jnp.mean(xf * xf, axis=-1, keepdims=True) + EPS)
        o_ref[...] = ((xf * r) * g_ref[...]).astype(BF16)

    return pl.pallas_call(body, out_shape=SDS((S, D), BF16), grid=(S // tm,), in_specs=[_row(tm, D), _full((1, D))],
                          out_specs=_row(tm, D), name=name, compiler_params=_cp(1))(x, g)


def _rms_bwd(x, g, dn, dres, name):
    S = x.shape[0]
    tm = min(S, TM_ROW)
    want_dx = dres is not None

    def body(x_ref, g_ref, dn_ref, *rest):
        i = pl.program_id(0)
        dg_ref = rest[-1]

        @pl.when(i == 0)
        def _():
            dg_ref[...] = jnp.zeros_like(dg_ref)

        xf = x_ref[...]
        r = lax.rsqrt(jnp.mean(xf * xf, axis=-1, keepdims=True) + EPS)
        y = xf * r
        dn_v = dn_ref[...]
        dg_ref[...] += jnp.sum(dn_v * y, axis=0, keepdims=True)
        if want_dx:
            dres_ref, dx_ref, dxb_ref = rest[0], rest[1], rest[2]
            dy = dn_v * g_ref[...]
            dx = r * (dy - y * jnp.mean(dy * y, axis=-1, keepdims=True)) + dres_ref[...]
            dx_ref[...] = dx
            dxb_ref[...] = dx.astype(BF16)

    ins = [x, g, dn] + ([dres] if want_dx else [])
    in_specs = [_row(tm, D), _full((1, D)), _row(tm, D)] + ([_row(tm, D)] if want_dx else [])
    outs = ([SDS((S, D), F32), SDS((S, D), BF16)] if want_dx else []) + [SDS((1, D), F32)]
    out_specs = ([_row(tm, D), _row(tm, D)] if want_dx else []) + [_full((1, D))]
    return pl.pallas_call(body, out_shape=outs, grid=(S // tm,), in_specs=in_specs, out_specs=out_specs, name=name,
                          compiler_params=_cp(1))(*ins)


def _loss_head(x, g, tgt):
    S = x.shape[0]
    tm = min(S, TM_ROW)

    def body(x_ref, g_ref, t_ref, loss_ref, dx_ref, dxb_ref, dg_ref):
        i = pl.program_id(0)

        @pl.when(i == 0)
        def _():
            loss_ref[...] = jnp.zeros_like(loss_ref)
            dg_ref[...] = jnp.zeros_like(dg_ref)

        xf = x_ref[...]
        r = lax.rsqrt(jnp.mean(xf * xf, axis=-1, keepdims=True) + EPS)
        y = xf * r
        gv = g_ref[...]
        err = y * gv - t_ref[...]
        per_row = jnp.mean(err * err, axis=-1, keepdims=True)
        loss_ref[...] += 0.5 * jnp.sum(per_row, axis=0, keepdims=True)
        dn_v = err * (1.0 / D)
        dg_ref[...] += jnp.sum(dn_v * y, axis=0, keepdims=True)
        dy = dn_v * gv
        dx = r * (dy - y * jnp.mean(dy * y, axis=-1, keepdims=True))
        dx_ref[...] = dx
        dxb_ref[...] = dx.astype(BF16)

    return pl.pallas_call(
        body, out_shape=[SDS((1, 1), F32), SDS((S, D), F32), SDS((S, D), BF16), SDS((1, D), F32)], grid=(S // tm,),
        in_specs=[_row(tm, D), _full((1, D)), _row(tm, D)],
        out_specs=[_full((1, 1)), _row(tm, D), _row(tm, D), _full((1, D))], name="loss_head", compiler_params=_cp(1),
    )(x, g, tgt)


def _softmax_rows(s):
    m = jnp.max(s, axis=-1, keepdims=True)
    e = jnp.exp(s - m)
    return e / jnp.sum(e, axis=-1, keepdims=True)


def _attn_fwd(q, k, v, name):
    S = q.shape[0]
    tm = min(S, TM_ROW)
    scale = XA_HD ** -0.5

    def body(q_ref, k_ref, v_ref, o_ref):
        for h in range(XA_HEADS):
            sl = slice(h * XA_HD, (h + 1) * XA_HD)
            s = lax.dot_general(q_ref[:, sl], k_ref[:, sl], _NT, preferred_element_type=F32) * scale
            p = _softmax_rows(s)
            o_ref[:, sl] = lax.dot_general(p.astype(BF16), v_ref[:, sl], _NN, preferred_element_type=F32).astype(BF16)

    return pl.pallas_call(body, out_shape=SDS((S, D), BF16), grid=(S // tm,),
                          in_specs=[_row(tm, D), _full((N_MEM, D)), _full((N_MEM, D))], out_specs=_row(tm, D),
                          name=name, compiler_params=_cp(1))(q, k, v)


def _attn_bwd(q, k, v, do, name):
    S = q.shape[0]
    tm = min(S, TM_ROW)
    scale = XA_HD ** -0.5

    def body(q_ref, k_ref, v_ref, do_ref, dq_ref, dk_ref, dv_ref):
        i = pl.program_id(0)

        @pl.when(i == 0)
        def _():
            dk_ref[...] = jnp.zeros_like(dk_ref)
            dv_ref[...] = jnp.zeros_like(dv_ref)

        for h in range(XA_HEADS):
            sl = slice(h * XA_HD, (h + 1) * XA_HD)
            qh, kh, vh, doh = q_ref[:, sl], k_ref[:, sl], v_ref[:, sl], do_ref[:, sl]
            s = lax.dot_general(qh, kh, _NT, preferred_element_type=F32) * scale
            p = _softmax_rows(s)
            pb = p.astype(BF16)
            dv_ref[:, sl] += lax.dot_general(pb, doh, _TN, preferred_element_type=F32)
            dp = lax.dot_general(doh, vh, _NT, preferred_element_type=F32)
            ds = (p * (dp - jnp.sum(dp * p, axis=-1, keepdims=True)) * scale).astype(BF16)
            dq_ref[:, sl] = lax.dot_general(ds, kh, _NN, preferred_element_type=F32).astype(BF16)
            dk_ref[:, sl] += lax.dot_general(ds, qh, _TN, preferred_element_type=F32)

    return pl.pallas_call(
        body, out_shape=[SDS((S, D), BF16), SDS((N_MEM, D), F32), SDS((N_MEM, D), F32)], grid=(S // tm,),
        in_specs=[_row(tm, D), _full((N_MEM, D)), _full((N_MEM, D)), _row(tm, D)],
        out_specs=[_row(tm, D), _full((N_MEM, D)), _full((N_MEM, D))], name=name, compiler_params=_cp(1),
    )(q, k, v, do)


def _sigmoid(x):
    return 1.0 / (1.0 + jnp.exp(-x))


def _ln_silu_fwd(cv, g, b):
    S = cv.shape[0]
    tm = min(S, TM_ROW)

    def body(x_ref, g_ref, b_ref, o_ref):
        xf = x_ref[...]
        mu = jnp.mean(xf, axis=-1, keepdims=True)
        xc = xf - mu
        rstd = lax.rsqrt(jnp.mean(xc * xc, axis=-1, keepdims=True) + EPS)
        ln = (xc * rstd) * g_ref[...] + b_ref[...]
        o_ref[...] = (ln * _sigmoid(ln)).astype(BF16)

    return pl.pallas_call(body, out_shape=SDS((S, D), BF16), grid=(S // tm,),
                          in_specs=[_row(tm, D), _full((1, D)), _full((1, D))], out_specs=_row(tm, D),
                          name="ln_silu_fwd", compiler_params=_cp(1))(cv, g, b)


def _ln_silu_bwd(ds, cv, g, b, dx):
    S = cv.shape[0]
    tm = min(S, TM_ROW)

    def body(ds_ref, x_ref, g_ref, b_ref, dx_ref, dcv_ref, dg_ref, db_ref, db2_ref):
        i = pl.program_id(0)

        @pl.when(i == 0)
        def _():
            dg_ref[...] = jnp.zeros_like(dg_ref)
            db_ref[...] = jnp.zeros_like(db_ref)
            db2_ref[...] = jnp.zeros_like(db2_ref)

        xf = x_ref[...]
        mu = jnp.mean(xf, axis=-1, keepdims=True)
        xc = xf - mu
        rstd = lax.rsqrt(jnp.mean(xc * xc, axis=-1, keepdims=True) + EPS)
        xhat = xc * rstd
        gv = g_ref[...]
        ln = xhat * gv + b_ref[...]
        sg = _sigmoid(ln)
        dln = ds_ref[...].astype(F32) * (sg + ln * sg * (1.0 - sg))
        dg_ref[...] += jnp.sum(dln * xhat, axis=0, keepdims=True)
        db_ref[...] += jnp.sum(dln, axis=0, keepdims=True)
        db2_ref[...] += jnp.sum(dx_ref[...], axis=0, keepdims=True)
        dxh = dln * gv
        dcv_ref[...] = rstd * (dxh - jnp.mean(dxh, axis=-1, keepdims=True)
                               - xhat * jnp.mean(dxh * xhat, axis=-1, keepdims=True))

    return pl.pallas_call(
        body, out_shape=[SDS((S, D), F32), SDS((1, D), F32), SDS((1, D), F32), SDS((1, D), F32)], grid=(S // tm,),
        in_specs=[_row(tm, D), _row(tm, D), _full((1, D)), _full((1, D)), _row(tm, D)],
        out_specs=[_row(tm, D), _full((1, D)), _full((1, D)), _full((1, D))], name="ln_silu_bwd",
        compiler_params=_cp(1),
    )(ds, cv, g, b, dx)


_GELU_C, _GELU_K = 0.7978845608028654, 0.044715


def _gelu(x, with_grad=False):
    x2 = x * x
    t = jnp.tanh(_GELU_C * (x + _GELU_K * x * x2))
    gel = 0.5 * x * (1.0 + t)
    if not with_grad:
        return gel
    return gel, 0.5 * (1.0 + t) + 0.5 * x * (1.0 - t * t) * (_GELU_C * (1.0 + 3.0 * _GELU_K * x2))


def _expm1(x):
    poly = x * (1.0 + x * (0.5 + x * (1.0 / 6.0 + x * (1.0 / 24.0 + x * (1.0 / 120.0)))))
    return jnp.where(jnp.abs(x) < 0.05, poly, jnp.exp(x) - 1.0)


def _softplus(x):
    return jnp.maximum(x, 0.0) + jnp.log1p(jnp.exp(-jnp.abs(x)))


_SCAN_UNROLL = 8
_RB = 32
_HB = 16


def _sub_blocks(n_rows, n_lanes, fn):
    def step(idx, c):
        r0 = pl.multiple_of(idx * _RB, _RB)
        for lt in range(n_lanes // LANE):
            fn(r0, lt)
        return c

    lax.fori_loop(0, n_rows // _RB, step, 0)


def _lanes(lt):
    return pl.ds(lt * LANE, LANE)


def _psum8(x):
    parts = [x[i * SUB:(i + 1) * SUB] for i in range(x.shape[0] // SUB)]
    return functools.reduce(lambda p, q: p + q, parts)


def _scan_fwd(a_s, b_s, out_ref, carry_ref, n_groups):
    row = lax.broadcasted_iota(jnp.int32, (SUB, LANE), 0)
    U = _SCAN_UNROLL

    def step(gi, carry):
        base = gi * (SUB * U)
        parts = []
        for u in range(U):
            i = pl.multiple_of(base + u * SUB, SUB)
            a8, b8 = a_s[pl.ds(i, SUB), :], b_s[pl.ds(i, SUB), :]
            for s in (1, 2, 4):
                a_sh = jnp.where(row >= s, pltpu.roll(a8, s, 0), 1.0)
                b_sh = jnp.where(row >= s, pltpu.roll(b8, s, 0), 0.0)
                b8 = a8 * b_sh + b8
                a8 = a8 * a_sh
            parts.append((i, a8, b8))
        for i, a8, b8 in parts:
            h8 = a8 * carry + b8
            out_ref[pl.ds(i, SUB), :] = h8
            carry = jnp.broadcast_to(h8[SUB - 1:SUB, :], (SUB, LANE))
        return carry

    carry_ref[...] = lax.fori_loop(0, n_groups // U, step, carry_ref[...])


def _scan_bwd(a_s, b_s, out_ref, carry_ref, n_groups):
    row = lax.broadcasted_iota(jnp.int32, (SUB, LANE), 0)
    U = _SCAN_UNROLL

    def step(gi, carry):
        base = (n_groups // U - 1 - gi) * (SUB * U)
        parts = []
        for u in reversed(range(U)):
            i = pl.multiple_of(base + u * SUB, SUB)
            a8, b8 = a_s[pl.ds(i, SUB), :], b_s[pl.ds(i, SUB), :]
            for s in (1, 2, 4):
                a_sh = jnp.where(row < SUB - s, pltpu.roll(a8, SUB - s, 0), 1.0)
                b_sh = jnp.where(row < SUB - s, pltpu.roll(b8, SUB - s, 0), 0.0)
                b8 = a8 * b_sh + b8
                a8 = a8 * a_sh
            parts.append((i, a8, b8))
        for i, a8, b8 in parts:
            h8 = a8 * carry + b8
            out_ref[pl.ds(i, SUB), :] = h8
            carry = jnp.broadcast_to(h8[0:1, :], (SUB, LANE))
        return carry

    carry_ref[...] = lax.fori_loop(0, n_groups // U, step, carry_ref[...])


def _rglru_pre(xr, wgx_ref, bgx_ref, wga_ref, bga_ref, lam_ref):
    xrb = xr.astype(BF16)
    wgx, wga = wgx_ref[0].astype(BF16), wga_ref[0].astype(BF16)
    gx = _sigmoid(lax.dot_general(xrb, wgx, _NN, preferred_element_type=F32) + bgx_ref[...])
    ga = _sigmoid(lax.dot_general(xrb, wga, _NN, preferred_element_type=F32) + bga_ref[...])
    sp = _softplus(-lam_ref[...])
    log_a = -C_RG * ga * sp
    a = jnp.exp(log_a)
    mult = jnp.sqrt(-_expm1(2.0 * log_a))
    return gx, ga, sp, a, mult, xrb, wgx, wga


def _a_specs():
    vec = pl.BlockSpec((1, HD_A), lambda c, j: (0, c))
    mat = pl.BlockSpec((1, HD_A, HD_A), lambda c, j: (c, 0, 0))
    return [pl.BlockSpec((CONV_A, HD_A), lambda c, j: (0, c)), vec, mat, vec, mat, vec, vec]


def _a_fwd(zp, conv_w, conv_b, wgx, bgx, wga, bga, lam):
    S = zp.shape[0]
    R, nt = R_RGLRU, D // HD_A
    H = SUB

    def body(zg_ref, zr_ref, cw_ref, cb_ref, wgx_ref, bgx_ref, wga_ref, bga_ref, lam_ref, ya_ref, h_ref,
             ext, a_s, b_s, hc):
        j = pl.program_id(1)

        @pl.when(j == 0)
        def _():
            ext[0:H, :] = jnp.zeros((H, HD_A), F32)
            hc[...] = jnp.zeros_like(hc)

        ext[H:H + R, :] = zr_ref[...].astype(F32)
        xr = cb_ref[...]
        for k in range(CONV_A):
            xr = xr + cw_ref[k:k + 1, :] * ext[pl.ds(H - (CONV_A - 1 - k), R), :]
        gx, _, _, a, mult, _, _, _ = _rglru_pre(xr, wgx_ref, bgx_ref, wga_ref, bga_ref, lam_ref)
        a_s[...] = a
        b_s[...] = mult * (gx * xr)
        _scan_fwd(a_s, b_s, h_ref, hc, R // SUB)
        ya_ref[...] = (_gelu(zg_ref[...].astype(F32)) * h_ref[...]).astype(BF16)
        ext[0:H, :] = ext[R:R + H, :]

    return pl.pallas_call(
        body, out_shape=[SDS((S, D + D // 2), BF16), SDS((S, D), F32)], grid=(nt, S // R),
        in_specs=[pl.BlockSpec((R, HD_A), lambda c, j: (j, c)), pl.BlockSpec((R, HD_A), lambda c, j: (j, nt + c))]
        + _a_specs(),
        out_specs=[pl.BlockSpec((R, HD_A), lambda c, j: (j, c)), pl.BlockSpec((R, HD_A), lambda c, j: (j, c))],
        scratch_shapes=[pltpu.VMEM((H + R, HD_A), F32), pltpu.VMEM((R, HD_A), F32), pltpu.VMEM((R, HD_A), F32),
                        pltpu.VMEM((SUB, HD_A), F32)],
        name="rglru_fwd", compiler_params=_cp(2),
    )(zp, zp, conv_w, conv_b, wgx, bgx, wga, bga, lam)


def _a_bwd(dyab, zp, h, conv_w, conv_b, wgx, bgx, wga, bga, lam):
    S = zp.shape[0]
    R, nt, nch = R_RGLRU, D // HD_A, S // R_RGLRU
    H = SUB

    def rows(c, j):
        return (nch - 1 - j, c)

    def rows_rec(c, j):
        return (nch - 1 - j, nt + c)

    def halo(c, j):
        return (jnp.maximum((nch - 1 - j) * (R // H) - 1, 0), c)

    def halo_z(c, j):
        return (jnp.maximum((nch - 1 - j) * (R // _HB) - 1, 0), nt + c)

    def body(dy_ref, zg_ref, zr_ref, zh_ref, h_ref, hh_ref, cw_ref, cb_ref, wgx_ref, bgx_ref, wga_ref, bga_ref,
             lam_ref, dzg_ref, dzr_ref, dcw_ref, dcb_ref, dwgx_ref, dbgx_ref, dwga_ref, dbga_ref, dlam_ref,
             ext_z, ext_h, ext_mu, ext_d, a_s, b_s, muc):
        j = pl.program_id(1)
        first_chunk = (nch - 1 - j) == 0

        @pl.when(j == 0)
        def _():
            ext_mu[R:R + H, :] = jnp.zeros((H, HD_A), F32)
            ext_d[R:R + H, :] = jnp.zeros((H, HD_A), F32)
            muc[...] = jnp.zeros_like(muc)
            for r in (dcw_ref, dcb_ref, dwgx_ref, dbgx_ref, dwga_ref, dbga_ref, dlam_ref):
                r[...] = jnp.zeros_like(r)

        zg = zg_ref[...].astype(F32)
        ext_z[0:H, :] = jnp.where(first_chunk, 0.0, zh_ref[_HB - H:_HB, :].astype(F32))
        ext_z[H:H + R, :] = zr_ref[...].astype(F32)
        ext_h[0:H, :] = jnp.where(first_chunk, 0.0, hh_ref[...])
        ext_h[H:H + R, :] = h_ref[...]
        xr = cb_ref[...]
        for k in range(CONV_A):
            xr = xr + cw_ref[k:k + 1, :] * ext_z[pl.ds(H - (CONV_A - 1 - k), R), :]
        gx, ga, sp, a, mult, xrb, wgxb, wgab = _rglru_pre(xr, wgx_ref, bgx_ref, wga_ref, bga_ref, lam_ref)
        gel, dgel = _gelu(zg, with_grad=True)
        dy = dy_ref[...].astype(F32)
        dh = dy * gel
        dzg_ref[...] = (dy * h_ref[...] * dgel).astype(BF16)
        a_s[...] = a
        b_s[...] = a * dh
        _scan_bwd(a_s, b_s, ext_mu, muc, R // SUB)
        lam_t = dh + ext_mu[pl.ds(1, R), :]
        ext_mu[R:R + H, :] = ext_mu[0:H, :]
        da = lam_t * ext_h[pl.ds(H - 1, R), :]
        gxr = gx * xr
        dlog_a = da * a - (lam_t * gxr) * (a * a) / mult
        dgx = lam_t * mult * xr
        dxr = lam_t * mult * gx
        lam_v = lam_ref[...]
        dlam_ref[...] += jnp.sum(dlog_a * ga, axis=0, keepdims=True) * (C_RG * _sigmoid(-lam_v))
        dpa = (dlog_a * (-C_RG * sp)) * ga * (1.0 - ga)
        dpx = dgx * gx * (1.0 - gx)
        dbga_ref[...] += jnp.sum(dpa, axis=0, keepdims=True)
        dbgx_ref[...] += jnp.sum(dpx, axis=0, keepdims=True)
        dpab, dpxb = dpa.astype(BF16), dpx.astype(BF16)
        dwga_ref[0] += lax.dot_general(xrb, dpab, _TN, preferred_element_type=F32)
        dwgx_ref[0] += lax.dot_general(xrb, dpxb, _TN, preferred_element_type=F32)
        dxr = (dxr + lax.dot_general(dpab, wgab, _NT, preferred_element_type=F32)
               + lax.dot_general(dpxb, wgxb, _NT, preferred_element_type=F32))
        dcb_ref[...] += jnp.sum(dxr, axis=0, keepdims=True)
        ext_d[0:R, :] = dxr
        dzr = jnp.zeros((R, HD_A), F32)
        for k in range(CONV_A):
            sh = CONV_A - 1 - k
            dcw_ref[k:k + 1, :] += jnp.sum(dxr * ext_z[pl.ds(H - sh, R), :], axis=0, keepdims=True)
            dzr = dzr + cw_ref[k:k + 1, :] * ext_d[pl.ds(sh, R), :]
        dzr_ref[...] = dzr.astype(BF16)
        ext_d[R:R + H, :] = ext_d[0:H, :]

    vec_o = pl.BlockSpec((1, HD_A), lambda c, j: (0, c))
    mat_o = pl.BlockSpec((1, HD_A, HD_A), lambda c, j: (c, 0, 0))
    return pl.pallas_call(
        body,
        out_shape=[SDS((S, D), BF16), SDS((S, D), BF16), SDS((CONV_A, D), F32), SDS((1, D), F32),
                   SDS((nt, HD_A, HD_A), F32), SDS((1, D), F32), SDS((nt, HD_A, HD_A), F32), SDS((1, D), F32),
                   SDS((1, D), F32)],
        grid=(nt, nch),
        in_specs=[pl.BlockSpec((R, HD_A), rows), pl.BlockSpec((R, HD_A), rows), pl.BlockSpec((R, HD_A), rows_rec),
                  pl.BlockSpec((_HB, HD_A), halo_z), pl.BlockSpec((R, HD_A), rows),
                  pl.BlockSpec((H, HD_A), halo)] + _a_specs(),
        out_specs=[pl.BlockSpec((R, HD_A), rows), pl.BlockSpec((R, HD_A), rows),
                   pl.BlockSpec((CONV_A, HD_A), lambda c, j: (0, c)), vec_o, mat_o, vec_o, mat_o, vec_o, vec_o],
        scratch_shapes=[pltpu.VMEM((H + R, HD_A), F32), pltpu.VMEM((H + R, HD_A), F32), pltpu.VMEM((R + H, HD_A), F32),
                        pltpu.VMEM((R + H, HD_A), F32), pltpu.VMEM((R, HD_A), F32), pltpu.VMEM((R, HD_A), F32),
                        pltpu.VMEM((SUB, HD_A), F32)],
        name="rglru_bwd", compiler_params=_cp(2),
    )(dyab, zp, zp, zp, h, h, conv_w, conv_b, wgx, bgx, wga, bga, lam)


_POOL_H = 16
_POOL_T0 = 2 * D // HD_A
_POOL_Y0 = D // HD_A


def _window_sum(lv, n, lo, rows, g, ahead):
    base = 0 if ahead else SUB
    cur, win = lv[0], None
    for i, s in enumerate((1, 2, 4, 8)):
        val = cur[pl.ds(base, n), :] + cur[pl.ds(base + (s if ahead else -s), n), :]
        sel = val[lo:lo + rows]
        win = sel if win is None else jnp.where(g >= i, sel, win)
        if i < 3:
            lv[i + 1][pl.ds(base, n), :] = val
            cur = lv[i + 1]
    return win


def _pool_width(g):
    return jnp.where(g == 0, 2.0, jnp.where(g == 1, 4.0, jnp.where(g == 2, 8.0, 16.0)))


def _b_fwd(zp, yab, wg, bg, sc):
    S = zp.shape[0]
    R, H = min(S, R_POOL), _POOL_H

    def body(z_ref, wg_ref, bg_ref, sc_ref, yab_in, yb_ref, *lv):
        del yab_in
        g, j = pl.program_id(0), pl.program_id(1)

        @pl.when(j == 0)
        def _():
            for r in lv:
                r[0:SUB, :] = jnp.zeros((SUB, HD_A), F32)
            lv[0][SUB:SUB + H, :] = jnp.zeros((H, HD_A), F32)

        u = z_ref[...].astype(F32)
        lv[0][SUB + H:SUB + H + R, :] = u
        t1 = (j * R + 1 + lax.broadcasted_iota(jnp.int32, (R, HD_A), 0)).astype(F32)
        p = _window_sum(lv, H + R, H, R, g, False) / jnp.minimum(t1, _pool_width(g)) - u
        lin = lax.dot_general(p.astype(BF16), wg_ref[0].astype(BF16), _NN, preferred_element_type=F32) + bg_ref[...]
        yb_ref[...] = (lin * sc_ref[...]).astype(BF16)
        lv[0][SUB:SUB + H, :] = lv[0][SUB + R:SUB + R + H, :]

    vec = pl.BlockSpec((1, HD_A), lambda g, j: (0, g))
    return pl.pallas_call(
        body, out_shape=SDS(yab.shape, yab.dtype), grid=(len(POOL_WINDOWS), S // R),
        in_specs=[pl.BlockSpec((R, HD_A), lambda g, j: (j, _POOL_T0 + g)),
                  pl.BlockSpec((1, HD_A, HD_A), lambda g, j: (g, 0, 0)), vec, vec, pl.BlockSpec(memory_space=pl.ANY)],
        out_specs=pl.BlockSpec((R, HD_A), lambda g, j: (j, _POOL_Y0 + g)),
        scratch_shapes=[pltpu.VMEM((SUB + H + R, HD_A), F32)] * 4, input_output_aliases={4: 0},
        name="pool_fwd", compiler_params=_cp(2),
    )(zp, wg, bg, sc, yab)


def _b_bwd(dyab, zp, wg, bg, sc):
    S = zp.shape[0]
    R, H, ng = min(S, R_POOL), _POOL_H, len(POOL_WINDOWS)
    nch = S // R

    def body(dy_ref, z_ref, zh_ref, wg_ref, bg_ref, sc_ref, dz_ref, dwg_ref, dbg_ref, dsc_ref, *scratch):
        lu, lq = scratch[:4], scratch[4:]
        g, j = pl.program_id(0), pl.program_id(1)
        jj = nch - 1 - j

        @pl.when(j == 0)
        def _():
            for r in lu:
                r[0:SUB, :] = jnp.zeros((SUB, HD_A), F32)
            for r in lq:
                r[R + H:R + H + SUB, :] = jnp.zeros((SUB, HD_A), F32)
            lq[0][R:R + H, :] = jnp.zeros((H, HD_A), F32)
            for r in (dwg_ref, dbg_ref, dsc_ref):
                r[...] = jnp.zeros_like(r)

        u = z_ref[...].astype(F32)
        lu[0][SUB:SUB + H, :] = jnp.where(jj == 0, 0.0, zh_ref[...].astype(F32))
        lu[0][SUB + H:SUB + H + R, :] = u
        t1 = (jj * R + 1 + lax.broadcasted_iota(jnp.int32, (R, HD_A), 0)).astype(F32)
        cnt = jnp.minimum(t1, _pool_width(g))
        pb = (_window_sum(lu, H + R, H, R, g, False) / cnt - u).astype(BF16)
        wgb = wg_ref[0].astype(BF16)
        lin = lax.dot_general(pb, wgb, _NN, preferred_element_type=F32) + bg_ref[...]
        dy = dy_ref[...].astype(F32)
        dsc_ref[...] += jnp.sum(dy * lin, axis=0, keepdims=True)
        dlin = dy * sc_ref[...]
        dbg_ref[...] += jnp.sum(dlin, axis=0, keepdims=True)
        dlb = dlin.astype(BF16)
        dwg_ref[0] += lax.dot_general(pb, dlb, _TN, preferred_element_type=F32)
        dp = lax.dot_general(dlb, wgb, _NT, preferred_element_type=F32)
        lq[0][0:R, :] = dp / cnt
        dz_ref[...] = (_window_sum(lq, R + H, 0, R, g, True) - dp).astype(BF16)
        lq[0][R:R + H, :] = lq[0][0:H, :]

    vec = pl.BlockSpec((1, HD_A), lambda g, j: (0, g))
    mat = pl.BlockSpec((1, HD_A, HD_A), lambda g, j: (g, 0, 0))
    return pl.pallas_call(
        body, out_shape=[SDS((S, D // 2), BF16), SDS((ng, HD_A, HD_A), F32), SDS((1, D // 2), F32),
                         SDS((1, D // 2), F32)],
        grid=(ng, nch),
        in_specs=[pl.BlockSpec((R, HD_A), lambda g, j: (nch - 1 - j, _POOL_Y0 + g)),
                  pl.BlockSpec((R, HD_A), lambda g, j: (nch - 1 - j, _POOL_T0 + g)),
                  pl.BlockSpec((H, HD_A), lambda g, j: (jnp.maximum((nch - 1 - j) * (R // H) - 1, 0), _POOL_T0 + g)),
                  mat, vec, vec],
        out_specs=[pl.BlockSpec((R, HD_A), lambda g, j: (nch - 1 - j, g)), mat, vec, vec],
        scratch_shapes=[pltpu.VMEM((SUB + H + R, HD_A), F32)] * 8,
        name="pool_bwd", compiler_params=_cp(2),
    )(dyab, zp, zp, wg, bg, sc)


_CW_F = 768


def _f_fwd(hp, w, b, name):
    S = hp.shape[0]
    R, H, cw = min(S, R_FFN), SUB, _CW_F
    nlt = cw // LANE

    def body(h_ref, w_ref, b_ref, o_ref, gel_ref, ud_ref, ext):
        j = pl.program_id(1)

        @pl.when(j == 0)
        def _():
            ext[:, 0:H, :] = jnp.zeros((nlt, H, LANE), F32)

        def stage(r0, lt):
            ext[lt, pl.ds(pl.multiple_of(r0 + H, SUB), _RB), :] = h_ref[pl.ds(r0, _RB), _lanes(lt)].astype(F32)

        def main(r0, lt):
            ls = _lanes(lt)
            gp = b_ref[:, ls]
            for k in range(CONV_F):
                gp = gp + w_ref[k:k + 1, ls] * ext[lt, pl.ds(r0 + (H - (CONV_F - 1 - k)), _RB), :]
            up = h_ref[pl.ds(r0, _RB), _lanes(lt + nlt)].astype(F32)
            gel, dgel = _gelu(gp, with_grad=True)
            rs = pl.ds(r0, _RB)
            o_ref[rs, ls] = (gel * up).astype(BF16)
            gel_ref[rs, ls] = gel.astype(BF16)
            ud_ref[rs, ls] = (up * dgel).astype(BF16)

        _sub_blocks(R, cw, stage)
        _sub_blocks(R, cw, main)
        ext[:, 0:H, :] = ext[:, R:R + H, :]

    tile = pl.BlockSpec((R, cw), lambda c, j: (j, c))
    return pl.pallas_call(
        body, out_shape=[SDS((S, D_FF), BF16)] * 3, grid=(D_FF // cw, S // R),
        in_specs=[pl.BlockSpec((R, 2 * cw), lambda c, j: (j, c)), pl.BlockSpec((CONV_F, cw), lambda c, j: (0, c)),
                  pl.BlockSpec((1, cw), lambda c, j: (0, c))],
        out_specs=[tile] * 3,
        scratch_shapes=[pltpu.VMEM((nlt, H + R, LANE), F32)], name=name, compiler_params=_cp(2),
    )(hp, w, b)


def _f_bwd(dact, hp, gel, ud, w, name):
    S = hp.shape[0]
    R, H, cw = min(S, R_FFN), SUB, _CW_F
    nch = S // R
    nlt = cw // LANE

    def body(da_ref, h_ref, hh_ref, gel_ref, ud_ref, w_ref, dh_ref, dw_ref, db_ref, ext_g, ext_d, acc):
        j = pl.program_id(1)
        jj = nch - 1 - j

        @pl.when(j == 0)
        def _():
            ext_d[:, R:R + H, :] = jnp.zeros((nlt, H, LANE), F32)
            acc[...] = jnp.zeros_like(acc)

        for lt in range(nlt):
            ext_g[lt, 0:H, :] = jnp.where(jj == 0, 0.0, hh_ref[_HB - H:_HB, lt * LANE:(lt + 1) * LANE].astype(F32))

        def stage(r0, lt):
            ext_g[lt, pl.ds(pl.multiple_of(r0 + H, SUB), _RB), :] = h_ref[pl.ds(r0, _RB), _lanes(lt)].astype(F32)

        def first(r0, lt):
            ls, lu, rs = _lanes(lt), _lanes(lt + nlt), pl.ds(r0, _RB)
            da = da_ref[rs, ls].astype(F32)
            dh_ref[rs, lu] = (da * gel_ref[rs, ls].astype(F32)).astype(BF16)
            dgp = da * ud_ref[rs, ls].astype(F32)
            ext_d[lt, rs, :] = dgp
            acc[CONV_F * SUB:(CONV_F + 1) * SUB, ls] += _psum8(dgp)
            for k in range(CONV_F):
                tap = ext_g[lt, pl.ds(r0 + (H - (CONV_F - 1 - k)), _RB), :]
                acc[k * SUB:(k + 1) * SUB, ls] += _psum8(dgp * tap)

        def second(r0, lt):
            ls = _lanes(lt)
            dhg = w_ref[CONV_F - 1:CONV_F, ls] * ext_d[lt, pl.ds(r0, _RB), :]
            for k in range(CONV_F - 1):
                dhg = dhg + w_ref[k:k + 1, ls] * ext_d[lt, pl.ds(r0 + (CONV_F - 1 - k), _RB), :]
            dh_ref[pl.ds(r0, _RB), ls] = dhg.astype(BF16)

        _sub_blocks(R, cw, stage)
        _sub_blocks(R, cw, first)
        _sub_blocks(R, cw, second)
        ext_d[:, R:R + H, :] = ext_d[:, 0:H, :]

        @pl.when(j == nch - 1)
        def _():
            for k in range(CONV_F):
                dw_ref[k:k + 1, :] = jnp.sum(acc[k * SUB:(k + 1) * SUB, :], axis=0, keepdims=True)
            db_ref[...] = jnp.sum(acc[CONV_F * SUB:(CONV_F + 1) * SUB, :], axis=0, keepdims=True)

    rows = lambda c, j: (nch - 1 - j, c)
    return pl.pallas_call(
        body, out_shape=[SDS((S, 2 * D_FF), BF16), SDS((CONV_F, D_FF), F32), SDS((1, D_FF), F32)],
        grid=(D_FF // cw, nch),
        in_specs=[pl.BlockSpec((R, cw), rows), pl.BlockSpec((R, cw), lambda c, j: (nch - 1 - j, 2 * c)),
                  pl.BlockSpec((_HB, cw), lambda c, j: (jnp.maximum((nch - 1 - j) * (R // _HB) - 1, 0), 2 * c)),
                  pl.BlockSpec((R, cw), rows), pl.BlockSpec((R, cw), rows),
                  pl.BlockSpec((CONV_F, cw), lambda c, j: (0, c))],
        out_specs=[pl.BlockSpec((R, 2 * cw), rows), pl.BlockSpec((CONV_F, cw), lambda c, j: (0, c)),
                   pl.BlockSpec((1, cw), lambda c, j: (0, c))],
        scratch_shapes=[pltpu.VMEM((nlt, H + R, LANE), F32), pltpu.VMEM((nlt, R + H, LANE), F32),
                        pltpu.VMEM(((CONV_F + 1) * SUB, cw), F32)], name=name,
        compiler_params=_cp(2),
    )(dact, hp, hp, gel, ud, w)


_CW_C = 256
_H_C = 32


def _c_fwd(h1p, w, b):
    S = h1p.shape[0]
    R, H, cw = R_SEQ, _H_C, _CW_C
    nlt = cw // LANE

    def body(h_ref, w_ref, b_ref, o_ref, ext):
        j = pl.program_id(1)

        @pl.when(j == 0)
        def _():
            ext[:, 0:H, :] = jnp.zeros((nlt, H, LANE), F32)

        def stage(r0, lt):
            rs = pl.ds(r0, _RB)
            gate = h_ref[rs, _lanes(lt + nlt)].astype(F32)
            ext[lt, pl.ds(pl.multiple_of(r0 + H, SUB), _RB), :] = h_ref[rs, _lanes(lt)].astype(F32) * _sigmoid(gate)

        def main(r0, lt):
            ls = _lanes(lt)
            cv = b_ref[:, ls]
            for k in range(CONV_C):
                cv = cv + w_ref[k:k + 1, ls] * ext[lt, pl.ds(r0 + (H - (CONV_C - 1 - k)), _RB), :]
            o_ref[pl.ds(r0, _RB), ls] = cv

        _sub_blocks(R, cw, stage)
        _sub_blocks(R, cw, main)
        ext[:, 0:H, :] = ext[:, R:R + H, :]

    return pl.pallas_call(
        body, out_shape=SDS((S, D), F32), grid=(D // cw, S // R),
        in_specs=[pl.BlockSpec((R, 2 * cw), lambda c, j: (j, c)), pl.BlockSpec((CONV_C, cw), lambda c, j: (0, c)),
                  pl.BlockSpec((1, cw), lambda c, j: (0, c))],
        out_specs=pl.BlockSpec((R, cw), lambda c, j: (j, c)),
        scratch_shapes=[pltpu.VMEM((nlt, H + R, LANE), F32)], name="conf_conv_fwd", compiler_params=_cp(2),
    )(h1p, w, b)


def _c_bwd(dcv, h1p, w):
    S = h1p.shape[0]
    R, H, cw, nch = R_SEQ, _H_C, _CW_C, S // R_SEQ
    nlt = cw // LANE
    a_b, a_val, a_gate = CONV_C * SUB, (CONV_C + 1) * SUB, (CONV_C + 2) * SUB

    def body(dc_ref, h_ref, hh_ref, w_ref, dh_ref, dw_ref, db_ref, db1_ref, ext_u, ext_d, acc):
        j = pl.program_id(1)
        jj = nch - 1 - j

        @pl.when(j == 0)
        def _():
            ext_d[:, R:R + H, :] = jnp.zeros((nlt, H, LANE), F32)
            acc[...] = jnp.zeros_like(acc)

        for lt in range(nlt):
            ext_u[lt, 0:H, :] = jnp.where(
                jj == 0, 0.0, hh_ref[:, lt * LANE:(lt + 1) * LANE].astype(F32)
                * _sigmoid(hh_ref[:, cw + lt * LANE:cw + (lt + 1) * LANE].astype(F32)))

        def stage(r0, lt):
            rs, ls = pl.ds(r0, _RB), _lanes(lt)
            gate = h_ref[rs, _lanes(lt + nlt)].astype(F32)
            ext_u[lt, pl.ds(pl.multiple_of(r0 + H, SUB), _RB), :] = h_ref[rs, ls].astype(F32) * _sigmoid(gate)
            ext_d[lt, rs, :] = dc_ref[rs, ls]

        def first(r0, lt):
            ls = _lanes(lt)
            dc = dc_ref[pl.ds(r0, _RB), ls]
            acc[a_b:a_b + SUB, ls] += _psum8(dc)
            for k in range(CONV_C):
                tap = ext_u[lt, pl.ds(r0 + (H - (CONV_C - 1 - k)), _RB), :]
                acc[k * SUB:(k + 1) * SUB, ls] += _psum8(dc * tap)

        def second(r0, lt):
            rs, ls, lg = pl.ds(r0, _RB), _lanes(lt), _lanes(lt + nlt)
            du = w_ref[CONV_C - 1:CONV_C, ls] * ext_d[lt, rs, :]
            for k in range(CONV_C - 1):
                du = du + w_ref[k:k + 1, ls] * ext_d[lt, pl.ds(r0 + (CONV_C - 1 - k), _RB), :]
            val = h_ref[rs, ls].astype(F32)
            sg = _sigmoid(h_ref[rs, lg].astype(F32))
            dval = du * sg
            dgate = du * val * sg * (1.0 - sg)
            acc[a_val:a_val + SUB, ls] += _psum8(dval)
            acc[a_gate:a_gate + SUB, ls] += _psum8(dgate)
            dh_ref[rs, ls] = dval.astype(BF16)
            dh_ref[rs, lg] = dgate.astype(BF16)

        _sub_blocks(R, cw, stage)
        _sub_blocks(R, cw, first)
        _sub_blocks(R, cw, second)
        ext_d[:, R:R + H, :] = ext_d[:, 0:H, :]

        @pl.when(j == nch - 1)
        def _():
            for k in range(CONV_C):
                dw_ref[k:k + 1, :] = jnp.sum(acc[k * SUB:(k + 1) * SUB, :], axis=0, keepdims=True)
            db_ref[...] = jnp.sum(acc[a_b:a_b + SUB, :], axis=0, keepdims=True)
            db1_ref[:, 0:cw] = jnp.sum(acc[a_val:a_val + SUB, :], axis=0, keepdims=True)
            db1_ref[:, cw:2 * cw] = jnp.sum(acc[a_gate:a_gate + SUB, :], axis=0, keepdims=True)

    rows = lambda c, j: (nch - 1 - j, c)
    return pl.pallas_call(
        body, out_shape=[SDS((S, 2 * D), BF16), SDS((CONV_C, D), F32), SDS((1, D), F32), SDS((1, 2 * D), F32)],
        grid=(D // cw, nch),
        in_specs=[pl.BlockSpec((R, cw), rows), pl.BlockSpec((R, 2 * cw), rows),
                  pl.BlockSpec((H, 2 * cw), lambda c, j: (jnp.maximum((nch - 1 - j) * (R // H) - 1, 0), c)),
                  pl.BlockSpec((CONV_C, cw), lambda c, j: (0, c))],
        out_specs=[pl.BlockSpec((R, 2 * cw), rows), pl.BlockSpec((CONV_C, cw), lambda c, j: (0, c)),
                   pl.BlockSpec((1, cw), lambda c, j: (0, c)), pl.BlockSpec((1, 2 * cw), lambda c, j: (0, c))],
        scratch_shapes=[pltpu.VMEM((nlt, H + R, LANE), F32), pltpu.VMEM((nlt, R + H, LANE), F32),
                        pltpu.VMEM(((CONV_C + 3) * SUB, cw), F32)], name="conf_conv_bwd",
        compiler_params=_cp(2),
    )(dcv, h1p, h1p, w)


def _local_step(x, mem, tgt, W, fetch=None, send=None):
    G = {}
    W = dict(W)

    def arrive(group, after):
        if fetch is None:
            return None
        got, tok = fetch(group, after)
        for key, val in got.items():
            W[key] = {**W.get(key, {}), **val} if isinstance(val, dict) else val
        return tok

    def gain(g, tok):
        return g if tok is None else g + tok

    def sent(group):
        return None if send is None else send(group, G)

    def xattn_fwd(xin, n, l):
        tok = arrive(("xa", l), n)
        mn = _rms_fwd(mem, gain(W["xa_mem_norm"][l:l + 1], tok), f"xa_memnorm_fwd{l}")
        q = _mm_nn(n, W["xa_wq"][l], out_dtype=BF16, name=f"xa_q{l}")
        k = _mm_nn(mn, W["xa_wk"][l], out_dtype=BF16, name=f"xa_k{l}")
        v = _mm_nn(mn, W["xa_wv"][l], out_dtype=BF16, name=f"xa_v{l}")
        o = _attn_fwd(q, k, v, f"xa_attn_fwd{l}")
        xout, nout = _mm_nn(o, W["xa_wo"][l], out_dtype=F32, name=f"xa_o{l}", add=xin, norm=W["f_norm"][l:l + 1])
        return xout, nout, (xin, n, q, mn, k, v, o)

    def xattn_bwd(dx, dxb, saved, l):
        xin, n, q, mn, k, v, o = saved
        do = _mm_nt(dxb, W["xa_wo"][l], out_dtype=BF16, name=f"xa_do{l}")
        G[f"xa_wo{l}"] = _mm_tn(o, dxb, out_dtype=BF16, name=f"xa_dwo{l}")
        dq, dk, dv = _attn_bwd(q, k, v, do, f"xa_attn_bwd{l}")
        dkb, dvb = dk.astype(BF16), dv.astype(BF16)
        G[f"xa_wq{l}"] = _mm_tn(n, dq, out_dtype=BF16, name=f"xa_dwq{l}")
        G[f"xa_wk{l}"] = _mm_tn(mn, dkb, out_dtype=BF16, name=f"xa_dwk{l}")
        G[f"xa_wv{l}"] = _mm_tn(mn, dvb, out_dtype=BF16, name=f"xa_dwv{l}")
        tok = sent(("xa", l))
        dmn = _mm_nt(dkb, W["xa_wk"][l], out_dtype=F32, name=f"xa_dmn_k{l}")
        dmn = _mm_nt(dvb, W["xa_wv"][l], out_dtype=F32, name=f"xa_dmn_v{l}", add=dmn)
        (G[f"xa_mem_norm{l}"],) = _rms_bwd(mem, W["xa_mem_norm"][l:l + 1], dmn, None, f"xa_memnorm_bwd{l}")
        dx, dxb, G[f"xa_norm{l}"] = _mm_nt(dq, W["xa_wq"][l], out_dtype=F32, name=f"xa_dn{l}",
                                           rms=(xin, gain(W["xa_norm"][l:l + 1], tok), dx))
        return dx, dxb

    def ffn_fwd(xin, n, l, next_gain):
        tok = arrive(("f", l), n)
        hp = _mm_nn(n, W["f_w_up"][l], out_dtype=BF16, name=f"f_up{l}")
        act, gel, ud = _f_fwd(hp, W["f_dw_w"][l], gain(W["f_dw_b"][l:l + 1], tok), f"f_conv_fwd{l}")
        arrive(("fd", l), act)
        res = _mm_nn(act, W["f_w_down"][l], out_dtype=F32, name=f"f_down{l}", add=xin, norm=next_gain)
        xout, nout = res if next_gain is not None else (res, None)
        return xout, nout, (xin, n, hp, act, gel, ud)

    def ffn_bwd(dx, dxb, saved, l):
        xin, n, hp, act, gel, ud = saved
        dact = _mm_nt(dxb, W["f_w_down"][l], out_dtype=BF16, name=f"f_dact{l}")
        G[f"f_w_down{l}"] = _mm_tn(act, dxb, out_dtype=BF16, name=f"f_dwdown{l}")
        dhp, G[f"f_dw_w{l}"], G[f"f_dw_b{l}"] = _f_bwd(dact, hp, gel, ud, W["f_dw_w"][l], f"f_conv_bwd{l}")
        G[f"f_w_up{l}"] = _mm_tn(n, dhp, out_dtype=BF16, name=f"f_dwup{l}", blocks=_CW_F)
        tok = sent(("f", l))
        dx, dxb, G[f"f_norm{l}"] = _mm_nt(dhp, W["f_w_up"][l], out_dtype=F32, name=f"f_dn{l}",
                                          rms=(xin, gain(W["f_norm"][l:l + 1], tok), dx))
        return dx, dxb

    n0 = _rms_fwd(x, W["ab_norm"], "ab_norm_fwd")
    tok = arrive(("ab", 0), n0)
    a_par = (W["a_conv_w"], gain(W["a_conv_b"], tok), W["a_gate_x_w"], W["a_gate_x_b"], W["a_gate_a_w"],
             W["a_gate_a_b"], W["a_lambda"])
    b_par = (W["b_group_w"], W["b_group_b"], W["b_scale"])
    zp = _mm_nn(n0, W["ab_w_in"], out_dtype=BF16, name="ab_in")
    yab, h_a = _a_fwd(zp, *a_par)
    yab = _b_fwd(zp, yab, *b_par)
    arrive(("ab", 1), yab)
    x1, n1 = _mm_nn(yab, W["ab_w_out"], out_dtype=F32, name="ab_out", add=x, norm=W["xa_norm"][0:1])
    x2, n2, s_xa0 = xattn_fwd(x1, n1, 0)
    x3, n3, s_f0 = ffn_fwd(x2, n2, 0, W["c_norm"])
    tok = arrive(("c", 0), n3)
    h1p = _mm_nn(n3, W["c_w_pw1"], out_dtype=BF16, name="c_pw1", bias=gain(W["c_b_pw1"], tok))
    cv = _c_fwd(h1p, W["c_dw_w"], W["c_dw_b"])
    sc = _ln_silu_fwd(cv, W["c_ln_g"], W["c_ln_b"])
    x4, n4 = _mm_nn(sc, W["c_w_pw2"], out_dtype=F32, name="c_pw2", bias=W["c_b_pw2"], add=x3, norm=W["xa_norm"][1:2])
    x5, n5, s_xa1 = xattn_fwd(x4, n4, 1)
    x6, _, s_f1 = ffn_fwd(x5, n5, 1, None)
    loss, dx, dxb, G["final_norm"] = _loss_head(x6, W["final_norm"], tgt)

    dx, dxb = ffn_bwd(dx, dxb, s_f1, 1)
    dx, dxb = xattn_bwd(dx, dxb, s_xa1, 1)
    dsc = _mm_nt(dxb, W["c_w_pw2"], out_dtype=BF16, name="c_dsc")
    G["c_w_pw2"] = _mm_tn(sc, dxb, out_dtype=BF16, name="c_dwpw2")
    dcv, G["c_ln_g"], G["c_ln_b"], G["c_b_pw2"] = _ln_silu_bwd(dsc, cv, W["c_ln_g"], W["c_ln_b"], dx)
    dh1p, G["c_dw_w"], G["c_dw_b"], G["c_b_pw1"] = _c_bwd(dcv, h1p, W["c_dw_w"])
    G["c_w_pw1"] = _mm_tn(n3, dh1p, out_dtype=BF16, name="c_dwpw1", blocks=_CW_C)
    tok = sent(("c", 0))
    dx, dxb, G["c_norm"] = _mm_nt(dh1p, W["c_w_pw1"], out_dtype=F32, name="c_dn",
                                  rms=(x3, gain(W["c_norm"], tok), dx))
    dx, dxb = ffn_bwd(dx, dxb, s_f0, 0)
    dx, dxb = xattn_bwd(dx, dxb, s_xa0, 0)
    dyab = _mm_nt(dxb, W["ab_w_out"], out_dtype=BF16, name="ab_dyab")
    G["ab_w_out"] = _mm_tn(yab, dxb, out_dtype=BF16, name="ab_dwout")
    tok = sent(("ab", 1))
    a_par = (a_par[0], gain(a_par[1], tok)) + a_par[2:]
    (dzg, dzr, G["a_conv_w"], G["a_conv_b"], G["a_gate_x_w"], G["a_gate_x_b"], G["a_gate_a_w"], G["a_gate_a_b"],
     G["a_lambda"]) = _a_bwd(dyab, zp, h_a, *a_par)
    dzq, G["b_group_w"], G["b_group_b"], G["b_scale"] = _b_bwd(dyab, zp, *b_par)
    G["ab_w_in"] = jnp.concatenate(
        [_mm_tn(n0, dz, out_dtype=BF16, name=f"ab_dwin_{part}")
         for part, dz in (("gate", dzg), ("rec", dzr), ("pool", dzq))], axis=1)
    tok = sent(("ab", 0))
    dx, _, G["ab_norm"] = _mm_nt_cols([dzg, dzr, dzq], W["ab_w_in"], name="ab_dn",
                                      rms=(x, gain(W["ab_norm"], tok), dx))
    return loss, dx, G


def _my_place():
    x, y, c = lax.axis_index("x"), lax.axis_index("y"), lax.axis_index("c")
    return x, y, c


def _all_gather(shards, name):
    n = len(shards)

    def body(*refs):
        ins, outs = refs[:n], refs[n:2 * n]
        send_sems, recv_sems, local_sems = refs[2 * n:]
        x, y, c = _my_place()
        me, sibling = (x, y, c), (x, y, 1 - c)
        chips = [(1 - x, y), (x, 1 - y), (1 - x, 1 - y)]

        def slab(a, place):
            px, py, pc = place
            return outs[a].at[4 * px + 2 * py + pc]

        def copy(a, k, block, to, src=None):
            return pltpu.make_async_remote_copy(
                src_ref=slab(a, block) if src is None else src, dst_ref=slab(a, block),
                send_sem=send_sems.at[a, k], recv_sem=recv_sems.at[a, k], device_id=to, device_id_type=MESH)

        mine = [pltpu.make_async_copy(ins[a], slab(a, me), local_sems.at[a]) for a in range(n)]
        for cp in mine:
            cp.start()
        first = []
        for j, chip in enumerate(chips):
            first += [copy(a, 1 + j, me, (*chip, c), src=ins[a]) for a in range(n)]
        first += [copy(a, 0, me, sibling, src=ins[a]) for a in range(n)]
        for cp in first:
            cp.start()
        passed = []
        for j, chip in enumerate(chips):
            for a in range(n):
                copy(a, 1 + j, (*chip, c), me).wait_recv()
                cp = copy(a, 4 + j, (*chip, c), sibling)
                cp.start()
                passed.append(cp)
        for a in range(n):
            copy(a, 0, sibling, me).wait_recv()
        for j, chip in enumerate(chips):
            for a in range(n):
                copy(a, 4 + j, (*chip, 1 - c), me).wait_recv()
        for cp in first + passed:
            cp.wait_send()
        for cp in mine:
            cp.wait()

    any_spec = pl.BlockSpec(memory_space=pl.ANY)
    return pl.pallas_call(
        body, out_shape=[SDS((N_DEV,) + s.shape, s.dtype) for s in shards], in_specs=[any_spec] * n,
        out_specs=[any_spec] * n,
        scratch_shapes=[pltpu.SemaphoreType.DMA((n, 7)), pltpu.SemaphoreType.DMA((n, 7)), pltpu.SemaphoreType.DMA((n,))],
        name=name,
    )(*shards)


_HBM = pl.BlockSpec(memory_space=pltpu.HBM)
_SEM = pl.BlockSpec(memory_space=pltpu.SEMAPHORE)
_EFFECT = pltpu.SideEffectType.DATAFLOW_SIDE_EFFECTING


def _peer_places():
    x, y, c = _my_place()
    peers = []
    for k in range(1, N_DEV):
        px = 1 - x if (k >> 2) & 1 else x
        py = 1 - y if (k >> 1) & 1 else y
        pc = 1 - c if k & 1 else c
        peers.append(((px, py, pc), 4 * px + 2 * py + pc))
    return (x, y, c), 4 * x + 2 * y + c, peers


def _send_start(srcs, per_dest, name):
    n = len(srcs)
    lands = [lax.empty((N_DEV,) + (s.shape[1:] if per_dest else s.shape), s.dtype) for s in srcs]

    def body(*refs):
        src, land = refs[:n], refs[n:2 * n]
        outs = refs[2 * n:]
        send, recv, token = outs[:n], outs[n:2 * n], outs[4 * n]
        place, me, peers = _peer_places()
        for a in range(n):
            for peer, pidx in peers + [(place, me)]:
                pltpu.make_async_remote_copy(
                    src_ref=src[a].at[pidx] if per_dest else src[a], dst_ref=land[a].at[me], send_sem=send[a],
                    recv_sem=recv[a], device_id=peer, device_id_type=MESH).start()
        token[...] = jnp.zeros_like(token)

    hbm = lambda a: pltpu.HBM(a.shape, a.dtype)
    sem = pltpu.SemaphoreType.DMA(())
    res = pl.pallas_call(
        body, name=name,
        out_shape=tuple([sem] * (2 * n) + [hbm(s) for s in srcs] + [hbm(l) for l in lands]
                        + [SDS((SUB, LANE), F32)]),
        in_specs=[_HBM] * (2 * n),
        out_specs=tuple([_SEM] * (2 * n) + [_HBM] * (2 * n) + [pl.BlockSpec(memory_space=pltpu.VMEM)]),
        input_output_aliases={i: 2 * n + i for i in range(2 * n)},
        compiler_params=pltpu.CompilerParams(has_side_effects=_EFFECT),
    )(*[pltpu.with_memory_space_constraint(s, pltpu.HBM) for s in srcs],
      *[pltpu.with_memory_space_constraint(l, pltpu.HBM) for l in lands])
    return res[:n], res[n:2 * n], res[2 * n:3 * n], res[3 * n:4 * n], res[4 * n]


def _send_wait(send, recv, srcs, lands, after, per_dest, name):
    n = len(srcs)

    def body(*refs):
        src, land = refs[:n], refs[n:2 * n]
        send_s, recv_s = refs[2 * n:3 * n], refs[3 * n:4 * n]
        token = refs[-1]
        place, _, _ = _peer_places()
        for a in range(n):
            copy = pltpu.make_async_remote_copy(
                src_ref=src[a] if per_dest else land[a], dst_ref=land[a], send_sem=send_s[a],
                recv_sem=recv_s[a], device_id=place, device_id_type=MESH)
            copy.wait_send()
            copy.wait_recv()
        token[...] = jnp.zeros_like(token)

    hbm = lambda a: pltpu.HBM(a.shape, a.dtype)
    res = pl.pallas_call(
        body, name=name,
        out_shape=tuple([hbm(s) for s in srcs] + [hbm(l) for l in lands] + [SDS((SUB, LANE), F32)]),
        in_specs=[_HBM] * (2 * n) + [_SEM] * (2 * n) + [pl.BlockSpec(memory_space=pl.ANY)],
        out_specs=tuple([_HBM] * (2 * n) + [pl.BlockSpec(memory_space=pltpu.VMEM)]),
        input_output_aliases={i: i for i in range(2 * n)},
        compiler_params=pltpu.CompilerParams(has_side_effects=_EFFECT),
    )(*srcs, *lands, *send, *recv, after)
    return res[:n], res[n:2 * n], res[2 * n]


def _adamw_math(w, g, m, v):
    m = ADAM_B1 * m + (1.0 - ADAM_B1) * g
    v = ADAM_B2 * v + (1.0 - ADAM_B2) * (g * g)
    m_hat = m / (1.0 - ADAM_B1 ** ADAM_STEP)
    v_hat = v / (1.0 - ADAM_B2 ** ADAM_STEP)
    delta = -ADAM_LR * (m_hat / (jnp.sqrt(v_hat) + ADAM_EPS) + ADAM_WD * w)
    return delta, m, v


def _row_tile(r, c, itemsize_rows):
    cap = max(SUB, (itemsize_rows // (4 * c)) // SUB * SUB)
    if r <= cap:
        return r
    best = None
    for t in range(SUB, cap + 1, SUB):
        if r % t == 0:
            best = t
    return best if best is not None else r


def _sum_adamw(landing, w, m, v, name, layer=0, prev=None, after=None):
    _, r, c = landing.shape
    tr = _row_tile(r, c, 2 << 20)
    off = layer * (r // tr)
    tail = ([] if prev is None else list(prev)) + ([] if after is None else [after])

    def body(l_ref, w_ref, m_ref, v_ref, *rest):
        g_ref, d_ref, mo_ref, vo_ref = rest[-4:]
        g = l_ref[0].astype(F32)
        for s in range(1, N_DEV):
            g = g + l_ref[s].astype(F32)
        g_ref[...] = g
        d_ref[...], mo_ref[...], vo_ref[...] = _adamw_math(w_ref[...], g, m_ref[...], v_ref[...])

    blk = pl.BlockSpec((tr, c), lambda i: (i + off, 0))
    n_prev = 0 if prev is None else 4
    return pl.pallas_call(
        body, out_shape=[SDS(w.shape, F32)] * 4, grid=(r // tr,),
        in_specs=[pl.BlockSpec((N_DEV, tr, c), lambda i: (0, i, 0)), blk, blk, blk]
        + [pl.BlockSpec(memory_space=pl.ANY)] * len(tail),
        out_specs=[blk] * 4, input_output_aliases={4 + i: i for i in range(n_prev)}, name=name,
        compiler_params=_cp(1),
    )(landing, w, m, v, *tail)


def _sum8(landing, name):
    _, r, c = landing.shape

    def body(l_ref, g_ref):
        g = l_ref[0]
        for s in range(1, N_DEV):
            g = g + l_ref[s]
        g_ref[...] = g

    return pl.pallas_call(body, out_shape=SDS((r, c), F32), name=name, compiler_params=_cp(0))(landing)


def _adamw_small(repl_pack, own_pack, P, M, V):
    table, off = [], 0
    for name, shape in _REPL.items():
        table.append((name, shape if len(shape) > 1 else (1,) + shape, 0, off // LANE))
        off += _size(shape)
    off = _REPL_ROWS * LANE
    for name, shape in _SMALL_SHARDED.items():
        table.append((name, shape, 1, off // LANE))
        off += _size(shape)
    n = len(table)

    def body(*refs):
        packs, ins, outs = refs[:2], refs[2:2 + 3 * n], refs[2 + 3 * n:]
        for p, (_, shape, which, r0) in enumerate(table):
            w_ref, m_ref, v_ref = ins[3 * p:3 * p + 3]
            g_ref, d_ref, mo_ref, vo_ref = outs[4 * p:4 * p + 4]
            pack, rows, q = packs[which], shape[-2], shape[-1] // LANE
            lead = [()]
            for dim in shape[:-2]:
                lead = [t + (i,) for t in lead for i in range(dim)]
            for li, idx in enumerate(lead):
                if q == 1:
                    dst = g_ref.at[idx] if idx else g_ref
                    dst[...] = pack[r0 + li * rows:r0 + (li + 1) * rows, :]
                    continue
                for i in range(rows):
                    for k in range(q):
                        row = r0 + (li * rows + i) * q + k
                        g_ref[idx + (slice(i, i + 1), slice(k * LANE, (k + 1) * LANE))] = pack[row:row + 1, :]
            d_ref[...], mo_ref[...], vo_ref[...] = _adamw_math(w_ref[...], g_ref[...], m_ref[...], v_ref[...])

    ins, out_shape = [], []
    for name, shape, _, _ in table:
        ins += [t[name].reshape(shape) for t in (P, M, V)]
        out_shape += [SDS(shape, F32)] * 4
    res = pl.pallas_call(body, out_shape=out_shape, name="adamw_small", compiler_params=_cp(0))(
        repl_pack, own_pack, *ins)
    dicts = ({}, {}, {}, {})
    for p, (name, shape, _, _) in enumerate(table):
        for d, arr in zip(dicts, res[4 * p:4 * p + 4]):
            d[name] = arr.reshape(P[name].shape)
    return dicts


_BIG = {
    "ab_w_in": (1, D, 320), "ab_w_out": (1, 192, D), "c_w_pw1": (1, D, 256), "c_w_pw2": (1, 128, D),
    "xa_wq": (2, 128, D), "xa_wk": (2, 128, D), "xa_wv": (2, 128, D), "xa_wo": (2, 128, D),
    "f_w_up": (2, D, 768), "f_w_down": (2, 384, D),
}
_SMALL_SHARDED = {
    "a_conv_w": (1, 4, 128), "c_norm": (1, 128), "c_b_pw1": (1, 256), "c_dw_w": (1, 31, 128), "c_dw_b": (1, 128),
    "c_ln_g": (1, 128), "c_ln_b": (1, 128), "c_b_pw2": (1, 128), "f_dw_w": (2, 3, 384),
}
_REPL = {
    "ab_norm": (1, D), "a_conv_b": (1, D), "a_gate_x_w": (1, 8, 128, 128), "a_gate_x_b": (1, D),
    "a_gate_a_w": (1, 8, 128, 128), "a_gate_a_b": (1, D), "a_lambda": (1, D), "b_group_w": (1, 4, 128, 128),
    "b_group_b": (1, 512), "b_scale": (1, 512), "xa_norm": (2, D), "xa_mem_norm": (2, D), "f_norm": (2, D),
    "f_dw_b": (2, D_FF), "final_norm": (D,),
}


def _size(shape):
    n = 1
    for s in shape:
        n *= s
    return n


_N_SS = sum(_size(s) for s in _SMALL_SHARDED.values())
_N_REPL = sum(_size(s) for s in _REPL.values())
_REPL_ROWS = -(-_N_REPL // (N_DEV * SUB * LANE)) * SUB
_SS_ROWS = _N_SS // LANE
_SMALL_ROWS = -(-(_REPL_ROWS + _SS_ROWS) // SUB) * SUB


def _pack(parts, rows):
    flat = jnp.concatenate([p.reshape(-1).astype(F32) for p in parts])
    return jnp.pad(flat, (0, rows * LANE - flat.shape[0])).reshape(rows, LANE)


def _pair_blocks(v, bw):
    lead, n = v.shape[:-1], v.shape[-1]
    return jnp.swapaxes(v.reshape(lead + (2, n // (2 * bw), bw)), -3, -2).reshape(lead + (n,))


def _unpair_blocks(v, bw):
    lead, n = v.shape[:-1], v.shape[-1]
    return jnp.swapaxes(v.reshape(lead + (n // (2 * bw), 2, bw)), -3, -2).reshape(lead + (n,))


_GROUPS = {
    ("ab", 0): (("ab_w_in", 0),),
    ("ab", 1): (("ab_w_out", 0),),
    ("xa", 0): (("xa_wq", 0), ("xa_wk", 0), ("xa_wv", 0), ("xa_wo", 0)),
    ("f", 0): (("f_w_up", 0),),
    ("fd", 0): (("f_w_down", 0),),
    ("c", 0): (("c_w_pw1", 0), ("c_w_pw2", 0)),
    ("xa", 1): (("xa_wq", 1), ("xa_wk", 1), ("xa_wv", 1), ("xa_wo", 1)),
    ("f", 1): (("f_w_up", 1),),
    ("fd", 1): (("f_w_down", 1),),
}
_SEND_GROUPS = {g: m for g, m in _GROUPS.items() if g[0] != "fd"}
_SEND_GROUPS[("f", 0)] = (("f_w_up", 0), ("f_w_down", 0))
_SEND_GROUPS[("f", 1)] = (("f_w_up", 1), ("f_w_down", 1))


def _weight_layout(name, g):
    if name == "ab_w_in":
        return jnp.swapaxes(g, 0, 1).reshape(D, N_DEV * 320)
    if name in ("c_w_pw1", "f_w_up"):
        return g
    return g.reshape(N_DEV * g.shape[1], D)


def _grad_blocks(name, l, G):
    _, r, c = _BIG[name]
    if name == "ab_w_in":
        return jnp.swapaxes(G[name].reshape(D, N_DEV, 320), 0, 1)
    if name == "c_w_pw1":
        return G[name]
    if name == "f_w_up":
        return G[f"{name}{l}"]
    return (G[name] if _BIG[name][0] == 1 else G[f"{name}{l}"]).reshape(N_DEV, r, c)


def _small_layouts(sm):
    W = {}
    sm = sm.reshape(N_DEV, -1)
    off = 0
    for name, shape in _SMALL_SHARDED.items():
        n = _size(shape)
        blocks = sm[:, off:off + n].reshape((N_DEV,) + shape)
        off += n
        W[name] = jnp.moveaxis(blocks, 0, -2).reshape(shape[:-1] + (N_DEV * shape[-1],))
    W["a_conv_w"], W["c_dw_w"] = W["a_conv_w"][0], W["c_dw_w"][0]
    W["c_b_pw1"] = _pair_blocks(W["c_b_pw1"], _CW_C)
    return W


def _to_dest_major(g, shape):
    full = g.reshape(shape[:-1] + (N_DEV, shape[-1]))
    return jnp.moveaxis(full, -2, 0).reshape(N_DEV, -1)


def kernel(x, mem, ab_norm, ab_w_in, a_conv_w, a_conv_b, a_gate_x_w, a_gate_x_b, a_gate_a_w, a_gate_a_b, a_lambda, b_group_w, b_group_b, b_scale, ab_w_out, c_norm, c_w_pw1, c_b_pw1, c_dw_w, c_dw_b, c_ln_g, c_ln_b, c_w_pw2, c_b_pw2, xa_norm, xa_mem_norm, xa_wq, xa_wk, xa_wv, xa_wo, f_norm, f_w_up, f_dw_w, f_dw_b, f_w_down, final_norm, loss_target, m_ab_norm, m_ab_w_in, m_a_conv_w, m_a_conv_b, m_a_gate_x_w, m_a_gate_x_b, m_a_gate_a_w, m_a_gate_a_b, m_a_lambda, m_b_group_w, m_b_group_b, m_b_scale, m_ab_w_out, m_c_norm, m_c_w_pw1, m_c_b_pw1, m_c_dw_w, m_c_dw_b, m_c_ln_g, m_c_ln_b, m_c_w_pw2, m_c_b_pw2, m_xa_norm, m_xa_mem_norm, m_xa_wq, m_xa_wk, m_xa_wv, m_xa_wo, m_f_norm, m_f_w_up, m_f_dw_w, m_f_dw_b, m_f_w_down, m_final_norm, v_ab_norm, v_ab_w_in, v_a_conv_w, v_a_conv_b, v_a_gate_x_w, v_a_gate_x_b, v_a_gate_a_w, v_a_gate_a_b, v_a_lambda, v_b_group_w, v_b_group_b, v_b_scale, v_ab_w_out, v_c_norm, v_c_w_pw1, v_c_b_pw1, v_c_dw_w, v_c_dw_b, v_c_ln_g, v_c_ln_b, v_c_w_pw2, v_c_b_pw2, v_xa_norm, v_xa_mem_norm, v_xa_wq, v_xa_wk, v_xa_wv, v_xa_wo, v_f_norm, v_f_w_up, v_f_dw_w, v_f_dw_b, v_f_w_down, v_final_norm):
    args = dict(locals())
    P = {n: args[n] for n in _NAMES}
    M = {n: args["m_" + n] for n in _NAMES}
    V = {n: args["v_" + n] for n in _NAMES}

    in_flight = {}

    def launch(groups, tok):
        shards, n_of = [], {}
        for grp in groups:
            for name, l in _GROUPS[grp]:
                w = P[name][l] if tok is None else P[name][l] + tok
                shards.append(w.astype(BF16))
            if grp == ("ab", 0):
                shards.append(_pack([P[n] for n in _SMALL_SHARDED], _SS_ROWS + 4))
            n_of[grp] = len(shards)
        res = _send_start(shards, False, "gather_start_" + "_".join(g[0] + str(g[1]) for g in groups))
        lo = 0
        for grp in groups:
            in_flight[grp] = [r[lo:n_of[grp]] for r in res[:4]]
            lo = n_of[grp]
        return res[4][:1, :1]

    follow = {("ab", 0): [("ab", 1), ("xa", 0), ("f", 0)], ("xa", 0): [("fd", 0), ("c", 0), ("xa", 1)],
              ("f", 0): [("f", 1), ("fd", 1)]}

    def fetch(grp, after):
        send_s, recv_s, srcs, lands = in_flight.pop(grp)
        srcs, lands, tok = _send_wait(send_s, recv_s, srcs, lands, after, False, f"gather_wait_{grp[0]}{grp[1]}")
        tok = launch(follow[grp], tok[:1, :1]) if grp in follow else None
        full = lands
        out = {}
        for (name, l), g in zip(_GROUPS[grp], full):
            w = _weight_layout(name, g)
            if _BIG[name][0] == 1:
                out[name] = w
            else:
                out[name] = {l: w}
        if grp == ("ab", 0):
            out.update(_small_layouts(full[-1]))
        return out, tok

    zero = launch([("ab", 0)], None)

    pending, held = [], []
    rides_with_next = {("xa", 1), ("f", 0)}

    def send(grp, G):
        held.extend(_SEND_GROUPS[grp])
        if grp in rides_with_next:
            return None
        members = tuple(held)
        del held[:]
        res = _send_start([_grad_blocks(name, l, G) for name, l in members], True, f"send_{grp[0]}{grp[1]}")
        pending.append((members, res))
        return res[4][:1, :1]

    W = {n: P[n] for n in _REPL}
    W["ab_norm"] = P["ab_norm"] + zero
    W["final_norm"] = P["final_norm"].reshape(1, D)
    W["a_gate_x_w"], W["a_gate_a_w"], W["b_group_w"] = P["a_gate_x_w"][0], P["a_gate_a_w"][0], P["b_group_w"][0]
    loss, grad_x, G = _local_step(x[0], mem[0], loss_target[0], W, fetch, send)
    loss = lax.psum(loss[0, 0], ("x", "y", "c"))

    Gs = dict(G)
    Gs["c_b_pw1"] = _unpair_blocks(G["c_b_pw1"], _CW_C)
    Gs["f_dw_w"] = jnp.stack([G["f_dw_w0"], G["f_dw_w1"]])
    Gs["a_conv_w"], Gs["c_dw_w"] = G["a_conv_w"][None], G["c_dw_w"][None]
    for n in ("xa_norm", "xa_mem_norm", "f_norm", "f_dw_b"):
        Gs[n] = jnp.concatenate([G[f"{n}0"], G[f"{n}1"]], axis=0)
    for n in ("a_gate_x_w", "a_gate_a_w", "b_group_w"):
        Gs[n] = G[n][None]
    repl_flat = jnp.concatenate([Gs[n].reshape(-1) for n in _REPL])
    repl_rows = jnp.pad(repl_flat, (0, N_DEV * _REPL_ROWS * LANE - _N_REPL)).reshape(N_DEV, _REPL_ROWS, LANE)
    ss_rows = jnp.concatenate([_to_dest_major(Gs[n], s) for n, s in _SMALL_SHARDED.items()], axis=1)
    ss_rows = ss_rows.reshape(N_DEV, _SS_ROWS, LANE)
    small_pack = jnp.concatenate(
        [repl_rows, ss_rows, jnp.zeros((N_DEV, _SMALL_ROWS - _REPL_ROWS - _SS_ROWS, LANE), F32)], axis=1)
    last = _send_start([small_pack], True, "send_small")
    pending.append(((("small", 0),), last))

    def arrived(some, after, name):
        members = [m for mem_, _ in some for m in mem_]
        cat = [[a for _, res in some for a in res[i]] for i in range(4)]
        srcs, lands, _ = _send_wait(cat[0], cat[1], cat[2], cat[3], after, True, name)
        return dict(zip(members, lands))

    out_g, out_d, out_m, out_v = {}, {}, {}, {}
    chain = [None]

    def update(name, landed):
        layers, r, c = _BIG[name]
        w2, m2, v2 = [t[name].reshape(layers * r, c) for t in (P, M, V)]
        res = None
        for l in range(layers):
            res = _sum_adamw(landed[(name, l)], w2, m2, v2, f"adamw_{name}{l}", layer=l, prev=res,
                             after=chain[0] if l == 0 else None)
        chain[0] = res[1]
        out_g[name], out_d[name], out_m[name], out_v[name] = [t.reshape(P[name].shape) for t in res]

    landed = arrived(pending[:-2], grad_x, "send_wait_early")
    for name in _BIG:
        if name != "ab_w_in":
            update(name, landed)
    landed = arrived(pending[-2:], out_v["f_w_down"], "send_wait_late")
    update("ab_w_in", landed)

    small_sum = _sum8(landed[("small", 0)], "sum_small")
    (repl_all,) = _all_gather([small_sum[:_REPL_ROWS]], "gather_small_grads")
    for out, got in zip((out_g, out_d, out_m, out_v),
                        _adamw_small(repl_all.reshape(N_DEV * _REPL_ROWS, LANE), small_sum, P, M, V)):
        out.update(got)

    return (loss, grad_x[None], *[out_g[n] for n in _NAMES], *[out_d[n] for n in _NAMES],
            *[out_m[n] for n in _NAMES], *[out_v[n] for n in _NAMES])


_NAMES = ("ab_norm", "ab_w_in", "a_conv_w", "a_conv_b", "a_gate_x_w", "a_gate_x_b", "a_gate_a_w", "a_gate_a_b",
          "a_lambda", "b_group_w", "b_group_b", "b_scale", "ab_w_out", "c_norm", "c_w_pw1", "c_b_pw1", "c_dw_w",
          "c_dw_b", "c_ln_g", "c_ln_b", "c_w_pw2", "c_b_pw2", "xa_norm", "xa_mem_norm", "xa_wq", "xa_wk", "xa_wv",
          "xa_wo", "f_norm", "f_w_up", "f_dw_w", "f_dw_b", "f_w_down", "final_norm")
```

```python
import functools

import jax
import jax.numpy as jnp
from jax import lax
from jax.experimental import pallas as pl
from jax.experimental.pallas import tpu as pltpu

F32, BF16 = jnp.float32, jnp.bfloat16
SDS = jax.ShapeDtypeStruct
MESH = pl.DeviceIdType.MESH

N_DEV = 8
D = 1024
N_MEM = 256
XA_HEADS, XA_HD = 4, 256
HD_A = 128
CONV_A, CONV_C, CONV_F = 4, 31, 3
C_RG = 8.0
POOL_WINDOWS = (2, 4, 8, 16)
D_FF = 3 * D
EPS = 1e-6
ADAM_LR, ADAM_B1, ADAM_B2, ADAM_EPS, ADAM_WD, ADAM_STEP = 0.001, 0.9, 0.999, 1e-08, 0.01, 10

LANE = 128
SUB = 8
VMEM_LIMIT = 56 * 1024 * 1024
R_SEQ = 1024
R_POOL = 2048
R_RGLRU = 2048
R_FFN = 2048
TM_ROW = 1024


def _cp(n_axes):
    return pltpu.CompilerParams(dimension_semantics=("arbitrary",) * n_axes, vmem_limit_bytes=VMEM_LIMIT)


def _tile(n, pref):
    if n <= pref:
        return n
    best = None
    for t in range(LANE, pref + 1, LANE):
        if n % t == 0:
            best = t
    assert best is not None, (n, pref)
    return best


def _perm2(n):
    return (n % 2) * 4 + n // 2


_NN = (((1,), (0,)), ((), ()))
_NT = (((1,), (1,)), ((), ()))
_TN = (((0,), (0,)), ((), ()))


def _mm_call(name, grid, ab, ab_specs, dims, acc_shape, extras, outs, finish, from_ref=False):
    nk = grid[2]
    n_ab, n_ex, n_out = len(ab), len(extras), len(outs)
    use_acc = nk > 1 or from_ref

    def product(refs):
        r = lax.dot_general(refs[0][...], refs[1][...], dims, preferred_element_type=F32)
        for i in range(1, n_ab):
            r = r + lax.dot_general(refs[2 * i][...], refs[2 * i + 1][...], dims, preferred_element_type=F32)
        return r

    def body(*refs):
        rest = refs[2 * n_ab:]
        ex_refs, o_refs = rest[:n_ex], rest[n_ex:n_ex + n_out]
        first_rows = pl.program_id(0) == 0
        if not use_acc:
            finish(product(refs), ex_refs, o_refs, first_rows)
            return
        acc = rest[n_ex + n_out]
        if nk == 1:
            acc[...] = product(refs)
            finish(acc, ex_refs, o_refs, first_rows)
            return
        k = pl.program_id(2)

        @pl.when(k == 0)
        def _():
            acc[...] = jnp.zeros_like(acc)

        acc[...] += product(refs)

        @pl.when(k == nk - 1)
        def _():
            finish(acc if from_ref else acc[...], ex_refs, o_refs, first_rows)

    res = pl.pallas_call(
        body, out_shape=[o for o, _ in outs], grid=grid,
        in_specs=list(ab_specs) + [s for _, s in extras], out_specs=[s for _, s in outs],
        scratch_shapes=[pltpu.VMEM(acc_shape, F32)] if use_acc else [], name=name, compiler_params=_cp(3),
    )(*[t for pair in ab for t in pair], *[e for e, _ in extras])
    return res[0] if n_out == 1 else res


def _finish_sum(r, ex_refs, o_refs, first_rows):
    del first_rows
    for e in ex_refs:
        r = r + e[...]
    o_refs[0][...] = r.astype(o_refs[0].dtype)


def _finish_sum_norm(r, ex_refs, o_refs, first_rows):
    del first_rows
    for e in ex_refs[:-1]:
        r = r + e[...]
    o_refs[0][...] = r
    o_refs[1][...] = ((r * lax.rsqrt(jnp.mean(r * r, axis=-1, keepdims=True) + EPS)) * ex_refs[-1][...]).astype(BF16)


_EPI_ROWS = 16


def _finish_rms_bwd(r_ref, ex_refs, o_refs, first_rows):
    x_ref, g_ref, dres_ref = ex_refs
    dx_ref, dxb_ref, dg_ref = o_refs

    @pl.when(first_rows)
    def _():
        dg_ref[...] = jnp.zeros_like(dg_ref)

    gv = g_ref[...]
    inv_d = 1.0 / r_ref.shape[1]

    def step(i, dg_acc):
        groups = [pl.ds(pl.multiple_of(i * (2 * _EPI_ROWS) + u * _EPI_ROWS, _EPI_ROWS), _EPI_ROWS) for u in range(2)]
        sums = []
        for rows in groups:
            r, xf = r_ref[rows, :], x_ref[rows, :]
            sums.append((jnp.sum(xf * xf, axis=-1, keepdims=True), jnp.sum((r * gv) * xf, axis=-1, keepdims=True)))
        for rows, (sxx, sax) in zip(groups, sums):
            r, xf = r_ref[rows, :], x_ref[rows, :]
            rs = lax.rsqrt(sxx * inv_d + EPS)
            dg_acc = dg_acc + _psum8(r * (xf * rs))
            dx = rs * (r * gv) - xf * (rs * rs * (sax * rs * inv_d)) + dres_ref[rows, :]
            dx_ref[rows, :] = dx
            dxb_ref[rows, :] = dx.astype(BF16)
        return dg_acc

    dg_acc = lax.fori_loop(0, r_ref.shape[0] // (2 * _EPI_ROWS), step, jnp.zeros((SUB, r_ref.shape[1]), F32))
    dg_ref[...] += jnp.sum(dg_acc, axis=0, keepdims=True)


def _rms_bwd_io(M, tm, x, g, dres):
    rows = pl.BlockSpec((tm, D), lambda m, n, k: (m, 0))
    vec = pl.BlockSpec((1, D), lambda m, n, k: (0, 0))
    return ([(x, rows), (g, vec), (dres, rows)],
            [(SDS((M, D), F32), rows), (SDS((M, D), BF16), rows), (SDS((1, D), F32), vec)])


_K_WHOLE = 3072


def _mm_nn(a, b, *, out_dtype, name, bias=None, add=None, norm=None):
    M, K = a.shape
    tk = K if K <= _K_WHOLE else _tile(K, 1024)
    if K <= 1024 and norm is None:
        tm = _tile(M, 2048 if add is None and out_dtype == BF16 else 1024)
    else:
        tm = _tile(M, 512)
    if b.ndim == 3:
        nb, _, bw = b.shape
        N, tn, nn = nb * bw, bw, nb
        b_spec = pl.BlockSpec((None, tk, bw), lambda m, n, k: (_perm2(n), k, 0))
    else:
        N = b.shape[1]
        tn = _tile(N, 1024)
        nn = N // tn
        b_spec = pl.BlockSpec((tk, tn), lambda m, n, k: (k, n))
    tile = pl.BlockSpec((tm, tn), lambda m, n, k: (m, n))
    vec = pl.BlockSpec((1, tn), lambda m, n, k: (0, n))
    extras = ([] if bias is None else [(bias, vec)]) + ([] if add is None else [(add, tile)])
    outs, finish = [(SDS((M, N), out_dtype), tile)], _finish_sum
    if norm is not None:
        assert tn == N == D and out_dtype == F32
        extras.append((norm, vec))
        outs, finish = outs + [(SDS((M, N), BF16), tile)], _finish_sum_norm
    return _mm_call(name, (M // tm, nn, K // tk), [(a, b)], [pl.BlockSpec((tm, tk), lambda m, n, k: (m, k)), b_spec],
                    _NN, (tm, tn), extras, outs, finish)


def _mm_nt(a, b, *, out_dtype, name, add=None, rms=None):
    M, N = a.shape
    if b.ndim == 3:
        nb, Ko, bw = b.shape
        tm = _tile(M, 1024)
        tn, tk, nk = _tile(Ko, 1024), bw, nb
        b_spec = pl.BlockSpec((None, tn, bw), lambda m, n, k: (_perm2(k), n, 0))
    else:
        Ko = b.shape[0]
        tk = N if N <= _K_WHOLE else _tile(N, 1024)
        if N <= 1024 and rms is None:
            tm = _tile(M, 2048 if add is None and out_dtype == BF16 else 1024)
        else:
            tm = _tile(M, 512)
        tn = _tile(Ko, 1024)
        nk = N // tk
        b_spec = pl.BlockSpec((tn, tk), lambda m, n, k: (n, k))
    tile = pl.BlockSpec((tm, tn), lambda m, n, k: (m, n))
    extras = [] if add is None else [(add, tile)]
    outs, finish = [(SDS((M, Ko), out_dtype), tile)], _finish_sum
    if rms is not None:
        assert tn == Ko == D and add is None
        (extras, outs), finish = _rms_bwd_io(M, tm, *rms), _finish_rms_bwd
    return _mm_call(name, (M // tm, Ko // tn, nk), [(a, b)], [pl.BlockSpec((tm, tk), lambda m, n, k: (m, k)), b_spec],
                    _NT, (tm, tn), extras, outs, finish, from_ref=rms is not None)


def _mm_nt_cols(parts, b, *, name, rms):
    M = parts[0].shape[0]
    tm = _tile(M, 512)
    specs, off = [], 0
    for p in parts:
        w = p.shape[1]
        assert off % w == 0
        specs.append(pl.BlockSpec((tm, w), lambda m, n, k: (m, 0)))
        specs.append(pl.BlockSpec((D, w), functools.partial(lambda m, n, k, o: (0, o), o=off // w)))
        off += w
    extras, outs = _rms_bwd_io(M, tm, *rms)
    return _mm_call(name, (M // tm, 1, 1), [(p, b) for p in parts], specs, _NT, (tm, D), extras, outs, _finish_rms_bwd,
                    from_ref=True)


def _mm_tn(a, b, *, out_dtype, name, blocks=None):
    S, Ka = a.shape
    Nb = b.shape[1]
    tm = _tile(Ka, 1024)
    if blocks is not None:
        bw = blocks
        tn, nn = bw, Nb // bw
        out = (SDS((nn, Ka, bw), out_dtype), pl.BlockSpec((None, tm, bw), lambda m, n, k: (_perm2(n), m, 0)))
    else:
        tn = _tile(Nb, 1024)
        nn = Nb // tn
        out = (SDS((Ka, Nb), out_dtype), pl.BlockSpec((tm, tn), lambda m, n, k: (m, n)))
    steps = (Ka // tm) * nn
    tk = _tile(S, 4096 if steps >= 4 else 2048 if steps >= 2 else 1024)
    return _mm_call(name, (Ka // tm, nn, S // tk), [(a, b)],
                    [pl.BlockSpec((tk, tm), lambda m, n, k: (k, m)), pl.BlockSpec((tk, tn), lambda m, n, k: (k, n))],
                    _TN, (tm, tn), [], [out], _finish_sum)


def _row(tm, c):
    return pl.BlockSpec((tm, c), lambda i: (i, 0))


def _full(shape):
    nd = len(shape)
    return pl.BlockSpec(shape, lambda i: (0,) * nd)


def _rms_fwd(x, g, name):
    S = x.shape[0]
    tm = min(S, TM_ROW)

    def body(x_ref, g_ref, o_ref):
        xf = x_ref[...]
        r = lax.rsqrt(jnp.mean(xf * xf, axis=-1, keepdims=True) + EPS)
        o_ref[...] = ((xf * r) * g_ref[...]).astype(BF16)

    return pl.pallas_call(body, out_shape=SDS((S, D), BF16), grid=(S // tm,), in_specs=[_row(tm, D), _full((1, D))],
                          out_specs=_row(tm, D), name=name, compiler_params=_cp(1))(x, g)


def _rms_bwd(x, g, dn, dres, name):
    S = x.shape[0]
    tm = min(S, TM_ROW)
    want_dx = dres is not None

    def body(x_ref, g_ref, dn_ref, *rest):
        i = pl.program_id(0)
        dg_ref = rest[-1]

        @pl.when(i == 0)
        def _():
            dg_ref[...] = jnp.zeros_like(dg_ref)

        xf = x_ref[...]
        r = lax.rsqrt(jnp.mean(xf * xf, axis=-1, keepdims=True) + EPS)
        y = xf * r
        dn_v = dn_ref[...]
        dg_ref[...] += jnp.sum(dn_v * y, axis=0, keepdims=True)
        if want_dx:
            dres_ref, dx_ref, dxb_ref = rest[0], rest[1], rest[2]
            dy = dn_v * g_ref[...]
            dx = r * (dy - y * jnp.mean(dy * y, axis=-1, keepdims=True)) + dres_ref[...]
            dx_ref[...] = dx
            dxb_ref[...] = dx.astype(BF16)

    ins = [x, g, dn] + ([dres] if want_dx else [])
    in_specs = [_row(tm, D), _full((1, D)), _row(tm, D)] + ([_row(tm, D)] if want_dx else [])
    outs = ([SDS((S, D), F32), SDS((S, D), BF16)] if want_dx else []) + [SDS((1, D), F32)]
    out_specs = ([_row(tm, D), _row(tm, D)] if want_dx else []) + [_full((1, D))]
    return pl.pallas_call(body, out_shape=outs, grid=(S // tm,), in_specs=in_specs, out_specs=out_specs, name=name,
                          compiler_params=_cp(1))(*ins)


def _loss_head(x, g, tgt):
    S = x.shape[0]
    tm = min(S, TM_ROW)

    def body(x_ref, g_ref, t_ref, loss_ref, dx_ref, dxb_ref, dg_ref):
        i = pl.program_id(0)

        @pl.when(i == 0)
        def _():
            loss_ref[...] = jnp.zeros_like(loss_ref)
            dg_ref[...] = jnp.zeros_like(dg_ref)

        xf = x_ref[...]
        r = lax.rsqrt(jnp.mean(xf * xf, axis=-1, keepdims=True) + EPS)
        y = xf * r
        gv = g_ref[...]
        err = y * gv - t_ref[...]
        per_row = jnp.mean(err * err, axis=-1, keepdims=True)
        loss_ref[...] += 0.5 * jnp.sum(per_row, axis=0, keepdims=True)
        dn_v = err * (1.0 / D)
        dg_ref[...] += jnp.sum(dn_v * y, axis=0, keepdims=True)
        dy = dn_v * gv
        dx = r * (dy - y * jnp.mean(dy * y, axis=-1, keepdims=True))
        dx_ref[...] = dx
        dxb_ref[...] = dx.astype(BF16)

    return pl.pallas_call(
        body, out_shape=[SDS((1, 1), F32), SDS((S, D), F32), SDS((S, D), BF16), SDS((1, D), F32)], grid=(S // tm,),
        in_specs=[_row(tm, D), _full((1, D)), _row(tm, D)],
        out_specs=[_full((1, 1)), _row(tm, D), _row(tm, D), _full((1, D))], name="loss_head", compiler_params=_cp(1),
    )(x, g, tgt)


def _softmax_rows(s):
    m = jnp.max(s, axis=-1, keepdims=True)
    e = jnp.exp(s - m)
    return e / jnp.sum(e, axis=-1, keepdims=True)


def _attn_fwd(q, k, v, name):
    S = q.shape[0]
    tm = min(S, TM_ROW)
    scale = XA_HD ** -0.5

    def body(q_ref, k_ref, v_ref, o_ref):
        for h in range(XA_HEADS):
            sl = slice(h * XA_HD, (h + 1) * XA_HD)
            s = lax.dot_general(q_ref[:, sl], k_ref[:, sl], _NT, preferred_element_type=F32) * scale
            p = _softmax_rows(s)
            o_ref[:, sl] = lax.dot_general(p.astype(BF16), v_ref[:, sl], _NN, preferred_element_type=F32).astype(BF16)

    return pl.pallas_call(body, out_shape=SDS((S, D), BF16), grid=(S // tm,),
                          in_specs=[_row(tm, D), _full((N_MEM, D)), _full((N_MEM, D))], out_specs=_row(tm, D),
                          name=name, compiler_params=_cp(1))(q, k, v)


def _attn_bwd(q, k, v, do, name):
    S = q.shape[0]
    tm = min(S, TM_ROW)
    scale = XA_HD ** -0.5

    def body(q_ref, k_ref, v_ref, do_ref, dq_ref, dk_ref, dv_ref):
        i = pl.program_id(0)

        @pl.when(i == 0)
        def _():
            dk_ref[...] = jnp.zeros_like(dk_ref)
            dv_ref[...] = jnp.zeros_like(dv_ref)

        for h in range(XA_HEADS):
            sl = slice(h * XA_HD, (h + 1) * XA_HD)
            qh, kh, vh, doh = q_ref[:, sl], k_ref[:, sl], v_ref[:, sl], do_ref[:, sl]
            s = lax.dot_general(qh, kh, _NT, preferred_element_type=F32) * scale
            p = _softmax_rows(s)
            pb = p.astype(BF16)
            dv_ref[:, sl] += lax.dot_general(pb, doh, _TN, preferred_element_type=F32)
            dp = lax.dot_general(doh, vh, _NT, preferred_element_type=F32)
            ds = (p * (dp - jnp.sum(dp * p, axis=-1, keepdims=True)) * scale).astype(BF16)
            dq_ref[:, sl] = lax.dot_general(ds, kh, _NN, preferred_element_type=F32).astype(BF16)
            dk_ref[:, sl] += lax.dot_general(ds, qh, _TN, preferred_element_type=F32)

    return pl.pallas_call(
        body, out_shape=[SDS((S, D), BF16), SDS((N_MEM, D), F32), SDS((N_MEM, D), F32)], grid=(S // tm,),
        in_specs=[_row(tm, D), _full((N_MEM, D)), _full((N_MEM, D)), _row(tm, D)],
        out_specs=[_row(tm, D), _full((N_MEM, D)), _full((N_MEM, D))], name=name, compiler_params=_cp(1),
    )(q, k, v, do)


def _sigmoid(x):
    return 1.0 / (1.0 + jnp.exp(-x))


def _ln_silu_fwd(cv, g, b):
    S = cv.shape[0]
    tm = min(S, TM_ROW)

    def body(x_ref, g_ref, b_ref, o_ref):
        xf = x_ref[...]
        mu = jnp.mean(xf, axis=-1, keepdims=True)
        xc = xf - mu
        rstd = lax.rsqrt(jnp.mean(xc * xc, axis=-1, keepdims=True) + EPS)
        ln = (xc * rstd) * g_ref[...] + b_ref[...]
        o_ref[...] = (ln * _sigmoid(ln)).astype(BF16)

    return pl.pallas_call(body, out_shape=SDS((S, D), BF16), grid=(S // tm,),
                          in_specs=[_row(tm, D), _full((1, D)), _full((1, D))], out_specs=_row(tm, D),
                          name="ln_silu_fwd", compiler_params=_cp(1))(cv, g, b)


def _ln_silu_bwd(ds, cv, g, b, dx):
    S = cv.shape[0]
    tm = min(S, TM_ROW)

    def body(ds_ref, x_ref, g_ref, b_ref, dx_ref, dcv_ref, dg_ref, db_ref, db2_ref):
        i = pl.program_id(0)

        @pl.when(i == 0)
        def _():
            dg_ref[...] = jnp.zeros_like(dg_ref)
            db_ref[...] = jnp.zeros_like(db_ref)
            db2_ref[...] = jnp.zeros_like(db2_ref)

        xf = x_ref[...]
        mu = jnp.mean(xf, axis=-1, keepdims=True)
        xc = xf - mu
        rstd = lax.rsqrt(jnp.mean(xc * xc, axis=-1, keepdims=True) + EPS)
        xhat = xc * rstd
        gv = g_ref[...]
        ln = xhat * gv + b_ref[...]
        sg = _sigmoid(ln)
        dln = ds_ref[...].astype(F32) * (sg + ln * sg * (1.0 - sg))
        dg_ref[...] += jnp.sum(dln * xhat, axis=0, keepdims=True)
        db_ref[...] += jnp.sum(dln, axis=0, keepdims=True)
        db2_ref[...] += jnp.sum(dx_ref[...], axis=0, keepdims=True)
        dxh = dln * gv
        dcv_ref[...] = rstd * (dxh - jnp.mean(dxh, axis=-1, keepdims=True)
                               - xhat * jnp.mean(dxh * xhat, axis=-1, keepdims=True))

    return pl.pallas_call(
        body, out_shape=[SDS((S, D), F32), SDS((1, D), F32), SDS((1, D), F32), SDS((1, D), F32)], grid=(S // tm,),
        in_specs=[_row(tm, D), _row(tm, D), _full((1, D)), _full((1, D)), _row(tm, D)],
        out_specs=[_row(tm, D), _full((1, D)), _full((1, D)), _full((1, D))], name="ln_silu_bwd",
        compiler_params=_cp(1),
    )(ds, cv, g, b, dx)


_GELU_C, _GELU_K = 0.7978845608028654, 0.044715


def _gelu(x, with_grad=False):
    x2 = x * x
    t = jnp.tanh(_GELU_C * (x + _GELU_K * x * x2))
    gel = 0.5 * x * (1.0 + t)
    if not with_grad:
        return gel
    return gel, 0.5 * (1.0 + t) + 0.5 * x * (1.0 - t * t) * (_GELU_C * (1.0 + 3.0 * _GELU_K * x2))


def _expm1(x):
    poly = x * (1.0 + x * (0.5 + x * (1.0 / 6.0 + x * (1.0 / 24.0 + x * (1.0 / 120.0)))))
    return jnp.where(jnp.abs(x) < 0.05, poly, jnp.exp(x) - 1.0)


def _softplus(x):
    return jnp.maximum(x, 0.0) + jnp.log1p(jnp.exp(-jnp.abs(x)))


_SCAN_UNROLL = 8
_RB = 32
_HB = 16


def _sub_blocks(n_rows, n_lanes, fn):
    def step(idx, c):
        r0 = pl.multiple_of(idx * _RB, _RB)
        for lt in range(n_lanes // LANE):
            fn(r0, lt)
        return c

    lax.fori_loop(0, n_rows // _RB, step, 0)


def _lanes(lt):
    return pl.ds(lt * LANE, LANE)


def _psum8(x):
    parts = [x[i * SUB:(i + 1) * SUB] for i in range(x.shape[0] // SUB)]
    return functools.reduce(lambda p, q: p + q, parts)


def _scan_fwd(a_s, b_s, out_ref, carry_ref, n_groups):
    row = lax.broadcasted_iota(jnp.int32, (SUB, LANE), 0)
    U = _SCAN_UNROLL

    def step(gi, carry):
        base = gi * (SUB * U)
        parts = []
        for u in range(U):
            i = pl.multiple_of(base + u * SUB, SUB)
            a8, b8 = a_s[pl.ds(i, SUB), :], b_s[pl.ds(i, SUB), :]
            for s in (1, 2, 4):
                a_sh = jnp.where(row >= s, pltpu.roll(a8, s, 0), 1.0)
                b_sh = jnp.where(row >= s, pltpu.roll(b8, s, 0), 0.0)
                b8 = a8 * b_sh + b8
                a8 = a8 * a_sh
            parts.append((i, a8, b8))
        for i, a8, b8 in parts:
            h8 = a8 * carry + b8
            out_ref[pl.ds(i, SUB), :] = h8
            carry = jnp.broadcast_to(h8[SUB - 1:SUB, :], (SUB, LANE))
        return carry

    carry_ref[...] = lax.fori_loop(0, n_groups // U, step, carry_ref[...])


def _scan_bwd(a_s, b_s, out_ref, carry_ref, n_groups):
    row = lax.broadcasted_iota(jnp.int32, (SUB, LANE), 0)
    U = _SCAN_UNROLL

    def step(gi, carry):
        base = (n_groups // U - 1 - gi) * (SUB * U)
        parts = []
        for u in reversed(range(U)):
            i = pl.multiple_of(base + u * SUB, SUB)
            a8, b8 = a_s[pl.ds(i, SUB), :], b_s[pl.ds(i, SUB), :]
            for s in (1, 2, 4):
                a_sh = jnp.where(row < SUB - s, pltpu.roll(a8, SUB - s, 0), 1.0)
                b_sh = jnp.where(row < SUB - s, pltpu.roll(b8, SUB - s, 0), 0.0)
                b8 = a8 * b_sh + b8
                a8 = a8 * a_sh
            parts.append((i, a8, b8))
        for i, a8, b8 in parts:
            h8 = a8 * carry + b8
            out_ref[pl.ds(i, SUB), :] = h8
            carry = jnp.broadcast_to(h8[0:1, :], (SUB, LANE))
        return carry

    carry_ref[...] = lax.fori_loop(0, n_groups // U, step, carry_ref[...])


def _rglru_pre(xr, wgx_ref, bgx_ref, wga_ref, bga_ref, lam_ref):
    xrb = xr.astype(BF16)
    wgx, wga = wgx_ref[0].astype(BF16), wga_ref[0].astype(BF16)
    gx = _sigmoid(lax.dot_general(xrb, wgx, _NN, preferred_element_type=F32) + bgx_ref[...])
    ga = _sigmoid(lax.dot_general(xrb, wga, _NN, preferred_element_type=F32) + bga_ref[...])
    sp = _softplus(-lam_ref[...])
    log_a = -C_RG * ga * sp
    a = jnp.exp(log_a)
    mult = jnp.sqrt(-_expm1(2.0 * log_a))
    return gx, ga, sp, a, mult, xrb, wgx, wga


def _a_specs():
    vec = pl.BlockSpec((1, HD_A), lambda c, j: (0, c))
    mat = pl.BlockSpec((1, HD_A, HD_A), lambda c, j: (c, 0, 0))
    return [pl.BlockSpec((CONV_A, HD_A), lambda c, j: (0, c)), vec, mat, vec, mat, vec, vec]


def _a_fwd(zp, conv_w, conv_b, wgx, bgx, wga, bga, lam):
    S = zp.shape[0]
    R, nt = R_RGLRU, D // HD_A
    H = SUB

    def body(zg_ref, zr_ref, cw_ref, cb_ref, wgx_ref, bgx_ref, wga_ref, bga_ref, lam_ref, ya_ref, h_ref,
             ext, a_s, b_s, hc):
        j = pl.program_id(1)

        @pl.when(j == 0)
        def _():
            ext[0:H, :] = jnp.zeros((H, HD_A), F32)
            hc[...] = jnp.zeros_like(hc)

        ext[H:H + R, :] = zr_ref[...].astype(F32)
        xr = cb_ref[...]
        for k in range(CONV_A):
            xr = xr + cw_ref[k:k + 1, :] * ext[pl.ds(H - (CONV_A - 1 - k), R), :]
        gx, _, _, a, mult, _, _, _ = _rglru_pre(xr, wgx_ref, bgx_ref, wga_ref, bga_ref, lam_ref)
        a_s[...] = a
        b_s[...] = mult * (gx * xr)
        _scan_fwd(a_s, b_s, h_ref, hc, R // SUB)
        ya_ref[...] = (_gelu(zg_ref[...].astype(F32)) * h_ref[...]).astype(BF16)
        ext[0:H, :] = ext[R:R + H, :]

    return pl.pallas_call(
        body, out_shape=[SDS((S, D + D // 2), BF16), SDS((S, D), F32)], grid=(nt, S // R),
        in_specs=[pl.BlockSpec((R, HD_A), lambda c, j: (j, c)), pl.BlockSpec((R, HD_A), lambda c, j: (j, nt + c))]
        + _a_specs(),
        out_specs=[pl.BlockSpec((R, HD_A), lambda c, j: (j, c)), pl.BlockSpec((R, HD_A), lambda c, j: (j, c))],
        scratch_shapes=[pltpu.VMEM((H + R, HD_A), F32), pltpu.VMEM((R, HD_A), F32), pltpu.VMEM((R, HD_A), F32),
                        pltpu.VMEM((SUB, HD_A), F32)],
        name="rglru_fwd", compiler_params=_cp(2),
    )(zp, zp, conv_w, conv_b, wgx, bgx, wga, bga, lam)


def _a_bwd(dyab, zp, h, conv_w, conv_b, wgx, bgx, wga, bga, lam):
    S = zp.shape[0]
    R, nt, nch = R_RGLRU, D // HD_A, S // R_RGLRU
    H = SUB

    def rows(c, j):
        return (nch - 1 - j, c)

    def rows_rec(c, j):
        return (nch - 1 - j, nt + c)

    def halo(c, j):
        return (jnp.maximum((nch - 1 - j) * (R // H) - 1, 0), c)

    def halo_z(c, j):
        return (jnp.maximum((nch - 1 - j) * (R // _HB) - 1, 0), nt + c)

    def body(dy_ref, zg_ref, zr_ref, zh_ref, h_ref, hh_ref, cw_ref, cb_ref, wgx_ref, bgx_ref, wga_ref, bga_ref,
             lam_ref, dzg_ref, dzr_ref, dcw_ref, dcb_ref, dwgx_ref, dbgx_ref, dwga_ref, dbga_ref, dlam_ref,
             ext_z, ext_h, ext_mu, ext_d, a_s, b_s, muc):
        j = pl.program_id(1)
        first_chunk = (nch - 1 - j) == 0

        @pl.when(j == 0)
        def _():
            ext_mu[R:R + H, :] = jnp.zeros((H, HD_A), F32)
            ext_d[R:R + H, :] = jnp.zeros((H, HD_A), F32)
            muc[...] = jnp.zeros_like(muc)
            for r in (dcw_ref, dcb_ref, dwgx_ref, dbgx_ref, dwga_ref, dbga_ref, dlam_ref):
                r[...] = jnp.zeros_like(r)

        zg = zg_ref[...].astype(F32)
        ext_z[0:H, :] = jnp.where(first_chunk, 0.0, zh_ref[_HB - H:_HB, :].astype(F32))
        ext_z[H:H + R, :] = zr_ref[...].astype(F32)
        ext_h[0:H, :] = jnp.where(first_chunk, 0.0, hh_ref[...])
        ext_h[H:H + R, :] = h_ref[...]
        xr = cb_ref[...]
        for k in range(CONV_A):
            xr = xr + cw_ref[k:k + 1, :] * ext_z[pl.ds(H - (CONV_A - 1 - k), R), :]
        gx, ga, sp, a, mult, xrb, wgxb, wgab = _rglru_pre(xr, wgx_ref, bgx_ref, wga_ref, bga_ref, lam_ref)
        gel, dgel = _gelu(zg, with_grad=True)
        dy = dy_ref[...].astype(F32)
        dh = dy * gel
        dzg_ref[...] = (dy * h_ref[...] * dgel).astype(BF16)
        a_s[...] = a
        b_s[...] = a * dh
        _scan_bwd(a_s, b_s, ext_mu, muc, R // SUB)
        lam_t = dh + ext_mu[pl.ds(1, R), :]
        ext_mu[R:R + H, :] = ext_mu[0:H, :]
        da = lam_t * ext_h[pl.ds(H - 1, R), :]
        gxr = gx * xr
        dlog_a = da * a - (lam_t * gxr) * (a * a) / mult
        dgx = lam_t * mult * xr
        dxr = lam_t * mult * gx
        lam_v = lam_ref[...]
        dlam_ref[...] += jnp.sum(dlog_a * ga, axis=0, keepdims=True) * (C_RG * _sigmoid(-lam_v))
        dpa = (dlog_a * (-C_RG * sp)) * ga * (1.0 - ga)
        dpx = dgx * gx * (1.0 - gx)
        dbga_ref[...] += jnp.sum(dpa, axis=0, keepdims=True)
        dbgx_ref[...] += jnp.sum(dpx, axis=0, keepdims=True)
        dpab, dpxb = dpa.astype(BF16), dpx.astype(BF16)
        dwga_ref[0] += lax.dot_general(xrb, dpab, _TN, preferred_element_type=F32)
        dwgx_ref[0] += lax.dot_general(xrb, dpxb, _TN, preferred_element_type=F32)
        dxr = (dxr + lax.dot_general(dpab, wgab, _NT, preferred_element_type=F32)
               + lax.dot_general(dpxb, wgxb, _NT, preferred_element_type=F32))
        dcb_ref[...] += jnp.sum(dxr, axis=0, keepdims=True)
        ext_d[0:R, :] = dxr
        dzr = jnp.zeros((R, HD_A), F32)
        for k in range(CONV_A):
            sh = CONV_A - 1 - k
            dcw_ref[k:k + 1, :] += jnp.sum(dxr * ext_z[pl.ds(H - sh, R), :], axis=0, keepdims=True)
            dzr = dzr + cw_ref[k:k + 1, :] * ext_d[pl.ds(sh, R), :]
        dzr_ref[...] = dzr.astype(BF16)
        ext_d[R:R + H, :] = ext_d[0:H, :]

    vec_o = pl.BlockSpec((1, HD_A), lambda c, j: (0, c))
    mat_o = pl.BlockSpec((1, HD_A, HD_A), lambda c, j: (c, 0, 0))
    return pl.pallas_call(
        body,
        out_shape=[SDS((S, D), BF16), SDS((S, D), BF16), SDS((CONV_A, D), F32), SDS((1, D), F32),
                   SDS((nt, HD_A, HD_A), F32), SDS((1, D), F32), SDS((nt, HD_A, HD_A), F32), SDS((1, D), F32),
                   SDS((1, D), F32)],
        grid=(nt, nch),
        in_specs=[pl.BlockSpec((R, HD_A), rows), pl.BlockSpec((R, HD_A), rows), pl.BlockSpec((R, HD_A), rows_rec),
                  pl.BlockSpec((_HB, HD_A), halo_z), pl.BlockSpec((R, HD_A), rows),
                  pl.BlockSpec((H, HD_A), halo)] + _a_specs(),
        out_specs=[pl.BlockSpec((R, HD_A), rows), pl.BlockSpec((R, HD_A), rows),
                   pl.BlockSpec((CONV_A, HD_A), lambda c, j: (0, c)), vec_o, mat_o, vec_o, mat_o, vec_o, vec_o],
        scratch_shapes=[pltpu.VMEM((H + R, HD_A), F32), pltpu.VMEM((H + R, HD_A), F32), pltpu.VMEM((R + H, HD_A), F32),
                        pltpu.VMEM((R + H, HD_A), F32), pltpu.VMEM((R, HD_A), F32), pltpu.VMEM((R, HD_A), F32),
                        pltpu.VMEM((SUB, HD_A), F32)],
        name="rglru_bwd", compiler_params=_cp(2),
    )(dyab, zp, zp, zp, h, h, conv_w, conv_b, wgx, bgx, wga, bga, lam)


_POOL_H = 16
_POOL_T0 = 2 * D // HD_A
_POOL_Y0 = D // HD_A


def _window_sum(lv, n, lo, rows, g, ahead):
    base = 0 if ahead else SUB
    cur, win = lv[0], None
    for i, s in enumerate((1, 2, 4, 8)):
        val = cur[pl.ds(base, n), :] + cur[pl.ds(base + (s if ahead else -s), n), :]
        sel = val[lo:lo + rows]
        win = sel if win is None else jnp.where(g >= i, sel, win)
        if i < 3:
            lv[i + 1][pl.ds(base, n), :] = val
            cur = lv[i + 1]
    return win


def _pool_width(g):
    return jnp.where(g == 0, 2.0, jnp.where(g == 1, 4.0, jnp.where(g == 2, 8.0, 16.0)))


def _b_fwd(zp, yab, wg, bg, sc):
    S = zp.shape[0]
    R, H = min(S, R_POOL), _POOL_H

    def body(z_ref, wg_ref, bg_ref, sc_ref, yab_in, yb_ref, *lv):
        del yab_in
        g, j = pl.program_id(0), pl.program_id(1)

        @pl.when(j == 0)
        def _():
            for r in lv:
                r[0:SUB, :] = jnp.zeros((SUB, HD_A), F32)
            lv[0][SUB:SUB + H, :] = jnp.zeros((H, HD_A), F32)

        u = z_ref[...].astype(F32)
        lv[0][SUB + H:SUB + H + R, :] = u
        t1 = (j * R + 1 + lax.broadcasted_iota(jnp.int32, (R, HD_A), 0)).astype(F32)
        p = _window_sum(lv, H + R, H, R, g, False) / jnp.minimum(t1, _pool_width(g)) - u
        lin = lax.dot_general(p.astype(BF16), wg_ref[0].astype(BF16), _NN, preferred_element_type=F32) + bg_ref[...]
        yb_ref[...] = (lin * sc_ref[...]).astype(BF16)
        lv[0][SUB:SUB + H, :] = lv[0][SUB + R:SUB + R + H, :]

    vec = pl.BlockSpec((1, HD_A), lambda g, j: (0, g))
    return pl.pallas_call(
        body, out_shape=SDS(yab.shape, yab.dtype), grid=(len(POOL_WINDOWS), S // R),
        in_specs=[pl.BlockSpec((R, HD_A), lambda g, j: (j, _POOL_T0 + g)),
                  pl.BlockSpec((1, HD_A, HD_A), lambda g, j: (g, 0, 0)), vec, vec, pl.BlockSpec(memory_space=pl.ANY)],
        out_specs=pl.BlockSpec((R, HD_A), lambda g, j: (j, _POOL_Y0 + g)),
        scratch_shapes=[pltpu.VMEM((SUB + H + R, HD_A), F32)] * 4, input_output_aliases={4: 0},
        name="pool_fwd", compiler_params=_cp(2),
    )(zp, wg, bg, sc, yab)


def _b_bwd(dyab, zp, wg, bg, sc):
    S = zp.shape[0]
    R, H, ng = min(S, R_POOL), _POOL_H, len(POOL_WINDOWS)
    nch = S // R

    def body(dy_ref, z_ref, zh_ref, wg_ref, bg_ref, sc_ref, dz_ref, dwg_ref, dbg_ref, dsc_ref, *scratch):
        lu, lq = scratch[:4], scratch[4:]
        g, j = pl.program_id(0), pl.program_id(1)
        jj = nch - 1 - j

        @pl.when(j == 0)
        def _():
            for r in lu:
                r[0:SUB, :] = jnp.zeros((SUB, HD_A), F32)
            for r in lq:
                r[R + H:R + H + SUB, :] = jnp.zeros((SUB, HD_A), F32)
            lq[0][R:R + H, :] = jnp.zeros((H, HD_A), F32)
            for r in (dwg_ref, dbg_ref, dsc_ref):
                r[...] = jnp.zeros_like(r)

        u = z_ref[...].astype(F32)
        lu[0][SUB:SUB + H, :] = jnp.where(jj == 0, 0.0, zh_ref[...].astype(F32))
        lu[0][SUB + H:SUB + H + R, :] = u
        t1 = (jj * R + 1 + lax.broadcasted_iota(jnp.int32, (R, HD_A), 0)).astype(F32)
        cnt = jnp.minimum(t1, _pool_width(g))
        pb = (_window_sum(lu, H + R, H, R, g, False) / cnt - u).astype(BF16)
        wgb = wg_ref[0].astype(BF16)
        lin = lax.dot_general(pb, wgb, _NN, preferred_element_type=F32) + bg_ref[...]
        dy = dy_ref[...].astype(F32)
        dsc_ref[...] += jnp.sum(dy * lin, axis=0, keepdims=True)
        dlin = dy * sc_ref[...]
        dbg_ref[...] += jnp.sum(dlin, axis=0, keepdims=True)
        dlb = dlin.astype(BF16)
        dwg_ref[0] += lax.dot_general(pb, dlb, _TN, preferred_element_type=F32)
        dp = lax.dot_general(dlb, wgb, _NT, preferred_element_type=F32)
        lq[0][0:R, :] = dp / cnt
        dz_ref[...] = (_window_sum(lq, R + H, 0, R, g, True) - dp).astype(BF16)
        lq[0][R:R + H, :] = lq[0][0:H, :]

    vec = pl.BlockSpec((1, HD_A), lambda g, j: (0, g))
    mat = pl.BlockSpec((1, HD_A, HD_A), lambda g, j: (g, 0, 0))
    return pl.pallas_call(
        body, out_shape=[SDS((S, D // 2), BF16), SDS((ng, HD_A, HD_A), F32), SDS((1, D // 2), F32),
                         SDS((1, D // 2), F32)],
        grid=(ng, nch),
        in_specs=[pl.BlockSpec((R, HD_A), lambda g, j: (nch - 1 - j, _POOL_Y0 + g)),
                  pl.BlockSpec((R, HD_A), lambda g, j: (nch - 1 - j, _POOL_T0 + g)),
                  pl.BlockSpec((H, HD_A), lambda g, j: (jnp.maximum((nch - 1 - j) * (R // H) - 1, 0), _POOL_T0 + g)),
                  mat, vec, vec],
        out_specs=[pl.BlockSpec((R, HD_A), lambda g, j: (nch - 1 - j, g)), mat, vec, vec],
        scratch_shapes=[pltpu.VMEM((SUB + H + R, HD_A), F32)] * 8,
        name="pool_bwd", compiler_params=_cp(2),
    )(dyab, zp, zp, wg, bg, sc)


_CW_F = 768


def _f_fwd(hp, w, b, name):
    S = hp.shape[0]
    R, H, cw = min(S, R_FFN), SUB, _CW_F
    nlt = cw // LANE

    def body(h_ref, w_ref, b_ref, o_ref, gel_ref, ud_ref, ext):
        j = pl.program_id(1)

        @pl.when(j == 0)
        def _():
            ext[:, 0:H, :] = jnp.zeros((nlt, H, LANE), F32)

        def stage(r0, lt):
            ext[lt, pl.ds(pl.multiple_of(r0 + H, SUB), _RB), :] = h_ref[pl.ds(r0, _RB), _lanes(lt)].astype(F32)

        def main(r0, lt):
            ls = _lanes(lt)
            gp = b_ref[:, ls]
            for k in range(CONV_F):
                gp = gp + w_ref[k:k + 1, ls] * ext[lt, pl.ds(r0 + (H - (CONV_F - 1 - k)), _RB), :]
            up = h_ref[pl.ds(r0, _RB), _lanes(lt + nlt)].astype(F32)
            gel, dgel = _gelu(gp, with_grad=True)
            rs = pl.ds(r0, _RB)
            o_ref[rs, ls] = (gel * up).astype(BF16)
            gel_ref[rs, ls] = gel.astype(BF16)
            ud_ref[rs, ls] = (up * dgel).astype(BF16)

        _sub_blocks(R, cw, stage)
        _sub_blocks(R, cw, main)
        ext[:, 0:H, :] = ext[:, R:R + H, :]

    tile = pl.BlockSpec((R, cw), lambda c, j: (j, c))
    return pl.pallas_call(
        body, out_shape=[SDS((S, D_FF), BF16)] * 3, grid=(D_FF // cw, S // R),
        in_specs=[pl.BlockSpec((R, 2 * cw), lambda c, j: (j, c)), pl.BlockSpec((CONV_F, cw), lambda c, j: (0, c)),
                  pl.BlockSpec((1, cw), lambda c, j: (0, c))],
        out_specs=[tile] * 3,
        scratch_shapes=[pltpu.VMEM((nlt, H + R, LANE), F32)], name=name, compiler_params=_cp(2),
    )(hp, w, b)


def _f_bwd(dact, hp, gel, ud, w, name):
    S = hp.shape[0]
    R, H, cw = min(S, R_FFN), SUB, _CW_F
    nch = S // R
    nlt = cw // LANE

    def body(da_ref, h_ref, hh_ref, gel_ref, ud_ref, w_ref, dh_ref, dw_ref, db_ref, ext_g, ext_d, acc):
        j = pl.program_id(1)
        jj = nch - 1 - j

        @pl.when(j == 0)
        def _():
            ext_d[:, R:R + H, :] = jnp.zeros((nlt, H, LANE), F32)
            acc[...] = jnp.zeros_like(acc)

        for lt in range(nlt):
            ext_g[lt, 0:H, :] = jnp.where(jj == 0, 0.0, hh_ref[_HB - H:_HB, lt * LANE:(lt + 1) * LANE].astype(F32))

        def stage(r0, lt):
            ext_g[lt, pl.ds(pl.multiple_of(r0 + H, SUB), _RB), :] = h_ref[pl.ds(r0, _RB), _lanes(lt)].astype(F32)

        def first(r0, lt):
            ls, lu, rs = _lanes(lt), _lanes(lt + nlt), pl.ds(r0, _RB)
            da = da_ref[rs, ls].astype(F32)
            dh_ref[rs, lu] = (da * gel_ref[rs, ls].astype(F32)).astype(BF16)
            dgp = da * ud_ref[rs, ls].astype(F32)
            ext_d[lt, rs, :] = dgp
            acc[CONV_F * SUB:(CONV_F + 1) * SUB, ls] += _psum8(dgp)
            for k in range(CONV_F):
                tap = ext_g[lt, pl.ds(r0 + (H - (CONV_F - 1 - k)), _RB), :]
                acc[k * SUB:(k + 1) * SUB, ls] += _psum8(dgp * tap)

        def second(r0, lt):
            ls = _lanes(lt)
            dhg = w_ref[CONV_F - 1:CONV_F, ls] * ext_d[lt, pl.ds(r0, _RB), :]
            for k in range(CONV_F - 1):
                dhg = dhg + w_ref[k:k + 1, ls] * ext_d[lt, pl.ds(r0 + (CONV_F - 1 - k), _RB), :]
            dh_ref[pl.ds(r0, _RB), ls] = dhg.astype(BF16)

        _sub_blocks(R, cw, stage)
        _sub_blocks(R, cw, first)
        _sub_blocks(R, cw, second)
        ext_d[:, R:R + H, :] = ext_d[:, 0:H, :]

        @pl.when(j == nch - 1)
        def _():
            for k in range(CONV_F):
                dw_ref[k:k + 1, :] = jnp.sum(acc[k * SUB:(k + 1) * SUB, :], axis=0, keepdims=True)
            db_ref[...] = jnp.sum(acc[CONV_F * SUB:(CONV_F + 1) * SUB, :], axis=0, keepdims=True)

    rows = lambda c, j: (nch - 1 - j, c)
    return pl.pallas_call(
        body, out_shape=[SDS((S, 2 * D_FF), BF16), SDS((CONV_F, D_FF), F32), SDS((1, D_FF), F32)],
        grid=(D_FF // cw, nch),
        in_specs=[pl.BlockSpec((R, cw), rows), pl.BlockSpec((R, cw), lambda c, j: (nch - 1 - j, 2 * c)),
                  pl.BlockSpec((_HB, cw), lambda c, j: (jnp.maximum((nch - 1 - j) * (R // _HB) - 1, 0), 2 * c)),
                  pl.BlockSpec((R, cw), rows), pl.BlockSpec((R, cw), rows),
                  pl.BlockSpec((CONV_F, cw), lambda c, j: (0, c))],
        out_specs=[pl.BlockSpec((R, 2 * cw), rows), pl.BlockSpec((CONV_F, cw), lambda c, j: (0, c)),
                   pl.BlockSpec((1, cw), lambda c, j: (0, c))],
        scratch_shapes=[pltpu.VMEM((nlt, H + R, LANE), F32), pltpu.VMEM((nlt, R + H, LANE), F32),
                        pltpu.VMEM(((CONV_F + 1) * SUB, cw), F32)], name=name,
        compiler_params=_cp(2),
    )(dact, hp, hp, gel, ud, w)


_CW_C = 256
_H_C = 32


def _c_fwd(h1p, w, b):
    S = h1p.shape[0]
    R, H, cw = R_SEQ, _H_C, _CW_C
    nlt = cw // LANE

    def body(h_ref, w_ref, b_ref, o_ref, ext):
        j = pl.program_id(1)

        @pl.when(j == 0)
        def _():
            ext[:, 0:H, :] = jnp.zeros((nlt, H, LANE), F32)

        def stage(r0, lt):
            rs = pl.ds(r0, _RB)
            gate = h_ref[rs, _lanes(lt + nlt)].astype(F32)
            ext[lt, pl.ds(pl.multiple_of(r0 + H, SUB), _RB), :] = h_ref[rs, _lanes(lt)].astype(F32) * _sigmoid(gate)

        def main(r0, lt):
            ls = _lanes(lt)
            cv = b_ref[:, ls]
            for k in range(CONV_C):
                cv = cv + w_ref[k:k + 1, ls] * ext[lt, pl.ds(r0 + (H - (CONV_C - 1 - k)), _RB), :]
            o_ref[pl.ds(r0, _RB), ls] = cv

        _sub_blocks(R, cw, stage)
        _sub_blocks(R, cw, main)
        ext[:, 0:H, :] = ext[:, R:R + H, :]

    return pl.pallas_call(
        body, out_shape=SDS((S, D), F32), grid=(D // cw, S // R),
        in_specs=[pl.BlockSpec((R, 2 * cw), lambda c, j: (j, c)), pl.BlockSpec((CONV_C, cw), lambda c, j: (0, c)),
                  pl.BlockSpec((1, cw), lambda c, j: (0, c))],
        out_specs=pl.BlockSpec((R, cw), lambda c, j: (j, c)),
        scratch_shapes=[pltpu.VMEM((nlt, H + R, LANE), F32)], name="conf_conv_fwd", compiler_params=_cp(2),
    )(h1p, w, b)


def _c_bwd(dcv, h1p, w):
    S = h1p.shape[0]
    R, H, cw, nch = R_SEQ, _H_C, _CW_C, S // R_SEQ
    nlt = cw // LANE
    a_b, a_val, a_gate = CONV_C * SUB, (CONV_C + 1) * SUB, (CONV_C + 2) * SUB

    def body(dc_ref, h_ref, hh_ref, w_ref, dh_ref, dw_ref, db_ref, db1_ref, ext_u, ext_d, acc):
        j = pl.program_id(1)
        jj = nch - 1 - j

        @pl.when(j == 0)
        def _():
            ext_d[:, R:R + H, :] = jnp.zeros((nlt, H, LANE), F32)
            acc[...] = jnp.zeros_like(acc)

        for lt in range(nlt):
            ext_u[lt, 0:H, :] = jnp.where(
                jj == 0, 0.0, hh_ref[:, lt * LANE:(lt + 1) * LANE].astype(F32)
                * _sigmoid(hh_ref[:, cw + lt * LANE:cw + (lt + 1) * LANE].astype(F32)))

        def stage(r0, lt):
            rs, ls = pl.ds(r0, _RB), _lanes(lt)
            gate = h_ref[rs, _lanes(lt + nlt)].astype(F32)
            ext_u[lt, pl.ds(pl.multiple_of(r0 + H, SUB), _RB), :] = h_ref[rs, ls].astype(F32) * _sigmoid(gate)
            ext_d[lt, rs, :] = dc_ref[rs, ls]

        def first(r0, lt):
            ls = _lanes(lt)
            dc = dc_ref[pl.ds(r0, _RB), ls]
            acc[a_b:a_b + SUB, ls] += _psum8(dc)
            for k in range(CONV_C):
                tap = ext_u[lt, pl.ds(r0 + (H - (CONV_C - 1 - k)), _RB), :]
                acc[k * SUB:(k + 1) * SUB, ls] += _psum8(dc * tap)

        def second(r0, lt):
            rs, ls, lg = pl.ds(r0, _RB), _lanes(lt), _lanes(lt + nlt)
            du = w_ref[CONV_C - 1:CONV_C, ls] * ext_d[lt, rs, :]
            for k in range(CONV_C - 1):
                du = du + w_ref[k:k + 1, ls] * ext_d[lt, pl.ds(r0 + (CONV_C - 1 - k), _RB), :]
            val = h_ref[rs, ls].astype(F32)
            sg = _sigmoid(h_ref[rs, lg].astype(F32))
            dval = du * sg
            dgate = du * val * sg * (1.0 - sg)
            acc[a_val:a_val + SUB, ls] += _psum8(dval)
            acc[a_gate:a_gate + SUB, ls] += _psum8(dgate)
            dh_ref[rs, ls] = dval.astype(BF16)
            dh_ref[rs, lg] = dgate.astype(BF16)

        _sub_blocks(R, cw, stage)
        _sub_blocks(R, cw, first)
        _sub_blocks(R, cw, second)
        ext_d[:, R:R + H, :] = ext_d[:, 0:H, :]

        @pl.when(j == nch - 1)
        def _():
            for k in range(CONV_C):
                dw_ref[k:k + 1, :] = jnp.sum(acc[k * SUB:(k + 1) * SUB, :], axis=0, keepdims=True)
            db_ref[...] = jnp.sum(acc[a_b:a_b + SUB, :], axis=0, keepdims=True)
            db1_ref[:, 0:cw] = jnp.sum(acc[a_val:a_val + SUB, :], axis=0, keepdims=True)
            db1_ref[:, cw:2 * cw] = jnp.sum(acc[a_gate:a_gate + SUB, :], axis=0, keepdims=True)

    rows = lambda c, j: (nch - 1 - j, c)
    return pl.pallas_call(
        body, out_shape=[SDS((S, 2 * D), BF16), SDS((CONV_C, D), F32), SDS((1, D), F32), SDS((1, 2 * D), F32)],
        grid=(D // cw, nch),
        in_specs=[pl.BlockSpec((R, cw), rows), pl.BlockSpec((R, 2 * cw), rows),
                  pl.BlockSpec((H, 2 * cw), lambda c, j: (jnp.maximum((nch - 1 - j) * (R // H) - 1, 0), c)),
                  pl.BlockSpec((CONV_C, cw), lambda c, j: (0, c))],
        out_specs=[pl.BlockSpec((R, 2 * cw), rows), pl.BlockSpec((CONV_C, cw), lambda c, j: (0, c)),
                   pl.BlockSpec((1, cw), lambda c, j: (0, c)), pl.BlockSpec((1, 2 * cw), lambda c, j: (0, c))],
        scratch_shapes=[pltpu.VMEM((nlt, H + R, LANE), F32), pltpu.VMEM((nlt, R + H, LANE), F32),
                        pltpu.VMEM(((CONV_C + 3) * SUB, cw), F32)], name="conf_conv_bwd",
        compiler_params=_cp(2),
    )(dcv, h1p, h1p, w)


def _local_step(x, mem, tgt, W, fetch=None, send=None):
    G = {}
    W = dict(W)

    def arrive(group, after):
        if fetch is None:
            return None
        got, tok = fetch(group, after)
        for key, val in got.items():
            W[key] = {**W.get(key, {}), **val} if isinstance(val, dict) else val
        return tok

    def gain(g, tok):
        return g if tok is None else g + tok

    def sent(group):
        return None if send is None else send(group, G)

    def xattn_fwd(xin, n, l):
        tok = arrive(("xa", l), n)
        mn = _rms_fwd(mem, gain(W["xa_mem_norm"][l:l + 1], tok), f"xa_memnorm_fwd{l}")
        q = _mm_nn(n, W["xa_wq"][l], out_dtype=BF16, name=f"xa_q{l}")
        k = _mm_nn(mn, W["xa_wk"][l], out_dtype=BF16, name=f"xa_k{l}")
        v = _mm_nn(mn, W["xa_wv"][l], out_dtype=BF16, name=f"xa_v{l}")
        o = _attn_fwd(q, k, v, f"xa_attn_fwd{l}")
        xout, nout = _mm_nn(o, W["xa_wo"][l], out_dtype=F32, name=f"xa_o{l}", add=xin, norm=W["f_norm"][l:l + 1])
        return xout, nout, (xin, n, q, mn, k, v, o)

    def xattn_bwd(dx, dxb, saved, l):
        xin, n, q, mn, k, v, o = saved
        do = _mm_nt(dxb, W["xa_wo"][l], out_dtype=BF16, name=f"xa_do{l}")
        G[f"xa_wo{l}"] = _mm_tn(o, dxb, out_dtype=BF16, name=f"xa_dwo{l}")
        dq, dk, dv = _attn_bwd(q, k, v, do, f"xa_attn_bwd{l}")
        dkb, dvb = dk.astype(BF16), dv.astype(BF16)
        G[f"xa_wq{l}"] = _mm_tn(n, dq, out_dtype=BF16, name=f"xa_dwq{l}")
        G[f"xa_wk{l}"] = _mm_tn(mn, dkb, out_dtype=BF16, name=f"xa_dwk{l}")
        G[f"xa_wv{l}"] = _mm_tn(mn, dvb, out_dtype=BF16, name=f"xa_dwv{l}")
        tok = sent(("xa", l))
        dmn = _mm_nt(dkb, W["xa_wk"][l], out_dtype=F32, name=f"xa_dmn_k{l}")
        dmn = _mm_nt(dvb, W["xa_wv"][l], out_dtype=F32, name=f"xa_dmn_v{l}", add=dmn)
        (G[f"xa_mem_norm{l}"],) = _rms_bwd(mem, W["xa_mem_norm"][l:l + 1], dmn, None, f"xa_memnorm_bwd{l}")
        dx, dxb, G[f"xa_norm{l}"] = _mm_nt(dq, W["xa_wq"][l], out_dtype=F32, name=f"xa_dn{l}",
                                           rms=(xin, gain(W["xa_norm"][l:l + 1], tok), dx))
        return dx, dxb

    def ffn_fwd(xin, n, l, next_gain):
        tok = arrive(("f", l), n)
        hp = _mm_nn(n, W["f_w_up"][l], out_dtype=BF16, name=f"f_up{l}")
        act, gel, ud = _f_fwd(hp, W["f_dw_w"][l], gain(W["f_dw_b"][l:l + 1], tok), f"f_conv_fwd{l}")
        arrive(("fd", l), act)
        res = _mm_nn(act, W["f_w_down"][l], out_dtype=F32, name=f"f_down{l}", add=xin, norm=next_gain)
        xout, nout = res if next_gain is not None else (res, None)
        return xout, nout, (xin, n, hp, act, gel, ud)

    def ffn_bwd(dx, dxb, saved, l):
        xin, n, hp, act, gel, ud = saved
        dact = _mm_nt(dxb, W["f_w_down"][l], out_dtype=BF16, name=f"f_dact{l}")
        G[f"f_w_down{l}"] = _mm_tn(act, dxb, out_dtype=BF16, name=f"f_dwdown{l}")
        dhp, G[f"f_dw_w{l}"], G[f"f_dw_b{l}"] = _f_bwd(dact, hp, gel, ud, W["f_dw_w"][l], f"f_conv_bwd{l}")
        G[f"f_w_up{l}"] = _mm_tn(n, dhp, out_dtype=BF16, name=f"f_dwup{l}", blocks=_CW_F)
        tok = sent(("f", l))
        dx, dxb, G[f"f_norm{l}"] = _mm_nt(dhp, W["f_w_up"][l], out_dtype=F32, name=f"f_dn{l}",
                                          rms=(xin, gain(W["f_norm"][l:l + 1], tok), dx))
        return dx, dxb

    n0 = _rms_fwd(x, W["ab_norm"], "ab_norm_fwd")
    tok = arrive(("ab", 0), n0)
    a_par = (W["a_conv_w"], gain(W["a_conv_b"], tok), W["a_gate_x_w"], W["a_gate_x_b"], W["a_gate_a_w"],
             W["a_gate_a_b"], W["a_lambda"])
    b_par = (W["b_group_w"], W["b_group_b"], W["b_scale"])
    zp = _mm_nn(n0, W["ab_w_in"], out_dtype=BF16, name="ab_in")
    yab, h_a = _a_fwd(zp, *a_par)
    yab = _b_fwd(zp, yab, *b_par)
    arrive(("ab", 1), yab)
    x1, n1 = _mm_nn(yab, W["ab_w_out"], out_dtype=F32, name="ab_out", add=x, norm=W["xa_norm"][0:1])
    x2, n2, s_xa0 = xattn_fwd(x1, n1, 0)
    x3, n3, s_f0 = ffn_fwd(x2, n2, 0, W["c_norm"])
    tok = arrive(("c", 0), n3)
    h1p = _mm_nn(n3, W["c_w_pw1"], out_dtype=BF16, name="c_pw1", bias=gain(W["c_b_pw1"], tok))
    cv = _c_fwd(h1p, W["c_dw_w"], W["c_dw_b"])
    sc = _ln_silu_fwd(cv, W["c_ln_g"], W["c_ln_b"])
    x4, n4 = _mm_nn(sc, W["c_w_pw2"], out_dtype=F32, name="c_pw2", bias=W["c_b_pw2"], add=x3, norm=W["xa_norm"][1:2])
    x5, n5, s_xa1 = xattn_fwd(x4, n4, 1)
    x6, _, s_f1 = ffn_fwd(x5, n5, 1, None)
    loss, dx, dxb, G["final_norm"] = _loss_head(x6, W["final_norm"], tgt)

    dx, dxb = ffn_bwd(dx, dxb, s_f1, 1)
    dx, dxb = xattn_bwd(dx, dxb, s_xa1, 1)
    dsc = _mm_nt(dxb, W["c_w_pw2"], out_dtype=BF16, name="c_dsc")
    G["c_w_pw2"] = _mm_tn(sc, dxb, out_dtype=BF16, name="c_dwpw2")
    dcv, G["c_ln_g"], G["c_ln_b"], G["c_b_pw2"] = _ln_silu_bwd(dsc, cv, W["c_ln_g"], W["c_ln_b"], dx)
    dh1p, G["c_dw_w"], G["c_dw_b"], G["c_b_pw1"] = _c_bwd(dcv, h1p, W["c_dw_w"])
    G["c_w_pw1"] = _mm_tn(n3, dh1p, out_dtype=BF16, name="c_dwpw1", blocks=_CW_C)
    tok = sent(("c", 0))
    dx, dxb, G["c_norm"] = _mm_nt(dh1p, W["c_w_pw1"], out_dtype=F32, name="c_dn",
                                  rms=(x3, gain(W["c_norm"], tok), dx))
    dx, dxb = ffn_bwd(dx, dxb, s_f0, 0)
    dx, dxb = xattn_bwd(dx, dxb, s_xa0, 0)
    dyab = _mm_nt(dxb, W["ab_w_out"], out_dtype=BF16, name="ab_dyab")
    G["ab_w_out"] = _mm_tn(yab, dxb, out_dtype=BF16, name="ab_dwout")
    tok = sent(("ab", 1))
    a_par = (a_par[0], gain(a_par[1], tok)) + a_par[2:]
    (dzg, dzr, G["a_conv_w"], G["a_conv_b"], G["a_gate_x_w"], G["a_gate_x_b"], G["a_gate_a_w"], G["a_gate_a_b"],
     G["a_lambda"]) = _a_bwd(dyab, zp, h_a, *a_par)
    dzq, G["b_group_w"], G["b_group_b"], G["b_scale"] = _b_bwd(dyab, zp, *b_par)
    G["ab_w_in"] = jnp.concatenate(
        [_mm_tn(n0, dz, out_dtype=BF16, name=f"ab_dwin_{part}")
         for part, dz in (("gate", dzg), ("rec", dzr), ("pool", dzq))], axis=1)
    tok = sent(("ab", 0))
    dx, _, G["ab_norm"] = _mm_nt_cols([dzg, dzr, dzq], W["ab_w_in"], name="ab_dn",
                                      rms=(x, gain(W["ab_norm"], tok), dx))
    return loss, dx, G


def _my_place():
    x, y, c = lax.axis_index("x"), lax.axis_index("y"), lax.axis_index("c")
    return x, y, c


def _all_gather(shards, name):
    n = len(shards)

    def body(*refs):
        ins, outs = refs[:n], refs[n:2 * n]
        send_sems, recv_sems, local_sems = refs[2 * n:]
        x, y, c = _my_place()
        me, sibling = (x, y, c), (x, y, 1 - c)
        chips = [(1 - x, y), (x, 1 - y), (1 - x, 1 - y)]

        def slab(a, place):
            px, py, pc = place
            return outs[a].at[4 * px + 2 * py + pc]

        def copy(a, k, block, to, src=None):
            return pltpu.make_async_remote_copy(
                src_ref=slab(a, block) if src is None else src, dst_ref=slab(a, block),
                send_sem=send_sems.at[a, k], recv_sem=recv_sems.at[a, k], device_id=to, device_id_type=MESH)

        mine = [pltpu.make_async_copy(ins[a], slab(a, me), local_sems.at[a]) for a in range(n)]
        for cp in mine:
            cp.start()
        first = []
        for j, chip in enumerate(chips):
            first += [copy(a, 1 + j, me, (*chip, c), src=ins[a]) for a in range(n)]
        first += [copy(a, 0, me, sibling, src=ins[a]) for a in range(n)]
        for cp in first:
            cp.start()
        passed = []
        for j, chip in enumerate(chips):
            for a in range(n):
                copy(a, 1 + j, (*chip, c), me).wait_recv()
                cp = copy(a, 4 + j, (*chip, c), sibling)
                cp.start()
                passed.append(cp)
        for a in range(n):
            copy(a, 0, sibling, me).wait_recv()
        for j, chip in enumerate(chips):
            for a in range(n):
                copy(a, 4 + j, (*chip, 1 - c), me).wait_recv()
        for cp in first + passed:
            cp.wait_send()
        for cp in mine:
            cp.wait()

    any_spec = pl.BlockSpec(memory_space=pl.ANY)
    return pl.pallas_call(
        body, out_shape=[SDS((N_DEV,) + s.shape, s.dtype) for s in shards], in_specs=[any_spec] * n,
        out_specs=[any_spec] * n,
        scratch_shapes=[pltpu.SemaphoreType.DMA((n, 7)), pltpu.SemaphoreType.DMA((n, 7)), pltpu.SemaphoreType.DMA((n,))],
        name=name,
    )(*shards)


_HBM = pl.BlockSpec(memory_space=pltpu.HBM)
_SEM = pl.BlockSpec(memory_space=pltpu.SEMAPHORE)
_EFFECT = pltpu.SideEffectType.DATAFLOW_SIDE_EFFECTING


def _peer_places():
    x, y, c = _my_place()
    peers = []
    for k in range(1, N_DEV):
        px = 1 - x if (k >> 2) & 1 else x
        py = 1 - y if (k >> 1) & 1 else y
        pc = 1 - c if k & 1 else c
        peers.append(((px, py, pc), 4 * px + 2 * py + pc))
    return (x, y, c), 4 * x + 2 * y + c, peers


def _send_start(srcs, per_dest, name):
    n = len(srcs)
    lands = [lax.empty((N_DEV,) + (s.shape[1:] if per_dest else s.shape), s.dtype) for s in srcs]

    def body(*refs):
        src, land = refs[:n], refs[n:2 * n]
        outs = refs[2 * n:]
        send, recv, token = outs[:n], outs[n:2 * n], outs[4 * n]
        place, me, peers = _peer_places()
        for a in range(n):
            for peer, pidx in peers + [(place, me)]:
                pltpu.make_async_remote_copy(
                    src_ref=src[a].at[pidx] if per_dest else src[a], dst_ref=land[a].at[me], send_sem=send[a],
                    recv_sem=recv[a], device_id=peer, device_id_type=MESH).start()
        token[...] = jnp.zeros_like(token)

    hbm = lambda a: pltpu.HBM(a.shape, a.dtype)
    sem = pltpu.SemaphoreType.DMA(())
    res = pl.pallas_call(
        body, name=name,
        out_shape=tuple([sem] * (2 * n) + [hbm(s) for s in srcs] + [hbm(l) for l in lands]
                        + [SDS((SUB, LANE), F32)]),
        in_specs=[_HBM] * (2 * n),
        out_specs=tuple([_SEM] * (2 * n) + [_HBM] * (2 * n) + [pl.BlockSpec(memory_space=pltpu.VMEM)]),
        input_output_aliases={i: 2 * n + i for i in range(2 * n)},
        compiler_params=pltpu.CompilerParams(has_side_effects=_EFFECT),
    )(*[pltpu.with_memory_space_constraint(s, pltpu.HBM) for s in srcs],
      *[pltpu.with_memory_space_constraint(l, pltpu.HBM) for l in lands])
    return res[:n], res[n:2 * n], res[2 * n:3 * n], res[3 * n:4 * n], res[4 * n]


def _send_wait(send, recv, srcs, lands, after, per_dest, name):
    n = len(srcs)

    def body(*refs):
        src, land = refs[:n], refs[n:2 * n]
        send_s, recv_s = refs[2 * n:3 * n], refs[3 * n:4 * n]
        token = refs[-1]
        place, _, _ = _peer_places()
        for a in range(n):
            copy = pltpu.make_async_remote_copy(
                src_ref=src[a] if per_dest else land[a], dst_ref=land[a], send_sem=send_s[a],
                recv_sem=recv_s[a], device_id=place, device_id_type=MESH)
            copy.wait_send()
            copy.wait_recv()
        token[...] = jnp.zeros_like(token)

    hbm = lambda a: pltpu.HBM(a.shape, a.dtype)
    res = pl.pallas_call(
        body, name=name,
        out_shape=tuple([hbm(s) for s in srcs] + [hbm(l) for l in lands] + [SDS((SUB, LANE), F32)]),
        in_specs=[_HBM] * (2 * n) + [_SEM] * (2 * n) + [pl.BlockSpec(memory_space=pl.ANY)],
        out_specs=tuple([_HBM] * (2 * n) + [pl.BlockSpec(memory_space=pltpu.VMEM)]),
        input_output_aliases={i: i for i in range(2 * n)},
        compiler_params=pltpu.CompilerParams(has_side_effects=_EFFECT),
    )(*srcs, *lands, *send, *recv, after)
    return res[:n], res[n:2 * n], res[2 * n]


def _adamw_math(w, g, m, v):
    m = ADAM_B1 * m + (1.0 - ADAM_B1) * g
    v = ADAM_B2 * v + (1.0 - ADAM_B2) * (g * g)
    m_hat = m / (1.0 - ADAM_B1 ** ADAM_STEP)
    v_hat = v / (1.0 - ADAM_B2 ** ADAM_STEP)
    delta = -ADAM_LR * (m_hat / (jnp.sqrt(v_hat) + ADAM_EPS) + ADAM_WD * w)
    return delta, m, v


def _row_tile(r, c, itemsize_rows):
    cap = max(SUB, (itemsize_rows // (4 * c)) // SUB * SUB)
    if r <= cap:
        return r
    best = None
    for t in range(SUB, cap + 1, SUB):
        if r % t == 0:
            best = t
    return best if best is not None else r


def _sum_adamw(landing, w, m, v, name, layer=0, prev=None, after=None):
    _, r, c = landing.shape
    tr = _row_tile(r, c, 2 << 20)
    off = layer * (r // tr)
    tail = ([] if prev is None else list(prev)) + ([] if after is None else [after])

    def body(l_ref, w_ref, m_ref, v_ref, *rest):
        g_ref, d_ref, mo_ref, vo_ref = rest[-4:]
        g = l_ref[0].astype(F32)
        for s in range(1, N_DEV):
            g = g + l_ref[s].astype(F32)
        g_ref[...] = g
        d_ref[...], mo_ref[...], vo_ref[...] = _adamw_math(w_ref[...], g, m_ref[...], v_ref[...])

    blk = pl.BlockSpec((tr, c), lambda i: (i + off, 0))
    n_prev = 0 if prev is None else 4
    return pl.pallas_call(
        body, out_shape=[SDS(w.shape, F32)] * 4, grid=(r // tr,),
        in_specs=[pl.BlockSpec((N_DEV, tr, c), lambda i: (0, i, 0)), blk, blk, blk]
        + [pl.BlockSpec(memory_space=pl.ANY)] * len(tail),
        out_specs=[blk] * 4, input_output_aliases={4 + i: i for i in range(n_prev)}, name=name,
        compiler_params=_cp(1),
    )(landing, w, m, v, *tail)


def _sum8(landing, name):
    _, r, c = landing.shape

    def body(l_ref, g_ref):
        g = l_ref[0]
        for s in range(1, N_DEV):
            g = g + l_ref[s]
        g_ref[...] = g

    return pl.pallas_call(body, out_shape=SDS((r, c), F32), name=name, compiler_params=_cp(0))(landing)


def _adamw_small(repl_pack, own_pack, P, M, V):
    table, off = [], 0
    for name, shape in _REPL.items():
        table.append((name, shape if len(shape) > 1 else (1,) + shape, 0, off // LANE))
        off += _size(shape)
    off = _REPL_ROWS * LANE
    for name, shape in _SMALL_SHARDED.items():
        table.append((name, shape, 1, off // LANE))
        off += _size(shape)
    n = len(table)

    def body(*refs):
        packs, ins, outs = refs[:2], refs[2:2 + 3 * n], refs[2 + 3 * n:]
        for p, (_, shape, which, r0) in enumerate(table):
            w_ref, m_ref, v_ref = ins[3 * p:3 * p + 3]
            g_ref, d_ref, mo_ref, vo_ref = outs[4 * p:4 * p + 4]
            pack, rows, q = packs[which], shape[-2], shape[-1] // LANE
            lead = [()]
            for dim in shape[:-2]:
                lead = [t + (i,) for t in lead for i in range(dim)]
            for li, idx in enumerate(lead):
                if q == 1:
                    dst = g_ref.at[idx] if idx else g_ref
                    dst[...] = pack[r0 + li * rows:r0 + (li + 1) * rows, :]
                    continue
                for i in range(rows):
                    for k in range(q):
                        row = r0 + (li * rows + i) * q + k
                        g_ref[idx + (slice(i, i + 1), slice(k * LANE, (k + 1) * LANE))] = pack[row:row + 1, :]
            d_ref[...], mo_ref[...], vo_ref[...] = _adamw_math(w_ref[...], g_ref[...], m_ref[...], v_ref[...])

    ins, out_shape = [], []
    for name, shape, _, _ in table:
        ins += [t[name].reshape(shape) for t in (P, M, V)]
        out_shape += [SDS(shape, F32)] * 4
    res = pl.pallas_call(body, out_shape=out_shape, name="adamw_small", compiler_params=_cp(0))(
        repl_pack, own_pack, *ins)
    dicts = ({}, {}, {}, {})
    for p, (name, shape, _, _) in enumerate(table):
        for d, arr in zip(dicts, res[4 * p:4 * p + 4]):
            d[name] = arr.reshape(P[name].shape)
    return dicts


_BIG = {
    "ab_w_in": (1, D, 320), "ab_w_out": (1, 192, D), "c_w_pw1": (1, D, 256), "c_w_pw2": (1, 128, D),
    "xa_wq": (2, 128, D), "xa_wk": (2, 128, D), "xa_wv": (2, 128, D), "xa_wo": (2, 128, D),
    "f_w_up": (2, D, 768), "f_w_down": (2, 384, D),
}
_SMALL_SHARDED = {
    "a_conv_w": (1, 4, 128), "c_norm": (1, 128), "c_b_pw1": (1, 256), "c_dw_w": (1, 31, 128), "c_dw_b": (1, 128),
    "c_ln_g": (1, 128), "c_ln_b": (1, 128), "c_b_pw2": (1, 128), "f_dw_w": (2, 3, 384),
}
_REPL = {
    "ab_norm": (1, D), "a_conv_b": (1, D), "a_gate_x_w": (1, 8, 128, 128), "a_gate_x_b": (1, D),
    "a_gate_a_w": (1, 8, 128, 128), "a_gate_a_b": (1, D), "a_lambda": (1, D), "b_group_w": (1, 4, 128, 128),
    "b_group_b": (1, 512), "b_scale": (1, 512), "xa_norm": (2, D), "xa_mem_norm": (2, D), "f_norm": (2, D),
    "f_dw_b": (2, D_FF), "final_norm": (D,),
}


def _size(shape):
    n = 1
    for s in shape:
        n *= s
    return n


_N_SS = sum(_size(s) for s in _SMALL_SHARDED.values())
_N_REPL = sum(_size(s) for s in _REPL.values())
_REPL_ROWS = -(-_N_REPL // (N_DEV * SUB * LANE)) * SUB
_SS_ROWS = _N_SS // LANE
_SMALL_ROWS = -(-(_REPL_ROWS + _SS_ROWS) // SUB) * SUB


def _pack(parts, rows):
    flat = jnp.concatenate([p.reshape(-1).astype(F32) for p in parts])
    return jnp.pad(flat, (0, rows * LANE - flat.shape[0])).reshape(rows, LANE)


def _pair_blocks(v, bw):
    lead, n = v.shape[:-1], v.shape[-1]
    return jnp.swapaxes(v.reshape(lead + (2, n // (2 * bw), bw)), -3, -2).reshape(lead + (n,))


def _unpair_blocks(v, bw):
    lead, n = v.shape[:-1], v.shape[-1]
    return jnp.swapaxes(v.reshape(lead + (n // (2 * bw), 2, bw)), -3, -2).reshape(lead + (n,))


_GROUPS = {
    ("ab", 0): (("ab_w_in", 0),),
    ("ab", 1): (("ab_w_out", 0),),
    ("xa", 0): (("xa_wq", 0), ("xa_wk", 0), ("xa_wv", 0), ("xa_wo", 0)),
    ("f", 0): (("f_w_up", 0),),
    ("fd", 0): (("f_w_down", 0),),
    ("c", 0): (("c_w_pw1", 0), ("c_w_pw2", 0)),
    ("xa", 1): (("xa_wq", 1), ("xa_wk", 1), ("xa_wv", 1), ("xa_wo", 1)),
    ("f", 1): (("f_w_up", 1),),
    ("fd", 1): (("f_w_down", 1),),
}
_SEND_GROUPS = {g: m for g, m in _GROUPS.items() if g[0] != "fd"}
_SEND_GROUPS[("f", 0)] = (("f_w_up", 0), ("f_w_down", 0))
_SEND_GROUPS[("f", 1)] = (("f_w_up", 1), ("f_w_down", 1))


def _weight_layout(name, g):
    if name == "ab_w_in":
        return jnp.swapaxes(g, 0, 1).reshape(D, N_DEV * 320)
    if name in ("c_w_pw1", "f_w_up"):
        return g
    return g.reshape(N_DEV * g.shape[1], D)


def _grad_blocks(name, l, G):
    _, r, c = _BIG[name]
    if name == "ab_w_in":
        return jnp.swapaxes(G[name].reshape(D, N_DEV, 320), 0, 1)
    if name == "c_w_pw1":
        return G[name]
    if name == "f_w_up":
        return G[f"{name}{l}"]
    return (G[name] if _BIG[name][0] == 1 else G[f"{name}{l}"]).reshape(N_DEV, r, c)


def _small_layouts(sm):
    W = {}
    sm = sm.reshape(N_DEV, -1)
    off = 0
    for name, shape in _SMALL_SHARDED.items():
        n = _size(shape)
        blocks = sm[:, off:off + n].reshape((N_DEV,) + shape)
        off += n
        W[name] = jnp.moveaxis(blocks, 0, -2).reshape(shape[:-1] + (N_DEV * shape[-1],))
    W["a_conv_w"], W["c_dw_w"] = W["a_conv_w"][0], W["c_dw_w"][0]
    W["c_b_pw1"] = _pair_blocks(W["c_b_pw1"], _CW_C)
    return W


def _to_dest_major(g, shape):
    full = g.reshape(shape[:-1] + (N_DEV, shape[-1]))
    return jnp.moveaxis(full, -2, 0).reshape(N_DEV, -1)


def kernel(x, mem, ab_norm, ab_w_in, a_conv_w, a_conv_b, a_gate_x_w, a_gate_x_b, a_gate_a_w, a_gate_a_b, a_lambda, b_group_w, b_group_b, b_scale, ab_w_out, c_norm, c_w_pw1, c_b_pw1, c_dw_w, c_dw_b, c_ln_g, c_ln_b, c_w_pw2, c_b_pw2, xa_norm, xa_mem_norm, xa_wq, xa_wk, xa_wv, xa_wo, f_norm, f_w_up, f_dw_w, f_dw_b, f_w_down, final_norm, loss_target, m_ab_norm, m_ab_w_in, m_a_conv_w, m_a_conv_b, m_a_gate_x_w, m_a_gate_x_b, m_a_gate_a_w, m_a_gate_a_b, m_a_lambda, m_b_group_w, m_b_group_b, m_b_scale, m_ab_w_out, m_c_norm, m_c_w_pw1, m_c_b_pw1, m_c_dw_w, m_c_dw_b, m_c_ln_g, m_c_ln_b, m_c_w_pw2, m_c_b_pw2, m_xa_norm, m_xa_mem_norm, m_xa_wq, m_xa_wk, m_xa_wv, m_xa_wo, m_f_norm, m_f_w_up, m_f_dw_w, m_f_dw_b, m_f_w_down, m_final_norm, v_ab_norm, v_ab_w_in, v_a_conv_w, v_a_conv_b, v_a_gate_x_w, v_a_gate_x_b, v_a_gate_a_w, v_a_gate_a_b, v_a_lambda, v_b_group_w, v_b_group_b, v_b_scale, v_ab_w_out, v_c_norm, v_c_w_pw1, v_c_b_pw1, v_c_dw_w, v_c_dw_b, v_c_ln_g, v_c_ln_b, v_c_w_pw2, v_c_b_pw2, v_xa_norm, v_xa_mem_norm, v_xa_wq, v_xa_wk, v_xa_wv, v_xa_wo, v_f_norm, v_f_w_up, v_f_dw_w, v_f_dw_b, v_f_w_down, v_final_norm):
    args = dict(locals())
    P = {n: args[n] for n in _NAMES}
    M = {n: args["m_" + n] for n in _NAMES}
    V = {n: args["v_" + n] for n in _NAMES}

    in_flight = {}

    def launch(groups, tok):
        shards, n_of = [], {}
        for grp in groups:
            for name, l in _GROUPS[grp]:
                w = P[name][l] if tok is None else P[name][l] + tok
                shards.append(w.astype(BF16))
            if grp == ("ab", 0):
                shards.append(_pack([P[n] for n in _SMALL_SHARDED], _SS_ROWS + 4))
            n_of[grp] = len(shards)
        res = _send_start(shards, False, "gather_start_" + "_".join(g[0] + str(g[1]) for g in groups))
        lo = 0
        for grp in groups:
            in_flight[grp] = [r[lo:n_of[grp]] for r in res[:4]]
            lo = n_of[grp]
        return res[4][:1, :1]

    follow = {("ab", 0): [("ab", 1), ("xa", 0), ("f", 0), ("fd", 0)], ("f", 0): [("c", 0), ("xa", 1)],
              ("c", 0): [("f", 1), ("fd", 1)]}

    def fetch(grp, after):
        send_s, recv_s, srcs, lands = in_flight.pop(grp)
        srcs, lands, tok = _send_wait(send_s, recv_s, srcs, lands, after, False, f"gather_wait_{grp[0]}{grp[1]}")
        tok = launch(follow[grp], tok[:1, :1]) if grp in follow else None
        full = lands
        out = {}
        for (name, l), g in zip(_GROUPS[grp], full):
            w = _weight_layout(name, g)
            if _BIG[name][0] == 1:
                out[name] = w
            else:
                out[name] = {l: w}
        if grp == ("ab", 0):
            out.update(_small_layouts(full[-1]))
        return out, tok

    zero = launch([("ab", 0)], None)

    pending, held = [], []
    rides_with_next = {("xa", 1), ("f", 0)}

    def send(grp, G):
        held.extend(_SEND_GROUPS[grp])
        if grp in rides_with_next:
            return None
        members = tuple(held)
        del held[:]
        res = _send_start([_grad_blocks(name, l, G) for name, l in members], True, f"send_{grp[0]}{grp[1]}")
        pending.append((members, res))
        return res[4][:1, :1]

    W = {n: P[n] for n in _REPL}
    W["ab_norm"] = P["ab_norm"] + zero
    W["final_norm"] = P["final_norm"].reshape(1, D)
    W["a_gate_x_w"], W["a_gate_a_w"], W["b_group_w"] = P["a_gate_x_w"][0], P["a_gate_a_w"][0], P["b_group_w"][0]
    loss, grad_x, G = _local_step(x[0], mem[0], loss_target[0], W, fetch, send)
    loss = lax.psum(loss[0, 0], ("x", "y", "c"))

    Gs = dict(G)
    Gs["c_b_pw1"] = _unpair_blocks(G["c_b_pw1"], _CW_C)
    Gs["f_dw_w"] = jnp.stack([G["f_dw_w0"], G["f_dw_w1"]])
    Gs["a_conv_w"], Gs["c_dw_w"] = G["a_conv_w"][None], G["c_dw_w"][None]
    for n in ("xa_norm", "xa_mem_norm", "f_norm", "f_dw_b"):
        Gs[n] = jnp.concatenate([G[f"{n}0"], G[f"{n}1"]], axis=0)
    for n in ("a_gate_x_w", "a_gate_a_w", "b_group_w"):
        Gs[n] = G[n][None]
    repl_flat = jnp.concatenate([Gs[n].reshape(-1) for n in _REPL])
    repl_rows = jnp.pad(repl_flat, (0, N_DEV * _REPL_ROWS * LANE - _N_REPL)).reshape(N_DEV, _REPL_ROWS, LANE)
    ss_rows = jnp.concatenate([_to_dest_major(Gs[n], s) for n, s in _SMALL_SHARDED.items()], axis=1)
    ss_rows = ss_rows.reshape(N_DEV, _SS_ROWS, LANE)
    small_pack = jnp.concatenate(
        [repl_rows, ss_rows, jnp.zeros((N_DEV, _SMALL_ROWS - _REPL_ROWS - _SS_ROWS, LANE), F32)], axis=1)
    last = _send_start([small_pack], True, "send_small")
    pending.append(((("small", 0),), last))

    def arrived(some, after, name):
        members = [m for mem_, _ in some for m in mem_]
        cat = [[a for _, res in some for a in res[i]] for i in range(4)]
        srcs, lands, _ = _send_wait(cat[0], cat[1], cat[2], cat[3], after, True, name)
        return dict(zip(members, lands))

    out_g, out_d, out_m, out_v = {}, {}, {}, {}
    chain = [None]

    def update(name, landed):
        layers, r, c = _BIG[name]
        w2, m2, v2 = [t[name].reshape(layers * r, c) for t in (P, M, V)]
        res = None
        for l in range(layers):
            res = _sum_adamw(landed[(name, l)], w2, m2, v2, f"adamw_{name}{l}", layer=l, prev=res,
                             after=chain[0] if l == 0 else None)
        chain[0] = res[1]
        out_g[name], out_d[name], out_m[name], out_v[name] = [t.reshape(P[name].shape) for t in res]

    landed = arrived(pending[:-2], grad_x, "send_wait_early")
    for name in _BIG:
        if name != "ab_w_in":
            update(name, landed)
    landed = arrived(pending[-2:], out_v["f_w_down"], "send_wait_late")
    update("ab_w_in", landed)

    small_sum = _sum8(landed[("small", 0)], "sum_small")
    (repl_all,) = _all_gather([small_sum[:_REPL_ROWS]], "gather_small_grads")
    for out, got in zip((out_g, out_d, out_m, out_v),
                        _adamw_small(repl_all.reshape(N_DEV * _REPL_ROWS, LANE), small_sum, P, M, V)):
        out.update(got)

    return (loss, grad_x[None], *[out_g[n] for n in _NAMES], *[out_d[n] for n in _NAMES],
            *[out_m[n] for n in _NAMES], *[out_v[n] for n in _NAMES])


_NAMES = ("ab_norm", "ab_w_in", "a_conv_w", "a_conv_b", "a_gate_x_w", "a_gate_x_b", "a_gate_a_w", "a_gate_a_b",
          "a_lambda", "b_group_w", "b_group_b", "b_scale", "ab_w_out", "c_norm", "c_w_pw1", "c_b_pw1", "c_dw_w",
          "c_dw_b", "c_ln_g", "c_ln_b", "c_w_pw2", "c_b_pw2", "xa_norm", "xa_mem_norm", "xa_wq", "xa_wk", "xa_wv",
          "xa_wo", "f_norm", "f_w_up", "f_dw_w", "f_dw_b", "f_w_down", "final_norm")
```

```python
import functools

import jax
import jax.numpy as jnp
from jax import lax
from jax.experimental import pallas as pl
from jax.experimental.pallas import tpu as pltpu

F32, BF16 = jnp.float32, jnp.bfloat16
SDS = jax.ShapeDtypeStruct
MESH = pl.DeviceIdType.MESH

N_DEV = 8
D = 1024
N_MEM = 256
XA_HEADS, XA_HD = 4, 256
HD_A = 128
CONV_A, CONV_C, CONV_F = 4, 31, 3
C_RG = 8.0
POOL_WINDOWS = (2, 4, 8, 16)
D_FF = 3 * D
EPS = 1e-6
ADAM_LR, ADAM_B1, ADAM_B2, ADAM_EPS, ADAM_WD, ADAM_STEP = 0.001, 0.9, 0.999, 1e-08, 0.01, 10

LANE = 128
SUB = 8
VMEM_LIMIT = 56 * 1024 * 1024
R_SEQ = 1024
R_POOL = 2048
R_RGLRU = 2048
R_FFN = 2048
TM_ROW = 1024


def _cp(n_axes):
    return pltpu.CompilerParams(dimension_semantics=("arbitrary",) * n_axes, vmem_limit_bytes=VMEM_LIMIT)


def _tile(n, pref):
    if n <= pref:
        return n
    best = None
    for t in range(LANE, pref + 1, LANE):
        if n % t == 0:
            best = t
    assert best is not None, (n, pref)
    return best


def _perm2(n):
    return (n % 2) * 4 + n // 2


_NN = (((1,), (0,)), ((), ()))
_NT = (((1,), (1,)), ((), ()))
_TN = (((0,), (0,)), ((), ()))


def _mm_call(name, grid, ab, ab_specs, dims, acc_shape, extras, outs, finish, from_ref=False):
    nk = grid[2]
    n_ab, n_ex, n_out = len(ab), len(extras), len(outs)
    use_acc = nk > 1 or from_ref

    def product(refs):
        r = lax.dot_general(refs[0][...], refs[1][...], dims, preferred_element_type=F32)
        for i in range(1, n_ab):
            r = r + lax.dot_general(refs[2 * i][...], refs[2 * i + 1][...], dims, preferred_element_type=F32)
        return r

    def body(*refs):
        rest = refs[2 * n_ab:]
        ex_refs, o_refs = rest[:n_ex], rest[n_ex:n_ex + n_out]
        first_rows = pl.program_id(0) == 0
        if not use_acc:
            finish(product(refs), ex_refs, o_refs, first_rows)
            return
        acc = rest[n_ex + n_out]
        if nk == 1:
            acc[...] = product(refs)
            finish(acc, ex_refs, o_refs, first_rows)
            return
        k = pl.program_id(2)

        @pl.when(k == 0)
        def _():
            acc[...] = jnp.zeros_like(acc)

        acc[...] += product(refs)

        @pl.when(k == nk - 1)
        def _():
            finish(acc if from_ref else acc[...], ex_refs, o_refs, first_rows)

    res = pl.pallas_call(
        body, out_shape=[o for o, _ in outs], grid=grid,
        in_specs=list(ab_specs) + [s for _, s in extras], out_specs=[s for _, s in outs],
        scratch_shapes=[pltpu.VMEM(acc_shape, F32)] if use_acc else [], name=name, compiler_params=_cp(3),
    )(*[t for pair in ab for t in pair], *[e for e, _ in extras])
    return res[0] if n_out == 1 else res


def _finish_sum(r, ex_refs, o_refs, first_rows):
    del first_rows
    for e in ex_refs:
        r = r + e[...]
    o_refs[0][...] = r.astype(o_refs[0].dtype)


def _finish_sum_norm(r, ex_refs, o_refs, first_rows):
    del first_rows
    for e in ex_refs[:-1]:
        r = r + e[...]
    o_refs[0][...] = r
    o_refs[1][...] = ((r * lax.rsqrt(jnp.mean(r * r, axis=-1, keepdims=True) + EPS)) * ex_refs[-1][...]).astype(BF16)


_EPI_ROWS = 16


def _finish_rms_bwd(r_ref, ex_refs, o_refs, first_rows):
    x_ref, g_ref, dres_ref = ex_refs
    dx_ref, dxb_ref, dg_ref = o_refs

    @pl.when(first_rows)
    def _():
        dg_ref[...] = jnp.zeros_like(dg_ref)

    gv = g_ref[...]
    inv_d = 1.0 / r_ref.shape[1]

    def step(i, dg_acc):
        groups = [pl.ds(pl.multiple_of(i * (2 * _EPI_ROWS) + u * _EPI_ROWS, _EPI_ROWS), _EPI_ROWS) for u in range(2)]
        sums = []
        for rows in groups:
            r, xf = r_ref[rows, :], x_ref[rows, :]
            sums.append((jnp.sum(xf * xf, axis=-1, keepdims=True), jnp.sum((r * gv) * xf, axis=-1, keepdims=True)))
        for rows, (sxx, sax) in zip(groups, sums):
            r, xf = r_ref[rows, :], x_ref[rows, :]
            rs = lax.rsqrt(sxx * inv_d + EPS)
            dg_acc = dg_acc + _psum8(r * (xf * rs))
            dx = rs * (r * gv) - xf * (rs * rs * (sax * rs * inv_d)) + dres_ref[rows, :]
            dx_ref[rows, :] = dx
            dxb_ref[rows, :] = dx.astype(BF16)
        return dg_acc

    dg_acc = lax.fori_loop(0, r_ref.shape[0] // (2 * _EPI_ROWS), step, jnp.zeros((SUB, r_ref.shape[1]), F32))
    dg_ref[...] += jnp.sum(dg_acc, axis=0, keepdims=True)


def _rms_bwd_io(M, tm, x, g, dres):
    rows = pl.BlockSpec((tm, D), lambda m, n, k: (m, 0))
    vec = pl.BlockSpec((1, D), lambda m, n, k: (0, 0))
    return ([(x, rows), (g, vec), (dres, rows)],
            [(SDS((M, D), F32), rows), (SDS((M, D), BF16), rows), (SDS((1, D), F32), vec)])


_K_WHOLE = 3072


def _mm_nn(a, b, *, out_dtype, name, bias=None, add=None, norm=None):
    M, K = a.shape
    tk = K if K <= _K_WHOLE else _tile(K, 1024)
    if K <= 1024 and norm is None:
        tm = _tile(M, 2048 if add is None and out_dtype == BF16 else 1024)
    else:
        tm = _tile(M, 512)
    if b.ndim == 3:
        nb, _, bw = b.shape
        N, tn, nn = nb * bw, bw, nb
        b_spec = pl.BlockSpec((None, tk, bw), lambda m, n, k: (_perm2(n), k, 0))
    else:
        N = b.shape[1]
        tn = _tile(N, 1024)
        nn = N // tn
        b_spec = pl.BlockSpec((tk, tn), lambda m, n, k: (k, n))
    tile = pl.BlockSpec((tm, tn), lambda m, n, k: (m, n))
    vec = pl.BlockSpec((1, tn), lambda m, n, k: (0, n))
    extras = ([] if bias is None else [(bias, vec)]) + ([] if add is None else [(add, tile)])
    outs, finish = [(SDS((M, N), out_dtype), tile)], _finish_sum
    if norm is not None:
        assert tn == N == D and out_dtype == F32
        extras.append((norm, vec))
        outs, finish = outs + [(SDS((M, N), BF16), tile)], _finish_sum_norm
    return _mm_call(name, (M // tm, nn, K // tk), [(a, b)], [pl.BlockSpec((tm, tk), lambda m, n, k: (m, k)), b_spec],
                    _NN, (tm, tn), extras, outs, finish)


def _mm_nt(a, b, *, out_dtype, name, add=None, rms=None):
    M, N = a.shape
    if b.ndim == 3:
        nb, Ko, bw = b.shape
        tm = _tile(M, 1024)
        tn, tk, nk = _tile(Ko, 1024), bw, nb
        b_spec = pl.BlockSpec((None, tn, bw), lambda m, n, k: (_perm2(k), n, 0))
    else:
        Ko = b.shape[0]
        tk = N if N <= _K_WHOLE else _tile(N, 1024)
        if N <= 1024 and rms is None:
            tm = _tile(M, 2048 if add is None and out_dtype == BF16 else 1024)
        else:
            tm = _tile(M, 512)
        tn = _tile(Ko, 1024)
        nk = N // tk
        b_spec = pl.BlockSpec((tn, tk), lambda m, n, k: (n, k))
    tile = pl.BlockSpec((tm, tn), lambda m, n, k: (m, n))
    extras = [] if add is None else [(add, tile)]
    outs, finish = [(SDS((M, Ko), out_dtype), tile)], _finish_sum
    if rms is not None:
        assert tn == Ko == D and add is None
        (extras, outs), finish = _rms_bwd_io(M, tm, *rms), _finish_rms_bwd
    return _mm_call(name, (M // tm, Ko // tn, nk), [(a, b)], [pl.BlockSpec((tm, tk), lambda m, n, k: (m, k)), b_spec],
                    _NT, (tm, tn), extras, outs, finish, from_ref=rms is not None)


def _mm_nt_cols(parts, b, *, name, rms):
    M = parts[0].shape[0]
    tm = _tile(M, 512)
    specs, off = [], 0
    for p in parts:
        w = p.shape[1]
        assert off % w == 0
        specs.append(pl.BlockSpec((tm, w), lambda m, n, k: (m, 0)))
        specs.append(pl.BlockSpec((D, w), functools.partial(lambda m, n, k, o: (0, o), o=off // w)))
        off += w
    extras, outs = _rms_bwd_io(M, tm, *rms)
    return _mm_call(name, (M // tm, 1, 1), [(p, b) for p in parts], specs, _NT, (tm, D), extras, outs, _finish_rms_bwd,
                    from_ref=True)


def _mm_tn(a, b, *, out_dtype, name, blocks=None):
    S, Ka = a.shape
    Nb = b.shape[1]
    tm = _tile(Ka, 1024)
    if blocks is not None:
        bw = blocks
        tn, nn = bw, Nb // bw
        out = (SDS((nn, Ka, bw), out_dtype), pl.BlockSpec((None, tm, bw), lambda m, n, k: (_perm2(n), m, 0)))
    else:
        tn = _tile(Nb, 1024)
        nn = Nb // tn
        out = (SDS((Ka, Nb), out_dtype), pl.BlockSpec((tm, tn), lambda m, n, k: (m, n)))
    steps = (Ka // tm) * nn
    tk = _tile(S, 4096 if steps >= 4 else 2048 if steps >= 2 else 1024)
    return _mm_call(name, (Ka // tm, nn, S // tk), [(a, b)],
                    [pl.BlockSpec((tk, tm), lambda m, n, k: (k, m)), pl.BlockSpec((tk, tn), lambda m, n, k: (k, n))],
                    _TN, (tm, tn), [], [out], _finish_sum)


def _row(tm, c):
    return pl.BlockSpec((tm, c), lambda i: (i, 0))


def _full(shape):
    nd = len(shape)
    return pl.BlockSpec(shape, lambda i: (0,) * nd)


def _rms_fwd(x, g, name):
    S = x.shape[0]
    tm = min(S, TM_ROW)

    def body(x_ref, g_ref, o_ref):
        xf = x_ref[...]
        r = lax.rsqrt(jnp.mean(xf * xf, axis=-1, keepdims=True) + EPS)
        o_ref[...] = ((xf * r) * g_ref[...]).astype(BF16)

    return pl.pallas_call(body, out_shape=SDS((S, D), BF16), grid=(S // tm,), in_specs=[_row(tm, D), _full((1, D))],
                          out_specs=_row(tm, D), name=name, compiler_params=_cp(1))(x, g)


def _rms_bwd(x, g, dn, dres, name):
    S = x.shape[0]
    tm = min(S, TM_ROW)
    want_dx = dres is not None

    def body(x_ref, g_ref, dn_ref, *rest):
        i = pl.program_id(0)
        dg_ref = rest[-1]

        @pl.when(i == 0)
        def _():
            dg_ref[...] = jnp.zeros_like(dg_ref)

        xf = x_ref[...]
        r = lax.rsqrt(jnp.mean(xf * xf, axis=-1, keepdims=True) + EPS)
        y = xf * r
        dn_v = dn_ref[...]
        dg_ref[...] += jnp.sum(dn_v * y, axis=0, keepdims=True)
        if want_dx:
            dres_ref, dx_ref, dxb_ref = rest[0], rest[1], rest[2]
            dy = dn_v * g_ref[...]
            dx = r * (dy - y * jnp.mean(dy * y, axis=-1, keepdims=True)) + dres_ref[...]
            dx_ref[...] = dx
            dxb_ref[...] = dx.astype(BF16)

    ins = [x, g, dn] + ([dres] if want_dx else [])
    in_specs = [_row(tm, D), _full((1, D)), _row(tm, D)] + ([_row(tm, D)] if want_dx else [])
    outs = ([SDS((S, D), F32), SDS((S, D), BF16)] if want_dx else []) + [SDS((1, D), F32)]
    out_specs = ([_row(tm, D), _row(tm, D)] if want_dx else []) + [_full((1, D))]
    return pl.pallas_call(body, out_shape=outs, grid=(S // tm,), in_specs=in_specs, out_specs=out_specs, name=name,
                          compiler_params=_cp(1))(*ins)


def _loss_head(x, g, tgt):
    S = x.shape[0]
    tm = min(S, TM_ROW)

    def body(x_ref, g_ref, t_ref, loss_ref, dx_ref, dxb_ref, dg_ref):
        i = pl.program_id(0)

        @pl.when(i == 0)
        def _():
            loss_ref[...] = jnp.zeros_like(loss_ref)
            dg_ref[...] = jnp.zeros_like(dg_ref)

        xf = x_ref[...]
        r = lax.rsqrt(jnp.mean(xf * xf, axis=-1, keepdims=True) + EPS)
        y = xf * r
        gv = g_ref[...]
        err = y * gv - t_ref[...]
        per_row = jnp.mean(err * err, axis=-1, keepdims=True)
        loss_ref[...] += 0.5 * jnp.sum(per_row, axis=0, keepdims=True)
        dn_v = err * (1.0 / D)
        dg_ref[...] += jnp.sum(dn_v * y, axis=0, keepdims=True)
        dy = dn_v * gv
        dx = r * (dy - y * jnp.mean(dy * y, axis=-1, keepdims=True))
        dx_ref[...] = dx
        dxb_ref[...] = dx.astype(BF16)

    return pl.pallas_call(
        body, out_shape=[SDS((1, 1), F32), SDS((S, D), F32), SDS((S, D), BF16), SDS((1, D), F32)], grid=(S // tm,),
        in_specs=[_row(tm, D), _full((1, D)), _row(tm, D)],
        out_specs=[_full((1, 1)), _row(tm, D), _row(tm, D), _full((1, D))], name="loss_head", compiler_params=_cp(1),
    )(x, g, tgt)


def _softmax_rows(s):
    m = jnp.max(s, axis=-1, keepdims=True)
    e = jnp.exp(s - m)
    return e / jnp.sum(e, axis=-1, keepdims=True)


def _attn_fwd(q, k, v, name):
    S = q.shape[0]
    tm = min(S, TM_ROW)
    scale = XA_HD ** -0.5

    def body(q_ref, k_ref, v_ref, o_ref):
        for h in range(XA_HEADS):
            sl = slice(h * XA_HD, (h + 1) * XA_HD)
            s = lax.dot_general(q_ref[:, sl], k_ref[:, sl], _NT, preferred_element_type=F32) * scale
            p = _softmax_rows(s)
            o_ref[:, sl] = lax.dot_general(p.astype(BF16), v_ref[:, sl], _NN, preferred_element_type=F32).astype(BF16)

    return pl.pallas_call(body, out_shape=SDS((S, D), BF16), grid=(S // tm,),
                          in_specs=[_row(tm, D), _full((N_MEM, D)), _full((N_MEM, D))], out_specs=_row(tm, D),
                          name=name, compiler_params=_cp(1))(q, k, v)


def _attn_bwd(q, k, v, do, name):
    S = q.shape[0]
    tm = min(S, TM_ROW)
    scale = XA_HD ** -0.5

    def body(q_ref, k_ref, v_ref, do_ref, dq_ref, dk_ref, dv_ref):
        i = pl.program_id(0)

        @pl.when(i == 0)
        def _():
            dk_ref[...] = jnp.zeros_like(dk_ref)
            dv_ref[...] = jnp.zeros_like(dv_ref)

        for h in range(XA_HEADS):
            sl = slice(h * XA_HD, (h + 1) * XA_HD)
            qh, kh, vh, doh = q_ref[:, sl], k_ref[:, sl], v_ref[:, sl], do_ref[:, sl]
            s = lax.dot_general(qh, kh, _NT, preferred_element_type=F32) * scale
            p = _softmax_rows(s)
            pb = p.astype(BF16)
            dv_ref[:, sl] += lax.dot_general(pb, doh, _TN, preferred_element_type=F32)
            dp = lax.dot_general(doh, vh, _NT, preferred_element_type=F32)
            ds = (p * (dp - jnp.sum(dp * p, axis=-1, keepdims=True)) * scale).astype(BF16)
            dq_ref[:, sl] = lax.dot_general(ds, kh, _NN, preferred_element_type=F32).astype(BF16)
            dk_ref[:, sl] += lax.dot_general(ds, qh, _TN, preferred_element_type=F32)

    return pl.pallas_call(
        body, out_shape=[SDS((S, D), BF16), SDS((N_MEM, D), F32), SDS((N_MEM, D), F32)], grid=(S // tm,),
        in_specs=[_row(tm, D), _full((N_MEM, D)), _full((N_MEM, D)), _row(tm, D)],
        out_specs=[_row(tm, D), _full((N_MEM, D)), _full((N_MEM, D))], name=name, compiler_params=_cp(1),
    )(q, k, v, do)


def _sigmoid(x):
    return 1.0 / (1.0 + jnp.exp(-x))


def _ln_silu_fwd(cv, g, b):
    S = cv.shape[0]
    tm = min(S, TM_ROW)

    def body(x_ref, g_ref, b_ref, o_ref):
        xf = x_ref[...]
        mu = jnp.mean(xf, axis=-1, keepdims=True)
        xc = xf - mu
        rstd = lax.rsqrt(jnp.mean(xc * xc, axis=-1, keepdims=True) + EPS)
        ln = (xc * rstd) * g_ref[...] + b_ref[...]
        o_ref[...] = (ln * _sigmoid(ln)).astype(BF16)

    return pl.pallas_call(body, out_shape=SDS((S, D), BF16), grid=(S // tm,),
                          in_specs=[_row(tm, D), _full((1, D)), _full((1, D))], out_specs=_row(tm, D),
                          name="ln_silu_fwd", compiler_params=_cp(1))(cv, g, b)


def _ln_silu_bwd(ds, cv, g, b, dx):
    S = cv.shape[0]
    tm = min(S, TM_ROW)

    def body(ds_ref, x_ref, g_ref, b_ref, dx_ref, dcv_ref, dg_ref, db_ref, db2_ref):
        i = pl.program_id(0)

        @pl.when(i == 0)
        def _():
            dg_ref[...] = jnp.zeros_like(dg_ref)
            db_ref[...] = jnp.zeros_like(db_ref)
            db2_ref[...] = jnp.zeros_like(db2_ref)

        xf = x_ref[...]
        mu = jnp.mean(xf, axis=-1, keepdims=True)
        xc = xf - mu
        rstd = lax.rsqrt(jnp.mean(xc * xc, axis=-1, keepdims=True) + EPS)
        xhat = xc * rstd
        gv = g_ref[...]
        ln = xhat * gv + b_ref[...]
        sg = _sigmoid(ln)
        dln = ds_ref[...].astype(F32) * (sg + ln * sg * (1.0 - sg))
        dg_ref[...] += jnp.sum(dln * xhat, axis=0, keepdims=True)
        db_ref[...] += jnp.sum(dln, axis=0, keepdims=True)
        db2_ref[...] += jnp.sum(dx_ref[...], axis=0, keepdims=True)
        dxh = dln * gv
        dcv_ref[...] = rstd * (dxh - jnp.mean(dxh, axis=-1, keepdims=True)
                               - xhat * jnp.mean(dxh * xhat, axis=-1, keepdims=True))

    return pl.pallas_call(
        body, out_shape=[SDS((S, D), F32), SDS((1, D), F32), SDS((1, D), F32), SDS((1, D), F32)], grid=(S // tm,),
        in_specs=[_row(tm, D), _row(tm, D), _full((1, D)), _full((1, D)), _row(tm, D)],
        out_specs=[_row(tm, D), _full((1, D)), _full((1, D)), _full((1, D))], name="ln_silu_bwd",
        compiler_params=_cp(1),
    )(ds, cv, g, b, dx)


_GELU_C, _GELU_K = 0.7978845608028654, 0.044715


def _gelu(x, with_grad=False):
    x2 = x * x
    t = jnp.tanh(_GELU_C * (x + _GELU_K * x * x2))
    gel = 0.5 * x * (1.0 + t)
    if not with_grad:
        return gel
    return gel, 0.5 * (1.0 + t) + 0.5 * x * (1.0 - t * t) * (_GELU_C * (1.0 + 3.0 * _GELU_K * x2))


def _expm1(x):
    poly = x * (1.0 + x * (0.5 + x * (1.0 / 6.0 + x * (1.0 / 24.0 + x * (1.0 / 120.0)))))
    return jnp.where(jnp.abs(x) < 0.05, poly, jnp.exp(x) - 1.0)


def _softplus(x):
    return jnp.maximum(x, 0.0) + jnp.log1p(jnp.exp(-jnp.abs(x)))


_SCAN_UNROLL = 8
_RB = 32
_HB = 16


def _sub_blocks(n_rows, n_lanes, fn):
    def step(idx, c):
        r0 = pl.multiple_of(idx * _RB, _RB)
        for lt in range(n_lanes // LANE):
            fn(r0, lt)
        return c

    lax.fori_loop(0, n_rows // _RB, step, 0)


def _lanes(lt):
    return pl.ds(lt * LANE, LANE)


def _psum8(x):
    parts = [x[i * SUB:(i + 1) * SUB] for i in range(x.shape[0] // SUB)]
    return functools.reduce(lambda p, q: p + q, parts)


def _scan_fwd(a_s, b_s, out_ref, carry_ref, n_groups):
    row = lax.broadcasted_iota(jnp.int32, (SUB, LANE), 0)
    U = _SCAN_UNROLL

    def step(gi, carry):
        base = gi * (SUB * U)
        parts = []
        for u in range(U):
            i = pl.multiple_of(base + u * SUB, SUB)
            a8, b8 = a_s[pl.ds(i, SUB), :], b_s[pl.ds(i, SUB), :]
            for s in (1, 2, 4):
                a_sh = jnp.where(row >= s, pltpu.roll(a8, s, 0), 1.0)
                b_sh = jnp.where(row >= s, pltpu.roll(b8, s, 0), 0.0)
                b8 = a8 * b_sh + b8
                a8 = a8 * a_sh
            parts.append((i, a8, b8))
        for i, a8, b8 in parts:
            h8 = a8 * carry + b8
            out_ref[pl.ds(i, SUB), :] = h8
            carry = jnp.broadcast_to(h8[SUB - 1:SUB, :], (SUB, LANE))
        return carry

    carry_ref[...] = lax.fori_loop(0, n_groups // U, step, carry_ref[...])


def _scan_bwd(a_s, b_s, out_ref, carry_ref, n_groups):
    row = lax.broadcasted_iota(jnp.int32, (SUB, LANE), 0)
    U = _SCAN_UNROLL

    def step(gi, carry):
        base = (n_groups // U - 1 - gi) * (SUB * U)
        parts = []
        for u in reversed(range(U)):
            i = pl.multiple_of(base + u * SUB, SUB)
            a8, b8 = a_s[pl.ds(i, SUB), :], b_s[pl.ds(i, SUB), :]
            for s in (1, 2, 4):
                a_sh = jnp.where(row < SUB - s, pltpu.roll(a8, SUB - s, 0), 1.0)
                b_sh = jnp.where(row < SUB - s, pltpu.roll(b8, SUB - s, 0), 0.0)
                b8 = a8 * b_sh + b8
                a8 = a8 * a_sh
            parts.append((i, a8, b8))
        for i, a8, b8 in parts:
            h8 = a8 * carry + b8
            out_ref[pl.ds(i, SUB), :] = h8
            carry = jnp.broadcast_to(h8[0:1, :], (SUB, LANE))
        return carry

    carry_ref[...] = lax.fori_loop(0, n_groups // U, step, carry_ref[...])


def _rglru_pre(xr, wgx_ref, bgx_ref, wga_ref, bga_ref, lam_ref):
    xrb = xr.astype(BF16)
    wgx, wga = wgx_ref[0].astype(BF16), wga_ref[0].astype(BF16)
    gx = _sigmoid(lax.dot_general(xrb, wgx, _NN, preferred_element_type=F32) + bgx_ref[...])
    ga = _sigmoid(lax.dot_general(xrb, wga, _NN, preferred_element_type=F32) + bga_ref[...])
    sp = _softplus(-lam_ref[...])
    log_a = -C_RG * ga * sp
    a = jnp.exp(log_a)
    mult = jnp.sqrt(-_expm1(2.0 * log_a))
    return gx, ga, sp, a, mult, xrb, wgx, wga


def _a_specs():
    vec = pl.BlockSpec((1, HD_A), lambda c, j: (0, c))
    mat = pl.BlockSpec((1, HD_A, HD_A), lambda c, j: (c, 0, 0))
    return [pl.BlockSpec((CONV_A, HD_A), lambda c, j: (0, c)), vec, mat, vec, mat, vec, vec]


def _a_fwd(zp, conv_w, conv_b, wgx, bgx, wga, bga, lam):
    S = zp.shape[0]
    R, nt = R_RGLRU, D // HD_A
    H = SUB

    def body(zg_ref, zr_ref, cw_ref, cb_ref, wgx_ref, bgx_ref, wga_ref, bga_ref, lam_ref, ya_ref, h_ref,
             ext, a_s, b_s, hc):
        j = pl.program_id(1)

        @pl.when(j == 0)
        def _():
            ext[0:H, :] = jnp.zeros((H, HD_A), F32)
            hc[...] = jnp.zeros_like(hc)

        ext[H:H + R, :] = zr_ref[...].astype(F32)
        xr = cb_ref[...]
        for k in range(CONV_A):
            xr = xr + cw_ref[k:k + 1, :] * ext[pl.ds(H - (CONV_A - 1 - k), R), :]
        gx, _, _, a, mult, _, _, _ = _rglru_pre(xr, wgx_ref, bgx_ref, wga_ref, bga_ref, lam_ref)
        a_s[...] = a
        b_s[...] = mult * (gx * xr)
        _scan_fwd(a_s, b_s, h_ref, hc, R // SUB)
        ya_ref[...] = (_gelu(zg_ref[...].astype(F32)) * h_ref[...]).astype(BF16)
        ext[0:H, :] = ext[R:R + H, :]

    return pl.pallas_call(
        body, out_shape=[SDS((S, D + D // 2), BF16), SDS((S, D), F32)], grid=(nt, S // R),
        in_specs=[pl.BlockSpec((R, HD_A), lambda c, j: (j, c)), pl.BlockSpec((R, HD_A), lambda c, j: (j, nt + c))]
        + _a_specs(),
        out_specs=[pl.BlockSpec((R, HD_A), lambda c, j: (j, c)), pl.BlockSpec((R, HD_A), lambda c, j: (j, c))],
        scratch_shapes=[pltpu.VMEM((H + R, HD_A), F32), pltpu.VMEM((R, HD_A), F32), pltpu.VMEM((R, HD_A), F32),
                        pltpu.VMEM((SUB, HD_A), F32)],
        name="rglru_fwd", compiler_params=_cp(2),
    )(zp, zp, conv_w, conv_b, wgx, bgx, wga, bga, lam)


def _a_bwd(dyab, zp, h, conv_w, conv_b, wgx, bgx, wga, bga, lam):
    S = zp.shape[0]
    R, nt, nch = R_RGLRU, D // HD_A, S // R_RGLRU
    H = SUB

    def rows(c, j):
        return (nch - 1 - j, c)

    def rows_rec(c, j):
        return (nch - 1 - j, nt + c)

    def halo(c, j):
        return (jnp.maximum((nch - 1 - j) * (R // H) - 1, 0), c)

    def halo_z(c, j):
        return (jnp.maximum((nch - 1 - j) * (R // _HB) - 1, 0), nt + c)

    def body(dy_ref, zg_ref, zr_ref, zh_ref, h_ref, hh_ref, cw_ref, cb_ref, wgx_ref, bgx_ref, wga_ref, bga_ref,
             lam_ref, dzg_ref, dzr_ref, dcw_ref, dcb_ref, dwgx_ref, dbgx_ref, dwga_ref, dbga_ref, dlam_ref,
             ext_z, ext_h, ext_mu, ext_d, a_s, b_s, muc):
        j = pl.program_id(1)
        first_chunk = (nch - 1 - j) == 0

        @pl.when(j == 0)
        def _():
            ext_mu[R:R + H, :] = jnp.zeros((H, HD_A), F32)
            ext_d[R:R + H, :] = jnp.zeros((H, HD_A), F32)
            muc[...] = jnp.zeros_like(muc)
            for r in (dcw_ref, dcb_ref, dwgx_ref, dbgx_ref, dwga_ref, dbga_ref, dlam_ref):
                r[...] = jnp.zeros_like(r)

        zg = zg_ref[...].astype(F32)
        ext_z[0:H, :] = jnp.where(first_chunk, 0.0, zh_ref[_HB - H:_HB, :].astype(F32))
        ext_z[H:H + R, :] = zr_ref[...].astype(F32)
        ext_h[0:H, :] = jnp.where(first_chunk, 0.0, hh_ref[...])
        ext_h[H:H + R, :] = h_ref[...]
        xr = cb_ref[...]
        for k in range(CONV_A):
            xr = xr + cw_ref[k:k + 1, :] * ext_z[pl.ds(H - (CONV_A - 1 - k), R), :]
        gx, ga, sp, a, mult, xrb, wgxb, wgab = _rglru_pre(xr, wgx_ref, bgx_ref, wga_ref, bga_ref, lam_ref)
        gel, dgel = _gelu(zg, with_grad=True)
        dy = dy_ref[...].astype(F32)
        dh = dy * gel
        dzg_ref[...] = (dy * h_ref[...] * dgel).astype(BF16)
        a_s[...] = a
        b_s[...] = a * dh
        _scan_bwd(a_s, b_s, ext_mu, muc, R // SUB)
        lam_t = dh + ext_mu[pl.ds(1, R), :]
        ext_mu[R:R + H, :] = ext_mu[0:H, :]
        da = lam_t * ext_h[pl.ds(H - 1, R), :]
        gxr = gx * xr
        dlog_a = da * a - (lam_t * gxr) * (a * a) / mult
        dgx = lam_t * mult * xr
        dxr = lam_t * mult * gx
        lam_v = lam_ref[...]
        dlam_ref[...] += jnp.sum(dlog_a * ga, axis=0, keepdims=True) * (C_RG * _sigmoid(-lam_v))
        dpa = (dlog_a * (-C_RG * sp)) * ga * (1.0 - ga)
        dpx = dgx * gx * (1.0 - gx)
        dbga_ref[...] += jnp.sum(dpa, axis=0, keepdims=True)
        dbgx_ref[...] += jnp.sum(dpx, axis=0, keepdims=True)
        dpab, dpxb = dpa.astype(BF16), dpx.astype(BF16)
        dwga_ref[0] += lax.dot_general(xrb, dpab, _TN, preferred_element_type=F32)
        dwgx_ref[0] += lax.dot_general(xrb, dpxb, _TN, preferred_element_type=F32)
        dxr = (dxr + lax.dot_general(dpab, wgab, _NT, preferred_element_type=F32)
               + lax.dot_general(dpxb, wgxb, _NT, preferred_element_type=F32))
        dcb_ref[...] += jnp.sum(dxr, axis=0, keepdims=True)
        ext_d[0:R, :] = dxr
        dzr = jnp.zeros((R, HD_A), F32)
        for k in range(CONV_A):
            sh = CONV_A - 1 - k
            dcw_ref[k:k + 1, :] += jnp.sum(dxr * ext_z[pl.ds(H - sh, R), :], axis=0, keepdims=True)
            dzr = dzr + cw_ref[k:k + 1, :] * ext_d[pl.ds(sh, R), :]
        dzr_ref[...] = dzr.astype(BF16)
        ext_d[R:R + H, :] = ext_d[0:H, :]

    vec_o = pl.BlockSpec((1, HD_A), lambda c, j: (0, c))
    mat_o = pl.BlockSpec((1, HD_A, HD_A), lambda c, j: (c, 0, 0))
    return pl.pallas_call(
        body,
        out_shape=[SDS((S, D), BF16), SDS((S, D), BF16), SDS((CONV_A, D), F32), SDS((1, D), F32),
                   SDS((nt, HD_A, HD_A), F32), SDS((1, D), F32), SDS((nt, HD_A, HD_A), F32), SDS((1, D), F32),
                   SDS((1, D), F32)],
        grid=(nt, nch),
        in_specs=[pl.BlockSpec((R, HD_A), rows), pl.BlockSpec((R, HD_A), rows), pl.BlockSpec((R, HD_A), rows_rec),
                  pl.BlockSpec((_HB, HD_A), halo_z), pl.BlockSpec((R, HD_A), rows),
                  pl.BlockSpec((H, HD_A), halo)] + _a_specs(),
        out_specs=[pl.BlockSpec((R, HD_A), rows), pl.BlockSpec((R, HD_A), rows),
                   pl.BlockSpec((CONV_A, HD_A), lambda c, j: (0, c)), vec_o, mat_o, vec_o, mat_o, vec_o, vec_o],
        scratch_shapes=[pltpu.VMEM((H + R, HD_A), F32), pltpu.VMEM((H + R, HD_A), F32), pltpu.VMEM((R + H, HD_A), F32),
                        pltpu.VMEM((R + H, HD_A), F32), pltpu.VMEM((R, HD_A), F32), pltpu.VMEM((R, HD_A), F32),
                        pltpu.VMEM((SUB, HD_A), F32)],
        name="rglru_bwd", compiler_params=_cp(2),
    )(dyab, zp, zp, zp, h, h, conv_w, conv_b, wgx, bgx, wga, bga, lam)


_POOL_H = 16
_POOL_T0 = 2 * D // HD_A
_POOL_Y0 = D // HD_A


def _window_sum(lv, n, lo, rows, g, ahead):
    base = 0 if ahead else SUB
    cur, win = lv[0], None
    for i, s in enumerate((1, 2, 4, 8)):
        val = cur[pl.ds(base, n), :] + cur[pl.ds(base + (s if ahead else -s), n), :]
        sel = val[lo:lo + rows]
        win = sel if win is None else jnp.where(g >= i, sel, win)
        if i < 3:
            lv[i + 1][pl.ds(base, n), :] = val
            cur = lv[i + 1]
    return win


def _pool_width(g):
    return jnp.where(g == 0, 2.0, jnp.where(g == 1, 4.0, jnp.where(g == 2, 8.0, 16.0)))


def _b_fwd(zp, yab, wg, bg, sc):
    S = zp.shape[0]
    R, H = min(S, R_POOL), _POOL_H

    def body(z_ref, wg_ref, bg_ref, sc_ref, yab_in, yb_ref, *lv):
        del yab_in
        g, j = pl.program_id(0), pl.program_id(1)

        @pl.when(j == 0)
        def _():
            for r in lv:
                r[0:SUB, :] = jnp.zeros((SUB, HD_A), F32)
            lv[0][SUB:SUB + H, :] = jnp.zeros((H, HD_A), F32)

        u = z_ref[...].astype(F32)
        lv[0][SUB + H:SUB + H + R, :] = u
        t1 = (j * R + 1 + lax.broadcasted_iota(jnp.int32, (R, HD_A), 0)).astype(F32)
        p = _window_sum(lv, H + R, H, R, g, False) / jnp.minimum(t1, _pool_width(g)) - u
        lin = lax.dot_general(p.astype(BF16), wg_ref[0].astype(BF16), _NN, preferred_element_type=F32) + bg_ref[...]
        yb_ref[...] = (lin * sc_ref[...]).astype(BF16)
        lv[0][SUB:SUB + H, :] = lv[0][SUB + R:SUB + R + H, :]

    vec = pl.BlockSpec((1, HD_A), lambda g, j: (0, g))
    return pl.pallas_call(
        body, out_shape=SDS(yab.shape, yab.dtype), grid=(len(POOL_WINDOWS), S // R),
        in_specs=[pl.BlockSpec((R, HD_A), lambda g, j: (j, _POOL_T0 + g)),
                  pl.BlockSpec((1, HD_A, HD_A), lambda g, j: (g, 0, 0)), vec, vec, pl.BlockSpec(memory_space=pl.ANY)],
        out_specs=pl.BlockSpec((R, HD_A), lambda g, j: (j, _POOL_Y0 + g)),
        scratch_shapes=[pltpu.VMEM((SUB + H + R, HD_A), F32)] * 4, input_output_aliases={4: 0},
        name="pool_fwd", compiler_params=_cp(2),
    )(zp, wg, bg, sc, yab)


def _b_bwd(dyab, zp, wg, bg, sc):
    S = zp.shape[0]
    R, H, ng = min(S, R_POOL), _POOL_H, len(POOL_WINDOWS)
    nch = S // R

    def body(dy_ref, z_ref, zh_ref, wg_ref, bg_ref, sc_ref, dz_ref, dwg_ref, dbg_ref, dsc_ref, *scratch):
        lu, lq = scratch[:4], scratch[4:]
        g, j = pl.program_id(0), pl.program_id(1)
        jj = nch - 1 - j

        @pl.when(j == 0)
        def _():
            for r in lu:
                r[0:SUB, :] = jnp.zeros((SUB, HD_A), F32)
            for r in lq:
                r[R + H:R + H + SUB, :] = jnp.zeros((SUB, HD_A), F32)
            lq[0][R:R + H, :] = jnp.zeros((H, HD_A), F32)
            for r in (dwg_ref, dbg_ref, dsc_ref):
                r[...] = jnp.zeros_like(r)

        u = z_ref[...].astype(F32)
        lu[0][SUB:SUB + H, :] = jnp.where(jj == 0, 0.0, zh_ref[...].astype(F32))
        lu[0][SUB + H:SUB + H + R, :] = u
        t1 = (jj * R + 1 + lax.broadcasted_iota(jnp.int32, (R, HD_A), 0)).astype(F32)
        cnt = jnp.minimum(t1, _pool_width(g))
        pb = (_window_sum(lu, H + R, H, R, g, False) / cnt - u).astype(BF16)
        wgb = wg_ref[0].astype(BF16)
        lin = lax.dot_general(pb, wgb, _NN, preferred_element_type=F32) + bg_ref[...]
        dy = dy_ref[...].astype(F32)
        dsc_ref[...] += jnp.sum(dy * lin, axis=0, keepdims=True)
        dlin = dy * sc_ref[...]
        dbg_ref[...] += jnp.sum(dlin, axis=0, keepdims=True)
        dlb = dlin.astype(BF16)
        dwg_ref[0] += lax.dot_general(pb, dlb, _TN, preferred_element_type=F32)
        dp = lax.dot_general(dlb, wgb, _NT, preferred_element_type=F32)
        lq[0][0:R, :] = dp / cnt
        dz_ref[...] = (_window_sum(lq, R + H, 0, R, g, True) - dp).astype(BF16)
        lq[0][R:R + H, :] = lq[0][0:H, :]

    vec = pl.BlockSpec((1, HD_A), lambda g, j: (0, g))
    mat = pl.BlockSpec((1, HD_A, HD_A), lambda g, j: (g, 0, 0))
    return pl.pallas_call(
        body, out_shape=[SDS((S, D // 2), BF16), SDS((ng, HD_A, HD_A), F32), SDS((1, D // 2), F32),
                         SDS((1, D // 2), F32)],
        grid=(ng, nch),
        in_specs=[pl.BlockSpec((R, HD_A), lambda g, j: (nch - 1 - j, _POOL_Y0 + g)),
                  pl.BlockSpec((R, HD_A), lambda g, j: (nch - 1 - j, _POOL_T0 + g)),
                  pl.BlockSpec((H, HD_A), lambda g, j: (jnp.maximum((nch - 1 - j) * (R // H) - 1, 0), _POOL_T0 + g)),
                  mat, vec, vec],
        out_specs=[pl.BlockSpec((R, HD_A), lambda g, j: (nch - 1 - j, g)), mat, vec, vec],
        scratch_shapes=[pltpu.VMEM((SUB + H + R, HD_A), F32)] * 8,
        name="pool_bwd", compiler_params=_cp(2),
    )(dyab, zp, zp, wg, bg, sc)


_CW_F = 768


def _f_fwd(hp, w, b, name):
    S = hp.shape[0]
    R, H, cw = min(S, R_FFN), SUB, _CW_F
    nlt = cw // LANE

    def body(h_ref, w_ref, b_ref, o_ref, gel_ref, ud_ref, ext):
        j = pl.program_id(1)

        @pl.when(j == 0)
        def _():
            ext[:, 0:H, :] = jnp.zeros((nlt, H, LANE), F32)

        def stage(r0, lt):
            ext[lt, pl.ds(pl.multiple_of(r0 + H, SUB), _RB), :] = h_ref[pl.ds(r0, _RB), _lanes(lt)].astype(F32)

        def main(r0, lt):
            ls = _lanes(lt)
            gp = b_ref[:, ls]
            for k in range(CONV_F):
                gp = gp + w_ref[k:k + 1, ls] * ext[lt, pl.ds(r0 + (H - (CONV_F - 1 - k)), _RB), :]
            up = h_ref[pl.ds(r0, _RB), _lanes(lt + nlt)].astype(F32)
            gel, dgel = _gelu(gp, with_grad=True)
            rs = pl.ds(r0, _RB)
            o_ref[rs, ls] = (gel * up).astype(BF16)
            gel_ref[rs, ls] = gel.astype(BF16)
            ud_ref[rs, ls] = (up * dgel).astype(BF16)

        _sub_blocks(R, cw, stage)
        _sub_blocks(R, cw, main)
        ext[:, 0:H, :] = ext[:, R:R + H, :]

    tile = pl.BlockSpec((R, cw), lambda c, j: (j, c))
    return pl.pallas_call(
        body, out_shape=[SDS((S, D_FF), BF16)] * 3, grid=(D_FF // cw, S // R),
        in_specs=[pl.BlockSpec((R, 2 * cw), lambda c, j: (j, c)), pl.BlockSpec((CONV_F, cw), lambda c, j: (0, c)),
                  pl.BlockSpec((1, cw), lambda c, j: (0, c))],
        out_specs=[tile] * 3,
        scratch_shapes=[pltpu.VMEM((nlt, H + R, LANE), F32)], name=name, compiler_params=_cp(2),
    )(hp, w, b)


def _f_bwd(dact, hp, gel, ud, w, name):
    S = hp.shape[0]
    R, H, cw = min(S, R_FFN), SUB, _CW_F
    nch = S // R
    nlt = cw // LANE

    def body(da_ref, h_ref, hh_ref, gel_ref, ud_ref, w_ref, dh_ref, dw_ref, db_ref, ext_g, ext_d, acc):
        j = pl.program_id(1)
        jj = nch - 1 - j

        @pl.when(j == 0)
        def _():
            ext_d[:, R:R + H, :] = jnp.zeros((nlt, H, LANE), F32)
            acc[...] = jnp.zeros_like(acc)

        for lt in range(nlt):
            ext_g[lt, 0:H, :] = jnp.where(jj == 0, 0.0, hh_ref[_HB - H:_HB, lt * LANE:(lt + 1) * LANE].astype(F32))

        def stage(r0, lt):
            ext_g[lt, pl.ds(pl.multiple_of(r0 + H, SUB), _RB), :] = h_ref[pl.ds(r0, _RB), _lanes(lt)].astype(F32)

        def first(r0, lt):
            ls, lu, rs = _lanes(lt), _lanes(lt + nlt), pl.ds(r0, _RB)
            da = da_ref[rs, ls].astype(F32)
            dh_ref[rs, lu] = (da * gel_ref[rs, ls].astype(F32)).astype(BF16)
            dgp = da * ud_ref[rs, ls].astype(F32)
            ext_d[lt, rs, :] = dgp
            acc[CONV_F * SUB:(CONV_F + 1) * SUB, ls] += _psum8(dgp)
            for k in range(CONV_F):
                tap = ext_g[lt, pl.ds(r0 + (H - (CONV_F - 1 - k)), _RB), :]
                acc[k * SUB:(k + 1) * SUB, ls] += _psum8(dgp * tap)

        def second(r0, lt):
            ls = _lanes(lt)
            dhg = w_ref[CONV_F - 1:CONV_F, ls] * ext_d[lt, pl.ds(r0, _RB), :]
            for k in range(CONV_F - 1):
                dhg = dhg + w_ref[k:k + 1, ls] * ext_d[lt, pl.ds(r0 + (CONV_F - 1 - k), _RB), :]
            dh_ref[pl.ds(r0, _RB), ls] = dhg.astype(BF16)

        _sub_blocks(R, cw, stage)
        _sub_blocks(R, cw, first)
        _sub_blocks(R, cw, second)
        ext_d[:, R:R + H, :] = ext_d[:, 0:H, :]

        @pl.when(j == nch - 1)
        def _():
            for k in range(CONV_F):
                dw_ref[k:k + 1, :] = jnp.sum(acc[k * SUB:(k + 1) * SUB, :], axis=0, keepdims=True)
            db_ref[...] = jnp.sum(acc[CONV_F * SUB:(CONV_F + 1) * SUB, :], axis=0, keepdims=True)

    rows = lambda c, j: (nch - 1 - j, c)
    return pl.pallas_call(
        body, out_shape=[SDS((S, 2 * D_FF), BF16), SDS((CONV_F, D_FF), F32), SDS((1, D_FF), F32)],
        grid=(D_FF // cw, nch),
        in_specs=[pl.BlockSpec((R, cw), rows), pl.BlockSpec((R, cw), lambda c, j: (nch - 1 - j, 2 * c)),
                  pl.BlockSpec((_HB, cw), lambda c, j: (jnp.maximum((nch - 1 - j) * (R // _HB) - 1, 0), 2 * c)),
                  pl.BlockSpec((R, cw), rows), pl.BlockSpec((R, cw), rows),
                  pl.BlockSpec((CONV_F, cw), lambda c, j: (0, c))],
        out_specs=[pl.BlockSpec((R, 2 * cw), rows), pl.BlockSpec((CONV_F, cw), lambda c, j: (0, c)),
                   pl.BlockSpec((1, cw), lambda c, j: (0, c))],
        scratch_shapes=[pltpu.VMEM((nlt, H + R, LANE), F32), pltpu.VMEM((nlt, R + H, LANE), F32),
                        pltpu.VMEM(((CONV_F + 1) * SUB, cw), F32)], name=name,
        compiler_params=_cp(2),
    )(dact, hp, hp, gel, ud, w)


_CW_C = 256
_H_C = 32


def _c_fwd(h1p, w, b):
    S = h1p.shape[0]
    R, H, cw = R_SEQ, _H_C, _CW_C
    nlt = cw // LANE

    def body(h_ref, w_ref, b_ref, o_ref, ext):
        j = pl.program_id(1)

        @pl.when(j == 0)
        def _():
            ext[:, 0:H, :] = jnp.zeros((nlt, H, LANE), F32)

        def stage(r0, lt):
            rs = pl.ds(r0, _RB)
            gate = h_ref[rs, _lanes(lt + nlt)].astype(F32)
            ext[lt, pl.ds(pl.multiple_of(r0 + H, SUB), _RB), :] = h_ref[rs, _lanes(lt)].astype(F32) * _sigmoid(gate)

        def main(r0, lt):
            ls = _lanes(lt)
            cv = b_ref[:, ls]
            for k in range(CONV_C):
                cv = cv + w_ref[k:k + 1, ls] * ext[lt, pl.ds(r0 + (H - (CONV_C - 1 - k)), _RB), :]
            o_ref[pl.ds(r0, _RB), ls] = cv

        _sub_blocks(R, cw, stage)
        _sub_blocks(R, cw, main)
        ext[:, 0:H, :] = ext[:, R:R + H, :]

    return pl.pallas_call(
        body, out_shape=SDS((S, D), F32), grid=(D // cw, S // R),
        in_specs=[pl.BlockSpec((R, 2 * cw), lambda c, j: (j, c)), pl.BlockSpec((CONV_C, cw), lambda c, j: (0, c)),
                  pl.BlockSpec((1, cw), lambda c, j: (0, c))],
        out_specs=pl.BlockSpec((R, cw), lambda c, j: (j, c)),
        scratch_shapes=[pltpu.VMEM((nlt, H + R, LANE), F32)], name="conf_conv_fwd", compiler_params=_cp(2),
    )(h1p, w, b)


def _c_bwd(dcv, h1p, w):
    S = h1p.shape[0]
    R, H, cw, nch = R_SEQ, _H_C, _CW_C, S // R_SEQ
    nlt = cw // LANE
    a_b, a_val, a_gate = CONV_C * SUB, (CONV_C + 1) * SUB, (CONV_C + 2) * SUB

    def body(dc_ref, h_ref, hh_ref, w_ref, dh_ref, dw_ref, db_ref, db1_ref, ext_u, ext_d, acc):
        j = pl.program_id(1)
        jj = nch - 1 - j

        @pl.when(j == 0)
        def _():
            ext_d[:, R:R + H, :] = jnp.zeros((nlt, H, LANE), F32)
            acc[...] = jnp.zeros_like(acc)

        for lt in range(nlt):
            ext_u[lt, 0:H, :] = jnp.where(
                jj == 0, 0.0, hh_ref[:, lt * LANE:(lt + 1) * LANE].astype(F32)
                * _sigmoid(hh_ref[:, cw + lt * LANE:cw + (lt + 1) * LANE].astype(F32)))

        def stage(r0, lt):
            rs, ls = pl.ds(r0, _RB), _lanes(lt)
            gate = h_ref[rs, _lanes(lt + nlt)].astype(F32)
            ext_u[lt, pl.ds(pl.multiple_of(r0 + H, SUB), _RB), :] = h_ref[rs, ls].astype(F32) * _sigmoid(gate)
            ext_d[lt, rs, :] = dc_ref[rs, ls]

        def first(r0, lt):
            ls = _lanes(lt)
            dc = dc_ref[pl.ds(r0, _RB), ls]
            acc[a_b:a_b + SUB, ls] += _psum8(dc)
            for k in range(CONV_C):
                tap = ext_u[lt, pl.ds(r0 + (H - (CONV_C - 1 - k)), _RB), :]
                acc[k * SUB:(k + 1) * SUB, ls] += _psum8(dc * tap)

        def second(r0, lt):
            rs, ls, lg = pl.ds(r0, _RB), _lanes(lt), _lanes(lt + nlt)
            du = w_ref[CONV_C - 1:CONV_C, ls] * ext_d[lt, rs, :]
            for k in range(CONV_C - 1):
                du = du + w_ref[k:k + 1, ls] * ext_d[lt, pl.ds(r0 + (CONV_C - 1 - k), _RB), :]
            val = h_ref[rs, ls].astype(F32)
            sg = _sigmoid(h_ref[rs, lg].astype(F32))
            dval = du * sg
            dgate = du * val * sg * (1.0 - sg)
            acc[a_val:a_val + SUB, ls] += _psum8(dval)
            acc[a_gate:a_gate + SUB, ls] += _psum8(dgate)
            dh_ref[rs, ls] = dval.astype(BF16)
            dh_ref[rs, lg] = dgate.astype(BF16)

        _sub_blocks(R, cw, stage)
        _sub_blocks(R, cw, first)
        _sub_blocks(R, cw, second)
        ext_d[:, R:R + H, :] = ext_d[:, 0:H, :]

        @pl.when(j == nch - 1)
        def _():
            for k in range(CONV_C):
                dw_ref[k:k + 1, :] = jnp.sum(acc[k * SUB:(k + 1) * SUB, :], axis=0, keepdims=True)
            db_ref[...] = jnp.sum(acc[a_b:a_b + SUB, :], axis=0, keepdims=True)
            db1_ref[:, 0:cw] = jnp.sum(acc[a_val:a_val + SUB, :], axis=0, keepdims=True)
            db1_ref[:, cw:2 * cw] = jnp.sum(acc[a_gate:a_gate + SUB, :], axis=0, keepdims=True)

    rows = lambda c, j: (nch - 1 - j, c)
    return pl.pallas_call(
        body, out_shape=[SDS((S, 2 * D), BF16), SDS((CONV_C, D), F32), SDS((1, D), F32), SDS((1, 2 * D), F32)],
        grid=(D // cw, nch),
        in_specs=[pl.BlockSpec((R, cw), rows), pl.BlockSpec((R, 2 * cw), rows),
                  pl.BlockSpec((H, 2 * cw), lambda c, j: (jnp.maximum((nch - 1 - j) * (R // H) - 1, 0), c)),
                  pl.BlockSpec((CONV_C, cw), lambda c, j: (0, c))],
        out_specs=[pl.BlockSpec((R, 2 * cw), rows), pl.BlockSpec((CONV_C, cw), lambda c, j: (0, c)),
                   pl.BlockSpec((1, cw), lambda c, j: (0, c)), pl.BlockSpec((1, 2 * cw), lambda c, j: (0, c))],
        scratch_shapes=[pltpu.VMEM((nlt, H + R, LANE), F32), pltpu.VMEM((nlt, R + H, LANE), F32),
                        pltpu.VMEM(((CONV_C + 3) * SUB, cw), F32)], name="conf_conv_bwd",
        compiler_params=_cp(2),
    )(dcv, h1p, h1p, w)


def _local_step(x, mem, tgt, W, fetch=None, send=None):
    G = {}
    W = dict(W)

    def arrive(group, after):
        if fetch is None:
            return None
        got, tok = fetch(group, after)
        for key, val in got.items():
            W[key] = {**W.get(key, {}), **val} if isinstance(val, dict) else val
        return tok

    def gain(g, tok):
        return g if tok is None else g + tok

    def sent(group):
        return None if send is None else send(group, G)

    def xattn_fwd(xin, n, l):
        tok = arrive(("xa", l), n)
        mn = _rms_fwd(mem, gain(W["xa_mem_norm"][l:l + 1], tok), f"xa_memnorm_fwd{l}")
        q = _mm_nn(n, W["xa_wq"][l], out_dtype=BF16, name=f"xa_q{l}")
        k = _mm_nn(mn, W["xa_wk"][l], out_dtype=BF16, name=f"xa_k{l}")
        v = _mm_nn(mn, W["xa_wv"][l], out_dtype=BF16, name=f"xa_v{l}")
        o = _attn_fwd(q, k, v, f"xa_attn_fwd{l}")
        xout, nout = _mm_nn(o, W["xa_wo"][l], out_dtype=F32, name=f"xa_o{l}", add=xin, norm=W["f_norm"][l:l + 1])
        return xout, nout, (xin, n, q, mn, k, v, o)

    def xattn_bwd(dx, dxb, saved, l):
        xin, n, q, mn, k, v, o = saved
        do = _mm_nt(dxb, W["xa_wo"][l], out_dtype=BF16, name=f"xa_do{l}")
        G[f"xa_wo{l}"] = _mm_tn(o, dxb, out_dtype=BF16, name=f"xa_dwo{l}")
        dq, dk, dv = _attn_bwd(q, k, v, do, f"xa_attn_bwd{l}")
        dkb, dvb = dk.astype(BF16), dv.astype(BF16)
        G[f"xa_wq{l}"] = _mm_tn(n, dq, out_dtype=BF16, name=f"xa_dwq{l}")
        G[f"xa_wk{l}"] = _mm_tn(mn, dkb, out_dtype=BF16, name=f"xa_dwk{l}")
        G[f"xa_wv{l}"] = _mm_tn(mn, dvb, out_dtype=BF16, name=f"xa_dwv{l}")
        tok = sent(("xa", l))
        dmn = _mm_nt(dkb, W["xa_wk"][l], out_dtype=F32, name=f"xa_dmn_k{l}")
        dmn = _mm_nt(dvb, W["xa_wv"][l], out_dtype=F32, name=f"xa_dmn_v{l}", add=dmn)
        (G[f"xa_mem_norm{l}"],) = _rms_bwd(mem, W["xa_mem_norm"][l:l + 1], dmn, None, f"xa_memnorm_bwd{l}")
        dx, dxb, G[f"xa_norm{l}"] = _mm_nt(dq, W["xa_wq"][l], out_dtype=F32, name=f"xa_dn{l}",
                                           rms=(xin, gain(W["xa_norm"][l:l + 1], tok), dx))
        return dx, dxb

    def ffn_fwd(xin, n, l, next_gain):
        tok = arrive(("f", l), n)
        hp = _mm_nn(n, W["f_w_up"][l], out_dtype=BF16, name=f"f_up{l}")
        act, gel, ud = _f_fwd(hp, W["f_dw_w"][l], gain(W["f_dw_b"][l:l + 1], tok), f"f_conv_fwd{l}")
        arrive(("fd", l), act)
        res = _mm_nn(act, W["f_w_down"][l], out_dtype=F32, name=f"f_down{l}", add=xin, norm=next_gain)
        xout, nout = res if next_gain is not None else (res, None)
        return xout, nout, (xin, n, hp, act, gel, ud)

    def ffn_bwd(dx, dxb, saved, l):
        xin, n, hp, act, gel, ud = saved
        dact = _mm_nt(dxb, W["f_w_down"][l], out_dtype=BF16, name=f"f_dact{l}")
        G[f"f_w_down{l}"] = _mm_tn(act, dxb, out_dtype=BF16, name=f"f_dwdown{l}")
        dhp, G[f"f_dw_w{l}"], G[f"f_dw_b{l}"] = _f_bwd(dact, hp, gel, ud, W["f_dw_w"][l], f"f_conv_bwd{l}")
        G[f"f_w_up{l}"] = _mm_tn(n, dhp, out_dtype=BF16, name=f"f_dwup{l}", blocks=_CW_F)
        tok = sent(("f", l))
        dx, dxb, G[f"f_norm{l}"] = _mm_nt(dhp, W["f_w_up"][l], out_dtype=F32, name=f"f_dn{l}",
                                          rms=(xin, gain(W["f_norm"][l:l + 1], tok), dx))
        return dx, dxb

    n0 = _rms_fwd(x, W["ab_norm"], "ab_norm_fwd")
    tok = arrive(("ab", 0), n0)
    a_par = (W["a_conv_w"], gain(W["a_conv_b"], tok), W["a_gate_x_w"], W["a_gate_x_b"], W["a_gate_a_w"],
             W["a_gate_a_b"], W["a_lambda"])
    b_par = (W["b_group_w"], W["b_group_b"], W["b_scale"])
    zp = _mm_nn(n0, W["ab_w_in"], out_dtype=BF16, name="ab_in")
    yab, h_a = _a_fwd(zp, *a_par)
    yab = _b_fwd(zp, yab, *b_par)
    tok = arrive(("ab", 1), yab)
    x1, n1 = _mm_nn(yab, W["ab_w_out"], out_dtype=F32, name="ab_out", add=x, norm=gain(W["xa_norm"][0:1], tok))
    x2, n2, s_xa0 = xattn_fwd(x1, n1, 0)
    x3, n3, s_f0 = ffn_fwd(x2, n2, 0, W["c_norm"])
    tok = arrive(("c", 0), n3)
    h1p = _mm_nn(n3, W["c_w_pw1"], out_dtype=BF16, name="c_pw1", bias=gain(W["c_b_pw1"], tok))
    cv = _c_fwd(h1p, W["c_dw_w"], W["c_dw_b"])
    sc = _ln_silu_fwd(cv, W["c_ln_g"], W["c_ln_b"])
    x4, n4 = _mm_nn(sc, W["c_w_pw2"], out_dtype=F32, name="c_pw2", bias=W["c_b_pw2"], add=x3, norm=W["xa_norm"][1:2])
    x5, n5, s_xa1 = xattn_fwd(x4, n4, 1)
    x6, _, s_f1 = ffn_fwd(x5, n5, 1, None)
    loss, dx, dxb, G["final_norm"] = _loss_head(x6, W["final_norm"], tgt)

    dx, dxb = ffn_bwd(dx, dxb, s_f1, 1)
    dx, dxb = xattn_bwd(dx, dxb, s_xa1, 1)
    dsc = _mm_nt(dxb, W["c_w_pw2"], out_dtype=BF16, name="c_dsc")
    G["c_w_pw2"] = _mm_tn(sc, dxb, out_dtype=BF16, name="c_dwpw2")
    dcv, G["c_ln_g"], G["c_ln_b"], G["c_b_pw2"] = _ln_silu_bwd(dsc, cv, W["c_ln_g"], W["c_ln_b"], dx)
    dh1p, G["c_dw_w"], G["c_dw_b"], G["c_b_pw1"] = _c_bwd(dcv, h1p, W["c_dw_w"])
    G["c_w_pw1"] = _mm_tn(n3, dh1p, out_dtype=BF16, name="c_dwpw1", blocks=_CW_C)
    tok = sent(("c", 0))
    dx, dxb, G["c_norm"] = _mm_nt(dh1p, W["c_w_pw1"], out_dtype=F32, name="c_dn",
                                  rms=(x3, gain(W["c_norm"], tok), dx))
    dx, dxb = ffn_bwd(dx, dxb, s_f0, 0)
    dx, dxb = xattn_bwd(dx, dxb, s_xa0, 0)
    dyab = _mm_nt(dxb, W["ab_w_out"], out_dtype=BF16, name="ab_dyab")
    G["ab_w_out"] = _mm_tn(yab, dxb, out_dtype=BF16, name="ab_dwout")
    tok = sent(("ab", 1))
    a_par = (a_par[0], gain(a_par[1], tok)) + a_par[2:]
    (dzg, dzr, G["a_conv_w"], G["a_conv_b"], G["a_gate_x_w"], G["a_gate_x_b"], G["a_gate_a_w"], G["a_gate_a_b"],
     G["a_lambda"]) = _a_bwd(dyab, zp, h_a, *a_par)
    dzq, G["b_group_w"], G["b_group_b"], G["b_scale"] = _b_bwd(dyab, zp, *b_par)
    G["ab_w_in"] = jnp.concatenate(
        [_mm_tn(n0, dz, out_dtype=BF16, name=f"ab_dwin_{part}")
         for part, dz in (("gate", dzg), ("rec", dzr), ("pool", dzq))], axis=1)
    tok = sent(("ab", 0))
    dx, _, G["ab_norm"] = _mm_nt_cols([dzg, dzr, dzq], W["ab_w_in"], name="ab_dn",
                                      rms=(x, gain(W["ab_norm"], tok), dx))
    return loss, dx, G


def _my_place():
    x, y, c = lax.axis_index("x"), lax.axis_index("y"), lax.axis_index("c")
    return x, y, c


def _all_gather(shards, name):
    n = len(shards)

    def body(*refs):
        ins, outs = refs[:n], refs[n:2 * n]
        send_sems, recv_sems, local_sems = refs[2 * n:]
        x, y, c = _my_place()
        me, sibling = (x, y, c), (x, y, 1 - c)
        chips = [(1 - x, y), (x, 1 - y), (1 - x, 1 - y)]

        def slab(a, place):
            px, py, pc = place
            return outs[a].at[4 * px + 2 * py + pc]

        def copy(a, k, block, to, src=None):
            return pltpu.make_async_remote_copy(
                src_ref=slab(a, block) if src is None else src, dst_ref=slab(a, block),
                send_sem=send_sems.at[a, k], recv_sem=recv_sems.at[a, k], device_id=to, device_id_type=MESH)

        mine = [pltpu.make_async_copy(ins[a], slab(a, me), local_sems.at[a]) for a in range(n)]
        for cp in mine:
            cp.start()
        first = []
        for j, chip in enumerate(chips):
            first += [copy(a, 1 + j, me, (*chip, c), src=ins[a]) for a in range(n)]
        first += [copy(a, 0, me, sibling, src=ins[a]) for a in range(n)]
        for cp in first:
            cp.start()
        passed = []
        for j, chip in enumerate(chips):
            for a in range(n):
                copy(a, 1 + j, (*chip, c), me).wait_recv()
                cp = copy(a, 4 + j, (*chip, c), sibling)
                cp.start()
                passed.append(cp)
        for a in range(n):
            copy(a, 0, sibling, me).wait_recv()
        for j, chip in enumerate(chips):
            for a in range(n):
                copy(a, 4 + j, (*chip, 1 - c), me).wait_recv()
        for cp in first + passed:
            cp.wait_send()
        for cp in mine:
            cp.wait()

    any_spec = pl.BlockSpec(memory_space=pl.ANY)
    return pl.pallas_call(
        body, out_shape=[SDS((N_DEV,) + s.shape, s.dtype) for s in shards], in_specs=[any_spec] * n,
        out_specs=[any_spec] * n,
        scratch_shapes=[pltpu.SemaphoreType.DMA((n, 7)), pltpu.SemaphoreType.DMA((n, 7)), pltpu.SemaphoreType.DMA((n,))],
        name=name,
    )(*shards)


_HBM = pl.BlockSpec(memory_space=pltpu.HBM)
_SEM = pl.BlockSpec(memory_space=pltpu.SEMAPHORE)
_EFFECT = pltpu.SideEffectType.DATAFLOW_SIDE_EFFECTING


def _peer_places():
    x, y, c = _my_place()
    peers = []
    for k in range(1, N_DEV):
        px = 1 - x if (k >> 2) & 1 else x
        py = 1 - y if (k >> 1) & 1 else y
        pc = 1 - c if k & 1 else c
        peers.append(((px, py, pc), 4 * px + 2 * py + pc))
    return (x, y, c), 4 * x + 2 * y + c, peers


def _send_start(srcs, per_dest, name):
    n = len(srcs)
    lands = [lax.empty((N_DEV,) + (s.shape[1:] if per_dest else s.shape), s.dtype) for s in srcs]

    def body(*refs):
        src, land = refs[:n], refs[n:2 * n]
        outs = refs[2 * n:]
        send, recv, token = outs[:n], outs[n:2 * n], outs[4 * n]
        place, me, peers = _peer_places()
        for a in range(n):
            for peer, pidx in peers + [(place, me)]:
                pltpu.make_async_remote_copy(
                    src_ref=src[a].at[pidx] if per_dest else src[a], dst_ref=land[a].at[me], send_sem=send[a],
                    recv_sem=recv[a], device_id=peer, device_id_type=MESH).start()
        token[...] = jnp.zeros_like(token)

    hbm = lambda a: pltpu.HBM(a.shape, a.dtype)
    sem = pltpu.SemaphoreType.DMA(())
    res = pl.pallas_call(
        body, name=name,
        out_shape=tuple([sem] * (2 * n) + [hbm(s) for s in srcs] + [hbm(l) for l in lands]
                        + [SDS((SUB, LANE), F32)]),
        in_specs=[_HBM] * (2 * n),
        out_specs=tuple([_SEM] * (2 * n) + [_HBM] * (2 * n) + [pl.BlockSpec(memory_space=pltpu.VMEM)]),
        input_output_aliases={i: 2 * n + i for i in range(2 * n)},
        compiler_params=pltpu.CompilerParams(has_side_effects=_EFFECT),
    )(*[pltpu.with_memory_space_constraint(s, pltpu.HBM) for s in srcs],
      *[pltpu.with_memory_space_constraint(l, pltpu.HBM) for l in lands])
    return res[:n], res[n:2 * n], res[2 * n:3 * n], res[3 * n:4 * n], res[4 * n]


def _send_wait(send, recv, srcs, lands, after, per_dest, name):
    n = len(srcs)

    def body(*refs):
        src, land = refs[:n], refs[n:2 * n]
        send_s, recv_s = refs[2 * n:3 * n], refs[3 * n:4 * n]
        token = refs[-1]
        place, _, _ = _peer_places()
        for a in range(n):
            copy = pltpu.make_async_remote_copy(
                src_ref=src[a] if per_dest else land[a], dst_ref=land[a], send_sem=send_s[a],
                recv_sem=recv_s[a], device_id=place, device_id_type=MESH)
            copy.wait_send()
            copy.wait_recv()
        token[...] = jnp.zeros_like(token)

    hbm = lambda a: pltpu.HBM(a.shape, a.dtype)
    res = pl.pallas_call(
        body, name=name,
        out_shape=tuple([hbm(s) for s in srcs] + [hbm(l) for l in lands] + [SDS((SUB, LANE), F32)]),
        in_specs=[_HBM] * (2 * n) + [_SEM] * (2 * n) + [pl.BlockSpec(memory_space=pl.ANY)],
        out_specs=tuple([_HBM] * (2 * n) + [pl.BlockSpec(memory_space=pltpu.VMEM)]),
        input_output_aliases={i: i for i in range(2 * n)},
        compiler_params=pltpu.CompilerParams(has_side_effects=_EFFECT),
    )(*srcs, *lands, *send, *recv, after)
    return res[:n], res[n:2 * n], res[2 * n]


def _adamw_math(w, g, m, v):
    m = ADAM_B1 * m + (1.0 - ADAM_B1) * g
    v = ADAM_B2 * v + (1.0 - ADAM_B2) * (g * g)
    m_hat = m / (1.0 - ADAM_B1 ** ADAM_STEP)
    v_hat = v / (1.0 - ADAM_B2 ** ADAM_STEP)
    delta = -ADAM_LR * (m_hat / (jnp.sqrt(v_hat) + ADAM_EPS) + ADAM_WD * w)
    return delta, m, v


def _row_tile(r, c, itemsize_rows):
    cap = max(SUB, (itemsize_rows // (4 * c)) // SUB * SUB)
    if r <= cap:
        return r
    best = None
    for t in range(SUB, cap + 1, SUB):
        if r % t == 0:
            best = t
    return best if best is not None else r


def _sum_adamw(landing, w, m, v, name, layer=0, prev=None, after=None):
    _, r, c = landing.shape
    tr = _row_tile(r, c, 2 << 20)
    off = layer * (r // tr)
    tail = ([] if prev is None else list(prev)) + ([] if after is None else [after])

    def body(l_ref, w_ref, m_ref, v_ref, *rest):
        g_ref, d_ref, mo_ref, vo_ref = rest[-4:]
        g = l_ref[0].astype(F32)
        for s in range(1, N_DEV):
            g = g + l_ref[s].astype(F32)
        g_ref[...] = g
        d_ref[...], mo_ref[...], vo_ref[...] = _adamw_math(w_ref[...], g, m_ref[...], v_ref[...])

    blk = pl.BlockSpec((tr, c), lambda i: (i + off, 0))
    n_prev = 0 if prev is None else 4
    return pl.pallas_call(
        body, out_shape=[SDS(w.shape, F32)] * 4, grid=(r // tr,),
        in_specs=[pl.BlockSpec((N_DEV, tr, c), lambda i: (0, i, 0)), blk, blk, blk]
        + [pl.BlockSpec(memory_space=pl.ANY)] * len(tail),
        out_specs=[blk] * 4, input_output_aliases={4 + i: i for i in range(n_prev)}, name=name,
        compiler_params=_cp(1),
    )(landing, w, m, v, *tail)


def _sum8(landing, name):
    _, r, c = landing.shape

    def body(l_ref, g_ref):
        g = l_ref[0]
        for s in range(1, N_DEV):
            g = g + l_ref[s]
        g_ref[...] = g

    return pl.pallas_call(body, out_shape=SDS((r, c), F32), name=name, compiler_params=_cp(0))(landing)


def _adamw_small(repl_pack, own_pack, P, M, V):
    table, off = [], 0
    for name, shape in _REPL.items():
        table.append((name, shape if len(shape) > 1 else (1,) + shape, 0, off // LANE))
        off += _size(shape)
    off = _REPL_ROWS * LANE
    for name, shape in _SMALL_SHARDED.items():
        table.append((name, shape, 1, off // LANE))
        off += _size(shape)
    n = len(table)

    def body(*refs):
        packs, ins, outs = refs[:2], refs[2:2 + 3 * n], refs[2 + 3 * n:]
        for p, (_, shape, which, r0) in enumerate(table):
            w_ref, m_ref, v_ref = ins[3 * p:3 * p + 3]
            g_ref, d_ref, mo_ref, vo_ref = outs[4 * p:4 * p + 4]
            pack, rows, q = packs[which], shape[-2], shape[-1] // LANE
            lead = [()]
            for dim in shape[:-2]:
                lead = [t + (i,) for t in lead for i in range(dim)]
            for li, idx in enumerate(lead):
                if q == 1:
                    dst = g_ref.at[idx] if idx else g_ref
                    dst[...] = pack[r0 + li * rows:r0 + (li + 1) * rows, :]
                    continue
                for i in range(rows):
                    for k in range(q):
                        row = r0 + (li * rows + i) * q + k
                        g_ref[idx + (slice(i, i + 1), slice(k * LANE, (k + 1) * LANE))] = pack[row:row + 1, :]
            d_ref[...], mo_ref[...], vo_ref[...] = _adamw_math(w_ref[...], g_ref[...], m_ref[...], v_ref[...])

    ins, out_shape = [], []
    for name, shape, _, _ in table:
        ins += [t[name].reshape(shape) for t in (P, M, V)]
        out_shape += [SDS(shape, F32)] * 4
    res = pl.pallas_call(body, out_shape=out_shape, name="adamw_small", compiler_params=_cp(0))(
        repl_pack, own_pack, *ins)
    dicts = ({}, {}, {}, {})
    for p, (name, shape, _, _) in enumerate(table):
        for d, arr in zip(dicts, res[4 * p:4 * p + 4]):
            d[name] = arr.reshape(P[name].shape)
    return dicts


_BIG = {
    "ab_w_in": (1, D, 320), "ab_w_out": (1, 192, D), "c_w_pw1": (1, D, 256), "c_w_pw2": (1, 128, D),
    "xa_wq": (2, 128, D), "xa_wk": (2, 128, D), "xa_wv": (2, 128, D), "xa_wo": (2, 128, D),
    "f_w_up": (2, D, 768), "f_w_down": (2, 384, D),
}
_SMALL_SHARDED = {
    "a_conv_w": (1, 4, 128), "c_norm": (1, 128), "c_b_pw1": (1, 256), "c_dw_w": (1, 31, 128), "c_dw_b": (1, 128),
    "c_ln_g": (1, 128), "c_ln_b": (1, 128), "c_b_pw2": (1, 128), "f_dw_w": (2, 3, 384),
}
_REPL = {
    "ab_norm": (1, D), "a_conv_b": (1, D), "a_gate_x_w": (1, 8, 128, 128), "a_gate_x_b": (1, D),
    "a_gate_a_w": (1, 8, 128, 128), "a_gate_a_b": (1, D), "a_lambda": (1, D), "b_group_w": (1, 4, 128, 128),
    "b_group_b": (1, 512), "b_scale": (1, 512), "xa_norm": (2, D), "xa_mem_norm": (2, D), "f_norm": (2, D),
    "f_dw_b": (2, D_FF), "final_norm": (D,),
}


def _size(shape):
    n = 1
    for s in shape:
        n *= s
    return n


_N_SS = sum(_size(s) for s in _SMALL_SHARDED.values())
_N_REPL = sum(_size(s) for s in _REPL.values())
_REPL_ROWS = -(-_N_REPL // (N_DEV * SUB * LANE)) * SUB
_SS_ROWS = _N_SS // LANE
_SMALL_ROWS = -(-(_REPL_ROWS + _SS_ROWS) // SUB) * SUB


def _pack(parts, rows):
    flat = jnp.concatenate([p.reshape(-1).astype(F32) for p in parts])
    return jnp.pad(flat, (0, rows * LANE - flat.shape[0])).reshape(rows, LANE)


def _pair_blocks(v, bw):
    lead, n = v.shape[:-1], v.shape[-1]
    return jnp.swapaxes(v.reshape(lead + (2, n // (2 * bw), bw)), -3, -2).reshape(lead + (n,))


def _unpair_blocks(v, bw):
    lead, n = v.shape[:-1], v.shape[-1]
    return jnp.swapaxes(v.reshape(lead + (n // (2 * bw), 2, bw)), -3, -2).reshape(lead + (n,))


_GROUPS = {
    ("ab", 0): (("ab_w_in", 0),),
    ("ab", 1): (("ab_w_out", 0),),
    ("xa", 0): (("xa_wq", 0), ("xa_wk", 0), ("xa_wv", 0), ("xa_wo", 0)),
    ("f", 0): (("f_w_up", 0),),
    ("fd", 0): (("f_w_down", 0),),
    ("c", 0): (("c_w_pw1", 0), ("c_w_pw2", 0)),
    ("xa", 1): (("xa_wq", 1), ("xa_wk", 1), ("xa_wv", 1), ("xa_wo", 1)),
    ("f", 1): (("f_w_up", 1),),
    ("fd", 1): (("f_w_down", 1),),
}
_SEND_GROUPS = {g: m for g, m in _GROUPS.items() if g[0] != "fd"}
_SEND_GROUPS[("f", 0)] = (("f_w_up", 0), ("f_w_down", 0))
_SEND_GROUPS[("f", 1)] = (("f_w_up", 1), ("f_w_down", 1))


def _weight_layout(name, g):
    if name == "ab_w_in":
        return jnp.swapaxes(g, 0, 1).reshape(D, N_DEV * 320)
    if name in ("c_w_pw1", "f_w_up"):
        return g
    return g.reshape(N_DEV * g.shape[1], D)


def _grad_blocks(name, l, G):
    _, r, c = _BIG[name]
    if name == "ab_w_in":
        return jnp.swapaxes(G[name].reshape(D, N_DEV, 320), 0, 1)
    if name == "c_w_pw1":
        return G[name]
    if name == "f_w_up":
        return G[f"{name}{l}"]
    return (G[name] if _BIG[name][0] == 1 else G[f"{name}{l}"]).reshape(N_DEV, r, c)


def _small_layouts(sm):
    W = {}
    sm = sm.reshape(N_DEV, -1)
    off = 0
    for name, shape in _SMALL_SHARDED.items():
        n = _size(shape)
        blocks = sm[:, off:off + n].reshape((N_DEV,) + shape)
        off += n
        W[name] = jnp.moveaxis(blocks, 0, -2).reshape(shape[:-1] + (N_DEV * shape[-1],))
    W["a_conv_w"], W["c_dw_w"] = W["a_conv_w"][0], W["c_dw_w"][0]
    W["c_b_pw1"] = _pair_blocks(W["c_b_pw1"], _CW_C)
    return W


def _to_dest_major(g, shape):
    full = g.reshape(shape[:-1] + (N_DEV, shape[-1]))
    return jnp.moveaxis(full, -2, 0).reshape(N_DEV, -1)


def kernel(x, mem, ab_norm, ab_w_in, a_conv_w, a_conv_b, a_gate_x_w, a_gate_x_b, a_gate_a_w, a_gate_a_b, a_lambda, b_group_w, b_group_b, b_scale, ab_w_out, c_norm, c_w_pw1, c_b_pw1, c_dw_w, c_dw_b, c_ln_g, c_ln_b, c_w_pw2, c_b_pw2, xa_norm, xa_mem_norm, xa_wq, xa_wk, xa_wv, xa_wo, f_norm, f_w_up, f_dw_w, f_dw_b, f_w_down, final_norm, loss_target, m_ab_norm, m_ab_w_in, m_a_conv_w, m_a_conv_b, m_a_gate_x_w, m_a_gate_x_b, m_a_gate_a_w, m_a_gate_a_b, m_a_lambda, m_b_group_w, m_b_group_b, m_b_scale, m_ab_w_out, m_c_norm, m_c_w_pw1, m_c_b_pw1, m_c_dw_w, m_c_dw_b, m_c_ln_g, m_c_ln_b, m_c_w_pw2, m_c_b_pw2, m_xa_norm, m_xa_mem_norm, m_xa_wq, m_xa_wk, m_xa_wv, m_xa_wo, m_f_norm, m_f_w_up, m_f_dw_w, m_f_dw_b, m_f_w_down, m_final_norm, v_ab_norm, v_ab_w_in, v_a_conv_w, v_a_conv_b, v_a_gate_x_w, v_a_gate_x_b, v_a_gate_a_w, v_a_gate_a_b, v_a_lambda, v_b_group_w, v_b_group_b, v_b_scale, v_ab_w_out, v_c_norm, v_c_w_pw1, v_c_b_pw1, v_c_dw_w, v_c_dw_b, v_c_ln_g, v_c_ln_b, v_c_w_pw2, v_c_b_pw2, v_xa_norm, v_xa_mem_norm, v_xa_wq, v_xa_wk, v_xa_wv, v_xa_wo, v_f_norm, v_f_w_up, v_f_dw_w, v_f_dw_b, v_f_w_down, v_final_norm):
    args = dict(locals())
    P = {n: args[n] for n in _NAMES}
    M = {n: args["m_" + n] for n in _NAMES}
    V = {n: args["v_" + n] for n in _NAMES}

    in_flight = {}

    def launch(groups, tok):
        shards, n_of = [], {}
        for grp in groups:
            for name, l in _GROUPS[grp]:
                w = P[name][l] if tok is None else P[name][l] + tok
                shards.append(w.astype(BF16))
            if grp == ("ab", 0):
                shards.append(_pack([P[n] for n in _SMALL_SHARDED], _SS_ROWS + 4))
            n_of[grp] = len(shards)
        res = _send_start(shards, False, "gather_start_" + "_".join(g[0] + str(g[1]) for g in groups))
        lo = 0
        for grp in groups:
            in_flight[grp] = [r[lo:n_of[grp]] for r in res[:4]]
            lo = n_of[grp]
        return res[4][:1, :1]

    follow = {("ab", 0): [("ab", 1), ("xa", 0), ("f", 0), ("fd", 0)], ("ab", 1): [("c", 0), ("xa", 1)],
              ("xa", 0): [("f", 1), ("fd", 1)]}

    def fetch(grp, after):
        send_s, recv_s, srcs, lands = in_flight.pop(grp)
        srcs, lands, tok = _send_wait(send_s, recv_s, srcs, lands, after, False, f"gather_wait_{grp[0]}{grp[1]}")
        tok = launch(follow[grp], tok[:1, :1]) if grp in follow else None
        full = lands
        out = {}
        for (name, l), g in zip(_GROUPS[grp], full):
            w = _weight_layout(name, g)
            if _BIG[name][0] == 1:
                out[name] = w
            else:
                out[name] = {l: w}
        if grp == ("ab", 0):
            out.update(_small_layouts(full[-1]))
        return out, tok

    zero = launch([("ab", 0)], None)

    pending, held = [], []
    rides_with_next = {("xa", 1), ("f", 0)}

    def send(grp, G):
        held.extend(_SEND_GROUPS[grp])
        if grp in rides_with_next:
            return None
        members = tuple(held)
        del held[:]
        res = _send_start([_grad_blocks(name, l, G) for name, l in members], True, f"send_{grp[0]}{grp[1]}")
        pending.append((members, res))
        return res[4][:1, :1]

    W = {n: P[n] for n in _REPL}
    W["ab_norm"] = P["ab_norm"] + zero
    W["final_norm"] = P["final_norm"].reshape(1, D)
    W["a_gate_x_w"], W["a_gate_a_w"], W["b_group_w"] = P["a_gate_x_w"][0], P["a_gate_a_w"][0], P["b_group_w"][0]
    loss, grad_x, G = _local_step(x[0], mem[0], loss_target[0], W, fetch, send)
    loss = lax.psum(loss[0, 0], ("x", "y", "c"))

    Gs = dict(G)
    Gs["c_b_pw1"] = _unpair_blocks(G["c_b_pw1"], _CW_C)
    Gs["f_dw_w"] = jnp.stack([G["f_dw_w0"], G["f_dw_w1"]])
    Gs["a_conv_w"], Gs["c_dw_w"] = G["a_conv_w"][None], G["c_dw_w"][None]
    for n in ("xa_norm", "xa_mem_norm", "f_norm", "f_dw_b"):
        Gs[n] = jnp.concatenate([G[f"{n}0"], G[f"{n}1"]], axis=0)
    for n in ("a_gate_x_w", "a_gate_a_w", "b_group_w"):
        Gs[n] = G[n][None]
    repl_flat = jnp.concatenate([Gs[n].reshape(-1) for n in _REPL])
    repl_rows = jnp.pad(repl_flat, (0, N_DEV * _REPL_ROWS * LANE - _N_REPL)).reshape(N_DEV, _REPL_ROWS, LANE)
    ss_rows = jnp.concatenate([_to_dest_major(Gs[n], s) for n, s in _SMALL_SHARDED.items()], axis=1)
    ss_rows = ss_rows.reshape(N_DEV, _SS_ROWS, LANE)
    small_pack = jnp.concatenate(
        [repl_rows, ss_rows, jnp.zeros((N_DEV, _SMALL_ROWS - _REPL_ROWS - _SS_ROWS, LANE), F32)], axis=1)
    last = _send_start([small_pack], True, "send_small")
    pending.append(((("small", 0),), last))

    def arrived(some, after, name):
        members = [m for mem_, _ in some for m in mem_]
        cat = [[a for _, res in some for a in res[i]] for i in range(4)]
        srcs, lands, _ = _send_wait(cat[0], cat[1], cat[2], cat[3], after, True, name)
        return dict(zip(members, lands))

    out_g, out_d, out_m, out_v = {}, {}, {}, {}
    chain = [None]

    def update(name, landed):
        layers, r, c = _BIG[name]
        w2, m2, v2 = [t[name].reshape(layers * r, c) for t in (P, M, V)]
        res = None
        for l in range(layers):
            res = _sum_adamw(landed[(name, l)], w2, m2, v2, f"adamw_{name}{l}", layer=l, prev=res,
                             after=chain[0] if l == 0 else None)
        chain[0] = res[1]
        out_g[name], out_d[name], out_m[name], out_v[name] = [t.reshape(P[name].shape) for t in res]

    landed = arrived(pending[:-2], grad_x, "send_wait_early")
    for name in _BIG:
        if name != "ab_w_in":
            update(name, landed)
    landed = arrived(pending[-2:], out_v["f_w_down"], "send_wait_late")
    update("ab_w_in", landed)

    small_sum = _sum8(landed[("small", 0)], "sum_small")
    (repl_all,) = _all_gather([small_sum[:_REPL_ROWS]], "gather_small_grads")
    for out, got in zip((out_g, out_d, out_m, out_v),
                        _adamw_small(repl_all.reshape(N_DEV * _REPL_ROWS, LANE), small_sum, P, M, V)):
        out.update(got)

    return (loss, grad_x[None], *[out_g[n] for n in _NAMES], *[out_d[n] for n in _NAMES],
            *[out_m[n] for n in _NAMES], *[out_v[n] for n in _NAMES])


_NAMES = ("ab_norm", "ab_w_in", "a_conv_w", "a_conv_b", "a_gate_x_w", "a_gate_x_b", "a_gate_a_w", "a_gate_a_b",
          "a_lambda", "b_group_w", "b_group_b", "b_scale", "ab_w_out", "c_norm", "c_w_pw1", "c_b_pw1", "c_dw_w",
          "c_dw_b", "c_ln_g", "c_ln_b", "c_w_pw2", "c_b_pw2", "xa_norm", "xa_mem_norm", "xa_wq", "xa_wk", "xa_wv",
          "xa_wo", "f_norm", "f_w_up", "f_dw_w", "f_dw_b", "f_w_down", "final_norm")
```

```python
import functools

import jax
import jax.numpy as jnp
from jax import lax
from jax.experimental import pallas as pl
from jax.experimental.pallas import tpu as pltpu

F32, BF16 = jnp.float32, jnp.bfloat16
SDS = jax.ShapeDtypeStruct
MESH = pl.DeviceIdType.MESH

N_DEV = 8
D = 1024
N_MEM = 256
XA_HEADS, XA_HD = 4, 256
HD_A = 128
CONV_A, CONV_C, CONV_F = 4, 31, 3
C_RG = 8.0
POOL_WINDOWS = (2, 4, 8, 16)
D_FF = 3 * D
EPS = 1e-6
ADAM_LR, ADAM_B1, ADAM_B2, ADAM_EPS, ADAM_WD, ADAM_STEP = 0.001, 0.9, 0.999, 1e-08, 0.01, 10

LANE = 128
SUB = 8
VMEM_LIMIT = 56 * 1024 * 1024
R_SEQ = 1024
R_POOL = 2048
R_RGLRU = 2048
R_FFN = 2048
TM_ROW = 1024


def _cp(n_axes):
    return pltpu.CompilerParams(dimension_semantics=("arbitrary",) * n_axes, vmem_limit_bytes=VMEM_LIMIT)


def _tile(n, pref):
    if n <= pref:
        return n
    best = None
    for t in range(LANE, pref + 1, LANE):
        if n % t == 0:
            best = t
    assert best is not None, (n, pref)
    return best


def _perm2(n):
    return (n % 2) * 4 + n // 2


_NN = (((1,), (0,)), ((), ()))
_NT = (((1,), (1,)), ((), ()))
_TN = (((0,), (0,)), ((), ()))


def _mm_call(name, grid, ab, ab_specs, dims, acc_shape, extras, outs, finish, from_ref=False):
    nk = grid[2]
    n_ab, n_ex, n_out = len(ab), len(extras), len(outs)
    use_acc = nk > 1 or from_ref

    def product(refs):
        r = lax.dot_general(refs[0][...], refs[1][...], dims, preferred_element_type=F32)
        for i in range(1, n_ab):
            r = r + lax.dot_general(refs[2 * i][...], refs[2 * i + 1][...], dims, preferred_element_type=F32)
        return r

    def body(*refs):
        rest = refs[2 * n_ab:]
        ex_refs, o_refs = rest[:n_ex], rest[n_ex:n_ex + n_out]
        first_rows = pl.program_id(0) == 0
        if not use_acc:
            finish(product(refs), ex_refs, o_refs, first_rows)
            return
        acc = rest[n_ex + n_out]
        if nk == 1:
            acc[...] = product(refs)
            finish(acc, ex_refs, o_refs, first_rows)
            return
        k = pl.program_id(2)

        @pl.when(k == 0)
        def _():
            acc[...] = jnp.zeros_like(acc)

        acc[...] += product(refs)

        @pl.when(k == nk - 1)
        def _():
            finish(acc if from_ref else acc[...], ex_refs, o_refs, first_rows)

    res = pl.pallas_call(
        body, out_shape=[o for o, _ in outs], grid=grid,
        in_specs=list(ab_specs) + [s for _, s in extras], out_specs=[s for _, s in outs],
        scratch_shapes=[pltpu.VMEM(acc_shape, F32)] if use_acc else [], name=name, compiler_params=_cp(3),
    )(*[t for pair in ab for t in pair], *[e for e, _ in extras])
    return res[0] if n_out == 1 else res


def _finish_sum(r, ex_refs, o_refs, first_rows):
    del first_rows
    for e in ex_refs:
        r = r + e[...]
    o_refs[0][...] = r.astype(o_refs[0].dtype)


def _finish_sum_norm(r, ex_refs, o_refs, first_rows):
    del first_rows
    for e in ex_refs[:-1]:
        r = r + e[...]
    o_refs[0][...] = r
    o_refs[1][...] = ((r * lax.rsqrt(jnp.mean(r * r, axis=-1, keepdims=True) + EPS)) * ex_refs[-1][...]).astype(BF16)


_EPI_ROWS = 16


def _finish_rms_bwd(r_ref, ex_refs, o_refs, first_rows):
    x_ref, g_ref, dres_ref = ex_refs
    dx_ref, dxb_ref, dg_ref = o_refs

    @pl.when(first_rows)
    def _():
        dg_ref[...] = jnp.zeros_like(dg_ref)

    gv = g_ref[...]
    inv_d = 1.0 / r_ref.shape[1]

    def step(i, dg_acc):
        groups = [pl.ds(pl.multiple_of(i * (2 * _EPI_ROWS) + u * _EPI_ROWS, _EPI_ROWS), _EPI_ROWS) for u in range(2)]
        sums = []
        for rows in groups:
            r, xf = r_ref[rows, :], x_ref[rows, :]
            sums.append((jnp.sum(xf * xf, axis=-1, keepdims=True), jnp.sum((r * gv) * xf, axis=-1, keepdims=True)))
        for rows, (sxx, sax) in zip(groups, sums):
            r, xf = r_ref[rows, :], x_ref[rows, :]
            rs = lax.rsqrt(sxx * inv_d + EPS)
            dg_acc = dg_acc + _psum8(r * (xf * rs))
            dx = rs * (r * gv) - xf * (rs * rs * (sax * rs * inv_d)) + dres_ref[rows, :]
            dx_ref[rows, :] = dx
            dxb_ref[rows, :] = dx.astype(BF16)
        return dg_acc

    dg_acc = lax.fori_loop(0, r_ref.shape[0] // (2 * _EPI_ROWS), step, jnp.zeros((SUB, r_ref.shape[1]), F32))
    dg_ref[...] += jnp.sum(dg_acc, axis=0, keepdims=True)


def _rms_bwd_io(M, tm, x, g, dres):
    rows = pl.BlockSpec((tm, D), lambda m, n, k: (m, 0))
    vec = pl.BlockSpec((1, D), lambda m, n, k: (0, 0))
    return ([(x, rows), (g, vec), (dres, rows)],
            [(SDS((M, D), F32), rows), (SDS((M, D), BF16), rows), (SDS((1, D), F32), vec)])


_K_WHOLE = 3072


def _mm_nn(a, b, *, out_dtype, name, bias=None, add=None, norm=None):
    M, K = a.shape
    tk = K if K <= _K_WHOLE else _tile(K, 1024)
    if K <= 1024 and norm is None:
        tm = _tile(M, 2048 if add is None and out_dtype == BF16 else 1024)
    else:
        tm = _tile(M, 512)
    if b.ndim == 3:
        nb, _, bw = b.shape
        N, tn, nn = nb * bw, bw, nb
        b_spec = pl.BlockSpec((None, tk, bw), lambda m, n, k: (_perm2(n), k, 0))
    else:
        N = b.shape[1]
        tn = _tile(N, 1024)
        nn = N // tn
        b_spec = pl.BlockSpec((tk, tn), lambda m, n, k: (k, n))
    tile = pl.BlockSpec((tm, tn), lambda m, n, k: (m, n))
    vec = pl.BlockSpec((1, tn), lambda m, n, k: (0, n))
    extras = ([] if bias is None else [(bias, vec)]) + ([] if add is None else [(add, tile)])
    outs, finish = [(SDS((M, N), out_dtype), tile)], _finish_sum
    if norm is not None:
        assert tn == N == D and out_dtype == F32
        extras.append((norm, vec))
        outs, finish = outs + [(SDS((M, N), BF16), tile)], _finish_sum_norm
    return _mm_call(name, (M // tm, nn, K // tk), [(a, b)], [pl.BlockSpec((tm, tk), lambda m, n, k: (m, k)), b_spec],
                    _NN, (tm, tn), extras, outs, finish)


def _mm_nt(a, b, *, out_dtype, name, add=None, rms=None):
    M, N = a.shape
    if b.ndim == 3:
        nb, Ko, bw = b.shape
        tm = _tile(M, 1024)
        tn, tk, nk = _tile(Ko, 1024), bw, nb
        b_spec = pl.BlockSpec((None, tn, bw), lambda m, n, k: (_perm2(k), n, 0))
    else:
        Ko = b.shape[0]
        tk = N if N <= _K_WHOLE else _tile(N, 1024)
        if N <= 1024 and rms is None:
            tm = _tile(M, 2048 if add is None and out_dtype == BF16 else 1024)
        else:
            tm = _tile(M, 512)
        tn = _tile(Ko, 1024)
        nk = N // tk
        b_spec = pl.BlockSpec((tn, tk), lambda m, n, k: (n, k))
    tile = pl.BlockSpec((tm, tn), lambda m, n, k: (m, n))
    extras = [] if add is None else [(add, tile)]
    outs, finish = [(SDS((M, Ko), out_dtype), tile)], _finish_sum
    if rms is not None:
        assert tn == Ko == D and add is None
        (extras, outs), finish = _rms_bwd_io(M, tm, *rms), _finish_rms_bwd
    return _mm_call(name, (M // tm, Ko // tn, nk), [(a, b)], [pl.BlockSpec((tm, tk), lambda m, n, k: (m, k)), b_spec],
                    _NT, (tm, tn), extras, outs, finish, from_ref=rms is not None)


def _mm_nt_cols(parts, b, *, name, rms):
    M = parts[0].shape[0]
    tm = _tile(M, 512)
    specs, off = [], 0
    for p in parts:
        w = p.shape[1]
        assert off % w == 0
        specs.append(pl.BlockSpec((tm, w), lambda m, n, k: (m, 0)))
        specs.append(pl.BlockSpec((D, w), functools.partial(lambda m, n, k, o: (0, o), o=off // w)))
        off += w
    extras, outs = _rms_bwd_io(M, tm, *rms)
    return _mm_call(name, (M // tm, 1, 1), [(p, b) for p in parts], specs, _NT, (tm, D), extras, outs, _finish_rms_bwd,
                    from_ref=True)


def _mm_tn(a, b, *, out_dtype, name, blocks=None):
    S, Ka = a.shape
    Nb = b.shape[1]
    tm = _tile(Ka, 1024)
    if blocks is not None:
        bw = blocks
        tn, nn = bw, Nb // bw
        out = (SDS((nn, Ka, bw), out_dtype), pl.BlockSpec((None, tm, bw), lambda m, n, k: (_perm2(n), m, 0)))
    else:
        tn = _tile(Nb, 1024)
        nn = Nb // tn
        out = (SDS((Ka, Nb), out_dtype), pl.BlockSpec((tm, tn), lambda m, n, k: (m, n)))
    steps = (Ka // tm) * nn
    tk = _tile(S, 4096 if steps >= 4 else 2048 if steps >= 2 else 1024)
    return _mm_call(name, (Ka // tm, nn, S // tk), [(a, b)],
                    [pl.BlockSpec((tk, tm), lambda m, n, k: (k, m)), pl.BlockSpec((tk, tn), lambda m, n, k: (k, n))],
                    _TN, (tm, tn), [], [out], _finish_sum)


def _row(tm, c):
    return pl.BlockSpec((tm, c), lambda i: (i, 0))


def _full(shape):
    nd = len(shape)
    return pl.BlockSpec(shape, lambda i: (0,) * nd)


def _rms_fwd(x, g, name):
    S = x.shape[0]
    tm = min(S, TM_ROW)

    def body(x_ref, g_ref, o_ref):
        xf = x_ref[...]
        r = lax.rsqrt(jnp.mean(xf * xf, axis=-1, keepdims=True) + EPS)
        o_ref[...] = ((xf * r) * g_ref[...]).astype(BF16)

    return pl.pallas_call(body, out_shape=SDS((S, D), BF16), grid=(S // tm,), in_specs=[_row(tm, D), _full((1, D))],
                          out_specs=_row(tm, D), name=name, compiler_params=_cp(1))(x, g)


def _rms_bwd(x, g, dn, dres, name):
    S = x.shape[0]
    tm = min(S, TM_ROW)
    want_dx = dres is not None

    def body(x_ref, g_ref, dn_ref, *rest):
        i = pl.program_id(0)
        dg_ref = rest[-1]

        @pl.when(i == 0)
        def _():
            dg_ref[...] = jnp.zeros_like(dg_ref)

        xf = x_ref[...]
        r = lax.rsqrt(jnp.mean(xf * xf, axis=-1, keepdims=True) + EPS)
        y = xf * r
        dn_v = dn_ref[...]
        dg_ref[...] += jnp.sum(dn_v * y, axis=0, keepdims=True)
        if want_dx:
            dres_ref, dx_ref, dxb_ref = rest[0], rest[1], rest[2]
            dy = dn_v * g_ref[...]
            dx = r * (dy - y * jnp.mean(dy * y, axis=-1, keepdims=True)) + dres_ref[...]
            dx_ref[...] = dx
            dxb_ref[...] = dx.astype(BF16)

    ins = [x, g, dn] + ([dres] if want_dx else [])
    in_specs = [_row(tm, D), _full((1, D)), _row(tm, D)] + ([_row(tm, D)] if want_dx else [])
    outs = ([SDS((S, D), F32), SDS((S, D), BF16)] if want_dx else []) + [SDS((1, D), F32)]
    out_specs = ([_row(tm, D), _row(tm, D)] if want_dx else []) + [_full((1, D))]
    return pl.pallas_call(body, out_shape=outs, grid=(S // tm,), in_specs=in_specs, out_specs=out_specs, name=name,
                          compiler_params=_cp(1))(*ins)


def _loss_head(x, g, tgt):
    S = x.shape[0]
    tm = min(S, TM_ROW)

    def body(x_ref, g_ref, t_ref, loss_ref, dx_ref, dxb_ref, dg_ref):
        i = pl.program_id(0)

        @pl.when(i == 0)
        def _():
            loss_ref[...] = jnp.zeros_like(loss_ref)
            dg_ref[...] = jnp.zeros_like(dg_ref)

        xf = x_ref[...]
        r = lax.rsqrt(jnp.mean(xf * xf, axis=-1, keepdims=True) + EPS)
        y = xf * r
        gv = g_ref[...]
        err = y * gv - t_ref[...]
        per_row = jnp.mean(err * err, axis=-1, keepdims=True)
        loss_ref[...] += 0.5 * jnp.sum(per_row, axis=0, keepdims=True)
        dn_v = err * (1.0 / D)
        dg_ref[...] += jnp.sum(dn_v * y, axis=0, keepdims=True)
        dy = dn_v * gv
        dx = r * (dy - y * jnp.mean(dy * y, axis=-1, keepdims=True))
        dx_ref[...] = dx
        dxb_ref[...] = dx.astype(BF16)

    return pl.pallas_call(
        body, out_shape=[SDS((1, 1), F32), SDS((S, D), F32), SDS((S, D), BF16), SDS((1, D), F32)], grid=(S // tm,),
        in_specs=[_row(tm, D), _full((1, D)), _row(tm, D)],
        out_specs=[_full((1, 1)), _row(tm, D), _row(tm, D), _full((1, D))], name="loss_head", compiler_params=_cp(1),
    )(x, g, tgt)


def _softmax_rows(s):
    m = jnp.max(s, axis=-1, keepdims=True)
    e = jnp.exp(s - m)
    return e / jnp.sum(e, axis=-1, keepdims=True)


def _attn_fwd(q, k, v, name):
    S = q.shape[0]
    tm = min(S, TM_ROW)
    scale = XA_HD ** -0.5

    def body(q_ref, k_ref, v_ref, o_ref):
        for h in range(XA_HEADS):
            sl = slice(h * XA_HD, (h + 1) * XA_HD)
            s = lax.dot_general(q_ref[:, sl], k_ref[:, sl], _NT, preferred_element_type=F32) * scale
            p = _softmax_rows(s)
            o_ref[:, sl] = lax.dot_general(p.astype(BF16), v_ref[:, sl], _NN, preferred_element_type=F32).astype(BF16)

    return pl.pallas_call(body, out_shape=SDS((S, D), BF16), grid=(S // tm,),
                          in_specs=[_row(tm, D), _full((N_MEM, D)), _full((N_MEM, D))], out_specs=_row(tm, D),
                          name=name, compiler_params=_cp(1))(q, k, v)


def _attn_bwd(q, k, v, do, name):
    S = q.shape[0]
    tm = min(S, TM_ROW)
    scale = XA_HD ** -0.5

    def body(q_ref, k_ref, v_ref, do_ref, dq_ref, dk_ref, dv_ref):
        i = pl.program_id(0)

        @pl.when(i == 0)
        def _():
            dk_ref[...] = jnp.zeros_like(dk_ref)
            dv_ref[...] = jnp.zeros_like(dv_ref)

        for h in range(XA_HEADS):
            sl = slice(h * XA_HD, (h + 1) * XA_HD)
            qh, kh, vh, doh = q_ref[:, sl], k_ref[:, sl], v_ref[:, sl], do_ref[:, sl]
            s = lax.dot_general(qh, kh, _NT, preferred_element_type=F32) * scale
            p = _softmax_rows(s)
            pb = p.astype(BF16)
            dv_ref[:, sl] += lax.dot_general(pb, doh, _TN, preferred_element_type=F32)
            dp = lax.dot_general(doh, vh, _NT, preferred_element_type=F32)
            ds = (p * (dp - jnp.sum(dp * p, axis=-1, keepdims=True)) * scale).astype(BF16)
            dq_ref[:, sl] = lax.dot_general(ds, kh, _NN, preferred_element_type=F32).astype(BF16)
            dk_ref[:, sl] += lax.dot_general(ds, qh, _TN, preferred_element_type=F32)

    return pl.pallas_call(
        body, out_shape=[SDS((S, D), BF16), SDS((N_MEM, D), F32), SDS((N_MEM, D), F32)], grid=(S // tm,),
        in_specs=[_row(tm, D), _full((N_MEM, D)), _full((N_MEM, D)), _row(tm, D)],
        out_specs=[_row(tm, D), _full((N_MEM, D)), _full((N_MEM, D))], name=name, compiler_params=_cp(1),
    )(q, k, v, do)


def _sigmoid(x):
    return 1.0 / (1.0 + jnp.exp(-x))


def _ln_silu_fwd(cv, g, b):
    S = cv.shape[0]
    tm = min(S, TM_ROW)

    def body(x_ref, g_ref, b_ref, o_ref):
        xf = x_ref[...]
        mu = jnp.mean(xf, axis=-1, keepdims=True)
        xc = xf - mu
        rstd = lax.rsqrt(jnp.mean(xc * xc, axis=-1, keepdims=True) + EPS)
        ln = (xc * rstd) * g_ref[...] + b_ref[...]
        o_ref[...] = (ln * _sigmoid(ln)).astype(BF16)

    return pl.pallas_call(body, out_shape=SDS((S, D), BF16), grid=(S // tm,),
                          in_specs=[_row(tm, D), _full((1, D)), _full((1, D))], out_specs=_row(tm, D),
                          name="ln_silu_fwd", compiler_params=_cp(1))(cv, g, b)


def _ln_silu_bwd(ds, cv, g, b, dx):
    S = cv.shape[0]
    tm = min(S, TM_ROW)

    def body(ds_ref, x_ref, g_ref, b_ref, dx_ref, dcv_ref, dg_ref, db_ref, db2_ref):
        i = pl.program_id(0)

        @pl.when(i == 0)
        def _():
            dg_ref[...] = jnp.zeros_like(dg_ref)
            db_ref[...] = jnp.zeros_like(db_ref)
            db2_ref[...] = jnp.zeros_like(db2_ref)

        xf = x_ref[...]
        mu = jnp.mean(xf, axis=-1, keepdims=True)
        xc = xf - mu
        rstd = lax.rsqrt(jnp.mean(xc * xc, axis=-1, keepdims=True) + EPS)
        xhat = xc * rstd
        gv = g_ref[...]
        ln = xhat * gv + b_ref[...]
        sg = _sigmoid(ln)
        dln = ds_ref[...].astype(F32) * (sg + ln * sg * (1.0 - sg))
        dg_ref[...] += jnp.sum(dln * xhat, axis=0, keepdims=True)
        db_ref[...] += jnp.sum(dln, axis=0, keepdims=True)
        db2_ref[...] += jnp.sum(dx_ref[...], axis=0, keepdims=True)
        dxh = dln * gv
        dcv_ref[...] = rstd * (dxh - jnp.mean(dxh, axis=-1, keepdims=True)
                               - xhat * jnp.mean(dxh * xhat, axis=-1, keepdims=True))

    return pl.pallas_call(
        body, out_shape=[SDS((S, D), F32), SDS((1, D), F32), SDS((1, D), F32), SDS((1, D), F32)], grid=(S // tm,),
        in_specs=[_row(tm, D), _row(tm, D), _full((1, D)), _full((1, D)), _row(tm, D)],
        out_specs=[_row(tm, D), _full((1, D)), _full((1, D)), _full((1, D))], name="ln_silu_bwd",
        compiler_params=_cp(1),
    )(ds, cv, g, b, dx)


_GELU_C, _GELU_K = 0.7978845608028654, 0.044715


def _gelu(x, with_grad=False):
    x2 = x * x
    t = jnp.tanh(_GELU_C * (x + _GELU_K * x * x2))
    gel = 0.5 * x * (1.0 + t)
    if not with_grad:
        return gel
    return gel, 0.5 * (1.0 + t) + 0.5 * x * (1.0 - t * t) * (_GELU_C * (1.0 + 3.0 * _GELU_K * x2))


def _expm1(x):
    poly = x * (1.0 + x * (0.5 + x * (1.0 / 6.0 + x * (1.0 / 24.0 + x * (1.0 / 120.0)))))
    return jnp.where(jnp.abs(x) < 0.05, poly, jnp.exp(x) - 1.0)


def _softplus(x):
    return jnp.maximum(x, 0.0) + jnp.log1p(jnp.exp(-jnp.abs(x)))


_SCAN_UNROLL = 8
_RB = 32
_HB = 16


def _sub_blocks(n_rows, n_lanes, fn):
    def step(idx, c):
        r0 = pl.multiple_of(idx * _RB, _RB)
        for lt in range(n_lanes // LANE):
            fn(r0, lt)
        return c

    lax.fori_loop(0, n_rows // _RB, step, 0)


def _lanes(lt):
    return pl.ds(lt * LANE, LANE)


def _psum8(x):
    parts = [x[i * SUB:(i + 1) * SUB] for i in range(x.shape[0] // SUB)]
    return functools.reduce(lambda p, q: p + q, parts)


def _scan_fwd(a_s, b_s, out_ref, carry_ref, n_groups):
    row = lax.broadcasted_iota(jnp.int32, (SUB, LANE), 0)
    U = _SCAN_UNROLL

    def step(gi, carry):
        base = gi * (SUB * U)
        parts = []
        for u in range(U):
            i = pl.multiple_of(base + u * SUB, SUB)
            a8, b8 = a_s[pl.ds(i, SUB), :], b_s[pl.ds(i, SUB), :]
            for s in (1, 2, 4):
                a_sh = jnp.where(row >= s, pltpu.roll(a8, s, 0), 1.0)
                b_sh = jnp.where(row >= s, pltpu.roll(b8, s, 0), 0.0)
                b8 = a8 * b_sh + b8
                a8 = a8 * a_sh
            parts.append((i, a8, b8))
        for i, a8, b8 in parts:
            h8 = a8 * carry + b8
            out_ref[pl.ds(i, SUB), :] = h8
            carry = jnp.broadcast_to(h8[SUB - 1:SUB, :], (SUB, LANE))
        return carry

    carry_ref[...] = lax.fori_loop(0, n_groups // U, step, carry_ref[...])


def _scan_bwd(a_s, b_s, out_ref, carry_ref, n_groups):
    row = lax.broadcasted_iota(jnp.int32, (SUB, LANE), 0)
    U = _SCAN_UNROLL

    def step(gi, carry):
        base = (n_groups // U - 1 - gi) * (SUB * U)
        parts = []
        for u in reversed(range(U)):
            i = pl.multiple_of(base + u * SUB, SUB)
            a8, b8 = a_s[pl.ds(i, SUB), :], b_s[pl.ds(i, SUB), :]
            for s in (1, 2, 4):
                a_sh = jnp.where(row < SUB - s, pltpu.roll(a8, SUB - s, 0), 1.0)
                b_sh = jnp.where(row < SUB - s, pltpu.roll(b8, SUB - s, 0), 0.0)
                b8 = a8 * b_sh + b8
                a8 = a8 * a_sh
            parts.append((i, a8, b8))
        for i, a8, b8 in parts:
            h8 = a8 * carry + b8
            out_ref[pl.ds(i, SUB), :] = h8
            carry = jnp.broadcast_to(h8[0:1, :], (SUB, LANE))
        return carry

    carry_ref[...] = lax.fori_loop(0, n_groups // U, step, carry_ref[...])


def _rglru_pre(xr, wgx_ref, bgx_ref, wga_ref, bga_ref, lam_ref):
    xrb = xr.astype(BF16)
    wgx, wga = wgx_ref[0].astype(BF16), wga_ref[0].astype(BF16)
    gx = _sigmoid(lax.dot_general(xrb, wgx, _NN, preferred_element_type=F32) + bgx_ref[...])
    ga = _sigmoid(lax.dot_general(xrb, wga, _NN, preferred_element_type=F32) + bga_ref[...])
    sp = _softplus(-lam_ref[...])
    log_a = -C_RG * ga * sp
    a = jnp.exp(log_a)
    mult = jnp.sqrt(-_expm1(2.0 * log_a))
    return gx, ga, sp, a, mult, xrb, wgx, wga


def _a_specs():
    vec = pl.BlockSpec((1, HD_A), lambda c, j: (0, c))
    mat = pl.BlockSpec((1, HD_A, HD_A), lambda c, j: (c, 0, 0))
    return [pl.BlockSpec((CONV_A, HD_A), lambda c, j: (0, c)), vec, mat, vec, mat, vec, vec]


def _a_fwd(zp, conv_w, conv_b, wgx, bgx, wga, bga, lam):
    S = zp.shape[0]
    R, nt = R_RGLRU, D // HD_A
    H = SUB

    def body(zg_ref, zr_ref, cw_ref, cb_ref, wgx_ref, bgx_ref, wga_ref, bga_ref, lam_ref, ya_ref, h_ref,
             ext, a_s, b_s, hc):
        j = pl.program_id(1)

        @pl.when(j == 0)
        def _():
            ext[0:H, :] = jnp.zeros((H, HD_A), F32)
            hc[...] = jnp.zeros_like(hc)

        ext[H:H + R, :] = zr_ref[...].astype(F32)
        xr = cb_ref[...]
        for k in range(CONV_A):
            xr = xr + cw_ref[k:k + 1, :] * ext[pl.ds(H - (CONV_A - 1 - k), R), :]
        gx, _, _, a, mult, _, _, _ = _rglru_pre(xr, wgx_ref, bgx_ref, wga_ref, bga_ref, lam_ref)
        a_s[...] = a
        b_s[...] = mult * (gx * xr)
        _scan_fwd(a_s, b_s, h_ref, hc, R // SUB)
        ya_ref[...] = (_gelu(zg_ref[...].astype(F32)) * h_ref[...]).astype(BF16)
        ext[0:H, :] = ext[R:R + H, :]

    return pl.pallas_call(
        body, out_shape=[SDS((S, D + D // 2), BF16), SDS((S, D), F32)], grid=(nt, S // R),
        in_specs=[pl.BlockSpec((R, HD_A), lambda c, j: (j, c)), pl.BlockSpec((R, HD_A), lambda c, j: (j, nt + c))]
        + _a_specs(),
        out_specs=[pl.BlockSpec((R, HD_A), lambda c, j: (j, c)), pl.BlockSpec((R, HD_A), lambda c, j: (j, c))],
        scratch_shapes=[pltpu.VMEM((H + R, HD_A), F32), pltpu.VMEM((R, HD_A), F32), pltpu.VMEM((R, HD_A), F32),
                        pltpu.VMEM((SUB, HD_A), F32)],
        name="rglru_fwd", compiler_params=_cp(2),
    )(zp, zp, conv_w, conv_b, wgx, bgx, wga, bga, lam)


def _a_bwd(dyab, zp, h, conv_w, conv_b, wgx, bgx, wga, bga, lam):
    S = zp.shape[0]
    R, nt, nch = R_RGLRU, D // HD_A, S // R_RGLRU
    H = SUB

    def rows(c, j):
        return (nch - 1 - j, c)

    def rows_rec(c, j):
        return (nch - 1 - j, nt + c)

    def halo(c, j):
        return (jnp.maximum((nch - 1 - j) * (R // H) - 1, 0), c)

    def halo_z(c, j):
        return (jnp.maximum((nch - 1 - j) * (R // _HB) - 1, 0), nt + c)

    def body(dy_ref, zg_ref, zr_ref, zh_ref, h_ref, hh_ref, cw_ref, cb_ref, wgx_ref, bgx_ref, wga_ref, bga_ref,
             lam_ref, dzg_ref, dzr_ref, dcw_ref, dcb_ref, dwgx_ref, dbgx_ref, dwga_ref, dbga_ref, dlam_ref,
             ext_z, ext_h, ext_mu, ext_d, a_s, b_s, muc):
        j = pl.program_id(1)
        first_chunk = (nch - 1 - j) == 0

        @pl.when(j == 0)
        def _():
            ext_mu[R:R + H, :] = jnp.zeros((H, HD_A), F32)
            ext_d[R:R + H, :] = jnp.zeros((H, HD_A), F32)
            muc[...] = jnp.zeros_like(muc)
            for r in (dcw_ref, dcb_ref, dwgx_ref, dbgx_ref, dwga_ref, dbga_ref, dlam_ref):
                r[...] = jnp.zeros_like(r)

        zg = zg_ref[...].astype(F32)
        ext_z[0:H, :] = jnp.where(first_chunk, 0.0, zh_ref[_HB - H:_HB, :].astype(F32))
        ext_z[H:H + R, :] = zr_ref[...].astype(F32)
        ext_h[0:H, :] = jnp.where(first_chunk, 0.0, hh_ref[...])
        ext_h[H:H + R, :] = h_ref[...]
        xr = cb_ref[...]
        for k in range(CONV_A):
            xr = xr + cw_ref[k:k + 1, :] * ext_z[pl.ds(H - (CONV_A - 1 - k), R), :]
        gx, ga, sp, a, mult, xrb, wgxb, wgab = _rglru_pre(xr, wgx_ref, bgx_ref, wga_ref, bga_ref, lam_ref)
        gel, dgel = _gelu(zg, with_grad=True)
        dy = dy_ref[...].astype(F32)
        dh = dy * gel
        dzg_ref[...] = (dy * h_ref[...] * dgel).astype(BF16)
        a_s[...] = a
        b_s[...] = a * dh
        _scan_bwd(a_s, b_s, ext_mu, muc, R // SUB)
        lam_t = dh + ext_mu[pl.ds(1, R), :]
        ext_mu[R:R + H, :] = ext_mu[0:H, :]
        da = lam_t * ext_h[pl.ds(H - 1, R), :]
        gxr = gx * xr
        dlog_a = da * a - (lam_t * gxr) * (a * a) / mult
        dgx = lam_t * mult * xr
        dxr = lam_t * mult * gx
        lam_v = lam_ref[...]
        dlam_ref[...] += jnp.sum(dlog_a * ga, axis=0, keepdims=True) * (C_RG * _sigmoid(-lam_v))
        dpa = (dlog_a * (-C_RG * sp)) * ga * (1.0 - ga)
        dpx = dgx * gx * (1.0 - gx)
        dbga_ref[...] += jnp.sum(dpa, axis=0, keepdims=True)
        dbgx_ref[...] += jnp.sum(dpx, axis=0, keepdims=True)
        dpab, dpxb = dpa.astype(BF16), dpx.astype(BF16)
        dwga_ref[0] += lax.dot_general(xrb, dpab, _TN, preferred_element_type=F32)
        dwgx_ref[0] += lax.dot_general(xrb, dpxb, _TN, preferred_element_type=F32)
        dxr = (dxr + lax.dot_general(dpab, wgab, _NT, preferred_element_type=F32)
               + lax.dot_general(dpxb, wgxb, _NT, preferred_element_type=F32))
        dcb_ref[...] += jnp.sum(dxr, axis=0, keepdims=True)
        ext_d[0:R, :] = dxr
        dzr = jnp.zeros((R, HD_A), F32)
        for k in range(CONV_A):
            sh = CONV_A - 1 - k
            dcw_ref[k:k + 1, :] += jnp.sum(dxr * ext_z[pl.ds(H - sh, R), :], axis=0, keepdims=True)
            dzr = dzr + cw_ref[k:k + 1, :] * ext_d[pl.ds(sh, R), :]
        dzr_ref[...] = dzr.astype(BF16)
        ext_d[R:R + H, :] = ext_d[0:H, :]

    vec_o = pl.BlockSpec((1, HD_A), lambda c, j: (0, c))
    mat_o = pl.BlockSpec((1, HD_A, HD_A), lambda c, j: (c, 0, 0))
    return pl.pallas_call(
        body,
        out_shape=[SDS((S, D), BF16), SDS((S, D), BF16), SDS((CONV_A, D), F32), SDS((1, D), F32),
                   SDS((nt, HD_A, HD_A), F32), SDS((1, D), F32), SDS((nt, HD_A, HD_A), F32), SDS((1, D), F32),
                   SDS((1, D), F32)],
        grid=(nt, nch),
        in_specs=[pl.BlockSpec((R, HD_A), rows), pl.BlockSpec((R, HD_A), rows), pl.BlockSpec((R, HD_A), rows_rec),
                  pl.BlockSpec((_HB, HD_A), halo_z), pl.BlockSpec((R, HD_A), rows),
                  pl.BlockSpec((H, HD_A), halo)] + _a_specs(),
        out_specs=[pl.BlockSpec((R, HD_A), rows), pl.BlockSpec((R, HD_A), rows),
                   pl.BlockSpec((CONV_A, HD_A), lambda c, j: (0, c)), vec_o, mat_o, vec_o, mat_o, vec_o, vec_o],
        scratch_shapes=[pltpu.VMEM((H + R, HD_A), F32), pltpu.VMEM((H + R, HD_A), F32), pltpu.VMEM((R + H, HD_A), F32),
                        pltpu.VMEM((R + H, HD_A), F32), pltpu.VMEM((R, HD_A), F32), pltpu.VMEM((R, HD_A), F32),
                        pltpu.VMEM((SUB, HD_A), F32)],
        name="rglru_bwd", compiler_params=_cp(2),
    )(dyab, zp, zp, zp, h, h, conv_w, conv_b, wgx, bgx, wga, bga, lam)


_POOL_H = 16
_POOL_T0 = 2 * D // HD_A
_POOL_Y0 = D // HD_A


def _window_sum(lv, n, lo, rows, g, ahead):
    base = 0 if ahead else SUB
    cur, win = lv[0], None
    for i, s in enumerate((1, 2, 4, 8)):
        val = cur[pl.ds(base, n), :] + cur[pl.ds(base + (s if ahead else -s), n), :]
        sel = val[lo:lo + rows]
        win = sel if win is None else jnp.where(g >= i, sel, win)
        if i < 3:
            lv[i + 1][pl.ds(base, n), :] = val
            cur = lv[i + 1]
    return win


def _pool_width(g):
    return jnp.where(g == 0, 2.0, jnp.where(g == 1, 4.0, jnp.where(g == 2, 8.0, 16.0)))


def _b_fwd(zp, yab, wg, bg, sc):
    S = zp.shape[0]
    R, H = min(S, R_POOL), _POOL_H

    def body(z_ref, wg_ref, bg_ref, sc_ref, yab_in, yb_ref, *lv):
        del yab_in
        g, j = pl.program_id(0), pl.program_id(1)

        @pl.when(j == 0)
        def _():
            for r in lv:
                r[0:SUB, :] = jnp.zeros((SUB, HD_A), F32)
            lv[0][SUB:SUB + H, :] = jnp.zeros((H, HD_A), F32)

        u = z_ref[...].astype(F32)
        lv[0][SUB + H:SUB + H + R, :] = u
        t1 = (j * R + 1 + lax.broadcasted_iota(jnp.int32, (R, HD_A), 0)).astype(F32)
        p = _window_sum(lv, H + R, H, R, g, False) / jnp.minimum(t1, _pool_width(g)) - u
        lin = lax.dot_general(p.astype(BF16), wg_ref[0].astype(BF16), _NN, preferred_element_type=F32) + bg_ref[...]
        yb_ref[...] = (lin * sc_ref[...]).astype(BF16)
        lv[0][SUB:SUB + H, :] = lv[0][SUB + R:SUB + R + H, :]

    vec = pl.BlockSpec((1, HD_A), lambda g, j: (0, g))
    return pl.pallas_call(
        body, out_shape=SDS(yab.shape, yab.dtype), grid=(len(POOL_WINDOWS), S // R),
        in_specs=[pl.BlockSpec((R, HD_A), lambda g, j: (j, _POOL_T0 + g)),
                  pl.BlockSpec((1, HD_A, HD_A), lambda g, j: (g, 0, 0)), vec, vec, pl.BlockSpec(memory_space=pl.ANY)],
        out_specs=pl.BlockSpec((R, HD_A), lambda g, j: (j, _POOL_Y0 + g)),
        scratch_shapes=[pltpu.VMEM((SUB + H + R, HD_A), F32)] * 4, input_output_aliases={4: 0},
        name="pool_fwd", compiler_params=_cp(2),
    )(zp, wg, bg, sc, yab)


def _b_bwd(dyab, zp, wg, bg, sc):
    S = zp.shape[0]
    R, H, ng = min(S, R_POOL), _POOL_H, len(POOL_WINDOWS)
    nch = S // R

    def body(dy_ref, z_ref, zh_ref, wg_ref, bg_ref, sc_ref, dz_ref, dwg_ref, dbg_ref, dsc_ref, *scratch):
        lu, lq = scratch[:4], scratch[4:]
        g, j = pl.program_id(0), pl.program_id(1)
        jj = nch - 1 - j

        @pl.when(j == 0)
        def _():
            for r in lu:
                r[0:SUB, :] = jnp.zeros((SUB, HD_A), F32)
            for r in lq:
                r[R + H:R + H + SUB, :] = jnp.zeros((SUB, HD_A), F32)
            lq[0][R:R + H, :] = jnp.zeros((H, HD_A), F32)
            for r in (dwg_ref, dbg_ref, dsc_ref):
                r[...] = jnp.zeros_like(r)

        u = z_ref[...].astype(F32)
        lu[0][SUB:SUB + H, :] = jnp.where(jj == 0, 0.0, zh_ref[...].astype(F32))
        lu[0][SUB + H:SUB + H + R, :] = u
        t1 = (jj * R + 1 + lax.broadcasted_iota(jnp.int32, (R, HD_A), 0)).astype(F32)
        cnt = jnp.minimum(t1, _pool_width(g))
        pb = (_window_sum(lu, H + R, H, R, g, False) / cnt - u).astype(BF16)
        wgb = wg_ref[0].astype(BF16)
        lin = lax.dot_general(pb, wgb, _NN, preferred_element_type=F32) + bg_ref[...]
        dy = dy_ref[...].astype(F32)
        dsc_ref[...] += jnp.sum(dy * lin, axis=0, keepdims=True)
        dlin = dy * sc_ref[...]
        dbg_ref[...] += jnp.sum(dlin, axis=0, keepdims=True)
        dlb = dlin.astype(BF16)
        dwg_ref[0] += lax.dot_general(pb, dlb, _TN, preferred_element_type=F32)
        dp = lax.dot_general(dlb, wgb, _NT, preferred_element_type=F32)
        lq[0][0:R, :] = dp / cnt
        dz_ref[...] = (_window_sum(lq, R + H, 0, R, g, True) - dp).astype(BF16)
        lq[0][R:R + H, :] = lq[0][0:H, :]

    vec = pl.BlockSpec((1, HD_A), lambda g, j: (0, g))
    mat = pl.BlockSpec((1, HD_A, HD_A), lambda g, j: (g, 0, 0))
    return pl.pallas_call(
        body, out_shape=[SDS((S, D // 2), BF16), SDS((ng, HD_A, HD_A), F32), SDS((1, D // 2), F32),
                         SDS((1, D // 2), F32)],
        grid=(ng, nch),
        in_specs=[pl.BlockSpec((R, HD_A), lambda g, j: (nch - 1 - j, _POOL_Y0 + g)),
                  pl.BlockSpec((R, HD_A), lambda g, j: (nch - 1 - j, _POOL_T0 + g)),
                  pl.BlockSpec((H, HD_A), lambda g, j: (jnp.maximum((nch - 1 - j) * (R // H) - 1, 0), _POOL_T0 + g)),
                  mat, vec, vec],
        out_specs=[pl.BlockSpec((R, HD_A), lambda g, j: (nch - 1 - j, g)), mat, vec, vec],
        scratch_shapes=[pltpu.VMEM((SUB + H + R, HD_A), F32)] * 8,
        name="pool_bwd", compiler_params=_cp(2),
    )(dyab, zp, zp, wg, bg, sc)


_CW_F = 768


def _f_fwd(hp, w, b, name):
    S = hp.shape[0]
    R, H, cw = min(S, R_FFN), SUB, _CW_F
    nlt = cw // LANE

    def body(h_ref, w_ref, b_ref, o_ref, gel_ref, ud_ref, ext):
        j = pl.program_id(1)

        @pl.when(j == 0)
        def _():
            ext[:, 0:H, :] = jnp.zeros((nlt, H, LANE), F32)

        def stage(r0, lt):
            ext[lt, pl.ds(pl.multiple_of(r0 + H, SUB), _RB), :] = h_ref[pl.ds(r0, _RB), _lanes(lt)].astype(F32)

        def main(r0, lt):
            ls = _lanes(lt)
            gp = b_ref[:, ls]
            for k in range(CONV_F):
                gp = gp + w_ref[k:k + 1, ls] * ext[lt, pl.ds(r0 + (H - (CONV_F - 1 - k)), _RB), :]
            up = h_ref[pl.ds(r0, _RB), _lanes(lt + nlt)].astype(F32)
            gel, dgel = _gelu(gp, with_grad=True)
            rs = pl.ds(r0, _RB)
            o_ref[rs, ls] = (gel * up).astype(BF16)
            gel_ref[rs, ls] = gel.astype(BF16)
            ud_ref[rs, ls] = (up * dgel).astype(BF16)

        _sub_blocks(R, cw, stage)
        _sub_blocks(R, cw, main)
        ext[:, 0:H, :] = ext[:, R:R + H, :]

    tile = pl.BlockSpec((R, cw), lambda c, j: (j, c))
    return pl.pallas_call(
        body, out_shape=[SDS((S, D_FF), BF16)] * 3, grid=(D_FF // cw, S // R),
        in_specs=[pl.BlockSpec((R, 2 * cw), lambda c, j: (j, c)), pl.BlockSpec((CONV_F, cw), lambda c, j: (0, c)),
                  pl.BlockSpec((1, cw), lambda c, j: (0, c))],
        out_specs=[tile] * 3,
        scratch_shapes=[pltpu.VMEM((nlt, H + R, LANE), F32)], name=name, compiler_params=_cp(2),
    )(hp, w, b)


def _f_bwd(dact, hp, gel, ud, w, name):
    S = hp.shape[0]
    R, H, cw = min(S, R_FFN), SUB, _CW_F
    nch = S // R
    nlt = cw // LANE

    def body(da_ref, h_ref, hh_ref, gel_ref, ud_ref, w_ref, dh_ref, dw_ref, db_ref, ext_g, ext_d, acc):
        j = pl.program_id(1)
        jj = nch - 1 - j

        @pl.when(j == 0)
        def _():
            ext_d[:, R:R + H, :] = jnp.zeros((nlt, H, LANE), F32)
            acc[...] = jnp.zeros_like(acc)

        for lt in range(nlt):
            ext_g[lt, 0:H, :] = jnp.where(jj == 0, 0.0, hh_ref[_HB - H:_HB, lt * LANE:(lt + 1) * LANE].astype(F32))

        def stage(r0, lt):
            ext_g[lt, pl.ds(pl.multiple_of(r0 + H, SUB), _RB), :] = h_ref[pl.ds(r0, _RB), _lanes(lt)].astype(F32)

        def first(r0, lt):
            ls, lu, rs = _lanes(lt), _lanes(lt + nlt), pl.ds(r0, _RB)
            da = da_ref[rs, ls].astype(F32)
            dh_ref[rs, lu] = (da * gel_ref[rs, ls].astype(F32)).astype(BF16)
            dgp = da * ud_ref[rs, ls].astype(F32)
            ext_d[lt, rs, :] = dgp
            acc[CONV_F * SUB:(CONV_F + 1) * SUB, ls] += _psum8(dgp)
            for k in range(CONV_F):
                tap = ext_g[lt, pl.ds(r0 + (H - (CONV_F - 1 - k)), _RB), :]
                acc[k * SUB:(k + 1) * SUB, ls] += _psum8(dgp * tap)

        def second(r0, lt):
            ls = _lanes(lt)
            dhg = w_ref[CONV_F - 1:CONV_F, ls] * ext_d[lt, pl.ds(r0, _RB), :]
            for k in range(CONV_F - 1):
                dhg = dhg + w_ref[k:k + 1, ls] * ext_d[lt, pl.ds(r0 + (CONV_F - 1 - k), _RB), :]
            dh_ref[pl.ds(r0, _RB), ls] = dhg.astype(BF16)

        _sub_blocks(R, cw, stage)
        _sub_blocks(R, cw, first)
        _sub_blocks(R, cw, second)
        ext_d[:, R:R + H, :] = ext_d[:, 0:H, :]

        @pl.when(j == nch - 1)
        def _():
            for k in range(CONV_F):
                dw_ref[k:k + 1, :] = jnp.sum(acc[k * SUB:(k + 1) * SUB, :], axis=0, keepdims=True)
            db_ref[...] = jnp.sum(acc[CONV_F * SUB:(CONV_F + 1) * SUB, :], axis=0, keepdims=True)

    rows = lambda c, j: (nch - 1 - j, c)
    return pl.pallas_call(
        body, out_shape=[SDS((S, 2 * D_FF), BF16), SDS((CONV_F, D_FF), F32), SDS((1, D_FF), F32)],
        grid=(D_FF // cw, nch),
        in_specs=[pl.BlockSpec((R, cw), rows), pl.BlockSpec((R, cw), lambda c, j: (nch - 1 - j, 2 * c)),
                  pl.BlockSpec((_HB, cw), lambda c, j: (jnp.maximum((nch - 1 - j) * (R // _HB) - 1, 0), 2 * c)),
                  pl.BlockSpec((R, cw), rows), pl.BlockSpec((R, cw), rows),
                  pl.BlockSpec((CONV_F, cw), lambda c, j: (0, c))],
        out_specs=[pl.BlockSpec((R, 2 * cw), rows), pl.BlockSpec((CONV_F, cw), lambda c, j: (0, c)),
                   pl.BlockSpec((1, cw), lambda c, j: (0, c))],
        scratch_shapes=[pltpu.VMEM((nlt, H + R, LANE), F32), pltpu.VMEM((nlt, R + H, LANE), F32),
                        pltpu.VMEM(((CONV_F + 1) * SUB, cw), F32)], name=name,
        compiler_params=_cp(2),
    )(dact, hp, hp, gel, ud, w)


_CW_C = 256
_H_C = 32


def _c_fwd(h1p, w, b):
    S = h1p.shape[0]
    R, H, cw = R_SEQ, _H_C, _CW_C
    nlt = cw // LANE

    def body(h_ref, w_ref, b_ref, o_ref, ext):
        j = pl.program_id(1)

        @pl.when(j == 0)
        def _():
            ext[:, 0:H, :] = jnp.zeros((nlt, H, LANE), F32)

        def stage(r0, lt):
            rs = pl.ds(r0, _RB)
            gate = h_ref[rs, _lanes(lt + nlt)].astype(F32)
            ext[lt, pl.ds(pl.multiple_of(r0 + H, SUB), _RB), :] = h_ref[rs, _lanes(lt)].astype(F32) * _sigmoid(gate)

        def main(r0, lt):
            ls = _lanes(lt)
            cv = b_ref[:, ls]
            for k in range(CONV_C):
                cv = cv + w_ref[k:k + 1, ls] * ext[lt, pl.ds(r0 + (H - (CONV_C - 1 - k)), _RB), :]
            o_ref[pl.ds(r0, _RB), ls] = cv

        _sub_blocks(R, cw, stage)
        _sub_blocks(R, cw, main)
        ext[:, 0:H, :] = ext[:, R:R + H, :]

    return pl.pallas_call(
        body, out_shape=SDS((S, D), F32), grid=(D // cw, S // R),
        in_specs=[pl.BlockSpec((R, 2 * cw), lambda c, j: (j, c)), pl.BlockSpec((CONV_C, cw), lambda c, j: (0, c)),
                  pl.BlockSpec((1, cw), lambda c, j: (0, c))],
        out_specs=pl.BlockSpec((R, cw), lambda c, j: (j, c)),
        scratch_shapes=[pltpu.VMEM((nlt, H + R, LANE), F32)], name="conf_conv_fwd", compiler_params=_cp(2),
    )(h1p, w, b)


def _c_bwd(dcv, h1p, w):
    S = h1p.shape[0]
    R, H, cw, nch = R_SEQ, _H_C, _CW_C, S // R_SEQ
    nlt = cw // LANE
    a_b, a_val, a_gate = CONV_C * SUB, (CONV_C + 1) * SUB, (CONV_C + 2) * SUB

    def body(dc_ref, h_ref, hh_ref, w_ref, dh_ref, dw_ref, db_ref, db1_ref, ext_u, ext_d, acc):
        j = pl.program_id(1)
        jj = nch - 1 - j

        @pl.when(j == 0)
        def _():
            ext_d[:, R:R + H, :] = jnp.zeros((nlt, H, LANE), F32)
            acc[...] = jnp.zeros_like(acc)

        for lt in range(nlt):
            ext_u[lt, 0:H, :] = jnp.where(
                jj == 0, 0.0, hh_ref[:, lt * LANE:(lt + 1) * LANE].astype(F32)
                * _sigmoid(hh_ref[:, cw + lt * LANE:cw + (lt + 1) * LANE].astype(F32)))

        def stage(r0, lt):
            rs, ls = pl.ds(r0, _RB), _lanes(lt)
            gate = h_ref[rs, _lanes(lt + nlt)].astype(F32)
            ext_u[lt, pl.ds(pl.multiple_of(r0 + H, SUB), _RB), :] = h_ref[rs, ls].astype(F32) * _sigmoid(gate)
            ext_d[lt, rs, :] = dc_ref[rs, ls]

        def first(r0, lt):
            ls = _lanes(lt)
            dc = dc_ref[pl.ds(r0, _RB), ls]
            acc[a_b:a_b + SUB, ls] += _psum8(dc)
            for k in range(CONV_C):
                tap = ext_u[lt, pl.ds(r0 + (H - (CONV_C - 1 - k)), _RB), :]
                acc[k * SUB:(k + 1) * SUB, ls] += _psum8(dc * tap)

        def second(r0, lt):
            rs, ls, lg = pl.ds(r0, _RB), _lanes(lt), _lanes(lt + nlt)
            du = w_ref[CONV_C - 1:CONV_C, ls] * ext_d[lt, rs, :]
            for k in range(CONV_C - 1):
                du = du + w_ref[k:k + 1, ls] * ext_d[lt, pl.ds(r0 + (CONV_C - 1 - k), _RB), :]
            val = h_ref[rs, ls].astype(F32)
            sg = _sigmoid(h_ref[rs, lg].astype(F32))
            dval = du * sg
            dgate = du * val * sg * (1.0 - sg)
            acc[a_val:a_val + SUB, ls] += _psum8(dval)
            acc[a_gate:a_gate + SUB, ls] += _psum8(dgate)
            dh_ref[rs, ls] = dval.astype(BF16)
            dh_ref[rs, lg] = dgate.astype(BF16)

        _sub_blocks(R, cw, stage)
        _sub_blocks(R, cw, first)
        _sub_blocks(R, cw, second)
        ext_d[:, R:R + H, :] = ext_d[:, 0:H, :]

        @pl.when(j == nch - 1)
        def _():
            for k in range(CONV_C):
                dw_ref[k:k + 1, :] = jnp.sum(acc[k * SUB:(k + 1) * SUB, :], axis=0, keepdims=True)
            db_ref[...] = jnp.sum(acc[a_b:a_b + SUB, :], axis=0, keepdims=True)
            db1_ref[:, 0:cw] = jnp.sum(acc[a_val:a_val + SUB, :], axis=0, keepdims=True)
            db1_ref[:, cw:2 * cw] = jnp.sum(acc[a_gate:a_gate + SUB, :], axis=0, keepdims=True)

    rows = lambda c, j: (nch - 1 - j, c)
    return pl.pallas_call(
        body, out_shape=[SDS((S, 2 * D), BF16), SDS((CONV_C, D), F32), SDS((1, D), F32), SDS((1, 2 * D), F32)],
        grid=(D // cw, nch),
        in_specs=[pl.BlockSpec((R, cw), rows), pl.BlockSpec((R, 2 * cw), rows),
                  pl.BlockSpec((H, 2 * cw), lambda c, j: (jnp.maximum((nch - 1 - j) * (R // H) - 1, 0), c)),
                  pl.BlockSpec((CONV_C, cw), lambda c, j: (0, c))],
        out_specs=[pl.BlockSpec((R, 2 * cw), rows), pl.BlockSpec((CONV_C, cw), lambda c, j: (0, c)),
                   pl.BlockSpec((1, cw), lambda c, j: (0, c)), pl.BlockSpec((1, 2 * cw), lambda c, j: (0, c))],
        scratch_shapes=[pltpu.VMEM((nlt, H + R, LANE), F32), pltpu.VMEM((nlt, R + H, LANE), F32),
                        pltpu.VMEM(((CONV_C + 3) * SUB, cw), F32)], name="conf_conv_bwd",
        compiler_params=_cp(2),
    )(dcv, h1p, h1p, w)


def _local_step(x, mem, tgt, W, fetch=None, send=None):
    G = {}
    W = dict(W)

    def arrive(group, after):
        if fetch is None:
            return None
        got, tok = fetch(group, after)
        for key, val in got.items():
            W[key] = {**W.get(key, {}), **val} if isinstance(val, dict) else val
        return tok

    def gain(g, tok):
        return g if tok is None else g + tok

    def sent(group):
        return None if send is None else send(group, G)

    def xattn_fwd(xin, n, l):
        tok = arrive(("xa", l), n)
        mn = _rms_fwd(mem, gain(W["xa_mem_norm"][l:l + 1], tok), f"xa_memnorm_fwd{l}")
        q = _mm_nn(n, W["xa_wq"][l], out_dtype=BF16, name=f"xa_q{l}")
        k = _mm_nn(mn, W["xa_wk"][l], out_dtype=BF16, name=f"xa_k{l}")
        v = _mm_nn(mn, W["xa_wv"][l], out_dtype=BF16, name=f"xa_v{l}")
        o = _attn_fwd(q, k, v, f"xa_attn_fwd{l}")
        xout, nout = _mm_nn(o, W["xa_wo"][l], out_dtype=F32, name=f"xa_o{l}", add=xin, norm=W["f_norm"][l:l + 1])
        return xout, nout, (xin, n, q, mn, k, v, o)

    def xattn_bwd(dx, dxb, saved, l):
        xin, n, q, mn, k, v, o = saved
        do = _mm_nt(dxb, W["xa_wo"][l], out_dtype=BF16, name=f"xa_do{l}")
        G[f"xa_wo{l}"] = _mm_tn(o, dxb, out_dtype=BF16, name=f"xa_dwo{l}")
        dq, dk, dv = _attn_bwd(q, k, v, do, f"xa_attn_bwd{l}")
        dkb, dvb = dk.astype(BF16), dv.astype(BF16)
        G[f"xa_wq{l}"] = _mm_tn(n, dq, out_dtype=BF16, name=f"xa_dwq{l}")
        G[f"xa_wk{l}"] = _mm_tn(mn, dkb, out_dtype=BF16, name=f"xa_dwk{l}")
        G[f"xa_wv{l}"] = _mm_tn(mn, dvb, out_dtype=BF16, name=f"xa_dwv{l}")
        tok = sent(("xa", l))
        dmn = _mm_nt(dkb, W["xa_wk"][l], out_dtype=F32, name=f"xa_dmn_k{l}")
        dmn = _mm_nt(dvb, W["xa_wv"][l], out_dtype=F32, name=f"xa_dmn_v{l}", add=dmn)
        (G[f"xa_mem_norm{l}"],) = _rms_bwd(mem, W["xa_mem_norm"][l:l + 1], dmn, None, f"xa_memnorm_bwd{l}")
        dx, dxb, G[f"xa_norm{l}"] = _mm_nt(dq, W["xa_wq"][l], out_dtype=F32, name=f"xa_dn{l}",
                                           rms=(xin, gain(W["xa_norm"][l:l + 1], tok), dx))
        return dx, dxb

    def ffn_fwd(xin, n, l, next_gain):
        tok = arrive(("f", l), n)
        hp = _mm_nn(n, W["f_w_up"][l], out_dtype=BF16, name=f"f_up{l}")
        act, gel, ud = _f_fwd(hp, W["f_dw_w"][l], gain(W["f_dw_b"][l:l + 1], tok), f"f_conv_fwd{l}")
        arrive(("fd", l), act)
        res = _mm_nn(act, W["f_w_down"][l], out_dtype=F32, name=f"f_down{l}", add=xin, norm=next_gain)
        xout, nout = res if next_gain is not None else (res, None)
        return xout, nout, (xin, n, hp, act, gel, ud)

    def ffn_bwd(dx, dxb, saved, l):
        xin, n, hp, act, gel, ud = saved
        dact = _mm_nt(dxb, W["f_w_down"][l], out_dtype=BF16, name=f"f_dact{l}")
        G[f"f_w_down{l}"] = _mm_tn(act, dxb, out_dtype=BF16, name=f"f_dwdown{l}")
        dhp, G[f"f_dw_w{l}"], G[f"f_dw_b{l}"] = _f_bwd(dact, hp, gel, ud, W["f_dw_w"][l], f"f_conv_bwd{l}")
        G[f"f_w_up{l}"] = _mm_tn(n, dhp, out_dtype=BF16, name=f"f_dwup{l}", blocks=_CW_F)
        tok = sent(("f", l))
        dx, dxb, G[f"f_norm{l}"] = _mm_nt(dhp, W["f_w_up"][l], out_dtype=F32, name=f"f_dn{l}",
                                          rms=(xin, gain(W["f_norm"][l:l + 1], tok), dx))
        return dx, dxb

    n0 = _rms_fwd(x, W["ab_norm"], "ab_norm_fwd")
    tok = arrive(("ab", 0), n0)
    a_par = (W["a_conv_w"], gain(W["a_conv_b"], tok), W["a_gate_x_w"], W["a_gate_x_b"], W["a_gate_a_w"],
             W["a_gate_a_b"], W["a_lambda"])
    b_par = (W["b_group_w"], W["b_group_b"], W["b_scale"])
    zp = _mm_nn(n0, W["ab_w_in"], out_dtype=BF16, name="ab_in")
    yab, h_a = _a_fwd(zp, *a_par)
    yab = _b_fwd(zp, yab, *b_par)
    tok = arrive(("ab", 1), yab)
    x1, n1 = _mm_nn(yab, W["ab_w_out"], out_dtype=F32, name="ab_out", add=x, norm=gain(W["xa_norm"][0:1], tok))
    x2, n2, s_xa0 = xattn_fwd(x1, n1, 0)
    x3, n3, s_f0 = ffn_fwd(x2, n2, 0, W["c_norm"])
    tok = arrive(("c", 0), n3)
    h1p = _mm_nn(n3, W["c_w_pw1"], out_dtype=BF16, name="c_pw1", bias=gain(W["c_b_pw1"], tok))
    cv = _c_fwd(h1p, W["c_dw_w"], W["c_dw_b"])
    sc = _ln_silu_fwd(cv, W["c_ln_g"], W["c_ln_b"])
    x4, n4 = _mm_nn(sc, W["c_w_pw2"], out_dtype=F32, name="c_pw2", bias=W["c_b_pw2"], add=x3, norm=W["xa_norm"][1:2])
    x5, n5, s_xa1 = xattn_fwd(x4, n4, 1)
    x6, _, s_f1 = ffn_fwd(x5, n5, 1, None)
    loss, dx, dxb, G["final_norm"] = _loss_head(x6, W["final_norm"], tgt)

    dx, dxb = ffn_bwd(dx, dxb, s_f1, 1)
    dx, dxb = xattn_bwd(dx, dxb, s_xa1, 1)
    dsc = _mm_nt(dxb, W["c_w_pw2"], out_dtype=BF16, name="c_dsc")
    G["c_w_pw2"] = _mm_tn(sc, dxb, out_dtype=BF16, name="c_dwpw2")
    dcv, G["c_ln_g"], G["c_ln_b"], G["c_b_pw2"] = _ln_silu_bwd(dsc, cv, W["c_ln_g"], W["c_ln_b"], dx)
    dh1p, G["c_dw_w"], G["c_dw_b"], G["c_b_pw1"] = _c_bwd(dcv, h1p, W["c_dw_w"])
    G["c_w_pw1"] = _mm_tn(n3, dh1p, out_dtype=BF16, name="c_dwpw1", blocks=_CW_C)
    tok = sent(("c", 0))
    dx, dxb, G["c_norm"] = _mm_nt(dh1p, W["c_w_pw1"], out_dtype=F32, name="c_dn",
                                  rms=(x3, gain(W["c_norm"], tok), dx))
    dx, dxb = ffn_bwd(dx, dxb, s_f0, 0)
    dx, dxb = xattn_bwd(dx, dxb, s_xa0, 0)
    dyab = _mm_nt(dxb, W["ab_w_out"], out_dtype=BF16, name="ab_dyab")
    G["ab_w_out"] = _mm_tn(yab, dxb, out_dtype=BF16, name="ab_dwout")
    tok = sent(("ab", 1))
    a_par = (a_par[0], gain(a_par[1], tok)) + a_par[2:]
    (dzg, dzr, G["a_conv_w"], G["a_conv_b"], G["a_gate_x_w"], G["a_gate_x_b"], G["a_gate_a_w"], G["a_gate_a_b"],
     G["a_lambda"]) = _a_bwd(dyab, zp, h_a, *a_par)
    dzq, G["b_group_w"], G["b_group_b"], G["b_scale"] = _b_bwd(dyab, zp, *b_par)
    G["ab_w_in"] = jnp.concatenate(
        [_mm_tn(n0, dz, out_dtype=BF16, name=f"ab_dwin_{part}")
         for part, dz in (("gate", dzg), ("rec", dzr), ("pool", dzq))], axis=1)
    tok = sent(("ab", 0))
    dx, _, G["ab_norm"] = _mm_nt_cols([dzg, dzr, dzq], W["ab_w_in"], name="ab_dn",
                                      rms=(x, gain(W["ab_norm"], tok), dx))
    return loss, dx, G


def _my_place():
    x, y, c = lax.axis_index("x"), lax.axis_index("y"), lax.axis_index("c")
    return x, y, c


def _all_gather(shards, name):
    n = len(shards)

    def body(*refs):
        ins, outs = refs[:n], refs[n:2 * n]
        send_sems, recv_sems, local_sems = refs[2 * n:]
        x, y, c = _my_place()
        me, sibling = (x, y, c), (x, y, 1 - c)
        chips = [(1 - x, y), (x, 1 - y), (1 - x, 1 - y)]

        def slab(a, place):
            px, py, pc = place
            return outs[a].at[4 * px + 2 * py + pc]

        def copy(a, k, block, to, src=None):
            return pltpu.make_async_remote_copy(
                src_ref=slab(a, block) if src is None else src, dst_ref=slab(a, block),
                send_sem=send_sems.at[a, k], recv_sem=recv_sems.at[a, k], device_id=to, device_id_type=MESH)

        mine = [pltpu.make_async_copy(ins[a], slab(a, me), local_sems.at[a]) for a in range(n)]
        for cp in mine:
            cp.start()
        first = []
        for j, chip in enumerate(chips):
            first += [copy(a, 1 + j, me, (*chip, c), src=ins[a]) for a in range(n)]
        first += [copy(a, 0, me, sibling, src=ins[a]) for a in range(n)]
        for cp in first:
            cp.start()
        passed = []
        for j, chip in enumerate(chips):
            for a in range(n):
                copy(a, 1 + j, (*chip, c), me).wait_recv()
                cp = copy(a, 4 + j, (*chip, c), sibling)
                cp.start()
                passed.append(cp)
        for a in range(n):
            copy(a, 0, sibling, me).wait_recv()
        for j, chip in enumerate(chips):
            for a in range(n):
                copy(a, 4 + j, (*chip, 1 - c), me).wait_recv()
        for cp in first + passed:
            cp.wait_send()
        for cp in mine:
            cp.wait()

    any_spec = pl.BlockSpec(memory_space=pl.ANY)
    return pl.pallas_call(
        body, out_shape=[SDS((N_DEV,) + s.shape, s.dtype) for s in shards], in_specs=[any_spec] * n,
        out_specs=[any_spec] * n,
        scratch_shapes=[pltpu.SemaphoreType.DMA((n, 7)), pltpu.SemaphoreType.DMA((n, 7)), pltpu.SemaphoreType.DMA((n,))],
        name=name,
    )(*shards)


_HBM = pl.BlockSpec(memory_space=pltpu.HBM)
_SEM = pl.BlockSpec(memory_space=pltpu.SEMAPHORE)
_EFFECT = pltpu.SideEffectType.DATAFLOW_SIDE_EFFECTING


def _peer_places():
    x, y, c = _my_place()
    peers = []
    for k in range(1, N_DEV):
        px = 1 - x if (k >> 2) & 1 else x
        py = 1 - y if (k >> 1) & 1 else y
        pc = 1 - c if k & 1 else c
        peers.append(((px, py, pc), 4 * px + 2 * py + pc))
    return (x, y, c), 4 * x + 2 * y + c, peers


def _send_start(srcs, per_dest, name):
    n = len(srcs)
    lands = [lax.empty((N_DEV,) + (s.shape[1:] if per_dest else s.shape), s.dtype) for s in srcs]

    def body(*refs):
        src, land = refs[:n], refs[n:2 * n]
        outs = refs[2 * n:]
        send, recv, token = outs[:n], outs[n:2 * n], outs[4 * n]
        place, me, peers = _peer_places()
        for a in range(n):
            for peer, pidx in peers + [(place, me)]:
                pltpu.make_async_remote_copy(
                    src_ref=src[a].at[pidx] if per_dest else src[a], dst_ref=land[a].at[me], send_sem=send[a],
                    recv_sem=recv[a], device_id=peer, device_id_type=MESH).start()
        token[...] = jnp.zeros_like(token)

    hbm = lambda a: pltpu.HBM(a.shape, a.dtype)
    sem = pltpu.SemaphoreType.DMA(())
    res = pl.pallas_call(
        body, name=name,
        out_shape=tuple([sem] * (2 * n) + [hbm(s) for s in srcs] + [hbm(l) for l in lands]
                        + [SDS((SUB, LANE), F32)]),
        in_specs=[_HBM] * (2 * n),
        out_specs=tuple([_SEM] * (2 * n) + [_HBM] * (2 * n) + [pl.BlockSpec(memory_space=pltpu.VMEM)]),
        input_output_aliases={i: 2 * n + i for i in range(2 * n)},
        compiler_params=pltpu.CompilerParams(has_side_effects=_EFFECT),
    )(*[pltpu.with_memory_space_constraint(s, pltpu.HBM) for s in srcs],
      *[pltpu.with_memory_space_constraint(l, pltpu.HBM) for l in lands])
    return res[:n], res[n:2 * n], res[2 * n:3 * n], res[3 * n:4 * n], res[4 * n]


def _send_wait(send, recv, srcs, lands, after, per_dest, name):
    n = len(srcs)

    def body(*refs):
        src, land = refs[:n], refs[n:2 * n]
        send_s, recv_s = refs[2 * n:3 * n], refs[3 * n:4 * n]
        token = refs[-1]
        place, _, _ = _peer_places()
        for a in range(n):
            copy = pltpu.make_async_remote_copy(
                src_ref=src[a] if per_dest else land[a], dst_ref=land[a], send_sem=send_s[a],
                recv_sem=recv_s[a], device_id=place, device_id_type=MESH)
            copy.wait_send()
            copy.wait_recv()
        token[...] = jnp.zeros_like(token)

    hbm = lambda a: pltpu.HBM(a.shape, a.dtype)
    res = pl.pallas_call(
        body, name=name,
        out_shape=tuple([hbm(s) for s in srcs] + [hbm(l) for l in lands] + [SDS((SUB, LANE), F32)]),
        in_specs=[_HBM] * (2 * n) + [_SEM] * (2 * n) + [pl.BlockSpec(memory_space=pl.ANY)],
        out_specs=tuple([_HBM] * (2 * n) + [pl.BlockSpec(memory_space=pltpu.VMEM)]),
        input_output_aliases={i: i for i in range(2 * n)},
        compiler_params=pltpu.CompilerParams(has_side_effects=_EFFECT),
    )(*srcs, *lands, *send, *recv, after)
    return res[:n], res[n:2 * n], res[2 * n]


def _adamw_math(w, g, m, v):
    m = ADAM_B1 * m + (1.0 - ADAM_B1) * g
    v = ADAM_B2 * v + (1.0 - ADAM_B2) * (g * g)
    m_hat = m / (1.0 - ADAM_B1 ** ADAM_STEP)
    v_hat = v / (1.0 - ADAM_B2 ** ADAM_STEP)
    delta = -ADAM_LR * (m_hat / (jnp.sqrt(v_hat) + ADAM_EPS) + ADAM_WD * w)
    return delta, m, v


def _row_tile(r, c, itemsize_rows):
    cap = max(SUB, (itemsize_rows // (4 * c)) // SUB * SUB)
    if r <= cap:
        return r
    best = None
    for t in range(SUB, cap + 1, SUB):
        if r % t == 0:
            best = t
    return best if best is not None else r


def _sum_adamw(landing, w, m, v, name, layer=0, prev=None, after=None):
    _, r, c = landing.shape
    tr = _row_tile(r, c, 2 << 20)
    off = layer * (r // tr)
    tail = ([] if prev is None else list(prev)) + ([] if after is None else [after])

    def body(l_ref, w_ref, m_ref, v_ref, *rest):
        g_ref, d_ref, mo_ref, vo_ref = rest[-4:]
        g = l_ref[0].astype(F32)
        for s in range(1, N_DEV):
            g = g + l_ref[s].astype(F32)
        g_ref[...] = g
        d_ref[...], mo_ref[...], vo_ref[...] = _adamw_math(w_ref[...], g, m_ref[...], v_ref[...])

    blk = pl.BlockSpec((tr, c), lambda i: (i + off, 0))
    n_prev = 0 if prev is None else 4
    return pl.pallas_call(
        body, out_shape=[SDS(w.shape, F32)] * 4, grid=(r // tr,),
        in_specs=[pl.BlockSpec((N_DEV, tr, c), lambda i: (0, i, 0)), blk, blk, blk]
        + [pl.BlockSpec(memory_space=pl.ANY)] * len(tail),
        out_specs=[blk] * 4, input_output_aliases={4 + i: i for i in range(n_prev)}, name=name,
        compiler_params=_cp(1),
    )(landing, w, m, v, *tail)


def _sum8(landing, name):
    _, r, c = landing.shape

    def body(l_ref, g_ref):
        g = l_ref[0]
        for s in range(1, N_DEV):
            g = g + l_ref[s]
        g_ref[...] = g

    return pl.pallas_call(body, out_shape=SDS((r, c), F32), name=name, compiler_params=_cp(0))(landing)


def _adamw_small(repl_pack, own_pack, P, M, V):
    table, off = [], 0
    for name, shape in _REPL.items():
        table.append((name, shape if len(shape) > 1 else (1,) + shape, 0, off // LANE))
        off += _size(shape)
    off = _REPL_ROWS * LANE
    for name, shape in _SMALL_SHARDED.items():
        table.append((name, shape, 1, off // LANE))
        off += _size(shape)
    n = len(table)

    def body(*refs):
        packs, ins, outs = refs[:2], refs[2:2 + 3 * n], refs[2 + 3 * n:]
        for p, (_, shape, which, r0) in enumerate(table):
            w_ref, m_ref, v_ref = ins[3 * p:3 * p + 3]
            g_ref, d_ref, mo_ref, vo_ref = outs[4 * p:4 * p + 4]
            pack, rows, q = packs[which], shape[-2], shape[-1] // LANE
            lead = [()]
            for dim in shape[:-2]:
                lead = [t + (i,) for t in lead for i in range(dim)]
            for li, idx in enumerate(lead):
                if q == 1:
                    dst = g_ref.at[idx] if idx else g_ref
                    dst[...] = pack[r0 + li * rows:r0 + (li + 1) * rows, :]
                    continue
                for i in range(rows):
                    for k in range(q):
                        row = r0 + (li * rows + i) * q + k
                        g_ref[idx + (slice(i, i + 1), slice(k * LANE, (k + 1) * LANE))] = pack[row:row + 1, :]
            d_ref[...], mo_ref[...], vo_ref[...] = _adamw_math(w_ref[...], g_ref[...], m_ref[...], v_ref[...])

    ins, out_shape = [], []
    for name, shape, _, _ in table:
        ins += [t[name].reshape(shape) for t in (P, M, V)]
        out_shape += [SDS(shape, F32)] * 4
    res = pl.pallas_call(body, out_shape=out_shape, name="adamw_small", compiler_params=_cp(0))(
        repl_pack, own_pack, *ins)
    dicts = ({}, {}, {}, {})
    for p, (name, shape, _, _) in enumerate(table):
        for d, arr in zip(dicts, res[4 * p:4 * p + 4]):
            d[name] = arr.reshape(P[name].shape)
    return dicts


_BIG = {
    "ab_w_in": (1, D, 320), "ab_w_out": (1, 192, D), "c_w_pw1": (1, D, 256), "c_w_pw2": (1, 128, D),
    "xa_wq": (2, 128, D), "xa_wk": (2, 128, D), "xa_wv": (2, 128, D), "xa_wo": (2, 128, D),
    "f_w_up": (2, D, 768), "f_w_down": (2, 384, D),
}
_SMALL_SHARDED = {
    "a_conv_w": (1, 4, 128), "c_norm": (1, 128), "c_b_pw1": (1, 256), "c_dw_w": (1, 31, 128), "c_dw_b": (1, 128),
    "c_ln_g": (1, 128), "c_ln_b": (1, 128), "c_b_pw2": (1, 128), "f_dw_w": (2, 3, 384),
}
_REPL = {
    "ab_norm": (1, D), "a_conv_b": (1, D), "a_gate_x_w": (1, 8, 128, 128), "a_gate_x_b": (1, D),
    "a_gate_a_w": (1, 8, 128, 128), "a_gate_a_b": (1, D), "a_lambda": (1, D), "b_group_w": (1, 4, 128, 128),
    "b_group_b": (1, 512), "b_scale": (1, 512), "xa_norm": (2, D), "xa_mem_norm": (2, D), "f_norm": (2, D),
    "f_dw_b": (2, D_FF), "final_norm": (D,),
}


def _size(shape):
    n = 1
    for s in shape:
        n *= s
    return n


_N_SS = sum(_size(s) for s in _SMALL_SHARDED.values())
_N_REPL = sum(_size(s) for s in _REPL.values())
_REPL_ROWS = -(-_N_REPL // (N_DEV * SUB * LANE)) * SUB
_SS_ROWS = _N_SS // LANE
_SMALL_ROWS = -(-(_REPL_ROWS + _SS_ROWS) // SUB) * SUB


def _pack(parts, rows):
    flat = jnp.concatenate([p.reshape(-1).astype(F32) for p in parts])
    return jnp.pad(flat, (0, rows * LANE - flat.shape[0])).reshape(rows, LANE)


def _pair_blocks(v, bw):
    lead, n = v.shape[:-1], v.shape[-1]
    return jnp.swapaxes(v.reshape(lead + (2, n // (2 * bw), bw)), -3, -2).reshape(lead + (n,))


def _unpair_blocks(v, bw):
    lead, n = v.shape[:-1], v.shape[-1]
    return jnp.swapaxes(v.reshape(lead + (n // (2 * bw), 2, bw)), -3, -2).reshape(lead + (n,))


_GROUPS = {
    ("ab", 0): (("ab_w_in", 0),),
    ("ab", 1): (("ab_w_out", 0),),
    ("xa", 0): (("xa_wq", 0), ("xa_wk", 0), ("xa_wv", 0), ("xa_wo", 0)),
    ("f", 0): (("f_w_up", 0),),
    ("fd", 0): (("f_w_down", 0),),
    ("c", 0): (("c_w_pw1", 0), ("c_w_pw2", 0)),
    ("xa", 1): (("xa_wq", 1), ("xa_wk", 1), ("xa_wv", 1), ("xa_wo", 1)),
    ("f", 1): (("f_w_up", 1),),
    ("fd", 1): (("f_w_down", 1),),
}
_SEND_GROUPS = {g: m for g, m in _GROUPS.items() if g[0] != "fd"}
_SEND_GROUPS[("f", 0)] = (("f_w_up", 0), ("f_w_down", 0))
_SEND_GROUPS[("f", 1)] = (("f_w_up", 1), ("f_w_down", 1))


def _weight_layout(name, g):
    if name == "ab_w_in":
        return jnp.swapaxes(g, 0, 1).reshape(D, N_DEV * 320)
    if name in ("c_w_pw1", "f_w_up"):
        return g
    return g.reshape(N_DEV * g.shape[1], D)


def _grad_blocks(name, l, G):
    _, r, c = _BIG[name]
    if name == "ab_w_in":
        return jnp.swapaxes(G[name].reshape(D, N_DEV, 320), 0, 1)
    if name == "c_w_pw1":
        return G[name]
    if name == "f_w_up":
        return G[f"{name}{l}"]
    return (G[name] if _BIG[name][0] == 1 else G[f"{name}{l}"]).reshape(N_DEV, r, c)


def _small_layouts(sm):
    W = {}
    sm = sm.reshape(N_DEV, -1)
    off = 0
    for name, shape in _SMALL_SHARDED.items():
        n = _size(shape)
        blocks = sm[:, off:off + n].reshape((N_DEV,) + shape)
        off += n
        W[name] = jnp.moveaxis(blocks, 0, -2).reshape(shape[:-1] + (N_DEV * shape[-1],))
    W["a_conv_w"], W["c_dw_w"] = W["a_conv_w"][0], W["c_dw_w"][0]
    W["c_b_pw1"] = _pair_blocks(W["c_b_pw1"], _CW_C)
    return W


def _to_dest_major(g, shape):
    full = g.reshape(shape[:-1] + (N_DEV, shape[-1]))
    return jnp.moveaxis(full, -2, 0).reshape(N_DEV, -1)


def kernel(x, mem, ab_norm, ab_w_in, a_conv_w, a_conv_b, a_gate_x_w, a_gate_x_b, a_gate_a_w, a_gate_a_b, a_lambda, b_group_w, b_group_b, b_scale, ab_w_out, c_norm, c_w_pw1, c_b_pw1, c_dw_w, c_dw_b, c_ln_g, c_ln_b, c_w_pw2, c_b_pw2, xa_norm, xa_mem_norm, xa_wq, xa_wk, xa_wv, xa_wo, f_norm, f_w_up, f_dw_w, f_dw_b, f_w_down, final_norm, loss_target, m_ab_norm, m_ab_w_in, m_a_conv_w, m_a_conv_b, m_a_gate_x_w, m_a_gate_x_b, m_a_gate_a_w, m_a_gate_a_b, m_a_lambda, m_b_group_w, m_b_group_b, m_b_scale, m_ab_w_out, m_c_norm, m_c_w_pw1, m_c_b_pw1, m_c_dw_w, m_c_dw_b, m_c_ln_g, m_c_ln_b, m_c_w_pw2, m_c_b_pw2, m_xa_norm, m_xa_mem_norm, m_xa_wq, m_xa_wk, m_xa_wv, m_xa_wo, m_f_norm, m_f_w_up, m_f_dw_w, m_f_dw_b, m_f_w_down, m_final_norm, v_ab_norm, v_ab_w_in, v_a_conv_w, v_a_conv_b, v_a_gate_x_w, v_a_gate_x_b, v_a_gate_a_w, v_a_gate_a_b, v_a_lambda, v_b_group_w, v_b_group_b, v_b_scale, v_ab_w_out, v_c_norm, v_c_w_pw1, v_c_b_pw1, v_c_dw_w, v_c_dw_b, v_c_ln_g, v_c_ln_b, v_c_w_pw2, v_c_b_pw2, v_xa_norm, v_xa_mem_norm, v_xa_wq, v_xa_wk, v_xa_wv, v_xa_wo, v_f_norm, v_f_w_up, v_f_dw_w, v_f_dw_b, v_f_w_down, v_final_norm):
    args = dict(locals())
    P = {n: args[n] for n in _NAMES}
    M = {n: args["m_" + n] for n in _NAMES}
    V = {n: args["v_" + n] for n in _NAMES}

    in_flight = {}

    def launch(groups, tok):
        shards, n_of = [], {}
        for grp in groups:
            for name, l in _GROUPS[grp]:
                w = P[name][l] if tok is None else P[name][l] + tok
                shards.append(w.astype(BF16))
            if grp == ("ab", 0):
                shards.append(_pack([P[n] for n in _SMALL_SHARDED], _SS_ROWS + 4))
            n_of[grp] = len(shards)
        res = _send_start(shards, False, "gather_start_" + "_".join(g[0] + str(g[1]) for g in groups))
        lo = 0
        for grp in groups:
            in_flight[grp] = [r[lo:n_of[grp]] for r in res[:4]]
            lo = n_of[grp]
        return res[4][:1, :1]

    follow = {("ab", 0): [("ab", 1), ("xa", 0), ("f", 0), ("fd", 0), ("c", 0), ("xa", 1), ("f", 1), ("fd", 1)]}

    def fetch(grp, after):
        send_s, recv_s, srcs, lands = in_flight.pop(grp)
        srcs, lands, tok = _send_wait(send_s, recv_s, srcs, lands, after, False, f"gather_wait_{grp[0]}{grp[1]}")
        tok = launch(follow[grp], tok[:1, :1]) if grp in follow else None
        full = lands
        out = {}
        for (name, l), g in zip(_GROUPS[grp], full):
            w = _weight_layout(name, g)
            if _BIG[name][0] == 1:
                out[name] = w
            else:
                out[name] = {l: w}
        if grp == ("ab", 0):
            out.update(_small_layouts(full[-1]))
        return out, tok

    zero = launch([("ab", 0)], None)

    pending, held = [], []
    rides_with_next = {("xa", 1), ("f", 0)}

    def send(grp, G):
        held.extend(_SEND_GROUPS[grp])
        if grp in rides_with_next:
            return None
        members = tuple(held)
        del held[:]
        res = _send_start([_grad_blocks(name, l, G) for name, l in members], True, f"send_{grp[0]}{grp[1]}")
        pending.append((members, res))
        return res[4][:1, :1]

    W = {n: P[n] for n in _REPL}
    W["ab_norm"] = P["ab_norm"] + zero
    W["final_norm"] = P["final_norm"].reshape(1, D)
    W["a_gate_x_w"], W["a_gate_a_w"], W["b_group_w"] = P["a_gate_x_w"][0], P["a_gate_a_w"][0], P["b_group_w"][0]
    loss, grad_x, G = _local_step(x[0], mem[0], loss_target[0], W, fetch, send)
    loss = lax.psum(loss[0, 0], ("x", "y", "c"))

    Gs = dict(G)
    Gs["c_b_pw1"] = _unpair_blocks(G["c_b_pw1"], _CW_C)
    Gs["f_dw_w"] = jnp.stack([G["f_dw_w0"], G["f_dw_w1"]])
    Gs["a_conv_w"], Gs["c_dw_w"] = G["a_conv_w"][None], G["c_dw_w"][None]
    for n in ("xa_norm", "xa_mem_norm", "f_norm", "f_dw_b"):
        Gs[n] = jnp.concatenate([G[f"{n}0"], G[f"{n}1"]], axis=0)
    for n in ("a_gate_x_w", "a_gate_a_w", "b_group_w"):
        Gs[n] = G[n][None]
    repl_flat = jnp.concatenate([Gs[n].reshape(-1) for n in _REPL])
    repl_rows = jnp.pad(repl_flat, (0, N_DEV * _REPL_ROWS * LANE - _N_REPL)).reshape(N_DEV, _REPL_ROWS, LANE)
    ss_rows = jnp.concatenate([_to_dest_major(Gs[n], s) for n, s in _SMALL_SHARDED.items()], axis=1)
    ss_rows = ss_rows.reshape(N_DEV, _SS_ROWS, LANE)
    small_pack = jnp.concatenate(
        [repl_rows, ss_rows, jnp.zeros((N_DEV, _SMALL_ROWS - _REPL_ROWS - _SS_ROWS, LANE), F32)], axis=1)
    last = _send_start([small_pack], True, "send_small")
    pending.append(((("small", 0),), last))

    def arrived(some, after, name):
        members = [m for mem_, _ in some for m in mem_]
        cat = [[a for _, res in some for a in res[i]] for i in range(4)]
        srcs, lands, _ = _send_wait(cat[0], cat[1], cat[2], cat[3], after, True, name)
        return dict(zip(members, lands))

    out_g, out_d, out_m, out_v = {}, {}, {}, {}
    chain = [None]

    def update(name, landed):
        layers, r, c = _BIG[name]
        w2, m2, v2 = [t[name].reshape(layers * r, c) for t in (P, M, V)]
        res = None
        for l in range(layers):
            res = _sum_adamw(landed[(name, l)], w2, m2, v2, f"adamw_{name}{l}", layer=l, prev=res,
                             after=chain[0] if l == 0 else None)
        chain[0] = res[1]
        out_g[name], out_d[name], out_m[name], out_v[name] = [t.reshape(P[name].shape) for t in res]

    landed = arrived(pending[:-2], grad_x, "send_wait_early")
    for name in _BIG:
        if name != "ab_w_in":
            update(name, landed)
    landed = arrived(pending[-2:], out_v["f_w_down"], "send_wait_late")
    update("ab_w_in", landed)

    small_sum = _sum8(landed[("small", 0)], "sum_small")
    (repl_all,) = _all_gather([small_sum[:_REPL_ROWS]], "gather_small_grads")
    for out, got in zip((out_g, out_d, out_m, out_v),
                        _adamw_small(repl_all.reshape(N_DEV * _REPL_ROWS, LANE), small_sum, P, M, V)):
        out.update(got)

    return (loss, grad_x[None], *[out_g[n] for n in _NAMES], *[out_d[n] for n in _NAMES],
            *[out_m[n] for n in _NAMES], *[out_v[n] for n in _NAMES])


_NAMES = ("ab_norm", "ab_w_in", "a_conv_w", "a_conv_b", "a_gate_x_w", "a_gate_x_b", "a_gate_a_w", "a_gate_a_b",
          "a_lambda", "b_group_w", "b_group_b", "b_scale", "ab_w_out", "c_norm", "c_w_pw1", "c_b_pw1", "c_dw_w",
          "c_dw_b", "c_ln_g", "c_ln_b", "c_w_pw2", "c_b_pw2", "xa_norm", "xa_mem_norm", "xa_wq", "xa_wk", "xa_wv",
          "xa_wo", "f_norm", "f_w_up", "f_dw_w", "f_dw_b", "f_w_down", "final_norm")
```

```python
import functools

import jax
import jax.numpy as jnp
from jax import lax
from jax.experimental import pallas as pl
from jax.experimental.pallas import tpu as pltpu

F32, BF16 = jnp.float32, jnp.bfloat16
SDS = jax.ShapeDtypeStruct
MESH = pl.DeviceIdType.MESH

N_DEV = 8
D = 1024
N_MEM = 256
XA_HEADS, XA_HD = 4, 256
HD_A = 128
CONV_A, CONV_C, CONV_F = 4, 31, 3
C_RG = 8.0
POOL_WINDOWS = (2, 4, 8, 16)
D_FF = 3 * D
EPS = 1e-6
ADAM_LR, ADAM_B1, ADAM_B2, ADAM_EPS, ADAM_WD, ADAM_STEP = 0.001, 0.9, 0.999, 1e-08, 0.01, 10

LANE = 128
SUB = 8
VMEM_LIMIT = 56 * 1024 * 1024
R_SEQ = 1024
R_POOL = 2048
R_RGLRU = 2048
R_FFN = 2048
TM_ROW = 1024


def _cp(n_axes):
    return pltpu.CompilerParams(dimension_semantics=("arbitrary",) * n_axes, vmem_limit_bytes=VMEM_LIMIT)


def _tile(n, pref):
    if n <= pref:
        return n
    best = None
    for t in range(LANE, pref + 1, LANE):
        if n % t == 0:
            best = t
    assert best is not None, (n, pref)
    return best


def _perm2(n):
    return (n % 2) * 4 + n // 2


_NN = (((1,), (0,)), ((), ()))
_NT = (((1,), (1,)), ((), ()))
_TN = (((0,), (0,)), ((), ()))


def _mm_call(name, grid, ab, ab_specs, dims, acc_shape, extras, outs, finish, from_ref=False):
    nk = grid[2]
    n_ab, n_ex, n_out = len(ab), len(extras), len(outs)
    use_acc = nk > 1 or from_ref

    def product(refs):
        r = lax.dot_general(refs[0][...], refs[1][...], dims, preferred_element_type=F32)
        for i in range(1, n_ab):
            r = r + lax.dot_general(refs[2 * i][...], refs[2 * i + 1][...], dims, preferred_element_type=F32)
        return r

    def body(*refs):
        rest = refs[2 * n_ab:]
        ex_refs, o_refs = rest[:n_ex], rest[n_ex:n_ex + n_out]
        first_rows = pl.program_id(0) == 0
        if not use_acc:
            finish(product(refs), ex_refs, o_refs, first_rows)
            return
        acc = rest[n_ex + n_out]
        if nk == 1:
            acc[...] = product(refs)
            finish(acc, ex_refs, o_refs, first_rows)
            return
        k = pl.program_id(2)

        @pl.when(k == 0)
        def _():
            acc[...] = jnp.zeros_like(acc)

        acc[...] += product(refs)

        @pl.when(k == nk - 1)
        def _():
            finish(acc if from_ref else acc[...], ex_refs, o_refs, first_rows)

    res = pl.pallas_call(
        body, out_shape=[o for o, _ in outs], grid=grid,
        in_specs=list(ab_specs) + [s for _, s in extras], out_specs=[s for _, s in outs],
        scratch_shapes=[pltpu.VMEM(acc_shape, F32)] if use_acc else [], name=name, compiler_params=_cp(3),
    )(*[t for pair in ab for t in pair], *[e for e, _ in extras])
    return res[0] if n_out == 1 else res


def _finish_sum(r, ex_refs, o_refs, first_rows):
    del first_rows
    for e in ex_refs:
        r = r + e[...]
    o_refs[0][...] = r.astype(o_refs[0].dtype)


def _finish_sum_norm(r, ex_refs, o_refs, first_rows):
    del first_rows
    for e in ex_refs[:-1]:
        r = r + e[...]
    o_refs[0][...] = r
    o_refs[1][...] = ((r * lax.rsqrt(jnp.mean(r * r, axis=-1, keepdims=True) + EPS)) * ex_refs[-1][...]).astype(BF16)


_EPI_ROWS = 16


def _finish_rms_bwd(r_ref, ex_refs, o_refs, first_rows):
    x_ref, g_ref, dres_ref = ex_refs
    dx_ref, dxb_ref, dg_ref = o_refs

    @pl.when(first_rows)
    def _():
        dg_ref[...] = jnp.zeros_like(dg_ref)

    gv = g_ref[...]
    inv_d = 1.0 / r_ref.shape[1]

    def step(i, dg_acc):
        groups = [pl.ds(pl.multiple_of(i * (2 * _EPI_ROWS) + u * _EPI_ROWS, _EPI_ROWS), _EPI_ROWS) for u in range(2)]
        sums = []
        for rows in groups:
            r, xf = r_ref[rows, :], x_ref[rows, :]
            sums.append((jnp.sum(xf * xf, axis=-1, keepdims=True), jnp.sum((r * gv) * xf, axis=-1, keepdims=True)))
        for rows, (sxx, sax) in zip(groups, sums):
            r, xf = r_ref[rows, :], x_ref[rows, :]
            rs = lax.rsqrt(sxx * inv_d + EPS)
            dg_acc = dg_acc + _psum8(r * (xf * rs))
            dx = rs * (r * gv) - xf * (rs * rs * (sax * rs * inv_d)) + dres_ref[rows, :]
            dx_ref[rows, :] = dx
            dxb_ref[rows, :] = dx.astype(BF16)
        return dg_acc

    dg_acc = lax.fori_loop(0, r_ref.shape[0] // (2 * _EPI_ROWS), step, jnp.zeros((SUB, r_ref.shape[1]), F32))
    dg_ref[...] += jnp.sum(dg_acc, axis=0, keepdims=True)


def _rms_bwd_io(M, tm, x, g, dres):
    rows = pl.BlockSpec((tm, D), lambda m, n, k: (m, 0))
    vec = pl.BlockSpec((1, D), lambda m, n, k: (0, 0))
    return ([(x, rows), (g, vec), (dres, rows)],
            [(SDS((M, D), F32), rows), (SDS((M, D), BF16), rows), (SDS((1, D), F32), vec)])


_K_WHOLE = 3072


def _mm_nn(a, b, *, out_dtype, name, bias=None, add=None, norm=None):
    M, K = a.shape
    tk = K if K <= _K_WHOLE else _tile(K, 1024)
    if K <= 1024 and norm is None:
        tm = _tile(M, 2048 if add is None and out_dtype == BF16 else 1024)
    else:
        tm = _tile(M, 512)
    if b.ndim == 3:
        nb, _, bw = b.shape
        N, tn, nn = nb * bw, bw, nb
        b_spec = pl.BlockSpec((None, tk, bw), lambda m, n, k: (_perm2(n), k, 0))
    else:
        N = b.shape[1]
        tn = _tile(N, 1024)
        nn = N // tn
        b_spec = pl.BlockSpec((tk, tn), lambda m, n, k: (k, n))
    tile = pl.BlockSpec((tm, tn), lambda m, n, k: (m, n))
    vec = pl.BlockSpec((1, tn), lambda m, n, k: (0, n))
    extras = ([] if bias is None else [(bias, vec)]) + ([] if add is None else [(add, tile)])
    outs, finish = [(SDS((M, N), out_dtype), tile)], _finish_sum
    if norm is not None:
        assert tn == N == D and out_dtype == F32
        extras.append((norm, vec))
        outs, finish = outs + [(SDS((M, N), BF16), tile)], _finish_sum_norm
    return _mm_call(name, (M // tm, nn, K // tk), [(a, b)], [pl.BlockSpec((tm, tk), lambda m, n, k: (m, k)), b_spec],
                    _NN, (tm, tn), extras, outs, finish)


def _mm_nt(a, b, *, out_dtype, name, add=None, rms=None):
    M, N = a.shape
    if b.ndim == 3:
        nb, Ko, bw = b.shape
        tm = _tile(M, 1024)
        tn, tk, nk = _tile(Ko, 1024), bw, nb
        b_spec = pl.BlockSpec((None, tn, bw), lambda m, n, k: (_perm2(k), n, 0))
    else:
        Ko = b.shape[0]
        tk = N if N <= _K_WHOLE else _tile(N, 1024)
        if N <= 1024 and rms is None:
            tm = _tile(M, 2048 if add is None and out_dtype == BF16 else 1024)
        else:
            tm = _tile(M, 512)
        tn = _tile(Ko, 1024)
        nk = N // tk
        b_spec = pl.BlockSpec((tn, tk), lambda m, n, k: (n, k))
    tile = pl.BlockSpec((tm, tn), lambda m, n, k: (m, n))
    extras = [] if add is None else [(add, tile)]
    outs, finish = [(SDS((M, Ko), out_dtype), tile)], _finish_sum
    if rms is not None:
        assert tn == Ko == D and add is None
        (extras, outs), finish = _rms_bwd_io(M, tm, *rms), _finish_rms_bwd
    return _mm_call(name, (M // tm, Ko // tn, nk), [(a, b)], [pl.BlockSpec((tm, tk), lambda m, n, k: (m, k)), b_spec],
                    _NT, (tm, tn), extras, outs, finish, from_ref=rms is not None)


def _mm_nt_cols(parts, b, *, name, rms):
    M = parts[0].shape[0]
    tm = _tile(M, 512)
    specs, off = [], 0
    for p in parts:
        w = p.shape[1]
        assert off % w == 0
        specs.append(pl.BlockSpec((tm, w), lambda m, n, k: (m, 0)))
        specs.append(pl.BlockSpec((D, w), functools.partial(lambda m, n, k, o: (0, o), o=off // w)))
        off += w
    extras, outs = _rms_bwd_io(M, tm, *rms)
    return _mm_call(name, (M // tm, 1, 1), [(p, b) for p in parts], specs, _NT, (tm, D), extras, outs, _finish_rms_bwd,
                    from_ref=True)


def _mm_tn(a, b, *, out_dtype, name, blocks=None):
    S, Ka = a.shape
    Nb = b.shape[1]
    tm = _tile(Ka, 1024)
    if blocks is not None:
        bw = blocks
        tn, nn = bw, Nb // bw
        out = (SDS((nn, Ka, bw), out_dtype), pl.BlockSpec((None, tm, bw), lambda m, n, k: (_perm2(n), m, 0)))
    else:
        tn = _tile(Nb, 1024)
        nn = Nb // tn
        out = (SDS((Ka, Nb), out_dtype), pl.BlockSpec((tm, tn), lambda m, n, k: (m, n)))
    steps = (Ka // tm) * nn
    tk = _tile(S, 4096 if steps >= 4 else 2048 if steps >= 2 else 1024)
    return _mm_call(name, (Ka // tm, nn, S // tk), [(a, b)],
                    [pl.BlockSpec((tk, tm), lambda m, n, k: (k, m)), pl.BlockSpec((tk, tn), lambda m, n, k: (k, n))],
                    _TN, (tm, tn), [], [out], _finish_sum)


def _row(tm, c):
    return pl.BlockSpec((tm, c), lambda i: (i, 0))


def _full(shape):
    nd = len(shape)
    return pl.BlockSpec(shape, lambda i: (0,) * nd)


def _rms_fwd(x, g, name):
    S = x.shape[0]
    tm = min(S, TM_ROW)

    def body(x_ref, g_ref, o_ref):
        xf = x_ref[...]
        r = lax.rsqrt(jnp.mean(xf * xf, axis=-1, keepdims=True) + EPS)
        o_ref[...] = ((xf * r) * g_ref[...]).astype(BF16)

    return pl.pallas_call(body, out_shape=SDS((S, D), BF16), grid=(S // tm,), in_specs=[_row(tm, D), _full((1, D))],
                          out_specs=_row(tm, D), name=name, compiler_params=_cp(1))(x, g)


def _rms_bwd(x, g, dn, dres, name):
    S = x.shape[0]
    tm = min(S, TM_ROW)
    want_dx = dres is not None

    def body(x_ref, g_ref, dn_ref, *rest):
        i = pl.program_id(0)
        dg_ref = rest[-1]

        @pl.when(i == 0)
        def _():
            dg_ref[...] = jnp.zeros_like(dg_ref)

        xf = x_ref[...]
        r = lax.rsqrt(jnp.mean(xf * xf, axis=-1, keepdims=True) + EPS)
        y = xf * r
        dn_v = dn_ref[...]
        dg_ref[...] += jnp.sum(dn_v * y, axis=0, keepdims=True)
        if want_dx:
            dres_ref, dx_ref, dxb_ref = rest[0], rest[1], rest[2]
            dy = dn_v * g_ref[...]
            dx = r * (dy - y * jnp.mean(dy * y, axis=-1, keepdims=True)) + dres_ref[...]
            dx_ref[...] = dx
            dxb_ref[...] = dx.astype(BF16)

    ins = [x, g, dn] + ([dres] if want_dx else [])
    in_specs = [_row(tm, D), _full((1, D)), _row(tm, D)] + ([_row(tm, D)] if want_dx else [])
    outs = ([SDS((S, D), F32), SDS((S, D), BF16)] if want_dx else []) + [SDS((1, D), F32)]
    out_specs = ([_row(tm, D), _row(tm, D)] if want_dx else []) + [_full((1, D))]
    return pl.pallas_call(body, out_shape=outs, grid=(S // tm,), in_specs=in_specs, out_specs=out_specs, name=name,
                          compiler_params=_cp(1))(*ins)


def _loss_head(x, g, tgt):
    S = x.shape[0]
    tm = min(S, TM_ROW)

    def body(x_ref, g_ref, t_ref, loss_ref, dx_ref, dxb_ref, dg_ref):
        i = pl.program_id(0)

        @pl.when(i == 0)
        def _():
            loss_ref[...] = jnp.zeros_like(loss_ref)
            dg_ref[...] = jnp.zeros_like(dg_ref)

        xf = x_ref[...]
        r = lax.rsqrt(jnp.mean(xf * xf, axis=-1, keepdims=True) + EPS)
        y = xf * r
        gv = g_ref[...]
        err = y * gv - t_ref[...]
        per_row = jnp.mean(err * err, axis=-1, keepdims=True)
        loss_ref[...] += 0.5 * jnp.sum(per_row, axis=0, keepdims=True)
        dn_v = err * (1.0 / D)
        dg_ref[...] += jnp.sum(dn_v * y, axis=0, keepdims=True)
        dy = dn_v * gv
        dx = r * (dy - y * jnp.mean(dy * y, axis=-1, keepdims=True))
        dx_ref[...] = dx
        dxb_ref[...] = dx.astype(BF16)

    return pl.pallas_call(
        body, out_shape=[SDS((1, 1), F32), SDS((S, D), F32), SDS((S, D), BF16), SDS((1, D), F32)], grid=(S // tm,),
        in_specs=[_row(tm, D), _full((1, D)), _row(tm, D)],
        out_specs=[_full((1, 1)), _row(tm, D), _row(tm, D), _full((1, D))], name="loss_head", compiler_params=_cp(1),
    )(x, g, tgt)


def _softmax_rows(s):
    m = jnp.max(s, axis=-1, keepdims=True)
    e = jnp.exp(s - m)
    return e / jnp.sum(e, axis=-1, keepdims=True)


def _attn_fwd(q, k, v, name):
    S = q.shape[0]
    tm = min(S, TM_ROW)
    scale = XA_HD ** -0.5

    def body(q_ref, k_ref, v_ref, o_ref):
        for h in range(XA_HEADS):
            sl = slice(h * XA_HD, (h + 1) * XA_HD)
            s = lax.dot_general(q_ref[:, sl], k_ref[:, sl], _NT, preferred_element_type=F32) * scale
            p = _softmax_rows(s)
            o_ref[:, sl] = lax.dot_general(p.astype(BF16), v_ref[:, sl], _NN, preferred_element_type=F32).astype(BF16)

    return pl.pallas_call(body, out_shape=SDS((S, D), BF16), grid=(S // tm,),
                          in_specs=[_row(tm, D), _full((N_MEM, D)), _full((N_MEM, D))], out_specs=_row(tm, D),
                          name=name, compiler_params=_cp(1))(q, k, v)


def _attn_bwd(q, k, v, do, name):
    S = q.shape[0]
    tm = min(S, TM_ROW)
    scale = XA_HD ** -0.5

    def body(q_ref, k_ref, v_ref, do_ref, dq_ref, dk_ref, dv_ref):
        i = pl.program_id(0)

        @pl.when(i == 0)
        def _():
            dk_ref[...] = jnp.zeros_like(dk_ref)
            dv_ref[...] = jnp.zeros_like(dv_ref)

        for h in range(XA_HEADS):
            sl = slice(h * XA_HD, (h + 1) * XA_HD)
            qh, kh, vh, doh = q_ref[:, sl], k_ref[:, sl], v_ref[:, sl], do_ref[:, sl]
            s = lax.dot_general(qh, kh, _NT, preferred_element_type=F32) * scale
            p = _softmax_rows(s)
            pb = p.astype(BF16)
            dv_ref[:, sl] += lax.dot_general(pb, doh, _TN, preferred_element_type=F32)
            dp = lax.dot_general(doh, vh, _NT, preferred_element_type=F32)
            ds = (p * (dp - jnp.sum(dp * p, axis=-1, keepdims=True)) * scale).astype(BF16)
            dq_ref[:, sl] = lax.dot_general(ds, kh, _NN, preferred_element_type=F32).astype(BF16)
            dk_ref[:, sl] += lax.dot_general(ds, qh, _TN, preferred_element_type=F32)

    return pl.pallas_call(
        body, out_shape=[SDS((S, D), BF16), SDS((N_MEM, D), F32), SDS((N_MEM, D), F32)], grid=(S // tm,),
        in_specs=[_row(tm, D), _full((N_MEM, D)), _full((N_MEM, D)), _row(tm, D)],
        out_specs=[_row(tm, D), _full((N_MEM, D)), _full((N_MEM, D))], name=name, compiler_params=_cp(1),
    )(q, k, v, do)


def _sigmoid(x):
    return 1.0 / (1.0 + jnp.exp(-x))


def _ln_silu_fwd(cv, g, b):
    S = cv.shape[0]
    tm = min(S, TM_ROW)

    def body(x_ref, g_ref, b_ref, o_ref):
        xf = x_ref[...]
        mu = jnp.mean(xf, axis=-1, keepdims=True)
        xc = xf - mu
        rstd = lax.rsqrt(jnp.mean(xc * xc, axis=-1, keepdims=True) + EPS)
        ln = (xc * rstd) * g_ref[...] + b_ref[...]
        o_ref[...] = (ln * _sigmoid(ln)).astype(BF16)

    return pl.pallas_call(body, out_shape=SDS((S, D), BF16), grid=(S // tm,),
                          in_specs=[_row(tm, D), _full((1, D)), _full((1, D))], out_specs=_row(tm, D),
                          name="ln_silu_fwd", compiler_params=_cp(1))(cv, g, b)


def _ln_silu_bwd(ds, cv, g, b, dx):
    S = cv.shape[0]
    tm = min(S, TM_ROW)

    def body(ds_ref, x_ref, g_ref, b_ref, dx_ref, dcv_ref, dg_ref, db_ref, db2_ref):
        i = pl.program_id(0)

        @pl.when(i == 0)
        def _():
            dg_ref[...] = jnp.zeros_like(dg_ref)
            db_ref[...] = jnp.zeros_like(db_ref)
            db2_ref[...] = jnp.zeros_like(db2_ref)

        xf = x_ref[...]
        mu = jnp.mean(xf, axis=-1, keepdims=True)
        xc = xf - mu
        rstd = lax.rsqrt(jnp.mean(xc * xc, axis=-1, keepdims=True) + EPS)
        xhat = xc * rstd
        gv = g_ref[...]
        ln = xhat * gv + b_ref[...]
        sg = _sigmoid(ln)
        dln = ds_ref[...].astype(F32) * (sg + ln * sg * (1.0 - sg))
        dg_ref[...] += jnp.sum(dln * xhat, axis=0, keepdims=True)
        db_ref[...] += jnp.sum(dln, axis=0, keepdims=True)
        db2_ref[...] += jnp.sum(dx_ref[...], axis=0, keepdims=True)
        dxh = dln * gv
        dcv_ref[...] = rstd * (dxh - jnp.mean(dxh, axis=-1, keepdims=True)
                               - xhat * jnp.mean(dxh * xhat, axis=-1, keepdims=True))

    return pl.pallas_call(
        body, out_shape=[SDS((S, D), F32), SDS((1, D), F32), SDS((1, D), F32), SDS((1, D), F32)], grid=(S // tm,),
        in_specs=[_row(tm, D), _row(tm, D), _full((1, D)), _full((1, D)), _row(tm, D)],
        out_specs=[_row(tm, D), _full((1, D)), _full((1, D)), _full((1, D))], name="ln_silu_bwd",
        compiler_params=_cp(1),
    )(ds, cv, g, b, dx)


_GELU_C, _GELU_K = 0.7978845608028654, 0.044715


def _gelu(x, with_grad=False):
    x2 = x * x
    t = jnp.tanh(_GELU_C * (x + _GELU_K * x * x2))
    gel = 0.5 * x * (1.0 + t)
    if not with_grad:
        return gel
    return gel, 0.5 * (1.0 + t) + 0.5 * x * (1.0 - t * t) * (_GELU_C * (1.0 + 3.0 * _GELU_K * x2))


def _expm1(x):
    poly = x * (1.0 + x * (0.5 + x * (1.0 / 6.0 + x * (1.0 / 24.0 + x * (1.0 / 120.0)))))
    return jnp.where(jnp.abs(x) < 0.05, poly, jnp.exp(x) - 1.0)


def _softplus(x):
    return jnp.maximum(x, 0.0) + jnp.log1p(jnp.exp(-jnp.abs(x)))


_SCAN_UNROLL = 8
_RB = 32
_HB = 16


def _sub_blocks(n_rows, n_lanes, fn):
    def step(idx, c):
        r0 = pl.multiple_of(idx * _RB, _RB)
        for lt in range(n_lanes // LANE):
            fn(r0, lt)
        return c

    lax.fori_loop(0, n_rows // _RB, step, 0)


def _lanes(lt):
    return pl.ds(lt * LANE, LANE)


def _psum8(x):
    parts = [x[i * SUB:(i + 1) * SUB] for i in range(x.shape[0] // SUB)]
    return functools.reduce(lambda p, q: p + q, parts)


def _scan_fwd(a_s, b_s, out_ref, carry_ref, n_groups):
    row = lax.broadcasted_iota(jnp.int32, (SUB, LANE), 0)
    U = _SCAN_UNROLL

    def step(gi, carry):
        base = gi * (SUB * U)
        parts = []
        for u in range(U):
            i = pl.multiple_of(base + u * SUB, SUB)
            a8, b8 = a_s[pl.ds(i, SUB), :], b_s[pl.ds(i, SUB), :]
            for s in (1, 2, 4):
                a_sh = jnp.where(row >= s, pltpu.roll(a8, s, 0), 1.0)
                b_sh = jnp.where(row >= s, pltpu.roll(b8, s, 0), 0.0)
                b8 = a8 * b_sh + b8
                a8 = a8 * a_sh
            parts.append((i, a8, b8))
        for i, a8, b8 in parts:
            h8 = a8 * carry + b8
            out_ref[pl.ds(i, SUB), :] = h8
            carry = jnp.broadcast_to(h8[SUB - 1:SUB, :], (SUB, LANE))
        return carry

    carry_ref[...] = lax.fori_loop(0, n_groups // U, step, carry_ref[...])


def _scan_bwd(a_s, b_s, out_ref, carry_ref, n_groups):
    row = lax.broadcasted_iota(jnp.int32, (SUB, LANE), 0)
    U = _SCAN_UNROLL

    def step(gi, carry):
        base = (n_groups // U - 1 - gi) * (SUB * U)
        parts = []
        for u in reversed(range(U)):
            i = pl.multiple_of(base + u * SUB, SUB)
            a8, b8 = a_s[pl.ds(i, SUB), :], b_s[pl.ds(i, SUB), :]
            for s in (1, 2, 4):
                a_sh = jnp.where(row < SUB - s, pltpu.roll(a8, SUB - s, 0), 1.0)
                b_sh = jnp.where(row < SUB - s, pltpu.roll(b8, SUB - s, 0), 0.0)
                b8 = a8 * b_sh + b8
                a8 = a8 * a_sh
            parts.append((i, a8, b8))
        for i, a8, b8 in parts:
            h8 = a8 * carry + b8
            out_ref[pl.ds(i, SUB), :] = h8
            carry = jnp.broadcast_to(h8[0:1, :], (SUB, LANE))
        return carry

    carry_ref[...] = lax.fori_loop(0, n_groups // U, step, carry_ref[...])


def _rglru_pre(xr, wgx_ref, bgx_ref, wga_ref, bga_ref, lam_ref):
    xrb = xr.astype(BF16)
    wgx, wga = wgx_ref[0].astype(BF16), wga_ref[0].astype(BF16)
    gx = _sigmoid(lax.dot_general(xrb, wgx, _NN, preferred_element_type=F32) + bgx_ref[...])
    ga = _sigmoid(lax.dot_general(xrb, wga, _NN, preferred_element_type=F32) + bga_ref[...])
    sp = _softplus(-lam_ref[...])
    log_a = -C_RG * ga * sp
    a = jnp.exp(log_a)
    mult = jnp.sqrt(-_expm1(2.0 * log_a))
    return gx, ga, sp, a, mult, xrb, wgx, wga


def _a_specs():
    vec = pl.BlockSpec((1, HD_A), lambda c, j: (0, c))
    mat = pl.BlockSpec((1, HD_A, HD_A), lambda c, j: (c, 0, 0))
    return [pl.BlockSpec((CONV_A, HD_A), lambda c, j: (0, c)), vec, mat, vec, mat, vec, vec]


def _a_fwd(zp, conv_w, conv_b, wgx, bgx, wga, bga, lam):
    S = zp.shape[0]
    R, nt = R_RGLRU, D // HD_A
    H = SUB

    def body(zg_ref, zr_ref, cw_ref, cb_ref, wgx_ref, bgx_ref, wga_ref, bga_ref, lam_ref, ya_ref, h_ref,
             ext, a_s, b_s, hc):
        j = pl.program_id(1)

        @pl.when(j == 0)
        def _():
            ext[0:H, :] = jnp.zeros((H, HD_A), F32)
            hc[...] = jnp.zeros_like(hc)

        ext[H:H + R, :] = zr_ref[...].astype(F32)
        xr = cb_ref[...]
        for k in range(CONV_A):
            xr = xr + cw_ref[k:k + 1, :] * ext[pl.ds(H - (CONV_A - 1 - k), R), :]
        gx, _, _, a, mult, _, _, _ = _rglru_pre(xr, wgx_ref, bgx_ref, wga_ref, bga_ref, lam_ref)
        a_s[...] = a
        b_s[...] = mult * (gx * xr)
        _scan_fwd(a_s, b_s, h_ref, hc, R // SUB)
        ya_ref[...] = (_gelu(zg_ref[...].astype(F32)) * h_ref[...]).astype(BF16)
        ext[0:H, :] = ext[R:R + H, :]

    return pl.pallas_call(
        body, out_shape=[SDS((S, D + D // 2), BF16), SDS((S, D), F32)], grid=(nt, S // R),
        in_specs=[pl.BlockSpec((R, HD_A), lambda c, j: (j, c)), pl.BlockSpec((R, HD_A), lambda c, j: (j, nt + c))]
        + _a_specs(),
        out_specs=[pl.BlockSpec((R, HD_A), lambda c, j: (j, c)), pl.BlockSpec((R, HD_A), lambda c, j: (j, c))],
        scratch_shapes=[pltpu.VMEM((H + R, HD_A), F32), pltpu.VMEM((R, HD_A), F32), pltpu.VMEM((R, HD_A), F32),
                        pltpu.VMEM((SUB, HD_A), F32)],
        name="rglru_fwd", compiler_params=_cp(2),
    )(zp, zp, conv_w, conv_b, wgx, bgx, wga, bga, lam)


def _a_bwd(dyab, zp, h, conv_w, conv_b, wgx, bgx, wga, bga, lam):
    S = zp.shape[0]
    R, nt, nch = R_RGLRU, D // HD_A, S // R_RGLRU
    H = SUB

    def rows(c, j):
        return (nch - 1 - j, c)

    def rows_rec(c, j):
        return (nch - 1 - j, nt + c)

    def halo(c, j):
        return (jnp.maximum((nch - 1 - j) * (R // H) - 1, 0), c)

    def halo_z(c, j):
        return (jnp.maximum((nch - 1 - j) * (R // _HB) - 1, 0), nt + c)

    def body(dy_ref, zg_ref, zr_ref, zh_ref, h_ref, hh_ref, cw_ref, cb_ref, wgx_ref, bgx_ref, wga_ref, bga_ref,
             lam_ref, dzg_ref, dzr_ref, dcw_ref, dcb_ref, dwgx_ref, dbgx_ref, dwga_ref, dbga_ref, dlam_ref,
             ext_z, ext_h, ext_mu, ext_d, a_s, b_s, muc):
        j = pl.program_id(1)
        first_chunk = (nch - 1 - j) == 0

        @pl.when(j == 0)
        def _():
            ext_mu[R:R + H, :] = jnp.zeros((H, HD_A), F32)
            ext_d[R:R + H, :] = jnp.zeros((H, HD_A), F32)
            muc[...] = jnp.zeros_like(muc)
            for r in (dcw_ref, dcb_ref, dwgx_ref, dbgx_ref, dwga_ref, dbga_ref, dlam_ref):
                r[...] = jnp.zeros_like(r)

        zg = zg_ref[...].astype(F32)
        ext_z[0:H, :] = jnp.where(first_chunk, 0.0, zh_ref[_HB - H:_HB, :].astype(F32))
        ext_z[H:H + R, :] = zr_ref[...].astype(F32)
        ext_h[0:H, :] = jnp.where(first_chunk, 0.0, hh_ref[...])
        ext_h[H:H + R, :] = h_ref[...]
        xr = cb_ref[...]
        for k in range(CONV_A):
            xr = xr + cw_ref[k:k + 1, :] * ext_z[pl.ds(H - (CONV_A - 1 - k), R), :]
        gx, ga, sp, a, mult, xrb, wgxb, wgab = _rglru_pre(xr, wgx_ref, bgx_ref, wga_ref, bga_ref, lam_ref)
        gel, dgel = _gelu(zg, with_grad=True)
        dy = dy_ref[...].astype(F32)
        dh = dy * gel
        dzg_ref[...] = (dy * h_ref[...] * dgel).astype(BF16)
        a_s[...] = a
        b_s[...] = a * dh
        _scan_bwd(a_s, b_s, ext_mu, muc, R // SUB)
        lam_t = dh + ext_mu[pl.ds(1, R), :]
        ext_mu[R:R + H, :] = ext_mu[0:H, :]
        da = lam_t * ext_h[pl.ds(H - 1, R), :]
        gxr = gx * xr
        dlog_a = da * a - (lam_t * gxr) * (a * a) / mult
        dgx = lam_t * mult * xr
        dxr = lam_t * mult * gx
        lam_v = lam_ref[...]
        dlam_ref[...] += jnp.sum(dlog_a * ga, axis=0, keepdims=True) * (C_RG * _sigmoid(-lam_v))
        dpa = (dlog_a * (-C_RG * sp)) * ga * (1.0 - ga)
        dpx = dgx * gx * (1.0 - gx)
        dbga_ref[...] += jnp.sum(dpa, axis=0, keepdims=True)
        dbgx_ref[...] += jnp.sum(dpx, axis=0, keepdims=True)
        dpab, dpxb = dpa.astype(BF16), dpx.astype(BF16)
        dwga_ref[0] += lax.dot_general(xrb, dpab, _TN, preferred_element_type=F32)
        dwgx_ref[0] += lax.dot_general(xrb, dpxb, _TN, preferred_element_type=F32)
        dxr = (dxr + lax.dot_general(dpab, wgab, _NT, preferred_element_type=F32)
               + lax.dot_general(dpxb, wgxb, _NT, preferred_element_type=F32))
        dcb_ref[...] += jnp.sum(dxr, axis=0, keepdims=True)
        ext_d[0:R, :] = dxr
        dzr = jnp.zeros((R, HD_A), F32)
        for k in range(CONV_A):
            sh = CONV_A - 1 - k
            dcw_ref[k:k + 1, :] += jnp.sum(dxr * ext_z[pl.ds(H - sh, R), :], axis=0, keepdims=True)
            dzr = dzr + cw_ref[k:k + 1, :] * ext_d[pl.ds(sh, R), :]
        dzr_ref[...] = dzr.astype(BF16)
        ext_d[R:R + H, :] = ext_d[0:H, :]

    vec_o = pl.BlockSpec((1, HD_A), lambda c, j: (0, c))
    mat_o = pl.BlockSpec((1, HD_A, HD_A), lambda c, j: (c, 0, 0))
    return pl.pallas_call(
        body,
        out_shape=[SDS((S, D), BF16), SDS((S, D), BF16), SDS((CONV_A, D), F32), SDS((1, D), F32),
                   SDS((nt, HD_A, HD_A), F32), SDS((1, D), F32), SDS((nt, HD_A, HD_A), F32), SDS((1, D), F32),
                   SDS((1, D), F32)],
        grid=(nt, nch),
        in_specs=[pl.BlockSpec((R, HD_A), rows), pl.BlockSpec((R, HD_A), rows), pl.BlockSpec((R, HD_A), rows_rec),
                  pl.BlockSpec((_HB, HD_A), halo_z), pl.BlockSpec((R, HD_A), rows),
                  pl.BlockSpec((H, HD_A), halo)] + _a_specs(),
        out_specs=[pl.BlockSpec((R, HD_A), rows), pl.BlockSpec((R, HD_A), rows),
                   pl.BlockSpec((CONV_A, HD_A), lambda c, j: (0, c)), vec_o, mat_o, vec_o, mat_o, vec_o, vec_o],
        scratch_shapes=[pltpu.VMEM((H + R, HD_A), F32), pltpu.VMEM((H + R, HD_A), F32), pltpu.VMEM((R + H, HD_A), F32),
                        pltpu.VMEM((R + H, HD_A), F32), pltpu.VMEM((R, HD_A), F32), pltpu.VMEM((R, HD_A), F32),
                        pltpu.VMEM((SUB, HD_A), F32)],
        name="rglru_bwd", compiler_params=_cp(2),
    )(dyab, zp, zp, zp, h, h, conv_w, conv_b, wgx, bgx, wga, bga, lam)


_POOL_H = 16
_POOL_T0 = 2 * D // HD_A
_POOL_Y0 = D // HD_A


def _window_sum(lv, n, lo, rows, g, ahead):
    base = 0 if ahead else SUB
    cur, win = lv[0], None
    for i, s in enumerate((1, 2, 4, 8)):
        val = cur[pl.ds(base, n), :] + cur[pl.ds(base + (s if ahead else -s), n), :]
        sel = val[lo:lo + rows]
        win = sel if win is None else jnp.where(g >= i, sel, win)
        if i < 3:
            lv[i + 1][pl.ds(base, n), :] = val
            cur = lv[i + 1]
    return win


def _pool_width(g):
    return jnp.where(g == 0, 2.0, jnp.where(g == 1, 4.0, jnp.where(g == 2, 8.0, 16.0)))


def _b_fwd(zp, yab, wg, bg, sc):
    S = zp.shape[0]
    R, H = min(S, R_POOL), _POOL_H

    def body(z_ref, wg_ref, bg_ref, sc_ref, yab_in, yb_ref, *lv):
        del yab_in
        g, j = pl.program_id(0), pl.program_id(1)

        @pl.when(j == 0)
        def _():
            for r in lv:
                r[0:SUB, :] = jnp.zeros((SUB, HD_A), F32)
            lv[0][SUB:SUB + H, :] = jnp.zeros((H, HD_A), F32)

        u = z_ref[...].astype(F32)
        lv[0][SUB + H:SUB + H + R, :] = u
        t1 = (j * R + 1 + lax.broadcasted_iota(jnp.int32, (R, HD_A), 0)).astype(F32)
        p = _window_sum(lv, H + R, H, R, g, False) / jnp.minimum(t1, _pool_width(g)) - u
        lin = lax.dot_general(p.astype(BF16), wg_ref[0].astype(BF16), _NN, preferred_element_type=F32) + bg_ref[...]
        yb_ref[...] = (lin * sc_ref[...]).astype(BF16)
        lv[0][SUB:SUB + H, :] = lv[0][SUB + R:SUB + R + H, :]

    vec = pl.BlockSpec((1, HD_A), lambda g, j: (0, g))
    return pl.pallas_call(
        body, out_shape=SDS(yab.shape, yab.dtype), grid=(len(POOL_WINDOWS), S // R),
        in_specs=[pl.BlockSpec((R, HD_A), lambda g, j: (j, _POOL_T0 + g)),
                  pl.BlockSpec((1, HD_A, HD_A), lambda g, j: (g, 0, 0)), vec, vec, pl.BlockSpec(memory_space=pl.ANY)],
        out_specs=pl.BlockSpec((R, HD_A), lambda g, j: (j, _POOL_Y0 + g)),
        scratch_shapes=[pltpu.VMEM((SUB + H + R, HD_A), F32)] * 4, input_output_aliases={4: 0},
        name="pool_fwd", compiler_params=_cp(2),
    )(zp, wg, bg, sc, yab)


def _b_bwd(dyab, zp, wg, bg, sc):
    S = zp.shape[0]
    R, H, ng = min(S, R_POOL), _POOL_H, len(POOL_WINDOWS)
    nch = S // R

    def body(dy_ref, z_ref, zh_ref, wg_ref, bg_ref, sc_ref, dz_ref, dwg_ref, dbg_ref, dsc_ref, *scratch):
        lu, lq = scratch[:4], scratch[4:]
        g, j = pl.program_id(0), pl.program_id(1)
        jj = nch - 1 - j

        @pl.when(j == 0)
        def _():
            for r in lu:
                r[0:SUB, :] = jnp.zeros((SUB, HD_A), F32)
            for r in lq:
                r[R + H:R + H + SUB, :] = jnp.zeros((SUB, HD_A), F32)
            lq[0][R:R + H, :] = jnp.zeros((H, HD_A), F32)
            for r in (dwg_ref, dbg_ref, dsc_ref):
                r[...] = jnp.zeros_like(r)

        u = z_ref[...].astype(F32)
        lu[0][SUB:SUB + H, :] = jnp.where(jj == 0, 0.0, zh_ref[...].astype(F32))
        lu[0][SUB + H:SUB + H + R, :] = u
        t1 = (jj * R + 1 + lax.broadcasted_iota(jnp.int32, (R, HD_A), 0)).astype(F32)
        cnt = jnp.minimum(t1, _pool_width(g))
        pb = (_window_sum(lu, H + R, H, R, g, False) / cnt - u).astype(BF16)
        wgb = wg_ref[0].astype(BF16)
        lin = lax.dot_general(pb, wgb, _NN, preferred_element_type=F32) + bg_ref[...]
        dy = dy_ref[...].astype(F32)
        dsc_ref[...] += jnp.sum(dy * lin, axis=0, keepdims=True)
        dlin = dy * sc_ref[...]
        dbg_ref[...] += jnp.sum(dlin, axis=0, keepdims=True)
        dlb = dlin.astype(BF16)
        dwg_ref[0] += lax.dot_general(pb, dlb, _TN, preferred_element_type=F32)
        dp = lax.dot_general(dlb, wgb, _NT, preferred_element_type=F32)
        lq[0][0:R, :] = dp / cnt
        dz_ref[...] = (_window_sum(lq, R + H, 0, R, g, True) - dp).astype(BF16)
        lq[0][R:R + H, :] = lq[0][0:H, :]

    vec = pl.BlockSpec((1, HD_A), lambda g, j: (0, g))
    mat = pl.BlockSpec((1, HD_A, HD_A), lambda g, j: (g, 0, 0))
    return pl.pallas_call(
        body, out_shape=[SDS((S, D // 2), BF16), SDS((ng, HD_A, HD_A), F32), SDS((1, D // 2), F32),
                         SDS((1, D // 2), F32)],
        grid=(ng, nch),
        in_specs=[pl.BlockSpec((R, HD_A), lambda g, j: (nch - 1 - j, _POOL_Y0 + g)),
                  pl.BlockSpec((R, HD_A), lambda g, j: (nch - 1 - j, _POOL_T0 + g)),
                  pl.BlockSpec((H, HD_A), lambda g, j: (jnp.maximum((nch - 1 - j) * (R // H) - 1, 0), _POOL_T0 + g)),
                  mat, vec, vec],
        out_specs=[pl.BlockSpec((R, HD_A), lambda g, j: (nch - 1 - j, g)), mat, vec, vec],
        scratch_shapes=[pltpu.VMEM((SUB + H + R, HD_A), F32)] * 8,
        name="pool_bwd", compiler_params=_cp(2),
    )(dyab, zp, zp, wg, bg, sc)


_CW_F = 768


def _f_fwd(hp, w, b, name):
    S = hp.shape[0]
    R, H, cw = min(S, R_FFN), SUB, _CW_F
    nlt = cw // LANE

    def body(h_ref, w_ref, b_ref, o_ref, gel_ref, ud_ref, ext):
        j = pl.program_id(1)

        @pl.when(j == 0)
        def _():
            ext[:, 0:H, :] = jnp.zeros((nlt, H, LANE), F32)

        def stage(r0, lt):
            ext[lt, pl.ds(pl.multiple_of(r0 + H, SUB), _RB), :] = h_ref[pl.ds(r0, _RB), _lanes(lt)].astype(F32)

        def main(r0, lt):
            ls = _lanes(lt)
            gp = b_ref[:, ls]
            for k in range(CONV_F):
                gp = gp + w_ref[k:k + 1, ls] * ext[lt, pl.ds(r0 + (H - (CONV_F - 1 - k)), _RB), :]
            up = h_ref[pl.ds(r0, _RB), _lanes(lt + nlt)].astype(F32)
            gel, dgel = _gelu(gp, with_grad=True)
            rs = pl.ds(r0, _RB)
            o_ref[rs, ls] = (gel * up).astype(BF16)
            gel_ref[rs, ls] = gel.astype(BF16)
            ud_ref[rs, ls] = (up * dgel).astype(BF16)

        _sub_blocks(R, cw, stage)
        _sub_blocks(R, cw, main)
        ext[:, 0:H, :] = ext[:, R:R + H, :]

    tile = pl.BlockSpec((R, cw), lambda c, j: (j, c))
    return pl.pallas_call(
        body, out_shape=[SDS((S, D_FF), BF16)] * 3, grid=(D_FF // cw, S // R),
        in_specs=[pl.BlockSpec((R, 2 * cw), lambda c, j: (j, c)), pl.BlockSpec((CONV_F, cw), lambda c, j: (0, c)),
                  pl.BlockSpec((1, cw), lambda c, j: (0, c))],
        out_specs=[tile] * 3,
        scratch_shapes=[pltpu.VMEM((nlt, H + R, LANE), F32)], name=name, compiler_params=_cp(2),
    )(hp, w, b)


def _f_bwd(dact, hp, gel, ud, w, name):
    S = hp.shape[0]
    R, H, cw = min(S, R_FFN), SUB, _CW_F
    nch = S // R
    nlt = cw // LANE

    def body(da_ref, h_ref, hh_ref, gel_ref, ud_ref, w_ref, dh_ref, dw_ref, db_ref, ext_g, ext_d, acc):
        j = pl.program_id(1)
        jj = nch - 1 - j

        @pl.when(j == 0)
        def _():
            ext_d[:, R:R + H, :] = jnp.zeros((nlt, H, LANE), F32)
            acc[...] = jnp.zeros_like(acc)

        for lt in range(nlt):
            ext_g[lt, 0:H, :] = jnp.where(jj == 0, 0.0, hh_ref[_HB - H:_HB, lt * LANE:(lt + 1) * LANE].astype(F32))

        def stage(r0, lt):
            ext_g[lt, pl.ds(pl.multiple_of(r0 + H, SUB), _RB), :] = h_ref[pl.ds(r0, _RB), _lanes(lt)].astype(F32)

        def first(r0, lt):
            ls, lu, rs = _lanes(lt), _lanes(lt + nlt), pl.ds(r0, _RB)
            da = da_ref[rs, ls].astype(F32)
            dh_ref[rs, lu] = (da * gel_ref[rs, ls].astype(F32)).astype(BF16)
            dgp = da * ud_ref[rs, ls].astype(F32)
            ext_d[lt, rs, :] = dgp
            acc[CONV_F * SUB:(CONV_F + 1) * SUB, ls] += _psum8(dgp)
            for k in range(CONV_F):
                tap = ext_g[lt, pl.ds(r0 + (H - (CONV_F - 1 - k)), _RB), :]
                acc[k * SUB:(k + 1) * SUB, ls] += _psum8(dgp * tap)

        def second(r0, lt):
            ls = _lanes(lt)
            dhg = w_ref[CONV_F - 1:CONV_F, ls] * ext_d[lt, pl.ds(r0, _RB), :]
            for k in range(CONV_F - 1):
                dhg = dhg + w_ref[k:k + 1, ls] * ext_d[lt, pl.ds(r0 + (CONV_F - 1 - k), _RB), :]
            dh_ref[pl.ds(r0, _RB), ls] = dhg.astype(BF16)

        _sub_blocks(R, cw, stage)
        _sub_blocks(R, cw, first)
        _sub_blocks(R, cw, second)
        ext_d[:, R:R + H, :] = ext_d[:, 0:H, :]

        @pl.when(j == nch - 1)
        def _():
            for k in range(CONV_F):
                dw_ref[k:k + 1, :] = jnp.sum(acc[k * SUB:(k + 1) * SUB, :], axis=0, keepdims=True)
            db_ref[...] = jnp.sum(acc[CONV_F * SUB:(CONV_F + 1) * SUB, :], axis=0, keepdims=True)

    rows = lambda c, j: (nch - 1 - j, c)
    return pl.pallas_call(
        body, out_shape=[SDS((S, 2 * D_FF), BF16), SDS((CONV_F, D_FF), F32), SDS((1, D_FF), F32)],
        grid=(D_FF // cw, nch),
        in_specs=[pl.BlockSpec((R, cw), rows), pl.BlockSpec((R, cw), lambda c, j: (nch - 1 - j, 2 * c)),
                  pl.BlockSpec((_HB, cw), lambda c, j: (jnp.maximum((nch - 1 - j) * (R // _HB) - 1, 0), 2 * c)),
                  pl.BlockSpec((R, cw), rows), pl.BlockSpec((R, cw), rows),
                  pl.BlockSpec((CONV_F, cw), lambda c, j: (0, c))],
        out_specs=[pl.BlockSpec((R, 2 * cw), rows), pl.BlockSpec((CONV_F, cw), lambda c, j: (0, c)),
                   pl.BlockSpec((1, cw), lambda c, j: (0, c))],
        scratch_shapes=[pltpu.VMEM((nlt, H + R, LANE), F32), pltpu.VMEM((nlt, R + H, LANE), F32),
                        pltpu.VMEM(((CONV_F + 1) * SUB, cw), F32)], name=name,
        compiler_params=_cp(2),
    )(dact, hp, hp, gel, ud, w)


_CW_C = 256
_H_C = 32


def _c_fwd(h1p, w, b):
    S = h1p.shape[0]
    R, H, cw = R_SEQ, _H_C, _CW_C
    nlt = cw // LANE

    def body(h_ref, w_ref, b_ref, o_ref, ext):
        j = pl.program_id(1)

        @pl.when(j == 0)
        def _():
            ext[:, 0:H, :] = jnp.zeros((nlt, H, LANE), F32)

        def stage(r0, lt):
            rs = pl.ds(r0, _RB)
            gate = h_ref[rs, _lanes(lt + nlt)].astype(F32)
            ext[lt, pl.ds(pl.multiple_of(r0 + H, SUB), _RB), :] = h_ref[rs, _lanes(lt)].astype(F32) * _sigmoid(gate)

        def main(r0, lt):
            ls = _lanes(lt)
            cv = b_ref[:, ls]
            for k in range(CONV_C):
                cv = cv + w_ref[k:k + 1, ls] * ext[lt, pl.ds(r0 + (H - (CONV_C - 1 - k)), _RB), :]
            o_ref[pl.ds(r0, _RB), ls] = cv

        _sub_blocks(R, cw, stage)
        _sub_blocks(R, cw, main)
        ext[:, 0:H, :] = ext[:, R:R + H, :]

    return pl.pallas_call(
        body, out_shape=SDS((S, D), F32), grid=(D // cw, S // R),
        in_specs=[pl.BlockSpec((R, 2 * cw), lambda c, j: (j, c)), pl.BlockSpec((CONV_C, cw), lambda c, j: (0, c)),
                  pl.BlockSpec((1, cw), lambda c, j: (0, c))],
        out_specs=pl.BlockSpec((R, cw), lambda c, j: (j, c)),
        scratch_shapes=[pltpu.VMEM((nlt, H + R, LANE), F32)], name="conf_conv_fwd", compiler_params=_cp(2),
    )(h1p, w, b)


def _c_bwd(dcv, h1p, w):
    S = h1p.shape[0]
    R, H, cw, nch = R_SEQ, _H_C, _CW_C, S // R_SEQ
    nlt = cw // LANE
    a_b, a_val, a_gate = CONV_C * SUB, (CONV_C + 1) * SUB, (CONV_C + 2) * SUB

    def body(dc_ref, h_ref, hh_ref, w_ref, dh_ref, dw_ref, db_ref, db1_ref, ext_u, ext_d, acc):
        j = pl.program_id(1)
        jj = nch - 1 - j

        @pl.when(j == 0)
        def _():
            ext_d[:, R:R + H, :] = jnp.zeros((nlt, H, LANE), F32)
            acc[...] = jnp.zeros_like(acc)

        for lt in range(nlt):
            ext_u[lt, 0:H, :] = jnp.where(
                jj == 0, 0.0, hh_ref[:, lt * LANE:(lt + 1) * LANE].astype(F32)
                * _sigmoid(hh_ref[:, cw + lt * LANE:cw + (lt + 1) * LANE].astype(F32)))

        def stage(r0, lt):
            rs, ls = pl.ds(r0, _RB), _lanes(lt)
            gate = h_ref[rs, _lanes(lt + nlt)].astype(F32)
            ext_u[lt, pl.ds(pl.multiple_of(r0 + H, SUB), _RB), :] = h_ref[rs, ls].astype(F32) * _sigmoid(gate)
            ext_d[lt, rs, :] = dc_ref[rs, ls]

        def first(r0, lt):
            ls = _lanes(lt)
            dc = dc_ref[pl.ds(r0, _RB), ls]
            acc[a_b:a_b + SUB, ls] += _psum8(dc)
            for k in range(CONV_C):
                tap = ext_u[lt, pl.ds(r0 + (H - (CONV_C - 1 - k)), _RB), :]
                acc[k * SUB:(k + 1) * SUB, ls] += _psum8(dc * tap)

        def second(r0, lt):
            rs, ls, lg = pl.ds(r0, _RB), _lanes(lt), _lanes(lt + nlt)
            du = w_ref[CONV_C - 1:CONV_C, ls] * ext_d[lt, rs, :]
            for k in range(CONV_C - 1):
                du = du + w_ref[k:k + 1, ls] * ext_d[lt, pl.ds(r0 + (CONV_C - 1 - k), _RB), :]
            val = h_ref[rs, ls].astype(F32)
            sg = _sigmoid(h_ref[rs, lg].astype(F32))
            dval = du * sg
            dgate = du * val * sg * (1.0 - sg)
            acc[a_val:a_val + SUB, ls] += _psum8(dval)
            acc[a_gate:a_gate + SUB, ls] += _psum8(dgate)
            dh_ref[rs, ls] = dval.astype(BF16)
            dh_ref[rs, lg] = dgate.astype(BF16)

        _sub_blocks(R, cw, stage)
        _sub_blocks(R, cw, first)
        _sub_blocks(R, cw, second)
        ext_d[:, R:R + H, :] = ext_d[:, 0:H, :]

        @pl.when(j == nch - 1)
        def _():
            for k in range(CONV_C):
                dw_ref[k:k + 1, :] = jnp.sum(acc[k * SUB:(k + 1) * SUB, :], axis=0, keepdims=True)
            db_ref[...] = jnp.sum(acc[a_b:a_b + SUB, :], axis=0, keepdims=True)
            db1_ref[:, 0:cw] = jnp.sum(acc[a_val:a_val + SUB, :], axis=0, keepdims=True)
            db1_ref[:, cw:2 * cw] = jnp.sum(acc[a_gate:a_gate + SUB, :], axis=0, keepdims=True)

    rows = lambda c, j: (nch - 1 - j, c)
    return pl.pallas_call(
        body, out_shape=[SDS((S, 2 * D), BF16), SDS((CONV_C, D), F32), SDS((1, D), F32), SDS((1, 2 * D), F32)],
        grid=(D // cw, nch),
        in_specs=[pl.BlockSpec((R, cw), rows), pl.BlockSpec((R, 2 * cw), rows),
                  pl.BlockSpec((H, 2 * cw), lambda c, j: (jnp.maximum((nch - 1 - j) * (R // H) - 1, 0), c)),
                  pl.BlockSpec((CONV_C, cw), lambda c, j: (0, c))],
        out_specs=[pl.BlockSpec((R, 2 * cw), rows), pl.BlockSpec((CONV_C, cw), lambda c, j: (0, c)),
                   pl.BlockSpec((1, cw), lambda c, j: (0, c)), pl.BlockSpec((1, 2 * cw), lambda c, j: (0, c))],
        scratch_shapes=[pltpu.VMEM((nlt, H + R, LANE), F32), pltpu.VMEM((nlt, R + H, LANE), F32),
                        pltpu.VMEM(((CONV_C + 3) * SUB, cw), F32)], name="conf_conv_bwd",
        compiler_params=_cp(2),
    )(dcv, h1p, h1p, w)


def _local_step(x, mem, tgt, W, fetch=None, send=None):
    G = {}
    W = dict(W)

    def arrive(group, after):
        if fetch is None:
            return None
        got, tok = fetch(group, after)
        for key, val in got.items():
            W[key] = {**W.get(key, {}), **val} if isinstance(val, dict) else val
        return tok

    def gain(g, tok):
        return g if tok is None else g + tok

    def sent(group):
        return None if send is None else send(group, G)

    def xattn_fwd(xin, n, l):
        tok = arrive(("xa", l), n)
        mn = _rms_fwd(mem, gain(W["xa_mem_norm"][l:l + 1], tok), f"xa_memnorm_fwd{l}")
        q = _mm_nn(n, W["xa_wq"][l], out_dtype=BF16, name=f"xa_q{l}")
        k = _mm_nn(mn, W["xa_wk"][l], out_dtype=BF16, name=f"xa_k{l}")
        v = _mm_nn(mn, W["xa_wv"][l], out_dtype=BF16, name=f"xa_v{l}")
        o = _attn_fwd(q, k, v, f"xa_attn_fwd{l}")
        xout, nout = _mm_nn(o, W["xa_wo"][l], out_dtype=F32, name=f"xa_o{l}", add=xin, norm=W["f_norm"][l:l + 1])
        return xout, nout, (xin, n, q, mn, k, v, o)

    def xattn_bwd(dx, dxb, saved, l):
        xin, n, q, mn, k, v, o = saved
        do = _mm_nt(dxb, W["xa_wo"][l], out_dtype=BF16, name=f"xa_do{l}")
        G[f"xa_wo{l}"] = _mm_tn(o, dxb, out_dtype=BF16, name=f"xa_dwo{l}")
        dq, dk, dv = _attn_bwd(q, k, v, do, f"xa_attn_bwd{l}")
        dkb, dvb = dk.astype(BF16), dv.astype(BF16)
        G[f"xa_wq{l}"] = _mm_tn(n, dq, out_dtype=BF16, name=f"xa_dwq{l}")
        G[f"xa_wk{l}"] = _mm_tn(mn, dkb, out_dtype=BF16, name=f"xa_dwk{l}")
        G[f"xa_wv{l}"] = _mm_tn(mn, dvb, out_dtype=BF16, name=f"xa_dwv{l}")
        tok = sent(("xa", l))
        dmn = _mm_nt(dkb, W["xa_wk"][l], out_dtype=F32, name=f"xa_dmn_k{l}")
        dmn = _mm_nt(dvb, W["xa_wv"][l], out_dtype=F32, name=f"xa_dmn_v{l}", add=dmn)
        (G[f"xa_mem_norm{l}"],) = _rms_bwd(mem, W["xa_mem_norm"][l:l + 1], dmn, None, f"xa_memnorm_bwd{l}")
        dx, dxb, G[f"xa_norm{l}"] = _mm_nt(dq, W["xa_wq"][l], out_dtype=F32, name=f"xa_dn{l}",
                                           rms=(xin, gain(W["xa_norm"][l:l + 1], tok), dx))
        return dx, dxb

    def ffn_fwd(xin, n, l, next_gain):
        tok = arrive(("f", l), n)
        hp = _mm_nn(n, W["f_w_up"][l], out_dtype=BF16, name=f"f_up{l}")
        act, gel, ud = _f_fwd(hp, W["f_dw_w"][l], gain(W["f_dw_b"][l:l + 1], tok), f"f_conv_fwd{l}")
        arrive(("fd", l), act)
        res = _mm_nn(act, W["f_w_down"][l], out_dtype=F32, name=f"f_down{l}", add=xin, norm=next_gain)
        xout, nout = res if next_gain is not None else (res, None)
        return xout, nout, (xin, n, hp, act, gel, ud)

    def ffn_bwd(dx, dxb, saved, l):
        xin, n, hp, act, gel, ud = saved
        dact = _mm_nt(dxb, W["f_w_down"][l], out_dtype=BF16, name=f"f_dact{l}")
        G[f"f_w_down{l}"] = _mm_tn(act, dxb, out_dtype=BF16, name=f"f_dwdown{l}")
        dhp, G[f"f_dw_w{l}"], G[f"f_dw_b{l}"] = _f_bwd(dact, hp, gel, ud, W["f_dw_w"][l], f"f_conv_bwd{l}")
        G[f"f_w_up{l}"] = _mm_tn(n, dhp, out_dtype=BF16, name=f"f_dwup{l}", blocks=_CW_F)
        tok = sent(("f", l))
        dx, dxb, G[f"f_norm{l}"] = _mm_nt(dhp, W["f_w_up"][l], out_dtype=F32, name=f"f_dn{l}",
                                          rms=(xin, gain(W["f_norm"][l:l + 1], tok), dx))
        return dx, dxb

    n0 = _rms_fwd(x, W["ab_norm"], "ab_norm_fwd")
    tok = arrive(("ab", 0), n0)
    a_par = (W["a_conv_w"], gain(W["a_conv_b"], tok), W["a_gate_x_w"], W["a_gate_x_b"], W["a_gate_a_w"],
             W["a_gate_a_b"], W["a_lambda"])
    b_par = (W["b_group_w"], W["b_group_b"], W["b_scale"])
    zp = _mm_nn(n0, W["ab_w_in"], out_dtype=BF16, name="ab_in")
    yab, h_a = _a_fwd(zp, *a_par)
    yab = _b_fwd(zp, yab, *b_par)
    tok = arrive(("ab", 1), yab)
    x1, n1 = _mm_nn(yab, W["ab_w_out"], out_dtype=F32, name="ab_out", add=x, norm=gain(W["xa_norm"][0:1], tok))
    x2, n2, s_xa0 = xattn_fwd(x1, n1, 0)
    x3, n3, s_f0 = ffn_fwd(x2, n2, 0, W["c_norm"])
    tok = arrive(("c", 0), n3)
    h1p = _mm_nn(n3, W["c_w_pw1"], out_dtype=BF16, name="c_pw1", bias=gain(W["c_b_pw1"], tok))
    cv = _c_fwd(h1p, W["c_dw_w"], W["c_dw_b"])
    sc = _ln_silu_fwd(cv, W["c_ln_g"], W["c_ln_b"])
    x4, n4 = _mm_nn(sc, W["c_w_pw2"], out_dtype=F32, name="c_pw2", bias=W["c_b_pw2"], add=x3, norm=W["xa_norm"][1:2])
    x5, n5, s_xa1 = xattn_fwd(x4, n4, 1)
    x6, _, s_f1 = ffn_fwd(x5, n5, 1, None)
    loss, dx, dxb, G["final_norm"] = _loss_head(x6, W["final_norm"], tgt)

    dx, dxb = ffn_bwd(dx, dxb, s_f1, 1)
    dx, dxb = xattn_bwd(dx, dxb, s_xa1, 1)
    dsc = _mm_nt(dxb, W["c_w_pw2"], out_dtype=BF16, name="c_dsc")
    G["c_w_pw2"] = _mm_tn(sc, dxb, out_dtype=BF16, name="c_dwpw2")
    dcv, G["c_ln_g"], G["c_ln_b"], G["c_b_pw2"] = _ln_silu_bwd(dsc, cv, W["c_ln_g"], W["c_ln_b"], dx)
    dh1p, G["c_dw_w"], G["c_dw_b"], G["c_b_pw1"] = _c_bwd(dcv, h1p, W["c_dw_w"])
    G["c_w_pw1"] = _mm_tn(n3, dh1p, out_dtype=BF16, name="c_dwpw1", blocks=_CW_C)
    tok = sent(("c", 0))
    dx, dxb, G["c_norm"] = _mm_nt(dh1p, W["c_w_pw1"], out_dtype=F32, name="c_dn",
                                  rms=(x3, gain(W["c_norm"], tok), dx))
    dx, dxb = ffn_bwd(dx, dxb, s_f0, 0)
    dx, dxb = xattn_bwd(dx, dxb, s_xa0, 0)
    dyab = _mm_nt(dxb, W["ab_w_out"], out_dtype=BF16, name="ab_dyab")
    G["ab_w_out"] = _mm_tn(yab, dxb, out_dtype=BF16, name="ab_dwout")
    tok = sent(("ab", 1))
    a_par = (a_par[0], gain(a_par[1], tok)) + a_par[2:]
    (dzg, dzr, G["a_conv_w"], G["a_conv_b"], G["a_gate_x_w"], G["a_gate_x_b"], G["a_gate_a_w"], G["a_gate_a_b"],
     G["a_lambda"]) = _a_bwd(dyab, zp, h_a, *a_par)
    dzq, G["b_group_w"], G["b_group_b"], G["b_scale"] = _b_bwd(dyab, zp, *b_par)
    G["ab_w_in"] = jnp.concatenate(
        [_mm_tn(n0, dz, out_dtype=BF16, name=f"ab_dwin_{part}")
         for part, dz in (("gate", dzg), ("rec", dzr), ("pool", dzq))], axis=1)
    tok = sent(("ab", 0))
    dx, _, G["ab_norm"] = _mm_nt_cols([dzg, dzr, dzq], W["ab_w_in"], name="ab_dn",
                                      rms=(x, gain(W["ab_norm"], tok), dx))
    return loss, dx, G


def _my_place():
    x, y, c = lax.axis_index("x"), lax.axis_index("y"), lax.axis_index("c")
    return x, y, c


def _all_gather(shards, name):
    n = len(shards)

    def body(*refs):
        ins, outs = refs[:n], refs[n:2 * n]
        send_sems, recv_sems, local_sems = refs[2 * n:]
        x, y, c = _my_place()
        me, sibling = (x, y, c), (x, y, 1 - c)
        chips = [(1 - x, y), (x, 1 - y), (1 - x, 1 - y)]

        def slab(a, place):
            px, py, pc = place
            return outs[a].at[4 * px + 2 * py + pc]

        def copy(a, k, block, to, src=None):
            return pltpu.make_async_remote_copy(
                src_ref=slab(a, block) if src is None else src, dst_ref=slab(a, block),
                send_sem=send_sems.at[a, k], recv_sem=recv_sems.at[a, k], device_id=to, device_id_type=MESH)

        mine = [pltpu.make_async_copy(ins[a], slab(a, me), local_sems.at[a]) for a in range(n)]
        for cp in mine:
            cp.start()
        first = []
        for j, chip in enumerate(chips):
            first += [copy(a, 1 + j, me, (*chip, c), src=ins[a]) for a in range(n)]
        first += [copy(a, 0, me, sibling, src=ins[a]) for a in range(n)]
        for cp in first:
            cp.start()
        passed = []
        for j, chip in enumerate(chips):
            for a in range(n):
                copy(a, 1 + j, (*chip, c), me).wait_recv()
                cp = copy(a, 4 + j, (*chip, c), sibling)
                cp.start()
                passed.append(cp)
        for a in range(n):
            copy(a, 0, sibling, me).wait_recv()
        for j, chip in enumerate(chips):
            for a in range(n):
                copy(a, 4 + j, (*chip, 1 - c), me).wait_recv()
        for cp in first + passed:
            cp.wait_send()
        for cp in mine:
            cp.wait()

    any_spec = pl.BlockSpec(memory_space=pl.ANY)
    return pl.pallas_call(
        body, out_shape=[SDS((N_DEV,) + s.shape, s.dtype) for s in shards], in_specs=[any_spec] * n,
        out_specs=[any_spec] * n,
        scratch_shapes=[pltpu.SemaphoreType.DMA((n, 7)), pltpu.SemaphoreType.DMA((n, 7)), pltpu.SemaphoreType.DMA((n,))],
        name=name,
    )(*shards)


_HBM = pl.BlockSpec(memory_space=pltpu.HBM)
_SEM = pl.BlockSpec(memory_space=pltpu.SEMAPHORE)
_EFFECT = pltpu.SideEffectType.DATAFLOW_SIDE_EFFECTING


_COPY_PARTS = 4


def _peer_places():
    x, y, c = _my_place()
    peers = []
    for k in range(1, N_DEV):
        px = 1 - x if (k >> 2) & 1 else x
        py = 1 - y if (k >> 1) & 1 else y
        pc = 1 - c if k & 1 else c
        peers.append(((px, py, pc), 4 * px + 2 * py + pc))
    return (x, y, c), 4 * x + 2 * y + c, peers


def _send_start(srcs, per_dest, name):
    n = len(srcs)
    lands = [lax.empty((N_DEV,) + (s.shape[1:] if per_dest else s.shape), s.dtype) for s in srcs]

    def body(*refs):
        src, land = refs[:n], refs[n:2 * n]
        outs = refs[2 * n:]
        send, recv, token = outs[:n], outs[n:2 * n], outs[4 * n]
        place, me, peers = _peer_places()
        for a in range(n):
            rows = land[a].shape[1]
            parts = _COPY_PARTS if land[a].ndim == 3 and rows % (_COPY_PARTS * _RB) == 0 else 1
            for peer, pidx in peers + [(place, me)]:
                block = src[a].at[pidx] if per_dest else src[a]
                for p in range(parts):
                    part = pl.ds(p * (rows // parts), rows // parts)
                    pltpu.make_async_remote_copy(
                        src_ref=block.at[part] if parts > 1 else block,
                        dst_ref=land[a].at[me, part] if parts > 1 else land[a].at[me], send_sem=send[a],
                        recv_sem=recv[a], device_id=peer, device_id_type=MESH).start()
        token[...] = jnp.zeros_like(token)

    hbm = lambda a: pltpu.HBM(a.shape, a.dtype)
    sem = pltpu.SemaphoreType.DMA(())
    res = pl.pallas_call(
        body, name=name,
        out_shape=tuple([sem] * (2 * n) + [hbm(s) for s in srcs] + [hbm(l) for l in lands]
                        + [SDS((SUB, LANE), F32)]),
        in_specs=[_HBM] * (2 * n),
        out_specs=tuple([_SEM] * (2 * n) + [_HBM] * (2 * n) + [pl.BlockSpec(memory_space=pltpu.VMEM)]),
        input_output_aliases={i: 2 * n + i for i in range(2 * n)},
        compiler_params=pltpu.CompilerParams(has_side_effects=_EFFECT),
    )(*[pltpu.with_memory_space_constraint(s, pltpu.HBM) for s in srcs],
      *[pltpu.with_memory_space_constraint(l, pltpu.HBM) for l in lands])
    return res[:n], res[n:2 * n], res[2 * n:3 * n], res[3 * n:4 * n], res[4 * n]


def _send_wait(send, recv, srcs, lands, after, per_dest, name):
    n = len(srcs)

    def body(*refs):
        src, land = refs[:n], refs[n:2 * n]
        send_s, recv_s = refs[2 * n:3 * n], refs[3 * n:4 * n]
        token = refs[-1]
        place, _, _ = _peer_places()
        for a in range(n):
            copy = pltpu.make_async_remote_copy(
                src_ref=src[a] if per_dest else land[a], dst_ref=land[a], send_sem=send_s[a],
                recv_sem=recv_s[a], device_id=place, device_id_type=MESH)
            copy.wait_send()
            copy.wait_recv()
        token[...] = jnp.zeros_like(token)

    hbm = lambda a: pltpu.HBM(a.shape, a.dtype)
    res = pl.pallas_call(
        body, name=name,
        out_shape=tuple([hbm(s) for s in srcs] + [hbm(l) for l in lands] + [SDS((SUB, LANE), F32)]),
        in_specs=[_HBM] * (2 * n) + [_SEM] * (2 * n) + [pl.BlockSpec(memory_space=pl.ANY)],
        out_specs=tuple([_HBM] * (2 * n) + [pl.BlockSpec(memory_space=pltpu.VMEM)]),
        input_output_aliases={i: i for i in range(2 * n)},
        compiler_params=pltpu.CompilerParams(has_side_effects=_EFFECT),
    )(*srcs, *lands, *send, *recv, after)
    return res[:n], res[n:2 * n], res[2 * n]


def _adamw_math(w, g, m, v):
    m = ADAM_B1 * m + (1.0 - ADAM_B1) * g
    v = ADAM_B2 * v + (1.0 - ADAM_B2) * (g * g)
    m_hat = m / (1.0 - ADAM_B1 ** ADAM_STEP)
    v_hat = v / (1.0 - ADAM_B2 ** ADAM_STEP)
    delta = -ADAM_LR * (m_hat / (jnp.sqrt(v_hat) + ADAM_EPS) + ADAM_WD * w)
    return delta, m, v


def _row_tile(r, c, itemsize_rows):
    cap = max(SUB, (itemsize_rows // (4 * c)) // SUB * SUB)
    if r <= cap:
        return r
    best = None
    for t in range(SUB, cap + 1, SUB):
        if r % t == 0:
            best = t
    return best if best is not None else r


def _sum_adamw(landing, w, m, v, name, layer=0, prev=None, after=None):
    _, r, c = landing.shape
    tr = _row_tile(r, c, 2 << 20)
    off = layer * (r // tr)
    tail = ([] if prev is None else list(prev)) + ([] if after is None else [after])

    def body(l_ref, w_ref, m_ref, v_ref, *rest):
        g_ref, d_ref, mo_ref, vo_ref = rest[-4:]
        g = l_ref[0].astype(F32)
        for s in range(1, N_DEV):
            g = g + l_ref[s].astype(F32)
        g_ref[...] = g
        d_ref[...], mo_ref[...], vo_ref[...] = _adamw_math(w_ref[...], g, m_ref[...], v_ref[...])

    blk = pl.BlockSpec((tr, c), lambda i: (i + off, 0))
    n_prev = 0 if prev is None else 4
    return pl.pallas_call(
        body, out_shape=[SDS(w.shape, F32)] * 4, grid=(r // tr,),
        in_specs=[pl.BlockSpec((N_DEV, tr, c), lambda i: (0, i, 0)), blk, blk, blk]
        + [pl.BlockSpec(memory_space=pl.ANY)] * len(tail),
        out_specs=[blk] * 4, input_output_aliases={4 + i: i for i in range(n_prev)}, name=name,
        compiler_params=_cp(1),
    )(landing, w, m, v, *tail)


def _sum8(landing, name):
    _, r, c = landing.shape

    def body(l_ref, g_ref):
        g = l_ref[0]
        for s in range(1, N_DEV):
            g = g + l_ref[s]
        g_ref[...] = g

    return pl.pallas_call(body, out_shape=SDS((r, c), F32), name=name, compiler_params=_cp(0))(landing)


def _adamw_small(repl_pack, own_pack, P, M, V):
    table, off = [], 0
    for name, shape in _REPL.items():
        table.append((name, shape if len(shape) > 1 else (1,) + shape, 0, off // LANE))
        off += _size(shape)
    off = _REPL_ROWS * LANE
    for name, shape in _SMALL_SHARDED.items():
        table.append((name, shape, 1, off // LANE))
        off += _size(shape)
    n = len(table)

    def body(*refs):
        packs, ins, outs = refs[:2], refs[2:2 + 3 * n], refs[2 + 3 * n:]
        for p, (_, shape, which, r0) in enumerate(table):
            w_ref, m_ref, v_ref = ins[3 * p:3 * p + 3]
            g_ref, d_ref, mo_ref, vo_ref = outs[4 * p:4 * p + 4]
            pack, rows, q = packs[which], shape[-2], shape[-1] // LANE
            lead = [()]
            for dim in shape[:-2]:
                lead = [t + (i,) for t in lead for i in range(dim)]
            for li, idx in enumerate(lead):
                if q == 1:
                    dst = g_ref.at[idx] if idx else g_ref
                    dst[...] = pack[r0 + li * rows:r0 + (li + 1) * rows, :]
                    continue
                for i in range(rows):
                    for k in range(q):
                        row = r0 + (li * rows + i) * q + k
                        g_ref[idx + (slice(i, i + 1), slice(k * LANE, (k + 1) * LANE))] = pack[row:row + 1, :]
            d_ref[...], mo_ref[...], vo_ref[...] = _adamw_math(w_ref[...], g_ref[...], m_ref[...], v_ref[...])

    ins, out_shape = [], []
    for name, shape, _, _ in table:
        ins += [t[name].reshape(shape) for t in (P, M, V)]
        out_shape += [SDS(shape, F32)] * 4
    res = pl.pallas_call(body, out_shape=out_shape, name="adamw_small", compiler_params=_cp(0))(
        repl_pack, own_pack, *ins)
    dicts = ({}, {}, {}, {})
    for p, (name, shape, _, _) in enumerate(table):
        for d, arr in zip(dicts, res[4 * p:4 * p + 4]):
            d[name] = arr.reshape(P[name].shape)
    return dicts


_BIG = {
    "ab_w_in": (1, D, 320), "ab_w_out": (1, 192, D), "c_w_pw1": (1, D, 256), "c_w_pw2": (1, 128, D),
    "xa_wq": (2, 128, D), "xa_wk": (2, 128, D), "xa_wv": (2, 128, D), "xa_wo": (2, 128, D),
    "f_w_up": (2, D, 768), "f_w_down": (2, 384, D),
}
_SMALL_SHARDED = {
    "a_conv_w": (1, 4, 128), "c_norm": (1, 128), "c_b_pw1": (1, 256), "c_dw_w": (1, 31, 128), "c_dw_b": (1, 128),
    "c_ln_g": (1, 128), "c_ln_b": (1, 128), "c_b_pw2": (1, 128), "f_dw_w": (2, 3, 384),
}
_REPL = {
    "ab_norm": (1, D), "a_conv_b": (1, D), "a_gate_x_w": (1, 8, 128, 128), "a_gate_x_b": (1, D),
    "a_gate_a_w": (1, 8, 128, 128), "a_gate_a_b": (1, D), "a_lambda": (1, D), "b_group_w": (1, 4, 128, 128),
    "b_group_b": (1, 512), "b_scale": (1, 512), "xa_norm": (2, D), "xa_mem_norm": (2, D), "f_norm": (2, D),
    "f_dw_b": (2, D_FF), "final_norm": (D,),
}


def _size(shape):
    n = 1
    for s in shape:
        n *= s
    return n


_N_SS = sum(_size(s) for s in _SMALL_SHARDED.values())
_N_REPL = sum(_size(s) for s in _REPL.values())
_REPL_ROWS = -(-_N_REPL // (N_DEV * SUB * LANE)) * SUB
_SS_ROWS = _N_SS // LANE
_SMALL_ROWS = -(-(_REPL_ROWS + _SS_ROWS) // SUB) * SUB


def _pack(parts, rows):
    flat = jnp.concatenate([p.reshape(-1).astype(F32) for p in parts])
    return jnp.pad(flat, (0, rows * LANE - flat.shape[0])).reshape(rows, LANE)


def _pair_blocks(v, bw):
    lead, n = v.shape[:-1], v.shape[-1]
    return jnp.swapaxes(v.reshape(lead + (2, n // (2 * bw), bw)), -3, -2).reshape(lead + (n,))


def _unpair_blocks(v, bw):
    lead, n = v.shape[:-1], v.shape[-1]
    return jnp.swapaxes(v.reshape(lead + (n // (2 * bw), 2, bw)), -3, -2).reshape(lead + (n,))


_GROUPS = {
    ("ab", 0): (("ab_w_in", 0),),
    ("ab", 1): (("ab_w_out", 0),),
    ("xa", 0): (("xa_wq", 0), ("xa_wk", 0), ("xa_wv", 0), ("xa_wo", 0)),
    ("f", 0): (("f_w_up", 0),),
    ("fd", 0): (("f_w_down", 0),),
    ("c", 0): (("c_w_pw1", 0), ("c_w_pw2", 0)),
    ("xa", 1): (("xa_wq", 1), ("xa_wk", 1), ("xa_wv", 1), ("xa_wo", 1)),
    ("f", 1): (("f_w_up", 1),),
    ("fd", 1): (("f_w_down", 1),),
}
_SEND_GROUPS = {g: m for g, m in _GROUPS.items() if g[0] != "fd"}
_SEND_GROUPS[("f", 0)] = (("f_w_up", 0), ("f_w_down", 0))
_SEND_GROUPS[("f", 1)] = (("f_w_up", 1), ("f_w_down", 1))


def _weight_layout(name, g):
    if name == "ab_w_in":
        return jnp.swapaxes(g, 0, 1).reshape(D, N_DEV * 320)
    if name in ("c_w_pw1", "f_w_up"):
        return g
    return g.reshape(N_DEV * g.shape[1], D)


def _grad_blocks(name, l, G):
    _, r, c = _BIG[name]
    if name == "ab_w_in":
        return jnp.swapaxes(G[name].reshape(D, N_DEV, 320), 0, 1)
    if name == "c_w_pw1":
        return G[name]
    if name == "f_w_up":
        return G[f"{name}{l}"]
    return (G[name] if _BIG[name][0] == 1 else G[f"{name}{l}"]).reshape(N_DEV, r, c)


def _small_layouts(sm):
    W = {}
    sm = sm.reshape(N_DEV, -1)
    off = 0
    for name, shape in _SMALL_SHARDED.items():
        n = _size(shape)
        blocks = sm[:, off:off + n].reshape((N_DEV,) + shape)
        off += n
        W[name] = jnp.moveaxis(blocks, 0, -2).reshape(shape[:-1] + (N_DEV * shape[-1],))
    W["a_conv_w"], W["c_dw_w"] = W["a_conv_w"][0], W["c_dw_w"][0]
    W["c_b_pw1"] = _pair_blocks(W["c_b_pw1"], _CW_C)
    return W


def _to_dest_major(g, shape):
    full = g.reshape(shape[:-1] + (N_DEV, shape[-1]))
    return jnp.moveaxis(full, -2, 0).reshape(N_DEV, -1)


def kernel(x, mem, ab_norm, ab_w_in, a_conv_w, a_conv_b, a_gate_x_w, a_gate_x_b, a_gate_a_w, a_gate_a_b, a_lambda, b_group_w, b_group_b, b_scale, ab_w_out, c_norm, c_w_pw1, c_b_pw1, c_dw_w, c_dw_b, c_ln_g, c_ln_b, c_w_pw2, c_b_pw2, xa_norm, xa_mem_norm, xa_wq, xa_wk, xa_wv, xa_wo, f_norm, f_w_up, f_dw_w, f_dw_b, f_w_down, final_norm, loss_target, m_ab_norm, m_ab_w_in, m_a_conv_w, m_a_conv_b, m_a_gate_x_w, m_a_gate_x_b, m_a_gate_a_w, m_a_gate_a_b, m_a_lambda, m_b_group_w, m_b_group_b, m_b_scale, m_ab_w_out, m_c_norm, m_c_w_pw1, m_c_b_pw1, m_c_dw_w, m_c_dw_b, m_c_ln_g, m_c_ln_b, m_c_w_pw2, m_c_b_pw2, m_xa_norm, m_xa_mem_norm, m_xa_wq, m_xa_wk, m_xa_wv, m_xa_wo, m_f_norm, m_f_w_up, m_f_dw_w, m_f_dw_b, m_f_w_down, m_final_norm, v_ab_norm, v_ab_w_in, v_a_conv_w, v_a_conv_b, v_a_gate_x_w, v_a_gate_x_b, v_a_gate_a_w, v_a_gate_a_b, v_a_lambda, v_b_group_w, v_b_group_b, v_b_scale, v_ab_w_out, v_c_norm, v_c_w_pw1, v_c_b_pw1, v_c_dw_w, v_c_dw_b, v_c_ln_g, v_c_ln_b, v_c_w_pw2, v_c_b_pw2, v_xa_norm, v_xa_mem_norm, v_xa_wq, v_xa_wk, v_xa_wv, v_xa_wo, v_f_norm, v_f_w_up, v_f_dw_w, v_f_dw_b, v_f_w_down, v_final_norm):
    args = dict(locals())
    P = {n: args[n] for n in _NAMES}
    M = {n: args["m_" + n] for n in _NAMES}
    V = {n: args["v_" + n] for n in _NAMES}

    in_flight = {}

    def launch(groups, tok):
        shards, n_of = [], {}
        for grp in groups:
            for name, l in _GROUPS[grp]:
                w = P[name][l] if tok is None else P[name][l] + tok
                shards.append(w.astype(BF16))
            if grp == ("ab", 0):
                shards.append(_pack([P[n] for n in _SMALL_SHARDED], _SS_ROWS + 4))
            n_of[grp] = len(shards)
        res = _send_start(shards, False, "gather_start_" + "_".join(g[0] + str(g[1]) for g in groups))
        lo = 0
        for grp in groups:
            in_flight[grp] = [r[lo:n_of[grp]] for r in res[:4]]
            lo = n_of[grp]
        return res[4][:1, :1]

    follow = {("ab", 0): [("ab", 1), ("xa", 0), ("f", 0), ("fd", 0)], ("ab", 1): [("c", 0), ("xa", 1)],
              ("xa", 0): [("f", 1), ("fd", 1)]}

    def fetch(grp, after):
        send_s, recv_s, srcs, lands = in_flight.pop(grp)
        srcs, lands, tok = _send_wait(send_s, recv_s, srcs, lands, after, False, f"gather_wait_{grp[0]}{grp[1]}")
        tok = launch(follow[grp], tok[:1, :1]) if grp in follow else None
        full = lands
        out = {}
        for (name, l), g in zip(_GROUPS[grp], full):
            w = _weight_layout(name, g)
            if _BIG[name][0] == 1:
                out[name] = w
            else:
                out[name] = {l: w}
        if grp == ("ab", 0):
            out.update(_small_layouts(full[-1]))
        return out, tok

    zero = launch([("ab", 0)], None)

    pending, held = [], []
    rides_with_next = {("xa", 1), ("f", 0)}

    def send(grp, G):
        held.extend(_SEND_GROUPS[grp])
        if grp in rides_with_next:
            return None
        members = tuple(held)
        del held[:]
        res = _send_start([_grad_blocks(name, l, G) for name, l in members], True, f"send_{grp[0]}{grp[1]}")
        pending.append((members, res))
        return res[4][:1, :1]

    W = {n: P[n] for n in _REPL}
    W["ab_norm"] = P["ab_norm"] + zero
    W["final_norm"] = P["final_norm"].reshape(1, D)
    W["a_gate_x_w"], W["a_gate_a_w"], W["b_group_w"] = P["a_gate_x_w"][0], P["a_gate_a_w"][0], P["b_group_w"][0]
    loss, grad_x, G = _local_step(x[0], mem[0], loss_target[0], W, fetch, send)
    loss = lax.psum(loss[0, 0], ("x", "y", "c"))

    Gs = dict(G)
    Gs["c_b_pw1"] = _unpair_blocks(G["c_b_pw1"], _CW_C)
    Gs["f_dw_w"] = jnp.stack([G["f_dw_w0"], G["f_dw_w1"]])
    Gs["a_conv_w"], Gs["c_dw_w"] = G["a_conv_w"][None], G["c_dw_w"][None]
    for n in ("xa_norm", "xa_mem_norm", "f_norm", "f_dw_b"):
        Gs[n] = jnp.concatenate([G[f"{n}0"], G[f"{n}1"]], axis=0)
    for n in ("a_gate_x_w", "a_gate_a_w", "b_group_w"):
        Gs[n] = G[n][None]
    repl_flat = jnp.concatenate([Gs[n].reshape(-1) for n in _REPL])
    repl_rows = jnp.pad(repl_flat, (0, N_DEV * _REPL_ROWS * LANE - _N_REPL)).reshape(N_DEV, _REPL_ROWS, LANE)
    ss_rows = jnp.concatenate([_to_dest_major(Gs[n], s) for n, s in _SMALL_SHARDED.items()], axis=1)
    ss_rows = ss_rows.reshape(N_DEV, _SS_ROWS, LANE)
    small_pack = jnp.concatenate(
        [repl_rows, ss_rows, jnp.zeros((N_DEV, _SMALL_ROWS - _REPL_ROWS - _SS_ROWS, LANE), F32)], axis=1)
    last = _send_start([small_pack], True, "send_small")
    pending.append(((("small", 0),), last))

    def arrived(some, after, name):
        members = [m for mem_, _ in some for m in mem_]
        cat = [[a for _, res in some for a in res[i]] for i in range(4)]
        srcs, lands, _ = _send_wait(cat[0], cat[1], cat[2], cat[3], after, True, name)
        return dict(zip(members, lands))

    out_g, out_d, out_m, out_v = {}, {}, {}, {}
    chain = [None]

    def update(name, landed):
        layers, r, c = _BIG[name]
        w2, m2, v2 = [t[name].reshape(layers * r, c) for t in (P, M, V)]
        res = None
        for l in range(layers):
            res = _sum_adamw(landed[(name, l)], w2, m2, v2, f"adamw_{name}{l}", layer=l, prev=res,
                             after=chain[0] if l == 0 else None)
        chain[0] = res[1]
        out_g[name], out_d[name], out_m[name], out_v[name] = [t.reshape(P[name].shape) for t in res]

    landed = arrived(pending[:-2], grad_x, "send_wait_early")
    for name in _BIG:
        if name != "ab_w_in":
            update(name, landed)
    landed = arrived(pending[-2:], out_v["f_w_down"], "send_wait_late")
    update("ab_w_in", landed)

    small_sum = _sum8(landed[("small", 0)], "sum_small")
    (repl_all,) = _all_gather([small_sum[:_REPL_ROWS]], "gather_small_grads")
    for out, got in zip((out_g, out_d, out_m, out_v),
                        _adamw_small(repl_all.reshape(N_DEV * _REPL_ROWS, LANE), small_sum, P, M, V)):
        out.update(got)

    return (loss, grad_x[None], *[out_g[n] for n in _NAMES], *[out_d[n] for n in _NAMES],
            *[out_m[n] for n in _NAMES], *[out_v[n] for n in _NAMES])


_NAMES = ("ab_norm", "ab_w_in", "a_conv_w", "a_conv_b", "a_gate_x_w", "a_gate_x_b", "a_gate_a_w", "a_gate_a_b",
          "a_lambda", "b_group_w", "b_group_b", "b_scale", "ab_w_out", "c_norm", "c_w_pw1", "c_b_pw1", "c_dw_w",
          "c_dw_b", "c_ln_g", "c_ln_b", "c_w_pw2", "c_b_pw2", "xa_norm", "xa_mem_norm", "xa_wq", "xa_wk", "xa_wv",
          "xa_wo", "f_norm", "f_w_up", "f_dw_w", "f_dw_b", "f_w_down", "final_norm")
```

```python
import functools

import jax
import jax.numpy as jnp
from jax import lax
from jax.experimental import pallas as pl
from jax.experimental.pallas import tpu as pltpu

F32, BF16 = jnp.float32, jnp.bfloat16
SDS = jax.ShapeDtypeStruct
MESH = pl.DeviceIdType.MESH

N_DEV = 8
D = 1024
N_MEM = 256
XA_HEADS, XA_HD = 4, 256
HD_A = 128
CONV_A, CONV_C, CONV_F = 4, 31, 3
C_RG = 8.0
POOL_WINDOWS = (2, 4, 8, 16)
D_FF = 3 * D
EPS = 1e-6
ADAM_LR, ADAM_B1, ADAM_B2, ADAM_EPS, ADAM_WD, ADAM_STEP = 0.001, 0.9, 0.999, 1e-08, 0.01, 10

LANE = 128
SUB = 8
VMEM_LIMIT = 56 * 1024 * 1024
R_SEQ = 1024
R_POOL = 2048
R_RGLRU = 2048
R_FFN = 2048
TM_ROW = 1024


def _cp(n_axes):
    return pltpu.CompilerParams(dimension_semantics=("arbitrary",) * n_axes, vmem_limit_bytes=VMEM_LIMIT)


def _tile(n, pref):
    if n <= pref:
        return n
    best = None
    for t in range(LANE, pref + 1, LANE):
        if n % t == 0:
            best = t
    assert best is not None, (n, pref)
    return best


def _perm2(n):
    return (n % 2) * 4 + n // 2


_NN = (((1,), (0,)), ((), ()))
_NT = (((1,), (1,)), ((), ()))
_TN = (((0,), (0,)), ((), ()))


def _mm_call(name, grid, ab, ab_specs, dims, acc_shape, extras, outs, finish, from_ref=False):
    nk = grid[2]
    n_ab, n_ex, n_out = len(ab), len(extras), len(outs)
    use_acc = nk > 1 or from_ref

    def product(refs):
        r = lax.dot_general(refs[0][...], refs[1][...], dims, preferred_element_type=F32)
        for i in range(1, n_ab):
            r = r + lax.dot_general(refs[2 * i][...], refs[2 * i + 1][...], dims, preferred_element_type=F32)
        return r

    def body(*refs):
        rest = refs[2 * n_ab:]
        ex_refs, o_refs = rest[:n_ex], rest[n_ex:n_ex + n_out]
        first_rows = pl.program_id(0) == 0
        if not use_acc:
            finish(product(refs), ex_refs, o_refs, first_rows)
            return
        acc = rest[n_ex + n_out]
        if nk == 1:
            acc[...] = product(refs)
            finish(acc, ex_refs, o_refs, first_rows)
            return
        k = pl.program_id(2)

        @pl.when(k == 0)
        def _():
            acc[...] = jnp.zeros_like(acc)

        acc[...] += product(refs)

        @pl.when(k == nk - 1)
        def _():
            finish(acc if from_ref else acc[...], ex_refs, o_refs, first_rows)

    res = pl.pallas_call(
        body, out_shape=[o for o, _ in outs], grid=grid,
        in_specs=list(ab_specs) + [s for _, s in extras], out_specs=[s for _, s in outs],
        scratch_shapes=[pltpu.VMEM(acc_shape, F32)] if use_acc else [], name=name, compiler_params=_cp(3),
    )(*[t for pair in ab for t in pair], *[e for e, _ in extras])
    return res[0] if n_out == 1 else res


def _finish_sum(r, ex_refs, o_refs, first_rows):
    del first_rows
    for e in ex_refs:
        r = r + e[...]
    o_refs[0][...] = r.astype(o_refs[0].dtype)


def _finish_sum_norm(r, ex_refs, o_refs, first_rows):
    del first_rows
    for e in ex_refs[:-1]:
        r = r + e[...]
    o_refs[0][...] = r
    o_refs[1][...] = ((r * lax.rsqrt(jnp.mean(r * r, axis=-1, keepdims=True) + EPS)) * ex_refs[-1][...]).astype(BF16)


_EPI_ROWS = 16


def _finish_rms_bwd(r_ref, ex_refs, o_refs, first_rows):
    x_ref, g_ref, dres_ref = ex_refs
    dx_ref, dxb_ref, dg_ref = o_refs

    @pl.when(first_rows)
    def _():
        dg_ref[...] = jnp.zeros_like(dg_ref)

    gv = g_ref[...]
    inv_d = 1.0 / r_ref.shape[1]

    def step(i, dg_acc):
        groups = [pl.ds(pl.multiple_of(i * (2 * _EPI_ROWS) + u * _EPI_ROWS, _EPI_ROWS), _EPI_ROWS) for u in range(2)]
        sums = []
        for rows in groups:
            r, xf = r_ref[rows, :], x_ref[rows, :]
            sums.append((jnp.sum(xf * xf, axis=-1, keepdims=True), jnp.sum((r * gv) * xf, axis=-1, keepdims=True)))
        for rows, (sxx, sax) in zip(groups, sums):
            r, xf = r_ref[rows, :], x_ref[rows, :]
            rs = lax.rsqrt(sxx * inv_d + EPS)
            dg_acc = dg_acc + _psum8(r * (xf * rs))
            dx = rs * (r * gv) - xf * (rs * rs * (sax * rs * inv_d)) + dres_ref[rows, :]
            dx_ref[rows, :] = dx
            dxb_ref[rows, :] = dx.astype(BF16)
        return dg_acc

    dg_acc = lax.fori_loop(0, r_ref.shape[0] // (2 * _EPI_ROWS), step, jnp.zeros((SUB, r_ref.shape[1]), F32))
    dg_ref[...] += jnp.sum(dg_acc, axis=0, keepdims=True)


def _rms_bwd_io(M, tm, x, g, dres):
    rows = pl.BlockSpec((tm, D), lambda m, n, k: (m, 0))
    vec = pl.BlockSpec((1, D), lambda m, n, k: (0, 0))
    return ([(x, rows), (g, vec), (dres, rows)],
            [(SDS((M, D), F32), rows), (SDS((M, D), BF16), rows), (SDS((1, D), F32), vec)])


_K_WHOLE = 3072


def _mm_nn(a, b, *, out_dtype, name, bias=None, add=None, norm=None):
    M, K = a.shape
    tk = K if K <= _K_WHOLE else _tile(K, 1024)
    if K <= 1024 and norm is None:
        tm = _tile(M, 2048 if add is None and out_dtype == BF16 else 1024)
    else:
        tm = _tile(M, 512)
    if b.ndim == 3:
        nb, _, bw = b.shape
        N, tn, nn = nb * bw, bw, nb
        b_spec = pl.BlockSpec((None, tk, bw), lambda m, n, k: (_perm2(n), k, 0))
    else:
        N = b.shape[1]
        tn = _tile(N, 1024)
        nn = N // tn
        b_spec = pl.BlockSpec((tk, tn), lambda m, n, k: (k, n))
    tile = pl.BlockSpec((tm, tn), lambda m, n, k: (m, n))
    vec = pl.BlockSpec((1, tn), lambda m, n, k: (0, n))
    extras = ([] if bias is None else [(bias, vec)]) + ([] if add is None else [(add, tile)])
    outs, finish = [(SDS((M, N), out_dtype), tile)], _finish_sum
    if norm is not None:
        assert tn == N == D and out_dtype == F32
        extras.append((norm, vec))
        outs, finish = outs + [(SDS((M, N), BF16), tile)], _finish_sum_norm
    return _mm_call(name, (M // tm, nn, K // tk), [(a, b)], [pl.BlockSpec((tm, tk), lambda m, n, k: (m, k)), b_spec],
                    _NN, (tm, tn), extras, outs, finish)


def _mm_nt(a, b, *, out_dtype, name, add=None, rms=None):
    M, N = a.shape
    if b.ndim == 3:
        nb, Ko, bw = b.shape
        tm = _tile(M, 1024)
        tn, tk, nk = _tile(Ko, 1024), bw, nb
        b_spec = pl.BlockSpec((None, tn, bw), lambda m, n, k: (_perm2(k), n, 0))
    else:
        Ko = b.shape[0]
        tk = N if N <= _K_WHOLE else _tile(N, 1024)
        if N <= 1024 and rms is None:
            tm = _tile(M, 2048 if add is None and out_dtype == BF16 else 1024)
        else:
            tm = _tile(M, 512)
        tn = _tile(Ko, 1024)
        nk = N // tk
        b_spec = pl.BlockSpec((tn, tk), lambda m, n, k: (n, k))
    tile = pl.BlockSpec((tm, tn), lambda m, n, k: (m, n))
    extras = [] if add is None else [(add, tile)]
    outs, finish = [(SDS((M, Ko), out_dtype), tile)], _finish_sum
    if rms is not None:
        assert tn == Ko == D and add is None
        (extras, outs), finish = _rms_bwd_io(M, tm, *rms), _finish_rms_bwd
    return _mm_call(name, (M // tm, Ko // tn, nk), [(a, b)], [pl.BlockSpec((tm, tk), lambda m, n, k: (m, k)), b_spec],
                    _NT, (tm, tn), extras, outs, finish, from_ref=rms is not None)


def _mm_nt_cols(parts, b, *, name, rms):
    M = parts[0].shape[0]
    tm = _tile(M, 512)
    specs, off = [], 0
    for p in parts:
        w = p.shape[1]
        assert off % w == 0
        specs.append(pl.BlockSpec((tm, w), lambda m, n, k: (m, 0)))
        specs.append(pl.BlockSpec((D, w), functools.partial(lambda m, n, k, o: (0, o), o=off // w)))
        off += w
    extras, outs = _rms_bwd_io(M, tm, *rms)
    return _mm_call(name, (M // tm, 1, 1), [(p, b) for p in parts], specs, _NT, (tm, D), extras, outs, _finish_rms_bwd,
                    from_ref=True)


def _mm_tn(a, b, *, out_dtype, name, blocks=None):
    S, Ka = a.shape
    Nb = b.shape[1]
    tm = _tile(Ka, 1024)
    if blocks is not None:
        bw = blocks
        tn, nn = bw, Nb // bw
        out = (SDS((nn, Ka, bw), out_dtype), pl.BlockSpec((None, tm, bw), lambda m, n, k: (_perm2(n), m, 0)))
    else:
        tn = _tile(Nb, 1024)
        nn = Nb // tn
        out = (SDS((Ka, Nb), out_dtype), pl.BlockSpec((tm, tn), lambda m, n, k: (m, n)))
    steps = (Ka // tm) * nn
    tk = _tile(S, 4096 if steps >= 4 else 2048 if steps >= 2 else 1024)
    return _mm_call(name, (Ka // tm, nn, S // tk), [(a, b)],
                    [pl.BlockSpec((tk, tm), lambda m, n, k: (k, m)), pl.BlockSpec((tk, tn), lambda m, n, k: (k, n))],
                    _TN, (tm, tn), [], [out], _finish_sum)


def _row(tm, c):
    return pl.BlockSpec((tm, c), lambda i: (i, 0))


def _full(shape):
    nd = len(shape)
    return pl.BlockSpec(shape, lambda i: (0,) * nd)


def _rms_fwd(x, g, name):
    S = x.shape[0]
    tm = min(S, TM_ROW)

    def body(x_ref, g_ref, o_ref):
        xf = x_ref[...]
        r = lax.rsqrt(jnp.mean(xf * xf, axis=-1, keepdims=True) + EPS)
        o_ref[...] = ((xf * r) * g_ref[...]).astype(BF16)

    return pl.pallas_call(body, out_shape=SDS((S, D), BF16), grid=(S // tm,), in_specs=[_row(tm, D), _full((1, D))],
                          out_specs=_row(tm, D), name=name, compiler_params=_cp(1))(x, g)


def _rms_bwd(x, g, dn, dres, name):
    S = x.shape[0]
    tm = min(S, TM_ROW)
    want_dx = dres is not None

    def body(x_ref, g_ref, dn_ref, *rest):
        i = pl.program_id(0)
        dg_ref = rest[-1]

        @pl.when(i == 0)
        def _():
            dg_ref[...] = jnp.zeros_like(dg_ref)

        xf = x_ref[...]
        r = lax.rsqrt(jnp.mean(xf * xf, axis=-1, keepdims=True) + EPS)
        y = xf * r
        dn_v = dn_ref[...]
        dg_ref[...] += jnp.sum(dn_v * y, axis=0, keepdims=True)
        if want_dx:
            dres_ref, dx_ref, dxb_ref = rest[0], rest[1], rest[2]
            dy = dn_v * g_ref[...]
            dx = r * (dy - y * jnp.mean(dy * y, axis=-1, keepdims=True)) + dres_ref[...]
            dx_ref[...] = dx
            dxb_ref[...] = dx.astype(BF16)

    ins = [x, g, dn] + ([dres] if want_dx else [])
    in_specs = [_row(tm, D), _full((1, D)), _row(tm, D)] + ([_row(tm, D)] if want_dx else [])
    outs = ([SDS((S, D), F32), SDS((S, D), BF16)] if want_dx else []) + [SDS((1, D), F32)]
    out_specs = ([_row(tm, D), _row(tm, D)] if want_dx else []) + [_full((1, D))]
    return pl.pallas_call(body, out_shape=outs, grid=(S // tm,), in_specs=in_specs, out_specs=out_specs, name=name,
                          compiler_params=_cp(1))(*ins)


def _loss_head(x, g, tgt):
    S = x.shape[0]
    tm = min(S, TM_ROW)

    def body(x_ref, g_ref, t_ref, loss_ref, dx_ref, dxb_ref, dg_ref):
        i = pl.program_id(0)

        @pl.when(i == 0)
        def _():
            loss_ref[...] = jnp.zeros_like(loss_ref)
            dg_ref[...] = jnp.zeros_like(dg_ref)

        xf = x_ref[...]
        r = lax.rsqrt(jnp.mean(xf * xf, axis=-1, keepdims=True) + EPS)
        y = xf * r
        gv = g_ref[...]
        err = y * gv - t_ref[...]
        per_row = jnp.mean(err * err, axis=-1, keepdims=True)
        loss_ref[...] += 0.5 * jnp.sum(per_row, axis=0, keepdims=True)
        dn_v = err * (1.0 / D)
        dg_ref[...] += jnp.sum(dn_v * y, axis=0, keepdims=True)
        dy = dn_v * gv
        dx = r * (dy - y * jnp.mean(dy * y, axis=-1, keepdims=True))
        dx_ref[...] = dx
        dxb_ref[...] = dx.astype(BF16)

    return pl.pallas_call(
        body, out_shape=[SDS((1, 1), F32), SDS((S, D), F32), SDS((S, D), BF16), SDS((1, D), F32)], grid=(S // tm,),
        in_specs=[_row(tm, D), _full((1, D)), _row(tm, D)],
        out_specs=[_full((1, 1)), _row(tm, D), _row(tm, D), _full((1, D))], name="loss_head", compiler_params=_cp(1),
    )(x, g, tgt)


def _softmax_rows(s):
    m = jnp.max(s, axis=-1, keepdims=True)
    e = jnp.exp(s - m)
    return e / jnp.sum(e, axis=-1, keepdims=True)


def _attn_fwd(q, k, v, name):
    S = q.shape[0]
    tm = min(S, TM_ROW)
    scale = XA_HD ** -0.5

    def body(q_ref, k_ref, v_ref, o_ref):
        for h in range(XA_HEADS):
            sl = slice(h * XA_HD, (h + 1) * XA_HD)
            s = lax.dot_general(q_ref[:, sl], k_ref[:, sl], _NT, preferred_element_type=F32) * scale
            p = _softmax_rows(s)
            o_ref[:, sl] = lax.dot_general(p.astype(BF16), v_ref[:, sl], _NN, preferred_element_type=F32).astype(BF16)

    return pl.pallas_call(body, out_shape=SDS((S, D), BF16), grid=(S // tm,),
                          in_specs=[_row(tm, D), _full((N_MEM, D)), _full((N_MEM, D))], out_specs=_row(tm, D),
                          name=name, compiler_params=_cp(1))(q, k, v)


def _attn_bwd(q, k, v, do, name):
    S = q.shape[0]
    tm = min(S, TM_ROW)
    scale = XA_HD ** -0.5

    def body(q_ref, k_ref, v_ref, do_ref, dq_ref, dk_ref, dv_ref):
        i = pl.program_id(0)

        @pl.when(i == 0)
        def _():
            dk_ref[...] = jnp.zeros_like(dk_ref)
            dv_ref[...] = jnp.zeros_like(dv_ref)

        for h in range(XA_HEADS):
            sl = slice(h * XA_HD, (h + 1) * XA_HD)
            qh, kh, vh, doh = q_ref[:, sl], k_ref[:, sl], v_ref[:, sl], do_ref[:, sl]
            s = lax.dot_general(qh, kh, _NT, preferred_element_type=F32) * scale
            p = _softmax_rows(s)
            pb = p.astype(BF16)
            dv_ref[:, sl] += lax.dot_general(pb, doh, _TN, preferred_element_type=F32)
            dp = lax.dot_general(doh, vh, _NT, preferred_element_type=F32)
            ds = (p * (dp - jnp.sum(dp * p, axis=-1, keepdims=True)) * scale).astype(BF16)
            dq_ref[:, sl] = lax.dot_general(ds, kh, _NN, preferred_element_type=F32).astype(BF16)
            dk_ref[:, sl] += lax.dot_general(ds, qh, _TN, preferred_element_type=F32)

    return pl.pallas_call(
        body, out_shape=[SDS((S, D), BF16), SDS((N_MEM, D), F32), SDS((N_MEM, D), F32)], grid=(S // tm,),
        in_specs=[_row(tm, D), _full((N_MEM, D)), _full((N_MEM, D)), _row(tm, D)],
        out_specs=[_row(tm, D), _full((N_MEM, D)), _full((N_MEM, D))], name=name, compiler_params=_cp(1),
    )(q, k, v, do)


def _sigmoid(x):
    return 1.0 / (1.0 + jnp.exp(-x))


def _ln_silu_fwd(cv, g, b):
    S = cv.shape[0]
    tm = min(S, TM_ROW)

    def body(x_ref, g_ref, b_ref, o_ref):
        xf = x_ref[...]
        mu = jnp.mean(xf, axis=-1, keepdims=True)
        xc = xf - mu
        rstd = lax.rsqrt(jnp.mean(xc * xc, axis=-1, keepdims=True) + EPS)
        ln = (xc * rstd) * g_ref[...] + b_ref[...]
        o_ref[...] = (ln * _sigmoid(ln)).astype(BF16)

    return pl.pallas_call(body, out_shape=SDS((S, D), BF16), grid=(S // tm,),
                          in_specs=[_row(tm, D), _full((1, D)), _full((1, D))], out_specs=_row(tm, D),
                          name="ln_silu_fwd", compiler_params=_cp(1))(cv, g, b)


def _ln_silu_bwd(ds, cv, g, b, dx):
    S = cv.shape[0]
    tm = min(S, TM_ROW)

    def body(ds_ref, x_ref, g_ref, b_ref, dx_ref, dcv_ref, dg_ref, db_ref, db2_ref):
        i = pl.program_id(0)

        @pl.when(i == 0)
        def _():
            dg_ref[...] = jnp.zeros_like(dg_ref)
            db_ref[...] = jnp.zeros_like(db_ref)
            db2_ref[...] = jnp.zeros_like(db2_ref)

        xf = x_ref[...]
        mu = jnp.mean(xf, axis=-1, keepdims=True)
        xc = xf - mu
        rstd = lax.rsqrt(jnp.mean(xc * xc, axis=-1, keepdims=True) + EPS)
        xhat = xc * rstd
        gv = g_ref[...]
        ln = xhat * gv + b_ref[...]
        sg = _sigmoid(ln)
        dln = ds_ref[...].astype(F32) * (sg + ln * sg * (1.0 - sg))
        dg_ref[...] += jnp.sum(dln * xhat, axis=0, keepdims=True)
        db_ref[...] += jnp.sum(dln, axis=0, keepdims=True)
        db2_ref[...] += jnp.sum(dx_ref[...], axis=0, keepdims=True)
        dxh = dln * gv
        dcv_ref[...] = rstd * (dxh - jnp.mean(dxh, axis=-1, keepdims=True)
                               - xhat * jnp.mean(dxh * xhat, axis=-1, keepdims=True))

    return pl.pallas_call(
        body, out_shape=[SDS((S, D), F32), SDS((1, D), F32), SDS((1, D), F32), SDS((1, D), F32)], grid=(S // tm,),
        in_specs=[_row(tm, D), _row(tm, D), _full((1, D)), _full((1, D)), _row(tm, D)],
        out_specs=[_row(tm, D), _full((1, D)), _full((1, D)), _full((1, D))], name="ln_silu_bwd",
        compiler_params=_cp(1),
    )(ds, cv, g, b, dx)


_GELU_C, _GELU_K = 0.7978845608028654, 0.044715


def _gelu(x, with_grad=False):
    x2 = x * x
    t = jnp.tanh(_GELU_C * (x + _GELU_K * x * x2))
    gel = 0.5 * x * (1.0 + t)
    if not with_grad:
        return gel
    return gel, 0.5 * (1.0 + t) + 0.5 * x * (1.0 - t * t) * (_GELU_C * (1.0 + 3.0 * _GELU_K * x2))


def _expm1(x):
    poly = x * (1.0 + x * (0.5 + x * (1.0 / 6.0 + x * (1.0 / 24.0 + x * (1.0 / 120.0)))))
    return jnp.where(jnp.abs(x) < 0.05, poly, jnp.exp(x) - 1.0)


def _softplus(x):
    return jnp.maximum(x, 0.0) + jnp.log1p(jnp.exp(-jnp.abs(x)))


_SCAN_UNROLL = 8
_RB = 32
_HB = 16


def _sub_blocks(n_rows, n_lanes, fn):
    def step(idx, c):
        r0 = pl.multiple_of(idx * _RB, _RB)
        for lt in range(n_lanes // LANE):
            fn(r0, lt)
        return c

    lax.fori_loop(0, n_rows // _RB, step, 0)


def _lanes(lt):
    return pl.ds(lt * LANE, LANE)


def _psum8(x):
    parts = [x[i * SUB:(i + 1) * SUB] for i in range(x.shape[0] // SUB)]
    return functools.reduce(lambda p, q: p + q, parts)


def _scan_fwd(a_s, b_s, out_ref, carry_ref, n_groups):
    row = lax.broadcasted_iota(jnp.int32, (SUB, LANE), 0)
    U = _SCAN_UNROLL

    def step(gi, carry):
        base = gi * (SUB * U)
        parts = []
        for u in range(U):
            i = pl.multiple_of(base + u * SUB, SUB)
            a8, b8 = a_s[pl.ds(i, SUB), :], b_s[pl.ds(i, SUB), :]
            for s in (1, 2, 4):
                a_sh = jnp.where(row >= s, pltpu.roll(a8, s, 0), 1.0)
                b_sh = jnp.where(row >= s, pltpu.roll(b8, s, 0), 0.0)
                b8 = a8 * b_sh + b8
                a8 = a8 * a_sh
            parts.append((i, a8, b8))
        for i, a8, b8 in parts:
            h8 = a8 * carry + b8
            out_ref[pl.ds(i, SUB), :] = h8
            carry = jnp.broadcast_to(h8[SUB - 1:SUB, :], (SUB, LANE))
        return carry

    carry_ref[...] = lax.fori_loop(0, n_groups // U, step, carry_ref[...])


def _scan_bwd(a_s, b_s, out_ref, carry_ref, n_groups):
    row = lax.broadcasted_iota(jnp.int32, (SUB, LANE), 0)
    U = _SCAN_UNROLL

    def step(gi, carry):
        base = (n_groups // U - 1 - gi) * (SUB * U)
        parts = []
        for u in reversed(range(U)):
            i = pl.multiple_of(base + u * SUB, SUB)
            a8, b8 = a_s[pl.ds(i, SUB), :], b_s[pl.ds(i, SUB), :]
            for s in (1, 2, 4):
                a_sh = jnp.where(row < SUB - s, pltpu.roll(a8, SUB - s, 0), 1.0)
                b_sh = jnp.where(row < SUB - s, pltpu.roll(b8, SUB - s, 0), 0.0)
                b8 = a8 * b_sh + b8
                a8 = a8 * a_sh
            parts.append((i, a8, b8))
        for i, a8, b8 in parts:
            h8 = a8 * carry + b8
            out_ref[pl.ds(i, SUB), :] = h8
            carry = jnp.broadcast_to(h8[0:1, :], (SUB, LANE))
        return carry

    carry_ref[...] = lax.fori_loop(0, n_groups // U, step, carry_ref[...])


def _rglru_pre(xr, wgx_ref, bgx_ref, wga_ref, bga_ref, lam_ref):
    xrb = xr.astype(BF16)
    wgx, wga = wgx_ref[0].astype(BF16), wga_ref[0].astype(BF16)
    gx = _sigmoid(lax.dot_general(xrb, wgx, _NN, preferred_element_type=F32) + bgx_ref[...])
    ga = _sigmoid(lax.dot_general(xrb, wga, _NN, preferred_element_type=F32) + bga_ref[...])
    sp = _softplus(-lam_ref[...])
    log_a = -C_RG * ga * sp
    a = jnp.exp(log_a)
    mult = jnp.sqrt(-_expm1(2.0 * log_a))
    return gx, ga, sp, a, mult, xrb, wgx, wga


def _a_specs():
    vec = pl.BlockSpec((1, HD_A), lambda c, j: (0, c))
    mat = pl.BlockSpec((1, HD_A, HD_A), lambda c, j: (c, 0, 0))
    return [pl.BlockSpec((CONV_A, HD_A), lambda c, j: (0, c)), vec, mat, vec, mat, vec, vec]


def _a_fwd(zp, conv_w, conv_b, wgx, bgx, wga, bga, lam):
    S = zp.shape[0]
    R, nt = R_RGLRU, D // HD_A
    H = SUB

    def body(zg_ref, zr_ref, cw_ref, cb_ref, wgx_ref, bgx_ref, wga_ref, bga_ref, lam_ref, ya_ref, h_ref,
             ext, a_s, b_s, hc):
        j = pl.program_id(1)

        @pl.when(j == 0)
        def _():
            ext[0:H, :] = jnp.zeros((H, HD_A), F32)
            hc[...] = jnp.zeros_like(hc)

        ext[H:H + R, :] = zr_ref[...].astype(F32)
        xr = cb_ref[...]
        for k in range(CONV_A):
            xr = xr + cw_ref[k:k + 1, :] * ext[pl.ds(H - (CONV_A - 1 - k), R), :]
        gx, _, _, a, mult, _, _, _ = _rglru_pre(xr, wgx_ref, bgx_ref, wga_ref, bga_ref, lam_ref)
        a_s[...] = a
        b_s[...] = mult * (gx * xr)
        _scan_fwd(a_s, b_s, h_ref, hc, R // SUB)
        ya_ref[...] = (_gelu(zg_ref[...].astype(F32)) * h_ref[...]).astype(BF16)
        ext[0:H, :] = ext[R:R + H, :]

    return pl.pallas_call(
        body, out_shape=[SDS((S, D + D // 2), BF16), SDS((S, D), F32)], grid=(nt, S // R),
        in_specs=[pl.BlockSpec((R, HD_A), lambda c, j: (j, c)), pl.BlockSpec((R, HD_A), lambda c, j: (j, nt + c))]
        + _a_specs(),
        out_specs=[pl.BlockSpec((R, HD_A), lambda c, j: (j, c)), pl.BlockSpec((R, HD_A), lambda c, j: (j, c))],
        scratch_shapes=[pltpu.VMEM((H + R, HD_A), F32), pltpu.VMEM((R, HD_A), F32), pltpu.VMEM((R, HD_A), F32),
                        pltpu.VMEM((SUB, HD_A), F32)],
        name="rglru_fwd", compiler_params=_cp(2),
    )(zp, zp, conv_w, conv_b, wgx, bgx, wga, bga, lam)


def _a_bwd(dyab, zp, h, conv_w, conv_b, wgx, bgx, wga, bga, lam):
    S = zp.shape[0]
    R, nt, nch = R_RGLRU, D // HD_A, S // R_RGLRU
    H = SUB

    def rows(c, j):
        return (nch - 1 - j, c)

    def rows_rec(c, j):
        return (nch - 1 - j, nt + c)

    def halo(c, j):
        return (jnp.maximum((nch - 1 - j) * (R // H) - 1, 0), c)

    def halo_z(c, j):
        return (jnp.maximum((nch - 1 - j) * (R // _HB) - 1, 0), nt + c)

    def body(dy_ref, zg_ref, zr_ref, zh_ref, h_ref, hh_ref, cw_ref, cb_ref, wgx_ref, bgx_ref, wga_ref, bga_ref,
             lam_ref, dzg_ref, dzr_ref, dcw_ref, dcb_ref, dwgx_ref, dbgx_ref, dwga_ref, dbga_ref, dlam_ref,
             ext_z, ext_h, ext_mu, ext_d, a_s, b_s, muc):
        j = pl.program_id(1)
        first_chunk = (nch - 1 - j) == 0

        @pl.when(j == 0)
        def _():
            ext_mu[R:R + H, :] = jnp.zeros((H, HD_A), F32)
            ext_d[R:R + H, :] = jnp.zeros((H, HD_A), F32)
            muc[...] = jnp.zeros_like(muc)
            for r in (dcw_ref, dcb_ref, dwgx_ref, dbgx_ref, dwga_ref, dbga_ref, dlam_ref):
                r[...] = jnp.zeros_like(r)

        zg = zg_ref[...].astype(F32)
        ext_z[0:H, :] = jnp.where(first_chunk, 0.0, zh_ref[_HB - H:_HB, :].astype(F32))
        ext_z[H:H + R, :] = zr_ref[...].astype(F32)
        ext_h[0:H, :] = jnp.where(first_chunk, 0.0, hh_ref[...])
        ext_h[H:H + R, :] = h_ref[...]
        xr = cb_ref[...]
        for k in range(CONV_A):
            xr = xr + cw_ref[k:k + 1, :] * ext_z[pl.ds(H - (CONV_A - 1 - k), R), :]
        gx, ga, sp, a, mult, xrb, wgxb, wgab = _rglru_pre(xr, wgx_ref, bgx_ref, wga_ref, bga_ref, lam_ref)
        gel, dgel = _gelu(zg, with_grad=True)
        dy = dy_ref[...].astype(F32)
        dh = dy * gel
        dzg_ref[...] = (dy * h_ref[...] * dgel).astype(BF16)
        a_s[...] = a
        b_s[...] = a * dh
        _scan_bwd(a_s, b_s, ext_mu, muc, R // SUB)
        lam_t = dh + ext_mu[pl.ds(1, R), :]
        ext_mu[R:R + H, :] = ext_mu[0:H, :]
        da = lam_t * ext_h[pl.ds(H - 1, R), :]
        gxr = gx * xr
        dlog_a = da * a - (lam_t * gxr) * (a * a) / mult
        dgx = lam_t * mult * xr
        dxr = lam_t * mult * gx
        lam_v = lam_ref[...]
        dlam_ref[...] += jnp.sum(dlog_a * ga, axis=0, keepdims=True) * (C_RG * _sigmoid(-lam_v))
        dpa = (dlog_a * (-C_RG * sp)) * ga * (1.0 - ga)
        dpx = dgx * gx * (1.0 - gx)
        dbga_ref[...] += jnp.sum(dpa, axis=0, keepdims=True)
        dbgx_ref[...] += jnp.sum(dpx, axis=0, keepdims=True)
        dpab, dpxb = dpa.astype(BF16), dpx.astype(BF16)
        dwga_ref[0] += lax.dot_general(xrb, dpab, _TN, preferred_element_type=F32)
        dwgx_ref[0] += lax.dot_general(xrb, dpxb, _TN, preferred_element_type=F32)
        dxr = (dxr + lax.dot_general(dpab, wgab, _NT, preferred_element_type=F32)
               + lax.dot_general(dpxb, wgxb, _NT, preferred_element_type=F32))
        dcb_ref[...] += jnp.sum(dxr, axis=0, keepdims=True)
        ext_d[0:R, :] = dxr
        dzr = jnp.zeros((R, HD_A), F32)
        for k in range(CONV_A):
            sh = CONV_A - 1 - k
            dcw_ref[k:k + 1, :] += jnp.sum(dxr * ext_z[pl.ds(H - sh, R), :], axis=0, keepdims=True)
            dzr = dzr + cw_ref[k:k + 1, :] * ext_d[pl.ds(sh, R), :]
        dzr_ref[...] = dzr.astype(BF16)
        ext_d[R:R + H, :] = ext_d[0:H, :]

    vec_o = pl.BlockSpec((1, HD_A), lambda c, j: (0, c))
    mat_o = pl.BlockSpec((1, HD_A, HD_A), lambda c, j: (c, 0, 0))
    return pl.pallas_call(
        body,
        out_shape=[SDS((S, D), BF16), SDS((S, D), BF16), SDS((CONV_A, D), F32), SDS((1, D), F32),
                   SDS((nt, HD_A, HD_A), F32), SDS((1, D), F32), SDS((nt, HD_A, HD_A), F32), SDS((1, D), F32),
                   SDS((1, D), F32)],
        grid=(nt, nch),
        in_specs=[pl.BlockSpec((R, HD_A), rows), pl.BlockSpec((R, HD_A), rows), pl.BlockSpec((R, HD_A), rows_rec),
                  pl.BlockSpec((_HB, HD_A), halo_z), pl.BlockSpec((R, HD_A), rows),
                  pl.BlockSpec((H, HD_A), halo)] + _a_specs(),
        out_specs=[pl.BlockSpec((R, HD_A), rows), pl.BlockSpec((R, HD_A), rows),
                   pl.BlockSpec((CONV_A, HD_A), lambda c, j: (0, c)), vec_o, mat_o, vec_o, mat_o, vec_o, vec_o],
        scratch_shapes=[pltpu.VMEM((H + R, HD_A), F32), pltpu.VMEM((H + R, HD_A), F32), pltpu.VMEM((R + H, HD_A), F32),
                        pltpu.VMEM((R + H, HD_A), F32), pltpu.VMEM((R, HD_A), F32), pltpu.VMEM((R, HD_A), F32),
                        pltpu.VMEM((SUB, HD_A), F32)],
        name="rglru_bwd", compiler_params=_cp(2),
    )(dyab, zp, zp, zp, h, h, conv_w, conv_b, wgx, bgx, wga, bga, lam)


_POOL_H = 16
_POOL_T0 = 2 * D // HD_A
_POOL_Y0 = D // HD_A


def _window_sum(lv, n, lo, rows, g, ahead):
    base = 0 if ahead else SUB
    cur, win = lv[0], None
    for i, s in enumerate((1, 2, 4, 8)):
        val = cur[pl.ds(base, n), :] + cur[pl.ds(base + (s if ahead else -s), n), :]
        sel = val[lo:lo + rows]
        win = sel if win is None else jnp.where(g >= i, sel, win)
        if i < 3:
            lv[i + 1][pl.ds(base, n), :] = val
            cur = lv[i + 1]
    return win


def _pool_width(g):
    return jnp.where(g == 0, 2.0, jnp.where(g == 1, 4.0, jnp.where(g == 2, 8.0, 16.0)))


def _b_fwd(zp, yab, wg, bg, sc):
    S = zp.shape[0]
    R, H = min(S, R_POOL), _POOL_H

    def body(z_ref, wg_ref, bg_ref, sc_ref, yab_in, yb_ref, *lv):
        del yab_in
        g, j = pl.program_id(0), pl.program_id(1)

        @pl.when(j == 0)
        def _():
            for r in lv:
                r[0:SUB, :] = jnp.zeros((SUB, HD_A), F32)
            lv[0][SUB:SUB + H, :] = jnp.zeros((H, HD_A), F32)

        u = z_ref[...].astype(F32)
        lv[0][SUB + H:SUB + H + R, :] = u
        t1 = (j * R + 1 + lax.broadcasted_iota(jnp.int32, (R, HD_A), 0)).astype(F32)
        p = _window_sum(lv, H + R, H, R, g, False) / jnp.minimum(t1, _pool_width(g)) - u
        lin = lax.dot_general(p.astype(BF16), wg_ref[0].astype(BF16), _NN, preferred_element_type=F32) + bg_ref[...]
        yb_ref[...] = (lin * sc_ref[...]).astype(BF16)
        lv[0][SUB:SUB + H, :] = lv[0][SUB + R:SUB + R + H, :]

    vec = pl.BlockSpec((1, HD_A), lambda g, j: (0, g))
    return pl.pallas_call(
        body, out_shape=SDS(yab.shape, yab.dtype), grid=(len(POOL_WINDOWS), S // R),
        in_specs=[pl.BlockSpec((R, HD_A), lambda g, j: (j, _POOL_T0 + g)),
                  pl.BlockSpec((1, HD_A, HD_A), lambda g, j: (g, 0, 0)), vec, vec, pl.BlockSpec(memory_space=pl.ANY)],
        out_specs=pl.BlockSpec((R, HD_A), lambda g, j: (j, _POOL_Y0 + g)),
        scratch_shapes=[pltpu.VMEM((SUB + H + R, HD_A), F32)] * 4, input_output_aliases={4: 0},
        name="pool_fwd", compiler_params=_cp(2),
    )(zp, wg, bg, sc, yab)


def _b_bwd(dyab, zp, wg, bg, sc):
    S = zp.shape[0]
    R, H, ng = min(S, R_POOL), _POOL_H, len(POOL_WINDOWS)
    nch = S // R

    def body(dy_ref, z_ref, zh_ref, wg_ref, bg_ref, sc_ref, dz_ref, dwg_ref, dbg_ref, dsc_ref, *scratch):
        lu, lq = scratch[:4], scratch[4:]
        g, j = pl.program_id(0), pl.program_id(1)
        jj = nch - 1 - j

        @pl.when(j == 0)
        def _():
            for r in lu:
                r[0:SUB, :] = jnp.zeros((SUB, HD_A), F32)
            for r in lq:
                r[R + H:R + H + SUB, :] = jnp.zeros((SUB, HD_A), F32)
            lq[0][R:R + H, :] = jnp.zeros((H, HD_A), F32)
            for r in (dwg_ref, dbg_ref, dsc_ref):
                r[...] = jnp.zeros_like(r)

        u = z_ref[...].astype(F32)
        lu[0][SUB:SUB + H, :] = jnp.where(jj == 0, 0.0, zh_ref[...].astype(F32))
        lu[0][SUB + H:SUB + H + R, :] = u
        t1 = (jj * R + 1 + lax.broadcasted_iota(jnp.int32, (R, HD_A), 0)).astype(F32)
        cnt = jnp.minimum(t1, _pool_width(g))
        pb = (_window_sum(lu, H + R, H, R, g, False) / cnt - u).astype(BF16)
        wgb = wg_ref[0].astype(BF16)
        lin = lax.dot_general(pb, wgb, _NN, preferred_element_type=F32) + bg_ref[...]
        dy = dy_ref[...].astype(F32)
        dsc_ref[...] += jnp.sum(dy * lin, axis=0, keepdims=True)
        dlin = dy * sc_ref[...]
        dbg_ref[...] += jnp.sum(dlin, axis=0, keepdims=True)
        dlb = dlin.astype(BF16)
        dwg_ref[0] += lax.dot_general(pb, dlb, _TN, preferred_element_type=F32)
        dp = lax.dot_general(dlb, wgb, _NT, preferred_element_type=F32)
        lq[0][0:R, :] = dp / cnt
        dz_ref[...] = (_window_sum(lq, R + H, 0, R, g, True) - dp).astype(BF16)
        lq[0][R:R + H, :] = lq[0][0:H, :]

    vec = pl.BlockSpec((1, HD_A), lambda g, j: (0, g))
    mat = pl.BlockSpec((1, HD_A, HD_A), lambda g, j: (g, 0, 0))
    return pl.pallas_call(
        body, out_shape=[SDS((S, D // 2), BF16), SDS((ng, HD_A, HD_A), F32), SDS((1, D // 2), F32),
                         SDS((1, D // 2), F32)],
        grid=(ng, nch),
        in_specs=[pl.BlockSpec((R, HD_A), lambda g, j: (nch - 1 - j, _POOL_Y0 + g)),
                  pl.BlockSpec((R, HD_A), lambda g, j: (nch - 1 - j, _POOL_T0 + g)),
                  pl.BlockSpec((H, HD_A), lambda g, j: (jnp.maximum((nch - 1 - j) * (R // H) - 1, 0), _POOL_T0 + g)),
                  mat, vec, vec],
        out_specs=[pl.BlockSpec((R, HD_A), lambda g, j: (nch - 1 - j, g)), mat, vec, vec],
        scratch_shapes=[pltpu.VMEM((SUB + H + R, HD_A), F32)] * 8,
        name="pool_bwd", compiler_params=_cp(2),
    )(dyab, zp, zp, wg, bg, sc)


_CW_F = 768


def _f_fwd(hp, w, b, name):
    S = hp.shape[0]
    R, H, cw = min(S, R_FFN), SUB, _CW_F
    nlt = cw // LANE

    def body(h_ref, w_ref, b_ref, o_ref, gel_ref, ud_ref, ext):
        j = pl.program_id(1)

        @pl.when(j == 0)
        def _():
            ext[:, 0:H, :] = jnp.zeros((nlt, H, LANE), F32)

        def stage(r0, lt):
            ext[lt, pl.ds(pl.multiple_of(r0 + H, SUB), _RB), :] = h_ref[pl.ds(r0, _RB), _lanes(lt)].astype(F32)

        def main(r0, lt):
            ls = _lanes(lt)
            gp = b_ref[:, ls]
            for k in range(CONV_F):
                gp = gp + w_ref[k:k + 1, ls] * ext[lt, pl.ds(r0 + (H - (CONV_F - 1 - k)), _RB), :]
            up = h_ref[pl.ds(r0, _RB), _lanes(lt + nlt)].astype(F32)
            gel, dgel = _gelu(gp, with_grad=True)
            rs = pl.ds(r0, _RB)
            o_ref[rs, ls] = (gel * up).astype(BF16)
            gel_ref[rs, ls] = gel.astype(BF16)
            ud_ref[rs, ls] = (up * dgel).astype(BF16)

        _sub_blocks(R, cw, stage)
        _sub_blocks(R, cw, main)
        ext[:, 0:H, :] = ext[:, R:R + H, :]

    tile = pl.BlockSpec((R, cw), lambda c, j: (j, c))
    return pl.pallas_call(
        body, out_shape=[SDS((S, D_FF), BF16)] * 3, grid=(D_FF // cw, S // R),
        in_specs=[pl.BlockSpec((R, 2 * cw), lambda c, j: (j, c)), pl.BlockSpec((CONV_F, cw), lambda c, j: (0, c)),
                  pl.BlockSpec((1, cw), lambda c, j: (0, c))],
        out_specs=[tile] * 3,
        scratch_shapes=[pltpu.VMEM((nlt, H + R, LANE), F32)], name=name, compiler_params=_cp(2),
    )(hp, w, b)


def _f_bwd(dact, hp, gel, ud, w, name):
    S = hp.shape[0]
    R, H, cw = min(S, R_FFN), SUB, _CW_F
    nch = S // R
    nlt = cw // LANE

    def body(da_ref, h_ref, hh_ref, gel_ref, ud_ref, w_ref, dh_ref, dw_ref, db_ref, ext_g, ext_d, acc):
        j = pl.program_id(1)
        jj = nch - 1 - j

        @pl.when(j == 0)
        def _():
            ext_d[:, R:R + H, :] = jnp.zeros((nlt, H, LANE), F32)
            acc[...] = jnp.zeros_like(acc)

        for lt in range(nlt):
            ext_g[lt, 0:H, :] = jnp.where(jj == 0, 0.0, hh_ref[_HB - H:_HB, lt * LANE:(lt + 1) * LANE].astype(F32))

        def stage(r0, lt):
            ext_g[lt, pl.ds(pl.multiple_of(r0 + H, SUB), _RB), :] = h_ref[pl.ds(r0, _RB), _lanes(lt)].astype(F32)

        def first(r0, lt):
            ls, lu, rs = _lanes(lt), _lanes(lt + nlt), pl.ds(r0, _RB)
            da = da_ref[rs, ls].astype(F32)
            dh_ref[rs, lu] = (da * gel_ref[rs, ls].astype(F32)).astype(BF16)
            dgp = da * ud_ref[rs, ls].astype(F32)
            ext_d[lt, rs, :] = dgp
            acc[CONV_F * SUB:(CONV_F + 1) * SUB, ls] += _psum8(dgp)
            for k in range(CONV_F):
                tap = ext_g[lt, pl.ds(r0 + (H - (CONV_F - 1 - k)), _RB), :]
                acc[k * SUB:(k + 1) * SUB, ls] += _psum8(dgp * tap)

        def second(r0, lt):
            ls = _lanes(lt)
            dhg = w_ref[CONV_F - 1:CONV_F, ls] * ext_d[lt, pl.ds(r0, _RB), :]
            for k in range(CONV_F - 1):
                dhg = dhg + w_ref[k:k + 1, ls] * ext_d[lt, pl.ds(r0 + (CONV_F - 1 - k), _RB), :]
            dh_ref[pl.ds(r0, _RB), ls] = dhg.astype(BF16)

        _sub_blocks(R, cw, stage)
        _sub_blocks(R, cw, first)
        _sub_blocks(R, cw, second)
        ext_d[:, R:R + H, :] = ext_d[:, 0:H, :]

        @pl.when(j == nch - 1)
        def _():
            for k in range(CONV_F):
                dw_ref[k:k + 1, :] = jnp.sum(acc[k * SUB:(k + 1) * SUB, :], axis=0, keepdims=True)
            db_ref[...] = jnp.sum(acc[CONV_F * SUB:(CONV_F + 1) * SUB, :], axis=0, keepdims=True)

    rows = lambda c, j: (nch - 1 - j, c)
    return pl.pallas_call(
        body, out_shape=[SDS((S, 2 * D_FF), BF16), SDS((CONV_F, D_FF), F32), SDS((1, D_FF), F32)],
        grid=(D_FF // cw, nch),
        in_specs=[pl.BlockSpec((R, cw), rows), pl.BlockSpec((R, cw), lambda c, j: (nch - 1 - j, 2 * c)),
                  pl.BlockSpec((_HB, cw), lambda c, j: (jnp.maximum((nch - 1 - j) * (R // _HB) - 1, 0), 2 * c)),
                  pl.BlockSpec((R, cw), rows), pl.BlockSpec((R, cw), rows),
                  pl.BlockSpec((CONV_F, cw), lambda c, j: (0, c))],
        out_specs=[pl.BlockSpec((R, 2 * cw), rows), pl.BlockSpec((CONV_F, cw), lambda c, j: (0, c)),
                   pl.BlockSpec((1, cw), lambda c, j: (0, c))],
        scratch_shapes=[pltpu.VMEM((nlt, H + R, LANE), F32), pltpu.VMEM((nlt, R + H, LANE), F32),
                        pltpu.VMEM(((CONV_F + 1) * SUB, cw), F32)], name=name,
        compiler_params=_cp(2),
    )(dact, hp, hp, gel, ud, w)


_CW_C = 256
_H_C = 32


def _c_fwd(h1p, w, b):
    S = h1p.shape[0]
    R, H, cw = R_SEQ, _H_C, _CW_C
    nlt = cw // LANE

    def body(h_ref, w_ref, b_ref, o_ref, ext):
        j = pl.program_id(1)

        @pl.when(j == 0)
        def _():
            ext[:, 0:H, :] = jnp.zeros((nlt, H, LANE), F32)

        def stage(r0, lt):
            rs = pl.ds(r0, _RB)
            gate = h_ref[rs, _lanes(lt + nlt)].astype(F32)
            ext[lt, pl.ds(pl.multiple_of(r0 + H, SUB), _RB), :] = h_ref[rs, _lanes(lt)].astype(F32) * _sigmoid(gate)

        def main(r0, lt):
            ls = _lanes(lt)
            cv = b_ref[:, ls]
            for k in range(CONV_C):
                cv = cv + w_ref[k:k + 1, ls] * ext[lt, pl.ds(r0 + (H - (CONV_C - 1 - k)), _RB), :]
            o_ref[pl.ds(r0, _RB), ls] = cv

        _sub_blocks(R, cw, stage)
        _sub_blocks(R, cw, main)
        ext[:, 0:H, :] = ext[:, R:R + H, :]

    return pl.pallas_call(
        body, out_shape=SDS((S, D), F32), grid=(D // cw, S // R),
        in_specs=[pl.BlockSpec((R, 2 * cw), lambda c, j: (j, c)), pl.BlockSpec((CONV_C, cw), lambda c, j: (0, c)),
                  pl.BlockSpec((1, cw), lambda c, j: (0, c))],
        out_specs=pl.BlockSpec((R, cw), lambda c, j: (j, c)),
        scratch_shapes=[pltpu.VMEM((nlt, H + R, LANE), F32)], name="conf_conv_fwd", compiler_params=_cp(2),
    )(h1p, w, b)


def _c_bwd(dcv, h1p, w):
    S = h1p.shape[0]
    R, H, cw, nch = R_SEQ, _H_C, _CW_C, S // R_SEQ
    nlt = cw // LANE
    a_b, a_val, a_gate = CONV_C * SUB, (CONV_C + 1) * SUB, (CONV_C + 2) * SUB

    def body(dc_ref, h_ref, hh_ref, w_ref, dh_ref, dw_ref, db_ref, db1_ref, ext_u, ext_d, acc):
        j = pl.program_id(1)
        jj = nch - 1 - j

        @pl.when(j == 0)
        def _():
            ext_d[:, R:R + H, :] = jnp.zeros((nlt, H, LANE), F32)
            acc[...] = jnp.zeros_like(acc)

        for lt in range(nlt):
            ext_u[lt, 0:H, :] = jnp.where(
                jj == 0, 0.0, hh_ref[:, lt * LANE:(lt + 1) * LANE].astype(F32)
                * _sigmoid(hh_ref[:, cw + lt * LANE:cw + (lt + 1) * LANE].astype(F32)))

        def stage(r0, lt):
            rs, ls = pl.ds(r0, _RB), _lanes(lt)
            gate = h_ref[rs, _lanes(lt + nlt)].astype(F32)
            ext_u[lt, pl.ds(pl.multiple_of(r0 + H, SUB), _RB), :] = h_ref[rs, ls].astype(F32) * _sigmoid(gate)
            ext_d[lt, rs, :] = dc_ref[rs, ls]

        def first(r0, lt):
            ls = _lanes(lt)
            dc = dc_ref[pl.ds(r0, _RB), ls]
            acc[a_b:a_b + SUB, ls] += _psum8(dc)
            for k in range(CONV_C):
                tap = ext_u[lt, pl.ds(r0 + (H - (CONV_C - 1 - k)), _RB), :]
                acc[k * SUB:(k + 1) * SUB, ls] += _psum8(dc * tap)

        def second(r0, lt):
            rs, ls, lg = pl.ds(r0, _RB), _lanes(lt), _lanes(lt + nlt)
            du = w_ref[CONV_C - 1:CONV_C, ls] * ext_d[lt, rs, :]
            for k in range(CONV_C - 1):
                du = du + w_ref[k:k + 1, ls] * ext_d[lt, pl.ds(r0 + (CONV_C - 1 - k), _RB), :]
            val = h_ref[rs, ls].astype(F32)
            sg = _sigmoid(h_ref[rs, lg].astype(F32))
            dval = du * sg
            dgate = du * val * sg * (1.0 - sg)
            acc[a_val:a_val + SUB, ls] += _psum8(dval)
            acc[a_gate:a_gate + SUB, ls] += _psum8(dgate)
            dh_ref[rs, ls] = dval.astype(BF16)
            dh_ref[rs, lg] = dgate.astype(BF16)

        _sub_blocks(R, cw, stage)
        _sub_blocks(R, cw, first)
        _sub_blocks(R, cw, second)
        ext_d[:, R:R + H, :] = ext_d[:, 0:H, :]

        @pl.when(j == nch - 1)
        def _():
            for k in range(CONV_C):
                dw_ref[k:k + 1, :] = jnp.sum(acc[k * SUB:(k + 1) * SUB, :], axis=0, keepdims=True)
            db_ref[...] = jnp.sum(acc[a_b:a_b + SUB, :], axis=0, keepdims=True)
            db1_ref[:, 0:cw] = jnp.sum(acc[a_val:a_val + SUB, :], axis=0, keepdims=True)
            db1_ref[:, cw:2 * cw] = jnp.sum(acc[a_gate:a_gate + SUB, :], axis=0, keepdims=True)

    rows = lambda c, j: (nch - 1 - j, c)
    return pl.pallas_call(
        body, out_shape=[SDS((S, 2 * D), BF16), SDS((CONV_C, D), F32), SDS((1, D), F32), SDS((1, 2 * D), F32)],
        grid=(D // cw, nch),
        in_specs=[pl.BlockSpec((R, cw), rows), pl.BlockSpec((R, 2 * cw), rows),
                  pl.BlockSpec((H, 2 * cw), lambda c, j: (jnp.maximum((nch - 1 - j) * (R // H) - 1, 0), c)),
                  pl.BlockSpec((CONV_C, cw), lambda c, j: (0, c))],
        out_specs=[pl.BlockSpec((R, 2 * cw), rows), pl.BlockSpec((CONV_C, cw), lambda c, j: (0, c)),
                   pl.BlockSpec((1, cw), lambda c, j: (0, c)), pl.BlockSpec((1, 2 * cw), lambda c, j: (0, c))],
        scratch_shapes=[pltpu.VMEM((nlt, H + R, LANE), F32), pltpu.VMEM((nlt, R + H, LANE), F32),
                        pltpu.VMEM(((CONV_C + 3) * SUB, cw), F32)], name="conf_conv_bwd",
        compiler_params=_cp(2),
    )(dcv, h1p, h1p, w)


def _local_step(x, mem, tgt, W, fetch=None, send=None):
    G = {}
    W = dict(W)

    def arrive(group, after):
        if fetch is None:
            return None
        got, tok = fetch(group, after)
        for key, val in got.items():
            W[key] = {**W.get(key, {}), **val} if isinstance(val, dict) else val
        return tok

    def gain(g, tok):
        return g if tok is None else g + tok

    def sent(group):
        return None if send is None else send(group, G)

    def xattn_fwd(xin, n, l):
        tok = arrive(("xa", l), n)
        mn = _rms_fwd(mem, gain(W["xa_mem_norm"][l:l + 1], tok), f"xa_memnorm_fwd{l}")
        q = _mm_nn(n, W["xa_wq"][l], out_dtype=BF16, name=f"xa_q{l}")
        k = _mm_nn(mn, W["xa_wk"][l], out_dtype=BF16, name=f"xa_k{l}")
        v = _mm_nn(mn, W["xa_wv"][l], out_dtype=BF16, name=f"xa_v{l}")
        o = _attn_fwd(q, k, v, f"xa_attn_fwd{l}")
        xout, nout = _mm_nn(o, W["xa_wo"][l], out_dtype=F32, name=f"xa_o{l}", add=xin, norm=W["f_norm"][l:l + 1])
        return xout, nout, (xin, n, q, mn, k, v, o)

    def xattn_bwd(dx, dxb, saved, l):
        xin, n, q, mn, k, v, o = saved
        do = _mm_nt(dxb, W["xa_wo"][l], out_dtype=BF16, name=f"xa_do{l}")
        G[f"xa_wo{l}"] = _mm_tn(o, dxb, out_dtype=BF16, name=f"xa_dwo{l}")
        dq, dk, dv = _attn_bwd(q, k, v, do, f"xa_attn_bwd{l}")
        dkb, dvb = dk.astype(BF16), dv.astype(BF16)
        G[f"xa_wq{l}"] = _mm_tn(n, dq, out_dtype=BF16, name=f"xa_dwq{l}")
        G[f"xa_wk{l}"] = _mm_tn(mn, dkb, out_dtype=BF16, name=f"xa_dwk{l}")
        G[f"xa_wv{l}"] = _mm_tn(mn, dvb, out_dtype=BF16, name=f"xa_dwv{l}")
        tok = sent(("xa", l))
        dmn = _mm_nt(dkb, W["xa_wk"][l], out_dtype=F32, name=f"xa_dmn_k{l}")
        dmn = _mm_nt(dvb, W["xa_wv"][l], out_dtype=F32, name=f"xa_dmn_v{l}", add=dmn)
        (G[f"xa_mem_norm{l}"],) = _rms_bwd(mem, W["xa_mem_norm"][l:l + 1], dmn, None, f"xa_memnorm_bwd{l}")
        dx, dxb, G[f"xa_norm{l}"] = _mm_nt(dq, W["xa_wq"][l], out_dtype=F32, name=f"xa_dn{l}",
                                           rms=(xin, gain(W["xa_norm"][l:l + 1], tok), dx))
        return dx, dxb

    def ffn_fwd(xin, n, l, next_gain):
        tok = arrive(("f", l), n)
        hp = _mm_nn(n, W["f_w_up"][l], out_dtype=BF16, name=f"f_up{l}")
        act, gel, ud = _f_fwd(hp, W["f_dw_w"][l], gain(W["f_dw_b"][l:l + 1], tok), f"f_conv_fwd{l}")
        arrive(("fd", l), act)
        res = _mm_nn(act, W["f_w_down"][l], out_dtype=F32, name=f"f_down{l}", add=xin, norm=next_gain)
        xout, nout = res if next_gain is not None else (res, None)
        return xout, nout, (xin, n, hp, act, gel, ud)

    def ffn_bwd(dx, dxb, saved, l):
        xin, n, hp, act, gel, ud = saved
        dact = _mm_nt(dxb, W["f_w_down"][l], out_dtype=BF16, name=f"f_dact{l}")
        G[f"f_w_down{l}"] = _mm_tn(act, dxb, out_dtype=BF16, name=f"f_dwdown{l}")
        dhp, G[f"f_dw_w{l}"], G[f"f_dw_b{l}"] = _f_bwd(dact, hp, gel, ud, W["f_dw_w"][l], f"f_conv_bwd{l}")
        G[f"f_w_up{l}"] = _mm_tn(n, dhp, out_dtype=BF16, name=f"f_dwup{l}", blocks=_CW_F)
        tok = sent(("f", l))
        dx, dxb, G[f"f_norm{l}"] = _mm_nt(dhp, W["f_w_up"][l], out_dtype=F32, name=f"f_dn{l}",
                                          rms=(xin, gain(W["f_norm"][l:l + 1], tok), dx))
        return dx, dxb

    n0 = _rms_fwd(x, W["ab_norm"], "ab_norm_fwd")
    tok = arrive(("ab", 0), n0)
    a_par = (W["a_conv_w"], gain(W["a_conv_b"], tok), W["a_gate_x_w"], W["a_gate_x_b"], W["a_gate_a_w"],
             W["a_gate_a_b"], W["a_lambda"])
    b_par = (W["b_group_w"], W["b_group_b"], W["b_scale"])
    zp = _mm_nn(n0, W["ab_w_in"], out_dtype=BF16, name="ab_in")
    yab, h_a = _a_fwd(zp, *a_par)
    yab = _b_fwd(zp, yab, *b_par)
    tok = arrive(("ab", 1), yab)
    x1, n1 = _mm_nn(yab, W["ab_w_out"], out_dtype=F32, name="ab_out", add=x, norm=gain(W["xa_norm"][0:1], tok))
    x2, n2, s_xa0 = xattn_fwd(x1, n1, 0)
    x3, n3, s_f0 = ffn_fwd(x2, n2, 0, W["c_norm"])
    tok = arrive(("c", 0), n3)
    h1p = _mm_nn(n3, W["c_w_pw1"], out_dtype=BF16, name="c_pw1", bias=gain(W["c_b_pw1"], tok))
    cv = _c_fwd(h1p, W["c_dw_w"], W["c_dw_b"])
    sc = _ln_silu_fwd(cv, W["c_ln_g"], W["c_ln_b"])
    x4, n4 = _mm_nn(sc, W["c_w_pw2"], out_dtype=F32, name="c_pw2", bias=W["c_b_pw2"], add=x3, norm=W["xa_norm"][1:2])
    x5, n5, s_xa1 = xattn_fwd(x4, n4, 1)
    x6, _, s_f1 = ffn_fwd(x5, n5, 1, None)
    loss, dx, dxb, G["final_norm"] = _loss_head(x6, W["final_norm"], tgt)

    dx, dxb = ffn_bwd(dx, dxb, s_f1, 1)
    dx, dxb = xattn_bwd(dx, dxb, s_xa1, 1)
    dsc = _mm_nt(dxb, W["c_w_pw2"], out_dtype=BF16, name="c_dsc")
    G["c_w_pw2"] = _mm_tn(sc, dxb, out_dtype=BF16, name="c_dwpw2")
    dcv, G["c_ln_g"], G["c_ln_b"], G["c_b_pw2"] = _ln_silu_bwd(dsc, cv, W["c_ln_g"], W["c_ln_b"], dx)
    dh1p, G["c_dw_w"], G["c_dw_b"], G["c_b_pw1"] = _c_bwd(dcv, h1p, W["c_dw_w"])
    G["c_w_pw1"] = _mm_tn(n3, dh1p, out_dtype=BF16, name="c_dwpw1", blocks=_CW_C)
    tok = sent(("c", 0))
    dx, dxb, G["c_norm"] = _mm_nt(dh1p, W["c_w_pw1"], out_dtype=F32, name="c_dn",
                                  rms=(x3, gain(W["c_norm"], tok), dx))
    dx, dxb = ffn_bwd(dx, dxb, s_f0, 0)
    dx, dxb = xattn_bwd(dx, dxb, s_xa0, 0)
    dyab = _mm_nt(dxb, W["ab_w_out"], out_dtype=BF16, name="ab_dyab")
    G["ab_w_out"] = _mm_tn(yab, dxb, out_dtype=BF16, name="ab_dwout")
    tok = sent(("ab", 1))
    a_par = (a_par[0], gain(a_par[1], tok)) + a_par[2:]
    (dzg, dzr, G["a_conv_w"], G["a_conv_b"], G["a_gate_x_w"], G["a_gate_x_b"], G["a_gate_a_w"], G["a_gate_a_b"],
     G["a_lambda"]) = _a_bwd(dyab, zp, h_a, *a_par)
    dzq, G["b_group_w"], G["b_group_b"], G["b_scale"] = _b_bwd(dyab, zp, *b_par)
    G["ab_w_in"] = jnp.concatenate(
        [_mm_tn(n0, dz, out_dtype=BF16, name=f"ab_dwin_{part}")
         for part, dz in (("gate", dzg), ("rec", dzr), ("pool", dzq))], axis=1)
    tok = sent(("ab", 0))
    dx, _, G["ab_norm"] = _mm_nt_cols([dzg, dzr, dzq], W["ab_w_in"], name="ab_dn",
                                      rms=(x, gain(W["ab_norm"], tok), dx))
    return loss, dx, G


def _my_place():
    x, y, c = lax.axis_index("x"), lax.axis_index("y"), lax.axis_index("c")
    return x, y, c


def _all_gather(shards, name):
    n = len(shards)

    def body(*refs):
        ins, outs = refs[:n], refs[n:2 * n]
        send_sems, recv_sems, local_sems = refs[2 * n:]
        x, y, c = _my_place()
        me, sibling = (x, y, c), (x, y, 1 - c)
        chips = [(1 - x, y), (x, 1 - y), (1 - x, 1 - y)]

        def slab(a, place):
            px, py, pc = place
            return outs[a].at[4 * px + 2 * py + pc]

        def copy(a, k, block, to, src=None):
            return pltpu.make_async_remote_copy(
                src_ref=slab(a, block) if src is None else src, dst_ref=slab(a, block),
                send_sem=send_sems.at[a, k], recv_sem=recv_sems.at[a, k], device_id=to, device_id_type=MESH)

        mine = [pltpu.make_async_copy(ins[a], slab(a, me), local_sems.at[a]) for a in range(n)]
        for cp in mine:
            cp.start()
        first = []
        for j, chip in enumerate(chips):
            first += [copy(a, 1 + j, me, (*chip, c), src=ins[a]) for a in range(n)]
        first += [copy(a, 0, me, sibling, src=ins[a]) for a in range(n)]
        for cp in first:
            cp.start()
        passed = []
        for j, chip in enumerate(chips):
            for a in range(n):
                copy(a, 1 + j, (*chip, c), me).wait_recv()
                cp = copy(a, 4 + j, (*chip, c), sibling)
                cp.start()
                passed.append(cp)
        for a in range(n):
            copy(a, 0, sibling, me).wait_recv()
        for j, chip in enumerate(chips):
            for a in range(n):
                copy(a, 4 + j, (*chip, 1 - c), me).wait_recv()
        for cp in first + passed:
            cp.wait_send()
        for cp in mine:
            cp.wait()

    any_spec = pl.BlockSpec(memory_space=pl.ANY)
    return pl.pallas_call(
        body, out_shape=[SDS((N_DEV,) + s.shape, s.dtype) for s in shards], in_specs=[any_spec] * n,
        out_specs=[any_spec] * n,
        scratch_shapes=[pltpu.SemaphoreType.DMA((n, 7)), pltpu.SemaphoreType.DMA((n, 7)), pltpu.SemaphoreType.DMA((n,))],
        name=name,
    )(*shards)


_HBM = pl.BlockSpec(memory_space=pltpu.HBM)
_SEM = pl.BlockSpec(memory_space=pltpu.SEMAPHORE)
_EFFECT = pltpu.SideEffectType.DATAFLOW_SIDE_EFFECTING


def _peer_places():
    x, y, c = _my_place()
    peers = []
    for k in range(1, N_DEV):
        px = 1 - x if (k >> 2) & 1 else x
        py = 1 - y if (k >> 1) & 1 else y
        pc = 1 - c if k & 1 else c
        peers.append(((px, py, pc), 4 * px + 2 * py + pc))
    return (x, y, c), 4 * x + 2 * y + c, peers


def _send_start(srcs, per_dest, name):
    n = len(srcs)
    lands = [lax.empty((N_DEV,) + (s.shape[1:] if per_dest else s.shape), s.dtype) for s in srcs]

    def body(*refs):
        src, land = refs[:n], refs[n:2 * n]
        outs = refs[2 * n:]
        send, recv, token = outs[:n], outs[n:2 * n], outs[4 * n]
        place, me, peers = _peer_places()
        for a in range(n):
            for peer, pidx in peers + [(place, me)]:
                pltpu.make_async_remote_copy(
                    src_ref=src[a].at[pidx] if per_dest else src[a], dst_ref=land[a].at[me], send_sem=send[a],
                    recv_sem=recv[a], device_id=peer, device_id_type=MESH).start()
        token[...] = jnp.zeros_like(token)

    hbm = lambda a: pltpu.HBM(a.shape, a.dtype)
    sem = pltpu.SemaphoreType.DMA(())
    res = pl.pallas_call(
        body, name=name,
        out_shape=tuple([sem] * (2 * n) + [hbm(s) for s in srcs] + [hbm(l) for l in lands]
                        + [SDS((SUB, LANE), F32)]),
        in_specs=[_HBM] * (2 * n),
        out_specs=tuple([_SEM] * (2 * n) + [_HBM] * (2 * n) + [pl.BlockSpec(memory_space=pltpu.VMEM)]),
        input_output_aliases={i: 2 * n + i for i in range(2 * n)},
        compiler_params=pltpu.CompilerParams(has_side_effects=_EFFECT),
    )(*[pltpu.with_memory_space_constraint(s, pltpu.HBM) for s in srcs],
      *[pltpu.with_memory_space_constraint(l, pltpu.HBM) for l in lands])
    return res[:n], res[n:2 * n], res[2 * n:3 * n], res[3 * n:4 * n], res[4 * n]


def _send_wait(send, recv, srcs, lands, after, per_dest, name):
    n = len(srcs)

    def body(*refs):
        src, land = refs[:n], refs[n:2 * n]
        send_s, recv_s = refs[2 * n:3 * n], refs[3 * n:4 * n]
        token = refs[-1]
        place, _, _ = _peer_places()
        for a in range(n):
            copy = pltpu.make_async_remote_copy(
                src_ref=src[a] if per_dest else land[a], dst_ref=land[a], send_sem=send_s[a],
                recv_sem=recv_s[a], device_id=place, device_id_type=MESH)
            copy.wait_send()
            copy.wait_recv()
        token[...] = jnp.zeros_like(token)

    hbm = lambda a: pltpu.HBM(a.shape, a.dtype)
    res = pl.pallas_call(
        body, name=name,
        out_shape=tuple([hbm(s) for s in srcs] + [hbm(l) for l in lands] + [SDS((SUB, LANE), F32)]),
        in_specs=[_HBM] * (2 * n) + [_SEM] * (2 * n) + [pl.BlockSpec(memory_space=pl.ANY)],
        out_specs=tuple([_HBM] * (2 * n) + [pl.BlockSpec(memory_space=pltpu.VMEM)]),
        input_output_aliases={i: i for i in range(2 * n)},
        compiler_params=pltpu.CompilerParams(has_side_effects=_EFFECT),
    )(*srcs, *lands, *send, *recv, after)
    return res[:n], res[n:2 * n], res[2 * n]


def _adamw_math(w, g, m, v):
    m = ADAM_B1 * m + (1.0 - ADAM_B1) * g
    v = ADAM_B2 * v + (1.0 - ADAM_B2) * (g * g)
    m_hat = m / (1.0 - ADAM_B1 ** ADAM_STEP)
    v_hat = v / (1.0 - ADAM_B2 ** ADAM_STEP)
    delta = -ADAM_LR * (m_hat / (jnp.sqrt(v_hat) + ADAM_EPS) + ADAM_WD * w)
    return delta, m, v


def _row_tile(r, c, itemsize_rows):
    cap = max(SUB, (itemsize_rows // (4 * c)) // SUB * SUB)
    if r <= cap:
        return r
    best = None
    for t in range(SUB, cap + 1, SUB):
        if r % t == 0:
            best = t
    return best if best is not None else r


def _sum_adamw(landing, w, m, v, name, layer=0, prev=None, after=None):
    _, r, c = landing.shape
    tr = _row_tile(r, c, 2 << 20)
    off = layer * (r // tr)
    tail = ([] if prev is None else list(prev)) + ([] if after is None else [after])

    def body(l_ref, w_ref, m_ref, v_ref, *rest):
        g_ref, d_ref, mo_ref, vo_ref = rest[-4:]
        g = l_ref[0].astype(F32)
        for s in range(1, N_DEV):
            g = g + l_ref[s].astype(F32)
        g_ref[...] = g
        d_ref[...], mo_ref[...], vo_ref[...] = _adamw_math(w_ref[...], g, m_ref[...], v_ref[...])

    blk = pl.BlockSpec((tr, c), lambda i: (i + off, 0))
    n_prev = 0 if prev is None else 4
    return pl.pallas_call(
        body, out_shape=[SDS(w.shape, F32)] * 4, grid=(r // tr,),
        in_specs=[pl.BlockSpec((N_DEV, tr, c), lambda i: (0, i, 0)), blk, blk, blk]
        + [pl.BlockSpec(memory_space=pl.ANY)] * len(tail),
        out_specs=[blk] * 4, input_output_aliases={4 + i: i for i in range(n_prev)}, name=name,
        compiler_params=_cp(1),
    )(landing, w, m, v, *tail)


def _sum8(landing, name):
    _, r, c = landing.shape

    def body(l_ref, g_ref):
        g = l_ref[0]
        for s in range(1, N_DEV):
            g = g + l_ref[s]
        g_ref[...] = g

    return pl.pallas_call(body, out_shape=SDS((r, c), F32), name=name, compiler_params=_cp(0))(landing)


def _adamw_small(repl_pack, own_pack, P, M, V):
    table, off = [], 0
    for name, shape in _REPL.items():
        table.append((name, shape if len(shape) > 1 else (1,) + shape, 0, off // LANE))
        off += _size(shape)
    off = _REPL_ROWS * LANE
    for name, shape in _SMALL_SHARDED.items():
        table.append((name, shape, 1, off // LANE))
        off += _size(shape)
    n = len(table)

    def body(*refs):
        packs, ins, outs = refs[:2], refs[2:2 + 3 * n], refs[2 + 3 * n:]
        for p, (_, shape, which, r0) in enumerate(table):
            w_ref, m_ref, v_ref = ins[3 * p:3 * p + 3]
            g_ref, d_ref, mo_ref, vo_ref = outs[4 * p:4 * p + 4]
            pack, rows, q = packs[which], shape[-2], shape[-1] // LANE
            lead = [()]
            for dim in shape[:-2]:
                lead = [t + (i,) for t in lead for i in range(dim)]
            for li, idx in enumerate(lead):
                if q == 1:
                    dst = g_ref.at[idx] if idx else g_ref
                    dst[...] = pack[r0 + li * rows:r0 + (li + 1) * rows, :]
                    continue
                for i in range(rows):
                    for k in range(q):
                        row = r0 + (li * rows + i) * q + k
                        g_ref[idx + (slice(i, i + 1), slice(k * LANE, (k + 1) * LANE))] = pack[row:row + 1, :]
            d_ref[...], mo_ref[...], vo_ref[...] = _adamw_math(w_ref[...], g_ref[...], m_ref[...], v_ref[...])

    ins, out_shape = [], []
    for name, shape, _, _ in table:
        ins += [t[name].reshape(shape) for t in (P, M, V)]
        out_shape += [SDS(shape, F32)] * 4
    res = pl.pallas_call(body, out_shape=out_shape, name="adamw_small", compiler_params=_cp(0))(
        repl_pack, own_pack, *ins)
    dicts = ({}, {}, {}, {})
    for p, (name, shape, _, _) in enumerate(table):
        for d, arr in zip(dicts, res[4 * p:4 * p + 4]):
            d[name] = arr.reshape(P[name].shape)
    return dicts


_BIG = {
    "ab_w_in": (1, D, 320), "ab_w_out": (1, 192, D), "c_w_pw1": (1, D, 256), "c_w_pw2": (1, 128, D),
    "xa_wq": (2, 128, D), "xa_wk": (2, 128, D), "xa_wv": (2, 128, D), "xa_wo": (2, 128, D),
    "f_w_up": (2, D, 768), "f_w_down": (2, 384, D),
}
_SMALL_SHARDED = {
    "a_conv_w": (1, 4, 128), "c_norm": (1, 128), "c_b_pw1": (1, 256), "c_dw_w": (1, 31, 128), "c_dw_b": (1, 128),
    "c_ln_g": (1, 128), "c_ln_b": (1, 128), "c_b_pw2": (1, 128), "f_dw_w": (2, 3, 384),
}
_REPL = {
    "ab_norm": (1, D), "a_conv_b": (1, D), "a_gate_x_w": (1, 8, 128, 128), "a_gate_x_b": (1, D),
    "a_gate_a_w": (1, 8, 128, 128), "a_gate_a_b": (1, D), "a_lambda": (1, D), "b_group_w": (1, 4, 128, 128),
    "b_group_b": (1, 512), "b_scale": (1, 512), "xa_norm": (2, D), "xa_mem_norm": (2, D), "f_norm": (2, D),
    "f_dw_b": (2, D_FF), "final_norm": (D,),
}


def _size(shape):
    n = 1
    for s in shape:
        n *= s
    return n


_N_SS = sum(_size(s) for s in _SMALL_SHARDED.values())
_N_REPL = sum(_size(s) for s in _REPL.values())
_REPL_ROWS = -(-_N_REPL // (N_DEV * SUB * LANE)) * SUB
_SS_ROWS = _N_SS // LANE
_SMALL_ROWS = -(-(_REPL_ROWS + _SS_ROWS) // SUB) * SUB


def _pack(parts, rows):
    flat = jnp.concatenate([p.reshape(-1).astype(F32) for p in parts])
    return jnp.pad(flat, (0, rows * LANE - flat.shape[0])).reshape(rows, LANE)


def _pair_blocks(v, bw):
    lead, n = v.shape[:-1], v.shape[-1]
    return jnp.swapaxes(v.reshape(lead + (2, n // (2 * bw), bw)), -3, -2).reshape(lead + (n,))


def _unpair_blocks(v, bw):
    lead, n = v.shape[:-1], v.shape[-1]
    return jnp.swapaxes(v.reshape(lead + (n // (2 * bw), 2, bw)), -3, -2).reshape(lead + (n,))


_GROUPS = {
    ("ab", 0): (("ab_w_in", 0),),
    ("ab", 1): (("ab_w_out", 0),),
    ("xa", 0): (("xa_wq", 0), ("xa_wk", 0), ("xa_wv", 0), ("xa_wo", 0)),
    ("f", 0): (("f_w_up", 0),),
    ("fd", 0): (("f_w_down", 0),),
    ("c", 0): (("c_w_pw1", 0), ("c_w_pw2", 0)),
    ("xa", 1): (("xa_wq", 1), ("xa_wk", 1), ("xa_wv", 1), ("xa_wo", 1)),
    ("f", 1): (("f_w_up", 1),),
    ("fd", 1): (("f_w_down", 1),),
}
_SEND_GROUPS = {g: m for g, m in _GROUPS.items() if g[0] != "fd"}
_SEND_GROUPS[("f", 0)] = (("f_w_up", 0), ("f_w_down", 0))
_SEND_GROUPS[("f", 1)] = (("f_w_up", 1), ("f_w_down", 1))


def _weight_layout(name, g):
    if name == "ab_w_in":
        return jnp.swapaxes(g, 0, 1).reshape(D, N_DEV * 320)
    if name in ("c_w_pw1", "f_w_up"):
        return g
    return g.reshape(N_DEV * g.shape[1], D)


def _grad_blocks(name, l, G):
    _, r, c = _BIG[name]
    if name == "ab_w_in":
        return jnp.swapaxes(G[name].reshape(D, N_DEV, 320), 0, 1)
    if name == "c_w_pw1":
        return G[name]
    if name == "f_w_up":
        return G[f"{name}{l}"]
    return (G[name] if _BIG[name][0] == 1 else G[f"{name}{l}"]).reshape(N_DEV, r, c)


def _small_layouts(sm):
    W = {}
    sm = sm.reshape(N_DEV, -1)
    off = 0
    for name, shape in _SMALL_SHARDED.items():
        n = _size(shape)
        blocks = sm[:, off:off + n].reshape((N_DEV,) + shape)
        off += n
        W[name] = jnp.moveaxis(blocks, 0, -2).reshape(shape[:-1] + (N_DEV * shape[-1],))
    W["a_conv_w"], W["c_dw_w"] = W["a_conv_w"][0], W["c_dw_w"][0]
    W["c_b_pw1"] = _pair_blocks(W["c_b_pw1"], _CW_C)
    return W


def _to_dest_major(g, shape):
    full = g.reshape(shape[:-1] + (N_DEV, shape[-1]))
    return jnp.moveaxis(full, -2, 0).reshape(N_DEV, -1)


def kernel(x, mem, ab_norm, ab_w_in, a_conv_w, a_conv_b, a_gate_x_w, a_gate_x_b, a_gate_a_w, a_gate_a_b, a_lambda, b_group_w, b_group_b, b_scale, ab_w_out, c_norm, c_w_pw1, c_b_pw1, c_dw_w, c_dw_b, c_ln_g, c_ln_b, c_w_pw2, c_b_pw2, xa_norm, xa_mem_norm, xa_wq, xa_wk, xa_wv, xa_wo, f_norm, f_w_up, f_dw_w, f_dw_b, f_w_down, final_norm, loss_target, m_ab_norm, m_ab_w_in, m_a_conv_w, m_a_conv_b, m_a_gate_x_w, m_a_gate_x_b, m_a_gate_a_w, m_a_gate_a_b, m_a_lambda, m_b_group_w, m_b_group_b, m_b_scale, m_ab_w_out, m_c_norm, m_c_w_pw1, m_c_b_pw1, m_c_dw_w, m_c_dw_b, m_c_ln_g, m_c_ln_b, m_c_w_pw2, m_c_b_pw2, m_xa_norm, m_xa_mem_norm, m_xa_wq, m_xa_wk, m_xa_wv, m_xa_wo, m_f_norm, m_f_w_up, m_f_dw_w, m_f_dw_b, m_f_w_down, m_final_norm, v_ab_norm, v_ab_w_in, v_a_conv_w, v_a_conv_b, v_a_gate_x_w, v_a_gate_x_b, v_a_gate_a_w, v_a_gate_a_b, v_a_lambda, v_b_group_w, v_b_group_b, v_b_scale, v_ab_w_out, v_c_norm, v_c_w_pw1, v_c_b_pw1, v_c_dw_w, v_c_dw_b, v_c_ln_g, v_c_ln_b, v_c_w_pw2, v_c_b_pw2, v_xa_norm, v_xa_mem_norm, v_xa_wq, v_xa_wk, v_xa_wv, v_xa_wo, v_f_norm, v_f_w_up, v_f_dw_w, v_f_dw_b, v_f_w_down, v_final_norm):
    args = dict(locals())
    P = {n: args[n] for n in _NAMES}
    M = {n: args["m_" + n] for n in _NAMES}
    V = {n: args["v_" + n] for n in _NAMES}

    in_flight = {}

    def launch(groups, tok):
        shards, n_of = [], {}
        for grp in groups:
            for name, l in _GROUPS[grp]:
                w = P[name][l] if tok is None else P[name][l] + tok
                shards.append(w.astype(BF16))
            if grp == ("ab", 0):
                shards.append(_pack([P[n] for n in _SMALL_SHARDED], _SS_ROWS + 4))
            n_of[grp] = len(shards)
        res = _send_start(shards, False, "gather_start_" + "_".join(g[0] + str(g[1]) for g in groups))
        lo = 0
        for grp in groups:
            in_flight[grp] = [r[lo:n_of[grp]] for r in res[:4]]
            lo = n_of[grp]
        return res[4][:1, :1]

    follow = {("ab", 0): [("ab", 1), ("xa", 0), ("f", 0), ("fd", 0)], ("ab", 1): [("c", 0), ("xa", 1)],
              ("xa", 0): [("f", 1), ("fd", 1)]}

    def fetch(grp, after):
        send_s, recv_s, srcs, lands = in_flight.pop(grp)
        srcs, lands, tok = _send_wait(send_s, recv_s, srcs, lands, after, False, f"gather_wait_{grp[0]}{grp[1]}")
        tok = launch(follow[grp], tok[:1, :1]) if grp in follow else None
        full = lands
        out = {}
        for (name, l), g in zip(_GROUPS[grp], full):
            w = _weight_layout(name, g)
            if _BIG[name][0] == 1:
                out[name] = w
            else:
                out[name] = {l: w}
        if grp == ("ab", 0):
            out.update(_small_layouts(full[-1]))
        return out, tok

    zero = launch([("ab", 0)], None)

    pending, held = [], []
    rides_with_next = {("xa", 1), ("f", 0)}

    def send(grp, G):
        held.extend(_SEND_GROUPS[grp])
        if grp in rides_with_next:
            return None
        members = tuple(held)
        del held[:]
        res = _send_start([_grad_blocks(name, l, G) for name, l in members], True, f"send_{grp[0]}{grp[1]}")
        pending.append((members, res))
        return res[4][:1, :1]

    W = {n: P[n] for n in _REPL}
    W["ab_norm"] = P["ab_norm"] + zero
    W["final_norm"] = P["final_norm"].reshape(1, D)
    W["a_gate_x_w"], W["a_gate_a_w"], W["b_group_w"] = P["a_gate_x_w"][0], P["a_gate_a_w"][0], P["b_group_w"][0]
    loss, grad_x, G = _local_step(x[0], mem[0], loss_target[0], W, fetch, send)
    loss = lax.psum(loss[0, 0], ("x", "y", "c"))

    Gs = dict(G)
    Gs["c_b_pw1"] = _unpair_blocks(G["c_b_pw1"], _CW_C)
    Gs["f_dw_w"] = jnp.stack([G["f_dw_w0"], G["f_dw_w1"]])
    Gs["a_conv_w"], Gs["c_dw_w"] = G["a_conv_w"][None], G["c_dw_w"][None]
    for n in ("xa_norm", "xa_mem_norm", "f_norm", "f_dw_b"):
        Gs[n] = jnp.concatenate([G[f"{n}0"], G[f"{n}1"]], axis=0)
    for n in ("a_gate_x_w", "a_gate_a_w", "b_group_w"):
        Gs[n] = G[n][None]
    repl_flat = jnp.concatenate([Gs[n].reshape(-1) for n in _REPL])
    repl_rows = jnp.pad(repl_flat, (0, N_DEV * _REPL_ROWS * LANE - _N_REPL)).reshape(N_DEV, _REPL_ROWS, LANE)
    ss_rows = jnp.concatenate([_to_dest_major(Gs[n], s) for n, s in _SMALL_SHARDED.items()], axis=1)
    ss_rows = ss_rows.reshape(N_DEV, _SS_ROWS, LANE)
    small_pack = jnp.concatenate(
        [repl_rows, ss_rows, jnp.zeros((N_DEV, _SMALL_ROWS - _REPL_ROWS - _SS_ROWS, LANE), F32)], axis=1)
    last = _send_start([small_pack], True, "send_small")
    pending.append(((("small", 0),), last))

    def arrived(some, after, name):
        members = [m for mem_, _ in some for m in mem_]
        cat = [[a for _, res in some for a in res[i]] for i in range(4)]
        srcs, lands, _ = _send_wait(cat[0], cat[1], cat[2], cat[3], after, True, name)
        return dict(zip(members, lands))

    out_g, out_d, out_m, out_v = {}, {}, {}, {}
    chain = [None]

    def update(name, landed):
        layers, r, c = _BIG[name]
        w2, m2, v2 = [t[name].reshape(layers * r, c) for t in (P, M, V)]
        res = None
        for l in range(layers):
            res = _sum_adamw(landed[(name, l)], w2, m2, v2, f"adamw_{name}{l}", layer=l, prev=res,
                             after=chain[0] if l == 0 else None)
        chain[0] = res[1]
        out_g[name], out_d[name], out_m[name], out_v[name] = [t.reshape(P[name].shape) for t in res]

    landed = arrived(pending[:-2], grad_x[:SUB, :LANE] + last[4], "send_wait_early")
    for name in _BIG:
        if name != "ab_w_in":
            update(name, landed)
    landed = arrived(pending[-2:], out_v["f_w_down"], "send_wait_late")
    update("ab_w_in", landed)

    small_sum = _sum8(landed[("small", 0)], "sum_small")
    (repl_all,) = _all_gather([small_sum[:_REPL_ROWS]], "gather_small_grads")
    for out, got in zip((out_g, out_d, out_m, out_v),
                        _adamw_small(repl_all.reshape(N_DEV * _REPL_ROWS, LANE), small_sum, P, M, V)):
        out.update(got)

    return (loss, grad_x[None], *[out_g[n] for n in _NAMES], *[out_d[n] for n in _NAMES],
            *[out_m[n] for n in _NAMES], *[out_v[n] for n in _NAMES])


_NAMES = ("ab_norm", "ab_w_in", "a_conv_w", "a_conv_b", "a_gate_x_w", "a_gate_x_b", "a_gate_a_w", "a_gate_a_b",
          "a_lambda", "b_group_w", "b_group_b", "b_scale", "ab_w_out", "c_norm", "c_w_pw1", "c_b_pw1", "c_dw_w",
          "c_dw_b", "c_ln_g", "c_ln_b", "c_w_pw2", "c_b_pw2", "xa_norm", "xa_mem_norm", "xa_wq", "xa_wk", "xa_wv",
          "xa_wo", "f_norm", "f_w_up", "f_dw_w", "f_dw_b", "f_w_down", "final_norm")
```
